```python
import math
import jax, jax.numpy as jnp
from jax import lax
import numpy as np

D_MODEL = 1024
BATCH = 16
SEQ = 2048
DEPTH = 2

N_EVEN = (DEPTH + 1) // 2
N_ODD = DEPTH // 2

A_WIDTH = D_MODEL // 2
A_EXPAND = 128
A_HEADS = A_WIDTH // A_EXPAND
A_DK = A_EXPAND
A_DV = A_WIDTH // A_HEADS
A_CHUNK = 64
B_WIDTH = D_MODEL - A_WIDTH
B_GROUPS = 4
B_GROUP_DIM = B_WIDTH // B_GROUPS
B_CHUNK = 128
C_HEADS = 16
C_HEAD_DIM = D_MODEL // C_HEADS
C_ROT_DIM = C_HEAD_DIM // 4
ROPE_THETA = 500000.0
C_BRANCHES = ((128, 1), (512, 4), (2048, 16))
C_BLOCK = 128
D_FF = 4 * D_MODEL
EPS = 1e-6

EVEN_IN = 4 * A_WIDTH + 2 * B_WIDTH
ODD_IN = 3 * D_MODEL

kernel_name = "hybrid_hgrn2_gmlp_dilated_attn"

F32 = jnp.float32


def rmsnorm(x, g):
    xf = x.astype(F32)
    y = xf * lax.rsqrt(jnp.mean(xf * xf, axis=-1, keepdims=True) + EPS)
    return (y * g.astype(F32)).astype(x.dtype)


def layernorm(x, g, b):
    xf = x.astype(F32)
    mu = jnp.mean(xf, axis=-1, keepdims=True)
    var = jnp.mean(jnp.square(xf - mu), axis=-1, keepdims=True)
    return ((xf - mu) * lax.rsqrt(var + EPS) * g.astype(F32) + b.astype(F32)).astype(x.dtype)


def hgrn2_mix(q, f_logit, i, g, lb, norm_g):
    b_, s_, _ = q.shape
    n_chunks = s_ // A_CHUNK
    f = lb[None, None, :] + (1.0 - lb[None, None, :]) * jax.nn.sigmoid(f_logit.astype(F32))
    k = 1.0 - f
    logf = jnp.log(f)
    qf = jax.nn.silu(q.astype(F32))

    def chunks(t, d):
        return t.reshape(b_, n_chunks, A_CHUNK, A_HEADS, d).transpose(1, 0, 3, 2, 4)

    xs = (chunks(qf, A_DK), chunks(k, A_DK), chunks(i.astype(F32), A_DV), chunks(logf, A_DK))
    causal = np.tril(np.ones((A_CHUNK, A_CHUNK), dtype=bool))

    def step(state, inp):
        qc, kc, vc, lfc = inp
        G = jnp.cumsum(lfc, axis=2)
        o_inter = jnp.einsum('bhtk,bhkv->bhtv', qc * jnp.exp(G), state)
        diff = G[:, :, :, None, :] - G[:, :, None, :, :]
        decay = jnp.exp(jnp.where(causal[None, None, :, :, None], diff, -jnp.inf))
        scores = jnp.einsum('bhtk,bhsk,bhtsk->bhts', qc, kc, decay)
        o_intra = jnp.einsum('bhts,bhsv->bhtv', scores, vc)
        G_last = G[:, :, -1:, :]
        new_state = jnp.exp(G_last[:, :, 0, :])[..., None] * state + jnp.einsum(
            'bhsk,bhsv->bhkv', kc * jnp.exp(G_last - G), vc)
        return new_state, o_inter + o_intra

    state0 = jnp.zeros((b_, A_HEADS, A_DK, A_DV), F32)
    _, o = lax.scan(step, state0, xs)
    o = o.transpose(1, 0, 3, 2, 4).reshape(b_, s_, A_HEADS, A_DV)
    o = rmsnorm(o, norm_g.reshape(A_HEADS, A_DV))
    o = o.reshape(b_, s_, A_WIDTH) * jax.nn.silu(g.astype(F32))
    return o.astype(q.dtype)


def chunk_gmlp_mix(u, v, ln_g, ln_b, w_s, b_s):
    b_, s_, _ = u.shape
    v = layernorm(v, ln_g, ln_b)
    vb = v.reshape(b_, s_ // B_CHUNK, B_CHUNK, B_GROUPS, B_GROUP_DIM)
    tril = np.tril(np.ones((B_CHUNK, B_CHUNK), dtype=bool))
    w = jnp.where(tril[None], w_s, jnp.zeros_like(w_s))
    mixed = jnp.einsum('gts,bnsgc->bntgc', w, vb) + b_s.T[None, None, :, :, None]
    return u * mixed.reshape(b_, s_, B_WIDTH).astype(u.dtype)


def rope_partial(x, pos):
    half = C_ROT_DIM // 2
    inv = ROPE_THETA ** (-jnp.arange(half, dtype=F32) / half)
    ang = pos[..., None].astype(F32) * inv
    cos, sin = jnp.cos(ang)[:, :, None, :], jnp.sin(ang)[:, :, None, :]
    xf = x.astype(F32)
    x1, x2, xp = xf[..., :half], xf[..., half:C_ROT_DIM], xf[..., C_ROT_DIM:]
    return jnp.concatenate([x1 * cos - x2 * sin, x1 * sin + x2 * cos, xp], axis=-1)


def dilated_branch(q, k, v, window, dilation):
    b_, h_, s_, dh = q.shape
    L = s_ // dilation
    W = window // dilation
    qb_len = min(C_BLOCK, L)
    n_blk = L // qb_len

    def sub(t):
        return t.reshape(b_, h_, L, dilation, dh).transpose(0, 1, 3, 2, 4)

    qs = sub(q).reshape(b_, h_, dilation, n_blk, qb_len, dh)
    pad = ((0, 0), (0, 0), (0, 0), (W, 0), (0, 0))
    kp, vp = jnp.pad(sub(k), pad), jnp.pad(sub(v), pad)
    idx = np.arange(n_blk)[:, None] * qb_len + np.arange(qb_len + W)[None, :]
    kb = jnp.take(kp, idx, axis=3)
    vb = jnp.take(vp, idx, axis=3)
    q_pos = np.arange(n_blk)[:, None] * qb_len + np.arange(qb_len)[None, :]
    k_pos = idx - W
    dist = q_pos[:, :, None] - k_pos[:, None, :]
    mask = (dist >= 0) & (dist <= W) & (k_pos[:, None, :] >= 0)
    s = jnp.einsum('bhrnqd,bhrnkd->bhrnqk', qs, kb) * (1.0 / math.sqrt(dh))
    s = jnp.where(mask, s, -jnp.inf)
    m = jnp.max(s, axis=-1, keepdims=True)
    p = jnp.exp(s - m)
    den = jnp.sum(p, axis=-1, keepdims=True)
    o = jnp.einsum('bhrnqk,bhrnkd->bhrnqd', p, vb) / den

    def unsub(t):
        c = t.shape[-1]
        return t.reshape(b_, h_, dilation, L, c).transpose(0, 1, 3, 2, 4).reshape(b_, h_, s_, c)

    return unsub(o), unsub(m), unsub(den)


def dilated_attention_mix(h, w_in, pos):
    b_, s_, _ = h.shape
    qkv = h @ w_in
    q, k, v = jnp.split(qkv, 3, axis=-1)
    q = rope_partial(q.reshape(b_, s_, C_HEADS, C_HEAD_DIM), pos).transpose(0, 2, 1, 3)
    k = rope_partial(k.reshape(b_, s_, C_HEADS, C_HEAD_DIM), pos).transpose(0, 2, 1, 3)
    v = v.reshape(b_, s_, C_HEADS, C_HEAD_DIM).transpose(0, 2, 1, 3).astype(F32)
    outs, maxes, dens = [], [], []
    for window, dilation in C_BRANCHES:
        o_g, m_g, d_g = dilated_branch(q, k, v, window, dilation)
        outs.append(o_g); maxes.append(m_g); dens.append(d_g)
    m_all = jnp.max(jnp.stack(maxes, 0), axis=0)
    weights = [d_g * jnp.exp(m_g - m_all) for m_g, d_g in zip(maxes, dens)]
    o = sum(w_g * o_g for w_g, o_g in zip(weights, outs)) / sum(weights)
    return o.transpose(0, 2, 1, 3).reshape(b_, s_, D_MODEL).astype(h.dtype)


def _fwd_setup_inputs(seed: int = 0) -> dict:
    key = jax.random.key(seed)
    ks = jax.random.split(key, 20)

    def nrm(k, shape, scale):
        return jax.random.normal(k, shape, F32) * scale

    def gain(k, shape):
        return 1.0 + 0.05 * jax.random.normal(k, shape, F32)

    return {
        'x': nrm(ks[0], (BATCH, SEQ, D_MODEL), 1.0),
        'positions': jnp.broadcast_to(jnp.arange(SEQ, dtype=jnp.int32), (BATCH, SEQ)),
        'norm_mix_pre': gain(ks[1], (DEPTH, D_MODEL)),
        'norm_mix_post': gain(ks[2], (DEPTH, D_MODEL)),
        'norm_ffn_pre': gain(ks[3], (DEPTH, D_MODEL)),
        'norm_ffn_post': gain(ks[4], (DEPTH, D_MODEL)),
        'w_in_even': nrm(ks[5], (N_EVEN, D_MODEL, EVEN_IN), D_MODEL ** -0.5),
        'lb_table': nrm(ks[6], (DEPTH + 1, A_WIDTH), 0.5),
        'a_norm': gain(ks[7], (N_EVEN, A_WIDTH)),
        'b_ln_g': gain(ks[8], (N_EVEN, B_WIDTH)),
        'b_ln_b': nrm(ks[9], (N_EVEN, B_WIDTH), 0.02),
        'b_ws': nrm(ks[10], (N_EVEN, B_GROUPS, B_CHUNK, B_CHUNK), B_CHUNK ** -0.5),
        'b_bias': 1.0 + nrm(ks[11], (N_EVEN, B_GROUPS, B_CHUNK), 0.1),
        'w_out_even': nrm(ks[12], (N_EVEN, A_WIDTH + B_WIDTH, D_MODEL), (A_WIDTH + B_WIDTH) ** -0.5),
        'w_in_odd': nrm(ks[13], (N_ODD, D_MODEL, ODD_IN), D_MODEL ** -0.5),
        'w_out_odd': nrm(ks[14], (N_ODD, D_MODEL, D_MODEL), D_MODEL ** -0.5),
        'w_ff1': nrm(ks[15], (DEPTH, D_MODEL, D_FF), D_MODEL ** -0.5),
        'w_ff2': nrm(ks[16], (DEPTH, D_FF, D_MODEL), D_FF ** -0.5),
    }


def _fwd_reference(x, positions, norm_mix_pre, norm_mix_post, norm_ffn_pre, norm_ffn_post,
              w_in_even, lb_table, a_norm, b_ln_g, b_ln_b, b_ws, b_bias, w_out_even,
              w_in_odd, w_out_odd, w_ff1, w_ff2):
    lb_all = jnp.cumsum(jax.nn.softmax(lb_table.astype(F32), axis=0), axis=0)
    splits = [A_WIDTH, 2 * A_WIDTH, 3 * A_WIDTH, 4 * A_WIDTH, 4 * A_WIDTH + B_WIDTH]
    for l in range(DEPTH):
        h = rmsnorm(x, norm_mix_pre[l])
        if l % 2 == 0:
            e = l // 2
            proj = h @ w_in_even[e]
            qa, fa, ia, ga, ub, vb = jnp.split(proj, splits, axis=-1)
            oa = hgrn2_mix(qa, fa, ia, ga, lb_all[l], a_norm[e])
            ob = chunk_gmlp_mix(jax.nn.gelu(ub), jax.nn.gelu(vb), b_ln_g[e], b_ln_b[e], b_ws[e], b_bias[e])
            mix = jnp.concatenate([oa, ob], axis=-1) @ w_out_even[e]
        else:
            o = l // 2
            mix = dilated_attention_mix(h, w_in_odd[o], positions) @ w_out_odd[o]
        x = x + rmsnorm(mix, norm_mix_post[l])
        h = rmsnorm(x, norm_ffn_pre[l])
        y = jnp.square(jax.nn.relu(h @ w_ff1[l])) @ w_ff2[l]
        x = x + rmsnorm(y, norm_ffn_post[l])
    return x


import jax as _jax
import jax.numpy as _jnp

TWIN_FORMAT = 'train_step'
FWD_PARAMS = ['x', 'positions', 'norm_mix_pre', 'norm_mix_post', 'norm_ffn_pre', 'norm_ffn_post', 'w_in_even', 'lb_table', 'a_norm', 'b_ln_g', 'b_ln_b', 'b_ws', 'b_bias', 'w_out_even', 'w_in_odd', 'w_out_odd', 'w_ff1', 'w_ff2']
TWIN_WEIGHTS = ['norm_mix_pre', 'norm_mix_post', 'norm_ffn_pre', 'norm_ffn_post', 'w_in_even', 'lb_table', 'a_norm', 'b_ln_g', 'b_ln_b', 'b_ws', 'b_bias', 'w_out_even', 'w_in_odd', 'w_out_odd', 'w_ff1', 'w_ff2']
TWIN_DIFF_INPUT = 'x'
TWIN_INPUTS = ['x', 'positions', 'norm_mix_pre', 'norm_mix_post', 'norm_ffn_pre', 'norm_ffn_post', 'w_in_even', 'lb_table', 'a_norm', 'b_ln_g', 'b_ln_b', 'b_ws', 'b_bias', 'w_out_even', 'w_in_odd', 'w_out_odd', 'w_ff1', 'w_ff2', 'loss_target', 'm_norm_mix_pre', 'm_norm_mix_post', 'm_norm_ffn_pre', 'm_norm_ffn_post', 'm_w_in_even', 'm_lb_table', 'm_a_norm', 'm_b_ln_g', 'm_b_ln_b', 'm_b_ws', 'm_b_bias', 'm_w_out_even', 'm_w_in_odd', 'm_w_out_odd', 'm_w_ff1', 'm_w_ff2', 'v_norm_mix_pre', 'v_norm_mix_post', 'v_norm_ffn_pre', 'v_norm_ffn_post', 'v_w_in_even', 'v_lb_table', 'v_a_norm', 'v_b_ln_g', 'v_b_ln_b', 'v_b_ws', 'v_b_bias', 'v_w_out_even', 'v_w_in_odd', 'v_w_out_odd', 'v_w_ff1', 'v_w_ff2']
TWIN_OUTPUTS = ['loss', 'grad_x', 'grad_norm_mix_pre', 'grad_norm_mix_post', 'grad_norm_ffn_pre', 'grad_norm_ffn_post', 'grad_w_in_even', 'grad_lb_table', 'grad_a_norm', 'grad_b_ln_g', 'grad_b_ln_b', 'grad_b_ws', 'grad_b_bias', 'grad_w_out_even', 'grad_w_in_odd', 'grad_w_out_odd', 'grad_w_ff1', 'grad_w_ff2', 'delta_norm_mix_pre', 'delta_norm_mix_post', 'delta_norm_ffn_pre', 'delta_norm_ffn_post', 'delta_w_in_even', 'delta_lb_table', 'delta_a_norm', 'delta_b_ln_g', 'delta_b_ln_b', 'delta_b_ws', 'delta_b_bias', 'delta_w_out_even', 'delta_w_in_odd', 'delta_w_out_odd', 'delta_w_ff1', 'delta_w_ff2', 'new_m_norm_mix_pre', 'new_m_norm_mix_post', 'new_m_norm_ffn_pre', 'new_m_norm_ffn_post', 'new_m_w_in_even', 'new_m_lb_table', 'new_m_a_norm', 'new_m_b_ln_g', 'new_m_b_ln_b', 'new_m_b_ws', 'new_m_b_bias', 'new_m_w_out_even', 'new_m_w_in_odd', 'new_m_w_out_odd', 'new_m_w_ff1', 'new_m_w_ff2', 'new_v_norm_mix_pre', 'new_v_norm_mix_post', 'new_v_norm_ffn_pre', 'new_v_norm_ffn_post', 'new_v_w_in_even', 'new_v_lb_table', 'new_v_a_norm', 'new_v_b_ln_g', 'new_v_b_ln_b', 'new_v_b_ws', 'new_v_b_bias', 'new_v_w_out_even', 'new_v_w_in_odd', 'new_v_w_out_odd', 'new_v_w_ff1', 'new_v_w_ff2']
TWIN_LEAF_KINDS = {'loss': 'loss', 'grad_x': 'grad_x', 'grad_norm_mix_pre': 'grad_w', 'grad_norm_mix_post': 'grad_w', 'grad_norm_ffn_pre': 'grad_w', 'grad_norm_ffn_post': 'grad_w', 'grad_w_in_even': 'grad_w', 'grad_lb_table': 'grad_w', 'grad_a_norm': 'grad_w', 'grad_b_ln_g': 'grad_w', 'grad_b_ln_b': 'grad_w', 'grad_b_ws': 'grad_w', 'grad_b_bias': 'grad_w', 'grad_w_out_even': 'grad_w', 'grad_w_in_odd': 'grad_w', 'grad_w_out_odd': 'grad_w', 'grad_w_ff1': 'grad_w', 'grad_w_ff2': 'grad_w', 'delta_norm_mix_pre': 'delta_w', 'delta_norm_mix_post': 'delta_w', 'delta_norm_ffn_pre': 'delta_w', 'delta_norm_ffn_post': 'delta_w', 'delta_w_in_even': 'delta_w', 'delta_lb_table': 'delta_w', 'delta_a_norm': 'delta_w', 'delta_b_ln_g': 'delta_w', 'delta_b_ln_b': 'delta_w', 'delta_b_ws': 'delta_w', 'delta_b_bias': 'delta_w', 'delta_w_out_even': 'delta_w', 'delta_w_in_odd': 'delta_w', 'delta_w_out_odd': 'delta_w', 'delta_w_ff1': 'delta_w', 'delta_w_ff2': 'delta_w', 'new_m_norm_mix_pre': 'new_m', 'new_m_norm_mix_post': 'new_m', 'new_m_norm_ffn_pre': 'new_m', 'new_m_norm_ffn_post': 'new_m', 'new_m_w_in_even': 'new_m', 'new_m_lb_table': 'new_m', 'new_m_a_norm': 'new_m', 'new_m_b_ln_g': 'new_m', 'new_m_b_ln_b': 'new_m', 'new_m_b_ws': 'new_m', 'new_m_b_bias': 'new_m', 'new_m_w_out_even': 'new_m', 'new_m_w_in_odd': 'new_m', 'new_m_w_out_odd': 'new_m', 'new_m_w_ff1': 'new_m', 'new_m_w_ff2': 'new_m', 'new_v_norm_mix_pre': 'new_v', 'new_v_norm_mix_post': 'new_v', 'new_v_norm_ffn_pre': 'new_v', 'new_v_norm_ffn_post': 'new_v', 'new_v_w_in_even': 'new_v', 'new_v_lb_table': 'new_v', 'new_v_a_norm': 'new_v', 'new_v_b_ln_g': 'new_v', 'new_v_b_ln_b': 'new_v', 'new_v_b_ws': 'new_v', 'new_v_b_bias': 'new_v', 'new_v_w_out_even': 'new_v', 'new_v_w_in_odd': 'new_v', 'new_v_w_out_odd': 'new_v', 'new_v_w_ff1': 'new_v', 'new_v_w_ff2': 'new_v'}


def _forward(args):
    return _fwd_reference(*[args[k] for k in FWD_PARAMS])


def _output_shape():
    out = _jax.eval_shape(lambda: _forward(_fwd_setup_inputs(0)))
    return out.shape, out.dtype

N_MICROBATCH = 1
ADAM_LR = 0.001
ADAM_B1 = 0.9
ADAM_B2 = 0.999
ADAM_EPS = 1e-08
ADAM_WD = 0.01
ADAM_STEP = 10
PER_EXAMPLE_BATCH_AXIS = {'x': 0, 'positions': 0, 'loss_target': 0}
SHARED_INPUTS = []
_WEIGHT_DTYPES = {'norm_mix_pre': _jnp.float32, 'norm_mix_post': _jnp.float32, 'norm_ffn_pre': _jnp.float32, 'norm_ffn_post': _jnp.float32, 'w_in_even': _jnp.float32, 'lb_table': _jnp.float32, 'a_norm': _jnp.float32, 'b_ln_g': _jnp.float32, 'b_ln_b': _jnp.float32, 'b_ws': _jnp.float32, 'b_bias': _jnp.float32, 'w_out_even': _jnp.float32, 'w_in_odd': _jnp.float32, 'w_out_odd': _jnp.float32, 'w_ff1': _jnp.float32, 'w_ff2': _jnp.float32}
MOMENT_SCALE = {'norm_mix_pre': 1.779014e+01, 'norm_mix_post': 4.517317e+01, 'norm_ffn_pre': 1.079608e+01, 'norm_ffn_post': 4.008407e+01, 'w_in_even': 8.073796e-01, 'lb_table': 3.046394e-02, 'a_norm': 1.284801e+00, 'b_ln_g': 6.622178e-01, 'b_ln_b': 1.042267e+00, 'b_ws': 5.022308e-01, 'b_bias': 1.384754e+00, 'w_out_even': 1.886907e+01, 'w_in_odd': 1.344109e+01, 'w_out_odd': 2.307916e+01, 'w_ff1': 5.424466e+00, 'w_ff2': 2.110391e+01}


def _to_microbatches(a, axis):
    t = _jnp.moveaxis(a, axis, 0)
    t = t.reshape((N_MICROBATCH, t.shape[0] // N_MICROBATCH) + t.shape[1:])
    return _jnp.moveaxis(t, 1, axis + 1)


def setup_inputs(seed: int = 0) -> dict:
    inp = _fwd_setup_inputs(seed)
    key = _jax.random.fold_in(_jax.random.key(seed), 7919)
    shape, _ = _output_shape()
    out = dict(inp)
    out["loss_target"] = _jax.random.normal(_jax.random.fold_in(key, 0), shape, _jnp.float32)
    for i, name in enumerate(TWIN_WEIGHTS):
        w = inp[name].astype(_jnp.float32)
        if MOMENT_SCALE is None:
            s = _jnp.sqrt(_jnp.mean(_jnp.square(w)) + 1e-30)
        else:
            s = MOMENT_SCALE[name]
        km, kv = _jax.random.split(_jax.random.fold_in(key, i + 1))
        out[name] = w
        out["m_" + name] = s * _jax.random.normal(km, w.shape, _jnp.float32)
        out["v_" + name] = (s * s) * _jax.random.uniform(kv, w.shape, _jnp.float32, 0.5, 1.5)
    if N_MICROBATCH > 1:
        for name, axis in PER_EXAMPLE_BATCH_AXIS.items():
            out[name] = _to_microbatches(out[name], axis)
    return {'x': out['x'], 'positions': out['positions'], 'norm_mix_pre': out['norm_mix_pre'], 'norm_mix_post': out['norm_mix_post'], 'norm_ffn_pre': out['norm_ffn_pre'], 'norm_ffn_post': out['norm_ffn_post'], 'w_in_even': out['w_in_even'], 'lb_table': out['lb_table'], 'a_norm': out['a_norm'], 'b_ln_g': out['b_ln_g'], 'b_ln_b': out['b_ln_b'], 'b_ws': out['b_ws'], 'b_bias': out['b_bias'], 'w_out_even': out['w_out_even'], 'w_in_odd': out['w_in_odd'], 'w_out_odd': out['w_out_odd'], 'w_ff1': out['w_ff1'], 'w_ff2': out['w_ff2'], 'loss_target': out['loss_target'], 'm_norm_mix_pre': out['m_norm_mix_pre'], 'm_norm_mix_post': out['m_norm_mix_post'], 'm_norm_ffn_pre': out['m_norm_ffn_pre'], 'm_norm_ffn_post': out['m_norm_ffn_post'], 'm_w_in_even': out['m_w_in_even'], 'm_lb_table': out['m_lb_table'], 'm_a_norm': out['m_a_norm'], 'm_b_ln_g': out['m_b_ln_g'], 'm_b_ln_b': out['m_b_ln_b'], 'm_b_ws': out['m_b_ws'], 'm_b_bias': out['m_b_bias'], 'm_w_out_even': out['m_w_out_even'], 'm_w_in_odd': out['m_w_in_odd'], 'm_w_out_odd': out['m_w_out_odd'], 'm_w_ff1': out['m_w_ff1'], 'm_w_ff2': out['m_w_ff2'], 'v_norm_mix_pre': out['v_norm_mix_pre'], 'v_norm_mix_post': out['v_norm_mix_post'], 'v_norm_ffn_pre': out['v_norm_ffn_pre'], 'v_norm_ffn_post': out['v_norm_ffn_post'], 'v_w_in_even': out['v_w_in_even'], 'v_lb_table': out['v_lb_table'], 'v_a_norm': out['v_a_norm'], 'v_b_ln_g': out['v_b_ln_g'], 'v_b_ln_b': out['v_b_ln_b'], 'v_b_ws': out['v_b_ws'], 'v_b_bias': out['v_b_bias'], 'v_w_out_even': out['v_w_out_even'], 'v_w_in_odd': out['v_w_in_odd'], 'v_w_out_odd': out['v_w_out_odd'], 'v_w_ff1': out['v_w_ff1'], 'v_w_ff2': out['v_w_ff2']}


def _loss(weights, diff, rest, loss_target):
    with _jax.named_scope("forward"):
        args = {**rest, TWIN_DIFF_INPUT: diff, **{k: w.astype(_WEIGHT_DTYPES[k]) for k, w in weights.items()}}
        y = _forward(args)
    with _jax.named_scope("loss_head"):
        err = _jnp.square(y.astype(_jnp.float32) - loss_target)
        return 0.5 * _jnp.sum(_jnp.mean(err, axis=-1)) if err.ndim else 0.5 * err


def _adamw(w, g, m, v):
    m = ADAM_B1 * m + (1.0 - ADAM_B1) * g
    v = ADAM_B2 * v + (1.0 - ADAM_B2) * _jnp.square(g)
    m_hat = m / (1.0 - ADAM_B1 ** ADAM_STEP)
    v_hat = v / (1.0 - ADAM_B2 ** ADAM_STEP)
    delta = -ADAM_LR * (m_hat / (_jnp.sqrt(v_hat) + ADAM_EPS) + ADAM_WD * w)
    return delta, m, v


def reference(x, positions, norm_mix_pre, norm_mix_post, norm_ffn_pre, norm_ffn_post, w_in_even, lb_table, a_norm, b_ln_g, b_ln_b, b_ws, b_bias, w_out_even, w_in_odd, w_out_odd, w_ff1, w_ff2, loss_target, m_norm_mix_pre, m_norm_mix_post, m_norm_ffn_pre, m_norm_ffn_post, m_w_in_even, m_lb_table, m_a_norm, m_b_ln_g, m_b_ln_b, m_b_ws, m_b_bias, m_w_out_even, m_w_in_odd, m_w_out_odd, m_w_ff1, m_w_ff2, v_norm_mix_pre, v_norm_mix_post, v_norm_ffn_pre, v_norm_ffn_post, v_w_in_even, v_lb_table, v_a_norm, v_b_ln_g, v_b_ln_b, v_b_ws, v_b_bias, v_w_out_even, v_w_in_odd, v_w_out_odd, v_w_ff1, v_w_ff2):
    given = dict(x=x, positions=positions, norm_mix_pre=norm_mix_pre, norm_mix_post=norm_mix_post, norm_ffn_pre=norm_ffn_pre, norm_ffn_post=norm_ffn_post, w_in_even=w_in_even, lb_table=lb_table, a_norm=a_norm, b_ln_g=b_ln_g, b_ln_b=b_ln_b, b_ws=b_ws, b_bias=b_bias, w_out_even=w_out_even, w_in_odd=w_in_odd, w_out_odd=w_out_odd, w_ff1=w_ff1, w_ff2=w_ff2, loss_target=loss_target, m_norm_mix_pre=m_norm_mix_pre, m_norm_mix_post=m_norm_mix_post, m_norm_ffn_pre=m_norm_ffn_pre, m_norm_ffn_post=m_norm_ffn_post, m_w_in_even=m_w_in_even, m_lb_table=m_lb_table, m_a_norm=m_a_norm, m_b_ln_g=m_b_ln_g, m_b_ln_b=m_b_ln_b, m_b_ws=m_b_ws, m_b_bias=m_b_bias, m_w_out_even=m_w_out_even, m_w_in_odd=m_w_in_odd, m_w_out_odd=m_w_out_odd, m_w_ff1=m_w_ff1, m_w_ff2=m_w_ff2, v_norm_mix_pre=v_norm_mix_pre, v_norm_mix_post=v_norm_mix_post, v_norm_ffn_pre=v_norm_ffn_pre, v_norm_ffn_post=v_norm_ffn_post, v_w_in_even=v_w_in_even, v_lb_table=v_lb_table, v_a_norm=v_a_norm, v_b_ln_g=v_b_ln_g, v_b_ln_b=v_b_ln_b, v_b_ws=v_b_ws, v_b_bias=v_b_bias, v_w_out_even=v_w_out_even, v_w_in_odd=v_w_in_odd, v_w_out_odd=v_w_out_odd, v_w_ff1=v_w_ff1, v_w_ff2=v_w_ff2)
    weights = {n: given[n] for n in TWIN_WEIGHTS}
    shared = {n: given[n] for n in SHARED_INPUTS}
    per_example = {n: given[n] for n in ['x', 'positions']}
    grad_fn = _jax.value_and_grad(_loss, argnums=(0, 1))

    def one_microbatch(ex, loss_target):
        ex = dict(ex)
        diff = ex.pop(TWIN_DIFF_INPUT)
        return grad_fn(weights, diff, {**shared, **ex}, loss_target)

    if N_MICROBATCH == 1:
        loss, (grad_w, grad_x) = one_microbatch(per_example, given["loss_target"])
    else:
        def body(carry, xs):
            loss_sum, grad_sum = carry
            l_k, (gw_k, gx_k) = one_microbatch(xs[0], xs[1])
            with _jax.named_scope("update"):
                return (loss_sum + l_k, _jax.tree.map(_jnp.add, grad_sum, gw_k)), gx_k

        init = (_jnp.zeros((), _jnp.float32), _jax.tree.map(_jnp.zeros_like, weights))
        (loss, grad_w), grad_x = _jax.lax.scan(body, init, (per_example, given["loss_target"]))
    with _jax.named_scope("update"):
        delta_w, new_m, new_v = {}, {}, {}
        for n in TWIN_WEIGHTS:
            delta_w[n], new_m[n], new_v[n] = _adamw(weights[n], grad_w[n], given["m_" + n], given["v_" + n])
    return (loss, grad_x, *[grad_w[n] for n in TWIN_WEIGHTS], *[delta_w[n] for n in TWIN_WEIGHTS],
            *[new_m[n] for n in TWIN_WEIGHTS], *[new_v[n] for n in TWIN_WEIGHTS])
```

```python
import functools
import math

import jax
import jax.numpy as jnp
from jax import lax
from jax.experimental import pallas as pl
from jax.experimental.pallas import tpu as pltpu

F32 = jnp.float32
BF16 = jnp.bfloat16
MESH = pl.DeviceIdType.MESH

N_DEV = 8
D_MODEL = 1024
SEQ = 2048
EPS = 1e-6
A_WIDTH = 512
A_HEADS = 4
HEAD_A = 128
B_WIDTH = 512
B_GROUPS = 4
B_CHUNK = 128
C_HEADS = 16
C_HEAD_DIM = 64
C_ROT_HALF = 8
ROPE_THETA = 500000.0
C_DILATIONS = (1, 4, 16)
C_BLOCK = 128
D_FF = 4096
EVEN_IN = 3072
ODD_IN = 3072

ADAM_LR = 0.001
ADAM_B1 = 0.9
ADAM_B2 = 0.999
ADAM_EPS = 1e-08
ADAM_WD = 0.01
ADAM_STEP = 10

LANES = 128
SUBLANES = 8
ROW_TILE = 512
SUB_CHUNK = 16
HGRN_BLOCK = 256
NEG = -1e30
VMEM_LIMIT = 56 * 1024 * 1024


def _params(sem):
    return pltpu.CompilerParams(dimension_semantics=sem, vmem_limit_bytes=VMEM_LIMIT)


def _dot(a, b):
    return jnp.dot(a, b, preferred_element_type=F32)


def _dot_nt(a, b):
    return lax.dot_general(a, b, (((1,), (1,)), ((), ())), preferred_element_type=F32)


def _dot_tn(a, b):
    return lax.dot_general(a, b, (((0,), (0,)), ((), ())), preferred_element_type=F32)


def _rms(x, g):
    r = lax.rsqrt(jnp.mean(x * x, axis=-1, keepdims=True) + EPS)
    return x * r * g


def _rms_bwd(x, g, dy):
    r = lax.rsqrt(jnp.mean(x * x, axis=-1, keepdims=True) + EPS)
    dyg = dy * g
    dx = r * dyg - x * (r * r * r) * jnp.mean(x * dyg, axis=-1, keepdims=True)
    return dx, dy * x * r


def _rows8(v):
    return v.reshape(v.shape[0] // SUBLANES, SUBLANES, v.shape[1]).sum(axis=0)


def _sigmoid(x):
    return 1.0 / (1.0 + jnp.exp(-x))


def _gelu(x):
    return 0.5 * x * (1.0 + jnp.tanh(math.sqrt(2.0 / math.pi) * (x + 0.044715 * (x * x * x))))


def _acc_rows8(ref, val, first):
    @pl.when(first)
    def _():
        ref[...] = val

    @pl.when(jnp.logical_not(first))
    def _():
        ref[...] += val


def _my_slot():
    return 4 * lax.axis_index("x") + 2 * lax.axis_index("y") + lax.axis_index("c")


def _peer(r):
    x, y, c = lax.axis_index("x"), lax.axis_index("y"), lax.axis_index("c")
    px = 1 - x if (r >> 2) & 1 else x
    py = 1 - y if (r >> 1) & 1 else y
    pc = 1 - c if r & 1 else c
    return (px, py, pc), 4 * px + 2 * py + pc


def _exchange(arrays, gather, name):
    n = len(arrays)
    if gather:
        out_shape = [jax.ShapeDtypeStruct((N_DEV,) + a.shape, a.dtype) for a in arrays]
    else:
        out_shape = [jax.ShapeDtypeStruct(a.shape, a.dtype) for a in arrays]

    def body(*refs):
        ins, outs = refs[:n], refs[n:2 * n]
        send_sems, recv_sems, local_sems = refs[2 * n:]
        me = _my_slot()
        local, remote = [], []
        for k in range(n):
            src = ins[k] if gather else ins[k].at[me]
            local.append(pltpu.make_async_copy(src, outs[k].at[me], local_sems.at[k]))
            for r in range(1, N_DEV):
                peer, slot = _peer(r)
                src = ins[k] if gather else ins[k].at[slot]
                remote.append((pltpu.make_async_remote_copy(
                    src_ref=src, dst_ref=outs[k].at[me], send_sem=send_sems.at[k, r - 1],
                    recv_sem=recv_sems.at[k, r - 1], device_id=peer, device_id_type=MESH), k, r, slot))
        for cp in local:
            cp.start()
        for cp, _, _, _ in remote:
            cp.start()
        for cp, k, r, slot in remote:
            pltpu.make_async_remote_copy(
                src_ref=outs[k].at[slot], dst_ref=outs[k].at[slot], send_sem=send_sems.at[k, r - 1],
                recv_sem=recv_sems.at[k, r - 1], device_id=_peer(r)[0], device_id_type=MESH).wait_recv()
        for cp, _, _, _ in remote:
            cp.wait_send()
        for cp in local:
            cp.wait()

    any_spec = pl.BlockSpec(memory_space=pl.ANY)
    return pl.pallas_call(
        body, name=name, out_shape=out_shape,
        in_specs=[any_spec] * n, out_specs=[any_spec] * n,
        scratch_shapes=[pltpu.SemaphoreType.DMA((n, N_DEV - 1)), pltpu.SemaphoreType.DMA((n, N_DEV - 1)),
                        pltpu.SemaphoreType.DMA((n,))],
        compiler_params=pltpu.CompilerParams(has_side_effects=True),
    )(*arrays)


def _cast_bf16(a, name):
    shape = a.shape
    a2 = a.reshape((-1, shape[-1]))
    rows = a2.shape[0]
    tr = min(rows, 512)

    def body(a_ref, o_ref):
        o_ref[...] = a_ref[...].astype(BF16)

    out = pl.pallas_call(
        body, name=name, grid=(rows // tr,),
        in_specs=[pl.BlockSpec((tr, shape[-1]), lambda i: (i, 0))],
        out_specs=pl.BlockSpec((tr, shape[-1]), lambda i: (i, 0)),
        out_shape=jax.ShapeDtypeStruct(a2.shape, BF16), compiler_params=_params(("parallel",)),
    )(a2)
    return out.reshape(shape)


def _rope_tables(positions):
    inv = ROPE_THETA ** (-jnp.arange(C_ROT_HALF, dtype=F32) / C_ROT_HALF)
    ang = positions.reshape(-1)[:, None].astype(F32) * inv
    cos, sin = jnp.cos(ang), jnp.sin(ang)
    t = ang.shape[0]
    ones = jnp.ones((t, C_HEAD_DIM - 2 * C_ROT_HALF), F32)
    c_head = jnp.concatenate([cos, cos, ones], axis=1)
    s_head = jnp.concatenate([-sin, sin, 0.0 * ones], axis=1)
    return jnp.concatenate([c_head, c_head], axis=1), jnp.concatenate([s_head, s_head], axis=1)


def _swap_halves(x):
    lane = lax.broadcasted_iota(jnp.int32, x.shape, 1) % C_HEAD_DIM
    return jnp.where(lane < C_ROT_HALF, pltpu.roll(x, LANES - C_ROT_HALF, 1), pltpu.roll(x, C_ROT_HALF, 1))


def _norm_inproj(x, g, w, name, rope=None):
    t = x.shape[0]
    nb = w.shape[2]
    tm = ROW_TILE
    out_dtype = F32 if rope is None else BF16

    def body(*refs):
        if rope is None:
            x_ref, g_ref, w_ref, o_ref, h_ref = refs
        else:
            x_ref, g_ref, w_ref, c_ref, s_ref, o_ref, h_ref = refs
        j = pl.program_id(1)

        @pl.when(j == 0)
        def _():
            h_ref[...] = _rms(x_ref[...], g_ref[...]).astype(BF16)

        acc = _dot(h_ref[...], w_ref[0])
        if rope is None:
            o_ref[...] = acc
        else:
            for p in range(nb // LANES):
                blk = acc[:, p * LANES:(p + 1) * LANES]
                roped = blk * c_ref[...] + _swap_halves(blk) * s_ref[...]
                is_qk = (j * (nb // LANES) + p) < 2 * (D_MODEL // LANES)
                o_ref[:, p * LANES:(p + 1) * LANES] = jnp.where(is_qk, roped, blk).astype(BF16)

    in_specs = [pl.BlockSpec((tm, D_MODEL), lambda i, j: (i, 0)),
                pl.BlockSpec((1, D_MODEL), lambda i, j: (0, 0)),
                pl.BlockSpec((1, D_MODEL, nb), lambda i, j: (j, 0, 0))]
    args = [x, g, w]
    if rope is not None:
        in_specs += [pl.BlockSpec((tm, LANES), lambda i, j: (i, 0))] * 2
        args += list(rope)
    return pl.pallas_call(
        body, name=name, grid=(t // tm, N_DEV), in_specs=in_specs,
        out_specs=[pl.BlockSpec((tm, nb), lambda i, j: (i, j)), pl.BlockSpec((tm, D_MODEL), lambda i, j: (i, 0))],
        out_shape=[jax.ShapeDtypeStruct((t, N_DEV * nb), out_dtype), jax.ShapeDtypeStruct((t, D_MODEL), BF16)],
        compiler_params=_params(("parallel", "arbitrary")),
    )(*args)


def _outproj(parts, w, x, g, name):
    t = x.shape[0]
    tm = ROW_TILE
    n = len(parts)
    widths = [p.shape[1] for p in parts]

    def body(*refs):
        p_refs = refs[:n]
        w_ref, x_ref, g_ref, xo_ref, mix_ref = refs[n:]
        mix = None
        off = 0
        for p_ref, wd in zip(p_refs, widths):
            term = _dot(p_ref[...].astype(BF16), w_ref[off:off + wd, :])
            mix = term if mix is None else mix + term
            off += wd
        mix_ref[...] = mix
        xo_ref[...] = x_ref[...] + _rms(mix, g_ref[...])

    row = lambda i: (i, 0)
    return pl.pallas_call(
        body, name=name, grid=(t // tm,),
        in_specs=[pl.BlockSpec((tm, wd), row) for wd in widths] + [
            pl.BlockSpec((sum(widths), D_MODEL), lambda i: (0, 0)),
            pl.BlockSpec((tm, D_MODEL), row), pl.BlockSpec((1, D_MODEL), lambda i: (0, 0))],
        out_specs=[pl.BlockSpec((tm, D_MODEL), row)] * 2,
        out_shape=[jax.ShapeDtypeStruct((t, D_MODEL), F32)] * 2,
        compiler_params=_params(("parallel",)),
    )(*parts, w, x, g)


def _outproj_bwd(dx, mix, g, w, name):
    t = dx.shape[0]
    tm = ROW_TILE
    k = w.shape[0]

    def body(dx_ref, mix_ref, g_ref, w_ref, dcat_ref, dz_ref, dg_ref):
        dz, dgr = _rms_bwd(mix_ref[...], g_ref[...], dx_ref[...])
        dzb = dz.astype(BF16)
        dz_ref[...] = dzb
        dcat_ref[...] = _dot_nt(dzb, w_ref[...])
        _acc_rows8(dg_ref, _rows8(dgr), pl.program_id(0) == 0)

    row = lambda i: (i, 0)
    return pl.pallas_call(
        body, name=name, grid=(t // tm,),
        in_specs=[pl.BlockSpec((tm, D_MODEL), row), pl.BlockSpec((tm, D_MODEL), row),
                  pl.BlockSpec((1, D_MODEL), lambda i: (0, 0)), pl.BlockSpec((k, D_MODEL), lambda i: (0, 0))],
        out_specs=[pl.BlockSpec((tm, k), row), pl.BlockSpec((tm, D_MODEL), row),
                   pl.BlockSpec((SUBLANES, D_MODEL), lambda i: (0, 0))],
        out_shape=[jax.ShapeDtypeStruct((t, k), F32), jax.ShapeDtypeStruct((t, D_MODEL), BF16),
                   jax.ShapeDtypeStruct((SUBLANES, D_MODEL), F32)],
        compiler_params=_params(("arbitrary",)),
    )(dx, mix, g, w)


def _inproj_bwd(dproj, w, dx, x, g, name):
    t = x.shape[0]
    nb = w.shape[2]
    tm = ROW_TILE

    def body(dp_ref, w_ref, dx_ref, x_ref, g_ref, o_ref, dg_ref, acc_ref):
        i, j = pl.program_id(0), pl.program_id(1)
        term = _dot_nt(dp_ref[...], w_ref[0])

        @pl.when(j == 0)
        def _():
            acc_ref[...] = term

        @pl.when(j > 0)
        def _():
            acc_ref[...] += term

        @pl.when(j == N_DEV - 1)
        def _():
            dxn, dgr = _rms_bwd(x_ref[...], g_ref[...], acc_ref[...])
            o_ref[...] = dx_ref[...] + dxn
            _acc_rows8(dg_ref, _rows8(dgr), i == 0)

    return pl.pallas_call(
        body, name=name, grid=(t // tm, N_DEV),
        in_specs=[pl.BlockSpec((tm, nb), lambda i, j: (i, j)), pl.BlockSpec((1, D_MODEL, nb), lambda i, j: (j, 0, 0)),
                  pl.BlockSpec((tm, D_MODEL), lambda i, j: (i, 0)), pl.BlockSpec((tm, D_MODEL), lambda i, j: (i, 0)),
                  pl.BlockSpec((1, D_MODEL), lambda i, j: (0, 0))],
        out_specs=[pl.BlockSpec((tm, D_MODEL), lambda i, j: (i, 0)),
                   pl.BlockSpec((SUBLANES, D_MODEL), lambda i, j: (0, 0))],
        out_shape=[jax.ShapeDtypeStruct((t, D_MODEL), F32), jax.ShapeDtypeStruct((SUBLANES, D_MODEL), F32)],
        scratch_shapes=[pltpu.VMEM((tm, D_MODEL), F32)],
        compiler_params=_params(("arbitrary", "arbitrary")),
    )(dproj, w, dx, x, g)


def _grad_w(a, b, col_blocks, name):
    t, k = a.shape
    n = b.shape[1]
    tt = ROW_TILE
    tk = min(k, 512)
    tn = n // N_DEV if col_blocks else min(n, 1024)
    nt = t // tt

    def body(a_ref, b_ref, o_ref, acc_ref):
        s = pl.program_id(2)
        term = _dot_tn(a_ref[...], b_ref[...])

        @pl.when(s == 0)
        def _():
            acc_ref[...] = term

        @pl.when(s > 0)
        def _():
            acc_ref[...] += term

        @pl.when(s == nt - 1)
        def _():
            o_ref[...] = acc_ref[...].astype(BF16).reshape(o_ref.shape)

    if col_blocks:
        out_spec = pl.BlockSpec((1, tk, tn), lambda i, j, s: (j, i, 0))
        out_shape = jax.ShapeDtypeStruct((N_DEV, k, tn), BF16)
    else:
        out_spec = pl.BlockSpec((tk, tn), lambda i, j, s: (i, j))
        out_shape = jax.ShapeDtypeStruct((k, n), BF16)
    return pl.pallas_call(
        body, name=name, grid=(k // tk, n // tn, nt),
        in_specs=[pl.BlockSpec((tt, tk), lambda i, j, s: (s, i)), pl.BlockSpec((tt, tn), lambda i, j, s: (s, j))],
        out_specs=out_spec, out_shape=out_shape,
        scratch_shapes=[pltpu.VMEM((tk, tn), F32)],
        compiler_params=_params(("parallel", "parallel", "arbitrary")),
    )(a, b)


FF_BLOCK = D_FF // N_DEV


def _ffn_fwd(x, g_pre, w1, w2, g_post, name):
    t = x.shape[0]
    tm = ROW_TILE

    def body(x_ref, gp_ref, w1_ref, w2_ref, gq_ref, xo_ref, y_ref, h_ref):
        j = pl.program_id(1)

        @pl.when(j == 0)
        def _():
            h_ref[...] = _rms(x_ref[...], gp_ref[...]).astype(BF16)

        a = _dot(h_ref[...], w1_ref[0])
        r = jnp.square(jnp.maximum(a, 0.0)).astype(BF16)
        term = _dot(r, w2_ref[...])

        @pl.when(j == 0)
        def _():
            y_ref[...] = term

        @pl.when(j > 0)
        def _():
            y_ref[...] += term

        @pl.when(j == N_DEV - 1)
        def _():
            xo_ref[...] = x_ref[...] + _rms(y_ref[...], gq_ref[...])

    row = lambda i, j: (i, 0)
    vec = pl.BlockSpec((1, D_MODEL), lambda i, j: (0, 0))
    return pl.pallas_call(
        body, name=name, grid=(t // tm, N_DEV),
        in_specs=[pl.BlockSpec((tm, D_MODEL), row), vec,
                  pl.BlockSpec((1, D_MODEL, FF_BLOCK), lambda i, j: (j, 0, 0)),
                  pl.BlockSpec((FF_BLOCK, D_MODEL), lambda i, j: (j, 0)), vec],
        out_specs=[pl.BlockSpec((tm, D_MODEL), row)] * 3,
        out_shape=[jax.ShapeDtypeStruct((t, D_MODEL), F32), jax.ShapeDtypeStruct((t, D_MODEL), F32),
                   jax.ShapeDtypeStruct((t, D_MODEL), BF16)],
        compiler_params=_params(("parallel", "arbitrary")),
    )(x, g_pre, w1, w2, g_post)


def _ffn_bwd(dxo, x, y, h, g_pre, w1, w2, g_post, name):
    t = x.shape[0]
    tm = ROW_TILE

    def body(dxo_ref, x_ref, y_ref, h_ref, gp_ref, w1_ref, w2_ref, gq_ref,
             dx_ref, dy_ref, r_ref, da_ref, dgp_ref, dgq_ref, acc_ref):
        i, j = pl.program_id(0), pl.program_id(1)

        @pl.when(j == 0)
        def _():
            dy, dgr = _rms_bwd(y_ref[...], gq_ref[...], dxo_ref[...])
            dy_ref[...] = dy.astype(BF16)
            _acc_rows8(dgq_ref, _rows8(dgr), i == 0)

        a = _dot(h_ref[...], w1_ref[0])
        ra = jnp.maximum(a, 0.0)
        r_ref[...] = jnp.square(ra).astype(BF16)
        dr = _dot_nt(dy_ref[...], w2_ref[...])
        da = (dr * (2.0 * ra)).astype(BF16)
        da_ref[...] = da
        term = _dot_nt(da, w1_ref[0])

        @pl.when(j == 0)
        def _():
            acc_ref[...] = term

        @pl.when(j > 0)
        def _():
            acc_ref[...] += term

        @pl.when(j == N_DEV - 1)
        def _():
            dxn, dgr = _rms_bwd(x_ref[...], gp_ref[...], acc_ref[...])
            dx_ref[...] = dxo_ref[...] + dxn
            _acc_rows8(dgp_ref, _rows8(dgr), i == 0)

    row = lambda i, j: (i, 0)
    vec = pl.BlockSpec((1, D_MODEL), lambda i, j: (0, 0))
    acc8 = pl.BlockSpec((SUBLANES, D_MODEL), lambda i, j: (0, 0))
    return pl.pallas_call(
        body, name=name, grid=(t // tm, N_DEV),
        in_specs=[pl.BlockSpec((tm, D_MODEL), row)] * 4 + [
            vec, pl.BlockSpec((1, D_MODEL, FF_BLOCK), lambda i, j: (j, 0, 0)),
            pl.BlockSpec((FF_BLOCK, D_MODEL), lambda i, j: (j, 0)), vec],
        out_specs=[pl.BlockSpec((tm, D_MODEL), row), pl.BlockSpec((tm, D_MODEL), row),
                   pl.BlockSpec((tm, FF_BLOCK), lambda i, j: (i, j)), pl.BlockSpec((tm, FF_BLOCK), lambda i, j: (i, j)),
                   acc8, acc8],
        out_shape=[jax.ShapeDtypeStruct((t, D_MODEL), F32), jax.ShapeDtypeStruct((t, D_MODEL), BF16),
                   jax.ShapeDtypeStruct((t, D_FF), BF16), jax.ShapeDtypeStruct((t, D_FF), BF16),
                   jax.ShapeDtypeStruct((SUBLANES, D_MODEL), F32), jax.ShapeDtypeStruct((SUBLANES, D_MODEL), F32)],
        scratch_shapes=[pltpu.VMEM((tm, D_MODEL), F32)],
        compiler_params=_params(("arbitrary", "arbitrary")),
    )(dxo, x, y, h, g_pre, w1, w2, g_post)


def _lower_bound(table):
    e = jnp.exp(table - jnp.max(table, axis=0, keepdims=True))
    return e[0:1, :] / jnp.sum(e, axis=0, keepdims=True)


def _hgrn2_step(st, qraw, fl, v, graw, lb, an, tri):
    f = lb + (1.0 - lb) * _sigmoid(fl)
    logf = jnp.log(f)
    kk = 1.0 - f
    q = qraw * _sigmoid(qraw)
    gsum = jnp.dot(tri, logf, precision=lax.Precision.HIGHEST, preferred_element_type=F32)
    o = _dot_nt((q * jnp.exp(gsum)).astype(BF16), st.astype(BF16))
    row = lax.broadcasted_iota(jnp.int32, gsum.shape, 0)
    for s in range(SUB_CHUNK):
        pick = row == s
        g_s = jnp.sum(jnp.where(pick, gsum, 0.0), axis=0, keepdims=True)
        k_s = jnp.sum(jnp.where(pick, kk, 0.0), axis=0, keepdims=True)
        v_s = jnp.sum(jnp.where(pick, v, 0.0), axis=0, keepdims=True)
        decay = jnp.exp(jnp.where(row >= s, gsum - g_s, NEG))
        score = jnp.sum(q * k_s * decay, axis=1, keepdims=True)
        o = o + score * v_s
    g_last = jnp.sum(jnp.where(row == SUB_CHUNK - 1, gsum, 0.0), axis=0, keepdims=True)
    kd = kk * jnp.exp(g_last - gsum)
    st_new = st * jnp.exp(g_last) + _dot_tn(v.astype(BF16), kd.astype(BF16))
    out = _rms(o, an) * (graw * _sigmoid(graw))
    return st_new, out


def _tri():
    r = lax.broadcasted_iota(jnp.int32, (SUB_CHUNK, SUB_CHUNK), 0)
    c = lax.broadcasted_iota(jnp.int32, (SUB_CHUNK, SUB_CHUNK), 1)
    return (c <= r).astype(F32)


def _hgrn2_fwd(proj, lb_table, a_norm, name):
    t = proj.shape[0]
    tb = HGRN_BLOCK
    n_tb = SEQ // tb
    n_seq = t // SEQ
    n_sub = tb // SUB_CHUNK

    def body(q_ref, f_ref, i_ref, g_ref, lbt_ref, an_ref, o_ref, st_out_ref, st_ref):
        @pl.when(pl.program_id(2) == 0)
        def _():
            st_ref[...] = jnp.zeros_like(st_ref)

        st_out_ref[0] = st_ref[...]
        lb = _lower_bound(lbt_ref[...])
        an = an_ref[...]
        tri = _tri()

        def step(c, carry):
            rows = pl.ds(pl.multiple_of(c * SUB_CHUNK, SUB_CHUNK), SUB_CHUNK)
            st_new, out = _hgrn2_step(st_ref[...], q_ref[rows, :], f_ref[rows, :], i_ref[rows, :], g_ref[rows, :],
                                      lb, an, tri)
            st_ref[...] = st_new
            o_ref[rows, :] = out.astype(BF16)
            return carry

        lax.fori_loop(0, n_sub, step, 0)

    def col(k):
        return pl.BlockSpec((tb, HEAD_A), lambda h, b, s, k=k: (b * n_tb + s, k * A_HEADS + h))

    return pl.pallas_call(
        body, name=name, grid=(A_HEADS, n_seq, n_tb),
        in_specs=[col(0), col(1), col(2), col(3),
                  pl.BlockSpec((3, HEAD_A), lambda h, b, s: (0, h)), pl.BlockSpec((1, HEAD_A), lambda h, b, s: (0, h))],
        out_specs=[pl.BlockSpec((tb, HEAD_A), lambda h, b, s: (b * n_tb + s, h)),
                   pl.BlockSpec((1, HEAD_A, HEAD_A), lambda h, b, s: ((h * n_seq + b) * n_tb + s, 0, 0))],
        out_shape=[jax.ShapeDtypeStruct((t, A_WIDTH), BF16),
                   jax.ShapeDtypeStruct((A_HEADS * n_seq * n_tb, HEAD_A, HEAD_A), F32)],
        scratch_shapes=[pltpu.VMEM((HEAD_A, HEAD_A), F32)],
        compiler_params=_params(("parallel", "parallel", "arbitrary")),
    )(proj, proj, proj, proj, lb_table, a_norm)


def _hgrn2_bwd(proj, dcat, states, lb_table, a_norm, name):
    t = proj.shape[0]
    tb = HGRN_BLOCK
    n_tb = SEQ // tb
    n_seq = t // SEQ
    n_sub = tb // SUB_CHUNK

    def body(q_ref, f_ref, i_ref, g_ref, do_ref, st_in_ref, lbt_ref, an_ref,
             dq_ref, df_ref, di_ref, dg_ref, dlb_ref, dan_ref, sts_ref, dst_ref, dlb_acc, dan_acc):
        b, s = pl.program_id(1), pl.program_id(2)

        @pl.when(s == 0)
        def _():
            dst_ref[...] = jnp.zeros_like(dst_ref)

        @pl.when((b == 0) & (s == 0))
        def _():
            dlb_acc[...] = jnp.zeros_like(dlb_acc)
            dan_acc[...] = jnp.zeros_like(dan_acc)

        lb = _lower_bound(lbt_ref[...])
        an = an_ref[...]
        tri = _tri()

        def rows_of(c):
            return pl.ds(pl.multiple_of(c * SUB_CHUNK, SUB_CHUNK), SUB_CHUNK)

        def fwd(c, st):
            sts_ref[c] = st
            rows = rows_of(c)
            st_new, _ = _hgrn2_step(st, q_ref[rows, :], f_ref[rows, :], i_ref[rows, :], g_ref[rows, :], lb, an, tri)
            return st_new

        lax.fori_loop(0, n_sub, fwd, st_in_ref[0])

        def bwd(k, carry):
            c = n_sub - 1 - k
            rows = rows_of(c)
            _, vjp = jax.vjp(
                lambda st, a1, a2, a3, a4, a5, a6: _hgrn2_step(st, a1, a2, a3, a4, a5, a6, tri),
                sts_ref[c], q_ref[rows, :], f_ref[rows, :], i_ref[rows, :], g_ref[rows, :], lb, an)
            dst, dq, df, di, dg, dlb, dan = vjp((dst_ref[...], do_ref[rows, :]))
            dst_ref[...] = dst
            dq_ref[rows, :] = dq.astype(BF16)
            df_ref[rows, :] = df.astype(BF16)
            di_ref[rows, :] = di.astype(BF16)
            dg_ref[rows, :] = dg.astype(BF16)
            dlb_acc[...] += dlb
            dan_acc[...] += dan
            return carry

        lax.fori_loop(0, n_sub, bwd, 0)

        @pl.when((b == n_seq - 1) & (s == n_tb - 1))
        def _():
            first = lax.broadcasted_iota(jnp.int32, (SUBLANES, HEAD_A), 0) == 0
            dlb_ref[...] = jnp.where(first, dlb_acc[...], 0.0)
            dan_ref[...] = jnp.where(first, dan_acc[...], 0.0)

    def rev(s):
        return n_tb - 1 - s

    def col(k):
        return pl.BlockSpec((tb, HEAD_A), lambda h, b, s, k=k: (b * n_tb + rev(s), k * A_HEADS + h))

    out_col = pl.BlockSpec((tb, HEAD_A), lambda h, b, s: (b * n_tb + rev(s), h))
    acc8 = pl.BlockSpec((SUBLANES, HEAD_A), lambda h, b, s: (0, h))
    return pl.pallas_call(
        body, name=name, grid=(A_HEADS, n_seq, n_tb),
        in_specs=[col(0), col(1), col(2), col(3), out_col,
                  pl.BlockSpec((1, HEAD_A, HEAD_A), lambda h, b, s: ((h * n_seq + b) * n_tb + rev(s), 0, 0)),
                  pl.BlockSpec((3, HEAD_A), lambda h, b, s: (0, h)), pl.BlockSpec((1, HEAD_A), lambda h, b, s: (0, h))],
        out_specs=[out_col] * 4 + [acc8, acc8],
        out_shape=[jax.ShapeDtypeStruct((t, A_WIDTH), BF16)] * 4 + [jax.ShapeDtypeStruct((SUBLANES, A_WIDTH), F32)] * 2,
        scratch_shapes=[pltpu.VMEM((n_sub, HEAD_A, HEAD_A), F32), pltpu.VMEM((HEAD_A, HEAD_A), F32),
                        pltpu.VMEM((1, HEAD_A), F32), pltpu.VMEM((1, HEAD_A), F32)],
        compiler_params=_params(("arbitrary", "arbitrary", "arbitrary")),
    )(proj, proj, proj, proj, dcat, states, lb_table, a_norm)


GMLP_ROWS = 512


def _gmlp_chunk(ub, vb, ln_g, ln_b, ws, bias):
    u = [_gelu(a) for a in ub]
    v = [_gelu(a) for a in vb]
    mu = sum(jnp.sum(a, axis=-1, keepdims=True) for a in v) * (1.0 / B_WIDTH)
    cen = [a - mu for a in v]
    var = sum(jnp.sum(a * a, axis=-1, keepdims=True) for a in cen) * (1.0 / B_WIDTH)
    inv = lax.rsqrt(var + EPS)
    r = lax.broadcasted_iota(jnp.int32, (B_CHUNK, B_CHUNK), 0)
    c = lax.broadcasted_iota(jnp.int32, (B_CHUNK, B_CHUNK), 1)
    outs = []
    for g in range(B_GROUPS):
        vn = (cen[g] * inv * ln_g[g] + ln_b[g]).astype(BF16)
        wm = jnp.where(c <= r, ws[g], 0.0).astype(BF16)
        outs.append(u[g] * (_dot(wm, vn) + bias[g]))
    return outs


def _lane_groups(ref, rows=slice(None)):
    return [ref[rows, g * LANES:(g + 1) * LANES] for g in range(B_GROUPS)]


def _gmlp_fwd(proj, ln_g, ln_b, ws, bias_t, name):
    t = proj.shape[0]
    tm = GMLP_ROWS

    def body(u_ref, v_ref, lg_ref, lb_ref, ws_ref, bt_ref, o_ref):
        for ch in range(tm // B_CHUNK):
            rows = slice(ch * B_CHUNK, (ch + 1) * B_CHUNK)
            outs = _gmlp_chunk(_lane_groups(u_ref, rows), _lane_groups(v_ref, rows), _lane_groups(lg_ref),
                               _lane_groups(lb_ref), [ws_ref[g] for g in range(B_GROUPS)],
                               [bt_ref[:, g:g + 1] for g in range(B_GROUPS)])
            for g in range(B_GROUPS):
                o_ref[rows, g * LANES:(g + 1) * LANES] = outs[g].astype(BF16)

    vec = pl.BlockSpec((1, B_WIDTH), lambda i: (0, 0))
    return pl.pallas_call(
        body, name=name, grid=(t // tm,),
        in_specs=[pl.BlockSpec((tm, B_WIDTH), lambda i: (i, 4)), pl.BlockSpec((tm, B_WIDTH), lambda i: (i, 5)), vec, vec,
                  pl.BlockSpec((B_GROUPS, B_CHUNK, B_CHUNK), lambda i: (0, 0, 0)),
                  pl.BlockSpec((B_CHUNK, B_GROUPS), lambda i: (0, 0))],
        out_specs=pl.BlockSpec((tm, B_WIDTH), lambda i: (i, 0)),
        out_shape=jax.ShapeDtypeStruct((t, B_WIDTH), BF16),
        compiler_params=_params(("parallel",)),
    )(proj, proj, ln_g, ln_b, ws, bias_t)


def _gmlp_bwd(proj, dcat, ln_g, ln_b, ws, bias_t, name):
    t = proj.shape[0]
    tm = GMLP_ROWS

    def body(u_ref, v_ref, do_ref, lg_ref, lb_ref, ws_ref, bt_ref, duv_ref, dlg_ref, dlb_ref, dws_ref, dbt_ref):
        @pl.when(pl.program_id(0) == 0)
        def _():
            dlg_ref[...] = jnp.zeros_like(dlg_ref)
            dlb_ref[...] = jnp.zeros_like(dlb_ref)
            dws_ref[...] = jnp.zeros_like(dws_ref)
            dbt_ref[...] = jnp.zeros_like(dbt_ref)

        for ch in range(tm // B_CHUNK):
            rows = slice(ch * B_CHUNK, (ch + 1) * B_CHUNK)
            _, vjp = jax.vjp(
                _gmlp_chunk, _lane_groups(u_ref, rows), _lane_groups(v_ref, rows), _lane_groups(lg_ref),
                _lane_groups(lb_ref), [ws_ref[g] for g in range(B_GROUPS)],
                [bt_ref[:, g:g + 1] for g in range(B_GROUPS)])
            du, dv, dlg, dlb, dw, dbt = vjp(_lane_groups(do_ref, rows))
            for g in range(B_GROUPS):
                lanes = slice(g * LANES, (g + 1) * LANES)
                duv_ref[rows, lanes] = du[g].astype(BF16)
                duv_ref[rows, B_WIDTH + g * LANES:B_WIDTH + (g + 1) * LANES] = dv[g].astype(BF16)
                dlg_ref[0:1, lanes] += dlg[g]
                dlb_ref[0:1, lanes] += dlb[g]
                dws_ref[g] += dw[g]
                dbt_ref[:, g:g + 1] += dbt[g]

    vec = pl.BlockSpec((1, B_WIDTH), lambda i: (0, 0))
    acc8 = pl.BlockSpec((SUBLANES, B_WIDTH), lambda i: (0, 0))
    ws_spec = pl.BlockSpec((B_GROUPS, B_CHUNK, B_CHUNK), lambda i: (0, 0, 0))
    bt_spec = pl.BlockSpec((B_CHUNK, B_GROUPS), lambda i: (0, 0))
    return pl.pallas_call(
        body, name=name, grid=(t // tm,),
        in_specs=[pl.BlockSpec((tm, B_WIDTH), lambda i: (i, 4)), pl.BlockSpec((tm, B_WIDTH), lambda i: (i, 5)),
                  pl.BlockSpec((tm, B_WIDTH), lambda i: (i, 1)), vec, vec, ws_spec, bt_spec],
        out_specs=[pl.BlockSpec((tm, 2 * B_WIDTH), lambda i: (i, 0)), acc8, acc8, ws_spec, bt_spec],
        out_shape=[jax.ShapeDtypeStruct((t, 2 * B_WIDTH), BF16), jax.ShapeDtypeStruct((SUBLANES, B_WIDTH), F32),
                   jax.ShapeDtypeStruct((SUBLANES, B_WIDTH), F32),
                   jax.ShapeDtypeStruct((B_GROUPS, B_CHUNK, B_CHUNK), F32),
                   jax.ShapeDtypeStruct((B_CHUNK, B_GROUPS), F32)],
        compiler_params=_params(("arbitrary",)),
    )(proj, proj, dcat, ln_g, ln_b, ws, bias_t)


QK_SCALE = 1.0 / math.sqrt(C_HEAD_DIM)
LANE_GROUPS = D_MODEL // LANES


def _attn_masks(n):
    i = lax.broadcasted_iota(jnp.int32, (C_BLOCK, C_BLOCK), 0)
    j = lax.broadcasted_iota(jnp.int32, (C_BLOCK, C_BLOCK), 1)
    return j <= i, (j >= i) & (n > 0)


def _attn_branch_fwd(qkv, d, name):
    t = qkv.shape[0]
    n_seq = t // SEQ
    l = SEQ // d
    n_blk = l // C_BLOCK
    view = qkv.reshape(n_seq, l, d * ODD_IN)
    groups = ODD_IN // LANES

    def body(q_ref, kc_ref, kp_ref, vc_ref, vp_ref, o_ref, m_ref, den_ref):
        mask_c, mask_p = _attn_masks(pl.program_id(3))
        for hh in range(2):
            sl = slice(hh * C_HEAD_DIM, (hh + 1) * C_HEAD_DIM)
            q = q_ref[0, :, sl]
            sc = jnp.where(mask_c, _dot_nt(q, kc_ref[0, :, sl]) * QK_SCALE, NEG)
            sp = jnp.where(mask_p, _dot_nt(q, kp_ref[0, :, sl]) * QK_SCALE, NEG)
            m = jnp.maximum(jnp.max(sc, axis=-1, keepdims=True), jnp.max(sp, axis=-1, keepdims=True))
            pc = jnp.exp(sc - m)
            pp = jnp.exp(sp - m)
            den = jnp.sum(pc, axis=-1, keepdims=True) + jnp.sum(pp, axis=-1, keepdims=True)
            o = (_dot(pc.astype(BF16), vc_ref[0, :, sl]) + _dot(pp.astype(BF16), vp_ref[0, :, sl])) / den
            o_ref[0, :, sl] = o
            m_ref[0, :, sl] = jnp.broadcast_to(m, (C_BLOCK, C_HEAD_DIM))
            den_ref[0, :, sl] = jnp.broadcast_to(den, (C_BLOCK, C_HEAD_DIM))

    def spec(section, prev):
        def index(b, r, g, n):
            return (b, jnp.maximum(n - 1, 0) if prev else n, r * groups + section * LANE_GROUPS + g)
        return pl.BlockSpec((1, C_BLOCK, LANES), index)

    out_spec = pl.BlockSpec((1, C_BLOCK, LANES), lambda b, r, g, n: (b, n, r * LANE_GROUPS + g))
    outs = pl.pallas_call(
        body, name=name, grid=(n_seq, d, LANE_GROUPS, n_blk),
        in_specs=[spec(0, False), spec(1, False), spec(1, True), spec(2, False), spec(2, True)],
        out_specs=[out_spec] * 3,
        out_shape=[jax.ShapeDtypeStruct((n_seq, l, d * D_MODEL), F32)] * 3,
        compiler_params=_params(("parallel", "parallel", "parallel", "parallel")),
    )(view, view, view, view, view)
    return [o.reshape(t, D_MODEL) for o in outs]


def _attn_merge(branches, name):
    t = branches[0][0].shape[0]
    tm = ROW_TILE

    def body(*refs):
        o_ref, lse_ref, ob_ref = refs[9:]
        os_ = [refs[3 * b][...] for b in range(3)]
        ms = [refs[3 * b + 1][...] for b in range(3)]
        dens = [refs[3 * b + 2][...] for b in range(3)]
        m_all = jnp.maximum(jnp.maximum(ms[0], ms[1]), ms[2])
        ws = [dens[b] * jnp.exp(ms[b] - m_all) for b in range(3)]
        total = ws[0] + ws[1] + ws[2]
        o = (ws[0] * os_[0] + ws[1] * os_[1] + ws[2] * os_[2]) / total
        o_ref[...] = o
        ob_ref[...] = o.astype(BF16)
        lse_ref[...] = m_all + jnp.log(total)

    row = pl.BlockSpec((tm, D_MODEL), lambda i: (i, 0))
    flat = [a for br in branches for a in br]
    return pl.pallas_call(
        body, name=name, grid=(t // tm,), in_specs=[row] * 9, out_specs=[row] * 3,
        out_shape=[jax.ShapeDtypeStruct((t, D_MODEL), F32)] * 2 + [jax.ShapeDtypeStruct((t, D_MODEL), BF16)],
        compiler_params=_params(("parallel",)),
    )(*flat)


def _attn_branch_bwd(qkv, dout, out, lse, d, name):
    t = qkv.shape[0]
    n_seq = t // SEQ
    l = SEQ // d
    n_blk = l // C_BLOCK
    view = qkv.reshape(n_seq, l, d * ODD_IN)
    groups = ODD_IN // LANES
    wide = lambda a: a.reshape(n_seq, l, d * D_MODEL)

    def body(q_ref, qn_ref, kc_ref, kp_ref, vc_ref, vp_ref, do_ref, don_ref, o_ref, on_ref, lse_ref, lsen_ref,
             dq_ref, dk_ref, dv_ref):
        n = pl.program_id(3)
        mask_c, mask_p = _attn_masks(n)
        i = lax.broadcasted_iota(jnp.int32, (C_BLOCK, C_BLOCK), 0)
        j = lax.broadcasted_iota(jnp.int32, (C_BLOCK, C_BLOCK), 1)
        mask_n = (j >= i) & (n < n_blk - 1)

        def pair(q, k, v, do, delta, lse_col, mask):
            s = jnp.where(mask, _dot_nt(q, k) * QK_SCALE, NEG)
            p = jnp.exp(s - lse_col)
            dp = _dot_nt(do, v)
            ds = (p * (dp - delta) * QK_SCALE).astype(BF16)
            return p.astype(BF16), ds

        for hh in range(2):
            sl = slice(hh * C_HEAD_DIM, (hh + 1) * C_HEAD_DIM)
            q, qn = q_ref[0, :, sl], qn_ref[0, :, sl]
            kc, kp, vc, vp = kc_ref[0, :, sl], kp_ref[0, :, sl], vc_ref[0, :, sl], vp_ref[0, :, sl]
            do_f, don_f = do_ref[0, :, sl], don_ref[0, :, sl]
            delta = jnp.sum(do_f * o_ref[0, :, sl], axis=-1, keepdims=True)
            delta_n = jnp.sum(don_f * on_ref[0, :, sl], axis=-1, keepdims=True)
            lse_c = lse_ref[0, :, sl][:, 0:1]
            lse_n = lsen_ref[0, :, sl][:, 0:1]
            do_b, don_b = do_f.astype(BF16), don_f.astype(BF16)
            p1, ds1 = pair(q, kc, vc, do_b, delta, lse_c, mask_c)
            _, ds2 = pair(q, kp, vp, do_b, delta, lse_c, mask_p)
            p3, ds3 = pair(qn, kc, vc, don_b, delta_n, lse_n, mask_n)
            dq_ref[0, :, sl] = _dot(ds1, kc) + _dot(ds2, kp)
            dk_ref[0, :, sl] = _dot_tn(ds1, q) + _dot_tn(ds3, qn)
            dv_ref[0, :, sl] = _dot_tn(p1, do_b) + _dot_tn(p3, don_b)

    def qkv_spec(section, shift):
        def index(b, r, g, n):
            blk = jnp.clip(n + shift, 0, n_blk - 1)
            return (b, blk, r * groups + section * LANE_GROUPS + g)
        return pl.BlockSpec((1, C_BLOCK, LANES), index)

    def act_spec(shift):
        def index(b, r, g, n):
            return (b, jnp.clip(n + shift, 0, n_blk - 1), r * LANE_GROUPS + g)
        return pl.BlockSpec((1, C_BLOCK, LANES), index)

    outs = pl.pallas_call(
        body, name=name, grid=(n_seq, d, LANE_GROUPS, n_blk),
        in_specs=[qkv_spec(0, 0), qkv_spec(0, 1), qkv_spec(1, 0), qkv_spec(1, -1), qkv_spec(2, 0), qkv_spec(2, -1),
                  act_spec(0), act_spec(1), act_spec(0), act_spec(1), act_spec(0), act_spec(1)],
        out_specs=[act_spec(0)] * 3,
        out_shape=[jax.ShapeDtypeStruct((n_seq, l, d * D_MODEL), F32)] * 3,
        compiler_params=_params(("parallel", "parallel", "parallel", "parallel")),
    )(view, view, view, view, view, view, wide(dout), wide(dout), wide(out), wide(out), wide(lse), wide(lse))
    return [o.reshape(t, D_MODEL) for o in outs]


def _attn_combine_bwd(grads, rope, name):
    t = grads[0][0].shape[0]
    tm = ROW_TILE

    def body(*refs):
        c_ref, s_ref, o_ref = refs[9:]
        for sec in range(3):
            tot = refs[sec][...] + refs[3 + sec][...] + refs[6 + sec][...]
            for p in range(LANE_GROUPS):
                blk = tot[:, p * LANES:(p + 1) * LANES]
                if sec < 2:
                    blk = blk * c_ref[...] - _swap_halves(blk) * s_ref[...]
                o_ref[:, sec * D_MODEL + p * LANES:sec * D_MODEL + (p + 1) * LANES] = blk.astype(BF16)

    row = pl.BlockSpec((tm, D_MODEL), lambda i: (i, 0))
    tab = pl.BlockSpec((tm, LANES), lambda i: (i, 0))
    flat = [a for br in grads for a in br]
    return pl.pallas_call(
        body, name=name, grid=(t // tm,), in_specs=[row] * 9 + [tab, tab],
        out_specs=pl.BlockSpec((tm, ODD_IN), lambda i: (i, 0)),
        out_shape=jax.ShapeDtypeStruct((t, ODD_IN), BF16), compiler_params=_params(("parallel",)),
    )(*flat, *rope)


def _loss_grad(y, target, name):
    t = y.shape[0]
    tm = ROW_TILE

    def body(y_ref, t_ref, d_ref, l_ref):
        diff = y_ref[...] - t_ref[...]
        d_ref[...] = diff * (1.0 / D_MODEL)
        _acc_rows8(l_ref, _rows8(diff * diff) * (0.5 / D_MODEL), pl.program_id(0) == 0)

    row = pl.BlockSpec((tm, D_MODEL), lambda i: (i, 0))
    return pl.pallas_call(
        body, name=name, grid=(t // tm,), in_specs=[row, row],
        out_specs=[row, pl.BlockSpec((SUBLANES, D_MODEL), lambda i: (0, 0))],
        out_shape=[jax.ShapeDtypeStruct((t, D_MODEL), F32), jax.ShapeDtypeStruct((SUBLANES, D_MODEL), F32)],
        compiler_params=_params(("arbitrary",)),
    )(y, target)


def _adamw(w, g, m, v):
    m = ADAM_B1 * m + (1.0 - ADAM_B1) * g
    v = ADAM_B2 * v + (1.0 - ADAM_B2) * jnp.square(g)
    m_hat = m / (1.0 - ADAM_B1 ** ADAM_STEP)
    v_hat = v / (1.0 - ADAM_B2 ** ADAM_STEP)
    delta = -ADAM_LR * (m_hat / (jnp.sqrt(v_hat) + ADAM_EPS) + ADAM_WD * w)
    return delta, m, v


def _adamw_sharded(parts, w, m, v, name):
    shape = w.shape
    cols = shape[-1]
    flat = lambda a: a.reshape((-1, cols))
    rows = flat(w).shape[0]
    tr = min(rows, 256)

    def body(p_ref, w_ref, m_ref, v_ref, g_ref, d_ref, mo_ref, vo_ref):
        g = p_ref[0].astype(F32)
        for s in range(1, N_DEV):
            g = g + p_ref[s].astype(F32)
        delta, mn, vn = _adamw(w_ref[...], g, m_ref[...], v_ref[...])
        g_ref[...] = g
        d_ref[...] = delta
        mo_ref[...] = mn
        vo_ref[...] = vn

    row = pl.BlockSpec((tr, cols), lambda i: (i, 0))
    outs = pl.pallas_call(
        body, name=name, grid=(rows // tr,),
        in_specs=[pl.BlockSpec((N_DEV, tr, cols), lambda i: (0, i, 0)), row, row, row],
        out_specs=[row] * 4, out_shape=[jax.ShapeDtypeStruct((rows, cols), F32)] * 4,
        compiler_params=_params(("parallel",)),
    )(parts.reshape((N_DEV, rows, cols)), flat(w), flat(m), flat(v))
    return [o.reshape(shape) for o in outs]


def _small_update(gathered, weights, moments_m, moments_v, lb_index, name):
    n = len(weights)

    def total(ref):
        acc = ref[0]
        for s in range(1, N_DEV):
            acc = acc + ref[s]
        return acc

    def body(*refs):
        g_refs = refs[:n + 1]
        w_refs, m_refs, v_refs = refs[n + 1:2 * n + 1], refs[2 * n + 1:3 * n + 1], refs[3 * n + 1:4 * n + 1]
        outs = refs[4 * n + 1:]
        loss_rows = total(g_refs[n])
        outs[0][...] = jnp.sum(jnp.sum(loss_rows, axis=1, keepdims=True), axis=0, keepdims=True)
        for k in range(n):
            part = total(g_refs[k])
            if k == lb_index:
                dlb = jnp.sum(part, axis=0, keepdims=True)
                tab = w_refs[k][...]
                e = jnp.exp(tab - jnp.max(tab, axis=0, keepdims=True))
                p = e / jnp.sum(e, axis=0, keepdims=True)
                first = lax.broadcasted_iota(jnp.int32, p.shape, 0) == 0
                grads = [(slice(None), p * (jnp.where(first, dlb, 0.0) - p[0:1, :] * dlb))]
            elif part.shape == w_refs[k].shape:
                grads = [(slice(None), part)]
            else:
                grads = [(slice(l, l + 1), jnp.sum(part[l * SUBLANES:(l + 1) * SUBLANES], axis=0, keepdims=True))
                         for l in range(w_refs[k].shape[0])]
            for rows, g in grads:
                delta, mn, vn = _adamw(w_refs[k][rows], g, m_refs[k][rows], v_refs[k][rows])
                outs[1 + 4 * k][rows] = g
                outs[2 + 4 * k][rows] = delta
                outs[3 + 4 * k][rows] = mn
                outs[4 + 4 * k][rows] = vn

    vmem = pl.BlockSpec(memory_space=pltpu.VMEM)
    out_shape = [jax.ShapeDtypeStruct((1, 1), F32)]
    for w in weights:
        out_shape += [jax.ShapeDtypeStruct(w.shape, F32)] * 4
    args = list(gathered) + list(weights) + list(moments_m) + list(moments_v)
    return pl.pallas_call(
        body, name=name, in_specs=[vmem] * len(args), out_specs=[vmem] * len(out_shape), out_shape=out_shape,
        compiler_params=pltpu.CompilerParams(vmem_limit_bytes=VMEM_LIMIT),
    )(*args)


def kernel(x, positions, norm_mix_pre, norm_mix_post, norm_ffn_pre, norm_ffn_post, w_in_even, lb_table, a_norm, b_ln_g, b_ln_b, b_ws, b_bias, w_out_even, w_in_odd, w_out_odd, w_ff1, w_ff2, loss_target, m_norm_mix_pre, m_norm_mix_post, m_norm_ffn_pre, m_norm_ffn_post, m_w_in_even, m_lb_table, m_a_norm, m_b_ln_g, m_b_ln_b, m_b_ws, m_b_bias, m_w_out_even, m_w_in_odd, m_w_out_odd, m_w_ff1, m_w_ff2, v_norm_mix_pre, v_norm_mix_post, v_norm_ffn_pre, v_norm_ffn_post, v_w_in_even, v_lb_table, v_a_norm, v_b_ln_g, v_b_ln_b, v_b_ws, v_b_bias, v_w_out_even, v_w_in_odd, v_w_out_odd, v_w_ff1, v_w_ff2):
    n_seq = x.shape[0]
    t = n_seq * SEQ
    x0 = x.reshape(t, D_MODEL)
    target = loss_target.reshape(t, D_MODEL)

    names = ["w_in_even", "w_out_even", "w_in_odd", "w_out_odd", "w_ff1", "w_ff2"]
    shards = [w_in_even[0], w_out_even[0], w_in_odd[0], w_out_odd[0], w_ff1[0], w_ff1[1], w_ff2[0], w_ff2[1]]
    shards = [_cast_bf16(a, "cast_w%d" % k) for k, a in enumerate(shards)]
    g_in_e, g_out_e, g_in_o, g_out_o, g_ff1_0, g_ff1_1, g_ff2_0, g_ff2_1 = _exchange(shards, True, "gather_weights")
    w_out_e = g_out_e.reshape(D_MODEL, D_MODEL)
    w_out_o = g_out_o.reshape(D_MODEL, D_MODEL)
    w1 = [g_ff1_0, g_ff1_1]
    w2 = [g_ff2_0.reshape(D_FF, D_MODEL), g_ff2_1.reshape(D_FF, D_MODEL)]

    rope = _rope_tables(positions)
    bias_t = b_bias[0].T
    grads = _local_step(x0, target, rope, norm_mix_pre, norm_mix_post, norm_ffn_pre, norm_ffn_post, g_in_e, lb_table,
                        a_norm, b_ln_g, b_ln_b, b_ws[0], bias_t, w_out_e, g_in_o, w_out_o, w1, w2)
    (dx0, loss_part, dg_mix_pre, dg_mix_post, dg_ffn_pre, dg_ffn_post, d_lb, d_a_norm, d_ln_g, d_ln_b, d_ws, d_bias_t,
     gw_in_e, gw_out_e, gw_in_o, gw_out_o, gw_ff1, gw_ff2) = grads

    shard = lambda a, per: a.reshape((N_DEV, per) + a.shape[1:])
    send = [gw_in_e, shard(gw_out_e, 128), gw_in_o, shard(gw_out_o, 128),
            jnp.stack(gw_ff1, axis=1), jnp.stack([shard(a, FF_BLOCK) for a in gw_ff2], axis=1)]
    recv = _exchange(send, False, "scatter_grads")
    big = [(w_in_even, m_w_in_even, v_w_in_even), (w_out_even, m_w_out_even, v_w_out_even),
           (w_in_odd, m_w_in_odd, v_w_in_odd), (w_out_odd, m_w_out_odd, v_w_out_odd),
           (w_ff1, m_w_ff1, v_w_ff1), (w_ff2, m_w_ff2, v_w_ff2)]
    big_out = []
    for parts, (w, m, v), nm in zip(recv, big, names):
        big_out.append(_adamw_sharded(parts.reshape((N_DEV,) + w.shape), w, m, v, "adamw_" + nm))

    small_parts = [dg_mix_pre, dg_mix_post, dg_ffn_pre, dg_ffn_post,
                   d_lb, d_a_norm, d_ln_g, d_ln_b, d_ws, d_bias_t, loss_part]
    gathered = _exchange(small_parts, True, "gather_small")
    small_w = [norm_mix_pre, norm_mix_post, norm_ffn_pre, norm_ffn_post, lb_table, a_norm, b_ln_g, b_ln_b,
               b_ws[0], bias_t]
    small_m = [m_norm_mix_pre, m_norm_mix_post, m_norm_ffn_pre, m_norm_ffn_post, m_lb_table, m_a_norm, m_b_ln_g,
               m_b_ln_b, m_b_ws[0], m_b_bias[0].T]
    small_v = [v_norm_mix_pre, v_norm_mix_post, v_norm_ffn_pre, v_norm_ffn_post, v_lb_table, v_a_norm, v_b_ln_g,
               v_b_ln_b, v_b_ws[0], v_b_bias[0].T]
    small_out = _small_update(gathered, small_w, small_m, small_v, 4, "small_update")
    loss = small_out[0].reshape(())
    small = [small_out[1 + 4 * k:5 + 4 * k] for k in range(len(small_w))]
    small[8] = [a[None] for a in small[8]]
    small[9] = [a.T[None] for a in small[9]]

    per_weight = small[0:4] + [big_out[0]] + small[4:10] + big_out[1:6]
    grad_x = dx0.reshape(x.shape)
    out = [loss, grad_x]
    for kind in range(4):
        out += [p[kind] for p in per_weight]
    return tuple(out)


def _local_step(x0, target, rope, norm_mix_pre, norm_mix_post, norm_ffn_pre, norm_ffn_post, g_in_e, lb_table, a_norm,
                b_ln_g, b_ln_b, ws, bias_t, w_out_e, g_in_o, w_out_o, w1, w2):
    gain = lambda a, l: a[l:l + 1]

    proj, h_mix0 = _norm_inproj(x0, gain(norm_mix_pre, 0), g_in_e, "inproj_even")
    oa, states = _hgrn2_fwd(proj, lb_table, a_norm, "hgrn2_fwd")
    ob = _gmlp_fwd(proj, b_ln_g, b_ln_b, ws, bias_t, "gmlp_fwd")
    x1, mix0 = _outproj([oa, ob], w_out_e, x0, gain(norm_mix_post, 0), "outproj_even")
    x2, y0, h_ffn0 = _ffn_fwd(x1, gain(norm_ffn_pre, 0), w1[0], w2[0], gain(norm_ffn_post, 0), "ffn_fwd_0")
    qkv, h_mix1 = _norm_inproj(x2, gain(norm_mix_pre, 1), g_in_o, "inproj_odd", rope=rope)
    branches = [_attn_branch_fwd(qkv, d, "attn_fwd_d%d" % d) for d in C_DILATIONS]
    attn, lse, attn_b = _attn_merge(branches, "attn_merge")
    x3, mix1 = _outproj([attn_b], w_out_o, x2, gain(norm_mix_post, 1), "outproj_odd")
    x4, y1, h_ffn1 = _ffn_fwd(x3, gain(norm_ffn_pre, 1), w1[1], w2[1], gain(norm_ffn_post, 1), "ffn_fwd_1")

    dx4, loss_part = _loss_grad(x4, target, "loss_grad")

    dx3, dy1, r1, da1, dg_ffn_pre1, dg_ffn_post1 = _ffn_bwd(
        dx4, x3, y1, h_ffn1, gain(norm_ffn_pre, 1), w1[1], w2[1], gain(norm_ffn_post, 1), "ffn_bwd_1")
    gw_ff1_1 = _grad_w(h_ffn1, da1, True, "grad_w_ff1_1")
    gw_ff2_1 = _grad_w(r1, dy1, False, "grad_w_ff2_1")
    dattn, dz1, dg_mix_post1 = _outproj_bwd(dx3, mix1, gain(norm_mix_post, 1), w_out_o, "outproj_bwd_odd")
    gw_out_o = _grad_w(attn_b, dz1, False, "grad_w_out_odd")
    grads_c = [_attn_branch_bwd(qkv, dattn, attn, lse, d, "attn_bwd_d%d" % d) for d in C_DILATIONS]
    dqkv = _attn_combine_bwd(grads_c, rope, "attn_combine_bwd")
    gw_in_o = _grad_w(h_mix1, dqkv, True, "grad_w_in_odd")
    dx2, dg_mix_pre1 = _inproj_bwd(dqkv, g_in_o, dx3, x2, gain(norm_mix_pre, 1), "inproj_bwd_odd")

    dx1, dy0, r0, da0, dg_ffn_pre0, dg_ffn_post0 = _ffn_bwd(
        dx2, x1, y0, h_ffn0, gain(norm_ffn_pre, 0), w1[0], w2[0], gain(norm_ffn_post, 0), "ffn_bwd_0")
    gw_ff1_0 = _grad_w(h_ffn0, da0, True, "grad_w_ff1_0")
    gw_ff2_0 = _grad_w(r0, dy0, False, "grad_w_ff2_0")
    dcat, dz0, dg_mix_post0 = _outproj_bwd(dx1, mix0, gain(norm_mix_post, 0), w_out_e, "outproj_bwd_even")
    gw_out_e = jnp.concatenate([_grad_w(oa, dz0, False, "grad_w_out_even_a"),
                                _grad_w(ob, dz0, False, "grad_w_out_even_b")], axis=0)
    dq, df, di, dg, d_lb, d_a_norm = _hgrn2_bwd(proj, dcat, states, lb_table, a_norm, "hgrn2_bwd")
    duv, d_ln_g, d_ln_b, d_ws, d_bias_t = _gmlp_bwd(proj, dcat, b_ln_g, b_ln_b, ws, bias_t, "gmlp_bwd")
    dproj = jnp.concatenate([dq, df, di, dg, duv], axis=1)
    gw_in_e = _grad_w(h_mix0, dproj, True, "grad_w_in_even")
    dx0, dg_mix_pre0 = _inproj_bwd(dproj, g_in_e, dx1, x0, gain(norm_mix_pre, 0), "inproj_bwd_even")

    layers = lambda a, b: jnp.concatenate([a, b], axis=0)
    return (dx0, loss_part, layers(dg_mix_pre0, dg_mix_pre1), layers(dg_mix_post0, dg_mix_post1),
            layers(dg_ffn_pre0, dg_ffn_pre1), layers(dg_ffn_post0, dg_ffn_post1),
            d_lb, d_a_norm, d_ln_g, d_ln_b, d_ws, d_bias_t,
            gw_in_e, gw_out_e, gw_in_o, gw_out_o, [gw_ff1_0, gw_ff1_1], [gw_ff2_0, gw_ff2_1])
```

```python
import functools
import math

import jax
import jax.numpy as jnp
from jax import lax
from jax.experimental import pallas as pl
from jax.experimental.pallas import tpu as pltpu

F32 = jnp.float32
BF16 = jnp.bfloat16
MESH = pl.DeviceIdType.MESH

N_DEV = 8
D_MODEL = 1024
SEQ = 2048
EPS = 1e-6
A_WIDTH = 512
A_HEADS = 4
HEAD_A = 128
B_WIDTH = 512
B_GROUPS = 4
B_CHUNK = 128
C_HEADS = 16
C_HEAD_DIM = 64
C_ROT_HALF = 8
ROPE_THETA = 500000.0
C_DILATIONS = (1, 4, 16)
C_BLOCK = 128
D_FF = 4096
EVEN_IN = 3072
ODD_IN = 3072

ADAM_LR = 0.001
ADAM_B1 = 0.9
ADAM_B2 = 0.999
ADAM_EPS = 1e-08
ADAM_WD = 0.01
ADAM_STEP = 10

LANES = 128
SUBLANES = 8
ROW_TILE = 512
SUB_CHUNK = 16
HGRN_BLOCK = 256
NEG = -1e30
VMEM_LIMIT = 56 * 1024 * 1024


def _params(sem):
    return pltpu.CompilerParams(dimension_semantics=sem, vmem_limit_bytes=VMEM_LIMIT)


def _dot(a, b):
    return jnp.dot(a, b, preferred_element_type=F32)


def _dot_nt(a, b):
    return lax.dot_general(a, b, (((1,), (1,)), ((), ())), preferred_element_type=F32)


def _dot_tn(a, b):
    return lax.dot_general(a, b, (((0,), (0,)), ((), ())), preferred_element_type=F32)


def _rms(x, g):
    r = lax.rsqrt(jnp.mean(x * x, axis=-1, keepdims=True) + EPS)
    return x * r * g


def _rms_bwd(x, g, dy):
    r = lax.rsqrt(jnp.mean(x * x, axis=-1, keepdims=True) + EPS)
    dyg = dy * g
    dx = r * dyg - x * (r * r * r) * jnp.mean(x * dyg, axis=-1, keepdims=True)
    return dx, dy * x * r


def _rows8(v):
    return v.reshape(v.shape[0] // SUBLANES, SUBLANES, v.shape[1]).sum(axis=0)


def _sigmoid(x):
    return 1.0 / (1.0 + jnp.exp(-x))


def _gelu(x):
    return 0.5 * x * (1.0 + jnp.tanh(math.sqrt(2.0 / math.pi) * (x + 0.044715 * (x * x * x))))


def _acc_rows8(ref, val, first):
    @pl.when(first)
    def _():
        ref[...] = val

    @pl.when(jnp.logical_not(first))
    def _():
        ref[...] += val


def _my_slot():
    return 4 * lax.axis_index("x") + 2 * lax.axis_index("y") + lax.axis_index("c")


def _peer(r):
    x, y, c = lax.axis_index("x"), lax.axis_index("y"), lax.axis_index("c")
    px = 1 - x if (r >> 2) & 1 else x
    py = 1 - y if (r >> 1) & 1 else y
    pc = 1 - c if r & 1 else c
    return (px, py, pc), 4 * px + 2 * py + pc


def _exchange(arrays, gather, name):
    n = len(arrays)
    if gather:
        out_shape = [jax.ShapeDtypeStruct((N_DEV,) + a.shape, a.dtype) for a in arrays]
    else:
        out_shape = [jax.ShapeDtypeStruct(a.shape, a.dtype) for a in arrays]

    def body(*refs):
        ins, outs = refs[:n], refs[n:2 * n]
        send_sems, recv_sems, local_sems = refs[2 * n:]
        me = _my_slot()
        local, remote = [], []
        for k in range(n):
            src = ins[k] if gather else ins[k].at[me]
            local.append(pltpu.make_async_copy(src, outs[k].at[me], local_sems.at[k]))
            for r in range(1, N_DEV):
                peer, slot = _peer(r)
                src = ins[k] if gather else ins[k].at[slot]
                remote.append((pltpu.make_async_remote_copy(
                    src_ref=src, dst_ref=outs[k].at[me], send_sem=send_sems.at[k, r - 1],
                    recv_sem=recv_sems.at[k, r - 1], device_id=peer, device_id_type=MESH), k, r, slot))
        for cp in local:
            cp.start()
        for cp, _, _, _ in remote:
            cp.start()
        for cp, k, r, slot in remote:
            pltpu.make_async_remote_copy(
                src_ref=outs[k].at[slot], dst_ref=outs[k].at[slot], send_sem=send_sems.at[k, r - 1],
                recv_sem=recv_sems.at[k, r - 1], device_id=_peer(r)[0], device_id_type=MESH).wait_recv()
        for cp, _, _, _ in remote:
            cp.wait_send()
        for cp in local:
            cp.wait()

    any_spec = pl.BlockSpec(memory_space=pl.ANY)
    return pl.pallas_call(
        body, name=name, out_shape=out_shape,
        in_specs=[any_spec] * n, out_specs=[any_spec] * n,
        scratch_shapes=[pltpu.SemaphoreType.DMA((n, N_DEV - 1)), pltpu.SemaphoreType.DMA((n, N_DEV - 1)),
                        pltpu.SemaphoreType.DMA((n,))],
        compiler_params=pltpu.CompilerParams(has_side_effects=True),
    )(*arrays)


def _cast_bf16(a, name):
    shape = a.shape
    a2 = a.reshape((-1, shape[-1]))
    rows = a2.shape[0]
    tr = min(rows, 512)

    def body(a_ref, o_ref):
        o_ref[...] = a_ref[...].astype(BF16)

    out = pl.pallas_call(
        body, name=name, grid=(rows // tr,),
        in_specs=[pl.BlockSpec((tr, shape[-1]), lambda i: (i, 0))],
        out_specs=pl.BlockSpec((tr, shape[-1]), lambda i: (i, 0)),
        out_shape=jax.ShapeDtypeStruct(a2.shape, BF16), compiler_params=_params(("parallel",)),
    )(a2)
    return out.reshape(shape)


def _rope_tables(positions):
    inv = ROPE_THETA ** (-jnp.arange(C_ROT_HALF, dtype=F32) / C_ROT_HALF)
    ang = positions.reshape(-1)[:, None].astype(F32) * inv
    cos, sin = jnp.cos(ang), jnp.sin(ang)
    t = ang.shape[0]
    ones = jnp.ones((t, C_HEAD_DIM - 2 * C_ROT_HALF), F32)
    c_head = jnp.concatenate([cos, cos, ones], axis=1)
    s_head = jnp.concatenate([-sin, sin, 0.0 * ones], axis=1)
    return jnp.concatenate([c_head, c_head], axis=1), jnp.concatenate([s_head, s_head], axis=1)


def _swap_halves(x):
    lane = lax.broadcasted_iota(jnp.int32, x.shape, 1) % C_HEAD_DIM
    return jnp.where(lane < C_ROT_HALF, pltpu.roll(x, LANES - C_ROT_HALF, 1), pltpu.roll(x, C_ROT_HALF, 1))


def _norm_inproj(x, g, w, name, rope=None):
    t = x.shape[0]
    nb = w.shape[2]
    tm = ROW_TILE
    out_dtype = F32 if rope is None else BF16

    def body(*refs):
        if rope is None:
            x_ref, g_ref, w_ref, o_ref, h_ref = refs
        else:
            x_ref, g_ref, w_ref, c_ref, s_ref, o_ref, h_ref = refs
        j = pl.program_id(1)

        @pl.when(j == 0)
        def _():
            h_ref[...] = _rms(x_ref[...], g_ref[...]).astype(BF16)

        acc = _dot(h_ref[...], w_ref[0])
        if rope is None:
            o_ref[...] = acc
        else:
            for p in range(nb // LANES):
                blk = acc[:, p * LANES:(p + 1) * LANES]
                roped = blk * c_ref[...] + _swap_halves(blk) * s_ref[...]
                is_qk = (j * (nb // LANES) + p) < 2 * (D_MODEL // LANES)
                o_ref[:, p * LANES:(p + 1) * LANES] = jnp.where(is_qk, roped, blk).astype(BF16)

    in_specs = [pl.BlockSpec((tm, D_MODEL), lambda i, j: (i, 0)),
                pl.BlockSpec((1, D_MODEL), lambda i, j: (0, 0)),
                pl.BlockSpec((1, D_MODEL, nb), lambda i, j: (j, 0, 0))]
    args = [x, g, w]
    if rope is not None:
        in_specs += [pl.BlockSpec((tm, LANES), lambda i, j: (i, 0))] * 2
        args += list(rope)
    return pl.pallas_call(
        body, name=name, grid=(t // tm, N_DEV), in_specs=in_specs,
        out_specs=[pl.BlockSpec((tm, nb), lambda i, j: (i, j)), pl.BlockSpec((tm, D_MODEL), lambda i, j: (i, 0))],
        out_shape=[jax.ShapeDtypeStruct((t, N_DEV * nb), out_dtype), jax.ShapeDtypeStruct((t, D_MODEL), BF16)],
        compiler_params=_params(("parallel", "arbitrary")),
    )(*args)


def _outproj(parts, w, x, g, name):
    t = x.shape[0]
    tm = ROW_TILE
    n = len(parts)
    widths = [p.shape[1] for p in parts]

    def body(*refs):
        p_refs = refs[:n]
        w_ref, x_ref, g_ref, xo_ref, mix_ref = refs[n:]
        mix = None
        off = 0
        for p_ref, wd in zip(p_refs, widths):
            term = _dot(p_ref[...].astype(BF16), w_ref[off:off + wd, :])
            mix = term if mix is None else mix + term
            off += wd
        mix_ref[...] = mix
        xo_ref[...] = x_ref[...] + _rms(mix, g_ref[...])

    row = lambda i: (i, 0)
    return pl.pallas_call(
        body, name=name, grid=(t // tm,),
        in_specs=[pl.BlockSpec((tm, wd), row) for wd in widths] + [
            pl.BlockSpec((sum(widths), D_MODEL), lambda i: (0, 0)),
            pl.BlockSpec((tm, D_MODEL), row), pl.BlockSpec((1, D_MODEL), lambda i: (0, 0))],
        out_specs=[pl.BlockSpec((tm, D_MODEL), row)] * 2,
        out_shape=[jax.ShapeDtypeStruct((t, D_MODEL), F32)] * 2,
        compiler_params=_params(("parallel",)),
    )(*parts, w, x, g)


def _outproj_bwd(dx, mix, g, w, name):
    t = dx.shape[0]
    tm = ROW_TILE
    k = w.shape[0]

    def body(dx_ref, mix_ref, g_ref, w_ref, dcat_ref, dz_ref, dg_ref):
        dz, dgr = _rms_bwd(mix_ref[...], g_ref[...], dx_ref[...])
        dzb = dz.astype(BF16)
        dz_ref[...] = dzb
        dcat_ref[...] = _dot_nt(dzb, w_ref[...])
        _acc_rows8(dg_ref, _rows8(dgr), pl.program_id(0) == 0)

    row = lambda i: (i, 0)
    return pl.pallas_call(
        body, name=name, grid=(t // tm,),
        in_specs=[pl.BlockSpec((tm, D_MODEL), row), pl.BlockSpec((tm, D_MODEL), row),
                  pl.BlockSpec((1, D_MODEL), lambda i: (0, 0)), pl.BlockSpec((k, D_MODEL), lambda i: (0, 0))],
        out_specs=[pl.BlockSpec((tm, k), row), pl.BlockSpec((tm, D_MODEL), row),
                   pl.BlockSpec((SUBLANES, D_MODEL), lambda i: (0, 0))],
        out_shape=[jax.ShapeDtypeStruct((t, k), F32), jax.ShapeDtypeStruct((t, D_MODEL), BF16),
                   jax.ShapeDtypeStruct((SUBLANES, D_MODEL), F32)],
        compiler_params=_params(("arbitrary",)),
    )(dx, mix, g, w)


def _inproj_bwd(dproj, w, dx, x, g, name):
    t = x.shape[0]
    nb = w.shape[2]
    tm = ROW_TILE

    def body(dp_ref, w_ref, dx_ref, x_ref, g_ref, o_ref, dg_ref, acc_ref):
        i, j = pl.program_id(0), pl.program_id(1)
        term = _dot_nt(dp_ref[...], w_ref[0])

        @pl.when(j == 0)
        def _():
            acc_ref[...] = term

        @pl.when(j > 0)
        def _():
            acc_ref[...] += term

        @pl.when(j == N_DEV - 1)
        def _():
            dxn, dgr = _rms_bwd(x_ref[...], g_ref[...], acc_ref[...])
            o_ref[...] = dx_ref[...] + dxn
            _acc_rows8(dg_ref, _rows8(dgr), i == 0)

    return pl.pallas_call(
        body, name=name, grid=(t // tm, N_DEV),
        in_specs=[pl.BlockSpec((tm, nb), lambda i, j: (i, j)), pl.BlockSpec((1, D_MODEL, nb), lambda i, j: (j, 0, 0)),
                  pl.BlockSpec((tm, D_MODEL), lambda i, j: (i, 0)), pl.BlockSpec((tm, D_MODEL), lambda i, j: (i, 0)),
                  pl.BlockSpec((1, D_MODEL), lambda i, j: (0, 0))],
        out_specs=[pl.BlockSpec((tm, D_MODEL), lambda i, j: (i, 0)),
                   pl.BlockSpec((SUBLANES, D_MODEL), lambda i, j: (0, 0))],
        out_shape=[jax.ShapeDtypeStruct((t, D_MODEL), F32), jax.ShapeDtypeStruct((SUBLANES, D_MODEL), F32)],
        scratch_shapes=[pltpu.VMEM((tm, D_MODEL), F32)],
        compiler_params=_params(("arbitrary", "arbitrary")),
    )(dproj, w, dx, x, g)


def _grad_w(a, b, col_blocks, name):
    t, k = a.shape
    n = b.shape[1]
    tt = ROW_TILE
    tk = min(k, 512)
    tn = n // N_DEV if col_blocks else min(n, 1024)
    nt = t // tt

    def body(a_ref, b_ref, o_ref, acc_ref):
        s = pl.program_id(2)
        term = _dot_tn(a_ref[...], b_ref[...])

        @pl.when(s == 0)
        def _():
            acc_ref[...] = term

        @pl.when(s > 0)
        def _():
            acc_ref[...] += term

        @pl.when(s == nt - 1)
        def _():
            o_ref[...] = acc_ref[...].astype(BF16).reshape(o_ref.shape)

    if col_blocks:
        out_spec = pl.BlockSpec((1, tk, tn), lambda i, j, s: (j, i, 0))
        out_shape = jax.ShapeDtypeStruct((N_DEV, k, tn), BF16)
    else:
        out_spec = pl.BlockSpec((tk, tn), lambda i, j, s: (i, j))
        out_shape = jax.ShapeDtypeStruct((k, n), BF16)
    return pl.pallas_call(
        body, name=name, grid=(k // tk, n // tn, nt),
        in_specs=[pl.BlockSpec((tt, tk), lambda i, j, s: (s, i)), pl.BlockSpec((tt, tn), lambda i, j, s: (s, j))],
        out_specs=out_spec, out_shape=out_shape,
        scratch_shapes=[pltpu.VMEM((tk, tn), F32)],
        compiler_params=_params(("parallel", "parallel", "arbitrary")),
    )(a, b)


FF_BLOCK = D_FF // N_DEV


def _ffn_fwd(x, g_pre, w1, w2, g_post, name):
    t = x.shape[0]
    tm = ROW_TILE

    def body(x_ref, gp_ref, w1_ref, w2_ref, gq_ref, xo_ref, y_ref, h_ref):
        j = pl.program_id(1)

        @pl.when(j == 0)
        def _():
            h_ref[...] = _rms(x_ref[...], gp_ref[...]).astype(BF16)

        a = _dot(h_ref[...], w1_ref[0])
        r = jnp.square(jnp.maximum(a, 0.0)).astype(BF16)
        term = _dot(r, w2_ref[...])

        @pl.when(j == 0)
        def _():
            y_ref[...] = term

        @pl.when(j > 0)
        def _():
            y_ref[...] += term

        @pl.when(j == N_DEV - 1)
        def _():
            xo_ref[...] = x_ref[...] + _rms(y_ref[...], gq_ref[...])

    row = lambda i, j: (i, 0)
    vec = pl.BlockSpec((1, D_MODEL), lambda i, j: (0, 0))
    return pl.pallas_call(
        body, name=name, grid=(t // tm, N_DEV),
        in_specs=[pl.BlockSpec((tm, D_MODEL), row), vec,
                  pl.BlockSpec((1, D_MODEL, FF_BLOCK), lambda i, j: (j, 0, 0)),
                  pl.BlockSpec((FF_BLOCK, D_MODEL), lambda i, j: (j, 0)), vec],
        out_specs=[pl.BlockSpec((tm, D_MODEL), row)] * 3,
        out_shape=[jax.ShapeDtypeStruct((t, D_MODEL), F32), jax.ShapeDtypeStruct((t, D_MODEL), F32),
                   jax.ShapeDtypeStruct((t, D_MODEL), BF16)],
        compiler_params=_params(("parallel", "arbitrary")),
    )(x, g_pre, w1, w2, g_post)


def _ffn_bwd(dxo, x, y, h, g_pre, w1, w2, g_post, name):
    t = x.shape[0]
    tm = ROW_TILE

    def body(dxo_ref, x_ref, y_ref, h_ref, gp_ref, w1_ref, w2_ref, gq_ref,
             dx_ref, dy_ref, r_ref, da_ref, dgp_ref, dgq_ref, acc_ref):
        i, j = pl.program_id(0), pl.program_id(1)

        @pl.when(j == 0)
        def _():
            dy, dgr = _rms_bwd(y_ref[...], gq_ref[...], dxo_ref[...])
            dy_ref[...] = dy.astype(BF16)
            _acc_rows8(dgq_ref, _rows8(dgr), i == 0)

        a = _dot(h_ref[...], w1_ref[0])
        ra = jnp.maximum(a, 0.0)
        r_ref[...] = jnp.square(ra).astype(BF16)
        dr = _dot_nt(dy_ref[...], w2_ref[...])
        da = (dr * (2.0 * ra)).astype(BF16)
        da_ref[...] = da
        term = _dot_nt(da, w1_ref[0])

        @pl.when(j == 0)
        def _():
            acc_ref[...] = term

        @pl.when(j > 0)
        def _():
            acc_ref[...] += term

        @pl.when(j == N_DEV - 1)
        def _():
            dxn, dgr = _rms_bwd(x_ref[...], gp_ref[...], acc_ref[...])
            dx_ref[...] = dxo_ref[...] + dxn
            _acc_rows8(dgp_ref, _rows8(dgr), i == 0)

    row = lambda i, j: (i, 0)
    vec = pl.BlockSpec((1, D_MODEL), lambda i, j: (0, 0))
    acc8 = pl.BlockSpec((SUBLANES, D_MODEL), lambda i, j: (0, 0))
    return pl.pallas_call(
        body, name=name, grid=(t // tm, N_DEV),
        in_specs=[pl.BlockSpec((tm, D_MODEL), row)] * 4 + [
            vec, pl.BlockSpec((1, D_MODEL, FF_BLOCK), lambda i, j: (j, 0, 0)),
            pl.BlockSpec((FF_BLOCK, D_MODEL), lambda i, j: (j, 0)), vec],
        out_specs=[pl.BlockSpec((tm, D_MODEL), row), pl.BlockSpec((tm, D_MODEL), row),
                   pl.BlockSpec((tm, FF_BLOCK), lambda i, j: (i, j)), pl.BlockSpec((tm, FF_BLOCK), lambda i, j: (i, j)),
                   acc8, acc8],
        out_shape=[jax.ShapeDtypeStruct((t, D_MODEL), F32), jax.ShapeDtypeStruct((t, D_MODEL), BF16),
                   jax.ShapeDtypeStruct((t, D_FF), BF16), jax.ShapeDtypeStruct((t, D_FF), BF16),
                   jax.ShapeDtypeStruct((SUBLANES, D_MODEL), F32), jax.ShapeDtypeStruct((SUBLANES, D_MODEL), F32)],
        scratch_shapes=[pltpu.VMEM((tm, D_MODEL), F32)],
        compiler_params=_params(("arbitrary", "arbitrary")),
    )(dxo, x, y, h, g_pre, w1, w2, g_post)


def _lower_bound(table):
    e = jnp.exp(table - jnp.max(table, axis=0, keepdims=True))
    return e[0:1, :] / jnp.sum(e, axis=0, keepdims=True)


def _hgrn2_step(st, qraw, fl, v, graw, lb, an, tri):
    f = lb + (1.0 - lb) * _sigmoid(fl)
    logf = jnp.log(f)
    kk = 1.0 - f
    q = qraw * _sigmoid(qraw)
    gsum = jnp.dot(tri, logf, precision=lax.Precision.HIGHEST, preferred_element_type=F32)
    o = _dot_nt((q * jnp.exp(gsum)).astype(BF16), st.astype(BF16))
    row = lax.broadcasted_iota(jnp.int32, gsum.shape, 0)
    for s in range(SUB_CHUNK):
        pick = row == s
        g_s = jnp.sum(jnp.where(pick, gsum, 0.0), axis=0, keepdims=True)
        k_s = jnp.sum(jnp.where(pick, kk, 0.0), axis=0, keepdims=True)
        v_s = jnp.sum(jnp.where(pick, v, 0.0), axis=0, keepdims=True)
        decay = jnp.exp(jnp.where(row >= s, gsum - g_s, NEG))
        score = jnp.sum(q * k_s * decay, axis=1, keepdims=True)
        o = o + score * v_s
    g_last = jnp.sum(jnp.where(row == SUB_CHUNK - 1, gsum, 0.0), axis=0, keepdims=True)
    kd = kk * jnp.exp(g_last - gsum)
    st_new = st * jnp.exp(g_last) + _dot_tn(v.astype(BF16), kd.astype(BF16))
    out = _rms(o, an) * (graw * _sigmoid(graw))
    return st_new, out


def _tri():
    r = lax.broadcasted_iota(jnp.int32, (SUB_CHUNK, SUB_CHUNK), 0)
    c = lax.broadcasted_iota(jnp.int32, (SUB_CHUNK, SUB_CHUNK), 1)
    return (c <= r).astype(F32)


def _hgrn2_fwd(proj, lb_table, a_norm, name):
    t = proj.shape[0]
    tb = HGRN_BLOCK
    n_tb = SEQ // tb
    n_seq = t // SEQ
    n_sub = tb // SUB_CHUNK

    def body(q_ref, f_ref, i_ref, g_ref, lbt_ref, an_ref, o_ref, sts_ref, st_ref):
        @pl.when(pl.program_id(1) == 0)
        def _():
            st_ref[...] = jnp.zeros_like(st_ref)

        lb = _lower_bound(lbt_ref[...])
        an = an_ref[...]
        tri = _tri()

        def step(c, carry):
            rows = pl.ds(pl.multiple_of(c * SUB_CHUNK, SUB_CHUNK), SUB_CHUNK)
            for h in range(A_HEADS):
                lanes = slice(h * HEAD_A, (h + 1) * HEAD_A)
                st = st_ref[h]
                sts_ref[0, c, h] = st
                st_new, out = _hgrn2_step(st, q_ref[rows, lanes], f_ref[rows, lanes], i_ref[rows, lanes],
                                          g_ref[rows, lanes], lb[:, lanes], an[:, lanes], tri)
                st_ref[h] = st_new
                o_ref[rows, lanes] = out.astype(BF16)
            return carry

        lax.fori_loop(0, n_sub, step, 0)

    def col(k):
        return pl.BlockSpec((tb, A_WIDTH), lambda b, s, k=k: (b * n_tb + s, k))

    return pl.pallas_call(
        body, name=name, grid=(n_seq, n_tb),
        in_specs=[col(0), col(1), col(2), col(3),
                  pl.BlockSpec((3, A_WIDTH), lambda b, s: (0, 0)), pl.BlockSpec((1, A_WIDTH), lambda b, s: (0, 0))],
        out_specs=[pl.BlockSpec((tb, A_WIDTH), lambda b, s: (b * n_tb + s, 0)),
                   pl.BlockSpec((1, n_sub, A_HEADS, HEAD_A, HEAD_A), lambda b, s: (b * n_tb + s, 0, 0, 0, 0))],
        out_shape=[jax.ShapeDtypeStruct((t, A_WIDTH), BF16),
                   jax.ShapeDtypeStruct((n_seq * n_tb, n_sub, A_HEADS, HEAD_A, HEAD_A), F32)],
        scratch_shapes=[pltpu.VMEM((A_HEADS, HEAD_A, HEAD_A), F32)],
        compiler_params=_params(("parallel", "arbitrary")),
    )(proj, proj, proj, proj, lb_table, a_norm)


def _hgrn2_bwd(proj, dcat, states, lb_table, a_norm, name):
    t = proj.shape[0]
    tb = HGRN_BLOCK
    n_tb = SEQ // tb
    n_seq = t // SEQ
    n_sub = tb // SUB_CHUNK

    def body(q_ref, f_ref, i_ref, g_ref, do_ref, sts_ref, lbt_ref, an_ref, dp_ref, dlb_ref, dan_ref, dst_ref):
        b, s = pl.program_id(0), pl.program_id(1)

        @pl.when(s == 0)
        def _():
            dst_ref[...] = jnp.zeros_like(dst_ref)

        @pl.when((b == 0) & (s == 0))
        def _():
            dlb_ref[...] = jnp.zeros_like(dlb_ref)
            dan_ref[...] = jnp.zeros_like(dan_ref)

        lb = _lower_bound(lbt_ref[...])
        an = an_ref[...]
        tri = _tri()

        def bwd(k, carry):
            c = n_sub - 1 - k
            rows = pl.ds(pl.multiple_of(c * SUB_CHUNK, SUB_CHUNK), SUB_CHUNK)
            for h in range(A_HEADS):
                lanes = slice(h * HEAD_A, (h + 1) * HEAD_A)
                _, vjp = jax.vjp(
                    lambda st, a1, a2, a3, a4, a5, a6: _hgrn2_step(st, a1, a2, a3, a4, a5, a6, tri),
                    sts_ref[0, c, h], q_ref[rows, lanes], f_ref[rows, lanes], i_ref[rows, lanes], g_ref[rows, lanes],
                    lb[:, lanes], an[:, lanes])
                dst, dq, df, di, dg, dlb, dan = vjp((dst_ref[h], do_ref[rows, lanes]))
                dst_ref[h] = dst
                for sec, val in enumerate((dq, df, di, dg)):
                    dp_ref[rows, sec * A_WIDTH + h * HEAD_A:sec * A_WIDTH + (h + 1) * HEAD_A] = val.astype(BF16)
                dlb_ref[0:1, lanes] += dlb
                dan_ref[0:1, lanes] += dan
            return carry

        lax.fori_loop(0, n_sub, bwd, 0)

    def rev(s):
        return n_tb - 1 - s

    def col(k):
        return pl.BlockSpec((tb, A_WIDTH), lambda b, s, k=k: (b * n_tb + rev(s), k))

    acc8 = pl.BlockSpec((SUBLANES, A_WIDTH), lambda b, s: (0, 0))
    return pl.pallas_call(
        body, name=name, grid=(n_seq, n_tb),
        in_specs=[col(0), col(1), col(2), col(3), col(0),
                  pl.BlockSpec((1, n_sub, A_HEADS, HEAD_A, HEAD_A), lambda b, s: (b * n_tb + rev(s), 0, 0, 0, 0)),
                  pl.BlockSpec((3, A_WIDTH), lambda b, s: (0, 0)), pl.BlockSpec((1, A_WIDTH), lambda b, s: (0, 0))],
        out_specs=[pl.BlockSpec((tb, 4 * A_WIDTH), lambda b, s: (b * n_tb + rev(s), 0)), acc8, acc8],
        out_shape=[jax.ShapeDtypeStruct((t, 4 * A_WIDTH), BF16)] + [jax.ShapeDtypeStruct((SUBLANES, A_WIDTH), F32)] * 2,
        scratch_shapes=[pltpu.VMEM((A_HEADS, HEAD_A, HEAD_A), F32)],
        compiler_params=_params(("arbitrary", "arbitrary")),
    )(proj, proj, proj, proj, dcat, states, lb_table, a_norm)


GMLP_ROWS = 512


def _gmlp_chunk(ub, vb, ln_g, ln_b, ws, bias):
    u = [_gelu(a) for a in ub]
    v = [_gelu(a) for a in vb]
    mu = sum(jnp.sum(a, axis=-1, keepdims=True) for a in v) * (1.0 / B_WIDTH)
    cen = [a - mu for a in v]
    var = sum(jnp.sum(a * a, axis=-1, keepdims=True) for a in cen) * (1.0 / B_WIDTH)
    inv = lax.rsqrt(var + EPS)
    r = lax.broadcasted_iota(jnp.int32, (B_CHUNK, B_CHUNK), 0)
    c = lax.broadcasted_iota(jnp.int32, (B_CHUNK, B_CHUNK), 1)
    outs = []
    for g in range(B_GROUPS):
        vn = (cen[g] * inv * ln_g[g] + ln_b[g]).astype(BF16)
        wm = jnp.where(c <= r, ws[g], 0.0).astype(BF16)
        outs.append(u[g] * (_dot(wm, vn) + bias[g]))
    return outs


def _lane_groups(ref, rows=slice(None)):
    return [ref[rows, g * LANES:(g + 1) * LANES] for g in range(B_GROUPS)]


def _gmlp_fwd(proj, ln_g, ln_b, ws, bias_t, name):
    t = proj.shape[0]
    tm = GMLP_ROWS

    def body(u_ref, v_ref, lg_ref, lb_ref, ws_ref, bt_ref, o_ref):
        for ch in range(tm // B_CHUNK):
            rows = slice(ch * B_CHUNK, (ch + 1) * B_CHUNK)
            outs = _gmlp_chunk(_lane_groups(u_ref, rows), _lane_groups(v_ref, rows), _lane_groups(lg_ref),
                               _lane_groups(lb_ref), [ws_ref[g] for g in range(B_GROUPS)],
                               [bt_ref[:, g:g + 1] for g in range(B_GROUPS)])
            for g in range(B_GROUPS):
                o_ref[rows, g * LANES:(g + 1) * LANES] = outs[g].astype(BF16)

    vec = pl.BlockSpec((1, B_WIDTH), lambda i: (0, 0))
    return pl.pallas_call(
        body, name=name, grid=(t // tm,),
        in_specs=[pl.BlockSpec((tm, B_WIDTH), lambda i: (i, 4)), pl.BlockSpec((tm, B_WIDTH), lambda i: (i, 5)), vec, vec,
                  pl.BlockSpec((B_GROUPS, B_CHUNK, B_CHUNK), lambda i: (0, 0, 0)),
                  pl.BlockSpec((B_CHUNK, B_GROUPS), lambda i: (0, 0))],
        out_specs=pl.BlockSpec((tm, B_WIDTH), lambda i: (i, 0)),
        out_shape=jax.ShapeDtypeStruct((t, B_WIDTH), BF16),
        compiler_params=_params(("parallel",)),
    )(proj, proj, ln_g, ln_b, ws, bias_t)


def _gmlp_bwd(proj, dcat, ln_g, ln_b, ws, bias_t, name):
    t = proj.shape[0]
    tm = GMLP_ROWS

    def body(u_ref, v_ref, do_ref, lg_ref, lb_ref, ws_ref, bt_ref, duv_ref, dlg_ref, dlb_ref, dws_ref, dbt_ref):
        @pl.when(pl.program_id(0) == 0)
        def _():
            dlg_ref[...] = jnp.zeros_like(dlg_ref)
            dlb_ref[...] = jnp.zeros_like(dlb_ref)
            dws_ref[...] = jnp.zeros_like(dws_ref)
            dbt_ref[...] = jnp.zeros_like(dbt_ref)

        for ch in range(tm // B_CHUNK):
            rows = slice(ch * B_CHUNK, (ch + 1) * B_CHUNK)
            _, vjp = jax.vjp(
                _gmlp_chunk, _lane_groups(u_ref, rows), _lane_groups(v_ref, rows), _lane_groups(lg_ref),
                _lane_groups(lb_ref), [ws_ref[g] for g in range(B_GROUPS)],
                [bt_ref[:, g:g + 1] for g in range(B_GROUPS)])
            du, dv, dlg, dlb, dw, dbt = vjp(_lane_groups(do_ref, rows))
            for g in range(B_GROUPS):
                lanes = slice(g * LANES, (g + 1) * LANES)
                duv_ref[rows, lanes] = du[g].astype(BF16)
                duv_ref[rows, B_WIDTH + g * LANES:B_WIDTH + (g + 1) * LANES] = dv[g].astype(BF16)
                dlg_ref[0:1, lanes] += dlg[g]
                dlb_ref[0:1, lanes] += dlb[g]
                dws_ref[g] += dw[g]
                dbt_ref[:, g:g + 1] += dbt[g]

    vec = pl.BlockSpec((1, B_WIDTH), lambda i: (0, 0))
    acc8 = pl.BlockSpec((SUBLANES, B_WIDTH), lambda i: (0, 0))
    ws_spec = pl.BlockSpec((B_GROUPS, B_CHUNK, B_CHUNK), lambda i: (0, 0, 0))
    bt_spec = pl.BlockSpec((B_CHUNK, B_GROUPS), lambda i: (0, 0))
    return pl.pallas_call(
        body, name=name, grid=(t // tm,),
        in_specs=[pl.BlockSpec((tm, B_WIDTH), lambda i: (i, 4)), pl.BlockSpec((tm, B_WIDTH), lambda i: (i, 5)),
                  pl.BlockSpec((tm, B_WIDTH), lambda i: (i, 1)), vec, vec, ws_spec, bt_spec],
        out_specs=[pl.BlockSpec((tm, 2 * B_WIDTH), lambda i: (i, 0)), acc8, acc8, ws_spec, bt_spec],
        out_shape=[jax.ShapeDtypeStruct((t, 2 * B_WIDTH), BF16), jax.ShapeDtypeStruct((SUBLANES, B_WIDTH), F32),
                   jax.ShapeDtypeStruct((SUBLANES, B_WIDTH), F32),
                   jax.ShapeDtypeStruct((B_GROUPS, B_CHUNK, B_CHUNK), F32),
                   jax.ShapeDtypeStruct((B_CHUNK, B_GROUPS), F32)],
        compiler_params=_params(("arbitrary",)),
    )(proj, proj, dcat, ln_g, ln_b, ws, bias_t)


QK_SCALE = 1.0 / math.sqrt(C_HEAD_DIM)
LANE_GROUPS = D_MODEL // LANES


def _attn_masks(n):
    i = lax.broadcasted_iota(jnp.int32, (C_BLOCK, C_BLOCK), 0)
    j = lax.broadcasted_iota(jnp.int32, (C_BLOCK, C_BLOCK), 1)
    return j <= i, (j >= i) & (n > 0)


def _attn_branch_fwd(qkv, d, name):
    t = qkv.shape[0]
    n_seq = t // SEQ
    l = SEQ // d
    n_blk = l // C_BLOCK
    view = qkv.reshape(n_seq, l, d * ODD_IN)
    groups = ODD_IN // LANES

    def body(q_ref, kc_ref, kp_ref, vc_ref, vp_ref, o_ref, m_ref, den_ref):
        mask_c, mask_p = _attn_masks(pl.program_id(3))
        for hh in range(2):
            sl = slice(hh * C_HEAD_DIM, (hh + 1) * C_HEAD_DIM)
            q = q_ref[0, :, sl]
            sc = jnp.where(mask_c, _dot_nt(q, kc_ref[0, :, sl]) * QK_SCALE, NEG)
            sp = jnp.where(mask_p, _dot_nt(q, kp_ref[0, :, sl]) * QK_SCALE, NEG)
            m = jnp.maximum(jnp.max(sc, axis=-1, keepdims=True), jnp.max(sp, axis=-1, keepdims=True))
            pc = jnp.exp(sc - m)
            pp = jnp.exp(sp - m)
            den = jnp.sum(pc, axis=-1, keepdims=True) + jnp.sum(pp, axis=-1, keepdims=True)
            o = (_dot(pc.astype(BF16), vc_ref[0, :, sl]) + _dot(pp.astype(BF16), vp_ref[0, :, sl])) / den
            o_ref[0, :, sl] = o
            m_ref[0, :, sl] = jnp.broadcast_to(m, (C_BLOCK, C_HEAD_DIM))
            den_ref[0, :, sl] = jnp.broadcast_to(den, (C_BLOCK, C_HEAD_DIM))

    def spec(section, prev):
        def index(b, r, g, n):
            return (b, jnp.maximum(n - 1, 0) if prev else n, r * groups + section * LANE_GROUPS + g)
        return pl.BlockSpec((1, C_BLOCK, LANES), index)

    out_spec = pl.BlockSpec((1, C_BLOCK, LANES), lambda b, r, g, n: (b, n, r * LANE_GROUPS + g))
    outs = pl.pallas_call(
        body, name=name, grid=(n_seq, d, LANE_GROUPS, n_blk),
        in_specs=[spec(0, False), spec(1, False), spec(1, True), spec(2, False), spec(2, True)],
        out_specs=[out_spec] * 3,
        out_shape=[jax.ShapeDtypeStruct((n_seq, l, d * D_MODEL), F32)] * 3,
        compiler_params=_params(("parallel", "parallel", "parallel", "parallel")),
    )(view, view, view, view, view)
    return [o.reshape(t, D_MODEL) for o in outs]


def _attn_merge(branches, name):
    t = branches[0][0].shape[0]
    tm = ROW_TILE

    def body(*refs):
        o_ref, lse_ref, ob_ref = refs[9:]
        os_ = [refs[3 * b][...] for b in range(3)]
        ms = [refs[3 * b + 1][...] for b in range(3)]
        dens = [refs[3 * b + 2][...] for b in range(3)]
        m_all = jnp.maximum(jnp.maximum(ms[0], ms[1]), ms[2])
        ws = [dens[b] * jnp.exp(ms[b] - m_all) for b in range(3)]
        total = ws[0] + ws[1] + ws[2]
        o = (ws[0] * os_[0] + ws[1] * os_[1] + ws[2] * os_[2]) / total
        o_ref[...] = o
        ob_ref[...] = o.astype(BF16)
        lse_ref[...] = m_all + jnp.log(total)

    row = pl.BlockSpec((tm, D_MODEL), lambda i: (i, 0))
    flat = [a for br in branches for a in br]
    return pl.pallas_call(
        body, name=name, grid=(t // tm,), in_specs=[row] * 9, out_specs=[row] * 3,
        out_shape=[jax.ShapeDtypeStruct((t, D_MODEL), F32)] * 2 + [jax.ShapeDtypeStruct((t, D_MODEL), BF16)],
        compiler_params=_params(("parallel",)),
    )(*flat)


def _attn_branch_bwd(qkv, dout, out, lse, d, name):
    t = qkv.shape[0]
    n_seq = t // SEQ
    l = SEQ // d
    n_blk = l // C_BLOCK
    view = qkv.reshape(n_seq, l, d * ODD_IN)
    groups = ODD_IN // LANES
    wide = lambda a: a.reshape(n_seq, l, d * D_MODEL)

    def body(q_ref, qn_ref, kc_ref, kp_ref, vc_ref, vp_ref, do_ref, don_ref, o_ref, on_ref, lse_ref, lsen_ref,
             dq_ref, dk_ref, dv_ref):
        n = pl.program_id(3)
        mask_c, mask_p = _attn_masks(n)
        i = lax.broadcasted_iota(jnp.int32, (C_BLOCK, C_BLOCK), 0)
        j = lax.broadcasted_iota(jnp.int32, (C_BLOCK, C_BLOCK), 1)
        mask_n = (j >= i) & (n < n_blk - 1)

        def pair(q, k, v, do, delta, lse_col, mask):
            s = jnp.where(mask, _dot_nt(q, k) * QK_SCALE, NEG)
            p = jnp.exp(s - lse_col)
            dp = _dot_nt(do, v)
            ds = (p * (dp - delta) * QK_SCALE).astype(BF16)
            return p.astype(BF16), ds

        for hh in range(2):
            sl = slice(hh * C_HEAD_DIM, (hh + 1) * C_HEAD_DIM)
            q, qn = q_ref[0, :, sl], qn_ref[0, :, sl]
            kc, kp, vc, vp = kc_ref[0, :, sl], kp_ref[0, :, sl], vc_ref[0, :, sl], vp_ref[0, :, sl]
            do_f, don_f = do_ref[0, :, sl], don_ref[0, :, sl]
            delta = jnp.sum(do_f * o_ref[0, :, sl], axis=-1, keepdims=True)
            delta_n = jnp.sum(don_f * on_ref[0, :, sl], axis=-1, keepdims=True)
            lse_c = lse_ref[0, :, sl][:, 0:1]
            lse_n = lsen_ref[0, :, sl][:, 0:1]
            do_b, don_b = do_f.astype(BF16), don_f.astype(BF16)
            p1, ds1 = pair(q, kc, vc, do_b, delta, lse_c, mask_c)
            _, ds2 = pair(q, kp, vp, do_b, delta, lse_c, mask_p)
            p3, ds3 = pair(qn, kc, vc, don_b, delta_n, lse_n, mask_n)
            dq_ref[0, :, sl] = _dot(ds1, kc) + _dot(ds2, kp)
            dk_ref[0, :, sl] = _dot_tn(ds1, q) + _dot_tn(ds3, qn)
            dv_ref[0, :, sl] = _dot_tn(p1, do_b) + _dot_tn(p3, don_b)

    def qkv_spec(section, shift):
        def index(b, r, g, n):
            blk = jnp.clip(n + shift, 0, n_blk - 1)
            return (b, blk, r * groups + section * LANE_GROUPS + g)
        return pl.BlockSpec((1, C_BLOCK, LANES), index)

    def act_spec(shift):
        def index(b, r, g, n):
            return (b, jnp.clip(n + shift, 0, n_blk - 1), r * LANE_GROUPS + g)
        return pl.BlockSpec((1, C_BLOCK, LANES), index)

    outs = pl.pallas_call(
        body, name=name, grid=(n_seq, d, LANE_GROUPS, n_blk),
        in_specs=[qkv_spec(0, 0), qkv_spec(0, 1), qkv_spec(1, 0), qkv_spec(1, -1), qkv_spec(2, 0), qkv_spec(2, -1),
                  act_spec(0), act_spec(1), act_spec(0), act_spec(1), act_spec(0), act_spec(1)],
        out_specs=[act_spec(0)] * 3,
        out_shape=[jax.ShapeDtypeStruct((n_seq, l, d * D_MODEL), F32)] * 3,
        compiler_params=_params(("parallel", "parallel", "parallel", "parallel")),
    )(view, view, view, view, view, view, wide(dout), wide(dout), wide(out), wide(out), wide(lse), wide(lse))
    return [o.reshape(t, D_MODEL) for o in outs]


def _attn_combine_bwd(grads, rope, name):
    t = grads[0][0].shape[0]
    tm = ROW_TILE

    def body(*refs):
        c_ref, s_ref, o_ref = refs[9:]
        for sec in range(3):
            tot = refs[sec][...] + refs[3 + sec][...] + refs[6 + sec][...]
            for p in range(LANE_GROUPS):
                blk = tot[:, p * LANES:(p + 1) * LANES]
                if sec < 2:
                    blk = blk * c_ref[...] - _swap_halves(blk) * s_ref[...]
                o_ref[:, sec * D_MODEL + p * LANES:sec * D_MODEL + (p + 1) * LANES] = blk.astype(BF16)

    row = pl.BlockSpec((tm, D_MODEL), lambda i: (i, 0))
    tab = pl.BlockSpec((tm, LANES), lambda i: (i, 0))
    flat = [a for br in grads for a in br]
    return pl.pallas_call(
        body, name=name, grid=(t // tm,), in_specs=[row] * 9 + [tab, tab],
        out_specs=pl.BlockSpec((tm, ODD_IN), lambda i: (i, 0)),
        out_shape=jax.ShapeDtypeStruct((t, ODD_IN), BF16), compiler_params=_params(("parallel",)),
    )(*flat, *rope)


def _loss_grad(y, target, name):
    t = y.shape[0]
    tm = ROW_TILE

    def body(y_ref, t_ref, d_ref, l_ref):
        diff = y_ref[...] - t_ref[...]
        d_ref[...] = diff * (1.0 / D_MODEL)
        _acc_rows8(l_ref, _rows8(diff * diff) * (0.5 / D_MODEL), pl.program_id(0) == 0)

    row = pl.BlockSpec((tm, D_MODEL), lambda i: (i, 0))
    return pl.pallas_call(
        body, name=name, grid=(t // tm,), in_specs=[row, row],
        out_specs=[row, pl.BlockSpec((SUBLANES, D_MODEL), lambda i: (0, 0))],
        out_shape=[jax.ShapeDtypeStruct((t, D_MODEL), F32), jax.ShapeDtypeStruct((SUBLANES, D_MODEL), F32)],
        compiler_params=_params(("arbitrary",)),
    )(y, target)


def _adamw(w, g, m, v):
    m = ADAM_B1 * m + (1.0 - ADAM_B1) * g
    v = ADAM_B2 * v + (1.0 - ADAM_B2) * jnp.square(g)
    m_hat = m / (1.0 - ADAM_B1 ** ADAM_STEP)
    v_hat = v / (1.0 - ADAM_B2 ** ADAM_STEP)
    delta = -ADAM_LR * (m_hat / (jnp.sqrt(v_hat) + ADAM_EPS) + ADAM_WD * w)
    return delta, m, v


def _adamw_sharded(parts, w, m, v, name):
    shape = w.shape
    cols = shape[-1]
    flat = lambda a: a.reshape((-1, cols))
    rows = flat(w).shape[0]
    tr = min(rows, 256)

    def body(p_ref, w_ref, m_ref, v_ref, g_ref, d_ref, mo_ref, vo_ref):
        g = p_ref[0].astype(F32)
        for s in range(1, N_DEV):
            g = g + p_ref[s].astype(F32)
        delta, mn, vn = _adamw(w_ref[...], g, m_ref[...], v_ref[...])
        g_ref[...] = g
        d_ref[...] = delta
        mo_ref[...] = mn
        vo_ref[...] = vn

    row = pl.BlockSpec((tr, cols), lambda i: (i, 0))
    outs = pl.pallas_call(
        body, name=name, grid=(rows // tr,),
        in_specs=[pl.BlockSpec((N_DEV, tr, cols), lambda i: (0, i, 0)), row, row, row],
        out_specs=[row] * 4, out_shape=[jax.ShapeDtypeStruct((rows, cols), F32)] * 4,
        compiler_params=_params(("parallel",)),
    )(parts.reshape((N_DEV, rows, cols)), flat(w), flat(m), flat(v))
    return [o.reshape(shape) for o in outs]


def _small_update(gathered, weights, moments_m, moments_v, lb_index, name):
    n = len(weights)

    def total(ref):
        acc = ref[0]
        for s in range(1, N_DEV):
            acc = acc + ref[s]
        return acc

    def body(*refs):
        g_refs = refs[:n + 1]
        w_refs, m_refs, v_refs = refs[n + 1:2 * n + 1], refs[2 * n + 1:3 * n + 1], refs[3 * n + 1:4 * n + 1]
        outs = refs[4 * n + 1:]
        loss_rows = total(g_refs[n])
        outs[0][...] = jnp.sum(jnp.sum(loss_rows, axis=1, keepdims=True), axis=0, keepdims=True)
        for k in range(n):
            part = total(g_refs[k])
            if k == lb_index:
                dlb = jnp.sum(part, axis=0, keepdims=True)
                tab = w_refs[k][...]
                e = jnp.exp(tab - jnp.max(tab, axis=0, keepdims=True))
                p = e / jnp.sum(e, axis=0, keepdims=True)
                first = lax.broadcasted_iota(jnp.int32, p.shape, 0) == 0
                grads = [(slice(None), p * (jnp.where(first, dlb, 0.0) - p[0:1, :] * dlb))]
            elif part.shape == w_refs[k].shape:
                grads = [(slice(None), part)]
            else:
                grads = [(slice(l, l + 1), jnp.sum(part[l * SUBLANES:(l + 1) * SUBLANES], axis=0, keepdims=True))
                         for l in range(w_refs[k].shape[0])]
            for rows, g in grads:
                delta, mn, vn = _adamw(w_refs[k][rows], g, m_refs[k][rows], v_refs[k][rows])
                outs[1 + 4 * k][rows] = g
                outs[2 + 4 * k][rows] = delta
                outs[3 + 4 * k][rows] = mn
                outs[4 + 4 * k][rows] = vn

    vmem = pl.BlockSpec(memory_space=pltpu.VMEM)
    out_shape = [jax.ShapeDtypeStruct((1, 1), F32)]
    for w in weights:
        out_shape += [jax.ShapeDtypeStruct(w.shape, F32)] * 4
    args = list(gathered) + list(weights) + list(moments_m) + list(moments_v)
    return pl.pallas_call(
        body, name=name, in_specs=[vmem] * len(args), out_specs=[vmem] * len(out_shape), out_shape=out_shape,
        compiler_params=pltpu.CompilerParams(vmem_limit_bytes=VMEM_LIMIT),
    )(*args)


def kernel(x, positions, norm_mix_pre, norm_mix_post, norm_ffn_pre, norm_ffn_post, w_in_even, lb_table, a_norm, b_ln_g, b_ln_b, b_ws, b_bias, w_out_even, w_in_odd, w_out_odd, w_ff1, w_ff2, loss_target, m_norm_mix_pre, m_norm_mix_post, m_norm_ffn_pre, m_norm_ffn_post, m_w_in_even, m_lb_table, m_a_norm, m_b_ln_g, m_b_ln_b, m_b_ws, m_b_bias, m_w_out_even, m_w_in_odd, m_w_out_odd, m_w_ff1, m_w_ff2, v_norm_mix_pre, v_norm_mix_post, v_norm_ffn_pre, v_norm_ffn_post, v_w_in_even, v_lb_table, v_a_norm, v_b_ln_g, v_b_ln_b, v_b_ws, v_b_bias, v_w_out_even, v_w_in_odd, v_w_out_odd, v_w_ff1, v_w_ff2):
    n_seq = x.shape[0]
    t = n_seq * SEQ
    x0 = x.reshape(t, D_MODEL)
    target = loss_target.reshape(t, D_MODEL)

    names = ["w_in_even", "w_out_even", "w_in_odd", "w_out_odd", "w_ff1", "w_ff2"]
    shards = [w_in_even[0], w_out_even[0], w_in_odd[0], w_out_odd[0], w_ff1[0], w_ff1[1], w_ff2[0], w_ff2[1]]
    shards = [_cast_bf16(a, "cast_w%d" % k) for k, a in enumerate(shards)]
    g_in_e, g_out_e, g_in_o, g_out_o, g_ff1_0, g_ff1_1, g_ff2_0, g_ff2_1 = _exchange(shards, True, "gather_weights")
    w_out_e = g_out_e.reshape(D_MODEL, D_MODEL)
    w_out_o = g_out_o.reshape(D_MODEL, D_MODEL)
    w1 = [g_ff1_0, g_ff1_1]
    w2 = [g_ff2_0.reshape(D_FF, D_MODEL), g_ff2_1.reshape(D_FF, D_MODEL)]

    rope = _rope_tables(positions)
    bias_t = b_bias[0].T
    grads = _local_step(x0, target, rope, norm_mix_pre, norm_mix_post, norm_ffn_pre, norm_ffn_post, g_in_e, lb_table,
                        a_norm, b_ln_g, b_ln_b, b_ws[0], bias_t, w_out_e, g_in_o, w_out_o, w1, w2)
    (dx0, loss_part, dg_mix_pre, dg_mix_post, dg_ffn_pre, dg_ffn_post, d_lb, d_a_norm, d_ln_g, d_ln_b, d_ws, d_bias_t,
     gw_in_e, gw_out_e, gw_in_o, gw_out_o, gw_ff1, gw_ff2) = grads

    shard = lambda a, per: a.reshape((N_DEV, per) + a.shape[1:])
    send = [gw_in_e, shard(gw_out_e, 128), gw_in_o, shard(gw_out_o, 128),
            jnp.stack(gw_ff1, axis=1), jnp.stack([shard(a, FF_BLOCK) for a in gw_ff2], axis=1)]
    recv = _exchange(send, False, "scatter_grads")
    big = [(w_in_even, m_w_in_even, v_w_in_even), (w_out_even, m_w_out_even, v_w_out_even),
           (w_in_odd, m_w_in_odd, v_w_in_odd), (w_out_odd, m_w_out_odd, v_w_out_odd),
           (w_ff1, m_w_ff1, v_w_ff1), (w_ff2, m_w_ff2, v_w_ff2)]
    big_out = []
    for parts, (w, m, v), nm in zip(recv, big, names):
        big_out.append(_adamw_sharded(parts.reshape((N_DEV,) + w.shape), w, m, v, "adamw_" + nm))

    small_parts = [dg_mix_pre, dg_mix_post, dg_ffn_pre, dg_ffn_post,
                   d_lb, d_a_norm, d_ln_g, d_ln_b, d_ws, d_bias_t, loss_part]
    gathered = _exchange(small_parts, True, "gather_small")
    small_w = [norm_mix_pre, norm_mix_post, norm_ffn_pre, norm_ffn_post, lb_table, a_norm, b_ln_g, b_ln_b,
               b_ws[0], bias_t]
    small_m = [m_norm_mix_pre, m_norm_mix_post, m_norm_ffn_pre, m_norm_ffn_post, m_lb_table, m_a_norm, m_b_ln_g,
               m_b_ln_b, m_b_ws[0], m_b_bias[0].T]
    small_v = [v_norm_mix_pre, v_norm_mix_post, v_norm_ffn_pre, v_norm_ffn_post, v_lb_table, v_a_norm, v_b_ln_g,
               v_b_ln_b, v_b_ws[0], v_b_bias[0].T]
    small_out = _small_update(gathered, small_w, small_m, small_v, 4, "small_update")
    loss = small_out[0].reshape(())
    small = [small_out[1 + 4 * k:5 + 4 * k] for k in range(len(small_w))]
    small[8] = [a[None] for a in small[8]]
    small[9] = [a.T[None] for a in small[9]]

    per_weight = small[0:4] + [big_out[0]] + small[4:10] + big_out[1:6]
    grad_x = dx0.reshape(x.shape)
    out = [loss, grad_x]
    for kind in range(4):
        out += [p[kind] for p in per_weight]
    return tuple(out)


def _local_step(x0, target, rope, norm_mix_pre, norm_mix_post, norm_ffn_pre, norm_ffn_post, g_in_e, lb_table, a_norm,
                b_ln_g, b_ln_b, ws, bias_t, w_out_e, g_in_o, w_out_o, w1, w2):
    gain = lambda a, l: a[l:l + 1]

    proj, h_mix0 = _norm_inproj(x0, gain(norm_mix_pre, 0), g_in_e, "inproj_even")
    oa, states = _hgrn2_fwd(proj, lb_table, a_norm, "hgrn2_fwd")
    ob = _gmlp_fwd(proj, b_ln_g, b_ln_b, ws, bias_t, "gmlp_fwd")
    x1, mix0 = _outproj([oa, ob], w_out_e, x0, gain(norm_mix_post, 0), "outproj_even")
    x2, y0, h_ffn0 = _ffn_fwd(x1, gain(norm_ffn_pre, 0), w1[0], w2[0], gain(norm_ffn_post, 0), "ffn_fwd_0")
    qkv, h_mix1 = _norm_inproj(x2, gain(norm_mix_pre, 1), g_in_o, "inproj_odd", rope=rope)
    branches = [_attn_branch_fwd(qkv, d, "attn_fwd_d%d" % d) for d in C_DILATIONS]
    attn, lse, attn_b = _attn_merge(branches, "attn_merge")
    x3, mix1 = _outproj([attn_b], w_out_o, x2, gain(norm_mix_post, 1), "outproj_odd")
    x4, y1, h_ffn1 = _ffn_fwd(x3, gain(norm_ffn_pre, 1), w1[1], w2[1], gain(norm_ffn_post, 1), "ffn_fwd_1")

    dx4, loss_part = _loss_grad(x4, target, "loss_grad")

    dx3, dy1, r1, da1, dg_ffn_pre1, dg_ffn_post1 = _ffn_bwd(
        dx4, x3, y1, h_ffn1, gain(norm_ffn_pre, 1), w1[1], w2[1], gain(norm_ffn_post, 1), "ffn_bwd_1")
    gw_ff1_1 = _grad_w(h_ffn1, da1, True, "grad_w_ff1_1")
    gw_ff2_1 = _grad_w(r1, dy1, False, "grad_w_ff2_1")
    dattn, dz1, dg_mix_post1 = _outproj_bwd(dx3, mix1, gain(norm_mix_post, 1), w_out_o, "outproj_bwd_odd")
    gw_out_o = _grad_w(attn_b, dz1, False, "grad_w_out_odd")
    grads_c = [_attn_branch_bwd(qkv, dattn, attn, lse, d, "attn_bwd_d%d" % d) for d in C_DILATIONS]
    dqkv = _attn_combine_bwd(grads_c, rope, "attn_combine_bwd")
    gw_in_o = _grad_w(h_mix1, dqkv, True, "grad_w_in_odd")
    dx2, dg_mix_pre1 = _inproj_bwd(dqkv, g_in_o, dx3, x2, gain(norm_mix_pre, 1), "inproj_bwd_odd")

    dx1, dy0, r0, da0, dg_ffn_pre0, dg_ffn_post0 = _ffn_bwd(
        dx2, x1, y0, h_ffn0, gain(norm_ffn_pre, 0), w1[0], w2[0], gain(norm_ffn_post, 0), "ffn_bwd_0")
    gw_ff1_0 = _grad_w(h_ffn0, da0, True, "grad_w_ff1_0")
    gw_ff2_0 = _grad_w(r0, dy0, False, "grad_w_ff2_0")
    dcat, dz0, dg_mix_post0 = _outproj_bwd(dx1, mix0, gain(norm_mix_post, 0), w_out_e, "outproj_bwd_even")
    gw_out_e = jnp.concatenate([_grad_w(oa, dz0, False, "grad_w_out_even_a"),
                                _grad_w(ob, dz0, False, "grad_w_out_even_b")], axis=0)
    dqfig, d_lb, d_a_norm = _hgrn2_bwd(proj, dcat, states, lb_table, a_norm, "hgrn2_bwd")
    duv, d_ln_g, d_ln_b, d_ws, d_bias_t = _gmlp_bwd(proj, dcat, b_ln_g, b_ln_b, ws, bias_t, "gmlp_bwd")
    dproj = jnp.concatenate([dqfig, duv], axis=1)
    gw_in_e = _grad_w(h_mix0, dproj, True, "grad_w_in_even")
    dx0, dg_mix_pre0 = _inproj_bwd(dproj, g_in_e, dx1, x0, gain(norm_mix_pre, 0), "inproj_bwd_even")

    layers = lambda a, b: jnp.concatenate([a, b], axis=0)
    return (dx0, loss_part, layers(dg_mix_pre0, dg_mix_pre1), layers(dg_mix_post0, dg_mix_post1),
            layers(dg_ffn_pre0, dg_ffn_pre1), layers(dg_ffn_post0, dg_ffn_post1),
            d_lb, d_a_norm, d_ln_g, d_ln_b, d_ws, d_bias_t,
            gw_in_e, gw_out_e, gw_in_o, gw_out_o, [gw_ff1_0, gw_ff1_1], [gw_ff2_0, gw_ff2_1])
```

```python
import functools
import math

import jax
import jax.numpy as jnp
from jax import lax
from jax.experimental import pallas as pl
from jax.experimental.pallas import tpu as pltpu

F32 = jnp.float32
BF16 = jnp.bfloat16
MESH = pl.DeviceIdType.MESH

N_DEV = 8
D_MODEL = 1024
SEQ = 2048
EPS = 1e-6
A_WIDTH = 512
A_HEADS = 4
HEAD_A = 128
B_WIDTH = 512
B_GROUPS = 4
B_CHUNK = 128
C_HEADS = 16
C_HEAD_DIM = 64
C_ROT_HALF = 8
ROPE_THETA = 500000.0
C_DILATIONS = (1, 4, 16)
C_BLOCK = 128
D_FF = 4096
EVEN_IN = 3072
ODD_IN = 3072

ADAM_LR = 0.001
ADAM_B1 = 0.9
ADAM_B2 = 0.999
ADAM_EPS = 1e-08
ADAM_WD = 0.01
ADAM_STEP = 10

LANES = 128
SUBLANES = 8
ROW_TILE = 512
MERGE_TILE = 256
SUB_CHUNK = 16
HGRN_BLOCK = 256
NEG = -1e30
VMEM_LIMIT = 56 * 1024 * 1024


def _params(sem):
    return pltpu.CompilerParams(dimension_semantics=sem, vmem_limit_bytes=VMEM_LIMIT)


def _dot(a, b):
    return jnp.dot(a, b, preferred_element_type=F32)


def _dot_nt(a, b):
    return lax.dot_general(a, b, (((1,), (1,)), ((), ())), preferred_element_type=F32)


def _dot_tn(a, b):
    return lax.dot_general(a, b, (((0,), (0,)), ((), ())), preferred_element_type=F32)


def _rms(x, g):
    r = lax.rsqrt(jnp.mean(x * x, axis=-1, keepdims=True) + EPS)
    return x * r * g


def _rms_bwd(x, g, dy):
    r = lax.rsqrt(jnp.mean(x * x, axis=-1, keepdims=True) + EPS)
    dyg = dy * g
    dx = r * dyg - x * (r * r * r) * jnp.mean(x * dyg, axis=-1, keepdims=True)
    return dx, dy * x * r


def _rows8(v):
    return v.reshape(v.shape[0] // SUBLANES, SUBLANES, v.shape[1]).sum(axis=0)


def _sigmoid(x):
    return 1.0 / (1.0 + jnp.exp(-x))


def _gelu(x):
    return 0.5 * x * (1.0 + jnp.tanh(math.sqrt(2.0 / math.pi) * (x + 0.044715 * (x * x * x))))


def _acc_rows8(ref, val, first):
    @pl.when(first)
    def _():
        ref[...] = val

    @pl.when(jnp.logical_not(first))
    def _():
        ref[...] += val


def _my_slot():
    return 4 * lax.axis_index("x") + 2 * lax.axis_index("y") + lax.axis_index("c")


def _peer(r):
    x, y, c = lax.axis_index("x"), lax.axis_index("y"), lax.axis_index("c")
    px = 1 - x if (r >> 2) & 1 else x
    py = 1 - y if (r >> 1) & 1 else y
    pc = 1 - c if r & 1 else c
    return (px, py, pc), 4 * px + 2 * py + pc


def _exchange(arrays, gather, name):
    n = len(arrays)
    if gather:
        out_shape = [jax.ShapeDtypeStruct((N_DEV,) + a.shape, a.dtype) for a in arrays]
    else:
        out_shape = [jax.ShapeDtypeStruct(a.shape, a.dtype) for a in arrays]

    def body(*refs):
        ins, outs = refs[:n], refs[n:2 * n]
        send_sems, recv_sems, local_sems = refs[2 * n:]
        me = _my_slot()
        local, remote = [], []
        for k in range(n):
            src = ins[k] if gather else ins[k].at[me]
            local.append(pltpu.make_async_copy(src, outs[k].at[me], local_sems.at[k]))
            for r in range(1, N_DEV):
                peer, slot = _peer(r)
                src = ins[k] if gather else ins[k].at[slot]
                remote.append((pltpu.make_async_remote_copy(
                    src_ref=src, dst_ref=outs[k].at[me], send_sem=send_sems.at[k, r - 1],
                    recv_sem=recv_sems.at[k, r - 1], device_id=peer, device_id_type=MESH), k, r, slot))
        for cp in local:
            cp.start()
        for cp, _, _, _ in remote:
            cp.start()
        for cp, k, r, slot in remote:
            pltpu.make_async_remote_copy(
                src_ref=outs[k].at[slot], dst_ref=outs[k].at[slot], send_sem=send_sems.at[k, r - 1],
                recv_sem=recv_sems.at[k, r - 1], device_id=_peer(r)[0], device_id_type=MESH).wait_recv()
        for cp, _, _, _ in remote:
            cp.wait_send()
        for cp in local:
            cp.wait()

    any_spec = pl.BlockSpec(memory_space=pl.ANY)
    return pl.pallas_call(
        body, name=name, out_shape=out_shape,
        in_specs=[any_spec] * n, out_specs=[any_spec] * n,
        scratch_shapes=[pltpu.SemaphoreType.DMA((n, N_DEV - 1)), pltpu.SemaphoreType.DMA((n, N_DEV - 1)),
                        pltpu.SemaphoreType.DMA((n,))],
        compiler_params=pltpu.CompilerParams(has_side_effects=True),
    )(*arrays)


def _cast_bf16(a, name):
    shape = a.shape
    a2 = a.reshape((-1, shape[-1]))
    rows = a2.shape[0]
    tr = min(rows, 512)

    def body(a_ref, o_ref):
        o_ref[...] = a_ref[...].astype(BF16)

    out = pl.pallas_call(
        body, name=name, grid=(rows // tr,),
        in_specs=[pl.BlockSpec((tr, shape[-1]), lambda i: (i, 0))],
        out_specs=pl.BlockSpec((tr, shape[-1]), lambda i: (i, 0)),
        out_shape=jax.ShapeDtypeStruct(a2.shape, BF16), compiler_params=_params(("parallel",)),
    )(a2)
    return out.reshape(shape)


def _rope_tables(positions):
    inv = ROPE_THETA ** (-jnp.arange(C_ROT_HALF, dtype=F32) / C_ROT_HALF)
    ang = positions.reshape(-1)[:, None].astype(F32) * inv
    cos, sin = jnp.cos(ang), jnp.sin(ang)
    t = ang.shape[0]
    ones = jnp.ones((t, C_HEAD_DIM - 2 * C_ROT_HALF), F32)
    c_head = jnp.concatenate([cos, cos, ones], axis=1)
    s_head = jnp.concatenate([-sin, sin, 0.0 * ones], axis=1)
    return jnp.concatenate([c_head, c_head], axis=1), jnp.concatenate([s_head, s_head], axis=1)


def _swap_halves(x):
    lane = lax.broadcasted_iota(jnp.int32, x.shape, 1) % C_HEAD_DIM
    return jnp.where(lane < C_ROT_HALF, pltpu.roll(x, LANES - C_ROT_HALF, 1), pltpu.roll(x, C_ROT_HALF, 1))


def _norm_inproj(x, g, w, name):
    t = x.shape[0]
    nb = w.shape[2]
    tm = ROW_TILE

    def body(x_ref, g_ref, w_ref, o_ref, h_ref):
        @pl.when(pl.program_id(1) == 0)
        def _():
            h_ref[...] = _rms(x_ref[...], g_ref[...]).astype(BF16)

        o_ref[...] = _dot(h_ref[...], w_ref[0])

    return pl.pallas_call(
        body, name=name, grid=(t // tm, N_DEV),
        in_specs=[pl.BlockSpec((tm, D_MODEL), lambda i, j: (i, 0)), pl.BlockSpec((1, D_MODEL), lambda i, j: (0, 0)),
                  pl.BlockSpec((1, D_MODEL, nb), lambda i, j: (j, 0, 0))],
        out_specs=[pl.BlockSpec((tm, nb), lambda i, j: (i, j)), pl.BlockSpec((tm, D_MODEL), lambda i, j: (i, 0))],
        out_shape=[jax.ShapeDtypeStruct((t, N_DEV * nb), F32), jax.ShapeDtypeStruct((t, D_MODEL), BF16)],
        compiler_params=_params(("parallel", "arbitrary")),
    )(x, g, w)


def _dilated_specs(tm, width, col_of):
    per_seq = SEQ // tm
    specs = []
    for d in C_DILATIONS:
        specs.append(pl.BlockSpec(
            (1, d, tm // d, width), lambda i, *rest: (i // per_seq, 0, i % per_seq, col_of(*rest))))
    return specs


def _dilated_shapes(n_seq, cols, dtype):
    return [jax.ShapeDtypeStruct((n_seq, d, SEQ // d, cols), dtype) for d in C_DILATIONS]


def _store_dilated(src_ref, out_refs, dtype):
    groups, tm, _ = src_ref.shape
    for d, o_ref in zip(C_DILATIONS, out_refs):
        for r in range(d):
            rows = pl.ds(r, tm // d, stride=d) if d > 1 else slice(None)
            for p in range(groups):
                o_ref[0, r, :, p * LANES:(p + 1) * LANES] = src_ref.at[p][rows, :].astype(dtype)


def _load_dilated(in_ref, d, dst_ref):
    groups, tm, _ = dst_ref.shape
    for r in range(d):
        rows = pl.ds(r, tm // d, stride=d)
        for p in range(groups):
            dst_ref.at[p][rows, :] = in_ref[0, r, :, p * LANES:(p + 1) * LANES].astype(F32)


def _norm_inproj_rope(x, g, w, rope, name):
    t = x.shape[0]
    nb = w.shape[2]
    tm = ROW_TILE

    def body(x_ref, g_ref, w_ref, c_ref, s_ref, o1_ref, o4_ref, o16_ref, h_ref, tile_ref):
        j = pl.program_id(1)

        @pl.when(j == 0)
        def _():
            h_ref[...] = _rms(x_ref[...], g_ref[...]).astype(BF16)

        acc = _dot(h_ref[...], w_ref[0])
        for p in range(nb // LANES):
            blk = acc[:, p * LANES:(p + 1) * LANES]
            roped = blk * c_ref[...] + _swap_halves(blk) * s_ref[...]
            is_qk = (j * (nb // LANES) + p) < 2 * (D_MODEL // LANES)
            tile_ref[p] = jnp.where(is_qk, roped, blk)
        _store_dilated(tile_ref, (o1_ref, o4_ref, o16_ref), BF16)

    return pl.pallas_call(
        body, name=name, grid=(t // tm, N_DEV),
        in_specs=[pl.BlockSpec((tm, D_MODEL), lambda i, j: (i, 0)), pl.BlockSpec((1, D_MODEL), lambda i, j: (0, 0)),
                  pl.BlockSpec((1, D_MODEL, nb), lambda i, j: (j, 0, 0)),
                  pl.BlockSpec((tm, LANES), lambda i, j: (i, 0)), pl.BlockSpec((tm, LANES), lambda i, j: (i, 0))],
        out_specs=_dilated_specs(tm, nb, lambda j: j) + [pl.BlockSpec((tm, D_MODEL), lambda i, j: (i, 0))],
        out_shape=_dilated_shapes(t // SEQ, N_DEV * nb, BF16) + [jax.ShapeDtypeStruct((t, D_MODEL), BF16)],
        scratch_shapes=[pltpu.VMEM((nb // LANES, tm, LANES), F32)],
        compiler_params=_params(("parallel", "arbitrary")),
    )(x, g, w, *rope)


def _outproj(parts, w, x, g, name):
    t = x.shape[0]
    tm = ROW_TILE
    n = len(parts)
    widths = [p.shape[1] for p in parts]

    def body(*refs):
        p_refs = refs[:n]
        w_ref, x_ref, g_ref, xo_ref, mix_ref = refs[n:]
        mix = None
        off = 0
        for p_ref, wd in zip(p_refs, widths):
            term = _dot(p_ref[...].astype(BF16), w_ref[off:off + wd, :])
            mix = term if mix is None else mix + term
            off += wd
        mix_ref[...] = mix
        xo_ref[...] = x_ref[...] + _rms(mix, g_ref[...])

    row = lambda i: (i, 0)
    return pl.pallas_call(
        body, name=name, grid=(t // tm,),
        in_specs=[pl.BlockSpec((tm, wd), row) for wd in widths] + [
            pl.BlockSpec((sum(widths), D_MODEL), lambda i: (0, 0)),
            pl.BlockSpec((tm, D_MODEL), row), pl.BlockSpec((1, D_MODEL), lambda i: (0, 0))],
        out_specs=[pl.BlockSpec((tm, D_MODEL), row)] * 2,
        out_shape=[jax.ShapeDtypeStruct((t, D_MODEL), F32)] * 2,
        compiler_params=_params(("parallel",)),
    )(*parts, w, x, g)


def _outproj_bwd(dx, mix, g, w, name):
    t = dx.shape[0]
    tm = ROW_TILE
    k = w.shape[0]

    def body(dx_ref, mix_ref, g_ref, w_ref, dcat_ref, dz_ref, dg_ref):
        dz, dgr = _rms_bwd(mix_ref[...], g_ref[...], dx_ref[...])
        dzb = dz.astype(BF16)
        dz_ref[...] = dzb
        dcat_ref[...] = _dot_nt(dzb, w_ref[...])
        _acc_rows8(dg_ref, _rows8(dgr), pl.program_id(0) == 0)

    row = lambda i: (i, 0)
    return pl.pallas_call(
        body, name=name, grid=(t // tm,),
        in_specs=[pl.BlockSpec((tm, D_MODEL), row), pl.BlockSpec((tm, D_MODEL), row),
                  pl.BlockSpec((1, D_MODEL), lambda i: (0, 0)), pl.BlockSpec((k, D_MODEL), lambda i: (0, 0))],
        out_specs=[pl.BlockSpec((tm, k), row), pl.BlockSpec((tm, D_MODEL), row),
                   pl.BlockSpec((SUBLANES, D_MODEL), lambda i: (0, 0))],
        out_shape=[jax.ShapeDtypeStruct((t, k), F32), jax.ShapeDtypeStruct((t, D_MODEL), BF16),
                   jax.ShapeDtypeStruct((SUBLANES, D_MODEL), F32)],
        compiler_params=_params(("arbitrary",)),
    )(dx, mix, g, w)


def _outproj_bwd_attn(dx, mix, g, w, out, name):
    t = dx.shape[0]
    tm = MERGE_TILE

    def body(dx_ref, mix_ref, g_ref, w_ref, out_ref, do1, do4, do16, dl1, dl4, dl16, dz_ref, dg_ref, tile_ref):
        dz, dgr = _rms_bwd(mix_ref[...], g_ref[...], dx_ref[...])
        dzb = dz.astype(BF16)
        dz_ref[...] = dzb
        _acc_rows8(dg_ref, _rows8(dgr), pl.program_id(0) == 0)
        dout = _dot_nt(dzb, w_ref[...])
        for p in range(LANE_GROUPS):
            tile_ref[p] = dout[:, p * LANES:(p + 1) * LANES]
        _store_dilated(tile_ref, (do1, do4, do16), BF16)
        r = lax.broadcasted_iota(jnp.int32, (LANES, LANES), 0) // C_HEAD_DIM
        c = lax.broadcasted_iota(jnp.int32, (LANES, LANES), 1) // C_HEAD_DIM
        same_head = (r == c).astype(F32)
        prod = dout * out_ref[...]
        for p in range(LANE_GROUPS):
            tile_ref[p] = jnp.dot(prod[:, p * LANES:(p + 1) * LANES], same_head, precision=lax.Precision.HIGHEST,
                                  preferred_element_type=F32)
        _store_dilated(tile_ref, (dl1, dl4, dl16), F32)

    row = lambda i: (i, 0)
    n_seq = t // SEQ
    return pl.pallas_call(
        body, name=name, grid=(t // tm,),
        in_specs=[pl.BlockSpec((tm, D_MODEL), row), pl.BlockSpec((tm, D_MODEL), row),
                  pl.BlockSpec((1, D_MODEL), lambda i: (0, 0)), pl.BlockSpec((D_MODEL, D_MODEL), lambda i: (0, 0)),
                  pl.BlockSpec((tm, D_MODEL), row)],
        out_specs=_dilated_specs(tm, D_MODEL, lambda: 0) * 2 + [
            pl.BlockSpec((tm, D_MODEL), row), pl.BlockSpec((SUBLANES, D_MODEL), lambda i: (0, 0))],
        out_shape=_dilated_shapes(n_seq, D_MODEL, BF16) + _dilated_shapes(n_seq, D_MODEL, F32) + [
            jax.ShapeDtypeStruct((t, D_MODEL), BF16), jax.ShapeDtypeStruct((SUBLANES, D_MODEL), F32)],
        scratch_shapes=[pltpu.VMEM((LANE_GROUPS, tm, LANES), F32)],
        compiler_params=_params(("arbitrary",)),
    )(dx, mix, g, w, out)


def _inproj_bwd(dproj, w, dx, x, g, name):
    t = x.shape[0]
    nb = w.shape[2]
    tm = ROW_TILE

    def body(dp_ref, w_ref, dx_ref, x_ref, g_ref, o_ref, dg_ref, acc_ref):
        i, j = pl.program_id(0), pl.program_id(1)
        term = _dot_nt(dp_ref[...], w_ref[0])

        @pl.when(j == 0)
        def _():
            acc_ref[...] = term

        @pl.when(j > 0)
        def _():
            acc_ref[...] += term

        @pl.when(j == N_DEV - 1)
        def _():
            dxn, dgr = _rms_bwd(x_ref[...], g_ref[...], acc_ref[...])
            o_ref[...] = dx_ref[...] + dxn
            _acc_rows8(dg_ref, _rows8(dgr), i == 0)

    return pl.pallas_call(
        body, name=name, grid=(t // tm, N_DEV),
        in_specs=[pl.BlockSpec((tm, nb), lambda i, j: (i, j)), pl.BlockSpec((1, D_MODEL, nb), lambda i, j: (j, 0, 0)),
                  pl.BlockSpec((tm, D_MODEL), lambda i, j: (i, 0)), pl.BlockSpec((tm, D_MODEL), lambda i, j: (i, 0)),
                  pl.BlockSpec((1, D_MODEL), lambda i, j: (0, 0))],
        out_specs=[pl.BlockSpec((tm, D_MODEL), lambda i, j: (i, 0)),
                   pl.BlockSpec((SUBLANES, D_MODEL), lambda i, j: (0, 0))],
        out_shape=[jax.ShapeDtypeStruct((t, D_MODEL), F32), jax.ShapeDtypeStruct((SUBLANES, D_MODEL), F32)],
        scratch_shapes=[pltpu.VMEM((tm, D_MODEL), F32)],
        compiler_params=_params(("arbitrary", "arbitrary")),
    )(dproj, w, dx, x, g)


def _grad_w(a, b, col_blocks, name):
    t, k = a.shape
    n = b.shape[1]
    tt = ROW_TILE
    tk = min(k, 512)
    tn = n // N_DEV if col_blocks else min(n, 1024)
    nt = t // tt

    def body(a_ref, b_ref, o_ref, acc_ref):
        s = pl.program_id(2)
        term = _dot_tn(a_ref[...], b_ref[...])

        @pl.when(s == 0)
        def _():
            acc_ref[...] = term

        @pl.when(s > 0)
        def _():
            acc_ref[...] += term

        @pl.when(s == nt - 1)
        def _():
            o_ref[...] = acc_ref[...].astype(BF16).reshape(o_ref.shape)

    if col_blocks:
        out_spec = pl.BlockSpec((1, tk, tn), lambda i, j, s: (j, i, 0))
        out_shape = jax.ShapeDtypeStruct((N_DEV, k, tn), BF16)
    else:
        out_spec = pl.BlockSpec((tk, tn), lambda i, j, s: (i, j))
        out_shape = jax.ShapeDtypeStruct((k, n), BF16)
    return pl.pallas_call(
        body, name=name, grid=(k // tk, n // tn, nt),
        in_specs=[pl.BlockSpec((tt, tk), lambda i, j, s: (s, i)), pl.BlockSpec((tt, tn), lambda i, j, s: (s, j))],
        out_specs=out_spec, out_shape=out_shape,
        scratch_shapes=[pltpu.VMEM((tk, tn), F32)],
        compiler_params=_params(("parallel", "parallel", "arbitrary")),
    )(a, b)


FF_BLOCK = D_FF // N_DEV


def _ffn_fwd(x, g_pre, w1, w2, g_post, name):
    t = x.shape[0]
    tm = ROW_TILE

    def body(x_ref, gp_ref, w1_ref, w2_ref, gq_ref, xo_ref, y_ref, h_ref):
        j = pl.program_id(1)

        @pl.when(j == 0)
        def _():
            h_ref[...] = _rms(x_ref[...], gp_ref[...]).astype(BF16)

        a = _dot(h_ref[...], w1_ref[0])
        r = jnp.square(jnp.maximum(a, 0.0)).astype(BF16)
        term = _dot(r, w2_ref[...])

        @pl.when(j == 0)
        def _():
            y_ref[...] = term

        @pl.when(j > 0)
        def _():
            y_ref[...] += term

        @pl.when(j == N_DEV - 1)
        def _():
            xo_ref[...] = x_ref[...] + _rms(y_ref[...], gq_ref[...])

    row = lambda i, j: (i, 0)
    vec = pl.BlockSpec((1, D_MODEL), lambda i, j: (0, 0))
    return pl.pallas_call(
        body, name=name, grid=(t // tm, N_DEV),
        in_specs=[pl.BlockSpec((tm, D_MODEL), row), vec,
                  pl.BlockSpec((1, D_MODEL, FF_BLOCK), lambda i, j: (j, 0, 0)),
                  pl.BlockSpec((FF_BLOCK, D_MODEL), lambda i, j: (j, 0)), vec],
        out_specs=[pl.BlockSpec((tm, D_MODEL), row)] * 3,
        out_shape=[jax.ShapeDtypeStruct((t, D_MODEL), F32), jax.ShapeDtypeStruct((t, D_MODEL), F32),
                   jax.ShapeDtypeStruct((t, D_MODEL), BF16)],
        compiler_params=_params(("parallel", "arbitrary")),
    )(x, g_pre, w1, w2, g_post)


def _ffn_bwd(dxo, x, y, h, g_pre, w1, w2, g_post, name):
    t = x.shape[0]
    tm = ROW_TILE

    def body(dxo_ref, x_ref, y_ref, h_ref, gp_ref, w1_ref, w2_ref, gq_ref,
             dx_ref, dy_ref, r_ref, da_ref, dgp_ref, dgq_ref, acc_ref):
        i, j = pl.program_id(0), pl.program_id(1)

        @pl.when(j == 0)
        def _():
            dy, dgr = _rms_bwd(y_ref[...], gq_ref[...], dxo_ref[...])
            dy_ref[...] = dy.astype(BF16)
            _acc_rows8(dgq_ref, _rows8(dgr), i == 0)

        a = _dot(h_ref[...], w1_ref[0])
        ra = jnp.maximum(a, 0.0)
        r_ref[...] = jnp.square(ra).astype(BF16)
        dr = _dot_nt(dy_ref[...], w2_ref[...])
        da = (dr * (2.0 * ra)).astype(BF16)
        da_ref[...] = da
        term = _dot_nt(da, w1_ref[0])

        @pl.when(j == 0)
        def _():
            acc_ref[...] = term

        @pl.when(j > 0)
        def _():
            acc_ref[...] += term

        @pl.when(j == N_DEV - 1)
        def _():
            dxn, dgr = _rms_bwd(x_ref[...], gp_ref[...], acc_ref[...])
            dx_ref[...] = dxo_ref[...] + dxn
            _acc_rows8(dgp_ref, _rows8(dgr), i == 0)

    row = lambda i, j: (i, 0)
    vec = pl.BlockSpec((1, D_MODEL), lambda i, j: (0, 0))
    acc8 = pl.BlockSpec((SUBLANES, D_MODEL), lambda i, j: (0, 0))
    return pl.pallas_call(
        body, name=name, grid=(t // tm, N_DEV),
        in_specs=[pl.BlockSpec((tm, D_MODEL), row)] * 4 + [
            vec, pl.BlockSpec((1, D_MODEL, FF_BLOCK), lambda i, j: (j, 0, 0)),
            pl.BlockSpec((FF_BLOCK, D_MODEL), lambda i, j: (j, 0)), vec],
        out_specs=[pl.BlockSpec((tm, D_MODEL), row), pl.BlockSpec((tm, D_MODEL), row),
                   pl.BlockSpec((tm, FF_BLOCK), lambda i, j: (i, j)), pl.BlockSpec((tm, FF_BLOCK), lambda i, j: (i, j)),
                   acc8, acc8],
        out_shape=[jax.ShapeDtypeStruct((t, D_MODEL), F32), jax.ShapeDtypeStruct((t, D_MODEL), BF16),
                   jax.ShapeDtypeStruct((t, D_FF), BF16), jax.ShapeDtypeStruct((t, D_FF), BF16),
                   jax.ShapeDtypeStruct((SUBLANES, D_MODEL), F32), jax.ShapeDtypeStruct((SUBLANES, D_MODEL), F32)],
        scratch_shapes=[pltpu.VMEM((tm, D_MODEL), F32)],
        compiler_params=_params(("arbitrary", "arbitrary")),
    )(dxo, x, y, h, g_pre, w1, w2, g_post)


def _lower_bound(table):
    e = jnp.exp(table - jnp.max(table, axis=0, keepdims=True))
    return e[0:1, :] / jnp.sum(e, axis=0, keepdims=True)


def _hgrn2_step(st, qraw, fl, v, graw, lb, an, tri):
    f = lb + (1.0 - lb) * _sigmoid(fl)
    logf = jnp.log(f)
    kk = 1.0 - f
    q = qraw * _sigmoid(qraw)
    gsum = jnp.dot(tri, logf, precision=lax.Precision.HIGHEST, preferred_element_type=F32)
    o = _dot_nt((q * jnp.exp(gsum)).astype(BF16), st.astype(BF16))
    row = lax.broadcasted_iota(jnp.int32, gsum.shape, 0)
    for s in range(SUB_CHUNK):
        pick = row == s
        g_s = jnp.sum(jnp.where(pick, gsum, 0.0), axis=0, keepdims=True)
        k_s = jnp.sum(jnp.where(pick, kk, 0.0), axis=0, keepdims=True)
        v_s = jnp.sum(jnp.where(pick, v, 0.0), axis=0, keepdims=True)
        decay = jnp.exp(jnp.where(row >= s, gsum - g_s, NEG))
        score = jnp.sum(q * k_s * decay, axis=1, keepdims=True)
        o = o + score * v_s
    g_last = jnp.sum(jnp.where(row == SUB_CHUNK - 1, gsum, 0.0), axis=0, keepdims=True)
    kd = kk * jnp.exp(g_last - gsum)
    st_new = st * jnp.exp(g_last) + _dot_tn(v.astype(BF16), kd.astype(BF16))
    out = _rms(o, an) * (graw * _sigmoid(graw))
    return st_new, out


def _tri():
    r = lax.broadcasted_iota(jnp.int32, (SUB_CHUNK, SUB_CHUNK), 0)
    c = lax.broadcasted_iota(jnp.int32, (SUB_CHUNK, SUB_CHUNK), 1)
    return (c <= r).astype(F32)


def _hgrn2_fwd(proj, lb_table, a_norm, name):
    t = proj.shape[0]
    tb = HGRN_BLOCK
    n_tb = SEQ // tb
    n_seq = t // SEQ
    n_sub = tb // SUB_CHUNK

    def body(q_ref, f_ref, i_ref, g_ref, lbt_ref, an_ref, o_ref, sts_ref, st_ref):
        @pl.when(pl.program_id(1) == 0)
        def _():
            st_ref[...] = jnp.zeros_like(st_ref)

        lb = _lower_bound(lbt_ref[...])
        an = an_ref[...]
        tri = _tri()

        def step(c, carry):
            rows = pl.ds(pl.multiple_of(c * SUB_CHUNK, SUB_CHUNK), SUB_CHUNK)
            for h in range(A_HEADS):
                lanes = slice(h * HEAD_A, (h + 1) * HEAD_A)
                st = st_ref[h]
                sts_ref[0, c, h] = st
                st_new, out = _hgrn2_step(st, q_ref[rows, lanes], f_ref[rows, lanes], i_ref[rows, lanes],
                                          g_ref[rows, lanes], lb[:, lanes], an[:, lanes], tri)
                st_ref[h] = st_new
                o_ref[rows, lanes] = out.astype(BF16)
            return carry

        lax.fori_loop(0, n_sub, step, 0)

    def col(k):
        return pl.BlockSpec((tb, A_WIDTH), lambda b, s, k=k: (b * n_tb + s, k))

    return pl.pallas_call(
        body, name=name, grid=(n_seq, n_tb),
        in_specs=[col(0), col(1), col(2), col(3),
                  pl.BlockSpec((3, A_WIDTH), lambda b, s: (0, 0)), pl.BlockSpec((1, A_WIDTH), lambda b, s: (0, 0))],
        out_specs=[pl.BlockSpec((tb, A_WIDTH), lambda b, s: (b * n_tb + s, 0)),
                   pl.BlockSpec((1, n_sub, A_HEADS, HEAD_A, HEAD_A), lambda b, s: (b * n_tb + s, 0, 0, 0, 0))],
        out_shape=[jax.ShapeDtypeStruct((t, A_WIDTH), BF16),
                   jax.ShapeDtypeStruct((n_seq * n_tb, n_sub, A_HEADS, HEAD_A, HEAD_A), F32)],
        scratch_shapes=[pltpu.VMEM((A_HEADS, HEAD_A, HEAD_A), F32)],
        compiler_params=_params(("parallel", "arbitrary")),
    )(proj, proj, proj, proj, lb_table, a_norm)


def _hgrn2_bwd(proj, dcat, states, lb_table, a_norm, name):
    t = proj.shape[0]
    tb = HGRN_BLOCK
    n_tb = SEQ // tb
    n_seq = t // SEQ
    n_sub = tb // SUB_CHUNK

    def body(q_ref, f_ref, i_ref, g_ref, do_ref, sts_ref, lbt_ref, an_ref, dp_ref, dlb_ref, dan_ref, dst_ref):
        b, s = pl.program_id(0), pl.program_id(1)

        @pl.when(s == 0)
        def _():
            dst_ref[...] = jnp.zeros_like(dst_ref)

        @pl.when((b == 0) & (s == 0))
        def _():
            dlb_ref[...] = jnp.zeros_like(dlb_ref)
            dan_ref[...] = jnp.zeros_like(dan_ref)

        lb = _lower_bound(lbt_ref[...])
        an = an_ref[...]
        tri = _tri()

        def bwd(k, carry):
            c = n_sub - 1 - k
            rows = pl.ds(pl.multiple_of(c * SUB_CHUNK, SUB_CHUNK), SUB_CHUNK)
            for h in range(A_HEADS):
                lanes = slice(h * HEAD_A, (h + 1) * HEAD_A)
                _, vjp = jax.vjp(
                    lambda st, a1, a2, a3, a4, a5, a6: _hgrn2_step(st, a1, a2, a3, a4, a5, a6, tri),
                    sts_ref[0, c, h], q_ref[rows, lanes], f_ref[rows, lanes], i_ref[rows, lanes], g_ref[rows, lanes],
                    lb[:, lanes], an[:, lanes])
                dst, dq, df, di, dg, dlb, dan = vjp((dst_ref[h], do_ref[rows, lanes]))
                dst_ref[h] = dst
                for sec, val in enumerate((dq, df, di, dg)):
                    dp_ref[rows, sec * A_WIDTH + h * HEAD_A:sec * A_WIDTH + (h + 1) * HEAD_A] = val.astype(BF16)
                dlb_ref[0:1, lanes] += dlb
                dan_ref[0:1, lanes] += dan
            return carry

        lax.fori_loop(0, n_sub, bwd, 0)

    def rev(s):
        return n_tb - 1 - s

    def col(k):
        return pl.BlockSpec((tb, A_WIDTH), lambda b, s, k=k: (b * n_tb + rev(s), k))

    acc8 = pl.BlockSpec((SUBLANES, A_WIDTH), lambda b, s: (0, 0))
    return pl.pallas_call(
        body, name=name, grid=(n_seq, n_tb),
        in_specs=[col(0), col(1), col(2), col(3), col(0),
                  pl.BlockSpec((1, n_sub, A_HEADS, HEAD_A, HEAD_A), lambda b, s: (b * n_tb + rev(s), 0, 0, 0, 0)),
                  pl.BlockSpec((3, A_WIDTH), lambda b, s: (0, 0)), pl.BlockSpec((1, A_WIDTH), lambda b, s: (0, 0))],
        out_specs=[pl.BlockSpec((tb, 4 * A_WIDTH), lambda b, s: (b * n_tb + rev(s), 0)), acc8, acc8],
        out_shape=[jax.ShapeDtypeStruct((t, 4 * A_WIDTH), BF16)] + [jax.ShapeDtypeStruct((SUBLANES, A_WIDTH), F32)] * 2,
        scratch_shapes=[pltpu.VMEM((A_HEADS, HEAD_A, HEAD_A), F32)],
        compiler_params=_params(("arbitrary", "arbitrary")),
    )(proj, proj, proj, proj, dcat, states, lb_table, a_norm)


GMLP_ROWS = 512


def _gmlp_chunk(ub, vb, ln_g, ln_b, ws, bias):
    u = [_gelu(a) for a in ub]
    v = [_gelu(a) for a in vb]
    mu = sum(jnp.sum(a, axis=-1, keepdims=True) for a in v) * (1.0 / B_WIDTH)
    cen = [a - mu for a in v]
    var = sum(jnp.sum(a * a, axis=-1, keepdims=True) for a in cen) * (1.0 / B_WIDTH)
    inv = lax.rsqrt(var + EPS)
    r = lax.broadcasted_iota(jnp.int32, (B_CHUNK, B_CHUNK), 0)
    c = lax.broadcasted_iota(jnp.int32, (B_CHUNK, B_CHUNK), 1)
    outs = []
    for g in range(B_GROUPS):
        vn = (cen[g] * inv * ln_g[g] + ln_b[g]).astype(BF16)
        wm = jnp.where(c <= r, ws[g], 0.0).astype(BF16)
        outs.append(u[g] * (_dot(wm, vn) + bias[g]))
    return outs


def _lane_groups(ref, rows=slice(None)):
    return [ref[rows, g * LANES:(g + 1) * LANES] for g in range(B_GROUPS)]


def _gmlp_fwd(proj, ln_g, ln_b, ws, bias_t, name):
    t = proj.shape[0]
    tm = GMLP_ROWS

    def body(u_ref, v_ref, lg_ref, lb_ref, ws_ref, bt_ref, o_ref):
        for ch in range(tm // B_CHUNK):
            rows = slice(ch * B_CHUNK, (ch + 1) * B_CHUNK)
            outs = _gmlp_chunk(_lane_groups(u_ref, rows), _lane_groups(v_ref, rows), _lane_groups(lg_ref),
                               _lane_groups(lb_ref), [ws_ref[g] for g in range(B_GROUPS)],
                               [bt_ref[:, g:g + 1] for g in range(B_GROUPS)])
            for g in range(B_GROUPS):
                o_ref[rows, g * LANES:(g + 1) * LANES] = outs[g].astype(BF16)

    vec = pl.BlockSpec((1, B_WIDTH), lambda i: (0, 0))
    return pl.pallas_call(
        body, name=name, grid=(t // tm,),
        in_specs=[pl.BlockSpec((tm, B_WIDTH), lambda i: (i, 4)), pl.BlockSpec((tm, B_WIDTH), lambda i: (i, 5)), vec, vec,
                  pl.BlockSpec((B_GROUPS, B_CHUNK, B_CHUNK), lambda i: (0, 0, 0)),
                  pl.BlockSpec((B_CHUNK, B_GROUPS), lambda i: (0, 0))],
        out_specs=pl.BlockSpec((tm, B_WIDTH), lambda i: (i, 0)),
        out_shape=jax.ShapeDtypeStruct((t, B_WIDTH), BF16),
        compiler_params=_params(("parallel",)),
    )(proj, proj, ln_g, ln_b, ws, bias_t)


def _gmlp_bwd(proj, dcat, ln_g, ln_b, ws, bias_t, name):
    t = proj.shape[0]
    tm = GMLP_ROWS

    def body(u_ref, v_ref, do_ref, lg_ref, lb_ref, ws_ref, bt_ref, duv_ref, dlg_ref, dlb_ref, dws_ref, dbt_ref):
        @pl.when(pl.program_id(0) == 0)
        def _():
            dlg_ref[...] = jnp.zeros_like(dlg_ref)
            dlb_ref[...] = jnp.zeros_like(dlb_ref)
            dws_ref[...] = jnp.zeros_like(dws_ref)
            dbt_ref[...] = jnp.zeros_like(dbt_ref)

        for ch in range(tm // B_CHUNK):
            rows = slice(ch * B_CHUNK, (ch + 1) * B_CHUNK)
            _, vjp = jax.vjp(
                _gmlp_chunk, _lane_groups(u_ref, rows), _lane_groups(v_ref, rows), _lane_groups(lg_ref),
                _lane_groups(lb_ref), [ws_ref[g] for g in range(B_GROUPS)],
                [bt_ref[:, g:g + 1] for g in range(B_GROUPS)])
            du, dv, dlg, dlb, dw, dbt = vjp(_lane_groups(do_ref, rows))
            for g in range(B_GROUPS):
                lanes = slice(g * LANES, (g + 1) * LANES)
                duv_ref[rows, lanes] = du[g].astype(BF16)
                duv_ref[rows, B_WIDTH + g * LANES:B_WIDTH + (g + 1) * LANES] = dv[g].astype(BF16)
                dlg_ref[0:1, lanes] += dlg[g]
                dlb_ref[0:1, lanes] += dlb[g]
                dws_ref[g] += dw[g]
                dbt_ref[:, g:g + 1] += dbt[g]

    vec = pl.BlockSpec((1, B_WIDTH), lambda i: (0, 0))
    acc8 = pl.BlockSpec((SUBLANES, B_WIDTH), lambda i: (0, 0))
    ws_spec = pl.BlockSpec((B_GROUPS, B_CHUNK, B_CHUNK), lambda i: (0, 0, 0))
    bt_spec = pl.BlockSpec((B_CHUNK, B_GROUPS), lambda i: (0, 0))
    return pl.pallas_call(
        body, name=name, grid=(t // tm,),
        in_specs=[pl.BlockSpec((tm, B_WIDTH), lambda i: (i, 4)), pl.BlockSpec((tm, B_WIDTH), lambda i: (i, 5)),
                  pl.BlockSpec((tm, B_WIDTH), lambda i: (i, 1)), vec, vec, ws_spec, bt_spec],
        out_specs=[pl.BlockSpec((tm, 2 * B_WIDTH), lambda i: (i, 0)), acc8, acc8, ws_spec, bt_spec],
        out_shape=[jax.ShapeDtypeStruct((t, 2 * B_WIDTH), BF16), jax.ShapeDtypeStruct((SUBLANES, B_WIDTH), F32),
                   jax.ShapeDtypeStruct((SUBLANES, B_WIDTH), F32),
                   jax.ShapeDtypeStruct((B_GROUPS, B_CHUNK, B_CHUNK), F32),
                   jax.ShapeDtypeStruct((B_CHUNK, B_GROUPS), F32)],
        compiler_params=_params(("arbitrary",)),
    )(proj, proj, dcat, ln_g, ln_b, ws, bias_t)


QK_SCALE = 1.0 / math.sqrt(C_HEAD_DIM)
LANE_GROUPS = D_MODEL // LANES


def _attn_window(n, l):
    kw = min(2 * C_BLOCK, l)
    q0 = pl.multiple_of(n * C_BLOCK, C_BLOCK)
    k0 = pl.multiple_of(jnp.maximum(n - 1, 0) * C_BLOCK, C_BLOCK)
    dist = (q0 + lax.broadcasted_iota(jnp.int32, (C_BLOCK, kw), 0)) - (k0 + lax.broadcasted_iota(jnp.int32, (C_BLOCK, kw), 1))
    return q0, k0, kw, (dist >= 0) & (dist <= C_BLOCK)


def _head_masks():
    lane = lax.broadcasted_iota(jnp.int32, (C_BLOCK, LANES), 1)
    return [lane < C_HEAD_DIM, lane >= C_HEAD_DIM]


def _plane_spec(d, col_of):
    return pl.BlockSpec((1, d, SEQ // d, LANES), lambda b, g: (b, 0, 0, col_of(g)))


def _attn_branch_fwd(qkv, d, name):
    n_seq, _, l, _ = qkv.shape
    n_blk = l // C_BLOCK

    def body(q_ref, k_ref, v_ref, o_ref, m_ref, l_ref):
        heads = _head_masks()

        def block(idx, carry):
            r, n = idx // n_blk, idx % n_blk
            q0, k0, kw, mask = _attn_window(n, l)
            q = q_ref[0, r, pl.ds(q0, C_BLOCK), :]
            k = k_ref[0, r, pl.ds(k0, kw), :]
            v = v_ref[0, r, pl.ds(k0, kw), :]
            res = []
            for hm in heads:
                s = jnp.where(mask, _dot_nt(jnp.where(hm, q, 0), k) * QK_SCALE, NEG)
                m = jnp.max(s, axis=-1, keepdims=True)
                p = jnp.exp(s - m)
                res.append((_dot(p.astype(BF16), v), m, jnp.sum(p, axis=-1, keepdims=True)))
            rows = pl.ds(q0, C_BLOCK)
            o_ref[0, r, rows, :] = jnp.where(heads[0], res[0][0], res[1][0])
            m_ref[0, r, rows, :] = jnp.where(heads[0], res[0][1], res[1][1])
            l_ref[0, r, rows, :] = jnp.where(heads[0], res[0][2], res[1][2])
            return carry

        lax.fori_loop(0, d * n_blk, block, 0)

    return pl.pallas_call(
        body, name=name, grid=(n_seq, LANE_GROUPS),
        in_specs=[_plane_spec(d, lambda g: g), _plane_spec(d, lambda g: LANE_GROUPS + g),
                  _plane_spec(d, lambda g: 2 * LANE_GROUPS + g)],
        out_specs=[_plane_spec(d, lambda g: g)] * 3,
        out_shape=[jax.ShapeDtypeStruct((n_seq, d, l, D_MODEL), F32)] * 3,
        compiler_params=_params(("parallel", "parallel")),
    )(qkv, qkv, qkv)


def _attn_merge(branches, name):
    n_seq = branches[0][0].shape[0]
    t = n_seq * SEQ
    tm = MERGE_TILE

    def body(*refs):
        ins = refs[:9]
        o_ref, ob_ref, lse1_ref, lse4_ref, lse16_ref = refs[9:14]
        nat = refs[14:]
        for b, d in enumerate(C_DILATIONS[1:]):
            for k in range(3):
                _load_dilated(ins[3 + 3 * b + k], d, nat[3 * b + k])
        for p in range(LANE_GROUPS):
            lanes = slice(p * LANES, (p + 1) * LANES)
            os_ = [ins[0][0, 0, :, lanes], nat[0][p], nat[3][p]]
            ms = [ins[1][0, 0, :, lanes], nat[1][p], nat[4][p]]
            ls = [ins[2][0, 0, :, lanes], nat[2][p], nat[5][p]]
            m_all = jnp.maximum(jnp.maximum(ms[0], ms[1]), ms[2])
            ws = [jnp.exp(ms[b] - m_all) for b in range(3)]
            total = ws[0] * ls[0] + ws[1] * ls[1] + ws[2] * ls[2]
            o = (ws[0] * os_[0] + ws[1] * os_[1] + ws[2] * os_[2]) / total
            o_ref[:, lanes] = o
            ob_ref[:, lanes] = o.astype(BF16)
            nat[0][p] = m_all + jnp.log(total)
        _store_dilated(nat[0], (lse1_ref, lse4_ref, lse16_ref), F32)

    row = pl.BlockSpec((tm, D_MODEL), lambda i: (i, 0))
    flat = [a for br in branches for a in br]
    in_specs = []
    for spec in _dilated_specs(tm, D_MODEL, lambda: 0):
        in_specs += [spec] * 3
    return pl.pallas_call(
        body, name=name, grid=(t // tm,), in_specs=in_specs,
        out_specs=[row, row] + _dilated_specs(tm, D_MODEL, lambda: 0),
        out_shape=[jax.ShapeDtypeStruct((t, D_MODEL), F32), jax.ShapeDtypeStruct((t, D_MODEL), BF16)]
        + _dilated_shapes(n_seq, D_MODEL, F32),
        scratch_shapes=[pltpu.VMEM((LANE_GROUPS, tm, LANES), F32)] * 6,
        compiler_params=_params(("parallel",)),
    )(*flat)


def _attn_branch_bwd(qkv, dout, lse, delta, d, name):
    n_seq, _, l, _ = qkv.shape
    n_blk = l // C_BLOCK

    def body(q_ref, k_ref, v_ref, do_ref, lse_ref, dl_ref, dq_ref, dk_ref, dv_ref):
        heads = _head_masks()
        dk_ref[...] = jnp.zeros_like(dk_ref)
        dv_ref[...] = jnp.zeros_like(dv_ref)

        def block(idx, carry):
            r, n = idx // n_blk, idx % n_blk
            q0, k0, kw, mask = _attn_window(n, l)
            rows, keys = pl.ds(q0, C_BLOCK), pl.ds(k0, kw)
            q, do = q_ref[0, r, rows, :], do_ref[0, r, rows, :]
            k, v = k_ref[0, r, keys, :], v_ref[0, r, keys, :]
            lse_b, dl_b = lse_ref[0, r, rows, :], dl_ref[0, r, rows, :]
            dq, dk, dv = [], None, None
            for hh, hm in enumerate(heads):
                col = slice(hh * C_HEAD_DIM, hh * C_HEAD_DIM + 1)
                qh, doh = jnp.where(hm, q, 0), jnp.where(hm, do, 0)
                s = jnp.where(mask, _dot_nt(qh, k) * QK_SCALE, NEG)
                p = jnp.exp(s - lse_b[:, col])
                ds = (p * (_dot_nt(doh, v) - dl_b[:, col]) * QK_SCALE).astype(BF16)
                dq.append(_dot(ds, k))
                dk_h, dv_h = _dot_tn(ds, qh), _dot_tn(p.astype(BF16), doh)
                dk = dk_h if dk is None else dk + dk_h
                dv = dv_h if dv is None else dv + dv_h
            dq_ref[0, r, rows, :] = jnp.where(heads[0], dq[0], dq[1])
            dk_ref[0, r, keys, :] += dk
            dv_ref[0, r, keys, :] += dv
            return carry

        lax.fori_loop(0, d * n_blk, block, 0)

    act = _plane_spec(d, lambda g: g)
    return pl.pallas_call(
        body, name=name, grid=(n_seq, LANE_GROUPS),
        in_specs=[_plane_spec(d, lambda g: g), _plane_spec(d, lambda g: LANE_GROUPS + g),
                  _plane_spec(d, lambda g: 2 * LANE_GROUPS + g), act, act, act],
        out_specs=[act] * 3,
        out_shape=[jax.ShapeDtypeStruct((n_seq, d, l, D_MODEL), F32)] * 3,
        compiler_params=_params(("parallel", "parallel")),
    )(qkv, qkv, qkv, dout, lse, delta)


def _attn_combine_bwd(grads, rope, name):
    n_seq = grads[0][0].shape[0]
    t = n_seq * SEQ
    tm = MERGE_TILE

    def body(*refs):
        c_ref, s_ref, o_ref, nat4_ref, nat16_ref = refs[9:]
        for sec in range(3):
            _load_dilated(refs[3 + sec], 4, nat4_ref)
            _load_dilated(refs[6 + sec], 16, nat16_ref)
            for p in range(LANE_GROUPS):
                blk = refs[sec][0, 0, :, p * LANES:(p + 1) * LANES] + nat4_ref[p] + nat16_ref[p]
                if sec < 2:
                    blk = blk * c_ref[...] - _swap_halves(blk) * s_ref[...]
                o_ref[:, sec * D_MODEL + p * LANES:sec * D_MODEL + (p + 1) * LANES] = blk.astype(BF16)

    tab = pl.BlockSpec((tm, LANES), lambda i: (i, 0))
    flat = [a for br in grads for a in br]
    in_specs = []
    for spec in _dilated_specs(tm, D_MODEL, lambda: 0):
        in_specs += [spec] * 3
    return pl.pallas_call(
        body, name=name, grid=(t // tm,), in_specs=in_specs + [tab, tab],
        out_specs=pl.BlockSpec((tm, ODD_IN), lambda i: (i, 0)),
        out_shape=jax.ShapeDtypeStruct((t, ODD_IN), BF16),
        scratch_shapes=[pltpu.VMEM((LANE_GROUPS, tm, LANES), F32)] * 2,
        compiler_params=_params(("parallel",)),
    )(*flat, *rope)


def _loss_grad(y, target, name):
    t = y.shape[0]
    tm = ROW_TILE

    def body(y_ref, t_ref, d_ref, l_ref):
        diff = y_ref[...] - t_ref[...]
        d_ref[...] = diff * (1.0 / D_MODEL)
        _acc_rows8(l_ref, _rows8(diff * diff) * (0.5 / D_MODEL), pl.program_id(0) == 0)

    row = pl.BlockSpec((tm, D_MODEL), lambda i: (i, 0))
    return pl.pallas_call(
        body, name=name, grid=(t // tm,), in_specs=[row, row],
        out_specs=[row, pl.BlockSpec((SUBLANES, D_MODEL), lambda i: (0, 0))],
        out_shape=[jax.ShapeDtypeStruct((t, D_MODEL), F32), jax.ShapeDtypeStruct((SUBLANES, D_MODEL), F32)],
        compiler_params=_params(("arbitrary",)),
    )(y, target)


def _adamw(w, g, m, v):
    m = ADAM_B1 * m + (1.0 - ADAM_B1) * g
    v = ADAM_B2 * v + (1.0 - ADAM_B2) * jnp.square(g)
    m_hat = m / (1.0 - ADAM_B1 ** ADAM_STEP)
    v_hat = v / (1.0 - ADAM_B2 ** ADAM_STEP)
    delta = -ADAM_LR * (m_hat / (jnp.sqrt(v_hat) + ADAM_EPS) + ADAM_WD * w)
    return delta, m, v


def _adamw_sharded(parts, w, m, v, name):
    shape = w.shape
    cols = shape[-1]
    flat = lambda a: a.reshape((-1, cols))
    rows = flat(w).shape[0]
    tr = min(rows, 256)

    def body(p_ref, w_ref, m_ref, v_ref, g_ref, d_ref, mo_ref, vo_ref):
        g = p_ref[0].astype(F32)
        for s in range(1, N_DEV):
            g = g + p_ref[s].astype(F32)
        delta, mn, vn = _adamw(w_ref[...], g, m_ref[...], v_ref[...])
        g_ref[...] = g
        d_ref[...] = delta
        mo_ref[...] = mn
        vo_ref[...] = vn

    row = pl.BlockSpec((tr, cols), lambda i: (i, 0))
    outs = pl.pallas_call(
        body, name=name, grid=(rows // tr,),
        in_specs=[pl.BlockSpec((N_DEV, tr, cols), lambda i: (0, i, 0)), row, row, row],
        out_specs=[row] * 4, out_shape=[jax.ShapeDtypeStruct((rows, cols), F32)] * 4,
        compiler_params=_params(("parallel",)),
    )(parts.reshape((N_DEV, rows, cols)), flat(w), flat(m), flat(v))
    return [o.reshape(shape) for o in outs]


def _small_update(gathered, weights, moments_m, moments_v, lb_index, name):
    n = len(weights)

    def total(ref):
        acc = ref[0]
        for s in range(1, N_DEV):
            acc = acc + ref[s]
        return acc

    def body(*refs):
        g_refs = refs[:n + 1]
        w_refs, m_refs, v_refs = refs[n + 1:2 * n + 1], refs[2 * n + 1:3 * n + 1], refs[3 * n + 1:4 * n + 1]
        outs = refs[4 * n + 1:]
        loss_rows = total(g_refs[n])
        outs[0][...] = jnp.sum(jnp.sum(loss_rows, axis=1, keepdims=True), axis=0, keepdims=True)
        for k in range(n):
            part = total(g_refs[k])
            if k == lb_index:
                dlb = jnp.sum(part, axis=0, keepdims=True)
                tab = w_refs[k][...]
                e = jnp.exp(tab - jnp.max(tab, axis=0, keepdims=True))
                p = e / jnp.sum(e, axis=0, keepdims=True)
                first = lax.broadcasted_iota(jnp.int32, p.shape, 0) == 0
                grads = [(slice(None), p * (jnp.where(first, dlb, 0.0) - p[0:1, :] * dlb))]
            elif part.shape == w_refs[k].shape:
                grads = [(slice(None), part)]
            else:
                grads = [(slice(l, l + 1), jnp.sum(part[l * SUBLANES:(l + 1) * SUBLANES], axis=0, keepdims=True))
                         for l in range(w_refs[k].shape[0])]
            for rows, g in grads:
                delta, mn, vn = _adamw(w_refs[k][rows], g, m_refs[k][rows], v_refs[k][rows])
                outs[1 + 4 * k][rows] = g
                outs[2 + 4 * k][rows] = delta
                outs[3 + 4 * k][rows] = mn
                outs[4 + 4 * k][rows] = vn

    vmem = pl.BlockSpec(memory_space=pltpu.VMEM)
    out_shape = [jax.ShapeDtypeStruct((1, 1), F32)]
    for w in weights:
        out_shape += [jax.ShapeDtypeStruct(w.shape, F32)] * 4
    args = list(gathered) + list(weights) + list(moments_m) + list(moments_v)
    return pl.pallas_call(
        body, name=name, in_specs=[vmem] * len(args), out_specs=[vmem] * len(out_shape), out_shape=out_shape,
        compiler_params=pltpu.CompilerParams(vmem_limit_bytes=VMEM_LIMIT),
    )(*args)


def kernel(x, positions, norm_mix_pre, norm_mix_post, norm_ffn_pre, norm_ffn_post, w_in_even, lb_table, a_norm, b_ln_g, b_ln_b, b_ws, b_bias, w_out_even, w_in_odd, w_out_odd, w_ff1, w_ff2, loss_target, m_norm_mix_pre, m_norm_mix_post, m_norm_ffn_pre, m_norm_ffn_post, m_w_in_even, m_lb_table, m_a_norm, m_b_ln_g, m_b_ln_b, m_b_ws, m_b_bias, m_w_out_even, m_w_in_odd, m_w_out_odd, m_w_ff1, m_w_ff2, v_norm_mix_pre, v_norm_mix_post, v_norm_ffn_pre, v_norm_ffn_post, v_w_in_even, v_lb_table, v_a_norm, v_b_ln_g, v_b_ln_b, v_b_ws, v_b_bias, v_w_out_even, v_w_in_odd, v_w_out_odd, v_w_ff1, v_w_ff2):
    n_seq = x.shape[0]
    t = n_seq * SEQ
    x0 = x.reshape(t, D_MODEL)
    target = loss_target.reshape(t, D_MODEL)

    names = ["w_in_even", "w_out_even", "w_in_odd", "w_out_odd", "w_ff1", "w_ff2"]
    shards = [w_in_even[0], w_out_even[0], w_in_odd[0], w_out_odd[0], w_ff1[0], w_ff1[1], w_ff2[0], w_ff2[1]]
    shards = [_cast_bf16(a, "cast_w%d" % k) for k, a in enumerate(shards)]
    g_in_e, g_out_e, g_in_o, g_out_o, g_ff1_0, g_ff1_1, g_ff2_0, g_ff2_1 = _exchange(shards, True, "gather_weights")
    w_out_e = g_out_e.reshape(D_MODEL, D_MODEL)
    w_out_o = g_out_o.reshape(D_MODEL, D_MODEL)
    w1 = [g_ff1_0, g_ff1_1]
    w2 = [g_ff2_0.reshape(D_FF, D_MODEL), g_ff2_1.reshape(D_FF, D_MODEL)]

    rope = _rope_tables(positions)
    bias_t = b_bias[0].T
    grads = _local_step(x0, target, rope, norm_mix_pre, norm_mix_post, norm_ffn_pre, norm_ffn_post, g_in_e, lb_table,
                        a_norm, b_ln_g, b_ln_b, b_ws[0], bias_t, w_out_e, g_in_o, w_out_o, w1, w2)
    (dx0, loss_part, dg_mix_pre, dg_mix_post, dg_ffn_pre, dg_ffn_post, d_lb, d_a_norm, d_ln_g, d_ln_b, d_ws, d_bias_t,
     gw_in_e, gw_out_e, gw_in_o, gw_out_o, gw_ff1, gw_ff2) = grads

    shard = lambda a, per: a.reshape((N_DEV, per) + a.shape[1:])
    send = [gw_in_e, shard(gw_out_e, 128), gw_in_o, shard(gw_out_o, 128),
            jnp.stack(gw_ff1, axis=1), jnp.stack([shard(a, FF_BLOCK) for a in gw_ff2], axis=1)]
    recv = _exchange(send, False, "scatter_grads")
    big = [(w_in_even, m_w_in_even, v_w_in_even), (w_out_even, m_w_out_even, v_w_out_even),
           (w_in_odd, m_w_in_odd, v_w_in_odd), (w_out_odd, m_w_out_odd, v_w_out_odd),
           (w_ff1, m_w_ff1, v_w_ff1), (w_ff2, m_w_ff2, v_w_ff2)]
    big_out = []
    for parts, (w, m, v), nm in zip(recv, big, names):
        big_out.append(_adamw_sharded(parts.reshape((N_DEV,) + w.shape), w, m, v, "adamw_" + nm))

    small_parts = [dg_mix_pre, dg_mix_post, dg_ffn_pre, dg_ffn_post,
                   d_lb, d_a_norm, d_ln_g, d_ln_b, d_ws, d_bias_t, loss_part]
    gathered = _exchange(small_parts, True, "gather_small")
    small_w = [norm_mix_pre, norm_mix_post, norm_ffn_pre, norm_ffn_post, lb_table, a_norm, b_ln_g, b_ln_b,
               b_ws[0], bias_t]
    small_m = [m_norm_mix_pre, m_norm_mix_post, m_norm_ffn_pre, m_norm_ffn_post, m_lb_table, m_a_norm, m_b_ln_g,
               m_b_ln_b, m_b_ws[0], m_b_bias[0].T]
    small_v = [v_norm_mix_pre, v_norm_mix_post, v_norm_ffn_pre, v_norm_ffn_post, v_lb_table, v_a_norm, v_b_ln_g,
               v_b_ln_b, v_b_ws[0], v_b_bias[0].T]
    small_out = _small_update(gathered, small_w, small_m, small_v, 4, "small_update")
    loss = small_out[0].reshape(())
    small = [small_out[1 + 4 * k:5 + 4 * k] for k in range(len(small_w))]
    small[8] = [a[None] for a in small[8]]
    small[9] = [a.T[None] for a in small[9]]

    per_weight = small[0:4] + [big_out[0]] + small[4:10] + big_out[1:6]
    grad_x = dx0.reshape(x.shape)
    out = [loss, grad_x]
    for kind in range(4):
        out += [p[kind] for p in per_weight]
    return tuple(out)


def _local_step(x0, target, rope, norm_mix_pre, norm_mix_post, norm_ffn_pre, norm_ffn_post, g_in_e, lb_table, a_norm,
                b_ln_g, b_ln_b, ws, bias_t, w_out_e, g_in_o, w_out_o, w1, w2):
    gain = lambda a, l: a[l:l + 1]

    proj, h_mix0 = _norm_inproj(x0, gain(norm_mix_pre, 0), g_in_e, "inproj_even")
    oa, states = _hgrn2_fwd(proj, lb_table, a_norm, "hgrn2_fwd")
    ob = _gmlp_fwd(proj, b_ln_g, b_ln_b, ws, bias_t, "gmlp_fwd")
    x1, mix0 = _outproj([oa, ob], w_out_e, x0, gain(norm_mix_post, 0), "outproj_even")
    x2, y0, h_ffn0 = _ffn_fwd(x1, gain(norm_ffn_pre, 0), w1[0], w2[0], gain(norm_ffn_post, 0), "ffn_fwd_0")
    *qkv, h_mix1 = _norm_inproj_rope(x2, gain(norm_mix_pre, 1), g_in_o, rope, "inproj_odd")
    branches = [_attn_branch_fwd(a, d, "attn_fwd_d%d" % d) for a, d in zip(qkv, C_DILATIONS)]
    attn, attn_b, *lse = _attn_merge(branches, "attn_merge")
    x3, mix1 = _outproj([attn_b], w_out_o, x2, gain(norm_mix_post, 1), "outproj_odd")
    x4, y1, h_ffn1 = _ffn_fwd(x3, gain(norm_ffn_pre, 1), w1[1], w2[1], gain(norm_ffn_post, 1), "ffn_fwd_1")

    dx4, loss_part = _loss_grad(x4, target, "loss_grad")

    dx3, dy1, r1, da1, dg_ffn_pre1, dg_ffn_post1 = _ffn_bwd(
        dx4, x3, y1, h_ffn1, gain(norm_ffn_pre, 1), w1[1], w2[1], gain(norm_ffn_post, 1), "ffn_bwd_1")
    gw_ff1_1 = _grad_w(h_ffn1, da1, True, "grad_w_ff1_1")
    gw_ff2_1 = _grad_w(r1, dy1, False, "grad_w_ff2_1")
    *dattn, dz1, dg_mix_post1 = _outproj_bwd_attn(dx3, mix1, gain(norm_mix_post, 1), w_out_o, attn, "outproj_bwd_odd")
    gw_out_o = _grad_w(attn_b, dz1, False, "grad_w_out_odd")
    grads_c = [_attn_branch_bwd(qkv[b], dattn[b], lse[b], dattn[3 + b], d, "attn_bwd_d%d" % d)
               for b, d in enumerate(C_DILATIONS)]
    dqkv = _attn_combine_bwd(grads_c, rope, "attn_combine_bwd")
    gw_in_o = _grad_w(h_mix1, dqkv, True, "grad_w_in_odd")
    dx2, dg_mix_pre1 = _inproj_bwd(dqkv, g_in_o, dx3, x2, gain(norm_mix_pre, 1), "inproj_bwd_odd")

    dx1, dy0, r0, da0, dg_ffn_pre0, dg_ffn_post0 = _ffn_bwd(
        dx2, x1, y0, h_ffn0, gain(norm_ffn_pre, 0), w1[0], w2[0], gain(norm_ffn_post, 0), "ffn_bwd_0")
    gw_ff1_0 = _grad_w(h_ffn0, da0, True, "grad_w_ff1_0")
    gw_ff2_0 = _grad_w(r0, dy0, False, "grad_w_ff2_0")
    dcat, dz0, dg_mix_post0 = _outproj_bwd(dx1, mix0, gain(norm_mix_post, 0), w_out_e, "outproj_bwd_even")
    gw_out_e = jnp.concatenate([_grad_w(oa, dz0, False, "grad_w_out_even_a"),
                                _grad_w(ob, dz0, False, "grad_w_out_even_b")], axis=0)
    dqfig, d_lb, d_a_norm = _hgrn2_bwd(proj, dcat, states, lb_table, a_norm, "hgrn2_bwd")
    duv, d_ln_g, d_ln_b, d_ws, d_bias_t = _gmlp_bwd(proj, dcat, b_ln_g, b_ln_b, ws, bias_t, "gmlp_bwd")
    dproj = jnp.concatenate([dqfig, duv], axis=1)
    gw_in_e = _grad_w(h_mix0, dproj, True, "grad_w_in_even")
    dx0, dg_mix_pre0 = _inproj_bwd(dproj, g_in_e, dx1, x0, gain(norm_mix_pre, 0), "inproj_bwd_even")

    layers = lambda a, b: jnp.concatenate([a, b], axis=0)
    return (dx0, loss_part, layers(dg_mix_pre0, dg_mix_pre1), layers(dg_mix_post0, dg_mix_post1),
            layers(dg_ffn_pre0, dg_ffn_pre1), layers(dg_ffn_post0, dg_ffn_post1),
            d_lb, d_a_norm, d_ln_g, d_ln_b, d_ws, d_bias_t,
            gw_in_e, gw_out_e, gw_in_o, gw_out_o, [gw_ff1_0, gw_ff1_1], [gw_ff2_0, gw_ff2_1])
```

```python
import functools
import math

import jax
import jax.numpy as jnp
from jax import lax
from jax.experimental import pallas as pl
from jax.experimental.pallas import tpu as pltpu

F32 = jnp.float32
BF16 = jnp.bfloat16
MESH = pl.DeviceIdType.MESH

N_DEV = 8
D_MODEL = 1024
SEQ = 2048
EPS = 1e-6
A_WIDTH = 512
A_HEADS = 4
HEAD_A = 128
B_WIDTH = 512
B_GROUPS = 4
B_CHUNK = 128
C_HEADS = 16
C_HEAD_DIM = 64
C_ROT_HALF = 8
ROPE_THETA = 500000.0
C_DILATIONS = (1, 4, 16)
C_BLOCK = 128
D_FF = 4096
EVEN_IN = 3072
ODD_IN = 3072

ADAM_LR = 0.001
ADAM_B1 = 0.9
ADAM_B2 = 0.999
ADAM_EPS = 1e-08
ADAM_WD = 0.01
ADAM_STEP = 10

LANES = 128
SUBLANES = 8
ROW_TILE = 512
MERGE_TILE = 256
SUB_CHUNK = 16
HGRN_BLOCK = 256
NEG = -1e30
VMEM_LIMIT = 56 * 1024 * 1024


def _params(sem):
    return pltpu.CompilerParams(dimension_semantics=sem, vmem_limit_bytes=VMEM_LIMIT)


def _dot(a, b):
    return jnp.dot(a, b, preferred_element_type=F32)


def _dot_nt(a, b):
    return lax.dot_general(a, b, (((1,), (1,)), ((), ())), preferred_element_type=F32)


def _dot_tn(a, b):
    return lax.dot_general(a, b, (((0,), (0,)), ((), ())), preferred_element_type=F32)


def _rms(x, g):
    r = lax.rsqrt(jnp.mean(x * x, axis=-1, keepdims=True) + EPS)
    return x * r * g


def _rms_bwd(x, g, dy):
    r = lax.rsqrt(jnp.mean(x * x, axis=-1, keepdims=True) + EPS)
    dyg = dy * g
    dx = r * dyg - x * (r * r * r) * jnp.mean(x * dyg, axis=-1, keepdims=True)
    return dx, dy * x * r


def _rows8(v):
    return v.reshape(v.shape[0] // SUBLANES, SUBLANES, v.shape[1]).sum(axis=0)


def _sigmoid(x):
    return 1.0 / (1.0 + jnp.exp(-x))


def _gelu(x):
    return 0.5 * x * (1.0 + jnp.tanh(math.sqrt(2.0 / math.pi) * (x + 0.044715 * (x * x * x))))


def _acc_rows8(ref, val, first):
    @pl.when(first)
    def _():
        ref[...] = val

    @pl.when(jnp.logical_not(first))
    def _():
        ref[...] += val


def _my_slot():
    return 4 * lax.axis_index("x") + 2 * lax.axis_index("y") + lax.axis_index("c")


def _peer(r):
    x, y, c = lax.axis_index("x"), lax.axis_index("y"), lax.axis_index("c")
    px = 1 - x if (r >> 2) & 1 else x
    py = 1 - y if (r >> 1) & 1 else y
    pc = 1 - c if r & 1 else c
    return (px, py, pc), 4 * px + 2 * py + pc


def _exchange(arrays, gather, name):
    n = len(arrays)
    if gather:
        out_shape = [jax.ShapeDtypeStruct((N_DEV,) + a.shape, a.dtype) for a in arrays]
    else:
        out_shape = [jax.ShapeDtypeStruct(a.shape, a.dtype) for a in arrays]

    def body(*refs):
        ins, outs = refs[:n], refs[n:2 * n]
        send_sems, recv_sems, local_sems = refs[2 * n:]
        me = _my_slot()
        local, remote = [], []
        for k in range(n):
            src = ins[k] if gather else ins[k].at[me]
            local.append(pltpu.make_async_copy(src, outs[k].at[me], local_sems.at[k]))
            for r in range(1, N_DEV):
                peer, slot = _peer(r)
                src = ins[k] if gather else ins[k].at[slot]
                remote.append((pltpu.make_async_remote_copy(
                    src_ref=src, dst_ref=outs[k].at[me], send_sem=send_sems.at[k, r - 1],
                    recv_sem=recv_sems.at[k, r - 1], device_id=peer, device_id_type=MESH), k, r, slot))
        for cp in local:
            cp.start()
        for cp, _, _, _ in remote:
            cp.start()
        for cp, k, r, slot in remote:
            pltpu.make_async_remote_copy(
                src_ref=outs[k].at[slot], dst_ref=outs[k].at[slot], send_sem=send_sems.at[k, r - 1],
                recv_sem=recv_sems.at[k, r - 1], device_id=_peer(r)[0], device_id_type=MESH).wait_recv()
        for cp, _, _, _ in remote:
            cp.wait_send()
        for cp in local:
            cp.wait()

    any_spec = pl.BlockSpec(memory_space=pl.ANY)
    return pl.pallas_call(
        body, name=name, out_shape=out_shape,
        in_specs=[any_spec] * n, out_specs=[any_spec] * n,
        scratch_shapes=[pltpu.SemaphoreType.DMA((n, N_DEV - 1)), pltpu.SemaphoreType.DMA((n, N_DEV - 1)),
                        pltpu.SemaphoreType.DMA((n,))],
        compiler_params=pltpu.CompilerParams(has_side_effects=True),
    )(*arrays)


HBM_SPEC = pl.BlockSpec(memory_space=pltpu.HBM)
SEM_SPEC = pl.BlockSpec(memory_space=pltpu.SEMAPHORE)
SPLIT_EFFECT = pltpu.SideEffectType.DATAFLOW_SIDE_EFFECTING


def _split_copies(land_ref, src_ref, send_sem, recv_sem):
    me = _my_slot()
    copies = []
    for r in range(1, N_DEV):
        peer, slot = _peer(r)
        src = land_ref.at[me] if src_ref is None else src_ref.at[slot]
        copies.append(pltpu.make_async_remote_copy(
            src_ref=src, dst_ref=land_ref.at[me], send_sem=send_sem, recv_sem=recv_sem,
            device_id=peer, device_id_type=MESH))
    return copies


def _exchange_start(lands, sources, name):
    n = len(lands)
    given = [s for s in sources if s is not None]
    arrays = list(lands) + given

    def body(*refs):
        land_refs, src_refs = refs[:n], list(refs[n:n + len(given)])
        sems = refs[len(arrays):len(arrays) + 2 * n]
        token = refs[-1]
        for k in range(n):
            src_ref = None if sources[k] is None else src_refs.pop(0)
            for copy in _split_copies(land_refs[k], src_ref, sems[k], sems[n + k]):
                copy.start()
        token[...] = jnp.zeros_like(token)

    outs = pl.pallas_call(
        body, name=name,
        out_shape=(pltpu.SemaphoreType.DMA(()),) * (2 * n) + tuple(pltpu.HBM(a.shape, a.dtype) for a in arrays)
        + (jax.ShapeDtypeStruct((SUBLANES, LANES), F32),),
        in_specs=[HBM_SPEC] * len(arrays),
        out_specs=(SEM_SPEC,) * (2 * n) + (HBM_SPEC,) * len(arrays) + (pl.BlockSpec(memory_space=pltpu.VMEM),),
        input_output_aliases={i: 2 * n + i for i in range(len(arrays))},
        compiler_params=pltpu.CompilerParams(has_side_effects=SPLIT_EFFECT),
    )(*[pltpu.with_memory_space_constraint(a, pltpu.HBM) for a in arrays])
    return list(outs[:n]), list(outs[n:2 * n]), list(outs[2 * n:3 * n]), list(outs[3 * n:-1]), outs[-1]


def _exchange_wait(lands, sources, send_sems, recv_sems, after, name):
    n = len(lands)
    given = [s for s in sources if s is not None]
    arrays = list(lands) + given

    def body(*refs):
        land_refs, src_refs = refs[:n], list(refs[n:n + len(given)])
        sems = refs[len(arrays):len(arrays) + 2 * n]
        for i in range(n):
            src_ref = None if sources[i] is None else src_refs.pop(0)
            copies = _split_copies(land_refs[i], src_ref, sems[i], sems[n + i])
            for copy in copies:
                copy.wait_recv()
            for copy in copies:
                copy.wait_send()

    outs = pl.pallas_call(
        body, name=name, out_shape=tuple(pltpu.HBM(a.shape, a.dtype) for a in arrays),
        in_specs=[HBM_SPEC] * len(arrays) + [SEM_SPEC] * (2 * n) + [pl.BlockSpec(memory_space=pl.ANY)],
        out_specs=(HBM_SPEC,) * len(arrays),
        input_output_aliases={i: i for i in range(len(arrays))},
        compiler_params=pltpu.CompilerParams(has_side_effects=SPLIT_EFFECT),
    )(*arrays, *send_sems, *recv_sems, after)
    return list(outs[:n])


def _place_own(a, me, name, own_block):
    shape = a.shape[1:] if own_block else a.shape
    cols = shape[-1]
    a3 = a.reshape((N_DEV if own_block else 1, -1, cols))
    rows = a3.shape[1]
    tr = min(rows, 512)

    def body(me_ref, a_ref, o_ref):
        o_ref[...] = a_ref[...].astype(BF16)

    grid_spec = pltpu.PrefetchScalarGridSpec(
        num_scalar_prefetch=1, grid=(rows // tr,),
        in_specs=[pl.BlockSpec((1, tr, cols), lambda i, me_ref: (me_ref[0] if own_block else 0, i, 0))],
        out_specs=pl.BlockSpec((1, tr, cols), lambda i, me_ref: (me_ref[0], i, 0)))
    out = pl.pallas_call(
        body, name=name, grid_spec=grid_spec, out_shape=jax.ShapeDtypeStruct((N_DEV, rows, cols), BF16),
        compiler_params=_params(("arbitrary",)),
    )(me, a3)
    return out.reshape((N_DEV,) + shape)


def _rope_tables(positions):
    inv = ROPE_THETA ** (-jnp.arange(C_ROT_HALF, dtype=F32) / C_ROT_HALF)
    ang = positions.reshape(-1)[:, None].astype(F32) * inv
    cos, sin = jnp.cos(ang), jnp.sin(ang)
    t = ang.shape[0]
    ones = jnp.ones((t, C_HEAD_DIM - 2 * C_ROT_HALF), F32)
    c_head = jnp.concatenate([cos, cos, ones], axis=1)
    s_head = jnp.concatenate([-sin, sin, 0.0 * ones], axis=1)
    return jnp.concatenate([c_head, c_head], axis=1), jnp.concatenate([s_head, s_head], axis=1)


def _swap_halves(x):
    lane = lax.broadcasted_iota(jnp.int32, x.shape, 1) % C_HEAD_DIM
    return jnp.where(lane < C_ROT_HALF, pltpu.roll(x, LANES - C_ROT_HALF, 1), pltpu.roll(x, C_ROT_HALF, 1))


def _norm_inproj(x, g, w, name):
    t = x.shape[0]
    nb = w.shape[2]
    tm = ROW_TILE

    def body(x_ref, g_ref, w_ref, o_ref, h_ref):
        @pl.when(pl.program_id(1) == 0)
        def _():
            h_ref[...] = _rms(x_ref[...], g_ref[...]).astype(BF16)

        o_ref[...] = _dot(h_ref[...], w_ref[0])

    return pl.pallas_call(
        body, name=name, grid=(t // tm, N_DEV),
        in_specs=[pl.BlockSpec((tm, D_MODEL), lambda i, j: (i, 0)), pl.BlockSpec((1, D_MODEL), lambda i, j: (0, 0)),
                  pl.BlockSpec((1, D_MODEL, nb), lambda i, j: (j, 0, 0))],
        out_specs=[pl.BlockSpec((tm, nb), lambda i, j: (i, j)), pl.BlockSpec((tm, D_MODEL), lambda i, j: (i, 0))],
        out_shape=[jax.ShapeDtypeStruct((t, N_DEV * nb), F32), jax.ShapeDtypeStruct((t, D_MODEL), BF16)],
        compiler_params=_params(("parallel", "arbitrary")),
    )(x, g, w)


def _dilated_specs(tm, width, col_of):
    per_seq = SEQ // tm
    specs = []
    for d in C_DILATIONS:
        specs.append(pl.BlockSpec(
            (1, d, tm // d, width), lambda i, *rest: (i // per_seq, 0, i % per_seq, col_of(*rest))))
    return specs


def _dilated_shapes(n_seq, cols, dtype):
    return [jax.ShapeDtypeStruct((n_seq, d, SEQ // d, cols), dtype) for d in C_DILATIONS]


def _store_dilated(src_ref, out_refs, dtype):
    groups, tm, _ = src_ref.shape
    for d, o_ref in zip(C_DILATIONS, out_refs):
        for r in range(d):
            rows = pl.ds(r, tm // d, stride=d) if d > 1 else slice(None)
            for p in range(groups):
                o_ref[0, r, :, p * LANES:(p + 1) * LANES] = src_ref.at[p][rows, :].astype(dtype)


def _load_dilated(in_ref, d, dst_ref):
    groups, tm, _ = dst_ref.shape
    for r in range(d):
        rows = pl.ds(r, tm // d, stride=d)
        for p in range(groups):
            dst_ref.at[p][rows, :] = in_ref[0, r, :, p * LANES:(p + 1) * LANES].astype(F32)


def _norm_inproj_rope(x, g, w, rope, name):
    t = x.shape[0]
    nb = w.shape[2]
    tm = ROW_TILE

    def body(x_ref, g_ref, w_ref, c_ref, s_ref, o1_ref, o4_ref, o16_ref, h_ref, tile_ref):
        j = pl.program_id(1)

        @pl.when(j == 0)
        def _():
            h_ref[...] = _rms(x_ref[...], g_ref[...]).astype(BF16)

        acc = _dot(h_ref[...], w_ref[0])
        for p in range(nb // LANES):
            blk = acc[:, p * LANES:(p + 1) * LANES]
            roped = blk * c_ref[...] + _swap_halves(blk) * s_ref[...]
            is_qk = (j * (nb // LANES) + p) < 2 * (D_MODEL // LANES)
            tile_ref[p] = jnp.where(is_qk, roped, blk)
        _store_dilated(tile_ref, (o1_ref, o4_ref, o16_ref), BF16)

    return pl.pallas_call(
        body, name=name, grid=(t // tm, N_DEV),
        in_specs=[pl.BlockSpec((tm, D_MODEL), lambda i, j: (i, 0)), pl.BlockSpec((1, D_MODEL), lambda i, j: (0, 0)),
                  pl.BlockSpec((1, D_MODEL, nb), lambda i, j: (j, 0, 0)),
                  pl.BlockSpec((tm, LANES), lambda i, j: (i, 0)), pl.BlockSpec((tm, LANES), lambda i, j: (i, 0))],
        out_specs=_dilated_specs(tm, nb, lambda j: j) + [pl.BlockSpec((tm, D_MODEL), lambda i, j: (i, 0))],
        out_shape=_dilated_shapes(t // SEQ, N_DEV * nb, BF16) + [jax.ShapeDtypeStruct((t, D_MODEL), BF16)],
        scratch_shapes=[pltpu.VMEM((nb // LANES, tm, LANES), F32)],
        compiler_params=_params(("parallel", "arbitrary")),
    )(x, g, w, *rope)


def _outproj(parts, w, x, g, name):
    t = x.shape[0]
    tm = ROW_TILE
    n = len(parts)
    widths = [p.shape[1] for p in parts]

    def body(*refs):
        p_refs = refs[:n]
        w_ref, x_ref, g_ref, xo_ref, mix_ref = refs[n:]
        mix = None
        off = 0
        for p_ref, wd in zip(p_refs, widths):
            term = _dot(p_ref[...].astype(BF16), w_ref[off:off + wd, :])
            mix = term if mix is None else mix + term
            off += wd
        mix_ref[...] = mix
        xo_ref[...] = x_ref[...] + _rms(mix, g_ref[...])

    row = lambda i: (i, 0)
    return pl.pallas_call(
        body, name=name, grid=(t // tm,),
        in_specs=[pl.BlockSpec((tm, wd), row) for wd in widths] + [
            pl.BlockSpec((sum(widths), D_MODEL), lambda i: (0, 0)),
            pl.BlockSpec((tm, D_MODEL), row), pl.BlockSpec((1, D_MODEL), lambda i: (0, 0))],
        out_specs=[pl.BlockSpec((tm, D_MODEL), row)] * 2,
        out_shape=[jax.ShapeDtypeStruct((t, D_MODEL), F32)] * 2,
        compiler_params=_params(("parallel",)),
    )(*parts, w, x, g)


def _outproj_bwd(dx, mix, g, w, name):
    t = dx.shape[0]
    tm = ROW_TILE
    k = w.shape[0]

    def body(dx_ref, mix_ref, g_ref, w_ref, dcat_ref, dz_ref, dg_ref):
        dz, dgr = _rms_bwd(mix_ref[...], g_ref[...], dx_ref[...])
        dzb = dz.astype(BF16)
        dz_ref[...] = dzb
        dcat_ref[...] = _dot_nt(dzb, w_ref[...])
        _acc_rows8(dg_ref, _rows8(dgr), pl.program_id(0) == 0)

    row = lambda i: (i, 0)
    return pl.pallas_call(
        body, name=name, grid=(t // tm,),
        in_specs=[pl.BlockSpec((tm, D_MODEL), row), pl.BlockSpec((tm, D_MODEL), row),
                  pl.BlockSpec((1, D_MODEL), lambda i: (0, 0)), pl.BlockSpec((k, D_MODEL), lambda i: (0, 0))],
        out_specs=[pl.BlockSpec((tm, k), row), pl.BlockSpec((tm, D_MODEL), row),
                   pl.BlockSpec((SUBLANES, D_MODEL), lambda i: (0, 0))],
        out_shape=[jax.ShapeDtypeStruct((t, k), F32), jax.ShapeDtypeStruct((t, D_MODEL), BF16),
                   jax.ShapeDtypeStruct((SUBLANES, D_MODEL), F32)],
        compiler_params=_params(("arbitrary",)),
    )(dx, mix, g, w)


def _outproj_bwd_attn(dx, mix, g, w, out, name):
    t = dx.shape[0]
    tm = MERGE_TILE

    def body(dx_ref, mix_ref, g_ref, w_ref, out_ref, do1, do4, do16, dl1, dl4, dl16, dz_ref, dg_ref, tile_ref):
        dz, dgr = _rms_bwd(mix_ref[...], g_ref[...], dx_ref[...])
        dzb = dz.astype(BF16)
        dz_ref[...] = dzb
        _acc_rows8(dg_ref, _rows8(dgr), pl.program_id(0) == 0)
        dout = _dot_nt(dzb, w_ref[...])
        for p in range(LANE_GROUPS):
            tile_ref[p] = dout[:, p * LANES:(p + 1) * LANES]
        _store_dilated(tile_ref, (do1, do4, do16), BF16)
        r = lax.broadcasted_iota(jnp.int32, (LANES, LANES), 0) // C_HEAD_DIM
        c = lax.broadcasted_iota(jnp.int32, (LANES, LANES), 1) // C_HEAD_DIM
        same_head = (r == c).astype(F32)
        prod = dout * out_ref[...]
        for p in range(LANE_GROUPS):
            tile_ref[p] = jnp.dot(prod[:, p * LANES:(p + 1) * LANES], same_head, precision=lax.Precision.HIGHEST,
                                  preferred_element_type=F32)
        _store_dilated(tile_ref, (dl1, dl4, dl16), F32)

    row = lambda i: (i, 0)
    n_seq = t // SEQ
    return pl.pallas_call(
        body, name=name, grid=(t // tm,),
        in_specs=[pl.BlockSpec((tm, D_MODEL), row), pl.BlockSpec((tm, D_MODEL), row),
                  pl.BlockSpec((1, D_MODEL), lambda i: (0, 0)), pl.BlockSpec((D_MODEL, D_MODEL), lambda i: (0, 0)),
                  pl.BlockSpec((tm, D_MODEL), row)],
        out_specs=_dilated_specs(tm, D_MODEL, lambda: 0) * 2 + [
            pl.BlockSpec((tm, D_MODEL), row), pl.BlockSpec((SUBLANES, D_MODEL), lambda i: (0, 0))],
        out_shape=_dilated_shapes(n_seq, D_MODEL, BF16) + _dilated_shapes(n_seq, D_MODEL, F32) + [
            jax.ShapeDtypeStruct((t, D_MODEL), BF16), jax.ShapeDtypeStruct((SUBLANES, D_MODEL), F32)],
        scratch_shapes=[pltpu.VMEM((LANE_GROUPS, tm, LANES), F32)],
        compiler_params=_params(("arbitrary",)),
    )(dx, mix, g, w, out)


def _inproj_bwd(dproj, w, dx, x, g, name):
    t = x.shape[0]
    nb = w.shape[2]
    tm = ROW_TILE

    def body(dp_ref, w_ref, dx_ref, x_ref, g_ref, o_ref, dg_ref, acc_ref):
        i, j = pl.program_id(0), pl.program_id(1)
        term = _dot_nt(dp_ref[...], w_ref[0])

        @pl.when(j == 0)
        def _():
            acc_ref[...] = term

        @pl.when(j > 0)
        def _():
            acc_ref[...] += term

        @pl.when(j == N_DEV - 1)
        def _():
            dxn, dgr = _rms_bwd(x_ref[...], g_ref[...], acc_ref[...])
            o_ref[...] = dx_ref[...] + dxn
            _acc_rows8(dg_ref, _rows8(dgr), i == 0)

    return pl.pallas_call(
        body, name=name, grid=(t // tm, N_DEV),
        in_specs=[pl.BlockSpec((tm, nb), lambda i, j: (i, j)), pl.BlockSpec((1, D_MODEL, nb), lambda i, j: (j, 0, 0)),
                  pl.BlockSpec((tm, D_MODEL), lambda i, j: (i, 0)), pl.BlockSpec((tm, D_MODEL), lambda i, j: (i, 0)),
                  pl.BlockSpec((1, D_MODEL), lambda i, j: (0, 0))],
        out_specs=[pl.BlockSpec((tm, D_MODEL), lambda i, j: (i, 0)),
                   pl.BlockSpec((SUBLANES, D_MODEL), lambda i, j: (0, 0))],
        out_shape=[jax.ShapeDtypeStruct((t, D_MODEL), F32), jax.ShapeDtypeStruct((SUBLANES, D_MODEL), F32)],
        scratch_shapes=[pltpu.VMEM((tm, D_MODEL), F32)],
        compiler_params=_params(("arbitrary", "arbitrary")),
    )(dproj, w, dx, x, g)


def _grad_w(a, b, col_blocks, name):
    t, k = a.shape
    n = b.shape[1]
    tt = ROW_TILE
    tk = min(k, 512)
    tn = n // N_DEV if col_blocks else min(n, 1024)
    nt = t // tt

    def body(a_ref, b_ref, o_ref, acc_ref):
        s = pl.program_id(2)
        term = _dot_tn(a_ref[...], b_ref[...])

        @pl.when(s == 0)
        def _():
            acc_ref[...] = term

        @pl.when(s > 0)
        def _():
            acc_ref[...] += term

        @pl.when(s == nt - 1)
        def _():
            o_ref[...] = acc_ref[...].astype(BF16).reshape(o_ref.shape)

    if col_blocks:
        out_spec = pl.BlockSpec((1, tk, tn), lambda i, j, s: (j, i, 0))
        out_shape = jax.ShapeDtypeStruct((N_DEV, k, tn), BF16)
    else:
        out_spec = pl.BlockSpec((tk, tn), lambda i, j, s: (i, j))
        out_shape = jax.ShapeDtypeStruct((k, n), BF16)
    return pl.pallas_call(
        body, name=name, grid=(k // tk, n // tn, nt),
        in_specs=[pl.BlockSpec((tt, tk), lambda i, j, s: (s, i)), pl.BlockSpec((tt, tn), lambda i, j, s: (s, j))],
        out_specs=out_spec, out_shape=out_shape,
        scratch_shapes=[pltpu.VMEM((tk, tn), F32)],
        compiler_params=_params(("parallel", "parallel", "arbitrary")),
    )(a, b)


FF_BLOCK = D_FF // N_DEV


def _ffn_fwd(x, g_pre, w1, w2, g_post, name):
    t = x.shape[0]
    tm = ROW_TILE

    def body(x_ref, gp_ref, w1_ref, w2_ref, gq_ref, xo_ref, y_ref, h_ref):
        j = pl.program_id(1)

        @pl.when(j == 0)
        def _():
            h_ref[...] = _rms(x_ref[...], gp_ref[...]).astype(BF16)

        a = _dot(h_ref[...], w1_ref[0])
        r = jnp.square(jnp.maximum(a, 0.0)).astype(BF16)
        term = _dot(r, w2_ref[...])

        @pl.when(j == 0)
        def _():
            y_ref[...] = term

        @pl.when(j > 0)
        def _():
            y_ref[...] += term

        @pl.when(j == N_DEV - 1)
        def _():
            xo_ref[...] = x_ref[...] + _rms(y_ref[...], gq_ref[...])

    row = lambda i, j: (i, 0)
    vec = pl.BlockSpec((1, D_MODEL), lambda i, j: (0, 0))
    return pl.pallas_call(
        body, name=name, grid=(t // tm, N_DEV),
        in_specs=[pl.BlockSpec((tm, D_MODEL), row), vec,
                  pl.BlockSpec((1, D_MODEL, FF_BLOCK), lambda i, j: (j, 0, 0)),
                  pl.BlockSpec((FF_BLOCK, D_MODEL), lambda i, j: (j, 0)), vec],
        out_specs=[pl.BlockSpec((tm, D_MODEL), row)] * 3,
        out_shape=[jax.ShapeDtypeStruct((t, D_MODEL), F32), jax.ShapeDtypeStruct((t, D_MODEL), F32),
                   jax.ShapeDtypeStruct((t, D_MODEL), BF16)],
        compiler_params=_params(("parallel", "arbitrary")),
    )(x, g_pre, w1, w2, g_post)


def _ffn_bwd(dxo, x, y, h, g_pre, w1, w2, g_post, name):
    t = x.shape[0]
    tm = ROW_TILE

    def body(dxo_ref, x_ref, y_ref, h_ref, gp_ref, w1_ref, w2_ref, gq_ref,
             dx_ref, dy_ref, r_ref, da_ref, dgp_ref, dgq_ref, acc_ref):
        i, j = pl.program_id(0), pl.program_id(1)

        @pl.when(j == 0)
        def _():
            dy, dgr = _rms_bwd(y_ref[...], gq_ref[...], dxo_ref[...])
            dy_ref[...] = dy.astype(BF16)
            _acc_rows8(dgq_ref, _rows8(dgr), i == 0)

        a = _dot(h_ref[...], w1_ref[0])
        ra = jnp.maximum(a, 0.0)
        r_ref[...] = jnp.square(ra).astype(BF16)
        dr = _dot_nt(dy_ref[...], w2_ref[...])
        da = (dr * (2.0 * ra)).astype(BF16)
        da_ref[...] = da
        term = _dot_nt(da, w1_ref[0])

        @pl.when(j == 0)
        def _():
            acc_ref[...] = term

        @pl.when(j > 0)
        def _():
            acc_ref[...] += term

        @pl.when(j == N_DEV - 1)
        def _():
            dxn, dgr = _rms_bwd(x_ref[...], gp_ref[...], acc_ref[...])
            dx_ref[...] = dxo_ref[...] + dxn
            _acc_rows8(dgp_ref, _rows8(dgr), i == 0)

    row = lambda i, j: (i, 0)
    vec = pl.BlockSpec((1, D_MODEL), lambda i, j: (0, 0))
    acc8 = pl.BlockSpec((SUBLANES, D_MODEL), lambda i, j: (0, 0))
    return pl.pallas_call(
        body, name=name, grid=(t // tm, N_DEV),
        in_specs=[pl.BlockSpec((tm, D_MODEL), row)] * 4 + [
            vec, pl.BlockSpec((1, D_MODEL, FF_BLOCK), lambda i, j: (j, 0, 0)),
            pl.BlockSpec((FF_BLOCK, D_MODEL), lambda i, j: (j, 0)), vec],
        out_specs=[pl.BlockSpec((tm, D_MODEL), row), pl.BlockSpec((tm, D_MODEL), row),
                   pl.BlockSpec((tm, FF_BLOCK), lambda i, j: (i, j)), pl.BlockSpec((tm, FF_BLOCK), lambda i, j: (i, j)),
                   acc8, acc8],
        out_shape=[jax.ShapeDtypeStruct((t, D_MODEL), F32), jax.ShapeDtypeStruct((t, D_MODEL), BF16),
                   jax.ShapeDtypeStruct((t, D_FF), BF16), jax.ShapeDtypeStruct((t, D_FF), BF16),
                   jax.ShapeDtypeStruct((SUBLANES, D_MODEL), F32), jax.ShapeDtypeStruct((SUBLANES, D_MODEL), F32)],
        scratch_shapes=[pltpu.VMEM((tm, D_MODEL), F32)],
        compiler_params=_params(("arbitrary", "arbitrary")),
    )(dxo, x, y, h, g_pre, w1, w2, g_post)


def _lower_bound(table):
    e = jnp.exp(table - jnp.max(table, axis=0, keepdims=True))
    return e[0:1, :] / jnp.sum(e, axis=0, keepdims=True)


def _hgrn2_step(st, qraw, fl, v, graw, lb, an, tri):
    f = lb + (1.0 - lb) * _sigmoid(fl)
    logf = jnp.log(f)
    kk = 1.0 - f
    q = qraw * _sigmoid(qraw)
    gsum = jnp.dot(tri, logf, precision=lax.Precision.HIGHEST, preferred_element_type=F32)
    o = _dot_nt((q * jnp.exp(gsum)).astype(BF16), st.astype(BF16))
    row = lax.broadcasted_iota(jnp.int32, gsum.shape, 0)
    for s in range(SUB_CHUNK):
        pick = row == s
        g_s = jnp.sum(jnp.where(pick, gsum, 0.0), axis=0, keepdims=True)
        k_s = jnp.sum(jnp.where(pick, kk, 0.0), axis=0, keepdims=True)
        v_s = jnp.sum(jnp.where(pick, v, 0.0), axis=0, keepdims=True)
        decay = jnp.exp(jnp.where(row >= s, gsum - g_s, NEG))
        score = jnp.sum(q * k_s * decay, axis=1, keepdims=True)
        o = o + score * v_s
    g_last = jnp.sum(jnp.where(row == SUB_CHUNK - 1, gsum, 0.0), axis=0, keepdims=True)
    kd = kk * jnp.exp(g_last - gsum)
    st_new = st * jnp.exp(g_last) + _dot_tn(v.astype(BF16), kd.astype(BF16))
    out = _rms(o, an) * (graw * _sigmoid(graw))
    return st_new, out


def _tri():
    r = lax.broadcasted_iota(jnp.int32, (SUB_CHUNK, SUB_CHUNK), 0)
    c = lax.broadcasted_iota(jnp.int32, (SUB_CHUNK, SUB_CHUNK), 1)
    return (c <= r).astype(F32)


def _hgrn2_fwd(proj, lb_table, a_norm, name):
    t = proj.shape[0]
    tb = HGRN_BLOCK
    n_tb = SEQ // tb
    n_seq = t // SEQ
    n_sub = tb // SUB_CHUNK

    def body(q_ref, f_ref, i_ref, g_ref, lbt_ref, an_ref, o_ref, sts_ref, st_ref):
        @pl.when(pl.program_id(1) == 0)
        def _():
            st_ref[...] = jnp.zeros_like(st_ref)

        lb = _lower_bound(lbt_ref[...])
        an = an_ref[...]
        tri = _tri()

        def step(c, carry):
            rows = pl.ds(pl.multiple_of(c * SUB_CHUNK, SUB_CHUNK), SUB_CHUNK)
            for h in range(A_HEADS):
                lanes = slice(h * HEAD_A, (h + 1) * HEAD_A)
                st = st_ref[h]
                sts_ref[0, c, h] = st
                st_new, out = _hgrn2_step(st, q_ref[rows, lanes], f_ref[rows, lanes], i_ref[rows, lanes],
                                          g_ref[rows, lanes], lb[:, lanes], an[:, lanes], tri)
                st_ref[h] = st_new
                o_ref[rows, lanes] = out.astype(BF16)
            return carry

        lax.fori_loop(0, n_sub, step, 0)

    def col(k):
        return pl.BlockSpec((tb, A_WIDTH), lambda b, s, k=k: (b * n_tb + s, k))

    return pl.pallas_call(
        body, name=name, grid=(n_seq, n_tb),
        in_specs=[col(0), col(1), col(2), col(3),
                  pl.BlockSpec((3, A_WIDTH), lambda b, s: (0, 0)), pl.BlockSpec((1, A_WIDTH), lambda b, s: (0, 0))],
        out_specs=[pl.BlockSpec((tb, A_WIDTH), lambda b, s: (b * n_tb + s, 0)),
                   pl.BlockSpec((1, n_sub, A_HEADS, HEAD_A, HEAD_A), lambda b, s: (b * n_tb + s, 0, 0, 0, 0))],
        out_shape=[jax.ShapeDtypeStruct((t, A_WIDTH), BF16),
                   jax.ShapeDtypeStruct((n_seq * n_tb, n_sub, A_HEADS, HEAD_A, HEAD_A), F32)],
        scratch_shapes=[pltpu.VMEM((A_HEADS, HEAD_A, HEAD_A), F32)],
        compiler_params=_params(("parallel", "arbitrary")),
    )(proj, proj, proj, proj, lb_table, a_norm)


def _hgrn2_bwd(proj, dcat, states, lb_table, a_norm, name):
    t = proj.shape[0]
    tb = HGRN_BLOCK
    n_tb = SEQ // tb
    n_seq = t // SEQ
    n_sub = tb // SUB_CHUNK

    def body(q_ref, f_ref, i_ref, g_ref, do_ref, sts_ref, lbt_ref, an_ref, dp_ref, dlb_ref, dan_ref, dst_ref):
        b, s = pl.program_id(0), pl.program_id(1)

        @pl.when(s == 0)
        def _():
            dst_ref[...] = jnp.zeros_like(dst_ref)

        @pl.when((b == 0) & (s == 0))
        def _():
            dlb_ref[...] = jnp.zeros_like(dlb_ref)
            dan_ref[...] = jnp.zeros_like(dan_ref)

        lb = _lower_bound(lbt_ref[...])
        an = an_ref[...]
        tri = _tri()

        def bwd(k, carry):
            c = n_sub - 1 - k
            rows = pl.ds(pl.multiple_of(c * SUB_CHUNK, SUB_CHUNK), SUB_CHUNK)
            for h in range(A_HEADS):
                lanes = slice(h * HEAD_A, (h + 1) * HEAD_A)
                _, vjp = jax.vjp(
                    lambda st, a1, a2, a3, a4, a5, a6: _hgrn2_step(st, a1, a2, a3, a4, a5, a6, tri),
                    sts_ref[0, c, h], q_ref[rows, lanes], f_ref[rows, lanes], i_ref[rows, lanes], g_ref[rows, lanes],
                    lb[:, lanes], an[:, lanes])
                dst, dq, df, di, dg, dlb, dan = vjp((dst_ref[h], do_ref[rows, lanes]))
                dst_ref[h] = dst
                for sec, val in enumerate((dq, df, di, dg)):
                    dp_ref[rows, sec * A_WIDTH + h * HEAD_A:sec * A_WIDTH + (h + 1) * HEAD_A] = val.astype(BF16)
                dlb_ref[0:1, lanes] += dlb
                dan_ref[0:1, lanes] += dan
            return carry

        lax.fori_loop(0, n_sub, bwd, 0)

    def rev(s):
        return n_tb - 1 - s

    def col(k):
        return pl.BlockSpec((tb, A_WIDTH), lambda b, s, k=k: (b * n_tb + rev(s), k))

    acc8 = pl.BlockSpec((SUBLANES, A_WIDTH), lambda b, s: (0, 0))
    return pl.pallas_call(
        body, name=name, grid=(n_seq, n_tb),
        in_specs=[col(0), col(1), col(2), col(3), col(0),
                  pl.BlockSpec((1, n_sub, A_HEADS, HEAD_A, HEAD_A), lambda b, s: (b * n_tb + rev(s), 0, 0, 0, 0)),
                  pl.BlockSpec((3, A_WIDTH), lambda b, s: (0, 0)), pl.BlockSpec((1, A_WIDTH), lambda b, s: (0, 0))],
        out_specs=[pl.BlockSpec((tb, 4 * A_WIDTH), lambda b, s: (b * n_tb + rev(s), 0)), acc8, acc8],
        out_shape=[jax.ShapeDtypeStruct((t, 4 * A_WIDTH), BF16)] + [jax.ShapeDtypeStruct((SUBLANES, A_WIDTH), F32)] * 2,
        scratch_shapes=[pltpu.VMEM((A_HEADS, HEAD_A, HEAD_A), F32)],
        compiler_params=_params(("arbitrary", "arbitrary")),
    )(proj, proj, proj, proj, dcat, states, lb_table, a_norm)


GMLP_ROWS = 512


def _gmlp_chunk(ub, vb, ln_g, ln_b, ws, bias):
    u = [_gelu(a) for a in ub]
    v = [_gelu(a) for a in vb]
    mu = sum(jnp.sum(a, axis=-1, keepdims=True) for a in v) * (1.0 / B_WIDTH)
    cen = [a - mu for a in v]
    var = sum(jnp.sum(a * a, axis=-1, keepdims=True) for a in cen) * (1.0 / B_WIDTH)
    inv = lax.rsqrt(var + EPS)
    r = lax.broadcasted_iota(jnp.int32, (B_CHUNK, B_CHUNK), 0)
    c = lax.broadcasted_iota(jnp.int32, (B_CHUNK, B_CHUNK), 1)
    outs = []
    for g in range(B_GROUPS):
        vn = (cen[g] * inv * ln_g[g] + ln_b[g]).astype(BF16)
        wm = jnp.where(c <= r, ws[g], 0.0).astype(BF16)
        outs.append(u[g] * (_dot(wm, vn) + bias[g]))
    return outs


def _lane_groups(ref, rows=slice(None)):
    return [ref[rows, g * LANES:(g + 1) * LANES] for g in range(B_GROUPS)]


def _gmlp_fwd(proj, ln_g, ln_b, ws, bias_t, name):
    t = proj.shape[0]
    tm = GMLP_ROWS

    def body(u_ref, v_ref, lg_ref, lb_ref, ws_ref, bt_ref, o_ref):
        for ch in range(tm // B_CHUNK):
            rows = slice(ch * B_CHUNK, (ch + 1) * B_CHUNK)
            outs = _gmlp_chunk(_lane_groups(u_ref, rows), _lane_groups(v_ref, rows), _lane_groups(lg_ref),
                               _lane_groups(lb_ref), [ws_ref[g] for g in range(B_GROUPS)],
                               [bt_ref[:, g:g + 1] for g in range(B_GROUPS)])
            for g in range(B_GROUPS):
                o_ref[rows, g * LANES:(g + 1) * LANES] = outs[g].astype(BF16)

    vec = pl.BlockSpec((1, B_WIDTH), lambda i: (0, 0))
    return pl.pallas_call(
        body, name=name, grid=(t // tm,),
        in_specs=[pl.BlockSpec((tm, B_WIDTH), lambda i: (i, 4)), pl.BlockSpec((tm, B_WIDTH), lambda i: (i, 5)), vec, vec,
                  pl.BlockSpec((B_GROUPS, B_CHUNK, B_CHUNK), lambda i: (0, 0, 0)),
                  pl.BlockSpec((B_CHUNK, B_GROUPS), lambda i: (0, 0))],
        out_specs=pl.BlockSpec((tm, B_WIDTH), lambda i: (i, 0)),
        out_shape=jax.ShapeDtypeStruct((t, B_WIDTH), BF16),
        compiler_params=_params(("parallel",)),
    )(proj, proj, ln_g, ln_b, ws, bias_t)


def _gmlp_bwd(proj, dcat, ln_g, ln_b, ws, bias_t, name):
    t = proj.shape[0]
    tm = GMLP_ROWS

    def body(u_ref, v_ref, do_ref, lg_ref, lb_ref, ws_ref, bt_ref, duv_ref, dlg_ref, dlb_ref, dws_ref, dbt_ref):
        @pl.when(pl.program_id(0) == 0)
        def _():
            dlg_ref[...] = jnp.zeros_like(dlg_ref)
            dlb_ref[...] = jnp.zeros_like(dlb_ref)
            dws_ref[...] = jnp.zeros_like(dws_ref)
            dbt_ref[...] = jnp.zeros_like(dbt_ref)

        for ch in range(tm // B_CHUNK):
            rows = slice(ch * B_CHUNK, (ch + 1) * B_CHUNK)
            _, vjp = jax.vjp(
                _gmlp_chunk, _lane_groups(u_ref, rows), _lane_groups(v_ref, rows), _lane_groups(lg_ref),
                _lane_groups(lb_ref), [ws_ref[g] for g in range(B_GROUPS)],
                [bt_ref[:, g:g + 1] for g in range(B_GROUPS)])
            du, dv, dlg, dlb, dw, dbt = vjp(_lane_groups(do_ref, rows))
            for g in range(B_GROUPS):
                lanes = slice(g * LANES, (g + 1) * LANES)
                duv_ref[rows, lanes] = du[g].astype(BF16)
                duv_ref[rows, B_WIDTH + g * LANES:B_WIDTH + (g + 1) * LANES] = dv[g].astype(BF16)
                dlg_ref[0:1, lanes] += dlg[g]
                dlb_ref[0:1, lanes] += dlb[g]
                dws_ref[g] += dw[g]
                dbt_ref[:, g:g + 1] += dbt[g]

    vec = pl.BlockSpec((1, B_WIDTH), lambda i: (0, 0))
    acc8 = pl.BlockSpec((SUBLANES, B_WIDTH), lambda i: (0, 0))
    ws_spec = pl.BlockSpec((B_GROUPS, B_CHUNK, B_CHUNK), lambda i: (0, 0, 0))
    bt_spec = pl.BlockSpec((B_CHUNK, B_GROUPS), lambda i: (0, 0))
    return pl.pallas_call(
        body, name=name, grid=(t // tm,),
        in_specs=[pl.BlockSpec((tm, B_WIDTH), lambda i: (i, 4)), pl.BlockSpec((tm, B_WIDTH), lambda i: (i, 5)),
                  pl.BlockSpec((tm, B_WIDTH), lambda i: (i, 1)), vec, vec, ws_spec, bt_spec],
        out_specs=[pl.BlockSpec((tm, 2 * B_WIDTH), lambda i: (i, 0)), acc8, acc8, ws_spec, bt_spec],
        out_shape=[jax.ShapeDtypeStruct((t, 2 * B_WIDTH), BF16), jax.ShapeDtypeStruct((SUBLANES, B_WIDTH), F32),
                   jax.ShapeDtypeStruct((SUBLANES, B_WIDTH), F32),
                   jax.ShapeDtypeStruct((B_GROUPS, B_CHUNK, B_CHUNK), F32),
                   jax.ShapeDtypeStruct((B_CHUNK, B_GROUPS), F32)],
        compiler_params=_params(("arbitrary",)),
    )(proj, proj, dcat, ln_g, ln_b, ws, bias_t)


QK_SCALE = 1.0 / math.sqrt(C_HEAD_DIM)
LANE_GROUPS = D_MODEL // LANES


def _attn_window(n, l):
    kw = min(2 * C_BLOCK, l)
    q0 = pl.multiple_of(n * C_BLOCK, C_BLOCK)
    k0 = pl.multiple_of(jnp.maximum(n - 1, 0) * C_BLOCK, C_BLOCK)
    dist = (q0 + lax.broadcasted_iota(jnp.int32, (C_BLOCK, kw), 0)) - (k0 + lax.broadcasted_iota(jnp.int32, (C_BLOCK, kw), 1))
    return q0, k0, kw, (dist >= 0) & (dist <= C_BLOCK)


def _head_masks():
    lane = lax.broadcasted_iota(jnp.int32, (C_BLOCK, LANES), 1)
    return [lane < C_HEAD_DIM, lane >= C_HEAD_DIM]


def _plane_spec(d, col_of):
    return pl.BlockSpec((1, d, SEQ // d, LANES), lambda b, g: (b, 0, 0, col_of(g)))


def _attn_branch_fwd(qkv, d, name):
    n_seq, _, l, _ = qkv.shape
    n_blk = l // C_BLOCK

    def body(q_ref, k_ref, v_ref, o_ref, m_ref, l_ref):
        heads = _head_masks()

        def block(idx, carry):
            r, n = idx // n_blk, idx % n_blk
            q0, k0, kw, mask = _attn_window(n, l)
            q = q_ref[0, r, pl.ds(q0, C_BLOCK), :]
            k = k_ref[0, r, pl.ds(k0, kw), :]
            v = v_ref[0, r, pl.ds(k0, kw), :]
            res = []
            for hm in heads:
                s = jnp.where(mask, _dot_nt(jnp.where(hm, q, 0), k) * QK_SCALE, NEG)
                m = jnp.max(s, axis=-1, keepdims=True)
                p = jnp.exp(s - m)
                res.append((_dot(p.astype(BF16), v), m, jnp.sum(p, axis=-1, keepdims=True)))
            rows = pl.ds(q0, C_BLOCK)
            o_ref[0, r, rows, :] = jnp.where(heads[0], res[0][0], res[1][0])
            m_ref[0, r, rows, :] = jnp.where(heads[0], res[0][1], res[1][1])
            l_ref[0, r, rows, :] = jnp.where(heads[0], res[0][2], res[1][2])
            return carry

        lax.fori_loop(0, d * n_blk, block, 0)

    return pl.pallas_call(
        body, name=name, grid=(n_seq, LANE_GROUPS),
        in_specs=[_plane_spec(d, lambda g: g), _plane_spec(d, lambda g: LANE_GROUPS + g),
                  _plane_spec(d, lambda g: 2 * LANE_GROUPS + g)],
        out_specs=[_plane_spec(d, lambda g: g)] * 3,
        out_shape=[jax.ShapeDtypeStruct((n_seq, d, l, D_MODEL), F32)] * 3,
        compiler_params=_params(("parallel", "parallel")),
    )(qkv, qkv, qkv)


def _attn_merge(branches, name):
    n_seq = branches[0][0].shape[0]
    t = n_seq * SEQ
    tm = MERGE_TILE

    def body(*refs):
        ins = refs[:9]
        o_ref, ob_ref, lse1_ref, lse4_ref, lse16_ref = refs[9:14]
        nat = refs[14:]
        for b, d in enumerate(C_DILATIONS[1:]):
            for k in range(3):
                _load_dilated(ins[3 + 3 * b + k], d, nat[3 * b + k])
        for p in range(LANE_GROUPS):
            lanes = slice(p * LANES, (p + 1) * LANES)
            os_ = [ins[0][0, 0, :, lanes], nat[0][p], nat[3][p]]
            ms = [ins[1][0, 0, :, lanes], nat[1][p], nat[4][p]]
            ls = [ins[2][0, 0, :, lanes], nat[2][p], nat[5][p]]
            m_all = jnp.maximum(jnp.maximum(ms[0], ms[1]), ms[2])
            ws = [jnp.exp(ms[b] - m_all) for b in range(3)]
            total = ws[0] * ls[0] + ws[1] * ls[1] + ws[2] * ls[2]
            o = (ws[0] * os_[0] + ws[1] * os_[1] + ws[2] * os_[2]) / total
            o_ref[:, lanes] = o
            ob_ref[:, lanes] = o.astype(BF16)
            nat[0][p] = m_all + jnp.log(total)
        _store_dilated(nat[0], (lse1_ref, lse4_ref, lse16_ref), F32)

    row = pl.BlockSpec((tm, D_MODEL), lambda i: (i, 0))
    flat = [a for br in branches for a in br]
    in_specs = []
    for spec in _dilated_specs(tm, D_MODEL, lambda: 0):
        in_specs += [spec] * 3
    return pl.pallas_call(
        body, name=name, grid=(t // tm,), in_specs=in_specs,
        out_specs=[row, row] + _dilated_specs(tm, D_MODEL, lambda: 0),
        out_shape=[jax.ShapeDtypeStruct((t, D_MODEL), F32), jax.ShapeDtypeStruct((t, D_MODEL), BF16)]
        + _dilated_shapes(n_seq, D_MODEL, F32),
        scratch_shapes=[pltpu.VMEM((LANE_GROUPS, tm, LANES), F32)] * 6,
        compiler_params=_params(("parallel",)),
    )(*flat)


def _attn_branch_bwd(qkv, dout, lse, delta, d, name):
    n_seq, _, l, _ = qkv.shape
    n_blk = l // C_BLOCK

    def body(q_ref, k_ref, v_ref, do_ref, lse_ref, dl_ref, dq_ref, dk_ref, dv_ref):
        heads = _head_masks()
        dk_ref[...] = jnp.zeros_like(dk_ref)
        dv_ref[...] = jnp.zeros_like(dv_ref)

        def block(idx, carry):
            r, n = idx // n_blk, idx % n_blk
            q0, k0, kw, mask = _attn_window(n, l)
            rows, keys = pl.ds(q0, C_BLOCK), pl.ds(k0, kw)
            q, do = q_ref[0, r, rows, :], do_ref[0, r, rows, :]
            k, v = k_ref[0, r, keys, :], v_ref[0, r, keys, :]
            lse_b, dl_b = lse_ref[0, r, rows, :], dl_ref[0, r, rows, :]
            dq, dk, dv = [], None, None
            for hh, hm in enumerate(heads):
                col = slice(hh * C_HEAD_DIM, hh * C_HEAD_DIM + 1)
                qh, doh = jnp.where(hm, q, 0), jnp.where(hm, do, 0)
                s = jnp.where(mask, _dot_nt(qh, k) * QK_SCALE, NEG)
                p = jnp.exp(s - lse_b[:, col])
                ds = (p * (_dot_nt(doh, v) - dl_b[:, col]) * QK_SCALE).astype(BF16)
                dq.append(_dot(ds, k))
                dk_h, dv_h = _dot_tn(ds, qh), _dot_tn(p.astype(BF16), doh)
                dk = dk_h if dk is None else dk + dk_h
                dv = dv_h if dv is None else dv + dv_h
            dq_ref[0, r, rows, :] = jnp.where(heads[0], dq[0], dq[1])
            dk_ref[0, r, keys, :] += dk
            dv_ref[0, r, keys, :] += dv
            return carry

        lax.fori_loop(0, d * n_blk, block, 0)

    act = _plane_spec(d, lambda g: g)
    return pl.pallas_call(
        body, name=name, grid=(n_seq, LANE_GROUPS),
        in_specs=[_plane_spec(d, lambda g: g), _plane_spec(d, lambda g: LANE_GROUPS + g),
                  _plane_spec(d, lambda g: 2 * LANE_GROUPS + g), act, act, act],
        out_specs=[act] * 3,
        out_shape=[jax.ShapeDtypeStruct((n_seq, d, l, D_MODEL), F32)] * 3,
        compiler_params=_params(("parallel", "parallel")),
    )(qkv, qkv, qkv, dout, lse, delta)


def _attn_combine_bwd(grads, rope, name):
    n_seq = grads[0][0].shape[0]
    t = n_seq * SEQ
    tm = MERGE_TILE

    def body(*refs):
        c_ref, s_ref, o_ref, nat4_ref, nat16_ref = refs[9:]
        for sec in range(3):
            _load_dilated(refs[3 + sec], 4, nat4_ref)
            _load_dilated(refs[6 + sec], 16, nat16_ref)
            for p in range(LANE_GROUPS):
                blk = refs[sec][0, 0, :, p * LANES:(p + 1) * LANES] + nat4_ref[p] + nat16_ref[p]
                if sec < 2:
                    blk = blk * c_ref[...] - _swap_halves(blk) * s_ref[...]
                o_ref[:, sec * D_MODEL + p * LANES:sec * D_MODEL + (p + 1) * LANES] = blk.astype(BF16)

    tab = pl.BlockSpec((tm, LANES), lambda i: (i, 0))
    flat = [a for br in grads for a in br]
    in_specs = []
    for spec in _dilated_specs(tm, D_MODEL, lambda: 0):
        in_specs += [spec] * 3
    return pl.pallas_call(
        body, name=name, grid=(t // tm,), in_specs=in_specs + [tab, tab],
        out_specs=pl.BlockSpec((tm, ODD_IN), lambda i: (i, 0)),
        out_shape=jax.ShapeDtypeStruct((t, ODD_IN), BF16),
        scratch_shapes=[pltpu.VMEM((LANE_GROUPS, tm, LANES), F32)] * 2,
        compiler_params=_params(("parallel",)),
    )(*flat, *rope)


def _loss_grad(y, target, name):
    t = y.shape[0]
    tm = ROW_TILE

    def body(y_ref, t_ref, d_ref, l_ref):
        diff = y_ref[...] - t_ref[...]
        d_ref[...] = diff * (1.0 / D_MODEL)
        _acc_rows8(l_ref, _rows8(diff * diff) * (0.5 / D_MODEL), pl.program_id(0) == 0)

    row = pl.BlockSpec((tm, D_MODEL), lambda i: (i, 0))
    return pl.pallas_call(
        body, name=name, grid=(t // tm,), in_specs=[row, row],
        out_specs=[row, pl.BlockSpec((SUBLANES, D_MODEL), lambda i: (0, 0))],
        out_shape=[jax.ShapeDtypeStruct((t, D_MODEL), F32), jax.ShapeDtypeStruct((SUBLANES, D_MODEL), F32)],
        compiler_params=_params(("arbitrary",)),
    )(y, target)


def _adamw(w, g, m, v):
    m = ADAM_B1 * m + (1.0 - ADAM_B1) * g
    v = ADAM_B2 * v + (1.0 - ADAM_B2) * jnp.square(g)
    m_hat = m / (1.0 - ADAM_B1 ** ADAM_STEP)
    v_hat = v / (1.0 - ADAM_B2 ** ADAM_STEP)
    delta = -ADAM_LR * (m_hat / (jnp.sqrt(v_hat) + ADAM_EPS) + ADAM_WD * w)
    return delta, m, v


def _adamw_sharded(parts, w, m, v, name):
    n_layers, rows, cols = w.shape
    tr = min(rows, 256)

    def body(*refs):
        p_refs = refs[:n_layers]
        w_ref, m_ref, v_ref, g_ref, d_ref, mo_ref, vo_ref = refs[n_layers:]
        layer = pl.program_id(0)
        g = None
        for l, p_ref in enumerate(p_refs):
            g_l = p_ref[0].astype(F32)
            for s in range(1, N_DEV):
                g_l = g_l + p_ref[s].astype(F32)
            g = g_l if g is None else jnp.where(layer == l, g_l, g)
        delta, mn, vn = _adamw(w_ref[0], g, m_ref[0], v_ref[0])
        g_ref[0] = g
        d_ref[0] = delta
        mo_ref[0] = mn
        vo_ref[0] = vn

    def part_spec(l):
        return pl.BlockSpec((N_DEV, tr, cols), lambda a, i: (0, jnp.where(a == l, i, 0), 0))

    row = pl.BlockSpec((1, tr, cols), lambda a, i: (a, i, 0))
    return pl.pallas_call(
        body, name=name, grid=(n_layers, rows // tr),
        in_specs=[part_spec(l) for l in range(n_layers)] + [row, row, row],
        out_specs=[row] * 4, out_shape=[jax.ShapeDtypeStruct(w.shape, F32)] * 4,
        compiler_params=_params(("arbitrary", "arbitrary")),
    )(*parts, w, m, v)


def _small_update(gathered, weights, moments_m, moments_v, lb_index, name):
    n = len(weights)

    def total(ref):
        acc = ref[0]
        for s in range(1, N_DEV):
            acc = acc + ref[s]
        return acc

    def body(*refs):
        g_refs = refs[:n + 1]
        w_refs, m_refs, v_refs = refs[n + 1:2 * n + 1], refs[2 * n + 1:3 * n + 1], refs[3 * n + 1:4 * n + 1]
        outs = refs[4 * n + 1:]
        loss_rows = total(g_refs[n])
        outs[0][...] = jnp.sum(jnp.sum(loss_rows, axis=1, keepdims=True), axis=0, keepdims=True)
        for k in range(n):
            part = total(g_refs[k])
            if k == lb_index:
                dlb = jnp.sum(part, axis=0, keepdims=True)
                tab = w_refs[k][...]
                e = jnp.exp(tab - jnp.max(tab, axis=0, keepdims=True))
                p = e / jnp.sum(e, axis=0, keepdims=True)
                first = lax.broadcasted_iota(jnp.int32, p.shape, 0) == 0
                grads = [(slice(None), p * (jnp.where(first, dlb, 0.0) - p[0:1, :] * dlb))]
            elif part.shape == w_refs[k].shape:
                grads = [(slice(None), part)]
            else:
                grads = [(slice(l, l + 1), jnp.sum(part[l * SUBLANES:(l + 1) * SUBLANES], axis=0, keepdims=True))
                         for l in range(w_refs[k].shape[0])]
            for rows, g in grads:
                delta, mn, vn = _adamw(w_refs[k][rows], g, m_refs[k][rows], v_refs[k][rows])
                outs[1 + 4 * k][rows] = g
                outs[2 + 4 * k][rows] = delta
                outs[3 + 4 * k][rows] = mn
                outs[4 + 4 * k][rows] = vn

    vmem = pl.BlockSpec(memory_space=pltpu.VMEM)
    out_shape = [jax.ShapeDtypeStruct((1, 1), F32)]
    for w in weights:
        out_shape += [jax.ShapeDtypeStruct(w.shape, F32)] * 4
    args = list(gathered) + list(weights) + list(moments_m) + list(moments_v)
    return pl.pallas_call(
        body, name=name, in_specs=[vmem] * len(args), out_specs=[vmem] * len(out_shape), out_shape=out_shape,
        compiler_params=pltpu.CompilerParams(vmem_limit_bytes=VMEM_LIMIT),
    )(*args)


def kernel(x, positions, norm_mix_pre, norm_mix_post, norm_ffn_pre, norm_ffn_post, w_in_even, lb_table, a_norm, b_ln_g, b_ln_b, b_ws, b_bias, w_out_even, w_in_odd, w_out_odd, w_ff1, w_ff2, loss_target, m_norm_mix_pre, m_norm_mix_post, m_norm_ffn_pre, m_norm_ffn_post, m_w_in_even, m_lb_table, m_a_norm, m_b_ln_g, m_b_ln_b, m_b_ws, m_b_bias, m_w_out_even, m_w_in_odd, m_w_out_odd, m_w_ff1, m_w_ff2, v_norm_mix_pre, v_norm_mix_post, v_norm_ffn_pre, v_norm_ffn_post, v_w_in_even, v_lb_table, v_a_norm, v_b_ln_g, v_b_ln_b, v_b_ws, v_b_bias, v_w_out_even, v_w_in_odd, v_w_out_odd, v_w_ff1, v_w_ff2):
    n_seq = x.shape[0]
    t = n_seq * SEQ
    x0 = x.reshape(t, D_MODEL)
    target = loss_target.reshape(t, D_MODEL)

    me = _my_slot().astype(jnp.int32).reshape(1)

    order = ["in_e", "out_e", "ff1_0", "ff2_0", "in_o", "out_o", "ff1_1", "ff2_1"]
    shards = dict(in_e=w_in_even[0], out_e=w_out_even[0], in_o=w_in_odd[0], out_o=w_out_odd[0],
                  ff1_0=w_ff1[0], ff1_1=w_ff1[1], ff2_0=w_ff2[0], ff2_1=w_ff2[1])
    lands = [_place_own(shards[k], me, "place_" + k, False) for k in order]
    g_send, g_recv, lands, _, g_token = _exchange_start(lands, [None] * len(order), "gather_start")

    def get_w(keys, after):
        ks = [order.index(k) for k in keys]
        return _exchange_wait([lands[k] for k in ks], [None] * len(ks), [g_send[k] for k in ks],
                              [g_recv[k] for k in ks], after, "gather_wait_" + keys[0])

    sent = {}

    def put_g(group, blocks):
        keys = list(blocks)
        own = [_place_own(blocks[k], me, "own_" + k, True) for k in keys]
        send_sems, recv_sems, own, srcs, token = _exchange_start(own, [blocks[k] for k in keys], "scatter_start_" + group)
        sent[group] = (keys, own, srcs, send_sems, recv_sems)
        return token

    rope = _rope_tables(positions)
    bias_t = b_bias[0].T
    grads = _local_step(x0, target, rope, norm_mix_pre, norm_mix_post, norm_ffn_pre, norm_ffn_post, lb_table,
                        a_norm, b_ln_g, b_ln_b, b_ws[0], bias_t, get_w, put_g, g_token)
    (dx0, loss_part, dg_mix_pre, dg_mix_post, dg_ffn_pre, dg_ffn_post, d_lb, d_a_norm, d_ln_g, d_ln_b, d_ws,
     d_bias_t) = grads

    recv = {}
    for group, (keys, own, srcs, send_sems, recv_sems) in sent.items():
        done = _exchange_wait(own, srcs, send_sems, recv_sems, dx0, "scatter_wait_" + group)
        recv.update(zip(keys, done))
    big = [("w_in_even", ["in_e"], w_in_even, m_w_in_even, v_w_in_even),
           ("w_out_even", ["out_e"], w_out_even, m_w_out_even, v_w_out_even),
           ("w_in_odd", ["in_o"], w_in_odd, m_w_in_odd, v_w_in_odd),
           ("w_out_odd", ["out_o"], w_out_odd, m_w_out_odd, v_w_out_odd),
           ("w_ff1", ["ff1_0", "ff1_1"], w_ff1, m_w_ff1, v_w_ff1), ("w_ff2", ["ff2_0", "ff2_1"], w_ff2, m_w_ff2, v_w_ff2)]
    big_out = [_adamw_sharded([recv[k] for k in keys], w, m, v, "adamw_" + nm) for nm, keys, w, m, v in big]

    small_parts = [dg_mix_pre, dg_mix_post, dg_ffn_pre, dg_ffn_post,
                   d_lb, d_a_norm, d_ln_g, d_ln_b, d_ws, d_bias_t, loss_part]
    gathered = _exchange(small_parts, True, "gather_small")
    small_w = [norm_mix_pre, norm_mix_post, norm_ffn_pre, norm_ffn_post, lb_table, a_norm, b_ln_g, b_ln_b,
               b_ws[0], bias_t]
    small_m = [m_norm_mix_pre, m_norm_mix_post, m_norm_ffn_pre, m_norm_ffn_post, m_lb_table, m_a_norm, m_b_ln_g,
               m_b_ln_b, m_b_ws[0], m_b_bias[0].T]
    small_v = [v_norm_mix_pre, v_norm_mix_post, v_norm_ffn_pre, v_norm_ffn_post, v_lb_table, v_a_norm, v_b_ln_g,
               v_b_ln_b, v_b_ws[0], v_b_bias[0].T]
    small_out = _small_update(gathered, small_w, small_m, small_v, 4, "small_update")
    loss = small_out[0].reshape(())
    small = [small_out[1 + 4 * k:5 + 4 * k] for k in range(len(small_w))]
    small[8] = [a[None] for a in small[8]]
    small[9] = [a.T[None] for a in small[9]]

    per_weight = small[0:4] + [big_out[0]] + small[4:10] + big_out[1:6]
    grad_x = dx0.reshape(x.shape)
    out = [loss, grad_x]
    for kind in range(4):
        out += [p[kind] for p in per_weight]
    return tuple(out)


def _local_step(x0, target, rope, norm_mix_pre, norm_mix_post, norm_ffn_pre, norm_ffn_post, lb_table, a_norm,
                b_ln_g, b_ln_b, ws, bias_t, get_w, put_g, token):
    def gain(a, l, tok):
        return a[l:l + 1] if tok is None else a[l:l + 1] + tok[0:1, 0:1]

    full = lambda a: a.reshape(-1, D_MODEL)
    owners = lambda a: a.reshape((N_DEV, -1) + a.shape[1:])

    (g_in_e,) = get_w(["in_e"], token)
    proj, h_mix0 = _norm_inproj(x0, gain(norm_mix_pre, 0, token), g_in_e, "inproj_even")
    oa, states = _hgrn2_fwd(proj, lb_table, a_norm, "hgrn2_fwd")
    ob = _gmlp_fwd(proj, b_ln_g, b_ln_b, ws, bias_t, "gmlp_fwd")
    w_out_e = full(get_w(["out_e"], ob)[0])
    x1, mix0 = _outproj([oa, ob], w_out_e, x0, gain(norm_mix_post, 0, None), "outproj_even")
    w1_0, w2_0 = get_w(["ff1_0", "ff2_0"], x1)
    w2_0 = full(w2_0)
    x2, y0, h_ffn0 = _ffn_fwd(x1, gain(norm_ffn_pre, 0, None), w1_0, w2_0, gain(norm_ffn_post, 0, None), "ffn_fwd_0")
    (g_in_o,) = get_w(["in_o"], x2)
    *qkv, h_mix1 = _norm_inproj_rope(x2, gain(norm_mix_pre, 1, None), g_in_o, rope, "inproj_odd")
    branches = [_attn_branch_fwd(a, d, "attn_fwd_d%d" % d) for a, d in zip(qkv, C_DILATIONS)]
    attn, attn_b, *lse = _attn_merge(branches, "attn_merge")
    w_out_o = full(get_w(["out_o"], attn_b)[0])
    x3, mix1 = _outproj([attn_b], w_out_o, x2, gain(norm_mix_post, 1, None), "outproj_odd")
    w1_1, w2_1 = get_w(["ff1_1", "ff2_1"], x3)
    w2_1 = full(w2_1)
    x4, y1, h_ffn1 = _ffn_fwd(x3, gain(norm_ffn_pre, 1, None), w1_1, w2_1, gain(norm_ffn_post, 1, None), "ffn_fwd_1")

    dx4, loss_part = _loss_grad(x4, target, "loss_grad")

    dx3, dy1, r1, da1, dg_ffn_pre1, dg_ffn_post1 = _ffn_bwd(
        dx4, x3, y1, h_ffn1, gain(norm_ffn_pre, 1, None), w1_1, w2_1, gain(norm_ffn_post, 1, None), "ffn_bwd_1")
    gw_ff1_1 = _grad_w(h_ffn1, da1, True, "grad_w_ff1_1")
    gw_ff2_1 = _grad_w(r1, dy1, False, "grad_w_ff2_1")
    tok = put_g("ffn1", dict(ff1_1=gw_ff1_1, ff2_1=owners(gw_ff2_1)))
    *dattn, dz1, dg_mix_post1 = _outproj_bwd_attn(dx3, mix1, gain(norm_mix_post, 1, tok), w_out_o, attn,
                                                  "outproj_bwd_odd")
    gw_out_o = _grad_w(attn_b, dz1, False, "grad_w_out_odd")
    grads_c = [_attn_branch_bwd(qkv[b], dattn[b], lse[b], dattn[3 + b], d, "attn_bwd_d%d" % d)
               for b, d in enumerate(C_DILATIONS)]
    dqkv = _attn_combine_bwd(grads_c, rope, "attn_combine_bwd")
    gw_in_o = _grad_w(h_mix1, dqkv, True, "grad_w_in_odd")
    tok = put_g("mix1", dict(out_o=owners(gw_out_o), in_o=gw_in_o))
    dx2, dg_mix_pre1 = _inproj_bwd(dqkv, g_in_o, dx3, x2, gain(norm_mix_pre, 1, tok), "inproj_bwd_odd")

    dx1, dy0, r0, da0, dg_ffn_pre0, dg_ffn_post0 = _ffn_bwd(
        dx2, x1, y0, h_ffn0, gain(norm_ffn_pre, 0, None), w1_0, w2_0, gain(norm_ffn_post, 0, None), "ffn_bwd_0")
    gw_ff1_0 = _grad_w(h_ffn0, da0, True, "grad_w_ff1_0")
    gw_ff2_0 = _grad_w(r0, dy0, False, "grad_w_ff2_0")
    tok = put_g("ffn0", dict(ff1_0=gw_ff1_0, ff2_0=owners(gw_ff2_0)))
    dcat, dz0, dg_mix_post0 = _outproj_bwd(dx1, mix0, gain(norm_mix_post, 0, tok), w_out_e, "outproj_bwd_even")
    gw_out_e = jnp.concatenate([_grad_w(oa, dz0, False, "grad_w_out_even_a"),
                                _grad_w(ob, dz0, False, "grad_w_out_even_b")], axis=0)
    dqfig, d_lb, d_a_norm = _hgrn2_bwd(proj, dcat, states, lb_table, a_norm, "hgrn2_bwd")
    duv, d_ln_g, d_ln_b, d_ws, d_bias_t = _gmlp_bwd(proj, dcat, b_ln_g, b_ln_b, ws, bias_t, "gmlp_bwd")
    dproj = jnp.concatenate([dqfig, duv], axis=1)
    gw_in_e = _grad_w(h_mix0, dproj, True, "grad_w_in_even")
    tok = put_g("mix0", dict(out_e=owners(gw_out_e), in_e=gw_in_e))
    dx0, dg_mix_pre0 = _inproj_bwd(dproj, g_in_e, dx1, x0, gain(norm_mix_pre, 0, tok), "inproj_bwd_even")

    layers = lambda a, b: jnp.concatenate([a, b], axis=0)
    return (dx0, loss_part, layers(dg_mix_pre0, dg_mix_pre1), layers(dg_mix_post0, dg_mix_post1),
            layers(dg_ffn_pre0, dg_ffn_pre1), layers(dg_ffn_post0, dg_ffn_post1),
            d_lb, d_a_norm, d_ln_g, d_ln_b, d_ws, d_bias_t)
```

```python
import functools
import math

import jax
import jax.numpy as jnp
from jax import lax
from jax.experimental import pallas as pl
from jax.experimental.pallas import tpu as pltpu

F32 = jnp.float32
BF16 = jnp.bfloat16
MESH = pl.DeviceIdType.MESH

N_DEV = 8
D_MODEL = 1024
SEQ = 2048
EPS = 1e-6
A_WIDTH = 512
A_HEADS = 4
HEAD_A = 128
B_WIDTH = 512
B_GROUPS = 4
B_CHUNK = 128
C_HEADS = 16
C_HEAD_DIM = 64
C_ROT_HALF = 8
ROPE_THETA = 500000.0
C_DILATIONS = (1, 4, 16)
C_BLOCK = 128
D_FF = 4096
EVEN_IN = 3072
ODD_IN = 3072

ADAM_LR = 0.001
ADAM_B1 = 0.9
ADAM_B2 = 0.999
ADAM_EPS = 1e-08
ADAM_WD = 0.01
ADAM_STEP = 10

LANES = 128
SUBLANES = 8
ROW_TILE = 512
PROJ_TILE = 1024
MERGE_TILE = 256
SUB_CHUNK = 16
HGRN_BLOCK = 256
NEG = -1e30
VMEM_LIMIT = 56 * 1024 * 1024


def _params(sem):
    return pltpu.CompilerParams(dimension_semantics=sem, vmem_limit_bytes=VMEM_LIMIT)


def _dot(a, b):
    return jnp.dot(a, b, preferred_element_type=F32)


def _dot_nt(a, b):
    return lax.dot_general(a, b, (((1,), (1,)), ((), ())), preferred_element_type=F32)


def _dot_tn(a, b):
    return lax.dot_general(a, b, (((0,), (0,)), ((), ())), preferred_element_type=F32)


def _rms(x, g):
    r = lax.rsqrt(jnp.mean(x * x, axis=-1, keepdims=True) + EPS)
    return x * r * g


def _rms_bwd(x, g, dy):
    r = lax.rsqrt(jnp.mean(x * x, axis=-1, keepdims=True) + EPS)
    dyg = dy * g
    dx = r * dyg - x * (r * r * r) * jnp.mean(x * dyg, axis=-1, keepdims=True)
    return dx, dy * x * r


def _rows8(v):
    return v.reshape(v.shape[0] // SUBLANES, SUBLANES, v.shape[1]).sum(axis=0)


def _sigmoid(x):
    return 1.0 / (1.0 + jnp.exp(-x))


def _gelu(x):
    return 0.5 * x * (1.0 + jnp.tanh(math.sqrt(2.0 / math.pi) * (x + 0.044715 * (x * x * x))))


def _acc_rows8(ref, val, first):
    @pl.when(first)
    def _():
        ref[...] = val

    @pl.when(jnp.logical_not(first))
    def _():
        ref[...] += val


def _my_slot():
    return 4 * lax.axis_index("x") + 2 * lax.axis_index("y") + lax.axis_index("c")


def _peer(r):
    x, y, c = lax.axis_index("x"), lax.axis_index("y"), lax.axis_index("c")
    px = 1 - x if (r >> 2) & 1 else x
    py = 1 - y if (r >> 1) & 1 else y
    pc = 1 - c if r & 1 else c
    return (px, py, pc), 4 * px + 2 * py + pc


def _exchange(arrays, gather, name):
    n = len(arrays)
    if gather:
        out_shape = [jax.ShapeDtypeStruct((N_DEV,) + a.shape, a.dtype) for a in arrays]
    else:
        out_shape = [jax.ShapeDtypeStruct(a.shape, a.dtype) for a in arrays]

    def body(*refs):
        ins, outs = refs[:n], refs[n:2 * n]
        send_sems, recv_sems, local_sems = refs[2 * n:]
        me = _my_slot()
        local, remote = [], []
        for k in range(n):
            src = ins[k] if gather else ins[k].at[me]
            local.append(pltpu.make_async_copy(src, outs[k].at[me], local_sems.at[k]))
            for r in range(1, N_DEV):
                peer, slot = _peer(r)
                src = ins[k] if gather else ins[k].at[slot]
                remote.append((pltpu.make_async_remote_copy(
                    src_ref=src, dst_ref=outs[k].at[me], send_sem=send_sems.at[k, r - 1],
                    recv_sem=recv_sems.at[k, r - 1], device_id=peer, device_id_type=MESH), k, r, slot))
        for cp in local:
            cp.start()
        for cp, _, _, _ in remote:
            cp.start()
        for cp, k, r, slot in remote:
            pltpu.make_async_remote_copy(
                src_ref=outs[k].at[slot], dst_ref=outs[k].at[slot], send_sem=send_sems.at[k, r - 1],
                recv_sem=recv_sems.at[k, r - 1], device_id=_peer(r)[0], device_id_type=MESH).wait_recv()
        for cp, _, _, _ in remote:
            cp.wait_send()
        for cp in local:
            cp.wait()

    any_spec = pl.BlockSpec(memory_space=pl.ANY)
    return pl.pallas_call(
        body, name=name, out_shape=out_shape,
        in_specs=[any_spec] * n, out_specs=[any_spec] * n,
        scratch_shapes=[pltpu.SemaphoreType.DMA((n, N_DEV - 1)), pltpu.SemaphoreType.DMA((n, N_DEV - 1)),
                        pltpu.SemaphoreType.DMA((n,))],
        compiler_params=pltpu.CompilerParams(has_side_effects=True),
    )(*arrays)


HBM_SPEC = pl.BlockSpec(memory_space=pltpu.HBM)
SEM_SPEC = pl.BlockSpec(memory_space=pltpu.SEMAPHORE)
SPLIT_EFFECT = pltpu.SideEffectType.DATAFLOW_SIDE_EFFECTING


def _split_copies(land_ref, src_ref, send_sem, recv_sem):
    me = _my_slot()
    copies = []
    for r in range(1, N_DEV):
        peer, slot = _peer(r)
        src = land_ref.at[me] if src_ref is None else src_ref.at[slot]
        copies.append(pltpu.make_async_remote_copy(
            src_ref=src, dst_ref=land_ref.at[me], send_sem=send_sem, recv_sem=recv_sem,
            device_id=peer, device_id_type=MESH))
    return copies


def _exchange_start(lands, sources, name):
    n = len(lands)
    given = [s for s in sources if s is not None]
    arrays = list(lands) + given

    def body(*refs):
        land_refs, src_refs = refs[:n], list(refs[n:n + len(given)])
        sems = refs[len(arrays):len(arrays) + 2 * n]
        token = refs[-1]
        for k in range(n):
            src_ref = None if sources[k] is None else src_refs.pop(0)
            for copy in _split_copies(land_refs[k], src_ref, sems[k], sems[n + k]):
                copy.start()
        token[...] = jnp.zeros_like(token)

    outs = pl.pallas_call(
        body, name=name,
        out_shape=(pltpu.SemaphoreType.DMA(()),) * (2 * n) + tuple(pltpu.HBM(a.shape, a.dtype) for a in arrays)
        + (jax.ShapeDtypeStruct((SUBLANES, LANES), F32),),
        in_specs=[HBM_SPEC] * len(arrays),
        out_specs=(SEM_SPEC,) * (2 * n) + (HBM_SPEC,) * len(arrays) + (pl.BlockSpec(memory_space=pltpu.VMEM),),
        input_output_aliases={i: 2 * n + i for i in range(len(arrays))},
        compiler_params=pltpu.CompilerParams(has_side_effects=SPLIT_EFFECT),
    )(*[pltpu.with_memory_space_constraint(a, pltpu.HBM) for a in arrays])
    return list(outs[:n]), list(outs[n:2 * n]), list(outs[2 * n:3 * n]), list(outs[3 * n:-1]), outs[-1]


def _exchange_wait(lands, sources, send_sems, recv_sems, after, name):
    n = len(lands)
    given = [s for s in sources if s is not None]
    arrays = list(lands) + given

    def body(*refs):
        land_refs, src_refs = refs[:n], list(refs[n:n + len(given)])
        sems = refs[len(arrays):len(arrays) + 2 * n]
        for i in range(n):
            src_ref = None if sources[i] is None else src_refs.pop(0)
            copies = _split_copies(land_refs[i], src_ref, sems[i], sems[n + i])
            for copy in copies:
                copy.wait_recv()
            for copy in copies:
                copy.wait_send()

    outs = pl.pallas_call(
        body, name=name, out_shape=tuple(pltpu.HBM(a.shape, a.dtype) for a in arrays),
        in_specs=[HBM_SPEC] * len(arrays) + [SEM_SPEC] * (2 * n) + [pl.BlockSpec(memory_space=pl.ANY)],
        out_specs=(HBM_SPEC,) * len(arrays),
        input_output_aliases={i: i for i in range(len(arrays))},
        compiler_params=pltpu.CompilerParams(has_side_effects=SPLIT_EFFECT),
    )(*arrays, *send_sems, *recv_sems, after)
    return list(outs[:n])


def _place_own(a, me, name, own_block):
    shape = a.shape[1:] if own_block else a.shape
    cols = shape[-1]
    a3 = a.reshape((N_DEV if own_block else 1, -1, cols))
    rows = a3.shape[1]
    tr = min(rows, 512)

    def body(me_ref, a_ref, o_ref):
        o_ref[...] = a_ref[...].astype(BF16)

    grid_spec = pltpu.PrefetchScalarGridSpec(
        num_scalar_prefetch=1, grid=(rows // tr,),
        in_specs=[pl.BlockSpec((1, tr, cols), lambda i, me_ref: (me_ref[0] if own_block else 0, i, 0))],
        out_specs=pl.BlockSpec((1, tr, cols), lambda i, me_ref: (me_ref[0], i, 0)))
    out = pl.pallas_call(
        body, name=name, grid_spec=grid_spec, out_shape=jax.ShapeDtypeStruct((N_DEV, rows, cols), BF16),
        compiler_params=_params(("arbitrary",)),
    )(me, a3)
    return out.reshape((N_DEV,) + shape)


def _rope_tables(positions):
    inv = ROPE_THETA ** (-jnp.arange(C_ROT_HALF, dtype=F32) / C_ROT_HALF)
    ang = positions.reshape(-1)[:, None].astype(F32) * inv
    cos, sin = jnp.cos(ang), jnp.sin(ang)
    t = ang.shape[0]
    ones = jnp.ones((t, C_HEAD_DIM - 2 * C_ROT_HALF), F32)
    c_head = jnp.concatenate([cos, cos, ones], axis=1)
    s_head = jnp.concatenate([-sin, sin, 0.0 * ones], axis=1)
    return jnp.concatenate([c_head, c_head], axis=1), jnp.concatenate([s_head, s_head], axis=1)


def _swap_halves(x):
    lane = lax.broadcasted_iota(jnp.int32, x.shape, 1) % C_HEAD_DIM
    return jnp.where(lane < C_ROT_HALF, pltpu.roll(x, LANES - C_ROT_HALF, 1), pltpu.roll(x, C_ROT_HALF, 1))


def _norm_inproj(x, g, w, name):
    t = x.shape[0]
    nb = w.shape[2]
    tm = PROJ_TILE

    def body(x_ref, g_ref, w_ref, o_ref, h_ref):
        @pl.when(pl.program_id(1) == 0)
        def _():
            h_ref[...] = _rms(x_ref[...], g_ref[...]).astype(BF16)

        o_ref[...] = _dot(h_ref[...], w_ref[0])

    return pl.pallas_call(
        body, name=name, grid=(t // tm, N_DEV),
        in_specs=[pl.BlockSpec((tm, D_MODEL), lambda i, j: (i, 0)), pl.BlockSpec((1, D_MODEL), lambda i, j: (0, 0)),
                  pl.BlockSpec((1, D_MODEL, nb), lambda i, j: (j, 0, 0))],
        out_specs=[pl.BlockSpec((tm, nb), lambda i, j: (i, j)), pl.BlockSpec((tm, D_MODEL), lambda i, j: (i, 0))],
        out_shape=[jax.ShapeDtypeStruct((t, N_DEV * nb), F32), jax.ShapeDtypeStruct((t, D_MODEL), BF16)],
        compiler_params=_params(("parallel", "arbitrary")),
    )(x, g, w)


def _dilated_specs(tm, width, col_of):
    per_seq = SEQ // tm
    specs = []
    for d in C_DILATIONS:
        specs.append(pl.BlockSpec(
            (1, d, tm // d, width), lambda i, *rest: (i // per_seq, 0, i % per_seq, col_of(*rest))))
    return specs


def _dilated_shapes(n_seq, cols, dtype):
    return [jax.ShapeDtypeStruct((n_seq, d, SEQ // d, cols), dtype) for d in C_DILATIONS]


def _store_dilated(src_ref, out_refs, dtype):
    groups, tm, _ = src_ref.shape
    for d, o_ref in zip(C_DILATIONS, out_refs):
        for r in range(d):
            rows = pl.ds(r, tm // d, stride=d) if d > 1 else slice(None)
            for p in range(groups):
                o_ref[0, r, :, p * LANES:(p + 1) * LANES] = src_ref.at[p][rows, :].astype(dtype)


def _load_dilated(in_ref, d, dst_ref):
    groups, tm, _ = dst_ref.shape
    for r in range(d):
        rows = pl.ds(r, tm // d, stride=d)
        for p in range(groups):
            dst_ref.at[p][rows, :] = in_ref[0, r, :, p * LANES:(p + 1) * LANES].astype(F32)


def _norm_inproj_rope(x, g, w, rope, name):
    t = x.shape[0]
    nb = w.shape[2]
    tm = PROJ_TILE

    def body(x_ref, g_ref, w_ref, c_ref, s_ref, o1_ref, o4_ref, o16_ref, h_ref, tile_ref):
        j = pl.program_id(1)

        @pl.when(j == 0)
        def _():
            h_ref[...] = _rms(x_ref[...], g_ref[...]).astype(BF16)

        acc = _dot(h_ref[...], w_ref[0])
        for p in range(nb // LANES):
            blk = acc[:, p * LANES:(p + 1) * LANES]
            roped = blk * c_ref[...] + _swap_halves(blk) * s_ref[...]
            is_qk = (j * (nb // LANES) + p) < 2 * (D_MODEL // LANES)
            tile_ref[p] = jnp.where(is_qk, roped, blk)
        _store_dilated(tile_ref, (o1_ref, o4_ref, o16_ref), BF16)

    return pl.pallas_call(
        body, name=name, grid=(t // tm, N_DEV),
        in_specs=[pl.BlockSpec((tm, D_MODEL), lambda i, j: (i, 0)), pl.BlockSpec((1, D_MODEL), lambda i, j: (0, 0)),
                  pl.BlockSpec((1, D_MODEL, nb), lambda i, j: (j, 0, 0)),
                  pl.BlockSpec((tm, LANES), lambda i, j: (i, 0)), pl.BlockSpec((tm, LANES), lambda i, j: (i, 0))],
        out_specs=_dilated_specs(tm, nb, lambda j: j) + [pl.BlockSpec((tm, D_MODEL), lambda i, j: (i, 0))],
        out_shape=_dilated_shapes(t // SEQ, N_DEV * nb, BF16) + [jax.ShapeDtypeStruct((t, D_MODEL), BF16)],
        scratch_shapes=[pltpu.VMEM((nb // LANES, tm, LANES), F32)],
        compiler_params=_params(("parallel", "arbitrary")),
    )(x, g, w, *rope)


def _outproj(parts, w, x, g, name):
    t = x.shape[0]
    tm = ROW_TILE
    n = len(parts)
    widths = [p.shape[1] for p in parts]

    def body(*refs):
        p_refs = refs[:n]
        w_ref, x_ref, g_ref, xo_ref, mix_ref = refs[n:]
        mix = None
        off = 0
        for p_ref, wd in zip(p_refs, widths):
            term = _dot(p_ref[...].astype(BF16), w_ref[off:off + wd, :])
            mix = term if mix is None else mix + term
            off += wd
        mix_ref[...] = mix
        xo_ref[...] = x_ref[...] + _rms(mix, g_ref[...])

    row = lambda i: (i, 0)
    return pl.pallas_call(
        body, name=name, grid=(t // tm,),
        in_specs=[pl.BlockSpec((tm, wd), row) for wd in widths] + [
            pl.BlockSpec((sum(widths), D_MODEL), lambda i: (0, 0)),
            pl.BlockSpec((tm, D_MODEL), row), pl.BlockSpec((1, D_MODEL), lambda i: (0, 0))],
        out_specs=[pl.BlockSpec((tm, D_MODEL), row)] * 2,
        out_shape=[jax.ShapeDtypeStruct((t, D_MODEL), F32)] * 2,
        compiler_params=_params(("parallel",)),
    )(*parts, w, x, g)


def _outproj_bwd(dx, mix, g, w, name):
    t = dx.shape[0]
    tm = ROW_TILE
    k = w.shape[0]

    def body(dx_ref, mix_ref, g_ref, w_ref, dcat_ref, dz_ref, dg_ref):
        dz, dgr = _rms_bwd(mix_ref[...], g_ref[...], dx_ref[...])
        dzb = dz.astype(BF16)
        dz_ref[...] = dzb
        dcat_ref[...] = _dot_nt(dzb, w_ref[...])
        _acc_rows8(dg_ref, _rows8(dgr), pl.program_id(0) == 0)

    row = lambda i: (i, 0)
    return pl.pallas_call(
        body, name=name, grid=(t // tm,),
        in_specs=[pl.BlockSpec((tm, D_MODEL), row), pl.BlockSpec((tm, D_MODEL), row),
                  pl.BlockSpec((1, D_MODEL), lambda i: (0, 0)), pl.BlockSpec((k, D_MODEL), lambda i: (0, 0))],
        out_specs=[pl.BlockSpec((tm, k), row), pl.BlockSpec((tm, D_MODEL), row),
                   pl.BlockSpec((SUBLANES, D_MODEL), lambda i: (0, 0))],
        out_shape=[jax.ShapeDtypeStruct((t, k), F32), jax.ShapeDtypeStruct((t, D_MODEL), BF16),
                   jax.ShapeDtypeStruct((SUBLANES, D_MODEL), F32)],
        compiler_params=_params(("arbitrary",)),
    )(dx, mix, g, w)


def _outproj_bwd_attn(dx, mix, g, w, out, name):
    t = dx.shape[0]
    tm = MERGE_TILE

    def body(dx_ref, mix_ref, g_ref, w_ref, out_ref, do1, do4, do16, dl1, dl4, dl16, dz_ref, dg_ref, tile_ref):
        dz, dgr = _rms_bwd(mix_ref[...], g_ref[...], dx_ref[...])
        dzb = dz.astype(BF16)
        dz_ref[...] = dzb
        _acc_rows8(dg_ref, _rows8(dgr), pl.program_id(0) == 0)
        dout = _dot_nt(dzb, w_ref[...])
        for p in range(LANE_GROUPS):
            tile_ref[p] = dout[:, p * LANES:(p + 1) * LANES]
        _store_dilated(tile_ref, (do1, do4, do16), BF16)
        r = lax.broadcasted_iota(jnp.int32, (LANES, LANES), 0) // C_HEAD_DIM
        c = lax.broadcasted_iota(jnp.int32, (LANES, LANES), 1) // C_HEAD_DIM
        same_head = (r == c).astype(F32)
        prod = dout * out_ref[...]
        for p in range(LANE_GROUPS):
            tile_ref[p] = jnp.dot(prod[:, p * LANES:(p + 1) * LANES], same_head, precision=lax.Precision.HIGHEST,
                                  preferred_element_type=F32)
        _store_dilated(tile_ref, (dl1, dl4, dl16), F32)

    row = lambda i: (i, 0)
    n_seq = t // SEQ
    return pl.pallas_call(
        body, name=name, grid=(t // tm,),
        in_specs=[pl.BlockSpec((tm, D_MODEL), row), pl.BlockSpec((tm, D_MODEL), row),
                  pl.BlockSpec((1, D_MODEL), lambda i: (0, 0)), pl.BlockSpec((D_MODEL, D_MODEL), lambda i: (0, 0)),
                  pl.BlockSpec((tm, D_MODEL), row)],
        out_specs=_dilated_specs(tm, D_MODEL, lambda: 0) * 2 + [
            pl.BlockSpec((tm, D_MODEL), row), pl.BlockSpec((SUBLANES, D_MODEL), lambda i: (0, 0))],
        out_shape=_dilated_shapes(n_seq, D_MODEL, BF16) + _dilated_shapes(n_seq, D_MODEL, F32) + [
            jax.ShapeDtypeStruct((t, D_MODEL), BF16), jax.ShapeDtypeStruct((SUBLANES, D_MODEL), F32)],
        scratch_shapes=[pltpu.VMEM((LANE_GROUPS, tm, LANES), F32)],
        compiler_params=_params(("arbitrary",)),
    )(dx, mix, g, w, out)


def _inproj_bwd(dproj, w, dx, x, g, name):
    t = x.shape[0]
    nb = w.shape[2]
    tm = PROJ_TILE

    def body(dp_ref, w_ref, dx_ref, x_ref, g_ref, o_ref, dg_ref, acc_ref):
        i, j = pl.program_id(0), pl.program_id(1)
        term = _dot_nt(dp_ref[...], w_ref[0])

        @pl.when(j == 0)
        def _():
            acc_ref[...] = term

        @pl.when(j > 0)
        def _():
            acc_ref[...] += term

        @pl.when(j == N_DEV - 1)
        def _():
            dxn, dgr = _rms_bwd(x_ref[...], g_ref[...], acc_ref[...])
            o_ref[...] = dx_ref[...] + dxn
            _acc_rows8(dg_ref, _rows8(dgr), i == 0)

    return pl.pallas_call(
        body, name=name, grid=(t // tm, N_DEV),
        in_specs=[pl.BlockSpec((tm, nb), lambda i, j: (i, j)), pl.BlockSpec((1, D_MODEL, nb), lambda i, j: (j, 0, 0)),
                  pl.BlockSpec((tm, D_MODEL), lambda i, j: (i, 0)), pl.BlockSpec((tm, D_MODEL), lambda i, j: (i, 0)),
                  pl.BlockSpec((1, D_MODEL), lambda i, j: (0, 0))],
        out_specs=[pl.BlockSpec((tm, D_MODEL), lambda i, j: (i, 0)),
                   pl.BlockSpec((SUBLANES, D_MODEL), lambda i, j: (0, 0))],
        out_shape=[jax.ShapeDtypeStruct((t, D_MODEL), F32), jax.ShapeDtypeStruct((SUBLANES, D_MODEL), F32)],
        scratch_shapes=[pltpu.VMEM((tm, D_MODEL), F32)],
        compiler_params=_params(("arbitrary", "arbitrary")),
    )(dproj, w, dx, x, g)


def _grad_w(a, b, col_blocks, name):
    t, k = a.shape
    n = b.shape[1]
    tt = ROW_TILE
    tk = min(k, 1024)
    tn = n // N_DEV if col_blocks else min(n, 1024)
    nt = t // tt

    def body(a_ref, b_ref, o_ref, acc_ref):
        s = pl.program_id(2)
        term = _dot_tn(a_ref[...], b_ref[...])

        @pl.when(s == 0)
        def _():
            acc_ref[...] = term

        @pl.when(s > 0)
        def _():
            acc_ref[...] += term

        @pl.when(s == nt - 1)
        def _():
            o_ref[...] = acc_ref[...].astype(BF16).reshape(o_ref.shape)

    if col_blocks:
        out_spec = pl.BlockSpec((1, tk, tn), lambda i, j, s: (j, i, 0))
        out_shape = jax.ShapeDtypeStruct((N_DEV, k, tn), BF16)
    else:
        out_spec = pl.BlockSpec((tk, tn), lambda i, j, s: (i, j))
        out_shape = jax.ShapeDtypeStruct((k, n), BF16)
    return pl.pallas_call(
        body, name=name, grid=(k // tk, n // tn, nt),
        in_specs=[pl.BlockSpec((tt, tk), lambda i, j, s: (s, i)), pl.BlockSpec((tt, tn), lambda i, j, s: (s, j))],
        out_specs=out_spec, out_shape=out_shape,
        scratch_shapes=[pltpu.VMEM((tk, tn), F32)],
        compiler_params=_params(("parallel", "parallel", "arbitrary")),
    )(a, b)


FF_BLOCK = D_FF // N_DEV


def _ffn_fwd(x, g_pre, w1, w2, g_post, name):
    t = x.shape[0]
    tm = ROW_TILE

    def body(x_ref, gp_ref, w1_ref, w2_ref, gq_ref, xo_ref, y_ref, h_ref):
        j = pl.program_id(1)

        @pl.when(j == 0)
        def _():
            h_ref[...] = _rms(x_ref[...], gp_ref[...]).astype(BF16)

        a = _dot(h_ref[...], w1_ref[0])
        r = jnp.square(jnp.maximum(a, 0.0)).astype(BF16)
        term = _dot(r, w2_ref[...])

        @pl.when(j == 0)
        def _():
            y_ref[...] = term

        @pl.when(j > 0)
        def _():
            y_ref[...] += term

        @pl.when(j == N_DEV - 1)
        def _():
            xo_ref[...] = x_ref[...] + _rms(y_ref[...], gq_ref[...])

    row = lambda i, j: (i, 0)
    vec = pl.BlockSpec((1, D_MODEL), lambda i, j: (0, 0))
    return pl.pallas_call(
        body, name=name, grid=(t // tm, N_DEV),
        in_specs=[pl.BlockSpec((tm, D_MODEL), row), vec,
                  pl.BlockSpec((1, D_MODEL, FF_BLOCK), lambda i, j: (j, 0, 0)),
                  pl.BlockSpec((FF_BLOCK, D_MODEL), lambda i, j: (j, 0)), vec],
        out_specs=[pl.BlockSpec((tm, D_MODEL), row)] * 3,
        out_shape=[jax.ShapeDtypeStruct((t, D_MODEL), F32), jax.ShapeDtypeStruct((t, D_MODEL), F32),
                   jax.ShapeDtypeStruct((t, D_MODEL), BF16)],
        compiler_params=_params(("parallel", "arbitrary")),
    )(x, g_pre, w1, w2, g_post)


def _ffn_bwd(dxo, x, y, h, g_pre, w1, w2, g_post, name):
    t = x.shape[0]
    tm = ROW_TILE

    def body(dxo_ref, x_ref, y_ref, h_ref, gp_ref, w1_ref, w2_ref, gq_ref,
             dx_ref, dy_ref, r_ref, da_ref, dgp_ref, dgq_ref, acc_ref):
        i, j = pl.program_id(0), pl.program_id(1)

        @pl.when(j == 0)
        def _():
            dy, dgr = _rms_bwd(y_ref[...], gq_ref[...], dxo_ref[...])
            dy_ref[...] = dy.astype(BF16)
            _acc_rows8(dgq_ref, _rows8(dgr), i == 0)

        a = _dot(h_ref[...], w1_ref[0])
        ra = jnp.maximum(a, 0.0)
        r_ref[...] = jnp.square(ra).astype(BF16)
        dr = _dot_nt(dy_ref[...], w2_ref[...])
        da = (dr * (2.0 * ra)).astype(BF16)
        da_ref[...] = da
        term = _dot_nt(da, w1_ref[0])

        @pl.when(j == 0)
        def _():
            acc_ref[...] = term

        @pl.when(j > 0)
        def _():
            acc_ref[...] += term

        @pl.when(j == N_DEV - 1)
        def _():
            dxn, dgr = _rms_bwd(x_ref[...], gp_ref[...], acc_ref[...])
            dx_ref[...] = dxo_ref[...] + dxn
            _acc_rows8(dgp_ref, _rows8(dgr), i == 0)

    row = lambda i, j: (i, 0)
    vec = pl.BlockSpec((1, D_MODEL), lambda i, j: (0, 0))
    acc8 = pl.BlockSpec((SUBLANES, D_MODEL), lambda i, j: (0, 0))
    return pl.pallas_call(
        body, name=name, grid=(t // tm, N_DEV),
        in_specs=[pl.BlockSpec((tm, D_MODEL), row)] * 4 + [
            vec, pl.BlockSpec((1, D_MODEL, FF_BLOCK), lambda i, j: (j, 0, 0)),
            pl.BlockSpec((FF_BLOCK, D_MODEL), lambda i, j: (j, 0)), vec],
        out_specs=[pl.BlockSpec((tm, D_MODEL), row), pl.BlockSpec((tm, D_MODEL), row),
                   pl.BlockSpec((tm, FF_BLOCK), lambda i, j: (i, j)), pl.BlockSpec((tm, FF_BLOCK), lambda i, j: (i, j)),
                   acc8, acc8],
        out_shape=[jax.ShapeDtypeStruct((t, D_MODEL), F32), jax.ShapeDtypeStruct((t, D_MODEL), BF16),
                   jax.ShapeDtypeStruct((t, D_FF), BF16), jax.ShapeDtypeStruct((t, D_FF), BF16),
                   jax.ShapeDtypeStruct((SUBLANES, D_MODEL), F32), jax.ShapeDtypeStruct((SUBLANES, D_MODEL), F32)],
        scratch_shapes=[pltpu.VMEM((tm, D_MODEL), F32)],
        compiler_params=_params(("arbitrary", "arbitrary")),
    )(dxo, x, y, h, g_pre, w1, w2, g_post)


def _lower_bound(table):
    e = jnp.exp(table - jnp.max(table, axis=0, keepdims=True))
    return e[0:1, :] / jnp.sum(e, axis=0, keepdims=True)


def _hgrn2_step(st, qraw, fl, v, graw, lb, an, tri):
    f = lb + (1.0 - lb) * _sigmoid(fl)
    logf = jnp.log(f)
    kk = 1.0 - f
    q = qraw * _sigmoid(qraw)
    gsum = jnp.dot(tri, logf, precision=lax.Precision.HIGHEST, preferred_element_type=F32)
    o = _dot_nt((q * jnp.exp(gsum)).astype(BF16), st.astype(BF16))
    row = lax.broadcasted_iota(jnp.int32, gsum.shape, 0)
    for s in range(SUB_CHUNK):
        pick = row == s
        g_s = jnp.sum(jnp.where(pick, gsum, 0.0), axis=0, keepdims=True)
        k_s = jnp.sum(jnp.where(pick, kk, 0.0), axis=0, keepdims=True)
        v_s = jnp.sum(jnp.where(pick, v, 0.0), axis=0, keepdims=True)
        decay = jnp.exp(jnp.where(row >= s, gsum - g_s, NEG))
        score = jnp.sum(q * k_s * decay, axis=1, keepdims=True)
        o = o + score * v_s
    g_last = jnp.sum(jnp.where(row == SUB_CHUNK - 1, gsum, 0.0), axis=0, keepdims=True)
    kd = kk * jnp.exp(g_last - gsum)
    st_new = st * jnp.exp(g_last) + _dot_tn(v.astype(BF16), kd.astype(BF16))
    out = _rms(o, an) * (graw * _sigmoid(graw))
    return st_new, out


def _tri():
    r = lax.broadcasted_iota(jnp.int32, (SUB_CHUNK, SUB_CHUNK), 0)
    c = lax.broadcasted_iota(jnp.int32, (SUB_CHUNK, SUB_CHUNK), 1)
    return (c <= r).astype(F32)


def _hgrn2_fwd(proj, lb_table, a_norm, name):
    t = proj.shape[0]
    tb = HGRN_BLOCK
    n_tb = SEQ // tb
    n_seq = t // SEQ
    n_sub = tb // SUB_CHUNK

    def body(q_ref, f_ref, i_ref, g_ref, lbt_ref, an_ref, o_ref, sts_ref, st_ref):
        @pl.when(pl.program_id(1) == 0)
        def _():
            st_ref[...] = jnp.zeros_like(st_ref)

        lb = _lower_bound(lbt_ref[...])
        an = an_ref[...]
        tri = _tri()

        def step(c, carry):
            rows = pl.ds(pl.multiple_of(c * SUB_CHUNK, SUB_CHUNK), SUB_CHUNK)
            for h in range(A_HEADS):
                lanes = slice(h * HEAD_A, (h + 1) * HEAD_A)
                st = st_ref[h]
                sts_ref[0, c, h] = st
                st_new, out = _hgrn2_step(st, q_ref[rows, lanes], f_ref[rows, lanes], i_ref[rows, lanes],
                                          g_ref[rows, lanes], lb[:, lanes], an[:, lanes], tri)
                st_ref[h] = st_new
                o_ref[rows, lanes] = out.astype(BF16)
            return carry

        lax.fori_loop(0, n_sub, step, 0)

    def col(k):
        return pl.BlockSpec((tb, A_WIDTH), lambda b, s, k=k: (b * n_tb + s, k))

    return pl.pallas_call(
        body, name=name, grid=(n_seq, n_tb),
        in_specs=[col(0), col(1), col(2), col(3),
                  pl.BlockSpec((3, A_WIDTH), lambda b, s: (0, 0)), pl.BlockSpec((1, A_WIDTH), lambda b, s: (0, 0))],
        out_specs=[pl.BlockSpec((tb, A_WIDTH), lambda b, s: (b * n_tb + s, 0)),
                   pl.BlockSpec((1, n_sub, A_HEADS, HEAD_A, HEAD_A), lambda b, s: (b * n_tb + s, 0, 0, 0, 0))],
        out_shape=[jax.ShapeDtypeStruct((t, A_WIDTH), BF16),
                   jax.ShapeDtypeStruct((n_seq * n_tb, n_sub, A_HEADS, HEAD_A, HEAD_A), F32)],
        scratch_shapes=[pltpu.VMEM((A_HEADS, HEAD_A, HEAD_A), F32)],
        compiler_params=_params(("parallel", "arbitrary")),
    )(proj, proj, proj, proj, lb_table, a_norm)


def _hgrn2_bwd(proj, dcat, states, lb_table, a_norm, name):
    t = proj.shape[0]
    tb = HGRN_BLOCK
    n_tb = SEQ // tb
    n_seq = t // SEQ
    n_sub = tb // SUB_CHUNK

    def body(q_ref, f_ref, i_ref, g_ref, do_ref, sts_ref, lbt_ref, an_ref, dp_ref, dlb_ref, dan_ref, dst_ref):
        b, s = pl.program_id(0), pl.program_id(1)

        @pl.when(s == 0)
        def _():
            dst_ref[...] = jnp.zeros_like(dst_ref)

        @pl.when((b == 0) & (s == 0))
        def _():
            dlb_ref[...] = jnp.zeros_like(dlb_ref)
            dan_ref[...] = jnp.zeros_like(dan_ref)

        lb = _lower_bound(lbt_ref[...])
        an = an_ref[...]
        tri = _tri()

        def bwd(k, carry):
            c = n_sub - 1 - k
            rows = pl.ds(pl.multiple_of(c * SUB_CHUNK, SUB_CHUNK), SUB_CHUNK)
            for h in range(A_HEADS):
                lanes = slice(h * HEAD_A, (h + 1) * HEAD_A)
                _, vjp = jax.vjp(
                    lambda st, a1, a2, a3, a4, a5, a6: _hgrn2_step(st, a1, a2, a3, a4, a5, a6, tri),
                    sts_ref[0, c, h], q_ref[rows, lanes], f_ref[rows, lanes], i_ref[rows, lanes], g_ref[rows, lanes],
                    lb[:, lanes], an[:, lanes])
                dst, dq, df, di, dg, dlb, dan = vjp((dst_ref[h], do_ref[rows, lanes]))
                dst_ref[h] = dst
                for sec, val in enumerate((dq, df, di, dg)):
                    dp_ref[rows, sec * A_WIDTH + h * HEAD_A:sec * A_WIDTH + (h + 1) * HEAD_A] = val.astype(BF16)
                dlb_ref[0:1, lanes] += dlb
                dan_ref[0:1, lanes] += dan
            return carry

        lax.fori_loop(0, n_sub, bwd, 0)

    def rev(s):
        return n_tb - 1 - s

    def col(k):
        return pl.BlockSpec((tb, A_WIDTH), lambda b, s, k=k: (b * n_tb + rev(s), k))

    acc8 = pl.BlockSpec((SUBLANES, A_WIDTH), lambda b, s: (0, 0))
    return pl.pallas_call(
        body, name=name, grid=(n_seq, n_tb),
        in_specs=[col(0), col(1), col(2), col(3), col(0),
                  pl.BlockSpec((1, n_sub, A_HEADS, HEAD_A, HEAD_A), lambda b, s: (b * n_tb + rev(s), 0, 0, 0, 0)),
                  pl.BlockSpec((3, A_WIDTH), lambda b, s: (0, 0)), pl.BlockSpec((1, A_WIDTH), lambda b, s: (0, 0))],
        out_specs=[pl.BlockSpec((tb, 4 * A_WIDTH), lambda b, s: (b * n_tb + rev(s), 0)), acc8, acc8],
        out_shape=[jax.ShapeDtypeStruct((t, 4 * A_WIDTH), BF16)] + [jax.ShapeDtypeStruct((SUBLANES, A_WIDTH), F32)] * 2,
        scratch_shapes=[pltpu.VMEM((A_HEADS, HEAD_A, HEAD_A), F32)],
        compiler_params=_params(("arbitrary", "arbitrary")),
    )(proj, proj, proj, proj, dcat, states, lb_table, a_norm)


GMLP_ROWS = 512


def _gmlp_chunk(ub, vb, ln_g, ln_b, ws, bias):
    u = [_gelu(a) for a in ub]
    v = [_gelu(a) for a in vb]
    mu = sum(jnp.sum(a, axis=-1, keepdims=True) for a in v) * (1.0 / B_WIDTH)
    cen = [a - mu for a in v]
    var = sum(jnp.sum(a * a, axis=-1, keepdims=True) for a in cen) * (1.0 / B_WIDTH)
    inv = lax.rsqrt(var + EPS)
    r = lax.broadcasted_iota(jnp.int32, (B_CHUNK, B_CHUNK), 0)
    c = lax.broadcasted_iota(jnp.int32, (B_CHUNK, B_CHUNK), 1)
    outs = []
    for g in range(B_GROUPS):
        vn = (cen[g] * inv * ln_g[g] + ln_b[g]).astype(BF16)
        wm = jnp.where(c <= r, ws[g], 0.0).astype(BF16)
        outs.append(u[g] * (_dot(wm, vn) + bias[g]))
    return outs


def _lane_groups(ref, rows=slice(None)):
    return [ref[rows, g * LANES:(g + 1) * LANES] for g in range(B_GROUPS)]


def _gmlp_fwd(proj, ln_g, ln_b, ws, bias_t, name):
    t = proj.shape[0]
    tm = GMLP_ROWS

    def body(u_ref, v_ref, lg_ref, lb_ref, ws_ref, bt_ref, o_ref):
        for ch in range(tm // B_CHUNK):
            rows = slice(ch * B_CHUNK, (ch + 1) * B_CHUNK)
            outs = _gmlp_chunk(_lane_groups(u_ref, rows), _lane_groups(v_ref, rows), _lane_groups(lg_ref),
                               _lane_groups(lb_ref), [ws_ref[g] for g in range(B_GROUPS)],
                               [bt_ref[:, g:g + 1] for g in range(B_GROUPS)])
            for g in range(B_GROUPS):
                o_ref[rows, g * LANES:(g + 1) * LANES] = outs[g].astype(BF16)

    vec = pl.BlockSpec((1, B_WIDTH), lambda i: (0, 0))
    return pl.pallas_call(
        body, name=name, grid=(t // tm,),
        in_specs=[pl.BlockSpec((tm, B_WIDTH), lambda i: (i, 4)), pl.BlockSpec((tm, B_WIDTH), lambda i: (i, 5)), vec, vec,
                  pl.BlockSpec((B_GROUPS, B_CHUNK, B_CHUNK), lambda i: (0, 0, 0)),
                  pl.BlockSpec((B_CHUNK, B_GROUPS), lambda i: (0, 0))],
        out_specs=pl.BlockSpec((tm, B_WIDTH), lambda i: (i, 0)),
        out_shape=jax.ShapeDtypeStruct((t, B_WIDTH), BF16),
        compiler_params=_params(("parallel",)),
    )(proj, proj, ln_g, ln_b, ws, bias_t)


def _gmlp_bwd(proj, dcat, ln_g, ln_b, ws, bias_t, name):
    t = proj.shape[0]
    tm = GMLP_ROWS

    def body(u_ref, v_ref, do_ref, lg_ref, lb_ref, ws_ref, bt_ref, duv_ref, dlg_ref, dlb_ref, dws_ref, dbt_ref):
        @pl.when(pl.program_id(0) == 0)
        def _():
            dlg_ref[...] = jnp.zeros_like(dlg_ref)
            dlb_ref[...] = jnp.zeros_like(dlb_ref)
            dws_ref[...] = jnp.zeros_like(dws_ref)
            dbt_ref[...] = jnp.zeros_like(dbt_ref)

        for ch in range(tm // B_CHUNK):
            rows = slice(ch * B_CHUNK, (ch + 1) * B_CHUNK)
            _, vjp = jax.vjp(
                _gmlp_chunk, _lane_groups(u_ref, rows), _lane_groups(v_ref, rows), _lane_groups(lg_ref),
                _lane_groups(lb_ref), [ws_ref[g] for g in range(B_GROUPS)],
                [bt_ref[:, g:g + 1] for g in range(B_GROUPS)])
            du, dv, dlg, dlb, dw, dbt = vjp(_lane_groups(do_ref, rows))
            for g in range(B_GROUPS):
                lanes = slice(g * LANES, (g + 1) * LANES)
                duv_ref[rows, lanes] = du[g].astype(BF16)
                duv_ref[rows, B_WIDTH + g * LANES:B_WIDTH + (g + 1) * LANES] = dv[g].astype(BF16)
                dlg_ref[0:1, lanes] += dlg[g]
                dlb_ref[0:1, lanes] += dlb[g]
                dws_ref[g] += dw[g]
                dbt_ref[:, g:g + 1] += dbt[g]

    vec = pl.BlockSpec((1, B_WIDTH), lambda i: (0, 0))
    acc8 = pl.BlockSpec((SUBLANES, B_WIDTH), lambda i: (0, 0))
    ws_spec = pl.BlockSpec((B_GROUPS, B_CHUNK, B_CHUNK), lambda i: (0, 0, 0))
    bt_spec = pl.BlockSpec((B_CHUNK, B_GROUPS), lambda i: (0, 0))
    return pl.pallas_call(
        body, name=name, grid=(t // tm,),
        in_specs=[pl.BlockSpec((tm, B_WIDTH), lambda i: (i, 4)), pl.BlockSpec((tm, B_WIDTH), lambda i: (i, 5)),
                  pl.BlockSpec((tm, B_WIDTH), lambda i: (i, 1)), vec, vec, ws_spec, bt_spec],
        out_specs=[pl.BlockSpec((tm, 2 * B_WIDTH), lambda i: (i, 0)), acc8, acc8, ws_spec, bt_spec],
        out_shape=[jax.ShapeDtypeStruct((t, 2 * B_WIDTH), BF16), jax.ShapeDtypeStruct((SUBLANES, B_WIDTH), F32),
                   jax.ShapeDtypeStruct((SUBLANES, B_WIDTH), F32),
                   jax.ShapeDtypeStruct((B_GROUPS, B_CHUNK, B_CHUNK), F32),
                   jax.ShapeDtypeStruct((B_CHUNK, B_GROUPS), F32)],
        compiler_params=_params(("arbitrary",)),
    )(proj, proj, dcat, ln_g, ln_b, ws, bias_t)


QK_SCALE = 1.0 / math.sqrt(C_HEAD_DIM)
ATTN_UNROLL = 4
LANE_GROUPS = D_MODEL // LANES
Q_BLOCKS = SEQ // C_BLOCK


def _attn_window(i, d):
    sub_blocks = Q_BLOCKS // d
    q0 = pl.multiple_of(i * C_BLOCK, C_BLOCK)
    k0 = pl.multiple_of(jnp.maximum(i - 1, 0) * C_BLOCK, C_BLOCK)
    key = k0 + lax.broadcasted_iota(jnp.int32, (C_BLOCK, 2 * C_BLOCK), 1)
    dist = (q0 + lax.broadcasted_iota(jnp.int32, (C_BLOCK, 2 * C_BLOCK), 0)) - key
    own_subsequence = (key >= q0) | (i % sub_blocks > 0)
    return pl.ds(q0, C_BLOCK), pl.ds(k0, 2 * C_BLOCK), (dist >= 0) & (dist <= C_BLOCK) & own_subsequence


def _head_masks():
    lane = lax.broadcasted_iota(jnp.int32, (C_BLOCK, LANES), 1)
    return [lane < C_HEAD_DIM, lane >= C_HEAD_DIM]


def _flat_spec(col_of):
    return pl.BlockSpec((1, SEQ, LANES), lambda b, g: (b, 0, col_of(g)))


def _attn_branch_fwd(qkv, name):
    n_seq, d, l, _ = qkv.shape
    flat = qkv.reshape(n_seq, SEQ, ODD_IN)

    def body(q_ref, k_ref, v_ref, o_ref, m_ref, l_ref):
        heads = _head_masks()

        def block(i, carry):
            rows, keys, mask = _attn_window(i, d)
            q, k, v = q_ref[0, rows, :], k_ref[0, keys, :], v_ref[0, keys, :]
            res = []
            for hm in heads:
                s = jnp.where(mask, _dot_nt(jnp.where(hm, q, 0), k) * QK_SCALE, NEG)
                m = jnp.max(s, axis=-1, keepdims=True)
                p = jnp.exp(s - m)
                res.append((_dot(p.astype(BF16), v), m, jnp.sum(p, axis=-1, keepdims=True)))
            o_ref[0, rows, :] = jnp.where(heads[0], res[0][0], res[1][0])
            m_ref[0, rows, :] = jnp.where(heads[0], res[0][1], res[1][1])
            l_ref[0, rows, :] = jnp.where(heads[0], res[0][2], res[1][2])
            return carry

        lax.fori_loop(0, Q_BLOCKS, block, 0, unroll=ATTN_UNROLL)

    outs = pl.pallas_call(
        body, name=name, grid=(n_seq, LANE_GROUPS),
        in_specs=[_flat_spec(lambda g: g), _flat_spec(lambda g: LANE_GROUPS + g),
                  _flat_spec(lambda g: 2 * LANE_GROUPS + g)],
        out_specs=[_flat_spec(lambda g: g)] * 3,
        out_shape=[jax.ShapeDtypeStruct((n_seq, SEQ, D_MODEL), F32)] * 3,
        compiler_params=_params(("parallel", "parallel")),
    )(flat, flat, flat)
    return [o.reshape(n_seq, d, l, D_MODEL) for o in outs]


def _attn_merge(branches, name):
    n_seq = branches[0][0].shape[0]
    t = n_seq * SEQ
    tm = MERGE_TILE

    def body(*refs):
        ins = refs[:9]
        o_ref, ob_ref, lse1_ref, lse4_ref, lse16_ref = refs[9:14]
        nat = refs[14:]
        for b, d in enumerate(C_DILATIONS[1:]):
            for k in range(3):
                _load_dilated(ins[3 + 3 * b + k], d, nat[3 * b + k])
        for p in range(LANE_GROUPS):
            lanes = slice(p * LANES, (p + 1) * LANES)
            os_ = [ins[0][0, 0, :, lanes], nat[0][p], nat[3][p]]
            ms = [ins[1][0, 0, :, lanes], nat[1][p], nat[4][p]]
            ls = [ins[2][0, 0, :, lanes], nat[2][p], nat[5][p]]
            m_all = jnp.maximum(jnp.maximum(ms[0], ms[1]), ms[2])
            ws = [jnp.exp(ms[b] - m_all) for b in range(3)]
            total = ws[0] * ls[0] + ws[1] * ls[1] + ws[2] * ls[2]
            o = (ws[0] * os_[0] + ws[1] * os_[1] + ws[2] * os_[2]) / total
            o_ref[:, lanes] = o
            ob_ref[:, lanes] = o.astype(BF16)
            nat[0][p] = m_all + jnp.log(total)
        _store_dilated(nat[0], (lse1_ref, lse4_ref, lse16_ref), F32)

    row = pl.BlockSpec((tm, D_MODEL), lambda i: (i, 0))
    flat = [a for br in branches for a in br]
    in_specs = []
    for spec in _dilated_specs(tm, D_MODEL, lambda: 0):
        in_specs += [spec] * 3
    return pl.pallas_call(
        body, name=name, grid=(t // tm,), in_specs=in_specs,
        out_specs=[row, row] + _dilated_specs(tm, D_MODEL, lambda: 0),
        out_shape=[jax.ShapeDtypeStruct((t, D_MODEL), F32), jax.ShapeDtypeStruct((t, D_MODEL), BF16)]
        + _dilated_shapes(n_seq, D_MODEL, F32),
        scratch_shapes=[pltpu.VMEM((LANE_GROUPS, tm, LANES), F32)] * 6,
        compiler_params=_params(("parallel",)),
    )(*flat)


def _attn_branch_bwd(qkv, dout, lse, delta, name):
    n_seq, d, l, _ = qkv.shape
    flat = lambda a: a.reshape(n_seq, SEQ, a.shape[-1])

    def body(q_ref, k_ref, v_ref, do_ref, lse_ref, dl_ref, dq_ref, dk_ref, dv_ref):
        heads = _head_masks()
        dk_ref[...] = jnp.zeros_like(dk_ref)
        dv_ref[...] = jnp.zeros_like(dv_ref)

        def block(i, carry):
            rows, keys, mask = _attn_window(i, d)
            q, do = q_ref[0, rows, :], do_ref[0, rows, :]
            k, v = k_ref[0, keys, :], v_ref[0, keys, :]
            lse_b, dl_b = lse_ref[0, rows, :], dl_ref[0, rows, :]
            dq, dk, dv = [], None, None
            for hh, hm in enumerate(heads):
                col = slice(hh * C_HEAD_DIM, hh * C_HEAD_DIM + 1)
                qh, doh = jnp.where(hm, q, 0), jnp.where(hm, do, 0)
                s = jnp.where(mask, _dot_nt(qh, k) * QK_SCALE, NEG)
                p = jnp.exp(s - lse_b[:, col])
                ds = (p * (_dot_nt(doh, v) - dl_b[:, col]) * QK_SCALE).astype(BF16)
                dq.append(_dot(ds, k))
                dk_h, dv_h = _dot_tn(ds, qh), _dot_tn(p.astype(BF16), doh)
                dk = dk_h if dk is None else dk + dk_h
                dv = dv_h if dv is None else dv + dv_h
            dq_ref[0, rows, :] = jnp.where(heads[0], dq[0], dq[1])
            dk_ref[0, keys, :] += dk
            dv_ref[0, keys, :] += dv
            return carry

        lax.fori_loop(0, Q_BLOCKS, block, 0, unroll=ATTN_UNROLL)

    act = _flat_spec(lambda g: g)
    outs = pl.pallas_call(
        body, name=name, grid=(n_seq, LANE_GROUPS),
        in_specs=[_flat_spec(lambda g: g), _flat_spec(lambda g: LANE_GROUPS + g),
                  _flat_spec(lambda g: 2 * LANE_GROUPS + g), act, act, act],
        out_specs=[act] * 3,
        out_shape=[jax.ShapeDtypeStruct((n_seq, SEQ, D_MODEL), F32)] * 3,
        compiler_params=_params(("parallel", "parallel")),
    )(flat(qkv), flat(qkv), flat(qkv), flat(dout), flat(lse), flat(delta))
    return [o.reshape(n_seq, d, l, D_MODEL) for o in outs]


def _attn_combine_bwd(grads, rope, name):
    n_seq = grads[0][0].shape[0]
    t = n_seq * SEQ
    tm = MERGE_TILE

    def body(*refs):
        c_ref, s_ref, o_ref, nat4_ref, nat16_ref = refs[9:]
        for sec in range(3):
            _load_dilated(refs[3 + sec], 4, nat4_ref)
            _load_dilated(refs[6 + sec], 16, nat16_ref)
            for p in range(LANE_GROUPS):
                blk = refs[sec][0, 0, :, p * LANES:(p + 1) * LANES] + nat4_ref[p] + nat16_ref[p]
                if sec < 2:
                    blk = blk * c_ref[...] - _swap_halves(blk) * s_ref[...]
                o_ref[:, sec * D_MODEL + p * LANES:sec * D_MODEL + (p + 1) * LANES] = blk.astype(BF16)

    tab = pl.BlockSpec((tm, LANES), lambda i: (i, 0))
    flat = [a for br in grads for a in br]
    in_specs = []
    for spec in _dilated_specs(tm, D_MODEL, lambda: 0):
        in_specs += [spec] * 3
    return pl.pallas_call(
        body, name=name, grid=(t // tm,), in_specs=in_specs + [tab, tab],
        out_specs=pl.BlockSpec((tm, ODD_IN), lambda i: (i, 0)),
        out_shape=jax.ShapeDtypeStruct((t, ODD_IN), BF16),
        scratch_shapes=[pltpu.VMEM((LANE_GROUPS, tm, LANES), F32)] * 2,
        compiler_params=_params(("parallel",)),
    )(*flat, *rope)


def _loss_grad(y, target, name):
    t = y.shape[0]
    tm = ROW_TILE

    def body(y_ref, t_ref, d_ref, l_ref):
        diff = y_ref[...] - t_ref[...]
        d_ref[...] = diff * (1.0 / D_MODEL)
        _acc_rows8(l_ref, _rows8(diff * diff) * (0.5 / D_MODEL), pl.program_id(0) == 0)

    row = pl.BlockSpec((tm, D_MODEL), lambda i: (i, 0))
    return pl.pallas_call(
        body, name=name, grid=(t // tm,), in_specs=[row, row],
        out_specs=[row, pl.BlockSpec((SUBLANES, D_MODEL), lambda i: (0, 0))],
        out_shape=[jax.ShapeDtypeStruct((t, D_MODEL), F32), jax.ShapeDtypeStruct((SUBLANES, D_MODEL), F32)],
        compiler_params=_params(("arbitrary",)),
    )(y, target)


def _adamw(w, g, m, v):
    m = ADAM_B1 * m + (1.0 - ADAM_B1) * g
    v = ADAM_B2 * v + (1.0 - ADAM_B2) * jnp.square(g)
    m_hat = m / (1.0 - ADAM_B1 ** ADAM_STEP)
    v_hat = v / (1.0 - ADAM_B2 ** ADAM_STEP)
    delta = -ADAM_LR * (m_hat / (jnp.sqrt(v_hat) + ADAM_EPS) + ADAM_WD * w)
    return delta, m, v


def _adamw_sharded(parts, w, m, v, name):
    n_layers, rows, cols = w.shape
    tr = min(rows, 256)

    def body(*refs):
        p_refs = refs[:n_layers]
        w_ref, m_ref, v_ref, g_ref, d_ref, mo_ref, vo_ref = refs[n_layers:]
        layer = pl.program_id(0)
        g = None
        for l, p_ref in enumerate(p_refs):
            g_l = p_ref[0].astype(F32)
            for s in range(1, N_DEV):
                g_l = g_l + p_ref[s].astype(F32)
            g = g_l if g is None else jnp.where(layer == l, g_l, g)
        delta, mn, vn = _adamw(w_ref[0], g, m_ref[0], v_ref[0])
        g_ref[0] = g
        d_ref[0] = delta
        mo_ref[0] = mn
        vo_ref[0] = vn

    def part_spec(l):
        return pl.BlockSpec((N_DEV, tr, cols), lambda a, i: (0, jnp.where(a == l, i, 0), 0))

    row = pl.BlockSpec((1, tr, cols), lambda a, i: (a, i, 0))
    return pl.pallas_call(
        body, name=name, grid=(n_layers, rows // tr),
        in_specs=[part_spec(l) for l in range(n_layers)] + [row, row, row],
        out_specs=[row] * 4, out_shape=[jax.ShapeDtypeStruct(w.shape, F32)] * 4,
        compiler_params=_params(("arbitrary", "arbitrary")),
    )(*parts, w, m, v)


def _small_update(gathered, weights, moments_m, moments_v, lb_index, name):
    n = len(weights)

    def total(ref):
        acc = ref[0]
        for s in range(1, N_DEV):
            acc = acc + ref[s]
        return acc

    def body(*refs):
        g_refs = refs[:n + 1]
        w_refs, m_refs, v_refs = refs[n + 1:2 * n + 1], refs[2 * n + 1:3 * n + 1], refs[3 * n + 1:4 * n + 1]
        outs = refs[4 * n + 1:]
        loss_rows = total(g_refs[n])
        outs[0][...] = jnp.sum(jnp.sum(loss_rows, axis=1, keepdims=True), axis=0, keepdims=True)
        for k in range(n):
            part = total(g_refs[k])
            if k == lb_index:
                dlb = jnp.sum(part, axis=0, keepdims=True)
                tab = w_refs[k][...]
                e = jnp.exp(tab - jnp.max(tab, axis=0, keepdims=True))
                p = e / jnp.sum(e, axis=0, keepdims=True)
                first = lax.broadcasted_iota(jnp.int32, p.shape, 0) == 0
                grads = [(slice(None), p * (jnp.where(first, dlb, 0.0) - p[0:1, :] * dlb))]
            elif part.shape == w_refs[k].shape:
                grads = [(slice(None), part)]
            else:
                grads = [(slice(l, l + 1), jnp.sum(part[l * SUBLANES:(l + 1) * SUBLANES], axis=0, keepdims=True))
                         for l in range(w_refs[k].shape[0])]
            for rows, g in grads:
                delta, mn, vn = _adamw(w_refs[k][rows], g, m_refs[k][rows], v_refs[k][rows])
                outs[1 + 4 * k][rows] = g
                outs[2 + 4 * k][rows] = delta
                outs[3 + 4 * k][rows] = mn
                outs[4 + 4 * k][rows] = vn

    vmem = pl.BlockSpec(memory_space=pltpu.VMEM)
    out_shape = [jax.ShapeDtypeStruct((1, 1), F32)]
    for w in weights:
        out_shape += [jax.ShapeDtypeStruct(w.shape, F32)] * 4
    args = list(gathered) + list(weights) + list(moments_m) + list(moments_v)
    return pl.pallas_call(
        body, name=name, in_specs=[vmem] * len(args), out_specs=[vmem] * len(out_shape), out_shape=out_shape,
        compiler_params=pltpu.CompilerParams(vmem_limit_bytes=VMEM_LIMIT),
    )(*args)


def kernel(x, positions, norm_mix_pre, norm_mix_post, norm_ffn_pre, norm_ffn_post, w_in_even, lb_table, a_norm, b_ln_g, b_ln_b, b_ws, b_bias, w_out_even, w_in_odd, w_out_odd, w_ff1, w_ff2, loss_target, m_norm_mix_pre, m_norm_mix_post, m_norm_ffn_pre, m_norm_ffn_post, m_w_in_even, m_lb_table, m_a_norm, m_b_ln_g, m_b_ln_b, m_b_ws, m_b_bias, m_w_out_even, m_w_in_odd, m_w_out_odd, m_w_ff1, m_w_ff2, v_norm_mix_pre, v_norm_mix_post, v_norm_ffn_pre, v_norm_ffn_post, v_w_in_even, v_lb_table, v_a_norm, v_b_ln_g, v_b_ln_b, v_b_ws, v_b_bias, v_w_out_even, v_w_in_odd, v_w_out_odd, v_w_ff1, v_w_ff2):
    n_seq = x.shape[0]
    t = n_seq * SEQ
    x0 = x.reshape(t, D_MODEL)
    target = loss_target.reshape(t, D_MODEL)

    me = _my_slot().astype(jnp.int32).reshape(1)

    order = ["in_e", "out_e", "ff1_0", "ff2_0", "in_o", "out_o", "ff1_1", "ff2_1"]
    shards = dict(in_e=w_in_even[0], out_e=w_out_even[0], in_o=w_in_odd[0], out_o=w_out_odd[0],
                  ff1_0=w_ff1[0], ff1_1=w_ff1[1], ff2_0=w_ff2[0], ff2_1=w_ff2[1])
    lands = [_place_own(shards[k], me, "place_" + k, False) for k in order]
    g_send, g_recv, lands, _, g_token = _exchange_start(lands, [None] * len(order), "gather_start")

    def get_w(keys, after):
        ks = [order.index(k) for k in keys]
        return _exchange_wait([lands[k] for k in ks], [None] * len(ks), [g_send[k] for k in ks],
                              [g_recv[k] for k in ks], after, "gather_wait_" + keys[0])

    sent = {}

    def put_g(group, blocks):
        keys = list(blocks)
        own = [_place_own(blocks[k], me, "own_" + k, True) for k in keys]
        send_sems, recv_sems, own, srcs, token = _exchange_start(own, [blocks[k] for k in keys], "scatter_start_" + group)
        sent[group] = (keys, own, srcs, send_sems, recv_sems)
        return token

    rope = _rope_tables(positions)
    bias_t = b_bias[0].T
    grads = _local_step(x0, target, rope, norm_mix_pre, norm_mix_post, norm_ffn_pre, norm_ffn_post, lb_table,
                        a_norm, b_ln_g, b_ln_b, b_ws[0], bias_t, get_w, put_g, g_token)
    (dx0, loss_part, dg_mix_pre, dg_mix_post, dg_ffn_pre, dg_ffn_post, d_lb, d_a_norm, d_ln_g, d_ln_b, d_ws,
     d_bias_t) = grads

    recv = {}
    for group, (keys, own, srcs, send_sems, recv_sems) in sent.items():
        done = _exchange_wait(own, srcs, send_sems, recv_sems, dx0, "scatter_wait_" + group)
        recv.update(zip(keys, done))
    big = [("w_in_even", ["in_e"], w_in_even, m_w_in_even, v_w_in_even),
           ("w_out_even", ["out_e"], w_out_even, m_w_out_even, v_w_out_even),
           ("w_in_odd", ["in_o"], w_in_odd, m_w_in_odd, v_w_in_odd),
           ("w_out_odd", ["out_o"], w_out_odd, m_w_out_odd, v_w_out_odd),
           ("w_ff1", ["ff1_0", "ff1_1"], w_ff1, m_w_ff1, v_w_ff1), ("w_ff2", ["ff2_0", "ff2_1"], w_ff2, m_w_ff2, v_w_ff2)]
    big_out = [_adamw_sharded([recv[k] for k in keys], w, m, v, "adamw_" + nm) for nm, keys, w, m, v in big]

    small_parts = [dg_mix_pre, dg_mix_post, dg_ffn_pre, dg_ffn_post,
                   d_lb, d_a_norm, d_ln_g, d_ln_b, d_ws, d_bias_t, loss_part]
    gathered = _exchange(small_parts, True, "gather_small")
    small_w = [norm_mix_pre, norm_mix_post, norm_ffn_pre, norm_ffn_post, lb_table, a_norm, b_ln_g, b_ln_b,
               b_ws[0], bias_t]
    small_m = [m_norm_mix_pre, m_norm_mix_post, m_norm_ffn_pre, m_norm_ffn_post, m_lb_table, m_a_norm, m_b_ln_g,
               m_b_ln_b, m_b_ws[0], m_b_bias[0].T]
    small_v = [v_norm_mix_pre, v_norm_mix_post, v_norm_ffn_pre, v_norm_ffn_post, v_lb_table, v_a_norm, v_b_ln_g,
               v_b_ln_b, v_b_ws[0], v_b_bias[0].T]
    small_out = _small_update(gathered, small_w, small_m, small_v, 4, "small_update")
    loss = small_out[0].reshape(())
    small = [small_out[1 + 4 * k:5 + 4 * k] for k in range(len(small_w))]
    small[8] = [a[None] for a in small[8]]
    small[9] = [a.T[None] for a in small[9]]

    per_weight = small[0:4] + [big_out[0]] + small[4:10] + big_out[1:6]
    grad_x = dx0.reshape(x.shape)
    out = [loss, grad_x]
    for kind in range(4):
        out += [p[kind] for p in per_weight]
    return tuple(out)


def _local_step(x0, target, rope, norm_mix_pre, norm_mix_post, norm_ffn_pre, norm_ffn_post, lb_table, a_norm,
                b_ln_g, b_ln_b, ws, bias_t, get_w, put_g, token):
    def gain(a, l, tok):
        return a[l:l + 1] if tok is None else a[l:l + 1] + tok[0:1, 0:1]

    full = lambda a: a.reshape(-1, D_MODEL)
    owners = lambda a: a.reshape((N_DEV, -1) + a.shape[1:])

    (g_in_e,) = get_w(["in_e"], token)
    proj, h_mix0 = _norm_inproj(x0, gain(norm_mix_pre, 0, token), g_in_e, "inproj_even")
    oa, states = _hgrn2_fwd(proj, lb_table, a_norm, "hgrn2_fwd")
    ob = _gmlp_fwd(proj, b_ln_g, b_ln_b, ws, bias_t, "gmlp_fwd")
    w_out_e = full(get_w(["out_e"], ob)[0])
    x1, mix0 = _outproj([oa, ob], w_out_e, x0, gain(norm_mix_post, 0, None), "outproj_even")
    w1_0, w2_0 = get_w(["ff1_0", "ff2_0"], x1)
    w2_0 = full(w2_0)
    x2, y0, h_ffn0 = _ffn_fwd(x1, gain(norm_ffn_pre, 0, None), w1_0, w2_0, gain(norm_ffn_post, 0, None), "ffn_fwd_0")
    (g_in_o,) = get_w(["in_o"], x2)
    *qkv, h_mix1 = _norm_inproj_rope(x2, gain(norm_mix_pre, 1, None), g_in_o, rope, "inproj_odd")
    branches = [_attn_branch_fwd(a, "attn_fwd_d%d" % d) for a, d in zip(qkv, C_DILATIONS)]
    attn, attn_b, *lse = _attn_merge(branches, "attn_merge")
    w_out_o = full(get_w(["out_o"], attn_b)[0])
    x3, mix1 = _outproj([attn_b], w_out_o, x2, gain(norm_mix_post, 1, None), "outproj_odd")
    w1_1, w2_1 = get_w(["ff1_1", "ff2_1"], x3)
    w2_1 = full(w2_1)
    x4, y1, h_ffn1 = _ffn_fwd(x3, gain(norm_ffn_pre, 1, None), w1_1, w2_1, gain(norm_ffn_post, 1, None), "ffn_fwd_1")

    dx4, loss_part = _loss_grad(x4, target, "loss_grad")

    dx3, dy1, r1, da1, dg_ffn_pre1, dg_ffn_post1 = _ffn_bwd(
        dx4, x3, y1, h_ffn1, gain(norm_ffn_pre, 1, None), w1_1, w2_1, gain(norm_ffn_post, 1, None), "ffn_bwd_1")
    gw_ff1_1 = _grad_w(h_ffn1, da1, True, "grad_w_ff1_1")
    gw_ff2_1 = _grad_w(r1, dy1, False, "grad_w_ff2_1")
    tok = put_g("ffn1", dict(ff1_1=gw_ff1_1, ff2_1=owners(gw_ff2_1)))
    *dattn, dz1, dg_mix_post1 = _outproj_bwd_attn(dx3, mix1, gain(norm_mix_post, 1, tok), w_out_o, attn,
                                                  "outproj_bwd_odd")
    gw_out_o = _grad_w(attn_b, dz1, False, "grad_w_out_odd")
    grads_c = [_attn_branch_bwd(qkv[b], dattn[b], lse[b], dattn[3 + b], "attn_bwd_d%d" % d)
               for b, d in enumerate(C_DILATIONS)]
    dqkv = _attn_combine_bwd(grads_c, rope, "attn_combine_bwd")
    gw_in_o = _grad_w(h_mix1, dqkv, True, "grad_w_in_odd")
    tok = put_g("mix1", dict(out_o=owners(gw_out_o), in_o=gw_in_o))
    dx2, dg_mix_pre1 = _inproj_bwd(dqkv, g_in_o, dx3, x2, gain(norm_mix_pre, 1, tok), "inproj_bwd_odd")

    dx1, dy0, r0, da0, dg_ffn_pre0, dg_ffn_post0 = _ffn_bwd(
        dx2, x1, y0, h_ffn0, gain(norm_ffn_pre, 0, None), w1_0, w2_0, gain(norm_ffn_post, 0, None), "ffn_bwd_0")
    gw_ff1_0 = _grad_w(h_ffn0, da0, True, "grad_w_ff1_0")
    gw_ff2_0 = _grad_w(r0, dy0, False, "grad_w_ff2_0")
    tok = put_g("ffn0", dict(ff1_0=gw_ff1_0, ff2_0=owners(gw_ff2_0)))
    dcat, dz0, dg_mix_post0 = _outproj_bwd(dx1, mix0, gain(norm_mix_post, 0, tok), w_out_e, "outproj_bwd_even")
    gw_out_e = jnp.concatenate([_grad_w(oa, dz0, False, "grad_w_out_even_a"),
                                _grad_w(ob, dz0, False, "grad_w_out_even_b")], axis=0)
    dqfig, d_lb, d_a_norm = _hgrn2_bwd(proj, dcat, states, lb_table, a_norm, "hgrn2_bwd")
    duv, d_ln_g, d_ln_b, d_ws, d_bias_t = _gmlp_bwd(proj, dcat, b_ln_g, b_ln_b, ws, bias_t, "gmlp_bwd")
    dproj = jnp.concatenate([dqfig, duv], axis=1)
    gw_in_e = _grad_w(h_mix0, dproj, True, "grad_w_in_even")
    tok = put_g("mix0", dict(out_e=owners(gw_out_e), in_e=gw_in_e))
    dx0, dg_mix_pre0 = _inproj_bwd(dproj, g_in_e, dx1, x0, gain(norm_mix_pre, 0, tok), "inproj_bwd_even")

    layers = lambda a, b: jnp.concatenate([a, b], axis=0)
    return (dx0, loss_part, layers(dg_mix_pre0, dg_mix_pre1), layers(dg_mix_post0, dg_mix_post1),
            layers(dg_ffn_pre0, dg_ffn_pre1), layers(dg_ffn_post0, dg_ffn_post1),
            d_lb, d_a_norm, d_ln_g, d_ln_b, d_ws, d_bias_t)
```

```python
import functools
import math

import jax
import jax.numpy as jnp
from jax import lax
from jax.experimental import pallas as pl
from jax.experimental.pallas import tpu as pltpu

F32 = jnp.float32
BF16 = jnp.bfloat16
MESH = pl.DeviceIdType.MESH

N_DEV = 8
D_MODEL = 1024
SEQ = 2048
EPS = 1e-6
A_WIDTH = 512
A_HEADS = 4
HEAD_A = 128
B_WIDTH = 512
B_GROUPS = 4
B_CHUNK = 128
C_HEADS = 16
C_HEAD_DIM = 64
C_ROT_HALF = 8
ROPE_THETA = 500000.0
C_DILATIONS = (1, 4, 16)
C_BLOCK = 128
D_FF = 4096
EVEN_IN = 3072
ODD_IN = 3072

ADAM_LR = 0.001
ADAM_B1 = 0.9
ADAM_B2 = 0.999
ADAM_EPS = 1e-08
ADAM_WD = 0.01
ADAM_STEP = 10

LANES = 128
SUBLANES = 8
ROW_TILE = 512
PROJ_TILE = 1024
PROJ_COLS = 768
MERGE_TILE = 256
SUB_CHUNK = 16
HGRN_BLOCK = 256
NEG = -1e30
VMEM_LIMIT = 56 * 1024 * 1024


def _params(sem):
    return pltpu.CompilerParams(dimension_semantics=sem, vmem_limit_bytes=VMEM_LIMIT)


def _dot(a, b):
    return jnp.dot(a, b, preferred_element_type=F32)


def _dot_nt(a, b):
    return lax.dot_general(a, b, (((1,), (1,)), ((), ())), preferred_element_type=F32)


def _dot_tn(a, b):
    return lax.dot_general(a, b, (((0,), (0,)), ((), ())), preferred_element_type=F32)


def _rms(x, g):
    r = lax.rsqrt(jnp.mean(x * x, axis=-1, keepdims=True) + EPS)
    return x * r * g


def _rms_bwd(x, g, dy):
    r = lax.rsqrt(jnp.mean(x * x, axis=-1, keepdims=True) + EPS)
    dyg = dy * g
    dx = r * dyg - x * (r * r * r) * jnp.mean(x * dyg, axis=-1, keepdims=True)
    return dx, dy * x * r


def _rows8(v):
    return v.reshape(v.shape[0] // SUBLANES, SUBLANES, v.shape[1]).sum(axis=0)


def _sigmoid(x):
    return 1.0 / (1.0 + jnp.exp(-x))


def _gelu(x):
    return 0.5 * x * (1.0 + jnp.tanh(math.sqrt(2.0 / math.pi) * (x + 0.044715 * (x * x * x))))


def _acc_rows8(ref, val, first):
    @pl.when(first)
    def _():
        ref[...] = val

    @pl.when(jnp.logical_not(first))
    def _():
        ref[...] += val


def _my_slot():
    return 4 * lax.axis_index("x") + 2 * lax.axis_index("y") + lax.axis_index("c")


def _peer(r):
    x, y, c = lax.axis_index("x"), lax.axis_index("y"), lax.axis_index("c")
    px = 1 - x if (r >> 2) & 1 else x
    py = 1 - y if (r >> 1) & 1 else y
    pc = 1 - c if r & 1 else c
    return (px, py, pc), 4 * px + 2 * py + pc


def _exchange(arrays, gather, name):
    n = len(arrays)
    if gather:
        out_shape = [jax.ShapeDtypeStruct((N_DEV,) + a.shape, a.dtype) for a in arrays]
    else:
        out_shape = [jax.ShapeDtypeStruct(a.shape, a.dtype) for a in arrays]

    def body(*refs):
        ins, outs = refs[:n], refs[n:2 * n]
        send_sems, recv_sems, local_sems = refs[2 * n:]
        me = _my_slot()
        local, remote = [], []
        for k in range(n):
            src = ins[k] if gather else ins[k].at[me]
            local.append(pltpu.make_async_copy(src, outs[k].at[me], local_sems.at[k]))
            for r in range(1, N_DEV):
                peer, slot = _peer(r)
                src = ins[k] if gather else ins[k].at[slot]
                remote.append((pltpu.make_async_remote_copy(
                    src_ref=src, dst_ref=outs[k].at[me], send_sem=send_sems.at[k, r - 1],
                    recv_sem=recv_sems.at[k, r - 1], device_id=peer, device_id_type=MESH), k, r, slot))
        for cp in local:
            cp.start()
        for cp, _, _, _ in remote:
            cp.start()
        for cp, k, r, slot in remote:
            pltpu.make_async_remote_copy(
                src_ref=outs[k].at[slot], dst_ref=outs[k].at[slot], send_sem=send_sems.at[k, r - 1],
                recv_sem=recv_sems.at[k, r - 1], device_id=_peer(r)[0], device_id_type=MESH).wait_recv()
        for cp, _, _, _ in remote:
            cp.wait_send()
        for cp in local:
            cp.wait()

    any_spec = pl.BlockSpec(memory_space=pl.ANY)
    return pl.pallas_call(
        body, name=name, out_shape=out_shape,
        in_specs=[any_spec] * n, out_specs=[any_spec] * n,
        scratch_shapes=[pltpu.SemaphoreType.DMA((n, N_DEV - 1)), pltpu.SemaphoreType.DMA((n, N_DEV - 1)),
                        pltpu.SemaphoreType.DMA((n,))],
        compiler_params=pltpu.CompilerParams(has_side_effects=True),
    )(*arrays)


HBM_SPEC = pl.BlockSpec(memory_space=pltpu.HBM)
SEM_SPEC = pl.BlockSpec(memory_space=pltpu.SEMAPHORE)
SPLIT_EFFECT = pltpu.SideEffectType.DATAFLOW_SIDE_EFFECTING


def _split_copies(land_ref, src_ref, send_sem, recv_sem):
    me = _my_slot()
    copies = []
    for r in range(1, N_DEV):
        peer, slot = _peer(r)
        src = land_ref.at[me] if src_ref is None else src_ref.at[slot]
        copies.append(pltpu.make_async_remote_copy(
            src_ref=src, dst_ref=land_ref.at[me], send_sem=send_sem, recv_sem=recv_sem,
            device_id=peer, device_id_type=MESH))
    return copies


def _exchange_start(lands, sources, name):
    n = len(lands)
    given = [s for s in sources if s is not None]
    arrays = list(lands) + given

    def body(*refs):
        land_refs, src_refs = refs[:n], list(refs[n:n + len(given)])
        sems = refs[len(arrays):len(arrays) + 2 * n]
        token = refs[-1]
        for k in range(n):
            src_ref = None if sources[k] is None else src_refs.pop(0)
            for copy in _split_copies(land_refs[k], src_ref, sems[k], sems[n + k]):
                copy.start()
        token[...] = jnp.zeros_like(token)

    outs = pl.pallas_call(
        body, name=name,
        out_shape=(pltpu.SemaphoreType.DMA(()),) * (2 * n) + tuple(pltpu.HBM(a.shape, a.dtype) for a in arrays)
        + (jax.ShapeDtypeStruct((SUBLANES, LANES), F32),),
        in_specs=[HBM_SPEC] * len(arrays),
        out_specs=(SEM_SPEC,) * (2 * n) + (HBM_SPEC,) * len(arrays) + (pl.BlockSpec(memory_space=pltpu.VMEM),),
        input_output_aliases={i: 2 * n + i for i in range(len(arrays))},
        compiler_params=pltpu.CompilerParams(has_side_effects=SPLIT_EFFECT),
    )(*[pltpu.with_memory_space_constraint(a, pltpu.HBM) for a in arrays])
    return list(outs[:n]), list(outs[n:2 * n]), list(outs[2 * n:3 * n]), list(outs[3 * n:-1]), outs[-1]


def _exchange_wait(lands, sources, send_sems, recv_sems, after, name):
    n = len(lands)
    given = [s for s in sources if s is not None]
    arrays = list(lands) + given

    def body(*refs):
        land_refs, src_refs = refs[:n], list(refs[n:n + len(given)])
        sems = refs[len(arrays):len(arrays) + 2 * n]
        for i in range(n):
            src_ref = None if sources[i] is None else src_refs.pop(0)
            copies = _split_copies(land_refs[i], src_ref, sems[i], sems[n + i])
            for copy in copies:
                copy.wait_recv()
            for copy in copies:
                copy.wait_send()

    outs = pl.pallas_call(
        body, name=name, out_shape=tuple(pltpu.HBM(a.shape, a.dtype) for a in arrays),
        in_specs=[HBM_SPEC] * len(arrays) + [SEM_SPEC] * (2 * n) + [pl.BlockSpec(memory_space=pl.ANY)],
        out_specs=(HBM_SPEC,) * len(arrays),
        input_output_aliases={i: i for i in range(len(arrays))},
        compiler_params=pltpu.CompilerParams(has_side_effects=SPLIT_EFFECT),
    )(*arrays, *send_sems, *recv_sems, after)
    return list(outs[:n])


def _place_own(a, me, name, own_block):
    shape = a.shape[1:] if own_block else a.shape
    cols = shape[-1]
    a3 = a.reshape((N_DEV if own_block else 1, -1, cols))
    rows = a3.shape[1]
    tr = min(rows, 512)

    def body(me_ref, a_ref, o_ref):
        o_ref[...] = a_ref[...].astype(BF16)

    grid_spec = pltpu.PrefetchScalarGridSpec(
        num_scalar_prefetch=1, grid=(rows // tr,),
        in_specs=[pl.BlockSpec((1, tr, cols), lambda i, me_ref: (me_ref[0] if own_block else 0, i, 0))],
        out_specs=pl.BlockSpec((1, tr, cols), lambda i, me_ref: (me_ref[0], i, 0)))
    out = pl.pallas_call(
        body, name=name, grid_spec=grid_spec, out_shape=jax.ShapeDtypeStruct((N_DEV, rows, cols), BF16),
        compiler_params=_params(("arbitrary",)),
    )(me, a3)
    return out.reshape((N_DEV,) + shape)


def _rope_tables(positions):
    inv = ROPE_THETA ** (-jnp.arange(C_ROT_HALF, dtype=F32) / C_ROT_HALF)
    ang = positions.reshape(-1)[:, None].astype(F32) * inv
    cos, sin = jnp.cos(ang), jnp.sin(ang)
    t = ang.shape[0]
    ones = jnp.ones((t, C_HEAD_DIM - 2 * C_ROT_HALF), F32)
    c_head = jnp.concatenate([cos, cos, ones], axis=1)
    s_head = jnp.concatenate([-sin, sin, 0.0 * ones], axis=1)
    return jnp.concatenate([c_head, c_head], axis=1), jnp.concatenate([s_head, s_head], axis=1)


def _swap_halves(x):
    lane = lax.broadcasted_iota(jnp.int32, x.shape, 1) % C_HEAD_DIM
    return jnp.where(lane < C_ROT_HALF, pltpu.roll(x, LANES - C_ROT_HALF, 1), pltpu.roll(x, C_ROT_HALF, 1))


def _norm_inproj(x, g, w, name):
    t = x.shape[0]
    n = w.shape[1]
    tm, tn = PROJ_TILE, PROJ_COLS

    def body(x_ref, g_ref, w_ref, o_ref, h_ref):
        @pl.when(pl.program_id(1) == 0)
        def _():
            h_ref[...] = _rms(x_ref[...], g_ref[...]).astype(BF16)

        o_ref[...] = _dot(h_ref[...], w_ref[...])

    return pl.pallas_call(
        body, name=name, grid=(t // tm, n // tn),
        in_specs=[pl.BlockSpec((tm, D_MODEL), lambda i, j: (i, 0)), pl.BlockSpec((1, D_MODEL), lambda i, j: (0, 0)),
                  pl.BlockSpec((D_MODEL, tn), lambda i, j: (0, j))],
        out_specs=[pl.BlockSpec((tm, tn), lambda i, j: (i, j)), pl.BlockSpec((tm, D_MODEL), lambda i, j: (i, 0))],
        out_shape=[jax.ShapeDtypeStruct((t, n), F32), jax.ShapeDtypeStruct((t, D_MODEL), BF16)],
        compiler_params=_params(("parallel", "arbitrary")),
    )(x, g, w)


def _dilated_specs(tm, width, col_of):
    per_seq = SEQ // tm
    specs = []
    for d in C_DILATIONS:
        specs.append(pl.BlockSpec(
            (1, d, tm // d, width), lambda i, *rest: (i // per_seq, 0, i % per_seq, col_of(*rest))))
    return specs


def _dilated_shapes(n_seq, cols, dtype):
    return [jax.ShapeDtypeStruct((n_seq, d, SEQ // d, cols), dtype) for d in C_DILATIONS]


def _store_dilated(src_ref, out_refs, dtype):
    groups, tm, _ = src_ref.shape
    for d, o_ref in zip(C_DILATIONS, out_refs):
        for r in range(d):
            rows = pl.ds(r, tm // d, stride=d) if d > 1 else slice(None)
            for p in range(groups):
                o_ref[0, r, :, p * LANES:(p + 1) * LANES] = src_ref.at[p][rows, :].astype(dtype)


def _load_dilated(in_ref, d, dst_ref):
    groups, tm, _ = dst_ref.shape
    for r in range(d):
        rows = pl.ds(r, tm // d, stride=d)
        for p in range(groups):
            dst_ref.at[p][rows, :] = in_ref[0, r, :, p * LANES:(p + 1) * LANES].astype(F32)


def _norm_inproj_rope(x, g, w, rope, name):
    t = x.shape[0]
    n = w.shape[1]
    tm, nb = PROJ_TILE, PROJ_COLS

    def body(x_ref, g_ref, w_ref, c_ref, s_ref, o1_ref, o4_ref, o16_ref, h_ref, tile_ref):
        j = pl.program_id(1)

        @pl.when(j == 0)
        def _():
            h_ref[...] = _rms(x_ref[...], g_ref[...]).astype(BF16)

        acc = _dot(h_ref[...], w_ref[...])
        for p in range(nb // LANES):
            blk = acc[:, p * LANES:(p + 1) * LANES]
            roped = blk * c_ref[...] + _swap_halves(blk) * s_ref[...]
            is_qk = (j * (nb // LANES) + p) < 2 * (D_MODEL // LANES)
            tile_ref[p] = jnp.where(is_qk, roped, blk)
        _store_dilated(tile_ref, (o1_ref, o4_ref, o16_ref), BF16)

    return pl.pallas_call(
        body, name=name, grid=(t // tm, n // nb),
        in_specs=[pl.BlockSpec((tm, D_MODEL), lambda i, j: (i, 0)), pl.BlockSpec((1, D_MODEL), lambda i, j: (0, 0)),
                  pl.BlockSpec((D_MODEL, nb), lambda i, j: (0, j)),
                  pl.BlockSpec((tm, LANES), lambda i, j: (i, 0)), pl.BlockSpec((tm, LANES), lambda i, j: (i, 0))],
        out_specs=_dilated_specs(tm, nb, lambda j: j) + [pl.BlockSpec((tm, D_MODEL), lambda i, j: (i, 0))],
        out_shape=_dilated_shapes(t // SEQ, n, BF16) + [jax.ShapeDtypeStruct((t, D_MODEL), BF16)],
        scratch_shapes=[pltpu.VMEM((nb // LANES, tm, LANES), F32)],
        compiler_params=_params(("parallel", "arbitrary")),
    )(x, g, w, *rope)


def _outproj(parts, w, x, g, name):
    t = x.shape[0]
    tm = ROW_TILE
    n = len(parts)
    widths = [p.shape[1] for p in parts]

    def body(*refs):
        p_refs = refs[:n]
        w_ref, x_ref, g_ref, xo_ref, mix_ref = refs[n:]
        mix = None
        off = 0
        for p_ref, wd in zip(p_refs, widths):
            term = _dot(p_ref[...].astype(BF16), w_ref[off:off + wd, :])
            mix = term if mix is None else mix + term
            off += wd
        mix_ref[...] = mix
        xo_ref[...] = x_ref[...] + _rms(mix, g_ref[...])

    row = lambda i: (i, 0)
    return pl.pallas_call(
        body, name=name, grid=(t // tm,),
        in_specs=[pl.BlockSpec((tm, wd), row) for wd in widths] + [
            pl.BlockSpec((sum(widths), D_MODEL), lambda i: (0, 0)),
            pl.BlockSpec((tm, D_MODEL), row), pl.BlockSpec((1, D_MODEL), lambda i: (0, 0))],
        out_specs=[pl.BlockSpec((tm, D_MODEL), row)] * 2,
        out_shape=[jax.ShapeDtypeStruct((t, D_MODEL), F32)] * 2,
        compiler_params=_params(("parallel",)),
    )(*parts, w, x, g)


def _outproj_bwd(dx, mix, g, w, name):
    t = dx.shape[0]
    tm = ROW_TILE
    k = w.shape[0]

    def body(dx_ref, mix_ref, g_ref, w_ref, dcat_ref, dz_ref, dg_ref):
        dz, dgr = _rms_bwd(mix_ref[...], g_ref[...], dx_ref[...])
        dzb = dz.astype(BF16)
        dz_ref[...] = dzb
        dcat_ref[...] = _dot_nt(dzb, w_ref[...])
        _acc_rows8(dg_ref, _rows8(dgr), pl.program_id(0) == 0)

    row = lambda i: (i, 0)
    return pl.pallas_call(
        body, name=name, grid=(t // tm,),
        in_specs=[pl.BlockSpec((tm, D_MODEL), row), pl.BlockSpec((tm, D_MODEL), row),
                  pl.BlockSpec((1, D_MODEL), lambda i: (0, 0)), pl.BlockSpec((k, D_MODEL), lambda i: (0, 0))],
        out_specs=[pl.BlockSpec((tm, k), row), pl.BlockSpec((tm, D_MODEL), row),
                   pl.BlockSpec((SUBLANES, D_MODEL), lambda i: (0, 0))],
        out_shape=[jax.ShapeDtypeStruct((t, k), F32), jax.ShapeDtypeStruct((t, D_MODEL), BF16),
                   jax.ShapeDtypeStruct((SUBLANES, D_MODEL), F32)],
        compiler_params=_params(("arbitrary",)),
    )(dx, mix, g, w)


def _outproj_bwd_attn(dx, mix, g, w, out, name):
    t = dx.shape[0]
    tm = MERGE_TILE

    def body(dx_ref, mix_ref, g_ref, w_ref, out_ref, do1, do4, do16, dl1, dl4, dl16, dz_ref, dg_ref, tile_ref):
        dz, dgr = _rms_bwd(mix_ref[...], g_ref[...], dx_ref[...])
        dzb = dz.astype(BF16)
        dz_ref[...] = dzb
        _acc_rows8(dg_ref, _rows8(dgr), pl.program_id(0) == 0)
        dout = _dot_nt(dzb, w_ref[...])
        for p in range(LANE_GROUPS):
            tile_ref[p] = dout[:, p * LANES:(p + 1) * LANES]
        _store_dilated(tile_ref, (do1, do4, do16), BF16)
        r = lax.broadcasted_iota(jnp.int32, (LANES, LANES), 0) // C_HEAD_DIM
        c = lax.broadcasted_iota(jnp.int32, (LANES, LANES), 1) // C_HEAD_DIM
        same_head = (r == c).astype(F32)
        prod = dout * out_ref[...]
        for p in range(LANE_GROUPS):
            tile_ref[p] = jnp.dot(prod[:, p * LANES:(p + 1) * LANES], same_head, precision=lax.Precision.HIGHEST,
                                  preferred_element_type=F32)
        _store_dilated(tile_ref, (dl1, dl4, dl16), F32)

    row = lambda i: (i, 0)
    n_seq = t // SEQ
    return pl.pallas_call(
        body, name=name, grid=(t // tm,),
        in_specs=[pl.BlockSpec((tm, D_MODEL), row), pl.BlockSpec((tm, D_MODEL), row),
                  pl.BlockSpec((1, D_MODEL), lambda i: (0, 0)), pl.BlockSpec((D_MODEL, D_MODEL), lambda i: (0, 0)),
                  pl.BlockSpec((tm, D_MODEL), row)],
        out_specs=_dilated_specs(tm, D_MODEL, lambda: 0) * 2 + [
            pl.BlockSpec((tm, D_MODEL), row), pl.BlockSpec((SUBLANES, D_MODEL), lambda i: (0, 0))],
        out_shape=_dilated_shapes(n_seq, D_MODEL, BF16) + _dilated_shapes(n_seq, D_MODEL, F32) + [
            jax.ShapeDtypeStruct((t, D_MODEL), BF16), jax.ShapeDtypeStruct((SUBLANES, D_MODEL), F32)],
        scratch_shapes=[pltpu.VMEM((LANE_GROUPS, tm, LANES), F32)],
        compiler_params=_params(("arbitrary",)),
    )(dx, mix, g, w, out)


def _inproj_bwd(dproj, w, dx, x, g, name):
    t = x.shape[0]
    n = w.shape[1]
    tm = ROW_TILE

    def body(dp_ref, w_ref, dx_ref, x_ref, g_ref, o_ref, dg_ref):
        dxn, dgr = _rms_bwd(x_ref[...], g_ref[...], _dot_nt(dp_ref[...], w_ref[...]))
        o_ref[...] = dx_ref[...] + dxn
        _acc_rows8(dg_ref, _rows8(dgr), pl.program_id(0) == 0)

    row = lambda i: (i, 0)
    return pl.pallas_call(
        body, name=name, grid=(t // tm,),
        in_specs=[pl.BlockSpec((tm, n), row), pl.BlockSpec((D_MODEL, n), lambda i: (0, 0)),
                  pl.BlockSpec((tm, D_MODEL), row), pl.BlockSpec((tm, D_MODEL), row),
                  pl.BlockSpec((1, D_MODEL), lambda i: (0, 0))],
        out_specs=[pl.BlockSpec((tm, D_MODEL), row), pl.BlockSpec((SUBLANES, D_MODEL), lambda i: (0, 0))],
        out_shape=[jax.ShapeDtypeStruct((t, D_MODEL), F32), jax.ShapeDtypeStruct((SUBLANES, D_MODEL), F32)],
        compiler_params=_params(("arbitrary",)),
    )(dproj, w, dx, x, g)


def _grad_w(a, b, col_blocks, name):
    t, k = a.shape
    n = b.shape[1]
    tk = min(k, 1024)
    per_owner = n // N_DEV
    tn = 2 * per_owner if col_blocks else min(n, 1024)

    def body(a_ref, b_ref, o_ref, at_ref):
        @pl.when(pl.program_id(1) == 0)
        def _():
            for c in range(t // ROW_TILE):
                rows = slice(c * ROW_TILE, (c + 1) * ROW_TILE)
                at_ref[:, rows] = a_ref[rows, :].T

        res = _dot(at_ref[...], b_ref[...]).astype(BF16)
        if col_blocks:
            o_ref[0] = res[:, :per_owner]
            o_ref[1] = res[:, per_owner:]
        else:
            o_ref[...] = res

    if col_blocks:
        out_spec = pl.BlockSpec((2, tk, per_owner), lambda i, j: (j, i, 0))
        out_shape = jax.ShapeDtypeStruct((N_DEV, k, per_owner), BF16)
    else:
        out_spec = pl.BlockSpec((tk, tn), lambda i, j: (i, j))
        out_shape = jax.ShapeDtypeStruct((k, n), BF16)
    return pl.pallas_call(
        body, name=name, grid=(k // tk, n // tn),
        in_specs=[pl.BlockSpec((t, tk), lambda i, j: (0, i)), pl.BlockSpec((t, tn), lambda i, j: (0, j))],
        out_specs=out_spec, out_shape=out_shape,
        scratch_shapes=[pltpu.VMEM((tk, t), BF16)],
        compiler_params=_params(("parallel", "arbitrary")),
    )(a, b)


FF_BLOCK = D_FF // N_DEV
FF_STEP = 1024
FF_STEPS = D_FF // FF_STEP


def _ffn_fwd(x, g_pre, w1, w2, g_post, name):
    t = x.shape[0]
    tm = ROW_TILE

    def body(x_ref, gp_ref, w1_ref, w2_ref, gq_ref, xo_ref, y_ref, h_ref):
        j = pl.program_id(1)

        @pl.when(j == 0)
        def _():
            h_ref[...] = _rms(x_ref[...], gp_ref[...]).astype(BF16)

        a = _dot(h_ref[...], w1_ref[...])
        r = jnp.square(jnp.maximum(a, 0.0)).astype(BF16)
        term = _dot(r, w2_ref[...])

        @pl.when(j == 0)
        def _():
            y_ref[...] = term

        @pl.when(j > 0)
        def _():
            y_ref[...] += term

        @pl.when(j == FF_STEPS - 1)
        def _():
            xo_ref[...] = x_ref[...] + _rms(y_ref[...], gq_ref[...])

    row = lambda i, j: (i, 0)
    vec = pl.BlockSpec((1, D_MODEL), lambda i, j: (0, 0))
    return pl.pallas_call(
        body, name=name, grid=(t // tm, FF_STEPS),
        in_specs=[pl.BlockSpec((tm, D_MODEL), row), vec,
                  pl.BlockSpec((D_MODEL, FF_STEP), lambda i, j: (0, j)),
                  pl.BlockSpec((FF_STEP, D_MODEL), lambda i, j: (j, 0)), vec],
        out_specs=[pl.BlockSpec((tm, D_MODEL), row)] * 3,
        out_shape=[jax.ShapeDtypeStruct((t, D_MODEL), F32), jax.ShapeDtypeStruct((t, D_MODEL), F32),
                   jax.ShapeDtypeStruct((t, D_MODEL), BF16)],
        compiler_params=_params(("parallel", "arbitrary")),
    )(x, g_pre, w1, w2, g_post)


def _ffn_bwd(dxo, x, y, h, g_pre, w1, w2, g_post, name):
    t = x.shape[0]
    tm = ROW_TILE

    def body(dxo_ref, x_ref, y_ref, h_ref, gp_ref, w1_ref, w2_ref, gq_ref,
             dx_ref, dy_ref, r_ref, da_ref, dgp_ref, dgq_ref, acc_ref):
        i, j = pl.program_id(0), pl.program_id(1)

        @pl.when(j == 0)
        def _():
            dy, dgr = _rms_bwd(y_ref[...], gq_ref[...], dxo_ref[...])
            dy_ref[...] = dy.astype(BF16)
            _acc_rows8(dgq_ref, _rows8(dgr), i == 0)

        a = _dot(h_ref[...], w1_ref[...])
        ra = jnp.maximum(a, 0.0)
        r_ref[...] = jnp.square(ra).astype(BF16)
        dr = _dot_nt(dy_ref[...], w2_ref[...])
        da = (dr * (2.0 * ra)).astype(BF16)
        da_ref[...] = da
        term = _dot_nt(da, w1_ref[...])

        @pl.when(j == 0)
        def _():
            acc_ref[...] = term

        @pl.when(j > 0)
        def _():
            acc_ref[...] += term

        @pl.when(j == FF_STEPS - 1)
        def _():
            dxn, dgr = _rms_bwd(x_ref[...], gp_ref[...], acc_ref[...])
            dx_ref[...] = dxo_ref[...] + dxn
            _acc_rows8(dgp_ref, _rows8(dgr), i == 0)

    row = lambda i, j: (i, 0)
    vec = pl.BlockSpec((1, D_MODEL), lambda i, j: (0, 0))
    acc8 = pl.BlockSpec((SUBLANES, D_MODEL), lambda i, j: (0, 0))
    return pl.pallas_call(
        body, name=name, grid=(t // tm, FF_STEPS),
        in_specs=[pl.BlockSpec((tm, D_MODEL), row)] * 4 + [
            vec, pl.BlockSpec((D_MODEL, FF_STEP), lambda i, j: (0, j)),
            pl.BlockSpec((FF_STEP, D_MODEL), lambda i, j: (j, 0)), vec],
        out_specs=[pl.BlockSpec((tm, D_MODEL), row), pl.BlockSpec((tm, D_MODEL), row),
                   pl.BlockSpec((tm, FF_STEP), lambda i, j: (i, j)), pl.BlockSpec((tm, FF_STEP), lambda i, j: (i, j)),
                   acc8, acc8],
        out_shape=[jax.ShapeDtypeStruct((t, D_MODEL), F32), jax.ShapeDtypeStruct((t, D_MODEL), BF16),
                   jax.ShapeDtypeStruct((t, D_FF), BF16), jax.ShapeDtypeStruct((t, D_FF), BF16),
                   jax.ShapeDtypeStruct((SUBLANES, D_MODEL), F32), jax.ShapeDtypeStruct((SUBLANES, D_MODEL), F32)],
        scratch_shapes=[pltpu.VMEM((tm, D_MODEL), F32)],
        compiler_params=_params(("arbitrary", "arbitrary")),
    )(dxo, x, y, h, g_pre, w1, w2, g_post)


def _lower_bound(table):
    e = jnp.exp(table - jnp.max(table, axis=0, keepdims=True))
    return e[0:1, :] / jnp.sum(e, axis=0, keepdims=True)


def _hgrn2_step(st, qraw, fl, v, graw, lb, an, tri):
    f = lb + (1.0 - lb) * _sigmoid(fl)
    logf = jnp.log(f)
    kk = 1.0 - f
    q = qraw * _sigmoid(qraw)
    gsum = jnp.dot(tri, logf, precision=lax.Precision.HIGHEST, preferred_element_type=F32)
    o = _dot_nt((q * jnp.exp(gsum)).astype(BF16), st.astype(BF16))
    row = lax.broadcasted_iota(jnp.int32, gsum.shape, 0)
    for s in range(SUB_CHUNK):
        pick = row == s
        g_s = jnp.sum(jnp.where(pick, gsum, 0.0), axis=0, keepdims=True)
        k_s = jnp.sum(jnp.where(pick, kk, 0.0), axis=0, keepdims=True)
        v_s = jnp.sum(jnp.where(pick, v, 0.0), axis=0, keepdims=True)
        decay = jnp.exp(jnp.where(row >= s, gsum - g_s, NEG))
        score = jnp.sum(q * k_s * decay, axis=1, keepdims=True)
        o = o + score * v_s
    g_last = jnp.sum(jnp.where(row == SUB_CHUNK - 1, gsum, 0.0), axis=0, keepdims=True)
    kd = kk * jnp.exp(g_last - gsum)
    st_new = st * jnp.exp(g_last) + _dot_tn(v.astype(BF16), kd.astype(BF16))
    out = _rms(o, an) * (graw * _sigmoid(graw))
    return st_new, out


def _tri():
    r = lax.broadcasted_iota(jnp.int32, (SUB_CHUNK, SUB_CHUNK), 0)
    c = lax.broadcasted_iota(jnp.int32, (SUB_CHUNK, SUB_CHUNK), 1)
    return (c <= r).astype(F32)


def _hgrn2_fwd(proj, lb_table, a_norm, name):
    t = proj.shape[0]
    tb = HGRN_BLOCK
    n_tb = SEQ // tb
    n_seq = t // SEQ
    n_sub = tb // SUB_CHUNK

    def body(q_ref, f_ref, i_ref, g_ref, lbt_ref, an_ref, o_ref, sts_ref, st_ref):
        @pl.when(pl.program_id(1) == 0)
        def _():
            st_ref[...] = jnp.zeros_like(st_ref)

        lb = _lower_bound(lbt_ref[...])
        an = an_ref[...]
        tri = _tri()

        def step(c, carry):
            rows = pl.ds(pl.multiple_of(c * SUB_CHUNK, SUB_CHUNK), SUB_CHUNK)
            for h in range(A_HEADS):
                lanes = slice(h * HEAD_A, (h + 1) * HEAD_A)
                st = st_ref[h]
                sts_ref[0, c, h] = st
                st_new, out = _hgrn2_step(st, q_ref[rows, lanes], f_ref[rows, lanes], i_ref[rows, lanes],
                                          g_ref[rows, lanes], lb[:, lanes], an[:, lanes], tri)
                st_ref[h] = st_new
                o_ref[rows, lanes] = out.astype(BF16)
            return carry

        lax.fori_loop(0, n_sub, step, 0)

    def col(k):
        return pl.BlockSpec((tb, A_WIDTH), lambda b, s, k=k: (b * n_tb + s, k))

    return pl.pallas_call(
        body, name=name, grid=(n_seq, n_tb),
        in_specs=[col(0), col(1), col(2), col(3),
                  pl.BlockSpec((3, A_WIDTH), lambda b, s: (0, 0)), pl.BlockSpec((1, A_WIDTH), lambda b, s: (0, 0))],
        out_specs=[pl.BlockSpec((tb, A_WIDTH), lambda b, s: (b * n_tb + s, 0)),
                   pl.BlockSpec((1, n_sub, A_HEADS, HEAD_A, HEAD_A), lambda b, s: (b * n_tb + s, 0, 0, 0, 0))],
        out_shape=[jax.ShapeDtypeStruct((t, A_WIDTH), BF16),
                   jax.ShapeDtypeStruct((n_seq * n_tb, n_sub, A_HEADS, HEAD_A, HEAD_A), F32)],
        scratch_shapes=[pltpu.VMEM((A_HEADS, HEAD_A, HEAD_A), F32)],
        compiler_params=_params(("parallel", "arbitrary")),
    )(proj, proj, proj, proj, lb_table, a_norm)


def _hgrn2_bwd(proj, dcat, states, lb_table, a_norm, name):
    t = proj.shape[0]
    tb = HGRN_BLOCK
    n_tb = SEQ // tb
    n_seq = t // SEQ
    n_sub = tb // SUB_CHUNK

    def body(q_ref, f_ref, i_ref, g_ref, do_ref, sts_ref, lbt_ref, an_ref, dp_ref, dlb_ref, dan_ref, dst_ref):
        b, s = pl.program_id(0), pl.program_id(1)

        @pl.when(s == 0)
        def _():
            dst_ref[...] = jnp.zeros_like(dst_ref)

        @pl.when((b == 0) & (s == 0))
        def _():
            dlb_ref[...] = jnp.zeros_like(dlb_ref)
            dan_ref[...] = jnp.zeros_like(dan_ref)

        lb = _lower_bound(lbt_ref[...])
        an = an_ref[...]
        tri = _tri()

        def bwd(k, carry):
            c = n_sub - 1 - k
            rows = pl.ds(pl.multiple_of(c * SUB_CHUNK, SUB_CHUNK), SUB_CHUNK)
            for h in range(A_HEADS):
                lanes = slice(h * HEAD_A, (h + 1) * HEAD_A)
                _, vjp = jax.vjp(
                    lambda st, a1, a2, a3, a4, a5, a6: _hgrn2_step(st, a1, a2, a3, a4, a5, a6, tri),
                    sts_ref[0, c, h], q_ref[rows, lanes], f_ref[rows, lanes], i_ref[rows, lanes], g_ref[rows, lanes],
                    lb[:, lanes], an[:, lanes])
                dst, dq, df, di, dg, dlb, dan = vjp((dst_ref[h], do_ref[rows, lanes]))
                dst_ref[h] = dst
                for sec, val in enumerate((dq, df, di, dg)):
                    dp_ref[rows, sec * A_WIDTH + h * HEAD_A:sec * A_WIDTH + (h + 1) * HEAD_A] = val.astype(BF16)
                dlb_ref[0:1, lanes] += dlb
                dan_ref[0:1, lanes] += dan
            return carry

        lax.fori_loop(0, n_sub, bwd, 0)

    def rev(s):
        return n_tb - 1 - s

    def col(k):
        return pl.BlockSpec((tb, A_WIDTH), lambda b, s, k=k: (b * n_tb + rev(s), k))

    acc8 = pl.BlockSpec((SUBLANES, A_WIDTH), lambda b, s: (0, 0))
    return pl.pallas_call(
        body, name=name, grid=(n_seq, n_tb),
        in_specs=[col(0), col(1), col(2), col(3), col(0),
                  pl.BlockSpec((1, n_sub, A_HEADS, HEAD_A, HEAD_A), lambda b, s: (b * n_tb + rev(s), 0, 0, 0, 0)),
                  pl.BlockSpec((3, A_WIDTH), lambda b, s: (0, 0)), pl.BlockSpec((1, A_WIDTH), lambda b, s: (0, 0))],
        out_specs=[pl.BlockSpec((tb, 4 * A_WIDTH), lambda b, s: (b * n_tb + rev(s), 0)), acc8, acc8],
        out_shape=[jax.ShapeDtypeStruct((t, 4 * A_WIDTH), BF16)] + [jax.ShapeDtypeStruct((SUBLANES, A_WIDTH), F32)] * 2,
        scratch_shapes=[pltpu.VMEM((A_HEADS, HEAD_A, HEAD_A), F32)],
        compiler_params=_params(("arbitrary", "arbitrary")),
    )(proj, proj, proj, proj, dcat, states, lb_table, a_norm)


GMLP_ROWS = 512


def _gmlp_chunk(ub, vb, ln_g, ln_b, ws, bias):
    u = [_gelu(a) for a in ub]
    v = [_gelu(a) for a in vb]
    mu = sum(jnp.sum(a, axis=-1, keepdims=True) for a in v) * (1.0 / B_WIDTH)
    cen = [a - mu for a in v]
    var = sum(jnp.sum(a * a, axis=-1, keepdims=True) for a in cen) * (1.0 / B_WIDTH)
    inv = lax.rsqrt(var + EPS)
    r = lax.broadcasted_iota(jnp.int32, (B_CHUNK, B_CHUNK), 0)
    c = lax.broadcasted_iota(jnp.int32, (B_CHUNK, B_CHUNK), 1)
    outs = []
    for g in range(B_GROUPS):
        vn = (cen[g] * inv * ln_g[g] + ln_b[g]).astype(BF16)
        wm = jnp.where(c <= r, ws[g], 0.0).astype(BF16)
        outs.append(u[g] * (_dot(wm, vn) + bias[g]))
    return outs


def _lane_groups(ref, rows=slice(None)):
    return [ref[rows, g * LANES:(g + 1) * LANES] for g in range(B_GROUPS)]


def _gmlp_fwd(proj, ln_g, ln_b, ws, bias_t, name):
    t = proj.shape[0]
    tm = GMLP_ROWS

    def body(u_ref, v_ref, lg_ref, lb_ref, ws_ref, bt_ref, o_ref):
        for ch in range(tm // B_CHUNK):
            rows = slice(ch * B_CHUNK, (ch + 1) * B_CHUNK)
            outs = _gmlp_chunk(_lane_groups(u_ref, rows), _lane_groups(v_ref, rows), _lane_groups(lg_ref),
                               _lane_groups(lb_ref), [ws_ref[g] for g in range(B_GROUPS)],
                               [bt_ref[:, g:g + 1] for g in range(B_GROUPS)])
            for g in range(B_GROUPS):
                o_ref[rows, g * LANES:(g + 1) * LANES] = outs[g].astype(BF16)

    vec = pl.BlockSpec((1, B_WIDTH), lambda i: (0, 0))
    return pl.pallas_call(
        body, name=name, grid=(t // tm,),
        in_specs=[pl.BlockSpec((tm, B_WIDTH), lambda i: (i, 4)), pl.BlockSpec((tm, B_WIDTH), lambda i: (i, 5)), vec, vec,
                  pl.BlockSpec((B_GROUPS, B_CHUNK, B_CHUNK), lambda i: (0, 0, 0)),
                  pl.BlockSpec((B_CHUNK, B_GROUPS), lambda i: (0, 0))],
        out_specs=pl.BlockSpec((tm, B_WIDTH), lambda i: (i, 0)),
        out_shape=jax.ShapeDtypeStruct((t, B_WIDTH), BF16),
        compiler_params=_params(("parallel",)),
    )(proj, proj, ln_g, ln_b, ws, bias_t)


def _gmlp_bwd(proj, dcat, ln_g, ln_b, ws, bias_t, name):
    t = proj.shape[0]
    tm = GMLP_ROWS

    def body(u_ref, v_ref, do_ref, lg_ref, lb_ref, ws_ref, bt_ref, duv_ref, dlg_ref, dlb_ref, dws_ref, dbt_ref):
        @pl.when(pl.program_id(0) == 0)
        def _():
            dlg_ref[...] = jnp.zeros_like(dlg_ref)
            dlb_ref[...] = jnp.zeros_like(dlb_ref)
            dws_ref[...] = jnp.zeros_like(dws_ref)
            dbt_ref[...] = jnp.zeros_like(dbt_ref)

        for ch in range(tm // B_CHUNK):
            rows = slice(ch * B_CHUNK, (ch + 1) * B_CHUNK)
            _, vjp = jax.vjp(
                _gmlp_chunk, _lane_groups(u_ref, rows), _lane_groups(v_ref, rows), _lane_groups(lg_ref),
                _lane_groups(lb_ref), [ws_ref[g] for g in range(B_GROUPS)],
                [bt_ref[:, g:g + 1] for g in range(B_GROUPS)])
            du, dv, dlg, dlb, dw, dbt = vjp(_lane_groups(do_ref, rows))
            for g in range(B_GROUPS):
                lanes = slice(g * LANES, (g + 1) * LANES)
                duv_ref[rows, lanes] = du[g].astype(BF16)
                duv_ref[rows, B_WIDTH + g * LANES:B_WIDTH + (g + 1) * LANES] = dv[g].astype(BF16)
                dlg_ref[0:1, lanes] += dlg[g]
                dlb_ref[0:1, lanes] += dlb[g]
                dws_ref[g] += dw[g]
                dbt_ref[:, g:g + 1] += dbt[g]

    vec = pl.BlockSpec((1, B_WIDTH), lambda i: (0, 0))
    acc8 = pl.BlockSpec((SUBLANES, B_WIDTH), lambda i: (0, 0))
    ws_spec = pl.BlockSpec((B_GROUPS, B_CHUNK, B_CHUNK), lambda i: (0, 0, 0))
    bt_spec = pl.BlockSpec((B_CHUNK, B_GROUPS), lambda i: (0, 0))
    return pl.pallas_call(
        body, name=name, grid=(t // tm,),
        in_specs=[pl.BlockSpec((tm, B_WIDTH), lambda i: (i, 4)), pl.BlockSpec((tm, B_WIDTH), lambda i: (i, 5)),
                  pl.BlockSpec((tm, B_WIDTH), lambda i: (i, 1)), vec, vec, ws_spec, bt_spec],
        out_specs=[pl.BlockSpec((tm, 2 * B_WIDTH), lambda i: (i, 0)), acc8, acc8, ws_spec, bt_spec],
        out_shape=[jax.ShapeDtypeStruct((t, 2 * B_WIDTH), BF16), jax.ShapeDtypeStruct((SUBLANES, B_WIDTH), F32),
                   jax.ShapeDtypeStruct((SUBLANES, B_WIDTH), F32),
                   jax.ShapeDtypeStruct((B_GROUPS, B_CHUNK, B_CHUNK), F32),
                   jax.ShapeDtypeStruct((B_CHUNK, B_GROUPS), F32)],
        compiler_params=_params(("arbitrary",)),
    )(proj, proj, dcat, ln_g, ln_b, ws, bias_t)


QK_SCALE = 1.0 / math.sqrt(C_HEAD_DIM)
ATTN_UNROLL = 4
LANE_GROUPS = D_MODEL // LANES
Q_BLOCKS = SEQ // C_BLOCK


def _attn_window(i, d):
    sub_blocks = Q_BLOCKS // d
    q0 = pl.multiple_of(i * C_BLOCK, C_BLOCK)
    k0 = pl.multiple_of(jnp.maximum(i - 1, 0) * C_BLOCK, C_BLOCK)
    key = k0 + lax.broadcasted_iota(jnp.int32, (C_BLOCK, 2 * C_BLOCK), 1)
    dist = (q0 + lax.broadcasted_iota(jnp.int32, (C_BLOCK, 2 * C_BLOCK), 0)) - key
    own_subsequence = (key >= q0) | (i % sub_blocks > 0)
    return pl.ds(q0, C_BLOCK), pl.ds(k0, 2 * C_BLOCK), (dist >= 0) & (dist <= C_BLOCK) & own_subsequence


def _head_masks():
    lane = lax.broadcasted_iota(jnp.int32, (C_BLOCK, LANES), 1)
    return [lane < C_HEAD_DIM, lane >= C_HEAD_DIM]


def _flat_spec(col_of):
    return pl.BlockSpec((1, SEQ, LANES), lambda b, g: (b, 0, col_of(g)))


def _attn_branch_fwd(qkv, name):
    n_seq, d, l, _ = qkv.shape
    flat = qkv.reshape(n_seq, SEQ, ODD_IN)

    def body(q_ref, k_ref, v_ref, o_ref, m_ref, l_ref):
        heads = _head_masks()

        def block(i, carry):
            rows, keys, mask = _attn_window(i, d)
            q, k, v = q_ref[0, rows, :], k_ref[0, keys, :], v_ref[0, keys, :]
            res = []
            for hm in heads:
                s = jnp.where(mask, _dot_nt(jnp.where(hm, q, 0), k) * QK_SCALE, NEG)
                m = jnp.max(s, axis=-1, keepdims=True)
                p = jnp.exp(s - m)
                res.append((_dot(p.astype(BF16), v), m, jnp.sum(p, axis=-1, keepdims=True)))
            o_ref[0, rows, :] = jnp.where(heads[0], res[0][0], res[1][0])
            m_ref[0, rows, :] = jnp.where(heads[0], res[0][1], res[1][1])
            l_ref[0, rows, :] = jnp.where(heads[0], res[0][2], res[1][2])
            return carry

        lax.fori_loop(0, Q_BLOCKS, block, 0, unroll=ATTN_UNROLL)

    outs = pl.pallas_call(
        body, name=name, grid=(n_seq, LANE_GROUPS),
        in_specs=[_flat_spec(lambda g: g), _flat_spec(lambda g: LANE_GROUPS + g),
                  _flat_spec(lambda g: 2 * LANE_GROUPS + g)],
        out_specs=[_flat_spec(lambda g: g)] * 3,
        out_shape=[jax.ShapeDtypeStruct((n_seq, SEQ, D_MODEL), F32)] * 3,
        compiler_params=_params(("parallel", "parallel")),
    )(flat, flat, flat)
    return [o.reshape(n_seq, d, l, D_MODEL) for o in outs]


def _attn_merge(branches, name):
    n_seq = branches[0][0].shape[0]
    t = n_seq * SEQ
    tm = MERGE_TILE

    def body(*refs):
        ins = refs[:9]
        o_ref, ob_ref, lse1_ref, lse4_ref, lse16_ref = refs[9:14]
        nat = refs[14:]
        for b, d in enumerate(C_DILATIONS[1:]):
            for k in range(3):
                _load_dilated(ins[3 + 3 * b + k], d, nat[3 * b + k])
        for p in range(LANE_GROUPS):
            lanes = slice(p * LANES, (p + 1) * LANES)
            os_ = [ins[0][0, 0, :, lanes], nat[0][p], nat[3][p]]
            ms = [ins[1][0, 0, :, lanes], nat[1][p], nat[4][p]]
            ls = [ins[2][0, 0, :, lanes], nat[2][p], nat[5][p]]
            m_all = jnp.maximum(jnp.maximum(ms[0], ms[1]), ms[2])
            ws = [jnp.exp(ms[b] - m_all) for b in range(3)]
            total = ws[0] * ls[0] + ws[1] * ls[1] + ws[2] * ls[2]
            o = (ws[0] * os_[0] + ws[1] * os_[1] + ws[2] * os_[2]) / total
            o_ref[:, lanes] = o
            ob_ref[:, lanes] = o.astype(BF16)
            nat[0][p] = m_all + jnp.log(total)
        _store_dilated(nat[0], (lse1_ref, lse4_ref, lse16_ref), F32)

    row = pl.BlockSpec((tm, D_MODEL), lambda i: (i, 0))
    flat = [a for br in branches for a in br]
    in_specs = []
    for spec in _dilated_specs(tm, D_MODEL, lambda: 0):
        in_specs += [spec] * 3
    return pl.pallas_call(
        body, name=name, grid=(t // tm,), in_specs=in_specs,
        out_specs=[row, row] + _dilated_specs(tm, D_MODEL, lambda: 0),
        out_shape=[jax.ShapeDtypeStruct((t, D_MODEL), F32), jax.ShapeDtypeStruct((t, D_MODEL), BF16)]
        + _dilated_shapes(n_seq, D_MODEL, F32),
        scratch_shapes=[pltpu.VMEM((LANE_GROUPS, tm, LANES), F32)] * 6,
        compiler_params=_params(("parallel",)),
    )(*flat)


def _attn_branch_bwd(qkv, dout, lse, delta, name):
    n_seq, d, l, _ = qkv.shape
    flat = lambda a: a.reshape(n_seq, SEQ, a.shape[-1])

    def body(q_ref, k_ref, v_ref, do_ref, lse_ref, dl_ref, dq_ref, dk_ref, dv_ref):
        heads = _head_masks()
        dk_ref[...] = jnp.zeros_like(dk_ref)
        dv_ref[...] = jnp.zeros_like(dv_ref)

        def block(i, carry):
            rows, keys, mask = _attn_window(i, d)
            q, do = q_ref[0, rows, :], do_ref[0, rows, :]
            k, v = k_ref[0, keys, :], v_ref[0, keys, :]
            lse_b, dl_b = lse_ref[0, rows, :], dl_ref[0, rows, :]
            dq, dk, dv = [], None, None
            for hh, hm in enumerate(heads):
                col = slice(hh * C_HEAD_DIM, hh * C_HEAD_DIM + 1)
                qh, doh = jnp.where(hm, q, 0), jnp.where(hm, do, 0)
                s = jnp.where(mask, _dot_nt(qh, k) * QK_SCALE, NEG)
                p = jnp.exp(s - lse_b[:, col])
                ds = (p * (_dot_nt(doh, v) - dl_b[:, col]) * QK_SCALE).astype(BF16)
                dq.append(_dot(ds, k))
                dk_h, dv_h = _dot_tn(ds, qh), _dot_tn(p.astype(BF16), doh)
                dk = dk_h if dk is None else dk + dk_h
                dv = dv_h if dv is None else dv + dv_h
            dq_ref[0, rows, :] = jnp.where(heads[0], dq[0], dq[1])
            dk_ref[0, keys, :] += dk
            dv_ref[0, keys, :] += dv
            return carry

        lax.fori_loop(0, Q_BLOCKS, block, 0, unroll=ATTN_UNROLL)

    act = _flat_spec(lambda g: g)
    outs = pl.pallas_call(
        body, name=name, grid=(n_seq, LANE_GROUPS),
        in_specs=[_flat_spec(lambda g: g), _flat_spec(lambda g: LANE_GROUPS + g),
                  _flat_spec(lambda g: 2 * LANE_GROUPS + g), act, act, act],
        out_specs=[act] * 3,
        out_shape=[jax.ShapeDtypeStruct((n_seq, SEQ, D_MODEL), F32)] * 3,
        compiler_params=_params(("parallel", "parallel")),
    )(flat(qkv), flat(qkv), flat(qkv), flat(dout), flat(lse), flat(delta))
    return [o.reshape(n_seq, d, l, D_MODEL) for o in outs]


def _attn_combine_bwd(grads, rope, name):
    n_seq = grads[0][0].shape[0]
    t = n_seq * SEQ
    tm = MERGE_TILE

    def body(*refs):
        c_ref, s_ref, o_ref, nat4_ref, nat16_ref = refs[9:]
        for sec in range(3):
            _load_dilated(refs[3 + sec], 4, nat4_ref)
            _load_dilated(refs[6 + sec], 16, nat16_ref)
            for p in range(LANE_GROUPS):
                blk = refs[sec][0, 0, :, p * LANES:(p + 1) * LANES] + nat4_ref[p] + nat16_ref[p]
                if sec < 2:
                    blk = blk * c_ref[...] - _swap_halves(blk) * s_ref[...]
                o_ref[:, sec * D_MODEL + p * LANES:sec * D_MODEL + (p + 1) * LANES] = blk.astype(BF16)

    tab = pl.BlockSpec((tm, LANES), lambda i: (i, 0))
    flat = [a for br in grads for a in br]
    in_specs = []
    for spec in _dilated_specs(tm, D_MODEL, lambda: 0):
        in_specs += [spec] * 3
    return pl.pallas_call(
        body, name=name, grid=(t // tm,), in_specs=in_specs + [tab, tab],
        out_specs=pl.BlockSpec((tm, ODD_IN), lambda i: (i, 0)),
        out_shape=jax.ShapeDtypeStruct((t, ODD_IN), BF16),
        scratch_shapes=[pltpu.VMEM((LANE_GROUPS, tm, LANES), F32)] * 2,
        compiler_params=_params(("parallel",)),
    )(*flat, *rope)


def _loss_grad(y, target, name):
    t = y.shape[0]
    tm = ROW_TILE

    def body(y_ref, t_ref, d_ref, l_ref):
        diff = y_ref[...] - t_ref[...]
        d_ref[...] = diff * (1.0 / D_MODEL)
        _acc_rows8(l_ref, _rows8(diff * diff) * (0.5 / D_MODEL), pl.program_id(0) == 0)

    row = pl.BlockSpec((tm, D_MODEL), lambda i: (i, 0))
    return pl.pallas_call(
        body, name=name, grid=(t // tm,), in_specs=[row, row],
        out_specs=[row, pl.BlockSpec((SUBLANES, D_MODEL), lambda i: (0, 0))],
        out_shape=[jax.ShapeDtypeStruct((t, D_MODEL), F32), jax.ShapeDtypeStruct((SUBLANES, D_MODEL), F32)],
        compiler_params=_params(("arbitrary",)),
    )(y, target)


def _adamw(w, g, m, v):
    m = ADAM_B1 * m + (1.0 - ADAM_B1) * g
    v = ADAM_B2 * v + (1.0 - ADAM_B2) * jnp.square(g)
    m_hat = m / (1.0 - ADAM_B1 ** ADAM_STEP)
    v_hat = v / (1.0 - ADAM_B2 ** ADAM_STEP)
    delta = -ADAM_LR * (m_hat / (jnp.sqrt(v_hat) + ADAM_EPS) + ADAM_WD * w)
    return delta, m, v


def _adamw_sharded(parts, w, m, v, name):
    n_layers, rows, cols = w.shape
    tr = min(rows, 256)

    def body(*refs):
        p_refs = refs[:n_layers]
        w_ref, m_ref, v_ref, g_ref, d_ref, mo_ref, vo_ref = refs[n_layers:]
        layer = pl.program_id(0)
        g = None
        for l, p_ref in enumerate(p_refs):
            g_l = p_ref[0].astype(F32)
            for s in range(1, N_DEV):
                g_l = g_l + p_ref[s].astype(F32)
            g = g_l if g is None else jnp.where(layer == l, g_l, g)
        delta, mn, vn = _adamw(w_ref[0], g, m_ref[0], v_ref[0])
        g_ref[0] = g
        d_ref[0] = delta
        mo_ref[0] = mn
        vo_ref[0] = vn

    def part_spec(l):
        return pl.BlockSpec((N_DEV, tr, cols), lambda a, i: (0, jnp.where(a == l, i, 0), 0))

    row = pl.BlockSpec((1, tr, cols), lambda a, i: (a, i, 0))
    return pl.pallas_call(
        body, name=name, grid=(n_layers, rows // tr),
        in_specs=[part_spec(l) for l in range(n_layers)] + [row, row, row],
        out_specs=[row] * 4, out_shape=[jax.ShapeDtypeStruct(w.shape, F32)] * 4,
        compiler_params=_params(("arbitrary", "arbitrary")),
    )(*parts, w, m, v)


def _small_update(gathered, weights, moments_m, moments_v, lb_index, name):
    n = len(weights)

    def total(ref):
        acc = ref[0]
        for s in range(1, N_DEV):
            acc = acc + ref[s]
        return acc

    def body(*refs):
        g_refs = refs[:n + 1]
        w_refs, m_refs, v_refs = refs[n + 1:2 * n + 1], refs[2 * n + 1:3 * n + 1], refs[3 * n + 1:4 * n + 1]
        outs = refs[4 * n + 1:]
        loss_rows = total(g_refs[n])
        outs[0][...] = jnp.sum(jnp.sum(loss_rows, axis=1, keepdims=True), axis=0, keepdims=True)
        for k in range(n):
            part = total(g_refs[k])
            if k == lb_index:
                dlb = jnp.sum(part, axis=0, keepdims=True)
                tab = w_refs[k][...]
                e = jnp.exp(tab - jnp.max(tab, axis=0, keepdims=True))
                p = e / jnp.sum(e, axis=0, keepdims=True)
                first = lax.broadcasted_iota(jnp.int32, p.shape, 0) == 0
                grads = [(slice(None), p * (jnp.where(first, dlb, 0.0) - p[0:1, :] * dlb))]
            elif part.shape == w_refs[k].shape:
                grads = [(slice(None), part)]
            else:
                grads = [(slice(l, l + 1), jnp.sum(part[l * SUBLANES:(l + 1) * SUBLANES], axis=0, keepdims=True))
                         for l in range(w_refs[k].shape[0])]
            for rows, g in grads:
                delta, mn, vn = _adamw(w_refs[k][rows], g, m_refs[k][rows], v_refs[k][rows])
                outs[1 + 4 * k][rows] = g
                outs[2 + 4 * k][rows] = delta
                outs[3 + 4 * k][rows] = mn
                outs[4 + 4 * k][rows] = vn

    vmem = pl.BlockSpec(memory_space=pltpu.VMEM)
    out_shape = [jax.ShapeDtypeStruct((1, 1), F32)]
    for w in weights:
        out_shape += [jax.ShapeDtypeStruct(w.shape, F32)] * 4
    args = list(gathered) + list(weights) + list(moments_m) + list(moments_v)
    return pl.pallas_call(
        body, name=name, in_specs=[vmem] * len(args), out_specs=[vmem] * len(out_shape), out_shape=out_shape,
        compiler_params=pltpu.CompilerParams(vmem_limit_bytes=VMEM_LIMIT),
    )(*args)


def kernel(x, positions, norm_mix_pre, norm_mix_post, norm_ffn_pre, norm_ffn_post, w_in_even, lb_table, a_norm, b_ln_g, b_ln_b, b_ws, b_bias, w_out_even, w_in_odd, w_out_odd, w_ff1, w_ff2, loss_target, m_norm_mix_pre, m_norm_mix_post, m_norm_ffn_pre, m_norm_ffn_post, m_w_in_even, m_lb_table, m_a_norm, m_b_ln_g, m_b_ln_b, m_b_ws, m_b_bias, m_w_out_even, m_w_in_odd, m_w_out_odd, m_w_ff1, m_w_ff2, v_norm_mix_pre, v_norm_mix_post, v_norm_ffn_pre, v_norm_ffn_post, v_w_in_even, v_lb_table, v_a_norm, v_b_ln_g, v_b_ln_b, v_b_ws, v_b_bias, v_w_out_even, v_w_in_odd, v_w_out_odd, v_w_ff1, v_w_ff2):
    n_seq = x.shape[0]
    t = n_seq * SEQ
    x0 = x.reshape(t, D_MODEL)
    target = loss_target.reshape(t, D_MODEL)

    me = _my_slot().astype(jnp.int32).reshape(1)

    order = ["in_e", "out_e", "ff1_0", "ff2_0", "in_o", "out_o", "ff1_1", "ff2_1"]
    shards = dict(in_e=w_in_even[0], out_e=w_out_even[0], in_o=w_in_odd[0], out_o=w_out_odd[0],
                  ff1_0=w_ff1[0], ff1_1=w_ff1[1], ff2_0=w_ff2[0], ff2_1=w_ff2[1])
    lands = [_place_own(shards[k], me, "place_" + k, False) for k in order]
    g_send, g_recv, lands, _, g_token = _exchange_start(lands, [None] * len(order), "gather_start")

    def get_w(keys, after):
        ks = [order.index(k) for k in keys]
        return _exchange_wait([lands[k] for k in ks], [None] * len(ks), [g_send[k] for k in ks],
                              [g_recv[k] for k in ks], after, "gather_wait_" + keys[0])

    sent = {}

    def put_g(group, blocks):
        keys = list(blocks)
        own = [_place_own(blocks[k], me, "own_" + k, True) for k in keys]
        send_sems, recv_sems, own, srcs, token = _exchange_start(own, [blocks[k] for k in keys], "scatter_start_" + group)
        sent[group] = (keys, own, srcs, send_sems, recv_sems)
        return token

    rope = _rope_tables(positions)
    bias_t = b_bias[0].T
    grads = _local_step(x0, target, rope, norm_mix_pre, norm_mix_post, norm_ffn_pre, norm_ffn_post, lb_table,
                        a_norm, b_ln_g, b_ln_b, b_ws[0], bias_t, get_w, put_g, g_token)
    (dx0, loss_part, dg_mix_pre, dg_mix_post, dg_ffn_pre, dg_ffn_post, d_lb, d_a_norm, d_ln_g, d_ln_b, d_ws,
     d_bias_t) = grads

    recv = {}
    for group, (keys, own, srcs, send_sems, recv_sems) in sent.items():
        done = _exchange_wait(own, srcs, send_sems, recv_sems, dx0, "scatter_wait_" + group)
        recv.update(zip(keys, done))
    big = [("w_in_even", ["in_e"], w_in_even, m_w_in_even, v_w_in_even),
           ("w_out_even", ["out_e"], w_out_even, m_w_out_even, v_w_out_even),
           ("w_in_odd", ["in_o"], w_in_odd, m_w_in_odd, v_w_in_odd),
           ("w_out_odd", ["out_o"], w_out_odd, m_w_out_odd, v_w_out_odd),
           ("w_ff1", ["ff1_0", "ff1_1"], w_ff1, m_w_ff1, v_w_ff1), ("w_ff2", ["ff2_0", "ff2_1"], w_ff2, m_w_ff2, v_w_ff2)]
    big_out = [_adamw_sharded([recv[k] for k in keys], w, m, v, "adamw_" + nm) for nm, keys, w, m, v in big]

    small_parts = [dg_mix_pre, dg_mix_post, dg_ffn_pre, dg_ffn_post,
                   d_lb, d_a_norm, d_ln_g, d_ln_b, d_ws, d_bias_t, loss_part]
    gathered = _exchange(small_parts, True, "gather_small")
    small_w = [norm_mix_pre, norm_mix_post, norm_ffn_pre, norm_ffn_post, lb_table, a_norm, b_ln_g, b_ln_b,
               b_ws[0], bias_t]
    small_m = [m_norm_mix_pre, m_norm_mix_post, m_norm_ffn_pre, m_norm_ffn_post, m_lb_table, m_a_norm, m_b_ln_g,
               m_b_ln_b, m_b_ws[0], m_b_bias[0].T]
    small_v = [v_norm_mix_pre, v_norm_mix_post, v_norm_ffn_pre, v_norm_ffn_post, v_lb_table, v_a_norm, v_b_ln_g,
               v_b_ln_b, v_b_ws[0], v_b_bias[0].T]
    small_out = _small_update(gathered, small_w, small_m, small_v, 4, "small_update")
    loss = small_out[0].reshape(())
    small = [small_out[1 + 4 * k:5 + 4 * k] for k in range(len(small_w))]
    small[8] = [a[None] for a in small[8]]
    small[9] = [a.T[None] for a in small[9]]

    per_weight = small[0:4] + [big_out[0]] + small[4:10] + big_out[1:6]
    grad_x = dx0.reshape(x.shape)
    out = [loss, grad_x]
    for kind in range(4):
        out += [p[kind] for p in per_weight]
    return tuple(out)


def _local_step(x0, target, rope, norm_mix_pre, norm_mix_post, norm_ffn_pre, norm_ffn_post, lb_table, a_norm,
                b_ln_g, b_ln_b, ws, bias_t, get_w, put_g, token):
    def gain(a, l, tok):
        return a[l:l + 1] if tok is None else a[l:l + 1] + tok[0:1, 0:1]

    full = lambda a: a.reshape(-1, D_MODEL)
    columns = lambda a: jnp.transpose(a, (1, 0, 2)).reshape(a.shape[1], -1)
    owners = lambda a: a.reshape((N_DEV, -1) + a.shape[1:])

    g_in_e = columns(get_w(["in_e"], token)[0])
    proj, h_mix0 = _norm_inproj(x0, gain(norm_mix_pre, 0, token), g_in_e, "inproj_even")
    oa, states = _hgrn2_fwd(proj, lb_table, a_norm, "hgrn2_fwd")
    ob = _gmlp_fwd(proj, b_ln_g, b_ln_b, ws, bias_t, "gmlp_fwd")
    w_out_e = full(get_w(["out_e"], ob)[0])
    x1, mix0 = _outproj([oa, ob], w_out_e, x0, gain(norm_mix_post, 0, None), "outproj_even")
    w1_0, w2_0 = get_w(["ff1_0", "ff2_0"], x1)
    w1_0, w2_0 = columns(w1_0), full(w2_0)
    x2, y0, h_ffn0 = _ffn_fwd(x1, gain(norm_ffn_pre, 0, None), w1_0, w2_0, gain(norm_ffn_post, 0, None), "ffn_fwd_0")
    g_in_o = columns(get_w(["in_o"], x2)[0])
    *qkv, h_mix1 = _norm_inproj_rope(x2, gain(norm_mix_pre, 1, None), g_in_o, rope, "inproj_odd")
    branches = [_attn_branch_fwd(a, "attn_fwd_d%d" % d) for a, d in zip(qkv, C_DILATIONS)]
    attn, attn_b, *lse = _attn_merge(branches, "attn_merge")
    w_out_o = full(get_w(["out_o"], attn_b)[0])
    x3, mix1 = _outproj([attn_b], w_out_o, x2, gain(norm_mix_post, 1, None), "outproj_odd")
    w1_1, w2_1 = get_w(["ff1_1", "ff2_1"], x3)
    w1_1, w2_1 = columns(w1_1), full(w2_1)
    x4, y1, h_ffn1 = _ffn_fwd(x3, gain(norm_ffn_pre, 1, None), w1_1, w2_1, gain(norm_ffn_post, 1, None), "ffn_fwd_1")

    dx4, loss_part = _loss_grad(x4, target, "loss_grad")

    dx3, dy1, r1, da1, dg_ffn_pre1, dg_ffn_post1 = _ffn_bwd(
        dx4, x3, y1, h_ffn1, gain(norm_ffn_pre, 1, None), w1_1, w2_1, gain(norm_ffn_post, 1, None), "ffn_bwd_1")
    gw_ff1_1 = _grad_w(h_ffn1, da1, True, "grad_w_ff1_1")
    gw_ff2_1 = _grad_w(r1, dy1, False, "grad_w_ff2_1")
    tok = put_g("ffn1", dict(ff1_1=gw_ff1_1, ff2_1=owners(gw_ff2_1)))
    *dattn, dz1, dg_mix_post1 = _outproj_bwd_attn(dx3, mix1, gain(norm_mix_post, 1, tok), w_out_o, attn,
                                                  "outproj_bwd_odd")
    gw_out_o = _grad_w(attn_b, dz1, False, "grad_w_out_odd")
    grads_c = [_attn_branch_bwd(qkv[b], dattn[b], lse[b], dattn[3 + b], "attn_bwd_d%d" % d)
               for b, d in enumerate(C_DILATIONS)]
    dqkv = _attn_combine_bwd(grads_c, rope, "attn_combine_bwd")
    gw_in_o = _grad_w(h_mix1, dqkv, True, "grad_w_in_odd")
    tok = put_g("mix1", dict(out_o=owners(gw_out_o), in_o=gw_in_o))
    dx2, dg_mix_pre1 = _inproj_bwd(dqkv, g_in_o, dx3, x2, gain(norm_mix_pre, 1, tok), "inproj_bwd_odd")

    dx1, dy0, r0, da0, dg_ffn_pre0, dg_ffn_post0 = _ffn_bwd(
        dx2, x1, y0, h_ffn0, gain(norm_ffn_pre, 0, None), w1_0, w2_0, gain(norm_ffn_post, 0, None), "ffn_bwd_0")
    gw_ff1_0 = _grad_w(h_ffn0, da0, True, "grad_w_ff1_0")
    gw_ff2_0 = _grad_w(r0, dy0, False, "grad_w_ff2_0")
    tok = put_g("ffn0", dict(ff1_0=gw_ff1_0, ff2_0=owners(gw_ff2_0)))
    dcat, dz0, dg_mix_post0 = _outproj_bwd(dx1, mix0, gain(norm_mix_post, 0, tok), w_out_e, "outproj_bwd_even")
    gw_out_e = jnp.concatenate([_grad_w(oa, dz0, False, "grad_w_out_even_a"),
                                _grad_w(ob, dz0, False, "grad_w_out_even_b")], axis=0)
    dqfig, d_lb, d_a_norm = _hgrn2_bwd(proj, dcat, states, lb_table, a_norm, "hgrn2_bwd")
    duv, d_ln_g, d_ln_b, d_ws, d_bias_t = _gmlp_bwd(proj, dcat, b_ln_g, b_ln_b, ws, bias_t, "gmlp_bwd")
    dproj = jnp.concatenate([dqfig, duv], axis=1)
    gw_in_e = _grad_w(h_mix0, dproj, True, "grad_w_in_even")
    tok = put_g("mix0", dict(out_e=owners(gw_out_e), in_e=gw_in_e))
    dx0, dg_mix_pre0 = _inproj_bwd(dproj, g_in_e, dx1, x0, gain(norm_mix_pre, 0, tok), "inproj_bwd_even")

    layers = lambda a, b: jnp.concatenate([a, b], axis=0)
    return (dx0, loss_part, layers(dg_mix_pre0, dg_mix_pre1), layers(dg_mix_post0, dg_mix_post1),
            layers(dg_ffn_pre0, dg_ffn_pre1), layers(dg_ffn_post0, dg_ffn_post1),
            d_lb, d_a_norm, d_ln_g, d_ln_b, d_ws, d_bias_t)
```

```python
import functools
import math

import jax
import jax.numpy as jnp
from jax import lax
from jax.experimental import pallas as pl
from jax.experimental.pallas import tpu as pltpu

F32 = jnp.float32
BF16 = jnp.bfloat16
MESH = pl.DeviceIdType.MESH

N_DEV = 8
D_MODEL = 1024
SEQ = 2048
EPS = 1e-6
A_WIDTH = 512
A_HEADS = 4
HEAD_A = 128
B_WIDTH = 512
B_GROUPS = 4
B_CHUNK = 128
C_HEADS = 16
C_HEAD_DIM = 64
C_ROT_HALF = 8
ROPE_THETA = 500000.0
C_DILATIONS = (1, 4, 16)
C_BLOCK = 128
D_FF = 4096
EVEN_IN = 3072
ODD_IN = 3072

ADAM_LR = 0.001
ADAM_B1 = 0.9
ADAM_B2 = 0.999
ADAM_EPS = 1e-08
ADAM_WD = 0.01
ADAM_STEP = 10

LANES = 128
SUBLANES = 8
ROW_TILE = 512
PROJ_TILE = 1024
PROJ_COLS = 768
MERGE_TILE = 256
SUB_CHUNK = 16
HGRN_BLOCK = 256
NEG = -1e30
VMEM_LIMIT = 56 * 1024 * 1024


def _params(sem):
    return pltpu.CompilerParams(dimension_semantics=sem, vmem_limit_bytes=VMEM_LIMIT)


def _dot(a, b):
    return jnp.dot(a, b, preferred_element_type=F32)


def _dot_nt(a, b):
    return lax.dot_general(a, b, (((1,), (1,)), ((), ())), preferred_element_type=F32)


def _dot_tn(a, b):
    return lax.dot_general(a, b, (((0,), (0,)), ((), ())), preferred_element_type=F32)


def _rms(x, g):
    r = lax.rsqrt(jnp.mean(x * x, axis=-1, keepdims=True) + EPS)
    return x * r * g


def _rms_bwd(x, g, dy):
    r = lax.rsqrt(jnp.mean(x * x, axis=-1, keepdims=True) + EPS)
    dyg = dy * g
    dx = r * dyg - x * (r * r * r) * jnp.mean(x * dyg, axis=-1, keepdims=True)
    return dx, dy * x * r


def _rows8(v):
    return v.reshape(v.shape[0] // SUBLANES, SUBLANES, v.shape[1]).sum(axis=0)


def _sigmoid(x):
    return 1.0 / (1.0 + jnp.exp(-x))


def _gelu(x):
    return 0.5 * x * (1.0 + jnp.tanh(math.sqrt(2.0 / math.pi) * (x + 0.044715 * (x * x * x))))


def _acc_rows8(ref, val, first):
    @pl.when(first)
    def _():
        ref[...] = val

    @pl.when(jnp.logical_not(first))
    def _():
        ref[...] += val


def _my_slot():
    return 4 * lax.axis_index("x") + 2 * lax.axis_index("y") + lax.axis_index("c")


def _peer(r):
    x, y, c = lax.axis_index("x"), lax.axis_index("y"), lax.axis_index("c")
    px = 1 - x if (r >> 2) & 1 else x
    py = 1 - y if (r >> 1) & 1 else y
    pc = 1 - c if r & 1 else c
    return (px, py, pc), 4 * px + 2 * py + pc


def _exchange(arrays, gather, name):
    n = len(arrays)
    if gather:
        out_shape = [jax.ShapeDtypeStruct((N_DEV,) + a.shape, a.dtype) for a in arrays]
    else:
        out_shape = [jax.ShapeDtypeStruct(a.shape, a.dtype) for a in arrays]

    def body(*refs):
        ins, outs = refs[:n], refs[n:2 * n]
        send_sems, recv_sems, local_sems = refs[2 * n:]
        me = _my_slot()
        local, remote = [], []
        for k in range(n):
            src = ins[k] if gather else ins[k].at[me]
            local.append(pltpu.make_async_copy(src, outs[k].at[me], local_sems.at[k]))
            for r in range(1, N_DEV):
                peer, slot = _peer(r)
                src = ins[k] if gather else ins[k].at[slot]
                remote.append((pltpu.make_async_remote_copy(
                    src_ref=src, dst_ref=outs[k].at[me], send_sem=send_sems.at[k, r - 1],
                    recv_sem=recv_sems.at[k, r - 1], device_id=peer, device_id_type=MESH), k, r, slot))
        for cp in local:
            cp.start()
        for cp, _, _, _ in remote:
            cp.start()
        for cp, k, r, slot in remote:
            pltpu.make_async_remote_copy(
                src_ref=outs[k].at[slot], dst_ref=outs[k].at[slot], send_sem=send_sems.at[k, r - 1],
                recv_sem=recv_sems.at[k, r - 1], device_id=_peer(r)[0], device_id_type=MESH).wait_recv()
        for cp, _, _, _ in remote:
            cp.wait_send()
        for cp in local:
            cp.wait()

    any_spec = pl.BlockSpec(memory_space=pl.ANY)
    return pl.pallas_call(
        body, name=name, out_shape=out_shape,
        in_specs=[any_spec] * n, out_specs=[any_spec] * n,
        scratch_shapes=[pltpu.SemaphoreType.DMA((n, N_DEV - 1)), pltpu.SemaphoreType.DMA((n, N_DEV - 1)),
                        pltpu.SemaphoreType.DMA((n,))],
        compiler_params=pltpu.CompilerParams(has_side_effects=True),
    )(*arrays)


HBM_SPEC = pl.BlockSpec(memory_space=pltpu.HBM)
SEM_SPEC = pl.BlockSpec(memory_space=pltpu.SEMAPHORE)
SPLIT_EFFECT = pltpu.SideEffectType.DATAFLOW_SIDE_EFFECTING


def _split_copies(land_ref, src_ref, send_sem, recv_sem):
    me = _my_slot()
    copies = []
    for r in range(1, N_DEV):
        peer, slot = _peer(r)
        src = _slot(land_ref, me) if src_ref is None else _slot(src_ref, slot)
        copies.append(pltpu.make_async_remote_copy(
            src_ref=src, dst_ref=_slot(land_ref, me), send_sem=send_sem, recv_sem=recv_sem,
            device_id=peer, device_id_type=MESH))
    return copies


def _slot(ref, s):
    if len(ref.shape) == 2:
        c = ref.shape[1] // N_DEV
        return ref.at[:, pl.ds(pl.multiple_of(s * c, LANES), c)]
    return ref.at[s]


def _exchange_start(lands, sources, name):
    n = len(lands)
    given = [s for s in sources if s is not None]
    arrays = list(lands) + given

    def body(*refs):
        land_refs, src_refs = refs[:n], list(refs[n:n + len(given)])
        sems = refs[len(arrays):len(arrays) + 2 * n]
        token = refs[-1]
        for k in range(n):
            src_ref = None if sources[k] is None else src_refs.pop(0)
            for copy in _split_copies(land_refs[k], src_ref, sems[k], sems[n + k]):
                copy.start()
        token[...] = jnp.zeros_like(token)

    outs = pl.pallas_call(
        body, name=name,
        out_shape=(pltpu.SemaphoreType.DMA(()),) * (2 * n) + tuple(pltpu.HBM(a.shape, a.dtype) for a in arrays)
        + (jax.ShapeDtypeStruct((SUBLANES, LANES), F32),),
        in_specs=[HBM_SPEC] * len(arrays),
        out_specs=(SEM_SPEC,) * (2 * n) + (HBM_SPEC,) * len(arrays) + (pl.BlockSpec(memory_space=pltpu.VMEM),),
        input_output_aliases={i: 2 * n + i for i in range(len(arrays))},
        compiler_params=pltpu.CompilerParams(has_side_effects=SPLIT_EFFECT),
    )(*[pltpu.with_memory_space_constraint(a, pltpu.HBM) for a in arrays])
    return list(outs[:n]), list(outs[n:2 * n]), list(outs[2 * n:3 * n]), list(outs[3 * n:-1]), outs[-1]


def _exchange_wait(lands, sources, send_sems, recv_sems, after, name):
    n = len(lands)
    given = [s for s in sources if s is not None]
    arrays = list(lands) + given

    def body(*refs):
        land_refs, src_refs = refs[:n], list(refs[n:n + len(given)])
        sems = refs[len(arrays):len(arrays) + 2 * n]
        for i in range(n):
            src_ref = None if sources[i] is None else src_refs.pop(0)
            copies = _split_copies(land_refs[i], src_ref, sems[i], sems[n + i])
            for copy in copies:
                copy.wait_recv()
            for copy in copies:
                copy.wait_send()

    outs = pl.pallas_call(
        body, name=name, out_shape=tuple(pltpu.HBM(a.shape, a.dtype) for a in arrays),
        in_specs=[HBM_SPEC] * len(arrays) + [SEM_SPEC] * (2 * n) + [pl.BlockSpec(memory_space=pl.ANY)],
        out_specs=(HBM_SPEC,) * len(arrays),
        input_output_aliases={i: i for i in range(len(arrays))},
        compiler_params=pltpu.CompilerParams(has_side_effects=SPLIT_EFFECT),
    )(*arrays, *send_sems, *recv_sems, after)
    return list(outs[:n])


def _place_own(a, me, name, own_block):
    shape = a.shape[1:] if own_block else a.shape
    cols = shape[-1]
    a3 = a.reshape((N_DEV if own_block else 1, -1, cols))
    rows = a3.shape[1]
    tr = min(rows, 512)

    def body(me_ref, a_ref, o_ref):
        o_ref[...] = a_ref[...].astype(BF16)

    grid_spec = pltpu.PrefetchScalarGridSpec(
        num_scalar_prefetch=1, grid=(rows // tr,),
        in_specs=[pl.BlockSpec((1, tr, cols), lambda i, me_ref: (me_ref[0] if own_block else 0, i, 0))],
        out_specs=pl.BlockSpec((1, tr, cols), lambda i, me_ref: (me_ref[0], i, 0)))
    out = pl.pallas_call(
        body, name=name, grid_spec=grid_spec, out_shape=jax.ShapeDtypeStruct((N_DEV, rows, cols), BF16),
        compiler_params=_params(("arbitrary",)),
    )(me, a3)
    return out.reshape((N_DEV,) + shape)


def _place_own_columns(a, me, name):
    rows, cols = a.shape
    tr = min(rows, 512)

    def body(me_ref, a_ref, o_ref):
        o_ref[...] = a_ref[...].astype(BF16)

    grid_spec = pltpu.PrefetchScalarGridSpec(
        num_scalar_prefetch=1, grid=(rows // tr,),
        in_specs=[pl.BlockSpec((tr, cols), lambda i, me_ref: (i, 0))],
        out_specs=pl.BlockSpec((tr, cols), lambda i, me_ref: (i, me_ref[0])))
    return pl.pallas_call(
        body, name=name, grid_spec=grid_spec, out_shape=jax.ShapeDtypeStruct((rows, N_DEV * cols), BF16),
        compiler_params=_params(("arbitrary",)),
    )(me, a)


def _rope_tables(positions):
    inv = ROPE_THETA ** (-jnp.arange(C_ROT_HALF, dtype=F32) / C_ROT_HALF)
    ang = positions.reshape(-1)[:, None].astype(F32) * inv
    cos, sin = jnp.cos(ang), jnp.sin(ang)
    t = ang.shape[0]
    ones = jnp.ones((t, C_HEAD_DIM - 2 * C_ROT_HALF), F32)
    c_head = jnp.concatenate([cos, cos, ones], axis=1)
    s_head = jnp.concatenate([-sin, sin, 0.0 * ones], axis=1)
    return jnp.concatenate([c_head, c_head], axis=1), jnp.concatenate([s_head, s_head], axis=1)


def _swap_halves(x):
    lane = lax.broadcasted_iota(jnp.int32, x.shape, 1) % C_HEAD_DIM
    return jnp.where(lane < C_ROT_HALF, pltpu.roll(x, LANES - C_ROT_HALF, 1), pltpu.roll(x, C_ROT_HALF, 1))


def _norm_inproj(x, g, w, name):
    t = x.shape[0]
    n = w.shape[1]
    tm, tn = PROJ_TILE, PROJ_COLS

    def body(x_ref, g_ref, w_ref, o_ref, h_ref):
        @pl.when(pl.program_id(1) == 0)
        def _():
            h_ref[...] = _rms(x_ref[...], g_ref[...]).astype(BF16)

        o_ref[...] = _dot(h_ref[...], w_ref[...])

    return pl.pallas_call(
        body, name=name, grid=(t // tm, n // tn),
        in_specs=[pl.BlockSpec((tm, D_MODEL), lambda i, j: (i, 0)), pl.BlockSpec((1, D_MODEL), lambda i, j: (0, 0)),
                  pl.BlockSpec((D_MODEL, tn), lambda i, j: (0, j))],
        out_specs=[pl.BlockSpec((tm, tn), lambda i, j: (i, j)), pl.BlockSpec((tm, D_MODEL), lambda i, j: (i, 0))],
        out_shape=[jax.ShapeDtypeStruct((t, n), F32), jax.ShapeDtypeStruct((t, D_MODEL), BF16)],
        compiler_params=_params(("parallel", "arbitrary")),
    )(x, g, w)


def _dilated_specs(tm, width, col_of):
    per_seq = SEQ // tm
    specs = []
    for d in C_DILATIONS:
        specs.append(pl.BlockSpec(
            (1, d, tm // d, width), lambda i, *rest: (i // per_seq, 0, i % per_seq, col_of(*rest))))
    return specs


def _dilated_shapes(n_seq, cols, dtype):
    return [jax.ShapeDtypeStruct((n_seq, d, SEQ // d, cols), dtype) for d in C_DILATIONS]


def _store_dilated(src_ref, out_refs, dtype):
    groups, tm, _ = src_ref.shape
    for d, o_ref in zip(C_DILATIONS, out_refs):
        for r in range(d):
            rows = pl.ds(r, tm // d, stride=d) if d > 1 else slice(None)
            for p in range(groups):
                o_ref[0, r, :, p * LANES:(p + 1) * LANES] = src_ref.at[p][rows, :].astype(dtype)


def _load_dilated(in_ref, d, dst_ref):
    groups, tm, _ = dst_ref.shape
    for r in range(d):
        rows = pl.ds(r, tm // d, stride=d)
        for p in range(groups):
            dst_ref.at[p][rows, :] = in_ref[0, r, :, p * LANES:(p + 1) * LANES].astype(F32)


def _norm_inproj_rope(x, g, w, rope, name):
    t = x.shape[0]
    n = w.shape[1]
    tm, nb = PROJ_TILE, PROJ_COLS

    def body(x_ref, g_ref, w_ref, c_ref, s_ref, o1_ref, o4_ref, o16_ref, h_ref, tile_ref):
        j = pl.program_id(1)

        @pl.when(j == 0)
        def _():
            h_ref[...] = _rms(x_ref[...], g_ref[...]).astype(BF16)

        acc = _dot(h_ref[...], w_ref[...])
        for p in range(nb // LANES):
            blk = acc[:, p * LANES:(p + 1) * LANES]
            roped = blk * c_ref[...] + _swap_halves(blk) * s_ref[...]
            is_qk = (j * (nb // LANES) + p) < 2 * (D_MODEL // LANES)
            tile_ref[p] = jnp.where(is_qk, roped, blk)
        _store_dilated(tile_ref, (o1_ref, o4_ref, o16_ref), BF16)

    return pl.pallas_call(
        body, name=name, grid=(t // tm, n // nb),
        in_specs=[pl.BlockSpec((tm, D_MODEL), lambda i, j: (i, 0)), pl.BlockSpec((1, D_MODEL), lambda i, j: (0, 0)),
                  pl.BlockSpec((D_MODEL, nb), lambda i, j: (0, j)),
                  pl.BlockSpec((tm, LANES), lambda i, j: (i, 0)), pl.BlockSpec((tm, LANES), lambda i, j: (i, 0))],
        out_specs=_dilated_specs(tm, nb, lambda j: j) + [pl.BlockSpec((tm, D_MODEL), lambda i, j: (i, 0))],
        out_shape=_dilated_shapes(t // SEQ, n, BF16) + [jax.ShapeDtypeStruct((t, D_MODEL), BF16)],
        scratch_shapes=[pltpu.VMEM((nb // LANES, tm, LANES), F32)],
        compiler_params=_params(("parallel", "arbitrary")),
    )(x, g, w, *rope)


def _outproj(parts, w, x, g, name):
    t = x.shape[0]
    tm = ROW_TILE
    n = len(parts)
    widths = [p.shape[1] for p in parts]

    def body(*refs):
        p_refs = refs[:n]
        w_ref, x_ref, g_ref, xo_ref, mix_ref = refs[n:]
        mix = None
        off = 0
        for p_ref, wd in zip(p_refs, widths):
            term = _dot(p_ref[...].astype(BF16), w_ref[off:off + wd, :])
            mix = term if mix is None else mix + term
            off += wd
        mix_ref[...] = mix
        xo_ref[...] = x_ref[...] + _rms(mix, g_ref[...])

    row = lambda i: (i, 0)
    return pl.pallas_call(
        body, name=name, grid=(t // tm,),
        in_specs=[pl.BlockSpec((tm, wd), row) for wd in widths] + [
            pl.BlockSpec((sum(widths), D_MODEL), lambda i: (0, 0)),
            pl.BlockSpec((tm, D_MODEL), row), pl.BlockSpec((1, D_MODEL), lambda i: (0, 0))],
        out_specs=[pl.BlockSpec((tm, D_MODEL), row)] * 2,
        out_shape=[jax.ShapeDtypeStruct((t, D_MODEL), F32)] * 2,
        compiler_params=_params(("parallel",)),
    )(*parts, w, x, g)


def _outproj_bwd(dx, mix, g, w, name):
    t = dx.shape[0]
    tm = ROW_TILE
    k = w.shape[0]

    def body(dx_ref, mix_ref, g_ref, w_ref, dcat_ref, dz_ref, dg_ref):
        dz, dgr = _rms_bwd(mix_ref[...], g_ref[...], dx_ref[...])
        dzb = dz.astype(BF16)
        dz_ref[...] = dzb
        dcat_ref[...] = _dot_nt(dzb, w_ref[...])
        _acc_rows8(dg_ref, _rows8(dgr), pl.program_id(0) == 0)

    row = lambda i: (i, 0)
    return pl.pallas_call(
        body, name=name, grid=(t // tm,),
        in_specs=[pl.BlockSpec((tm, D_MODEL), row), pl.BlockSpec((tm, D_MODEL), row),
                  pl.BlockSpec((1, D_MODEL), lambda i: (0, 0)), pl.BlockSpec((k, D_MODEL), lambda i: (0, 0))],
        out_specs=[pl.BlockSpec((tm, k), row), pl.BlockSpec((tm, D_MODEL), row),
                   pl.BlockSpec((SUBLANES, D_MODEL), lambda i: (0, 0))],
        out_shape=[jax.ShapeDtypeStruct((t, k), F32), jax.ShapeDtypeStruct((t, D_MODEL), BF16),
                   jax.ShapeDtypeStruct((SUBLANES, D_MODEL), F32)],
        compiler_params=_params(("arbitrary",)),
    )(dx, mix, g, w)


def _outproj_bwd_attn(dx, mix, g, w, out, name):
    t = dx.shape[0]
    tm = MERGE_TILE

    def body(dx_ref, mix_ref, g_ref, w_ref, out_ref, do1, do4, do16, dl1, dl4, dl16, dz_ref, dg_ref, tile_ref):
        dz, dgr = _rms_bwd(mix_ref[...], g_ref[...], dx_ref[...])
        dzb = dz.astype(BF16)
        dz_ref[...] = dzb
        _acc_rows8(dg_ref, _rows8(dgr), pl.program_id(0) == 0)
        dout = _dot_nt(dzb, w_ref[...])
        for p in range(LANE_GROUPS):
            tile_ref[p] = dout[:, p * LANES:(p + 1) * LANES]
        _store_dilated(tile_ref, (do1, do4, do16), BF16)
        r = lax.broadcasted_iota(jnp.int32, (LANES, LANES), 0) // C_HEAD_DIM
        c = lax.broadcasted_iota(jnp.int32, (LANES, LANES), 1) // C_HEAD_DIM
        same_head = (r == c).astype(F32)
        prod = dout * out_ref[...]
        for p in range(LANE_GROUPS):
            tile_ref[p] = jnp.dot(prod[:, p * LANES:(p + 1) * LANES], same_head, precision=lax.Precision.HIGHEST,
                                  preferred_element_type=F32)
        _store_dilated(tile_ref, (dl1, dl4, dl16), F32)

    row = lambda i: (i, 0)
    n_seq = t // SEQ
    return pl.pallas_call(
        body, name=name, grid=(t // tm,),
        in_specs=[pl.BlockSpec((tm, D_MODEL), row), pl.BlockSpec((tm, D_MODEL), row),
                  pl.BlockSpec((1, D_MODEL), lambda i: (0, 0)), pl.BlockSpec((D_MODEL, D_MODEL), lambda i: (0, 0)),
                  pl.BlockSpec((tm, D_MODEL), row)],
        out_specs=_dilated_specs(tm, D_MODEL, lambda: 0) * 2 + [
            pl.BlockSpec((tm, D_MODEL), row), pl.BlockSpec((SUBLANES, D_MODEL), lambda i: (0, 0))],
        out_shape=_dilated_shapes(n_seq, D_MODEL, BF16) + _dilated_shapes(n_seq, D_MODEL, F32) + [
            jax.ShapeDtypeStruct((t, D_MODEL), BF16), jax.ShapeDtypeStruct((SUBLANES, D_MODEL), F32)],
        scratch_shapes=[pltpu.VMEM((LANE_GROUPS, tm, LANES), F32)],
        compiler_params=_params(("arbitrary",)),
    )(dx, mix, g, w, out)


def _inproj_bwd(dproj, w, dx, x, g, name):
    t = x.shape[0]
    n = w.shape[1]
    tm = ROW_TILE

    def body(dp_ref, w_ref, dx_ref, x_ref, g_ref, o_ref, dg_ref):
        dxn, dgr = _rms_bwd(x_ref[...], g_ref[...], _dot_nt(dp_ref[...], w_ref[...]))
        o_ref[...] = dx_ref[...] + dxn
        _acc_rows8(dg_ref, _rows8(dgr), pl.program_id(0) == 0)

    row = lambda i: (i, 0)
    return pl.pallas_call(
        body, name=name, grid=(t // tm,),
        in_specs=[pl.BlockSpec((tm, n), row), pl.BlockSpec((D_MODEL, n), lambda i: (0, 0)),
                  pl.BlockSpec((tm, D_MODEL), row), pl.BlockSpec((tm, D_MODEL), row),
                  pl.BlockSpec((1, D_MODEL), lambda i: (0, 0))],
        out_specs=[pl.BlockSpec((tm, D_MODEL), row), pl.BlockSpec((SUBLANES, D_MODEL), lambda i: (0, 0))],
        out_shape=[jax.ShapeDtypeStruct((t, D_MODEL), F32), jax.ShapeDtypeStruct((SUBLANES, D_MODEL), F32)],
        compiler_params=_params(("arbitrary",)),
    )(dproj, w, dx, x, g)


def _grad_w(a, b, col_blocks, name):
    t, k = a.shape
    n = b.shape[1]
    tk = min(k, 1024)
    per_owner = n // N_DEV
    tn = 2 * per_owner if col_blocks else min(n, 1024)

    def body(a_ref, b_ref, o_ref, at_ref):
        @pl.when(pl.program_id(1) == 0)
        def _():
            for c in range(t // ROW_TILE):
                rows = slice(c * ROW_TILE, (c + 1) * ROW_TILE)
                at_ref[:, rows] = a_ref[rows, :].T

        res = _dot(at_ref[...], b_ref[...]).astype(BF16)
        if col_blocks:
            o_ref[0] = res[:, :per_owner]
            o_ref[1] = res[:, per_owner:]
        else:
            o_ref[...] = res

    if col_blocks:
        out_spec = pl.BlockSpec((2, tk, per_owner), lambda i, j: (j, i, 0))
        out_shape = jax.ShapeDtypeStruct((N_DEV, k, per_owner), BF16)
    else:
        out_spec = pl.BlockSpec((tk, tn), lambda i, j: (i, j))
        out_shape = jax.ShapeDtypeStruct((k, n), BF16)
    return pl.pallas_call(
        body, name=name, grid=(k // tk, n // tn),
        in_specs=[pl.BlockSpec((t, tk), lambda i, j: (0, i)), pl.BlockSpec((t, tn), lambda i, j: (0, j))],
        out_specs=out_spec, out_shape=out_shape,
        scratch_shapes=[pltpu.VMEM((tk, t), BF16)],
        compiler_params=_params(("parallel", "arbitrary")),
    )(a, b)


FF_BLOCK = D_FF // N_DEV
FF_STEP = 1024
FF_STEPS = D_FF // FF_STEP


def _ffn_fwd(x, g_pre, w1, w2, g_post, name):
    t = x.shape[0]
    tm = ROW_TILE

    def body(x_ref, gp_ref, w1_ref, w2_ref, gq_ref, xo_ref, y_ref, h_ref):
        j = pl.program_id(1)

        @pl.when(j == 0)
        def _():
            h_ref[...] = _rms(x_ref[...], gp_ref[...]).astype(BF16)

        a = _dot(h_ref[...], w1_ref[...])
        r = jnp.square(jnp.maximum(a, 0.0)).astype(BF16)
        term = _dot(r, w2_ref[...])

        @pl.when(j == 0)
        def _():
            y_ref[...] = term

        @pl.when(j > 0)
        def _():
            y_ref[...] += term

        @pl.when(j == FF_STEPS - 1)
        def _():
            xo_ref[...] = x_ref[...] + _rms(y_ref[...], gq_ref[...])

    row = lambda i, j: (i, 0)
    vec = pl.BlockSpec((1, D_MODEL), lambda i, j: (0, 0))
    return pl.pallas_call(
        body, name=name, grid=(t // tm, FF_STEPS),
        in_specs=[pl.BlockSpec((tm, D_MODEL), row), vec,
                  pl.BlockSpec((D_MODEL, FF_STEP), lambda i, j: (0, j)),
                  pl.BlockSpec((FF_STEP, D_MODEL), lambda i, j: (j, 0)), vec],
        out_specs=[pl.BlockSpec((tm, D_MODEL), row)] * 3,
        out_shape=[jax.ShapeDtypeStruct((t, D_MODEL), F32), jax.ShapeDtypeStruct((t, D_MODEL), F32),
                   jax.ShapeDtypeStruct((t, D_MODEL), BF16)],
        compiler_params=_params(("parallel", "arbitrary")),
    )(x, g_pre, w1, w2, g_post)


def _ffn_bwd(dxo, x, y, h, g_pre, w1, w2, g_post, name):
    t = x.shape[0]
    tm = ROW_TILE

    def body(dxo_ref, x_ref, y_ref, h_ref, gp_ref, w1_ref, w2_ref, gq_ref,
             dx_ref, dy_ref, r_ref, da_ref, dgp_ref, dgq_ref, acc_ref):
        i, j = pl.program_id(0), pl.program_id(1)

        @pl.when(j == 0)
        def _():
            dy, dgr = _rms_bwd(y_ref[...], gq_ref[...], dxo_ref[...])
            dy_ref[...] = dy.astype(BF16)
            _acc_rows8(dgq_ref, _rows8(dgr), i == 0)

        a = _dot(h_ref[...], w1_ref[...])
        ra = jnp.maximum(a, 0.0)
        r_ref[...] = jnp.square(ra).astype(BF16)
        dr = _dot_nt(dy_ref[...], w2_ref[...])
        da = (dr * (2.0 * ra)).astype(BF16)
        da_ref[...] = da
        term = _dot_nt(da, w1_ref[...])

        @pl.when(j == 0)
        def _():
            acc_ref[...] = term

        @pl.when(j > 0)
        def _():
            acc_ref[...] += term

        @pl.when(j == FF_STEPS - 1)
        def _():
            dxn, dgr = _rms_bwd(x_ref[...], gp_ref[...], acc_ref[...])
            dx_ref[...] = dxo_ref[...] + dxn
            _acc_rows8(dgp_ref, _rows8(dgr), i == 0)

    row = lambda i, j: (i, 0)
    vec = pl.BlockSpec((1, D_MODEL), lambda i, j: (0, 0))
    acc8 = pl.BlockSpec((SUBLANES, D_MODEL), lambda i, j: (0, 0))
    return pl.pallas_call(
        body, name=name, grid=(t // tm, FF_STEPS),
        in_specs=[pl.BlockSpec((tm, D_MODEL), row)] * 4 + [
            vec, pl.BlockSpec((D_MODEL, FF_STEP), lambda i, j: (0, j)),
            pl.BlockSpec((FF_STEP, D_MODEL), lambda i, j: (j, 0)), vec],
        out_specs=[pl.BlockSpec((tm, D_MODEL), row), pl.BlockSpec((tm, D_MODEL), row),
                   pl.BlockSpec((tm, FF_STEP), lambda i, j: (i, j)), pl.BlockSpec((tm, FF_STEP), lambda i, j: (i, j)),
                   acc8, acc8],
        out_shape=[jax.ShapeDtypeStruct((t, D_MODEL), F32), jax.ShapeDtypeStruct((t, D_MODEL), BF16),
                   jax.ShapeDtypeStruct((t, D_FF), BF16), jax.ShapeDtypeStruct((t, D_FF), BF16),
                   jax.ShapeDtypeStruct((SUBLANES, D_MODEL), F32), jax.ShapeDtypeStruct((SUBLANES, D_MODEL), F32)],
        scratch_shapes=[pltpu.VMEM((tm, D_MODEL), F32)],
        compiler_params=_params(("arbitrary", "arbitrary")),
    )(dxo, x, y, h, g_pre, w1, w2, g_post)


def _lower_bound(table):
    e = jnp.exp(table - jnp.max(table, axis=0, keepdims=True))
    return e[0:1, :] / jnp.sum(e, axis=0, keepdims=True)


def _hgrn2_block(q_ref, f_ref, lb):
    tb = f_ref.shape[0]
    sig = _sigmoid(f_ref[...])
    f = lb + (1.0 - lb) * sig
    qraw = q_ref[...]
    sq = _sigmoid(qraw)
    r = lax.broadcasted_iota(jnp.int32, (tb, tb), 0)
    c = lax.broadcasted_iota(jnp.int32, (tb, tb), 1)
    same = (r // SUB_CHUNK) == (c // SUB_CHUNK)
    logf = jnp.log(f)
    gsum = jnp.dot((same & (c <= r)).astype(F32), logf, precision=lax.Precision.HIGHEST, preferred_element_type=F32)
    glast = jnp.dot(same.astype(F32), logf, precision=lax.Precision.HIGHEST, preferred_element_type=F32)
    return dict(sig=sig, f=f, kk=1.0 - f, qraw=qraw, sq=sq, qs=qraw * sq, gsum=gsum,
                eg=jnp.exp(gsum), ekd=jnp.exp(glast - gsum), a=jnp.exp(glast))


def _head_sums(x):
    parts = [jnp.broadcast_to(jnp.sum(x[:, h * HEAD_A:(h + 1) * HEAD_A], axis=1, keepdims=True), (x.shape[0], HEAD_A))
             for h in range(A_HEADS)]
    return jnp.concatenate(parts, axis=1)


def _lag_decay(g, j, row):
    back = jnp.exp(jnp.where(row >= j, g - pltpu.roll(g, j, 0), NEG))
    ahead = jnp.exp(jnp.where(row < SUB_CHUNK - j, pltpu.roll(g, SUB_CHUNK - j, 0) - g, NEG))
    return back, ahead


def _hgrn2_intra(g, kk, qs, v):
    row = lax.broadcasted_iota(jnp.int32, g.shape, 0)
    o = _head_sums(qs * kk) * v
    for j in range(1, SUB_CHUNK):
        decay = jnp.exp(jnp.where(row >= j, g - pltpu.roll(g, j, 0), NEG))
        o = o + _head_sums(qs * pltpu.roll(kk, j, 0) * decay) * pltpu.roll(v, j, 0)
    return o


def _hgrn2_intra_bwd(g, kk, qs, v, do):
    row = lax.broadcasted_iota(jnp.int32, g.shape, 0)
    dsc = _head_sums(do * v)
    dqs, dkk, dv = dsc * kk, dsc * qs, _head_sums(qs * kk) * do
    for j in range(1, SUB_CHUNK):
        back, ahead = _lag_decay(g, j, row)
        dqs = dqs + _head_sums(do * pltpu.roll(v, j, 0)) * pltpu.roll(kk, j, 0) * back
        q_up, do_up = pltpu.roll(qs, SUB_CHUNK - j, 0), pltpu.roll(do, SUB_CHUNK - j, 0)
        dkk = dkk + _head_sums(do_up * v) * q_up * ahead
        dv = dv + _head_sums(q_up * kk * ahead) * do_up
    return dqs, dkk, dv


def _hgrn2_fwd(proj, lb_table, a_norm, name):
    t = proj.shape[0]
    tb = HGRN_BLOCK
    n_tb = SEQ // tb
    n_seq = t // SEQ
    n_sub = tb // SUB_CHUNK

    def body(q_ref, f_ref, i_ref, g_ref, lbt_ref, an_ref, o_ref, pre_ref, sts_ref, st_ref,
             gs_ref, kk_ref, qs_ref, eg_ref, ekd_ref, a_ref):
        @pl.when(pl.program_id(1) == 0)
        def _():
            st_ref[...] = jnp.zeros_like(st_ref)

        an = an_ref[...]
        blk = _hgrn2_block(q_ref, f_ref, _lower_bound(lbt_ref[...]))
        for ref, key in ((gs_ref, "gsum"), (kk_ref, "kk"), (qs_ref, "qs"), (eg_ref, "eg"), (ekd_ref, "ekd"), (a_ref, "a")):
            ref[...] = blk[key]

        def step(c, carry):
            rows = pl.ds(pl.multiple_of(c * SUB_CHUNK, SUB_CHUNK), SUB_CHUNK)
            kk, qs, v = kk_ref[rows, :], qs_ref[rows, :], i_ref[rows, :]
            o = _hgrn2_intra(gs_ref[rows, :], kk, qs, v)
            qg, kd, vb = (qs * eg_ref[rows, :]).astype(BF16), (kk * ekd_ref[rows, :]).astype(BF16), v.astype(BF16)
            for h in range(A_HEADS):
                lanes = slice(h * HEAD_A, (h + 1) * HEAD_A)
                st = st_ref[h]
                sts_ref[0, c, h] = st
                o_h = o[:, lanes] + _dot_nt(qg[:, lanes], st.astype(BF16))
                st_ref[h] = st * a_ref[rows, lanes][0:1] + _dot_tn(vb[:, lanes], kd[:, lanes])
                pre_ref[rows, lanes] = o_h
                graw = g_ref[rows, lanes]
                o_ref[rows, lanes] = (_rms(o_h, an[:, lanes]) * (graw * _sigmoid(graw))).astype(BF16)
            return carry

        lax.fori_loop(0, n_sub, step, 0)

    def col(k):
        return pl.BlockSpec((tb, A_WIDTH), lambda b, s, k=k: (b * n_tb + s, k))

    out_rows = pl.BlockSpec((tb, A_WIDTH), lambda b, s: (b * n_tb + s, 0))
    return pl.pallas_call(
        body, name=name, grid=(n_seq, n_tb),
        in_specs=[col(0), col(1), col(2), col(3),
                  pl.BlockSpec((3, A_WIDTH), lambda b, s: (0, 0)), pl.BlockSpec((1, A_WIDTH), lambda b, s: (0, 0))],
        out_specs=[out_rows, out_rows,
                   pl.BlockSpec((1, n_sub, A_HEADS, HEAD_A, HEAD_A), lambda b, s: (b * n_tb + s, 0, 0, 0, 0))],
        out_shape=[jax.ShapeDtypeStruct((t, A_WIDTH), BF16), jax.ShapeDtypeStruct((t, A_WIDTH), F32),
                   jax.ShapeDtypeStruct((n_seq * n_tb, n_sub, A_HEADS, HEAD_A, HEAD_A), F32)],
        scratch_shapes=[pltpu.VMEM((A_HEADS, HEAD_A, HEAD_A), F32)] + [pltpu.VMEM((tb, A_WIDTH), F32)] * 6,
        compiler_params=_params(("parallel", "arbitrary")),
    )(proj, proj, proj, proj, lb_table, a_norm)


def _hgrn2_bwd(proj, dcat, pre, states, lb_table, a_norm, name):
    t = proj.shape[0]
    tb = HGRN_BLOCK
    n_tb = SEQ // tb
    n_seq = t // SEQ
    n_sub = tb // SUB_CHUNK

    def body(q_ref, f_ref, i_ref, g_ref, do_ref, pre_ref, sts_ref, lbt_ref, an_ref, dp_ref, dlb_ref, dan_ref, dst_ref,
             gs_ref, kk_ref, qs_ref, eg_ref, ekd_ref, a_ref, dpre_ref, dlf_ref, dqs_ref, dkk_ref):
        b, s = pl.program_id(0), pl.program_id(1)

        @pl.when(s == 0)
        def _():
            dst_ref[...] = jnp.zeros_like(dst_ref)

        @pl.when((b == 0) & (s == 0))
        def _():
            dlb_ref[...] = jnp.zeros_like(dlb_ref)
            dan_ref[...] = jnp.zeros_like(dan_ref)

        lb = _lower_bound(lbt_ref[...])
        an = an_ref[...]
        heads = [slice(h * HEAD_A, (h + 1) * HEAD_A) for h in range(A_HEADS)]
        blk = _hgrn2_block(q_ref, f_ref, lb)
        for ref, key in ((gs_ref, "gsum"), (kk_ref, "kk"), (qs_ref, "qs"), (eg_ref, "eg"), (ekd_ref, "ekd"), (a_ref, "a")):
            ref[...] = blk[key]
        for h, lanes in enumerate(heads):
            graw, o = g_ref[:, lanes], pre_ref[:, lanes]
            sg = _sigmoid(graw)
            dout = do_ref[:, lanes]
            d_o, dgr = _rms_bwd(o, an[:, lanes], dout * (graw * sg))
            dan_ref[0:1, lanes] += jnp.sum(dgr, axis=0, keepdims=True)
            dp_ref[:, 3 * A_WIDTH + h * HEAD_A:3 * A_WIDTH + (h + 1) * HEAD_A] = (
                dout * _rms(o, an[:, lanes]) * (sg * (1.0 + graw * (1.0 - sg)))).astype(BF16)
            dpre_ref[:, lanes] = d_o

        tri_t = (lax.broadcasted_iota(jnp.int32, (SUB_CHUNK, SUB_CHUNK), 0)
                 <= lax.broadcasted_iota(jnp.int32, (SUB_CHUNK, SUB_CHUNK), 1)).astype(F32)

        def back(k, carry):
            c = n_sub - 1 - k
            rows = pl.ds(pl.multiple_of(c * SUB_CHUNK, SUB_CHUNK), SUB_CHUNK)
            g, kk, qs, v, d_o = gs_ref[rows, :], kk_ref[rows, :], qs_ref[rows, :], i_ref[rows, :], dpre_ref[rows, :]
            eg, ekd, a = eg_ref[rows, :], ekd_ref[rows, :], a_ref[rows, :]
            dqs, dkk, dv = _hgrn2_intra_bwd(g, kk, qs, v, d_o)
            qg_f, kd_f = qs * eg, kk * ekd
            qg, kd, vb, dob = qg_f.astype(BF16), kd_f.astype(BF16), v.astype(BF16), d_o.astype(BF16)
            dqg, dkd, da, dv_st = [], [], [], []
            for h, lanes in enumerate(heads):
                st, dst = sts_ref[0, c, h], dst_ref[h]
                dstb = dst.astype(BF16)
                dqg.append(_dot(dob[:, lanes], st.astype(BF16)))
                dv_st.append(_dot_nt(kd[:, lanes], dstb))
                dkd.append(_dot(vb[:, lanes], dstb))
                da.append(jnp.broadcast_to(jnp.sum(dst * st, axis=0, keepdims=True), (SUB_CHUNK, HEAD_A)))
                dst_ref[h] = dst * a[0:1, lanes] + _dot_tn(dob[:, lanes], qg[:, lanes])
            dqg, dkd, da, dv_st = [jnp.concatenate(p, axis=1) for p in (dqg, dkd, da, dv_st)]
            d_gsum = qs * dqs - kk * dkk + dqg * qg_f - dkd * kd_f
            d_glast = jnp.sum(dkd * kd_f, axis=0, keepdims=True) + da * a
            dlf_ref[rows, :] = jnp.dot(tri_t, d_gsum, precision=lax.Precision.HIGHEST,
                                       preferred_element_type=F32) + d_glast
            dqs_ref[rows, :] = dqs + dqg * eg
            dkk_ref[rows, :] = dkk + dkd * ekd
            dp_ref[rows, 2 * A_WIDTH:3 * A_WIDTH] = (dv + dv_st).astype(BF16)
            return carry

        lax.fori_loop(0, n_sub, back, 0)
        sig, sq, qraw = blk["sig"], blk["sq"], blk["qraw"]
        d_f = dlf_ref[...] / blk["f"] - dkk_ref[...]
        dlb_ref[0:1, :] += jnp.sum(d_f * (1.0 - sig), axis=0, keepdims=True)
        dp_ref[:, 0:A_WIDTH] = (dqs_ref[...] * (sq * (1.0 + qraw * (1.0 - sq)))).astype(BF16)
        dp_ref[:, A_WIDTH:2 * A_WIDTH] = (d_f * (1.0 - lb) * sig * (1.0 - sig)).astype(BF16)

    def rev(s):
        return n_tb - 1 - s

    def col(k):
        return pl.BlockSpec((tb, A_WIDTH), lambda b, s, k=k: (b * n_tb + rev(s), k))

    acc8 = pl.BlockSpec((SUBLANES, A_WIDTH), lambda b, s: (0, 0))
    return pl.pallas_call(
        body, name=name, grid=(n_seq, n_tb),
        in_specs=[col(0), col(1), col(2), col(3), col(0), col(0),
                  pl.BlockSpec((1, n_sub, A_HEADS, HEAD_A, HEAD_A), lambda b, s: (b * n_tb + rev(s), 0, 0, 0, 0)),
                  pl.BlockSpec((3, A_WIDTH), lambda b, s: (0, 0)), pl.BlockSpec((1, A_WIDTH), lambda b, s: (0, 0))],
        out_specs=[pl.BlockSpec((tb, 4 * A_WIDTH), lambda b, s: (b * n_tb + rev(s), 0)), acc8, acc8],
        out_shape=[jax.ShapeDtypeStruct((t, 4 * A_WIDTH), BF16)] + [jax.ShapeDtypeStruct((SUBLANES, A_WIDTH), F32)] * 2,
        scratch_shapes=[pltpu.VMEM((A_HEADS, HEAD_A, HEAD_A), F32)] + [pltpu.VMEM((tb, A_WIDTH), F32)] * 10,
        compiler_params=_params(("arbitrary", "arbitrary")),
    )(proj, proj, proj, proj, dcat, pre, states, lb_table, a_norm)


GMLP_ROWS = 512


def _gmlp_chunk(ub, vb, ln_g, ln_b, ws, bias):
    u = [_gelu(a) for a in ub]
    v = [_gelu(a) for a in vb]
    mu = sum(jnp.sum(a, axis=-1, keepdims=True) for a in v) * (1.0 / B_WIDTH)
    cen = [a - mu for a in v]
    var = sum(jnp.sum(a * a, axis=-1, keepdims=True) for a in cen) * (1.0 / B_WIDTH)
    inv = lax.rsqrt(var + EPS)
    r = lax.broadcasted_iota(jnp.int32, (B_CHUNK, B_CHUNK), 0)
    c = lax.broadcasted_iota(jnp.int32, (B_CHUNK, B_CHUNK), 1)
    outs = []
    for g in range(B_GROUPS):
        vn = (cen[g] * inv * ln_g[g] + ln_b[g]).astype(BF16)
        wm = jnp.where(c <= r, ws[g], 0.0).astype(BF16)
        outs.append(u[g] * (_dot(wm, vn) + bias[g]))
    return outs


def _lane_groups(ref, rows=slice(None)):
    return [ref[rows, g * LANES:(g + 1) * LANES] for g in range(B_GROUPS)]


def _gmlp_fwd(proj, ln_g, ln_b, ws, bias_t, name):
    t = proj.shape[0]
    tm = GMLP_ROWS

    def body(u_ref, v_ref, lg_ref, lb_ref, ws_ref, bt_ref, o_ref):
        for ch in range(tm // B_CHUNK):
            rows = slice(ch * B_CHUNK, (ch + 1) * B_CHUNK)
            outs = _gmlp_chunk(_lane_groups(u_ref, rows), _lane_groups(v_ref, rows), _lane_groups(lg_ref),
                               _lane_groups(lb_ref), [ws_ref[g] for g in range(B_GROUPS)],
                               [bt_ref[:, g:g + 1] for g in range(B_GROUPS)])
            for g in range(B_GROUPS):
                o_ref[rows, g * LANES:(g + 1) * LANES] = outs[g].astype(BF16)

    vec = pl.BlockSpec((1, B_WIDTH), lambda i: (0, 0))
    return pl.pallas_call(
        body, name=name, grid=(t // tm,),
        in_specs=[pl.BlockSpec((tm, B_WIDTH), lambda i: (i, 4)), pl.BlockSpec((tm, B_WIDTH), lambda i: (i, 5)), vec, vec,
                  pl.BlockSpec((B_GROUPS, B_CHUNK, B_CHUNK), lambda i: (0, 0, 0)),
                  pl.BlockSpec((B_CHUNK, B_GROUPS), lambda i: (0, 0))],
        out_specs=pl.BlockSpec((tm, B_WIDTH), lambda i: (i, 0)),
        out_shape=jax.ShapeDtypeStruct((t, B_WIDTH), BF16),
        compiler_params=_params(("parallel",)),
    )(proj, proj, ln_g, ln_b, ws, bias_t)


def _gmlp_bwd(proj, dcat, ln_g, ln_b, ws, bias_t, name):
    t = proj.shape[0]
    tm = GMLP_ROWS

    def body(u_ref, v_ref, do_ref, lg_ref, lb_ref, ws_ref, bt_ref, duv_ref, dlg_ref, dlb_ref, dws_ref, dbt_ref):
        @pl.when(pl.program_id(0) == 0)
        def _():
            dlg_ref[...] = jnp.zeros_like(dlg_ref)
            dlb_ref[...] = jnp.zeros_like(dlb_ref)
            dws_ref[...] = jnp.zeros_like(dws_ref)
            dbt_ref[...] = jnp.zeros_like(dbt_ref)

        for ch in range(tm // B_CHUNK):
            rows = slice(ch * B_CHUNK, (ch + 1) * B_CHUNK)
            _, vjp = jax.vjp(
                _gmlp_chunk, _lane_groups(u_ref, rows), _lane_groups(v_ref, rows), _lane_groups(lg_ref),
                _lane_groups(lb_ref), [ws_ref[g] for g in range(B_GROUPS)],
                [bt_ref[:, g:g + 1] for g in range(B_GROUPS)])
            du, dv, dlg, dlb, dw, dbt = vjp(_lane_groups(do_ref, rows))
            for g in range(B_GROUPS):
                lanes = slice(g * LANES, (g + 1) * LANES)
                duv_ref[rows, lanes] = du[g].astype(BF16)
                duv_ref[rows, B_WIDTH + g * LANES:B_WIDTH + (g + 1) * LANES] = dv[g].astype(BF16)
                dlg_ref[0:1, lanes] += dlg[g]
                dlb_ref[0:1, lanes] += dlb[g]
                dws_ref[g] += dw[g]
                dbt_ref[:, g:g + 1] += dbt[g]

    vec = pl.BlockSpec((1, B_WIDTH), lambda i: (0, 0))
    acc8 = pl.BlockSpec((SUBLANES, B_WIDTH), lambda i: (0, 0))
    ws_spec = pl.BlockSpec((B_GROUPS, B_CHUNK, B_CHUNK), lambda i: (0, 0, 0))
    bt_spec = pl.BlockSpec((B_CHUNK, B_GROUPS), lambda i: (0, 0))
    return pl.pallas_call(
        body, name=name, grid=(t // tm,),
        in_specs=[pl.BlockSpec((tm, B_WIDTH), lambda i: (i, 4)), pl.BlockSpec((tm, B_WIDTH), lambda i: (i, 5)),
                  pl.BlockSpec((tm, B_WIDTH), lambda i: (i, 1)), vec, vec, ws_spec, bt_spec],
        out_specs=[pl.BlockSpec((tm, 2 * B_WIDTH), lambda i: (i, 0)), acc8, acc8, ws_spec, bt_spec],
        out_shape=[jax.ShapeDtypeStruct((t, 2 * B_WIDTH), BF16), jax.ShapeDtypeStruct((SUBLANES, B_WIDTH), F32),
                   jax.ShapeDtypeStruct((SUBLANES, B_WIDTH), F32),
                   jax.ShapeDtypeStruct((B_GROUPS, B_CHUNK, B_CHUNK), F32),
                   jax.ShapeDtypeStruct((B_CHUNK, B_GROUPS), F32)],
        compiler_params=_params(("arbitrary",)),
    )(proj, proj, dcat, ln_g, ln_b, ws, bias_t)


QK_SCALE = 1.0 / math.sqrt(C_HEAD_DIM)
ATTN_UNROLL = 4
LANE_GROUPS = D_MODEL // LANES
Q_BLOCKS = SEQ // C_BLOCK


def _attn_window(i, d):
    sub_blocks = Q_BLOCKS // d
    q0 = pl.multiple_of(i * C_BLOCK, C_BLOCK)
    k0 = pl.multiple_of(jnp.maximum(i - 1, 0) * C_BLOCK, C_BLOCK)
    key = k0 + lax.broadcasted_iota(jnp.int32, (C_BLOCK, 2 * C_BLOCK), 1)
    dist = (q0 + lax.broadcasted_iota(jnp.int32, (C_BLOCK, 2 * C_BLOCK), 0)) - key
    own_subsequence = (key >= q0) | (i % sub_blocks > 0)
    return pl.ds(q0, C_BLOCK), pl.ds(k0, 2 * C_BLOCK), (dist >= 0) & (dist <= C_BLOCK) & own_subsequence


def _head_masks():
    lane = lax.broadcasted_iota(jnp.int32, (C_BLOCK, LANES), 1)
    return [lane < C_HEAD_DIM, lane >= C_HEAD_DIM]


def _flat_spec(col_of):
    return pl.BlockSpec((1, SEQ, LANES), lambda b, g: (b, 0, col_of(g)))


def _attn_branch_fwd(qkv, name):
    n_seq, d, l, _ = qkv.shape
    flat = qkv.reshape(n_seq, SEQ, ODD_IN)

    def body(q_ref, k_ref, v_ref, o_ref, m_ref, l_ref):
        heads = _head_masks()

        def block(i, carry):
            rows, keys, mask = _attn_window(i, d)
            q, k, v = q_ref[0, rows, :], k_ref[0, keys, :], v_ref[0, keys, :]
            res = []
            for hm in heads:
                s = jnp.where(mask, _dot_nt(jnp.where(hm, q, 0), k) * QK_SCALE, NEG)
                m = jnp.max(s, axis=-1, keepdims=True)
                p = jnp.exp(s - m)
                res.append((_dot(p.astype(BF16), v), m, jnp.sum(p, axis=-1, keepdims=True)))
            o_ref[0, rows, :] = jnp.where(heads[0], res[0][0], res[1][0])
            m_ref[0, rows, :] = jnp.where(heads[0], res[0][1], res[1][1])
            l_ref[0, rows, :] = jnp.where(heads[0], res[0][2], res[1][2])
            return carry

        lax.fori_loop(0, Q_BLOCKS, block, 0, unroll=ATTN_UNROLL)

    outs = pl.pallas_call(
        body, name=name, grid=(n_seq, LANE_GROUPS),
        in_specs=[_flat_spec(lambda g: g), _flat_spec(lambda g: LANE_GROUPS + g),
                  _flat_spec(lambda g: 2 * LANE_GROUPS + g)],
        out_specs=[_flat_spec(lambda g: g)] * 3,
        out_shape=[jax.ShapeDtypeStruct((n_seq, SEQ, D_MODEL), F32)] * 3,
        compiler_params=_params(("parallel", "parallel")),
    )(flat, flat, flat)
    return [o.reshape(n_seq, d, l, D_MODEL) for o in outs]


def _attn_merge(branches, name):
    n_seq = branches[0][0].shape[0]
    t = n_seq * SEQ
    tm = MERGE_TILE

    def body(*refs):
        ins = refs[:9]
        o_ref, ob_ref, lse1_ref, lse4_ref, lse16_ref = refs[9:14]
        nat = refs[14:]
        for b, d in enumerate(C_DILATIONS[1:]):
            for k in range(3):
                _load_dilated(ins[3 + 3 * b + k], d, nat[3 * b + k])
        for p in range(LANE_GROUPS):
            lanes = slice(p * LANES, (p + 1) * LANES)
            os_ = [ins[0][0, 0, :, lanes], nat[0][p], nat[3][p]]
            ms = [ins[1][0, 0, :, lanes], nat[1][p], nat[4][p]]
            ls = [ins[2][0, 0, :, lanes], nat[2][p], nat[5][p]]
            m_all = jnp.maximum(jnp.maximum(ms[0], ms[1]), ms[2])
            ws = [jnp.exp(ms[b] - m_all) for b in range(3)]
            total = ws[0] * ls[0] + ws[1] * ls[1] + ws[2] * ls[2]
            o = (ws[0] * os_[0] + ws[1] * os_[1] + ws[2] * os_[2]) / total
            o_ref[:, lanes] = o
            ob_ref[:, lanes] = o.astype(BF16)
            nat[0][p] = m_all + jnp.log(total)
        _store_dilated(nat[0], (lse1_ref, lse4_ref, lse16_ref), F32)

    row = pl.BlockSpec((tm, D_MODEL), lambda i: (i, 0))
    flat = [a for br in branches for a in br]
    in_specs = []
    for spec in _dilated_specs(tm, D_MODEL, lambda: 0):
        in_specs += [spec] * 3
    return pl.pallas_call(
        body, name=name, grid=(t // tm,), in_specs=in_specs,
        out_specs=[row, row] + _dilated_specs(tm, D_MODEL, lambda: 0),
        out_shape=[jax.ShapeDtypeStruct((t, D_MODEL), F32), jax.ShapeDtypeStruct((t, D_MODEL), BF16)]
        + _dilated_shapes(n_seq, D_MODEL, F32),
        scratch_shapes=[pltpu.VMEM((LANE_GROUPS, tm, LANES), F32)] * 6,
        compiler_params=_params(("parallel",)),
    )(*flat)


def _attn_branch_bwd(qkv, dout, lse, delta, name):
    n_seq, d, l, _ = qkv.shape
    flat = lambda a: a.reshape(n_seq, SEQ, a.shape[-1])

    def body(q_ref, k_ref, v_ref, do_ref, lse_ref, dl_ref, dq_ref, dk_ref, dv_ref):
        heads = _head_masks()
        dk_ref[...] = jnp.zeros_like(dk_ref)
        dv_ref[...] = jnp.zeros_like(dv_ref)

        def block(i, carry):
            rows, keys, mask = _attn_window(i, d)
            q, do = q_ref[0, rows, :], do_ref[0, rows, :]
            k, v = k_ref[0, keys, :], v_ref[0, keys, :]
            lse_b, dl_b = lse_ref[0, rows, :], dl_ref[0, rows, :]
            dq, dk, dv = [], None, None
            for hh, hm in enumerate(heads):
                col = slice(hh * C_HEAD_DIM, hh * C_HEAD_DIM + 1)
                qh, doh = jnp.where(hm, q, 0), jnp.where(hm, do, 0)
                s = jnp.where(mask, _dot_nt(qh, k) * QK_SCALE, NEG)
                p = jnp.exp(s - lse_b[:, col])
                ds = (p * (_dot_nt(doh, v) - dl_b[:, col]) * QK_SCALE).astype(BF16)
                dq.append(_dot(ds, k))
                dk_h, dv_h = _dot_tn(ds, qh), _dot_tn(p.astype(BF16), doh)
                dk = dk_h if dk is None else dk + dk_h
                dv = dv_h if dv is None else dv + dv_h
            dq_ref[0, rows, :] = jnp.where(heads[0], dq[0], dq[1])
            dk_ref[0, keys, :] += dk
            dv_ref[0, keys, :] += dv
            return carry

        lax.fori_loop(0, Q_BLOCKS, block, 0, unroll=ATTN_UNROLL)

    act = _flat_spec(lambda g: g)
    outs = pl.pallas_call(
        body, name=name, grid=(n_seq, LANE_GROUPS),
        in_specs=[_flat_spec(lambda g: g), _flat_spec(lambda g: LANE_GROUPS + g),
                  _flat_spec(lambda g: 2 * LANE_GROUPS + g), act, act, act],
        out_specs=[act] * 3,
        out_shape=[jax.ShapeDtypeStruct((n_seq, SEQ, D_MODEL), F32)] * 3,
        compiler_params=_params(("parallel", "parallel")),
    )(flat(qkv), flat(qkv), flat(qkv), flat(dout), flat(lse), flat(delta))
    return [o.reshape(n_seq, d, l, D_MODEL) for o in outs]


def _attn_combine_bwd(grads, rope, name):
    n_seq = grads[0][0].shape[0]
    t = n_seq * SEQ
    tm = MERGE_TILE

    def body(*refs):
        c_ref, s_ref, o_ref, nat4_ref, nat16_ref = refs[9:]
        for sec in range(3):
            _load_dilated(refs[3 + sec], 4, nat4_ref)
            _load_dilated(refs[6 + sec], 16, nat16_ref)
            for p in range(LANE_GROUPS):
                blk = refs[sec][0, 0, :, p * LANES:(p + 1) * LANES] + nat4_ref[p] + nat16_ref[p]
                if sec < 2:
                    blk = blk * c_ref[...] - _swap_halves(blk) * s_ref[...]
                o_ref[:, sec * D_MODEL + p * LANES:sec * D_MODEL + (p + 1) * LANES] = blk.astype(BF16)

    tab = pl.BlockSpec((tm, LANES), lambda i: (i, 0))
    flat = [a for br in grads for a in br]
    in_specs = []
    for spec in _dilated_specs(tm, D_MODEL, lambda: 0):
        in_specs += [spec] * 3
    return pl.pallas_call(
        body, name=name, grid=(t // tm,), in_specs=in_specs + [tab, tab],
        out_specs=pl.BlockSpec((tm, ODD_IN), lambda i: (i, 0)),
        out_shape=jax.ShapeDtypeStruct((t, ODD_IN), BF16),
        scratch_shapes=[pltpu.VMEM((LANE_GROUPS, tm, LANES), F32)] * 2,
        compiler_params=_params(("parallel",)),
    )(*flat, *rope)


def _loss_grad(y, target, name):
    t = y.shape[0]
    tm = ROW_TILE

    def body(y_ref, t_ref, d_ref, l_ref):
        diff = y_ref[...] - t_ref[...]
        d_ref[...] = diff * (1.0 / D_MODEL)
        _acc_rows8(l_ref, _rows8(diff * diff) * (0.5 / D_MODEL), pl.program_id(0) == 0)

    row = pl.BlockSpec((tm, D_MODEL), lambda i: (i, 0))
    return pl.pallas_call(
        body, name=name, grid=(t // tm,), in_specs=[row, row],
        out_specs=[row, pl.BlockSpec((SUBLANES, D_MODEL), lambda i: (0, 0))],
        out_shape=[jax.ShapeDtypeStruct((t, D_MODEL), F32), jax.ShapeDtypeStruct((SUBLANES, D_MODEL), F32)],
        compiler_params=_params(("arbitrary",)),
    )(y, target)


def _adamw(w, g, m, v):
    m = ADAM_B1 * m + (1.0 - ADAM_B1) * g
    v = ADAM_B2 * v + (1.0 - ADAM_B2) * jnp.square(g)
    m_hat = m / (1.0 - ADAM_B1 ** ADAM_STEP)
    v_hat = v / (1.0 - ADAM_B2 ** ADAM_STEP)
    delta = -ADAM_LR * (m_hat / (jnp.sqrt(v_hat) + ADAM_EPS) + ADAM_WD * w)
    return delta, m, v


def _adamw_sharded(parts, w, m, v, name):
    n_layers, rows, cols = w.shape
    tr = min(rows, 256)

    def body(*refs):
        p_refs = refs[:n_layers]
        w_ref, m_ref, v_ref, g_ref, d_ref, mo_ref, vo_ref = refs[n_layers:]
        layer = pl.program_id(0)
        g = None
        for l, p_ref in enumerate(p_refs):
            g_l = p_ref[0].astype(F32)
            for s in range(1, N_DEV):
                g_l = g_l + p_ref[s].astype(F32)
            g = g_l if g is None else jnp.where(layer == l, g_l, g)
        delta, mn, vn = _adamw(w_ref[0], g, m_ref[0], v_ref[0])
        g_ref[0] = g
        d_ref[0] = delta
        mo_ref[0] = mn
        vo_ref[0] = vn

    def part_spec(l):
        return pl.BlockSpec((N_DEV, tr, cols), lambda a, i: (0, jnp.where(a == l, i, 0), 0))

    row = pl.BlockSpec((1, tr, cols), lambda a, i: (a, i, 0))
    return pl.pallas_call(
        body, name=name, grid=(n_layers, rows // tr),
        in_specs=[part_spec(l) for l in range(n_layers)] + [row, row, row],
        out_specs=[row] * 4, out_shape=[jax.ShapeDtypeStruct(w.shape, F32)] * 4,
        compiler_params=_params(("arbitrary", "arbitrary")),
    )(*parts, w, m, v)


def _small_update(gathered, weights, moments_m, moments_v, lb_index, name):
    n = len(weights)

    def total(ref):
        acc = ref[0]
        for s in range(1, N_DEV):
            acc = acc + ref[s]
        return acc

    def body(*refs):
        g_refs = refs[:n + 1]
        w_refs, m_refs, v_refs = refs[n + 1:2 * n + 1], refs[2 * n + 1:3 * n + 1], refs[3 * n + 1:4 * n + 1]
        outs = refs[4 * n + 1:]
        loss_rows = total(g_refs[n])
        outs[0][...] = jnp.sum(jnp.sum(loss_rows, axis=1, keepdims=True), axis=0, keepdims=True)
        for k in range(n):
            part = total(g_refs[k])
            if k == lb_index:
                dlb = jnp.sum(part, axis=0, keepdims=True)
                tab = w_refs[k][...]
                e = jnp.exp(tab - jnp.max(tab, axis=0, keepdims=True))
                p = e / jnp.sum(e, axis=0, keepdims=True)
                first = lax.broadcasted_iota(jnp.int32, p.shape, 0) == 0
                grads = [(slice(None), p * (jnp.where(first, dlb, 0.0) - p[0:1, :] * dlb))]
            elif part.shape == w_refs[k].shape:
                grads = [(slice(None), part)]
            else:
                grads = [(slice(l, l + 1), jnp.sum(part[l * SUBLANES:(l + 1) * SUBLANES], axis=0, keepdims=True))
                         for l in range(w_refs[k].shape[0])]
            for rows, g in grads:
                delta, mn, vn = _adamw(w_refs[k][rows], g, m_refs[k][rows], v_refs[k][rows])
                outs[1 + 4 * k][rows] = g
                outs[2 + 4 * k][rows] = delta
                outs[3 + 4 * k][rows] = mn
                outs[4 + 4 * k][rows] = vn

    vmem = pl.BlockSpec(memory_space=pltpu.VMEM)
    out_shape = [jax.ShapeDtypeStruct((1, 1), F32)]
    for w in weights:
        out_shape += [jax.ShapeDtypeStruct(w.shape, F32)] * 4
    args = list(gathered) + list(weights) + list(moments_m) + list(moments_v)
    return pl.pallas_call(
        body, name=name, in_specs=[vmem] * len(args), out_specs=[vmem] * len(out_shape), out_shape=out_shape,
        compiler_params=pltpu.CompilerParams(vmem_limit_bytes=VMEM_LIMIT),
    )(*args)


def kernel(x, positions, norm_mix_pre, norm_mix_post, norm_ffn_pre, norm_ffn_post, w_in_even, lb_table, a_norm, b_ln_g, b_ln_b, b_ws, b_bias, w_out_even, w_in_odd, w_out_odd, w_ff1, w_ff2, loss_target, m_norm_mix_pre, m_norm_mix_post, m_norm_ffn_pre, m_norm_ffn_post, m_w_in_even, m_lb_table, m_a_norm, m_b_ln_g, m_b_ln_b, m_b_ws, m_b_bias, m_w_out_even, m_w_in_odd, m_w_out_odd, m_w_ff1, m_w_ff2, v_norm_mix_pre, v_norm_mix_post, v_norm_ffn_pre, v_norm_ffn_post, v_w_in_even, v_lb_table, v_a_norm, v_b_ln_g, v_b_ln_b, v_b_ws, v_b_bias, v_w_out_even, v_w_in_odd, v_w_out_odd, v_w_ff1, v_w_ff2):
    n_seq = x.shape[0]
    t = n_seq * SEQ
    x0 = x.reshape(t, D_MODEL)
    target = loss_target.reshape(t, D_MODEL)

    me = _my_slot().astype(jnp.int32).reshape(1)

    order = ["in_e", "out_e", "ff1_0", "ff2_0", "in_o", "out_o", "ff1_1", "ff2_1"]
    shards = dict(in_e=w_in_even[0], out_e=w_out_even[0], in_o=w_in_odd[0], out_o=w_out_odd[0],
                  ff1_0=w_ff1[0], ff1_1=w_ff1[1], ff2_0=w_ff2[0], ff2_1=w_ff2[1])
    by_columns = ("in_e", "in_o", "ff1_0", "ff1_1")
    lands = [_place_own_columns(shards[k], me, "place_" + k) if k in by_columns
             else _place_own(shards[k], me, "place_" + k, False) for k in order]
    g_send, g_recv, lands, _, g_token = _exchange_start(lands, [None] * len(order), "gather_start")

    def get_w(keys, after):
        ks = [order.index(k) for k in keys]
        return _exchange_wait([lands[k] for k in ks], [None] * len(ks), [g_send[k] for k in ks],
                              [g_recv[k] for k in ks], after, "gather_wait_" + keys[0])

    sent = {}

    def put_g(group, blocks):
        keys = list(blocks)
        own = [_place_own(blocks[k], me, "own_" + k, True) for k in keys]
        send_sems, recv_sems, own, srcs, token = _exchange_start(own, [blocks[k] for k in keys], "scatter_start_" + group)
        sent[group] = (keys, own, srcs, send_sems, recv_sems)
        return token

    rope = _rope_tables(positions)
    bias_t = b_bias[0].T
    grads = _local_step(x0, target, rope, norm_mix_pre, norm_mix_post, norm_ffn_pre, norm_ffn_post, lb_table,
                        a_norm, b_ln_g, b_ln_b, b_ws[0], bias_t, get_w, put_g, g_token)
    (dx0, loss_part, dg_mix_pre, dg_mix_post, dg_ffn_pre, dg_ffn_post, d_lb, d_a_norm, d_ln_g, d_ln_b, d_ws,
     d_bias_t) = grads

    recv = {}
    for group, (keys, own, srcs, send_sems, recv_sems) in sent.items():
        done = _exchange_wait(own, srcs, send_sems, recv_sems, dx0, "scatter_wait_" + group)
        recv.update(zip(keys, done))
    big = [("w_in_even", ["in_e"], w_in_even, m_w_in_even, v_w_in_even),
           ("w_out_even", ["out_e"], w_out_even, m_w_out_even, v_w_out_even),
           ("w_in_odd", ["in_o"], w_in_odd, m_w_in_odd, v_w_in_odd),
           ("w_out_odd", ["out_o"], w_out_odd, m_w_out_odd, v_w_out_odd),
           ("w_ff1", ["ff1_0", "ff1_1"], w_ff1, m_w_ff1, v_w_ff1), ("w_ff2", ["ff2_0", "ff2_1"], w_ff2, m_w_ff2, v_w_ff2)]
    big_out = [_adamw_sharded([recv[k] for k in keys], w, m, v, "adamw_" + nm) for nm, keys, w, m, v in big]

    small_parts = [dg_mix_pre, dg_mix_post, dg_ffn_pre, dg_ffn_post,
                   d_lb, d_a_norm, d_ln_g, d_ln_b, d_ws, d_bias_t, loss_part]
    gathered = _exchange(small_parts, True, "gather_small")
    small_w = [norm_mix_pre, norm_mix_post, norm_ffn_pre, norm_ffn_post, lb_table, a_norm, b_ln_g, b_ln_b,
               b_ws[0], bias_t]
    small_m = [m_norm_mix_pre, m_norm_mix_post, m_norm_ffn_pre, m_norm_ffn_post, m_lb_table, m_a_norm, m_b_ln_g,
               m_b_ln_b, m_b_ws[0], m_b_bias[0].T]
    small_v = [v_norm_mix_pre, v_norm_mix_post, v_norm_ffn_pre, v_norm_ffn_post, v_lb_table, v_a_norm, v_b_ln_g,
               v_b_ln_b, v_b_ws[0], v_b_bias[0].T]
    small_out = _small_update(gathered, small_w, small_m, small_v, 4, "small_update")
    loss = small_out[0].reshape(())
    small = [small_out[1 + 4 * k:5 + 4 * k] for k in range(len(small_w))]
    small[8] = [a[None] for a in small[8]]
    small[9] = [a.T[None] for a in small[9]]

    per_weight = small[0:4] + [big_out[0]] + small[4:10] + big_out[1:6]
    grad_x = dx0.reshape(x.shape)
    out = [loss, grad_x]
    for kind in range(4):
        out += [p[kind] for p in per_weight]
    return tuple(out)


def _local_step(x0, target, rope, norm_mix_pre, norm_mix_post, norm_ffn_pre, norm_ffn_post, lb_table, a_norm,
                b_ln_g, b_ln_b, ws, bias_t, get_w, put_g, token):
    def gain(a, l, tok):
        return a[l:l + 1] if tok is None else a[l:l + 1] + tok[0:1, 0:1]

    full = lambda a: a.reshape(-1, D_MODEL)
    owners = lambda a: a.reshape((N_DEV, -1) + a.shape[1:])

    (g_in_e,) = get_w(["in_e"], token)
    proj, h_mix0 = _norm_inproj(x0, gain(norm_mix_pre, 0, token), g_in_e, "inproj_even")
    oa, pre_a, states = _hgrn2_fwd(proj, lb_table, a_norm, "hgrn2_fwd")
    ob = _gmlp_fwd(proj, b_ln_g, b_ln_b, ws, bias_t, "gmlp_fwd")
    w_out_e = full(get_w(["out_e"], ob)[0])
    x1, mix0 = _outproj([oa, ob], w_out_e, x0, gain(norm_mix_post, 0, None), "outproj_even")
    w1_0, w2_0 = get_w(["ff1_0", "ff2_0"], x1)
    w2_0 = full(w2_0)
    x2, y0, h_ffn0 = _ffn_fwd(x1, gain(norm_ffn_pre, 0, None), w1_0, w2_0, gain(norm_ffn_post, 0, None), "ffn_fwd_0")
    (g_in_o,) = get_w(["in_o"], x2)
    *qkv, h_mix1 = _norm_inproj_rope(x2, gain(norm_mix_pre, 1, None), g_in_o, rope, "inproj_odd")
    branches = [_attn_branch_fwd(a, "attn_fwd_d%d" % d) for a, d in zip(qkv, C_DILATIONS)]
    attn, attn_b, *lse = _attn_merge(branches, "attn_merge")
    w_out_o = full(get_w(["out_o"], attn_b)[0])
    x3, mix1 = _outproj([attn_b], w_out_o, x2, gain(norm_mix_post, 1, None), "outproj_odd")
    w1_1, w2_1 = get_w(["ff1_1", "ff2_1"], x3)
    w2_1 = full(w2_1)
    x4, y1, h_ffn1 = _ffn_fwd(x3, gain(norm_ffn_pre, 1, None), w1_1, w2_1, gain(norm_ffn_post, 1, None), "ffn_fwd_1")

    dx4, loss_part = _loss_grad(x4, target, "loss_grad")

    dx3, dy1, r1, da1, dg_ffn_pre1, dg_ffn_post1 = _ffn_bwd(
        dx4, x3, y1, h_ffn1, gain(norm_ffn_pre, 1, None), w1_1, w2_1, gain(norm_ffn_post, 1, None), "ffn_bwd_1")
    gw_ff1_1 = _grad_w(h_ffn1, da1, True, "grad_w_ff1_1")
    gw_ff2_1 = _grad_w(r1, dy1, False, "grad_w_ff2_1")
    tok = put_g("ffn1", dict(ff1_1=gw_ff1_1, ff2_1=owners(gw_ff2_1)))
    *dattn, dz1, dg_mix_post1 = _outproj_bwd_attn(dx3, mix1, gain(norm_mix_post, 1, tok), w_out_o, attn,
                                                  "outproj_bwd_odd")
    gw_out_o = _grad_w(attn_b, dz1, False, "grad_w_out_odd")
    grads_c = [_attn_branch_bwd(qkv[b], dattn[b], lse[b], dattn[3 + b], "attn_bwd_d%d" % d)
               for b, d in enumerate(C_DILATIONS)]
    dqkv = _attn_combine_bwd(grads_c, rope, "attn_combine_bwd")
    gw_in_o = _grad_w(h_mix1, dqkv, True, "grad_w_in_odd")
    tok = put_g("mix1", dict(out_o=owners(gw_out_o), in_o=gw_in_o))
    dx2, dg_mix_pre1 = _inproj_bwd(dqkv, g_in_o, dx3, x2, gain(norm_mix_pre, 1, tok), "inproj_bwd_odd")

    dx1, dy0, r0, da0, dg_ffn_pre0, dg_ffn_post0 = _ffn_bwd(
        dx2, x1, y0, h_ffn0, gain(norm_ffn_pre, 0, None), w1_0, w2_0, gain(norm_ffn_post, 0, None), "ffn_bwd_0")
    gw_ff1_0 = _grad_w(h_ffn0, da0, True, "grad_w_ff1_0")
    gw_ff2_0 = _grad_w(r0, dy0, False, "grad_w_ff2_0")
    tok = put_g("ffn0", dict(ff1_0=gw_ff1_0, ff2_0=owners(gw_ff2_0)))
    dcat, dz0, dg_mix_post0 = _outproj_bwd(dx1, mix0, gain(norm_mix_post, 0, tok), w_out_e, "outproj_bwd_even")
    gw_out_e = jnp.concatenate([_grad_w(oa, dz0, False, "grad_w_out_even_a"),
                                _grad_w(ob, dz0, False, "grad_w_out_even_b")], axis=0)
    dqfig, d_lb, d_a_norm = _hgrn2_bwd(proj, dcat, pre_a, states, lb_table, a_norm, "hgrn2_bwd")
    duv, d_ln_g, d_ln_b, d_ws, d_bias_t = _gmlp_bwd(proj, dcat, b_ln_g, b_ln_b, ws, bias_t, "gmlp_bwd")
    dproj = jnp.concatenate([dqfig, duv], axis=1)
    gw_in_e = _grad_w(h_mix0, dproj, True, "grad_w_in_even")
    tok = put_g("mix0", dict(out_e=owners(gw_out_e), in_e=gw_in_e))
    dx0, dg_mix_pre0 = _inproj_bwd(dproj, g_in_e, dx1, x0, gain(norm_mix_pre, 0, tok), "inproj_bwd_even")

    layers = lambda a, b: jnp.concatenate([a, b], axis=0)
    return (dx0, loss_part, layers(dg_mix_pre0, dg_mix_pre1), layers(dg_mix_post0, dg_mix_post1),
            layers(dg_ffn_pre0, dg_ffn_pre1), layers(dg_ffn_post0, dg_ffn_post1),
            d_lb, d_a_norm, d_ln_g, d_ln_b, d_ws, d_bias_t)
```

```python
import functools
import math

import jax
import jax.numpy as jnp
from jax import lax
from jax.experimental import pallas as pl
from jax.experimental.pallas import tpu as pltpu

F32 = jnp.float32
BF16 = jnp.bfloat16
MESH = pl.DeviceIdType.MESH

N_DEV = 8
D_MODEL = 1024
SEQ = 2048
EPS = 1e-6
A_WIDTH = 512
A_HEADS = 4
HEAD_A = 128
B_WIDTH = 512
B_GROUPS = 4
B_CHUNK = 128
C_HEADS = 16
C_HEAD_DIM = 64
C_ROT_HALF = 8
ROPE_THETA = 500000.0
C_DILATIONS = (1, 4, 16)
C_BLOCK = 128
D_FF = 4096
EVEN_IN = 3072
ODD_IN = 3072

ADAM_LR = 0.001
ADAM_B1 = 0.9
ADAM_B2 = 0.999
ADAM_EPS = 1e-08
ADAM_WD = 0.01
ADAM_STEP = 10

LANES = 128
SUBLANES = 8
ROW_TILE = 512
PROJ_TILE = 1024
PROJ_COLS = 768
MERGE_TILE = 256
SUB_CHUNK = 16
HGRN_BLOCK = 256
NEG = -1e30
VMEM_LIMIT = 56 * 1024 * 1024


def _params(sem):
    return pltpu.CompilerParams(dimension_semantics=sem, vmem_limit_bytes=VMEM_LIMIT)


def _dot(a, b):
    return jnp.dot(a, b, preferred_element_type=F32)


def _dot_nt(a, b):
    return lax.dot_general(a, b, (((1,), (1,)), ((), ())), preferred_element_type=F32)


def _dot_tn(a, b):
    return lax.dot_general(a, b, (((0,), (0,)), ((), ())), preferred_element_type=F32)


def _rms(x, g):
    r = lax.rsqrt(jnp.mean(x * x, axis=-1, keepdims=True) + EPS)
    return x * r * g


def _rms_bwd(x, g, dy):
    r = lax.rsqrt(jnp.mean(x * x, axis=-1, keepdims=True) + EPS)
    dyg = dy * g
    dx = r * dyg - x * (r * r * r) * jnp.mean(x * dyg, axis=-1, keepdims=True)
    return dx, dy * x * r


def _rows8(v):
    return v.reshape(v.shape[0] // SUBLANES, SUBLANES, v.shape[1]).sum(axis=0)


def _sigmoid(x):
    return 1.0 / (1.0 + jnp.exp(-x))


def _gelu(x):
    return 0.5 * x * (1.0 + jnp.tanh(math.sqrt(2.0 / math.pi) * (x + 0.044715 * (x * x * x))))


def _acc_rows8(ref, val, first):
    @pl.when(first)
    def _():
        ref[...] = val

    @pl.when(jnp.logical_not(first))
    def _():
        ref[...] += val


def _my_slot():
    return 4 * lax.axis_index("x") + 2 * lax.axis_index("y") + lax.axis_index("c")


def _peer(r):
    x, y, c = lax.axis_index("x"), lax.axis_index("y"), lax.axis_index("c")
    px = 1 - x if (r >> 2) & 1 else x
    py = 1 - y if (r >> 1) & 1 else y
    pc = 1 - c if r & 1 else c
    return (px, py, pc), 4 * px + 2 * py + pc


def _exchange(arrays, gather, name):
    n = len(arrays)
    if gather:
        out_shape = [jax.ShapeDtypeStruct((N_DEV,) + a.shape, a.dtype) for a in arrays]
    else:
        out_shape = [jax.ShapeDtypeStruct(a.shape, a.dtype) for a in arrays]

    def body(*refs):
        ins, outs = refs[:n], refs[n:2 * n]
        send_sems, recv_sems, local_sems = refs[2 * n:]
        me = _my_slot()
        local, remote = [], []
        for k in range(n):
            src = ins[k] if gather else ins[k].at[me]
            local.append(pltpu.make_async_copy(src, outs[k].at[me], local_sems.at[k]))
            for r in range(1, N_DEV):
                peer, slot = _peer(r)
                src = ins[k] if gather else ins[k].at[slot]
                remote.append((pltpu.make_async_remote_copy(
                    src_ref=src, dst_ref=outs[k].at[me], send_sem=send_sems.at[k, r - 1],
                    recv_sem=recv_sems.at[k, r - 1], device_id=peer, device_id_type=MESH), k, r, slot))
        for cp in local:
            cp.start()
        for cp, _, _, _ in remote:
            cp.start()
        for cp, k, r, slot in remote:
            pltpu.make_async_remote_copy(
                src_ref=outs[k].at[slot], dst_ref=outs[k].at[slot], send_sem=send_sems.at[k, r - 1],
                recv_sem=recv_sems.at[k, r - 1], device_id=_peer(r)[0], device_id_type=MESH).wait_recv()
        for cp, _, _, _ in remote:
            cp.wait_send()
        for cp in local:
            cp.wait()

    any_spec = pl.BlockSpec(memory_space=pl.ANY)
    return pl.pallas_call(
        body, name=name, out_shape=out_shape,
        in_specs=[any_spec] * n, out_specs=[any_spec] * n,
        scratch_shapes=[pltpu.SemaphoreType.DMA((n, N_DEV - 1)), pltpu.SemaphoreType.DMA((n, N_DEV - 1)),
                        pltpu.SemaphoreType.DMA((n,))],
        compiler_params=pltpu.CompilerParams(has_side_effects=True),
    )(*arrays)


HBM_SPEC = pl.BlockSpec(memory_space=pltpu.HBM)
SEM_SPEC = pl.BlockSpec(memory_space=pltpu.SEMAPHORE)
SPLIT_EFFECT = pltpu.SideEffectType.DATAFLOW_SIDE_EFFECTING


def _split_copies(land_ref, src_ref, send_sem, recv_sem):
    me = _my_slot()
    copies = []
    for r in range(1, N_DEV):
        peer, slot = _peer(r)
        src = _slot(land_ref, me) if src_ref is None else _slot(src_ref, slot)
        copies.append(pltpu.make_async_remote_copy(
            src_ref=src, dst_ref=_slot(land_ref, me), send_sem=send_sem, recv_sem=recv_sem,
            device_id=peer, device_id_type=MESH))
    return copies


def _slot(ref, s):
    if len(ref.shape) == 2:
        c = ref.shape[1] // N_DEV
        return ref.at[:, pl.ds(pl.multiple_of(s * c, LANES), c)]
    return ref.at[s]


def _exchange_start(lands, sources, name):
    n = len(lands)
    given = [s for s in sources if s is not None]
    arrays = list(lands) + given

    def body(*refs):
        land_refs, src_refs = refs[:n], list(refs[n:n + len(given)])
        sems = refs[len(arrays):len(arrays) + 2 * n]
        token = refs[-1]
        for k in range(n):
            src_ref = None if sources[k] is None else src_refs.pop(0)
            for copy in _split_copies(land_refs[k], src_ref, sems[k], sems[n + k]):
                copy.start()
        token[...] = jnp.zeros_like(token)

    outs = pl.pallas_call(
        body, name=name,
        out_shape=(pltpu.SemaphoreType.DMA(()),) * (2 * n) + tuple(pltpu.HBM(a.shape, a.dtype) for a in arrays)
        + (jax.ShapeDtypeStruct((SUBLANES, LANES), F32),),
        in_specs=[HBM_SPEC] * len(arrays),
        out_specs=(SEM_SPEC,) * (2 * n) + (HBM_SPEC,) * len(arrays) + (pl.BlockSpec(memory_space=pltpu.VMEM),),
        input_output_aliases={i: 2 * n + i for i in range(len(arrays))},
        compiler_params=pltpu.CompilerParams(has_side_effects=SPLIT_EFFECT),
    )(*[pltpu.with_memory_space_constraint(a, pltpu.HBM) for a in arrays])
    return list(outs[:n]), list(outs[n:2 * n]), list(outs[2 * n:3 * n]), list(outs[3 * n:-1]), outs[-1]


def _exchange_wait(lands, sources, send_sems, recv_sems, after, name):
    n = len(lands)
    given = [s for s in sources if s is not None]
    arrays = list(lands) + given

    def body(*refs):
        land_refs, src_refs = refs[:n], list(refs[n:n + len(given)])
        sems = refs[len(arrays):len(arrays) + 2 * n]
        for i in range(n):
            src_ref = None if sources[i] is None else src_refs.pop(0)
            copies = _split_copies(land_refs[i], src_ref, sems[i], sems[n + i])
            for copy in copies:
                copy.wait_recv()
            for copy in copies:
                copy.wait_send()

    outs = pl.pallas_call(
        body, name=name, out_shape=tuple(pltpu.HBM(a.shape, a.dtype) for a in arrays),
        in_specs=[HBM_SPEC] * len(arrays) + [SEM_SPEC] * (2 * n) + [pl.BlockSpec(memory_space=pl.ANY)],
        out_specs=(HBM_SPEC,) * len(arrays),
        input_output_aliases={i: i for i in range(len(arrays))},
        compiler_params=pltpu.CompilerParams(has_side_effects=SPLIT_EFFECT),
    )(*arrays, *send_sems, *recv_sems, after)
    return list(outs[:n])


def _place_own(a, me, name, own_block):
    shape = a.shape[1:] if own_block else a.shape
    cols = shape[-1]
    a3 = a.reshape((N_DEV if own_block else 1, -1, cols))
    rows = a3.shape[1]
    tr = min(rows, 512)

    def body(me_ref, a_ref, o_ref):
        o_ref[...] = a_ref[...].astype(BF16)

    grid_spec = pltpu.PrefetchScalarGridSpec(
        num_scalar_prefetch=1, grid=(rows // tr,),
        in_specs=[pl.BlockSpec((1, tr, cols), lambda i, me_ref: (me_ref[0] if own_block else 0, i, 0))],
        out_specs=pl.BlockSpec((1, tr, cols), lambda i, me_ref: (me_ref[0], i, 0)))
    out = pl.pallas_call(
        body, name=name, grid_spec=grid_spec, out_shape=jax.ShapeDtypeStruct((N_DEV, rows, cols), BF16),
        compiler_params=_params(("arbitrary",)),
    )(me, a3)
    return out.reshape((N_DEV,) + shape)


def _place_own_columns(a, me, name):
    rows, cols = a.shape
    tr = min(rows, 512)

    def body(me_ref, a_ref, o_ref):
        o_ref[...] = a_ref[...].astype(BF16)

    grid_spec = pltpu.PrefetchScalarGridSpec(
        num_scalar_prefetch=1, grid=(rows // tr,),
        in_specs=[pl.BlockSpec((tr, cols), lambda i, me_ref: (i, 0))],
        out_specs=pl.BlockSpec((tr, cols), lambda i, me_ref: (i, me_ref[0])))
    return pl.pallas_call(
        body, name=name, grid_spec=grid_spec, out_shape=jax.ShapeDtypeStruct((rows, N_DEV * cols), BF16),
        compiler_params=_params(("arbitrary",)),
    )(me, a)


def _rope_tables(positions):
    inv = ROPE_THETA ** (-jnp.arange(C_ROT_HALF, dtype=F32) / C_ROT_HALF)
    ang = positions.reshape(-1)[:, None].astype(F32) * inv
    cos, sin = jnp.cos(ang), jnp.sin(ang)
    t = ang.shape[0]
    ones = jnp.ones((t, C_HEAD_DIM - 2 * C_ROT_HALF), F32)
    c_head = jnp.concatenate([cos, cos, ones], axis=1)
    s_head = jnp.concatenate([-sin, sin, 0.0 * ones], axis=1)
    return jnp.concatenate([c_head, c_head], axis=1), jnp.concatenate([s_head, s_head], axis=1)


def _swap_halves(x):
    lane = lax.broadcasted_iota(jnp.int32, x.shape, 1) % C_HEAD_DIM
    return jnp.where(lane < C_ROT_HALF, pltpu.roll(x, LANES - C_ROT_HALF, 1), pltpu.roll(x, C_ROT_HALF, 1))


def _norm_inproj(x, g, w, name):
    t = x.shape[0]
    n = w.shape[1]
    tm, tn = PROJ_TILE, PROJ_COLS

    def body(x_ref, g_ref, w_ref, o_ref, h_ref):
        @pl.when(pl.program_id(1) == 0)
        def _():
            h_ref[...] = _rms(x_ref[...], g_ref[...]).astype(BF16)

        o_ref[...] = _dot(h_ref[...], w_ref[...])

    return pl.pallas_call(
        body, name=name, grid=(t // tm, n // tn),
        in_specs=[pl.BlockSpec((tm, D_MODEL), lambda i, j: (i, 0)), pl.BlockSpec((1, D_MODEL), lambda i, j: (0, 0)),
                  pl.BlockSpec((D_MODEL, tn), lambda i, j: (0, j))],
        out_specs=[pl.BlockSpec((tm, tn), lambda i, j: (i, j)), pl.BlockSpec((tm, D_MODEL), lambda i, j: (i, 0))],
        out_shape=[jax.ShapeDtypeStruct((t, n), F32), jax.ShapeDtypeStruct((t, D_MODEL), BF16)],
        compiler_params=_params(("parallel", "arbitrary")),
    )(x, g, w)


def _dilated_specs(tm, width, col_of):
    per_seq = SEQ // tm
    specs = []
    for d in C_DILATIONS:
        specs.append(pl.BlockSpec(
            (1, d, tm // d, width), lambda i, *rest: (i // per_seq, 0, i % per_seq, col_of(*rest))))
    return specs


def _dilated_shapes(n_seq, cols, dtype):
    return [jax.ShapeDtypeStruct((n_seq, d, SEQ // d, cols), dtype) for d in C_DILATIONS]


def _store_dilated(src_ref, out_refs, dtype):
    groups, tm, _ = src_ref.shape
    for d, o_ref in zip(C_DILATIONS, out_refs):
        for r in range(d):
            rows = pl.ds(r, tm // d, stride=d) if d > 1 else slice(None)
            for p in range(groups):
                o_ref[0, r, :, p * LANES:(p + 1) * LANES] = src_ref.at[p][rows, :].astype(dtype)


def _load_dilated(in_ref, d, dst_ref):
    groups, tm, _ = dst_ref.shape
    for r in range(d):
        rows = pl.ds(r, tm // d, stride=d)
        for p in range(groups):
            dst_ref.at[p][rows, :] = in_ref[0, r, :, p * LANES:(p + 1) * LANES].astype(F32)


def _norm_inproj_rope(x, g, w, rope, name):
    t = x.shape[0]
    n = w.shape[1]
    tm, nb = PROJ_TILE, PROJ_COLS

    def body(x_ref, g_ref, w_ref, c_ref, s_ref, o1_ref, o4_ref, o16_ref, h_ref, tile_ref):
        j = pl.program_id(1)

        @pl.when(j == 0)
        def _():
            h_ref[...] = _rms(x_ref[...], g_ref[...]).astype(BF16)

        acc = _dot(h_ref[...], w_ref[...])
        for p in range(nb // LANES):
            blk = acc[:, p * LANES:(p + 1) * LANES]
            roped = blk * c_ref[...] + _swap_halves(blk) * s_ref[...]
            is_qk = (j * (nb // LANES) + p) < 2 * (D_MODEL // LANES)
            tile_ref[p] = jnp.where(is_qk, roped, blk)
        _store_dilated(tile_ref, (o1_ref, o4_ref, o16_ref), BF16)

    return pl.pallas_call(
        body, name=name, grid=(t // tm, n // nb),
        in_specs=[pl.BlockSpec((tm, D_MODEL), lambda i, j: (i, 0)), pl.BlockSpec((1, D_MODEL), lambda i, j: (0, 0)),
                  pl.BlockSpec((D_MODEL, nb), lambda i, j: (0, j)),
                  pl.BlockSpec((tm, LANES), lambda i, j: (i, 0)), pl.BlockSpec((tm, LANES), lambda i, j: (i, 0))],
        out_specs=_dilated_specs(tm, nb, lambda j: j) + [pl.BlockSpec((tm, D_MODEL), lambda i, j: (i, 0))],
        out_shape=_dilated_shapes(t // SEQ, n, BF16) + [jax.ShapeDtypeStruct((t, D_MODEL), BF16)],
        scratch_shapes=[pltpu.VMEM((nb // LANES, tm, LANES), F32)],
        compiler_params=_params(("parallel", "arbitrary")),
    )(x, g, w, *rope)


def _outproj(parts, w, x, g, name):
    t = x.shape[0]
    tm = ROW_TILE
    n = len(parts)
    widths = [p.shape[1] for p in parts]

    def body(*refs):
        p_refs = refs[:n]
        w_ref, x_ref, g_ref, xo_ref, mix_ref = refs[n:]
        mix = None
        off = 0
        for p_ref, wd in zip(p_refs, widths):
            term = _dot(p_ref[...].astype(BF16), w_ref[off:off + wd, :])
            mix = term if mix is None else mix + term
            off += wd
        mix_ref[...] = mix
        xo_ref[...] = x_ref[...] + _rms(mix, g_ref[...])

    row = lambda i: (i, 0)
    return pl.pallas_call(
        body, name=name, grid=(t // tm,),
        in_specs=[pl.BlockSpec((tm, wd), row) for wd in widths] + [
            pl.BlockSpec((sum(widths), D_MODEL), lambda i: (0, 0)),
            pl.BlockSpec((tm, D_MODEL), row), pl.BlockSpec((1, D_MODEL), lambda i: (0, 0))],
        out_specs=[pl.BlockSpec((tm, D_MODEL), row)] * 2,
        out_shape=[jax.ShapeDtypeStruct((t, D_MODEL), F32)] * 2,
        compiler_params=_params(("parallel",)),
    )(*parts, w, x, g)


def _outproj_bwd(dx, mix, g, w, name):
    t = dx.shape[0]
    tm = ROW_TILE
    k = w.shape[0]

    def body(dx_ref, mix_ref, g_ref, w_ref, dcat_ref, dz_ref, dg_ref):
        dz, dgr = _rms_bwd(mix_ref[...], g_ref[...], dx_ref[...])
        dzb = dz.astype(BF16)
        dz_ref[...] = dzb
        dcat_ref[...] = _dot_nt(dzb, w_ref[...])
        _acc_rows8(dg_ref, _rows8(dgr), pl.program_id(0) == 0)

    row = lambda i: (i, 0)
    return pl.pallas_call(
        body, name=name, grid=(t // tm,),
        in_specs=[pl.BlockSpec((tm, D_MODEL), row), pl.BlockSpec((tm, D_MODEL), row),
                  pl.BlockSpec((1, D_MODEL), lambda i: (0, 0)), pl.BlockSpec((k, D_MODEL), lambda i: (0, 0))],
        out_specs=[pl.BlockSpec((tm, k), row), pl.BlockSpec((tm, D_MODEL), row),
                   pl.BlockSpec((SUBLANES, D_MODEL), lambda i: (0, 0))],
        out_shape=[jax.ShapeDtypeStruct((t, k), F32), jax.ShapeDtypeStruct((t, D_MODEL), BF16),
                   jax.ShapeDtypeStruct((SUBLANES, D_MODEL), F32)],
        compiler_params=_params(("arbitrary",)),
    )(dx, mix, g, w)


def _outproj_bwd_attn(dx, mix, g, w, out, name):
    t = dx.shape[0]
    tm = MERGE_TILE

    def body(dx_ref, mix_ref, g_ref, w_ref, out_ref, do1, do4, do16, dl1, dl4, dl16, dz_ref, dg_ref, tile_ref):
        dz, dgr = _rms_bwd(mix_ref[...], g_ref[...], dx_ref[...])
        dzb = dz.astype(BF16)
        dz_ref[...] = dzb
        _acc_rows8(dg_ref, _rows8(dgr), pl.program_id(0) == 0)
        dout = _dot_nt(dzb, w_ref[...])
        for p in range(LANE_GROUPS):
            tile_ref[p] = dout[:, p * LANES:(p + 1) * LANES]
        _store_dilated(tile_ref, (do1, do4, do16), BF16)
        r = lax.broadcasted_iota(jnp.int32, (LANES, LANES), 0) // C_HEAD_DIM
        c = lax.broadcasted_iota(jnp.int32, (LANES, LANES), 1) // C_HEAD_DIM
        same_head = (r == c).astype(F32)
        prod = dout * out_ref[...]
        for p in range(LANE_GROUPS):
            tile_ref[p] = jnp.dot(prod[:, p * LANES:(p + 1) * LANES], same_head, precision=lax.Precision.HIGHEST,
                                  preferred_element_type=F32)
        _store_dilated(tile_ref, (dl1, dl4, dl16), F32)

    row = lambda i: (i, 0)
    n_seq = t // SEQ
    return pl.pallas_call(
        body, name=name, grid=(t // tm,),
        in_specs=[pl.BlockSpec((tm, D_MODEL), row), pl.BlockSpec((tm, D_MODEL), row),
                  pl.BlockSpec((1, D_MODEL), lambda i: (0, 0)), pl.BlockSpec((D_MODEL, D_MODEL), lambda i: (0, 0)),
                  pl.BlockSpec((tm, D_MODEL), row)],
        out_specs=_dilated_specs(tm, D_MODEL, lambda: 0) * 2 + [
            pl.BlockSpec((tm, D_MODEL), row), pl.BlockSpec((SUBLANES, D_MODEL), lambda i: (0, 0))],
        out_shape=_dilated_shapes(n_seq, D_MODEL, BF16) + _dilated_shapes(n_seq, D_MODEL, F32) + [
            jax.ShapeDtypeStruct((t, D_MODEL), BF16), jax.ShapeDtypeStruct((SUBLANES, D_MODEL), F32)],
        scratch_shapes=[pltpu.VMEM((LANE_GROUPS, tm, LANES), F32)],
        compiler_params=_params(("arbitrary",)),
    )(dx, mix, g, w, out)


def _inproj_bwd(dproj, w, dx, x, g, name):
    t = x.shape[0]
    n = w.shape[1]
    tm = ROW_TILE

    def body(dp_ref, w_ref, dx_ref, x_ref, g_ref, o_ref, dg_ref):
        dxn, dgr = _rms_bwd(x_ref[...], g_ref[...], _dot_nt(dp_ref[...], w_ref[...]))
        o_ref[...] = dx_ref[...] + dxn
        _acc_rows8(dg_ref, _rows8(dgr), pl.program_id(0) == 0)

    row = lambda i: (i, 0)
    return pl.pallas_call(
        body, name=name, grid=(t // tm,),
        in_specs=[pl.BlockSpec((tm, n), row), pl.BlockSpec((D_MODEL, n), lambda i: (0, 0)),
                  pl.BlockSpec((tm, D_MODEL), row), pl.BlockSpec((tm, D_MODEL), row),
                  pl.BlockSpec((1, D_MODEL), lambda i: (0, 0))],
        out_specs=[pl.BlockSpec((tm, D_MODEL), row), pl.BlockSpec((SUBLANES, D_MODEL), lambda i: (0, 0))],
        out_shape=[jax.ShapeDtypeStruct((t, D_MODEL), F32), jax.ShapeDtypeStruct((SUBLANES, D_MODEL), F32)],
        compiler_params=_params(("arbitrary",)),
    )(dproj, w, dx, x, g)


def _grad_w(a, b, col_blocks, name):
    t, k = a.shape
    n = b.shape[1]
    tk = min(k, 1024)
    per_owner = n // N_DEV
    tn = 2 * per_owner if col_blocks else min(n, 1024)

    def body(a_ref, b_ref, o_ref, at_ref):
        @pl.when(pl.program_id(1) == 0)
        def _():
            for c in range(t // ROW_TILE):
                rows = slice(c * ROW_TILE, (c + 1) * ROW_TILE)
                at_ref[:, rows] = a_ref[rows, :].T

        res = _dot(at_ref[...], b_ref[...]).astype(BF16)
        if col_blocks:
            o_ref[0] = res[:, :per_owner]
            o_ref[1] = res[:, per_owner:]
        else:
            o_ref[...] = res

    if col_blocks:
        out_spec = pl.BlockSpec((2, tk, per_owner), lambda i, j: (j, i, 0))
        out_shape = jax.ShapeDtypeStruct((N_DEV, k, per_owner), BF16)
    else:
        out_spec = pl.BlockSpec((tk, tn), lambda i, j: (i, j))
        out_shape = jax.ShapeDtypeStruct((k, n), BF16)
    return pl.pallas_call(
        body, name=name, grid=(k // tk, n // tn),
        in_specs=[pl.BlockSpec((t, tk), lambda i, j: (0, i)), pl.BlockSpec((t, tn), lambda i, j: (0, j))],
        out_specs=out_spec, out_shape=out_shape,
        scratch_shapes=[pltpu.VMEM((tk, t), BF16)],
        compiler_params=_params(("parallel", "arbitrary")),
    )(a, b)


FF_BLOCK = D_FF // N_DEV
FF_STEP = 1024
FF_STEPS = D_FF // FF_STEP


def _ffn_fwd(x, g_pre, w1, w2, g_post, name):
    t = x.shape[0]
    tm = ROW_TILE

    def body(x_ref, gp_ref, w1_ref, w2_ref, gq_ref, xo_ref, y_ref, h_ref):
        j = pl.program_id(1)

        @pl.when(j == 0)
        def _():
            h_ref[...] = _rms(x_ref[...], gp_ref[...]).astype(BF16)

        a = _dot(h_ref[...], w1_ref[...])
        r = jnp.square(jnp.maximum(a, 0.0)).astype(BF16)
        term = _dot(r, w2_ref[...])

        @pl.when(j == 0)
        def _():
            y_ref[...] = term

        @pl.when(j > 0)
        def _():
            y_ref[...] += term

        @pl.when(j == FF_STEPS - 1)
        def _():
            xo_ref[...] = x_ref[...] + _rms(y_ref[...], gq_ref[...])

    row = lambda i, j: (i, 0)
    vec = pl.BlockSpec((1, D_MODEL), lambda i, j: (0, 0))
    return pl.pallas_call(
        body, name=name, grid=(t // tm, FF_STEPS),
        in_specs=[pl.BlockSpec((tm, D_MODEL), row), vec,
                  pl.BlockSpec((D_MODEL, FF_STEP), lambda i, j: (0, j)),
                  pl.BlockSpec((FF_STEP, D_MODEL), lambda i, j: (j, 0)), vec],
        out_specs=[pl.BlockSpec((tm, D_MODEL), row)] * 3,
        out_shape=[jax.ShapeDtypeStruct((t, D_MODEL), F32), jax.ShapeDtypeStruct((t, D_MODEL), F32),
                   jax.ShapeDtypeStruct((t, D_MODEL), BF16)],
        compiler_params=_params(("parallel", "arbitrary")),
    )(x, g_pre, w1, w2, g_post)


def _ffn_bwd(dxo, x, y, h, g_pre, w1, w2, g_post, name):
    t = x.shape[0]
    tm = ROW_TILE

    def body(dxo_ref, x_ref, y_ref, h_ref, gp_ref, w1_ref, w2_ref, gq_ref,
             dx_ref, dy_ref, r_ref, da_ref, dgp_ref, dgq_ref, acc_ref):
        i, j = pl.program_id(0), pl.program_id(1)

        @pl.when(j == 0)
        def _():
            dy, dgr = _rms_bwd(y_ref[...], gq_ref[...], dxo_ref[...])
            dy_ref[...] = dy.astype(BF16)
            _acc_rows8(dgq_ref, _rows8(dgr), i == 0)

        a = _dot(h_ref[...], w1_ref[...])
        ra = jnp.maximum(a, 0.0)
        r_ref[...] = jnp.square(ra).astype(BF16)
        dr = _dot_nt(dy_ref[...], w2_ref[...])
        da = (dr * (2.0 * ra)).astype(BF16)
        da_ref[...] = da
        term = _dot_nt(da, w1_ref[...])

        @pl.when(j == 0)
        def _():
            acc_ref[...] = term

        @pl.when(j > 0)
        def _():
            acc_ref[...] += term

        @pl.when(j == FF_STEPS - 1)
        def _():
            dxn, dgr = _rms_bwd(x_ref[...], gp_ref[...], acc_ref[...])
            dx_ref[...] = dxo_ref[...] + dxn
            _acc_rows8(dgp_ref, _rows8(dgr), i == 0)

    row = lambda i, j: (i, 0)
    vec = pl.BlockSpec((1, D_MODEL), lambda i, j: (0, 0))
    acc8 = pl.BlockSpec((SUBLANES, D_MODEL), lambda i, j: (0, 0))
    return pl.pallas_call(
        body, name=name, grid=(t // tm, FF_STEPS),
        in_specs=[pl.BlockSpec((tm, D_MODEL), row)] * 4 + [
            vec, pl.BlockSpec((D_MODEL, FF_STEP), lambda i, j: (0, j)),
            pl.BlockSpec((FF_STEP, D_MODEL), lambda i, j: (j, 0)), vec],
        out_specs=[pl.BlockSpec((tm, D_MODEL), row), pl.BlockSpec((tm, D_MODEL), row),
                   pl.BlockSpec((tm, FF_STEP), lambda i, j: (i, j)), pl.BlockSpec((tm, FF_STEP), lambda i, j: (i, j)),
                   acc8, acc8],
        out_shape=[jax.ShapeDtypeStruct((t, D_MODEL), F32), jax.ShapeDtypeStruct((t, D_MODEL), BF16),
                   jax.ShapeDtypeStruct((t, D_FF), BF16), jax.ShapeDtypeStruct((t, D_FF), BF16),
                   jax.ShapeDtypeStruct((SUBLANES, D_MODEL), F32), jax.ShapeDtypeStruct((SUBLANES, D_MODEL), F32)],
        scratch_shapes=[pltpu.VMEM((tm, D_MODEL), F32)],
        compiler_params=_params(("arbitrary", "arbitrary")),
    )(dxo, x, y, h, g_pre, w1, w2, g_post)


def _lower_bound(table):
    e = jnp.exp(table - jnp.max(table, axis=0, keepdims=True))
    return e[0:1, :] / jnp.sum(e, axis=0, keepdims=True)


def _hgrn2_block(q_ref, f_ref, lb):
    tb = f_ref.shape[0]
    sig = _sigmoid(f_ref[...])
    f = lb + (1.0 - lb) * sig
    qraw = q_ref[...]
    sq = _sigmoid(qraw)
    r = lax.broadcasted_iota(jnp.int32, (tb, tb), 0)
    c = lax.broadcasted_iota(jnp.int32, (tb, tb), 1)
    same = (r // SUB_CHUNK) == (c // SUB_CHUNK)
    logf = jnp.log(f)
    gsum = jnp.dot((same & (c <= r)).astype(F32), logf, precision=lax.Precision.HIGHEST, preferred_element_type=F32)
    glast = jnp.dot(same.astype(F32), logf, precision=lax.Precision.HIGHEST, preferred_element_type=F32)
    return dict(sig=sig, f=f, kk=1.0 - f, qraw=qraw, sq=sq, qs=qraw * sq, gsum=gsum,
                eg=jnp.exp(gsum), ekd=jnp.exp(glast - gsum), a=jnp.exp(glast))


def _head_sums(x):
    parts = [jnp.broadcast_to(jnp.sum(x[:, h * HEAD_A:(h + 1) * HEAD_A], axis=1, keepdims=True), (x.shape[0], HEAD_A))
             for h in range(A_HEADS)]
    return jnp.concatenate(parts, axis=1)


def _lag_decay(g, j, row):
    back = jnp.exp(jnp.where(row >= j, g - pltpu.roll(g, j, 0), NEG))
    ahead = jnp.exp(jnp.where(row < SUB_CHUNK - j, pltpu.roll(g, SUB_CHUNK - j, 0) - g, NEG))
    return back, ahead


def _hgrn2_intra(g, kk, qs, v):
    row = lax.broadcasted_iota(jnp.int32, g.shape, 0)
    o = _head_sums(qs * kk) * v
    for j in range(1, SUB_CHUNK):
        decay = jnp.exp(jnp.where(row >= j, g - pltpu.roll(g, j, 0), NEG))
        o = o + _head_sums(qs * pltpu.roll(kk, j, 0) * decay) * pltpu.roll(v, j, 0)
    return o


def _hgrn2_intra_bwd(g, kk, qs, v, do):
    row = lax.broadcasted_iota(jnp.int32, g.shape, 0)
    dsc = _head_sums(do * v)
    dqs, dkk, dv = dsc * kk, dsc * qs, _head_sums(qs * kk) * do
    for j in range(1, SUB_CHUNK):
        back, ahead = _lag_decay(g, j, row)
        dqs = dqs + _head_sums(do * pltpu.roll(v, j, 0)) * pltpu.roll(kk, j, 0) * back
        q_up, do_up = pltpu.roll(qs, SUB_CHUNK - j, 0), pltpu.roll(do, SUB_CHUNK - j, 0)
        dkk = dkk + _head_sums(do_up * v) * q_up * ahead
        dv = dv + _head_sums(q_up * kk * ahead) * do_up
    return dqs, dkk, dv


def _hgrn2_fwd(proj, lb_table, a_norm, name):
    t = proj.shape[0]
    tb = HGRN_BLOCK
    n_tb = SEQ // tb
    n_seq = t // SEQ
    n_sub = tb // SUB_CHUNK

    def body(q_ref, f_ref, i_ref, g_ref, lbt_ref, an_ref, o_ref, pre_ref, sts_ref, st_ref,
             gs_ref, kk_ref, qs_ref, eg_ref, ekd_ref, a_ref):
        @pl.when(pl.program_id(1) == 0)
        def _():
            st_ref[...] = jnp.zeros_like(st_ref)

        an = an_ref[...]
        blk = _hgrn2_block(q_ref, f_ref, _lower_bound(lbt_ref[...]))
        for ref, key in ((gs_ref, "gsum"), (kk_ref, "kk"), (qs_ref, "qs"), (eg_ref, "eg"), (ekd_ref, "ekd"), (a_ref, "a")):
            ref[...] = blk[key]

        def step(c, carry):
            rows = pl.ds(pl.multiple_of(c * SUB_CHUNK, SUB_CHUNK), SUB_CHUNK)
            kk, qs, v = kk_ref[rows, :], qs_ref[rows, :], i_ref[rows, :]
            o = _hgrn2_intra(gs_ref[rows, :], kk, qs, v)
            qg, kd, vb = (qs * eg_ref[rows, :]).astype(BF16), (kk * ekd_ref[rows, :]).astype(BF16), v.astype(BF16)
            for h in range(A_HEADS):
                lanes = slice(h * HEAD_A, (h + 1) * HEAD_A)
                st = st_ref[h]
                sts_ref[0, c, h] = st
                o_h = o[:, lanes] + _dot_nt(qg[:, lanes], st.astype(BF16))
                st_ref[h] = st * a_ref[rows, lanes][0:1] + _dot_tn(vb[:, lanes], kd[:, lanes])
                pre_ref[rows, lanes] = o_h
                graw = g_ref[rows, lanes]
                o_ref[rows, lanes] = (_rms(o_h, an[:, lanes]) * (graw * _sigmoid(graw))).astype(BF16)
            return carry

        lax.fori_loop(0, n_sub, step, 0, unroll=2)

    def col(k):
        return pl.BlockSpec((tb, A_WIDTH), lambda b, s, k=k: (b * n_tb + s, k))

    out_rows = pl.BlockSpec((tb, A_WIDTH), lambda b, s: (b * n_tb + s, 0))
    return pl.pallas_call(
        body, name=name, grid=(n_seq, n_tb),
        in_specs=[col(0), col(1), col(2), col(3),
                  pl.BlockSpec((3, A_WIDTH), lambda b, s: (0, 0)), pl.BlockSpec((1, A_WIDTH), lambda b, s: (0, 0))],
        out_specs=[out_rows, out_rows,
                   pl.BlockSpec((1, n_sub, A_HEADS, HEAD_A, HEAD_A), lambda b, s: (b * n_tb + s, 0, 0, 0, 0))],
        out_shape=[jax.ShapeDtypeStruct((t, A_WIDTH), BF16), jax.ShapeDtypeStruct((t, A_WIDTH), F32),
                   jax.ShapeDtypeStruct((n_seq * n_tb, n_sub, A_HEADS, HEAD_A, HEAD_A), F32)],
        scratch_shapes=[pltpu.VMEM((A_HEADS, HEAD_A, HEAD_A), F32)] + [pltpu.VMEM((tb, A_WIDTH), F32)] * 6,
        compiler_params=_params(("parallel", "arbitrary")),
    )(proj, proj, proj, proj, lb_table, a_norm)


def _hgrn2_bwd(proj, dcat, pre, states, lb_table, a_norm, name):
    t = proj.shape[0]
    tb = HGRN_BLOCK
    n_tb = SEQ // tb
    n_seq = t // SEQ
    n_sub = tb // SUB_CHUNK

    def body(q_ref, f_ref, i_ref, g_ref, do_ref, pre_ref, sts_ref, lbt_ref, an_ref, dp_ref, dlb_ref, dan_ref, dst_ref,
             gs_ref, kk_ref, qs_ref, eg_ref, ekd_ref, a_ref, dpre_ref, dlf_ref, dqs_ref, dkk_ref):
        b, s = pl.program_id(0), pl.program_id(1)

        @pl.when(s == 0)
        def _():
            dst_ref[...] = jnp.zeros_like(dst_ref)

        @pl.when((b == 0) & (s == 0))
        def _():
            dlb_ref[...] = jnp.zeros_like(dlb_ref)
            dan_ref[...] = jnp.zeros_like(dan_ref)

        lb = _lower_bound(lbt_ref[...])
        an = an_ref[...]
        heads = [slice(h * HEAD_A, (h + 1) * HEAD_A) for h in range(A_HEADS)]
        blk = _hgrn2_block(q_ref, f_ref, lb)
        for ref, key in ((gs_ref, "gsum"), (kk_ref, "kk"), (qs_ref, "qs"), (eg_ref, "eg"), (ekd_ref, "ekd"), (a_ref, "a")):
            ref[...] = blk[key]
        for h, lanes in enumerate(heads):
            graw, o = g_ref[:, lanes], pre_ref[:, lanes]
            sg = _sigmoid(graw)
            dout = do_ref[:, lanes]
            d_o, dgr = _rms_bwd(o, an[:, lanes], dout * (graw * sg))
            dan_ref[0:1, lanes] += jnp.sum(dgr, axis=0, keepdims=True)
            dp_ref[:, 3 * A_WIDTH + h * HEAD_A:3 * A_WIDTH + (h + 1) * HEAD_A] = (
                dout * _rms(o, an[:, lanes]) * (sg * (1.0 + graw * (1.0 - sg)))).astype(BF16)
            dpre_ref[:, lanes] = d_o

        tri_t = (lax.broadcasted_iota(jnp.int32, (SUB_CHUNK, SUB_CHUNK), 0)
                 <= lax.broadcasted_iota(jnp.int32, (SUB_CHUNK, SUB_CHUNK), 1)).astype(F32)

        def back(k, carry):
            c = n_sub - 1 - k
            rows = pl.ds(pl.multiple_of(c * SUB_CHUNK, SUB_CHUNK), SUB_CHUNK)
            g, kk, qs, v, d_o = gs_ref[rows, :], kk_ref[rows, :], qs_ref[rows, :], i_ref[rows, :], dpre_ref[rows, :]
            eg, ekd, a = eg_ref[rows, :], ekd_ref[rows, :], a_ref[rows, :]
            dqs, dkk, dv = _hgrn2_intra_bwd(g, kk, qs, v, d_o)
            qg_f, kd_f = qs * eg, kk * ekd
            qg, kd, vb, dob = qg_f.astype(BF16), kd_f.astype(BF16), v.astype(BF16), d_o.astype(BF16)
            dqg, dkd, da, dv_st = [], [], [], []
            for h, lanes in enumerate(heads):
                st, dst = sts_ref[0, c, h], dst_ref[h]
                dstb = dst.astype(BF16)
                dqg.append(_dot(dob[:, lanes], st.astype(BF16)))
                dv_st.append(_dot_nt(kd[:, lanes], dstb))
                dkd.append(_dot(vb[:, lanes], dstb))
                da.append(jnp.broadcast_to(jnp.sum(dst * st, axis=0, keepdims=True), (SUB_CHUNK, HEAD_A)))
                dst_ref[h] = dst * a[0:1, lanes] + _dot_tn(dob[:, lanes], qg[:, lanes])
            dqg, dkd, da, dv_st = [jnp.concatenate(p, axis=1) for p in (dqg, dkd, da, dv_st)]
            d_gsum = qs * dqs - kk * dkk + dqg * qg_f - dkd * kd_f
            d_glast = jnp.sum(dkd * kd_f, axis=0, keepdims=True) + da * a
            dlf_ref[rows, :] = jnp.dot(tri_t, d_gsum, precision=lax.Precision.HIGHEST,
                                       preferred_element_type=F32) + d_glast
            dqs_ref[rows, :] = dqs + dqg * eg
            dkk_ref[rows, :] = dkk + dkd * ekd
            dp_ref[rows, 2 * A_WIDTH:3 * A_WIDTH] = (dv + dv_st).astype(BF16)
            return carry

        lax.fori_loop(0, n_sub, back, 0, unroll=2)
        sig, sq, qraw = blk["sig"], blk["sq"], blk["qraw"]
        d_f = dlf_ref[...] / blk["f"] - dkk_ref[...]
        dlb_ref[0:1, :] += jnp.sum(d_f * (1.0 - sig), axis=0, keepdims=True)
        dp_ref[:, 0:A_WIDTH] = (dqs_ref[...] * (sq * (1.0 + qraw * (1.0 - sq)))).astype(BF16)
        dp_ref[:, A_WIDTH:2 * A_WIDTH] = (d_f * (1.0 - lb) * sig * (1.0 - sig)).astype(BF16)

    def rev(s):
        return n_tb - 1 - s

    def col(k):
        return pl.BlockSpec((tb, A_WIDTH), lambda b, s, k=k: (b * n_tb + rev(s), k))

    acc8 = pl.BlockSpec((SUBLANES, A_WIDTH), lambda b, s: (0, 0))
    return pl.pallas_call(
        body, name=name, grid=(n_seq, n_tb),
        in_specs=[col(0), col(1), col(2), col(3), col(0), col(0),
                  pl.BlockSpec((1, n_sub, A_HEADS, HEAD_A, HEAD_A), lambda b, s: (b * n_tb + rev(s), 0, 0, 0, 0)),
                  pl.BlockSpec((3, A_WIDTH), lambda b, s: (0, 0)), pl.BlockSpec((1, A_WIDTH), lambda b, s: (0, 0))],
        out_specs=[pl.BlockSpec((tb, 4 * A_WIDTH), lambda b, s: (b * n_tb + rev(s), 0)), acc8, acc8],
        out_shape=[jax.ShapeDtypeStruct((t, 4 * A_WIDTH), BF16)] + [jax.ShapeDtypeStruct((SUBLANES, A_WIDTH), F32)] * 2,
        scratch_shapes=[pltpu.VMEM((A_HEADS, HEAD_A, HEAD_A), F32)] + [pltpu.VMEM((tb, A_WIDTH), F32)] * 10,
        compiler_params=_params(("arbitrary", "arbitrary")),
    )(proj, proj, proj, proj, dcat, pre, states, lb_table, a_norm)


GMLP_ROWS = 512


def _gmlp_chunk(ub, vb, ln_g, ln_b, ws, bias):
    u = [_gelu(a) for a in ub]
    v = [_gelu(a) for a in vb]
    mu = sum(jnp.sum(a, axis=-1, keepdims=True) for a in v) * (1.0 / B_WIDTH)
    cen = [a - mu for a in v]
    var = sum(jnp.sum(a * a, axis=-1, keepdims=True) for a in cen) * (1.0 / B_WIDTH)
    inv = lax.rsqrt(var + EPS)
    r = lax.broadcasted_iota(jnp.int32, (B_CHUNK, B_CHUNK), 0)
    c = lax.broadcasted_iota(jnp.int32, (B_CHUNK, B_CHUNK), 1)
    outs = []
    for g in range(B_GROUPS):
        vn = (cen[g] * inv * ln_g[g] + ln_b[g]).astype(BF16)
        wm = jnp.where(c <= r, ws[g], 0.0).astype(BF16)
        outs.append(u[g] * (_dot(wm, vn) + bias[g]))
    return outs


def _lane_groups(ref, rows=slice(None)):
    return [ref[rows, g * LANES:(g + 1) * LANES] for g in range(B_GROUPS)]


def _gmlp_fwd(proj, ln_g, ln_b, ws, bias_t, name):
    t = proj.shape[0]
    tm = GMLP_ROWS

    def body(u_ref, v_ref, lg_ref, lb_ref, ws_ref, bt_ref, o_ref):
        for ch in range(tm // B_CHUNK):
            rows = slice(ch * B_CHUNK, (ch + 1) * B_CHUNK)
            outs = _gmlp_chunk(_lane_groups(u_ref, rows), _lane_groups(v_ref, rows), _lane_groups(lg_ref),
                               _lane_groups(lb_ref), [ws_ref[g] for g in range(B_GROUPS)],
                               [bt_ref[:, g:g + 1] for g in range(B_GROUPS)])
            for g in range(B_GROUPS):
                o_ref[rows, g * LANES:(g + 1) * LANES] = outs[g].astype(BF16)

    vec = pl.BlockSpec((1, B_WIDTH), lambda i: (0, 0))
    return pl.pallas_call(
        body, name=name, grid=(t // tm,),
        in_specs=[pl.BlockSpec((tm, B_WIDTH), lambda i: (i, 4)), pl.BlockSpec((tm, B_WIDTH), lambda i: (i, 5)), vec, vec,
                  pl.BlockSpec((B_GROUPS, B_CHUNK, B_CHUNK), lambda i: (0, 0, 0)),
                  pl.BlockSpec((B_CHUNK, B_GROUPS), lambda i: (0, 0))],
        out_specs=pl.BlockSpec((tm, B_WIDTH), lambda i: (i, 0)),
        out_shape=jax.ShapeDtypeStruct((t, B_WIDTH), BF16),
        compiler_params=_params(("parallel",)),
    )(proj, proj, ln_g, ln_b, ws, bias_t)


def _gmlp_bwd(proj, dcat, ln_g, ln_b, ws, bias_t, name):
    t = proj.shape[0]
    tm = GMLP_ROWS

    def body(u_ref, v_ref, do_ref, lg_ref, lb_ref, ws_ref, bt_ref, duv_ref, dlg_ref, dlb_ref, dws_ref, dbt_ref):
        @pl.when(pl.program_id(0) == 0)
        def _():
            dlg_ref[...] = jnp.zeros_like(dlg_ref)
            dlb_ref[...] = jnp.zeros_like(dlb_ref)
            dws_ref[...] = jnp.zeros_like(dws_ref)
            dbt_ref[...] = jnp.zeros_like(dbt_ref)

        for ch in range(tm // B_CHUNK):
            rows = slice(ch * B_CHUNK, (ch + 1) * B_CHUNK)
            _, vjp = jax.vjp(
                _gmlp_chunk, _lane_groups(u_ref, rows), _lane_groups(v_ref, rows), _lane_groups(lg_ref),
                _lane_groups(lb_ref), [ws_ref[g] for g in range(B_GROUPS)],
                [bt_ref[:, g:g + 1] for g in range(B_GROUPS)])
            du, dv, dlg, dlb, dw, dbt = vjp(_lane_groups(do_ref, rows))
            for g in range(B_GROUPS):
                lanes = slice(g * LANES, (g + 1) * LANES)
                duv_ref[rows, lanes] = du[g].astype(BF16)
                duv_ref[rows, B_WIDTH + g * LANES:B_WIDTH + (g + 1) * LANES] = dv[g].astype(BF16)
                dlg_ref[0:1, lanes] += dlg[g]
                dlb_ref[0:1, lanes] += dlb[g]
                dws_ref[g] += dw[g]
                dbt_ref[:, g:g + 1] += dbt[g]

    vec = pl.BlockSpec((1, B_WIDTH), lambda i: (0, 0))
    acc8 = pl.BlockSpec((SUBLANES, B_WIDTH), lambda i: (0, 0))
    ws_spec = pl.BlockSpec((B_GROUPS, B_CHUNK, B_CHUNK), lambda i: (0, 0, 0))
    bt_spec = pl.BlockSpec((B_CHUNK, B_GROUPS), lambda i: (0, 0))
    return pl.pallas_call(
        body, name=name, grid=(t // tm,),
        in_specs=[pl.BlockSpec((tm, B_WIDTH), lambda i: (i, 4)), pl.BlockSpec((tm, B_WIDTH), lambda i: (i, 5)),
                  pl.BlockSpec((tm, B_WIDTH), lambda i: (i, 1)), vec, vec, ws_spec, bt_spec],
        out_specs=[pl.BlockSpec((tm, 2 * B_WIDTH), lambda i: (i, 0)), acc8, acc8, ws_spec, bt_spec],
        out_shape=[jax.ShapeDtypeStruct((t, 2 * B_WIDTH), BF16), jax.ShapeDtypeStruct((SUBLANES, B_WIDTH), F32),
                   jax.ShapeDtypeStruct((SUBLANES, B_WIDTH), F32),
                   jax.ShapeDtypeStruct((B_GROUPS, B_CHUNK, B_CHUNK), F32),
                   jax.ShapeDtypeStruct((B_CHUNK, B_GROUPS), F32)],
        compiler_params=_params(("arbitrary",)),
    )(proj, proj, dcat, ln_g, ln_b, ws, bias_t)


QK_SCALE = 1.0 / math.sqrt(C_HEAD_DIM)
ATTN_UNROLL = 8
LANE_GROUPS = D_MODEL // LANES
Q_BLOCKS = SEQ // C_BLOCK


def _attn_window(i, d):
    sub_blocks = Q_BLOCKS // d
    q0 = pl.multiple_of(i * C_BLOCK, C_BLOCK)
    k0 = pl.multiple_of(jnp.maximum(i - 1, 0) * C_BLOCK, C_BLOCK)
    key = k0 + lax.broadcasted_iota(jnp.int32, (C_BLOCK, 2 * C_BLOCK), 1)
    dist = (q0 + lax.broadcasted_iota(jnp.int32, (C_BLOCK, 2 * C_BLOCK), 0)) - key
    own_subsequence = (key >= q0) | (i % sub_blocks > 0)
    return pl.ds(q0, C_BLOCK), pl.ds(k0, 2 * C_BLOCK), (dist >= 0) & (dist <= C_BLOCK) & own_subsequence


def _head_masks():
    lane = lax.broadcasted_iota(jnp.int32, (C_BLOCK, LANES), 1)
    return [lane < C_HEAD_DIM, lane >= C_HEAD_DIM]


def _flat_spec(col_of):
    return pl.BlockSpec((1, SEQ, LANES), lambda b, g: (b, 0, col_of(g)))


def _attn_branch_fwd(qkv, name):
    n_seq, d, l, _ = qkv.shape
    flat = qkv.reshape(n_seq, SEQ, ODD_IN)

    def body(q_ref, k_ref, v_ref, o_ref, m_ref, l_ref):
        heads = _head_masks()

        def block(i, carry):
            rows, keys, mask = _attn_window(i, d)
            q, k, v = q_ref[0, rows, :], k_ref[0, keys, :], v_ref[0, keys, :]
            res = []
            for hm in heads:
                s = jnp.where(mask, _dot_nt(jnp.where(hm, q, 0), k) * QK_SCALE, NEG)
                m = jnp.max(s, axis=-1, keepdims=True)
                p = jnp.exp(s - m)
                res.append((_dot(p.astype(BF16), v), m, jnp.sum(p, axis=-1, keepdims=True)))
            o_ref[0, rows, :] = jnp.where(heads[0], res[0][0], res[1][0])
            m_ref[0, rows, :] = jnp.where(heads[0], res[0][1], res[1][1])
            l_ref[0, rows, :] = jnp.where(heads[0], res[0][2], res[1][2])
            return carry

        lax.fori_loop(0, Q_BLOCKS, block, 0, unroll=ATTN_UNROLL)

    outs = pl.pallas_call(
        body, name=name, grid=(n_seq, LANE_GROUPS),
        in_specs=[_flat_spec(lambda g: g), _flat_spec(lambda g: LANE_GROUPS + g),
                  _flat_spec(lambda g: 2 * LANE_GROUPS + g)],
        out_specs=[_flat_spec(lambda g: g)] * 3,
        out_shape=[jax.ShapeDtypeStruct((n_seq, SEQ, D_MODEL), F32)] * 3,
        compiler_params=_params(("parallel", "parallel")),
    )(flat, flat, flat)
    return [o.reshape(n_seq, d, l, D_MODEL) for o in outs]


def _attn_merge(branches, name):
    n_seq = branches[0][0].shape[0]
    t = n_seq * SEQ
    tm = MERGE_TILE

    def body(*refs):
        ins = refs[:9]
        o_ref, ob_ref, lse1_ref, lse4_ref, lse16_ref = refs[9:14]
        nat = refs[14:]
        for b, d in enumerate(C_DILATIONS[1:]):
            for k in range(3):
                _load_dilated(ins[3 + 3 * b + k], d, nat[3 * b + k])
        for p in range(LANE_GROUPS):
            lanes = slice(p * LANES, (p + 1) * LANES)
            os_ = [ins[0][0, 0, :, lanes], nat[0][p], nat[3][p]]
            ms = [ins[1][0, 0, :, lanes], nat[1][p], nat[4][p]]
            ls = [ins[2][0, 0, :, lanes], nat[2][p], nat[5][p]]
            m_all = jnp.maximum(jnp.maximum(ms[0], ms[1]), ms[2])
            ws = [jnp.exp(ms[b] - m_all) for b in range(3)]
            total = ws[0] * ls[0] + ws[1] * ls[1] + ws[2] * ls[2]
            o = (ws[0] * os_[0] + ws[1] * os_[1] + ws[2] * os_[2]) / total
            o_ref[:, lanes] = o
            ob_ref[:, lanes] = o.astype(BF16)
            nat[0][p] = m_all + jnp.log(total)
        _store_dilated(nat[0], (lse1_ref, lse4_ref, lse16_ref), F32)

    row = pl.BlockSpec((tm, D_MODEL), lambda i: (i, 0))
    flat = [a for br in branches for a in br]
    in_specs = []
    for spec in _dilated_specs(tm, D_MODEL, lambda: 0):
        in_specs += [spec] * 3
    return pl.pallas_call(
        body, name=name, grid=(t // tm,), in_specs=in_specs,
        out_specs=[row, row] + _dilated_specs(tm, D_MODEL, lambda: 0),
        out_shape=[jax.ShapeDtypeStruct((t, D_MODEL), F32), jax.ShapeDtypeStruct((t, D_MODEL), BF16)]
        + _dilated_shapes(n_seq, D_MODEL, F32),
        scratch_shapes=[pltpu.VMEM((LANE_GROUPS, tm, LANES), F32)] * 6,
        compiler_params=_params(("parallel",)),
    )(*flat)


def _attn_branch_bwd(qkv, dout, lse, delta, name):
    n_seq, d, l, _ = qkv.shape
    flat = lambda a: a.reshape(n_seq, SEQ, a.shape[-1])

    def body(q_ref, k_ref, v_ref, do_ref, lse_ref, dl_ref, dq_ref, dk_ref, dv_ref):
        heads = _head_masks()
        dk_ref[...] = jnp.zeros_like(dk_ref)
        dv_ref[...] = jnp.zeros_like(dv_ref)

        def block(i, carry):
            rows, keys, mask = _attn_window(i, d)
            q, do = q_ref[0, rows, :], do_ref[0, rows, :]
            k, v = k_ref[0, keys, :], v_ref[0, keys, :]
            lse_b, dl_b = lse_ref[0, rows, :], dl_ref[0, rows, :]
            dq, dk, dv = [], None, None
            for hh, hm in enumerate(heads):
                col = slice(hh * C_HEAD_DIM, hh * C_HEAD_DIM + 1)
                qh, doh = jnp.where(hm, q, 0), jnp.where(hm, do, 0)
                s = jnp.where(mask, _dot_nt(qh, k) * QK_SCALE, NEG)
                p = jnp.exp(s - lse_b[:, col])
                ds = (p * (_dot_nt(doh, v) - dl_b[:, col]) * QK_SCALE).astype(BF16)
                dq.append(_dot(ds, k))
                dk_h, dv_h = _dot_tn(ds, qh), _dot_tn(p.astype(BF16), doh)
                dk = dk_h if dk is None else dk + dk_h
                dv = dv_h if dv is None else dv + dv_h
            dq_ref[0, rows, :] = jnp.where(heads[0], dq[0], dq[1])
            dk_ref[0, keys, :] += dk
            dv_ref[0, keys, :] += dv
            return carry

        lax.fori_loop(0, Q_BLOCKS, block, 0, unroll=ATTN_UNROLL)

    act = _flat_spec(lambda g: g)
    outs = pl.pallas_call(
        body, name=name, grid=(n_seq, LANE_GROUPS),
        in_specs=[_flat_spec(lambda g: g), _flat_spec(lambda g: LANE_GROUPS + g),
                  _flat_spec(lambda g: 2 * LANE_GROUPS + g), act, act, act],
        out_specs=[act] * 3,
        out_shape=[jax.ShapeDtypeStruct((n_seq, SEQ, D_MODEL), F32)] * 3,
        compiler_params=_params(("parallel", "parallel")),
    )(flat(qkv), flat(qkv), flat(qkv), flat(dout), flat(lse), flat(delta))
    return [o.reshape(n_seq, d, l, D_MODEL) for o in outs]


def _attn_combine_bwd(grads, rope, name):
    n_seq = grads[0][0].shape[0]
    t = n_seq * SEQ
    tm = MERGE_TILE

    def body(*refs):
        c_ref, s_ref, o_ref, nat4_ref, nat16_ref = refs[9:]
        for sec in range(3):
            _load_dilated(refs[3 + sec], 4, nat4_ref)
            _load_dilated(refs[6 + sec], 16, nat16_ref)
            for p in range(LANE_GROUPS):
                blk = refs[sec][0, 0, :, p * LANES:(p + 1) * LANES] + nat4_ref[p] + nat16_ref[p]
                if sec < 2:
                    blk = blk * c_ref[...] - _swap_halves(blk) * s_ref[...]
                o_ref[:, sec * D_MODEL + p * LANES:sec * D_MODEL + (p + 1) * LANES] = blk.astype(BF16)

    tab = pl.BlockSpec((tm, LANES), lambda i: (i, 0))
    flat = [a for br in grads for a in br]
    in_specs = []
    for spec in _dilated_specs(tm, D_MODEL, lambda: 0):
        in_specs += [spec] * 3
    return pl.pallas_call(
        body, name=name, grid=(t // tm,), in_specs=in_specs + [tab, tab],
        out_specs=pl.BlockSpec((tm, ODD_IN), lambda i: (i, 0)),
        out_shape=jax.ShapeDtypeStruct((t, ODD_IN), BF16),
        scratch_shapes=[pltpu.VMEM((LANE_GROUPS, tm, LANES), F32)] * 2,
        compiler_params=_params(("parallel",)),
    )(*flat, *rope)


def _loss_grad(y, target, name):
    t = y.shape[0]
    tm = ROW_TILE

    def body(y_ref, t_ref, d_ref, l_ref):
        diff = y_ref[...] - t_ref[...]
        d_ref[...] = diff * (1.0 / D_MODEL)
        _acc_rows8(l_ref, _rows8(diff * diff) * (0.5 / D_MODEL), pl.program_id(0) == 0)

    row = pl.BlockSpec((tm, D_MODEL), lambda i: (i, 0))
    return pl.pallas_call(
        body, name=name, grid=(t // tm,), in_specs=[row, row],
        out_specs=[row, pl.BlockSpec((SUBLANES, D_MODEL), lambda i: (0, 0))],
        out_shape=[jax.ShapeDtypeStruct((t, D_MODEL), F32), jax.ShapeDtypeStruct((SUBLANES, D_MODEL), F32)],
        compiler_params=_params(("arbitrary",)),
    )(y, target)


def _adamw(w, g, m, v):
    m = ADAM_B1 * m + (1.0 - ADAM_B1) * g
    v = ADAM_B2 * v + (1.0 - ADAM_B2) * jnp.square(g)
    m_hat = m / (1.0 - ADAM_B1 ** ADAM_STEP)
    v_hat = v / (1.0 - ADAM_B2 ** ADAM_STEP)
    delta = -ADAM_LR * (m_hat / (jnp.sqrt(v_hat) + ADAM_EPS) + ADAM_WD * w)
    return delta, m, v


def _adamw_sharded(parts, w, m, v, name):
    n_layers, rows, cols = w.shape
    tr = min(rows, 256)

    def body(*refs):
        p_refs = refs[:n_layers]
        w_ref, m_ref, v_ref, g_ref, d_ref, mo_ref, vo_ref = refs[n_layers:]
        layer = pl.program_id(0)
        g = None
        for l, p_ref in enumerate(p_refs):
            g_l = p_ref[0].astype(F32)
            for s in range(1, N_DEV):
                g_l = g_l + p_ref[s].astype(F32)
            g = g_l if g is None else jnp.where(layer == l, g_l, g)
        delta, mn, vn = _adamw(w_ref[0], g, m_ref[0], v_ref[0])
        g_ref[0] = g
        d_ref[0] = delta
        mo_ref[0] = mn
        vo_ref[0] = vn

    def part_spec(l):
        return pl.BlockSpec((N_DEV, tr, cols), lambda a, i: (0, jnp.where(a == l, i, 0), 0))

    row = pl.BlockSpec((1, tr, cols), lambda a, i: (a, i, 0))
    return pl.pallas_call(
        body, name=name, grid=(n_layers, rows // tr),
        in_specs=[part_spec(l) for l in range(n_layers)] + [row, row, row],
        out_specs=[row] * 4, out_shape=[jax.ShapeDtypeStruct(w.shape, F32)] * 4,
        compiler_params=_params(("arbitrary", "arbitrary")),
    )(*parts, w, m, v)


def _small_update(gathered, weights, moments_m, moments_v, lb_index, name):
    n = len(weights)

    def total(ref):
        acc = ref[0]
        for s in range(1, N_DEV):
            acc = acc + ref[s]
        return acc

    def body(*refs):
        g_refs = refs[:n + 1]
        w_refs, m_refs, v_refs = refs[n + 1:2 * n + 1], refs[2 * n + 1:3 * n + 1], refs[3 * n + 1:4 * n + 1]
        outs = refs[4 * n + 1:]
        loss_rows = total(g_refs[n])
        outs[0][...] = jnp.sum(jnp.sum(loss_rows, axis=1, keepdims=True), axis=0, keepdims=True)
        for k in range(n):
            part = total(g_refs[k])
            if k == lb_index:
                dlb = jnp.sum(part, axis=0, keepdims=True)
                tab = w_refs[k][...]
                e = jnp.exp(tab - jnp.max(tab, axis=0, keepdims=True))
                p = e / jnp.sum(e, axis=0, keepdims=True)
                first = lax.broadcasted_iota(jnp.int32, p.shape, 0) == 0
                grads = [(slice(None), p * (jnp.where(first, dlb, 0.0) - p[0:1, :] * dlb))]
            elif part.shape == w_refs[k].shape:
                grads = [(slice(None), part)]
            else:
                grads = [(slice(l, l + 1), jnp.sum(part[l * SUBLANES:(l + 1) * SUBLANES], axis=0, keepdims=True))
                         for l in range(w_refs[k].shape[0])]
            for rows, g in grads:
                delta, mn, vn = _adamw(w_refs[k][rows], g, m_refs[k][rows], v_refs[k][rows])
                outs[1 + 4 * k][rows] = g
                outs[2 + 4 * k][rows] = delta
                outs[3 + 4 * k][rows] = mn
                outs[4 + 4 * k][rows] = vn

    vmem = pl.BlockSpec(memory_space=pltpu.VMEM)
    out_shape = [jax.ShapeDtypeStruct((1, 1), F32)]
    for w in weights:
        out_shape += [jax.ShapeDtypeStruct(w.shape, F32)] * 4
    args = list(gathered) + list(weights) + list(moments_m) + list(moments_v)
    return pl.pallas_call(
        body, name=name, in_specs=[vmem] * len(args), out_specs=[vmem] * len(out_shape), out_shape=out_shape,
        compiler_params=pltpu.CompilerParams(vmem_limit_bytes=VMEM_LIMIT),
    )(*args)


def kernel(x, positions, norm_mix_pre, norm_mix_post, norm_ffn_pre, norm_ffn_post, w_in_even, lb_table, a_norm, b_ln_g, b_ln_b, b_ws, b_bias, w_out_even, w_in_odd, w_out_odd, w_ff1, w_ff2, loss_target, m_norm_mix_pre, m_norm_mix_post, m_norm_ffn_pre, m_norm_ffn_post, m_w_in_even, m_lb_table, m_a_norm, m_b_ln_g, m_b_ln_b, m_b_ws, m_b_bias, m_w_out_even, m_w_in_odd, m_w_out_odd, m_w_ff1, m_w_ff2, v_norm_mix_pre, v_norm_mix_post, v_norm_ffn_pre, v_norm_ffn_post, v_w_in_even, v_lb_table, v_a_norm, v_b_ln_g, v_b_ln_b, v_b_ws, v_b_bias, v_w_out_even, v_w_in_odd, v_w_out_odd, v_w_ff1, v_w_ff2):
    n_seq = x.shape[0]
    t = n_seq * SEQ
    x0 = x.reshape(t, D_MODEL)
    target = loss_target.reshape(t, D_MODEL)

    me = _my_slot().astype(jnp.int32).reshape(1)

    order = ["in_e", "out_e", "ff1_0", "ff2_0", "in_o", "out_o", "ff1_1", "ff2_1"]
    shards = dict(in_e=w_in_even[0], out_e=w_out_even[0], in_o=w_in_odd[0], out_o=w_out_odd[0],
                  ff1_0=w_ff1[0], ff1_1=w_ff1[1], ff2_0=w_ff2[0], ff2_1=w_ff2[1])
    by_columns = ("in_e", "in_o", "ff1_0", "ff1_1")
    lands = [_place_own_columns(shards[k], me, "place_" + k) if k in by_columns
             else _place_own(shards[k], me, "place_" + k, False) for k in order]
    g_send, g_recv, lands, _, g_token = _exchange_start(lands, [None] * len(order), "gather_start")

    def get_w(keys, after):
        ks = [order.index(k) for k in keys]
        return _exchange_wait([lands[k] for k in ks], [None] * len(ks), [g_send[k] for k in ks],
                              [g_recv[k] for k in ks], after, "gather_wait_" + keys[0])

    sent = {}

    def put_g(group, blocks):
        keys = list(blocks)
        own = [_place_own(blocks[k], me, "own_" + k, True) for k in keys]
        send_sems, recv_sems, own, srcs, token = _exchange_start(own, [blocks[k] for k in keys], "scatter_start_" + group)
        sent[group] = (keys, own, srcs, send_sems, recv_sems)
        return token

    rope = _rope_tables(positions)
    bias_t = b_bias[0].T
    grads = _local_step(x0, target, rope, norm_mix_pre, norm_mix_post, norm_ffn_pre, norm_ffn_post, lb_table,
                        a_norm, b_ln_g, b_ln_b, b_ws[0], bias_t, get_w, put_g, g_token)
    (dx0, loss_part, dg_mix_pre, dg_mix_post, dg_ffn_pre, dg_ffn_post, d_lb, d_a_norm, d_ln_g, d_ln_b, d_ws,
     d_bias_t) = grads

    recv = {}
    for group, (keys, own, srcs, send_sems, recv_sems) in sent.items():
        done = _exchange_wait(own, srcs, send_sems, recv_sems, dx0, "scatter_wait_" + group)
        recv.update(zip(keys, done))
    big = [("w_in_even", ["in_e"], w_in_even, m_w_in_even, v_w_in_even),
           ("w_out_even", ["out_e"], w_out_even, m_w_out_even, v_w_out_even),
           ("w_in_odd", ["in_o"], w_in_odd, m_w_in_odd, v_w_in_odd),
           ("w_out_odd", ["out_o"], w_out_odd, m_w_out_odd, v_w_out_odd),
           ("w_ff1", ["ff1_0", "ff1_1"], w_ff1, m_w_ff1, v_w_ff1), ("w_ff2", ["ff2_0", "ff2_1"], w_ff2, m_w_ff2, v_w_ff2)]
    big_out = [_adamw_sharded([recv[k] for k in keys], w, m, v, "adamw_" + nm) for nm, keys, w, m, v in big]

    small_parts = [dg_mix_pre, dg_mix_post, dg_ffn_pre, dg_ffn_post,
                   d_lb, d_a_norm, d_ln_g, d_ln_b, d_ws, d_bias_t, loss_part]
    gathered = _exchange(small_parts, True, "gather_small")
    small_w = [norm_mix_pre, norm_mix_post, norm_ffn_pre, norm_ffn_post, lb_table, a_norm, b_ln_g, b_ln_b,
               b_ws[0], bias_t]
    small_m = [m_norm_mix_pre, m_norm_mix_post, m_norm_ffn_pre, m_norm_ffn_post, m_lb_table, m_a_norm, m_b_ln_g,
               m_b_ln_b, m_b_ws[0], m_b_bias[0].T]
    small_v = [v_norm_mix_pre, v_norm_mix_post, v_norm_ffn_pre, v_norm_ffn_post, v_lb_table, v_a_norm, v_b_ln_g,
               v_b_ln_b, v_b_ws[0], v_b_bias[0].T]
    small_out = _small_update(gathered, small_w, small_m, small_v, 4, "small_update")
    loss = small_out[0].reshape(())
    small = [small_out[1 + 4 * k:5 + 4 * k] for k in range(len(small_w))]
    small[8] = [a[None] for a in small[8]]
    small[9] = [a.T[None] for a in small[9]]

    per_weight = small[0:4] + [big_out[0]] + small[4:10] + big_out[1:6]
    grad_x = dx0.reshape(x.shape)
    out = [loss, grad_x]
    for kind in range(4):
        out += [p[kind] for p in per_weight]
    return tuple(out)


def _local_step(x0, target, rope, norm_mix_pre, norm_mix_post, norm_ffn_pre, norm_ffn_post, lb_table, a_norm,
                b_ln_g, b_ln_b, ws, bias_t, get_w, put_g, token):
    def gain(a, l, tok):
        return a[l:l + 1] if tok is None else a[l:l + 1] + tok[0:1, 0:1]

    full = lambda a: a.reshape(-1, D_MODEL)
    owners = lambda a: a.reshape((N_DEV, -1) + a.shape[1:])

    (g_in_e,) = get_w(["in_e"], token)
    proj, h_mix0 = _norm_inproj(x0, gain(norm_mix_pre, 0, token), g_in_e, "inproj_even")
    oa, pre_a, states = _hgrn2_fwd(proj, lb_table, a_norm, "hgrn2_fwd")
    ob = _gmlp_fwd(proj, b_ln_g, b_ln_b, ws, bias_t, "gmlp_fwd")
    w_out_e = full(get_w(["out_e"], ob)[0])
    x1, mix0 = _outproj([oa, ob], w_out_e, x0, gain(norm_mix_post, 0, None), "outproj_even")
    w1_0, w2_0 = get_w(["ff1_0", "ff2_0"], x1)
    w2_0 = full(w2_0)
    x2, y0, h_ffn0 = _ffn_fwd(x1, gain(norm_ffn_pre, 0, None), w1_0, w2_0, gain(norm_ffn_post, 0, None), "ffn_fwd_0")
    (g_in_o,) = get_w(["in_o"], x2)
    *qkv, h_mix1 = _norm_inproj_rope(x2, gain(norm_mix_pre, 1, None), g_in_o, rope, "inproj_odd")
    branches = [_attn_branch_fwd(a, "attn_fwd_d%d" % d) for a, d in zip(qkv, C_DILATIONS)]
    attn, attn_b, *lse = _attn_merge(branches, "attn_merge")
    w_out_o = full(get_w(["out_o"], attn_b)[0])
    x3, mix1 = _outproj([attn_b], w_out_o, x2, gain(norm_mix_post, 1, None), "outproj_odd")
    w1_1, w2_1 = get_w(["ff1_1", "ff2_1"], x3)
    w2_1 = full(w2_1)
    x4, y1, h_ffn1 = _ffn_fwd(x3, gain(norm_ffn_pre, 1, None), w1_1, w2_1, gain(norm_ffn_post, 1, None), "ffn_fwd_1")

    dx4, loss_part = _loss_grad(x4, target, "loss_grad")

    dx3, dy1, r1, da1, dg_ffn_pre1, dg_ffn_post1 = _ffn_bwd(
        dx4, x3, y1, h_ffn1, gain(norm_ffn_pre, 1, None), w1_1, w2_1, gain(norm_ffn_post, 1, None), "ffn_bwd_1")
    gw_ff1_1 = _grad_w(h_ffn1, da1, True, "grad_w_ff1_1")
    gw_ff2_1 = _grad_w(r1, dy1, False, "grad_w_ff2_1")
    tok = put_g("ffn1", dict(ff1_1=gw_ff1_1, ff2_1=owners(gw_ff2_1)))
    *dattn, dz1, dg_mix_post1 = _outproj_bwd_attn(dx3, mix1, gain(norm_mix_post, 1, tok), w_out_o, attn,
                                                  "outproj_bwd_odd")
    gw_out_o = _grad_w(attn_b, dz1, False, "grad_w_out_odd")
    grads_c = [_attn_branch_bwd(qkv[b], dattn[b], lse[b], dattn[3 + b], "attn_bwd_d%d" % d)
               for b, d in enumerate(C_DILATIONS)]
    dqkv = _attn_combine_bwd(grads_c, rope, "attn_combine_bwd")
    gw_in_o = _grad_w(h_mix1, dqkv, True, "grad_w_in_odd")
    tok = put_g("mix1", dict(out_o=owners(gw_out_o), in_o=gw_in_o))
    dx2, dg_mix_pre1 = _inproj_bwd(dqkv, g_in_o, dx3, x2, gain(norm_mix_pre, 1, tok), "inproj_bwd_odd")

    dx1, dy0, r0, da0, dg_ffn_pre0, dg_ffn_post0 = _ffn_bwd(
        dx2, x1, y0, h_ffn0, gain(norm_ffn_pre, 0, None), w1_0, w2_0, gain(norm_ffn_post, 0, None), "ffn_bwd_0")
    gw_ff1_0 = _grad_w(h_ffn0, da0, True, "grad_w_ff1_0")
    gw_ff2_0 = _grad_w(r0, dy0, False, "grad_w_ff2_0")
    tok = put_g("ffn0", dict(ff1_0=gw_ff1_0, ff2_0=owners(gw_ff2_0)))
    dcat, dz0, dg_mix_post0 = _outproj_bwd(dx1, mix0, gain(norm_mix_post, 0, tok), w_out_e, "outproj_bwd_even")
    gw_out_e = jnp.concatenate([_grad_w(oa, dz0, False, "grad_w_out_even_a"),
                                _grad_w(ob, dz0, False, "grad_w_out_even_b")], axis=0)
    dqfig, d_lb, d_a_norm = _hgrn2_bwd(proj, dcat, pre_a, states, lb_table, a_norm, "hgrn2_bwd")
    duv, d_ln_g, d_ln_b, d_ws, d_bias_t = _gmlp_bwd(proj, dcat, b_ln_g, b_ln_b, ws, bias_t, "gmlp_bwd")
    dproj = jnp.concatenate([dqfig, duv], axis=1)
    gw_in_e = _grad_w(h_mix0, dproj, True, "grad_w_in_even")
    tok = put_g("mix0", dict(out_e=owners(gw_out_e), in_e=gw_in_e))
    dx0, dg_mix_pre0 = _inproj_bwd(dproj, g_in_e, dx1, x0, gain(norm_mix_pre, 0, tok), "inproj_bwd_even")

    layers = lambda a, b: jnp.concatenate([a, b], axis=0)
    return (dx0, loss_part, layers(dg_mix_pre0, dg_mix_pre1), layers(dg_mix_post0, dg_mix_post1),
            layers(dg_ffn_pre0, dg_ffn_pre1), layers(dg_ffn_post0, dg_ffn_post1),
            d_lb, d_a_norm, d_ln_g, d_ln_b, d_ws, d_bias_t)
```

```python
import functools
import math

import jax
import jax.numpy as jnp
from jax import lax
from jax.experimental import pallas as pl
from jax.experimental.pallas import tpu as pltpu

F32 = jnp.float32
BF16 = jnp.bfloat16
MESH = pl.DeviceIdType.MESH

N_DEV = 8
D_MODEL = 1024
SEQ = 2048
EPS = 1e-6
A_WIDTH = 512
A_HEADS = 4
HEAD_A = 128
B_WIDTH = 512
B_GROUPS = 4
B_CHUNK = 128
C_HEADS = 16
C_HEAD_DIM = 64
C_ROT_HALF = 8
ROPE_THETA = 500000.0
C_DILATIONS = (1, 4, 16)
C_BLOCK = 128
D_FF = 4096
EVEN_IN = 3072
ODD_IN = 3072

ADAM_LR = 0.001
ADAM_B1 = 0.9
ADAM_B2 = 0.999
ADAM_EPS = 1e-08
ADAM_WD = 0.01
ADAM_STEP = 10

LANES = 128
SUBLANES = 8
ROW_TILE = 512
PROJ_TILE = 1024
PROJ_COLS = 768
MERGE_TILE = 256
SUB_CHUNK = 16
HGRN_BLOCK = 256
NEG = -1e30
VMEM_LIMIT = 56 * 1024 * 1024


def _params(sem):
    return pltpu.CompilerParams(dimension_semantics=sem, vmem_limit_bytes=VMEM_LIMIT)


def _dot(a, b):
    return jnp.dot(a, b, preferred_element_type=F32)


def _dot_nt(a, b):
    return lax.dot_general(a, b, (((1,), (1,)), ((), ())), preferred_element_type=F32)


def _dot_tn(a, b):
    return lax.dot_general(a, b, (((0,), (0,)), ((), ())), preferred_element_type=F32)


def _rms(x, g):
    r = lax.rsqrt(jnp.mean(x * x, axis=-1, keepdims=True) + EPS)
    return x * r * g


def _rms_bwd(x, g, dy):
    r = lax.rsqrt(jnp.mean(x * x, axis=-1, keepdims=True) + EPS)
    dyg = dy * g
    dx = r * dyg - x * (r * r * r) * jnp.mean(x * dyg, axis=-1, keepdims=True)
    return dx, dy * x * r


def _rows8(v):
    return v.reshape(v.shape[0] // SUBLANES, SUBLANES, v.shape[1]).sum(axis=0)


def _sigmoid(x):
    return 1.0 / (1.0 + jnp.exp(-x))


def _gelu(x):
    return 0.5 * x * (1.0 + jnp.tanh(math.sqrt(2.0 / math.pi) * (x + 0.044715 * (x * x * x))))


def _acc_rows8(ref, val, first):
    @pl.when(first)
    def _():
        ref[...] = val

    @pl.when(jnp.logical_not(first))
    def _():
        ref[...] += val


def _my_slot():
    return 4 * lax.axis_index("x") + 2 * lax.axis_index("y") + lax.axis_index("c")


def _peer(r):
    x, y, c = lax.axis_index("x"), lax.axis_index("y"), lax.axis_index("c")
    px = 1 - x if (r >> 2) & 1 else x
    py = 1 - y if (r >> 1) & 1 else y
    pc = 1 - c if r & 1 else c
    return (px, py, pc), 4 * px + 2 * py + pc


def _exchange(arrays, gather, name):
    n = len(arrays)
    if gather:
        out_shape = [jax.ShapeDtypeStruct((N_DEV,) + a.shape, a.dtype) for a in arrays]
    else:
        out_shape = [jax.ShapeDtypeStruct(a.shape, a.dtype) for a in arrays]

    def body(*refs):
        ins, outs = refs[:n], refs[n:2 * n]
        send_sems, recv_sems, local_sems = refs[2 * n:]
        me = _my_slot()
        local, remote = [], []
        for k in range(n):
            src = ins[k] if gather else ins[k].at[me]
            local.append(pltpu.make_async_copy(src, outs[k].at[me], local_sems.at[k]))
            for r in range(1, N_DEV):
                peer, slot = _peer(r)
                src = ins[k] if gather else ins[k].at[slot]
                remote.append((pltpu.make_async_remote_copy(
                    src_ref=src, dst_ref=outs[k].at[me], send_sem=send_sems.at[k, r - 1],
                    recv_sem=recv_sems.at[k, r - 1], device_id=peer, device_id_type=MESH), k, r, slot))
        for cp in local:
            cp.start()
        for cp, _, _, _ in remote:
            cp.start()
        for cp, k, r, slot in remote:
            pltpu.make_async_remote_copy(
                src_ref=outs[k].at[slot], dst_ref=outs[k].at[slot], send_sem=send_sems.at[k, r - 1],
                recv_sem=recv_sems.at[k, r - 1], device_id=_peer(r)[0], device_id_type=MESH).wait_recv()
        for cp, _, _, _ in remote:
            cp.wait_send()
        for cp in local:
            cp.wait()

    any_spec = pl.BlockSpec(memory_space=pl.ANY)
    return pl.pallas_call(
        body, name=name, out_shape=out_shape,
        in_specs=[any_spec] * n, out_specs=[any_spec] * n,
        scratch_shapes=[pltpu.SemaphoreType.DMA((n, N_DEV - 1)), pltpu.SemaphoreType.DMA((n, N_DEV - 1)),
                        pltpu.SemaphoreType.DMA((n,))],
        compiler_params=pltpu.CompilerParams(has_side_effects=True),
    )(*arrays)


HBM_SPEC = pl.BlockSpec(memory_space=pltpu.HBM)
SEM_SPEC = pl.BlockSpec(memory_space=pltpu.SEMAPHORE)
SPLIT_EFFECT = pltpu.SideEffectType.DATAFLOW_SIDE_EFFECTING


def _split_copies(land_ref, src_ref, send_sem, recv_sem):
    me = _my_slot()
    copies = []
    for r in range(1, N_DEV):
        peer, slot = _peer(r)
        src = _slot(land_ref, me) if src_ref is None else _slot(src_ref, slot)
        copies.append(pltpu.make_async_remote_copy(
            src_ref=src, dst_ref=_slot(land_ref, me), send_sem=send_sem, recv_sem=recv_sem,
            device_id=peer, device_id_type=MESH))
    return copies


def _slot(ref, s):
    if len(ref.shape) == 2:
        c = ref.shape[1] // N_DEV
        return ref.at[:, pl.ds(pl.multiple_of(s * c, LANES), c)]
    return ref.at[s]


def _exchange_start(lands, sources, name):
    n = len(lands)
    given = [s for s in sources if s is not None]
    arrays = list(lands) + given

    def body(*refs):
        land_refs, src_refs = refs[:n], list(refs[n:n + len(given)])
        sems = refs[len(arrays):len(arrays) + 2 * n]
        token = refs[-1]
        for k in range(n):
            src_ref = None if sources[k] is None else src_refs.pop(0)
            for copy in _split_copies(land_refs[k], src_ref, sems[k], sems[n + k]):
                copy.start()
        token[...] = jnp.zeros_like(token)

    outs = pl.pallas_call(
        body, name=name,
        out_shape=(pltpu.SemaphoreType.DMA(()),) * (2 * n) + tuple(pltpu.HBM(a.shape, a.dtype) for a in arrays)
        + (jax.ShapeDtypeStruct((SUBLANES, LANES), F32),),
        in_specs=[HBM_SPEC] * len(arrays),
        out_specs=(SEM_SPEC,) * (2 * n) + (HBM_SPEC,) * len(arrays) + (pl.BlockSpec(memory_space=pltpu.VMEM),),
        input_output_aliases={i: 2 * n + i for i in range(len(arrays))},
        compiler_params=pltpu.CompilerParams(has_side_effects=SPLIT_EFFECT),
    )(*[pltpu.with_memory_space_constraint(a, pltpu.HBM) for a in arrays])
    return list(outs[:n]), list(outs[n:2 * n]), list(outs[2 * n:3 * n]), list(outs[3 * n:-1]), outs[-1]


def _exchange_wait(lands, sources, send_sems, recv_sems, after, name):
    n = len(lands)
    given = [s for s in sources if s is not None]
    arrays = list(lands) + given

    def body(*refs):
        land_refs, src_refs = refs[:n], list(refs[n:n + len(given)])
        sems = refs[len(arrays):len(arrays) + 2 * n]
        for i in range(n):
            src_ref = None if sources[i] is None else src_refs.pop(0)
            copies = _split_copies(land_refs[i], src_ref, sems[i], sems[n + i])
            for copy in copies:
                copy.wait_recv()
            for copy in copies:
                copy.wait_send()

    outs = pl.pallas_call(
        body, name=name, out_shape=tuple(pltpu.HBM(a.shape, a.dtype) for a in arrays),
        in_specs=[HBM_SPEC] * len(arrays) + [SEM_SPEC] * (2 * n) + [pl.BlockSpec(memory_space=pl.ANY)],
        out_specs=(HBM_SPEC,) * len(arrays),
        input_output_aliases={i: i for i in range(len(arrays))},
        compiler_params=pltpu.CompilerParams(has_side_effects=SPLIT_EFFECT),
    )(*arrays, *send_sems, *recv_sems, after)
    return list(outs[:n])


def _place_own(a, me, name, own_block):
    shape = a.shape[1:] if own_block else a.shape
    cols = shape[-1]
    a3 = a.reshape((N_DEV if own_block else 1, -1, cols))
    rows = a3.shape[1]
    tr = min(rows, 512)

    def body(me_ref, a_ref, o_ref):
        o_ref[...] = a_ref[...].astype(BF16)

    grid_spec = pltpu.PrefetchScalarGridSpec(
        num_scalar_prefetch=1, grid=(rows // tr,),
        in_specs=[pl.BlockSpec((1, tr, cols), lambda i, me_ref: (me_ref[0] if own_block else 0, i, 0))],
        out_specs=pl.BlockSpec((1, tr, cols), lambda i, me_ref: (me_ref[0], i, 0)))
    out = pl.pallas_call(
        body, name=name, grid_spec=grid_spec, out_shape=jax.ShapeDtypeStruct((N_DEV, rows, cols), BF16),
        compiler_params=_params(("arbitrary",)),
    )(me, a3)
    return out.reshape((N_DEV,) + shape)


def _place_own_columns(a, me, name):
    rows, cols = a.shape
    tr = min(rows, 512)

    def body(me_ref, a_ref, o_ref):
        o_ref[...] = a_ref[...].astype(BF16)

    grid_spec = pltpu.PrefetchScalarGridSpec(
        num_scalar_prefetch=1, grid=(rows // tr,),
        in_specs=[pl.BlockSpec((tr, cols), lambda i, me_ref: (i, 0))],
        out_specs=pl.BlockSpec((tr, cols), lambda i, me_ref: (i, me_ref[0])))
    return pl.pallas_call(
        body, name=name, grid_spec=grid_spec, out_shape=jax.ShapeDtypeStruct((rows, N_DEV * cols), BF16),
        compiler_params=_params(("arbitrary",)),
    )(me, a)


def _rope_tables(positions):
    inv = ROPE_THETA ** (-jnp.arange(C_ROT_HALF, dtype=F32) / C_ROT_HALF)
    ang = positions.reshape(-1)[:, None].astype(F32) * inv
    cos, sin = jnp.cos(ang), jnp.sin(ang)
    t = ang.shape[0]
    ones = jnp.ones((t, C_HEAD_DIM - 2 * C_ROT_HALF), F32)
    c_head = jnp.concatenate([cos, cos, ones], axis=1)
    s_head = jnp.concatenate([-sin, sin, 0.0 * ones], axis=1)
    return jnp.concatenate([c_head, c_head], axis=1), jnp.concatenate([s_head, s_head], axis=1)


def _swap_halves(x):
    lane = lax.broadcasted_iota(jnp.int32, x.shape, 1) % C_HEAD_DIM
    return jnp.where(lane < C_ROT_HALF, pltpu.roll(x, LANES - C_ROT_HALF, 1), pltpu.roll(x, C_ROT_HALF, 1))


def _norm_inproj(x, g, w, name):
    t = x.shape[0]
    n = w.shape[1]
    tm, tn = PROJ_TILE, PROJ_COLS

    def body(x_ref, g_ref, w_ref, o_ref, h_ref):
        @pl.when(pl.program_id(1) == 0)
        def _():
            h_ref[...] = _rms(x_ref[...], g_ref[...]).astype(BF16)

        o_ref[...] = _dot(h_ref[...], w_ref[...])

    return pl.pallas_call(
        body, name=name, grid=(t // tm, n // tn),
        in_specs=[pl.BlockSpec((tm, D_MODEL), lambda i, j: (i, 0)), pl.BlockSpec((1, D_MODEL), lambda i, j: (0, 0)),
                  pl.BlockSpec((D_MODEL, tn), lambda i, j: (0, j))],
        out_specs=[pl.BlockSpec((tm, tn), lambda i, j: (i, j)), pl.BlockSpec((tm, D_MODEL), lambda i, j: (i, 0))],
        out_shape=[jax.ShapeDtypeStruct((t, n), F32), jax.ShapeDtypeStruct((t, D_MODEL), BF16)],
        compiler_params=_params(("parallel", "arbitrary")),
    )(x, g, w)


def _dilated_specs(tm, width, col_of):
    per_seq = SEQ // tm
    specs = []
    for d in C_DILATIONS:
        specs.append(pl.BlockSpec(
            (1, d, tm // d, width), lambda i, *rest: (i // per_seq, 0, i % per_seq, col_of(*rest))))
    return specs


def _dilated_shapes(n_seq, cols, dtype):
    return [jax.ShapeDtypeStruct((n_seq, d, SEQ // d, cols), dtype) for d in C_DILATIONS]


def _store_dilated(src_ref, out_refs, dtype):
    groups, tm, _ = src_ref.shape
    for d, o_ref in zip(C_DILATIONS, out_refs):
        for r in range(d):
            rows = pl.ds(r, tm // d, stride=d) if d > 1 else slice(None)
            for p in range(groups):
                o_ref[0, r, :, p * LANES:(p + 1) * LANES] = src_ref.at[p][rows, :].astype(dtype)


def _load_dilated(in_ref, d, dst_ref):
    groups, tm, _ = dst_ref.shape
    for r in range(d):
        rows = pl.ds(r, tm // d, stride=d)
        for p in range(groups):
            dst_ref.at[p][rows, :] = in_ref[0, r, :, p * LANES:(p + 1) * LANES].astype(F32)


def _norm_inproj_rope(x, g, w, rope, name):
    t = x.shape[0]
    n = w.shape[1]
    tm, nb = PROJ_TILE, PROJ_COLS

    def body(x_ref, g_ref, w_ref, c_ref, s_ref, o1_ref, o4_ref, o16_ref, h_ref, tile_ref):
        j = pl.program_id(1)

        @pl.when(j == 0)
        def _():
            h_ref[...] = _rms(x_ref[...], g_ref[...]).astype(BF16)

        acc = _dot(h_ref[...], w_ref[...])
        for p in range(nb // LANES):
            blk = acc[:, p * LANES:(p + 1) * LANES]
            roped = blk * c_ref[...] + _swap_halves(blk) * s_ref[...]
            is_qk = (j * (nb // LANES) + p) < 2 * (D_MODEL // LANES)
            tile_ref[p] = jnp.where(is_qk, roped, blk)
        _store_dilated(tile_ref, (o1_ref, o4_ref, o16_ref), BF16)

    return pl.pallas_call(
        body, name=name, grid=(t // tm, n // nb),
        in_specs=[pl.BlockSpec((tm, D_MODEL), lambda i, j: (i, 0)), pl.BlockSpec((1, D_MODEL), lambda i, j: (0, 0)),
                  pl.BlockSpec((D_MODEL, nb), lambda i, j: (0, j)),
                  pl.BlockSpec((tm, LANES), lambda i, j: (i, 0)), pl.BlockSpec((tm, LANES), lambda i, j: (i, 0))],
        out_specs=_dilated_specs(tm, nb, lambda j: j) + [pl.BlockSpec((tm, D_MODEL), lambda i, j: (i, 0))],
        out_shape=_dilated_shapes(t // SEQ, n, BF16) + [jax.ShapeDtypeStruct((t, D_MODEL), BF16)],
        scratch_shapes=[pltpu.VMEM((nb // LANES, tm, LANES), F32)],
        compiler_params=_params(("parallel", "arbitrary")),
    )(x, g, w, *rope)


def _outproj(parts, w, x, g, name):
    t = x.shape[0]
    tm = ROW_TILE
    n = len(parts)
    widths = [p.shape[1] for p in parts]

    def body(*refs):
        p_refs = refs[:n]
        w_ref, x_ref, g_ref, xo_ref, mix_ref = refs[n:]
        mix = None
        off = 0
        for p_ref, wd in zip(p_refs, widths):
            term = _dot(p_ref[...].astype(BF16), w_ref[off:off + wd, :])
            mix = term if mix is None else mix + term
            off += wd
        mix_ref[...] = mix
        xo_ref[...] = x_ref[...] + _rms(mix, g_ref[...])

    row = lambda i: (i, 0)
    return pl.pallas_call(
        body, name=name, grid=(t // tm,),
        in_specs=[pl.BlockSpec((tm, wd), row) for wd in widths] + [
            pl.BlockSpec((sum(widths), D_MODEL), lambda i: (0, 0)),
            pl.BlockSpec((tm, D_MODEL), row), pl.BlockSpec((1, D_MODEL), lambda i: (0, 0))],
        out_specs=[pl.BlockSpec((tm, D_MODEL), row)] * 2,
        out_shape=[jax.ShapeDtypeStruct((t, D_MODEL), F32)] * 2,
        compiler_params=_params(("parallel",)),
    )(*parts, w, x, g)


def _outproj_bwd(dx, mix, g, w, name):
    t = dx.shape[0]
    tm = ROW_TILE
    k = w.shape[0]

    def body(dx_ref, mix_ref, g_ref, w_ref, dcat_ref, dz_ref, dg_ref):
        dz, dgr = _rms_bwd(mix_ref[...], g_ref[...], dx_ref[...])
        dzb = dz.astype(BF16)
        dz_ref[...] = dzb
        dcat_ref[...] = _dot_nt(dzb, w_ref[...])
        _acc_rows8(dg_ref, _rows8(dgr), pl.program_id(0) == 0)

    row = lambda i: (i, 0)
    return pl.pallas_call(
        body, name=name, grid=(t // tm,),
        in_specs=[pl.BlockSpec((tm, D_MODEL), row), pl.BlockSpec((tm, D_MODEL), row),
                  pl.BlockSpec((1, D_MODEL), lambda i: (0, 0)), pl.BlockSpec((k, D_MODEL), lambda i: (0, 0))],
        out_specs=[pl.BlockSpec((tm, k), row), pl.BlockSpec((tm, D_MODEL), row),
                   pl.BlockSpec((SUBLANES, D_MODEL), lambda i: (0, 0))],
        out_shape=[jax.ShapeDtypeStruct((t, k), F32), jax.ShapeDtypeStruct((t, D_MODEL), BF16),
                   jax.ShapeDtypeStruct((SUBLANES, D_MODEL), F32)],
        compiler_params=_params(("arbitrary",)),
    )(dx, mix, g, w)


def _outproj_bwd_attn(dx, mix, g, w, out, name):
    t = dx.shape[0]
    tm = MERGE_TILE

    def body(dx_ref, mix_ref, g_ref, w_ref, out_ref, do1, do4, do16, dl_ref, dz_ref, dg_ref, tile_ref):
        dz, dgr = _rms_bwd(mix_ref[...], g_ref[...], dx_ref[...])
        dzb = dz.astype(BF16)
        dz_ref[...] = dzb
        _acc_rows8(dg_ref, _rows8(dgr), pl.program_id(0) == 0)
        dout = _dot_nt(dzb, w_ref[...])
        for p in range(LANE_GROUPS):
            tile_ref[p] = dout[:, p * LANES:(p + 1) * LANES]
        _store_dilated(tile_ref, (do1, do4, do16), BF16)
        column = lax.broadcasted_iota(jnp.int32, (D_MODEL, LANES), 0) // C_HEAD_DIM
        head = lax.broadcasted_iota(jnp.int32, (D_MODEL, LANES), 1)
        dl_ref[...] = jnp.dot(dout * out_ref[...], (column == head).astype(F32), precision=lax.Precision.HIGHEST,
                              preferred_element_type=F32)

    row = lambda i: (i, 0)
    n_seq = t // SEQ
    return pl.pallas_call(
        body, name=name, grid=(t // tm,),
        in_specs=[pl.BlockSpec((tm, D_MODEL), row), pl.BlockSpec((tm, D_MODEL), row),
                  pl.BlockSpec((1, D_MODEL), lambda i: (0, 0)), pl.BlockSpec((D_MODEL, D_MODEL), lambda i: (0, 0)),
                  pl.BlockSpec((tm, D_MODEL), row)],
        out_specs=_dilated_specs(tm, D_MODEL, lambda: 0) + [
            pl.BlockSpec((tm, LANES), row), pl.BlockSpec((tm, D_MODEL), row),
            pl.BlockSpec((SUBLANES, D_MODEL), lambda i: (0, 0))],
        out_shape=_dilated_shapes(n_seq, D_MODEL, BF16) + [
            jax.ShapeDtypeStruct((t, LANES), F32), jax.ShapeDtypeStruct((t, D_MODEL), BF16),
            jax.ShapeDtypeStruct((SUBLANES, D_MODEL), F32)],
        scratch_shapes=[pltpu.VMEM((LANE_GROUPS, tm, LANES), F32)],
        compiler_params=_params(("arbitrary",)),
    )(dx, mix, g, w, out)


def _inproj_bwd(dproj, w, dx, x, g, name):
    t = x.shape[0]
    n = w.shape[1]
    tm = ROW_TILE

    def body(dp_ref, w_ref, dx_ref, x_ref, g_ref, o_ref, dg_ref):
        dxn, dgr = _rms_bwd(x_ref[...], g_ref[...], _dot_nt(dp_ref[...], w_ref[...]))
        o_ref[...] = dx_ref[...] + dxn
        _acc_rows8(dg_ref, _rows8(dgr), pl.program_id(0) == 0)

    row = lambda i: (i, 0)
    return pl.pallas_call(
        body, name=name, grid=(t // tm,),
        in_specs=[pl.BlockSpec((tm, n), row), pl.BlockSpec((D_MODEL, n), lambda i: (0, 0)),
                  pl.BlockSpec((tm, D_MODEL), row), pl.BlockSpec((tm, D_MODEL), row),
                  pl.BlockSpec((1, D_MODEL), lambda i: (0, 0))],
        out_specs=[pl.BlockSpec((tm, D_MODEL), row), pl.BlockSpec((SUBLANES, D_MODEL), lambda i: (0, 0))],
        out_shape=[jax.ShapeDtypeStruct((t, D_MODEL), F32), jax.ShapeDtypeStruct((SUBLANES, D_MODEL), F32)],
        compiler_params=_params(("arbitrary",)),
    )(dproj, w, dx, x, g)


def _grad_w(a, b, col_blocks, name):
    t, k = a.shape
    n = b.shape[1]
    tk = min(k, 1024)
    per_owner = n // N_DEV
    tn = 2 * per_owner if col_blocks else min(n, 1024)

    def body(a_ref, b_ref, o_ref, at_ref):
        @pl.when(pl.program_id(1) == 0)
        def _():
            for c in range(t // ROW_TILE):
                rows = slice(c * ROW_TILE, (c + 1) * ROW_TILE)
                at_ref[:, rows] = a_ref[rows, :].T

        res = _dot(at_ref[...], b_ref[...]).astype(BF16)
        if col_blocks:
            o_ref[0] = res[:, :per_owner]
            o_ref[1] = res[:, per_owner:]
        else:
            o_ref[...] = res

    if col_blocks:
        out_spec = pl.BlockSpec((2, tk, per_owner), lambda i, j: (j, i, 0))
        out_shape = jax.ShapeDtypeStruct((N_DEV, k, per_owner), BF16)
    else:
        out_spec = pl.BlockSpec((tk, tn), lambda i, j: (i, j))
        out_shape = jax.ShapeDtypeStruct((k, n), BF16)
    return pl.pallas_call(
        body, name=name, grid=(k // tk, n // tn),
        in_specs=[pl.BlockSpec((t, tk), lambda i, j: (0, i)), pl.BlockSpec((t, tn), lambda i, j: (0, j))],
        out_specs=out_spec, out_shape=out_shape,
        scratch_shapes=[pltpu.VMEM((tk, t), BF16)],
        compiler_params=_params(("parallel", "arbitrary")),
    )(a, b)


FF_BLOCK = D_FF // N_DEV
FF_STEP = 1024
FF_STEPS = D_FF // FF_STEP


def _ffn_fwd(x, g_pre, w1, w2, g_post, name):
    t = x.shape[0]
    tm = ROW_TILE

    def body(x_ref, gp_ref, w1_ref, w2_ref, gq_ref, xo_ref, y_ref, h_ref):
        j = pl.program_id(1)

        @pl.when(j == 0)
        def _():
            h_ref[...] = _rms(x_ref[...], gp_ref[...]).astype(BF16)

        a = _dot(h_ref[...], w1_ref[...])
        r = jnp.square(jnp.maximum(a, 0.0)).astype(BF16)
        term = _dot(r, w2_ref[...])

        @pl.when(j == 0)
        def _():
            y_ref[...] = term

        @pl.when(j > 0)
        def _():
            y_ref[...] += term

        @pl.when(j == FF_STEPS - 1)
        def _():
            xo_ref[...] = x_ref[...] + _rms(y_ref[...], gq_ref[...])

    row = lambda i, j: (i, 0)
    vec = pl.BlockSpec((1, D_MODEL), lambda i, j: (0, 0))
    return pl.pallas_call(
        body, name=name, grid=(t // tm, FF_STEPS),
        in_specs=[pl.BlockSpec((tm, D_MODEL), row), vec,
                  pl.BlockSpec((D_MODEL, FF_STEP), lambda i, j: (0, j)),
                  pl.BlockSpec((FF_STEP, D_MODEL), lambda i, j: (j, 0)), vec],
        out_specs=[pl.BlockSpec((tm, D_MODEL), row)] * 3,
        out_shape=[jax.ShapeDtypeStruct((t, D_MODEL), F32), jax.ShapeDtypeStruct((t, D_MODEL), F32),
                   jax.ShapeDtypeStruct((t, D_MODEL), BF16)],
        compiler_params=_params(("parallel", "arbitrary")),
    )(x, g_pre, w1, w2, g_post)


def _ffn_bwd(dxo, x, y, h, g_pre, w1, w2, g_post, name):
    t = x.shape[0]
    tm = ROW_TILE

    def body(dxo_ref, x_ref, y_ref, h_ref, gp_ref, w1_ref, w2_ref, gq_ref,
             dx_ref, dy_ref, r_ref, da_ref, dgp_ref, dgq_ref, acc_ref):
        i, j = pl.program_id(0), pl.program_id(1)

        @pl.when(j == 0)
        def _():
            dy, dgr = _rms_bwd(y_ref[...], gq_ref[...], dxo_ref[...])
            dy_ref[...] = dy.astype(BF16)
            _acc_rows8(dgq_ref, _rows8(dgr), i == 0)

        a = _dot(h_ref[...], w1_ref[...])
        ra = jnp.maximum(a, 0.0)
        r_ref[...] = jnp.square(ra).astype(BF16)
        dr = _dot_nt(dy_ref[...], w2_ref[...])
        da = (dr * (2.0 * ra)).astype(BF16)
        da_ref[...] = da
        term = _dot_nt(da, w1_ref[...])

        @pl.when(j == 0)
        def _():
            acc_ref[...] = term

        @pl.when(j > 0)
        def _():
            acc_ref[...] += term

        @pl.when(j == FF_STEPS - 1)
        def _():
            dxn, dgr = _rms_bwd(x_ref[...], gp_ref[...], acc_ref[...])
            dx_ref[...] = dxo_ref[...] + dxn
            _acc_rows8(dgp_ref, _rows8(dgr), i == 0)

    row = lambda i, j: (i, 0)
    vec = pl.BlockSpec((1, D_MODEL), lambda i, j: (0, 0))
    acc8 = pl.BlockSpec((SUBLANES, D_MODEL), lambda i, j: (0, 0))
    return pl.pallas_call(
        body, name=name, grid=(t // tm, FF_STEPS),
        in_specs=[pl.BlockSpec((tm, D_MODEL), row)] * 4 + [
            vec, pl.BlockSpec((D_MODEL, FF_STEP), lambda i, j: (0, j)),
            pl.BlockSpec((FF_STEP, D_MODEL), lambda i, j: (j, 0)), vec],
        out_specs=[pl.BlockSpec((tm, D_MODEL), row), pl.BlockSpec((tm, D_MODEL), row),
                   pl.BlockSpec((tm, FF_STEP), lambda i, j: (i, j)), pl.BlockSpec((tm, FF_STEP), lambda i, j: (i, j)),
                   acc8, acc8],
        out_shape=[jax.ShapeDtypeStruct((t, D_MODEL), F32), jax.ShapeDtypeStruct((t, D_MODEL), BF16),
                   jax.ShapeDtypeStruct((t, D_FF), BF16), jax.ShapeDtypeStruct((t, D_FF), BF16),
                   jax.ShapeDtypeStruct((SUBLANES, D_MODEL), F32), jax.ShapeDtypeStruct((SUBLANES, D_MODEL), F32)],
        scratch_shapes=[pltpu.VMEM((tm, D_MODEL), F32)],
        compiler_params=_params(("arbitrary", "arbitrary")),
    )(dxo, x, y, h, g_pre, w1, w2, g_post)


def _lower_bound(table):
    e = jnp.exp(table - jnp.max(table, axis=0, keepdims=True))
    return e[0:1, :] / jnp.sum(e, axis=0, keepdims=True)


def _hgrn2_block(q_ref, f_ref, lb):
    tb = f_ref.shape[0]
    sig = _sigmoid(f_ref[...])
    f = lb + (1.0 - lb) * sig
    qraw = q_ref[...]
    sq = _sigmoid(qraw)
    r = lax.broadcasted_iota(jnp.int32, (tb, tb), 0)
    c = lax.broadcasted_iota(jnp.int32, (tb, tb), 1)
    same = (r // SUB_CHUNK) == (c // SUB_CHUNK)
    logf = jnp.log(f)
    gsum = jnp.dot((same & (c <= r)).astype(F32), logf, precision=lax.Precision.HIGHEST, preferred_element_type=F32)
    glast = jnp.dot(same.astype(F32), logf, precision=lax.Precision.HIGHEST, preferred_element_type=F32)
    return dict(sig=sig, f=f, kk=1.0 - f, qraw=qraw, sq=sq, qs=qraw * sq, gsum=gsum,
                eg=jnp.exp(gsum), ekd=jnp.exp(glast - gsum), a=jnp.exp(glast))


def _head_sums(x):
    parts = [jnp.broadcast_to(jnp.sum(x[:, h * HEAD_A:(h + 1) * HEAD_A], axis=1, keepdims=True), (x.shape[0], HEAD_A))
             for h in range(A_HEADS)]
    return jnp.concatenate(parts, axis=1)


def _lag_decay(g, j, row):
    back = jnp.exp(jnp.where(row >= j, g - pltpu.roll(g, j, 0), NEG))
    ahead = jnp.exp(jnp.where(row < SUB_CHUNK - j, pltpu.roll(g, SUB_CHUNK - j, 0) - g, NEG))
    return back, ahead


def _hgrn2_intra(g, kk, qs, v):
    row = lax.broadcasted_iota(jnp.int32, g.shape, 0)
    o = _head_sums(qs * kk) * v
    for j in range(1, SUB_CHUNK):
        decay = jnp.exp(jnp.where(row >= j, g - pltpu.roll(g, j, 0), NEG))
        o = o + _head_sums(qs * pltpu.roll(kk, j, 0) * decay) * pltpu.roll(v, j, 0)
    return o


def _hgrn2_intra_bwd(g, kk, qs, v, do):
    row = lax.broadcasted_iota(jnp.int32, g.shape, 0)
    dsc = _head_sums(do * v)
    dqs, dkk, dv = dsc * kk, dsc * qs, _head_sums(qs * kk) * do
    for j in range(1, SUB_CHUNK):
        back, ahead = _lag_decay(g, j, row)
        dqs = dqs + _head_sums(do * pltpu.roll(v, j, 0)) * pltpu.roll(kk, j, 0) * back
        q_up, do_up = pltpu.roll(qs, SUB_CHUNK - j, 0), pltpu.roll(do, SUB_CHUNK - j, 0)
        dkk = dkk + _head_sums(do_up * v) * q_up * ahead
        dv = dv + _head_sums(q_up * kk * ahead) * do_up
    return dqs, dkk, dv


def _hgrn2_fwd(proj, lb_table, a_norm, name):
    t = proj.shape[0]
    tb = HGRN_BLOCK
    n_tb = SEQ // tb
    n_seq = t // SEQ
    n_sub = tb // SUB_CHUNK

    def body(q_ref, f_ref, i_ref, g_ref, lbt_ref, an_ref, o_ref, pre_ref, sts_ref, st_ref,
             gs_ref, kk_ref, qs_ref, eg_ref, ekd_ref, a_ref):
        @pl.when(pl.program_id(1) == 0)
        def _():
            st_ref[...] = jnp.zeros_like(st_ref)

        an = an_ref[...]
        blk = _hgrn2_block(q_ref, f_ref, _lower_bound(lbt_ref[...]))
        for ref, key in ((gs_ref, "gsum"), (kk_ref, "kk"), (qs_ref, "qs"), (eg_ref, "eg"), (ekd_ref, "ekd"), (a_ref, "a")):
            ref[...] = blk[key]

        def step(c, carry):
            rows = pl.ds(pl.multiple_of(c * SUB_CHUNK, SUB_CHUNK), SUB_CHUNK)
            kk, qs, v = kk_ref[rows, :], qs_ref[rows, :], i_ref[rows, :]
            o = _hgrn2_intra(gs_ref[rows, :], kk, qs, v)
            qg, kd, vb = (qs * eg_ref[rows, :]).astype(BF16), (kk * ekd_ref[rows, :]).astype(BF16), v.astype(BF16)
            for h in range(A_HEADS):
                lanes = slice(h * HEAD_A, (h + 1) * HEAD_A)
                st = st_ref[h]
                sts_ref[0, c, h] = st
                o_h = o[:, lanes] + _dot_nt(qg[:, lanes], st.astype(BF16))
                st_ref[h] = st * a_ref[rows, lanes][0:1] + _dot_tn(vb[:, lanes], kd[:, lanes])
                pre_ref[rows, lanes] = o_h
                graw = g_ref[rows, lanes]
                o_ref[rows, lanes] = (_rms(o_h, an[:, lanes]) * (graw * _sigmoid(graw))).astype(BF16)
            return carry

        lax.fori_loop(0, n_sub, step, 0, unroll=2)

    def col(k):
        return pl.BlockSpec((tb, A_WIDTH), lambda b, s, k=k: (b * n_tb + s, k))

    out_rows = pl.BlockSpec((tb, A_WIDTH), lambda b, s: (b * n_tb + s, 0))
    return pl.pallas_call(
        body, name=name, grid=(n_seq, n_tb),
        in_specs=[col(0), col(1), col(2), col(3),
                  pl.BlockSpec((3, A_WIDTH), lambda b, s: (0, 0)), pl.BlockSpec((1, A_WIDTH), lambda b, s: (0, 0))],
        out_specs=[out_rows, out_rows,
                   pl.BlockSpec((1, n_sub, A_HEADS, HEAD_A, HEAD_A), lambda b, s: (b * n_tb + s, 0, 0, 0, 0))],
        out_shape=[jax.ShapeDtypeStruct((t, A_WIDTH), BF16), jax.ShapeDtypeStruct((t, A_WIDTH), F32),
                   jax.ShapeDtypeStruct((n_seq * n_tb, n_sub, A_HEADS, HEAD_A, HEAD_A), F32)],
        scratch_shapes=[pltpu.VMEM((A_HEADS, HEAD_A, HEAD_A), F32)] + [pltpu.VMEM((tb, A_WIDTH), F32)] * 6,
        compiler_params=_params(("parallel", "arbitrary")),
    )(proj, proj, proj, proj, lb_table, a_norm)


def _hgrn2_bwd(proj, dcat, pre, states, lb_table, a_norm, name):
    t = proj.shape[0]
    tb = HGRN_BLOCK
    n_tb = SEQ // tb
    n_seq = t // SEQ
    n_sub = tb // SUB_CHUNK

    def body(q_ref, f_ref, i_ref, g_ref, do_ref, pre_ref, sts_ref, lbt_ref, an_ref, dp_ref, dlb_ref, dan_ref, dst_ref,
             gs_ref, kk_ref, qs_ref, eg_ref, ekd_ref, a_ref, dpre_ref, dlf_ref, dqs_ref, dkk_ref):
        b, s = pl.program_id(0), pl.program_id(1)

        @pl.when(s == 0)
        def _():
            dst_ref[...] = jnp.zeros_like(dst_ref)

        @pl.when((b == 0) & (s == 0))
        def _():
            dlb_ref[...] = jnp.zeros_like(dlb_ref)
            dan_ref[...] = jnp.zeros_like(dan_ref)

        lb = _lower_bound(lbt_ref[...])
        an = an_ref[...]
        heads = [slice(h * HEAD_A, (h + 1) * HEAD_A) for h in range(A_HEADS)]
        blk = _hgrn2_block(q_ref, f_ref, lb)
        for ref, key in ((gs_ref, "gsum"), (kk_ref, "kk"), (qs_ref, "qs"), (eg_ref, "eg"), (ekd_ref, "ekd"), (a_ref, "a")):
            ref[...] = blk[key]
        for h, lanes in enumerate(heads):
            graw, o = g_ref[:, lanes], pre_ref[:, lanes]
            sg = _sigmoid(graw)
            dout = do_ref[:, lanes]
            d_o, dgr = _rms_bwd(o, an[:, lanes], dout * (graw * sg))
            dan_ref[0:1, lanes] += jnp.sum(dgr, axis=0, keepdims=True)
            dp_ref[:, 3 * A_WIDTH + h * HEAD_A:3 * A_WIDTH + (h + 1) * HEAD_A] = (
                dout * _rms(o, an[:, lanes]) * (sg * (1.0 + graw * (1.0 - sg)))).astype(BF16)
            dpre_ref[:, lanes] = d_o

        tri_t = (lax.broadcasted_iota(jnp.int32, (SUB_CHUNK, SUB_CHUNK), 0)
                 <= lax.broadcasted_iota(jnp.int32, (SUB_CHUNK, SUB_CHUNK), 1)).astype(F32)

        def back(k, carry):
            c = n_sub - 1 - k
            rows = pl.ds(pl.multiple_of(c * SUB_CHUNK, SUB_CHUNK), SUB_CHUNK)
            g, kk, qs, v, d_o = gs_ref[rows, :], kk_ref[rows, :], qs_ref[rows, :], i_ref[rows, :], dpre_ref[rows, :]
            eg, ekd, a = eg_ref[rows, :], ekd_ref[rows, :], a_ref[rows, :]
            dqs, dkk, dv = _hgrn2_intra_bwd(g, kk, qs, v, d_o)
            qg_f, kd_f = qs * eg, kk * ekd
            qg, kd, vb, dob = qg_f.astype(BF16), kd_f.astype(BF16), v.astype(BF16), d_o.astype(BF16)
            dqg, dkd, da, dv_st = [], [], [], []
            for h, lanes in enumerate(heads):
                st, dst = sts_ref[0, c, h], dst_ref[h]
                dstb = dst.astype(BF16)
                dqg.append(_dot(dob[:, lanes], st.astype(BF16)))
                dv_st.append(_dot_nt(kd[:, lanes], dstb))
                dkd.append(_dot(vb[:, lanes], dstb))
                da.append(jnp.broadcast_to(jnp.sum(dst * st, axis=0, keepdims=True), (SUB_CHUNK, HEAD_A)))
                dst_ref[h] = dst * a[0:1, lanes] + _dot_tn(dob[:, lanes], qg[:, lanes])
            dqg, dkd, da, dv_st = [jnp.concatenate(p, axis=1) for p in (dqg, dkd, da, dv_st)]
            d_gsum = qs * dqs - kk * dkk + dqg * qg_f - dkd * kd_f
            d_glast = jnp.sum(dkd * kd_f, axis=0, keepdims=True) + da * a
            dlf_ref[rows, :] = jnp.dot(tri_t, d_gsum, precision=lax.Precision.HIGHEST,
                                       preferred_element_type=F32) + d_glast
            dqs_ref[rows, :] = dqs + dqg * eg
            dkk_ref[rows, :] = dkk + dkd * ekd
            dp_ref[rows, 2 * A_WIDTH:3 * A_WIDTH] = (dv + dv_st).astype(BF16)
            return carry

        lax.fori_loop(0, n_sub, back, 0, unroll=2)
        sig, sq, qraw = blk["sig"], blk["sq"], blk["qraw"]
        d_f = dlf_ref[...] / blk["f"] - dkk_ref[...]
        dlb_ref[0:1, :] += jnp.sum(d_f * (1.0 - sig), axis=0, keepdims=True)
        dp_ref[:, 0:A_WIDTH] = (dqs_ref[...] * (sq * (1.0 + qraw * (1.0 - sq)))).astype(BF16)
        dp_ref[:, A_WIDTH:2 * A_WIDTH] = (d_f * (1.0 - lb) * sig * (1.0 - sig)).astype(BF16)

    def rev(s):
        return n_tb - 1 - s

    def col(k):
        return pl.BlockSpec((tb, A_WIDTH), lambda b, s, k=k: (b * n_tb + rev(s), k))

    acc8 = pl.BlockSpec((SUBLANES, A_WIDTH), lambda b, s: (0, 0))
    return pl.pallas_call(
        body, name=name, grid=(n_seq, n_tb),
        in_specs=[col(0), col(1), col(2), col(3), col(0), col(0),
                  pl.BlockSpec((1, n_sub, A_HEADS, HEAD_A, HEAD_A), lambda b, s: (b * n_tb + rev(s), 0, 0, 0, 0)),
                  pl.BlockSpec((3, A_WIDTH), lambda b, s: (0, 0)), pl.BlockSpec((1, A_WIDTH), lambda b, s: (0, 0))],
        out_specs=[pl.BlockSpec((tb, 4 * A_WIDTH), lambda b, s: (b * n_tb + rev(s), 0)), acc8, acc8],
        out_shape=[jax.ShapeDtypeStruct((t, 4 * A_WIDTH), BF16)] + [jax.ShapeDtypeStruct((SUBLANES, A_WIDTH), F32)] * 2,
        scratch_shapes=[pltpu.VMEM((A_HEADS, HEAD_A, HEAD_A), F32)] + [pltpu.VMEM((tb, A_WIDTH), F32)] * 10,
        compiler_params=_params(("arbitrary", "arbitrary")),
    )(proj, proj, proj, proj, dcat, pre, states, lb_table, a_norm)


GMLP_ROWS = 512


def _gmlp_chunk(ub, vb, ln_g, ln_b, ws, bias):
    u = [_gelu(a) for a in ub]
    v = [_gelu(a) for a in vb]
    mu = sum(jnp.sum(a, axis=-1, keepdims=True) for a in v) * (1.0 / B_WIDTH)
    cen = [a - mu for a in v]
    var = sum(jnp.sum(a * a, axis=-1, keepdims=True) for a in cen) * (1.0 / B_WIDTH)
    inv = lax.rsqrt(var + EPS)
    r = lax.broadcasted_iota(jnp.int32, (B_CHUNK, B_CHUNK), 0)
    c = lax.broadcasted_iota(jnp.int32, (B_CHUNK, B_CHUNK), 1)
    outs = []
    for g in range(B_GROUPS):
        vn = (cen[g] * inv * ln_g[g] + ln_b[g]).astype(BF16)
        wm = jnp.where(c <= r, ws[g], 0.0).astype(BF16)
        outs.append(u[g] * (_dot(wm, vn) + bias[g]))
    return outs


def _lane_groups(ref, rows=slice(None)):
    return [ref[rows, g * LANES:(g + 1) * LANES] for g in range(B_GROUPS)]


def _gmlp_fwd(proj, ln_g, ln_b, ws, bias_t, name):
    t = proj.shape[0]
    tm = GMLP_ROWS

    def body(u_ref, v_ref, lg_ref, lb_ref, ws_ref, bt_ref, o_ref):
        for ch in range(tm // B_CHUNK):
            rows = slice(ch * B_CHUNK, (ch + 1) * B_CHUNK)
            outs = _gmlp_chunk(_lane_groups(u_ref, rows), _lane_groups(v_ref, rows), _lane_groups(lg_ref),
                               _lane_groups(lb_ref), [ws_ref[g] for g in range(B_GROUPS)],
                               [bt_ref[:, g:g + 1] for g in range(B_GROUPS)])
            for g in range(B_GROUPS):
                o_ref[rows, g * LANES:(g + 1) * LANES] = outs[g].astype(BF16)

    vec = pl.BlockSpec((1, B_WIDTH), lambda i: (0, 0))
    return pl.pallas_call(
        body, name=name, grid=(t // tm,),
        in_specs=[pl.BlockSpec((tm, B_WIDTH), lambda i: (i, 4)), pl.BlockSpec((tm, B_WIDTH), lambda i: (i, 5)), vec, vec,
                  pl.BlockSpec((B_GROUPS, B_CHUNK, B_CHUNK), lambda i: (0, 0, 0)),
                  pl.BlockSpec((B_CHUNK, B_GROUPS), lambda i: (0, 0))],
        out_specs=pl.BlockSpec((tm, B_WIDTH), lambda i: (i, 0)),
        out_shape=jax.ShapeDtypeStruct((t, B_WIDTH), BF16),
        compiler_params=_params(("parallel",)),
    )(proj, proj, ln_g, ln_b, ws, bias_t)


def _gmlp_bwd(proj, dcat, ln_g, ln_b, ws, bias_t, name):
    t = proj.shape[0]
    tm = GMLP_ROWS

    def body(u_ref, v_ref, do_ref, lg_ref, lb_ref, ws_ref, bt_ref, duv_ref, dlg_ref, dlb_ref, dws_ref, dbt_ref):
        @pl.when(pl.program_id(0) == 0)
        def _():
            dlg_ref[...] = jnp.zeros_like(dlg_ref)
            dlb_ref[...] = jnp.zeros_like(dlb_ref)
            dws_ref[...] = jnp.zeros_like(dws_ref)
            dbt_ref[...] = jnp.zeros_like(dbt_ref)

        for ch in range(tm // B_CHUNK):
            rows = slice(ch * B_CHUNK, (ch + 1) * B_CHUNK)
            _, vjp = jax.vjp(
                _gmlp_chunk, _lane_groups(u_ref, rows), _lane_groups(v_ref, rows), _lane_groups(lg_ref),
                _lane_groups(lb_ref), [ws_ref[g] for g in range(B_GROUPS)],
                [bt_ref[:, g:g + 1] for g in range(B_GROUPS)])
            du, dv, dlg, dlb, dw, dbt = vjp(_lane_groups(do_ref, rows))
            for g in range(B_GROUPS):
                lanes = slice(g * LANES, (g + 1) * LANES)
                duv_ref[rows, lanes] = du[g].astype(BF16)
                duv_ref[rows, B_WIDTH + g * LANES:B_WIDTH + (g + 1) * LANES] = dv[g].astype(BF16)
                dlg_ref[0:1, lanes] += dlg[g]
                dlb_ref[0:1, lanes] += dlb[g]
                dws_ref[g] += dw[g]
                dbt_ref[:, g:g + 1] += dbt[g]

    vec = pl.BlockSpec((1, B_WIDTH), lambda i: (0, 0))
    acc8 = pl.BlockSpec((SUBLANES, B_WIDTH), lambda i: (0, 0))
    ws_spec = pl.BlockSpec((B_GROUPS, B_CHUNK, B_CHUNK), lambda i: (0, 0, 0))
    bt_spec = pl.BlockSpec((B_CHUNK, B_GROUPS), lambda i: (0, 0))
    return pl.pallas_call(
        body, name=name, grid=(t // tm,),
        in_specs=[pl.BlockSpec((tm, B_WIDTH), lambda i: (i, 4)), pl.BlockSpec((tm, B_WIDTH), lambda i: (i, 5)),
                  pl.BlockSpec((tm, B_WIDTH), lambda i: (i, 1)), vec, vec, ws_spec, bt_spec],
        out_specs=[pl.BlockSpec((tm, 2 * B_WIDTH), lambda i: (i, 0)), acc8, acc8, ws_spec, bt_spec],
        out_shape=[jax.ShapeDtypeStruct((t, 2 * B_WIDTH), BF16), jax.ShapeDtypeStruct((SUBLANES, B_WIDTH), F32),
                   jax.ShapeDtypeStruct((SUBLANES, B_WIDTH), F32),
                   jax.ShapeDtypeStruct((B_GROUPS, B_CHUNK, B_CHUNK), F32),
                   jax.ShapeDtypeStruct((B_CHUNK, B_GROUPS), F32)],
        compiler_params=_params(("arbitrary",)),
    )(proj, proj, dcat, ln_g, ln_b, ws, bias_t)


QK_SCALE = 1.0 / math.sqrt(C_HEAD_DIM)
ATTN_UNROLL = 8
LANE_GROUPS = D_MODEL // LANES
Q_BLOCKS = SEQ // C_BLOCK


def _attn_window(i, d):
    sub_blocks = Q_BLOCKS // d
    q0 = pl.multiple_of(i * C_BLOCK, C_BLOCK)
    k0 = pl.multiple_of(jnp.maximum(i - 1, 0) * C_BLOCK, C_BLOCK)
    key = k0 + lax.broadcasted_iota(jnp.int32, (C_BLOCK, 2 * C_BLOCK), 1)
    dist = (q0 + lax.broadcasted_iota(jnp.int32, (C_BLOCK, 2 * C_BLOCK), 0)) - key
    own_subsequence = (key >= q0) | (i % sub_blocks > 0)
    return pl.ds(q0, C_BLOCK), pl.ds(k0, 2 * C_BLOCK), (dist >= 0) & (dist <= C_BLOCK) & own_subsequence


def _head_masks():
    lane = lax.broadcasted_iota(jnp.int32, (C_BLOCK, LANES), 1)
    return [lane < C_HEAD_DIM, lane >= C_HEAD_DIM]


def _flat_spec(col_of):
    return pl.BlockSpec((1, SEQ, LANES), lambda b, g: (b, 0, col_of(g)))


def _put_heads(tile, g, col0, col1):
    lane = lax.broadcasted_iota(jnp.int32, tile.shape, 1)
    return jnp.where(lane == 2 * g, col0, jnp.where(lane == 2 * g + 1, col1, tile))


def _get_head(tile, h):
    lane = lax.broadcasted_iota(jnp.int32, tile.shape, 1)
    return jnp.sum(jnp.where(lane == h, tile, 0.0), axis=1, keepdims=True)


PER_HEAD_SPEC = pl.BlockSpec((1, SEQ, LANES), lambda b, g: (b, 0, 0))


def _attn_branch_fwd(qkv, name):
    n_seq, d, l, _ = qkv.shape
    flat = qkv.reshape(n_seq, SEQ, ODD_IN)

    def body(q_ref, k_ref, v_ref, o_ref, m_ref, l_ref):
        heads = _head_masks()
        g = pl.program_id(1)

        @pl.when(g == 0)
        def _():
            m_ref[...] = jnp.zeros_like(m_ref)
            l_ref[...] = jnp.zeros_like(l_ref)

        def block(i, carry):
            rows, keys, mask = _attn_window(i, d)
            q, k, v = q_ref[0, rows, :], k_ref[0, keys, :], v_ref[0, keys, :]
            res = []
            for hm in heads:
                s = jnp.where(mask, _dot_nt(jnp.where(hm, q, 0), k) * QK_SCALE, NEG)
                m = jnp.max(s, axis=-1, keepdims=True)
                p = jnp.exp(s - m)
                res.append((_dot(p.astype(BF16), v), m, jnp.sum(p, axis=-1, keepdims=True)))
            o_ref[0, rows, :] = jnp.where(heads[0], res[0][0], res[1][0])
            m_ref[0, rows, :] = _put_heads(m_ref[0, rows, :], g, res[0][1], res[1][1])
            l_ref[0, rows, :] = _put_heads(l_ref[0, rows, :], g, res[0][2], res[1][2])
            return carry

        lax.fori_loop(0, Q_BLOCKS, block, 0, unroll=ATTN_UNROLL)

    o, m, l_sum = pl.pallas_call(
        body, name=name, grid=(n_seq, LANE_GROUPS),
        in_specs=[_flat_spec(lambda g: g), _flat_spec(lambda g: LANE_GROUPS + g),
                  _flat_spec(lambda g: 2 * LANE_GROUPS + g)],
        out_specs=[_flat_spec(lambda g: g), PER_HEAD_SPEC, PER_HEAD_SPEC],
        out_shape=[jax.ShapeDtypeStruct((n_seq, SEQ, D_MODEL), F32)] + [jax.ShapeDtypeStruct((n_seq, SEQ, LANES), F32)] * 2,
        compiler_params=_params(("parallel", "arbitrary")),
    )(flat, flat, flat)
    return [o.reshape(n_seq, d, l, D_MODEL), m.reshape(n_seq, d, l, LANES), l_sum.reshape(n_seq, d, l, LANES)]


def _attn_merge(branches, name):
    n_seq = branches[0][0].shape[0]
    t = n_seq * SEQ
    tm = MERGE_TILE

    def body(*refs):
        ins = refs[:9]
        o_ref, ob_ref, lse_ref = refs[9:12]
        nat = refs[12:]
        for b, d in enumerate(C_DILATIONS[1:]):
            for k in range(3):
                _load_dilated(ins[3 + 3 * b + k], d, nat[3 * b + k])
        ms = [ins[1][0, 0], nat[1][0], nat[4][0]]
        ls = [ins[2][0, 0], nat[2][0], nat[5][0]]
        m_all = jnp.maximum(jnp.maximum(ms[0], ms[1]), ms[2])
        ws = [jnp.exp(ms[b] - m_all) for b in range(3)]
        lane = lax.broadcasted_iota(jnp.int32, m_all.shape, 1)
        total = jnp.where(lane < C_HEADS, ws[0] * ls[0] + ws[1] * ls[1] + ws[2] * ls[2], 1.0)
        lse_ref[...] = m_all + jnp.log(total)
        first_head = lane < C_HEAD_DIM
        for p in range(LANE_GROUPS):
            lanes = slice(p * LANES, (p + 1) * LANES)
            spread = lambda c: jnp.where(first_head, c[:, 2 * p:2 * p + 1], c[:, 2 * p + 1:2 * p + 2])
            os_ = [ins[0][0, 0, :, lanes], nat[0][p], nat[3][p]]
            o = (spread(ws[0]) * os_[0] + spread(ws[1]) * os_[1] + spread(ws[2]) * os_[2]) / spread(total)
            o_ref[:, lanes] = o
            ob_ref[:, lanes] = o.astype(BF16)

    row = pl.BlockSpec((tm, D_MODEL), lambda i: (i, 0))
    flat = [a for br in branches for a in br]
    in_specs = []
    for wide, narrow in zip(_dilated_specs(tm, D_MODEL, lambda: 0), _dilated_specs(tm, LANES, lambda: 0)):
        in_specs += [wide, narrow, narrow]
    per_head = pltpu.VMEM((1, tm, LANES), F32)
    return pl.pallas_call(
        body, name=name, grid=(t // tm,), in_specs=in_specs,
        out_specs=[row, row, pl.BlockSpec((tm, LANES), lambda i: (i, 0))],
        out_shape=[jax.ShapeDtypeStruct((t, D_MODEL), F32), jax.ShapeDtypeStruct((t, D_MODEL), BF16),
                   jax.ShapeDtypeStruct((t, LANES), F32)],
        scratch_shapes=[pltpu.VMEM((LANE_GROUPS, tm, LANES), F32), per_head, per_head] * 2,
        compiler_params=_params(("parallel",)),
    )(*flat)


def _attn_branch_bwd(qkv, dout, lse, delta, name):
    n_seq, d, l, _ = qkv.shape
    flat = lambda a: a.reshape(n_seq, SEQ, a.shape[-1])

    def body(q_ref, k_ref, v_ref, do_ref, lse_nat_ref, dl_nat_ref, dq_ref, dk_ref, dv_ref, lse_ref, dl_ref):
        heads = _head_masks()
        g = pl.program_id(1)
        dk_ref[...] = jnp.zeros_like(dk_ref)
        dv_ref[...] = jnp.zeros_like(dv_ref)
        for nat_ref, dst_ref in ((lse_nat_ref, lse_ref), (dl_nat_ref, dl_ref)):
            for r in range(d):
                rows = pl.ds(r, l, stride=d) if d > 1 else slice(None)
                dst_ref[r * l:(r + 1) * l, :] = nat_ref.at[0][rows, :]

        def block(i, carry):
            rows, keys, mask = _attn_window(i, d)
            q, do = q_ref[0, rows, :], do_ref[0, rows, :]
            k, v = k_ref[0, keys, :], v_ref[0, keys, :]
            lse_b, dl_b = lse_ref[rows, :], dl_ref[rows, :]
            dq, dk, dv = [], None, None
            for hh, hm in enumerate(heads):
                qh, doh = jnp.where(hm, q, 0), jnp.where(hm, do, 0)
                s = jnp.where(mask, _dot_nt(qh, k) * QK_SCALE, NEG)
                p = jnp.exp(s - _get_head(lse_b, 2 * g + hh))
                ds = (p * (_dot_nt(doh, v) - _get_head(dl_b, 2 * g + hh)) * QK_SCALE).astype(BF16)
                dq.append(_dot(ds, k))
                dk_h, dv_h = _dot_tn(ds, qh), _dot_tn(p.astype(BF16), doh)
                dk = dk_h if dk is None else dk + dk_h
                dv = dv_h if dv is None else dv + dv_h
            dq_ref[0, rows, :] = jnp.where(heads[0], dq[0], dq[1])
            dk_ref[0, keys, :] += dk
            dv_ref[0, keys, :] += dv
            return carry

        lax.fori_loop(0, Q_BLOCKS, block, 0, unroll=ATTN_UNROLL)

    act = _flat_spec(lambda g: g)
    outs = pl.pallas_call(
        body, name=name, grid=(n_seq, LANE_GROUPS),
        in_specs=[_flat_spec(lambda g: g), _flat_spec(lambda g: LANE_GROUPS + g),
                  _flat_spec(lambda g: 2 * LANE_GROUPS + g), act, PER_HEAD_SPEC, PER_HEAD_SPEC],
        out_specs=[act] * 3,
        out_shape=[jax.ShapeDtypeStruct((n_seq, SEQ, D_MODEL), F32)] * 3,
        scratch_shapes=[pltpu.VMEM((SEQ, LANES), F32)] * 2,
        compiler_params=_params(("parallel", "parallel")),
    )(flat(qkv), flat(qkv), flat(qkv), flat(dout), lse, delta)
    return [o.reshape(n_seq, d, l, D_MODEL) for o in outs]


def _attn_combine_bwd(grads, rope, name):
    n_seq = grads[0][0].shape[0]
    t = n_seq * SEQ
    tm = MERGE_TILE

    def body(*refs):
        c_ref, s_ref, o_ref, nat4_ref, nat16_ref = refs[9:]
        for sec in range(3):
            _load_dilated(refs[3 + sec], 4, nat4_ref)
            _load_dilated(refs[6 + sec], 16, nat16_ref)
            for p in range(LANE_GROUPS):
                blk = refs[sec][0, 0, :, p * LANES:(p + 1) * LANES] + nat4_ref[p] + nat16_ref[p]
                if sec < 2:
                    blk = blk * c_ref[...] - _swap_halves(blk) * s_ref[...]
                o_ref[:, sec * D_MODEL + p * LANES:sec * D_MODEL + (p + 1) * LANES] = blk.astype(BF16)

    tab = pl.BlockSpec((tm, LANES), lambda i: (i, 0))
    flat = [a for br in grads for a in br]
    in_specs = []
    for spec in _dilated_specs(tm, D_MODEL, lambda: 0):
        in_specs += [spec] * 3
    return pl.pallas_call(
        body, name=name, grid=(t // tm,), in_specs=in_specs + [tab, tab],
        out_specs=pl.BlockSpec((tm, ODD_IN), lambda i: (i, 0)),
        out_shape=jax.ShapeDtypeStruct((t, ODD_IN), BF16),
        scratch_shapes=[pltpu.VMEM((LANE_GROUPS, tm, LANES), F32)] * 2,
        compiler_params=_params(("parallel",)),
    )(*flat, *rope)


def _loss_grad(y, target, name):
    t = y.shape[0]
    tm = ROW_TILE

    def body(y_ref, t_ref, d_ref, l_ref):
        diff = y_ref[...] - t_ref[...]
        d_ref[...] = diff * (1.0 / D_MODEL)
        _acc_rows8(l_ref, _rows8(diff * diff) * (0.5 / D_MODEL), pl.program_id(0) == 0)

    row = pl.BlockSpec((tm, D_MODEL), lambda i: (i, 0))
    return pl.pallas_call(
        body, name=name, grid=(t // tm,), in_specs=[row, row],
        out_specs=[row, pl.BlockSpec((SUBLANES, D_MODEL), lambda i: (0, 0))],
        out_shape=[jax.ShapeDtypeStruct((t, D_MODEL), F32), jax.ShapeDtypeStruct((SUBLANES, D_MODEL), F32)],
        compiler_params=_params(("arbitrary",)),
    )(y, target)


def _adamw(w, g, m, v):
    m = ADAM_B1 * m + (1.0 - ADAM_B1) * g
    v = ADAM_B2 * v + (1.0 - ADAM_B2) * jnp.square(g)
    m_hat = m / (1.0 - ADAM_B1 ** ADAM_STEP)
    v_hat = v / (1.0 - ADAM_B2 ** ADAM_STEP)
    delta = -ADAM_LR * (m_hat / (jnp.sqrt(v_hat) + ADAM_EPS) + ADAM_WD * w)
    return delta, m, v


def _adamw_sharded(parts, w, m, v, name):
    n_layers, rows, cols = w.shape
    tr = min(rows, 256)

    def body(*refs):
        p_refs = refs[:n_layers]
        w_ref, m_ref, v_ref, g_ref, d_ref, mo_ref, vo_ref = refs[n_layers:]
        layer = pl.program_id(0)
        g = None
        for l, p_ref in enumerate(p_refs):
            g_l = p_ref[0].astype(F32)
            for s in range(1, N_DEV):
                g_l = g_l + p_ref[s].astype(F32)
            g = g_l if g is None else jnp.where(layer == l, g_l, g)
        delta, mn, vn = _adamw(w_ref[0], g, m_ref[0], v_ref[0])
        g_ref[0] = g
        d_ref[0] = delta
        mo_ref[0] = mn
        vo_ref[0] = vn

    def part_spec(l):
        return pl.BlockSpec((N_DEV, tr, cols), lambda a, i: (0, jnp.where(a == l, i, 0), 0))

    row = pl.BlockSpec((1, tr, cols), lambda a, i: (a, i, 0))
    return pl.pallas_call(
        body, name=name, grid=(n_layers, rows // tr),
        in_specs=[part_spec(l) for l in range(n_layers)] + [row, row, row],
        out_specs=[row] * 4, out_shape=[jax.ShapeDtypeStruct(w.shape, F32)] * 4,
        compiler_params=_params(("arbitrary", "arbitrary")),
    )(*parts, w, m, v)


def _small_update(gathered, where, weights, moments_m, moments_v, lb_index, name):
    n = len(weights)
    n_g = len(gathered)

    def body(*refs):
        g_refs = refs[:n_g]
        w_refs, m_refs, v_refs = refs[n_g:n_g + n], refs[n_g + n:n_g + 2 * n], refs[n_g + 2 * n:n_g + 3 * n]
        outs = refs[n_g + 3 * n:]

        def total(k):
            array, rows, lanes = where[k]
            ref = g_refs[array]
            index = (slice(None),) * (len(ref.shape) - 1) if rows is None else (rows, lanes)
            acc = ref[(0,) + index]
            for s in range(1, N_DEV):
                acc = acc + ref[(s,) + index]
            return acc

        loss_rows = total(n)
        outs[0][...] = jnp.sum(jnp.sum(loss_rows, axis=1, keepdims=True), axis=0, keepdims=True)
        for k in range(n):
            part = total(k)
            if k == lb_index:
                dlb = jnp.sum(part, axis=0, keepdims=True)
                tab = w_refs[k][...]
                e = jnp.exp(tab - jnp.max(tab, axis=0, keepdims=True))
                p = e / jnp.sum(e, axis=0, keepdims=True)
                first = lax.broadcasted_iota(jnp.int32, p.shape, 0) == 0
                grads = [(slice(None), p * (jnp.where(first, dlb, 0.0) - p[0:1, :] * dlb))]
            elif part.shape == w_refs[k].shape:
                grads = [(slice(None), part)]
            else:
                grads = [(slice(l, l + 1), jnp.sum(part[l * SUBLANES:(l + 1) * SUBLANES], axis=0, keepdims=True))
                         for l in range(w_refs[k].shape[0])]
            for rows, g in grads:
                delta, mn, vn = _adamw(w_refs[k][rows], g, m_refs[k][rows], v_refs[k][rows])
                outs[1 + 4 * k][rows] = g
                outs[2 + 4 * k][rows] = delta
                outs[3 + 4 * k][rows] = mn
                outs[4 + 4 * k][rows] = vn

    vmem = pl.BlockSpec(memory_space=pltpu.VMEM)
    out_shape = [jax.ShapeDtypeStruct((1, 1), F32)]
    for w in weights:
        out_shape += [jax.ShapeDtypeStruct(w.shape, F32)] * 4
    args = list(gathered) + list(weights) + list(moments_m) + list(moments_v)
    return pl.pallas_call(
        body, name=name, in_specs=[vmem] * len(args), out_specs=[vmem] * len(out_shape), out_shape=out_shape,
        compiler_params=pltpu.CompilerParams(vmem_limit_bytes=VMEM_LIMIT),
    )(*args)


def kernel(x, positions, norm_mix_pre, norm_mix_post, norm_ffn_pre, norm_ffn_post, w_in_even, lb_table, a_norm, b_ln_g, b_ln_b, b_ws, b_bias, w_out_even, w_in_odd, w_out_odd, w_ff1, w_ff2, loss_target, m_norm_mix_pre, m_norm_mix_post, m_norm_ffn_pre, m_norm_ffn_post, m_w_in_even, m_lb_table, m_a_norm, m_b_ln_g, m_b_ln_b, m_b_ws, m_b_bias, m_w_out_even, m_w_in_odd, m_w_out_odd, m_w_ff1, m_w_ff2, v_norm_mix_pre, v_norm_mix_post, v_norm_ffn_pre, v_norm_ffn_post, v_w_in_even, v_lb_table, v_a_norm, v_b_ln_g, v_b_ln_b, v_b_ws, v_b_bias, v_w_out_even, v_w_in_odd, v_w_out_odd, v_w_ff1, v_w_ff2):
    n_seq = x.shape[0]
    t = n_seq * SEQ
    x0 = x.reshape(t, D_MODEL)
    target = loss_target.reshape(t, D_MODEL)

    me = _my_slot().astype(jnp.int32).reshape(1)

    order = ["in_e", "out_e", "ff1_0", "ff2_0", "in_o", "out_o", "ff1_1", "ff2_1"]
    shards = dict(in_e=w_in_even[0], out_e=w_out_even[0], in_o=w_in_odd[0], out_o=w_out_odd[0],
                  ff1_0=w_ff1[0], ff1_1=w_ff1[1], ff2_0=w_ff2[0], ff2_1=w_ff2[1])
    by_columns = ("in_e", "in_o", "ff1_0", "ff1_1")
    lands = [_place_own_columns(shards[k], me, "place_" + k) if k in by_columns
             else _place_own(shards[k], me, "place_" + k, False) for k in order]
    g_send, g_recv, lands, _, g_token = _exchange_start(lands, [None] * len(order), "gather_start")

    def get_w(keys, after):
        ks = [order.index(k) for k in keys]
        return _exchange_wait([lands[k] for k in ks], [None] * len(ks), [g_send[k] for k in ks],
                              [g_recv[k] for k in ks], after, "gather_wait_" + keys[0])

    sent = {}

    def put_g(group, blocks):
        keys = list(blocks)
        own = [_place_own(blocks[k], me, "own_" + k, True) for k in keys]
        send_sems, recv_sems, own, srcs, token = _exchange_start(own, [blocks[k] for k in keys], "scatter_start_" + group)
        sent[group] = (keys, own, srcs, send_sems, recv_sems)
        return token

    rope = _rope_tables(positions)
    bias_t = b_bias[0].T
    grads = _local_step(x0, target, rope, norm_mix_pre, norm_mix_post, norm_ffn_pre, norm_ffn_post, lb_table,
                        a_norm, b_ln_g, b_ln_b, b_ws[0], bias_t, get_w, put_g, g_token)
    (dx0, loss_part, dg_mix_pre, dg_mix_post, dg_ffn_pre, dg_ffn_post, d_lb, d_a_norm, d_ln_g, d_ln_b, d_ws,
     d_bias_t) = grads

    recv = {}
    for group, (keys, own, srcs, send_sems, recv_sems) in sent.items():
        done = _exchange_wait(own, srcs, send_sems, recv_sems, dx0, "scatter_wait_" + group)
        recv.update(zip(keys, done))
    big = [("w_in_even", ["in_e"], w_in_even, m_w_in_even, v_w_in_even),
           ("w_out_even", ["out_e"], w_out_even, m_w_out_even, v_w_out_even),
           ("w_in_odd", ["in_o"], w_in_odd, m_w_in_odd, v_w_in_odd),
           ("w_out_odd", ["out_o"], w_out_odd, m_w_out_odd, v_w_out_odd),
           ("w_ff1", ["ff1_0", "ff1_1"], w_ff1, m_w_ff1, v_w_ff1), ("w_ff2", ["ff2_0", "ff2_1"], w_ff2, m_w_ff2, v_w_ff2)]
    big_out = [_adamw_sharded([recv[k] for k in keys], w, m, v, "adamw_" + nm) for nm, keys, w, m, v in big]

    packed = jnp.concatenate([dg_mix_pre, dg_mix_post, dg_ffn_pre, dg_ffn_post,
                              jnp.concatenate([d_lb, d_a_norm], axis=1), jnp.concatenate([d_ln_g, d_ln_b], axis=1),
                              loss_part], axis=0)
    gathered = _exchange([packed, d_ws, d_bias_t], True, "gather_small")
    rows8 = lambda k: slice(SUBLANES * k, SUBLANES * (k + 1))
    left, right, every = slice(0, A_WIDTH), slice(A_WIDTH, 2 * A_WIDTH), slice(None)
    where = [(0, slice(0, 16), every), (0, slice(16, 32), every), (0, slice(32, 48), every), (0, slice(48, 64), every),
             (0, rows8(8), left), (0, rows8(8), right), (0, rows8(9), left), (0, rows8(9), right),
             (1, None, None), (2, None, None), (0, rows8(10), every)]
    small_w = [norm_mix_pre, norm_mix_post, norm_ffn_pre, norm_ffn_post, lb_table, a_norm, b_ln_g, b_ln_b,
               b_ws[0], bias_t]
    small_m = [m_norm_mix_pre, m_norm_mix_post, m_norm_ffn_pre, m_norm_ffn_post, m_lb_table, m_a_norm, m_b_ln_g,
               m_b_ln_b, m_b_ws[0], m_b_bias[0].T]
    small_v = [v_norm_mix_pre, v_norm_mix_post, v_norm_ffn_pre, v_norm_ffn_post, v_lb_table, v_a_norm, v_b_ln_g,
               v_b_ln_b, v_b_ws[0], v_b_bias[0].T]
    small_out = _small_update(gathered, where, small_w, small_m, small_v, 4, "small_update")
    loss = small_out[0].reshape(())
    small = [small_out[1 + 4 * k:5 + 4 * k] for k in range(len(small_w))]
    small[8] = [a[None] for a in small[8]]
    small[9] = [a.T[None] for a in small[9]]

    per_weight = small[0:4] + [big_out[0]] + small[4:10] + big_out[1:6]
    grad_x = dx0.reshape(x.shape)
    out = [loss, grad_x]
    for kind in range(4):
        out += [p[kind] for p in per_weight]
    return tuple(out)


def _local_step(x0, target, rope, norm_mix_pre, norm_mix_post, norm_ffn_pre, norm_ffn_post, lb_table, a_norm,
                b_ln_g, b_ln_b, ws, bias_t, get_w, put_g, token):
    def gain(a, l, tok):
        return a[l:l + 1] if tok is None else a[l:l + 1] + tok[0:1, 0:1]

    full = lambda a: a.reshape(-1, D_MODEL)
    owners = lambda a: a.reshape((N_DEV, -1) + a.shape[1:])

    (g_in_e,) = get_w(["in_e"], token)
    proj, h_mix0 = _norm_inproj(x0, gain(norm_mix_pre, 0, token), g_in_e, "inproj_even")
    oa, pre_a, states = _hgrn2_fwd(proj, lb_table, a_norm, "hgrn2_fwd")
    ob = _gmlp_fwd(proj, b_ln_g, b_ln_b, ws, bias_t, "gmlp_fwd")
    w_out_e = full(get_w(["out_e"], ob)[0])
    x1, mix0 = _outproj([oa, ob], w_out_e, x0, gain(norm_mix_post, 0, None), "outproj_even")
    w1_0, w2_0 = get_w(["ff1_0", "ff2_0"], x1)
    w2_0 = full(w2_0)
    x2, y0, h_ffn0 = _ffn_fwd(x1, gain(norm_ffn_pre, 0, None), w1_0, w2_0, gain(norm_ffn_post, 0, None), "ffn_fwd_0")
    (g_in_o,) = get_w(["in_o"], x2)
    *qkv, h_mix1 = _norm_inproj_rope(x2, gain(norm_mix_pre, 1, None), g_in_o, rope, "inproj_odd")
    branches = [_attn_branch_fwd(a, "attn_fwd_d%d" % d) for a, d in zip(qkv, C_DILATIONS)]
    attn, attn_b, lse = _attn_merge(branches, "attn_merge")
    w_out_o = full(get_w(["out_o"], attn_b)[0])
    x3, mix1 = _outproj([attn_b], w_out_o, x2, gain(norm_mix_post, 1, None), "outproj_odd")
    w1_1, w2_1 = get_w(["ff1_1", "ff2_1"], x3)
    w2_1 = full(w2_1)
    x4, y1, h_ffn1 = _ffn_fwd(x3, gain(norm_ffn_pre, 1, None), w1_1, w2_1, gain(norm_ffn_post, 1, None), "ffn_fwd_1")

    dx4, loss_part = _loss_grad(x4, target, "loss_grad")

    dx3, dy1, r1, da1, dg_ffn_pre1, dg_ffn_post1 = _ffn_bwd(
        dx4, x3, y1, h_ffn1, gain(norm_ffn_pre, 1, None), w1_1, w2_1, gain(norm_ffn_post, 1, None), "ffn_bwd_1")
    gw_ff1_1 = _grad_w(h_ffn1, da1, True, "grad_w_ff1_1")
    gw_ff2_1 = _grad_w(r1, dy1, False, "grad_w_ff2_1")
    tok = put_g("ffn1", dict(ff1_1=gw_ff1_1, ff2_1=owners(gw_ff2_1)))
    *dattn, delta, dz1, dg_mix_post1 = _outproj_bwd_attn(dx3, mix1, gain(norm_mix_post, 1, tok), w_out_o, attn,
                                                  "outproj_bwd_odd")
    gw_out_o = _grad_w(attn_b, dz1, False, "grad_w_out_odd")
    per_seq = lambda a: a.reshape(-1, SEQ, LANES)
    grads_c = [_attn_branch_bwd(qkv[b], dattn[b], per_seq(lse), per_seq(delta), "attn_bwd_d%d" % d)
               for b, d in enumerate(C_DILATIONS)]
    dqkv = _attn_combine_bwd(grads_c, rope, "attn_combine_bwd")
    gw_in_o = _grad_w(h_mix1, dqkv, True, "grad_w_in_odd")
    tok = put_g("mix1", dict(out_o=owners(gw_out_o), in_o=gw_in_o))
    dx2, dg_mix_pre1 = _inproj_bwd(dqkv, g_in_o, dx3, x2, gain(norm_mix_pre, 1, tok), "inproj_bwd_odd")

    dx1, dy0, r0, da0, dg_ffn_pre0, dg_ffn_post0 = _ffn_bwd(
        dx2, x1, y0, h_ffn0, gain(norm_ffn_pre, 0, None), w1_0, w2_0, gain(norm_ffn_post, 0, None), "ffn_bwd_0")
    gw_ff1_0 = _grad_w(h_ffn0, da0, True, "grad_w_ff1_0")
    gw_ff2_0 = _grad_w(r0, dy0, False, "grad_w_ff2_0")
    tok = put_g("ffn0", dict(ff1_0=gw_ff1_0, ff2_0=owners(gw_ff2_0)))
    dcat, dz0, dg_mix_post0 = _outproj_bwd(dx1, mix0, gain(norm_mix_post, 0, tok), w_out_e, "outproj_bwd_even")
    gw_out_e = jnp.concatenate([_grad_w(oa, dz0, False, "grad_w_out_even_a"),
                                _grad_w(ob, dz0, False, "grad_w_out_even_b")], axis=0)
    dqfig, d_lb, d_a_norm = _hgrn2_bwd(proj, dcat, pre_a, states, lb_table, a_norm, "hgrn2_bwd")
    duv, d_ln_g, d_ln_b, d_ws, d_bias_t = _gmlp_bwd(proj, dcat, b_ln_g, b_ln_b, ws, bias_t, "gmlp_bwd")
    dproj = jnp.concatenate([dqfig, duv], axis=1)
    gw_in_e = _grad_w(h_mix0, dproj, True, "grad_w_in_even")
    tok = put_g("mix0", dict(out_e=owners(gw_out_e), in_e=gw_in_e))
    dx0, dg_mix_pre0 = _inproj_bwd(dproj, g_in_e, dx1, x0, gain(norm_mix_pre, 0, tok), "inproj_bwd_even")

    layers = lambda a, b: jnp.concatenate([a, b], axis=0)
    return (dx0, loss_part, layers(dg_mix_pre0, dg_mix_pre1), layers(dg_mix_post0, dg_mix_post1),
            layers(dg_ffn_pre0, dg_ffn_pre1), layers(dg_ffn_post0, dg_ffn_post1),
            d_lb, d_a_norm, d_ln_g, d_ln_b, d_ws, d_bias_t)
```

```python
import functools
import math

import jax
import jax.numpy as jnp
from jax import lax
from jax.experimental import pallas as pl
from jax.experimental.pallas import tpu as pltpu

F32 = jnp.float32
BF16 = jnp.bfloat16
MESH = pl.DeviceIdType.MESH

N_DEV = 8
D_MODEL = 1024
SEQ = 2048
EPS = 1e-6
A_WIDTH = 512
A_HEADS = 4
HEAD_A = 128
B_WIDTH = 512
B_GROUPS = 4
B_CHUNK = 128
C_HEADS = 16
C_HEAD_DIM = 64
C_ROT_HALF = 8
ROPE_THETA = 500000.0
C_DILATIONS = (1, 4, 16)
C_BLOCK = 128
D_FF = 4096
EVEN_IN = 3072
ODD_IN = 3072

ADAM_LR = 0.001
ADAM_B1 = 0.9
ADAM_B2 = 0.999
ADAM_EPS = 1e-08
ADAM_WD = 0.01
ADAM_STEP = 10

LANES = 128
SUBLANES = 8
ROW_TILE = 512
PROJ_TILE = 1024
PROJ_COLS = 768
MERGE_TILE = 256
SUB_CHUNK = 16
HGRN_BLOCK = 256
NEG = -1e30
VMEM_LIMIT = 56 * 1024 * 1024


def _params(sem):
    return pltpu.CompilerParams(dimension_semantics=sem, vmem_limit_bytes=VMEM_LIMIT)


def _dot(a, b):
    return jnp.dot(a, b, preferred_element_type=F32)


def _dot_nt(a, b):
    return lax.dot_general(a, b, (((1,), (1,)), ((), ())), preferred_element_type=F32)


def _dot_tn(a, b):
    return lax.dot_general(a, b, (((0,), (0,)), ((), ())), preferred_element_type=F32)


def _rms(x, g):
    r = lax.rsqrt(jnp.mean(x * x, axis=-1, keepdims=True) + EPS)
    return x * r * g


def _rms_bwd(x, g, dy):
    r = lax.rsqrt(jnp.mean(x * x, axis=-1, keepdims=True) + EPS)
    dyg = dy * g
    dx = r * dyg - x * (r * r * r) * jnp.mean(x * dyg, axis=-1, keepdims=True)
    return dx, dy * x * r


def _rows8(v):
    return v.reshape(v.shape[0] // SUBLANES, SUBLANES, v.shape[1]).sum(axis=0)


def _sigmoid(x):
    return 1.0 / (1.0 + jnp.exp(-x))


def _gelu(x):
    return 0.5 * x * (1.0 + jnp.tanh(math.sqrt(2.0 / math.pi) * (x + 0.044715 * (x * x * x))))


def _acc_rows8(ref, val, first):
    @pl.when(first)
    def _():
        ref[...] = val

    @pl.when(jnp.logical_not(first))
    def _():
        ref[...] += val


def _my_slot():
    return 4 * lax.axis_index("x") + 2 * lax.axis_index("y") + lax.axis_index("c")


def _peer(r):
    x, y, c = lax.axis_index("x"), lax.axis_index("y"), lax.axis_index("c")
    px = 1 - x if (r >> 2) & 1 else x
    py = 1 - y if (r >> 1) & 1 else y
    pc = 1 - c if r & 1 else c
    return (px, py, pc), 4 * px + 2 * py + pc


def _exchange(arrays, gather, name):
    n = len(arrays)
    if gather:
        out_shape = [jax.ShapeDtypeStruct((N_DEV,) + a.shape, a.dtype) for a in arrays]
    else:
        out_shape = [jax.ShapeDtypeStruct(a.shape, a.dtype) for a in arrays]

    def body(*refs):
        ins, outs = refs[:n], refs[n:2 * n]
        send_sems, recv_sems, local_sems = refs[2 * n:]
        me = _my_slot()
        local, remote = [], []
        for k in range(n):
            src = ins[k] if gather else ins[k].at[me]
            local.append(pltpu.make_async_copy(src, outs[k].at[me], local_sems.at[k]))
            for r in range(1, N_DEV):
                peer, slot = _peer(r)
                src = ins[k] if gather else ins[k].at[slot]
                remote.append((pltpu.make_async_remote_copy(
                    src_ref=src, dst_ref=outs[k].at[me], send_sem=send_sems.at[k, r - 1],
                    recv_sem=recv_sems.at[k, r - 1], device_id=peer, device_id_type=MESH), k, r, slot))
        for cp in local:
            cp.start()
        for cp, _, _, _ in remote:
            cp.start()
        for cp, k, r, slot in remote:
            pltpu.make_async_remote_copy(
                src_ref=outs[k].at[slot], dst_ref=outs[k].at[slot], send_sem=send_sems.at[k, r - 1],
                recv_sem=recv_sems.at[k, r - 1], device_id=_peer(r)[0], device_id_type=MESH).wait_recv()
        for cp, _, _, _ in remote:
            cp.wait_send()
        for cp in local:
            cp.wait()

    any_spec = pl.BlockSpec(memory_space=pl.ANY)
    return pl.pallas_call(
        body, name=name, out_shape=out_shape,
        in_specs=[any_spec] * n, out_specs=[any_spec] * n,
        scratch_shapes=[pltpu.SemaphoreType.DMA((n, N_DEV - 1)), pltpu.SemaphoreType.DMA((n, N_DEV - 1)),
                        pltpu.SemaphoreType.DMA((n,))],
        compiler_params=pltpu.CompilerParams(has_side_effects=True),
    )(*arrays)


HBM_SPEC = pl.BlockSpec(memory_space=pltpu.HBM)
SEM_SPEC = pl.BlockSpec(memory_space=pltpu.SEMAPHORE)
SPLIT_EFFECT = pltpu.SideEffectType.DATAFLOW_SIDE_EFFECTING


def _split_copies(land_ref, src_ref, send_sem, recv_sem):
    me = _my_slot()
    copies = []
    for r in range(1, N_DEV):
        peer, slot = _peer(r)
        src = _slot(land_ref, me) if src_ref is None else _slot(src_ref, slot)
        copies.append(pltpu.make_async_remote_copy(
            src_ref=src, dst_ref=_slot(land_ref, me), send_sem=send_sem, recv_sem=recv_sem,
            device_id=peer, device_id_type=MESH))
    return copies


def _slot(ref, s):
    if len(ref.shape) == 2:
        c = ref.shape[1] // N_DEV
        return ref.at[:, pl.ds(pl.multiple_of(s * c, LANES), c)]
    return ref.at[s]


def _exchange_start(lands, sources, name):
    n = len(lands)
    given = [s for s in sources if s is not None]
    arrays = list(lands) + given

    def body(*refs):
        land_refs, src_refs = refs[:n], list(refs[n:n + len(given)])
        sems = refs[len(arrays):len(arrays) + 2 * n]
        token = refs[-1]
        for k in range(n):
            src_ref = None if sources[k] is None else src_refs.pop(0)
            for copy in _split_copies(land_refs[k], src_ref, sems[k], sems[n + k]):
                copy.start()
        token[...] = jnp.zeros_like(token)

    outs = pl.pallas_call(
        body, name=name,
        out_shape=(pltpu.SemaphoreType.DMA(()),) * (2 * n) + tuple(pltpu.HBM(a.shape, a.dtype) for a in arrays)
        + (jax.ShapeDtypeStruct((SUBLANES, LANES), F32),),
        in_specs=[HBM_SPEC] * len(arrays),
        out_specs=(SEM_SPEC,) * (2 * n) + (HBM_SPEC,) * len(arrays) + (pl.BlockSpec(memory_space=pltpu.VMEM),),
        input_output_aliases={i: 2 * n + i for i in range(len(arrays))},
        compiler_params=pltpu.CompilerParams(has_side_effects=SPLIT_EFFECT),
    )(*[pltpu.with_memory_space_constraint(a, pltpu.HBM) for a in arrays])
    return list(outs[:n]), list(outs[n:2 * n]), list(outs[2 * n:3 * n]), list(outs[3 * n:-1]), outs[-1]


def _exchange_wait(lands, sources, send_sems, recv_sems, after, name):
    n = len(lands)
    given = [s for s in sources if s is not None]
    arrays = list(lands) + given

    def body(*refs):
        land_refs, src_refs = refs[:n], list(refs[n:n + len(given)])
        sems = refs[len(arrays):len(arrays) + 2 * n]
        for i in range(n):
            src_ref = None if sources[i] is None else src_refs.pop(0)
            copies = _split_copies(land_refs[i], src_ref, sems[i], sems[n + i])
            for copy in copies:
                copy.wait_recv()
            for copy in copies:
                copy.wait_send()

    outs = pl.pallas_call(
        body, name=name, out_shape=tuple(pltpu.HBM(a.shape, a.dtype) for a in arrays),
        in_specs=[HBM_SPEC] * len(arrays) + [SEM_SPEC] * (2 * n) + [pl.BlockSpec(memory_space=pl.ANY)],
        out_specs=(HBM_SPEC,) * len(arrays),
        input_output_aliases={i: i for i in range(len(arrays))},
        compiler_params=pltpu.CompilerParams(has_side_effects=SPLIT_EFFECT),
    )(*arrays, *send_sems, *recv_sems, after)
    return list(outs[:n])


def _place_own(a, me, name, own_block):
    shape = a.shape[1:] if own_block else a.shape
    cols = shape[-1]
    a3 = a.reshape((N_DEV if own_block else 1, -1, cols))
    rows = a3.shape[1]
    tr = min(rows, 512)

    def body(me_ref, a_ref, o_ref):
        o_ref[...] = a_ref[...].astype(BF16)

    grid_spec = pltpu.PrefetchScalarGridSpec(
        num_scalar_prefetch=1, grid=(rows // tr,),
        in_specs=[pl.BlockSpec((1, tr, cols), lambda i, me_ref: (me_ref[0] if own_block else 0, i, 0))],
        out_specs=pl.BlockSpec((1, tr, cols), lambda i, me_ref: (me_ref[0], i, 0)))
    out = pl.pallas_call(
        body, name=name, grid_spec=grid_spec, out_shape=jax.ShapeDtypeStruct((N_DEV, rows, cols), BF16),
        compiler_params=_params(("arbitrary",)),
    )(me, a3)
    return out.reshape((N_DEV,) + shape)


def _place_own_columns(a, me, name):
    rows, cols = a.shape
    tr = min(rows, 512)

    def body(me_ref, a_ref, o_ref):
        o_ref[...] = a_ref[...].astype(BF16)

    grid_spec = pltpu.PrefetchScalarGridSpec(
        num_scalar_prefetch=1, grid=(rows // tr,),
        in_specs=[pl.BlockSpec((tr, cols), lambda i, me_ref: (i, 0))],
        out_specs=pl.BlockSpec((tr, cols), lambda i, me_ref: (i, me_ref[0])))
    return pl.pallas_call(
        body, name=name, grid_spec=grid_spec, out_shape=jax.ShapeDtypeStruct((rows, N_DEV * cols), BF16),
        compiler_params=_params(("arbitrary",)),
    )(me, a)


def _rope_tables(positions):
    inv = ROPE_THETA ** (-jnp.arange(C_ROT_HALF, dtype=F32) / C_ROT_HALF)
    ang = positions.reshape(-1)[:, None].astype(F32) * inv
    cos, sin = jnp.cos(ang), jnp.sin(ang)
    t = ang.shape[0]
    ones = jnp.ones((t, C_HEAD_DIM - 2 * C_ROT_HALF), F32)
    c_head = jnp.concatenate([cos, cos, ones], axis=1)
    s_head = jnp.concatenate([-sin, sin, 0.0 * ones], axis=1)
    return jnp.concatenate([c_head, c_head], axis=1), jnp.concatenate([s_head, s_head], axis=1)


def _swap_halves(x):
    lane = lax.broadcasted_iota(jnp.int32, x.shape, 1) % C_HEAD_DIM
    return jnp.where(lane < C_ROT_HALF, pltpu.roll(x, LANES - C_ROT_HALF, 1), pltpu.roll(x, C_ROT_HALF, 1))


def _norm_inproj(x, g, w, name):
    t = x.shape[0]
    n = w.shape[1]
    tm, tn = PROJ_TILE, PROJ_COLS

    def body(x_ref, g_ref, w_ref, o_ref, h_ref):
        @pl.when(pl.program_id(1) == 0)
        def _():
            h_ref[...] = _rms(x_ref[...], g_ref[...]).astype(BF16)

        o_ref[...] = _dot(h_ref[...], w_ref[...])

    return pl.pallas_call(
        body, name=name, grid=(t // tm, n // tn),
        in_specs=[pl.BlockSpec((tm, D_MODEL), lambda i, j: (i, 0)), pl.BlockSpec((1, D_MODEL), lambda i, j: (0, 0)),
                  pl.BlockSpec((D_MODEL, tn), lambda i, j: (0, j))],
        out_specs=[pl.BlockSpec((tm, tn), lambda i, j: (i, j)), pl.BlockSpec((tm, D_MODEL), lambda i, j: (i, 0))],
        out_shape=[jax.ShapeDtypeStruct((t, n), F32), jax.ShapeDtypeStruct((t, D_MODEL), BF16)],
        compiler_params=_params(("parallel", "arbitrary")),
    )(x, g, w)


def _dilated_specs(tm, width, col_of):
    per_seq = SEQ // tm
    specs = []
    for d in C_DILATIONS:
        specs.append(pl.BlockSpec(
            (1, d, tm // d, width), lambda i, *rest: (i // per_seq, 0, i % per_seq, col_of(*rest))))
    return specs


def _dilated_shapes(n_seq, cols, dtype):
    return [jax.ShapeDtypeStruct((n_seq, d, SEQ // d, cols), dtype) for d in C_DILATIONS]


def _store_dilated(src_ref, out_refs, dtype):
    groups, tm, _ = src_ref.shape
    for d, o_ref in zip(C_DILATIONS, out_refs):
        for r in range(d):
            rows = pl.ds(r, tm // d, stride=d) if d > 1 else slice(None)
            for p in range(groups):
                o_ref[0, r, :, p * LANES:(p + 1) * LANES] = src_ref.at[p][rows, :].astype(dtype)


def _load_dilated(in_ref, d, dst_ref):
    groups, tm, _ = dst_ref.shape
    for r in range(d):
        rows = pl.ds(r, tm // d, stride=d)
        for p in range(groups):
            dst_ref.at[p][rows, :] = in_ref[0, r, :, p * LANES:(p + 1) * LANES].astype(F32)


def _norm_inproj_rope(x, g, w, rope, name):
    t = x.shape[0]
    n = w.shape[1]
    tm, nb = PROJ_TILE, PROJ_COLS

    def body(x_ref, g_ref, w_ref, c_ref, s_ref, o1_ref, o4_ref, o16_ref, h_ref, tile_ref):
        j = pl.program_id(1)

        @pl.when(j == 0)
        def _():
            h_ref[...] = _rms(x_ref[...], g_ref[...]).astype(BF16)

        acc = _dot(h_ref[...], w_ref[...])
        for p in range(nb // LANES):
            blk = acc[:, p * LANES:(p + 1) * LANES]
            roped = blk * c_ref[...] + _swap_halves(blk) * s_ref[...]
            is_qk = (j * (nb // LANES) + p) < 2 * (D_MODEL // LANES)
            tile_ref[p] = jnp.where(is_qk, roped, blk)
        _store_dilated(tile_ref, (o1_ref, o4_ref, o16_ref), BF16)

    return pl.pallas_call(
        body, name=name, grid=(t // tm, n // nb),
        in_specs=[pl.BlockSpec((tm, D_MODEL), lambda i, j: (i, 0)), pl.BlockSpec((1, D_MODEL), lambda i, j: (0, 0)),
                  pl.BlockSpec((D_MODEL, nb), lambda i, j: (0, j)),
                  pl.BlockSpec((tm, LANES), lambda i, j: (i, 0)), pl.BlockSpec((tm, LANES), lambda i, j: (i, 0))],
        out_specs=_dilated_specs(tm, nb, lambda j: j) + [pl.BlockSpec((tm, D_MODEL), lambda i, j: (i, 0))],
        out_shape=_dilated_shapes(t // SEQ, n, BF16) + [jax.ShapeDtypeStruct((t, D_MODEL), BF16)],
        scratch_shapes=[pltpu.VMEM((nb // LANES, tm, LANES), F32)],
        compiler_params=_params(("parallel", "arbitrary")),
    )(x, g, w, *rope)


def _outproj(parts, w, x, g, name):
    t = x.shape[0]
    tm = ROW_TILE
    n = len(parts)
    widths = [p.shape[1] for p in parts]

    def body(*refs):
        p_refs = refs[:n]
        w_ref, x_ref, g_ref, xo_ref, mix_ref = refs[n:]
        mix = None
        off = 0
        for p_ref, wd in zip(p_refs, widths):
            term = _dot(p_ref[...].astype(BF16), w_ref[off:off + wd, :])
            mix = term if mix is None else mix + term
            off += wd
        mix_ref[...] = mix
        xo_ref[...] = x_ref[...] + _rms(mix, g_ref[...])

    row = lambda i: (i, 0)
    return pl.pallas_call(
        body, name=name, grid=(t // tm,),
        in_specs=[pl.BlockSpec((tm, wd), row) for wd in widths] + [
            pl.BlockSpec((sum(widths), D_MODEL), lambda i: (0, 0)),
            pl.BlockSpec((tm, D_MODEL), row), pl.BlockSpec((1, D_MODEL), lambda i: (0, 0))],
        out_specs=[pl.BlockSpec((tm, D_MODEL), row)] * 2,
        out_shape=[jax.ShapeDtypeStruct((t, D_MODEL), F32)] * 2,
        compiler_params=_params(("parallel",)),
    )(*parts, w, x, g)


def _outproj_bwd(dx, mix, g, w, name):
    t = dx.shape[0]
    tm = ROW_TILE
    k = w.shape[0]

    def body(dx_ref, mix_ref, g_ref, w_ref, dcat_ref, dz_ref, dg_ref):
        dz, dgr = _rms_bwd(mix_ref[...], g_ref[...], dx_ref[...])
        dzb = dz.astype(BF16)
        dz_ref[...] = dzb
        dcat_ref[...] = _dot_nt(dzb, w_ref[...])
        _acc_rows8(dg_ref, _rows8(dgr), pl.program_id(0) == 0)

    row = lambda i: (i, 0)
    return pl.pallas_call(
        body, name=name, grid=(t // tm,),
        in_specs=[pl.BlockSpec((tm, D_MODEL), row), pl.BlockSpec((tm, D_MODEL), row),
                  pl.BlockSpec((1, D_MODEL), lambda i: (0, 0)), pl.BlockSpec((k, D_MODEL), lambda i: (0, 0))],
        out_specs=[pl.BlockSpec((tm, k), row), pl.BlockSpec((tm, D_MODEL), row),
                   pl.BlockSpec((SUBLANES, D_MODEL), lambda i: (0, 0))],
        out_shape=[jax.ShapeDtypeStruct((t, k), F32), jax.ShapeDtypeStruct((t, D_MODEL), BF16),
                   jax.ShapeDtypeStruct((SUBLANES, D_MODEL), F32)],
        compiler_params=_params(("arbitrary",)),
    )(dx, mix, g, w)


def _outproj_bwd_attn(dx, mix, g, w, out, name):
    t = dx.shape[0]
    tm = MERGE_TILE

    def body(dx_ref, mix_ref, g_ref, w_ref, out_ref, do1, do4, do16, dl_ref, dz_ref, dg_ref, tile_ref):
        dz, dgr = _rms_bwd(mix_ref[...], g_ref[...], dx_ref[...])
        dzb = dz.astype(BF16)
        dz_ref[...] = dzb
        _acc_rows8(dg_ref, _rows8(dgr), pl.program_id(0) == 0)
        dout = _dot_nt(dzb, w_ref[...])
        for p in range(LANE_GROUPS):
            tile_ref[p] = dout[:, p * LANES:(p + 1) * LANES]
        _store_dilated(tile_ref, (do1, do4, do16), BF16)
        column = lax.broadcasted_iota(jnp.int32, (D_MODEL, LANES), 0) // C_HEAD_DIM
        head = lax.broadcasted_iota(jnp.int32, (D_MODEL, LANES), 1)
        dl_ref[...] = jnp.dot(dout * out_ref[...], (column == head).astype(F32), precision=lax.Precision.HIGHEST,
                              preferred_element_type=F32)

    row = lambda i: (i, 0)
    n_seq = t // SEQ
    return pl.pallas_call(
        body, name=name, grid=(t // tm,),
        in_specs=[pl.BlockSpec((tm, D_MODEL), row), pl.BlockSpec((tm, D_MODEL), row),
                  pl.BlockSpec((1, D_MODEL), lambda i: (0, 0)), pl.BlockSpec((D_MODEL, D_MODEL), lambda i: (0, 0)),
                  pl.BlockSpec((tm, D_MODEL), row)],
        out_specs=_dilated_specs(tm, D_MODEL, lambda: 0) + [
            pl.BlockSpec((tm, LANES), row), pl.BlockSpec((tm, D_MODEL), row),
            pl.BlockSpec((SUBLANES, D_MODEL), lambda i: (0, 0))],
        out_shape=_dilated_shapes(n_seq, D_MODEL, BF16) + [
            jax.ShapeDtypeStruct((t, LANES), F32), jax.ShapeDtypeStruct((t, D_MODEL), BF16),
            jax.ShapeDtypeStruct((SUBLANES, D_MODEL), F32)],
        scratch_shapes=[pltpu.VMEM((LANE_GROUPS, tm, LANES), F32)],
        compiler_params=_params(("arbitrary",)),
    )(dx, mix, g, w, out)


def _inproj_bwd(dproj, w, dx, x, g, name):
    t = x.shape[0]
    n = w.shape[1]
    tm = ROW_TILE

    def body(dp_ref, w_ref, dx_ref, x_ref, g_ref, o_ref, dg_ref):
        dxn, dgr = _rms_bwd(x_ref[...], g_ref[...], _dot_nt(dp_ref[...], w_ref[...]))
        o_ref[...] = dx_ref[...] + dxn
        _acc_rows8(dg_ref, _rows8(dgr), pl.program_id(0) == 0)

    row = lambda i: (i, 0)
    return pl.pallas_call(
        body, name=name, grid=(t // tm,),
        in_specs=[pl.BlockSpec((tm, n), row), pl.BlockSpec((D_MODEL, n), lambda i: (0, 0)),
                  pl.BlockSpec((tm, D_MODEL), row), pl.BlockSpec((tm, D_MODEL), row),
                  pl.BlockSpec((1, D_MODEL), lambda i: (0, 0))],
        out_specs=[pl.BlockSpec((tm, D_MODEL), row), pl.BlockSpec((SUBLANES, D_MODEL), lambda i: (0, 0))],
        out_shape=[jax.ShapeDtypeStruct((t, D_MODEL), F32), jax.ShapeDtypeStruct((SUBLANES, D_MODEL), F32)],
        compiler_params=_params(("arbitrary",)),
    )(dproj, w, dx, x, g)


def _grad_w(a, b, col_blocks, name):
    t, k = a.shape
    n = b.shape[1]
    tk = min(k, 1024)
    per_owner = n // N_DEV
    tn = 2 * per_owner if col_blocks else min(n, 1024)

    def body(a_ref, b_ref, o_ref, at_ref):
        @pl.when(pl.program_id(1) == 0)
        def _():
            for c in range(t // ROW_TILE):
                rows = slice(c * ROW_TILE, (c + 1) * ROW_TILE)
                at_ref[:, rows] = a_ref[rows, :].T

        res = _dot(at_ref[...], b_ref[...]).astype(BF16)
        if col_blocks:
            o_ref[0] = res[:, :per_owner]
            o_ref[1] = res[:, per_owner:]
        else:
            o_ref[...] = res

    if col_blocks:
        out_spec = pl.BlockSpec((2, tk, per_owner), lambda i, j: (j, i, 0))
        out_shape = jax.ShapeDtypeStruct((N_DEV, k, per_owner), BF16)
    else:
        out_spec = pl.BlockSpec((tk, tn), lambda i, j: (i, j))
        out_shape = jax.ShapeDtypeStruct((k, n), BF16)
    return pl.pallas_call(
        body, name=name, grid=(k // tk, n // tn),
        in_specs=[pl.BlockSpec((t, tk), lambda i, j: (0, i)), pl.BlockSpec((t, tn), lambda i, j: (0, j))],
        out_specs=out_spec, out_shape=out_shape,
        scratch_shapes=[pltpu.VMEM((tk, t), BF16)],
        compiler_params=_params(("parallel", "arbitrary")),
    )(a, b)


FF_BLOCK = D_FF // N_DEV
FF_STEP = 1024
FF_STEPS = D_FF // FF_STEP


def _ffn_fwd(x, g_pre, w1, w2, g_post, name):
    t = x.shape[0]
    tm = ROW_TILE

    def body(x_ref, gp_ref, w1_ref, w2_ref, gq_ref, xo_ref, y_ref, h_ref, r_ref):
        j = pl.program_id(1)

        @pl.when(j == 0)
        def _():
            h_ref[...] = _rms(x_ref[...], gp_ref[...]).astype(BF16)

        a = _dot(h_ref[...], w1_ref[...])
        r = jnp.square(jnp.maximum(a, 0.0)).astype(BF16)
        r_ref[...] = r
        term = _dot(r, w2_ref[...])

        @pl.when(j == 0)
        def _():
            y_ref[...] = term

        @pl.when(j > 0)
        def _():
            y_ref[...] += term

        @pl.when(j == FF_STEPS - 1)
        def _():
            xo_ref[...] = x_ref[...] + _rms(y_ref[...], gq_ref[...])

    row = lambda i, j: (i, 0)
    vec = pl.BlockSpec((1, D_MODEL), lambda i, j: (0, 0))
    return pl.pallas_call(
        body, name=name, grid=(t // tm, FF_STEPS),
        in_specs=[pl.BlockSpec((tm, D_MODEL), row), vec,
                  pl.BlockSpec((D_MODEL, FF_STEP), lambda i, j: (0, j)),
                  pl.BlockSpec((FF_STEP, D_MODEL), lambda i, j: (j, 0)), vec],
        out_specs=[pl.BlockSpec((tm, D_MODEL), row)] * 3 + [pl.BlockSpec((tm, FF_STEP), lambda i, j: (i, j))],
        out_shape=[jax.ShapeDtypeStruct((t, D_MODEL), F32), jax.ShapeDtypeStruct((t, D_MODEL), F32),
                   jax.ShapeDtypeStruct((t, D_MODEL), BF16), jax.ShapeDtypeStruct((t, D_FF), BF16)],
        compiler_params=_params(("parallel", "arbitrary")),
    )(x, g_pre, w1, w2, g_post)


def _ffn_bwd(dxo, x, y, r, g_pre, w1, w2, g_post, name):
    t = x.shape[0]
    tm = ROW_TILE

    def body(dxo_ref, x_ref, y_ref, r_ref, gp_ref, w1_ref, w2_ref, gq_ref,
             dx_ref, dy_ref, da_ref, dgp_ref, dgq_ref, acc_ref):
        i, j = pl.program_id(0), pl.program_id(1)

        @pl.when(j == 0)
        def _():
            dy, dgr = _rms_bwd(y_ref[...], gq_ref[...], dxo_ref[...])
            dy_ref[...] = dy.astype(BF16)
            _acc_rows8(dgq_ref, _rows8(dgr), i == 0)

        dr = _dot_nt(dy_ref[...], w2_ref[...])
        da = (dr * (2.0 * jnp.sqrt(r_ref[...].astype(F32)))).astype(BF16)
        da_ref[...] = da
        term = _dot_nt(da, w1_ref[...])

        @pl.when(j == 0)
        def _():
            acc_ref[...] = term

        @pl.when(j > 0)
        def _():
            acc_ref[...] += term

        @pl.when(j == FF_STEPS - 1)
        def _():
            dxn, dgr = _rms_bwd(x_ref[...], gp_ref[...], acc_ref[...])
            dx_ref[...] = dxo_ref[...] + dxn
            _acc_rows8(dgp_ref, _rows8(dgr), i == 0)

    row = lambda i, j: (i, 0)
    vec = pl.BlockSpec((1, D_MODEL), lambda i, j: (0, 0))
    acc8 = pl.BlockSpec((SUBLANES, D_MODEL), lambda i, j: (0, 0))
    return pl.pallas_call(
        body, name=name, grid=(t // tm, FF_STEPS),
        in_specs=[pl.BlockSpec((tm, D_MODEL), row)] * 3 + [
            pl.BlockSpec((tm, FF_STEP), lambda i, j: (i, j)),
            vec, pl.BlockSpec((D_MODEL, FF_STEP), lambda i, j: (0, j)),
            pl.BlockSpec((FF_STEP, D_MODEL), lambda i, j: (j, 0)), vec],
        out_specs=[pl.BlockSpec((tm, D_MODEL), row), pl.BlockSpec((tm, D_MODEL), row),
                   pl.BlockSpec((tm, FF_STEP), lambda i, j: (i, j)), acc8, acc8],
        out_shape=[jax.ShapeDtypeStruct((t, D_MODEL), F32), jax.ShapeDtypeStruct((t, D_MODEL), BF16),
                   jax.ShapeDtypeStruct((t, D_FF), BF16),
                   jax.ShapeDtypeStruct((SUBLANES, D_MODEL), F32), jax.ShapeDtypeStruct((SUBLANES, D_MODEL), F32)],
        scratch_shapes=[pltpu.VMEM((tm, D_MODEL), F32)],
        compiler_params=_params(("arbitrary", "arbitrary")),
    )(dxo, x, y, r, g_pre, w1, w2, g_post)


def _lower_bound(table):
    e = jnp.exp(table - jnp.max(table, axis=0, keepdims=True))
    return e[0:1, :] / jnp.sum(e, axis=0, keepdims=True)


def _hgrn2_block(q_ref, f_ref, lb):
    tb = f_ref.shape[0]
    sig = _sigmoid(f_ref[...])
    f = lb + (1.0 - lb) * sig
    qraw = q_ref[...]
    sq = _sigmoid(qraw)
    r = lax.broadcasted_iota(jnp.int32, (tb, tb), 0)
    c = lax.broadcasted_iota(jnp.int32, (tb, tb), 1)
    same = (r // SUB_CHUNK) == (c // SUB_CHUNK)
    logf = jnp.log(f)
    gsum = jnp.dot((same & (c <= r)).astype(F32), logf, precision=lax.Precision.HIGHEST, preferred_element_type=F32)
    glast = jnp.dot(same.astype(F32), logf, precision=lax.Precision.HIGHEST, preferred_element_type=F32)
    return dict(sig=sig, f=f, kk=1.0 - f, qraw=qraw, sq=sq, qs=qraw * sq, gsum=gsum,
                eg=jnp.exp(gsum), ekd=jnp.exp(glast - gsum), a=jnp.exp(glast))


def _head_sums(x):
    parts = [jnp.broadcast_to(jnp.sum(x[:, h * HEAD_A:(h + 1) * HEAD_A], axis=1, keepdims=True), (x.shape[0], HEAD_A))
             for h in range(A_HEADS)]
    return jnp.concatenate(parts, axis=1)


def _lag_decay(g, j, row):
    back = jnp.exp(jnp.where(row >= j, g - pltpu.roll(g, j, 0), NEG))
    ahead = jnp.exp(jnp.where(row < SUB_CHUNK - j, pltpu.roll(g, SUB_CHUNK - j, 0) - g, NEG))
    return back, ahead


def _hgrn2_intra(g, kk, qs, v):
    row = lax.broadcasted_iota(jnp.int32, g.shape, 0)
    o = _head_sums(qs * kk) * v
    for j in range(1, SUB_CHUNK):
        decay = jnp.exp(jnp.where(row >= j, g - pltpu.roll(g, j, 0), NEG))
        o = o + _head_sums(qs * pltpu.roll(kk, j, 0) * decay) * pltpu.roll(v, j, 0)
    return o


def _hgrn2_intra_bwd(g, kk, qs, v, do):
    row = lax.broadcasted_iota(jnp.int32, g.shape, 0)
    dsc = _head_sums(do * v)
    dqs, dkk, dv = dsc * kk, dsc * qs, _head_sums(qs * kk) * do
    for j in range(1, SUB_CHUNK):
        back, ahead = _lag_decay(g, j, row)
        dqs = dqs + _head_sums(do * pltpu.roll(v, j, 0)) * pltpu.roll(kk, j, 0) * back
        q_up, do_up = pltpu.roll(qs, SUB_CHUNK - j, 0), pltpu.roll(do, SUB_CHUNK - j, 0)
        dkk = dkk + _head_sums(do_up * v) * q_up * ahead
        dv = dv + _head_sums(q_up * kk * ahead) * do_up
    return dqs, dkk, dv


def _hgrn2_fwd(proj, lb_table, a_norm, name):
    t = proj.shape[0]
    tb = HGRN_BLOCK
    n_tb = SEQ // tb
    n_seq = t // SEQ
    n_sub = tb // SUB_CHUNK

    def body(q_ref, f_ref, i_ref, g_ref, lbt_ref, an_ref, o_ref, pre_ref, sts_ref, st_ref,
             gs_ref, kk_ref, qs_ref, eg_ref, ekd_ref, a_ref):
        @pl.when(pl.program_id(1) == 0)
        def _():
            st_ref[...] = jnp.zeros_like(st_ref)

        an = an_ref[...]
        blk = _hgrn2_block(q_ref, f_ref, _lower_bound(lbt_ref[...]))
        for ref, key in ((gs_ref, "gsum"), (kk_ref, "kk"), (qs_ref, "qs"), (eg_ref, "eg"), (ekd_ref, "ekd"), (a_ref, "a")):
            ref[...] = blk[key]

        def step(c, carry):
            rows = pl.ds(pl.multiple_of(c * SUB_CHUNK, SUB_CHUNK), SUB_CHUNK)
            kk, qs, v = kk_ref[rows, :], qs_ref[rows, :], i_ref[rows, :]
            o = _hgrn2_intra(gs_ref[rows, :], kk, qs, v)
            qg, kd, vb = (qs * eg_ref[rows, :]).astype(BF16), (kk * ekd_ref[rows, :]).astype(BF16), v.astype(BF16)
            for h in range(A_HEADS):
                lanes = slice(h * HEAD_A, (h + 1) * HEAD_A)
                st = st_ref[h]
                sts_ref[0, c, h] = st
                o_h = o[:, lanes] + _dot_nt(qg[:, lanes], st.astype(BF16))
                st_ref[h] = st * a_ref[rows, lanes][0:1] + _dot_tn(vb[:, lanes], kd[:, lanes])
                pre_ref[rows, lanes] = o_h
                graw = g_ref[rows, lanes]
                o_ref[rows, lanes] = (_rms(o_h, an[:, lanes]) * (graw * _sigmoid(graw))).astype(BF16)
            return carry

        lax.fori_loop(0, n_sub, step, 0, unroll=2)

    def col(k):
        return pl.BlockSpec((tb, A_WIDTH), lambda b, s, k=k: (b * n_tb + s, k))

    out_rows = pl.BlockSpec((tb, A_WIDTH), lambda b, s: (b * n_tb + s, 0))
    return pl.pallas_call(
        body, name=name, grid=(n_seq, n_tb),
        in_specs=[col(0), col(1), col(2), col(3),
                  pl.BlockSpec((3, A_WIDTH), lambda b, s: (0, 0)), pl.BlockSpec((1, A_WIDTH), lambda b, s: (0, 0))],
        out_specs=[out_rows, out_rows,
                   pl.BlockSpec((1, n_sub, A_HEADS, HEAD_A, HEAD_A), lambda b, s: (b * n_tb + s, 0, 0, 0, 0))],
        out_shape=[jax.ShapeDtypeStruct((t, A_WIDTH), BF16), jax.ShapeDtypeStruct((t, A_WIDTH), F32),
                   jax.ShapeDtypeStruct((n_seq * n_tb, n_sub, A_HEADS, HEAD_A, HEAD_A), F32)],
        scratch_shapes=[pltpu.VMEM((A_HEADS, HEAD_A, HEAD_A), F32)] + [pltpu.VMEM((tb, A_WIDTH), F32)] * 6,
        compiler_params=_params(("parallel", "arbitrary")),
    )(proj, proj, proj, proj, lb_table, a_norm)


def _hgrn2_bwd(proj, dcat, pre, states, lb_table, a_norm, name):
    t = proj.shape[0]
    tb = HGRN_BLOCK
    n_tb = SEQ // tb
    n_seq = t // SEQ
    n_sub = tb // SUB_CHUNK

    def body(q_ref, f_ref, i_ref, g_ref, do_ref, pre_ref, sts_ref, lbt_ref, an_ref, dp_ref, dlb_ref, dan_ref, dst_ref,
             gs_ref, kk_ref, qs_ref, eg_ref, ekd_ref, a_ref, dpre_ref, dlf_ref, dqs_ref, dkk_ref):
        b, s = pl.program_id(0), pl.program_id(1)

        @pl.when(s == 0)
        def _():
            dst_ref[...] = jnp.zeros_like(dst_ref)

        @pl.when((b == 0) & (s == 0))
        def _():
            dlb_ref[...] = jnp.zeros_like(dlb_ref)
            dan_ref[...] = jnp.zeros_like(dan_ref)

        lb = _lower_bound(lbt_ref[...])
        an = an_ref[...]
        heads = [slice(h * HEAD_A, (h + 1) * HEAD_A) for h in range(A_HEADS)]
        blk = _hgrn2_block(q_ref, f_ref, lb)
        for ref, key in ((gs_ref, "gsum"), (kk_ref, "kk"), (qs_ref, "qs"), (eg_ref, "eg"), (ekd_ref, "ekd"), (a_ref, "a")):
            ref[...] = blk[key]
        for h, lanes in enumerate(heads):
            graw, o = g_ref[:, lanes], pre_ref[:, lanes]
            sg = _sigmoid(graw)
            dout = do_ref[:, lanes]
            d_o, dgr = _rms_bwd(o, an[:, lanes], dout * (graw * sg))
            dan_ref[0:1, lanes] += jnp.sum(dgr, axis=0, keepdims=True)
            dp_ref[:, 3 * A_WIDTH + h * HEAD_A:3 * A_WIDTH + (h + 1) * HEAD_A] = (
                dout * _rms(o, an[:, lanes]) * (sg * (1.0 + graw * (1.0 - sg)))).astype(BF16)
            dpre_ref[:, lanes] = d_o

        tri_t = (lax.broadcasted_iota(jnp.int32, (SUB_CHUNK, SUB_CHUNK), 0)
                 <= lax.broadcasted_iota(jnp.int32, (SUB_CHUNK, SUB_CHUNK), 1)).astype(F32)

        def back(k, carry):
            c = n_sub - 1 - k
            rows = pl.ds(pl.multiple_of(c * SUB_CHUNK, SUB_CHUNK), SUB_CHUNK)
            g, kk, qs, v, d_o = gs_ref[rows, :], kk_ref[rows, :], qs_ref[rows, :], i_ref[rows, :], dpre_ref[rows, :]
            eg, ekd, a = eg_ref[rows, :], ekd_ref[rows, :], a_ref[rows, :]
            dqs, dkk, dv = _hgrn2_intra_bwd(g, kk, qs, v, d_o)
            qg_f, kd_f = qs * eg, kk * ekd
            qg, kd, vb, dob = qg_f.astype(BF16), kd_f.astype(BF16), v.astype(BF16), d_o.astype(BF16)
            dqg, dkd, da, dv_st = [], [], [], []
            for h, lanes in enumerate(heads):
                st, dst = sts_ref[0, c, h], dst_ref[h]
                dstb = dst.astype(BF16)
                dqg.append(_dot(dob[:, lanes], st.astype(BF16)))
                dv_st.append(_dot_nt(kd[:, lanes], dstb))
                dkd.append(_dot(vb[:, lanes], dstb))
                da.append(jnp.broadcast_to(jnp.sum(dst * st, axis=0, keepdims=True), (SUB_CHUNK, HEAD_A)))
                dst_ref[h] = dst * a[0:1, lanes] + _dot_tn(dob[:, lanes], qg[:, lanes])
            dqg, dkd, da, dv_st = [jnp.concatenate(p, axis=1) for p in (dqg, dkd, da, dv_st)]
            d_gsum = qs * dqs - kk * dkk + dqg * qg_f - dkd * kd_f
            d_glast = jnp.sum(dkd * kd_f, axis=0, keepdims=True) + da * a
            dlf_ref[rows, :] = jnp.dot(tri_t, d_gsum, precision=lax.Precision.HIGHEST,
                                       preferred_element_type=F32) + d_glast
            dqs_ref[rows, :] = dqs + dqg * eg
            dkk_ref[rows, :] = dkk + dkd * ekd
            dp_ref[rows, 2 * A_WIDTH:3 * A_WIDTH] = (dv + dv_st).astype(BF16)
            return carry

        lax.fori_loop(0, n_sub, back, 0, unroll=2)
        sig, sq, qraw = blk["sig"], blk["sq"], blk["qraw"]
        d_f = dlf_ref[...] / blk["f"] - dkk_ref[...]
        dlb_ref[0:1, :] += jnp.sum(d_f * (1.0 - sig), axis=0, keepdims=True)
        dp_ref[:, 0:A_WIDTH] = (dqs_ref[...] * (sq * (1.0 + qraw * (1.0 - sq)))).astype(BF16)
        dp_ref[:, A_WIDTH:2 * A_WIDTH] = (d_f * (1.0 - lb) * sig * (1.0 - sig)).astype(BF16)

    def rev(s):
        return n_tb - 1 - s

    def col(k):
        return pl.BlockSpec((tb, A_WIDTH), lambda b, s, k=k: (b * n_tb + rev(s), k))

    acc8 = pl.BlockSpec((SUBLANES, A_WIDTH), lambda b, s: (0, 0))
    return pl.pallas_call(
        body, name=name, grid=(n_seq, n_tb),
        in_specs=[col(0), col(1), col(2), col(3), col(0), col(0),
                  pl.BlockSpec((1, n_sub, A_HEADS, HEAD_A, HEAD_A), lambda b, s: (b * n_tb + rev(s), 0, 0, 0, 0)),
                  pl.BlockSpec((3, A_WIDTH), lambda b, s: (0, 0)), pl.BlockSpec((1, A_WIDTH), lambda b, s: (0, 0))],
        out_specs=[pl.BlockSpec((tb, 4 * A_WIDTH), lambda b, s: (b * n_tb + rev(s), 0)), acc8, acc8],
        out_shape=[jax.ShapeDtypeStruct((t, 4 * A_WIDTH), BF16)] + [jax.ShapeDtypeStruct((SUBLANES, A_WIDTH), F32)] * 2,
        scratch_shapes=[pltpu.VMEM((A_HEADS, HEAD_A, HEAD_A), F32)] + [pltpu.VMEM((tb, A_WIDTH), F32)] * 10,
        compiler_params=_params(("arbitrary", "arbitrary")),
    )(proj, proj, proj, proj, dcat, pre, states, lb_table, a_norm)


GMLP_ROWS = 512


def _gmlp_chunk(ub, vb, ln_g, ln_b, ws, bias):
    u = [_gelu(a) for a in ub]
    v = [_gelu(a) for a in vb]
    mu = sum(jnp.sum(a, axis=-1, keepdims=True) for a in v) * (1.0 / B_WIDTH)
    cen = [a - mu for a in v]
    var = sum(jnp.sum(a * a, axis=-1, keepdims=True) for a in cen) * (1.0 / B_WIDTH)
    inv = lax.rsqrt(var + EPS)
    r = lax.broadcasted_iota(jnp.int32, (B_CHUNK, B_CHUNK), 0)
    c = lax.broadcasted_iota(jnp.int32, (B_CHUNK, B_CHUNK), 1)
    outs = []
    for g in range(B_GROUPS):
        vn = (cen[g] * inv * ln_g[g] + ln_b[g]).astype(BF16)
        wm = jnp.where(c <= r, ws[g], 0.0).astype(BF16)
        outs.append(u[g] * (_dot(wm, vn) + bias[g]))
    return outs


def _lane_groups(ref, rows=slice(None)):
    return [ref[rows, g * LANES:(g + 1) * LANES] for g in range(B_GROUPS)]


def _gmlp_fwd(proj, ln_g, ln_b, ws, bias_t, name):
    t = proj.shape[0]
    tm = GMLP_ROWS

    def body(u_ref, v_ref, lg_ref, lb_ref, ws_ref, bt_ref, o_ref):
        for ch in range(tm // B_CHUNK):
            rows = slice(ch * B_CHUNK, (ch + 1) * B_CHUNK)
            outs = _gmlp_chunk(_lane_groups(u_ref, rows), _lane_groups(v_ref, rows), _lane_groups(lg_ref),
                               _lane_groups(lb_ref), [ws_ref[g] for g in range(B_GROUPS)],
                               [bt_ref[:, g:g + 1] for g in range(B_GROUPS)])
            for g in range(B_GROUPS):
                o_ref[rows, g * LANES:(g + 1) * LANES] = outs[g].astype(BF16)

    vec = pl.BlockSpec((1, B_WIDTH), lambda i: (0, 0))
    return pl.pallas_call(
        body, name=name, grid=(t // tm,),
        in_specs=[pl.BlockSpec((tm, B_WIDTH), lambda i: (i, 4)), pl.BlockSpec((tm, B_WIDTH), lambda i: (i, 5)), vec, vec,
                  pl.BlockSpec((B_GROUPS, B_CHUNK, B_CHUNK), lambda i: (0, 0, 0)),
                  pl.BlockSpec((B_CHUNK, B_GROUPS), lambda i: (0, 0))],
        out_specs=pl.BlockSpec((tm, B_WIDTH), lambda i: (i, 0)),
        out_shape=jax.ShapeDtypeStruct((t, B_WIDTH), BF16),
        compiler_params=_params(("parallel",)),
    )(proj, proj, ln_g, ln_b, ws, bias_t)


def _gmlp_bwd(proj, dcat, ln_g, ln_b, ws, bias_t, name):
    t = proj.shape[0]
    tm = GMLP_ROWS

    def body(u_ref, v_ref, do_ref, lg_ref, lb_ref, ws_ref, bt_ref, duv_ref, dlg_ref, dlb_ref, dws_ref, dbt_ref):
        @pl.when(pl.program_id(0) == 0)
        def _():
            dlg_ref[...] = jnp.zeros_like(dlg_ref)
            dlb_ref[...] = jnp.zeros_like(dlb_ref)
            dws_ref[...] = jnp.zeros_like(dws_ref)
            dbt_ref[...] = jnp.zeros_like(dbt_ref)

        for ch in range(tm // B_CHUNK):
            rows = slice(ch * B_CHUNK, (ch + 1) * B_CHUNK)
            _, vjp = jax.vjp(
                _gmlp_chunk, _lane_groups(u_ref, rows), _lane_groups(v_ref, rows), _lane_groups(lg_ref),
                _lane_groups(lb_ref), [ws_ref[g] for g in range(B_GROUPS)],
                [bt_ref[:, g:g + 1] for g in range(B_GROUPS)])
            du, dv, dlg, dlb, dw, dbt = vjp(_lane_groups(do_ref, rows))
            for g in range(B_GROUPS):
                lanes = slice(g * LANES, (g + 1) * LANES)
                duv_ref[rows, lanes] = du[g].astype(BF16)
                duv_ref[rows, B_WIDTH + g * LANES:B_WIDTH + (g + 1) * LANES] = dv[g].astype(BF16)
                dlg_ref[0:1, lanes] += dlg[g]
                dlb_ref[0:1, lanes] += dlb[g]
                dws_ref[g] += dw[g]
                dbt_ref[:, g:g + 1] += dbt[g]

    vec = pl.BlockSpec((1, B_WIDTH), lambda i: (0, 0))
    acc8 = pl.BlockSpec((SUBLANES, B_WIDTH), lambda i: (0, 0))
    ws_spec = pl.BlockSpec((B_GROUPS, B_CHUNK, B_CHUNK), lambda i: (0, 0, 0))
    bt_spec = pl.BlockSpec((B_CHUNK, B_GROUPS), lambda i: (0, 0))
    return pl.pallas_call(
        body, name=name, grid=(t // tm,),
        in_specs=[pl.BlockSpec((tm, B_WIDTH), lambda i: (i, 4)), pl.BlockSpec((tm, B_WIDTH), lambda i: (i, 5)),
                  pl.BlockSpec((tm, B_WIDTH), lambda i: (i, 1)), vec, vec, ws_spec, bt_spec],
        out_specs=[pl.BlockSpec((tm, 2 * B_WIDTH), lambda i: (i, 0)), acc8, acc8, ws_spec, bt_spec],
        out_shape=[jax.ShapeDtypeStruct((t, 2 * B_WIDTH), BF16), jax.ShapeDtypeStruct((SUBLANES, B_WIDTH), F32),
                   jax.ShapeDtypeStruct((SUBLANES, B_WIDTH), F32),
                   jax.ShapeDtypeStruct((B_GROUPS, B_CHUNK, B_CHUNK), F32),
                   jax.ShapeDtypeStruct((B_CHUNK, B_GROUPS), F32)],
        compiler_params=_params(("arbitrary",)),
    )(proj, proj, dcat, ln_g, ln_b, ws, bias_t)


QK_SCALE = 1.0 / math.sqrt(C_HEAD_DIM)
ATTN_UNROLL = 8
LANE_GROUPS = D_MODEL // LANES
Q_BLOCKS = SEQ // C_BLOCK


def _attn_window(i, d):
    sub_blocks = Q_BLOCKS // d
    q0 = pl.multiple_of(i * C_BLOCK, C_BLOCK)
    k0 = pl.multiple_of(jnp.maximum(i - 1, 0) * C_BLOCK, C_BLOCK)
    key = k0 + lax.broadcasted_iota(jnp.int32, (C_BLOCK, 2 * C_BLOCK), 1)
    dist = (q0 + lax.broadcasted_iota(jnp.int32, (C_BLOCK, 2 * C_BLOCK), 0)) - key
    own_subsequence = (key >= q0) | (i % sub_blocks > 0)
    return pl.ds(q0, C_BLOCK), pl.ds(k0, 2 * C_BLOCK), (dist >= 0) & (dist <= C_BLOCK) & own_subsequence


def _head_masks():
    lane = lax.broadcasted_iota(jnp.int32, (C_BLOCK, LANES), 1)
    return [lane < C_HEAD_DIM, lane >= C_HEAD_DIM]


def _flat_spec(col_of):
    return pl.BlockSpec((1, SEQ, LANES), lambda b, g: (b, 0, col_of(g)))


def _put_heads(tile, g, col0, col1):
    lane = lax.broadcasted_iota(jnp.int32, tile.shape, 1)
    return jnp.where(lane == 2 * g, col0, jnp.where(lane == 2 * g + 1, col1, tile))


def _get_head(tile, h):
    lane = lax.broadcasted_iota(jnp.int32, tile.shape, 1)
    return jnp.sum(jnp.where(lane == h, tile, 0.0), axis=1, keepdims=True)


PER_HEAD_SPEC = pl.BlockSpec((1, SEQ, LANES), lambda b, g: (b, 0, 0))


def _attn_branch_fwd(qkv, name):
    n_seq, d, l, _ = qkv.shape
    flat = qkv.reshape(n_seq, SEQ, ODD_IN)

    def body(q_ref, k_ref, v_ref, o_ref, m_ref, l_ref):
        heads = _head_masks()
        g = pl.program_id(1)

        @pl.when(g == 0)
        def _():
            m_ref[...] = jnp.zeros_like(m_ref)
            l_ref[...] = jnp.zeros_like(l_ref)

        def block(i, carry):
            rows, keys, mask = _attn_window(i, d)
            q, k, v = q_ref[0, rows, :], k_ref[0, keys, :], v_ref[0, keys, :]
            res = []
            for hm in heads:
                s = jnp.where(mask, _dot_nt(jnp.where(hm, q, 0), k) * QK_SCALE, NEG)
                m = jnp.max(s, axis=-1, keepdims=True)
                p = jnp.exp(s - m)
                res.append((_dot(p.astype(BF16), v), m, jnp.sum(p, axis=-1, keepdims=True)))
            o_ref[0, rows, :] = jnp.where(heads[0], res[0][0], res[1][0])
            m_ref[0, rows, :] = _put_heads(m_ref[0, rows, :], g, res[0][1], res[1][1])
            l_ref[0, rows, :] = _put_heads(l_ref[0, rows, :], g, res[0][2], res[1][2])
            return carry

        lax.fori_loop(0, Q_BLOCKS, block, 0, unroll=ATTN_UNROLL)

    o, m, l_sum = pl.pallas_call(
        body, name=name, grid=(n_seq, LANE_GROUPS),
        in_specs=[_flat_spec(lambda g: g), _flat_spec(lambda g: LANE_GROUPS + g),
                  _flat_spec(lambda g: 2 * LANE_GROUPS + g)],
        out_specs=[_flat_spec(lambda g: g), PER_HEAD_SPEC, PER_HEAD_SPEC],
        out_shape=[jax.ShapeDtypeStruct((n_seq, SEQ, D_MODEL), F32)] + [jax.ShapeDtypeStruct((n_seq, SEQ, LANES), F32)] * 2,
        compiler_params=_params(("parallel", "arbitrary")),
    )(flat, flat, flat)
    return [o.reshape(n_seq, d, l, D_MODEL), m.reshape(n_seq, d, l, LANES), l_sum.reshape(n_seq, d, l, LANES)]


def _attn_merge(branches, name):
    n_seq = branches[0][0].shape[0]
    t = n_seq * SEQ
    tm = MERGE_TILE

    def body(*refs):
        ins = refs[:9]
        o_ref, ob_ref, lse_ref = refs[9:12]
        nat = refs[12:]
        for b, d in enumerate(C_DILATIONS[1:]):
            for k in range(3):
                _load_dilated(ins[3 + 3 * b + k], d, nat[3 * b + k])
        ms = [ins[1][0, 0], nat[1][0], nat[4][0]]
        ls = [ins[2][0, 0], nat[2][0], nat[5][0]]
        m_all = jnp.maximum(jnp.maximum(ms[0], ms[1]), ms[2])
        ws = [jnp.exp(ms[b] - m_all) for b in range(3)]
        lane = lax.broadcasted_iota(jnp.int32, m_all.shape, 1)
        total = jnp.where(lane < C_HEADS, ws[0] * ls[0] + ws[1] * ls[1] + ws[2] * ls[2], 1.0)
        lse_ref[...] = m_all + jnp.log(total)
        first_head = lane < C_HEAD_DIM
        for p in range(LANE_GROUPS):
            lanes = slice(p * LANES, (p + 1) * LANES)
            spread = lambda c: jnp.where(first_head, c[:, 2 * p:2 * p + 1], c[:, 2 * p + 1:2 * p + 2])
            os_ = [ins[0][0, 0, :, lanes], nat[0][p], nat[3][p]]
            o = (spread(ws[0]) * os_[0] + spread(ws[1]) * os_[1] + spread(ws[2]) * os_[2]) / spread(total)
            o_ref[:, lanes] = o
            ob_ref[:, lanes] = o.astype(BF16)

    row = pl.BlockSpec((tm, D_MODEL), lambda i: (i, 0))
    flat = [a for br in branches for a in br]
    in_specs = []
    for wide, narrow in zip(_dilated_specs(tm, D_MODEL, lambda: 0), _dilated_specs(tm, LANES, lambda: 0)):
        in_specs += [wide, narrow, narrow]
    per_head = pltpu.VMEM((1, tm, LANES), F32)
    return pl.pallas_call(
        body, name=name, grid=(t // tm,), in_specs=in_specs,
        out_specs=[row, row, pl.BlockSpec((tm, LANES), lambda i: (i, 0))],
        out_shape=[jax.ShapeDtypeStruct((t, D_MODEL), F32), jax.ShapeDtypeStruct((t, D_MODEL), BF16),
                   jax.ShapeDtypeStruct((t, LANES), F32)],
        scratch_shapes=[pltpu.VMEM((LANE_GROUPS, tm, LANES), F32), per_head, per_head] * 2,
        compiler_params=_params(("parallel",)),
    )(*flat)


def _attn_branch_bwd(qkv, dout, lse, delta, name):
    n_seq, d, l, _ = qkv.shape
    flat = lambda a: a.reshape(n_seq, SEQ, a.shape[-1])

    def body(q_ref, k_ref, v_ref, do_ref, lse_nat_ref, dl_nat_ref, dq_ref, dk_ref, dv_ref, lse_ref, dl_ref,
             dkt_ref, dvt_ref):
        heads = _head_masks()
        g = pl.program_id(1)
        dkt_ref[...] = jnp.zeros_like(dkt_ref)
        dvt_ref[...] = jnp.zeros_like(dvt_ref)
        for nat_ref, dst_ref in ((lse_nat_ref, lse_ref), (dl_nat_ref, dl_ref)):
            for r in range(d):
                rows = pl.ds(r, l, stride=d) if d > 1 else slice(None)
                dst_ref[r * l:(r + 1) * l, :] = nat_ref.at[0][rows, :]

        def block(i, carry):
            rows, keys, mask = _attn_window(i, d)
            q, do = q_ref[0, rows, :], do_ref[0, rows, :]
            k, v = k_ref[0, keys, :], v_ref[0, keys, :]
            lse_b, dl_b = lse_ref[rows, :], dl_ref[rows, :]
            dq, dk, dv = [], None, None
            for hh, hm in enumerate(heads):
                qh, doh = jnp.where(hm, q, 0), jnp.where(hm, do, 0)
                s = jnp.where(mask, _dot_nt(qh, k) * QK_SCALE, NEG)
                p = jnp.exp(s - _get_head(lse_b, 2 * g + hh))
                ds = (p * (_dot_nt(doh, v) - _get_head(dl_b, 2 * g + hh)) * QK_SCALE).astype(BF16)
                dq.append(_dot(ds, k))
                dk_h, dv_h = _dot_tn(qh, ds), _dot_tn(doh, p.astype(BF16))
                dk = dk_h if dk is None else dk + dk_h
                dv = dv_h if dv is None else dv + dv_h
            dq_ref[0, rows, :] = jnp.where(heads[0], dq[0], dq[1])
            dkt_ref[:, keys] += dk
            dvt_ref[:, keys] += dv
            return carry

        lax.fori_loop(0, Q_BLOCKS, block, 0, unroll=ATTN_UNROLL)
        for c in range(SEQ // ROW_TILE):
            rows = slice(c * ROW_TILE, (c + 1) * ROW_TILE)
            dk_ref[0, rows, :] = dkt_ref[:, rows].T
            dv_ref[0, rows, :] = dvt_ref[:, rows].T

    act = _flat_spec(lambda g: g)
    outs = pl.pallas_call(
        body, name=name, grid=(n_seq, LANE_GROUPS),
        in_specs=[_flat_spec(lambda g: g), _flat_spec(lambda g: LANE_GROUPS + g),
                  _flat_spec(lambda g: 2 * LANE_GROUPS + g), act, PER_HEAD_SPEC, PER_HEAD_SPEC],
        out_specs=[act] * 3,
        out_shape=[jax.ShapeDtypeStruct((n_seq, SEQ, D_MODEL), F32)] * 3,
        scratch_shapes=[pltpu.VMEM((SEQ, LANES), F32)] * 2 + [pltpu.VMEM((LANES, SEQ), F32)] * 2,
        compiler_params=_params(("parallel", "parallel")),
    )(flat(qkv), flat(qkv), flat(qkv), flat(dout), lse, delta)
    return [o.reshape(n_seq, d, l, D_MODEL) for o in outs]


def _attn_combine_bwd(grads, rope, name):
    n_seq = grads[0][0].shape[0]
    t = n_seq * SEQ
    tm = MERGE_TILE

    def body(*refs):
        c_ref, s_ref, o_ref, nat4_ref, nat16_ref = refs[9:]
        for sec in range(3):
            _load_dilated(refs[3 + sec], 4, nat4_ref)
            _load_dilated(refs[6 + sec], 16, nat16_ref)
            for p in range(LANE_GROUPS):
                blk = refs[sec][0, 0, :, p * LANES:(p + 1) * LANES] + nat4_ref[p] + nat16_ref[p]
                if sec < 2:
                    blk = blk * c_ref[...] - _swap_halves(blk) * s_ref[...]
                o_ref[:, sec * D_MODEL + p * LANES:sec * D_MODEL + (p + 1) * LANES] = blk.astype(BF16)

    tab = pl.BlockSpec((tm, LANES), lambda i: (i, 0))
    flat = [a for br in grads for a in br]
    in_specs = []
    for spec in _dilated_specs(tm, D_MODEL, lambda: 0):
        in_specs += [spec] * 3
    return pl.pallas_call(
        body, name=name, grid=(t // tm,), in_specs=in_specs + [tab, tab],
        out_specs=pl.BlockSpec((tm, ODD_IN), lambda i: (i, 0)),
        out_shape=jax.ShapeDtypeStruct((t, ODD_IN), BF16),
        scratch_shapes=[pltpu.VMEM((LANE_GROUPS, tm, LANES), F32)] * 2,
        compiler_params=_params(("parallel",)),
    )(*flat, *rope)


def _loss_grad(y, target, name):
    t = y.shape[0]
    tm = ROW_TILE

    def body(y_ref, t_ref, d_ref, l_ref):
        diff = y_ref[...] - t_ref[...]
        d_ref[...] = diff * (1.0 / D_MODEL)
        _acc_rows8(l_ref, _rows8(diff * diff) * (0.5 / D_MODEL), pl.program_id(0) == 0)

    row = pl.BlockSpec((tm, D_MODEL), lambda i: (i, 0))
    return pl.pallas_call(
        body, name=name, grid=(t // tm,), in_specs=[row, row],
        out_specs=[row, pl.BlockSpec((SUBLANES, D_MODEL), lambda i: (0, 0))],
        out_shape=[jax.ShapeDtypeStruct((t, D_MODEL), F32), jax.ShapeDtypeStruct((SUBLANES, D_MODEL), F32)],
        compiler_params=_params(("arbitrary",)),
    )(y, target)


def _adamw(w, g, m, v):
    m = ADAM_B1 * m + (1.0 - ADAM_B1) * g
    v = ADAM_B2 * v + (1.0 - ADAM_B2) * jnp.square(g)
    m_hat = m / (1.0 - ADAM_B1 ** ADAM_STEP)
    v_hat = v / (1.0 - ADAM_B2 ** ADAM_STEP)
    delta = -ADAM_LR * (m_hat / (jnp.sqrt(v_hat) + ADAM_EPS) + ADAM_WD * w)
    return delta, m, v


def _adamw_sharded(parts, w, m, v, name):
    n_layers, rows, cols = w.shape
    tr = min(rows, 256)

    def body(*refs):
        p_refs = refs[:n_layers]
        w_ref, m_ref, v_ref, g_ref, d_ref, mo_ref, vo_ref = refs[n_layers:]
        layer = pl.program_id(0)
        g = None
        for l, p_ref in enumerate(p_refs):
            g_l = p_ref[0].astype(F32)
            for s in range(1, N_DEV):
                g_l = g_l + p_ref[s].astype(F32)
            g = g_l if g is None else jnp.where(layer == l, g_l, g)
        delta, mn, vn = _adamw(w_ref[0], g, m_ref[0], v_ref[0])
        g_ref[0] = g
        d_ref[0] = delta
        mo_ref[0] = mn
        vo_ref[0] = vn

    def part_spec(l):
        return pl.BlockSpec((N_DEV, tr, cols), lambda a, i: (0, jnp.where(a == l, i, 0), 0))

    row = pl.BlockSpec((1, tr, cols), lambda a, i: (a, i, 0))
    return pl.pallas_call(
        body, name=name, grid=(n_layers, rows // tr),
        in_specs=[part_spec(l) for l in range(n_layers)] + [row, row, row],
        out_specs=[row] * 4, out_shape=[jax.ShapeDtypeStruct(w.shape, F32)] * 4,
        compiler_params=_params(("arbitrary", "arbitrary")),
    )(*parts, w, m, v)


def _small_update(gathered, where, weights, moments_m, moments_v, lb_index, name):
    n = len(weights)
    n_g = len(gathered)

    def body(*refs):
        g_refs = refs[:n_g]
        w_refs, m_refs, v_refs = refs[n_g:n_g + n], refs[n_g + n:n_g + 2 * n], refs[n_g + 2 * n:n_g + 3 * n]
        outs = refs[n_g + 3 * n:]

        def total(k):
            array, rows, lanes = where[k]
            ref = g_refs[array]
            index = (slice(None),) * (len(ref.shape) - 1) if rows is None else (rows, lanes)
            acc = ref[(0,) + index]
            for s in range(1, N_DEV):
                acc = acc + ref[(s,) + index]
            return acc

        loss_rows = total(n)
        outs[0][...] = jnp.sum(jnp.sum(loss_rows, axis=1, keepdims=True), axis=0, keepdims=True)
        for k in range(n):
            part = total(k)
            if k == lb_index:
                dlb = jnp.sum(part, axis=0, keepdims=True)
                tab = w_refs[k][...]
                e = jnp.exp(tab - jnp.max(tab, axis=0, keepdims=True))
                p = e / jnp.sum(e, axis=0, keepdims=True)
                first = lax.broadcasted_iota(jnp.int32, p.shape, 0) == 0
                grads = [(slice(None), p * (jnp.where(first, dlb, 0.0) - p[0:1, :] * dlb))]
            elif part.shape == w_refs[k].shape:
                grads = [(slice(None), part)]
            else:
                grads = [(slice(l, l + 1), jnp.sum(part[l * SUBLANES:(l + 1) * SUBLANES], axis=0, keepdims=True))
                         for l in range(w_refs[k].shape[0])]
            for rows, g in grads:
                delta, mn, vn = _adamw(w_refs[k][rows], g, m_refs[k][rows], v_refs[k][rows])
                outs[1 + 4 * k][rows] = g
                outs[2 + 4 * k][rows] = delta
                outs[3 + 4 * k][rows] = mn
                outs[4 + 4 * k][rows] = vn

    vmem = pl.BlockSpec(memory_space=pltpu.VMEM)
    out_shape = [jax.ShapeDtypeStruct((1, 1), F32)]
    for w in weights:
        out_shape += [jax.ShapeDtypeStruct(w.shape, F32)] * 4
    args = list(gathered) + list(weights) + list(moments_m) + list(moments_v)
    return pl.pallas_call(
        body, name=name, in_specs=[vmem] * len(args), out_specs=[vmem] * len(out_shape), out_shape=out_shape,
        compiler_params=pltpu.CompilerParams(vmem_limit_bytes=VMEM_LIMIT),
    )(*args)


def kernel(x, positions, norm_mix_pre, norm_mix_post, norm_ffn_pre, norm_ffn_post, w_in_even, lb_table, a_norm, b_ln_g, b_ln_b, b_ws, b_bias, w_out_even, w_in_odd, w_out_odd, w_ff1, w_ff2, loss_target, m_norm_mix_pre, m_norm_mix_post, m_norm_ffn_pre, m_norm_ffn_post, m_w_in_even, m_lb_table, m_a_norm, m_b_ln_g, m_b_ln_b, m_b_ws, m_b_bias, m_w_out_even, m_w_in_odd, m_w_out_odd, m_w_ff1, m_w_ff2, v_norm_mix_pre, v_norm_mix_post, v_norm_ffn_pre, v_norm_ffn_post, v_w_in_even, v_lb_table, v_a_norm, v_b_ln_g, v_b_ln_b, v_b_ws, v_b_bias, v_w_out_even, v_w_in_odd, v_w_out_odd, v_w_ff1, v_w_ff2):
    n_seq = x.shape[0]
    t = n_seq * SEQ
    x0 = x.reshape(t, D_MODEL)
    target = loss_target.reshape(t, D_MODEL)

    me = _my_slot().astype(jnp.int32).reshape(1)

    order = ["in_e", "out_e", "ff1_0", "ff2_0", "in_o", "out_o", "ff1_1", "ff2_1"]
    shards = dict(in_e=w_in_even[0], out_e=w_out_even[0], in_o=w_in_odd[0], out_o=w_out_odd[0],
                  ff1_0=w_ff1[0], ff1_1=w_ff1[1], ff2_0=w_ff2[0], ff2_1=w_ff2[1])
    by_columns = ("in_e", "in_o", "ff1_0", "ff1_1")
    lands = [_place_own_columns(shards[k], me, "place_" + k) if k in by_columns
             else _place_own(shards[k], me, "place_" + k, False) for k in order]
    g_send, g_recv, lands, _, g_token = _exchange_start(lands, [None] * len(order), "gather_start")

    def get_w(keys, after):
        ks = [order.index(k) for k in keys]
        return _exchange_wait([lands[k] for k in ks], [None] * len(ks), [g_send[k] for k in ks],
                              [g_recv[k] for k in ks], after, "gather_wait_" + keys[0])

    sent = {}

    def put_g(group, blocks):
        keys = list(blocks)
        own = [_place_own(blocks[k], me, "own_" + k, True) for k in keys]
        send_sems, recv_sems, own, srcs, token = _exchange_start(own, [blocks[k] for k in keys], "scatter_start_" + group)
        sent[group] = (keys, own, srcs, send_sems, recv_sems)
        return token

    rope = _rope_tables(positions)
    bias_t = b_bias[0].T
    grads = _local_step(x0, target, rope, norm_mix_pre, norm_mix_post, norm_ffn_pre, norm_ffn_post, lb_table,
                        a_norm, b_ln_g, b_ln_b, b_ws[0], bias_t, get_w, put_g, g_token)
    (dx0, loss_part, dg_mix_pre, dg_mix_post, dg_ffn_pre, dg_ffn_post, d_lb, d_a_norm, d_ln_g, d_ln_b, d_ws,
     d_bias_t) = grads

    recv = {}
    for group, (keys, own, srcs, send_sems, recv_sems) in sent.items():
        done = _exchange_wait(own, srcs, send_sems, recv_sems, dx0, "scatter_wait_" + group)
        recv.update(zip(keys, done))
    big = [("w_in_even", ["in_e"], w_in_even, m_w_in_even, v_w_in_even),
           ("w_out_even", ["out_e"], w_out_even, m_w_out_even, v_w_out_even),
           ("w_in_odd", ["in_o"], w_in_odd, m_w_in_odd, v_w_in_odd),
           ("w_out_odd", ["out_o"], w_out_odd, m_w_out_odd, v_w_out_odd),
           ("w_ff1", ["ff1_0", "ff1_1"], w_ff1, m_w_ff1, v_w_ff1), ("w_ff2", ["ff2_0", "ff2_1"], w_ff2, m_w_ff2, v_w_ff2)]
    big_out = [_adamw_sharded([recv[k] for k in keys], w, m, v, "adamw_" + nm) for nm, keys, w, m, v in big]

    packed = jnp.concatenate([dg_mix_pre, dg_mix_post, dg_ffn_pre, dg_ffn_post,
                              jnp.concatenate([d_lb, d_a_norm], axis=1), jnp.concatenate([d_ln_g, d_ln_b], axis=1),
                              loss_part], axis=0)
    gathered = _exchange([packed, d_ws, d_bias_t], True, "gather_small")
    rows8 = lambda k: slice(SUBLANES * k, SUBLANES * (k + 1))
    left, right, every = slice(0, A_WIDTH), slice(A_WIDTH, 2 * A_WIDTH), slice(None)
    where = [(0, slice(0, 16), every), (0, slice(16, 32), every), (0, slice(32, 48), every), (0, slice(48, 64), every),
             (0, rows8(8), left), (0, rows8(8), right), (0, rows8(9), left), (0, rows8(9), right),
             (1, None, None), (2, None, None), (0, rows8(10), every)]
    small_w = [norm_mix_pre, norm_mix_post, norm_ffn_pre, norm_ffn_post, lb_table, a_norm, b_ln_g, b_ln_b,
               b_ws[0], bias_t]
    small_m = [m_norm_mix_pre, m_norm_mix_post, m_norm_ffn_pre, m_norm_ffn_post, m_lb_table, m_a_norm, m_b_ln_g,
               m_b_ln_b, m_b_ws[0], m_b_bias[0].T]
    small_v = [v_norm_mix_pre, v_norm_mix_post, v_norm_ffn_pre, v_norm_ffn_post, v_lb_table, v_a_norm, v_b_ln_g,
               v_b_ln_b, v_b_ws[0], v_b_bias[0].T]
    small_out = _small_update(gathered, where, small_w, small_m, small_v, 4, "small_update")
    loss = small_out[0].reshape(())
    small = [small_out[1 + 4 * k:5 + 4 * k] for k in range(len(small_w))]
    small[8] = [a[None] for a in small[8]]
    small[9] = [a.T[None] for a in small[9]]

    per_weight = small[0:4] + [big_out[0]] + small[4:10] + big_out[1:6]
    grad_x = dx0.reshape(x.shape)
    out = [loss, grad_x]
    for kind in range(4):
        out += [p[kind] for p in per_weight]
    return tuple(out)


def _local_step(x0, target, rope, norm_mix_pre, norm_mix_post, norm_ffn_pre, norm_ffn_post, lb_table, a_norm,
                b_ln_g, b_ln_b, ws, bias_t, get_w, put_g, token):
    def gain(a, l, tok):
        return a[l:l + 1] if tok is None else a[l:l + 1] + tok[0:1, 0:1]

    full = lambda a: a.reshape(-1, D_MODEL)
    owners = lambda a: a.reshape((N_DEV, -1) + a.shape[1:])

    (g_in_e,) = get_w(["in_e"], token)
    proj, h_mix0 = _norm_inproj(x0, gain(norm_mix_pre, 0, token), g_in_e, "inproj_even")
    oa, pre_a, states = _hgrn2_fwd(proj, lb_table, a_norm, "hgrn2_fwd")
    ob = _gmlp_fwd(proj, b_ln_g, b_ln_b, ws, bias_t, "gmlp_fwd")
    w_out_e = full(get_w(["out_e"], ob)[0])
    x1, mix0 = _outproj([oa, ob], w_out_e, x0, gain(norm_mix_post, 0, None), "outproj_even")
    w1_0, w2_0 = get_w(["ff1_0", "ff2_0"], x1)
    w2_0 = full(w2_0)
    x2, y0, h_ffn0, r0 = _ffn_fwd(x1, gain(norm_ffn_pre, 0, None), w1_0, w2_0, gain(norm_ffn_post, 0, None), "ffn_fwd_0")
    (g_in_o,) = get_w(["in_o"], x2)
    *qkv, h_mix1 = _norm_inproj_rope(x2, gain(norm_mix_pre, 1, None), g_in_o, rope, "inproj_odd")
    branches = [_attn_branch_fwd(a, "attn_fwd_d%d" % d) for a, d in zip(qkv, C_DILATIONS)]
    attn, attn_b, lse = _attn_merge(branches, "attn_merge")
    w_out_o = full(get_w(["out_o"], attn_b)[0])
    x3, mix1 = _outproj([attn_b], w_out_o, x2, gain(norm_mix_post, 1, None), "outproj_odd")
    w1_1, w2_1 = get_w(["ff1_1", "ff2_1"], x3)
    w2_1 = full(w2_1)
    x4, y1, h_ffn1, r1 = _ffn_fwd(x3, gain(norm_ffn_pre, 1, None), w1_1, w2_1, gain(norm_ffn_post, 1, None), "ffn_fwd_1")

    dx4, loss_part = _loss_grad(x4, target, "loss_grad")

    dx3, dy1, da1, dg_ffn_pre1, dg_ffn_post1 = _ffn_bwd(
        dx4, x3, y1, r1, gain(norm_ffn_pre, 1, None), w1_1, w2_1, gain(norm_ffn_post, 1, None), "ffn_bwd_1")
    gw_ff1_1 = _grad_w(h_ffn1, da1, True, "grad_w_ff1_1")
    gw_ff2_1 = _grad_w(r1, dy1, False, "grad_w_ff2_1")
    tok = put_g("ffn1", dict(ff1_1=gw_ff1_1, ff2_1=owners(gw_ff2_1)))
    *dattn, delta, dz1, dg_mix_post1 = _outproj_bwd_attn(dx3, mix1, gain(norm_mix_post, 1, tok), w_out_o, attn,
                                                  "outproj_bwd_odd")
    gw_out_o = _grad_w(attn_b, dz1, False, "grad_w_out_odd")
    per_seq = lambda a: a.reshape(-1, SEQ, LANES)
    grads_c = [_attn_branch_bwd(qkv[b], dattn[b], per_seq(lse), per_seq(delta), "attn_bwd_d%d" % d)
               for b, d in enumerate(C_DILATIONS)]
    dqkv = _attn_combine_bwd(grads_c, rope, "attn_combine_bwd")
    gw_in_o = _grad_w(h_mix1, dqkv, True, "grad_w_in_odd")
    tok = put_g("mix1", dict(out_o=owners(gw_out_o), in_o=gw_in_o))
    dx2, dg_mix_pre1 = _inproj_bwd(dqkv, g_in_o, dx3, x2, gain(norm_mix_pre, 1, tok), "inproj_bwd_odd")

    dx1, dy0, da0, dg_ffn_pre0, dg_ffn_post0 = _ffn_bwd(
        dx2, x1, y0, r0, gain(norm_ffn_pre, 0, None), w1_0, w2_0, gain(norm_ffn_post, 0, None), "ffn_bwd_0")
    gw_ff1_0 = _grad_w(h_ffn0, da0, True, "grad_w_ff1_0")
    gw_ff2_0 = _grad_w(r0, dy0, False, "grad_w_ff2_0")
    tok = put_g("ffn0", dict(ff1_0=gw_ff1_0, ff2_0=owners(gw_ff2_0)))
    dcat, dz0, dg_mix_post0 = _outproj_bwd(dx1, mix0, gain(norm_mix_post, 0, tok), w_out_e, "outproj_bwd_even")
    gw_out_e = jnp.concatenate([_grad_w(oa, dz0, False, "grad_w_out_even_a"),
                                _grad_w(ob, dz0, False, "grad_w_out_even_b")], axis=0)
    dqfig, d_lb, d_a_norm = _hgrn2_bwd(proj, dcat, pre_a, states, lb_table, a_norm, "hgrn2_bwd")
    duv, d_ln_g, d_ln_b, d_ws, d_bias_t = _gmlp_bwd(proj, dcat, b_ln_g, b_ln_b, ws, bias_t, "gmlp_bwd")
    dproj = jnp.concatenate([dqfig, duv], axis=1)
    gw_in_e = _grad_w(h_mix0, dproj, True, "grad_w_in_even")
    tok = put_g("mix0", dict(out_e=owners(gw_out_e), in_e=gw_in_e))
    dx0, dg_mix_pre0 = _inproj_bwd(dproj, g_in_e, dx1, x0, gain(norm_mix_pre, 0, tok), "inproj_bwd_even")

    layers = lambda a, b: jnp.concatenate([a, b], axis=0)
    return (dx0, loss_part, layers(dg_mix_pre0, dg_mix_pre1), layers(dg_mix_post0, dg_mix_post1),
            layers(dg_ffn_pre0, dg_ffn_pre1), layers(dg_ffn_post0, dg_ffn_post1),
            d_lb, d_a_norm, d_ln_g, d_ln_b, d_ws, d_bias_t)
```

```python
import functools
import math

import jax
import jax.numpy as jnp
from jax import lax
from jax.experimental import pallas as pl
from jax.experimental.pallas import tpu as pltpu

F32 = jnp.float32
BF16 = jnp.bfloat16
MESH = pl.DeviceIdType.MESH

N_DEV = 8
D_MODEL = 1024
SEQ = 2048
EPS = 1e-6
A_WIDTH = 512
A_HEADS = 4
HEAD_A = 128
B_WIDTH = 512
B_GROUPS = 4
B_CHUNK = 128
C_HEADS = 16
C_HEAD_DIM = 64
C_ROT_HALF = 8
ROPE_THETA = 500000.0
C_DILATIONS = (1, 4, 16)
C_BLOCK = 128
D_FF = 4096
EVEN_IN = 3072
ODD_IN = 3072

ADAM_LR = 0.001
ADAM_B1 = 0.9
ADAM_B2 = 0.999
ADAM_EPS = 1e-08
ADAM_WD = 0.01
ADAM_STEP = 10

LANES = 128
SUBLANES = 8
ROW_TILE = 512
PROJ_TILE = 1024
PROJ_COLS = 768
MERGE_TILE = 256
SUB_CHUNK = 16
HGRN_BLOCK = 256
NEG = -1e30
VMEM_LIMIT = 56 * 1024 * 1024


def _params(sem):
    return pltpu.CompilerParams(dimension_semantics=sem, vmem_limit_bytes=VMEM_LIMIT)


def _dot(a, b):
    return jnp.dot(a, b, preferred_element_type=F32)


def _dot_nt(a, b):
    return lax.dot_general(a, b, (((1,), (1,)), ((), ())), preferred_element_type=F32)


def _dot_tn(a, b):
    return lax.dot_general(a, b, (((0,), (0,)), ((), ())), preferred_element_type=F32)


def _rms(x, g):
    r = lax.rsqrt(jnp.mean(x * x, axis=-1, keepdims=True) + EPS)
    return x * r * g


def _rms_bwd(x, g, dy):
    r = lax.rsqrt(jnp.mean(x * x, axis=-1, keepdims=True) + EPS)
    dyg = dy * g
    dx = r * dyg - x * (r * r * r) * jnp.mean(x * dyg, axis=-1, keepdims=True)
    return dx, dy * x * r


def _rows8(v):
    return v.reshape(v.shape[0] // SUBLANES, SUBLANES, v.shape[1]).sum(axis=0)


def _sigmoid(x):
    return 1.0 / (1.0 + jnp.exp(-x))


def _gelu(x):
    return 0.5 * x * (1.0 + jnp.tanh(math.sqrt(2.0 / math.pi) * (x + 0.044715 * (x * x * x))))


def _acc_rows8(ref, val, first):
    @pl.when(first)
    def _():
        ref[...] = val

    @pl.when(jnp.logical_not(first))
    def _():
        ref[...] += val


def _my_slot():
    return 4 * lax.axis_index("x") + 2 * lax.axis_index("y") + lax.axis_index("c")


def _peer(r):
    x, y, c = lax.axis_index("x"), lax.axis_index("y"), lax.axis_index("c")
    px = 1 - x if (r >> 2) & 1 else x
    py = 1 - y if (r >> 1) & 1 else y
    pc = 1 - c if r & 1 else c
    return (px, py, pc), 4 * px + 2 * py + pc


def _exchange(arrays, gather, name):
    n = len(arrays)
    if gather:
        out_shape = [jax.ShapeDtypeStruct((N_DEV,) + a.shape, a.dtype) for a in arrays]
    else:
        out_shape = [jax.ShapeDtypeStruct(a.shape, a.dtype) for a in arrays]

    def body(*refs):
        ins, outs = refs[:n], refs[n:2 * n]
        send_sems, recv_sems, local_sems = refs[2 * n:]
        me = _my_slot()
        local, remote = [], []
        for k in range(n):
            src = ins[k] if gather else ins[k].at[me]
            local.append(pltpu.make_async_copy(src, outs[k].at[me], local_sems.at[k]))
            for r in range(1, N_DEV):
                peer, slot = _peer(r)
                src = ins[k] if gather else ins[k].at[slot]
                remote.append((pltpu.make_async_remote_copy(
                    src_ref=src, dst_ref=outs[k].at[me], send_sem=send_sems.at[k, r - 1],
                    recv_sem=recv_sems.at[k, r - 1], device_id=peer, device_id_type=MESH), k, r, slot))
        for cp in local:
            cp.start()
        for cp, _, _, _ in remote:
            cp.start()
        for cp, k, r, slot in remote:
            pltpu.make_async_remote_copy(
                src_ref=outs[k].at[slot], dst_ref=outs[k].at[slot], send_sem=send_sems.at[k, r - 1],
                recv_sem=recv_sems.at[k, r - 1], device_id=_peer(r)[0], device_id_type=MESH).wait_recv()
        for cp, _, _, _ in remote:
            cp.wait_send()
        for cp in local:
            cp.wait()

    any_spec = pl.BlockSpec(memory_space=pl.ANY)
    return pl.pallas_call(
        body, name=name, out_shape=out_shape,
        in_specs=[any_spec] * n, out_specs=[any_spec] * n,
        scratch_shapes=[pltpu.SemaphoreType.DMA((n, N_DEV - 1)), pltpu.SemaphoreType.DMA((n, N_DEV - 1)),
                        pltpu.SemaphoreType.DMA((n,))],
        compiler_params=pltpu.CompilerParams(has_side_effects=True),
    )(*arrays)


HBM_SPEC = pl.BlockSpec(memory_space=pltpu.HBM)
SEM_SPEC = pl.BlockSpec(memory_space=pltpu.SEMAPHORE)
SPLIT_EFFECT = pltpu.SideEffectType.DATAFLOW_SIDE_EFFECTING


def _split_copies(land_ref, src_ref, send_sem, recv_sem):
    me = _my_slot()
    copies = []
    for r in range(1, N_DEV):
        peer, slot = _peer(r)
        src = _slot(land_ref, me) if src_ref is None else _slot(src_ref, slot)
        copies.append(pltpu.make_async_remote_copy(
            src_ref=src, dst_ref=_slot(land_ref, me), send_sem=send_sem, recv_sem=recv_sem,
            device_id=peer, device_id_type=MESH))
    return copies


def _slot(ref, s):
    if len(ref.shape) == 2:
        c = ref.shape[1] // N_DEV
        return ref.at[:, pl.ds(pl.multiple_of(s * c, LANES), c)]
    return ref.at[s]


def _exchange_start(lands, sources, name):
    n = len(lands)
    given = [s for s in sources if s is not None]
    arrays = list(lands) + given

    def body(*refs):
        land_refs, src_refs = refs[:n], list(refs[n:n + len(given)])
        sems = refs[len(arrays):len(arrays) + 2 * n]
        token = refs[-1]
        for k in range(n):
            src_ref = None if sources[k] is None else src_refs.pop(0)
            for copy in _split_copies(land_refs[k], src_ref, sems[k], sems[n + k]):
                copy.start()
        token[...] = jnp.zeros_like(token)

    outs = pl.pallas_call(
        body, name=name,
        out_shape=(pltpu.SemaphoreType.DMA(()),) * (2 * n) + tuple(pltpu.HBM(a.shape, a.dtype) for a in arrays)
        + (jax.ShapeDtypeStruct((SUBLANES, LANES), F32),),
        in_specs=[HBM_SPEC] * len(arrays),
        out_specs=(SEM_SPEC,) * (2 * n) + (HBM_SPEC,) * len(arrays) + (pl.BlockSpec(memory_space=pltpu.VMEM),),
        input_output_aliases={i: 2 * n + i for i in range(len(arrays))},
        compiler_params=pltpu.CompilerParams(has_side_effects=SPLIT_EFFECT),
    )(*[pltpu.with_memory_space_constraint(a, pltpu.HBM) for a in arrays])
    return list(outs[:n]), list(outs[n:2 * n]), list(outs[2 * n:3 * n]), list(outs[3 * n:-1]), outs[-1]


def _exchange_wait(lands, sources, send_sems, recv_sems, after, name):
    n = len(lands)
    given = [s for s in sources if s is not None]
    arrays = list(lands) + given

    def body(*refs):
        land_refs, src_refs = refs[:n], list(refs[n:n + len(given)])
        sems = refs[len(arrays):len(arrays) + 2 * n]
        for i in range(n):
            src_ref = None if sources[i] is None else src_refs.pop(0)
            copies = _split_copies(land_refs[i], src_ref, sems[i], sems[n + i])
            for copy in copies:
                copy.wait_recv()
            for copy in copies:
                copy.wait_send()

    outs = pl.pallas_call(
        body, name=name, out_shape=tuple(pltpu.HBM(a.shape, a.dtype) for a in arrays),
        in_specs=[HBM_SPEC] * len(arrays) + [SEM_SPEC] * (2 * n) + [pl.BlockSpec(memory_space=pl.ANY)],
        out_specs=(HBM_SPEC,) * len(arrays),
        input_output_aliases={i: i for i in range(len(arrays))},
        compiler_params=pltpu.CompilerParams(has_side_effects=SPLIT_EFFECT),
    )(*arrays, *send_sems, *recv_sems, after)
    return list(outs[:n])


def _place_own(a, me, name, own_block):
    shape = a.shape[1:] if own_block else a.shape
    cols = shape[-1]
    a3 = a.reshape((N_DEV if own_block else 1, -1, cols))
    rows = a3.shape[1]
    tr = min(rows, 512)

    def body(me_ref, a_ref, o_ref):
        o_ref[...] = a_ref[...].astype(BF16)

    grid_spec = pltpu.PrefetchScalarGridSpec(
        num_scalar_prefetch=1, grid=(rows // tr,),
        in_specs=[pl.BlockSpec((1, tr, cols), lambda i, me_ref: (me_ref[0] if own_block else 0, i, 0))],
        out_specs=pl.BlockSpec((1, tr, cols), lambda i, me_ref: (me_ref[0], i, 0)))
    out = pl.pallas_call(
        body, name=name, grid_spec=grid_spec, out_shape=jax.ShapeDtypeStruct((N_DEV, rows, cols), BF16),
        compiler_params=_params(("arbitrary",)),
    )(me, a3)
    return out.reshape((N_DEV,) + shape)


def _place_own_columns(a, me, name):
    rows, cols = a.shape
    tr = min(rows, 512)

    def body(me_ref, a_ref, o_ref):
        o_ref[...] = a_ref[...].astype(BF16)

    grid_spec = pltpu.PrefetchScalarGridSpec(
        num_scalar_prefetch=1, grid=(rows // tr,),
        in_specs=[pl.BlockSpec((tr, cols), lambda i, me_ref: (i, 0))],
        out_specs=pl.BlockSpec((tr, cols), lambda i, me_ref: (i, me_ref[0])))
    return pl.pallas_call(
        body, name=name, grid_spec=grid_spec, out_shape=jax.ShapeDtypeStruct((rows, N_DEV * cols), BF16),
        compiler_params=_params(("arbitrary",)),
    )(me, a)


def _rope_tables(positions):
    inv = ROPE_THETA ** (-jnp.arange(C_ROT_HALF, dtype=F32) / C_ROT_HALF)
    ang = positions.reshape(-1)[:, None].astype(F32) * inv
    cos, sin = jnp.cos(ang), jnp.sin(ang)
    t = ang.shape[0]
    ones = jnp.ones((t, C_HEAD_DIM - 2 * C_ROT_HALF), F32)
    c_head = jnp.concatenate([cos, cos, ones], axis=1)
    s_head = jnp.concatenate([-sin, sin, 0.0 * ones], axis=1)
    return jnp.concatenate([c_head, c_head], axis=1), jnp.concatenate([s_head, s_head], axis=1)


def _swap_halves(x):
    lane = lax.broadcasted_iota(jnp.int32, x.shape, 1) % C_HEAD_DIM
    return jnp.where(lane < C_ROT_HALF, pltpu.roll(x, LANES - C_ROT_HALF, 1), pltpu.roll(x, C_ROT_HALF, 1))


def _norm_inproj(x, g, w, name):
    t = x.shape[0]
    n = w.shape[1]
    tm, tn = PROJ_TILE, PROJ_COLS

    def body(x_ref, g_ref, w_ref, o_ref, h_ref):
        @pl.when(pl.program_id(1) == 0)
        def _():
            h_ref[...] = _rms(x_ref[...], g_ref[...]).astype(BF16)

        o_ref[...] = _dot(h_ref[...], w_ref[...])

    return pl.pallas_call(
        body, name=name, grid=(t // tm, n // tn),
        in_specs=[pl.BlockSpec((tm, D_MODEL), lambda i, j: (i, 0)), pl.BlockSpec((1, D_MODEL), lambda i, j: (0, 0)),
                  pl.BlockSpec((D_MODEL, tn), lambda i, j: (0, j))],
        out_specs=[pl.BlockSpec((tm, tn), lambda i, j: (i, j)), pl.BlockSpec((tm, D_MODEL), lambda i, j: (i, 0))],
        out_shape=[jax.ShapeDtypeStruct((t, n), F32), jax.ShapeDtypeStruct((t, D_MODEL), BF16)],
        compiler_params=_params(("parallel", "arbitrary")),
    )(x, g, w)


def _dilated_specs(tm, width, col_of):
    per_seq = SEQ // tm
    specs = []
    for d in C_DILATIONS:
        specs.append(pl.BlockSpec(
            (1, d, tm // d, width), lambda i, *rest: (i // per_seq, 0, i % per_seq, col_of(*rest))))
    return specs


def _dilated_shapes(n_seq, cols, dtype):
    return [jax.ShapeDtypeStruct((n_seq, d, SEQ // d, cols), dtype) for d in C_DILATIONS]


def _store_dilated(src_ref, out_refs, dtype):
    groups, tm, _ = src_ref.shape
    for d, o_ref in zip(C_DILATIONS, out_refs):
        for r in range(d):
            rows = pl.ds(r, tm // d, stride=d) if d > 1 else slice(None)
            for p in range(groups):
                o_ref[0, r, :, p * LANES:(p + 1) * LANES] = src_ref.at[p][rows, :].astype(dtype)


def _load_dilated(in_ref, d, dst_ref):
    groups, tm, _ = dst_ref.shape
    for r in range(d):
        rows = pl.ds(r, tm // d, stride=d)
        for p in range(groups):
            dst_ref.at[p][rows, :] = in_ref[0, r, :, p * LANES:(p + 1) * LANES].astype(F32)


def _norm_inproj_rope(x, g, w, rope, name):
    t = x.shape[0]
    n = w.shape[1]
    tm, nb = PROJ_TILE, PROJ_COLS

    def body(x_ref, g_ref, w_ref, c_ref, s_ref, o1_ref, o4_ref, o16_ref, h_ref, tile_ref):
        j = pl.program_id(1)

        @pl.when(j == 0)
        def _():
            h_ref[...] = _rms(x_ref[...], g_ref[...]).astype(BF16)

        acc = _dot(h_ref[...], w_ref[...])
        for p in range(nb // LANES):
            blk = acc[:, p * LANES:(p + 1) * LANES]
            roped = blk * c_ref[...] + _swap_halves(blk) * s_ref[...]
            is_qk = (j * (nb // LANES) + p) < 2 * (D_MODEL // LANES)
            tile_ref[p] = jnp.where(is_qk, roped, blk)
        _store_dilated(tile_ref, (o1_ref, o4_ref, o16_ref), BF16)

    return pl.pallas_call(
        body, name=name, grid=(t // tm, n // nb),
        in_specs=[pl.BlockSpec((tm, D_MODEL), lambda i, j: (i, 0)), pl.BlockSpec((1, D_MODEL), lambda i, j: (0, 0)),
                  pl.BlockSpec((D_MODEL, nb), lambda i, j: (0, j)),
                  pl.BlockSpec((tm, LANES), lambda i, j: (i, 0)), pl.BlockSpec((tm, LANES), lambda i, j: (i, 0))],
        out_specs=_dilated_specs(tm, nb, lambda j: j) + [pl.BlockSpec((tm, D_MODEL), lambda i, j: (i, 0))],
        out_shape=_dilated_shapes(t // SEQ, n, BF16) + [jax.ShapeDtypeStruct((t, D_MODEL), BF16)],
        scratch_shapes=[pltpu.VMEM((nb // LANES, tm, LANES), F32)],
        compiler_params=_params(("parallel", "arbitrary")),
    )(x, g, w, *rope)


def _outproj(parts, w, x, g, name):
    t = x.shape[0]
    tm = PROJ_TILE
    n = len(parts)
    widths = [p.shape[1] for p in parts]

    def body(*refs):
        p_refs = refs[:n]
        w_ref, x_ref, g_ref, xo_ref, mix_ref = refs[n:]
        mix = None
        off = 0
        for p_ref, wd in zip(p_refs, widths):
            term = _dot(p_ref[...].astype(BF16), w_ref[off:off + wd, :])
            mix = term if mix is None else mix + term
            off += wd
        mix_ref[...] = mix
        xo_ref[...] = x_ref[...] + _rms(mix, g_ref[...])

    row = lambda i: (i, 0)
    return pl.pallas_call(
        body, name=name, grid=(t // tm,),
        in_specs=[pl.BlockSpec((tm, wd), row) for wd in widths] + [
            pl.BlockSpec((sum(widths), D_MODEL), lambda i: (0, 0)),
            pl.BlockSpec((tm, D_MODEL), row), pl.BlockSpec((1, D_MODEL), lambda i: (0, 0))],
        out_specs=[pl.BlockSpec((tm, D_MODEL), row)] * 2,
        out_shape=[jax.ShapeDtypeStruct((t, D_MODEL), F32)] * 2,
        compiler_params=_params(("parallel",)),
    )(*parts, w, x, g)


def _outproj_bwd(dx, mix, g, w, name):
    t = dx.shape[0]
    tm = PROJ_TILE
    k = w.shape[0]

    def body(dx_ref, mix_ref, g_ref, w_ref, dcat_ref, dz_ref, dg_ref):
        dz, dgr = _rms_bwd(mix_ref[...], g_ref[...], dx_ref[...])
        dzb = dz.astype(BF16)
        dz_ref[...] = dzb
        dcat_ref[...] = _dot_nt(dzb, w_ref[...])
        _acc_rows8(dg_ref, _rows8(dgr), pl.program_id(0) == 0)

    row = lambda i: (i, 0)
    return pl.pallas_call(
        body, name=name, grid=(t // tm,),
        in_specs=[pl.BlockSpec((tm, D_MODEL), row), pl.BlockSpec((tm, D_MODEL), row),
                  pl.BlockSpec((1, D_MODEL), lambda i: (0, 0)), pl.BlockSpec((k, D_MODEL), lambda i: (0, 0))],
        out_specs=[pl.BlockSpec((tm, k), row), pl.BlockSpec((tm, D_MODEL), row),
                   pl.BlockSpec((SUBLANES, D_MODEL), lambda i: (0, 0))],
        out_shape=[jax.ShapeDtypeStruct((t, k), F32), jax.ShapeDtypeStruct((t, D_MODEL), BF16),
                   jax.ShapeDtypeStruct((SUBLANES, D_MODEL), F32)],
        compiler_params=_params(("arbitrary",)),
    )(dx, mix, g, w)


def _outproj_bwd_attn(dx, mix, g, w, out, name):
    t = dx.shape[0]
    tm = MERGE_TILE

    def body(dx_ref, mix_ref, g_ref, w_ref, out_ref, do1, do4, do16, dl_ref, dz_ref, dg_ref, tile_ref):
        dz, dgr = _rms_bwd(mix_ref[...], g_ref[...], dx_ref[...])
        dzb = dz.astype(BF16)
        dz_ref[...] = dzb
        _acc_rows8(dg_ref, _rows8(dgr), pl.program_id(0) == 0)
        dout = _dot_nt(dzb, w_ref[...])
        for p in range(LANE_GROUPS):
            tile_ref[p] = dout[:, p * LANES:(p + 1) * LANES]
        _store_dilated(tile_ref, (do1, do4, do16), BF16)
        column = lax.broadcasted_iota(jnp.int32, (D_MODEL, LANES), 0) // C_HEAD_DIM
        head = lax.broadcasted_iota(jnp.int32, (D_MODEL, LANES), 1)
        dl_ref[...] = jnp.dot(dout * out_ref[...], (column == head).astype(F32), precision=lax.Precision.HIGHEST,
                              preferred_element_type=F32)

    row = lambda i: (i, 0)
    n_seq = t // SEQ
    return pl.pallas_call(
        body, name=name, grid=(t // tm,),
        in_specs=[pl.BlockSpec((tm, D_MODEL), row), pl.BlockSpec((tm, D_MODEL), row),
                  pl.BlockSpec((1, D_MODEL), lambda i: (0, 0)), pl.BlockSpec((D_MODEL, D_MODEL), lambda i: (0, 0)),
                  pl.BlockSpec((tm, D_MODEL), row)],
        out_specs=_dilated_specs(tm, D_MODEL, lambda: 0) + [
            pl.BlockSpec((tm, LANES), row), pl.BlockSpec((tm, D_MODEL), row),
            pl.BlockSpec((SUBLANES, D_MODEL), lambda i: (0, 0))],
        out_shape=_dilated_shapes(n_seq, D_MODEL, BF16) + [
            jax.ShapeDtypeStruct((t, LANES), F32), jax.ShapeDtypeStruct((t, D_MODEL), BF16),
            jax.ShapeDtypeStruct((SUBLANES, D_MODEL), F32)],
        scratch_shapes=[pltpu.VMEM((LANE_GROUPS, tm, LANES), F32)],
        compiler_params=_params(("arbitrary",)),
    )(dx, mix, g, w, out)


def _inproj_bwd(dproj, w, dx, x, g, name):
    t = x.shape[0]
    n = w.shape[1]
    tm = ROW_TILE

    def body(dp_ref, w_ref, dx_ref, x_ref, g_ref, o_ref, dg_ref):
        dxn, dgr = _rms_bwd(x_ref[...], g_ref[...], _dot_nt(dp_ref[...], w_ref[...]))
        o_ref[...] = dx_ref[...] + dxn
        _acc_rows8(dg_ref, _rows8(dgr), pl.program_id(0) == 0)

    row = lambda i: (i, 0)
    return pl.pallas_call(
        body, name=name, grid=(t // tm,),
        in_specs=[pl.BlockSpec((tm, n), row), pl.BlockSpec((D_MODEL, n), lambda i: (0, 0)),
                  pl.BlockSpec((tm, D_MODEL), row), pl.BlockSpec((tm, D_MODEL), row),
                  pl.BlockSpec((1, D_MODEL), lambda i: (0, 0))],
        out_specs=[pl.BlockSpec((tm, D_MODEL), row), pl.BlockSpec((SUBLANES, D_MODEL), lambda i: (0, 0))],
        out_shape=[jax.ShapeDtypeStruct((t, D_MODEL), F32), jax.ShapeDtypeStruct((SUBLANES, D_MODEL), F32)],
        compiler_params=_params(("arbitrary",)),
    )(dproj, w, dx, x, g)


def _grad_w(a, b, col_blocks, name):
    t, k = a.shape
    n = b.shape[1]
    tk = min(k, 1024)
    per_owner = n // N_DEV
    tn = 2 * per_owner if col_blocks else min(n, 1024)

    def body(a_ref, b_ref, o_ref, at_ref):
        @pl.when(pl.program_id(1) == 0)
        def _():
            for c in range(t // ROW_TILE):
                rows = slice(c * ROW_TILE, (c + 1) * ROW_TILE)
                at_ref[:, rows] = a_ref[rows, :].T

        res = _dot(at_ref[...], b_ref[...]).astype(BF16)
        if col_blocks:
            o_ref[0] = res[:, :per_owner]
            o_ref[1] = res[:, per_owner:]
        else:
            o_ref[...] = res

    if col_blocks:
        out_spec = pl.BlockSpec((2, tk, per_owner), lambda i, j: (j, i, 0))
        out_shape = jax.ShapeDtypeStruct((N_DEV, k, per_owner), BF16)
    else:
        out_spec = pl.BlockSpec((tk, tn), lambda i, j: (i, j))
        out_shape = jax.ShapeDtypeStruct((k, n), BF16)
    return pl.pallas_call(
        body, name=name, grid=(k // tk, n // tn),
        in_specs=[pl.BlockSpec((t, tk), lambda i, j: (0, i)), pl.BlockSpec((t, tn), lambda i, j: (0, j))],
        out_specs=out_spec, out_shape=out_shape,
        scratch_shapes=[pltpu.VMEM((tk, t), BF16)],
        compiler_params=_params(("parallel", "arbitrary")),
    )(a, b)


FF_BLOCK = D_FF // N_DEV
FF_STEP = 1024
FF_STEPS = D_FF // FF_STEP


def _ffn_fwd(x, g_pre, w1, w2, g_post, name):
    t = x.shape[0]
    tm = PROJ_TILE

    def body(x_ref, gp_ref, w1_ref, w2_ref, gq_ref, xo_ref, y_ref, h_ref, r_ref):
        j = pl.program_id(1)

        @pl.when(j == 0)
        def _():
            h_ref[...] = _rms(x_ref[...], gp_ref[...]).astype(BF16)

        a = _dot(h_ref[...], w1_ref[...])
        r = jnp.square(jnp.maximum(a, 0.0)).astype(BF16)
        r_ref[...] = r
        term = _dot(r, w2_ref[...])

        @pl.when(j == 0)
        def _():
            y_ref[...] = term

        @pl.when(j > 0)
        def _():
            y_ref[...] += term

        @pl.when(j == FF_STEPS - 1)
        def _():
            xo_ref[...] = x_ref[...] + _rms(y_ref[...], gq_ref[...])

    row = lambda i, j: (i, 0)
    vec = pl.BlockSpec((1, D_MODEL), lambda i, j: (0, 0))
    return pl.pallas_call(
        body, name=name, grid=(t // tm, FF_STEPS),
        in_specs=[pl.BlockSpec((tm, D_MODEL), row), vec,
                  pl.BlockSpec((D_MODEL, FF_STEP), lambda i, j: (0, j)),
                  pl.BlockSpec((FF_STEP, D_MODEL), lambda i, j: (j, 0)), vec],
        out_specs=[pl.BlockSpec((tm, D_MODEL), row)] * 3 + [pl.BlockSpec((tm, FF_STEP), lambda i, j: (i, j))],
        out_shape=[jax.ShapeDtypeStruct((t, D_MODEL), F32), jax.ShapeDtypeStruct((t, D_MODEL), F32),
                   jax.ShapeDtypeStruct((t, D_MODEL), BF16), jax.ShapeDtypeStruct((t, D_FF), BF16)],
        compiler_params=_params(("parallel", "arbitrary")),
    )(x, g_pre, w1, w2, g_post)


def _ffn_bwd(dxo, x, y, r, g_pre, w1, w2, g_post, name):
    t = x.shape[0]
    tm = ROW_TILE

    def body(dxo_ref, x_ref, y_ref, r_ref, gp_ref, w1_ref, w2_ref, gq_ref,
             dx_ref, dy_ref, da_ref, dgp_ref, dgq_ref, acc_ref):
        i, j = pl.program_id(0), pl.program_id(1)

        @pl.when(j == 0)
        def _():
            dy, dgr = _rms_bwd(y_ref[...], gq_ref[...], dxo_ref[...])
            dy_ref[...] = dy.astype(BF16)
            _acc_rows8(dgq_ref, _rows8(dgr), i == 0)

        dr = _dot_nt(dy_ref[...], w2_ref[...])
        da = (dr * (2.0 * jnp.sqrt(r_ref[...].astype(F32)))).astype(BF16)
        da_ref[...] = da
        term = _dot_nt(da, w1_ref[...])

        @pl.when(j == 0)
        def _():
            acc_ref[...] = term

        @pl.when(j > 0)
        def _():
            acc_ref[...] += term

        @pl.when(j == FF_STEPS - 1)
        def _():
            dxn, dgr = _rms_bwd(x_ref[...], gp_ref[...], acc_ref[...])
            dx_ref[...] = dxo_ref[...] + dxn
            _acc_rows8(dgp_ref, _rows8(dgr), i == 0)

    row = lambda i, j: (i, 0)
    vec = pl.BlockSpec((1, D_MODEL), lambda i, j: (0, 0))
    acc8 = pl.BlockSpec((SUBLANES, D_MODEL), lambda i, j: (0, 0))
    return pl.pallas_call(
        body, name=name, grid=(t // tm, FF_STEPS),
        in_specs=[pl.BlockSpec((tm, D_MODEL), row)] * 3 + [
            pl.BlockSpec((tm, FF_STEP), lambda i, j: (i, j)),
            vec, pl.BlockSpec((D_MODEL, FF_STEP), lambda i, j: (0, j)),
            pl.BlockSpec((FF_STEP, D_MODEL), lambda i, j: (j, 0)), vec],
        out_specs=[pl.BlockSpec((tm, D_MODEL), row), pl.BlockSpec((tm, D_MODEL), row),
                   pl.BlockSpec((tm, FF_STEP), lambda i, j: (i, j)), acc8, acc8],
        out_shape=[jax.ShapeDtypeStruct((t, D_MODEL), F32), jax.ShapeDtypeStruct((t, D_MODEL), BF16),
                   jax.ShapeDtypeStruct((t, D_FF), BF16),
                   jax.ShapeDtypeStruct((SUBLANES, D_MODEL), F32), jax.ShapeDtypeStruct((SUBLANES, D_MODEL), F32)],
        scratch_shapes=[pltpu.VMEM((tm, D_MODEL), F32)],
        compiler_params=_params(("arbitrary", "arbitrary")),
    )(dxo, x, y, r, g_pre, w1, w2, g_post)


def _lower_bound(table):
    e = jnp.exp(table - jnp.max(table, axis=0, keepdims=True))
    return e[0:1, :] / jnp.sum(e, axis=0, keepdims=True)


def _hgrn2_block(q_ref, f_ref, lb):
    tb = f_ref.shape[0]
    sig = _sigmoid(f_ref[...])
    f = lb + (1.0 - lb) * sig
    qraw = q_ref[...]
    sq = _sigmoid(qraw)
    r = lax.broadcasted_iota(jnp.int32, (tb, tb), 0)
    c = lax.broadcasted_iota(jnp.int32, (tb, tb), 1)
    same = (r // SUB_CHUNK) == (c // SUB_CHUNK)
    logf = jnp.log(f)
    gsum = jnp.dot((same & (c <= r)).astype(F32), logf, precision=lax.Precision.HIGHEST, preferred_element_type=F32)
    glast = jnp.dot(same.astype(F32), logf, precision=lax.Precision.HIGHEST, preferred_element_type=F32)
    return dict(sig=sig, f=f, kk=1.0 - f, qraw=qraw, sq=sq, qs=qraw * sq, gsum=gsum,
                eg=jnp.exp(gsum), ekd=jnp.exp(glast - gsum), a=jnp.exp(glast))


def _head_sums(x):
    parts = [jnp.broadcast_to(jnp.sum(x[:, h * HEAD_A:(h + 1) * HEAD_A], axis=1, keepdims=True), (x.shape[0], HEAD_A))
             for h in range(A_HEADS)]
    return jnp.concatenate(parts, axis=1)


def _lag_decay(g, j, row):
    back = jnp.exp(jnp.where(row >= j, g - pltpu.roll(g, j, 0), NEG))
    ahead = jnp.exp(jnp.where(row < SUB_CHUNK - j, pltpu.roll(g, SUB_CHUNK - j, 0) - g, NEG))
    return back, ahead


def _hgrn2_intra(g, kk, qs, v):
    row = lax.broadcasted_iota(jnp.int32, g.shape, 0)
    o = _head_sums(qs * kk) * v
    for j in range(1, SUB_CHUNK):
        decay = jnp.exp(jnp.where(row >= j, g - pltpu.roll(g, j, 0), NEG))
        o = o + _head_sums(qs * pltpu.roll(kk, j, 0) * decay) * pltpu.roll(v, j, 0)
    return o


def _hgrn2_intra_bwd(g, kk, qs, v, do):
    row = lax.broadcasted_iota(jnp.int32, g.shape, 0)
    dsc = _head_sums(do * v)
    dqs, dkk, dv = dsc * kk, dsc * qs, _head_sums(qs * kk) * do
    for j in range(1, SUB_CHUNK):
        back, ahead = _lag_decay(g, j, row)
        dqs = dqs + _head_sums(do * pltpu.roll(v, j, 0)) * pltpu.roll(kk, j, 0) * back
        q_up, do_up = pltpu.roll(qs, SUB_CHUNK - j, 0), pltpu.roll(do, SUB_CHUNK - j, 0)
        dkk = dkk + _head_sums(do_up * v) * q_up * ahead
        dv = dv + _head_sums(q_up * kk * ahead) * do_up
    return dqs, dkk, dv


def _hgrn2_fwd(proj, lb_table, a_norm, name):
    t = proj.shape[0]
    tb = HGRN_BLOCK
    n_tb = SEQ // tb
    n_seq = t // SEQ
    n_sub = tb // SUB_CHUNK

    def body(q_ref, f_ref, i_ref, g_ref, lbt_ref, an_ref, o_ref, pre_ref, sts_ref, st_ref,
             gs_ref, kk_ref, qs_ref, eg_ref, ekd_ref, a_ref):
        @pl.when(pl.program_id(1) == 0)
        def _():
            st_ref[...] = jnp.zeros_like(st_ref)

        an = an_ref[...]
        blk = _hgrn2_block(q_ref, f_ref, _lower_bound(lbt_ref[...]))
        for ref, key in ((gs_ref, "gsum"), (kk_ref, "kk"), (qs_ref, "qs"), (eg_ref, "eg"), (ekd_ref, "ekd"), (a_ref, "a")):
            ref[...] = blk[key]

        def step(c, carry):
            rows = pl.ds(pl.multiple_of(c * SUB_CHUNK, SUB_CHUNK), SUB_CHUNK)
            kk, qs, v = kk_ref[rows, :], qs_ref[rows, :], i_ref[rows, :]
            o = _hgrn2_intra(gs_ref[rows, :], kk, qs, v)
            qg, kd, vb = (qs * eg_ref[rows, :]).astype(BF16), (kk * ekd_ref[rows, :]).astype(BF16), v.astype(BF16)
            for h in range(A_HEADS):
                lanes = slice(h * HEAD_A, (h + 1) * HEAD_A)
                st = st_ref[h]
                sts_ref[0, c, h] = st
                o_h = o[:, lanes] + _dot_nt(qg[:, lanes], st.astype(BF16))
                st_ref[h] = st * a_ref[rows, lanes][0:1] + _dot_tn(vb[:, lanes], kd[:, lanes])
                pre_ref[rows, lanes] = o_h
                graw = g_ref[rows, lanes]
                o_ref[rows, lanes] = (_rms(o_h, an[:, lanes]) * (graw * _sigmoid(graw))).astype(BF16)
            return carry

        lax.fori_loop(0, n_sub, step, 0, unroll=2)

    def col(k):
        return pl.BlockSpec((tb, A_WIDTH), lambda b, s, k=k: (b * n_tb + s, k))

    out_rows = pl.BlockSpec((tb, A_WIDTH), lambda b, s: (b * n_tb + s, 0))
    return pl.pallas_call(
        body, name=name, grid=(n_seq, n_tb),
        in_specs=[col(0), col(1), col(2), col(3),
                  pl.BlockSpec((3, A_WIDTH), lambda b, s: (0, 0)), pl.BlockSpec((1, A_WIDTH), lambda b, s: (0, 0))],
        out_specs=[out_rows, out_rows,
                   pl.BlockSpec((1, n_sub, A_HEADS, HEAD_A, HEAD_A), lambda b, s: (b * n_tb + s, 0, 0, 0, 0))],
        out_shape=[jax.ShapeDtypeStruct((t, A_WIDTH), BF16), jax.ShapeDtypeStruct((t, A_WIDTH), F32),
                   jax.ShapeDtypeStruct((n_seq * n_tb, n_sub, A_HEADS, HEAD_A, HEAD_A), F32)],
        scratch_shapes=[pltpu.VMEM((A_HEADS, HEAD_A, HEAD_A), F32)] + [pltpu.VMEM((tb, A_WIDTH), F32)] * 6,
        compiler_params=_params(("parallel", "arbitrary")),
    )(proj, proj, proj, proj, lb_table, a_norm)


def _hgrn2_bwd(proj, dcat, pre, states, lb_table, a_norm, name):
    t = proj.shape[0]
    tb = HGRN_BLOCK
    n_tb = SEQ // tb
    n_seq = t // SEQ
    n_sub = tb // SUB_CHUNK

    def body(q_ref, f_ref, i_ref, g_ref, do_ref, pre_ref, sts_ref, lbt_ref, an_ref, dp_ref, dlb_ref, dan_ref, dst_ref,
             gs_ref, kk_ref, qs_ref, eg_ref, ekd_ref, a_ref, dpre_ref, dlf_ref, dqs_ref, dkk_ref):
        b, s = pl.program_id(0), pl.program_id(1)

        @pl.when(s == 0)
        def _():
            dst_ref[...] = jnp.zeros_like(dst_ref)

        @pl.when((b == 0) & (s == 0))
        def _():
            dlb_ref[...] = jnp.zeros_like(dlb_ref)
            dan_ref[...] = jnp.zeros_like(dan_ref)

        lb = _lower_bound(lbt_ref[...])
        an = an_ref[...]
        heads = [slice(h * HEAD_A, (h + 1) * HEAD_A) for h in range(A_HEADS)]
        blk = _hgrn2_block(q_ref, f_ref, lb)
        for ref, key in ((gs_ref, "gsum"), (kk_ref, "kk"), (qs_ref, "qs"), (eg_ref, "eg"), (ekd_ref, "ekd"), (a_ref, "a")):
            ref[...] = blk[key]
        for h, lanes in enumerate(heads):
            graw, o = g_ref[:, lanes], pre_ref[:, lanes]
            sg = _sigmoid(graw)
            dout = do_ref[:, lanes]
            d_o, dgr = _rms_bwd(o, an[:, lanes], dout * (graw * sg))
            dan_ref[0:1, lanes] += jnp.sum(dgr, axis=0, keepdims=True)
            dp_ref[:, 3 * A_WIDTH + h * HEAD_A:3 * A_WIDTH + (h + 1) * HEAD_A] = (
                dout * _rms(o, an[:, lanes]) * (sg * (1.0 + graw * (1.0 - sg)))).astype(BF16)
            dpre_ref[:, lanes] = d_o

        tri_t = (lax.broadcasted_iota(jnp.int32, (SUB_CHUNK, SUB_CHUNK), 0)
                 <= lax.broadcasted_iota(jnp.int32, (SUB_CHUNK, SUB_CHUNK), 1)).astype(F32)

        def back(k, carry):
            c = n_sub - 1 - k
            rows = pl.ds(pl.multiple_of(c * SUB_CHUNK, SUB_CHUNK), SUB_CHUNK)
            g, kk, qs, v, d_o = gs_ref[rows, :], kk_ref[rows, :], qs_ref[rows, :], i_ref[rows, :], dpre_ref[rows, :]
            eg, ekd, a = eg_ref[rows, :], ekd_ref[rows, :], a_ref[rows, :]
            dqs, dkk, dv = _hgrn2_intra_bwd(g, kk, qs, v, d_o)
            qg_f, kd_f = qs * eg, kk * ekd
            qg, kd, vb, dob = qg_f.astype(BF16), kd_f.astype(BF16), v.astype(BF16), d_o.astype(BF16)
            dqg, dkd, da, dv_st = [], [], [], []
            for h, lanes in enumerate(heads):
                st, dst = sts_ref[0, c, h], dst_ref[h]
                dstb = dst.astype(BF16)
                dqg.append(_dot(dob[:, lanes], st.astype(BF16)))
                dv_st.append(_dot_nt(kd[:, lanes], dstb))
                dkd.append(_dot(vb[:, lanes], dstb))
                da.append(jnp.broadcast_to(jnp.sum(dst * st, axis=0, keepdims=True), (SUB_CHUNK, HEAD_A)))
                dst_ref[h] = dst * a[0:1, lanes] + _dot_tn(dob[:, lanes], qg[:, lanes])
            dqg, dkd, da, dv_st = [jnp.concatenate(p, axis=1) for p in (dqg, dkd, da, dv_st)]
            d_gsum = qs * dqs - kk * dkk + dqg * qg_f - dkd * kd_f
            d_glast = jnp.sum(dkd * kd_f, axis=0, keepdims=True) + da * a
            dlf_ref[rows, :] = jnp.dot(tri_t, d_gsum, precision=lax.Precision.HIGHEST,
                                       preferred_element_type=F32) + d_glast
            dqs_ref[rows, :] = dqs + dqg * eg
            dkk_ref[rows, :] = dkk + dkd * ekd
            dp_ref[rows, 2 * A_WIDTH:3 * A_WIDTH] = (dv + dv_st).astype(BF16)
            return carry

        lax.fori_loop(0, n_sub, back, 0, unroll=2)
        sig, sq, qraw = blk["sig"], blk["sq"], blk["qraw"]
        d_f = dlf_ref[...] / blk["f"] - dkk_ref[...]
        dlb_ref[0:1, :] += jnp.sum(d_f * (1.0 - sig), axis=0, keepdims=True)
        dp_ref[:, 0:A_WIDTH] = (dqs_ref[...] * (sq * (1.0 + qraw * (1.0 - sq)))).astype(BF16)
        dp_ref[:, A_WIDTH:2 * A_WIDTH] = (d_f * (1.0 - lb) * sig * (1.0 - sig)).astype(BF16)

    def rev(s):
        return n_tb - 1 - s

    def col(k):
        return pl.BlockSpec((tb, A_WIDTH), lambda b, s, k=k: (b * n_tb + rev(s), k))

    acc8 = pl.BlockSpec((SUBLANES, A_WIDTH), lambda b, s: (0, 0))
    return pl.pallas_call(
        body, name=name, grid=(n_seq, n_tb),
        in_specs=[col(0), col(1), col(2), col(3), col(0), col(0),
                  pl.BlockSpec((1, n_sub, A_HEADS, HEAD_A, HEAD_A), lambda b, s: (b * n_tb + rev(s), 0, 0, 0, 0)),
                  pl.BlockSpec((3, A_WIDTH), lambda b, s: (0, 0)), pl.BlockSpec((1, A_WIDTH), lambda b, s: (0, 0))],
        out_specs=[pl.BlockSpec((tb, 4 * A_WIDTH), lambda b, s: (b * n_tb + rev(s), 0)), acc8, acc8],
        out_shape=[jax.ShapeDtypeStruct((t, 4 * A_WIDTH), BF16)] + [jax.ShapeDtypeStruct((SUBLANES, A_WIDTH), F32)] * 2,
        scratch_shapes=[pltpu.VMEM((A_HEADS, HEAD_A, HEAD_A), F32)] + [pltpu.VMEM((tb, A_WIDTH), F32)] * 10,
        compiler_params=_params(("arbitrary", "arbitrary")),
    )(proj, proj, proj, proj, dcat, pre, states, lb_table, a_norm)


GMLP_ROWS = 512


def _gmlp_chunk(ub, vb, ln_g, ln_b, ws, bias):
    u = [_gelu(a) for a in ub]
    v = [_gelu(a) for a in vb]
    mu = sum(jnp.sum(a, axis=-1, keepdims=True) for a in v) * (1.0 / B_WIDTH)
    cen = [a - mu for a in v]
    var = sum(jnp.sum(a * a, axis=-1, keepdims=True) for a in cen) * (1.0 / B_WIDTH)
    inv = lax.rsqrt(var + EPS)
    r = lax.broadcasted_iota(jnp.int32, (B_CHUNK, B_CHUNK), 0)
    c = lax.broadcasted_iota(jnp.int32, (B_CHUNK, B_CHUNK), 1)
    outs = []
    for g in range(B_GROUPS):
        vn = (cen[g] * inv * ln_g[g] + ln_b[g]).astype(BF16)
        wm = jnp.where(c <= r, ws[g], 0.0).astype(BF16)
        outs.append(u[g] * (_dot(wm, vn) + bias[g]))
    return outs


def _lane_groups(ref, rows=slice(None)):
    return [ref[rows, g * LANES:(g + 1) * LANES] for g in range(B_GROUPS)]


def _gmlp_fwd(proj, ln_g, ln_b, ws, bias_t, name):
    t = proj.shape[0]
    tm = GMLP_ROWS

    def body(u_ref, v_ref, lg_ref, lb_ref, ws_ref, bt_ref, o_ref):
        for ch in range(tm // B_CHUNK):
            rows = slice(ch * B_CHUNK, (ch + 1) * B_CHUNK)
            outs = _gmlp_chunk(_lane_groups(u_ref, rows), _lane_groups(v_ref, rows), _lane_groups(lg_ref),
                               _lane_groups(lb_ref), [ws_ref[g] for g in range(B_GROUPS)],
                               [bt_ref[:, g:g + 1] for g in range(B_GROUPS)])
            for g in range(B_GROUPS):
                o_ref[rows, g * LANES:(g + 1) * LANES] = outs[g].astype(BF16)

    vec = pl.BlockSpec((1, B_WIDTH), lambda i: (0, 0))
    return pl.pallas_call(
        body, name=name, grid=(t // tm,),
        in_specs=[pl.BlockSpec((tm, B_WIDTH), lambda i: (i, 4)), pl.BlockSpec((tm, B_WIDTH), lambda i: (i, 5)), vec, vec,
                  pl.BlockSpec((B_GROUPS, B_CHUNK, B_CHUNK), lambda i: (0, 0, 0)),
                  pl.BlockSpec((B_CHUNK, B_GROUPS), lambda i: (0, 0))],
        out_specs=pl.BlockSpec((tm, B_WIDTH), lambda i: (i, 0)),
        out_shape=jax.ShapeDtypeStruct((t, B_WIDTH), BF16),
        compiler_params=_params(("parallel",)),
    )(proj, proj, ln_g, ln_b, ws, bias_t)


def _gmlp_bwd(proj, dcat, ln_g, ln_b, ws, bias_t, name):
    t = proj.shape[0]
    tm = GMLP_ROWS

    def body(u_ref, v_ref, do_ref, lg_ref, lb_ref, ws_ref, bt_ref, duv_ref, dlg_ref, dlb_ref, dws_ref, dbt_ref):
        @pl.when(pl.program_id(0) == 0)
        def _():
            dlg_ref[...] = jnp.zeros_like(dlg_ref)
            dlb_ref[...] = jnp.zeros_like(dlb_ref)
            dws_ref[...] = jnp.zeros_like(dws_ref)
            dbt_ref[...] = jnp.zeros_like(dbt_ref)

        for ch in range(tm // B_CHUNK):
            rows = slice(ch * B_CHUNK, (ch + 1) * B_CHUNK)
            _, vjp = jax.vjp(
                _gmlp_chunk, _lane_groups(u_ref, rows), _lane_groups(v_ref, rows), _lane_groups(lg_ref),
                _lane_groups(lb_ref), [ws_ref[g] for g in range(B_GROUPS)],
                [bt_ref[:, g:g + 1] for g in range(B_GROUPS)])
            du, dv, dlg, dlb, dw, dbt = vjp(_lane_groups(do_ref, rows))
            for g in range(B_GROUPS):
                lanes = slice(g * LANES, (g + 1) * LANES)
                duv_ref[rows, lanes] = du[g].astype(BF16)
                duv_ref[rows, B_WIDTH + g * LANES:B_WIDTH + (g + 1) * LANES] = dv[g].astype(BF16)
                dlg_ref[0:1, lanes] += dlg[g]
                dlb_ref[0:1, lanes] += dlb[g]
                dws_ref[g] += dw[g]
                dbt_ref[:, g:g + 1] += dbt[g]

    vec = pl.BlockSpec((1, B_WIDTH), lambda i: (0, 0))
    acc8 = pl.BlockSpec((SUBLANES, B_WIDTH), lambda i: (0, 0))
    ws_spec = pl.BlockSpec((B_GROUPS, B_CHUNK, B_CHUNK), lambda i: (0, 0, 0))
    bt_spec = pl.BlockSpec((B_CHUNK, B_GROUPS), lambda i: (0, 0))
    return pl.pallas_call(
        body, name=name, grid=(t // tm,),
        in_specs=[pl.BlockSpec((tm, B_WIDTH), lambda i: (i, 4)), pl.BlockSpec((tm, B_WIDTH), lambda i: (i, 5)),
                  pl.BlockSpec((tm, B_WIDTH), lambda i: (i, 1)), vec, vec, ws_spec, bt_spec],
        out_specs=[pl.BlockSpec((tm, 2 * B_WIDTH), lambda i: (i, 0)), acc8, acc8, ws_spec, bt_spec],
        out_shape=[jax.ShapeDtypeStruct((t, 2 * B_WIDTH), BF16), jax.ShapeDtypeStruct((SUBLANES, B_WIDTH), F32),
                   jax.ShapeDtypeStruct((SUBLANES, B_WIDTH), F32),
                   jax.ShapeDtypeStruct((B_GROUPS, B_CHUNK, B_CHUNK), F32),
                   jax.ShapeDtypeStruct((B_CHUNK, B_GROUPS), F32)],
        compiler_params=_params(("arbitrary",)),
    )(proj, proj, dcat, ln_g, ln_b, ws, bias_t)


QK_SCALE = 1.0 / math.sqrt(C_HEAD_DIM)
ATTN_UNROLL = 8
LANE_GROUPS = D_MODEL // LANES
Q_BLOCKS = SEQ // C_BLOCK


def _attn_window(i, d):
    sub_blocks = Q_BLOCKS // d
    q0 = pl.multiple_of(i * C_BLOCK, C_BLOCK)
    k0 = pl.multiple_of(jnp.maximum(i - 1, 0) * C_BLOCK, C_BLOCK)
    key = k0 + lax.broadcasted_iota(jnp.int32, (C_BLOCK, 2 * C_BLOCK), 1)
    dist = (q0 + lax.broadcasted_iota(jnp.int32, (C_BLOCK, 2 * C_BLOCK), 0)) - key
    own_subsequence = (key >= q0) | (i % sub_blocks > 0)
    return pl.ds(q0, C_BLOCK), pl.ds(k0, 2 * C_BLOCK), (dist >= 0) & (dist <= C_BLOCK) & own_subsequence


def _head_masks():
    lane = lax.broadcasted_iota(jnp.int32, (C_BLOCK, LANES), 1)
    return [lane < C_HEAD_DIM, lane >= C_HEAD_DIM]


def _flat_spec(col_of):
    return pl.BlockSpec((1, SEQ, LANES), lambda b, g: (b, 0, col_of(g)))


def _put_heads(tile, g, col0, col1):
    lane = lax.broadcasted_iota(jnp.int32, tile.shape, 1)
    return jnp.where(lane == 2 * g, col0, jnp.where(lane == 2 * g + 1, col1, tile))


def _get_head(tile, h):
    lane = lax.broadcasted_iota(jnp.int32, tile.shape, 1)
    return jnp.sum(jnp.where(lane == h, tile, 0.0), axis=1, keepdims=True)


PER_HEAD_SPEC = pl.BlockSpec((1, SEQ, LANES), lambda b, g: (b, 0, 0))


def _attn_branch_fwd(qkv, name):
    n_seq, d, l, _ = qkv.shape
    flat = qkv.reshape(n_seq, SEQ, ODD_IN)

    def body(q_ref, k_ref, v_ref, o_ref, m_ref, l_ref):
        heads = _head_masks()
        g = pl.program_id(1)

        @pl.when(g == 0)
        def _():
            m_ref[...] = jnp.zeros_like(m_ref)
            l_ref[...] = jnp.zeros_like(l_ref)

        def block(i, carry):
            rows, keys, mask = _attn_window(i, d)
            q, k, v = q_ref[0, rows, :], k_ref[0, keys, :], v_ref[0, keys, :]
            res = []
            for hm in heads:
                s = jnp.where(mask, _dot_nt(jnp.where(hm, q, 0), k) * QK_SCALE, NEG)
                m = jnp.max(s, axis=-1, keepdims=True)
                p = jnp.exp(s - m)
                res.append((_dot(p.astype(BF16), v), m, jnp.sum(p, axis=-1, keepdims=True)))
            o_ref[0, rows, :] = jnp.where(heads[0], res[0][0], res[1][0])
            m_ref[0, rows, :] = _put_heads(m_ref[0, rows, :], g, res[0][1], res[1][1])
            l_ref[0, rows, :] = _put_heads(l_ref[0, rows, :], g, res[0][2], res[1][2])
            return carry

        lax.fori_loop(0, Q_BLOCKS, block, 0, unroll=ATTN_UNROLL)

    o, m, l_sum = pl.pallas_call(
        body, name=name, grid=(n_seq, LANE_GROUPS),
        in_specs=[_flat_spec(lambda g: g), _flat_spec(lambda g: LANE_GROUPS + g),
                  _flat_spec(lambda g: 2 * LANE_GROUPS + g)],
        out_specs=[_flat_spec(lambda g: g), PER_HEAD_SPEC, PER_HEAD_SPEC],
        out_shape=[jax.ShapeDtypeStruct((n_seq, SEQ, D_MODEL), F32)] + [jax.ShapeDtypeStruct((n_seq, SEQ, LANES), F32)] * 2,
        compiler_params=_params(("parallel", "arbitrary")),
    )(flat, flat, flat)
    return [o.reshape(n_seq, d, l, D_MODEL), m.reshape(n_seq, d, l, LANES), l_sum.reshape(n_seq, d, l, LANES)]


def _attn_merge(branches, name):
    n_seq = branches[0][0].shape[0]
    t = n_seq * SEQ
    tm = MERGE_TILE

    def body(*refs):
        ins = refs[:9]
        o_ref, ob_ref, lse_ref = refs[9:12]
        nat = refs[12:]
        for b, d in enumerate(C_DILATIONS[1:]):
            for k in range(3):
                _load_dilated(ins[3 + 3 * b + k], d, nat[3 * b + k])
        ms = [ins[1][0, 0], nat[1][0], nat[4][0]]
        ls = [ins[2][0, 0], nat[2][0], nat[5][0]]
        m_all = jnp.maximum(jnp.maximum(ms[0], ms[1]), ms[2])
        ws = [jnp.exp(ms[b] - m_all) for b in range(3)]
        lane = lax.broadcasted_iota(jnp.int32, m_all.shape, 1)
        total = jnp.where(lane < C_HEADS, ws[0] * ls[0] + ws[1] * ls[1] + ws[2] * ls[2], 1.0)
        lse_ref[...] = m_all + jnp.log(total)
        first_head = lane < C_HEAD_DIM
        for p in range(LANE_GROUPS):
            lanes = slice(p * LANES, (p + 1) * LANES)
            spread = lambda c: jnp.where(first_head, c[:, 2 * p:2 * p + 1], c[:, 2 * p + 1:2 * p + 2])
            os_ = [ins[0][0, 0, :, lanes], nat[0][p], nat[3][p]]
            o = (spread(ws[0]) * os_[0] + spread(ws[1]) * os_[1] + spread(ws[2]) * os_[2]) / spread(total)
            o_ref[:, lanes] = o
            ob_ref[:, lanes] = o.astype(BF16)

    row = pl.BlockSpec((tm, D_MODEL), lambda i: (i, 0))
    flat = [a for br in branches for a in br]
    in_specs = []
    for wide, narrow in zip(_dilated_specs(tm, D_MODEL, lambda: 0), _dilated_specs(tm, LANES, lambda: 0)):
        in_specs += [wide, narrow, narrow]
    per_head = pltpu.VMEM((1, tm, LANES), F32)
    return pl.pallas_call(
        body, name=name, grid=(t // tm,), in_specs=in_specs,
        out_specs=[row, row, pl.BlockSpec((tm, LANES), lambda i: (i, 0))],
        out_shape=[jax.ShapeDtypeStruct((t, D_MODEL), F32), jax.ShapeDtypeStruct((t, D_MODEL), BF16),
                   jax.ShapeDtypeStruct((t, LANES), F32)],
        scratch_shapes=[pltpu.VMEM((LANE_GROUPS, tm, LANES), F32), per_head, per_head] * 2,
        compiler_params=_params(("parallel",)),
    )(*flat)


def _attn_branch_bwd(qkv, dout, lse, delta, name):
    n_seq, d, l, _ = qkv.shape
    flat = lambda a: a.reshape(n_seq, SEQ, a.shape[-1])

    def body(q_ref, k_ref, v_ref, do_ref, lse_nat_ref, dl_nat_ref, dq_ref, dk_ref, dv_ref, lse_ref, dl_ref,
             dkt_ref, dvt_ref):
        heads = _head_masks()
        g = pl.program_id(1)
        dkt_ref[...] = jnp.zeros_like(dkt_ref)
        dvt_ref[...] = jnp.zeros_like(dvt_ref)
        for nat_ref, dst_ref in ((lse_nat_ref, lse_ref), (dl_nat_ref, dl_ref)):
            for r in range(d):
                rows = pl.ds(r, l, stride=d) if d > 1 else slice(None)
                dst_ref[r * l:(r + 1) * l, :] = nat_ref.at[0][rows, :]

        def block(i, carry):
            rows, keys, mask = _attn_window(i, d)
            q, do = q_ref[0, rows, :], do_ref[0, rows, :]
            k, v = k_ref[0, keys, :], v_ref[0, keys, :]
            lse_b, dl_b = lse_ref[rows, :], dl_ref[rows, :]
            dq, dk, dv = [], None, None
            for hh, hm in enumerate(heads):
                qh, doh = jnp.where(hm, q, 0), jnp.where(hm, do, 0)
                s = jnp.where(mask, _dot_nt(qh, k) * QK_SCALE, NEG)
                p = jnp.exp(s - _get_head(lse_b, 2 * g + hh))
                ds = (p * (_dot_nt(doh, v) - _get_head(dl_b, 2 * g + hh)) * QK_SCALE).astype(BF16)
                dq.append(_dot(ds, k))
                dk_h, dv_h = _dot_tn(qh, ds), _dot_tn(doh, p.astype(BF16))
                dk = dk_h if dk is None else dk + dk_h
                dv = dv_h if dv is None else dv + dv_h
            dq_ref[0, rows, :] = jnp.where(heads[0], dq[0], dq[1]).astype(BF16)
            dkt_ref[:, keys] += dk
            dvt_ref[:, keys] += dv
            return carry

        lax.fori_loop(0, Q_BLOCKS, block, 0, unroll=ATTN_UNROLL)
        for c in range(SEQ // ROW_TILE):
            rows = slice(c * ROW_TILE, (c + 1) * ROW_TILE)
            dk_ref[0, rows, :] = dkt_ref[:, rows].T.astype(BF16)
            dv_ref[0, rows, :] = dvt_ref[:, rows].T.astype(BF16)

    act = _flat_spec(lambda g: g)
    outs = pl.pallas_call(
        body, name=name, grid=(n_seq, LANE_GROUPS),
        in_specs=[_flat_spec(lambda g: g), _flat_spec(lambda g: LANE_GROUPS + g),
                  _flat_spec(lambda g: 2 * LANE_GROUPS + g), act, PER_HEAD_SPEC, PER_HEAD_SPEC],
        out_specs=[act] * 3,
        out_shape=[jax.ShapeDtypeStruct((n_seq, SEQ, D_MODEL), BF16)] * 3,
        scratch_shapes=[pltpu.VMEM((SEQ, LANES), F32)] * 2 + [pltpu.VMEM((LANES, SEQ), F32)] * 2,
        compiler_params=_params(("parallel", "parallel")),
    )(flat(qkv), flat(qkv), flat(qkv), flat(dout), lse, delta)
    return [o.reshape(n_seq, d, l, D_MODEL) for o in outs]


def _attn_combine_bwd(grads, rope, name):
    n_seq = grads[0][0].shape[0]
    t = n_seq * SEQ
    tm = MERGE_TILE

    def body(*refs):
        c_ref, s_ref, o_ref, nat4_ref, nat16_ref = refs[9:]
        for sec in range(3):
            _load_dilated(refs[3 + sec], 4, nat4_ref)
            _load_dilated(refs[6 + sec], 16, nat16_ref)
            for p in range(LANE_GROUPS):
                blk = refs[sec][0, 0, :, p * LANES:(p + 1) * LANES] + nat4_ref[p] + nat16_ref[p]
                if sec < 2:
                    blk = blk * c_ref[...] - _swap_halves(blk) * s_ref[...]
                o_ref[:, sec * D_MODEL + p * LANES:sec * D_MODEL + (p + 1) * LANES] = blk.astype(BF16)

    tab = pl.BlockSpec((tm, LANES), lambda i: (i, 0))
    flat = [a for br in grads for a in br]
    in_specs = []
    for spec in _dilated_specs(tm, D_MODEL, lambda: 0):
        in_specs += [spec] * 3
    return pl.pallas_call(
        body, name=name, grid=(t // tm,), in_specs=in_specs + [tab, tab],
        out_specs=pl.BlockSpec((tm, ODD_IN), lambda i: (i, 0)),
        out_shape=jax.ShapeDtypeStruct((t, ODD_IN), BF16),
        scratch_shapes=[pltpu.VMEM((LANE_GROUPS, tm, LANES), F32)] * 2,
        compiler_params=_params(("parallel",)),
    )(*flat, *rope)


def _loss_grad(y, target, name):
    t = y.shape[0]
    tm = ROW_TILE

    def body(y_ref, t_ref, d_ref, l_ref):
        diff = y_ref[...] - t_ref[...]
        d_ref[...] = diff * (1.0 / D_MODEL)
        _acc_rows8(l_ref, _rows8(diff * diff) * (0.5 / D_MODEL), pl.program_id(0) == 0)

    row = pl.BlockSpec((tm, D_MODEL), lambda i: (i, 0))
    return pl.pallas_call(
        body, name=name, grid=(t // tm,), in_specs=[row, row],
        out_specs=[row, pl.BlockSpec((SUBLANES, D_MODEL), lambda i: (0, 0))],
        out_shape=[jax.ShapeDtypeStruct((t, D_MODEL), F32), jax.ShapeDtypeStruct((SUBLANES, D_MODEL), F32)],
        compiler_params=_params(("arbitrary",)),
    )(y, target)


def _adamw(w, g, m, v):
    m = ADAM_B1 * m + (1.0 - ADAM_B1) * g
    v = ADAM_B2 * v + (1.0 - ADAM_B2) * jnp.square(g)
    m_hat = m / (1.0 - ADAM_B1 ** ADAM_STEP)
    v_hat = v / (1.0 - ADAM_B2 ** ADAM_STEP)
    delta = -ADAM_LR * (m_hat / (jnp.sqrt(v_hat) + ADAM_EPS) + ADAM_WD * w)
    return delta, m, v


def _adamw_sharded(parts, w, m, v, name):
    n_layers, rows, cols = w.shape
    tr = min(rows, 256)

    def body(*refs):
        p_refs = refs[:n_layers]
        w_ref, m_ref, v_ref, g_ref, d_ref, mo_ref, vo_ref = refs[n_layers:]
        layer = pl.program_id(0)
        g = None
        for l, p_ref in enumerate(p_refs):
            g_l = p_ref[0].astype(F32)
            for s in range(1, N_DEV):
                g_l = g_l + p_ref[s].astype(F32)
            g = g_l if g is None else jnp.where(layer == l, g_l, g)
        delta, mn, vn = _adamw(w_ref[0], g, m_ref[0], v_ref[0])
        g_ref[0] = g
        d_ref[0] = delta
        mo_ref[0] = mn
        vo_ref[0] = vn

    def part_spec(l):
        return pl.BlockSpec((N_DEV, tr, cols), lambda a, i: (0, jnp.where(a == l, i, 0), 0))

    row = pl.BlockSpec((1, tr, cols), lambda a, i: (a, i, 0))
    return pl.pallas_call(
        body, name=name, grid=(n_layers, rows // tr),
        in_specs=[part_spec(l) for l in range(n_layers)] + [row, row, row],
        out_specs=[row] * 4, out_shape=[jax.ShapeDtypeStruct(w.shape, F32)] * 4,
        compiler_params=_params(("arbitrary", "arbitrary")),
    )(*parts, w, m, v)


def _small_update(gathered, where, weights, moments_m, moments_v, lb_index, name):
    n = len(weights)
    n_g = len(gathered)

    def body(*refs):
        g_refs = refs[:n_g]
        w_refs, m_refs, v_refs = refs[n_g:n_g + n], refs[n_g + n:n_g + 2 * n], refs[n_g + 2 * n:n_g + 3 * n]
        outs = refs[n_g + 3 * n:]

        def total(k):
            array, rows, lanes = where[k]
            ref = g_refs[array]
            index = (slice(None),) * (len(ref.shape) - 1) if rows is None else (rows, lanes)
            acc = ref[(0,) + index]
            for s in range(1, N_DEV):
                acc = acc + ref[(s,) + index]
            return acc

        loss_rows = total(n)
        outs[0][...] = jnp.sum(jnp.sum(loss_rows, axis=1, keepdims=True), axis=0, keepdims=True)
        for k in range(n):
            part = total(k)
            if k == lb_index:
                dlb = jnp.sum(part, axis=0, keepdims=True)
                tab = w_refs[k][...]
                e = jnp.exp(tab - jnp.max(tab, axis=0, keepdims=True))
                p = e / jnp.sum(e, axis=0, keepdims=True)
                first = lax.broadcasted_iota(jnp.int32, p.shape, 0) == 0
                grads = [(slice(None), p * (jnp.where(first, dlb, 0.0) - p[0:1, :] * dlb))]
            elif part.shape == w_refs[k].shape:
                grads = [(slice(None), part)]
            else:
                grads = [(slice(l, l + 1), jnp.sum(part[l * SUBLANES:(l + 1) * SUBLANES], axis=0, keepdims=True))
                         for l in range(w_refs[k].shape[0])]
            for rows, g in grads:
                delta, mn, vn = _adamw(w_refs[k][rows], g, m_refs[k][rows], v_refs[k][rows])
                outs[1 + 4 * k][rows] = g
                outs[2 + 4 * k][rows] = delta
                outs[3 + 4 * k][rows] = mn
                outs[4 + 4 * k][rows] = vn

    vmem = pl.BlockSpec(memory_space=pltpu.VMEM)
    out_shape = [jax.ShapeDtypeStruct((1, 1), F32)]
    for w in weights:
        out_shape += [jax.ShapeDtypeStruct(w.shape, F32)] * 4
    args = list(gathered) + list(weights) + list(moments_m) + list(moments_v)
    return pl.pallas_call(
        body, name=name, in_specs=[vmem] * len(args), out_specs=[vmem] * len(out_shape), out_shape=out_shape,
        compiler_params=pltpu.CompilerParams(vmem_limit_bytes=VMEM_LIMIT),
    )(*args)


def kernel(x, positions, norm_mix_pre, norm_mix_post, norm_ffn_pre, norm_ffn_post, w_in_even, lb_table, a_norm, b_ln_g, b_ln_b, b_ws, b_bias, w_out_even, w_in_odd, w_out_odd, w_ff1, w_ff2, loss_target, m_norm_mix_pre, m_norm_mix_post, m_norm_ffn_pre, m_norm_ffn_post, m_w_in_even, m_lb_table, m_a_norm, m_b_ln_g, m_b_ln_b, m_b_ws, m_b_bias, m_w_out_even, m_w_in_odd, m_w_out_odd, m_w_ff1, m_w_ff2, v_norm_mix_pre, v_norm_mix_post, v_norm_ffn_pre, v_norm_ffn_post, v_w_in_even, v_lb_table, v_a_norm, v_b_ln_g, v_b_ln_b, v_b_ws, v_b_bias, v_w_out_even, v_w_in_odd, v_w_out_odd, v_w_ff1, v_w_ff2):
    n_seq = x.shape[0]
    t = n_seq * SEQ
    x0 = x.reshape(t, D_MODEL)
    target = loss_target.reshape(t, D_MODEL)

    me = _my_slot().astype(jnp.int32).reshape(1)

    order = ["in_e", "out_e", "ff1_0", "ff2_0", "in_o", "out_o", "ff1_1", "ff2_1"]
    shards = dict(in_e=w_in_even[0], out_e=w_out_even[0], in_o=w_in_odd[0], out_o=w_out_odd[0],
                  ff1_0=w_ff1[0], ff1_1=w_ff1[1], ff2_0=w_ff2[0], ff2_1=w_ff2[1])
    by_columns = ("in_e", "in_o", "ff1_0", "ff1_1")
    lands = [_place_own_columns(shards[k], me, "place_" + k) if k in by_columns
             else _place_own(shards[k], me, "place_" + k, False) for k in order]
    g_send, g_recv, lands, _, g_token = _exchange_start(lands, [None] * len(order), "gather_start")

    def get_w(keys, after):
        ks = [order.index(k) for k in keys]
        return _exchange_wait([lands[k] for k in ks], [None] * len(ks), [g_send[k] for k in ks],
                              [g_recv[k] for k in ks], after, "gather_wait_" + keys[0])

    sent = {}

    def put_g(group, blocks):
        keys = list(blocks)
        own = [_place_own(blocks[k], me, "own_" + k, True) for k in keys]
        send_sems, recv_sems, own, srcs, token = _exchange_start(own, [blocks[k] for k in keys], "scatter_start_" + group)
        sent[group] = (keys, own, srcs, send_sems, recv_sems)
        return token

    rope = _rope_tables(positions)
    bias_t = b_bias[0].T
    grads = _local_step(x0, target, rope, norm_mix_pre, norm_mix_post, norm_ffn_pre, norm_ffn_post, lb_table,
                        a_norm, b_ln_g, b_ln_b, b_ws[0], bias_t, get_w, put_g, g_token)
    (dx0, loss_part, dg_mix_pre, dg_mix_post, dg_ffn_pre, dg_ffn_post, d_lb, d_a_norm, d_ln_g, d_ln_b, d_ws,
     d_bias_t) = grads

    recv = {}
    for group, (keys, own, srcs, send_sems, recv_sems) in sent.items():
        done = _exchange_wait(own, srcs, send_sems, recv_sems, dx0, "scatter_wait_" + group)
        recv.update(zip(keys, done))
    big = [("w_in_even", ["in_e"], w_in_even, m_w_in_even, v_w_in_even),
           ("w_out_even", ["out_e"], w_out_even, m_w_out_even, v_w_out_even),
           ("w_in_odd", ["in_o"], w_in_odd, m_w_in_odd, v_w_in_odd),
           ("w_out_odd", ["out_o"], w_out_odd, m_w_out_odd, v_w_out_odd),
           ("w_ff1", ["ff1_0", "ff1_1"], w_ff1, m_w_ff1, v_w_ff1), ("w_ff2", ["ff2_0", "ff2_1"], w_ff2, m_w_ff2, v_w_ff2)]
    big_out = [_adamw_sharded([recv[k] for k in keys], w, m, v, "adamw_" + nm) for nm, keys, w, m, v in big]

    packed = jnp.concatenate([dg_mix_pre, dg_mix_post, dg_ffn_pre, dg_ffn_post,
                              jnp.concatenate([d_lb, d_a_norm], axis=1), jnp.concatenate([d_ln_g, d_ln_b], axis=1),
                              loss_part], axis=0)
    gathered = _exchange([packed, d_ws, d_bias_t], True, "gather_small")
    rows8 = lambda k: slice(SUBLANES * k, SUBLANES * (k + 1))
    left, right, every = slice(0, A_WIDTH), slice(A_WIDTH, 2 * A_WIDTH), slice(None)
    where = [(0, slice(0, 16), every), (0, slice(16, 32), every), (0, slice(32, 48), every), (0, slice(48, 64), every),
             (0, rows8(8), left), (0, rows8(8), right), (0, rows8(9), left), (0, rows8(9), right),
             (1, None, None), (2, None, None), (0, rows8(10), every)]
    small_w = [norm_mix_pre, norm_mix_post, norm_ffn_pre, norm_ffn_post, lb_table, a_norm, b_ln_g, b_ln_b,
               b_ws[0], bias_t]
    small_m = [m_norm_mix_pre, m_norm_mix_post, m_norm_ffn_pre, m_norm_ffn_post, m_lb_table, m_a_norm, m_b_ln_g,
               m_b_ln_b, m_b_ws[0], m_b_bias[0].T]
    small_v = [v_norm_mix_pre, v_norm_mix_post, v_norm_ffn_pre, v_norm_ffn_post, v_lb_table, v_a_norm, v_b_ln_g,
               v_b_ln_b, v_b_ws[0], v_b_bias[0].T]
    small_out = _small_update(gathered, where, small_w, small_m, small_v, 4, "small_update")
    loss = small_out[0].reshape(())
    small = [small_out[1 + 4 * k:5 + 4 * k] for k in range(len(small_w))]
    small[8] = [a[None] for a in small[8]]
    small[9] = [a.T[None] for a in small[9]]

    per_weight = small[0:4] + [big_out[0]] + small[4:10] + big_out[1:6]
    grad_x = dx0.reshape(x.shape)
    out = [loss, grad_x]
    for kind in range(4):
        out += [p[kind] for p in per_weight]
    return tuple(out)


def _local_step(x0, target, rope, norm_mix_pre, norm_mix_post, norm_ffn_pre, norm_ffn_post, lb_table, a_norm,
                b_ln_g, b_ln_b, ws, bias_t, get_w, put_g, token):
    def gain(a, l, tok):
        return a[l:l + 1] if tok is None else a[l:l + 1] + tok[0:1, 0:1]

    full = lambda a: a.reshape(-1, D_MODEL)
    owners = lambda a: a.reshape((N_DEV, -1) + a.shape[1:])

    (g_in_e,) = get_w(["in_e"], token)
    proj, h_mix0 = _norm_inproj(x0, gain(norm_mix_pre, 0, token), g_in_e, "inproj_even")
    oa, pre_a, states = _hgrn2_fwd(proj, lb_table, a_norm, "hgrn2_fwd")
    ob = _gmlp_fwd(proj, b_ln_g, b_ln_b, ws, bias_t, "gmlp_fwd")
    w_out_e = full(get_w(["out_e"], ob)[0])
    x1, mix0 = _outproj([oa, ob], w_out_e, x0, gain(norm_mix_post, 0, None), "outproj_even")
    w1_0, w2_0 = get_w(["ff1_0", "ff2_0"], x1)
    w2_0 = full(w2_0)
    x2, y0, h_ffn0, r0 = _ffn_fwd(x1, gain(norm_ffn_pre, 0, None), w1_0, w2_0, gain(norm_ffn_post, 0, None), "ffn_fwd_0")
    (g_in_o,) = get_w(["in_o"], x2)
    *qkv, h_mix1 = _norm_inproj_rope(x2, gain(norm_mix_pre, 1, None), g_in_o, rope, "inproj_odd")
    branches = [_attn_branch_fwd(a, "attn_fwd_d%d" % d) for a, d in zip(qkv, C_DILATIONS)]
    attn, attn_b, lse = _attn_merge(branches, "attn_merge")
    w_out_o = full(get_w(["out_o"], attn_b)[0])
    x3, mix1 = _outproj([attn_b], w_out_o, x2, gain(norm_mix_post, 1, None), "outproj_odd")
    w1_1, w2_1 = get_w(["ff1_1", "ff2_1"], x3)
    w2_1 = full(w2_1)
    x4, y1, h_ffn1, r1 = _ffn_fwd(x3, gain(norm_ffn_pre, 1, None), w1_1, w2_1, gain(norm_ffn_post, 1, None), "ffn_fwd_1")

    dx4, loss_part = _loss_grad(x4, target, "loss_grad")

    dx3, dy1, da1, dg_ffn_pre1, dg_ffn_post1 = _ffn_bwd(
        dx4, x3, y1, r1, gain(norm_ffn_pre, 1, None), w1_1, w2_1, gain(norm_ffn_post, 1, None), "ffn_bwd_1")
    gw_ff1_1 = _grad_w(h_ffn1, da1, True, "grad_w_ff1_1")
    gw_ff2_1 = _grad_w(r1, dy1, False, "grad_w_ff2_1")
    tok = put_g("ffn1", dict(ff1_1=gw_ff1_1, ff2_1=owners(gw_ff2_1)))
    *dattn, delta, dz1, dg_mix_post1 = _outproj_bwd_attn(dx3, mix1, gain(norm_mix_post, 1, tok), w_out_o, attn,
                                                  "outproj_bwd_odd")
    gw_out_o = _grad_w(attn_b, dz1, False, "grad_w_out_odd")
    per_seq = lambda a: a.reshape(-1, SEQ, LANES)
    grads_c = [_attn_branch_bwd(qkv[b], dattn[b], per_seq(lse), per_seq(delta), "attn_bwd_d%d" % d)
               for b, d in enumerate(C_DILATIONS)]
    dqkv = _attn_combine_bwd(grads_c, rope, "attn_combine_bwd")
    gw_in_o = _grad_w(h_mix1, dqkv, True, "grad_w_in_odd")
    tok = put_g("mix1", dict(out_o=owners(gw_out_o), in_o=gw_in_o))
    dx2, dg_mix_pre1 = _inproj_bwd(dqkv, g_in_o, dx3, x2, gain(norm_mix_pre, 1, tok), "inproj_bwd_odd")

    dx1, dy0, da0, dg_ffn_pre0, dg_ffn_post0 = _ffn_bwd(
        dx2, x1, y0, r0, gain(norm_ffn_pre, 0, None), w1_0, w2_0, gain(norm_ffn_post, 0, None), "ffn_bwd_0")
    gw_ff1_0 = _grad_w(h_ffn0, da0, True, "grad_w_ff1_0")
    gw_ff2_0 = _grad_w(r0, dy0, False, "grad_w_ff2_0")
    tok = put_g("ffn0", dict(ff1_0=gw_ff1_0, ff2_0=owners(gw_ff2_0)))
    dcat, dz0, dg_mix_post0 = _outproj_bwd(dx1, mix0, gain(norm_mix_post, 0, tok), w_out_e, "outproj_bwd_even")
    gw_out_e = jnp.concatenate([_grad_w(oa, dz0, False, "grad_w_out_even_a"),
                                _grad_w(ob, dz0, False, "grad_w_out_even_b")], axis=0)
    dqfig, d_lb, d_a_norm = _hgrn2_bwd(proj, dcat, pre_a, states, lb_table, a_norm, "hgrn2_bwd")
    duv, d_ln_g, d_ln_b, d_ws, d_bias_t = _gmlp_bwd(proj, dcat, b_ln_g, b_ln_b, ws, bias_t, "gmlp_bwd")
    dproj = jnp.concatenate([dqfig, duv], axis=1)
    gw_in_e = _grad_w(h_mix0, dproj, True, "grad_w_in_even")
    tok = put_g("mix0", dict(out_e=owners(gw_out_e), in_e=gw_in_e))
    dx0, dg_mix_pre0 = _inproj_bwd(dproj, g_in_e, dx1, x0, gain(norm_mix_pre, 0, tok), "inproj_bwd_even")

    layers = lambda a, b: jnp.concatenate([a, b], axis=0)
    return (dx0, loss_part, layers(dg_mix_pre0, dg_mix_pre1), layers(dg_mix_post0, dg_mix_post1),
            layers(dg_ffn_pre0, dg_ffn_pre1), layers(dg_ffn_post0, dg_ffn_post1),
            d_lb, d_a_norm, d_ln_g, d_ln_b, d_ws, d_bias_t)
```

```python
import functools
import math

import jax
import jax.numpy as jnp
from jax import lax
from jax.experimental import pallas as pl
from jax.experimental.pallas import tpu as pltpu

F32 = jnp.float32
BF16 = jnp.bfloat16
MESH = pl.DeviceIdType.MESH

N_DEV = 8
D_MODEL = 1024
SEQ = 2048
EPS = 1e-6
A_WIDTH = 512
A_HEADS = 4
HEAD_A = 128
B_WIDTH = 512
B_GROUPS = 4
B_CHUNK = 128
C_HEADS = 16
C_HEAD_DIM = 64
C_ROT_HALF = 8
ROPE_THETA = 500000.0
C_DILATIONS = (1, 4, 16)
C_BLOCK = 128
D_FF = 4096
EVEN_IN = 3072
ODD_IN = 3072

ADAM_LR = 0.001
ADAM_B1 = 0.9
ADAM_B2 = 0.999
ADAM_EPS = 1e-08
ADAM_WD = 0.01
ADAM_STEP = 10

LANES = 128
SUBLANES = 8
ROW_TILE = 512
PROJ_TILE = 1024
PROJ_COLS = 768
MERGE_TILE = 256
SUB_CHUNK = 16
HGRN_BLOCK = 256
NEG = -1e30
VMEM_LIMIT = 56 * 1024 * 1024


def _params(sem):
    return pltpu.CompilerParams(dimension_semantics=sem, vmem_limit_bytes=VMEM_LIMIT)


def _dot(a, b):
    return jnp.dot(a, b, preferred_element_type=F32)


def _dot_nt(a, b):
    return lax.dot_general(a, b, (((1,), (1,)), ((), ())), preferred_element_type=F32)


def _dot_tn(a, b):
    return lax.dot_general(a, b, (((0,), (0,)), ((), ())), preferred_element_type=F32)


def _rms(x, g):
    r = lax.rsqrt(jnp.mean(x * x, axis=-1, keepdims=True) + EPS)
    return x * r * g


def _rms_bwd(x, g, dy):
    r = lax.rsqrt(jnp.mean(x * x, axis=-1, keepdims=True) + EPS)
    dyg = dy * g
    dx = r * dyg - x * (r * r * r) * jnp.mean(x * dyg, axis=-1, keepdims=True)
    return dx, dy * x * r


def _rows8(v):
    return v.reshape(v.shape[0] // SUBLANES, SUBLANES, v.shape[1]).sum(axis=0)


def _sigmoid(x):
    return 1.0 / (1.0 + jnp.exp(-x))


def _gelu(x):
    return 0.5 * x * (1.0 + jnp.tanh(math.sqrt(2.0 / math.pi) * (x + 0.044715 * (x * x * x))))


def _acc_rows8(ref, val, first):
    @pl.when(first)
    def _():
        ref[...] = val

    @pl.when(jnp.logical_not(first))
    def _():
        ref[...] += val


def _my_slot():
    return 4 * lax.axis_index("x") + 2 * lax.axis_index("y") + lax.axis_index("c")


def _peer(r):
    x, y, c = lax.axis_index("x"), lax.axis_index("y"), lax.axis_index("c")
    px = 1 - x if (r >> 2) & 1 else x
    py = 1 - y if (r >> 1) & 1 else y
    pc = 1 - c if r & 1 else c
    return (px, py, pc), 4 * px + 2 * py + pc


HBM_SPEC = pl.BlockSpec(memory_space=pltpu.HBM)
SEM_SPEC = pl.BlockSpec(memory_space=pltpu.SEMAPHORE)
SPLIT_EFFECT = pltpu.SideEffectType.DATAFLOW_SIDE_EFFECTING


def _split_copies(land_ref, src_ref, send_sem, recv_sem):
    me = _my_slot()
    copies = []
    for r in range(1, N_DEV):
        peer, slot = _peer(r)
        src = _slot(land_ref, me) if src_ref is None else _slot(src_ref, slot)
        copies.append(pltpu.make_async_remote_copy(
            src_ref=src, dst_ref=_slot(land_ref, me), send_sem=send_sem, recv_sem=recv_sem,
            device_id=peer, device_id_type=MESH))
    return copies


def _slot(ref, s):
    if len(ref.shape) == 2:
        c = ref.shape[1] // N_DEV
        return ref.at[:, pl.ds(pl.multiple_of(s * c, LANES), c)]
    return ref.at[s]


def _exchange_start(lands, sources, name):
    n = len(lands)
    given = [s for s in sources if s is not None]
    arrays = list(lands) + given

    def body(*refs):
        land_refs, src_refs = refs[:n], list(refs[n:n + len(given)])
        sems = refs[len(arrays):len(arrays) + 2 * n]
        token = refs[-1]
        for k in range(n):
            src_ref = None if sources[k] is None else src_refs.pop(0)
            for copy in _split_copies(land_refs[k], src_ref, sems[k], sems[n + k]):
                copy.start()
        token[...] = jnp.zeros_like(token)

    outs = pl.pallas_call(
        body, name=name,
        out_shape=(pltpu.SemaphoreType.DMA(()),) * (2 * n) + tuple(pltpu.HBM(a.shape, a.dtype) for a in arrays)
        + (jax.ShapeDtypeStruct((SUBLANES, LANES), F32),),
        in_specs=[HBM_SPEC] * len(arrays),
        out_specs=(SEM_SPEC,) * (2 * n) + (HBM_SPEC,) * len(arrays) + (pl.BlockSpec(memory_space=pltpu.VMEM),),
        input_output_aliases={i: 2 * n + i for i in range(len(arrays))},
        compiler_params=pltpu.CompilerParams(has_side_effects=SPLIT_EFFECT),
    )(*[pltpu.with_memory_space_constraint(a, pltpu.HBM) for a in arrays])
    return list(outs[:n]), list(outs[n:2 * n]), list(outs[2 * n:3 * n]), list(outs[3 * n:-1]), outs[-1]


def _exchange_wait(lands, sources, send_sems, recv_sems, after, name):
    n = len(lands)
    given = [s for s in sources if s is not None]
    arrays = list(lands) + given

    def body(*refs):
        land_refs, src_refs = refs[:n], list(refs[n:n + len(given)])
        sems = refs[len(arrays):len(arrays) + 2 * n]
        for i in range(n):
            src_ref = None if sources[i] is None else src_refs.pop(0)
            copies = _split_copies(land_refs[i], src_ref, sems[i], sems[n + i])
            for copy in copies:
                copy.wait_recv()
            for copy in copies:
                copy.wait_send()

    outs = pl.pallas_call(
        body, name=name, out_shape=tuple(pltpu.HBM(a.shape, a.dtype) for a in arrays),
        in_specs=[HBM_SPEC] * len(arrays) + [SEM_SPEC] * (2 * n) + [pl.BlockSpec(memory_space=pl.ANY)],
        out_specs=(HBM_SPEC,) * len(arrays),
        input_output_aliases={i: i for i in range(len(arrays))},
        compiler_params=pltpu.CompilerParams(has_side_effects=SPLIT_EFFECT),
    )(*arrays, *send_sems, *recv_sems, after)
    return list(outs[:n])


def _place_own(a, me, name, own_block, dtype=BF16):
    shape = a.shape[1:] if own_block else a.shape
    cols = shape[-1]
    a3 = a.reshape((N_DEV if own_block else 1, -1, cols))
    rows = a3.shape[1]
    tr = min(rows, 512)

    def body(me_ref, a_ref, o_ref):
        o_ref[...] = a_ref[...].astype(dtype)

    grid_spec = pltpu.PrefetchScalarGridSpec(
        num_scalar_prefetch=1, grid=(rows // tr,),
        in_specs=[pl.BlockSpec((1, tr, cols), lambda i, me_ref: (me_ref[0] if own_block else 0, i, 0))],
        out_specs=pl.BlockSpec((1, tr, cols), lambda i, me_ref: (me_ref[0], i, 0)))
    out = pl.pallas_call(
        body, name=name, grid_spec=grid_spec, out_shape=jax.ShapeDtypeStruct((N_DEV, rows, cols), dtype),
        compiler_params=_params(("arbitrary",)),
    )(me, a3)
    return out.reshape((N_DEV,) + shape)


def _place_own_columns(a, me, name):
    rows, cols = a.shape
    tr = min(rows, 512)

    def body(me_ref, a_ref, o_ref):
        o_ref[...] = a_ref[...].astype(BF16)

    grid_spec = pltpu.PrefetchScalarGridSpec(
        num_scalar_prefetch=1, grid=(rows // tr,),
        in_specs=[pl.BlockSpec((tr, cols), lambda i, me_ref: (i, 0))],
        out_specs=pl.BlockSpec((tr, cols), lambda i, me_ref: (i, me_ref[0])))
    return pl.pallas_call(
        body, name=name, grid_spec=grid_spec, out_shape=jax.ShapeDtypeStruct((rows, N_DEV * cols), BF16),
        compiler_params=_params(("arbitrary",)),
    )(me, a)


def _rope_tables(positions):
    inv = ROPE_THETA ** (-jnp.arange(C_ROT_HALF, dtype=F32) / C_ROT_HALF)
    ang = positions.reshape(-1)[:, None].astype(F32) * inv
    cos, sin = jnp.cos(ang), jnp.sin(ang)
    t = ang.shape[0]
    ones = jnp.ones((t, C_HEAD_DIM - 2 * C_ROT_HALF), F32)
    c_head = jnp.concatenate([cos, cos, ones], axis=1)
    s_head = jnp.concatenate([-sin, sin, 0.0 * ones], axis=1)
    return jnp.concatenate([c_head, c_head], axis=1), jnp.concatenate([s_head, s_head], axis=1)


def _swap_halves(x):
    lane = lax.broadcasted_iota(jnp.int32, x.shape, 1) % C_HEAD_DIM
    return jnp.where(lane < C_ROT_HALF, pltpu.roll(x, LANES - C_ROT_HALF, 1), pltpu.roll(x, C_ROT_HALF, 1))


def _norm_inproj(x, g, w, name):
    t = x.shape[0]
    n = w.shape[1]
    tm, tn = PROJ_TILE, PROJ_COLS

    def body(x_ref, g_ref, w_ref, o_ref, h_ref):
        @pl.when(pl.program_id(1) == 0)
        def _():
            h_ref[...] = _rms(x_ref[...], g_ref[...]).astype(BF16)

        o_ref[...] = _dot(h_ref[...], w_ref[...])

    return pl.pallas_call(
        body, name=name, grid=(t // tm, n // tn),
        in_specs=[pl.BlockSpec((tm, D_MODEL), lambda i, j: (i, 0)), pl.BlockSpec((1, D_MODEL), lambda i, j: (0, 0)),
                  pl.BlockSpec((D_MODEL, tn), lambda i, j: (0, j))],
        out_specs=[pl.BlockSpec((tm, tn), lambda i, j: (i, j)), pl.BlockSpec((tm, D_MODEL), lambda i, j: (i, 0))],
        out_shape=[jax.ShapeDtypeStruct((t, n), F32), jax.ShapeDtypeStruct((t, D_MODEL), BF16)],
        compiler_params=_params(("parallel", "arbitrary")),
    )(x, g, w)


def _dilated_specs(tm, width, col_of):
    per_seq = SEQ // tm
    specs = []
    for d in C_DILATIONS:
        specs.append(pl.BlockSpec(
            (1, d, tm // d, width), lambda i, *rest: (i // per_seq, 0, i % per_seq, col_of(*rest))))
    return specs


def _dilated_shapes(n_seq, cols, dtype):
    return [jax.ShapeDtypeStruct((n_seq, d, SEQ // d, cols), dtype) for d in C_DILATIONS]


def _store_dilated(src_ref, out_refs, dtype):
    groups, tm, _ = src_ref.shape
    for d, o_ref in zip(C_DILATIONS, out_refs):
        for r in range(d):
            rows = pl.ds(r, tm // d, stride=d) if d > 1 else slice(None)
            for p in range(groups):
                o_ref[0, r, :, p * LANES:(p + 1) * LANES] = src_ref.at[p][rows, :].astype(dtype)


def _load_dilated(in_ref, d, dst_ref):
    groups, tm, _ = dst_ref.shape
    for r in range(d):
        rows = pl.ds(r, tm // d, stride=d)
        for p in range(groups):
            dst_ref.at[p][rows, :] = in_ref[0, r, :, p * LANES:(p + 1) * LANES].astype(F32)


def _norm_inproj_rope(x, g, w, rope, name):
    t = x.shape[0]
    n = w.shape[1]
    tm, nb = PROJ_TILE, PROJ_COLS

    def body(x_ref, g_ref, w_ref, c_ref, s_ref, o1_ref, o4_ref, o16_ref, h_ref, tile_ref):
        j = pl.program_id(1)

        @pl.when(j == 0)
        def _():
            h_ref[...] = _rms(x_ref[...], g_ref[...]).astype(BF16)

        acc = _dot(h_ref[...], w_ref[...])
        for p in range(nb // LANES):
            blk = acc[:, p * LANES:(p + 1) * LANES]
            roped = blk * c_ref[...] + _swap_halves(blk) * s_ref[...]
            is_qk = (j * (nb // LANES) + p) < 2 * (D_MODEL // LANES)
            tile_ref[p] = jnp.where(is_qk, roped, blk)
        _store_dilated(tile_ref, (o1_ref, o4_ref, o16_ref), BF16)

    return pl.pallas_call(
        body, name=name, grid=(t // tm, n // nb),
        in_specs=[pl.BlockSpec((tm, D_MODEL), lambda i, j: (i, 0)), pl.BlockSpec((1, D_MODEL), lambda i, j: (0, 0)),
                  pl.BlockSpec((D_MODEL, nb), lambda i, j: (0, j)),
                  pl.BlockSpec((tm, LANES), lambda i, j: (i, 0)), pl.BlockSpec((tm, LANES), lambda i, j: (i, 0))],
        out_specs=_dilated_specs(tm, nb, lambda j: j) + [pl.BlockSpec((tm, D_MODEL), lambda i, j: (i, 0))],
        out_shape=_dilated_shapes(t // SEQ, n, BF16) + [jax.ShapeDtypeStruct((t, D_MODEL), BF16)],
        scratch_shapes=[pltpu.VMEM((nb // LANES, tm, LANES), F32)],
        compiler_params=_params(("parallel", "arbitrary")),
    )(x, g, w, *rope)


def _outproj(parts, w, x, g, name):
    t = x.shape[0]
    tm = PROJ_TILE
    n = len(parts)
    widths = [p.shape[1] for p in parts]

    def body(*refs):
        p_refs = refs[:n]
        w_ref, x_ref, g_ref, xo_ref, mix_ref = refs[n:]
        mix = None
        off = 0
        for p_ref, wd in zip(p_refs, widths):
            term = _dot(p_ref[...].astype(BF16), w_ref[off:off + wd, :])
            mix = term if mix is None else mix + term
            off += wd
        mix_ref[...] = mix
        xo_ref[...] = x_ref[...] + _rms(mix, g_ref[...])

    row = lambda i: (i, 0)
    return pl.pallas_call(
        body, name=name, grid=(t // tm,),
        in_specs=[pl.BlockSpec((tm, wd), row) for wd in widths] + [
            pl.BlockSpec((sum(widths), D_MODEL), lambda i: (0, 0)),
            pl.BlockSpec((tm, D_MODEL), row), pl.BlockSpec((1, D_MODEL), lambda i: (0, 0))],
        out_specs=[pl.BlockSpec((tm, D_MODEL), row)] * 2,
        out_shape=[jax.ShapeDtypeStruct((t, D_MODEL), F32)] * 2,
        compiler_params=_params(("parallel",)),
    )(*parts, w, x, g)


def _outproj_bwd(dx, mix, g, w, name):
    t = dx.shape[0]
    tm = PROJ_TILE
    k = w.shape[0]

    def body(dx_ref, mix_ref, g_ref, w_ref, dcat_ref, dz_ref, dg_ref):
        dz, dgr = _rms_bwd(mix_ref[...], g_ref[...], dx_ref[...])
        dzb = dz.astype(BF16)
        dz_ref[...] = dzb
        dcat_ref[...] = _dot_nt(dzb, w_ref[...])
        _acc_rows8(dg_ref, _rows8(dgr), pl.program_id(0) == 0)

    row = lambda i: (i, 0)
    return pl.pallas_call(
        body, name=name, grid=(t // tm,),
        in_specs=[pl.BlockSpec((tm, D_MODEL), row), pl.BlockSpec((tm, D_MODEL), row),
                  pl.BlockSpec((1, D_MODEL), lambda i: (0, 0)), pl.BlockSpec((k, D_MODEL), lambda i: (0, 0))],
        out_specs=[pl.BlockSpec((tm, k), row), pl.BlockSpec((tm, D_MODEL), row),
                   pl.BlockSpec((SUBLANES, D_MODEL), lambda i: (0, 0))],
        out_shape=[jax.ShapeDtypeStruct((t, k), F32), jax.ShapeDtypeStruct((t, D_MODEL), BF16),
                   jax.ShapeDtypeStruct((SUBLANES, D_MODEL), F32)],
        compiler_params=_params(("arbitrary",)),
    )(dx, mix, g, w)


def _outproj_bwd_attn(dx, mix, g, w, out, name):
    t = dx.shape[0]
    tm = MERGE_TILE

    def body(dx_ref, mix_ref, g_ref, w_ref, out_ref, do1, do4, do16, dl_ref, dz_ref, dg_ref, tile_ref):
        dz, dgr = _rms_bwd(mix_ref[...], g_ref[...], dx_ref[...])
        dzb = dz.astype(BF16)
        dz_ref[...] = dzb
        _acc_rows8(dg_ref, _rows8(dgr), pl.program_id(0) == 0)
        dout = _dot_nt(dzb, w_ref[...])
        for p in range(LANE_GROUPS):
            tile_ref[p] = dout[:, p * LANES:(p + 1) * LANES]
        _store_dilated(tile_ref, (do1, do4, do16), BF16)
        column = lax.broadcasted_iota(jnp.int32, (D_MODEL, LANES), 0) // C_HEAD_DIM
        head = lax.broadcasted_iota(jnp.int32, (D_MODEL, LANES), 1)
        dl_ref[...] = jnp.dot(dout * out_ref[...], (column == head).astype(F32), precision=lax.Precision.HIGHEST,
                              preferred_element_type=F32)

    row = lambda i: (i, 0)
    n_seq = t // SEQ
    return pl.pallas_call(
        body, name=name, grid=(t // tm,),
        in_specs=[pl.BlockSpec((tm, D_MODEL), row), pl.BlockSpec((tm, D_MODEL), row),
                  pl.BlockSpec((1, D_MODEL), lambda i: (0, 0)), pl.BlockSpec((D_MODEL, D_MODEL), lambda i: (0, 0)),
                  pl.BlockSpec((tm, D_MODEL), row)],
        out_specs=_dilated_specs(tm, D_MODEL, lambda: 0) + [
            pl.BlockSpec((tm, LANES), row), pl.BlockSpec((tm, D_MODEL), row),
            pl.BlockSpec((SUBLANES, D_MODEL), lambda i: (0, 0))],
        out_shape=_dilated_shapes(n_seq, D_MODEL, BF16) + [
            jax.ShapeDtypeStruct((t, LANES), F32), jax.ShapeDtypeStruct((t, D_MODEL), BF16),
            jax.ShapeDtypeStruct((SUBLANES, D_MODEL), F32)],
        scratch_shapes=[pltpu.VMEM((LANE_GROUPS, tm, LANES), F32)],
        compiler_params=_params(("arbitrary",)),
    )(dx, mix, g, w, out)


def _inproj_bwd(dproj, w, dx, x, g, name):
    t = x.shape[0]
    n = w.shape[1]
    tm = ROW_TILE

    def body(dp_ref, w_ref, dx_ref, x_ref, g_ref, o_ref, dg_ref):
        dxn, dgr = _rms_bwd(x_ref[...], g_ref[...], _dot_nt(dp_ref[...], w_ref[...]))
        o_ref[...] = dx_ref[...] + dxn
        _acc_rows8(dg_ref, _rows8(dgr), pl.program_id(0) == 0)

    row = lambda i: (i, 0)
    return pl.pallas_call(
        body, name=name, grid=(t // tm,),
        in_specs=[pl.BlockSpec((tm, n), row), pl.BlockSpec((D_MODEL, n), lambda i: (0, 0)),
                  pl.BlockSpec((tm, D_MODEL), row), pl.BlockSpec((tm, D_MODEL), row),
                  pl.BlockSpec((1, D_MODEL), lambda i: (0, 0))],
        out_specs=[pl.BlockSpec((tm, D_MODEL), row), pl.BlockSpec((SUBLANES, D_MODEL), lambda i: (0, 0))],
        out_shape=[jax.ShapeDtypeStruct((t, D_MODEL), F32), jax.ShapeDtypeStruct((SUBLANES, D_MODEL), F32)],
        compiler_params=_params(("arbitrary",)),
    )(dproj, w, dx, x, g)


def _grad_w(a, b, col_blocks, name):
    t, k = a.shape
    n = b.shape[1]
    tk = min(k, 1024)
    per_owner = n // N_DEV
    tn = 2 * per_owner if col_blocks else min(n, 1024)

    def body(a_ref, b_ref, o_ref, at_ref):
        @pl.when(pl.program_id(1) == 0)
        def _():
            for c in range(t // ROW_TILE):
                rows = slice(c * ROW_TILE, (c + 1) * ROW_TILE)
                at_ref[:, rows] = a_ref[rows, :].T

        res = _dot(at_ref[...], b_ref[...]).astype(BF16)
        if col_blocks:
            o_ref[0] = res[:, :per_owner]
            o_ref[1] = res[:, per_owner:]
        else:
            o_ref[...] = res

    if col_blocks:
        out_spec = pl.BlockSpec((2, tk, per_owner), lambda i, j: (j, i, 0))
        out_shape = jax.ShapeDtypeStruct((N_DEV, k, per_owner), BF16)
    else:
        out_spec = pl.BlockSpec((tk, tn), lambda i, j: (i, j))
        out_shape = jax.ShapeDtypeStruct((k, n), BF16)
    return pl.pallas_call(
        body, name=name, grid=(k // tk, n // tn),
        in_specs=[pl.BlockSpec((t, tk), lambda i, j: (0, i)), pl.BlockSpec((t, tn), lambda i, j: (0, j))],
        out_specs=out_spec, out_shape=out_shape,
        scratch_shapes=[pltpu.VMEM((tk, t), BF16)],
        compiler_params=_params(("parallel", "arbitrary")),
    )(a, b)


FF_BLOCK = D_FF // N_DEV
FF_STEP = 1024
FF_STEPS = D_FF // FF_STEP


def _ffn_fwd(x, g_pre, w1, w2, g_post, name):
    t = x.shape[0]
    tm = PROJ_TILE

    def body(x_ref, gp_ref, w1_ref, w2_ref, gq_ref, xo_ref, y_ref, h_ref, r_ref):
        j = pl.program_id(1)

        @pl.when(j == 0)
        def _():
            h_ref[...] = _rms(x_ref[...], gp_ref[...]).astype(BF16)

        a = _dot(h_ref[...], w1_ref[...])
        r = jnp.square(jnp.maximum(a, 0.0)).astype(BF16)
        r_ref[...] = r
        term = _dot(r, w2_ref[...])

        @pl.when(j == 0)
        def _():
            y_ref[...] = term

        @pl.when(j > 0)
        def _():
            y_ref[...] += term

        @pl.when(j == FF_STEPS - 1)
        def _():
            xo_ref[...] = x_ref[...] + _rms(y_ref[...], gq_ref[...])

    row = lambda i, j: (i, 0)
    vec = pl.BlockSpec((1, D_MODEL), lambda i, j: (0, 0))
    return pl.pallas_call(
        body, name=name, grid=(t // tm, FF_STEPS),
        in_specs=[pl.BlockSpec((tm, D_MODEL), row), vec,
                  pl.BlockSpec((D_MODEL, FF_STEP), lambda i, j: (0, j)),
                  pl.BlockSpec((FF_STEP, D_MODEL), lambda i, j: (j, 0)), vec],
        out_specs=[pl.BlockSpec((tm, D_MODEL), row)] * 3 + [pl.BlockSpec((tm, FF_STEP), lambda i, j: (i, j))],
        out_shape=[jax.ShapeDtypeStruct((t, D_MODEL), F32), jax.ShapeDtypeStruct((t, D_MODEL), F32),
                   jax.ShapeDtypeStruct((t, D_MODEL), BF16), jax.ShapeDtypeStruct((t, D_FF), BF16)],
        compiler_params=_params(("parallel", "arbitrary")),
    )(x, g_pre, w1, w2, g_post)


def _ffn_bwd(dxo, x, y, r, g_pre, w1, w2, g_post, name):
    t = x.shape[0]
    tm = ROW_TILE

    def body(dxo_ref, x_ref, y_ref, r_ref, gp_ref, w1_ref, w2_ref, gq_ref,
             dx_ref, dy_ref, da_ref, dgp_ref, dgq_ref, acc_ref):
        i, j = pl.program_id(0), pl.program_id(1)

        @pl.when(j == 0)
        def _():
            dy, dgr = _rms_bwd(y_ref[...], gq_ref[...], dxo_ref[...])
            dy_ref[...] = dy.astype(BF16)
            _acc_rows8(dgq_ref, _rows8(dgr), i == 0)

        dr = _dot_nt(dy_ref[...], w2_ref[...])
        da = (dr * (2.0 * jnp.sqrt(r_ref[...].astype(F32)))).astype(BF16)
        da_ref[...] = da
        term = _dot_nt(da, w1_ref[...])

        @pl.when(j == 0)
        def _():
            acc_ref[...] = term

        @pl.when(j > 0)
        def _():
            acc_ref[...] += term

        @pl.when(j == FF_STEPS - 1)
        def _():
            dxn, dgr = _rms_bwd(x_ref[...], gp_ref[...], acc_ref[...])
            dx_ref[...] = dxo_ref[...] + dxn
            _acc_rows8(dgp_ref, _rows8(dgr), i == 0)

    row = lambda i, j: (i, 0)
    vec = pl.BlockSpec((1, D_MODEL), lambda i, j: (0, 0))
    acc8 = pl.BlockSpec((SUBLANES, D_MODEL), lambda i, j: (0, 0))
    return pl.pallas_call(
        body, name=name, grid=(t // tm, FF_STEPS),
        in_specs=[pl.BlockSpec((tm, D_MODEL), row)] * 3 + [
            pl.BlockSpec((tm, FF_STEP), lambda i, j: (i, j)),
            vec, pl.BlockSpec((D_MODEL, FF_STEP), lambda i, j: (0, j)),
            pl.BlockSpec((FF_STEP, D_MODEL), lambda i, j: (j, 0)), vec],
        out_specs=[pl.BlockSpec((tm, D_MODEL), row), pl.BlockSpec((tm, D_MODEL), row),
                   pl.BlockSpec((tm, FF_STEP), lambda i, j: (i, j)), acc8, acc8],
        out_shape=[jax.ShapeDtypeStruct((t, D_MODEL), F32), jax.ShapeDtypeStruct((t, D_MODEL), BF16),
                   jax.ShapeDtypeStruct((t, D_FF), BF16),
                   jax.ShapeDtypeStruct((SUBLANES, D_MODEL), F32), jax.ShapeDtypeStruct((SUBLANES, D_MODEL), F32)],
        scratch_shapes=[pltpu.VMEM((tm, D_MODEL), F32)],
        compiler_params=_params(("arbitrary", "arbitrary")),
    )(dxo, x, y, r, g_pre, w1, w2, g_post)


def _lower_bound(table):
    e = jnp.exp(table - jnp.max(table, axis=0, keepdims=True))
    return e[0:1, :] / jnp.sum(e, axis=0, keepdims=True)


def _hgrn2_block(q_ref, f_ref, lb):
    tb = f_ref.shape[0]
    sig = _sigmoid(f_ref[...])
    f = lb + (1.0 - lb) * sig
    qraw = q_ref[...]
    sq = _sigmoid(qraw)
    r = lax.broadcasted_iota(jnp.int32, (tb, tb), 0)
    c = lax.broadcasted_iota(jnp.int32, (tb, tb), 1)
    same = (r // SUB_CHUNK) == (c // SUB_CHUNK)
    logf = jnp.log(f)
    gsum = jnp.dot((same & (c <= r)).astype(F32), logf, precision=lax.Precision.HIGHEST, preferred_element_type=F32)
    glast = jnp.dot(same.astype(F32), logf, precision=lax.Precision.HIGHEST, preferred_element_type=F32)
    return dict(sig=sig, f=f, kk=1.0 - f, qraw=qraw, sq=sq, qs=qraw * sq, gsum=gsum,
                eg=jnp.exp(gsum), ekd=jnp.exp(glast - gsum), a=jnp.exp(glast))


def _head_sums(x):
    parts = [jnp.broadcast_to(jnp.sum(x[:, h * HEAD_A:(h + 1) * HEAD_A], axis=1, keepdims=True), (x.shape[0], HEAD_A))
             for h in range(A_HEADS)]
    return jnp.concatenate(parts, axis=1)


def _lag_decay(g, j, row):
    back = jnp.exp(jnp.where(row >= j, g - pltpu.roll(g, j, 0), NEG))
    ahead = jnp.exp(jnp.where(row < SUB_CHUNK - j, pltpu.roll(g, SUB_CHUNK - j, 0) - g, NEG))
    return back, ahead


def _hgrn2_intra(g, kk, qs, v):
    row = lax.broadcasted_iota(jnp.int32, g.shape, 0)
    o = _head_sums(qs * kk) * v
    for j in range(1, SUB_CHUNK):
        decay = jnp.exp(jnp.where(row >= j, g - pltpu.roll(g, j, 0), NEG))
        o = o + _head_sums(qs * pltpu.roll(kk, j, 0) * decay) * pltpu.roll(v, j, 0)
    return o


def _hgrn2_intra_bwd(g, kk, qs, v, do):
    row = lax.broadcasted_iota(jnp.int32, g.shape, 0)
    dsc = _head_sums(do * v)
    dqs, dkk, dv = dsc * kk, dsc * qs, _head_sums(qs * kk) * do
    for j in range(1, SUB_CHUNK):
        back, ahead = _lag_decay(g, j, row)
        dqs = dqs + _head_sums(do * pltpu.roll(v, j, 0)) * pltpu.roll(kk, j, 0) * back
        q_up, do_up = pltpu.roll(qs, SUB_CHUNK - j, 0), pltpu.roll(do, SUB_CHUNK - j, 0)
        dkk = dkk + _head_sums(do_up * v) * q_up * ahead
        dv = dv + _head_sums(q_up * kk * ahead) * do_up
    return dqs, dkk, dv


def _hgrn2_fwd(proj, lb_table, a_norm, name):
    t = proj.shape[0]
    tb = HGRN_BLOCK
    n_tb = SEQ // tb
    n_seq = t // SEQ
    n_sub = tb // SUB_CHUNK

    def body(q_ref, f_ref, i_ref, g_ref, lbt_ref, an_ref, o_ref, pre_ref, sts_ref, st_ref,
             gs_ref, kk_ref, qs_ref, eg_ref, ekd_ref, a_ref):
        @pl.when(pl.program_id(1) == 0)
        def _():
            st_ref[...] = jnp.zeros_like(st_ref)

        an = an_ref[...]
        blk = _hgrn2_block(q_ref, f_ref, _lower_bound(lbt_ref[...]))
        for ref, key in ((gs_ref, "gsum"), (kk_ref, "kk"), (qs_ref, "qs"), (eg_ref, "eg"), (ekd_ref, "ekd"), (a_ref, "a")):
            ref[...] = blk[key]

        def step(c, carry):
            rows = pl.ds(pl.multiple_of(c * SUB_CHUNK, SUB_CHUNK), SUB_CHUNK)
            kk, qs, v = kk_ref[rows, :], qs_ref[rows, :], i_ref[rows, :]
            o = _hgrn2_intra(gs_ref[rows, :], kk, qs, v)
            qg, kd, vb = (qs * eg_ref[rows, :]).astype(BF16), (kk * ekd_ref[rows, :]).astype(BF16), v.astype(BF16)
            for h in range(A_HEADS):
                lanes = slice(h * HEAD_A, (h + 1) * HEAD_A)
                st = st_ref[h]
                sts_ref[0, c, h] = st
                o_h = o[:, lanes] + _dot_nt(qg[:, lanes], st.astype(BF16))
                st_ref[h] = st * a_ref[rows, lanes][0:1] + _dot_tn(vb[:, lanes], kd[:, lanes])
                pre_ref[rows, lanes] = o_h
                graw = g_ref[rows, lanes]
                o_ref[rows, lanes] = (_rms(o_h, an[:, lanes]) * (graw * _sigmoid(graw))).astype(BF16)
            return carry

        lax.fori_loop(0, n_sub, step, 0, unroll=2)

    def col(k):
        return pl.BlockSpec((tb, A_WIDTH), lambda b, s, k=k: (b * n_tb + s, k))

    out_rows = pl.BlockSpec((tb, A_WIDTH), lambda b, s: (b * n_tb + s, 0))
    return pl.pallas_call(
        body, name=name, grid=(n_seq, n_tb),
        in_specs=[col(0), col(1), col(2), col(3),
                  pl.BlockSpec((3, A_WIDTH), lambda b, s: (0, 0)), pl.BlockSpec((1, A_WIDTH), lambda b, s: (0, 0))],
        out_specs=[out_rows, out_rows,
                   pl.BlockSpec((1, n_sub, A_HEADS, HEAD_A, HEAD_A), lambda b, s: (b * n_tb + s, 0, 0, 0, 0))],
        out_shape=[jax.ShapeDtypeStruct((t, A_WIDTH), BF16), jax.ShapeDtypeStruct((t, A_WIDTH), F32),
                   jax.ShapeDtypeStruct((n_seq * n_tb, n_sub, A_HEADS, HEAD_A, HEAD_A), F32)],
        scratch_shapes=[pltpu.VMEM((A_HEADS, HEAD_A, HEAD_A), F32)] + [pltpu.VMEM((tb, A_WIDTH), F32)] * 6,
        compiler_params=_params(("parallel", "arbitrary")),
    )(proj, proj, proj, proj, lb_table, a_norm)


def _hgrn2_bwd(proj, dcat, pre, states, lb_table, a_norm, name):
    t = proj.shape[0]
    tb = HGRN_BLOCK
    n_tb = SEQ // tb
    n_seq = t // SEQ
    n_sub = tb // SUB_CHUNK

    def body(q_ref, f_ref, i_ref, g_ref, do_ref, pre_ref, sts_ref, lbt_ref, an_ref, dp_ref, dlb_ref, dan_ref, dst_ref,
             gs_ref, kk_ref, qs_ref, eg_ref, ekd_ref, a_ref, dpre_ref, dlf_ref, dqs_ref, dkk_ref):
        b, s = pl.program_id(0), pl.program_id(1)

        @pl.when(s == 0)
        def _():
            dst_ref[...] = jnp.zeros_like(dst_ref)

        @pl.when((b == 0) & (s == 0))
        def _():
            dlb_ref[...] = jnp.zeros_like(dlb_ref)
            dan_ref[...] = jnp.zeros_like(dan_ref)

        lb = _lower_bound(lbt_ref[...])
        an = an_ref[...]
        heads = [slice(h * HEAD_A, (h + 1) * HEAD_A) for h in range(A_HEADS)]
        blk = _hgrn2_block(q_ref, f_ref, lb)
        for ref, key in ((gs_ref, "gsum"), (kk_ref, "kk"), (qs_ref, "qs"), (eg_ref, "eg"), (ekd_ref, "ekd"), (a_ref, "a")):
            ref[...] = blk[key]
        for h, lanes in enumerate(heads):
            graw, o = g_ref[:, lanes], pre_ref[:, lanes]
            sg = _sigmoid(graw)
            dout = do_ref[:, lanes]
            d_o, dgr = _rms_bwd(o, an[:, lanes], dout * (graw * sg))
            dan_ref[0:1, lanes] += jnp.sum(dgr, axis=0, keepdims=True)
            dp_ref[:, 3 * A_WIDTH + h * HEAD_A:3 * A_WIDTH + (h + 1) * HEAD_A] = (
                dout * _rms(o, an[:, lanes]) * (sg * (1.0 + graw * (1.0 - sg)))).astype(BF16)
            dpre_ref[:, lanes] = d_o

        tri_t = (lax.broadcasted_iota(jnp.int32, (SUB_CHUNK, SUB_CHUNK), 0)
                 <= lax.broadcasted_iota(jnp.int32, (SUB_CHUNK, SUB_CHUNK), 1)).astype(F32)

        def back(k, carry):
            c = n_sub - 1 - k
            rows = pl.ds(pl.multiple_of(c * SUB_CHUNK, SUB_CHUNK), SUB_CHUNK)
            g, kk, qs, v, d_o = gs_ref[rows, :], kk_ref[rows, :], qs_ref[rows, :], i_ref[rows, :], dpre_ref[rows, :]
            eg, ekd, a = eg_ref[rows, :], ekd_ref[rows, :], a_ref[rows, :]
            dqs, dkk, dv = _hgrn2_intra_bwd(g, kk, qs, v, d_o)
            qg_f, kd_f = qs * eg, kk * ekd
            qg, kd, vb, dob = qg_f.astype(BF16), kd_f.astype(BF16), v.astype(BF16), d_o.astype(BF16)
            dqg, dkd, da, dv_st = [], [], [], []
            for h, lanes in enumerate(heads):
                st, dst = sts_ref[0, c, h], dst_ref[h]
                dstb = dst.astype(BF16)
                dqg.append(_dot(dob[:, lanes], st.astype(BF16)))
                dv_st.append(_dot_nt(kd[:, lanes], dstb))
                dkd.append(_dot(vb[:, lanes], dstb))
                da.append(jnp.broadcast_to(jnp.sum(dst * st, axis=0, keepdims=True), (SUB_CHUNK, HEAD_A)))
                dst_ref[h] = dst * a[0:1, lanes] + _dot_tn(dob[:, lanes], qg[:, lanes])
            dqg, dkd, da, dv_st = [jnp.concatenate(p, axis=1) for p in (dqg, dkd, da, dv_st)]
            d_gsum = qs * dqs - kk * dkk + dqg * qg_f - dkd * kd_f
            d_glast = jnp.sum(dkd * kd_f, axis=0, keepdims=True) + da * a
            dlf_ref[rows, :] = jnp.dot(tri_t, d_gsum, precision=lax.Precision.HIGHEST,
                                       preferred_element_type=F32) + d_glast
            dqs_ref[rows, :] = dqs + dqg * eg
            dkk_ref[rows, :] = dkk + dkd * ekd
            dp_ref[rows, 2 * A_WIDTH:3 * A_WIDTH] = (dv + dv_st).astype(BF16)
            return carry

        lax.fori_loop(0, n_sub, back, 0, unroll=2)
        sig, sq, qraw = blk["sig"], blk["sq"], blk["qraw"]
        d_f = dlf_ref[...] / blk["f"] - dkk_ref[...]
        dlb_ref[0:1, :] += jnp.sum(d_f * (1.0 - sig), axis=0, keepdims=True)
        dp_ref[:, 0:A_WIDTH] = (dqs_ref[...] * (sq * (1.0 + qraw * (1.0 - sq)))).astype(BF16)
        dp_ref[:, A_WIDTH:2 * A_WIDTH] = (d_f * (1.0 - lb) * sig * (1.0 - sig)).astype(BF16)

    def rev(s):
        return n_tb - 1 - s

    def col(k):
        return pl.BlockSpec((tb, A_WIDTH), lambda b, s, k=k: (b * n_tb + rev(s), k))

    acc8 = pl.BlockSpec((SUBLANES, A_WIDTH), lambda b, s: (0, 0))
    return pl.pallas_call(
        body, name=name, grid=(n_seq, n_tb),
        in_specs=[col(0), col(1), col(2), col(3), col(0), col(0),
                  pl.BlockSpec((1, n_sub, A_HEADS, HEAD_A, HEAD_A), lambda b, s: (b * n_tb + rev(s), 0, 0, 0, 0)),
                  pl.BlockSpec((3, A_WIDTH), lambda b, s: (0, 0)), pl.BlockSpec((1, A_WIDTH), lambda b, s: (0, 0))],
        out_specs=[pl.BlockSpec((tb, 4 * A_WIDTH), lambda b, s: (b * n_tb + rev(s), 0)), acc8, acc8],
        out_shape=[jax.ShapeDtypeStruct((t, 4 * A_WIDTH), BF16)] + [jax.ShapeDtypeStruct((SUBLANES, A_WIDTH), F32)] * 2,
        scratch_shapes=[pltpu.VMEM((A_HEADS, HEAD_A, HEAD_A), F32)] + [pltpu.VMEM((tb, A_WIDTH), F32)] * 10,
        compiler_params=_params(("arbitrary", "arbitrary")),
    )(proj, proj, proj, proj, dcat, pre, states, lb_table, a_norm)


GMLP_ROWS = 512


def _gmlp_chunk(ub, vb, ln_g, ln_b, ws, bias):
    u = [_gelu(a) for a in ub]
    v = [_gelu(a) for a in vb]
    mu = sum(jnp.sum(a, axis=-1, keepdims=True) for a in v) * (1.0 / B_WIDTH)
    cen = [a - mu for a in v]
    var = sum(jnp.sum(a * a, axis=-1, keepdims=True) for a in cen) * (1.0 / B_WIDTH)
    inv = lax.rsqrt(var + EPS)
    r = lax.broadcasted_iota(jnp.int32, (B_CHUNK, B_CHUNK), 0)
    c = lax.broadcasted_iota(jnp.int32, (B_CHUNK, B_CHUNK), 1)
    outs = []
    for g in range(B_GROUPS):
        vn = (cen[g] * inv * ln_g[g] + ln_b[g]).astype(BF16)
        wm = jnp.where(c <= r, ws[g], 0.0).astype(BF16)
        outs.append(u[g] * (_dot(wm, vn) + bias[g]))
    return outs


def _lane_groups(ref, rows=slice(None)):
    return [ref[rows, g * LANES:(g + 1) * LANES] for g in range(B_GROUPS)]


def _gmlp_fwd(proj, ln_g, ln_b, ws, bias_t, name):
    t = proj.shape[0]
    tm = GMLP_ROWS

    def body(u_ref, v_ref, lg_ref, lb_ref, ws_ref, bt_ref, o_ref):
        for ch in range(tm // B_CHUNK):
            rows = slice(ch * B_CHUNK, (ch + 1) * B_CHUNK)
            outs = _gmlp_chunk(_lane_groups(u_ref, rows), _lane_groups(v_ref, rows), _lane_groups(lg_ref),
                               _lane_groups(lb_ref), [ws_ref[g] for g in range(B_GROUPS)],
                               [bt_ref[:, g:g + 1] for g in range(B_GROUPS)])
            for g in range(B_GROUPS):
                o_ref[rows, g * LANES:(g + 1) * LANES] = outs[g].astype(BF16)

    vec = pl.BlockSpec((1, B_WIDTH), lambda i: (0, 0))
    return pl.pallas_call(
        body, name=name, grid=(t // tm,),
        in_specs=[pl.BlockSpec((tm, B_WIDTH), lambda i: (i, 4)), pl.BlockSpec((tm, B_WIDTH), lambda i: (i, 5)), vec, vec,
                  pl.BlockSpec((B_GROUPS, B_CHUNK, B_CHUNK), lambda i: (0, 0, 0)),
                  pl.BlockSpec((B_CHUNK, B_GROUPS), lambda i: (0, 0))],
        out_specs=pl.BlockSpec((tm, B_WIDTH), lambda i: (i, 0)),
        out_shape=jax.ShapeDtypeStruct((t, B_WIDTH), BF16),
        compiler_params=_params(("parallel",)),
    )(proj, proj, ln_g, ln_b, ws, bias_t)


def _gmlp_bwd(proj, dcat, ln_g, ln_b, ws, bias_t, name):
    t = proj.shape[0]
    tm = GMLP_ROWS

    def body(u_ref, v_ref, do_ref, lg_ref, lb_ref, ws_ref, bt_ref, duv_ref, dlg_ref, dlb_ref, dws_ref, dbt_ref):
        @pl.when(pl.program_id(0) == 0)
        def _():
            dlg_ref[...] = jnp.zeros_like(dlg_ref)
            dlb_ref[...] = jnp.zeros_like(dlb_ref)
            dws_ref[...] = jnp.zeros_like(dws_ref)
            dbt_ref[...] = jnp.zeros_like(dbt_ref)

        for ch in range(tm // B_CHUNK):
            rows = slice(ch * B_CHUNK, (ch + 1) * B_CHUNK)
            _, vjp = jax.vjp(
                _gmlp_chunk, _lane_groups(u_ref, rows), _lane_groups(v_ref, rows), _lane_groups(lg_ref),
                _lane_groups(lb_ref), [ws_ref[g] for g in range(B_GROUPS)],
                [bt_ref[:, g:g + 1] for g in range(B_GROUPS)])
            du, dv, dlg, dlb, dw, dbt = vjp(_lane_groups(do_ref, rows))
            for g in range(B_GROUPS):
                lanes = slice(g * LANES, (g + 1) * LANES)
                duv_ref[rows, lanes] = du[g].astype(BF16)
                duv_ref[rows, B_WIDTH + g * LANES:B_WIDTH + (g + 1) * LANES] = dv[g].astype(BF16)
                dlg_ref[0:1, lanes] += dlg[g]
                dlb_ref[0:1, lanes] += dlb[g]
                dws_ref[g] += dw[g]
                dbt_ref[:, g:g + 1] += dbt[g]

    vec = pl.BlockSpec((1, B_WIDTH), lambda i: (0, 0))
    acc8 = pl.BlockSpec((SUBLANES, B_WIDTH), lambda i: (0, 0))
    ws_spec = pl.BlockSpec((B_GROUPS, B_CHUNK, B_CHUNK), lambda i: (0, 0, 0))
    bt_spec = pl.BlockSpec((B_CHUNK, B_GROUPS), lambda i: (0, 0))
    return pl.pallas_call(
        body, name=name, grid=(t // tm,),
        in_specs=[pl.BlockSpec((tm, B_WIDTH), lambda i: (i, 4)), pl.BlockSpec((tm, B_WIDTH), lambda i: (i, 5)),
                  pl.BlockSpec((tm, B_WIDTH), lambda i: (i, 1)), vec, vec, ws_spec, bt_spec],
        out_specs=[pl.BlockSpec((tm, 2 * B_WIDTH), lambda i: (i, 0)), acc8, acc8, ws_spec, bt_spec],
        out_shape=[jax.ShapeDtypeStruct((t, 2 * B_WIDTH), BF16), jax.ShapeDtypeStruct((SUBLANES, B_WIDTH), F32),
                   jax.ShapeDtypeStruct((SUBLANES, B_WIDTH), F32),
                   jax.ShapeDtypeStruct((B_GROUPS, B_CHUNK, B_CHUNK), F32),
                   jax.ShapeDtypeStruct((B_CHUNK, B_GROUPS), F32)],
        compiler_params=_params(("arbitrary",)),
    )(proj, proj, dcat, ln_g, ln_b, ws, bias_t)


QK_SCALE = 1.0 / math.sqrt(C_HEAD_DIM)
ATTN_UNROLL = 8
LANE_GROUPS = D_MODEL // LANES
Q_BLOCKS = SEQ // C_BLOCK


def _attn_window(i, d):
    sub_blocks = Q_BLOCKS // d
    q0 = pl.multiple_of(i * C_BLOCK, C_BLOCK)
    k0 = pl.multiple_of(jnp.maximum(i - 1, 0) * C_BLOCK, C_BLOCK)
    key = k0 + lax.broadcasted_iota(jnp.int32, (C_BLOCK, 2 * C_BLOCK), 1)
    dist = (q0 + lax.broadcasted_iota(jnp.int32, (C_BLOCK, 2 * C_BLOCK), 0)) - key
    own_subsequence = (key >= q0) | (i % sub_blocks > 0)
    return pl.ds(q0, C_BLOCK), pl.ds(k0, 2 * C_BLOCK), (dist >= 0) & (dist <= C_BLOCK) & own_subsequence


def _head_masks():
    lane = lax.broadcasted_iota(jnp.int32, (C_BLOCK, LANES), 1)
    return [lane < C_HEAD_DIM, lane >= C_HEAD_DIM]


def _flat_spec(col_of):
    return pl.BlockSpec((1, SEQ, LANES), lambda b, g: (b, 0, col_of(g)))


def _put_heads(tile, g, col0, col1):
    lane = lax.broadcasted_iota(jnp.int32, tile.shape, 1)
    return jnp.where(lane == 2 * g, col0, jnp.where(lane == 2 * g + 1, col1, tile))


def _get_head(tile, h):
    lane = lax.broadcasted_iota(jnp.int32, tile.shape, 1)
    return jnp.sum(jnp.where(lane == h, tile, 0.0), axis=1, keepdims=True)


PER_HEAD_SPEC = pl.BlockSpec((1, SEQ, LANES), lambda b, g: (b, 0, 0))


def _attn_branch_fwd(qkv, name):
    n_seq, d, l, _ = qkv.shape
    flat = qkv.reshape(n_seq, SEQ, ODD_IN)

    def body(q_ref, k_ref, v_ref, o_ref, m_ref, l_ref):
        heads = _head_masks()
        g = pl.program_id(1)

        @pl.when(g == 0)
        def _():
            m_ref[...] = jnp.zeros_like(m_ref)
            l_ref[...] = jnp.zeros_like(l_ref)

        def block(i, carry):
            rows, keys, mask = _attn_window(i, d)
            q, k, v = q_ref[0, rows, :], k_ref[0, keys, :], v_ref[0, keys, :]
            res = []
            for hm in heads:
                s = jnp.where(mask, _dot_nt(jnp.where(hm, q, 0), k) * QK_SCALE, NEG)
                m = jnp.max(s, axis=-1, keepdims=True)
                p = jnp.exp(s - m)
                res.append((_dot(p.astype(BF16), v), m, jnp.sum(p, axis=-1, keepdims=True)))
            o_ref[0, rows, :] = jnp.where(heads[0], res[0][0], res[1][0])
            m_ref[0, rows, :] = _put_heads(m_ref[0, rows, :], g, res[0][1], res[1][1])
            l_ref[0, rows, :] = _put_heads(l_ref[0, rows, :], g, res[0][2], res[1][2])
            return carry

        lax.fori_loop(0, Q_BLOCKS, block, 0, unroll=ATTN_UNROLL)

    o, m, l_sum = pl.pallas_call(
        body, name=name, grid=(n_seq, LANE_GROUPS),
        in_specs=[_flat_spec(lambda g: g), _flat_spec(lambda g: LANE_GROUPS + g),
                  _flat_spec(lambda g: 2 * LANE_GROUPS + g)],
        out_specs=[_flat_spec(lambda g: g), PER_HEAD_SPEC, PER_HEAD_SPEC],
        out_shape=[jax.ShapeDtypeStruct((n_seq, SEQ, D_MODEL), F32)] + [jax.ShapeDtypeStruct((n_seq, SEQ, LANES), F32)] * 2,
        compiler_params=_params(("parallel", "arbitrary")),
    )(flat, flat, flat)
    return [o.reshape(n_seq, d, l, D_MODEL), m.reshape(n_seq, d, l, LANES), l_sum.reshape(n_seq, d, l, LANES)]


def _attn_merge(branches, name):
    n_seq = branches[0][0].shape[0]
    t = n_seq * SEQ
    tm = MERGE_TILE

    def body(*refs):
        ins = refs[:9]
        o_ref, ob_ref, lse_ref = refs[9:12]
        nat = refs[12:]
        for b, d in enumerate(C_DILATIONS[1:]):
            for k in range(3):
                _load_dilated(ins[3 + 3 * b + k], d, nat[3 * b + k])
        ms = [ins[1][0, 0], nat[1][0], nat[4][0]]
        ls = [ins[2][0, 0], nat[2][0], nat[5][0]]
        m_all = jnp.maximum(jnp.maximum(ms[0], ms[1]), ms[2])
        ws = [jnp.exp(ms[b] - m_all) for b in range(3)]
        lane = lax.broadcasted_iota(jnp.int32, m_all.shape, 1)
        total = jnp.where(lane < C_HEADS, ws[0] * ls[0] + ws[1] * ls[1] + ws[2] * ls[2], 1.0)
        lse_ref[...] = m_all + jnp.log(total)
        first_head = lane < C_HEAD_DIM
        for p in range(LANE_GROUPS):
            lanes = slice(p * LANES, (p + 1) * LANES)
            spread = lambda c: jnp.where(first_head, c[:, 2 * p:2 * p + 1], c[:, 2 * p + 1:2 * p + 2])
            os_ = [ins[0][0, 0, :, lanes], nat[0][p], nat[3][p]]
            o = (spread(ws[0]) * os_[0] + spread(ws[1]) * os_[1] + spread(ws[2]) * os_[2]) / spread(total)
            o_ref[:, lanes] = o
            ob_ref[:, lanes] = o.astype(BF16)

    row = pl.BlockSpec((tm, D_MODEL), lambda i: (i, 0))
    flat = [a for br in branches for a in br]
    in_specs = []
    for wide, narrow in zip(_dilated_specs(tm, D_MODEL, lambda: 0), _dilated_specs(tm, LANES, lambda: 0)):
        in_specs += [wide, narrow, narrow]
    per_head = pltpu.VMEM((1, tm, LANES), F32)
    return pl.pallas_call(
        body, name=name, grid=(t // tm,), in_specs=in_specs,
        out_specs=[row, row, pl.BlockSpec((tm, LANES), lambda i: (i, 0))],
        out_shape=[jax.ShapeDtypeStruct((t, D_MODEL), F32), jax.ShapeDtypeStruct((t, D_MODEL), BF16),
                   jax.ShapeDtypeStruct((t, LANES), F32)],
        scratch_shapes=[pltpu.VMEM((LANE_GROUPS, tm, LANES), F32), per_head, per_head] * 2,
        compiler_params=_params(("parallel",)),
    )(*flat)


def _attn_branch_bwd(qkv, dout, lse, delta, name):
    n_seq, d, l, _ = qkv.shape
    flat = lambda a: a.reshape(n_seq, SEQ, a.shape[-1])

    def body(q_ref, k_ref, v_ref, do_ref, lse_nat_ref, dl_nat_ref, dq_ref, dk_ref, dv_ref, lse_ref, dl_ref,
             dkt_ref, dvt_ref):
        heads = _head_masks()
        g = pl.program_id(1)
        dkt_ref[...] = jnp.zeros_like(dkt_ref)
        dvt_ref[...] = jnp.zeros_like(dvt_ref)
        for nat_ref, dst_ref in ((lse_nat_ref, lse_ref), (dl_nat_ref, dl_ref)):
            for r in range(d):
                rows = pl.ds(r, l, stride=d) if d > 1 else slice(None)
                dst_ref[r * l:(r + 1) * l, :] = nat_ref.at[0][rows, :]

        def block(i, carry):
            rows, keys, mask = _attn_window(i, d)
            q, do = q_ref[0, rows, :], do_ref[0, rows, :]
            k, v = k_ref[0, keys, :], v_ref[0, keys, :]
            lse_b, dl_b = lse_ref[rows, :], dl_ref[rows, :]
            dq, dk, dv = [], None, None
            for hh, hm in enumerate(heads):
                qh, doh = jnp.where(hm, q, 0), jnp.where(hm, do, 0)
                s = jnp.where(mask, _dot_nt(qh, k) * QK_SCALE, NEG)
                p = jnp.exp(s - _get_head(lse_b, 2 * g + hh))
                ds = (p * (_dot_nt(doh, v) - _get_head(dl_b, 2 * g + hh)) * QK_SCALE).astype(BF16)
                dq.append(_dot(ds, k))
                dk_h, dv_h = _dot_tn(qh, ds), _dot_tn(doh, p.astype(BF16))
                dk = dk_h if dk is None else dk + dk_h
                dv = dv_h if dv is None else dv + dv_h
            dq_ref[0, rows, :] = jnp.where(heads[0], dq[0], dq[1]).astype(BF16)
            dkt_ref[:, keys] += dk
            dvt_ref[:, keys] += dv
            return carry

        lax.fori_loop(0, Q_BLOCKS, block, 0, unroll=ATTN_UNROLL)
        for c in range(SEQ // ROW_TILE):
            rows = slice(c * ROW_TILE, (c + 1) * ROW_TILE)
            dk_ref[0, rows, :] = dkt_ref[:, rows].T.astype(BF16)
            dv_ref[0, rows, :] = dvt_ref[:, rows].T.astype(BF16)

    act = _flat_spec(lambda g: g)
    outs = pl.pallas_call(
        body, name=name, grid=(n_seq, LANE_GROUPS),
        in_specs=[_flat_spec(lambda g: g), _flat_spec(lambda g: LANE_GROUPS + g),
                  _flat_spec(lambda g: 2 * LANE_GROUPS + g), act, PER_HEAD_SPEC, PER_HEAD_SPEC],
        out_specs=[act] * 3,
        out_shape=[jax.ShapeDtypeStruct((n_seq, SEQ, D_MODEL), BF16)] * 3,
        scratch_shapes=[pltpu.VMEM((SEQ, LANES), F32)] * 2 + [pltpu.VMEM((LANES, SEQ), F32)] * 2,
        compiler_params=_params(("parallel", "parallel")),
    )(flat(qkv), flat(qkv), flat(qkv), flat(dout), lse, delta)
    return [o.reshape(n_seq, d, l, D_MODEL) for o in outs]


def _attn_combine_bwd(grads, rope, name):
    n_seq = grads[0][0].shape[0]
    t = n_seq * SEQ
    tm = MERGE_TILE

    def body(*refs):
        c_ref, s_ref, o_ref, nat4_ref, nat16_ref = refs[9:]
        for sec in range(3):
            _load_dilated(refs[3 + sec], 4, nat4_ref)
            _load_dilated(refs[6 + sec], 16, nat16_ref)
            for p in range(LANE_GROUPS):
                blk = refs[sec][0, 0, :, p * LANES:(p + 1) * LANES] + nat4_ref[p] + nat16_ref[p]
                if sec < 2:
                    blk = blk * c_ref[...] - _swap_halves(blk) * s_ref[...]
                o_ref[:, sec * D_MODEL + p * LANES:sec * D_MODEL + (p + 1) * LANES] = blk.astype(BF16)

    tab = pl.BlockSpec((tm, LANES), lambda i: (i, 0))
    flat = [a for br in grads for a in br]
    in_specs = []
    for spec in _dilated_specs(tm, D_MODEL, lambda: 0):
        in_specs += [spec] * 3
    return pl.pallas_call(
        body, name=name, grid=(t // tm,), in_specs=in_specs + [tab, tab],
        out_specs=pl.BlockSpec((tm, ODD_IN), lambda i: (i, 0)),
        out_shape=jax.ShapeDtypeStruct((t, ODD_IN), BF16),
        scratch_shapes=[pltpu.VMEM((LANE_GROUPS, tm, LANES), F32)] * 2,
        compiler_params=_params(("parallel",)),
    )(*flat, *rope)


def _loss_grad(y, target, name):
    t = y.shape[0]
    tm = ROW_TILE

    def body(y_ref, t_ref, d_ref, l_ref):
        diff = y_ref[...] - t_ref[...]
        d_ref[...] = diff * (1.0 / D_MODEL)
        _acc_rows8(l_ref, _rows8(diff * diff) * (0.5 / D_MODEL), pl.program_id(0) == 0)

    row = pl.BlockSpec((tm, D_MODEL), lambda i: (i, 0))
    return pl.pallas_call(
        body, name=name, grid=(t // tm,), in_specs=[row, row],
        out_specs=[row, pl.BlockSpec((SUBLANES, D_MODEL), lambda i: (0, 0))],
        out_shape=[jax.ShapeDtypeStruct((t, D_MODEL), F32), jax.ShapeDtypeStruct((SUBLANES, D_MODEL), F32)],
        compiler_params=_params(("arbitrary",)),
    )(y, target)


def _adamw(w, g, m, v):
    m = ADAM_B1 * m + (1.0 - ADAM_B1) * g
    v = ADAM_B2 * v + (1.0 - ADAM_B2) * jnp.square(g)
    m_hat = m / (1.0 - ADAM_B1 ** ADAM_STEP)
    v_hat = v / (1.0 - ADAM_B2 ** ADAM_STEP)
    delta = -ADAM_LR * (m_hat / (jnp.sqrt(v_hat) + ADAM_EPS) + ADAM_WD * w)
    return delta, m, v


def _adamw_sharded(parts, w, m, v, after, name):
    n_layers, rows, cols = w.shape
    tr = min(rows, 256)

    def body(*refs):
        p_refs = refs[:n_layers]
        w_ref, m_ref, v_ref, _, g_ref, d_ref, mo_ref, vo_ref = refs[n_layers:]
        layer = pl.program_id(0)
        g = None
        for l, p_ref in enumerate(p_refs):
            g_l = p_ref[0].astype(F32)
            for s in range(1, N_DEV):
                g_l = g_l + p_ref[s].astype(F32)
            g = g_l if g is None else jnp.where(layer == l, g_l, g)
        delta, mn, vn = _adamw(w_ref[0], g, m_ref[0], v_ref[0])
        g_ref[0] = g
        d_ref[0] = delta
        mo_ref[0] = mn
        vo_ref[0] = vn

    def part_spec(l):
        return pl.BlockSpec((N_DEV, tr, cols), lambda a, i: (0, jnp.where(a == l, i, 0), 0))

    row = pl.BlockSpec((1, tr, cols), lambda a, i: (a, i, 0))
    return pl.pallas_call(
        body, name=name, grid=(n_layers, rows // tr),
        in_specs=[part_spec(l) for l in range(n_layers)] + [row, row, row, pl.BlockSpec(memory_space=pl.ANY)],
        out_specs=[row] * 4, out_shape=[jax.ShapeDtypeStruct(w.shape, F32)] * 4,
        compiler_params=_params(("arbitrary", "arbitrary")),
    )(*parts, w, m, v, after)


def _small_update(gathered, where, weights, moments_m, moments_v, lb_index, name):
    n = len(weights)
    n_g = len(gathered)

    def body(*refs):
        g_refs = refs[:n_g]
        w_refs, m_refs, v_refs = refs[n_g:n_g + n], refs[n_g + n:n_g + 2 * n], refs[n_g + 2 * n:n_g + 3 * n]
        outs = refs[n_g + 3 * n:]

        def total(k):
            array, rows, lanes = where[k]
            ref = g_refs[array]
            index = (slice(None),) * (len(ref.shape) - 1) if rows is None else (rows, lanes)
            acc = ref[(0,) + index]
            for s in range(1, N_DEV):
                acc = acc + ref[(s,) + index]
            return acc

        loss_rows = total(n)
        outs[0][...] = jnp.sum(jnp.sum(loss_rows, axis=1, keepdims=True), axis=0, keepdims=True)
        for k in range(n):
            part = total(k)
            if k == lb_index:
                dlb = jnp.sum(part, axis=0, keepdims=True)
                tab = w_refs[k][...]
                e = jnp.exp(tab - jnp.max(tab, axis=0, keepdims=True))
                p = e / jnp.sum(e, axis=0, keepdims=True)
                first = lax.broadcasted_iota(jnp.int32, p.shape, 0) == 0
                grads = [(slice(None), p * (jnp.where(first, dlb, 0.0) - p[0:1, :] * dlb))]
            elif part.shape == w_refs[k].shape:
                grads = [(slice(None), part)]
            else:
                grads = [(slice(l, l + 1), jnp.sum(part[l * SUBLANES:(l + 1) * SUBLANES], axis=0, keepdims=True))
                         for l in range(w_refs[k].shape[0])]
            for rows, g in grads:
                delta, mn, vn = _adamw(w_refs[k][rows], g, m_refs[k][rows], v_refs[k][rows])
                outs[1 + 4 * k][rows] = g
                outs[2 + 4 * k][rows] = delta
                outs[3 + 4 * k][rows] = mn
                outs[4 + 4 * k][rows] = vn

    vmem = pl.BlockSpec(memory_space=pltpu.VMEM)
    out_shape = [jax.ShapeDtypeStruct((1, 1), F32)]
    for w in weights:
        out_shape += [jax.ShapeDtypeStruct(w.shape, F32)] * 4
    args = list(gathered) + list(weights) + list(moments_m) + list(moments_v)
    return pl.pallas_call(
        body, name=name, in_specs=[vmem] * len(args), out_specs=[vmem] * len(out_shape), out_shape=out_shape,
        compiler_params=pltpu.CompilerParams(vmem_limit_bytes=VMEM_LIMIT),
    )(*args)


def kernel(x, positions, norm_mix_pre, norm_mix_post, norm_ffn_pre, norm_ffn_post, w_in_even, lb_table, a_norm, b_ln_g, b_ln_b, b_ws, b_bias, w_out_even, w_in_odd, w_out_odd, w_ff1, w_ff2, loss_target, m_norm_mix_pre, m_norm_mix_post, m_norm_ffn_pre, m_norm_ffn_post, m_w_in_even, m_lb_table, m_a_norm, m_b_ln_g, m_b_ln_b, m_b_ws, m_b_bias, m_w_out_even, m_w_in_odd, m_w_out_odd, m_w_ff1, m_w_ff2, v_norm_mix_pre, v_norm_mix_post, v_norm_ffn_pre, v_norm_ffn_post, v_w_in_even, v_lb_table, v_a_norm, v_b_ln_g, v_b_ln_b, v_b_ws, v_b_bias, v_w_out_even, v_w_in_odd, v_w_out_odd, v_w_ff1, v_w_ff2):
    n_seq = x.shape[0]
    t = n_seq * SEQ
    x0 = x.reshape(t, D_MODEL)
    target = loss_target.reshape(t, D_MODEL)

    me = _my_slot().astype(jnp.int32).reshape(1)

    order = ["in_e", "out_e", "ff1_0", "ff2_0", "in_o", "out_o", "ff1_1", "ff2_1"]
    shards = dict(in_e=w_in_even[0], out_e=w_out_even[0], in_o=w_in_odd[0], out_o=w_out_odd[0],
                  ff1_0=w_ff1[0], ff1_1=w_ff1[1], ff2_0=w_ff2[0], ff2_1=w_ff2[1])
    by_columns = ("in_e", "in_o", "ff1_0", "ff1_1")
    lands = [_place_own_columns(shards[k], me, "place_" + k) if k in by_columns
             else _place_own(shards[k], me, "place_" + k, False) for k in order]
    g_send, g_recv, lands, _, g_token = _exchange_start(lands, [None] * len(order), "gather_start")

    def get_w(keys, after):
        ks = [order.index(k) for k in keys]
        return _exchange_wait([lands[k] for k in ks], [None] * len(ks), [g_send[k] for k in ks],
                              [g_recv[k] for k in ks], after, "gather_wait_" + keys[0])

    sent = {}

    def put_g(group, blocks):
        keys = list(blocks)
        own = [_place_own(blocks[k], me, "own_" + k, True) for k in keys]
        send_sems, recv_sems, own, srcs, token = _exchange_start(own, [blocks[k] for k in keys], "scatter_start_" + group)
        sent[group] = (keys, own, srcs, send_sems, recv_sems)
        return token

    rope = _rope_tables(positions)
    bias_t = b_bias[0].T
    grads = _local_step(x0, target, rope, norm_mix_pre, norm_mix_post, norm_ffn_pre, norm_ffn_post, lb_table,
                        a_norm, b_ln_g, b_ln_b, b_ws[0], bias_t, get_w, put_g, g_token)
    (dx0, loss_part, dg_mix_pre, dg_mix_post, dg_ffn_pre, dg_ffn_post, d_lb, d_a_norm, d_ln_g, d_ln_b, d_ws,
     d_bias_t) = grads

    packed = jnp.concatenate([dg_mix_pre, dg_mix_post, dg_ffn_pre, dg_ffn_post,
                              jnp.concatenate([d_lb, d_a_norm], axis=1), jnp.concatenate([d_ln_g, d_ln_b], axis=1),
                              loss_part], axis=0)
    small_lands = [_place_own(a, me, "own_small%d" % k, False, F32) for k, a in enumerate((packed, d_ws, d_bias_t))]
    s_send, s_recv, small_lands, _, after = _exchange_start(small_lands, [None] * 3, "gather_small_start")

    big = dict(w_in_even=(["in_e"], w_in_even, m_w_in_even, v_w_in_even),
               w_out_even=(["out_e"], w_out_even, m_w_out_even, v_w_out_even),
               w_in_odd=(["in_o"], w_in_odd, m_w_in_odd, v_w_in_odd),
               w_out_odd=(["out_o"], w_out_odd, m_w_out_odd, v_w_out_odd),
               w_ff1=(["ff1_0", "ff1_1"], w_ff1, m_w_ff1, v_w_ff1), w_ff2=(["ff2_0", "ff2_1"], w_ff2, m_w_ff2, v_w_ff2))
    recv, big_out = {}, {}
    for groups, names in ((("ffn1", "ffn0"), ("w_ff1", "w_ff2")), (("mix1",), ("w_in_odd", "w_out_odd")),
                          (("mix0",), ("w_in_even", "w_out_even"))):
        for group in groups:
            keys, own, srcs, send_sems, recv_sems = sent[group]
            recv.update(zip(keys, _exchange_wait(own, srcs, send_sems, recv_sems, after, "scatter_wait_" + group)))
        for nm in names:
            keys, w, m, v = big[nm]
            big_out[nm] = _adamw_sharded([recv[k] for k in keys], w, m, v, after, "adamw_" + nm)
            after = big_out[nm][0]
    big_out = [big_out[nm] for nm in ("w_in_even", "w_out_even", "w_in_odd", "w_out_odd", "w_ff1", "w_ff2")]
    gathered = _exchange_wait(small_lands, [None] * 3, s_send, s_recv, after, "gather_small_wait")
    rows8 = lambda k: slice(SUBLANES * k, SUBLANES * (k + 1))
    left, right, every = slice(0, A_WIDTH), slice(A_WIDTH, 2 * A_WIDTH), slice(None)
    where = [(0, slice(0, 16), every), (0, slice(16, 32), every), (0, slice(32, 48), every), (0, slice(48, 64), every),
             (0, rows8(8), left), (0, rows8(8), right), (0, rows8(9), left), (0, rows8(9), right),
             (1, None, None), (2, None, None), (0, rows8(10), every)]
    small_w = [norm_mix_pre, norm_mix_post, norm_ffn_pre, norm_ffn_post, lb_table, a_norm, b_ln_g, b_ln_b,
               b_ws[0], bias_t]
    small_m = [m_norm_mix_pre, m_norm_mix_post, m_norm_ffn_pre, m_norm_ffn_post, m_lb_table, m_a_norm, m_b_ln_g,
               m_b_ln_b, m_b_ws[0], m_b_bias[0].T]
    small_v = [v_norm_mix_pre, v_norm_mix_post, v_norm_ffn_pre, v_norm_ffn_post, v_lb_table, v_a_norm, v_b_ln_g,
               v_b_ln_b, v_b_ws[0], v_b_bias[0].T]
    small_out = _small_update(gathered, where, small_w, small_m, small_v, 4, "small_update")
    loss = small_out[0].reshape(())
    small = [small_out[1 + 4 * k:5 + 4 * k] for k in range(len(small_w))]
    small[8] = [a[None] for a in small[8]]
    small[9] = [a.T[None] for a in small[9]]

    per_weight = small[0:4] + [big_out[0]] + small[4:10] + big_out[1:6]
    grad_x = dx0.reshape(x.shape)
    out = [loss, grad_x]
    for kind in range(4):
        out += [p[kind] for p in per_weight]
    return tuple(out)


def _local_step(x0, target, rope, norm_mix_pre, norm_mix_post, norm_ffn_pre, norm_ffn_post, lb_table, a_norm,
                b_ln_g, b_ln_b, ws, bias_t, get_w, put_g, token):
    def gain(a, l, tok):
        return a[l:l + 1] if tok is None else a[l:l + 1] + tok[0:1, 0:1]

    full = lambda a: a.reshape(-1, D_MODEL)
    owners = lambda a: a.reshape((N_DEV, -1) + a.shape[1:])

    (g_in_e,) = get_w(["in_e"], token)
    proj, h_mix0 = _norm_inproj(x0, gain(norm_mix_pre, 0, token), g_in_e, "inproj_even")
    oa, pre_a, states = _hgrn2_fwd(proj, lb_table, a_norm, "hgrn2_fwd")
    ob = _gmlp_fwd(proj, b_ln_g, b_ln_b, ws, bias_t, "gmlp_fwd")
    w_out_e = full(get_w(["out_e"], ob)[0])
    x1, mix0 = _outproj([oa, ob], w_out_e, x0, gain(norm_mix_post, 0, None), "outproj_even")
    w1_0, w2_0 = get_w(["ff1_0", "ff2_0"], x1)
    w2_0 = full(w2_0)
    x2, y0, h_ffn0, r0 = _ffn_fwd(x1, gain(norm_ffn_pre, 0, None), w1_0, w2_0, gain(norm_ffn_post, 0, None), "ffn_fwd_0")
    (g_in_o,) = get_w(["in_o"], x2)
    *qkv, h_mix1 = _norm_inproj_rope(x2, gain(norm_mix_pre, 1, None), g_in_o, rope, "inproj_odd")
    branches = [_attn_branch_fwd(a, "attn_fwd_d%d" % d) for a, d in zip(qkv, C_DILATIONS)]
    attn, attn_b, lse = _attn_merge(branches, "attn_merge")
    w_out_o = full(get_w(["out_o"], attn_b)[0])
    x3, mix1 = _outproj([attn_b], w_out_o, x2, gain(norm_mix_post, 1, None), "outproj_odd")
    w1_1, w2_1 = get_w(["ff1_1", "ff2_1"], x3)
    w2_1 = full(w2_1)
    x4, y1, h_ffn1, r1 = _ffn_fwd(x3, gain(norm_ffn_pre, 1, None), w1_1, w2_1, gain(norm_ffn_post, 1, None), "ffn_fwd_1")

    dx4, loss_part = _loss_grad(x4, target, "loss_grad")

    dx3, dy1, da1, dg_ffn_pre1, dg_ffn_post1 = _ffn_bwd(
        dx4, x3, y1, r1, gain(norm_ffn_pre, 1, None), w1_1, w2_1, gain(norm_ffn_post, 1, None), "ffn_bwd_1")
    gw_ff1_1 = _grad_w(h_ffn1, da1, True, "grad_w_ff1_1")
    gw_ff2_1 = _grad_w(r1, dy1, False, "grad_w_ff2_1")
    tok = put_g("ffn1", dict(ff1_1=gw_ff1_1, ff2_1=owners(gw_ff2_1)))
    *dattn, delta, dz1, dg_mix_post1 = _outproj_bwd_attn(dx3, mix1, gain(norm_mix_post, 1, tok), w_out_o, attn,
                                                  "outproj_bwd_odd")
    gw_out_o = _grad_w(attn_b, dz1, False, "grad_w_out_odd")
    per_seq = lambda a: a.reshape(-1, SEQ, LANES)
    grads_c = [_attn_branch_bwd(qkv[b], dattn[b], per_seq(lse), per_seq(delta), "attn_bwd_d%d" % d)
               for b, d in enumerate(C_DILATIONS)]
    dqkv = _attn_combine_bwd(grads_c, rope, "attn_combine_bwd")
    gw_in_o = _grad_w(h_mix1, dqkv, True, "grad_w_in_odd")
    tok = put_g("mix1", dict(out_o=owners(gw_out_o), in_o=gw_in_o))
    dx2, dg_mix_pre1 = _inproj_bwd(dqkv, g_in_o, dx3, x2, gain(norm_mix_pre, 1, tok), "inproj_bwd_odd")

    dx1, dy0, da0, dg_ffn_pre0, dg_ffn_post0 = _ffn_bwd(
        dx2, x1, y0, r0, gain(norm_ffn_pre, 0, None), w1_0, w2_0, gain(norm_ffn_post, 0, None), "ffn_bwd_0")
    gw_ff1_0 = _grad_w(h_ffn0, da0, True, "grad_w_ff1_0")
    gw_ff2_0 = _grad_w(r0, dy0, False, "grad_w_ff2_0")
    tok = put_g("ffn0", dict(ff1_0=gw_ff1_0, ff2_0=owners(gw_ff2_0)))
    dcat, dz0, dg_mix_post0 = _outproj_bwd(dx1, mix0, gain(norm_mix_post, 0, tok), w_out_e, "outproj_bwd_even")
    gw_out_e = jnp.concatenate([_grad_w(oa, dz0, False, "grad_w_out_even_a"),
                                _grad_w(ob, dz0, False, "grad_w_out_even_b")], axis=0)
    dqfig, d_lb, d_a_norm = _hgrn2_bwd(proj, dcat, pre_a, states, lb_table, a_norm, "hgrn2_bwd")
    duv, d_ln_g, d_ln_b, d_ws, d_bias_t = _gmlp_bwd(proj, dcat, b_ln_g, b_ln_b, ws, bias_t, "gmlp_bwd")
    dproj = jnp.concatenate([dqfig, duv], axis=1)
    gw_in_e = _grad_w(h_mix0, dproj, True, "grad_w_in_even")
    tok = put_g("mix0", dict(out_e=owners(gw_out_e), in_e=gw_in_e))
    dx0, dg_mix_pre0 = _inproj_bwd(dproj, g_in_e, dx1, x0, gain(norm_mix_pre, 0, tok), "inproj_bwd_even")

    layers = lambda a, b: jnp.concatenate([a, b], axis=0)
    return (dx0, loss_part, layers(dg_mix_pre0, dg_mix_pre1), layers(dg_mix_post0, dg_mix_post1),
            layers(dg_ffn_pre0, dg_ffn_pre1), layers(dg_ffn_post0, dg_ffn_post1),
            d_lb, d_a_norm, d_ln_g, d_ln_b, d_ws, d_bias_t)
```

```python
import functools
import math

import jax
import jax.numpy as jnp
from jax import lax
from jax.experimental import pallas as pl
from jax.experimental.pallas import tpu as pltpu

F32 = jnp.float32
BF16 = jnp.bfloat16
MESH = pl.DeviceIdType.MESH

N_DEV = 8
D_MODEL = 1024
SEQ = 2048
EPS = 1e-6
A_WIDTH = 512
A_HEADS = 4
HEAD_A = 128
B_WIDTH = 512
B_GROUPS = 4
B_CHUNK = 128
C_HEADS = 16
C_HEAD_DIM = 64
C_ROT_HALF = 8
ROPE_THETA = 500000.0
C_DILATIONS = (1, 4, 16)
C_BLOCK = 128
D_FF = 4096
EVEN_IN = 3072
ODD_IN = 3072

ADAM_LR = 0.001
ADAM_B1 = 0.9
ADAM_B2 = 0.999
ADAM_EPS = 1e-08
ADAM_WD = 0.01
ADAM_STEP = 10

LANES = 128
SUBLANES = 8
ROW_TILE = 512
PROJ_TILE = 1024
PROJ_COLS = 768
MERGE_TILE = 256
SUB_CHUNK = 16
HGRN_BLOCK = 256
NEG = -1e30
VMEM_LIMIT = 56 * 1024 * 1024


def _params(sem):
    return pltpu.CompilerParams(dimension_semantics=sem, vmem_limit_bytes=VMEM_LIMIT)


def _dot(a, b):
    return jnp.dot(a, b, preferred_element_type=F32)


def _dot_nt(a, b):
    return lax.dot_general(a, b, (((1,), (1,)), ((), ())), preferred_element_type=F32)


def _dot_tn(a, b):
    return lax.dot_general(a, b, (((0,), (0,)), ((), ())), preferred_element_type=F32)


def _rms(x, g):
    r = lax.rsqrt(jnp.mean(x * x, axis=-1, keepdims=True) + EPS)
    return x * r * g


def _rms_bwd(x, g, dy):
    r = lax.rsqrt(jnp.mean(x * x, axis=-1, keepdims=True) + EPS)
    dyg = dy * g
    dx = r * dyg - x * (r * r * r) * jnp.mean(x * dyg, axis=-1, keepdims=True)
    return dx, dy * x * r


def _rows8(v):
    return v.reshape(v.shape[0] // SUBLANES, SUBLANES, v.shape[1]).sum(axis=0)


def _sigmoid(x):
    return 1.0 / (1.0 + jnp.exp(-x))


def _gelu(x):
    return 0.5 * x * (1.0 + jnp.tanh(math.sqrt(2.0 / math.pi) * (x + 0.044715 * (x * x * x))))


def _acc_rows8(ref, val, first):
    @pl.when(first)
    def _():
        ref[...] = val

    @pl.when(jnp.logical_not(first))
    def _():
        ref[...] += val


def _my_slot():
    return 4 * lax.axis_index("x") + 2 * lax.axis_index("y") + lax.axis_index("c")


def _peer(r):
    x, y, c = lax.axis_index("x"), lax.axis_index("y"), lax.axis_index("c")
    px = 1 - x if (r >> 2) & 1 else x
    py = 1 - y if (r >> 1) & 1 else y
    pc = 1 - c if r & 1 else c
    return (px, py, pc), 4 * px + 2 * py + pc


HBM_SPEC = pl.BlockSpec(memory_space=pltpu.HBM)
SEM_SPEC = pl.BlockSpec(memory_space=pltpu.SEMAPHORE)
SPLIT_EFFECT = pltpu.SideEffectType.DATAFLOW_SIDE_EFFECTING


def _split_copies(land_ref, src_ref, send_sem, recv_sem):
    me = _my_slot()
    copies = []
    for r in range(1, N_DEV):
        peer, slot = _peer(r)
        src = _slot(land_ref, me) if src_ref is None else _slot(src_ref, slot)
        copies.append(pltpu.make_async_remote_copy(
            src_ref=src, dst_ref=_slot(land_ref, me), send_sem=send_sem, recv_sem=recv_sem,
            device_id=peer, device_id_type=MESH))
    return copies


def _slot(ref, s):
    if len(ref.shape) == 2:
        c = ref.shape[1] // N_DEV
        return ref.at[:, pl.ds(pl.multiple_of(s * c, LANES), c)]
    return ref.at[s]


def _exchange_start(lands, sources, name):
    n = len(lands)
    given = [s for s in sources if s is not None]
    arrays = list(lands) + given

    def body(*refs):
        land_refs, src_refs = refs[:n], list(refs[n:n + len(given)])
        sems = refs[len(arrays):len(arrays) + 2 * n]
        token = refs[-1]
        for k in range(n):
            src_ref = None if sources[k] is None else src_refs.pop(0)
            for copy in _split_copies(land_refs[k], src_ref, sems[k], sems[n + k]):
                copy.start()
        token[...] = jnp.zeros_like(token)

    outs = pl.pallas_call(
        body, name=name,
        out_shape=(pltpu.SemaphoreType.DMA(()),) * (2 * n) + tuple(pltpu.HBM(a.shape, a.dtype) for a in arrays)
        + (jax.ShapeDtypeStruct((SUBLANES, LANES), F32),),
        in_specs=[HBM_SPEC] * len(arrays),
        out_specs=(SEM_SPEC,) * (2 * n) + (HBM_SPEC,) * len(arrays) + (pl.BlockSpec(memory_space=pltpu.VMEM),),
        input_output_aliases={i: 2 * n + i for i in range(len(arrays))},
        compiler_params=pltpu.CompilerParams(has_side_effects=SPLIT_EFFECT),
    )(*[pltpu.with_memory_space_constraint(a, pltpu.HBM) for a in arrays])
    return list(outs[:n]), list(outs[n:2 * n]), list(outs[2 * n:3 * n]), list(outs[3 * n:-1]), outs[-1]


def _exchange_wait(lands, sources, send_sems, recv_sems, after, name):
    n = len(lands)
    given = [s for s in sources if s is not None]
    arrays = list(lands) + given

    def body(*refs):
        land_refs, src_refs = refs[:n], list(refs[n:n + len(given)])
        sems = refs[len(arrays):len(arrays) + 2 * n]
        for i in range(n):
            src_ref = None if sources[i] is None else src_refs.pop(0)
            copies = _split_copies(land_refs[i], src_ref, sems[i], sems[n + i])
            for copy in copies:
                copy.wait_recv()
            for copy in copies:
                copy.wait_send()

    outs = pl.pallas_call(
        body, name=name, out_shape=tuple(pltpu.HBM(a.shape, a.dtype) for a in arrays),
        in_specs=[HBM_SPEC] * len(arrays) + [SEM_SPEC] * (2 * n) + [pl.BlockSpec(memory_space=pl.ANY)],
        out_specs=(HBM_SPEC,) * len(arrays),
        input_output_aliases={i: i for i in range(len(arrays))},
        compiler_params=pltpu.CompilerParams(has_side_effects=SPLIT_EFFECT),
    )(*arrays, *send_sems, *recv_sems, after)
    return list(outs[:n])


def _place_own(a, me, name, own_block, dtype=BF16):
    shape = a.shape[1:] if own_block else a.shape
    cols = shape[-1]
    a3 = a.reshape((N_DEV if own_block else 1, -1, cols))
    rows = a3.shape[1]
    tr = min(rows, 512)

    def body(me_ref, a_ref, o_ref):
        o_ref[...] = a_ref[...].astype(dtype)

    grid_spec = pltpu.PrefetchScalarGridSpec(
        num_scalar_prefetch=1, grid=(rows // tr,),
        in_specs=[pl.BlockSpec((1, tr, cols), lambda i, me_ref: (me_ref[0] if own_block else 0, i, 0))],
        out_specs=pl.BlockSpec((1, tr, cols), lambda i, me_ref: (me_ref[0], i, 0)))
    out = pl.pallas_call(
        body, name=name, grid_spec=grid_spec, out_shape=jax.ShapeDtypeStruct((N_DEV, rows, cols), dtype),
        compiler_params=_params(("arbitrary",)),
    )(me, a3)
    return out.reshape((N_DEV,) + shape)


def _place_own_columns(a, me, name):
    rows, cols = a.shape
    tr = min(rows, 512)

    def body(me_ref, a_ref, o_ref):
        o_ref[...] = a_ref[...].astype(BF16)

    grid_spec = pltpu.PrefetchScalarGridSpec(
        num_scalar_prefetch=1, grid=(rows // tr,),
        in_specs=[pl.BlockSpec((tr, cols), lambda i, me_ref: (i, 0))],
        out_specs=pl.BlockSpec((tr, cols), lambda i, me_ref: (i, me_ref[0])))
    return pl.pallas_call(
        body, name=name, grid_spec=grid_spec, out_shape=jax.ShapeDtypeStruct((rows, N_DEV * cols), BF16),
        compiler_params=_params(("arbitrary",)),
    )(me, a)


def _rope_tables(positions):
    inv = ROPE_THETA ** (-jnp.arange(C_ROT_HALF, dtype=F32) / C_ROT_HALF)
    ang = positions.reshape(-1)[:, None].astype(F32) * inv
    cos, sin = jnp.cos(ang), jnp.sin(ang)
    t = ang.shape[0]
    ones = jnp.ones((t, C_HEAD_DIM - 2 * C_ROT_HALF), F32)
    c_head = jnp.concatenate([cos, cos, ones], axis=1)
    s_head = jnp.concatenate([-sin, sin, 0.0 * ones], axis=1)
    return jnp.concatenate([c_head, c_head], axis=1), jnp.concatenate([s_head, s_head], axis=1)


def _swap_halves(x):
    lane = lax.broadcasted_iota(jnp.int32, x.shape, 1) % C_HEAD_DIM
    return jnp.where(lane < C_ROT_HALF, pltpu.roll(x, LANES - C_ROT_HALF, 1), pltpu.roll(x, C_ROT_HALF, 1))


def _norm_inproj(x, g, w, name):
    t = x.shape[0]
    n = w.shape[1]
    tm, tn = PROJ_TILE, PROJ_COLS

    def body(x_ref, g_ref, w_ref, o_ref, h_ref):
        @pl.when(pl.program_id(1) == 0)
        def _():
            h_ref[...] = _rms(x_ref[...], g_ref[...]).astype(BF16)

        o_ref[...] = _dot(h_ref[...], w_ref[...])

    return pl.pallas_call(
        body, name=name, grid=(t // tm, n // tn),
        in_specs=[pl.BlockSpec((tm, D_MODEL), lambda i, j: (i, 0)), pl.BlockSpec((1, D_MODEL), lambda i, j: (0, 0)),
                  pl.BlockSpec((D_MODEL, tn), lambda i, j: (0, j))],
        out_specs=[pl.BlockSpec((tm, tn), lambda i, j: (i, j)), pl.BlockSpec((tm, D_MODEL), lambda i, j: (i, 0))],
        out_shape=[jax.ShapeDtypeStruct((t, n), F32), jax.ShapeDtypeStruct((t, D_MODEL), BF16)],
        compiler_params=_params(("parallel", "arbitrary")),
    )(x, g, w)


def _dilated_specs(tm, width, col_of):
    per_seq = SEQ // tm
    specs = []
    for d in C_DILATIONS:
        specs.append(pl.BlockSpec(
            (1, d, tm // d, width), lambda i, *rest: (i // per_seq, 0, i % per_seq, col_of(*rest))))
    return specs


def _dilated_shapes(n_seq, cols, dtype):
    return [jax.ShapeDtypeStruct((n_seq, d, SEQ // d, cols), dtype) for d in C_DILATIONS]


def _store_dilated(src_ref, out_refs, dtype):
    groups, tm, _ = src_ref.shape
    for d, o_ref in zip(C_DILATIONS, out_refs):
        for r in range(d):
            rows = pl.ds(r, tm // d, stride=d) if d > 1 else slice(None)
            for p in range(groups):
                o_ref[0, r, :, p * LANES:(p + 1) * LANES] = src_ref.at[p][rows, :].astype(dtype)


def _load_dilated(in_ref, d, dst_ref):
    groups, tm, _ = dst_ref.shape
    for r in range(d):
        rows = pl.ds(r, tm // d, stride=d)
        for p in range(groups):
            dst_ref.at[p][rows, :] = in_ref[0, r, :, p * LANES:(p + 1) * LANES].astype(F32)


def _norm_inproj_rope(x, g, w, rope, name):
    t = x.shape[0]
    n = w.shape[1]
    tm, nb = PROJ_TILE, PROJ_COLS

    def body(x_ref, g_ref, w_ref, c_ref, s_ref, o1_ref, o4_ref, o16_ref, h_ref, tile_ref):
        j = pl.program_id(1)

        @pl.when(j == 0)
        def _():
            h_ref[...] = _rms(x_ref[...], g_ref[...]).astype(BF16)

        acc = _dot(h_ref[...], w_ref[...])
        for p in range(nb // LANES):
            blk = acc[:, p * LANES:(p + 1) * LANES]
            roped = blk * c_ref[...] + _swap_halves(blk) * s_ref[...]
            is_qk = (j * (nb // LANES) + p) < 2 * (D_MODEL // LANES)
            tile_ref[p] = jnp.where(is_qk, roped, blk)
        _store_dilated(tile_ref, (o1_ref, o4_ref, o16_ref), BF16)

    return pl.pallas_call(
        body, name=name, grid=(t // tm, n // nb),
        in_specs=[pl.BlockSpec((tm, D_MODEL), lambda i, j: (i, 0)), pl.BlockSpec((1, D_MODEL), lambda i, j: (0, 0)),
                  pl.BlockSpec((D_MODEL, nb), lambda i, j: (0, j)),
                  pl.BlockSpec((tm, LANES), lambda i, j: (i, 0)), pl.BlockSpec((tm, LANES), lambda i, j: (i, 0))],
        out_specs=_dilated_specs(tm, nb, lambda j: j) + [pl.BlockSpec((tm, D_MODEL), lambda i, j: (i, 0))],
        out_shape=_dilated_shapes(t // SEQ, n, BF16) + [jax.ShapeDtypeStruct((t, D_MODEL), BF16)],
        scratch_shapes=[pltpu.VMEM((nb // LANES, tm, LANES), F32)],
        compiler_params=_params(("parallel", "arbitrary")),
    )(x, g, w, *rope)


def _outproj(parts, w, x, g, name):
    t = x.shape[0]
    tm = PROJ_TILE
    n = len(parts)
    widths = [p.shape[1] for p in parts]

    def body(*refs):
        p_refs = refs[:n]
        w_ref, x_ref, g_ref, xo_ref, mix_ref = refs[n:]
        mix = None
        off = 0
        for p_ref, wd in zip(p_refs, widths):
            term = _dot(p_ref[...].astype(BF16), w_ref[off:off + wd, :])
            mix = term if mix is None else mix + term
            off += wd
        mix_ref[...] = mix
        xo_ref[...] = x_ref[...] + _rms(mix, g_ref[...])

    row = lambda i: (i, 0)
    return pl.pallas_call(
        body, name=name, grid=(t // tm,),
        in_specs=[pl.BlockSpec((tm, wd), row) for wd in widths] + [
            pl.BlockSpec((sum(widths), D_MODEL), lambda i: (0, 0)),
            pl.BlockSpec((tm, D_MODEL), row), pl.BlockSpec((1, D_MODEL), lambda i: (0, 0))],
        out_specs=[pl.BlockSpec((tm, D_MODEL), row)] * 2,
        out_shape=[jax.ShapeDtypeStruct((t, D_MODEL), F32)] * 2,
        compiler_params=_params(("parallel",)),
    )(*parts, w, x, g)


def _outproj_bwd(dx, mix, g, w, name):
    t = dx.shape[0]
    tm = PROJ_TILE
    k = w.shape[0]

    def body(dx_ref, mix_ref, g_ref, w_ref, dcat_ref, dz_ref, dg_ref):
        dz, dgr = _rms_bwd(mix_ref[...], g_ref[...], dx_ref[...])
        dzb = dz.astype(BF16)
        dz_ref[...] = dzb
        dcat_ref[...] = _dot_nt(dzb, w_ref[...])
        _acc_rows8(dg_ref, _rows8(dgr), pl.program_id(0) == 0)

    row = lambda i: (i, 0)
    return pl.pallas_call(
        body, name=name, grid=(t // tm,),
        in_specs=[pl.BlockSpec((tm, D_MODEL), row), pl.BlockSpec((tm, D_MODEL), row),
                  pl.BlockSpec((1, D_MODEL), lambda i: (0, 0)), pl.BlockSpec((k, D_MODEL), lambda i: (0, 0))],
        out_specs=[pl.BlockSpec((tm, k), row), pl.BlockSpec((tm, D_MODEL), row),
                   pl.BlockSpec((SUBLANES, D_MODEL), lambda i: (0, 0))],
        out_shape=[jax.ShapeDtypeStruct((t, k), F32), jax.ShapeDtypeStruct((t, D_MODEL), BF16),
                   jax.ShapeDtypeStruct((SUBLANES, D_MODEL), F32)],
        compiler_params=_params(("arbitrary",)),
    )(dx, mix, g, w)


def _outproj_bwd_attn(dx, mix, g, w, out, name):
    t = dx.shape[0]
    tm = MERGE_TILE

    def body(dx_ref, mix_ref, g_ref, w_ref, out_ref, do1, do4, do16, dl_ref, dz_ref, dg_ref, tile_ref):
        dz, dgr = _rms_bwd(mix_ref[...], g_ref[...], dx_ref[...])
        dzb = dz.astype(BF16)
        dz_ref[...] = dzb
        _acc_rows8(dg_ref, _rows8(dgr), pl.program_id(0) == 0)
        dout = _dot_nt(dzb, w_ref[...])
        for p in range(LANE_GROUPS):
            tile_ref[p] = dout[:, p * LANES:(p + 1) * LANES]
        _store_dilated(tile_ref, (do1, do4, do16), BF16)
        column = lax.broadcasted_iota(jnp.int32, (D_MODEL, LANES), 0) // C_HEAD_DIM
        head = lax.broadcasted_iota(jnp.int32, (D_MODEL, LANES), 1)
        dl_ref[...] = jnp.dot(dout * out_ref[...], (column == head).astype(F32), precision=lax.Precision.HIGHEST,
                              preferred_element_type=F32)

    row = lambda i: (i, 0)
    n_seq = t // SEQ
    return pl.pallas_call(
        body, name=name, grid=(t // tm,),
        in_specs=[pl.BlockSpec((tm, D_MODEL), row), pl.BlockSpec((tm, D_MODEL), row),
                  pl.BlockSpec((1, D_MODEL), lambda i: (0, 0)), pl.BlockSpec((D_MODEL, D_MODEL), lambda i: (0, 0)),
                  pl.BlockSpec((tm, D_MODEL), row)],
        out_specs=_dilated_specs(tm, D_MODEL, lambda: 0) + [
            pl.BlockSpec((tm, LANES), row), pl.BlockSpec((tm, D_MODEL), row),
            pl.BlockSpec((SUBLANES, D_MODEL), lambda i: (0, 0))],
        out_shape=_dilated_shapes(n_seq, D_MODEL, BF16) + [
            jax.ShapeDtypeStruct((t, LANES), F32), jax.ShapeDtypeStruct((t, D_MODEL), BF16),
            jax.ShapeDtypeStruct((SUBLANES, D_MODEL), F32)],
        scratch_shapes=[pltpu.VMEM((LANE_GROUPS, tm, LANES), F32)],
        compiler_params=_params(("arbitrary",)),
    )(dx, mix, g, w, out)


def _inproj_bwd(dproj, w, dx, x, g, name):
    t = x.shape[0]
    n = w.shape[1]
    tm = ROW_TILE

    def body(dp_ref, w_ref, dx_ref, x_ref, g_ref, o_ref, dg_ref):
        dxn, dgr = _rms_bwd(x_ref[...], g_ref[...], _dot_nt(dp_ref[...], w_ref[...]))
        o_ref[...] = dx_ref[...] + dxn
        _acc_rows8(dg_ref, _rows8(dgr), pl.program_id(0) == 0)

    row = lambda i: (i, 0)
    return pl.pallas_call(
        body, name=name, grid=(t // tm,),
        in_specs=[pl.BlockSpec((tm, n), row), pl.BlockSpec((D_MODEL, n), lambda i: (0, 0)),
                  pl.BlockSpec((tm, D_MODEL), row), pl.BlockSpec((tm, D_MODEL), row),
                  pl.BlockSpec((1, D_MODEL), lambda i: (0, 0))],
        out_specs=[pl.BlockSpec((tm, D_MODEL), row), pl.BlockSpec((SUBLANES, D_MODEL), lambda i: (0, 0))],
        out_shape=[jax.ShapeDtypeStruct((t, D_MODEL), F32), jax.ShapeDtypeStruct((SUBLANES, D_MODEL), F32)],
        compiler_params=_params(("arbitrary",)),
    )(dproj, w, dx, x, g)


def _grad_w(a, b, col_blocks, name):
    t, k = a.shape
    n = b.shape[1]
    tk = min(k, 1024)
    per_owner = n // N_DEV
    tn = 2 * per_owner if col_blocks else min(n, 1024)

    def body(a_ref, b_ref, o_ref, at_ref):
        @pl.when(pl.program_id(1) == 0)
        def _():
            for c in range(t // ROW_TILE):
                rows = slice(c * ROW_TILE, (c + 1) * ROW_TILE)
                at_ref[:, rows] = a_ref[rows, :].T

        res = _dot(at_ref[...], b_ref[...]).astype(BF16)
        if col_blocks:
            o_ref[0] = res[:, :per_owner]
            o_ref[1] = res[:, per_owner:]
        else:
            o_ref[...] = res

    if col_blocks:
        out_spec = pl.BlockSpec((2, tk, per_owner), lambda i, j: (j, i, 0))
        out_shape = jax.ShapeDtypeStruct((N_DEV, k, per_owner), BF16)
    else:
        out_spec = pl.BlockSpec((tk, tn), lambda i, j: (i, j))
        out_shape = jax.ShapeDtypeStruct((k, n), BF16)
    return pl.pallas_call(
        body, name=name, grid=(k // tk, n // tn),
        in_specs=[pl.BlockSpec((t, tk), lambda i, j: (0, i)), pl.BlockSpec((t, tn), lambda i, j: (0, j))],
        out_specs=out_spec, out_shape=out_shape,
        scratch_shapes=[pltpu.VMEM((tk, t), BF16)],
        compiler_params=_params(("parallel", "arbitrary")),
    )(a, b)


FF_BLOCK = D_FF // N_DEV
FF_STEP = 1024
FF_STEPS = D_FF // FF_STEP


def _ffn_fwd(x, g_pre, w1, w2, g_post, name, target=None):
    t = x.shape[0]
    tm = PROJ_TILE

    def body(*refs):
        if target is None:
            x_ref, gp_ref, w1_ref, w2_ref, gq_ref, xo_ref, y_ref, h_ref, r_ref = refs
        else:
            x_ref, gp_ref, w1_ref, w2_ref, gq_ref, t_ref, xo_ref, y_ref, h_ref, r_ref, l_ref = refs
        i, j = pl.program_id(0), pl.program_id(1)

        @pl.when(j == 0)
        def _():
            h_ref[...] = _rms(x_ref[...], gp_ref[...]).astype(BF16)

        a = _dot(h_ref[...], w1_ref[...])
        r = jnp.square(jnp.maximum(a, 0.0)).astype(BF16)
        r_ref[...] = r
        term = _dot(r, w2_ref[...])

        @pl.when(j == 0)
        def _():
            y_ref[...] = term

        @pl.when(j > 0)
        def _():
            y_ref[...] += term

        @pl.when(j == FF_STEPS - 1)
        def _():
            x_new = x_ref[...] + _rms(y_ref[...], gq_ref[...])
            if target is None:
                xo_ref[...] = x_new
            else:
                diff = x_new - t_ref[...]
                xo_ref[...] = diff * (1.0 / D_MODEL)
                _acc_rows8(l_ref, _rows8(diff * diff) * (0.5 / D_MODEL), i == 0)

    row = lambda i, j: (i, 0)
    vec = pl.BlockSpec((1, D_MODEL), lambda i, j: (0, 0))
    in_specs = [pl.BlockSpec((tm, D_MODEL), row), vec, pl.BlockSpec((D_MODEL, FF_STEP), lambda i, j: (0, j)),
                pl.BlockSpec((FF_STEP, D_MODEL), lambda i, j: (j, 0)), vec]
    out_specs = [pl.BlockSpec((tm, D_MODEL), row)] * 3 + [pl.BlockSpec((tm, FF_STEP), lambda i, j: (i, j))]
    out_shape = [jax.ShapeDtypeStruct((t, D_MODEL), F32), jax.ShapeDtypeStruct((t, D_MODEL), F32),
                 jax.ShapeDtypeStruct((t, D_MODEL), BF16), jax.ShapeDtypeStruct((t, D_FF), BF16)]
    args = [x, g_pre, w1, w2, g_post]
    if target is not None:
        in_specs.append(pl.BlockSpec((tm, D_MODEL), row))
        out_specs.append(pl.BlockSpec((SUBLANES, D_MODEL), lambda i, j: (0, 0)))
        out_shape.append(jax.ShapeDtypeStruct((SUBLANES, D_MODEL), F32))
        args.append(target)
    return pl.pallas_call(
        body, name=name, grid=(t // tm, FF_STEPS), in_specs=in_specs, out_specs=out_specs, out_shape=out_shape,
        compiler_params=_params(("parallel" if target is None else "arbitrary", "arbitrary")),
    )(*args)


def _ffn_bwd(dxo, x, y, r, g_pre, w1, w2, g_post, name):
    t = x.shape[0]
    tm = ROW_TILE

    def body(dxo_ref, x_ref, y_ref, r_ref, gp_ref, w1_ref, w2_ref, gq_ref,
             dx_ref, dy_ref, da_ref, dgp_ref, dgq_ref, acc_ref):
        i, j = pl.program_id(0), pl.program_id(1)

        @pl.when(j == 0)
        def _():
            dy, dgr = _rms_bwd(y_ref[...], gq_ref[...], dxo_ref[...])
            dy_ref[...] = dy.astype(BF16)
            _acc_rows8(dgq_ref, _rows8(dgr), i == 0)

        dr = _dot_nt(dy_ref[...], w2_ref[...])
        da = (dr * (2.0 * jnp.sqrt(r_ref[...].astype(F32)))).astype(BF16)
        da_ref[...] = da
        term = _dot_nt(da, w1_ref[...])

        @pl.when(j == 0)
        def _():
            acc_ref[...] = term

        @pl.when(j > 0)
        def _():
            acc_ref[...] += term

        @pl.when(j == FF_STEPS - 1)
        def _():
            dxn, dgr = _rms_bwd(x_ref[...], gp_ref[...], acc_ref[...])
            dx_ref[...] = dxo_ref[...] + dxn
            _acc_rows8(dgp_ref, _rows8(dgr), i == 0)

    row = lambda i, j: (i, 0)
    vec = pl.BlockSpec((1, D_MODEL), lambda i, j: (0, 0))
    acc8 = pl.BlockSpec((SUBLANES, D_MODEL), lambda i, j: (0, 0))
    return pl.pallas_call(
        body, name=name, grid=(t // tm, FF_STEPS),
        in_specs=[pl.BlockSpec((tm, D_MODEL), row)] * 3 + [
            pl.BlockSpec((tm, FF_STEP), lambda i, j: (i, j)),
            vec, pl.BlockSpec((D_MODEL, FF_STEP), lambda i, j: (0, j)),
            pl.BlockSpec((FF_STEP, D_MODEL), lambda i, j: (j, 0)), vec],
        out_specs=[pl.BlockSpec((tm, D_MODEL), row), pl.BlockSpec((tm, D_MODEL), row),
                   pl.BlockSpec((tm, FF_STEP), lambda i, j: (i, j)), acc8, acc8],
        out_shape=[jax.ShapeDtypeStruct((t, D_MODEL), F32), jax.ShapeDtypeStruct((t, D_MODEL), BF16),
                   jax.ShapeDtypeStruct((t, D_FF), BF16),
                   jax.ShapeDtypeStruct((SUBLANES, D_MODEL), F32), jax.ShapeDtypeStruct((SUBLANES, D_MODEL), F32)],
        scratch_shapes=[pltpu.VMEM((tm, D_MODEL), F32)],
        compiler_params=_params(("arbitrary", "arbitrary")),
    )(dxo, x, y, r, g_pre, w1, w2, g_post)


def _lower_bound(table):
    e = jnp.exp(table - jnp.max(table, axis=0, keepdims=True))
    return e[0:1, :] / jnp.sum(e, axis=0, keepdims=True)


def _hgrn2_block(q_ref, f_ref, lb):
    tb = f_ref.shape[0]
    sig = _sigmoid(f_ref[...])
    f = lb + (1.0 - lb) * sig
    qraw = q_ref[...]
    sq = _sigmoid(qraw)
    r = lax.broadcasted_iota(jnp.int32, (tb, tb), 0)
    c = lax.broadcasted_iota(jnp.int32, (tb, tb), 1)
    same = (r // SUB_CHUNK) == (c // SUB_CHUNK)
    logf = jnp.log(f)
    gsum = jnp.dot((same & (c <= r)).astype(F32), logf, precision=lax.Precision.HIGHEST, preferred_element_type=F32)
    glast = jnp.dot(same.astype(F32), logf, precision=lax.Precision.HIGHEST, preferred_element_type=F32)
    return dict(sig=sig, f=f, kk=1.0 - f, qraw=qraw, sq=sq, qs=qraw * sq, gsum=gsum,
                eg=jnp.exp(gsum), ekd=jnp.exp(glast - gsum), a=jnp.exp(glast))


def _head_sums(x):
    parts = [jnp.broadcast_to(jnp.sum(x[:, h * HEAD_A:(h + 1) * HEAD_A], axis=1, keepdims=True), (x.shape[0], HEAD_A))
             for h in range(A_HEADS)]
    return jnp.concatenate(parts, axis=1)


def _hgrn2_intra(g, kk, qs, v):
    row = lax.broadcasted_iota(jnp.int32, g.shape, 0)
    o = _head_sums(qs * kk) * v
    for j in range(1, SUB_CHUNK):
        decay = jnp.exp(jnp.where(row >= j, g - pltpu.roll(g, j, 0), NEG))
        o = o + _head_sums(qs * pltpu.roll(kk, j, 0) * decay) * pltpu.roll(v, j, 0)
    return o


def _hgrn2_intra_bwd(g, kk, qs, v, do):
    row = lax.broadcasted_iota(jnp.int32, g.shape, 0)
    dsc = _head_sums(do * v)
    dqs, dkk, dv = dsc * kk, dsc * qs, _head_sums(qs * kk) * do
    for j in range(1, SUB_CHUNK):
        k_dn = pltpu.roll(kk, j, 0)
        decay = jnp.exp(jnp.where(row >= j, g - pltpu.roll(g, j, 0), NEG))
        d_score = _head_sums(do * pltpu.roll(v, j, 0)) * decay
        dqs = dqs + d_score * k_dn
        dkk = dkk + pltpu.roll(d_score * qs, SUB_CHUNK - j, 0)
        dv = dv + pltpu.roll(_head_sums(qs * k_dn * decay) * do, SUB_CHUNK - j, 0)
    return dqs, dkk, dv


def _hgrn2_fwd(proj, lb_table, a_norm, name):
    t = proj.shape[0]
    tb = HGRN_BLOCK
    n_tb = SEQ // tb
    n_seq = t // SEQ
    n_sub = tb // SUB_CHUNK

    def body(q_ref, f_ref, i_ref, g_ref, lbt_ref, an_ref, o_ref, pre_ref, sts_ref, st_ref,
             gs_ref, kk_ref, qs_ref, eg_ref, ekd_ref, a_ref):
        @pl.when(pl.program_id(1) == 0)
        def _():
            st_ref[...] = jnp.zeros_like(st_ref)

        an = an_ref[...]
        blk = _hgrn2_block(q_ref, f_ref, _lower_bound(lbt_ref[...]))
        for ref, key in ((gs_ref, "gsum"), (kk_ref, "kk"), (qs_ref, "qs"), (eg_ref, "eg"), (ekd_ref, "ekd"), (a_ref, "a")):
            ref[...] = blk[key]

        def step(c, carry):
            rows = pl.ds(pl.multiple_of(c * SUB_CHUNK, SUB_CHUNK), SUB_CHUNK)
            kk, qs, v = kk_ref[rows, :], qs_ref[rows, :], i_ref[rows, :]
            o = _hgrn2_intra(gs_ref[rows, :], kk, qs, v)
            qg, kd, vb = (qs * eg_ref[rows, :]).astype(BF16), (kk * ekd_ref[rows, :]).astype(BF16), v.astype(BF16)
            for h in range(A_HEADS):
                lanes = slice(h * HEAD_A, (h + 1) * HEAD_A)
                st = st_ref[h]
                sts_ref[0, c, h] = st
                o_h = o[:, lanes] + _dot_nt(qg[:, lanes], st.astype(BF16))
                st_ref[h] = st * a_ref[rows, lanes][0:1] + _dot_tn(vb[:, lanes], kd[:, lanes])
                pre_ref[rows, lanes] = o_h
                graw = g_ref[rows, lanes]
                o_ref[rows, lanes] = (_rms(o_h, an[:, lanes]) * (graw * _sigmoid(graw))).astype(BF16)
            return carry

        lax.fori_loop(0, n_sub, step, 0, unroll=2)

    def col(k):
        return pl.BlockSpec((tb, A_WIDTH), lambda b, s, k=k: (b * n_tb + s, k))

    out_rows = pl.BlockSpec((tb, A_WIDTH), lambda b, s: (b * n_tb + s, 0))
    return pl.pallas_call(
        body, name=name, grid=(n_seq, n_tb),
        in_specs=[col(0), col(1), col(2), col(3),
                  pl.BlockSpec((3, A_WIDTH), lambda b, s: (0, 0)), pl.BlockSpec((1, A_WIDTH), lambda b, s: (0, 0))],
        out_specs=[out_rows, out_rows,
                   pl.BlockSpec((1, n_sub, A_HEADS, HEAD_A, HEAD_A), lambda b, s: (b * n_tb + s, 0, 0, 0, 0))],
        out_shape=[jax.ShapeDtypeStruct((t, A_WIDTH), BF16), jax.ShapeDtypeStruct((t, A_WIDTH), F32),
                   jax.ShapeDtypeStruct((n_seq * n_tb, n_sub, A_HEADS, HEAD_A, HEAD_A), F32)],
        scratch_shapes=[pltpu.VMEM((A_HEADS, HEAD_A, HEAD_A), F32)] + [pltpu.VMEM((tb, A_WIDTH), F32)] * 6,
        compiler_params=_params(("parallel", "arbitrary")),
    )(proj, proj, proj, proj, lb_table, a_norm)


def _hgrn2_bwd(proj, dcat, pre, states, lb_table, a_norm, name):
    t = proj.shape[0]
    tb = HGRN_BLOCK
    n_tb = SEQ // tb
    n_seq = t // SEQ
    n_sub = tb // SUB_CHUNK

    def body(q_ref, f_ref, i_ref, g_ref, do_ref, pre_ref, sts_ref, lbt_ref, an_ref, dp_ref, dlb_ref, dan_ref, dst_ref,
             gs_ref, kk_ref, qs_ref, eg_ref, ekd_ref, a_ref, dpre_ref, dlf_ref, dqs_ref, dkk_ref):
        b, s = pl.program_id(0), pl.program_id(1)

        @pl.when(s == 0)
        def _():
            dst_ref[...] = jnp.zeros_like(dst_ref)

        @pl.when((b == 0) & (s == 0))
        def _():
            dlb_ref[...] = jnp.zeros_like(dlb_ref)
            dan_ref[...] = jnp.zeros_like(dan_ref)

        lb = _lower_bound(lbt_ref[...])
        an = an_ref[...]
        heads = [slice(h * HEAD_A, (h + 1) * HEAD_A) for h in range(A_HEADS)]
        blk = _hgrn2_block(q_ref, f_ref, lb)
        for ref, key in ((gs_ref, "gsum"), (kk_ref, "kk"), (qs_ref, "qs"), (eg_ref, "eg"), (ekd_ref, "ekd"), (a_ref, "a")):
            ref[...] = blk[key]
        for h, lanes in enumerate(heads):
            graw, o = g_ref[:, lanes], pre_ref[:, lanes]
            sg = _sigmoid(graw)
            dout = do_ref[:, lanes]
            d_o, dgr = _rms_bwd(o, an[:, lanes], dout * (graw * sg))
            dan_ref[0:1, lanes] += jnp.sum(dgr, axis=0, keepdims=True)
            dp_ref[:, 3 * A_WIDTH + h * HEAD_A:3 * A_WIDTH + (h + 1) * HEAD_A] = (
                dout * _rms(o, an[:, lanes]) * (sg * (1.0 + graw * (1.0 - sg)))).astype(BF16)
            dpre_ref[:, lanes] = d_o

        tri_t = (lax.broadcasted_iota(jnp.int32, (SUB_CHUNK, SUB_CHUNK), 0)
                 <= lax.broadcasted_iota(jnp.int32, (SUB_CHUNK, SUB_CHUNK), 1)).astype(F32)

        def back(k, carry):
            c = n_sub - 1 - k
            rows = pl.ds(pl.multiple_of(c * SUB_CHUNK, SUB_CHUNK), SUB_CHUNK)
            g, kk, qs, v, d_o = gs_ref[rows, :], kk_ref[rows, :], qs_ref[rows, :], i_ref[rows, :], dpre_ref[rows, :]
            eg, ekd, a = eg_ref[rows, :], ekd_ref[rows, :], a_ref[rows, :]
            dqs, dkk, dv = _hgrn2_intra_bwd(g, kk, qs, v, d_o)
            qg_f, kd_f = qs * eg, kk * ekd
            qg, kd, vb, dob = qg_f.astype(BF16), kd_f.astype(BF16), v.astype(BF16), d_o.astype(BF16)
            dqg, dkd, da, dv_st = [], [], [], []
            for h, lanes in enumerate(heads):
                st, dst = sts_ref[0, c, h], dst_ref[h]
                dstb = dst.astype(BF16)
                dqg.append(_dot(dob[:, lanes], st.astype(BF16)))
                dv_st.append(_dot_nt(kd[:, lanes], dstb))
                dkd.append(_dot(vb[:, lanes], dstb))
                da.append(jnp.broadcast_to(jnp.sum(dst * st, axis=0, keepdims=True), (SUB_CHUNK, HEAD_A)))
                dst_ref[h] = dst * a[0:1, lanes] + _dot_tn(dob[:, lanes], qg[:, lanes])
            dqg, dkd, da, dv_st = [jnp.concatenate(p, axis=1) for p in (dqg, dkd, da, dv_st)]
            d_gsum = qs * dqs - kk * dkk + dqg * qg_f - dkd * kd_f
            d_glast = jnp.sum(dkd * kd_f, axis=0, keepdims=True) + da * a
            dlf_ref[rows, :] = jnp.dot(tri_t, d_gsum, precision=lax.Precision.HIGHEST,
                                       preferred_element_type=F32) + d_glast
            dqs_ref[rows, :] = dqs + dqg * eg
            dkk_ref[rows, :] = dkk + dkd * ekd
            dp_ref[rows, 2 * A_WIDTH:3 * A_WIDTH] = (dv + dv_st).astype(BF16)
            return carry

        lax.fori_loop(0, n_sub, back, 0, unroll=2)
        sig, sq, qraw = blk["sig"], blk["sq"], blk["qraw"]
        d_f = dlf_ref[...] / blk["f"] - dkk_ref[...]
        dlb_ref[0:1, :] += jnp.sum(d_f * (1.0 - sig), axis=0, keepdims=True)
        dp_ref[:, 0:A_WIDTH] = (dqs_ref[...] * (sq * (1.0 + qraw * (1.0 - sq)))).astype(BF16)
        dp_ref[:, A_WIDTH:2 * A_WIDTH] = (d_f * (1.0 - lb) * sig * (1.0 - sig)).astype(BF16)

    def rev(s):
        return n_tb - 1 - s

    def col(k):
        return pl.BlockSpec((tb, A_WIDTH), lambda b, s, k=k: (b * n_tb + rev(s), k))

    acc8 = pl.BlockSpec((SUBLANES, A_WIDTH), lambda b, s: (0, 0))
    return pl.pallas_call(
        body, name=name, grid=(n_seq, n_tb),
        in_specs=[col(0), col(1), col(2), col(3), col(0), col(0),
                  pl.BlockSpec((1, n_sub, A_HEADS, HEAD_A, HEAD_A), lambda b, s: (b * n_tb + rev(s), 0, 0, 0, 0)),
                  pl.BlockSpec((3, A_WIDTH), lambda b, s: (0, 0)), pl.BlockSpec((1, A_WIDTH), lambda b, s: (0, 0))],
        out_specs=[pl.BlockSpec((tb, 4 * A_WIDTH), lambda b, s: (b * n_tb + rev(s), 0)), acc8, acc8],
        out_shape=[jax.ShapeDtypeStruct((t, 4 * A_WIDTH), BF16)] + [jax.ShapeDtypeStruct((SUBLANES, A_WIDTH), F32)] * 2,
        scratch_shapes=[pltpu.VMEM((A_HEADS, HEAD_A, HEAD_A), F32)] + [pltpu.VMEM((tb, A_WIDTH), F32)] * 10,
        compiler_params=_params(("arbitrary", "arbitrary")),
    )(proj, proj, proj, proj, dcat, pre, states, lb_table, a_norm)


GMLP_ROWS = 512


def _gmlp_chunk(ub, vb, ln_g, ln_b, ws, bias):
    u = [_gelu(a) for a in ub]
    v = [_gelu(a) for a in vb]
    mu = sum(jnp.sum(a, axis=-1, keepdims=True) for a in v) * (1.0 / B_WIDTH)
    cen = [a - mu for a in v]
    var = sum(jnp.sum(a * a, axis=-1, keepdims=True) for a in cen) * (1.0 / B_WIDTH)
    inv = lax.rsqrt(var + EPS)
    r = lax.broadcasted_iota(jnp.int32, (B_CHUNK, B_CHUNK), 0)
    c = lax.broadcasted_iota(jnp.int32, (B_CHUNK, B_CHUNK), 1)
    outs = []
    for g in range(B_GROUPS):
        vn = (cen[g] * inv * ln_g[g] + ln_b[g]).astype(BF16)
        wm = jnp.where(c <= r, ws[g], 0.0).astype(BF16)
        outs.append(u[g] * (_dot(wm, vn) + bias[g]))
    return outs


def _lane_groups(ref, rows=slice(None)):
    return [ref[rows, g * LANES:(g + 1) * LANES] for g in range(B_GROUPS)]


def _gmlp_fwd(proj, ln_g, ln_b, ws, bias_t, name):
    t = proj.shape[0]
    tm = GMLP_ROWS

    def body(u_ref, v_ref, lg_ref, lb_ref, ws_ref, bt_ref, o_ref):
        for ch in range(tm // B_CHUNK):
            rows = slice(ch * B_CHUNK, (ch + 1) * B_CHUNK)
            outs = _gmlp_chunk(_lane_groups(u_ref, rows), _lane_groups(v_ref, rows), _lane_groups(lg_ref),
                               _lane_groups(lb_ref), [ws_ref[g] for g in range(B_GROUPS)],
                               [bt_ref[:, g:g + 1] for g in range(B_GROUPS)])
            for g in range(B_GROUPS):
                o_ref[rows, g * LANES:(g + 1) * LANES] = outs[g].astype(BF16)

    vec = pl.BlockSpec((1, B_WIDTH), lambda i: (0, 0))
    return pl.pallas_call(
        body, name=name, grid=(t // tm,),
        in_specs=[pl.BlockSpec((tm, B_WIDTH), lambda i: (i, 4)), pl.BlockSpec((tm, B_WIDTH), lambda i: (i, 5)), vec, vec,
                  pl.BlockSpec((B_GROUPS, B_CHUNK, B_CHUNK), lambda i: (0, 0, 0)),
                  pl.BlockSpec((B_CHUNK, B_GROUPS), lambda i: (0, 0))],
        out_specs=pl.BlockSpec((tm, B_WIDTH), lambda i: (i, 0)),
        out_shape=jax.ShapeDtypeStruct((t, B_WIDTH), BF16),
        compiler_params=_params(("parallel",)),
    )(proj, proj, ln_g, ln_b, ws, bias_t)


def _gmlp_bwd(proj, dcat, ln_g, ln_b, ws, bias_t, name):
    t = proj.shape[0]
    tm = GMLP_ROWS

    def body(u_ref, v_ref, do_ref, lg_ref, lb_ref, ws_ref, bt_ref, duv_ref, dlg_ref, dlb_ref, dws_ref, dbt_ref):
        @pl.when(pl.program_id(0) == 0)
        def _():
            dlg_ref[...] = jnp.zeros_like(dlg_ref)
            dlb_ref[...] = jnp.zeros_like(dlb_ref)
            dws_ref[...] = jnp.zeros_like(dws_ref)
            dbt_ref[...] = jnp.zeros_like(dbt_ref)

        for ch in range(tm // B_CHUNK):
            rows = slice(ch * B_CHUNK, (ch + 1) * B_CHUNK)
            _, vjp = jax.vjp(
                _gmlp_chunk, _lane_groups(u_ref, rows), _lane_groups(v_ref, rows), _lane_groups(lg_ref),
                _lane_groups(lb_ref), [ws_ref[g] for g in range(B_GROUPS)],
                [bt_ref[:, g:g + 1] for g in range(B_GROUPS)])
            du, dv, dlg, dlb, dw, dbt = vjp(_lane_groups(do_ref, rows))
            for g in range(B_GROUPS):
                lanes = slice(g * LANES, (g + 1) * LANES)
                duv_ref[rows, lanes] = du[g].astype(BF16)
                duv_ref[rows, B_WIDTH + g * LANES:B_WIDTH + (g + 1) * LANES] = dv[g].astype(BF16)
                dlg_ref[0:1, lanes] += dlg[g]
                dlb_ref[0:1, lanes] += dlb[g]
                dws_ref[g] += dw[g]
                dbt_ref[:, g:g + 1] += dbt[g]

    vec = pl.BlockSpec((1, B_WIDTH), lambda i: (0, 0))
    acc8 = pl.BlockSpec((SUBLANES, B_WIDTH), lambda i: (0, 0))
    ws_spec = pl.BlockSpec((B_GROUPS, B_CHUNK, B_CHUNK), lambda i: (0, 0, 0))
    bt_spec = pl.BlockSpec((B_CHUNK, B_GROUPS), lambda i: (0, 0))
    return pl.pallas_call(
        body, name=name, grid=(t // tm,),
        in_specs=[pl.BlockSpec((tm, B_WIDTH), lambda i: (i, 4)), pl.BlockSpec((tm, B_WIDTH), lambda i: (i, 5)),
                  pl.BlockSpec((tm, B_WIDTH), lambda i: (i, 1)), vec, vec, ws_spec, bt_spec],
        out_specs=[pl.BlockSpec((tm, 2 * B_WIDTH), lambda i: (i, 0)), acc8, acc8, ws_spec, bt_spec],
        out_shape=[jax.ShapeDtypeStruct((t, 2 * B_WIDTH), BF16), jax.ShapeDtypeStruct((SUBLANES, B_WIDTH), F32),
                   jax.ShapeDtypeStruct((SUBLANES, B_WIDTH), F32),
                   jax.ShapeDtypeStruct((B_GROUPS, B_CHUNK, B_CHUNK), F32),
                   jax.ShapeDtypeStruct((B_CHUNK, B_GROUPS), F32)],
        compiler_params=_params(("arbitrary",)),
    )(proj, proj, dcat, ln_g, ln_b, ws, bias_t)


QK_SCALE = 1.0 / math.sqrt(C_HEAD_DIM)
ATTN_UNROLL = 8
LANE_GROUPS = D_MODEL // LANES
Q_BLOCKS = SEQ // C_BLOCK


def _attn_window(i, d):
    sub_blocks = Q_BLOCKS // d
    q0 = pl.multiple_of(i * C_BLOCK, C_BLOCK)
    k0 = pl.multiple_of(jnp.maximum(i - 1, 0) * C_BLOCK, C_BLOCK)
    key = k0 + lax.broadcasted_iota(jnp.int32, (C_BLOCK, 2 * C_BLOCK), 1)
    dist = (q0 + lax.broadcasted_iota(jnp.int32, (C_BLOCK, 2 * C_BLOCK), 0)) - key
    own_subsequence = (key >= q0) | (i % sub_blocks > 0)
    return pl.ds(q0, C_BLOCK), pl.ds(k0, 2 * C_BLOCK), (dist >= 0) & (dist <= C_BLOCK) & own_subsequence


def _head_masks():
    lane = lax.broadcasted_iota(jnp.int32, (C_BLOCK, LANES), 1)
    return [lane < C_HEAD_DIM, lane >= C_HEAD_DIM]


def _flat_spec(col_of):
    return pl.BlockSpec((1, SEQ, LANES), lambda b, g: (b, 0, col_of(g)))


def _put_heads(tile, g, col0, col1):
    lane = lax.broadcasted_iota(jnp.int32, tile.shape, 1)
    return jnp.where(lane == 2 * g, col0, jnp.where(lane == 2 * g + 1, col1, tile))


def _get_head(tile, h):
    lane = lax.broadcasted_iota(jnp.int32, tile.shape, 1)
    return jnp.sum(jnp.where(lane == h, tile, 0.0), axis=1, keepdims=True)


PER_HEAD_SPEC = pl.BlockSpec((1, SEQ, LANES), lambda b, g: (b, 0, 0))


def _attn_branch_fwd(qkv, name):
    n_seq, d, l, _ = qkv.shape
    flat = qkv.reshape(n_seq, SEQ, ODD_IN)

    def body(q_ref, k_ref, v_ref, o_ref, m_ref, l_ref):
        heads = _head_masks()
        g = pl.program_id(1)

        @pl.when(g == 0)
        def _():
            m_ref[...] = jnp.zeros_like(m_ref)
            l_ref[...] = jnp.zeros_like(l_ref)

        def block(i, carry):
            rows, keys, mask = _attn_window(i, d)
            q, k, v = q_ref[0, rows, :], k_ref[0, keys, :], v_ref[0, keys, :]
            res = []
            for hm in heads:
                s = jnp.where(mask, _dot_nt(jnp.where(hm, q, 0), k) * QK_SCALE, NEG)
                m = jnp.max(s, axis=-1, keepdims=True)
                p = jnp.exp(s - m)
                res.append((_dot(p.astype(BF16), v), m, jnp.sum(p, axis=-1, keepdims=True)))
            o_ref[0, rows, :] = jnp.where(heads[0], res[0][0], res[1][0])
            m_ref[0, rows, :] = _put_heads(m_ref[0, rows, :], g, res[0][1], res[1][1])
            l_ref[0, rows, :] = _put_heads(l_ref[0, rows, :], g, res[0][2], res[1][2])
            return carry

        lax.fori_loop(0, Q_BLOCKS, block, 0, unroll=ATTN_UNROLL)

    o, m, l_sum = pl.pallas_call(
        body, name=name, grid=(n_seq, LANE_GROUPS),
        in_specs=[_flat_spec(lambda g: g), _flat_spec(lambda g: LANE_GROUPS + g),
                  _flat_spec(lambda g: 2 * LANE_GROUPS + g)],
        out_specs=[_flat_spec(lambda g: g), PER_HEAD_SPEC, PER_HEAD_SPEC],
        out_shape=[jax.ShapeDtypeStruct((n_seq, SEQ, D_MODEL), F32)] + [jax.ShapeDtypeStruct((n_seq, SEQ, LANES), F32)] * 2,
        compiler_params=_params(("parallel", "arbitrary")),
    )(flat, flat, flat)
    return [o.reshape(n_seq, d, l, D_MODEL), m.reshape(n_seq, d, l, LANES), l_sum.reshape(n_seq, d, l, LANES)]


def _attn_merge(branches, name):
    n_seq = branches[0][0].shape[0]
    t = n_seq * SEQ
    tm = MERGE_TILE

    def body(*refs):
        ins = refs[:9]
        o_ref, ob_ref, lse_ref = refs[9:12]
        nat = refs[12:]
        for b, d in enumerate(C_DILATIONS[1:]):
            for k in range(3):
                _load_dilated(ins[3 + 3 * b + k], d, nat[3 * b + k])
        ms = [ins[1][0, 0], nat[1][0], nat[4][0]]
        ls = [ins[2][0, 0], nat[2][0], nat[5][0]]
        m_all = jnp.maximum(jnp.maximum(ms[0], ms[1]), ms[2])
        ws = [jnp.exp(ms[b] - m_all) for b in range(3)]
        lane = lax.broadcasted_iota(jnp.int32, m_all.shape, 1)
        total = jnp.where(lane < C_HEADS, ws[0] * ls[0] + ws[1] * ls[1] + ws[2] * ls[2], 1.0)
        lse_ref[...] = m_all + jnp.log(total)
        first_head = lane < C_HEAD_DIM
        for p in range(LANE_GROUPS):
            lanes = slice(p * LANES, (p + 1) * LANES)
            spread = lambda c: jnp.where(first_head, c[:, 2 * p:2 * p + 1], c[:, 2 * p + 1:2 * p + 2])
            os_ = [ins[0][0, 0, :, lanes], nat[0][p], nat[3][p]]
            o = (spread(ws[0]) * os_[0] + spread(ws[1]) * os_[1] + spread(ws[2]) * os_[2]) / spread(total)
            o_ref[:, lanes] = o
            ob_ref[:, lanes] = o.astype(BF16)

    row = pl.BlockSpec((tm, D_MODEL), lambda i: (i, 0))
    flat = [a for br in branches for a in br]
    in_specs = []
    for wide, narrow in zip(_dilated_specs(tm, D_MODEL, lambda: 0), _dilated_specs(tm, LANES, lambda: 0)):
        in_specs += [wide, narrow, narrow]
    per_head = pltpu.VMEM((1, tm, LANES), F32)
    return pl.pallas_call(
        body, name=name, grid=(t // tm,), in_specs=in_specs,
        out_specs=[row, row, pl.BlockSpec((tm, LANES), lambda i: (i, 0))],
        out_shape=[jax.ShapeDtypeStruct((t, D_MODEL), F32), jax.ShapeDtypeStruct((t, D_MODEL), BF16),
                   jax.ShapeDtypeStruct((t, LANES), F32)],
        scratch_shapes=[pltpu.VMEM((LANE_GROUPS, tm, LANES), F32), per_head, per_head] * 2,
        compiler_params=_params(("parallel",)),
    )(*flat)


def _attn_branch_bwd(qkv, dout, lse, delta, name):
    n_seq, d, l, _ = qkv.shape
    flat = lambda a: a.reshape(n_seq, SEQ, a.shape[-1])

    def body(q_ref, k_ref, v_ref, do_ref, lse_nat_ref, dl_nat_ref, dq_ref, dk_ref, dv_ref, lse_ref, dl_ref,
             dkt_ref, dvt_ref):
        heads = _head_masks()
        g = pl.program_id(1)
        dkt_ref[...] = jnp.zeros_like(dkt_ref)
        dvt_ref[...] = jnp.zeros_like(dvt_ref)
        for nat_ref, dst_ref in ((lse_nat_ref, lse_ref), (dl_nat_ref, dl_ref)):
            for r in range(d):
                rows = pl.ds(r, l, stride=d) if d > 1 else slice(None)
                dst_ref[r * l:(r + 1) * l, :] = nat_ref.at[0][rows, :]

        def block(i, carry):
            rows, keys, mask = _attn_window(i, d)
            q, do = q_ref[0, rows, :], do_ref[0, rows, :]
            k, v = k_ref[0, keys, :], v_ref[0, keys, :]
            lse_b, dl_b = lse_ref[rows, :], dl_ref[rows, :]
            dq, dk, dv = [], None, None
            for hh, hm in enumerate(heads):
                qh, doh = jnp.where(hm, q, 0), jnp.where(hm, do, 0)
                s = jnp.where(mask, _dot_nt(qh, k) * QK_SCALE, NEG)
                p = jnp.exp(s - _get_head(lse_b, 2 * g + hh))
                ds = (p * (_dot_nt(doh, v) - _get_head(dl_b, 2 * g + hh)) * QK_SCALE).astype(BF16)
                dq.append(_dot(ds, k))
                dk_h, dv_h = _dot_tn(qh, ds), _dot_tn(doh, p.astype(BF16))
                dk = dk_h if dk is None else dk + dk_h
                dv = dv_h if dv is None else dv + dv_h
            dq_ref[0, rows, :] = jnp.where(heads[0], dq[0], dq[1]).astype(BF16)
            dkt_ref[:, keys] += dk
            dvt_ref[:, keys] += dv
            return carry

        lax.fori_loop(0, Q_BLOCKS, block, 0, unroll=ATTN_UNROLL)
        for c in range(SEQ // ROW_TILE):
            rows = slice(c * ROW_TILE, (c + 1) * ROW_TILE)
            dk_ref[0, rows, :] = dkt_ref[:, rows].T.astype(BF16)
            dv_ref[0, rows, :] = dvt_ref[:, rows].T.astype(BF16)

    act = _flat_spec(lambda g: g)
    outs = pl.pallas_call(
        body, name=name, grid=(n_seq, LANE_GROUPS),
        in_specs=[_flat_spec(lambda g: g), _flat_spec(lambda g: LANE_GROUPS + g),
                  _flat_spec(lambda g: 2 * LANE_GROUPS + g), act, PER_HEAD_SPEC, PER_HEAD_SPEC],
        out_specs=[act] * 3,
        out_shape=[jax.ShapeDtypeStruct((n_seq, SEQ, D_MODEL), BF16)] * 3,
        scratch_shapes=[pltpu.VMEM((SEQ, LANES), F32)] * 2 + [pltpu.VMEM((LANES, SEQ), F32)] * 2,
        compiler_params=_params(("parallel", "parallel")),
    )(flat(qkv), flat(qkv), flat(qkv), flat(dout), lse, delta)
    return [o.reshape(n_seq, d, l, D_MODEL) for o in outs]


def _attn_combine_bwd(grads, rope, name):
    n_seq = grads[0][0].shape[0]
    t = n_seq * SEQ
    tm = MERGE_TILE

    def body(*refs):
        c_ref, s_ref, o_ref, nat4_ref, nat16_ref = refs[9:]
        for sec in range(3):
            _load_dilated(refs[3 + sec], 4, nat4_ref)
            _load_dilated(refs[6 + sec], 16, nat16_ref)
            for p in range(LANE_GROUPS):
                blk = refs[sec][0, 0, :, p * LANES:(p + 1) * LANES] + nat4_ref[p] + nat16_ref[p]
                if sec < 2:
                    blk = blk * c_ref[...] - _swap_halves(blk) * s_ref[...]
                o_ref[:, sec * D_MODEL + p * LANES:sec * D_MODEL + (p + 1) * LANES] = blk.astype(BF16)

    tab = pl.BlockSpec((tm, LANES), lambda i: (i, 0))
    flat = [a for br in grads for a in br]
    in_specs = []
    for spec in _dilated_specs(tm, D_MODEL, lambda: 0):
        in_specs += [spec] * 3
    return pl.pallas_call(
        body, name=name, grid=(t // tm,), in_specs=in_specs + [tab, tab],
        out_specs=pl.BlockSpec((tm, ODD_IN), lambda i: (i, 0)),
        out_shape=jax.ShapeDtypeStruct((t, ODD_IN), BF16),
        scratch_shapes=[pltpu.VMEM((LANE_GROUPS, tm, LANES), F32)] * 2,
        compiler_params=_params(("parallel",)),
    )(*flat, *rope)


def _adamw(w, g, m, v):
    m = ADAM_B1 * m + (1.0 - ADAM_B1) * g
    v = ADAM_B2 * v + (1.0 - ADAM_B2) * jnp.square(g)
    m_hat = m / (1.0 - ADAM_B1 ** ADAM_STEP)
    v_hat = v / (1.0 - ADAM_B2 ** ADAM_STEP)
    delta = -ADAM_LR * (m_hat / (jnp.sqrt(v_hat) + ADAM_EPS) + ADAM_WD * w)
    return delta, m, v


def _adamw_sharded(parts, w, m, v, after, name):
    n_layers, rows, cols = w.shape
    tr = min(rows, 256)

    def body(*refs):
        p_refs = refs[:n_layers]
        w_ref, m_ref, v_ref, _, g_ref, d_ref, mo_ref, vo_ref = refs[n_layers:]
        layer = pl.program_id(0)
        g = None
        for l, p_ref in enumerate(p_refs):
            g_l = p_ref[0].astype(F32)
            for s in range(1, N_DEV):
                g_l = g_l + p_ref[s].astype(F32)
            g = g_l if g is None else jnp.where(layer == l, g_l, g)
        delta, mn, vn = _adamw(w_ref[0], g, m_ref[0], v_ref[0])
        g_ref[0] = g
        d_ref[0] = delta
        mo_ref[0] = mn
        vo_ref[0] = vn

    def part_spec(l):
        return pl.BlockSpec((N_DEV, tr, cols), lambda a, i: (0, jnp.where(a == l, i, 0), 0))

    row = pl.BlockSpec((1, tr, cols), lambda a, i: (a, i, 0))
    return pl.pallas_call(
        body, name=name, grid=(n_layers, rows // tr),
        in_specs=[part_spec(l) for l in range(n_layers)] + [row, row, row, pl.BlockSpec(memory_space=pl.ANY)],
        out_specs=[row] * 4, out_shape=[jax.ShapeDtypeStruct(w.shape, F32)] * 4,
        compiler_params=_params(("arbitrary", "arbitrary")),
    )(*parts, w, m, v, after)


def _small_update(gathered, where, weights, moments_m, moments_v, lb_index, name):
    n = len(weights)
    n_g = len(gathered)

    def body(*refs):
        g_refs = refs[:n_g]
        w_refs, m_refs, v_refs = refs[n_g:n_g + n], refs[n_g + n:n_g + 2 * n], refs[n_g + 2 * n:n_g + 3 * n]
        outs = refs[n_g + 3 * n:]

        def total(k):
            array, rows, lanes = where[k]
            ref = g_refs[array]
            index = (slice(None),) * (len(ref.shape) - 1) if rows is None else (rows, lanes)
            acc = ref[(0,) + index]
            for s in range(1, N_DEV):
                acc = acc + ref[(s,) + index]
            return acc

        loss_rows = total(n)
        outs[0][...] = jnp.sum(jnp.sum(loss_rows, axis=1, keepdims=True), axis=0, keepdims=True)
        for k in range(n):
            part = total(k)
            if k == lb_index:
                dlb = jnp.sum(part, axis=0, keepdims=True)
                tab = w_refs[k][...]
                e = jnp.exp(tab - jnp.max(tab, axis=0, keepdims=True))
                p = e / jnp.sum(e, axis=0, keepdims=True)
                first = lax.broadcasted_iota(jnp.int32, p.shape, 0) == 0
                grads = [(slice(None), p * (jnp.where(first, dlb, 0.0) - p[0:1, :] * dlb))]
            elif part.shape == w_refs[k].shape:
                grads = [(slice(None), part)]
            else:
                grads = [(slice(l, l + 1), jnp.sum(part[l * SUBLANES:(l + 1) * SUBLANES], axis=0, keepdims=True))
                         for l in range(w_refs[k].shape[0])]
            for rows, g in grads:
                delta, mn, vn = _adamw(w_refs[k][rows], g, m_refs[k][rows], v_refs[k][rows])
                outs[1 + 4 * k][rows] = g
                outs[2 + 4 * k][rows] = delta
                outs[3 + 4 * k][rows] = mn
                outs[4 + 4 * k][rows] = vn

    vmem = pl.BlockSpec(memory_space=pltpu.VMEM)
    out_shape = [jax.ShapeDtypeStruct((1, 1), F32)]
    for w in weights:
        out_shape += [jax.ShapeDtypeStruct(w.shape, F32)] * 4
    args = list(gathered) + list(weights) + list(moments_m) + list(moments_v)
    return pl.pallas_call(
        body, name=name, in_specs=[vmem] * len(args), out_specs=[vmem] * len(out_shape), out_shape=out_shape,
        compiler_params=pltpu.CompilerParams(vmem_limit_bytes=VMEM_LIMIT),
    )(*args)


def kernel(x, positions, norm_mix_pre, norm_mix_post, norm_ffn_pre, norm_ffn_post, w_in_even, lb_table, a_norm, b_ln_g, b_ln_b, b_ws, b_bias, w_out_even, w_in_odd, w_out_odd, w_ff1, w_ff2, loss_target, m_norm_mix_pre, m_norm_mix_post, m_norm_ffn_pre, m_norm_ffn_post, m_w_in_even, m_lb_table, m_a_norm, m_b_ln_g, m_b_ln_b, m_b_ws, m_b_bias, m_w_out_even, m_w_in_odd, m_w_out_odd, m_w_ff1, m_w_ff2, v_norm_mix_pre, v_norm_mix_post, v_norm_ffn_pre, v_norm_ffn_post, v_w_in_even, v_lb_table, v_a_norm, v_b_ln_g, v_b_ln_b, v_b_ws, v_b_bias, v_w_out_even, v_w_in_odd, v_w_out_odd, v_w_ff1, v_w_ff2):
    n_seq = x.shape[0]
    t = n_seq * SEQ
    x0 = x.reshape(t, D_MODEL)
    target = loss_target.reshape(t, D_MODEL)

    me = _my_slot().astype(jnp.int32).reshape(1)

    order = ["in_e", "out_e", "ff1_0", "ff2_0", "in_o", "out_o", "ff1_1", "ff2_1"]
    shards = dict(in_e=w_in_even[0], out_e=w_out_even[0], in_o=w_in_odd[0], out_o=w_out_odd[0],
                  ff1_0=w_ff1[0], ff1_1=w_ff1[1], ff2_0=w_ff2[0], ff2_1=w_ff2[1])
    by_columns = ("in_e", "in_o", "ff1_0", "ff1_1")
    lands = [_place_own_columns(shards[k], me, "place_" + k) if k in by_columns
             else _place_own(shards[k], me, "place_" + k, False) for k in order]
    g_send, g_recv, lands, _, g_token = _exchange_start(lands, [None] * len(order), "gather_start")

    def get_w(keys, after):
        ks = [order.index(k) for k in keys]
        return _exchange_wait([lands[k] for k in ks], [None] * len(ks), [g_send[k] for k in ks],
                              [g_recv[k] for k in ks], after, "gather_wait_" + keys[0])

    sent = {}

    def put_g(group, blocks):
        keys = list(blocks)
        own = [_place_own(blocks[k], me, "own_" + k, True) for k in keys]
        send_sems, recv_sems, own, srcs, token = _exchange_start(own, [blocks[k] for k in keys], "scatter_start_" + group)
        sent[group] = (keys, own, srcs, send_sems, recv_sems)
        return token

    rope = _rope_tables(positions)
    bias_t = b_bias[0].T
    grads = _local_step(x0, target, rope, norm_mix_pre, norm_mix_post, norm_ffn_pre, norm_ffn_post, lb_table,
                        a_norm, b_ln_g, b_ln_b, b_ws[0], bias_t, get_w, put_g, g_token)
    (dx0, loss_part, dg_mix_pre, dg_mix_post, dg_ffn_pre, dg_ffn_post, d_lb, d_a_norm, d_ln_g, d_ln_b, d_ws,
     d_bias_t) = grads

    packed = jnp.concatenate([dg_mix_pre, dg_mix_post, dg_ffn_pre, dg_ffn_post,
                              jnp.concatenate([d_lb, d_a_norm], axis=1), jnp.concatenate([d_ln_g, d_ln_b], axis=1),
                              loss_part], axis=0)
    small_lands = [_place_own(a, me, "own_small%d" % k, False, F32) for k, a in enumerate((packed, d_ws, d_bias_t))]
    s_send, s_recv, small_lands, _, after = _exchange_start(small_lands, [None] * 3, "gather_small_start")

    big = dict(w_in_even=(["in_e"], w_in_even, m_w_in_even, v_w_in_even),
               w_out_even=(["out_e"], w_out_even, m_w_out_even, v_w_out_even),
               w_in_odd=(["in_o"], w_in_odd, m_w_in_odd, v_w_in_odd),
               w_out_odd=(["out_o"], w_out_odd, m_w_out_odd, v_w_out_odd),
               w_ff1=(["ff1_0", "ff1_1"], w_ff1, m_w_ff1, v_w_ff1), w_ff2=(["ff2_0", "ff2_1"], w_ff2, m_w_ff2, v_w_ff2))
    recv, big_out = {}, {}
    for groups, names in ((("ffn1", "ffn0"), ("w_ff1", "w_ff2")), (("mix1",), ("w_in_odd", "w_out_odd")),
                          (("mix0",), ("w_in_even", "w_out_even"))):
        for group in groups:
            keys, own, srcs, send_sems, recv_sems = sent[group]
            recv.update(zip(keys, _exchange_wait(own, srcs, send_sems, recv_sems, after, "scatter_wait_" + group)))
        for nm in names:
            keys, w, m, v = big[nm]
            big_out[nm] = _adamw_sharded([recv[k] for k in keys], w, m, v, after, "adamw_" + nm)
            after = big_out[nm][0]
    big_out = [big_out[nm] for nm in ("w_in_even", "w_out_even", "w_in_odd", "w_out_odd", "w_ff1", "w_ff2")]
    gathered = _exchange_wait(small_lands, [None] * 3, s_send, s_recv, after, "gather_small_wait")
    rows8 = lambda k: slice(SUBLANES * k, SUBLANES * (k + 1))
    left, right, every = slice(0, A_WIDTH), slice(A_WIDTH, 2 * A_WIDTH), slice(None)
    where = [(0, slice(0, 16), every), (0, slice(16, 32), every), (0, slice(32, 48), every), (0, slice(48, 64), every),
             (0, rows8(8), left), (0, rows8(8), right), (0, rows8(9), left), (0, rows8(9), right),
             (1, None, None), (2, None, None), (0, rows8(10), every)]
    small_w = [norm_mix_pre, norm_mix_post, norm_ffn_pre, norm_ffn_post, lb_table, a_norm, b_ln_g, b_ln_b,
               b_ws[0], bias_t]
    small_m = [m_norm_mix_pre, m_norm_mix_post, m_norm_ffn_pre, m_norm_ffn_post, m_lb_table, m_a_norm, m_b_ln_g,
               m_b_ln_b, m_b_ws[0], m_b_bias[0].T]
    small_v = [v_norm_mix_pre, v_norm_mix_post, v_norm_ffn_pre, v_norm_ffn_post, v_lb_table, v_a_norm, v_b_ln_g,
               v_b_ln_b, v_b_ws[0], v_b_bias[0].T]
    small_out = _small_update(gathered, where, small_w, small_m, small_v, 4, "small_update")
    loss = small_out[0].reshape(())
    small = [small_out[1 + 4 * k:5 + 4 * k] for k in range(len(small_w))]
    small[8] = [a[None] for a in small[8]]
    small[9] = [a.T[None] for a in small[9]]

    per_weight = small[0:4] + [big_out[0]] + small[4:10] + big_out[1:6]
    grad_x = dx0.reshape(x.shape)
    out = [loss, grad_x]
    for kind in range(4):
        out += [p[kind] for p in per_weight]
    return tuple(out)


def _local_step(x0, target, rope, norm_mix_pre, norm_mix_post, norm_ffn_pre, norm_ffn_post, lb_table, a_norm,
                b_ln_g, b_ln_b, ws, bias_t, get_w, put_g, token):
    def gain(a, l, tok):
        return a[l:l + 1] if tok is None else a[l:l + 1] + tok[0:1, 0:1]

    full = lambda a: a.reshape(-1, D_MODEL)
    owners = lambda a: a.reshape((N_DEV, -1) + a.shape[1:])

    (g_in_e,) = get_w(["in_e"], token)
    proj, h_mix0 = _norm_inproj(x0, gain(norm_mix_pre, 0, token), g_in_e, "inproj_even")
    oa, pre_a, states = _hgrn2_fwd(proj, lb_table, a_norm, "hgrn2_fwd")
    ob = _gmlp_fwd(proj, b_ln_g, b_ln_b, ws, bias_t, "gmlp_fwd")
    w_out_e = full(get_w(["out_e"], ob)[0])
    x1, mix0 = _outproj([oa, ob], w_out_e, x0, gain(norm_mix_post, 0, None), "outproj_even")
    w1_0, w2_0 = get_w(["ff1_0", "ff2_0"], x1)
    w2_0 = full(w2_0)
    x2, y0, h_ffn0, r0 = _ffn_fwd(x1, gain(norm_ffn_pre, 0, None), w1_0, w2_0, gain(norm_ffn_post, 0, None), "ffn_fwd_0")
    (g_in_o,) = get_w(["in_o"], x2)
    *qkv, h_mix1 = _norm_inproj_rope(x2, gain(norm_mix_pre, 1, None), g_in_o, rope, "inproj_odd")
    branches = [_attn_branch_fwd(a, "attn_fwd_d%d" % d) for a, d in zip(qkv, C_DILATIONS)]
    attn, attn_b, lse = _attn_merge(branches, "attn_merge")
    w_out_o = full(get_w(["out_o"], attn_b)[0])
    x3, mix1 = _outproj([attn_b], w_out_o, x2, gain(norm_mix_post, 1, None), "outproj_odd")
    w1_1, w2_1 = get_w(["ff1_1", "ff2_1"], x3)
    w2_1 = full(w2_1)
    dx4, y1, h_ffn1, r1, loss_part = _ffn_fwd(x3, gain(norm_ffn_pre, 1, None), w1_1, w2_1, gain(norm_ffn_post, 1, None),
                                              "ffn_fwd_1", target)

    dx3, dy1, da1, dg_ffn_pre1, dg_ffn_post1 = _ffn_bwd(
        dx4, x3, y1, r1, gain(norm_ffn_pre, 1, None), w1_1, w2_1, gain(norm_ffn_post, 1, None), "ffn_bwd_1")
    gw_ff1_1 = _grad_w(h_ffn1, da1, True, "grad_w_ff1_1")
    gw_ff2_1 = _grad_w(r1, dy1, False, "grad_w_ff2_1")
    tok = put_g("ffn1", dict(ff1_1=gw_ff1_1, ff2_1=owners(gw_ff2_1)))
    *dattn, delta, dz1, dg_mix_post1 = _outproj_bwd_attn(dx3, mix1, gain(norm_mix_post, 1, tok), w_out_o, attn,
                                                  "outproj_bwd_odd")
    gw_out_o = _grad_w(attn_b, dz1, False, "grad_w_out_odd")
    per_seq = lambda a: a.reshape(-1, SEQ, LANES)
    grads_c = [_attn_branch_bwd(qkv[b], dattn[b], per_seq(lse), per_seq(delta), "attn_bwd_d%d" % d)
               for b, d in enumerate(C_DILATIONS)]
    dqkv = _attn_combine_bwd(grads_c, rope, "attn_combine_bwd")
    gw_in_o = _grad_w(h_mix1, dqkv, True, "grad_w_in_odd")
    tok = put_g("mix1", dict(out_o=owners(gw_out_o), in_o=gw_in_o))
    dx2, dg_mix_pre1 = _inproj_bwd(dqkv, g_in_o, dx3, x2, gain(norm_mix_pre, 1, tok), "inproj_bwd_odd")

    dx1, dy0, da0, dg_ffn_pre0, dg_ffn_post0 = _ffn_bwd(
        dx2, x1, y0, r0, gain(norm_ffn_pre, 0, None), w1_0, w2_0, gain(norm_ffn_post, 0, None), "ffn_bwd_0")
    gw_ff1_0 = _grad_w(h_ffn0, da0, True, "grad_w_ff1_0")
    gw_ff2_0 = _grad_w(r0, dy0, False, "grad_w_ff2_0")
    tok = put_g("ffn0", dict(ff1_0=gw_ff1_0, ff2_0=owners(gw_ff2_0)))
    dcat, dz0, dg_mix_post0 = _outproj_bwd(dx1, mix0, gain(norm_mix_post, 0, tok), w_out_e, "outproj_bwd_even")
    gw_out_e = jnp.concatenate([_grad_w(oa, dz0, False, "grad_w_out_even_a"),
                                _grad_w(ob, dz0, False, "grad_w_out_even_b")], axis=0)
    dqfig, d_lb, d_a_norm = _hgrn2_bwd(proj, dcat, pre_a, states, lb_table, a_norm, "hgrn2_bwd")
    duv, d_ln_g, d_ln_b, d_ws, d_bias_t = _gmlp_bwd(proj, dcat, b_ln_g, b_ln_b, ws, bias_t, "gmlp_bwd")
    dproj = jnp.concatenate([dqfig, duv], axis=1)
    gw_in_e = _grad_w(h_mix0, dproj, True, "grad_w_in_even")
    tok = put_g("mix0", dict(out_e=owners(gw_out_e), in_e=gw_in_e))
    dx0, dg_mix_pre0 = _inproj_bwd(dproj, g_in_e, dx1, x0, gain(norm_mix_pre, 0, tok), "inproj_bwd_even")

    layers = lambda a, b: jnp.concatenate([a, b], axis=0)
    return (dx0, loss_part, layers(dg_mix_pre0, dg_mix_pre1), layers(dg_mix_post0, dg_mix_post1),
            layers(dg_ffn_pre0, dg_ffn_pre1), layers(dg_ffn_post0, dg_ffn_post1),
            d_lb, d_a_norm, d_ln_g, d_ln_b, d_ws, d_bias_t)
```

```python
import functools
import math

import jax
import jax.numpy as jnp
from jax import lax
from jax.experimental import pallas as pl
from jax.experimental.pallas import tpu as pltpu

F32 = jnp.float32
BF16 = jnp.bfloat16
MESH = pl.DeviceIdType.MESH

N_DEV = 8
D_MODEL = 1024
SEQ = 2048
EPS = 1e-6
A_WIDTH = 512
A_HEADS = 4
HEAD_A = 128
B_WIDTH = 512
B_GROUPS = 4
B_CHUNK = 128
C_HEADS = 16
C_HEAD_DIM = 64
C_ROT_HALF = 8
ROPE_THETA = 500000.0
C_DILATIONS = (1, 4, 16)
C_BLOCK = 128
D_FF = 4096
EVEN_IN = 3072
ODD_IN = 3072

ADAM_LR = 0.001
ADAM_B1 = 0.9
ADAM_B2 = 0.999
ADAM_EPS = 1e-08
ADAM_WD = 0.01
ADAM_STEP = 10

LANES = 128
SUBLANES = 8
ROW_TILE = 512
PROJ_TILE = 1024
PROJ_COLS = 768
MERGE_TILE = 256
SUB_CHUNK = 16
HGRN_BLOCK = 256
NEG = -1e30
VMEM_LIMIT = 56 * 1024 * 1024


def _params(sem):
    return pltpu.CompilerParams(dimension_semantics=sem, vmem_limit_bytes=VMEM_LIMIT)


def _dot(a, b):
    return jnp.dot(a, b, preferred_element_type=F32)


def _dot_nt(a, b):
    return lax.dot_general(a, b, (((1,), (1,)), ((), ())), preferred_element_type=F32)


def _dot_tn(a, b):
    return lax.dot_general(a, b, (((0,), (0,)), ((), ())), preferred_element_type=F32)


def _rms(x, g):
    r = lax.rsqrt(jnp.mean(x * x, axis=-1, keepdims=True) + EPS)
    return x * r * g


def _rms_bwd(x, g, dy):
    r = lax.rsqrt(jnp.mean(x * x, axis=-1, keepdims=True) + EPS)
    dyg = dy * g
    dx = r * dyg - x * (r * r * r) * jnp.mean(x * dyg, axis=-1, keepdims=True)
    return dx, dy * x * r


def _rows8(v):
    return v.reshape(v.shape[0] // SUBLANES, SUBLANES, v.shape[1]).sum(axis=0)


def _sigmoid(x):
    return 1.0 / (1.0 + jnp.exp(-x))


def _gelu(x):
    return 0.5 * x * (1.0 + jnp.tanh(math.sqrt(2.0 / math.pi) * (x + 0.044715 * (x * x * x))))


def _acc_rows8(ref, val, first):
    @pl.when(first)
    def _():
        ref[...] = val

    @pl.when(jnp.logical_not(first))
    def _():
        ref[...] += val


def _my_slot():
    return 4 * lax.axis_index("x") + 2 * lax.axis_index("y") + lax.axis_index("c")


def _peer(r):
    x, y, c = lax.axis_index("x"), lax.axis_index("y"), lax.axis_index("c")
    px = 1 - x if (r >> 2) & 1 else x
    py = 1 - y if (r >> 1) & 1 else y
    pc = 1 - c if r & 1 else c
    return (px, py, pc), 4 * px + 2 * py + pc


HBM_SPEC = pl.BlockSpec(memory_space=pltpu.HBM)
SEM_SPEC = pl.BlockSpec(memory_space=pltpu.SEMAPHORE)
SPLIT_EFFECT = pltpu.SideEffectType.DATAFLOW_SIDE_EFFECTING


def _split_copies(land_ref, src_ref, send_sem, recv_sem):
    me = _my_slot()
    copies = []
    for r in range(1, N_DEV):
        peer, slot = _peer(r)
        src = _slot(land_ref, me) if src_ref is None else _slot(src_ref, slot)
        copies.append(pltpu.make_async_remote_copy(
            src_ref=src, dst_ref=_slot(land_ref, me), send_sem=send_sem, recv_sem=recv_sem,
            device_id=peer, device_id_type=MESH))
    return copies


def _slot(ref, s):
    if len(ref.shape) == 2:
        c = ref.shape[1] // N_DEV
        return ref.at[:, pl.ds(pl.multiple_of(s * c, LANES), c)]
    return ref.at[s]


def _exchange_start(lands, sources, name):
    n = len(lands)
    given = [s for s in sources if s is not None]
    arrays = list(lands) + given

    def body(*refs):
        land_refs, src_refs = refs[:n], list(refs[n:n + len(given)])
        sems = refs[len(arrays):len(arrays) + 2 * n]
        token = refs[-1]
        for k in range(n):
            src_ref = None if sources[k] is None else src_refs.pop(0)
            for copy in _split_copies(land_refs[k], src_ref, sems[k], sems[n + k]):
                copy.start()
        token[...] = jnp.zeros_like(token)

    outs = pl.pallas_call(
        body, name=name,
        out_shape=(pltpu.SemaphoreType.DMA(()),) * (2 * n) + tuple(pltpu.HBM(a.shape, a.dtype) for a in arrays)
        + (jax.ShapeDtypeStruct((SUBLANES, LANES), F32),),
        in_specs=[HBM_SPEC] * len(arrays),
        out_specs=(SEM_SPEC,) * (2 * n) + (HBM_SPEC,) * len(arrays) + (pl.BlockSpec(memory_space=pltpu.VMEM),),
        input_output_aliases={i: 2 * n + i for i in range(len(arrays))},
        compiler_params=pltpu.CompilerParams(has_side_effects=SPLIT_EFFECT),
    )(*[pltpu.with_memory_space_constraint(a, pltpu.HBM) for a in arrays])
    return list(outs[:n]), list(outs[n:2 * n]), list(outs[2 * n:3 * n]), list(outs[3 * n:-1]), outs[-1]


def _exchange_wait(lands, sources, send_sems, recv_sems, after, name):
    n = len(lands)
    given = [s for s in sources if s is not None]
    arrays = list(lands) + given

    def body(*refs):
        land_refs, src_refs = refs[:n], list(refs[n:n + len(given)])
        sems = refs[len(arrays):len(arrays) + 2 * n]
        for i in range(n):
            src_ref = None if sources[i] is None else src_refs.pop(0)
            copies = _split_copies(land_refs[i], src_ref, sems[i], sems[n + i])
            for copy in copies:
                copy.wait_recv()
            for copy in copies:
                copy.wait_send()

    outs = pl.pallas_call(
        body, name=name, out_shape=tuple(pltpu.HBM(a.shape, a.dtype) for a in arrays),
        in_specs=[HBM_SPEC] * len(arrays) + [SEM_SPEC] * (2 * n) + [pl.BlockSpec(memory_space=pl.ANY)],
        out_specs=(HBM_SPEC,) * len(arrays),
        input_output_aliases={i: i for i in range(len(arrays))},
        compiler_params=pltpu.CompilerParams(has_side_effects=SPLIT_EFFECT),
    )(*arrays, *send_sems, *recv_sems, after)
    return list(outs[:n])


def _place_own(a, me, name, own_block, dtype=BF16, after=None):
    shape = a.shape[1:] if own_block else a.shape
    cols = shape[-1]
    a3 = a.reshape((N_DEV if own_block else 1, -1, cols))
    rows = a3.shape[1]
    tr = min(rows, 512)

    def body(me_ref, a_ref, _, o_ref):
        o_ref[...] = a_ref[...].astype(dtype)

    grid_spec = pltpu.PrefetchScalarGridSpec(
        num_scalar_prefetch=1, grid=(rows // tr,),
        in_specs=[pl.BlockSpec((1, tr, cols), lambda i, me_ref: (me_ref[0] if own_block else 0, i, 0)),
                  pl.BlockSpec(memory_space=pl.ANY)],
        out_specs=pl.BlockSpec((1, tr, cols), lambda i, me_ref: (me_ref[0], i, 0)))
    out = pl.pallas_call(
        body, name=name, grid_spec=grid_spec, out_shape=jax.ShapeDtypeStruct((N_DEV, rows, cols), dtype),
        compiler_params=_params(("arbitrary",)),
    )(me, a3, a3 if after is None else after)
    return out.reshape((N_DEV,) + shape)


def _place_own_columns(a, me, name, after=None):
    rows, cols = a.shape
    tr = min(rows, 512)

    def body(me_ref, a_ref, _, o_ref):
        o_ref[...] = a_ref[...].astype(BF16)

    grid_spec = pltpu.PrefetchScalarGridSpec(
        num_scalar_prefetch=1, grid=(rows // tr,),
        in_specs=[pl.BlockSpec((tr, cols), lambda i, me_ref: (i, 0)), pl.BlockSpec(memory_space=pl.ANY)],
        out_specs=pl.BlockSpec((tr, cols), lambda i, me_ref: (i, me_ref[0])))
    return pl.pallas_call(
        body, name=name, grid_spec=grid_spec, out_shape=jax.ShapeDtypeStruct((rows, N_DEV * cols), BF16),
        compiler_params=_params(("arbitrary",)),
    )(me, a, a if after is None else after)


def _rope_tables(positions):
    inv = ROPE_THETA ** (-jnp.arange(C_ROT_HALF, dtype=F32) / C_ROT_HALF)
    ang = positions.reshape(-1)[:, None].astype(F32) * inv
    cos, sin = jnp.cos(ang), jnp.sin(ang)
    t = ang.shape[0]
    ones = jnp.ones((t, C_HEAD_DIM - 2 * C_ROT_HALF), F32)
    c_head = jnp.concatenate([cos, cos, ones], axis=1)
    s_head = jnp.concatenate([-sin, sin, 0.0 * ones], axis=1)
    return jnp.concatenate([c_head, c_head], axis=1), jnp.concatenate([s_head, s_head], axis=1)


def _swap_halves(x):
    lane = lax.broadcasted_iota(jnp.int32, x.shape, 1) % C_HEAD_DIM
    return jnp.where(lane < C_ROT_HALF, pltpu.roll(x, LANES - C_ROT_HALF, 1), pltpu.roll(x, C_ROT_HALF, 1))


def _norm_inproj(x, g, w, name):
    t = x.shape[0]
    n = w.shape[1]
    tm, tn = PROJ_TILE, PROJ_COLS

    def body(x_ref, g_ref, w_ref, o_ref, h_ref):
        @pl.when(pl.program_id(1) == 0)
        def _():
            h_ref[...] = _rms(x_ref[...], g_ref[...]).astype(BF16)

        o_ref[...] = _dot(h_ref[...], w_ref[...])

    return pl.pallas_call(
        body, name=name, grid=(t // tm, n // tn),
        in_specs=[pl.BlockSpec((tm, D_MODEL), lambda i, j: (i, 0)), pl.BlockSpec((1, D_MODEL), lambda i, j: (0, 0)),
                  pl.BlockSpec((D_MODEL, tn), lambda i, j: (0, j))],
        out_specs=[pl.BlockSpec((tm, tn), lambda i, j: (i, j)), pl.BlockSpec((tm, D_MODEL), lambda i, j: (i, 0))],
        out_shape=[jax.ShapeDtypeStruct((t, n), F32), jax.ShapeDtypeStruct((t, D_MODEL), BF16)],
        compiler_params=_params(("parallel", "arbitrary")),
    )(x, g, w)


def _dilated_specs(tm, width, col_of):
    per_seq = SEQ // tm
    specs = []
    for d in C_DILATIONS:
        specs.append(pl.BlockSpec(
            (1, d, tm // d, width), lambda i, *rest: (i // per_seq, 0, i % per_seq, col_of(*rest))))
    return specs


def _dilated_shapes(n_seq, cols, dtype):
    return [jax.ShapeDtypeStruct((n_seq, d, SEQ // d, cols), dtype) for d in C_DILATIONS]


def _store_dilated(src_ref, out_refs, dtype):
    groups, tm, _ = src_ref.shape
    for d, o_ref in zip(C_DILATIONS, out_refs):
        for r in range(d):
            rows = pl.ds(r, tm // d, stride=d) if d > 1 else slice(None)
            for p in range(groups):
                o_ref[0, r, :, p * LANES:(p + 1) * LANES] = src_ref.at[p][rows, :].astype(dtype)


def _load_dilated(in_ref, d, dst_ref):
    groups, tm, _ = dst_ref.shape
    for r in range(d):
        rows = pl.ds(r, tm // d, stride=d)
        for p in range(groups):
            dst_ref.at[p][rows, :] = in_ref[0, r, :, p * LANES:(p + 1) * LANES].astype(F32)


def _norm_inproj_rope(x, g, w, rope, name):
    t = x.shape[0]
    n = w.shape[1]
    tm, nb = PROJ_TILE, PROJ_COLS

    def body(x_ref, g_ref, w_ref, c_ref, s_ref, o1_ref, o4_ref, o16_ref, h_ref, tile_ref):
        j = pl.program_id(1)

        @pl.when(j == 0)
        def _():
            h_ref[...] = _rms(x_ref[...], g_ref[...]).astype(BF16)

        acc = _dot(h_ref[...], w_ref[...])
        for p in range(nb // LANES):
            blk = acc[:, p * LANES:(p + 1) * LANES]
            roped = blk * c_ref[...] + _swap_halves(blk) * s_ref[...]
            is_qk = (j * (nb // LANES) + p) < 2 * (D_MODEL // LANES)
            tile_ref[p] = jnp.where(is_qk, roped, blk)
        _store_dilated(tile_ref, (o1_ref, o4_ref, o16_ref), BF16)

    return pl.pallas_call(
        body, name=name, grid=(t // tm, n // nb),
        in_specs=[pl.BlockSpec((tm, D_MODEL), lambda i, j: (i, 0)), pl.BlockSpec((1, D_MODEL), lambda i, j: (0, 0)),
                  pl.BlockSpec((D_MODEL, nb), lambda i, j: (0, j)),
                  pl.BlockSpec((tm, LANES), lambda i, j: (i, 0)), pl.BlockSpec((tm, LANES), lambda i, j: (i, 0))],
        out_specs=_dilated_specs(tm, nb, lambda j: j) + [pl.BlockSpec((tm, D_MODEL), lambda i, j: (i, 0))],
        out_shape=_dilated_shapes(t // SEQ, n, BF16) + [jax.ShapeDtypeStruct((t, D_MODEL), BF16)],
        scratch_shapes=[pltpu.VMEM((nb // LANES, tm, LANES), F32)],
        compiler_params=_params(("parallel", "arbitrary")),
    )(x, g, w, *rope)


def _outproj(parts, w, x, g, name):
    t = x.shape[0]
    tm = PROJ_TILE
    n = len(parts)
    widths = [p.shape[1] for p in parts]

    def body(*refs):
        p_refs = refs[:n]
        w_ref, x_ref, g_ref, xo_ref, mix_ref = refs[n:]
        mix = None
        off = 0
        for p_ref, wd in zip(p_refs, widths):
            term = _dot(p_ref[...].astype(BF16), w_ref[off:off + wd, :])
            mix = term if mix is None else mix + term
            off += wd
        mix_ref[...] = mix
        xo_ref[...] = x_ref[...] + _rms(mix, g_ref[...])

    row = lambda i: (i, 0)
    return pl.pallas_call(
        body, name=name, grid=(t // tm,),
        in_specs=[pl.BlockSpec((tm, wd), row) for wd in widths] + [
            pl.BlockSpec((sum(widths), D_MODEL), lambda i: (0, 0)),
            pl.BlockSpec((tm, D_MODEL), row), pl.BlockSpec((1, D_MODEL), lambda i: (0, 0))],
        out_specs=[pl.BlockSpec((tm, D_MODEL), row)] * 2,
        out_shape=[jax.ShapeDtypeStruct((t, D_MODEL), F32)] * 2,
        compiler_params=_params(("parallel",)),
    )(*parts, w, x, g)


def _outproj_bwd(dx, mix, g, w, name):
    t = dx.shape[0]
    tm = PROJ_TILE
    k = w.shape[0]

    def body(dx_ref, mix_ref, g_ref, w_ref, dcat_ref, dz_ref, dg_ref):
        dz, dgr = _rms_bwd(mix_ref[...], g_ref[...], dx_ref[...])
        dzb = dz.astype(BF16)
        dz_ref[...] = dzb
        dcat_ref[...] = _dot_nt(dzb, w_ref[...])
        _acc_rows8(dg_ref, _rows8(dgr), pl.program_id(0) == 0)

    row = lambda i: (i, 0)
    return pl.pallas_call(
        body, name=name, grid=(t // tm,),
        in_specs=[pl.BlockSpec((tm, D_MODEL), row), pl.BlockSpec((tm, D_MODEL), row),
                  pl.BlockSpec((1, D_MODEL), lambda i: (0, 0)), pl.BlockSpec((k, D_MODEL), lambda i: (0, 0))],
        out_specs=[pl.BlockSpec((tm, k), row), pl.BlockSpec((tm, D_MODEL), row),
                   pl.BlockSpec((SUBLANES, D_MODEL), lambda i: (0, 0))],
        out_shape=[jax.ShapeDtypeStruct((t, k), F32), jax.ShapeDtypeStruct((t, D_MODEL), BF16),
                   jax.ShapeDtypeStruct((SUBLANES, D_MODEL), F32)],
        compiler_params=_params(("arbitrary",)),
    )(dx, mix, g, w)


def _outproj_bwd_attn(dx, mix, g, w, out, name):
    t = dx.shape[0]
    tm = MERGE_TILE

    def body(dx_ref, mix_ref, g_ref, w_ref, out_ref, do1, do4, do16, dl_ref, dz_ref, dg_ref, tile_ref):
        dz, dgr = _rms_bwd(mix_ref[...], g_ref[...], dx_ref[...])
        dzb = dz.astype(BF16)
        dz_ref[...] = dzb
        _acc_rows8(dg_ref, _rows8(dgr), pl.program_id(0) == 0)
        dout = _dot_nt(dzb, w_ref[...])
        for p in range(LANE_GROUPS):
            tile_ref[p] = dout[:, p * LANES:(p + 1) * LANES]
        _store_dilated(tile_ref, (do1, do4, do16), BF16)
        column = lax.broadcasted_iota(jnp.int32, (D_MODEL, LANES), 0) // C_HEAD_DIM
        head = lax.broadcasted_iota(jnp.int32, (D_MODEL, LANES), 1)
        dl_ref[...] = jnp.dot(dout * out_ref[...], (column == head).astype(F32), precision=lax.Precision.HIGHEST,
                              preferred_element_type=F32)

    row = lambda i: (i, 0)
    n_seq = t // SEQ
    return pl.pallas_call(
        body, name=name, grid=(t // tm,),
        in_specs=[pl.BlockSpec((tm, D_MODEL), row), pl.BlockSpec((tm, D_MODEL), row),
                  pl.BlockSpec((1, D_MODEL), lambda i: (0, 0)), pl.BlockSpec((D_MODEL, D_MODEL), lambda i: (0, 0)),
                  pl.BlockSpec((tm, D_MODEL), row)],
        out_specs=_dilated_specs(tm, D_MODEL, lambda: 0) + [
            pl.BlockSpec((tm, LANES), row), pl.BlockSpec((tm, D_MODEL), row),
            pl.BlockSpec((SUBLANES, D_MODEL), lambda i: (0, 0))],
        out_shape=_dilated_shapes(n_seq, D_MODEL, BF16) + [
            jax.ShapeDtypeStruct((t, LANES), F32), jax.ShapeDtypeStruct((t, D_MODEL), BF16),
            jax.ShapeDtypeStruct((SUBLANES, D_MODEL), F32)],
        scratch_shapes=[pltpu.VMEM((LANE_GROUPS, tm, LANES), F32)],
        compiler_params=_params(("arbitrary",)),
    )(dx, mix, g, w, out)


def _inproj_bwd(dproj, w, dx, x, g, name):
    t = x.shape[0]
    n = w.shape[1]
    tm = ROW_TILE

    def body(dp_ref, w_ref, dx_ref, x_ref, g_ref, o_ref, dg_ref):
        dxn, dgr = _rms_bwd(x_ref[...], g_ref[...], _dot_nt(dp_ref[...], w_ref[...]))
        o_ref[...] = dx_ref[...] + dxn
        _acc_rows8(dg_ref, _rows8(dgr), pl.program_id(0) == 0)

    row = lambda i: (i, 0)
    return pl.pallas_call(
        body, name=name, grid=(t // tm,),
        in_specs=[pl.BlockSpec((tm, n), row), pl.BlockSpec((D_MODEL, n), lambda i: (0, 0)),
                  pl.BlockSpec((tm, D_MODEL), row), pl.BlockSpec((tm, D_MODEL), row),
                  pl.BlockSpec((1, D_MODEL), lambda i: (0, 0))],
        out_specs=[pl.BlockSpec((tm, D_MODEL), row), pl.BlockSpec((SUBLANES, D_MODEL), lambda i: (0, 0))],
        out_shape=[jax.ShapeDtypeStruct((t, D_MODEL), F32), jax.ShapeDtypeStruct((SUBLANES, D_MODEL), F32)],
        compiler_params=_params(("arbitrary",)),
    )(dproj, w, dx, x, g)


def _grad_w(a, b, col_blocks, name):
    t, k = a.shape
    n = b.shape[1]
    tk = min(k, 1024)
    per_owner = n // N_DEV
    tn = 2 * per_owner if col_blocks else min(n, 1024)

    def body(a_ref, b_ref, o_ref, at_ref):
        @pl.when(pl.program_id(1) == 0)
        def _():
            for c in range(t // ROW_TILE):
                rows = slice(c * ROW_TILE, (c + 1) * ROW_TILE)
                at_ref[:, rows] = a_ref[rows, :].T

        res = _dot(at_ref[...], b_ref[...]).astype(BF16)
        if col_blocks:
            o_ref[0] = res[:, :per_owner]
            o_ref[1] = res[:, per_owner:]
        else:
            o_ref[...] = res

    if col_blocks:
        out_spec = pl.BlockSpec((2, tk, per_owner), lambda i, j: (j, i, 0))
        out_shape = jax.ShapeDtypeStruct((N_DEV, k, per_owner), BF16)
    else:
        out_spec = pl.BlockSpec((tk, tn), lambda i, j: (i, j))
        out_shape = jax.ShapeDtypeStruct((k, n), BF16)
    return pl.pallas_call(
        body, name=name, grid=(k // tk, n // tn),
        in_specs=[pl.BlockSpec((t, tk), lambda i, j: (0, i)), pl.BlockSpec((t, tn), lambda i, j: (0, j))],
        out_specs=out_spec, out_shape=out_shape,
        scratch_shapes=[pltpu.VMEM((tk, t), BF16)],
        compiler_params=_params(("parallel", "arbitrary")),
    )(a, b)


FF_BLOCK = D_FF // N_DEV
FF_STEP = 1024
FF_STEPS = D_FF // FF_STEP


def _ffn_fwd(x, g_pre, w1, w2, g_post, name, target=None):
    t = x.shape[0]
    tm = PROJ_TILE

    def body(*refs):
        if target is None:
            x_ref, gp_ref, w1_ref, w2_ref, gq_ref, xo_ref, y_ref, h_ref, r_ref = refs
        else:
            x_ref, gp_ref, w1_ref, w2_ref, gq_ref, t_ref, xo_ref, y_ref, h_ref, r_ref, l_ref = refs
        i, j = pl.program_id(0), pl.program_id(1)

        @pl.when(j == 0)
        def _():
            h_ref[...] = _rms(x_ref[...], gp_ref[...]).astype(BF16)

        a = _dot(h_ref[...], w1_ref[...])
        r = jnp.square(jnp.maximum(a, 0.0)).astype(BF16)
        r_ref[...] = r
        term = _dot(r, w2_ref[...])

        @pl.when(j == 0)
        def _():
            y_ref[...] = term

        @pl.when(j > 0)
        def _():
            y_ref[...] += term

        @pl.when(j == FF_STEPS - 1)
        def _():
            x_new = x_ref[...] + _rms(y_ref[...], gq_ref[...])
            if target is None:
                xo_ref[...] = x_new
            else:
                diff = x_new - t_ref[...]
                xo_ref[...] = diff * (1.0 / D_MODEL)
                _acc_rows8(l_ref, _rows8(diff * diff) * (0.5 / D_MODEL), i == 0)

    row = lambda i, j: (i, 0)
    vec = pl.BlockSpec((1, D_MODEL), lambda i, j: (0, 0))
    in_specs = [pl.BlockSpec((tm, D_MODEL), row), vec, pl.BlockSpec((D_MODEL, FF_STEP), lambda i, j: (0, j)),
                pl.BlockSpec((FF_STEP, D_MODEL), lambda i, j: (j, 0)), vec]
    out_specs = [pl.BlockSpec((tm, D_MODEL), row)] * 3 + [pl.BlockSpec((tm, FF_STEP), lambda i, j: (i, j))]
    out_shape = [jax.ShapeDtypeStruct((t, D_MODEL), F32), jax.ShapeDtypeStruct((t, D_MODEL), F32),
                 jax.ShapeDtypeStruct((t, D_MODEL), BF16), jax.ShapeDtypeStruct((t, D_FF), BF16)]
    args = [x, g_pre, w1, w2, g_post]
    if target is not None:
        in_specs.append(pl.BlockSpec((tm, D_MODEL), row))
        out_specs.append(pl.BlockSpec((SUBLANES, D_MODEL), lambda i, j: (0, 0)))
        out_shape.append(jax.ShapeDtypeStruct((SUBLANES, D_MODEL), F32))
        args.append(target)
    return pl.pallas_call(
        body, name=name, grid=(t // tm, FF_STEPS), in_specs=in_specs, out_specs=out_specs, out_shape=out_shape,
        compiler_params=_params(("parallel" if target is None else "arbitrary", "arbitrary")),
    )(*args)


def _ffn_bwd(dxo, x, y, r, g_pre, w1, w2, g_post, name):
    t = x.shape[0]
    tm = ROW_TILE

    def body(dxo_ref, x_ref, y_ref, r_ref, gp_ref, w1_ref, w2_ref, gq_ref,
             dx_ref, dy_ref, da_ref, dgp_ref, dgq_ref, acc_ref):
        i, j = pl.program_id(0), pl.program_id(1)

        @pl.when(j == 0)
        def _():
            dy, dgr = _rms_bwd(y_ref[...], gq_ref[...], dxo_ref[...])
            dy_ref[...] = dy.astype(BF16)
            _acc_rows8(dgq_ref, _rows8(dgr), i == 0)

        dr = _dot_nt(dy_ref[...], w2_ref[...])
        da = (dr * (2.0 * jnp.sqrt(r_ref[...].astype(F32)))).astype(BF16)
        da_ref[...] = da
        term = _dot_nt(da, w1_ref[...])

        @pl.when(j == 0)
        def _():
            acc_ref[...] = term

        @pl.when(j > 0)
        def _():
            acc_ref[...] += term

        @pl.when(j == FF_STEPS - 1)
        def _():
            dxn, dgr = _rms_bwd(x_ref[...], gp_ref[...], acc_ref[...])
            dx_ref[...] = dxo_ref[...] + dxn
            _acc_rows8(dgp_ref, _rows8(dgr), i == 0)

    row = lambda i, j: (i, 0)
    vec = pl.BlockSpec((1, D_MODEL), lambda i, j: (0, 0))
    acc8 = pl.BlockSpec((SUBLANES, D_MODEL), lambda i, j: (0, 0))
    return pl.pallas_call(
        body, name=name, grid=(t // tm, FF_STEPS),
        in_specs=[pl.BlockSpec((tm, D_MODEL), row)] * 3 + [
            pl.BlockSpec((tm, FF_STEP), lambda i, j: (i, j)),
            vec, pl.BlockSpec((D_MODEL, FF_STEP), lambda i, j: (0, j)),
            pl.BlockSpec((FF_STEP, D_MODEL), lambda i, j: (j, 0)), vec],
        out_specs=[pl.BlockSpec((tm, D_MODEL), row), pl.BlockSpec((tm, D_MODEL), row),
                   pl.BlockSpec((tm, FF_STEP), lambda i, j: (i, j)), acc8, acc8],
        out_shape=[jax.ShapeDtypeStruct((t, D_MODEL), F32), jax.ShapeDtypeStruct((t, D_MODEL), BF16),
                   jax.ShapeDtypeStruct((t, D_FF), BF16),
                   jax.ShapeDtypeStruct((SUBLANES, D_MODEL), F32), jax.ShapeDtypeStruct((SUBLANES, D_MODEL), F32)],
        scratch_shapes=[pltpu.VMEM((tm, D_MODEL), F32)],
        compiler_params=_params(("arbitrary", "arbitrary")),
    )(dxo, x, y, r, g_pre, w1, w2, g_post)


def _lower_bound(table):
    e = jnp.exp(table - jnp.max(table, axis=0, keepdims=True))
    return e[0:1, :] / jnp.sum(e, axis=0, keepdims=True)


def _hgrn2_block(q_ref, f_ref, lb):
    tb = f_ref.shape[0]
    sig = _sigmoid(f_ref[...])
    f = lb + (1.0 - lb) * sig
    qraw = q_ref[...]
    sq = _sigmoid(qraw)
    r = lax.broadcasted_iota(jnp.int32, (tb, tb), 0)
    c = lax.broadcasted_iota(jnp.int32, (tb, tb), 1)
    same = (r // SUB_CHUNK) == (c // SUB_CHUNK)
    logf = jnp.log(f)
    gsum = jnp.dot((same & (c <= r)).astype(F32), logf, precision=lax.Precision.HIGHEST, preferred_element_type=F32)
    glast = jnp.dot(same.astype(F32), logf, precision=lax.Precision.HIGHEST, preferred_element_type=F32)
    return dict(sig=sig, f=f, kk=1.0 - f, qraw=qraw, sq=sq, qs=qraw * sq, gsum=gsum,
                eg=jnp.exp(gsum), ekd=jnp.exp(glast - gsum), a=jnp.exp(glast))


def _head_sums(x):
    parts = [jnp.broadcast_to(jnp.sum(x[:, h * HEAD_A:(h + 1) * HEAD_A], axis=1, keepdims=True), (x.shape[0], HEAD_A))
             for h in range(A_HEADS)]
    return jnp.concatenate(parts, axis=1)


def _hgrn2_intra(g, kk, qs, v):
    row = lax.broadcasted_iota(jnp.int32, g.shape, 0)
    o = _head_sums(qs * kk) * v
    for j in range(1, SUB_CHUNK):
        decay = jnp.exp(jnp.where(row >= j, g - pltpu.roll(g, j, 0), NEG))
        o = o + _head_sums(qs * pltpu.roll(kk, j, 0) * decay) * pltpu.roll(v, j, 0)
    return o


def _hgrn2_intra_bwd(g, kk, qs, v, do):
    row = lax.broadcasted_iota(jnp.int32, g.shape, 0)
    dsc = _head_sums(do * v)
    dqs, dkk, dv = dsc * kk, dsc * qs, _head_sums(qs * kk) * do
    for j in range(1, SUB_CHUNK):
        k_dn = pltpu.roll(kk, j, 0)
        decay = jnp.exp(jnp.where(row >= j, g - pltpu.roll(g, j, 0), NEG))
        d_score = _head_sums(do * pltpu.roll(v, j, 0)) * decay
        dqs = dqs + d_score * k_dn
        dkk = dkk + pltpu.roll(d_score * qs, SUB_CHUNK - j, 0)
        dv = dv + pltpu.roll(_head_sums(qs * k_dn * decay) * do, SUB_CHUNK - j, 0)
    return dqs, dkk, dv


def _hgrn2_fwd(proj, lb_table, a_norm, name):
    t = proj.shape[0]
    tb = HGRN_BLOCK
    n_tb = SEQ // tb
    n_seq = t // SEQ
    n_sub = tb // SUB_CHUNK

    def body(q_ref, f_ref, i_ref, g_ref, lbt_ref, an_ref, o_ref, pre_ref, sts_ref, st_ref,
             gs_ref, kk_ref, qs_ref, eg_ref, ekd_ref, a_ref):
        @pl.when(pl.program_id(1) == 0)
        def _():
            st_ref[...] = jnp.zeros_like(st_ref)

        an = an_ref[...]
        blk = _hgrn2_block(q_ref, f_ref, _lower_bound(lbt_ref[...]))
        for ref, key in ((gs_ref, "gsum"), (kk_ref, "kk"), (qs_ref, "qs"), (eg_ref, "eg"), (ekd_ref, "ekd"), (a_ref, "a")):
            ref[...] = blk[key]

        def step(c, carry):
            rows = pl.ds(pl.multiple_of(c * SUB_CHUNK, SUB_CHUNK), SUB_CHUNK)
            kk, qs, v = kk_ref[rows, :], qs_ref[rows, :], i_ref[rows, :]
            o = _hgrn2_intra(gs_ref[rows, :], kk, qs, v)
            qg, kd, vb = (qs * eg_ref[rows, :]).astype(BF16), (kk * ekd_ref[rows, :]).astype(BF16), v.astype(BF16)
            for h in range(A_HEADS):
                lanes = slice(h * HEAD_A, (h + 1) * HEAD_A)
                st = st_ref[h]
                sts_ref[0, c, h] = st
                o_h = o[:, lanes] + _dot_nt(qg[:, lanes], st.astype(BF16))
                st_ref[h] = st * a_ref[rows, lanes][0:1] + _dot_tn(vb[:, lanes], kd[:, lanes])
                pre_ref[rows, lanes] = o_h
                graw = g_ref[rows, lanes]
                o_ref[rows, lanes] = (_rms(o_h, an[:, lanes]) * (graw * _sigmoid(graw))).astype(BF16)
            return carry

        lax.fori_loop(0, n_sub, step, 0, unroll=2)

    def col(k):
        return pl.BlockSpec((tb, A_WIDTH), lambda b, s, k=k: (b * n_tb + s, k))

    out_rows = pl.BlockSpec((tb, A_WIDTH), lambda b, s: (b * n_tb + s, 0))
    return pl.pallas_call(
        body, name=name, grid=(n_seq, n_tb),
        in_specs=[col(0), col(1), col(2), col(3),
                  pl.BlockSpec((3, A_WIDTH), lambda b, s: (0, 0)), pl.BlockSpec((1, A_WIDTH), lambda b, s: (0, 0))],
        out_specs=[out_rows, out_rows,
                   pl.BlockSpec((1, n_sub, A_HEADS, HEAD_A, HEAD_A), lambda b, s: (b * n_tb + s, 0, 0, 0, 0))],
        out_shape=[jax.ShapeDtypeStruct((t, D_MODEL), BF16), jax.ShapeDtypeStruct((t, A_WIDTH), F32),
                   jax.ShapeDtypeStruct((n_seq * n_tb, n_sub, A_HEADS, HEAD_A, HEAD_A), F32)],
        scratch_shapes=[pltpu.VMEM((A_HEADS, HEAD_A, HEAD_A), F32)] + [pltpu.VMEM((tb, A_WIDTH), F32)] * 6,
        compiler_params=_params(("parallel", "arbitrary")),
    )(proj, proj, proj, proj, lb_table, a_norm)


def _hgrn2_bwd(proj, dcat, pre, states, lb_table, a_norm, name):
    t = proj.shape[0]
    tb = HGRN_BLOCK
    n_tb = SEQ // tb
    n_seq = t // SEQ
    n_sub = tb // SUB_CHUNK

    def body(q_ref, f_ref, i_ref, g_ref, do_ref, pre_ref, sts_ref, lbt_ref, an_ref, dp_ref, dlb_ref, dan_ref, dst_ref,
             gs_ref, kk_ref, qs_ref, eg_ref, ekd_ref, a_ref, dpre_ref, dlf_ref, dqs_ref, dkk_ref):
        b, s = pl.program_id(0), pl.program_id(1)

        @pl.when(s == 0)
        def _():
            dst_ref[...] = jnp.zeros_like(dst_ref)

        @pl.when((b == 0) & (s == 0))
        def _():
            dlb_ref[...] = jnp.zeros_like(dlb_ref)
            dan_ref[...] = jnp.zeros_like(dan_ref)

        lb = _lower_bound(lbt_ref[...])
        an = an_ref[...]
        heads = [slice(h * HEAD_A, (h + 1) * HEAD_A) for h in range(A_HEADS)]
        blk = _hgrn2_block(q_ref, f_ref, lb)
        for ref, key in ((gs_ref, "gsum"), (kk_ref, "kk"), (qs_ref, "qs"), (eg_ref, "eg"), (ekd_ref, "ekd"), (a_ref, "a")):
            ref[...] = blk[key]
        for h, lanes in enumerate(heads):
            graw, o = g_ref[:, lanes], pre_ref[:, lanes]
            sg = _sigmoid(graw)
            dout = do_ref[:, lanes]
            d_o, dgr = _rms_bwd(o, an[:, lanes], dout * (graw * sg))
            dan_ref[0:1, lanes] += jnp.sum(dgr, axis=0, keepdims=True)
            dp_ref[:, 3 * A_WIDTH + h * HEAD_A:3 * A_WIDTH + (h + 1) * HEAD_A] = (
                dout * _rms(o, an[:, lanes]) * (sg * (1.0 + graw * (1.0 - sg)))).astype(BF16)
            dpre_ref[:, lanes] = d_o

        tri_t = (lax.broadcasted_iota(jnp.int32, (SUB_CHUNK, SUB_CHUNK), 0)
                 <= lax.broadcasted_iota(jnp.int32, (SUB_CHUNK, SUB_CHUNK), 1)).astype(F32)

        def back(k, carry):
            c = n_sub - 1 - k
            rows = pl.ds(pl.multiple_of(c * SUB_CHUNK, SUB_CHUNK), SUB_CHUNK)
            g, kk, qs, v, d_o = gs_ref[rows, :], kk_ref[rows, :], qs_ref[rows, :], i_ref[rows, :], dpre_ref[rows, :]
            eg, ekd, a = eg_ref[rows, :], ekd_ref[rows, :], a_ref[rows, :]
            dqs, dkk, dv = _hgrn2_intra_bwd(g, kk, qs, v, d_o)
            qg_f, kd_f = qs * eg, kk * ekd
            qg, kd, vb, dob = qg_f.astype(BF16), kd_f.astype(BF16), v.astype(BF16), d_o.astype(BF16)
            dqg, dkd, da, dv_st = [], [], [], []
            for h, lanes in enumerate(heads):
                st, dst = sts_ref[0, c, h], dst_ref[h]
                dstb = dst.astype(BF16)
                dqg.append(_dot(dob[:, lanes], st.astype(BF16)))
                dv_st.append(_dot_nt(kd[:, lanes], dstb))
                dkd.append(_dot(vb[:, lanes], dstb))
                da.append(jnp.broadcast_to(jnp.sum(dst * st, axis=0, keepdims=True), (SUB_CHUNK, HEAD_A)))
                dst_ref[h] = dst * a[0:1, lanes] + _dot_tn(dob[:, lanes], qg[:, lanes])
            dqg, dkd, da, dv_st = [jnp.concatenate(p, axis=1) for p in (dqg, dkd, da, dv_st)]
            d_gsum = qs * dqs - kk * dkk + dqg * qg_f - dkd * kd_f
            d_glast = jnp.sum(dkd * kd_f, axis=0, keepdims=True) + da * a
            dlf_ref[rows, :] = jnp.dot(tri_t, d_gsum, precision=lax.Precision.HIGHEST,
                                       preferred_element_type=F32) + d_glast
            dqs_ref[rows, :] = dqs + dqg * eg
            dkk_ref[rows, :] = dkk + dkd * ekd
            dp_ref[rows, 2 * A_WIDTH:3 * A_WIDTH] = (dv + dv_st).astype(BF16)
            return carry

        lax.fori_loop(0, n_sub, back, 0, unroll=2)
        sig, sq, qraw = blk["sig"], blk["sq"], blk["qraw"]
        d_f = dlf_ref[...] / blk["f"] - dkk_ref[...]
        dlb_ref[0:1, :] += jnp.sum(d_f * (1.0 - sig), axis=0, keepdims=True)
        dp_ref[:, 0:A_WIDTH] = (dqs_ref[...] * (sq * (1.0 + qraw * (1.0 - sq)))).astype(BF16)
        dp_ref[:, A_WIDTH:2 * A_WIDTH] = (d_f * (1.0 - lb) * sig * (1.0 - sig)).astype(BF16)

    def rev(s):
        return n_tb - 1 - s

    def col(k):
        return pl.BlockSpec((tb, A_WIDTH), lambda b, s, k=k: (b * n_tb + rev(s), k))

    acc8 = pl.BlockSpec((SUBLANES, A_WIDTH), lambda b, s: (0, 0))
    return pl.pallas_call(
        body, name=name, grid=(n_seq, n_tb),
        in_specs=[col(0), col(1), col(2), col(3), col(0), col(0),
                  pl.BlockSpec((1, n_sub, A_HEADS, HEAD_A, HEAD_A), lambda b, s: (b * n_tb + rev(s), 0, 0, 0, 0)),
                  pl.BlockSpec((3, A_WIDTH), lambda b, s: (0, 0)), pl.BlockSpec((1, A_WIDTH), lambda b, s: (0, 0))],
        out_specs=[pl.BlockSpec((tb, 4 * A_WIDTH), lambda b, s: (b * n_tb + rev(s), 0)), acc8, acc8],
        out_shape=[jax.ShapeDtypeStruct((t, EVEN_IN), BF16)] + [jax.ShapeDtypeStruct((SUBLANES, A_WIDTH), F32)] * 2,
        scratch_shapes=[pltpu.VMEM((A_HEADS, HEAD_A, HEAD_A), F32)] + [pltpu.VMEM((tb, A_WIDTH), F32)] * 10,
        compiler_params=_params(("arbitrary", "arbitrary")),
    )(proj, proj, proj, proj, dcat, pre, states, lb_table, a_norm)


GMLP_ROWS = 512


def _gmlp_chunk(ub, vb, ln_g, ln_b, ws, bias):
    u = [_gelu(a) for a in ub]
    v = [_gelu(a) for a in vb]
    mu = sum(jnp.sum(a, axis=-1, keepdims=True) for a in v) * (1.0 / B_WIDTH)
    cen = [a - mu for a in v]
    var = sum(jnp.sum(a * a, axis=-1, keepdims=True) for a in cen) * (1.0 / B_WIDTH)
    inv = lax.rsqrt(var + EPS)
    r = lax.broadcasted_iota(jnp.int32, (B_CHUNK, B_CHUNK), 0)
    c = lax.broadcasted_iota(jnp.int32, (B_CHUNK, B_CHUNK), 1)
    outs = []
    for g in range(B_GROUPS):
        vn = (cen[g] * inv * ln_g[g] + ln_b[g]).astype(BF16)
        wm = jnp.where(c <= r, ws[g], 0.0).astype(BF16)
        outs.append(u[g] * (_dot(wm, vn) + bias[g]))
    return outs


def _lane_groups(ref, rows=slice(None)):
    return [ref[rows, g * LANES:(g + 1) * LANES] for g in range(B_GROUPS)]


def _gmlp_fwd(proj, mixed, ln_g, ln_b, ws, bias_t, name):
    t = proj.shape[0]
    tm = GMLP_ROWS

    def body(u_ref, v_ref, lg_ref, lb_ref, ws_ref, bt_ref, _, o_ref):
        for ch in range(tm // B_CHUNK):
            rows = slice(ch * B_CHUNK, (ch + 1) * B_CHUNK)
            outs = _gmlp_chunk(_lane_groups(u_ref, rows), _lane_groups(v_ref, rows), _lane_groups(lg_ref),
                               _lane_groups(lb_ref), [ws_ref[g] for g in range(B_GROUPS)],
                               [bt_ref[:, g:g + 1] for g in range(B_GROUPS)])
            for g in range(B_GROUPS):
                o_ref[rows, g * LANES:(g + 1) * LANES] = outs[g].astype(BF16)

    vec = pl.BlockSpec((1, B_WIDTH), lambda i: (0, 0))
    return pl.pallas_call(
        body, name=name, grid=(t // tm,),
        in_specs=[pl.BlockSpec((tm, B_WIDTH), lambda i: (i, 4)), pl.BlockSpec((tm, B_WIDTH), lambda i: (i, 5)), vec, vec,
                  pl.BlockSpec((B_GROUPS, B_CHUNK, B_CHUNK), lambda i: (0, 0, 0)),
                  pl.BlockSpec((B_CHUNK, B_GROUPS), lambda i: (0, 0)), pl.BlockSpec(memory_space=pl.ANY)],
        out_specs=pl.BlockSpec((tm, B_WIDTH), lambda i: (i, 1)),
        out_shape=jax.ShapeDtypeStruct(mixed.shape, BF16),
        input_output_aliases={6: 0},
        compiler_params=_params(("parallel",)),
    )(proj, proj, ln_g, ln_b, ws, bias_t, mixed)


def _gmlp_bwd(proj, dcat, dproj, ln_g, ln_b, ws, bias_t, name):
    t = proj.shape[0]
    tm = GMLP_ROWS

    def body(u_ref, v_ref, do_ref, lg_ref, lb_ref, ws_ref, bt_ref, _, duv_ref, dlg_ref, dlb_ref, dws_ref, dbt_ref):
        @pl.when(pl.program_id(0) == 0)
        def _():
            dlg_ref[...] = jnp.zeros_like(dlg_ref)
            dlb_ref[...] = jnp.zeros_like(dlb_ref)
            dws_ref[...] = jnp.zeros_like(dws_ref)
            dbt_ref[...] = jnp.zeros_like(dbt_ref)

        for ch in range(tm // B_CHUNK):
            rows = slice(ch * B_CHUNK, (ch + 1) * B_CHUNK)
            _, vjp = jax.vjp(
                _gmlp_chunk, _lane_groups(u_ref, rows), _lane_groups(v_ref, rows), _lane_groups(lg_ref),
                _lane_groups(lb_ref), [ws_ref[g] for g in range(B_GROUPS)],
                [bt_ref[:, g:g + 1] for g in range(B_GROUPS)])
            du, dv, dlg, dlb, dw, dbt = vjp(_lane_groups(do_ref, rows))
            for g in range(B_GROUPS):
                lanes = slice(g * LANES, (g + 1) * LANES)
                duv_ref[rows, lanes] = du[g].astype(BF16)
                duv_ref[rows, B_WIDTH + g * LANES:B_WIDTH + (g + 1) * LANES] = dv[g].astype(BF16)
                dlg_ref[0:1, lanes] += dlg[g]
                dlb_ref[0:1, lanes] += dlb[g]
                dws_ref[g] += dw[g]
                dbt_ref[:, g:g + 1] += dbt[g]

    vec = pl.BlockSpec((1, B_WIDTH), lambda i: (0, 0))
    acc8 = pl.BlockSpec((SUBLANES, B_WIDTH), lambda i: (0, 0))
    ws_spec = pl.BlockSpec((B_GROUPS, B_CHUNK, B_CHUNK), lambda i: (0, 0, 0))
    bt_spec = pl.BlockSpec((B_CHUNK, B_GROUPS), lambda i: (0, 0))
    return pl.pallas_call(
        body, name=name, grid=(t // tm,),
        in_specs=[pl.BlockSpec((tm, B_WIDTH), lambda i: (i, 4)), pl.BlockSpec((tm, B_WIDTH), lambda i: (i, 5)),
                  pl.BlockSpec((tm, B_WIDTH), lambda i: (i, 1)), vec, vec, ws_spec, bt_spec,
                  pl.BlockSpec(memory_space=pl.ANY)],
        out_specs=[pl.BlockSpec((tm, 2 * B_WIDTH), lambda i: (i, 2)), acc8, acc8, ws_spec, bt_spec],
        out_shape=[jax.ShapeDtypeStruct(dproj.shape, BF16), jax.ShapeDtypeStruct((SUBLANES, B_WIDTH), F32),
                   jax.ShapeDtypeStruct((SUBLANES, B_WIDTH), F32),
                   jax.ShapeDtypeStruct((B_GROUPS, B_CHUNK, B_CHUNK), F32),
                   jax.ShapeDtypeStruct((B_CHUNK, B_GROUPS), F32)],
        input_output_aliases={7: 0},
        compiler_params=_params(("arbitrary",)),
    )(proj, proj, dcat, ln_g, ln_b, ws, bias_t, dproj)


QK_SCALE = 1.0 / math.sqrt(C_HEAD_DIM)
ATTN_UNROLL = 8
LANE_GROUPS = D_MODEL // LANES
Q_BLOCKS = SEQ // C_BLOCK


def _attn_window(i, d):
    sub_blocks = Q_BLOCKS // d
    q0 = pl.multiple_of(i * C_BLOCK, C_BLOCK)
    k0 = pl.multiple_of(jnp.maximum(i - 1, 0) * C_BLOCK, C_BLOCK)
    key = k0 + lax.broadcasted_iota(jnp.int32, (C_BLOCK, 2 * C_BLOCK), 1)
    dist = (q0 + lax.broadcasted_iota(jnp.int32, (C_BLOCK, 2 * C_BLOCK), 0)) - key
    own_subsequence = (key >= q0) | (i % sub_blocks > 0)
    return pl.ds(q0, C_BLOCK), pl.ds(k0, 2 * C_BLOCK), (dist >= 0) & (dist <= C_BLOCK) & own_subsequence


def _head_masks():
    lane = lax.broadcasted_iota(jnp.int32, (C_BLOCK, LANES), 1)
    return [lane < C_HEAD_DIM, lane >= C_HEAD_DIM]


def _flat_spec(col_of):
    return pl.BlockSpec((1, SEQ, LANES), lambda b, g: (b, 0, col_of(g)))


def _put_heads(tile, g, col0, col1):
    lane = lax.broadcasted_iota(jnp.int32, tile.shape, 1)
    return jnp.where(lane == 2 * g, col0, jnp.where(lane == 2 * g + 1, col1, tile))


def _get_head(tile, h):
    lane = lax.broadcasted_iota(jnp.int32, tile.shape, 1)
    return jnp.sum(jnp.where(lane == h, tile, 0.0), axis=1, keepdims=True)


PER_HEAD_SPEC = pl.BlockSpec((1, SEQ, LANES), lambda b, g: (b, 0, 0))


def _attn_branch_fwd(qkv, name):
    n_seq, d, l, _ = qkv.shape
    flat = qkv.reshape(n_seq, SEQ, ODD_IN)

    def body(q_ref, k_ref, v_ref, o_ref, m_ref, l_ref):
        heads = _head_masks()
        g = pl.program_id(1)

        @pl.when(g == 0)
        def _():
            m_ref[...] = jnp.zeros_like(m_ref)
            l_ref[...] = jnp.zeros_like(l_ref)

        def block(i, carry):
            rows, keys, mask = _attn_window(i, d)
            q, k, v = q_ref[0, rows, :], k_ref[0, keys, :], v_ref[0, keys, :]
            res = []
            for hm in heads:
                s = jnp.where(mask, _dot_nt(jnp.where(hm, q, 0), k) * QK_SCALE, NEG)
                m = jnp.max(s, axis=-1, keepdims=True)
                p = jnp.exp(s - m)
                res.append((_dot(p.astype(BF16), v), m, jnp.sum(p, axis=-1, keepdims=True)))
            o_ref[0, rows, :] = jnp.where(heads[0], res[0][0], res[1][0])
            m_ref[0, rows, :] = _put_heads(m_ref[0, rows, :], g, res[0][1], res[1][1])
            l_ref[0, rows, :] = _put_heads(l_ref[0, rows, :], g, res[0][2], res[1][2])
            return carry

        lax.fori_loop(0, Q_BLOCKS, block, 0, unroll=ATTN_UNROLL)

    o, m, l_sum = pl.pallas_call(
        body, name=name, grid=(n_seq, LANE_GROUPS),
        in_specs=[_flat_spec(lambda g: g), _flat_spec(lambda g: LANE_GROUPS + g),
                  _flat_spec(lambda g: 2 * LANE_GROUPS + g)],
        out_specs=[_flat_spec(lambda g: g), PER_HEAD_SPEC, PER_HEAD_SPEC],
        out_shape=[jax.ShapeDtypeStruct((n_seq, SEQ, D_MODEL), F32)] + [jax.ShapeDtypeStruct((n_seq, SEQ, LANES), F32)] * 2,
        compiler_params=_params(("parallel", "arbitrary")),
    )(flat, flat, flat)
    return [o.reshape(n_seq, d, l, D_MODEL), m.reshape(n_seq, d, l, LANES), l_sum.reshape(n_seq, d, l, LANES)]


def _attn_merge(branches, name):
    n_seq = branches[0][0].shape[0]
    t = n_seq * SEQ
    tm = MERGE_TILE

    def body(*refs):
        ins = refs[:9]
        o_ref, ob_ref, lse_ref = refs[9:12]
        nat = refs[12:]
        for b, d in enumerate(C_DILATIONS[1:]):
            for k in range(3):
                _load_dilated(ins[3 + 3 * b + k], d, nat[3 * b + k])
        ms = [ins[1][0, 0], nat[1][0], nat[4][0]]
        ls = [ins[2][0, 0], nat[2][0], nat[5][0]]
        m_all = jnp.maximum(jnp.maximum(ms[0], ms[1]), ms[2])
        ws = [jnp.exp(ms[b] - m_all) for b in range(3)]
        lane = lax.broadcasted_iota(jnp.int32, m_all.shape, 1)
        total = jnp.where(lane < C_HEADS, ws[0] * ls[0] + ws[1] * ls[1] + ws[2] * ls[2], 1.0)
        lse_ref[...] = m_all + jnp.log(total)
        first_head = lane < C_HEAD_DIM
        for p in range(LANE_GROUPS):
            lanes = slice(p * LANES, (p + 1) * LANES)
            spread = lambda c: jnp.where(first_head, c[:, 2 * p:2 * p + 1], c[:, 2 * p + 1:2 * p + 2])
            os_ = [ins[0][0, 0, :, lanes], nat[0][p], nat[3][p]]
            o = (spread(ws[0]) * os_[0] + spread(ws[1]) * os_[1] + spread(ws[2]) * os_[2]) / spread(total)
            o_ref[:, lanes] = o
            ob_ref[:, lanes] = o.astype(BF16)

    row = pl.BlockSpec((tm, D_MODEL), lambda i: (i, 0))
    flat = [a for br in branches for a in br]
    in_specs = []
    for wide, narrow in zip(_dilated_specs(tm, D_MODEL, lambda: 0), _dilated_specs(tm, LANES, lambda: 0)):
        in_specs += [wide, narrow, narrow]
    per_head = pltpu.VMEM((1, tm, LANES), F32)
    return pl.pallas_call(
        body, name=name, grid=(t // tm,), in_specs=in_specs,
        out_specs=[row, row, pl.BlockSpec((tm, LANES), lambda i: (i, 0))],
        out_shape=[jax.ShapeDtypeStruct((t, D_MODEL), F32), jax.ShapeDtypeStruct((t, D_MODEL), BF16),
                   jax.ShapeDtypeStruct((t, LANES), F32)],
        scratch_shapes=[pltpu.VMEM((LANE_GROUPS, tm, LANES), F32), per_head, per_head] * 2,
        compiler_params=_params(("parallel",)),
    )(*flat)


def _attn_branch_bwd(qkv, dout, lse, delta, name):
    n_seq, d, l, _ = qkv.shape
    flat = lambda a: a.reshape(n_seq, SEQ, a.shape[-1])

    def body(q_ref, k_ref, v_ref, do_ref, lse_nat_ref, dl_nat_ref, dq_ref, dk_ref, dv_ref, lse_ref, dl_ref,
             dkt_ref, dvt_ref):
        heads = _head_masks()
        g = pl.program_id(1)
        dkt_ref[...] = jnp.zeros_like(dkt_ref)
        dvt_ref[...] = jnp.zeros_like(dvt_ref)
        for nat_ref, dst_ref in ((lse_nat_ref, lse_ref), (dl_nat_ref, dl_ref)):
            for r in range(d):
                rows = pl.ds(r, l, stride=d) if d > 1 else slice(None)
                dst_ref[r * l:(r + 1) * l, :] = nat_ref.at[0][rows, :]

        def block(i, carry):
            rows, keys, mask = _attn_window(i, d)
            q, do = q_ref[0, rows, :], do_ref[0, rows, :]
            k, v = k_ref[0, keys, :], v_ref[0, keys, :]
            lse_b, dl_b = lse_ref[rows, :], dl_ref[rows, :]
            dq, dk, dv = [], None, None
            for hh, hm in enumerate(heads):
                qh, doh = jnp.where(hm, q, 0), jnp.where(hm, do, 0)
                s = jnp.where(mask, _dot_nt(qh, k) * QK_SCALE, NEG)
                p = jnp.exp(s - _get_head(lse_b, 2 * g + hh))
                ds = (p * (_dot_nt(doh, v) - _get_head(dl_b, 2 * g + hh)) * QK_SCALE).astype(BF16)
                dq.append(_dot(ds, k))
                dk_h, dv_h = _dot_tn(qh, ds), _dot_tn(doh, p.astype(BF16))
                dk = dk_h if dk is None else dk + dk_h
                dv = dv_h if dv is None else dv + dv_h
            dq_ref[0, rows, :] = jnp.where(heads[0], dq[0], dq[1]).astype(BF16)
            dkt_ref[:, keys] += dk
            dvt_ref[:, keys] += dv
            return carry

        lax.fori_loop(0, Q_BLOCKS, block, 0, unroll=ATTN_UNROLL)
        for c in range(SEQ // ROW_TILE):
            rows = slice(c * ROW_TILE, (c + 1) * ROW_TILE)
            dk_ref[0, rows, :] = dkt_ref[:, rows].T.astype(BF16)
            dv_ref[0, rows, :] = dvt_ref[:, rows].T.astype(BF16)

    act = _flat_spec(lambda g: g)
    outs = pl.pallas_call(
        body, name=name, grid=(n_seq, LANE_GROUPS),
        in_specs=[_flat_spec(lambda g: g), _flat_spec(lambda g: LANE_GROUPS + g),
                  _flat_spec(lambda g: 2 * LANE_GROUPS + g), act, PER_HEAD_SPEC, PER_HEAD_SPEC],
        out_specs=[act] * 3,
        out_shape=[jax.ShapeDtypeStruct((n_seq, SEQ, D_MODEL), BF16)] * 3,
        scratch_shapes=[pltpu.VMEM((SEQ, LANES), F32)] * 2 + [pltpu.VMEM((LANES, SEQ), F32)] * 2,
        compiler_params=_params(("parallel", "parallel")),
    )(flat(qkv), flat(qkv), flat(qkv), flat(dout), lse, delta)
    return [o.reshape(n_seq, d, l, D_MODEL) for o in outs]


def _attn_combine_bwd(grads, rope, name):
    n_seq = grads[0][0].shape[0]
    t = n_seq * SEQ
    tm = MERGE_TILE

    def body(*refs):
        c_ref, s_ref, o_ref, nat4_ref, nat16_ref = refs[9:]
        for sec in range(3):
            _load_dilated(refs[3 + sec], 4, nat4_ref)
            _load_dilated(refs[6 + sec], 16, nat16_ref)
            for p in range(LANE_GROUPS):
                blk = refs[sec][0, 0, :, p * LANES:(p + 1) * LANES] + nat4_ref[p] + nat16_ref[p]
                if sec < 2:
                    blk = blk * c_ref[...] - _swap_halves(blk) * s_ref[...]
                o_ref[:, sec * D_MODEL + p * LANES:sec * D_MODEL + (p + 1) * LANES] = blk.astype(BF16)

    tab = pl.BlockSpec((tm, LANES), lambda i: (i, 0))
    flat = [a for br in grads for a in br]
    in_specs = []
    for spec in _dilated_specs(tm, D_MODEL, lambda: 0):
        in_specs += [spec] * 3
    return pl.pallas_call(
        body, name=name, grid=(t // tm,), in_specs=in_specs + [tab, tab],
        out_specs=pl.BlockSpec((tm, ODD_IN), lambda i: (i, 0)),
        out_shape=jax.ShapeDtypeStruct((t, ODD_IN), BF16),
        scratch_shapes=[pltpu.VMEM((LANE_GROUPS, tm, LANES), F32)] * 2,
        compiler_params=_params(("parallel",)),
    )(*flat, *rope)


def _adamw(w, g, m, v):
    m = ADAM_B1 * m + (1.0 - ADAM_B1) * g
    v = ADAM_B2 * v + (1.0 - ADAM_B2) * jnp.square(g)
    m_hat = m / (1.0 - ADAM_B1 ** ADAM_STEP)
    v_hat = v / (1.0 - ADAM_B2 ** ADAM_STEP)
    delta = -ADAM_LR * (m_hat / (jnp.sqrt(v_hat) + ADAM_EPS) + ADAM_WD * w)
    return delta, m, v


def _adamw_sharded(parts, w, m, v, after, name):
    n_layers, rows, cols = w.shape
    tr = min(rows, 256)

    def body(*refs):
        p_refs = refs[:n_layers]
        w_ref, m_ref, v_ref, _, g_ref, d_ref, mo_ref, vo_ref = refs[n_layers:]
        layer = pl.program_id(0)
        g = None
        for l, p_ref in enumerate(p_refs):
            g_l = p_ref[0].astype(F32)
            for s in range(1, N_DEV):
                g_l = g_l + p_ref[s].astype(F32)
            g = g_l if g is None else jnp.where(layer == l, g_l, g)
        delta, mn, vn = _adamw(w_ref[0], g, m_ref[0], v_ref[0])
        g_ref[0] = g
        d_ref[0] = delta
        mo_ref[0] = mn
        vo_ref[0] = vn

    def part_spec(l):
        return pl.BlockSpec((N_DEV, tr, cols), lambda a, i: (0, jnp.where(a == l, i, 0), 0))

    row = pl.BlockSpec((1, tr, cols), lambda a, i: (a, i, 0))
    return pl.pallas_call(
        body, name=name, grid=(n_layers, rows // tr),
        in_specs=[part_spec(l) for l in range(n_layers)] + [row, row, row, pl.BlockSpec(memory_space=pl.ANY)],
        out_specs=[row] * 4, out_shape=[jax.ShapeDtypeStruct(w.shape, F32)] * 4,
        compiler_params=_params(("arbitrary", "arbitrary")),
    )(*parts, w, m, v, after)


def _small_update(gathered, where, weights, moments_m, moments_v, lb_index, name):
    n = len(weights)
    n_g = len(gathered)

    def body(*refs):
        g_refs = refs[:n_g]
        w_refs, m_refs, v_refs = refs[n_g:n_g + n], refs[n_g + n:n_g + 2 * n], refs[n_g + 2 * n:n_g + 3 * n]
        outs = refs[n_g + 3 * n:]

        def total(k):
            array, rows, lanes = where[k]
            ref = g_refs[array]
            index = (slice(None),) * (len(ref.shape) - 1) if rows is None else (rows, lanes)
            acc = ref[(0,) + index]
            for s in range(1, N_DEV):
                acc = acc + ref[(s,) + index]
            return acc

        loss_rows = total(n)
        outs[0][...] = jnp.sum(jnp.sum(loss_rows, axis=1, keepdims=True), axis=0, keepdims=True)
        for k in range(n):
            part = total(k)
            if k == lb_index:
                dlb = jnp.sum(part, axis=0, keepdims=True)
                tab = w_refs[k][...]
                e = jnp.exp(tab - jnp.max(tab, axis=0, keepdims=True))
                p = e / jnp.sum(e, axis=0, keepdims=True)
                first = lax.broadcasted_iota(jnp.int32, p.shape, 0) == 0
                grads = [(slice(None), p * (jnp.where(first, dlb, 0.0) - p[0:1, :] * dlb))]
            elif part.shape == w_refs[k].shape:
                grads = [(slice(None), part)]
            else:
                grads = [(slice(l, l + 1), jnp.sum(part[l * SUBLANES:(l + 1) * SUBLANES], axis=0, keepdims=True))
                         for l in range(w_refs[k].shape[0])]
            for rows, g in grads:
                delta, mn, vn = _adamw(w_refs[k][rows], g, m_refs[k][rows], v_refs[k][rows])
                outs[1 + 4 * k][rows] = g
                outs[2 + 4 * k][rows] = delta
                outs[3 + 4 * k][rows] = mn
                outs[4 + 4 * k][rows] = vn

    vmem = pl.BlockSpec(memory_space=pltpu.VMEM)
    out_shape = [jax.ShapeDtypeStruct((1, 1), F32)]
    for w in weights:
        out_shape += [jax.ShapeDtypeStruct(w.shape, F32)] * 4
    args = list(gathered) + list(weights) + list(moments_m) + list(moments_v)
    return pl.pallas_call(
        body, name=name, in_specs=[vmem] * len(args), out_specs=[vmem] * len(out_shape), out_shape=out_shape,
        compiler_params=pltpu.CompilerParams(vmem_limit_bytes=VMEM_LIMIT),
    )(*args)


def kernel(x, positions, norm_mix_pre, norm_mix_post, norm_ffn_pre, norm_ffn_post, w_in_even, lb_table, a_norm, b_ln_g, b_ln_b, b_ws, b_bias, w_out_even, w_in_odd, w_out_odd, w_ff1, w_ff2, loss_target, m_norm_mix_pre, m_norm_mix_post, m_norm_ffn_pre, m_norm_ffn_post, m_w_in_even, m_lb_table, m_a_norm, m_b_ln_g, m_b_ln_b, m_b_ws, m_b_bias, m_w_out_even, m_w_in_odd, m_w_out_odd, m_w_ff1, m_w_ff2, v_norm_mix_pre, v_norm_mix_post, v_norm_ffn_pre, v_norm_ffn_post, v_w_in_even, v_lb_table, v_a_norm, v_b_ln_g, v_b_ln_b, v_b_ws, v_b_bias, v_w_out_even, v_w_in_odd, v_w_out_odd, v_w_ff1, v_w_ff2):
    n_seq = x.shape[0]
    t = n_seq * SEQ
    x0 = x.reshape(t, D_MODEL)
    target = loss_target.reshape(t, D_MODEL)

    me = _my_slot().astype(jnp.int32).reshape(1)

    order = ["in_e", "out_e", "ff1_0", "ff2_0", "in_o", "out_o", "ff1_1", "ff2_1"]
    shards = dict(in_e=w_in_even[0], out_e=w_out_even[0], in_o=w_in_odd[0], out_o=w_out_odd[0],
                  ff1_0=w_ff1[0], ff1_1=w_ff1[1], ff2_0=w_ff2[0], ff2_1=w_ff2[1])
    by_columns = ("in_e", "in_o", "ff1_0", "ff1_1")

    def place(k, after):
        if k in by_columns:
            return _place_own_columns(shards[k], me, "place_" + k, after)
        return _place_own(shards[k], me, "place_" + k, False, after=after)

    gathers = {}
    send0, recv0, land0, _, token0 = _exchange_start([place(order[0], None)], [None], "gather_start_first")
    gathers[order[0]] = (land0[0], send0[0], recv0[0])
    sends, recvs, lands, _, g_token = _exchange_start([place(k, token0) for k in order[1:]],
                                                      [None] * (len(order) - 1), "gather_start")
    for k, land, send, recv in zip(order[1:], lands, sends, recvs):
        gathers[k] = (land, send, recv)

    def get_w(keys, after):
        lands_k, sends_k, recvs_k = zip(*[gathers[k] for k in keys])
        return _exchange_wait(list(lands_k), [None] * len(keys), list(sends_k), list(recvs_k), after,
                              "gather_wait_" + keys[0])

    sent = {}

    def put_g(group, blocks):
        keys = list(blocks)
        own = [_place_own(blocks[k], me, "own_" + k, True) for k in keys]
        send_sems, recv_sems, own, srcs, token = _exchange_start(own, [blocks[k] for k in keys], "scatter_start_" + group)
        sent[group] = (keys, own, srcs, send_sems, recv_sems)
        return token

    rope = _rope_tables(positions)
    bias_t = b_bias[0].T
    grads = _local_step(x0, target, rope, norm_mix_pre, norm_mix_post, norm_ffn_pre, norm_ffn_post, lb_table,
                        a_norm, b_ln_g, b_ln_b, b_ws[0], bias_t, get_w, put_g, g_token)
    (dx0, loss_part, dg_mix_pre, dg_mix_post, dg_ffn_pre, dg_ffn_post, d_lb, d_a_norm, d_ln_g, d_ln_b, d_ws,
     d_bias_t) = grads

    packed = jnp.concatenate([dg_mix_pre, dg_mix_post, dg_ffn_pre, dg_ffn_post,
                              jnp.concatenate([d_lb, d_a_norm], axis=1), jnp.concatenate([d_ln_g, d_ln_b], axis=1),
                              loss_part], axis=0)
    small_lands = [_place_own(a, me, "own_small%d" % k, False, F32) for k, a in enumerate((packed, d_ws, d_bias_t))]
    s_send, s_recv, small_lands, _, after = _exchange_start(small_lands, [None] * 3, "gather_small_start")

    big = dict(w_in_even=(["in_e"], w_in_even, m_w_in_even, v_w_in_even),
               w_out_even=(["out_e"], w_out_even, m_w_out_even, v_w_out_even),
               w_in_odd=(["in_o"], w_in_odd, m_w_in_odd, v_w_in_odd),
               w_out_odd=(["out_o"], w_out_odd, m_w_out_odd, v_w_out_odd),
               w_ff1=(["ff1_0", "ff1_1"], w_ff1, m_w_ff1, v_w_ff1), w_ff2=(["ff2_0", "ff2_1"], w_ff2, m_w_ff2, v_w_ff2))
    recv, big_out = {}, {}
    for groups, names in ((("ffn1", "ffn0"), ("w_ff1", "w_ff2")), (("mix1",), ("w_in_odd", "w_out_odd")),
                          (("mix0",), ("w_in_even", "w_out_even"))):
        for group in groups:
            keys, own, srcs, send_sems, recv_sems = sent[group]
            recv.update(zip(keys, _exchange_wait(own, srcs, send_sems, recv_sems, after, "scatter_wait_" + group)))
        for nm in names:
            keys, w, m, v = big[nm]
            big_out[nm] = _adamw_sharded([recv[k] for k in keys], w, m, v, after, "adamw_" + nm)
            after = big_out[nm][0]
    big_out = [big_out[nm] for nm in ("w_in_even", "w_out_even", "w_in_odd", "w_out_odd", "w_ff1", "w_ff2")]
    gathered = _exchange_wait(small_lands, [None] * 3, s_send, s_recv, after, "gather_small_wait")
    rows8 = lambda k: slice(SUBLANES * k, SUBLANES * (k + 1))
    left, right, every = slice(0, A_WIDTH), slice(A_WIDTH, 2 * A_WIDTH), slice(None)
    where = [(0, slice(0, 16), every), (0, slice(16, 32), every), (0, slice(32, 48), every), (0, slice(48, 64), every),
             (0, rows8(8), left), (0, rows8(8), right), (0, rows8(9), left), (0, rows8(9), right),
             (1, None, None), (2, None, None), (0, rows8(10), every)]
    small_w = [norm_mix_pre, norm_mix_post, norm_ffn_pre, norm_ffn_post, lb_table, a_norm, b_ln_g, b_ln_b,
               b_ws[0], bias_t]
    small_m = [m_norm_mix_pre, m_norm_mix_post, m_norm_ffn_pre, m_norm_ffn_post, m_lb_table, m_a_norm, m_b_ln_g,
               m_b_ln_b, m_b_ws[0], m_b_bias[0].T]
    small_v = [v_norm_mix_pre, v_norm_mix_post, v_norm_ffn_pre, v_norm_ffn_post, v_lb_table, v_a_norm, v_b_ln_g,
               v_b_ln_b, v_b_ws[0], v_b_bias[0].T]
    small_out = _small_update(gathered, where, small_w, small_m, small_v, 4, "small_update")
    loss = small_out[0].reshape(())
    small = [small_out[1 + 4 * k:5 + 4 * k] for k in range(len(small_w))]
    small[8] = [a[None] for a in small[8]]
    small[9] = [a.T[None] for a in small[9]]

    per_weight = small[0:4] + [big_out[0]] + small[4:10] + big_out[1:6]
    grad_x = dx0.reshape(x.shape)
    out = [loss, grad_x]
    for kind in range(4):
        out += [p[kind] for p in per_weight]
    return tuple(out)


def _local_step(x0, target, rope, norm_mix_pre, norm_mix_post, norm_ffn_pre, norm_ffn_post, lb_table, a_norm,
                b_ln_g, b_ln_b, ws, bias_t, get_w, put_g, token):
    def gain(a, l, tok):
        return a[l:l + 1] if tok is None else a[l:l + 1] + tok[0:1, 0:1]

    full = lambda a: a.reshape(-1, D_MODEL)
    owners = lambda a: a.reshape((N_DEV, -1) + a.shape[1:])

    (g_in_e,) = get_w(["in_e"], token)
    proj, h_mix0 = _norm_inproj(x0, gain(norm_mix_pre, 0, token), g_in_e, "inproj_even")
    mixed, pre_a, states = _hgrn2_fwd(proj, lb_table, a_norm, "hgrn2_fwd")
    mixed = _gmlp_fwd(proj, mixed, b_ln_g, b_ln_b, ws, bias_t, "gmlp_fwd")
    w_out_e = full(get_w(["out_e"], mixed)[0])
    x1, mix0 = _outproj([mixed], w_out_e, x0, gain(norm_mix_post, 0, None), "outproj_even")
    w1_0, w2_0 = get_w(["ff1_0", "ff2_0"], x1)
    w2_0 = full(w2_0)
    x2, y0, h_ffn0, r0 = _ffn_fwd(x1, gain(norm_ffn_pre, 0, None), w1_0, w2_0, gain(norm_ffn_post, 0, None), "ffn_fwd_0")
    (g_in_o,) = get_w(["in_o"], x2)
    *qkv, h_mix1 = _norm_inproj_rope(x2, gain(norm_mix_pre, 1, None), g_in_o, rope, "inproj_odd")
    branches = [_attn_branch_fwd(a, "attn_fwd_d%d" % d) for a, d in zip(qkv, C_DILATIONS)]
    attn, attn_b, lse = _attn_merge(branches, "attn_merge")
    w_out_o = full(get_w(["out_o"], attn_b)[0])
    x3, mix1 = _outproj([attn_b], w_out_o, x2, gain(norm_mix_post, 1, None), "outproj_odd")
    w1_1, w2_1 = get_w(["ff1_1", "ff2_1"], x3)
    w2_1 = full(w2_1)
    dx4, y1, h_ffn1, r1, loss_part = _ffn_fwd(x3, gain(norm_ffn_pre, 1, None), w1_1, w2_1, gain(norm_ffn_post, 1, None),
                                              "ffn_fwd_1", target)

    dx3, dy1, da1, dg_ffn_pre1, dg_ffn_post1 = _ffn_bwd(
        dx4, x3, y1, r1, gain(norm_ffn_pre, 1, None), w1_1, w2_1, gain(norm_ffn_post, 1, None), "ffn_bwd_1")
    gw_ff1_1 = _grad_w(h_ffn1, da1, True, "grad_w_ff1_1")
    gw_ff2_1 = _grad_w(r1, dy1, False, "grad_w_ff2_1")
    tok = put_g("ffn1", dict(ff1_1=gw_ff1_1, ff2_1=owners(gw_ff2_1)))
    *dattn, delta, dz1, dg_mix_post1 = _outproj_bwd_attn(dx3, mix1, gain(norm_mix_post, 1, tok), w_out_o, attn,
                                                  "outproj_bwd_odd")
    gw_out_o = _grad_w(attn_b, dz1, False, "grad_w_out_odd")
    per_seq = lambda a: a.reshape(-1, SEQ, LANES)
    grads_c = [_attn_branch_bwd(qkv[b], dattn[b], per_seq(lse), per_seq(delta), "attn_bwd_d%d" % d)
               for b, d in enumerate(C_DILATIONS)]
    dqkv = _attn_combine_bwd(grads_c, rope, "attn_combine_bwd")
    gw_in_o = _grad_w(h_mix1, dqkv, True, "grad_w_in_odd")
    tok = put_g("mix1", dict(out_o=owners(gw_out_o), in_o=gw_in_o))
    dx2, dg_mix_pre1 = _inproj_bwd(dqkv, g_in_o, dx3, x2, gain(norm_mix_pre, 1, tok), "inproj_bwd_odd")

    dx1, dy0, da0, dg_ffn_pre0, dg_ffn_post0 = _ffn_bwd(
        dx2, x1, y0, r0, gain(norm_ffn_pre, 0, None), w1_0, w2_0, gain(norm_ffn_post, 0, None), "ffn_bwd_0")
    gw_ff1_0 = _grad_w(h_ffn0, da0, True, "grad_w_ff1_0")
    gw_ff2_0 = _grad_w(r0, dy0, False, "grad_w_ff2_0")
    tok = put_g("ffn0", dict(ff1_0=gw_ff1_0, ff2_0=owners(gw_ff2_0)))
    dcat, dz0, dg_mix_post0 = _outproj_bwd(dx1, mix0, gain(norm_mix_post, 0, tok), w_out_e, "outproj_bwd_even")
    gw_out_e = _grad_w(mixed, dz0, False, "grad_w_out_even")
    dproj, d_lb, d_a_norm = _hgrn2_bwd(proj, dcat, pre_a, states, lb_table, a_norm, "hgrn2_bwd")
    dproj, d_ln_g, d_ln_b, d_ws, d_bias_t = _gmlp_bwd(proj, dcat, dproj, b_ln_g, b_ln_b, ws, bias_t, "gmlp_bwd")
    gw_in_e = _grad_w(h_mix0, dproj, True, "grad_w_in_even")
    tok = put_g("mix0", dict(out_e=owners(gw_out_e), in_e=gw_in_e))
    dx0, dg_mix_pre0 = _inproj_bwd(dproj, g_in_e, dx1, x0, gain(norm_mix_pre, 0, tok), "inproj_bwd_even")

    layers = lambda a, b: jnp.concatenate([a, b], axis=0)
    return (dx0, loss_part, layers(dg_mix_pre0, dg_mix_pre1), layers(dg_mix_post0, dg_mix_post1),
            layers(dg_ffn_pre0, dg_ffn_pre1), layers(dg_ffn_post0, dg_ffn_post1),
            d_lb, d_a_norm, d_ln_g, d_ln_b, d_ws, d_bias_t)
```

```python
import functools
import math

import jax
import jax.numpy as jnp
from jax import lax
from jax.experimental import pallas as pl
from jax.experimental.pallas import tpu as pltpu

F32 = jnp.float32
BF16 = jnp.bfloat16
MESH = pl.DeviceIdType.MESH

N_DEV = 8
D_MODEL = 1024
SEQ = 2048
EPS = 1e-6
A_WIDTH = 512
A_HEADS = 4
HEAD_A = 128
B_WIDTH = 512
B_GROUPS = 4
B_CHUNK = 128
C_HEADS = 16
C_HEAD_DIM = 64
C_ROT_HALF = 8
ROPE_THETA = 500000.0
C_DILATIONS = (1, 4, 16)
C_BLOCK = 128
D_FF = 4096
EVEN_IN = 3072
ODD_IN = 3072

ADAM_LR = 0.001
ADAM_B1 = 0.9
ADAM_B2 = 0.999
ADAM_EPS = 1e-08
ADAM_WD = 0.01
ADAM_STEP = 10

LANES = 128
SUBLANES = 8
ROW_TILE = 512
PROJ_TILE = 1024
PROJ_COLS = 768
MERGE_TILE = 256
SUB_CHUNK = 16
HGRN_BLOCK = 256
NEG = -1e30
VMEM_LIMIT = 56 * 1024 * 1024


def _params(sem):
    return pltpu.CompilerParams(dimension_semantics=sem, vmem_limit_bytes=VMEM_LIMIT)


def _dot(a, b):
    return jnp.dot(a, b, preferred_element_type=F32)


def _dot_nt(a, b):
    return lax.dot_general(a, b, (((1,), (1,)), ((), ())), preferred_element_type=F32)


def _dot_tn(a, b):
    return lax.dot_general(a, b, (((0,), (0,)), ((), ())), preferred_element_type=F32)


def _rms(x, g):
    r = lax.rsqrt(jnp.mean(x * x, axis=-1, keepdims=True) + EPS)
    return x * r * g


def _rms_bwd(x, g, dy):
    r = lax.rsqrt(jnp.mean(x * x, axis=-1, keepdims=True) + EPS)
    dyg = dy * g
    dx = r * dyg - x * (r * r * r) * jnp.mean(x * dyg, axis=-1, keepdims=True)
    return dx, dy * x * r


def _rows8(v):
    return v.reshape(v.shape[0] // SUBLANES, SUBLANES, v.shape[1]).sum(axis=0)


def _sigmoid(x):
    return 1.0 / (1.0 + jnp.exp(-x))


def _gelu(x):
    return 0.5 * x * (1.0 + jnp.tanh(math.sqrt(2.0 / math.pi) * (x + 0.044715 * (x * x * x))))


def _acc_rows8(ref, val, first):
    @pl.when(first)
    def _():
        ref[...] = val

    @pl.when(jnp.logical_not(first))
    def _():
        ref[...] += val


def _my_slot():
    return 4 * lax.axis_index("x") + 2 * lax.axis_index("y") + lax.axis_index("c")


def _peer(r):
    x, y, c = lax.axis_index("x"), lax.axis_index("y"), lax.axis_index("c")
    px = 1 - x if (r >> 2) & 1 else x
    py = 1 - y if (r >> 1) & 1 else y
    pc = 1 - c if r & 1 else c
    return (px, py, pc), 4 * px + 2 * py + pc


HBM_SPEC = pl.BlockSpec(memory_space=pltpu.HBM)
SEM_SPEC = pl.BlockSpec(memory_space=pltpu.SEMAPHORE)
SPLIT_EFFECT = pltpu.SideEffectType.DATAFLOW_SIDE_EFFECTING


def _split_copies(land_ref, src_ref, send_sem, recv_sem):
    me = _my_slot()
    copies = []
    for r in range(1, N_DEV):
        peer, slot = _peer(r)
        src = _slot(land_ref, me) if src_ref is None else _slot(src_ref, slot)
        copies.append(pltpu.make_async_remote_copy(
            src_ref=src, dst_ref=_slot(land_ref, me), send_sem=send_sem, recv_sem=recv_sem,
            device_id=peer, device_id_type=MESH))
    return copies


def _slot(ref, s):
    if len(ref.shape) == 2:
        c = ref.shape[1] // N_DEV
        return ref.at[:, pl.ds(pl.multiple_of(s * c, LANES), c)]
    return ref.at[s]


def _exchange_start(lands, sources, name):
    n = len(lands)
    given = [s for s in sources if s is not None]
    arrays = list(lands) + given

    def body(*refs):
        land_refs, src_refs = refs[:n], list(refs[n:n + len(given)])
        sems = refs[len(arrays):len(arrays) + 2 * n]
        token = refs[-1]
        for k in range(n):
            src_ref = None if sources[k] is None else src_refs.pop(0)
            for copy in _split_copies(land_refs[k], src_ref, sems[k], sems[n + k]):
                copy.start()
        token[...] = jnp.zeros_like(token)

    outs = pl.pallas_call(
        body, name=name,
        out_shape=(pltpu.SemaphoreType.DMA(()),) * (2 * n) + tuple(pltpu.HBM(a.shape, a.dtype) for a in arrays)
        + (jax.ShapeDtypeStruct((SUBLANES, LANES), F32),),
        in_specs=[HBM_SPEC] * len(arrays),
        out_specs=(SEM_SPEC,) * (2 * n) + (HBM_SPEC,) * len(arrays) + (pl.BlockSpec(memory_space=pltpu.VMEM),),
        input_output_aliases={i: 2 * n + i for i in range(len(arrays))},
        compiler_params=pltpu.CompilerParams(has_side_effects=SPLIT_EFFECT),
    )(*[pltpu.with_memory_space_constraint(a, pltpu.HBM) for a in arrays])
    return list(outs[:n]), list(outs[n:2 * n]), list(outs[2 * n:3 * n]), list(outs[3 * n:-1]), outs[-1]


def _exchange_wait(lands, sources, send_sems, recv_sems, after, name):
    n = len(lands)
    given = [s for s in sources if s is not None]
    arrays = list(lands) + given

    def body(*refs):
        land_refs, src_refs = refs[:n], list(refs[n:n + len(given)])
        sems = refs[len(arrays):len(arrays) + 2 * n]
        for i in range(n):
            src_ref = None if sources[i] is None else src_refs.pop(0)
            copies = _split_copies(land_refs[i], src_ref, sems[i], sems[n + i])
            for copy in copies:
                copy.wait_recv()
            for copy in copies:
                copy.wait_send()

    outs = pl.pallas_call(
        body, name=name, out_shape=tuple(pltpu.HBM(a.shape, a.dtype) for a in arrays),
        in_specs=[HBM_SPEC] * len(arrays) + [SEM_SPEC] * (2 * n) + [pl.BlockSpec(memory_space=pl.ANY)],
        out_specs=(HBM_SPEC,) * len(arrays),
        input_output_aliases={i: i for i in range(len(arrays))},
        compiler_params=pltpu.CompilerParams(has_side_effects=SPLIT_EFFECT),
    )(*arrays, *send_sems, *recv_sems, after)
    return list(outs[:n])


def _place_own(a, me, name, own_block, dtype=BF16, after=None):
    shape = a.shape[1:] if own_block else a.shape
    cols = shape[-1]
    a3 = a.reshape((N_DEV if own_block else 1, -1, cols))
    rows = a3.shape[1]
    tr = min(rows, 512)

    def body(me_ref, a_ref, _, o_ref):
        o_ref[...] = a_ref[...].astype(dtype)

    grid_spec = pltpu.PrefetchScalarGridSpec(
        num_scalar_prefetch=1, grid=(rows // tr,),
        in_specs=[pl.BlockSpec((1, tr, cols), lambda i, me_ref: (me_ref[0] if own_block else 0, i, 0)),
                  pl.BlockSpec(memory_space=pl.ANY)],
        out_specs=pl.BlockSpec((1, tr, cols), lambda i, me_ref: (me_ref[0], i, 0)))
    out = pl.pallas_call(
        body, name=name, grid_spec=grid_spec, out_shape=jax.ShapeDtypeStruct((N_DEV, rows, cols), dtype),
        compiler_params=_params(("arbitrary",)),
    )(me, a3, a3 if after is None else after)
    return out.reshape((N_DEV,) + shape)


def _place_own_columns(a, me, name, after=None):
    rows, cols = a.shape
    tr = min(rows, 512)

    def body(me_ref, a_ref, _, o_ref):
        o_ref[...] = a_ref[...].astype(BF16)

    grid_spec = pltpu.PrefetchScalarGridSpec(
        num_scalar_prefetch=1, grid=(rows // tr,),
        in_specs=[pl.BlockSpec((tr, cols), lambda i, me_ref: (i, 0)), pl.BlockSpec(memory_space=pl.ANY)],
        out_specs=pl.BlockSpec((tr, cols), lambda i, me_ref: (i, me_ref[0])))
    return pl.pallas_call(
        body, name=name, grid_spec=grid_spec, out_shape=jax.ShapeDtypeStruct((rows, N_DEV * cols), BF16),
        compiler_params=_params(("arbitrary",)),
    )(me, a, a if after is None else after)


def _rope_tables(positions):
    in_head = jnp.arange(LANES) % C_HEAD_DIM
    inv = ROPE_THETA ** (-(in_head % C_ROT_HALF).astype(F32) / C_ROT_HALF)
    ang = positions.reshape(-1)[:, None].astype(F32) * inv
    rotated = in_head < 2 * C_ROT_HALF
    sin = jnp.sin(ang)
    return (jnp.where(rotated, jnp.cos(ang), 1.0),
            jnp.where(in_head < C_ROT_HALF, -sin, jnp.where(rotated, sin, 0.0)))


def _swap_halves(x):
    lane = lax.broadcasted_iota(jnp.int32, x.shape, 1) % C_HEAD_DIM
    return jnp.where(lane < C_ROT_HALF, pltpu.roll(x, LANES - C_ROT_HALF, 1), pltpu.roll(x, C_ROT_HALF, 1))


def _norm_inproj(x, g, w, name):
    t = x.shape[0]
    n = w.shape[1]
    tm, tn = PROJ_TILE, PROJ_COLS

    def body(x_ref, g_ref, w_ref, o_ref, h_ref):
        @pl.when(pl.program_id(1) == 0)
        def _():
            h_ref[...] = _rms(x_ref[...], g_ref[...]).astype(BF16)

        o_ref[...] = _dot(h_ref[...], w_ref[...])

    return pl.pallas_call(
        body, name=name, grid=(t // tm, n // tn),
        in_specs=[pl.BlockSpec((tm, D_MODEL), lambda i, j: (i, 0)), pl.BlockSpec((1, D_MODEL), lambda i, j: (0, 0)),
                  pl.BlockSpec((D_MODEL, tn), lambda i, j: (0, j))],
        out_specs=[pl.BlockSpec((tm, tn), lambda i, j: (i, j)), pl.BlockSpec((tm, D_MODEL), lambda i, j: (i, 0))],
        out_shape=[jax.ShapeDtypeStruct((t, n), F32), jax.ShapeDtypeStruct((t, D_MODEL), BF16)],
        compiler_params=_params(("parallel", "arbitrary")),
    )(x, g, w)


def _dilated_specs(tm, width, col_of):
    per_seq = SEQ // tm
    specs = []
    for d in C_DILATIONS:
        specs.append(pl.BlockSpec(
            (1, d, tm // d, width), lambda i, *rest: (i // per_seq, 0, i % per_seq, col_of(*rest))))
    return specs


def _dilated_shapes(n_seq, cols, dtype):
    return [jax.ShapeDtypeStruct((n_seq, d, SEQ // d, cols), dtype) for d in C_DILATIONS]


def _store_dilated(src_ref, out_refs, dtype):
    groups, tm, _ = src_ref.shape
    for d, o_ref in zip(C_DILATIONS, out_refs):
        for r in range(d):
            rows = pl.ds(r, tm // d, stride=d) if d > 1 else slice(None)
            for p in range(groups):
                o_ref[0, r, :, p * LANES:(p + 1) * LANES] = src_ref.at[p][rows, :].astype(dtype)


def _load_dilated(in_ref, d, dst_ref):
    groups, tm, _ = dst_ref.shape
    for r in range(d):
        rows = pl.ds(r, tm // d, stride=d)
        for p in range(groups):
            dst_ref.at[p][rows, :] = in_ref[0, r, :, p * LANES:(p + 1) * LANES].astype(F32)


def _norm_inproj_rope(x, g, w, rope, name):
    t = x.shape[0]
    n = w.shape[1]
    tm, nb = PROJ_TILE, PROJ_COLS

    def body(x_ref, g_ref, w_ref, c_ref, s_ref, o1_ref, o4_ref, o16_ref, h_ref, tile_ref):
        j = pl.program_id(1)

        @pl.when(j == 0)
        def _():
            h_ref[...] = _rms(x_ref[...], g_ref[...]).astype(BF16)

        acc = _dot(h_ref[...], w_ref[...])
        for p in range(nb // LANES):
            blk = acc[:, p * LANES:(p + 1) * LANES]
            roped = blk * c_ref[...] + _swap_halves(blk) * s_ref[...]
            piece = j * (nb // LANES) + p
            is_qk = piece < 2 * (D_MODEL // LANES)
            tile_ref[p] = jnp.where(is_qk, roped, blk) * jnp.where(piece < D_MODEL // LANES, QK_SCALE, 1.0)
        _store_dilated(tile_ref, (o1_ref, o4_ref, o16_ref), BF16)

    return pl.pallas_call(
        body, name=name, grid=(t // tm, n // nb),
        in_specs=[pl.BlockSpec((tm, D_MODEL), lambda i, j: (i, 0)), pl.BlockSpec((1, D_MODEL), lambda i, j: (0, 0)),
                  pl.BlockSpec((D_MODEL, nb), lambda i, j: (0, j)),
                  pl.BlockSpec((tm, LANES), lambda i, j: (i, 0)), pl.BlockSpec((tm, LANES), lambda i, j: (i, 0))],
        out_specs=_dilated_specs(tm, nb, lambda j: j) + [pl.BlockSpec((tm, D_MODEL), lambda i, j: (i, 0))],
        out_shape=_dilated_shapes(t // SEQ, n, BF16) + [jax.ShapeDtypeStruct((t, D_MODEL), BF16)],
        scratch_shapes=[pltpu.VMEM((nb // LANES, tm, LANES), F32)],
        compiler_params=_params(("parallel", "arbitrary")),
    )(x, g, w, *rope)


def _outproj(parts, w, x, g, name):
    t = x.shape[0]
    tm = PROJ_TILE
    n = len(parts)
    widths = [p.shape[1] for p in parts]

    def body(*refs):
        p_refs = refs[:n]
        w_ref, x_ref, g_ref, xo_ref, mix_ref = refs[n:]
        mix = None
        off = 0
        for p_ref, wd in zip(p_refs, widths):
            term = _dot(p_ref[...].astype(BF16), w_ref[off:off + wd, :])
            mix = term if mix is None else mix + term
            off += wd
        mix_ref[...] = mix
        xo_ref[...] = x_ref[...] + _rms(mix, g_ref[...])

    row = lambda i: (i, 0)
    return pl.pallas_call(
        body, name=name, grid=(t // tm,),
        in_specs=[pl.BlockSpec((tm, wd), row) for wd in widths] + [
            pl.BlockSpec((sum(widths), D_MODEL), lambda i: (0, 0)),
            pl.BlockSpec((tm, D_MODEL), row), pl.BlockSpec((1, D_MODEL), lambda i: (0, 0))],
        out_specs=[pl.BlockSpec((tm, D_MODEL), row)] * 2,
        out_shape=[jax.ShapeDtypeStruct((t, D_MODEL), F32)] * 2,
        compiler_params=_params(("parallel",)),
    )(*parts, w, x, g)


def _outproj_bwd(dx, mix, g, w, name):
    t = dx.shape[0]
    tm = PROJ_TILE
    k = w.shape[0]

    def body(dx_ref, mix_ref, g_ref, w_ref, dcat_ref, dz_ref, dg_ref):
        dz, dgr = _rms_bwd(mix_ref[...], g_ref[...], dx_ref[...])
        dzb = dz.astype(BF16)
        dz_ref[...] = dzb
        dcat_ref[...] = _dot_nt(dzb, w_ref[...])
        _acc_rows8(dg_ref, _rows8(dgr), pl.program_id(0) == 0)

    row = lambda i: (i, 0)
    return pl.pallas_call(
        body, name=name, grid=(t // tm,),
        in_specs=[pl.BlockSpec((tm, D_MODEL), row), pl.BlockSpec((tm, D_MODEL), row),
                  pl.BlockSpec((1, D_MODEL), lambda i: (0, 0)), pl.BlockSpec((k, D_MODEL), lambda i: (0, 0))],
        out_specs=[pl.BlockSpec((tm, k), row), pl.BlockSpec((tm, D_MODEL), row),
                   pl.BlockSpec((SUBLANES, D_MODEL), lambda i: (0, 0))],
        out_shape=[jax.ShapeDtypeStruct((t, k), F32), jax.ShapeDtypeStruct((t, D_MODEL), BF16),
                   jax.ShapeDtypeStruct((SUBLANES, D_MODEL), F32)],
        compiler_params=_params(("arbitrary",)),
    )(dx, mix, g, w)


def _outproj_bwd_attn(dx, mix, g, w, out, name):
    t = dx.shape[0]
    tm = MERGE_TILE

    def body(dx_ref, mix_ref, g_ref, w_ref, out_ref, do1, do4, do16, dl_ref, dz_ref, dg_ref, tile_ref):
        dz, dgr = _rms_bwd(mix_ref[...], g_ref[...], dx_ref[...])
        dzb = dz.astype(BF16)
        dz_ref[...] = dzb
        _acc_rows8(dg_ref, _rows8(dgr), pl.program_id(0) == 0)
        dout = _dot_nt(dzb, w_ref[...])
        for p in range(LANE_GROUPS):
            tile_ref[p] = dout[:, p * LANES:(p + 1) * LANES]
        _store_dilated(tile_ref, (do1, do4, do16), BF16)
        column = lax.broadcasted_iota(jnp.int32, (D_MODEL, LANES), 0) // C_HEAD_DIM
        head = lax.broadcasted_iota(jnp.int32, (D_MODEL, LANES), 1)
        dl_ref[...] = jnp.dot(dout * out_ref[...], (column == head).astype(F32), precision=lax.Precision.HIGHEST,
                              preferred_element_type=F32)

    row = lambda i: (i, 0)
    n_seq = t // SEQ
    return pl.pallas_call(
        body, name=name, grid=(t // tm,),
        in_specs=[pl.BlockSpec((tm, D_MODEL), row), pl.BlockSpec((tm, D_MODEL), row),
                  pl.BlockSpec((1, D_MODEL), lambda i: (0, 0)), pl.BlockSpec((D_MODEL, D_MODEL), lambda i: (0, 0)),
                  pl.BlockSpec((tm, D_MODEL), row)],
        out_specs=_dilated_specs(tm, D_MODEL, lambda: 0) + [
            pl.BlockSpec((tm, LANES), row), pl.BlockSpec((tm, D_MODEL), row),
            pl.BlockSpec((SUBLANES, D_MODEL), lambda i: (0, 0))],
        out_shape=_dilated_shapes(n_seq, D_MODEL, BF16) + [
            jax.ShapeDtypeStruct((t, LANES), F32), jax.ShapeDtypeStruct((t, D_MODEL), BF16),
            jax.ShapeDtypeStruct((SUBLANES, D_MODEL), F32)],
        scratch_shapes=[pltpu.VMEM((LANE_GROUPS, tm, LANES), F32)],
        compiler_params=_params(("arbitrary",)),
    )(dx, mix, g, w, out)


def _inproj_bwd(dproj, w, dx, x, g, name):
    t = x.shape[0]
    n = w.shape[1]
    tm = ROW_TILE

    def body(dp_ref, w_ref, dx_ref, x_ref, g_ref, o_ref, dg_ref):
        dxn, dgr = _rms_bwd(x_ref[...], g_ref[...], _dot_nt(dp_ref[...], w_ref[...]))
        o_ref[...] = dx_ref[...] + dxn
        _acc_rows8(dg_ref, _rows8(dgr), pl.program_id(0) == 0)

    row = lambda i: (i, 0)
    return pl.pallas_call(
        body, name=name, grid=(t // tm,),
        in_specs=[pl.BlockSpec((tm, n), row), pl.BlockSpec((D_MODEL, n), lambda i: (0, 0)),
                  pl.BlockSpec((tm, D_MODEL), row), pl.BlockSpec((tm, D_MODEL), row),
                  pl.BlockSpec((1, D_MODEL), lambda i: (0, 0))],
        out_specs=[pl.BlockSpec((tm, D_MODEL), row), pl.BlockSpec((SUBLANES, D_MODEL), lambda i: (0, 0))],
        out_shape=[jax.ShapeDtypeStruct((t, D_MODEL), F32), jax.ShapeDtypeStruct((SUBLANES, D_MODEL), F32)],
        compiler_params=_params(("arbitrary",)),
    )(dproj, w, dx, x, g)


def _grad_w(a, b, col_blocks, name):
    t, k = a.shape
    n = b.shape[1]
    tk = min(k, 1024)
    per_owner = n // N_DEV
    tn = 2 * per_owner if col_blocks else min(n, 1024)

    def body(a_ref, b_ref, o_ref, at_ref):
        @pl.when(pl.program_id(1) == 0)
        def _():
            for c in range(t // ROW_TILE):
                rows = slice(c * ROW_TILE, (c + 1) * ROW_TILE)
                at_ref[:, rows] = a_ref[rows, :].T

        res = _dot(at_ref[...], b_ref[...]).astype(BF16)
        if col_blocks:
            o_ref[0] = res[:, :per_owner]
            o_ref[1] = res[:, per_owner:]
        else:
            o_ref[...] = res

    if col_blocks:
        out_spec = pl.BlockSpec((2, tk, per_owner), lambda i, j: (j, i, 0))
        out_shape = jax.ShapeDtypeStruct((N_DEV, k, per_owner), BF16)
    else:
        out_spec = pl.BlockSpec((tk, tn), lambda i, j: (i, j))
        out_shape = jax.ShapeDtypeStruct((k, n), BF16)
    return pl.pallas_call(
        body, name=name, grid=(k // tk, n // tn),
        in_specs=[pl.BlockSpec((t, tk), lambda i, j: (0, i)), pl.BlockSpec((t, tn), lambda i, j: (0, j))],
        out_specs=out_spec, out_shape=out_shape,
        scratch_shapes=[pltpu.VMEM((tk, t), BF16)],
        compiler_params=_params(("parallel", "arbitrary")),
    )(a, b)


FF_BLOCK = D_FF // N_DEV
FF_STEP = 1024
FF_STEPS = D_FF // FF_STEP


def _ffn_fwd(x, g_pre, w1, w2, g_post, name, target=None):
    t = x.shape[0]
    tm = PROJ_TILE

    def body(*refs):
        if target is None:
            x_ref, gp_ref, w1_ref, w2_ref, gq_ref, xo_ref, y_ref, h_ref, r_ref = refs
        else:
            x_ref, gp_ref, w1_ref, w2_ref, gq_ref, t_ref, xo_ref, y_ref, h_ref, r_ref, l_ref = refs
        i, j = pl.program_id(0), pl.program_id(1)

        @pl.when(j == 0)
        def _():
            h_ref[...] = _rms(x_ref[...], gp_ref[...]).astype(BF16)

        a = _dot(h_ref[...], w1_ref[...])
        r = jnp.square(jnp.maximum(a, 0.0)).astype(BF16)
        r_ref[...] = r
        term = _dot(r, w2_ref[...])

        @pl.when(j == 0)
        def _():
            y_ref[...] = term

        @pl.when(j > 0)
        def _():
            y_ref[...] += term

        @pl.when(j == FF_STEPS - 1)
        def _():
            x_new = x_ref[...] + _rms(y_ref[...], gq_ref[...])
            if target is None:
                xo_ref[...] = x_new
            else:
                diff = x_new - t_ref[...]
                xo_ref[...] = diff * (1.0 / D_MODEL)
                _acc_rows8(l_ref, _rows8(diff * diff) * (0.5 / D_MODEL), i == 0)

    row = lambda i, j: (i, 0)
    vec = pl.BlockSpec((1, D_MODEL), lambda i, j: (0, 0))
    in_specs = [pl.BlockSpec((tm, D_MODEL), row), vec, pl.BlockSpec((D_MODEL, FF_STEP), lambda i, j: (0, j)),
                pl.BlockSpec((FF_STEP, D_MODEL), lambda i, j: (j, 0)), vec]
    out_specs = [pl.BlockSpec((tm, D_MODEL), row)] * 3 + [pl.BlockSpec((tm, FF_STEP), lambda i, j: (i, j))]
    out_shape = [jax.ShapeDtypeStruct((t, D_MODEL), F32), jax.ShapeDtypeStruct((t, D_MODEL), F32),
                 jax.ShapeDtypeStruct((t, D_MODEL), BF16), jax.ShapeDtypeStruct((t, D_FF), BF16)]
    args = [x, g_pre, w1, w2, g_post]
    if target is not None:
        in_specs.append(pl.BlockSpec((tm, D_MODEL), row))
        out_specs.append(pl.BlockSpec((SUBLANES, D_MODEL), lambda i, j: (0, 0)))
        out_shape.append(jax.ShapeDtypeStruct((SUBLANES, D_MODEL), F32))
        args.append(target)
    return pl.pallas_call(
        body, name=name, grid=(t // tm, FF_STEPS), in_specs=in_specs, out_specs=out_specs, out_shape=out_shape,
        compiler_params=_params(("parallel" if target is None else "arbitrary", "arbitrary")),
    )(*args)


def _ffn_bwd(dxo, x, y, r, g_pre, w1, w2, g_post, name):
    t = x.shape[0]
    tm = ROW_TILE

    def body(dxo_ref, x_ref, y_ref, r_ref, gp_ref, w1_ref, w2_ref, gq_ref,
             dx_ref, dy_ref, da_ref, dgp_ref, dgq_ref, acc_ref):
        i, j = pl.program_id(0), pl.program_id(1)

        @pl.when(j == 0)
        def _():
            dy, dgr = _rms_bwd(y_ref[...], gq_ref[...], dxo_ref[...])
            dy_ref[...] = dy.astype(BF16)
            _acc_rows8(dgq_ref, _rows8(dgr), i == 0)

        dr = _dot_nt(dy_ref[...], w2_ref[...])
        da = (dr * (2.0 * jnp.sqrt(r_ref[...].astype(F32)))).astype(BF16)
        da_ref[...] = da
        term = _dot_nt(da, w1_ref[...])

        @pl.when(j == 0)
        def _():
            acc_ref[...] = term

        @pl.when(j > 0)
        def _():
            acc_ref[...] += term

        @pl.when(j == FF_STEPS - 1)
        def _():
            dxn, dgr = _rms_bwd(x_ref[...], gp_ref[...], acc_ref[...])
            dx_ref[...] = dxo_ref[...] + dxn
            _acc_rows8(dgp_ref, _rows8(dgr), i == 0)

    row = lambda i, j: (i, 0)
    vec = pl.BlockSpec((1, D_MODEL), lambda i, j: (0, 0))
    acc8 = pl.BlockSpec((SUBLANES, D_MODEL), lambda i, j: (0, 0))
    return pl.pallas_call(
        body, name=name, grid=(t // tm, FF_STEPS),
        in_specs=[pl.BlockSpec((tm, D_MODEL), row)] * 3 + [
            pl.BlockSpec((tm, FF_STEP), lambda i, j: (i, j)),
            vec, pl.BlockSpec((D_MODEL, FF_STEP), lambda i, j: (0, j)),
            pl.BlockSpec((FF_STEP, D_MODEL), lambda i, j: (j, 0)), vec],
        out_specs=[pl.BlockSpec((tm, D_MODEL), row), pl.BlockSpec((tm, D_MODEL), row),
                   pl.BlockSpec((tm, FF_STEP), lambda i, j: (i, j)), acc8, acc8],
        out_shape=[jax.ShapeDtypeStruct((t, D_MODEL), F32), jax.ShapeDtypeStruct((t, D_MODEL), BF16),
                   jax.ShapeDtypeStruct((t, D_FF), BF16),
                   jax.ShapeDtypeStruct((SUBLANES, D_MODEL), F32), jax.ShapeDtypeStruct((SUBLANES, D_MODEL), F32)],
        scratch_shapes=[pltpu.VMEM((tm, D_MODEL), F32)],
        compiler_params=_params(("arbitrary", "arbitrary")),
    )(dxo, x, y, r, g_pre, w1, w2, g_post)


def _lower_bound(table):
    e = jnp.exp(table - jnp.max(table, axis=0, keepdims=True))
    return e[0:1, :] / jnp.sum(e, axis=0, keepdims=True)


def _hgrn2_block(q_ref, f_ref, lb):
    tb = f_ref.shape[0]
    sig = _sigmoid(f_ref[...])
    f = lb + (1.0 - lb) * sig
    qraw = q_ref[...]
    sq = _sigmoid(qraw)
    r = lax.broadcasted_iota(jnp.int32, (tb, tb), 0)
    c = lax.broadcasted_iota(jnp.int32, (tb, tb), 1)
    same = (r // SUB_CHUNK) == (c // SUB_CHUNK)
    logf = jnp.log(f)
    gsum = jnp.dot((same & (c <= r)).astype(F32), logf, precision=lax.Precision.HIGHEST, preferred_element_type=F32)
    glast = jnp.dot(same.astype(F32), logf, precision=lax.Precision.HIGHEST, preferred_element_type=F32)
    return dict(sig=sig, f=f, kk=1.0 - f, qraw=qraw, sq=sq, qs=qraw * sq, gsum=gsum,
                eg=jnp.exp(gsum), ekd=jnp.exp(glast - gsum), a=jnp.exp(glast))


def _head_sums(x):
    parts = [jnp.broadcast_to(jnp.sum(x[:, h * HEAD_A:(h + 1) * HEAD_A], axis=1, keepdims=True), (x.shape[0], HEAD_A))
             for h in range(A_HEADS)]
    return jnp.concatenate(parts, axis=1)


def _hgrn2_intra(g, kk, qs, v):
    row = lax.broadcasted_iota(jnp.int32, g.shape, 0)
    o = _head_sums(qs * kk) * v
    for j in range(1, SUB_CHUNK):
        decay = jnp.exp(jnp.where(row >= j, g - pltpu.roll(g, j, 0), NEG))
        o = o + _head_sums(qs * pltpu.roll(kk, j, 0) * decay) * pltpu.roll(v, j, 0)
    return o


def _hgrn2_intra_bwd(g, kk, qs, v, do):
    row = lax.broadcasted_iota(jnp.int32, g.shape, 0)
    dsc = _head_sums(do * v)
    dqs, dkk, dv = dsc * kk, dsc * qs, _head_sums(qs * kk) * do
    for j in range(1, SUB_CHUNK):
        k_dn = pltpu.roll(kk, j, 0)
        decay = jnp.exp(jnp.where(row >= j, g - pltpu.roll(g, j, 0), NEG))
        d_score = _head_sums(do * pltpu.roll(v, j, 0)) * decay
        dqs = dqs + d_score * k_dn
        dkk = dkk + pltpu.roll(d_score * qs, SUB_CHUNK - j, 0)
        dv = dv + pltpu.roll(_head_sums(qs * k_dn * decay) * do, SUB_CHUNK - j, 0)
    return dqs, dkk, dv


def _hgrn2_fwd(proj, lb_table, a_norm, name):
    t = proj.shape[0]
    tb = HGRN_BLOCK
    n_tb = SEQ // tb
    n_seq = t // SEQ
    n_sub = tb // SUB_CHUNK

    def body(q_ref, f_ref, i_ref, g_ref, lbt_ref, an_ref, o_ref, pre_ref, sts_ref, st_ref,
             gs_ref, kk_ref, qs_ref, eg_ref, ekd_ref, a_ref):
        @pl.when(pl.program_id(1) == 0)
        def _():
            st_ref[...] = jnp.zeros_like(st_ref)

        an = an_ref[...]
        blk = _hgrn2_block(q_ref, f_ref, _lower_bound(lbt_ref[...]))
        for ref, key in ((gs_ref, "gsum"), (kk_ref, "kk"), (qs_ref, "qs"), (eg_ref, "eg"), (ekd_ref, "ekd"), (a_ref, "a")):
            ref[...] = blk[key]

        def step(c, carry):
            rows = pl.ds(pl.multiple_of(c * SUB_CHUNK, SUB_CHUNK), SUB_CHUNK)
            kk, qs, v = kk_ref[rows, :], qs_ref[rows, :], i_ref[rows, :]
            o = _hgrn2_intra(gs_ref[rows, :], kk, qs, v)
            qg, kd, vb = (qs * eg_ref[rows, :]).astype(BF16), (kk * ekd_ref[rows, :]).astype(BF16), v.astype(BF16)
            for h in range(A_HEADS):
                lanes = slice(h * HEAD_A, (h + 1) * HEAD_A)
                st = st_ref[h]
                sts_ref[0, c, h] = st
                o_h = o[:, lanes] + _dot_nt(qg[:, lanes], st.astype(BF16))
                st_ref[h] = st * a_ref[rows, lanes][0:1] + _dot_tn(vb[:, lanes], kd[:, lanes])
                pre_ref[rows, lanes] = o_h
                graw = g_ref[rows, lanes]
                o_ref[rows, lanes] = (_rms(o_h, an[:, lanes]) * (graw * _sigmoid(graw))).astype(BF16)
            return carry

        lax.fori_loop(0, n_sub, step, 0, unroll=2)

    def col(k):
        return pl.BlockSpec((tb, A_WIDTH), lambda b, s, k=k: (b * n_tb + s, k))

    out_rows = pl.BlockSpec((tb, A_WIDTH), lambda b, s: (b * n_tb + s, 0))
    return pl.pallas_call(
        body, name=name, grid=(n_seq, n_tb),
        in_specs=[col(0), col(1), col(2), col(3),
                  pl.BlockSpec((3, A_WIDTH), lambda b, s: (0, 0)), pl.BlockSpec((1, A_WIDTH), lambda b, s: (0, 0))],
        out_specs=[out_rows, out_rows,
                   pl.BlockSpec((1, n_sub, A_HEADS, HEAD_A, HEAD_A), lambda b, s: (b * n_tb + s, 0, 0, 0, 0))],
        out_shape=[jax.ShapeDtypeStruct((t, D_MODEL), BF16), jax.ShapeDtypeStruct((t, A_WIDTH), F32),
                   jax.ShapeDtypeStruct((n_seq * n_tb, n_sub, A_HEADS, HEAD_A, HEAD_A), F32)],
        scratch_shapes=[pltpu.VMEM((A_HEADS, HEAD_A, HEAD_A), F32)] + [pltpu.VMEM((tb, A_WIDTH), F32)] * 6,
        compiler_params=_params(("parallel", "arbitrary")),
    )(proj, proj, proj, proj, lb_table, a_norm)


def _hgrn2_bwd(proj, dcat, pre, states, lb_table, a_norm, name):
    t = proj.shape[0]
    tb = HGRN_BLOCK
    n_tb = SEQ // tb
    n_seq = t // SEQ
    n_sub = tb // SUB_CHUNK

    def body(q_ref, f_ref, i_ref, g_ref, do_ref, pre_ref, sts_ref, lbt_ref, an_ref, dp_ref, dlb_ref, dan_ref, dst_ref,
             gs_ref, kk_ref, qs_ref, eg_ref, ekd_ref, a_ref, dpre_ref, dlf_ref, dqs_ref, dkk_ref):
        b, s = pl.program_id(0), pl.program_id(1)

        @pl.when(s == 0)
        def _():
            dst_ref[...] = jnp.zeros_like(dst_ref)

        @pl.when((b == 0) & (s == 0))
        def _():
            dlb_ref[...] = jnp.zeros_like(dlb_ref)
            dan_ref[...] = jnp.zeros_like(dan_ref)

        lb = _lower_bound(lbt_ref[...])
        an = an_ref[...]
        heads = [slice(h * HEAD_A, (h + 1) * HEAD_A) for h in range(A_HEADS)]
        blk = _hgrn2_block(q_ref, f_ref, lb)
        for ref, key in ((gs_ref, "gsum"), (kk_ref, "kk"), (qs_ref, "qs"), (eg_ref, "eg"), (ekd_ref, "ekd"), (a_ref, "a")):
            ref[...] = blk[key]
        for h, lanes in enumerate(heads):
            graw, o = g_ref[:, lanes], pre_ref[:, lanes]
            sg = _sigmoid(graw)
            dout = do_ref[:, lanes]
            d_o, dgr = _rms_bwd(o, an[:, lanes], dout * (graw * sg))
            dan_ref[0:1, lanes] += jnp.sum(dgr, axis=0, keepdims=True)
            dp_ref[:, 3 * A_WIDTH + h * HEAD_A:3 * A_WIDTH + (h + 1) * HEAD_A] = (
                dout * _rms(o, an[:, lanes]) * (sg * (1.0 + graw * (1.0 - sg)))).astype(BF16)
            dpre_ref[:, lanes] = d_o

        tri_t = (lax.broadcasted_iota(jnp.int32, (SUB_CHUNK, SUB_CHUNK), 0)
                 <= lax.broadcasted_iota(jnp.int32, (SUB_CHUNK, SUB_CHUNK), 1)).astype(F32)

        def back(k, carry):
            c = n_sub - 1 - k
            rows = pl.ds(pl.multiple_of(c * SUB_CHUNK, SUB_CHUNK), SUB_CHUNK)
            g, kk, qs, v, d_o = gs_ref[rows, :], kk_ref[rows, :], qs_ref[rows, :], i_ref[rows, :], dpre_ref[rows, :]
            eg, ekd, a = eg_ref[rows, :], ekd_ref[rows, :], a_ref[rows, :]
            dqs, dkk, dv = _hgrn2_intra_bwd(g, kk, qs, v, d_o)
            qg_f, kd_f = qs * eg, kk * ekd
            qg, kd, vb, dob = qg_f.astype(BF16), kd_f.astype(BF16), v.astype(BF16), d_o.astype(BF16)
            dqg, dkd, da, dv_st = [], [], [], []
            for h, lanes in enumerate(heads):
                st, dst = sts_ref[0, c, h], dst_ref[h]
                dstb = dst.astype(BF16)
                dqg.append(_dot(dob[:, lanes], st.astype(BF16)))
                dv_st.append(_dot_nt(kd[:, lanes], dstb))
                dkd.append(_dot(vb[:, lanes], dstb))
                da.append(jnp.broadcast_to(jnp.sum(dst * st, axis=0, keepdims=True), (SUB_CHUNK, HEAD_A)))
                dst_ref[h] = dst * a[0:1, lanes] + _dot_tn(dob[:, lanes], qg[:, lanes])
            dqg, dkd, da, dv_st = [jnp.concatenate(p, axis=1) for p in (dqg, dkd, da, dv_st)]
            d_gsum = qs * dqs - kk * dkk + dqg * qg_f - dkd * kd_f
            d_glast = jnp.sum(dkd * kd_f, axis=0, keepdims=True) + da * a
            dlf_ref[rows, :] = jnp.dot(tri_t, d_gsum, precision=lax.Precision.HIGHEST,
                                       preferred_element_type=F32) + d_glast
            dqs_ref[rows, :] = dqs + dqg * eg
            dkk_ref[rows, :] = dkk + dkd * ekd
            dp_ref[rows, 2 * A_WIDTH:3 * A_WIDTH] = (dv + dv_st).astype(BF16)
            return carry

        lax.fori_loop(0, n_sub, back, 0, unroll=2)
        sig, sq, qraw = blk["sig"], blk["sq"], blk["qraw"]
        d_f = dlf_ref[...] / blk["f"] - dkk_ref[...]
        dlb_ref[0:1, :] += jnp.sum(d_f * (1.0 - sig), axis=0, keepdims=True)
        dp_ref[:, 0:A_WIDTH] = (dqs_ref[...] * (sq * (1.0 + qraw * (1.0 - sq)))).astype(BF16)
        dp_ref[:, A_WIDTH:2 * A_WIDTH] = (d_f * (1.0 - lb) * sig * (1.0 - sig)).astype(BF16)

    def rev(s):
        return n_tb - 1 - s

    def col(k):
        return pl.BlockSpec((tb, A_WIDTH), lambda b, s, k=k: (b * n_tb + rev(s), k))

    acc8 = pl.BlockSpec((SUBLANES, A_WIDTH), lambda b, s: (0, 0))
    return pl.pallas_call(
        body, name=name, grid=(n_seq, n_tb),
        in_specs=[col(0), col(1), col(2), col(3), col(0), col(0),
                  pl.BlockSpec((1, n_sub, A_HEADS, HEAD_A, HEAD_A), lambda b, s: (b * n_tb + rev(s), 0, 0, 0, 0)),
                  pl.BlockSpec((3, A_WIDTH), lambda b, s: (0, 0)), pl.BlockSpec((1, A_WIDTH), lambda b, s: (0, 0))],
        out_specs=[pl.BlockSpec((tb, 4 * A_WIDTH), lambda b, s: (b * n_tb + rev(s), 0)), acc8, acc8],
        out_shape=[jax.ShapeDtypeStruct((t, EVEN_IN), BF16)] + [jax.ShapeDtypeStruct((SUBLANES, A_WIDTH), F32)] * 2,
        scratch_shapes=[pltpu.VMEM((A_HEADS, HEAD_A, HEAD_A), F32)] + [pltpu.VMEM((tb, A_WIDTH), F32)] * 10,
        compiler_params=_params(("arbitrary", "arbitrary")),
    )(proj, proj, proj, proj, dcat, pre, states, lb_table, a_norm)


GMLP_ROWS = 512


def _gmlp_chunk(ub, vb, ln_g, ln_b, ws, bias):
    u = [_gelu(a) for a in ub]
    v = [_gelu(a) for a in vb]
    mu = sum(jnp.sum(a, axis=-1, keepdims=True) for a in v) * (1.0 / B_WIDTH)
    cen = [a - mu for a in v]
    var = sum(jnp.sum(a * a, axis=-1, keepdims=True) for a in cen) * (1.0 / B_WIDTH)
    inv = lax.rsqrt(var + EPS)
    r = lax.broadcasted_iota(jnp.int32, (B_CHUNK, B_CHUNK), 0)
    c = lax.broadcasted_iota(jnp.int32, (B_CHUNK, B_CHUNK), 1)
    outs = []
    for g in range(B_GROUPS):
        vn = (cen[g] * inv * ln_g[g] + ln_b[g]).astype(BF16)
        wm = jnp.where(c <= r, ws[g], 0.0).astype(BF16)
        outs.append(u[g] * (_dot(wm, vn) + bias[g]))
    return outs


def _lane_groups(ref, rows=slice(None)):
    return [ref[rows, g * LANES:(g + 1) * LANES] for g in range(B_GROUPS)]


def _gmlp_fwd(proj, mixed, ln_g, ln_b, ws, bias_t, name):
    t = proj.shape[0]
    tm = GMLP_ROWS

    def body(u_ref, v_ref, lg_ref, lb_ref, ws_ref, bt_ref, _, o_ref):
        for ch in range(tm // B_CHUNK):
            rows = slice(ch * B_CHUNK, (ch + 1) * B_CHUNK)
            outs = _gmlp_chunk(_lane_groups(u_ref, rows), _lane_groups(v_ref, rows), _lane_groups(lg_ref),
                               _lane_groups(lb_ref), [ws_ref[g] for g in range(B_GROUPS)],
                               [bt_ref[:, g:g + 1] for g in range(B_GROUPS)])
            for g in range(B_GROUPS):
                o_ref[rows, g * LANES:(g + 1) * LANES] = outs[g].astype(BF16)

    vec = pl.BlockSpec((1, B_WIDTH), lambda i: (0, 0))
    return pl.pallas_call(
        body, name=name, grid=(t // tm,),
        in_specs=[pl.BlockSpec((tm, B_WIDTH), lambda i: (i, 4)), pl.BlockSpec((tm, B_WIDTH), lambda i: (i, 5)), vec, vec,
                  pl.BlockSpec((B_GROUPS, B_CHUNK, B_CHUNK), lambda i: (0, 0, 0)),
                  pl.BlockSpec((B_CHUNK, B_GROUPS), lambda i: (0, 0)), pl.BlockSpec(memory_space=pl.ANY)],
        out_specs=pl.BlockSpec((tm, B_WIDTH), lambda i: (i, 1)),
        out_shape=jax.ShapeDtypeStruct(mixed.shape, BF16),
        input_output_aliases={6: 0},
        compiler_params=_params(("parallel",)),
    )(proj, proj, ln_g, ln_b, ws, bias_t, mixed)


def _gmlp_bwd(proj, dcat, dproj, ln_g, ln_b, ws, bias_t, name):
    t = proj.shape[0]
    tm = GMLP_ROWS

    def body(u_ref, v_ref, do_ref, lg_ref, lb_ref, ws_ref, bt_ref, _, duv_ref, dlg_ref, dlb_ref, dws_ref, dbt_ref):
        @pl.when(pl.program_id(0) == 0)
        def _():
            dlg_ref[...] = jnp.zeros_like(dlg_ref)
            dlb_ref[...] = jnp.zeros_like(dlb_ref)
            dws_ref[...] = jnp.zeros_like(dws_ref)
            dbt_ref[...] = jnp.zeros_like(dbt_ref)

        for ch in range(tm // B_CHUNK):
            rows = slice(ch * B_CHUNK, (ch + 1) * B_CHUNK)
            _, vjp = jax.vjp(
                _gmlp_chunk, _lane_groups(u_ref, rows), _lane_groups(v_ref, rows), _lane_groups(lg_ref),
                _lane_groups(lb_ref), [ws_ref[g] for g in range(B_GROUPS)],
                [bt_ref[:, g:g + 1] for g in range(B_GROUPS)])
            du, dv, dlg, dlb, dw, dbt = vjp(_lane_groups(do_ref, rows))
            for g in range(B_GROUPS):
                lanes = slice(g * LANES, (g + 1) * LANES)
                duv_ref[rows, lanes] = du[g].astype(BF16)
                duv_ref[rows, B_WIDTH + g * LANES:B_WIDTH + (g + 1) * LANES] = dv[g].astype(BF16)
                dlg_ref[0:1, lanes] += dlg[g]
                dlb_ref[0:1, lanes] += dlb[g]
                dws_ref[g] += dw[g]
                dbt_ref[:, g:g + 1] += dbt[g]

    vec = pl.BlockSpec((1, B_WIDTH), lambda i: (0, 0))
    acc8 = pl.BlockSpec((SUBLANES, B_WIDTH), lambda i: (0, 0))
    ws_spec = pl.BlockSpec((B_GROUPS, B_CHUNK, B_CHUNK), lambda i: (0, 0, 0))
    bt_spec = pl.BlockSpec((B_CHUNK, B_GROUPS), lambda i: (0, 0))
    return pl.pallas_call(
        body, name=name, grid=(t // tm,),
        in_specs=[pl.BlockSpec((tm, B_WIDTH), lambda i: (i, 4)), pl.BlockSpec((tm, B_WIDTH), lambda i: (i, 5)),
                  pl.BlockSpec((tm, B_WIDTH), lambda i: (i, 1)), vec, vec, ws_spec, bt_spec,
                  pl.BlockSpec(memory_space=pl.ANY)],
        out_specs=[pl.BlockSpec((tm, 2 * B_WIDTH), lambda i: (i, 2)), acc8, acc8, ws_spec, bt_spec],
        out_shape=[jax.ShapeDtypeStruct(dproj.shape, BF16), jax.ShapeDtypeStruct((SUBLANES, B_WIDTH), F32),
                   jax.ShapeDtypeStruct((SUBLANES, B_WIDTH), F32),
                   jax.ShapeDtypeStruct((B_GROUPS, B_CHUNK, B_CHUNK), F32),
                   jax.ShapeDtypeStruct((B_CHUNK, B_GROUPS), F32)],
        input_output_aliases={7: 0},
        compiler_params=_params(("arbitrary",)),
    )(proj, proj, dcat, ln_g, ln_b, ws, bias_t, dproj)


QK_SCALE = 1.0 / math.sqrt(C_HEAD_DIM)
ATTN_UNROLL = 8
LANE_GROUPS = D_MODEL // LANES
Q_BLOCKS = SEQ // C_BLOCK


def _attn_window(i, d):
    sub_blocks = Q_BLOCKS // d
    q0 = pl.multiple_of(i * C_BLOCK, C_BLOCK)
    k0 = pl.multiple_of(jnp.maximum(i - 1, 0) * C_BLOCK, C_BLOCK)
    key = k0 + lax.broadcasted_iota(jnp.int32, (C_BLOCK, 2 * C_BLOCK), 1)
    dist = (q0 + lax.broadcasted_iota(jnp.int32, (C_BLOCK, 2 * C_BLOCK), 0)) - key
    own_subsequence = (key >= q0) | (i % sub_blocks > 0)
    return pl.ds(q0, C_BLOCK), pl.ds(k0, 2 * C_BLOCK), (dist >= 0) & (dist <= C_BLOCK) & own_subsequence


def _head_masks():
    lane = lax.broadcasted_iota(jnp.int32, (C_BLOCK, LANES), 1)
    return [lane < C_HEAD_DIM, lane >= C_HEAD_DIM]


def _flat_spec(col_of):
    return pl.BlockSpec((1, SEQ, LANES), lambda b, g: (b, 0, col_of(g)))


def _put_heads(tile, g, col0, col1):
    lane = lax.broadcasted_iota(jnp.int32, tile.shape, 1)
    return jnp.where(lane == 2 * g, col0, jnp.where(lane == 2 * g + 1, col1, tile))


def _get_head(tile, h):
    lane = lax.broadcasted_iota(jnp.int32, tile.shape, 1)
    return jnp.sum(jnp.where(lane == h, tile, 0.0), axis=1, keepdims=True)


PER_HEAD_SPEC = pl.BlockSpec((1, SEQ, LANES), lambda b, g: (b, 0, 0))


def _attn_branch_fwd(qkv, name):
    n_seq, d, l, _ = qkv.shape
    flat = qkv.reshape(n_seq, SEQ, ODD_IN)

    def body(q_ref, k_ref, v_ref, o_ref, m_ref, l_ref):
        heads = _head_masks()
        g = pl.program_id(1)

        @pl.when(g == 0)
        def _():
            m_ref[...] = jnp.zeros_like(m_ref)
            l_ref[...] = jnp.zeros_like(l_ref)

        def block(i, carry):
            rows, keys, mask = _attn_window(i, d)
            q, k, v = q_ref[0, rows, :], k_ref[0, keys, :], v_ref[0, keys, :]
            res = []
            for hm in heads:
                s = jnp.where(mask, _dot_nt(jnp.where(hm, q, 0), k), NEG)
                m = jnp.max(s, axis=-1, keepdims=True)
                p = jnp.exp(s - m)
                res.append((_dot(p.astype(BF16), v), m, jnp.sum(p, axis=-1, keepdims=True)))
            o_ref[0, rows, :] = jnp.where(heads[0], res[0][0], res[1][0])
            m_ref[0, rows, :] = _put_heads(m_ref[0, rows, :], g, res[0][1], res[1][1])
            l_ref[0, rows, :] = _put_heads(l_ref[0, rows, :], g, res[0][2], res[1][2])
            return carry

        lax.fori_loop(0, Q_BLOCKS, block, 0, unroll=ATTN_UNROLL)

    o, m, l_sum = pl.pallas_call(
        body, name=name, grid=(n_seq, LANE_GROUPS),
        in_specs=[_flat_spec(lambda g: g), _flat_spec(lambda g: LANE_GROUPS + g),
                  _flat_spec(lambda g: 2 * LANE_GROUPS + g)],
        out_specs=[_flat_spec(lambda g: g), PER_HEAD_SPEC, PER_HEAD_SPEC],
        out_shape=[jax.ShapeDtypeStruct((n_seq, SEQ, D_MODEL), F32)] + [jax.ShapeDtypeStruct((n_seq, SEQ, LANES), F32)] * 2,
        compiler_params=_params(("parallel", "arbitrary")),
    )(flat, flat, flat)
    return [o.reshape(n_seq, d, l, D_MODEL), m.reshape(n_seq, d, l, LANES), l_sum.reshape(n_seq, d, l, LANES)]


def _attn_merge(branches, name):
    n_seq = branches[0][0].shape[0]
    t = n_seq * SEQ
    tm = MERGE_TILE

    def body(*refs):
        ins = refs[:9]
        o_ref, ob_ref, lse_ref = refs[9:12]
        nat = refs[12:]
        for b, d in enumerate(C_DILATIONS[1:]):
            for k in range(3):
                _load_dilated(ins[3 + 3 * b + k], d, nat[3 * b + k])
        ms = [ins[1][0, 0], nat[1][0], nat[4][0]]
        ls = [ins[2][0, 0], nat[2][0], nat[5][0]]
        m_all = jnp.maximum(jnp.maximum(ms[0], ms[1]), ms[2])
        ws = [jnp.exp(ms[b] - m_all) for b in range(3)]
        lane = lax.broadcasted_iota(jnp.int32, m_all.shape, 1)
        total = jnp.where(lane < C_HEADS, ws[0] * ls[0] + ws[1] * ls[1] + ws[2] * ls[2], 1.0)
        lse_ref[...] = m_all + jnp.log(total)
        first_head = lane < C_HEAD_DIM
        for p in range(LANE_GROUPS):
            lanes = slice(p * LANES, (p + 1) * LANES)
            spread = lambda c: jnp.where(first_head, c[:, 2 * p:2 * p + 1], c[:, 2 * p + 1:2 * p + 2])
            os_ = [ins[0][0, 0, :, lanes], nat[0][p], nat[3][p]]
            o = (spread(ws[0]) * os_[0] + spread(ws[1]) * os_[1] + spread(ws[2]) * os_[2]) / spread(total)
            o_ref[:, lanes] = o
            ob_ref[:, lanes] = o.astype(BF16)

    row = pl.BlockSpec((tm, D_MODEL), lambda i: (i, 0))
    flat = [a for br in branches for a in br]
    in_specs = []
    for wide, narrow in zip(_dilated_specs(tm, D_MODEL, lambda: 0), _dilated_specs(tm, LANES, lambda: 0)):
        in_specs += [wide, narrow, narrow]
    per_head = pltpu.VMEM((1, tm, LANES), F32)
    return pl.pallas_call(
        body, name=name, grid=(t // tm,), in_specs=in_specs,
        out_specs=[row, row, pl.BlockSpec((tm, LANES), lambda i: (i, 0))],
        out_shape=[jax.ShapeDtypeStruct((t, D_MODEL), F32), jax.ShapeDtypeStruct((t, D_MODEL), BF16),
                   jax.ShapeDtypeStruct((t, LANES), F32)],
        scratch_shapes=[pltpu.VMEM((LANE_GROUPS, tm, LANES), F32), per_head, per_head] * 2,
        compiler_params=_params(("parallel",)),
    )(*flat)


def _attn_branch_bwd(qkv, dout, lse, delta, name):
    n_seq, d, l, _ = qkv.shape
    flat = lambda a: a.reshape(n_seq, SEQ, a.shape[-1])

    def body(q_ref, k_ref, v_ref, do_ref, lse_nat_ref, dl_nat_ref, dq_ref, dk_ref, dv_ref, lse_ref, dl_ref,
             dkt_ref, dvt_ref):
        heads = _head_masks()
        g = pl.program_id(1)
        dkt_ref[...] = jnp.zeros_like(dkt_ref)
        dvt_ref[...] = jnp.zeros_like(dvt_ref)
        for nat_ref, dst_ref in ((lse_nat_ref, lse_ref), (dl_nat_ref, dl_ref)):
            for r in range(d):
                rows = pl.ds(r, l, stride=d) if d > 1 else slice(None)
                dst_ref[r * l:(r + 1) * l, :] = nat_ref.at[0][rows, :]

        def block(i, carry):
            rows, keys, mask = _attn_window(i, d)
            q, do = q_ref[0, rows, :], do_ref[0, rows, :]
            k, v = k_ref[0, keys, :], v_ref[0, keys, :]
            lse_b, dl_b = lse_ref[rows, :], dl_ref[rows, :]
            dq, dk, dv = [], None, None
            for hh, hm in enumerate(heads):
                qh, doh = jnp.where(hm, q, 0), jnp.where(hm, do, 0)
                s = jnp.where(mask, _dot_nt(qh, k), NEG)
                p = jnp.exp(s - _get_head(lse_b, 2 * g + hh))
                ds = (p * (_dot_nt(doh, v) - _get_head(dl_b, 2 * g + hh))).astype(BF16)
                dq.append(_dot(ds, k) * QK_SCALE)
                dk_h, dv_h = _dot_tn(qh, ds), _dot_tn(doh, p.astype(BF16))
                dk = dk_h if dk is None else dk + dk_h
                dv = dv_h if dv is None else dv + dv_h
            dq_ref[0, rows, :] = jnp.where(heads[0], dq[0], dq[1]).astype(BF16)
            dkt_ref[:, keys] += dk
            dvt_ref[:, keys] += dv
            return carry

        lax.fori_loop(0, Q_BLOCKS, block, 0, unroll=ATTN_UNROLL)
        for c in range(SEQ // ROW_TILE):
            rows = slice(c * ROW_TILE, (c + 1) * ROW_TILE)
            dk_ref[0, rows, :] = dkt_ref[:, rows].T.astype(BF16)
            dv_ref[0, rows, :] = dvt_ref[:, rows].T.astype(BF16)

    act = _flat_spec(lambda g: g)
    outs = pl.pallas_call(
        body, name=name, grid=(n_seq, LANE_GROUPS),
        in_specs=[_flat_spec(lambda g: g), _flat_spec(lambda g: LANE_GROUPS + g),
                  _flat_spec(lambda g: 2 * LANE_GROUPS + g), act, PER_HEAD_SPEC, PER_HEAD_SPEC],
        out_specs=[act] * 3,
        out_shape=[jax.ShapeDtypeStruct((n_seq, SEQ, D_MODEL), BF16)] * 3,
        scratch_shapes=[pltpu.VMEM((SEQ, LANES), F32)] * 2 + [pltpu.VMEM((LANES, SEQ), F32)] * 2,
        compiler_params=_params(("parallel", "parallel")),
    )(flat(qkv), flat(qkv), flat(qkv), flat(dout), lse, delta)
    return [o.reshape(n_seq, d, l, D_MODEL) for o in outs]


def _attn_combine_bwd(grads, rope, name):
    n_seq = grads[0][0].shape[0]
    t = n_seq * SEQ
    tm = MERGE_TILE

    def body(*refs):
        c_ref, s_ref, o_ref, nat4_ref, nat16_ref = refs[9:]
        for sec in range(3):
            _load_dilated(refs[3 + sec], 4, nat4_ref)
            _load_dilated(refs[6 + sec], 16, nat16_ref)
            for p in range(LANE_GROUPS):
                blk = refs[sec][0, 0, :, p * LANES:(p + 1) * LANES] + nat4_ref[p] + nat16_ref[p]
                if sec < 2:
                    blk = blk * c_ref[...] - _swap_halves(blk) * s_ref[...]
                o_ref[:, sec * D_MODEL + p * LANES:sec * D_MODEL + (p + 1) * LANES] = blk.astype(BF16)

    tab = pl.BlockSpec((tm, LANES), lambda i: (i, 0))
    flat = [a for br in grads for a in br]
    in_specs = []
    for spec in _dilated_specs(tm, D_MODEL, lambda: 0):
        in_specs += [spec] * 3
    return pl.pallas_call(
        body, name=name, grid=(t // tm,), in_specs=in_specs + [tab, tab],
        out_specs=pl.BlockSpec((tm, ODD_IN), lambda i: (i, 0)),
        out_shape=jax.ShapeDtypeStruct((t, ODD_IN), BF16),
        scratch_shapes=[pltpu.VMEM((LANE_GROUPS, tm, LANES), F32)] * 2,
        compiler_params=_params(("parallel",)),
    )(*flat, *rope)


def _adamw(w, g, m, v):
    m = ADAM_B1 * m + (1.0 - ADAM_B1) * g
    v = ADAM_B2 * v + (1.0 - ADAM_B2) * jnp.square(g)
    m_hat = m / (1.0 - ADAM_B1 ** ADAM_STEP)
    v_hat = v / (1.0 - ADAM_B2 ** ADAM_STEP)
    delta = -ADAM_LR * (m_hat / (jnp.sqrt(v_hat) + ADAM_EPS) + ADAM_WD * w)
    return delta, m, v


def _adamw_sharded(parts, w, m, v, after, name):
    n_layers, rows, cols = w.shape
    tr = min(rows, 256)

    def body(*refs):
        p_refs = refs[:n_layers]
        w_ref, m_ref, v_ref, _, g_ref, d_ref, mo_ref, vo_ref = refs[n_layers:]
        layer = pl.program_id(0)
        g = None
        for l, p_ref in enumerate(p_refs):
            g_l = p_ref[0].astype(F32)
            for s in range(1, N_DEV):
                g_l = g_l + p_ref[s].astype(F32)
            g = g_l if g is None else jnp.where(layer == l, g_l, g)
        delta, mn, vn = _adamw(w_ref[0], g, m_ref[0], v_ref[0])
        g_ref[0] = g
        d_ref[0] = delta
        mo_ref[0] = mn
        vo_ref[0] = vn

    def part_spec(l):
        return pl.BlockSpec((N_DEV, tr, cols), lambda a, i: (0, jnp.where(a == l, i, 0), 0))

    row = pl.BlockSpec((1, tr, cols), lambda a, i: (a, i, 0))
    return pl.pallas_call(
        body, name=name, grid=(n_layers, rows // tr),
        in_specs=[part_spec(l) for l in range(n_layers)] + [row, row, row, pl.BlockSpec(memory_space=pl.ANY)],
        out_specs=[row] * 4, out_shape=[jax.ShapeDtypeStruct(w.shape, F32)] * 4,
        compiler_params=_params(("arbitrary", "arbitrary")),
    )(*parts, w, m, v, after)


def _small_update(gathered, where, weights, moments_m, moments_v, lb_index, name):
    n = len(weights)
    n_g = len(gathered)

    def body(*refs):
        g_refs = refs[:n_g]
        w_refs, m_refs, v_refs = refs[n_g:n_g + n], refs[n_g + n:n_g + 2 * n], refs[n_g + 2 * n:n_g + 3 * n]
        outs = refs[n_g + 3 * n:]

        def total(k):
            array, rows, lanes = where[k]
            ref = g_refs[array]
            index = (slice(None),) * (len(ref.shape) - 1) if rows is None else (rows, lanes)
            acc = ref[(0,) + index]
            for s in range(1, N_DEV):
                acc = acc + ref[(s,) + index]
            return acc

        loss_rows = total(n)
        outs[0][...] = jnp.sum(jnp.sum(loss_rows, axis=1, keepdims=True), axis=0, keepdims=True)
        for k in range(n):
            part = total(k)
            if k == lb_index:
                dlb = jnp.sum(part, axis=0, keepdims=True)
                tab = w_refs[k][...]
                e = jnp.exp(tab - jnp.max(tab, axis=0, keepdims=True))
                p = e / jnp.sum(e, axis=0, keepdims=True)
                first = lax.broadcasted_iota(jnp.int32, p.shape, 0) == 0
                grads = [(slice(None), p * (jnp.where(first, dlb, 0.0) - p[0:1, :] * dlb))]
            elif part.shape == w_refs[k].shape:
                grads = [(slice(None), part)]
            else:
                grads = [(slice(l, l + 1), jnp.sum(part[l * SUBLANES:(l + 1) * SUBLANES], axis=0, keepdims=True))
                         for l in range(w_refs[k].shape[0])]
            for rows, g in grads:
                delta, mn, vn = _adamw(w_refs[k][rows], g, m_refs[k][rows], v_refs[k][rows])
                outs[1 + 4 * k][rows] = g
                outs[2 + 4 * k][rows] = delta
                outs[3 + 4 * k][rows] = mn
                outs[4 + 4 * k][rows] = vn

    vmem = pl.BlockSpec(memory_space=pltpu.VMEM)
    out_shape = [jax.ShapeDtypeStruct((1, 1), F32)]
    for w in weights:
        out_shape += [jax.ShapeDtypeStruct(w.shape, F32)] * 4
    args = list(gathered) + list(weights) + list(moments_m) + list(moments_v)
    return pl.pallas_call(
        body, name=name, in_specs=[vmem] * len(args), out_specs=[vmem] * len(out_shape), out_shape=out_shape,
        compiler_params=pltpu.CompilerParams(vmem_limit_bytes=VMEM_LIMIT),
    )(*args)


def kernel(x, positions, norm_mix_pre, norm_mix_post, norm_ffn_pre, norm_ffn_post, w_in_even, lb_table, a_norm, b_ln_g, b_ln_b, b_ws, b_bias, w_out_even, w_in_odd, w_out_odd, w_ff1, w_ff2, loss_target, m_norm_mix_pre, m_norm_mix_post, m_norm_ffn_pre, m_norm_ffn_post, m_w_in_even, m_lb_table, m_a_norm, m_b_ln_g, m_b_ln_b, m_b_ws, m_b_bias, m_w_out_even, m_w_in_odd, m_w_out_odd, m_w_ff1, m_w_ff2, v_norm_mix_pre, v_norm_mix_post, v_norm_ffn_pre, v_norm_ffn_post, v_w_in_even, v_lb_table, v_a_norm, v_b_ln_g, v_b_ln_b, v_b_ws, v_b_bias, v_w_out_even, v_w_in_odd, v_w_out_odd, v_w_ff1, v_w_ff2):
    n_seq = x.shape[0]
    t = n_seq * SEQ
    x0 = x.reshape(t, D_MODEL)
    target = loss_target.reshape(t, D_MODEL)

    me = _my_slot().astype(jnp.int32).reshape(1)

    order = ["in_e", "out_e", "ff1_0", "ff2_0", "in_o", "out_o", "ff1_1", "ff2_1"]
    shards = dict(in_e=w_in_even[0], out_e=w_out_even[0], in_o=w_in_odd[0], out_o=w_out_odd[0],
                  ff1_0=w_ff1[0], ff1_1=w_ff1[1], ff2_0=w_ff2[0], ff2_1=w_ff2[1])
    by_columns = ("in_e", "in_o", "ff1_0", "ff1_1")

    def place(k, after):
        if k in by_columns:
            return _place_own_columns(shards[k], me, "place_" + k, after)
        return _place_own(shards[k], me, "place_" + k, False, after=after)

    gathers = {}
    send0, recv0, land0, _, token0 = _exchange_start([place(order[0], None)], [None], "gather_start_first")
    gathers[order[0]] = (land0[0], send0[0], recv0[0])
    sends, recvs, lands, _, g_token = _exchange_start([place(k, token0) for k in order[1:]],
                                                      [None] * (len(order) - 1), "gather_start")
    for k, land, send, recv in zip(order[1:], lands, sends, recvs):
        gathers[k] = (land, send, recv)

    def get_w(keys, after):
        lands_k, sends_k, recvs_k = zip(*[gathers[k] for k in keys])
        return _exchange_wait(list(lands_k), [None] * len(keys), list(sends_k), list(recvs_k), after,
                              "gather_wait_" + keys[0])

    sent = {}

    def put_g(group, blocks):
        keys = list(blocks)
        own = [_place_own(blocks[k], me, "own_" + k, True) for k in keys]
        send_sems, recv_sems, own, srcs, token = _exchange_start(own, [blocks[k] for k in keys], "scatter_start_" + group)
        sent[group] = (keys, own, srcs, send_sems, recv_sems)
        return token

    rope = _rope_tables(positions)
    bias_t = b_bias[0].T
    grads = _local_step(x0, target, rope, norm_mix_pre, norm_mix_post, norm_ffn_pre, norm_ffn_post, lb_table,
                        a_norm, b_ln_g, b_ln_b, b_ws[0], bias_t, get_w, put_g, g_token)
    (dx0, loss_part, dg_mix_pre, dg_mix_post, dg_ffn_pre, dg_ffn_post, d_lb, d_a_norm, d_ln_g, d_ln_b, d_ws,
     d_bias_t) = grads

    packed = jnp.concatenate([dg_mix_pre, dg_mix_post, dg_ffn_pre, dg_ffn_post,
                              jnp.concatenate([d_lb, d_a_norm], axis=1), jnp.concatenate([d_ln_g, d_ln_b], axis=1),
                              loss_part], axis=0)
    small_lands = [_place_own(a, me, "own_small%d" % k, False, F32) for k, a in enumerate((packed, d_ws, d_bias_t))]
    s_send, s_recv, small_lands, _, after = _exchange_start(small_lands, [None] * 3, "gather_small_start")

    big = dict(w_in_even=(["in_e"], w_in_even, m_w_in_even, v_w_in_even),
               w_out_even=(["out_e"], w_out_even, m_w_out_even, v_w_out_even),
               w_in_odd=(["in_o"], w_in_odd, m_w_in_odd, v_w_in_odd),
               w_out_odd=(["out_o"], w_out_odd, m_w_out_odd, v_w_out_odd),
               w_ff1=(["ff1_0", "ff1_1"], w_ff1, m_w_ff1, v_w_ff1), w_ff2=(["ff2_0", "ff2_1"], w_ff2, m_w_ff2, v_w_ff2))
    recv, big_out = {}, {}
    for groups, names in ((("ffn1", "ffn0"), ("w_ff1", "w_ff2")), (("mix1",), ("w_in_odd", "w_out_odd")),
                          (("mix0",), ("w_in_even", "w_out_even"))):
        for group in groups:
            keys, own, srcs, send_sems, recv_sems = sent[group]
            recv.update(zip(keys, _exchange_wait(own, srcs, send_sems, recv_sems, after, "scatter_wait_" + group)))
        for nm in names:
            keys, w, m, v = big[nm]
            big_out[nm] = _adamw_sharded([recv[k] for k in keys], w, m, v, after, "adamw_" + nm)
            after = big_out[nm][0]
    big_out = [big_out[nm] for nm in ("w_in_even", "w_out_even", "w_in_odd", "w_out_odd", "w_ff1", "w_ff2")]
    gathered = _exchange_wait(small_lands, [None] * 3, s_send, s_recv, after, "gather_small_wait")
    rows8 = lambda k: slice(SUBLANES * k, SUBLANES * (k + 1))
    left, right, every = slice(0, A_WIDTH), slice(A_WIDTH, 2 * A_WIDTH), slice(None)
    where = [(0, slice(0, 16), every), (0, slice(16, 32), every), (0, slice(32, 48), every), (0, slice(48, 64), every),
             (0, rows8(8), left), (0, rows8(8), right), (0, rows8(9), left), (0, rows8(9), right),
             (1, None, None), (2, None, None), (0, rows8(10), every)]
    small_w = [norm_mix_pre, norm_mix_post, norm_ffn_pre, norm_ffn_post, lb_table, a_norm, b_ln_g, b_ln_b,
               b_ws[0], bias_t]
    small_m = [m_norm_mix_pre, m_norm_mix_post, m_norm_ffn_pre, m_norm_ffn_post, m_lb_table, m_a_norm, m_b_ln_g,
               m_b_ln_b, m_b_ws[0], m_b_bias[0].T]
    small_v = [v_norm_mix_pre, v_norm_mix_post, v_norm_ffn_pre, v_norm_ffn_post, v_lb_table, v_a_norm, v_b_ln_g,
               v_b_ln_b, v_b_ws[0], v_b_bias[0].T]
    small_out = _small_update(gathered, where, small_w, small_m, small_v, 4, "small_update")
    loss = small_out[0].reshape(())
    small = [small_out[1 + 4 * k:5 + 4 * k] for k in range(len(small_w))]
    small[8] = [a[None] for a in small[8]]
    small[9] = [a.T[None] for a in small[9]]

    per_weight = small[0:4] + [big_out[0]] + small[4:10] + big_out[1:6]
    grad_x = dx0.reshape(x.shape)
    out = [loss, grad_x]
    for kind in range(4):
        out += [p[kind] for p in per_weight]
    return tuple(out)


def _local_step(x0, target, rope, norm_mix_pre, norm_mix_post, norm_ffn_pre, norm_ffn_post, lb_table, a_norm,
                b_ln_g, b_ln_b, ws, bias_t, get_w, put_g, token):
    def gain(a, l, tok):
        return a[l:l + 1] if tok is None else a[l:l + 1] + tok[0:1, 0:1]

    full = lambda a: a.reshape(-1, D_MODEL)
    owners = lambda a: a.reshape((N_DEV, -1) + a.shape[1:])

    (g_in_e,) = get_w(["in_e"], token)
    proj, h_mix0 = _norm_inproj(x0, gain(norm_mix_pre, 0, token), g_in_e, "inproj_even")
    mixed, pre_a, states = _hgrn2_fwd(proj, lb_table, a_norm, "hgrn2_fwd")
    mixed = _gmlp_fwd(proj, mixed, b_ln_g, b_ln_b, ws, bias_t, "gmlp_fwd")
    w_out_e = full(get_w(["out_e"], mixed)[0])
    x1, mix0 = _outproj([mixed], w_out_e, x0, gain(norm_mix_post, 0, None), "outproj_even")
    w1_0, w2_0 = get_w(["ff1_0", "ff2_0"], x1)
    w2_0 = full(w2_0)
    x2, y0, h_ffn0, r0 = _ffn_fwd(x1, gain(norm_ffn_pre, 0, None), w1_0, w2_0, gain(norm_ffn_post, 0, None), "ffn_fwd_0")
    (g_in_o,) = get_w(["in_o"], x2)
    *qkv, h_mix1 = _norm_inproj_rope(x2, gain(norm_mix_pre, 1, None), g_in_o, rope, "inproj_odd")
    branches = [_attn_branch_fwd(a, "attn_fwd_d%d" % d) for a, d in zip(qkv, C_DILATIONS)]
    attn, attn_b, lse = _attn_merge(branches, "attn_merge")
    w_out_o = full(get_w(["out_o"], attn_b)[0])
    x3, mix1 = _outproj([attn_b], w_out_o, x2, gain(norm_mix_post, 1, None), "outproj_odd")
    w1_1, w2_1 = get_w(["ff1_1", "ff2_1"], x3)
    w2_1 = full(w2_1)
    dx4, y1, h_ffn1, r1, loss_part = _ffn_fwd(x3, gain(norm_ffn_pre, 1, None), w1_1, w2_1, gain(norm_ffn_post, 1, None),
                                              "ffn_fwd_1", target)

    dx3, dy1, da1, dg_ffn_pre1, dg_ffn_post1 = _ffn_bwd(
        dx4, x3, y1, r1, gain(norm_ffn_pre, 1, None), w1_1, w2_1, gain(norm_ffn_post, 1, None), "ffn_bwd_1")
    gw_ff1_1 = _grad_w(h_ffn1, da1, True, "grad_w_ff1_1")
    gw_ff2_1 = _grad_w(r1, dy1, False, "grad_w_ff2_1")
    tok = put_g("ffn1", dict(ff1_1=gw_ff1_1, ff2_1=owners(gw_ff2_1)))
    *dattn, delta, dz1, dg_mix_post1 = _outproj_bwd_attn(dx3, mix1, gain(norm_mix_post, 1, tok), w_out_o, attn,
                                                  "outproj_bwd_odd")
    gw_out_o = _grad_w(attn_b, dz1, False, "grad_w_out_odd")
    per_seq = lambda a: a.reshape(-1, SEQ, LANES)
    grads_c = [_attn_branch_bwd(qkv[b], dattn[b], per_seq(lse), per_seq(delta), "attn_bwd_d%d" % d)
               for b, d in enumerate(C_DILATIONS)]
    dqkv = _attn_combine_bwd(grads_c, rope, "attn_combine_bwd")
    gw_in_o = _grad_w(h_mix1, dqkv, True, "grad_w_in_odd")
    tok = put_g("mix1", dict(out_o=owners(gw_out_o), in_o=gw_in_o))
    dx2, dg_mix_pre1 = _inproj_bwd(dqkv, g_in_o, dx3, x2, gain(norm_mix_pre, 1, tok), "inproj_bwd_odd")

    dx1, dy0, da0, dg_ffn_pre0, dg_ffn_post0 = _ffn_bwd(
        dx2, x1, y0, r0, gain(norm_ffn_pre, 0, None), w1_0, w2_0, gain(norm_ffn_post, 0, None), "ffn_bwd_0")
    gw_ff1_0 = _grad_w(h_ffn0, da0, True, "grad_w_ff1_0")
    gw_ff2_0 = _grad_w(r0, dy0, False, "grad_w_ff2_0")
    tok = put_g("ffn0", dict(ff1_0=gw_ff1_0, ff2_0=owners(gw_ff2_0)))
    dcat, dz0, dg_mix_post0 = _outproj_bwd(dx1, mix0, gain(norm_mix_post, 0, tok), w_out_e, "outproj_bwd_even")
    gw_out_e = _grad_w(mixed, dz0, False, "grad_w_out_even")
    dproj, d_lb, d_a_norm = _hgrn2_bwd(proj, dcat, pre_a, states, lb_table, a_norm, "hgrn2_bwd")
    dproj, d_ln_g, d_ln_b, d_ws, d_bias_t = _gmlp_bwd(proj, dcat, dproj, b_ln_g, b_ln_b, ws, bias_t, "gmlp_bwd")
    gw_in_e = _grad_w(h_mix0, dproj, True, "grad_w_in_even")
    tok = put_g("mix0", dict(out_e=owners(gw_out_e), in_e=gw_in_e))
    dx0, dg_mix_pre0 = _inproj_bwd(dproj, g_in_e, dx1, x0, gain(norm_mix_pre, 0, tok), "inproj_bwd_even")

    layers = lambda a, b: jnp.concatenate([a, b], axis=0)
    return (dx0, loss_part, layers(dg_mix_pre0, dg_mix_pre1), layers(dg_mix_post0, dg_mix_post1),
            layers(dg_ffn_pre0, dg_ffn_pre1), layers(dg_ffn_post0, dg_ffn_post1),
            d_lb, d_a_norm, d_ln_g, d_ln_b, d_ws, d_bias_t)
```

```python
import functools
import math

import jax
import jax.numpy as jnp
from jax import lax
from jax.experimental import pallas as pl
from jax.experimental.pallas import tpu as pltpu

F32 = jnp.float32
BF16 = jnp.bfloat16
MESH = pl.DeviceIdType.MESH

N_DEV = 8
D_MODEL = 1024
SEQ = 2048
EPS = 1e-6
A_WIDTH = 512
A_HEADS = 4
HEAD_A = 128
B_WIDTH = 512
B_GROUPS = 4
B_CHUNK = 128
C_HEADS = 16
C_HEAD_DIM = 64
C_ROT_HALF = 8
ROPE_THETA = 500000.0
C_DILATIONS = (1, 4, 16)
C_BLOCK = 128
D_FF = 4096
EVEN_IN = 3072
ODD_IN = 3072

ADAM_LR = 0.001
ADAM_B1 = 0.9
ADAM_B2 = 0.999
ADAM_EPS = 1e-08
ADAM_WD = 0.01
ADAM_STEP = 10

LANES = 128
SUBLANES = 8
ROW_TILE = 512
PROJ_TILE = 1024
PROJ_COLS = 768
MERGE_TILE = 256
SUB_CHUNK = 16
HGRN_BLOCK = 256
NEG = -1e30
VMEM_LIMIT = 56 * 1024 * 1024


def _params(sem):
    return pltpu.CompilerParams(dimension_semantics=sem, vmem_limit_bytes=VMEM_LIMIT)


def _dot(a, b):
    return jnp.dot(a, b, preferred_element_type=F32)


def _dot_nt(a, b):
    return lax.dot_general(a, b, (((1,), (1,)), ((), ())), preferred_element_type=F32)


def _dot_tn(a, b):
    return lax.dot_general(a, b, (((0,), (0,)), ((), ())), preferred_element_type=F32)


def _rms(x, g):
    r = lax.rsqrt(jnp.mean(x * x, axis=-1, keepdims=True) + EPS)
    return x * r * g


def _rms_bwd(x, g, dy):
    r = lax.rsqrt(jnp.mean(x * x, axis=-1, keepdims=True) + EPS)
    dyg = dy * g
    dx = r * dyg - x * (r * r * r) * jnp.mean(x * dyg, axis=-1, keepdims=True)
    return dx, dy * x * r


def _split3(x):
    hi = x.astype(BF16)
    rest = x - hi.astype(F32)
    mid = rest.astype(BF16)
    return hi, mid, (rest - mid.astype(F32)).astype(BF16)


def _mask_dot(mask, x):
    m = mask.astype(BF16)
    hi, mid, lo = _split3(x)
    return _dot(m, hi) + (_dot(m, mid) + _dot(m, lo))


def _dot_mask(x, mask):
    m = mask.astype(BF16)
    hi, mid, lo = _split3(x)
    return _dot(hi, m) + (_dot(mid, m) + _dot(lo, m))


def _rows8(v):
    return v.reshape(v.shape[0] // SUBLANES, SUBLANES, v.shape[1]).sum(axis=0)


def _sigmoid(x):
    return 1.0 / (1.0 + jnp.exp(-x))


def _gelu(x):
    return 0.5 * x * (1.0 + jnp.tanh(math.sqrt(2.0 / math.pi) * (x + 0.044715 * (x * x * x))))


def _acc_rows8(ref, val, first):
    @pl.when(first)
    def _():
        ref[...] = val

    @pl.when(jnp.logical_not(first))
    def _():
        ref[...] += val


def _my_slot():
    return 4 * lax.axis_index("x") + 2 * lax.axis_index("y") + lax.axis_index("c")


def _peer(r):
    x, y, c = lax.axis_index("x"), lax.axis_index("y"), lax.axis_index("c")
    px = 1 - x if (r >> 2) & 1 else x
    py = 1 - y if (r >> 1) & 1 else y
    pc = 1 - c if r & 1 else c
    return (px, py, pc), 4 * px + 2 * py + pc


HBM_SPEC = pl.BlockSpec(memory_space=pltpu.HBM)
SEM_SPEC = pl.BlockSpec(memory_space=pltpu.SEMAPHORE)
SPLIT_EFFECT = pltpu.SideEffectType.DATAFLOW_SIDE_EFFECTING


def _split_copies(land_ref, src_ref, send_sem, recv_sem):
    me = _my_slot()
    copies = []
    for r in range(1, N_DEV):
        peer, slot = _peer(r)
        src = _slot(land_ref, me) if src_ref is None else _slot(src_ref, slot)
        copies.append(pltpu.make_async_remote_copy(
            src_ref=src, dst_ref=_slot(land_ref, me), send_sem=send_sem, recv_sem=recv_sem,
            device_id=peer, device_id_type=MESH))
    return copies


def _slot(ref, s):
    if len(ref.shape) == 2:
        c = ref.shape[1] // N_DEV
        return ref.at[:, pl.ds(pl.multiple_of(s * c, LANES), c)]
    return ref.at[s]


def _exchange_start(lands, sources, name):
    n = len(lands)
    given = [s for s in sources if s is not None]
    arrays = list(lands) + given

    def body(*refs):
        land_refs, src_refs = refs[:n], list(refs[n:n + len(given)])
        sems = refs[len(arrays):len(arrays) + 2 * n]
        token = refs[-1]
        for k in range(n):
            src_ref = None if sources[k] is None else src_refs.pop(0)
            for copy in _split_copies(land_refs[k], src_ref, sems[k], sems[n + k]):
                copy.start()
        token[...] = jnp.zeros_like(token)

    outs = pl.pallas_call(
        body, name=name,
        out_shape=(pltpu.SemaphoreType.DMA(()),) * (2 * n) + tuple(pltpu.HBM(a.shape, a.dtype) for a in arrays)
        + (jax.ShapeDtypeStruct((SUBLANES, LANES), F32),),
        in_specs=[HBM_SPEC] * len(arrays),
        out_specs=(SEM_SPEC,) * (2 * n) + (HBM_SPEC,) * len(arrays) + (pl.BlockSpec(memory_space=pltpu.VMEM),),
        input_output_aliases={i: 2 * n + i for i in range(len(arrays))},
        compiler_params=pltpu.CompilerParams(has_side_effects=SPLIT_EFFECT),
    )(*[pltpu.with_memory_space_constraint(a, pltpu.HBM) for a in arrays])
    return list(outs[:n]), list(outs[n:2 * n]), list(outs[2 * n:3 * n]), list(outs[3 * n:-1]), outs[-1]


def _exchange_wait(lands, sources, send_sems, recv_sems, after, name):
    n = len(lands)
    given = [s for s in sources if s is not None]
    arrays = list(lands) + given

    def body(*refs):
        land_refs, src_refs = refs[:n], list(refs[n:n + len(given)])
        sems = refs[len(arrays):len(arrays) + 2 * n]
        for i in range(n):
            src_ref = None if sources[i] is None else src_refs.pop(0)
            copies = _split_copies(land_refs[i], src_ref, sems[i], sems[n + i])
            for copy in copies:
                copy.wait_recv()
            for copy in copies:
                copy.wait_send()

    outs = pl.pallas_call(
        body, name=name, out_shape=tuple(pltpu.HBM(a.shape, a.dtype) for a in arrays),
        in_specs=[HBM_SPEC] * len(arrays) + [SEM_SPEC] * (2 * n) + [pl.BlockSpec(memory_space=pl.ANY)],
        out_specs=(HBM_SPEC,) * len(arrays),
        input_output_aliases={i: i for i in range(len(arrays))},
        compiler_params=pltpu.CompilerParams(has_side_effects=SPLIT_EFFECT),
    )(*arrays, *send_sems, *recv_sems, after)
    return list(outs[:n])


def _place_own(a, me, name, own_block, dtype=BF16, after=None):
    shape = a.shape[1:] if own_block else a.shape
    cols = shape[-1]
    a3 = a.reshape((N_DEV if own_block else 1, -1, cols))
    rows = a3.shape[1]
    tr = min(rows, 512)

    def body(me_ref, a_ref, _, o_ref):
        o_ref[...] = a_ref[...].astype(dtype)

    grid_spec = pltpu.PrefetchScalarGridSpec(
        num_scalar_prefetch=1, grid=(rows // tr,),
        in_specs=[pl.BlockSpec((1, tr, cols), lambda i, me_ref: (me_ref[0] if own_block else 0, i, 0)),
                  pl.BlockSpec(memory_space=pl.ANY)],
        out_specs=pl.BlockSpec((1, tr, cols), lambda i, me_ref: (me_ref[0], i, 0)))
    out = pl.pallas_call(
        body, name=name, grid_spec=grid_spec, out_shape=jax.ShapeDtypeStruct((N_DEV, rows, cols), dtype),
        compiler_params=_params(("arbitrary",)),
    )(me, a3, a3 if after is None else after)
    return out.reshape((N_DEV,) + shape)


def _place_own_columns(a, me, name, after=None):
    rows, cols = a.shape
    tr = min(rows, 512)

    def body(me_ref, a_ref, _, o_ref):
        o_ref[...] = a_ref[...].astype(BF16)

    grid_spec = pltpu.PrefetchScalarGridSpec(
        num_scalar_prefetch=1, grid=(rows // tr,),
        in_specs=[pl.BlockSpec((tr, cols), lambda i, me_ref: (i, 0)), pl.BlockSpec(memory_space=pl.ANY)],
        out_specs=pl.BlockSpec((tr, cols), lambda i, me_ref: (i, me_ref[0])))
    return pl.pallas_call(
        body, name=name, grid_spec=grid_spec, out_shape=jax.ShapeDtypeStruct((rows, N_DEV * cols), BF16),
        compiler_params=_params(("arbitrary",)),
    )(me, a, a if after is None else after)


def _rope_tables(positions):
    in_head = jnp.arange(LANES) % C_HEAD_DIM
    inv = ROPE_THETA ** (-(in_head % C_ROT_HALF).astype(F32) / C_ROT_HALF)
    ang = positions.reshape(-1)[:, None].astype(F32) * inv
    rotated = in_head < 2 * C_ROT_HALF
    sin = jnp.sin(ang)
    return (jnp.where(rotated, jnp.cos(ang), 1.0),
            jnp.where(in_head < C_ROT_HALF, -sin, jnp.where(rotated, sin, 0.0)))


def _swap_halves(x):
    lane = lax.broadcasted_iota(jnp.int32, x.shape, 1) % C_HEAD_DIM
    return jnp.where(lane < C_ROT_HALF, pltpu.roll(x, LANES - C_ROT_HALF, 1), pltpu.roll(x, C_ROT_HALF, 1))


def _norm_inproj(x, g, w, name):
    t = x.shape[0]
    n = w.shape[1]
    tm, tn = PROJ_TILE, PROJ_COLS

    def body(x_ref, g_ref, w_ref, o_ref, h_ref):
        @pl.when(pl.program_id(1) == 0)
        def _():
            h_ref[...] = _rms(x_ref[...], g_ref[...]).astype(BF16)

        o_ref[...] = _dot(h_ref[...], w_ref[...])

    return pl.pallas_call(
        body, name=name, grid=(t // tm, n // tn),
        in_specs=[pl.BlockSpec((tm, D_MODEL), lambda i, j: (i, 0)), pl.BlockSpec((1, D_MODEL), lambda i, j: (0, 0)),
                  pl.BlockSpec((D_MODEL, tn), lambda i, j: (0, j))],
        out_specs=[pl.BlockSpec((tm, tn), lambda i, j: (i, j)), pl.BlockSpec((tm, D_MODEL), lambda i, j: (i, 0))],
        out_shape=[jax.ShapeDtypeStruct((t, n), F32), jax.ShapeDtypeStruct((t, D_MODEL), BF16)],
        compiler_params=_params(("parallel", "arbitrary")),
    )(x, g, w)


def _dilated_specs(tm, width, col_of):
    per_seq = SEQ // tm
    specs = []
    for d in C_DILATIONS:
        specs.append(pl.BlockSpec(
            (1, d, tm // d, width), lambda i, *rest: (i // per_seq, 0, i % per_seq, col_of(*rest))))
    return specs


def _dilated_shapes(n_seq, cols, dtype):
    return [jax.ShapeDtypeStruct((n_seq, d, SEQ // d, cols), dtype) for d in C_DILATIONS]


def _store_dilated(src_ref, out_refs, dtype):
    groups, tm, _ = src_ref.shape
    for d, o_ref in zip(C_DILATIONS, out_refs):
        for r in range(d):
            rows = pl.ds(r, tm // d, stride=d) if d > 1 else slice(None)
            for p in range(groups):
                o_ref[0, r, :, p * LANES:(p + 1) * LANES] = src_ref.at[p][rows, :].astype(dtype)


def _load_dilated(in_ref, d, dst_ref):
    groups, tm, _ = dst_ref.shape
    for r in range(d):
        rows = pl.ds(r, tm // d, stride=d)
        for p in range(groups):
            dst_ref.at[p][rows, :] = in_ref[0, r, :, p * LANES:(p + 1) * LANES].astype(F32)


def _norm_inproj_rope(x, g, w, rope, name):
    t = x.shape[0]
    n = w.shape[1]
    tm, nb = PROJ_TILE, PROJ_COLS

    def body(x_ref, g_ref, w_ref, c_ref, s_ref, o1_ref, o4_ref, o16_ref, h_ref, tile_ref):
        j = pl.program_id(1)

        @pl.when(j == 0)
        def _():
            h_ref[...] = _rms(x_ref[...], g_ref[...]).astype(BF16)

        acc = _dot(h_ref[...], w_ref[...])
        for p in range(nb // LANES):
            blk = acc[:, p * LANES:(p + 1) * LANES]
            roped = blk * c_ref[...] + _swap_halves(blk) * s_ref[...]
            piece = j * (nb // LANES) + p
            is_qk = piece < 2 * (D_MODEL // LANES)
            tile_ref[p] = jnp.where(is_qk, roped, blk) * jnp.where(piece < D_MODEL // LANES, QK_SCALE, 1.0)
        _store_dilated(tile_ref, (o1_ref, o4_ref, o16_ref), BF16)

    return pl.pallas_call(
        body, name=name, grid=(t // tm, n // nb),
        in_specs=[pl.BlockSpec((tm, D_MODEL), lambda i, j: (i, 0)), pl.BlockSpec((1, D_MODEL), lambda i, j: (0, 0)),
                  pl.BlockSpec((D_MODEL, nb), lambda i, j: (0, j)),
                  pl.BlockSpec((tm, LANES), lambda i, j: (i, 0)), pl.BlockSpec((tm, LANES), lambda i, j: (i, 0))],
        out_specs=_dilated_specs(tm, nb, lambda j: j) + [pl.BlockSpec((tm, D_MODEL), lambda i, j: (i, 0))],
        out_shape=_dilated_shapes(t // SEQ, n, BF16) + [jax.ShapeDtypeStruct((t, D_MODEL), BF16)],
        scratch_shapes=[pltpu.VMEM((nb // LANES, tm, LANES), F32)],
        compiler_params=_params(("parallel", "arbitrary")),
    )(x, g, w, *rope)


def _outproj(parts, w, x, g, name):
    t = x.shape[0]
    tm = PROJ_TILE
    n = len(parts)
    widths = [p.shape[1] for p in parts]

    def body(*refs):
        p_refs = refs[:n]
        w_ref, x_ref, g_ref, xo_ref, mix_ref = refs[n:]
        mix = None
        off = 0
        for p_ref, wd in zip(p_refs, widths):
            term = _dot(p_ref[...].astype(BF16), w_ref[off:off + wd, :])
            mix = term if mix is None else mix + term
            off += wd
        mix_ref[...] = mix
        xo_ref[...] = x_ref[...] + _rms(mix, g_ref[...])

    row = lambda i: (i, 0)
    return pl.pallas_call(
        body, name=name, grid=(t // tm,),
        in_specs=[pl.BlockSpec((tm, wd), row) for wd in widths] + [
            pl.BlockSpec((sum(widths), D_MODEL), lambda i: (0, 0)),
            pl.BlockSpec((tm, D_MODEL), row), pl.BlockSpec((1, D_MODEL), lambda i: (0, 0))],
        out_specs=[pl.BlockSpec((tm, D_MODEL), row)] * 2,
        out_shape=[jax.ShapeDtypeStruct((t, D_MODEL), F32)] * 2,
        compiler_params=_params(("parallel",)),
    )(*parts, w, x, g)


def _outproj_bwd(dx, mix, g, w, name):
    t = dx.shape[0]
    tm = PROJ_TILE
    k = w.shape[0]

    def body(dx_ref, mix_ref, g_ref, w_ref, dcat_ref, dz_ref, dg_ref):
        dz, dgr = _rms_bwd(mix_ref[...], g_ref[...], dx_ref[...])
        dzb = dz.astype(BF16)
        dz_ref[...] = dzb
        dcat_ref[...] = _dot_nt(dzb, w_ref[...])
        _acc_rows8(dg_ref, _rows8(dgr), pl.program_id(0) == 0)

    row = lambda i: (i, 0)
    return pl.pallas_call(
        body, name=name, grid=(t // tm,),
        in_specs=[pl.BlockSpec((tm, D_MODEL), row), pl.BlockSpec((tm, D_MODEL), row),
                  pl.BlockSpec((1, D_MODEL), lambda i: (0, 0)), pl.BlockSpec((k, D_MODEL), lambda i: (0, 0))],
        out_specs=[pl.BlockSpec((tm, k), row), pl.BlockSpec((tm, D_MODEL), row),
                   pl.BlockSpec((SUBLANES, D_MODEL), lambda i: (0, 0))],
        out_shape=[jax.ShapeDtypeStruct((t, k), F32), jax.ShapeDtypeStruct((t, D_MODEL), BF16),
                   jax.ShapeDtypeStruct((SUBLANES, D_MODEL), F32)],
        compiler_params=_params(("arbitrary",)),
    )(dx, mix, g, w)


def _outproj_bwd_attn(dx, mix, g, w, out, name):
    t = dx.shape[0]
    tm = MERGE_TILE

    def body(dx_ref, mix_ref, g_ref, w_ref, out_ref, do1, do4, do16, dl_ref, dz_ref, dg_ref, tile_ref):
        dz, dgr = _rms_bwd(mix_ref[...], g_ref[...], dx_ref[...])
        dzb = dz.astype(BF16)
        dz_ref[...] = dzb
        _acc_rows8(dg_ref, _rows8(dgr), pl.program_id(0) == 0)
        dout = _dot_nt(dzb, w_ref[...])
        for p in range(LANE_GROUPS):
            tile_ref[p] = dout[:, p * LANES:(p + 1) * LANES]
        _store_dilated(tile_ref, (do1, do4, do16), BF16)
        column = lax.broadcasted_iota(jnp.int32, (D_MODEL, LANES), 0) // C_HEAD_DIM
        head = lax.broadcasted_iota(jnp.int32, (D_MODEL, LANES), 1)
        dl_ref[...] = _dot_mask(dout * out_ref[...], column == head)

    row = lambda i: (i, 0)
    n_seq = t // SEQ
    return pl.pallas_call(
        body, name=name, grid=(t // tm,),
        in_specs=[pl.BlockSpec((tm, D_MODEL), row), pl.BlockSpec((tm, D_MODEL), row),
                  pl.BlockSpec((1, D_MODEL), lambda i: (0, 0)), pl.BlockSpec((D_MODEL, D_MODEL), lambda i: (0, 0)),
                  pl.BlockSpec((tm, D_MODEL), row)],
        out_specs=_dilated_specs(tm, D_MODEL, lambda: 0) + [
            pl.BlockSpec((tm, LANES), row), pl.BlockSpec((tm, D_MODEL), row),
            pl.BlockSpec((SUBLANES, D_MODEL), lambda i: (0, 0))],
        out_shape=_dilated_shapes(n_seq, D_MODEL, BF16) + [
            jax.ShapeDtypeStruct((t, LANES), F32), jax.ShapeDtypeStruct((t, D_MODEL), BF16),
            jax.ShapeDtypeStruct((SUBLANES, D_MODEL), F32)],
        scratch_shapes=[pltpu.VMEM((LANE_GROUPS, tm, LANES), F32)],
        compiler_params=_params(("arbitrary",)),
    )(dx, mix, g, w, out)


def _inproj_bwd(dproj, w, dx, x, g, name):
    t = x.shape[0]
    n = w.shape[1]
    tm = ROW_TILE

    def body(dp_ref, w_ref, dx_ref, x_ref, g_ref, o_ref, dg_ref):
        dxn, dgr = _rms_bwd(x_ref[...], g_ref[...], _dot_nt(dp_ref[...], w_ref[...]))
        o_ref[...] = dx_ref[...] + dxn
        _acc_rows8(dg_ref, _rows8(dgr), pl.program_id(0) == 0)

    row = lambda i: (i, 0)
    return pl.pallas_call(
        body, name=name, grid=(t // tm,),
        in_specs=[pl.BlockSpec((tm, n), row), pl.BlockSpec((D_MODEL, n), lambda i: (0, 0)),
                  pl.BlockSpec((tm, D_MODEL), row), pl.BlockSpec((tm, D_MODEL), row),
                  pl.BlockSpec((1, D_MODEL), lambda i: (0, 0))],
        out_specs=[pl.BlockSpec((tm, D_MODEL), row), pl.BlockSpec((SUBLANES, D_MODEL), lambda i: (0, 0))],
        out_shape=[jax.ShapeDtypeStruct((t, D_MODEL), F32), jax.ShapeDtypeStruct((SUBLANES, D_MODEL), F32)],
        compiler_params=_params(("arbitrary",)),
    )(dproj, w, dx, x, g)


def _grad_w(a, b, col_blocks, name):
    t, k = a.shape
    n = b.shape[1]
    tk = min(k, 1024)
    per_owner = n // N_DEV
    tn = 2 * per_owner if col_blocks else min(n, 1024)

    def body(a_ref, b_ref, o_ref, at_ref):
        @pl.when(pl.program_id(1) == 0)
        def _():
            for c in range(t // ROW_TILE):
                rows = slice(c * ROW_TILE, (c + 1) * ROW_TILE)
                at_ref[:, rows] = a_ref[rows, :].T

        res = _dot(at_ref[...], b_ref[...]).astype(BF16)
        if col_blocks:
            o_ref[0] = res[:, :per_owner]
            o_ref[1] = res[:, per_owner:]
        else:
            o_ref[...] = res

    if col_blocks:
        out_spec = pl.BlockSpec((2, tk, per_owner), lambda i, j: (j, i, 0))
        out_shape = jax.ShapeDtypeStruct((N_DEV, k, per_owner), BF16)
    else:
        out_spec = pl.BlockSpec((tk, tn), lambda i, j: (i, j))
        out_shape = jax.ShapeDtypeStruct((k, n), BF16)
    return pl.pallas_call(
        body, name=name, grid=(k // tk, n // tn),
        in_specs=[pl.BlockSpec((t, tk), lambda i, j: (0, i)), pl.BlockSpec((t, tn), lambda i, j: (0, j))],
        out_specs=out_spec, out_shape=out_shape,
        scratch_shapes=[pltpu.VMEM((tk, t), BF16)],
        compiler_params=_params(("parallel", "arbitrary")),
    )(a, b)


FF_BLOCK = D_FF // N_DEV
FF_STEP = 1024
FF_STEPS = D_FF // FF_STEP


def _ffn_fwd(x, g_pre, w1, w2, g_post, name, target=None):
    t = x.shape[0]
    tm = PROJ_TILE

    def body(*refs):
        if target is None:
            x_ref, gp_ref, w1_ref, w2_ref, gq_ref, xo_ref, y_ref, h_ref, r_ref = refs
        else:
            x_ref, gp_ref, w1_ref, w2_ref, gq_ref, t_ref, xo_ref, y_ref, h_ref, r_ref, l_ref = refs
        i, j = pl.program_id(0), pl.program_id(1)

        @pl.when(j == 0)
        def _():
            h_ref[...] = _rms(x_ref[...], gp_ref[...]).astype(BF16)

        a = _dot(h_ref[...], w1_ref[...])
        r = jnp.square(jnp.maximum(a, 0.0)).astype(BF16)
        r_ref[...] = r
        term = _dot(r, w2_ref[...])

        @pl.when(j == 0)
        def _():
            y_ref[...] = term

        @pl.when(j > 0)
        def _():
            y_ref[...] += term

        @pl.when(j == FF_STEPS - 1)
        def _():
            x_new = x_ref[...] + _rms(y_ref[...], gq_ref[...])
            if target is None:
                xo_ref[...] = x_new
            else:
                diff = x_new - t_ref[...]
                xo_ref[...] = diff * (1.0 / D_MODEL)
                _acc_rows8(l_ref, _rows8(diff * diff) * (0.5 / D_MODEL), i == 0)

    row = lambda i, j: (i, 0)
    vec = pl.BlockSpec((1, D_MODEL), lambda i, j: (0, 0))
    in_specs = [pl.BlockSpec((tm, D_MODEL), row), vec, pl.BlockSpec((D_MODEL, FF_STEP), lambda i, j: (0, j)),
                pl.BlockSpec((FF_STEP, D_MODEL), lambda i, j: (j, 0)), vec]
    out_specs = [pl.BlockSpec((tm, D_MODEL), row)] * 3 + [pl.BlockSpec((tm, FF_STEP), lambda i, j: (i, j))]
    out_shape = [jax.ShapeDtypeStruct((t, D_MODEL), F32), jax.ShapeDtypeStruct((t, D_MODEL), F32),
                 jax.ShapeDtypeStruct((t, D_MODEL), BF16), jax.ShapeDtypeStruct((t, D_FF), BF16)]
    args = [x, g_pre, w1, w2, g_post]
    if target is not None:
        in_specs.append(pl.BlockSpec((tm, D_MODEL), row))
        out_specs.append(pl.BlockSpec((SUBLANES, D_MODEL), lambda i, j: (0, 0)))
        out_shape.append(jax.ShapeDtypeStruct((SUBLANES, D_MODEL), F32))
        args.append(target)
    return pl.pallas_call(
        body, name=name, grid=(t // tm, FF_STEPS), in_specs=in_specs, out_specs=out_specs, out_shape=out_shape,
        compiler_params=_params(("parallel" if target is None else "arbitrary", "arbitrary")),
    )(*args)


def _ffn_bwd(dxo, x, y, r, g_pre, w1, w2, g_post, name):
    t = x.shape[0]
    tm = ROW_TILE

    def body(dxo_ref, x_ref, y_ref, r_ref, gp_ref, w1_ref, w2_ref, gq_ref,
             dx_ref, dy_ref, da_ref, dgp_ref, dgq_ref, acc_ref):
        i, j = pl.program_id(0), pl.program_id(1)

        @pl.when(j == 0)
        def _():
            dy, dgr = _rms_bwd(y_ref[...], gq_ref[...], dxo_ref[...])
            dy_ref[...] = dy.astype(BF16)
            _acc_rows8(dgq_ref, _rows8(dgr), i == 0)

        dr = _dot_nt(dy_ref[...], w2_ref[...])
        da = (dr * (2.0 * jnp.sqrt(r_ref[...].astype(F32)))).astype(BF16)
        da_ref[...] = da
        term = _dot_nt(da, w1_ref[...])

        @pl.when(j == 0)
        def _():
            acc_ref[...] = term

        @pl.when(j > 0)
        def _():
            acc_ref[...] += term

        @pl.when(j == FF_STEPS - 1)
        def _():
            dxn, dgr = _rms_bwd(x_ref[...], gp_ref[...], acc_ref[...])
            dx_ref[...] = dxo_ref[...] + dxn
            _acc_rows8(dgp_ref, _rows8(dgr), i == 0)

    row = lambda i, j: (i, 0)
    vec = pl.BlockSpec((1, D_MODEL), lambda i, j: (0, 0))
    acc8 = pl.BlockSpec((SUBLANES, D_MODEL), lambda i, j: (0, 0))
    return pl.pallas_call(
        body, name=name, grid=(t // tm, FF_STEPS),
        in_specs=[pl.BlockSpec((tm, D_MODEL), row)] * 3 + [
            pl.BlockSpec((tm, FF_STEP), lambda i, j: (i, j)),
            vec, pl.BlockSpec((D_MODEL, FF_STEP), lambda i, j: (0, j)),
            pl.BlockSpec((FF_STEP, D_MODEL), lambda i, j: (j, 0)), vec],
        out_specs=[pl.BlockSpec((tm, D_MODEL), row), pl.BlockSpec((tm, D_MODEL), row),
                   pl.BlockSpec((tm, FF_STEP), lambda i, j: (i, j)), acc8, acc8],
        out_shape=[jax.ShapeDtypeStruct((t, D_MODEL), F32), jax.ShapeDtypeStruct((t, D_MODEL), BF16),
                   jax.ShapeDtypeStruct((t, D_FF), BF16),
                   jax.ShapeDtypeStruct((SUBLANES, D_MODEL), F32), jax.ShapeDtypeStruct((SUBLANES, D_MODEL), F32)],
        scratch_shapes=[pltpu.VMEM((tm, D_MODEL), F32)],
        compiler_params=_params(("arbitrary", "arbitrary")),
    )(dxo, x, y, r, g_pre, w1, w2, g_post)


def _lower_bound(table):
    e = jnp.exp(table - jnp.max(table, axis=0, keepdims=True))
    return e[0:1, :] / jnp.sum(e, axis=0, keepdims=True)


def _hgrn2_block(q_ref, f_ref, lb):
    tb = f_ref.shape[0]
    sig = _sigmoid(f_ref[...])
    f = lb + (1.0 - lb) * sig
    qraw = q_ref[...]
    sq = _sigmoid(qraw)
    r = lax.broadcasted_iota(jnp.int32, (tb, tb), 0)
    c = lax.broadcasted_iota(jnp.int32, (tb, tb), 1)
    same = (r // SUB_CHUNK) == (c // SUB_CHUNK)
    logf = jnp.log(f)
    gsum = _mask_dot(same & (c <= r), logf)
    glast = _mask_dot(same, logf)
    return dict(sig=sig, f=f, kk=1.0 - f, qraw=qraw, sq=sq, qs=qraw * sq, gsum=gsum,
                eg=jnp.exp(gsum), ekd=jnp.exp(glast - gsum), a=jnp.exp(glast))


def _head_sums(x):
    parts = [jnp.broadcast_to(jnp.sum(x[:, h * HEAD_A:(h + 1) * HEAD_A], axis=1, keepdims=True), (x.shape[0], HEAD_A))
             for h in range(A_HEADS)]
    return jnp.concatenate(parts, axis=1)


def _hgrn2_intra(g, kk, qs, v):
    row = lax.broadcasted_iota(jnp.int32, g.shape, 0)
    o = _head_sums(qs * kk) * v
    for j in range(1, SUB_CHUNK):
        decay = jnp.exp(jnp.where(row >= j, g - pltpu.roll(g, j, 0), NEG))
        o = o + _head_sums(qs * pltpu.roll(kk, j, 0) * decay) * pltpu.roll(v, j, 0)
    return o


def _hgrn2_intra_bwd(g, kk, qs, v, do):
    row = lax.broadcasted_iota(jnp.int32, g.shape, 0)
    dsc = _head_sums(do * v)
    dqs, dkk, dv = dsc * kk, dsc * qs, _head_sums(qs * kk) * do
    for j in range(1, SUB_CHUNK):
        k_dn = pltpu.roll(kk, j, 0)
        decay = jnp.exp(jnp.where(row >= j, g - pltpu.roll(g, j, 0), NEG))
        d_score = _head_sums(do * pltpu.roll(v, j, 0)) * decay
        dqs = dqs + d_score * k_dn
        dkk = dkk + pltpu.roll(d_score * qs, SUB_CHUNK - j, 0)
        dv = dv + pltpu.roll(_head_sums(qs * k_dn * decay) * do, SUB_CHUNK - j, 0)
    return dqs, dkk, dv


def _hgrn2_fwd(proj, lb_table, a_norm, name):
    t = proj.shape[0]
    tb = HGRN_BLOCK
    n_tb = SEQ // tb
    n_seq = t // SEQ
    n_sub = tb // SUB_CHUNK

    def body(q_ref, f_ref, i_ref, g_ref, lbt_ref, an_ref, o_ref, pre_ref, sts_ref, st_ref,
             gs_ref, kk_ref, qs_ref, eg_ref, ekd_ref, a_ref):
        @pl.when(pl.program_id(1) == 0)
        def _():
            st_ref[...] = jnp.zeros_like(st_ref)

        an = an_ref[...]
        blk = _hgrn2_block(q_ref, f_ref, _lower_bound(lbt_ref[...]))
        for ref, key in ((gs_ref, "gsum"), (kk_ref, "kk"), (qs_ref, "qs"), (eg_ref, "eg"), (ekd_ref, "ekd"), (a_ref, "a")):
            ref[...] = blk[key]

        def step(c, carry):
            rows = pl.ds(pl.multiple_of(c * SUB_CHUNK, SUB_CHUNK), SUB_CHUNK)
            kk, qs, v = kk_ref[rows, :], qs_ref[rows, :], i_ref[rows, :]
            o = _hgrn2_intra(gs_ref[rows, :], kk, qs, v)
            qg, kd, vb = (qs * eg_ref[rows, :]).astype(BF16), (kk * ekd_ref[rows, :]).astype(BF16), v.astype(BF16)
            for h in range(A_HEADS):
                lanes = slice(h * HEAD_A, (h + 1) * HEAD_A)
                st = st_ref[h]
                sts_ref[0, c, h] = st
                o_h = o[:, lanes] + _dot_nt(qg[:, lanes], st.astype(BF16))
                st_ref[h] = st * a_ref[rows, lanes][0:1] + _dot_tn(vb[:, lanes], kd[:, lanes])
                pre_ref[rows, lanes] = o_h
                graw = g_ref[rows, lanes]
                o_ref[rows, lanes] = (_rms(o_h, an[:, lanes]) * (graw * _sigmoid(graw))).astype(BF16)
            return carry

        lax.fori_loop(0, n_sub, step, 0, unroll=2)

    def col(k):
        return pl.BlockSpec((tb, A_WIDTH), lambda b, s, k=k: (b * n_tb + s, k))

    out_rows = pl.BlockSpec((tb, A_WIDTH), lambda b, s: (b * n_tb + s, 0))
    return pl.pallas_call(
        body, name=name, grid=(n_seq, n_tb),
        in_specs=[col(0), col(1), col(2), col(3),
                  pl.BlockSpec((3, A_WIDTH), lambda b, s: (0, 0)), pl.BlockSpec((1, A_WIDTH), lambda b, s: (0, 0))],
        out_specs=[out_rows, out_rows,
                   pl.BlockSpec((1, n_sub, A_HEADS, HEAD_A, HEAD_A), lambda b, s: (b * n_tb + s, 0, 0, 0, 0))],
        out_shape=[jax.ShapeDtypeStruct((t, D_MODEL), BF16), jax.ShapeDtypeStruct((t, A_WIDTH), F32),
                   jax.ShapeDtypeStruct((n_seq * n_tb, n_sub, A_HEADS, HEAD_A, HEAD_A), F32)],
        scratch_shapes=[pltpu.VMEM((A_HEADS, HEAD_A, HEAD_A), F32)] + [pltpu.VMEM((tb, A_WIDTH), F32)] * 6,
        compiler_params=_params(("parallel", "arbitrary")),
    )(proj, proj, proj, proj, lb_table, a_norm)


def _hgrn2_bwd(proj, dcat, pre, states, lb_table, a_norm, name):
    t = proj.shape[0]
    tb = HGRN_BLOCK
    n_tb = SEQ // tb
    n_seq = t // SEQ
    n_sub = tb // SUB_CHUNK

    def body(q_ref, f_ref, i_ref, g_ref, do_ref, pre_ref, sts_ref, lbt_ref, an_ref, dp_ref, dlb_ref, dan_ref, dst_ref,
             gs_ref, kk_ref, qs_ref, eg_ref, ekd_ref, a_ref, dpre_ref, dlf_ref, dqs_ref, dkk_ref):
        b, s = pl.program_id(0), pl.program_id(1)

        @pl.when(s == 0)
        def _():
            dst_ref[...] = jnp.zeros_like(dst_ref)

        @pl.when((b == 0) & (s == 0))
        def _():
            dlb_ref[...] = jnp.zeros_like(dlb_ref)
            dan_ref[...] = jnp.zeros_like(dan_ref)

        lb = _lower_bound(lbt_ref[...])
        an = an_ref[...]
        heads = [slice(h * HEAD_A, (h + 1) * HEAD_A) for h in range(A_HEADS)]
        blk = _hgrn2_block(q_ref, f_ref, lb)
        for ref, key in ((gs_ref, "gsum"), (kk_ref, "kk"), (qs_ref, "qs"), (eg_ref, "eg"), (ekd_ref, "ekd"), (a_ref, "a")):
            ref[...] = blk[key]
        for h, lanes in enumerate(heads):
            graw, o = g_ref[:, lanes], pre_ref[:, lanes]
            sg = _sigmoid(graw)
            dout = do_ref[:, lanes]
            d_o, dgr = _rms_bwd(o, an[:, lanes], dout * (graw * sg))
            dan_ref[0:1, lanes] += jnp.sum(dgr, axis=0, keepdims=True)
            dp_ref[:, 3 * A_WIDTH + h * HEAD_A:3 * A_WIDTH + (h + 1) * HEAD_A] = (
                dout * _rms(o, an[:, lanes]) * (sg * (1.0 + graw * (1.0 - sg)))).astype(BF16)
            dpre_ref[:, lanes] = d_o

        tri_t = (lax.broadcasted_iota(jnp.int32, (SUB_CHUNK, SUB_CHUNK), 0)
                 <= lax.broadcasted_iota(jnp.int32, (SUB_CHUNK, SUB_CHUNK), 1)).astype(F32)

        def back(k, carry):
            c = n_sub - 1 - k
            rows = pl.ds(pl.multiple_of(c * SUB_CHUNK, SUB_CHUNK), SUB_CHUNK)
            g, kk, qs, v, d_o = gs_ref[rows, :], kk_ref[rows, :], qs_ref[rows, :], i_ref[rows, :], dpre_ref[rows, :]
            eg, ekd, a = eg_ref[rows, :], ekd_ref[rows, :], a_ref[rows, :]
            dqs, dkk, dv = _hgrn2_intra_bwd(g, kk, qs, v, d_o)
            qg_f, kd_f = qs * eg, kk * ekd
            qg, kd, vb, dob = qg_f.astype(BF16), kd_f.astype(BF16), v.astype(BF16), d_o.astype(BF16)
            dqg, dkd, da, dv_st = [], [], [], []
            for h, lanes in enumerate(heads):
                st, dst = sts_ref[0, c, h], dst_ref[h]
                dstb = dst.astype(BF16)
                dqg.append(_dot(dob[:, lanes], st.astype(BF16)))
                dv_st.append(_dot_nt(kd[:, lanes], dstb))
                dkd.append(_dot(vb[:, lanes], dstb))
                da.append(jnp.broadcast_to(jnp.sum(dst * st, axis=0, keepdims=True), (SUB_CHUNK, HEAD_A)))
                dst_ref[h] = dst * a[0:1, lanes] + _dot_tn(dob[:, lanes], qg[:, lanes])
            dqg, dkd, da, dv_st = [jnp.concatenate(p, axis=1) for p in (dqg, dkd, da, dv_st)]
            d_gsum = qs * dqs - kk * dkk + dqg * qg_f - dkd * kd_f
            d_glast = jnp.sum(dkd * kd_f, axis=0, keepdims=True) + da * a
            dlf_ref[rows, :] = jnp.dot(tri_t, d_gsum, precision=lax.Precision.HIGHEST,
                                       preferred_element_type=F32) + d_glast
            dqs_ref[rows, :] = dqs + dqg * eg
            dkk_ref[rows, :] = dkk + dkd * ekd
            dp_ref[rows, 2 * A_WIDTH:3 * A_WIDTH] = (dv + dv_st).astype(BF16)
            return carry

        lax.fori_loop(0, n_sub, back, 0, unroll=2)
        sig, sq, qraw = blk["sig"], blk["sq"], blk["qraw"]
        d_f = dlf_ref[...] / blk["f"] - dkk_ref[...]
        dlb_ref[0:1, :] += jnp.sum(d_f * (1.0 - sig), axis=0, keepdims=True)
        dp_ref[:, 0:A_WIDTH] = (dqs_ref[...] * (sq * (1.0 + qraw * (1.0 - sq)))).astype(BF16)
        dp_ref[:, A_WIDTH:2 * A_WIDTH] = (d_f * (1.0 - lb) * sig * (1.0 - sig)).astype(BF16)

    def rev(s):
        return n_tb - 1 - s

    def col(k):
        return pl.BlockSpec((tb, A_WIDTH), lambda b, s, k=k: (b * n_tb + rev(s), k))

    acc8 = pl.BlockSpec((SUBLANES, A_WIDTH), lambda b, s: (0, 0))
    return pl.pallas_call(
        body, name=name, grid=(n_seq, n_tb),
        in_specs=[col(0), col(1), col(2), col(3), col(0), col(0),
                  pl.BlockSpec((1, n_sub, A_HEADS, HEAD_A, HEAD_A), lambda b, s: (b * n_tb + rev(s), 0, 0, 0, 0)),
                  pl.BlockSpec((3, A_WIDTH), lambda b, s: (0, 0)), pl.BlockSpec((1, A_WIDTH), lambda b, s: (0, 0))],
        out_specs=[pl.BlockSpec((tb, 4 * A_WIDTH), lambda b, s: (b * n_tb + rev(s), 0)), acc8, acc8],
        out_shape=[jax.ShapeDtypeStruct((t, EVEN_IN), BF16)] + [jax.ShapeDtypeStruct((SUBLANES, A_WIDTH), F32)] * 2,
        scratch_shapes=[pltpu.VMEM((A_HEADS, HEAD_A, HEAD_A), F32)] + [pltpu.VMEM((tb, A_WIDTH), F32)] * 10,
        compiler_params=_params(("arbitrary", "arbitrary")),
    )(proj, proj, proj, proj, dcat, pre, states, lb_table, a_norm)


GMLP_ROWS = 512


def _gmlp_chunk(ub, vb, ln_g, ln_b, ws, bias):
    u = [_gelu(a) for a in ub]
    v = [_gelu(a) for a in vb]
    mu = sum(jnp.sum(a, axis=-1, keepdims=True) for a in v) * (1.0 / B_WIDTH)
    cen = [a - mu for a in v]
    var = sum(jnp.sum(a * a, axis=-1, keepdims=True) for a in cen) * (1.0 / B_WIDTH)
    inv = lax.rsqrt(var + EPS)
    r = lax.broadcasted_iota(jnp.int32, (B_CHUNK, B_CHUNK), 0)
    c = lax.broadcasted_iota(jnp.int32, (B_CHUNK, B_CHUNK), 1)
    outs = []
    for g in range(B_GROUPS):
        vn = (cen[g] * inv * ln_g[g] + ln_b[g]).astype(BF16)
        wm = jnp.where(c <= r, ws[g], 0.0).astype(BF16)
        outs.append(u[g] * (_dot(wm, vn) + bias[g]))
    return outs


def _lane_groups(ref, rows=slice(None)):
    return [ref[rows, g * LANES:(g + 1) * LANES] for g in range(B_GROUPS)]


def _gmlp_fwd(proj, mixed, ln_g, ln_b, ws, bias_t, name):
    t = proj.shape[0]
    tm = GMLP_ROWS

    def body(u_ref, v_ref, lg_ref, lb_ref, ws_ref, bt_ref, _, o_ref):
        for ch in range(tm // B_CHUNK):
            rows = slice(ch * B_CHUNK, (ch + 1) * B_CHUNK)
            outs = _gmlp_chunk(_lane_groups(u_ref, rows), _lane_groups(v_ref, rows), _lane_groups(lg_ref),
                               _lane_groups(lb_ref), [ws_ref[g] for g in range(B_GROUPS)],
                               [bt_ref[:, g:g + 1] for g in range(B_GROUPS)])
            for g in range(B_GROUPS):
                o_ref[rows, g * LANES:(g + 1) * LANES] = outs[g].astype(BF16)

    vec = pl.BlockSpec((1, B_WIDTH), lambda i: (0, 0))
    return pl.pallas_call(
        body, name=name, grid=(t // tm,),
        in_specs=[pl.BlockSpec((tm, B_WIDTH), lambda i: (i, 4)), pl.BlockSpec((tm, B_WIDTH), lambda i: (i, 5)), vec, vec,
                  pl.BlockSpec((B_GROUPS, B_CHUNK, B_CHUNK), lambda i: (0, 0, 0)),
                  pl.BlockSpec((B_CHUNK, B_GROUPS), lambda i: (0, 0)), pl.BlockSpec(memory_space=pl.ANY)],
        out_specs=pl.BlockSpec((tm, B_WIDTH), lambda i: (i, 1)),
        out_shape=jax.ShapeDtypeStruct(mixed.shape, BF16),
        input_output_aliases={6: 0},
        compiler_params=_params(("parallel",)),
    )(proj, proj, ln_g, ln_b, ws, bias_t, mixed)


def _gmlp_bwd(proj, dcat, dproj, ln_g, ln_b, ws, bias_t, name):
    t = proj.shape[0]
    tm = GMLP_ROWS

    def body(u_ref, v_ref, do_ref, lg_ref, lb_ref, ws_ref, bt_ref, _, duv_ref, dlg_ref, dlb_ref, dws_ref, dbt_ref):
        @pl.when(pl.program_id(0) == 0)
        def _():
            dlg_ref[...] = jnp.zeros_like(dlg_ref)
            dlb_ref[...] = jnp.zeros_like(dlb_ref)
            dws_ref[...] = jnp.zeros_like(dws_ref)
            dbt_ref[...] = jnp.zeros_like(dbt_ref)

        for ch in range(tm // B_CHUNK):
            rows = slice(ch * B_CHUNK, (ch + 1) * B_CHUNK)
            _, vjp = jax.vjp(
                _gmlp_chunk, _lane_groups(u_ref, rows), _lane_groups(v_ref, rows), _lane_groups(lg_ref),
                _lane_groups(lb_ref), [ws_ref[g] for g in range(B_GROUPS)],
                [bt_ref[:, g:g + 1] for g in range(B_GROUPS)])
            du, dv, dlg, dlb, dw, dbt = vjp(_lane_groups(do_ref, rows))
            for g in range(B_GROUPS):
                lanes = slice(g * LANES, (g + 1) * LANES)
                duv_ref[rows, lanes] = du[g].astype(BF16)
                duv_ref[rows, B_WIDTH + g * LANES:B_WIDTH + (g + 1) * LANES] = dv[g].astype(BF16)
                dlg_ref[0:1, lanes] += dlg[g]
                dlb_ref[0:1, lanes] += dlb[g]
                dws_ref[g] += dw[g]
                dbt_ref[:, g:g + 1] += dbt[g]

    vec = pl.BlockSpec((1, B_WIDTH), lambda i: (0, 0))
    acc8 = pl.BlockSpec((SUBLANES, B_WIDTH), lambda i: (0, 0))
    ws_spec = pl.BlockSpec((B_GROUPS, B_CHUNK, B_CHUNK), lambda i: (0, 0, 0))
    bt_spec = pl.BlockSpec((B_CHUNK, B_GROUPS), lambda i: (0, 0))
    return pl.pallas_call(
        body, name=name, grid=(t // tm,),
        in_specs=[pl.BlockSpec((tm, B_WIDTH), lambda i: (i, 4)), pl.BlockSpec((tm, B_WIDTH), lambda i: (i, 5)),
                  pl.BlockSpec((tm, B_WIDTH), lambda i: (i, 1)), vec, vec, ws_spec, bt_spec,
                  pl.BlockSpec(memory_space=pl.ANY)],
        out_specs=[pl.BlockSpec((tm, 2 * B_WIDTH), lambda i: (i, 2)), acc8, acc8, ws_spec, bt_spec],
        out_shape=[jax.ShapeDtypeStruct(dproj.shape, BF16), jax.ShapeDtypeStruct((SUBLANES, B_WIDTH), F32),
                   jax.ShapeDtypeStruct((SUBLANES, B_WIDTH), F32),
                   jax.ShapeDtypeStruct((B_GROUPS, B_CHUNK, B_CHUNK), F32),
                   jax.ShapeDtypeStruct((B_CHUNK, B_GROUPS), F32)],
        input_output_aliases={7: 0},
        compiler_params=_params(("arbitrary",)),
    )(proj, proj, dcat, ln_g, ln_b, ws, bias_t, dproj)


QK_SCALE = 1.0 / math.sqrt(C_HEAD_DIM)
ATTN_UNROLL = 8
LANE_GROUPS = D_MODEL // LANES
Q_BLOCKS = SEQ // C_BLOCK


def _attn_window(i, d):
    sub_blocks = Q_BLOCKS // d
    q0 = pl.multiple_of(i * C_BLOCK, C_BLOCK)
    k0 = pl.multiple_of(jnp.maximum(i - 1, 0) * C_BLOCK, C_BLOCK)
    key = k0 + lax.broadcasted_iota(jnp.int32, (C_BLOCK, 2 * C_BLOCK), 1)
    dist = (q0 + lax.broadcasted_iota(jnp.int32, (C_BLOCK, 2 * C_BLOCK), 0)) - key
    own_subsequence = (key >= q0) | (i % sub_blocks > 0)
    return pl.ds(q0, C_BLOCK), pl.ds(k0, 2 * C_BLOCK), (dist >= 0) & (dist <= C_BLOCK) & own_subsequence


def _head_masks():
    lane = lax.broadcasted_iota(jnp.int32, (C_BLOCK, LANES), 1)
    return [lane < C_HEAD_DIM, lane >= C_HEAD_DIM]


def _flat_spec(col_of):
    return pl.BlockSpec((1, SEQ, LANES), lambda b, g: (b, 0, col_of(g)))


def _put_heads(tile, g, col0, col1):
    lane = lax.broadcasted_iota(jnp.int32, tile.shape, 1)
    return jnp.where(lane == 2 * g, col0, jnp.where(lane == 2 * g + 1, col1, tile))


def _get_head(tile, h):
    lane = lax.broadcasted_iota(jnp.int32, tile.shape, 1)
    return jnp.sum(jnp.where(lane == h, tile, 0.0), axis=1, keepdims=True)


PER_HEAD_SPEC = pl.BlockSpec((1, SEQ, LANES), lambda b, g: (b, 0, 0))


def _attn_branch_fwd(qkv, name):
    n_seq, d, l, _ = qkv.shape
    flat = qkv.reshape(n_seq, SEQ, ODD_IN)

    def body(q_ref, k_ref, v_ref, o_ref, m_ref, l_ref):
        heads = _head_masks()
        g = pl.program_id(1)

        @pl.when(g == 0)
        def _():
            m_ref[...] = jnp.zeros_like(m_ref)
            l_ref[...] = jnp.zeros_like(l_ref)

        def block(i, carry):
            rows, keys, mask = _attn_window(i, d)
            q, k, v = q_ref[0, rows, :], k_ref[0, keys, :], v_ref[0, keys, :]
            res = []
            for hm in heads:
                s = jnp.where(mask, _dot_nt(jnp.where(hm, q, 0), k), NEG)
                m = jnp.max(s, axis=-1, keepdims=True)
                p = jnp.exp(s - m)
                res.append((_dot(p.astype(BF16), v), m, jnp.sum(p, axis=-1, keepdims=True)))
            o_ref[0, rows, :] = jnp.where(heads[0], res[0][0], res[1][0])
            m_ref[0, rows, :] = _put_heads(m_ref[0, rows, :], g, res[0][1], res[1][1])
            l_ref[0, rows, :] = _put_heads(l_ref[0, rows, :], g, res[0][2], res[1][2])
            return carry

        lax.fori_loop(0, Q_BLOCKS, block, 0, unroll=ATTN_UNROLL)

    o, m, l_sum = pl.pallas_call(
        body, name=name, grid=(n_seq, LANE_GROUPS),
        in_specs=[_flat_spec(lambda g: g), _flat_spec(lambda g: LANE_GROUPS + g),
                  _flat_spec(lambda g: 2 * LANE_GROUPS + g)],
        out_specs=[_flat_spec(lambda g: g), PER_HEAD_SPEC, PER_HEAD_SPEC],
        out_shape=[jax.ShapeDtypeStruct((n_seq, SEQ, D_MODEL), F32)] + [jax.ShapeDtypeStruct((n_seq, SEQ, LANES), F32)] * 2,
        compiler_params=_params(("parallel", "arbitrary")),
    )(flat, flat, flat)
    return [o.reshape(n_seq, d, l, D_MODEL), m.reshape(n_seq, d, l, LANES), l_sum.reshape(n_seq, d, l, LANES)]


def _attn_merge(branches, name):
    n_seq = branches[0][0].shape[0]
    t = n_seq * SEQ
    tm = MERGE_TILE

    def body(*refs):
        ins = refs[:9]
        o_ref, ob_ref, lse_ref = refs[9:12]
        nat = refs[12:]
        for b, d in enumerate(C_DILATIONS[1:]):
            for k in range(3):
                _load_dilated(ins[3 + 3 * b + k], d, nat[3 * b + k])
        ms = [ins[1][0, 0], nat[1][0], nat[4][0]]
        ls = [ins[2][0, 0], nat[2][0], nat[5][0]]
        m_all = jnp.maximum(jnp.maximum(ms[0], ms[1]), ms[2])
        ws = [jnp.exp(ms[b] - m_all) for b in range(3)]
        lane = lax.broadcasted_iota(jnp.int32, m_all.shape, 1)
        total = jnp.where(lane < C_HEADS, ws[0] * ls[0] + ws[1] * ls[1] + ws[2] * ls[2], 1.0)
        lse_ref[...] = m_all + jnp.log(total)
        first_head = lane < C_HEAD_DIM
        for p in range(LANE_GROUPS):
            lanes = slice(p * LANES, (p + 1) * LANES)
            spread = lambda c: jnp.where(first_head, c[:, 2 * p:2 * p + 1], c[:, 2 * p + 1:2 * p + 2])
            os_ = [ins[0][0, 0, :, lanes], nat[0][p], nat[3][p]]
            o = (spread(ws[0]) * os_[0] + spread(ws[1]) * os_[1] + spread(ws[2]) * os_[2]) / spread(total)
            o_ref[:, lanes] = o
            ob_ref[:, lanes] = o.astype(BF16)

    row = pl.BlockSpec((tm, D_MODEL), lambda i: (i, 0))
    flat = [a for br in branches for a in br]
    in_specs = []
    for wide, narrow in zip(_dilated_specs(tm, D_MODEL, lambda: 0), _dilated_specs(tm, LANES, lambda: 0)):
        in_specs += [wide, narrow, narrow]
    per_head = pltpu.VMEM((1, tm, LANES), F32)
    return pl.pallas_call(
        body, name=name, grid=(t // tm,), in_specs=in_specs,
        out_specs=[row, row, pl.BlockSpec((tm, LANES), lambda i: (i, 0))],
        out_shape=[jax.ShapeDtypeStruct((t, D_MODEL), F32), jax.ShapeDtypeStruct((t, D_MODEL), BF16),
                   jax.ShapeDtypeStruct((t, LANES), F32)],
        scratch_shapes=[pltpu.VMEM((LANE_GROUPS, tm, LANES), F32), per_head, per_head] * 2,
        compiler_params=_params(("parallel",)),
    )(*flat)


def _attn_branch_bwd(qkv, dout, lse, delta, name):
    n_seq, d, l, _ = qkv.shape
    flat = lambda a: a.reshape(n_seq, SEQ, a.shape[-1])

    def body(q_ref, k_ref, v_ref, do_ref, lse_nat_ref, dl_nat_ref, dq_ref, dk_ref, dv_ref, lse_ref, dl_ref,
             dkt_ref, dvt_ref):
        heads = _head_masks()
        g = pl.program_id(1)
        dkt_ref[...] = jnp.zeros_like(dkt_ref)
        dvt_ref[...] = jnp.zeros_like(dvt_ref)
        for nat_ref, dst_ref in ((lse_nat_ref, lse_ref), (dl_nat_ref, dl_ref)):
            for r in range(d):
                rows = pl.ds(r, l, stride=d) if d > 1 else slice(None)
                dst_ref[r * l:(r + 1) * l, :] = nat_ref.at[0][rows, :]

        def block(i, carry):
            rows, keys, mask = _attn_window(i, d)
            q, do = q_ref[0, rows, :], do_ref[0, rows, :]
            k, v = k_ref[0, keys, :], v_ref[0, keys, :]
            lse_b, dl_b = lse_ref[rows, :], dl_ref[rows, :]
            dq, dk, dv = [], None, None
            for hh, hm in enumerate(heads):
                qh, doh = jnp.where(hm, q, 0), jnp.where(hm, do, 0)
                s = jnp.where(mask, _dot_nt(qh, k), NEG)
                p = jnp.exp(s - _get_head(lse_b, 2 * g + hh))
                ds = (p * (_dot_nt(doh, v) - _get_head(dl_b, 2 * g + hh))).astype(BF16)
                dq.append(_dot(ds, k) * QK_SCALE)
                dk_h, dv_h = _dot_tn(qh, ds), _dot_tn(doh, p.astype(BF16))
                dk = dk_h if dk is None else dk + dk_h
                dv = dv_h if dv is None else dv + dv_h
            dq_ref[0, rows, :] = jnp.where(heads[0], dq[0], dq[1]).astype(BF16)
            dkt_ref[:, keys] += dk
            dvt_ref[:, keys] += dv
            return carry

        lax.fori_loop(0, Q_BLOCKS, block, 0, unroll=ATTN_UNROLL)
        for c in range(SEQ // ROW_TILE):
            rows = slice(c * ROW_TILE, (c + 1) * ROW_TILE)
            dk_ref[0, rows, :] = dkt_ref[:, rows].T.astype(BF16)
            dv_ref[0, rows, :] = dvt_ref[:, rows].T.astype(BF16)

    act = _flat_spec(lambda g: g)
    outs = pl.pallas_call(
        body, name=name, grid=(n_seq, LANE_GROUPS),
        in_specs=[_flat_spec(lambda g: g), _flat_spec(lambda g: LANE_GROUPS + g),
                  _flat_spec(lambda g: 2 * LANE_GROUPS + g), act, PER_HEAD_SPEC, PER_HEAD_SPEC],
        out_specs=[act] * 3,
        out_shape=[jax.ShapeDtypeStruct((n_seq, SEQ, D_MODEL), BF16)] * 3,
        scratch_shapes=[pltpu.VMEM((SEQ, LANES), F32)] * 2 + [pltpu.VMEM((LANES, SEQ), F32)] * 2,
        compiler_params=_params(("parallel", "parallel")),
    )(flat(qkv), flat(qkv), flat(qkv), flat(dout), lse, delta)
    return [o.reshape(n_seq, d, l, D_MODEL) for o in outs]


def _attn_combine_bwd(grads, rope, name):
    n_seq = grads[0][0].shape[0]
    t = n_seq * SEQ
    tm = MERGE_TILE

    def body(*refs):
        c_ref, s_ref, o_ref, nat4_ref, nat16_ref = refs[9:]
        for sec in range(3):
            _load_dilated(refs[3 + sec], 4, nat4_ref)
            _load_dilated(refs[6 + sec], 16, nat16_ref)
            for p in range(LANE_GROUPS):
                blk = refs[sec][0, 0, :, p * LANES:(p + 1) * LANES] + nat4_ref[p] + nat16_ref[p]
                if sec < 2:
                    blk = blk * c_ref[...] - _swap_halves(blk) * s_ref[...]
                o_ref[:, sec * D_MODEL + p * LANES:sec * D_MODEL + (p + 1) * LANES] = blk.astype(BF16)

    tab = pl.BlockSpec((tm, LANES), lambda i: (i, 0))
    flat = [a for br in grads for a in br]
    in_specs = []
    for spec in _dilated_specs(tm, D_MODEL, lambda: 0):
        in_specs += [spec] * 3
    return pl.pallas_call(
        body, name=name, grid=(t // tm,), in_specs=in_specs + [tab, tab],
        out_specs=pl.BlockSpec((tm, ODD_IN), lambda i: (i, 0)),
        out_shape=jax.ShapeDtypeStruct((t, ODD_IN), BF16),
        scratch_shapes=[pltpu.VMEM((LANE_GROUPS, tm, LANES), F32)] * 2,
        compiler_params=_params(("parallel",)),
    )(*flat, *rope)


def _adamw(w, g, m, v):
    m = ADAM_B1 * m + (1.0 - ADAM_B1) * g
    v = ADAM_B2 * v + (1.0 - ADAM_B2) * jnp.square(g)
    m_hat = m / (1.0 - ADAM_B1 ** ADAM_STEP)
    v_hat = v / (1.0 - ADAM_B2 ** ADAM_STEP)
    delta = -ADAM_LR * (m_hat / (jnp.sqrt(v_hat) + ADAM_EPS) + ADAM_WD * w)
    return delta, m, v


def _adamw_sharded(parts, w, m, v, after, name):
    n_layers, rows, cols = w.shape
    tr = min(rows, 256)

    def body(*refs):
        p_refs = refs[:n_layers]
        w_ref, m_ref, v_ref, _, g_ref, d_ref, mo_ref, vo_ref = refs[n_layers:]
        layer = pl.program_id(0)
        g = None
        for l, p_ref in enumerate(p_refs):
            g_l = p_ref[0].astype(F32)
            for s in range(1, N_DEV):
                g_l = g_l + p_ref[s].astype(F32)
            g = g_l if g is None else jnp.where(layer == l, g_l, g)
        delta, mn, vn = _adamw(w_ref[0], g, m_ref[0], v_ref[0])
        g_ref[0] = g
        d_ref[0] = delta
        mo_ref[0] = mn
        vo_ref[0] = vn

    def part_spec(l):
        return pl.BlockSpec((N_DEV, tr, cols), lambda a, i: (0, jnp.where(a == l, i, 0), 0))

    row = pl.BlockSpec((1, tr, cols), lambda a, i: (a, i, 0))
    return pl.pallas_call(
        body, name=name, grid=(n_layers, rows // tr),
        in_specs=[part_spec(l) for l in range(n_layers)] + [row, row, row, pl.BlockSpec(memory_space=pl.ANY)],
        out_specs=[row] * 4, out_shape=[jax.ShapeDtypeStruct(w.shape, F32)] * 4,
        compiler_params=_params(("arbitrary", "arbitrary")),
    )(*parts, w, m, v, after)


def _small_update(gathered, where, weights, moments_m, moments_v, lb_index, name):
    n = len(weights)
    n_g = len(gathered)

    def body(*refs):
        g_refs = refs[:n_g]
        w_refs, m_refs, v_refs = refs[n_g:n_g + n], refs[n_g + n:n_g + 2 * n], refs[n_g + 2 * n:n_g + 3 * n]
        outs = refs[n_g + 3 * n:]

        def total(k):
            array, rows, lanes = where[k]
            ref = g_refs[array]
            index = (slice(None),) * (len(ref.shape) - 1) if rows is None else (rows, lanes)
            acc = ref[(0,) + index]
            for s in range(1, N_DEV):
                acc = acc + ref[(s,) + index]
            return acc

        loss_rows = total(n)
        outs[0][...] = jnp.sum(jnp.sum(loss_rows, axis=1, keepdims=True), axis=0, keepdims=True)
        for k in range(n):
            part = total(k)
            if k == lb_index:
                dlb = jnp.sum(part, axis=0, keepdims=True)
                tab = w_refs[k][...]
                e = jnp.exp(tab - jnp.max(tab, axis=0, keepdims=True))
                p = e / jnp.sum(e, axis=0, keepdims=True)
                first = lax.broadcasted_iota(jnp.int32, p.shape, 0) == 0
                grads = [(slice(None), p * (jnp.where(first, dlb, 0.0) - p[0:1, :] * dlb))]
            elif part.shape == w_refs[k].shape:
                grads = [(slice(None), part)]
            else:
                grads = [(slice(l, l + 1), jnp.sum(part[l * SUBLANES:(l + 1) * SUBLANES], axis=0, keepdims=True))
                         for l in range(w_refs[k].shape[0])]
            for rows, g in grads:
                delta, mn, vn = _adamw(w_refs[k][rows], g, m_refs[k][rows], v_refs[k][rows])
                outs[1 + 4 * k][rows] = g
                outs[2 + 4 * k][rows] = delta
                outs[3 + 4 * k][rows] = mn
                outs[4 + 4 * k][rows] = vn

    vmem = pl.BlockSpec(memory_space=pltpu.VMEM)
    out_shape = [jax.ShapeDtypeStruct((1, 1), F32)]
    for w in weights:
        out_shape += [jax.ShapeDtypeStruct(w.shape, F32)] * 4
    args = list(gathered) + list(weights) + list(moments_m) + list(moments_v)
    return pl.pallas_call(
        body, name=name, in_specs=[vmem] * len(args), out_specs=[vmem] * len(out_shape), out_shape=out_shape,
        compiler_params=pltpu.CompilerParams(vmem_limit_bytes=VMEM_LIMIT),
    )(*args)


def kernel(x, positions, norm_mix_pre, norm_mix_post, norm_ffn_pre, norm_ffn_post, w_in_even, lb_table, a_norm, b_ln_g, b_ln_b, b_ws, b_bias, w_out_even, w_in_odd, w_out_odd, w_ff1, w_ff2, loss_target, m_norm_mix_pre, m_norm_mix_post, m_norm_ffn_pre, m_norm_ffn_post, m_w_in_even, m_lb_table, m_a_norm, m_b_ln_g, m_b_ln_b, m_b_ws, m_b_bias, m_w_out_even, m_w_in_odd, m_w_out_odd, m_w_ff1, m_w_ff2, v_norm_mix_pre, v_norm_mix_post, v_norm_ffn_pre, v_norm_ffn_post, v_w_in_even, v_lb_table, v_a_norm, v_b_ln_g, v_b_ln_b, v_b_ws, v_b_bias, v_w_out_even, v_w_in_odd, v_w_out_odd, v_w_ff1, v_w_ff2):
    n_seq = x.shape[0]
    t = n_seq * SEQ
    x0 = x.reshape(t, D_MODEL)
    target = loss_target.reshape(t, D_MODEL)

    me = _my_slot().astype(jnp.int32).reshape(1)

    order = ["in_e", "out_e", "ff1_0", "ff2_0", "in_o", "out_o", "ff1_1", "ff2_1"]
    shards = dict(in_e=w_in_even[0], out_e=w_out_even[0], in_o=w_in_odd[0], out_o=w_out_odd[0],
                  ff1_0=w_ff1[0], ff1_1=w_ff1[1], ff2_0=w_ff2[0], ff2_1=w_ff2[1])
    by_columns = ("in_e", "in_o", "ff1_0", "ff1_1")

    def place(k, after):
        if k in by_columns:
            return _place_own_columns(shards[k], me, "place_" + k, after)
        return _place_own(shards[k], me, "place_" + k, False, after=after)

    gathers = {}
    send0, recv0, land0, _, token0 = _exchange_start([place(order[0], None)], [None], "gather_start_first")
    gathers[order[0]] = (land0[0], send0[0], recv0[0])
    sends, recvs, lands, _, g_token = _exchange_start([place(k, token0) for k in order[1:]],
                                                      [None] * (len(order) - 1), "gather_start")
    for k, land, send, recv in zip(order[1:], lands, sends, recvs):
        gathers[k] = (land, send, recv)

    def get_w(keys, after):
        lands_k, sends_k, recvs_k = zip(*[gathers[k] for k in keys])
        return _exchange_wait(list(lands_k), [None] * len(keys), list(sends_k), list(recvs_k), after,
                              "gather_wait_" + keys[0])

    sent = {}

    def put_g(group, blocks):
        keys = list(blocks)
        own = [_place_own(blocks[k], me, "own_" + k, True) for k in keys]
        send_sems, recv_sems, own, srcs, token = _exchange_start(own, [blocks[k] for k in keys], "scatter_start_" + group)
        sent[group] = (keys, own, srcs, send_sems, recv_sems)
        return token

    rope = _rope_tables(positions)
    bias_t = b_bias[0].T
    grads = _local_step(x0, target, rope, norm_mix_pre, norm_mix_post, norm_ffn_pre, norm_ffn_post, lb_table,
                        a_norm, b_ln_g, b_ln_b, b_ws[0], bias_t, get_w, put_g, g_token)
    (dx0, loss_part, dg_mix_pre, dg_mix_post, dg_ffn_pre, dg_ffn_post, d_lb, d_a_norm, d_ln_g, d_ln_b, d_ws,
     d_bias_t) = grads

    packed = jnp.concatenate([dg_mix_pre, dg_mix_post, dg_ffn_pre, dg_ffn_post,
                              jnp.concatenate([d_lb, d_a_norm], axis=1), jnp.concatenate([d_ln_g, d_ln_b], axis=1),
                              loss_part], axis=0)
    small_lands = [_place_own(a, me, "own_small%d" % k, False, F32) for k, a in enumerate((packed, d_ws, d_bias_t))]
    s_send, s_recv, small_lands, _, after = _exchange_start(small_lands, [None] * 3, "gather_small_start")

    big = dict(w_in_even=(["in_e"], w_in_even, m_w_in_even, v_w_in_even),
               w_out_even=(["out_e"], w_out_even, m_w_out_even, v_w_out_even),
               w_in_odd=(["in_o"], w_in_odd, m_w_in_odd, v_w_in_odd),
               w_out_odd=(["out_o"], w_out_odd, m_w_out_odd, v_w_out_odd),
               w_ff1=(["ff1_0", "ff1_1"], w_ff1, m_w_ff1, v_w_ff1), w_ff2=(["ff2_0", "ff2_1"], w_ff2, m_w_ff2, v_w_ff2))
    recv, big_out = {}, {}
    for groups, names in ((("ffn1", "ffn0"), ("w_ff1", "w_ff2")), (("mix1",), ("w_in_odd", "w_out_odd")),
                          (("mix0",), ("w_in_even", "w_out_even"))):
        for group in groups:
            keys, own, srcs, send_sems, recv_sems = sent[group]
            recv.update(zip(keys, _exchange_wait(own, srcs, send_sems, recv_sems, after, "scatter_wait_" + group)))
        for nm in names:
            keys, w, m, v = big[nm]
            big_out[nm] = _adamw_sharded([recv[k] for k in keys], w, m, v, after, "adamw_" + nm)
            after = big_out[nm][0]
    big_out = [big_out[nm] for nm in ("w_in_even", "w_out_even", "w_in_odd", "w_out_odd", "w_ff1", "w_ff2")]
    gathered = _exchange_wait(small_lands, [None] * 3, s_send, s_recv, after, "gather_small_wait")
    rows8 = lambda k: slice(SUBLANES * k, SUBLANES * (k + 1))
    left, right, every = slice(0, A_WIDTH), slice(A_WIDTH, 2 * A_WIDTH), slice(None)
    where = [(0, slice(0, 16), every), (0, slice(16, 32), every), (0, slice(32, 48), every), (0, slice(48, 64), every),
             (0, rows8(8), left), (0, rows8(8), right), (0, rows8(9), left), (0, rows8(9), right),
             (1, None, None), (2, None, None), (0, rows8(10), every)]
    small_w = [norm_mix_pre, norm_mix_post, norm_ffn_pre, norm_ffn_post, lb_table, a_norm, b_ln_g, b_ln_b,
               b_ws[0], bias_t]
    small_m = [m_norm_mix_pre, m_norm_mix_post, m_norm_ffn_pre, m_norm_ffn_post, m_lb_table, m_a_norm, m_b_ln_g,
               m_b_ln_b, m_b_ws[0], m_b_bias[0].T]
    small_v = [v_norm_mix_pre, v_norm_mix_post, v_norm_ffn_pre, v_norm_ffn_post, v_lb_table, v_a_norm, v_b_ln_g,
               v_b_ln_b, v_b_ws[0], v_b_bias[0].T]
    small_out = _small_update(gathered, where, small_w, small_m, small_v, 4, "small_update")
    loss = small_out[0].reshape(())
    small = [small_out[1 + 4 * k:5 + 4 * k] for k in range(len(small_w))]
    small[8] = [a[None] for a in small[8]]
    small[9] = [a.T[None] for a in small[9]]

    per_weight = small[0:4] + [big_out[0]] + small[4:10] + big_out[1:6]
    grad_x = dx0.reshape(x.shape)
    out = [loss, grad_x]
    for kind in range(4):
        out += [p[kind] for p in per_weight]
    return tuple(out)


def _local_step(x0, target, rope, norm_mix_pre, norm_mix_post, norm_ffn_pre, norm_ffn_post, lb_table, a_norm,
                b_ln_g, b_ln_b, ws, bias_t, get_w, put_g, token):
    def gain(a, l, tok):
        return a[l:l + 1] if tok is None else a[l:l + 1] + tok[0:1, 0:1]

    full = lambda a: a.reshape(-1, D_MODEL)
    owners = lambda a: a.reshape((N_DEV, -1) + a.shape[1:])

    (g_in_e,) = get_w(["in_e"], token)
    proj, h_mix0 = _norm_inproj(x0, gain(norm_mix_pre, 0, token), g_in_e, "inproj_even")
    mixed, pre_a, states = _hgrn2_fwd(proj, lb_table, a_norm, "hgrn2_fwd")
    mixed = _gmlp_fwd(proj, mixed, b_ln_g, b_ln_b, ws, bias_t, "gmlp_fwd")
    w_out_e = full(get_w(["out_e"], mixed)[0])
    x1, mix0 = _outproj([mixed], w_out_e, x0, gain(norm_mix_post, 0, None), "outproj_even")
    w1_0, w2_0 = get_w(["ff1_0", "ff2_0"], x1)
    w2_0 = full(w2_0)
    x2, y0, h_ffn0, r0 = _ffn_fwd(x1, gain(norm_ffn_pre, 0, None), w1_0, w2_0, gain(norm_ffn_post, 0, None), "ffn_fwd_0")
    (g_in_o,) = get_w(["in_o"], x2)
    *qkv, h_mix1 = _norm_inproj_rope(x2, gain(norm_mix_pre, 1, None), g_in_o, rope, "inproj_odd")
    branches = [_attn_branch_fwd(a, "attn_fwd_d%d" % d) for a, d in zip(qkv, C_DILATIONS)]
    attn, attn_b, lse = _attn_merge(branches, "attn_merge")
    w_out_o = full(get_w(["out_o"], attn_b)[0])
    x3, mix1 = _outproj([attn_b], w_out_o, x2, gain(norm_mix_post, 1, None), "outproj_odd")
    w1_1, w2_1 = get_w(["ff1_1", "ff2_1"], x3)
    w2_1 = full(w2_1)
    dx4, y1, h_ffn1, r1, loss_part = _ffn_fwd(x3, gain(norm_ffn_pre, 1, None), w1_1, w2_1, gain(norm_ffn_post, 1, None),
                                              "ffn_fwd_1", target)

    dx3, dy1, da1, dg_ffn_pre1, dg_ffn_post1 = _ffn_bwd(
        dx4, x3, y1, r1, gain(norm_ffn_pre, 1, None), w1_1, w2_1, gain(norm_ffn_post, 1, None), "ffn_bwd_1")
    gw_ff1_1 = _grad_w(h_ffn1, da1, True, "grad_w_ff1_1")
    gw_ff2_1 = _grad_w(r1, dy1, False, "grad_w_ff2_1")
    tok = put_g("ffn1", dict(ff1_1=gw_ff1_1, ff2_1=owners(gw_ff2_1)))
    *dattn, delta, dz1, dg_mix_post1 = _outproj_bwd_attn(dx3, mix1, gain(norm_mix_post, 1, tok), w_out_o, attn,
                                                  "outproj_bwd_odd")
    gw_out_o = _grad_w(attn_b, dz1, False, "grad_w_out_odd")
    per_seq = lambda a: a.reshape(-1, SEQ, LANES)
    grads_c = [_attn_branch_bwd(qkv[b], dattn[b], per_seq(lse), per_seq(delta), "attn_bwd_d%d" % d)
               for b, d in enumerate(C_DILATIONS)]
    dqkv = _attn_combine_bwd(grads_c, rope, "attn_combine_bwd")
    gw_in_o = _grad_w(h_mix1, dqkv, True, "grad_w_in_odd")
    tok = put_g("mix1", dict(out_o=owners(gw_out_o), in_o=gw_in_o))
    dx2, dg_mix_pre1 = _inproj_bwd(dqkv, g_in_o, dx3, x2, gain(norm_mix_pre, 1, tok), "inproj_bwd_odd")

    dx1, dy0, da0, dg_ffn_pre0, dg_ffn_post0 = _ffn_bwd(
        dx2, x1, y0, r0, gain(norm_ffn_pre, 0, None), w1_0, w2_0, gain(norm_ffn_post, 0, None), "ffn_bwd_0")
    gw_ff1_0 = _grad_w(h_ffn0, da0, True, "grad_w_ff1_0")
    gw_ff2_0 = _grad_w(r0, dy0, False, "grad_w_ff2_0")
    tok = put_g("ffn0", dict(ff1_0=gw_ff1_0, ff2_0=owners(gw_ff2_0)))
    dcat, dz0, dg_mix_post0 = _outproj_bwd(dx1, mix0, gain(norm_mix_post, 0, tok), w_out_e, "outproj_bwd_even")
    gw_out_e = _grad_w(mixed, dz0, False, "grad_w_out_even")
    dproj, d_lb, d_a_norm = _hgrn2_bwd(proj, dcat, pre_a, states, lb_table, a_norm, "hgrn2_bwd")
    dproj, d_ln_g, d_ln_b, d_ws, d_bias_t = _gmlp_bwd(proj, dcat, dproj, b_ln_g, b_ln_b, ws, bias_t, "gmlp_bwd")
    gw_in_e = _grad_w(h_mix0, dproj, True, "grad_w_in_even")
    tok = put_g("mix0", dict(out_e=owners(gw_out_e), in_e=gw_in_e))
    dx0, dg_mix_pre0 = _inproj_bwd(dproj, g_in_e, dx1, x0, gain(norm_mix_pre, 0, tok), "inproj_bwd_even")

    layers = lambda a, b: jnp.concatenate([a, b], axis=0)
    return (dx0, loss_part, layers(dg_mix_pre0, dg_mix_pre1), layers(dg_mix_post0, dg_mix_post1),
            layers(dg_ffn_pre0, dg_ffn_pre1), layers(dg_ffn_post0, dg_ffn_post1),
            d_lb, d_a_norm, d_ln_g, d_ln_b, d_ws, d_bias_t)
```

```python
import math

import jax
import jax.numpy as jnp
from jax import lax
from jax.experimental import pallas as pl
from jax.experimental.pallas import tpu as pltpu

F32 = jnp.float32
BF16 = jnp.bfloat16
MESH = pl.DeviceIdType.MESH

N_DEV = 8
D_MODEL = 1024
SEQ = 2048
EPS = 1e-6
A_WIDTH = 512
A_HEADS = 4
HEAD_A = 128
B_WIDTH = 512
B_GROUPS = 4
B_CHUNK = 128
C_HEADS = 16
C_HEAD_DIM = 64
C_ROT_HALF = 8
ROPE_THETA = 500000.0
C_DILATIONS = (1, 4, 16)
C_BLOCK = 128
D_FF = 4096
EVEN_IN = 3072
ODD_IN = 3072

ADAM_LR = 0.001
ADAM_B1 = 0.9
ADAM_B2 = 0.999
ADAM_EPS = 1e-08
ADAM_WD = 0.01
ADAM_STEP = 10

LANES = 128
SUBLANES = 8
ROW_TILE = 512
PROJ_TILE = 1024
PROJ_COLS = 768
MERGE_TILE = 256
SUB_CHUNK = 16
HGRN_BLOCK = 256
NEG = -1e30
VMEM_LIMIT = 56 * 1024 * 1024


def _params(sem):
    return pltpu.CompilerParams(dimension_semantics=sem, vmem_limit_bytes=VMEM_LIMIT)


def _dot(a, b):
    return jnp.dot(a, b, preferred_element_type=F32)


def _dot_nt(a, b):
    return lax.dot_general(a, b, (((1,), (1,)), ((), ())), preferred_element_type=F32)


def _dot_tn(a, b):
    return lax.dot_general(a, b, (((0,), (0,)), ((), ())), preferred_element_type=F32)


def _rms(x, g):
    r = lax.rsqrt(jnp.mean(x * x, axis=-1, keepdims=True) + EPS)
    return x * r * g


def _rms_bwd(x, g, dy):
    r = lax.rsqrt(jnp.mean(x * x, axis=-1, keepdims=True) + EPS)
    dyg = dy * g
    dx = r * dyg - x * (r * r * r) * jnp.mean(x * dyg, axis=-1, keepdims=True)
    return dx, dy * x * r


def _split3(x):
    hi = x.astype(BF16)
    rest = x - hi.astype(F32)
    mid = rest.astype(BF16)
    return hi, mid, (rest - mid.astype(F32)).astype(BF16)


def _mask_dot(mask, x):
    m = mask.astype(BF16)
    hi, mid, lo = _split3(x)
    return _dot(m, hi) + (_dot(m, mid) + _dot(m, lo))


def _dot_mask(x, mask):
    m = mask.astype(BF16)
    hi, mid, lo = _split3(x)
    return _dot(hi, m) + (_dot(mid, m) + _dot(lo, m))


def _rows8(v):
    return v.reshape(v.shape[0] // SUBLANES, SUBLANES, v.shape[1]).sum(axis=0)


def _sigmoid(x):
    return 1.0 / (1.0 + jnp.exp(-x))


def _gelu(x):
    return 0.5 * x * (1.0 + jnp.tanh(math.sqrt(2.0 / math.pi) * (x + 0.044715 * (x * x * x))))


def _acc_rows8(ref, val, first):
    @pl.when(first)
    def _():
        ref[...] = val

    @pl.when(jnp.logical_not(first))
    def _():
        ref[...] += val


def _my_slot():
    return 4 * lax.axis_index("x") + 2 * lax.axis_index("y") + lax.axis_index("c")


def _peer(r):
    x, y, c = lax.axis_index("x"), lax.axis_index("y"), lax.axis_index("c")
    px = 1 - x if (r >> 2) & 1 else x
    py = 1 - y if (r >> 1) & 1 else y
    pc = 1 - c if r & 1 else c
    return (px, py, pc), 4 * px + 2 * py + pc


HBM_SPEC = pl.BlockSpec(memory_space=pltpu.HBM)
SEM_SPEC = pl.BlockSpec(memory_space=pltpu.SEMAPHORE)
SPLIT_EFFECT = pltpu.SideEffectType.DATAFLOW_SIDE_EFFECTING


def _split_copies(land_ref, src_ref, send_sem, recv_sem):
    me = _my_slot()
    copies = []
    for r in range(1, N_DEV):
        peer, slot = _peer(r)
        src = _slot(land_ref, me) if src_ref is None else _slot(src_ref, slot)
        copies.append(pltpu.make_async_remote_copy(
            src_ref=src, dst_ref=_slot(land_ref, me), send_sem=send_sem, recv_sem=recv_sem,
            device_id=peer, device_id_type=MESH))
    return copies


def _slot(ref, s):
    if len(ref.shape) == 2:
        c = ref.shape[1] // N_DEV
        return ref.at[:, pl.ds(pl.multiple_of(s * c, LANES), c)]
    return ref.at[s]


def _exchange_start(lands, sources, name):
    n = len(lands)
    given = [s for s in sources if s is not None]
    arrays = list(lands) + given

    def body(*refs):
        land_refs, src_refs = refs[:n], list(refs[n:n + len(given)])
        sems = refs[len(arrays):len(arrays) + 2 * n]
        token = refs[-1]
        for k in range(n):
            src_ref = None if sources[k] is None else src_refs.pop(0)
            for copy in _split_copies(land_refs[k], src_ref, sems[k], sems[n + k]):
                copy.start()
        token[...] = jnp.zeros_like(token)

    outs = pl.pallas_call(
        body, name=name,
        out_shape=(pltpu.SemaphoreType.DMA(()),) * (2 * n) + tuple(pltpu.HBM(a.shape, a.dtype) for a in arrays)
        + (jax.ShapeDtypeStruct((SUBLANES, LANES), F32),),
        in_specs=[HBM_SPEC] * len(arrays),
        out_specs=(SEM_SPEC,) * (2 * n) + (HBM_SPEC,) * len(arrays) + (pl.BlockSpec(memory_space=pltpu.VMEM),),
        input_output_aliases={i: 2 * n + i for i in range(len(arrays))},
        compiler_params=pltpu.CompilerParams(has_side_effects=SPLIT_EFFECT),
    )(*[pltpu.with_memory_space_constraint(a, pltpu.HBM) for a in arrays])
    return list(outs[:n]), list(outs[n:2 * n]), list(outs[2 * n:3 * n]), list(outs[3 * n:-1]), outs[-1]


def _exchange_wait(lands, sources, send_sems, recv_sems, after, name):
    n = len(lands)
    given = [s for s in sources if s is not None]
    arrays = list(lands) + given

    def body(*refs):
        land_refs, src_refs = refs[:n], list(refs[n:n + len(given)])
        sems = refs[len(arrays):len(arrays) + 2 * n]
        for i in range(n):
            src_ref = None if sources[i] is None else src_refs.pop(0)
            copies = _split_copies(land_refs[i], src_ref, sems[i], sems[n + i])
            for copy in copies:
                copy.wait_recv()
            for copy in copies:
                copy.wait_send()

    outs = pl.pallas_call(
        body, name=name, out_shape=tuple(pltpu.HBM(a.shape, a.dtype) for a in arrays),
        in_specs=[HBM_SPEC] * len(arrays) + [SEM_SPEC] * (2 * n) + [pl.BlockSpec(memory_space=pl.ANY)],
        out_specs=(HBM_SPEC,) * len(arrays),
        input_output_aliases={i: i for i in range(len(arrays))},
        compiler_params=pltpu.CompilerParams(has_side_effects=SPLIT_EFFECT),
    )(*arrays, *send_sems, *recv_sems, after)
    return list(outs[:n])


def _place_own(a, me, name, own_block, dtype=BF16, after=None):
    shape = a.shape[1:] if own_block else a.shape
    cols = shape[-1]
    a3 = a.reshape((N_DEV if own_block else 1, -1, cols))
    rows = a3.shape[1]
    tr = min(rows, 512)

    def body(me_ref, a_ref, _, o_ref):
        o_ref[...] = a_ref[...].astype(dtype)

    grid_spec = pltpu.PrefetchScalarGridSpec(
        num_scalar_prefetch=1, grid=(rows // tr,),
        in_specs=[pl.BlockSpec((1, tr, cols), lambda i, me_ref: (me_ref[0] if own_block else 0, i, 0)),
                  pl.BlockSpec(memory_space=pl.ANY)],
        out_specs=pl.BlockSpec((1, tr, cols), lambda i, me_ref: (me_ref[0], i, 0)))
    out = pl.pallas_call(
        body, name=name, grid_spec=grid_spec, out_shape=jax.ShapeDtypeStruct((N_DEV, rows, cols), dtype),
        compiler_params=_params(("arbitrary",)),
    )(me, a3, a3 if after is None else after)
    return out.reshape((N_DEV,) + shape)


def _place_own_columns(a, me, name, after=None):
    rows, cols = a.shape
    tr = min(rows, 512)

    def body(me_ref, a_ref, _, o_ref):
        o_ref[...] = a_ref[...].astype(BF16)

    grid_spec = pltpu.PrefetchScalarGridSpec(
        num_scalar_prefetch=1, grid=(rows // tr,),
        in_specs=[pl.BlockSpec((tr, cols), lambda i, me_ref: (i, 0)), pl.BlockSpec(memory_space=pl.ANY)],
        out_specs=pl.BlockSpec((tr, cols), lambda i, me_ref: (i, me_ref[0])))
    return pl.pallas_call(
        body, name=name, grid_spec=grid_spec, out_shape=jax.ShapeDtypeStruct((rows, N_DEV * cols), BF16),
        compiler_params=_params(("arbitrary",)),
    )(me, a, a if after is None else after)


def _rope_tables(positions):
    in_head = jnp.arange(LANES) % C_HEAD_DIM
    inv = ROPE_THETA ** (-(in_head % C_ROT_HALF).astype(F32) / C_ROT_HALF)
    ang = positions.reshape(-1)[:, None].astype(F32) * inv
    rotated = in_head < 2 * C_ROT_HALF
    sin = jnp.sin(ang)
    return (jnp.where(rotated, jnp.cos(ang), 1.0),
            jnp.where(in_head < C_ROT_HALF, -sin, jnp.where(rotated, sin, 0.0)))


def _swap_halves(x):
    lane = lax.broadcasted_iota(jnp.int32, x.shape, 1) % C_HEAD_DIM
    return jnp.where(lane < C_ROT_HALF, pltpu.roll(x, LANES - C_ROT_HALF, 1), pltpu.roll(x, C_ROT_HALF, 1))


def _norm_inproj(x, g, w, name):
    t = x.shape[0]
    n = w.shape[1]
    tm, tn = PROJ_TILE, PROJ_COLS

    def body(x_ref, g_ref, w_ref, o_ref, h_ref):
        @pl.when(pl.program_id(1) == 0)
        def _():
            h_ref[...] = _rms(x_ref[...], g_ref[...]).astype(BF16)

        o_ref[...] = _dot(h_ref[...], w_ref[...])

    return pl.pallas_call(
        body, name=name, grid=(t // tm, n // tn),
        in_specs=[pl.BlockSpec((tm, D_MODEL), lambda i, j: (i, 0)), pl.BlockSpec((1, D_MODEL), lambda i, j: (0, 0)),
                  pl.BlockSpec((D_MODEL, tn), lambda i, j: (0, j))],
        out_specs=[pl.BlockSpec((tm, tn), lambda i, j: (i, j)), pl.BlockSpec((tm, D_MODEL), lambda i, j: (i, 0))],
        out_shape=[jax.ShapeDtypeStruct((t, n), F32), jax.ShapeDtypeStruct((t, D_MODEL), BF16)],
        compiler_params=_params(("parallel", "arbitrary")),
    )(x, g, w)


def _dilated_specs(tm, width, col_of):
    per_seq = SEQ // tm
    specs = []
    for d in C_DILATIONS:
        specs.append(pl.BlockSpec(
            (1, d, tm // d, width), lambda i, *rest: (i // per_seq, 0, i % per_seq, col_of(*rest))))
    return specs


def _dilated_shapes(n_seq, cols, dtype):
    return [jax.ShapeDtypeStruct((n_seq, d, SEQ // d, cols), dtype) for d in C_DILATIONS]


def _store_dilated(src_ref, out_refs, dtype):
    groups, tm, _ = src_ref.shape
    for d, o_ref in zip(C_DILATIONS, out_refs):
        for r in range(d):
            rows = pl.ds(r, tm // d, stride=d) if d > 1 else slice(None)
            for p in range(groups):
                o_ref[0, r, :, p * LANES:(p + 1) * LANES] = src_ref.at[p][rows, :].astype(dtype)


def _load_dilated(in_ref, d, dst_ref):
    groups, tm, _ = dst_ref.shape
    for r in range(d):
        rows = pl.ds(r, tm // d, stride=d)
        for p in range(groups):
            dst_ref.at[p][rows, :] = in_ref[0, r, :, p * LANES:(p + 1) * LANES].astype(F32)


def _norm_inproj_rope(x, g, w, rope, name):
    t = x.shape[0]
    n = w.shape[1]
    tm, nb = PROJ_TILE, PROJ_COLS

    def body(x_ref, g_ref, w_ref, c_ref, s_ref, o1_ref, o4_ref, o16_ref, h_ref, tile_ref):
        j = pl.program_id(1)

        @pl.when(j == 0)
        def _():
            h_ref[...] = _rms(x_ref[...], g_ref[...]).astype(BF16)

        acc = _dot(h_ref[...], w_ref[...])
        for p in range(nb // LANES):
            blk = acc[:, p * LANES:(p + 1) * LANES]
            roped = blk * c_ref[...] + _swap_halves(blk) * s_ref[...]
            piece = j * (nb // LANES) + p
            is_qk = piece < 2 * (D_MODEL // LANES)
            tile_ref[p] = jnp.where(is_qk, roped, blk) * jnp.where(piece < D_MODEL // LANES, QK_SCALE, 1.0)
        _store_dilated(tile_ref, (o1_ref, o4_ref, o16_ref), BF16)

    return pl.pallas_call(
        body, name=name, grid=(t // tm, n // nb),
        in_specs=[pl.BlockSpec((tm, D_MODEL), lambda i, j: (i, 0)), pl.BlockSpec((1, D_MODEL), lambda i, j: (0, 0)),
                  pl.BlockSpec((D_MODEL, nb), lambda i, j: (0, j)),
                  pl.BlockSpec((tm, LANES), lambda i, j: (i, 0)), pl.BlockSpec((tm, LANES), lambda i, j: (i, 0))],
        out_specs=_dilated_specs(tm, nb, lambda j: j) + [pl.BlockSpec((tm, D_MODEL), lambda i, j: (i, 0))],
        out_shape=_dilated_shapes(t // SEQ, n, BF16) + [jax.ShapeDtypeStruct((t, D_MODEL), BF16)],
        scratch_shapes=[pltpu.VMEM((nb // LANES, tm, LANES), F32)],
        compiler_params=_params(("parallel", "arbitrary")),
    )(x, g, w, *rope)


def _outproj(parts, w, x, g, name):
    t = x.shape[0]
    tm = PROJ_TILE
    n = len(parts)
    widths = [p.shape[1] for p in parts]

    def body(*refs):
        p_refs = refs[:n]
        w_ref, x_ref, g_ref, xo_ref, mix_ref = refs[n:]
        mix = None
        off = 0
        for p_ref, wd in zip(p_refs, widths):
            term = _dot(p_ref[...].astype(BF16), w_ref[off:off + wd, :])
            mix = term if mix is None else mix + term
            off += wd
        mix_ref[...] = mix
        xo_ref[...] = x_ref[...] + _rms(mix, g_ref[...])

    row = lambda i: (i, 0)
    return pl.pallas_call(
        body, name=name, grid=(t // tm,),
        in_specs=[pl.BlockSpec((tm, wd), row) for wd in widths] + [
            pl.BlockSpec((sum(widths), D_MODEL), lambda i: (0, 0)),
            pl.BlockSpec((tm, D_MODEL), row), pl.BlockSpec((1, D_MODEL), lambda i: (0, 0))],
        out_specs=[pl.BlockSpec((tm, D_MODEL), row)] * 2,
        out_shape=[jax.ShapeDtypeStruct((t, D_MODEL), F32)] * 2,
        compiler_params=_params(("parallel",)),
    )(*parts, w, x, g)


def _outproj_bwd(dx, mix, g, w, name):
    t = dx.shape[0]
    tm = PROJ_TILE
    k = w.shape[0]

    def body(dx_ref, mix_ref, g_ref, w_ref, dcat_ref, dz_ref, dg_ref):
        dz, dgr = _rms_bwd(mix_ref[...], g_ref[...], dx_ref[...])
        dzb = dz.astype(BF16)
        dz_ref[...] = dzb
        dcat_ref[...] = _dot_nt(dzb, w_ref[...])
        _acc_rows8(dg_ref, _rows8(dgr), pl.program_id(0) == 0)

    row = lambda i: (i, 0)
    return pl.pallas_call(
        body, name=name, grid=(t // tm,),
        in_specs=[pl.BlockSpec((tm, D_MODEL), row), pl.BlockSpec((tm, D_MODEL), row),
                  pl.BlockSpec((1, D_MODEL), lambda i: (0, 0)), pl.BlockSpec((k, D_MODEL), lambda i: (0, 0))],
        out_specs=[pl.BlockSpec((tm, k), row), pl.BlockSpec((tm, D_MODEL), row),
                   pl.BlockSpec((SUBLANES, D_MODEL), lambda i: (0, 0))],
        out_shape=[jax.ShapeDtypeStruct((t, k), F32), jax.ShapeDtypeStruct((t, D_MODEL), BF16),
                   jax.ShapeDtypeStruct((SUBLANES, D_MODEL), F32)],
        compiler_params=_params(("arbitrary",)),
    )(dx, mix, g, w)


def _outproj_bwd_attn(dx, mix, g, w, out, name):
    t = dx.shape[0]
    tm = MERGE_TILE

    def body(dx_ref, mix_ref, g_ref, w_ref, out_ref, do1, do4, do16, dl_ref, dz_ref, dg_ref, tile_ref):
        dz, dgr = _rms_bwd(mix_ref[...], g_ref[...], dx_ref[...])
        dzb = dz.astype(BF16)
        dz_ref[...] = dzb
        _acc_rows8(dg_ref, _rows8(dgr), pl.program_id(0) == 0)
        dout = _dot_nt(dzb, w_ref[...])
        for p in range(LANE_GROUPS):
            tile_ref[p] = dout[:, p * LANES:(p + 1) * LANES]
        _store_dilated(tile_ref, (do1, do4, do16), BF16)
        column = lax.broadcasted_iota(jnp.int32, (D_MODEL, LANES), 0) // C_HEAD_DIM
        head = lax.broadcasted_iota(jnp.int32, (D_MODEL, LANES), 1)
        dl_ref[...] = _dot_mask(dout * out_ref[...], column == head)

    row = lambda i: (i, 0)
    n_seq = t // SEQ
    return pl.pallas_call(
        body, name=name, grid=(t // tm,),
        in_specs=[pl.BlockSpec((tm, D_MODEL), row), pl.BlockSpec((tm, D_MODEL), row),
                  pl.BlockSpec((1, D_MODEL), lambda i: (0, 0)), pl.BlockSpec((D_MODEL, D_MODEL), lambda i: (0, 0)),
                  pl.BlockSpec((tm, D_MODEL), row)],
        out_specs=_dilated_specs(tm, D_MODEL, lambda: 0) + [
            pl.BlockSpec((tm, LANES), row), pl.BlockSpec((tm, D_MODEL), row),
            pl.BlockSpec((SUBLANES, D_MODEL), lambda i: (0, 0))],
        out_shape=_dilated_shapes(n_seq, D_MODEL, BF16) + [
            jax.ShapeDtypeStruct((t, LANES), F32), jax.ShapeDtypeStruct((t, D_MODEL), BF16),
            jax.ShapeDtypeStruct((SUBLANES, D_MODEL), F32)],
        scratch_shapes=[pltpu.VMEM((LANE_GROUPS, tm, LANES), F32)],
        compiler_params=_params(("arbitrary",)),
    )(dx, mix, g, w, out)


def _inproj_bwd(dproj, w, dx, x, g, name):
    t = x.shape[0]
    n = w.shape[1]
    tm = ROW_TILE

    def body(dp_ref, w_ref, dx_ref, x_ref, g_ref, o_ref, dg_ref):
        dxn, dgr = _rms_bwd(x_ref[...], g_ref[...], _dot_nt(dp_ref[...], w_ref[...]))
        o_ref[...] = dx_ref[...] + dxn
        _acc_rows8(dg_ref, _rows8(dgr), pl.program_id(0) == 0)

    row = lambda i: (i, 0)
    return pl.pallas_call(
        body, name=name, grid=(t // tm,),
        in_specs=[pl.BlockSpec((tm, n), row), pl.BlockSpec((D_MODEL, n), lambda i: (0, 0)),
                  pl.BlockSpec((tm, D_MODEL), row), pl.BlockSpec((tm, D_MODEL), row),
                  pl.BlockSpec((1, D_MODEL), lambda i: (0, 0))],
        out_specs=[pl.BlockSpec((tm, D_MODEL), row), pl.BlockSpec((SUBLANES, D_MODEL), lambda i: (0, 0))],
        out_shape=[jax.ShapeDtypeStruct((t, D_MODEL), F32), jax.ShapeDtypeStruct((SUBLANES, D_MODEL), F32)],
        compiler_params=_params(("arbitrary",)),
    )(dproj, w, dx, x, g)


def _grad_w(a, b, col_blocks, name):
    t, k = a.shape
    n = b.shape[1]
    tk = min(k, 1024)
    per_owner = n // N_DEV
    tn = 2 * per_owner if col_blocks else min(n, 1024)

    def body(a_ref, b_ref, o_ref, at_ref):
        @pl.when(pl.program_id(1) == 0)
        def _():
            for c in range(t // ROW_TILE):
                rows = slice(c * ROW_TILE, (c + 1) * ROW_TILE)
                at_ref[:, rows] = a_ref[rows, :].T

        res = _dot(at_ref[...], b_ref[...]).astype(BF16)
        if col_blocks:
            o_ref[0] = res[:, :per_owner]
            o_ref[1] = res[:, per_owner:]
        else:
            o_ref[...] = res

    if col_blocks:
        out_spec = pl.BlockSpec((2, tk, per_owner), lambda i, j: (j, i, 0))
        out_shape = jax.ShapeDtypeStruct((N_DEV, k, per_owner), BF16)
    else:
        out_spec = pl.BlockSpec((tk, tn), lambda i, j: (i, j))
        out_shape = jax.ShapeDtypeStruct((k, n), BF16)
    return pl.pallas_call(
        body, name=name, grid=(k // tk, n // tn),
        in_specs=[pl.BlockSpec((t, tk), lambda i, j: (0, i)), pl.BlockSpec((t, tn), lambda i, j: (0, j))],
        out_specs=out_spec, out_shape=out_shape,
        scratch_shapes=[pltpu.VMEM((tk, t), BF16)],
        compiler_params=_params(("parallel", "arbitrary")),
    )(a, b)


FF_STEP = 1024
FF_STEPS = D_FF // FF_STEP


def _ffn_fwd(x, g_pre, w1, w2, g_post, name, target=None):
    t = x.shape[0]
    tm = PROJ_TILE

    def body(*refs):
        if target is None:
            x_ref, gp_ref, w1_ref, w2_ref, gq_ref, xo_ref, y_ref, h_ref, r_ref = refs
        else:
            x_ref, gp_ref, w1_ref, w2_ref, gq_ref, t_ref, xo_ref, y_ref, h_ref, r_ref, l_ref = refs
        i, j = pl.program_id(0), pl.program_id(1)

        @pl.when(j == 0)
        def _():
            h_ref[...] = _rms(x_ref[...], gp_ref[...]).astype(BF16)

        a = _dot(h_ref[...], w1_ref[...])
        r = jnp.square(jnp.maximum(a, 0.0)).astype(BF16)
        r_ref[...] = r
        term = _dot(r, w2_ref[...])

        @pl.when(j == 0)
        def _():
            y_ref[...] = term

        @pl.when(j > 0)
        def _():
            y_ref[...] += term

        @pl.when(j == FF_STEPS - 1)
        def _():
            x_new = x_ref[...] + _rms(y_ref[...], gq_ref[...])
            if target is None:
                xo_ref[...] = x_new
            else:
                diff = x_new - t_ref[...]
                xo_ref[...] = diff * (1.0 / D_MODEL)
                _acc_rows8(l_ref, _rows8(diff * diff) * (0.5 / D_MODEL), i == 0)

    row = lambda i, j: (i, 0)
    vec = pl.BlockSpec((1, D_MODEL), lambda i, j: (0, 0))
    in_specs = [pl.BlockSpec((tm, D_MODEL), row), vec, pl.BlockSpec((D_MODEL, FF_STEP), lambda i, j: (0, j)),
                pl.BlockSpec((FF_STEP, D_MODEL), lambda i, j: (j, 0)), vec]
    out_specs = [pl.BlockSpec((tm, D_MODEL), row)] * 3 + [pl.BlockSpec((tm, FF_STEP), lambda i, j: (i, j))]
    out_shape = [jax.ShapeDtypeStruct((t, D_MODEL), F32), jax.ShapeDtypeStruct((t, D_MODEL), F32),
                 jax.ShapeDtypeStruct((t, D_MODEL), BF16), jax.ShapeDtypeStruct((t, D_FF), BF16)]
    args = [x, g_pre, w1, w2, g_post]
    if target is not None:
        in_specs.append(pl.BlockSpec((tm, D_MODEL), row))
        out_specs.append(pl.BlockSpec((SUBLANES, D_MODEL), lambda i, j: (0, 0)))
        out_shape.append(jax.ShapeDtypeStruct((SUBLANES, D_MODEL), F32))
        args.append(target)
    return pl.pallas_call(
        body, name=name, grid=(t // tm, FF_STEPS), in_specs=in_specs, out_specs=out_specs, out_shape=out_shape,
        compiler_params=_params(("parallel" if target is None else "arbitrary", "arbitrary")),
    )(*args)


def _ffn_bwd(dxo, x, y, r, g_pre, w1, w2, g_post, name):
    t = x.shape[0]
    tm = ROW_TILE

    def body(dxo_ref, x_ref, y_ref, r_ref, gp_ref, w1_ref, w2_ref, gq_ref,
             dx_ref, dy_ref, da_ref, dgp_ref, dgq_ref, acc_ref):
        i, j = pl.program_id(0), pl.program_id(1)

        @pl.when(j == 0)
        def _():
            dy, dgr = _rms_bwd(y_ref[...], gq_ref[...], dxo_ref[...])
            dy_ref[...] = dy.astype(BF16)
            _acc_rows8(dgq_ref, _rows8(dgr), i == 0)

        dr = _dot_nt(dy_ref[...], w2_ref[...])
        da = (dr * (2.0 * jnp.sqrt(r_ref[...].astype(F32)))).astype(BF16)
        da_ref[...] = da
        term = _dot_nt(da, w1_ref[...])

        @pl.when(j == 0)
        def _():
            acc_ref[...] = term

        @pl.when(j > 0)
        def _():
            acc_ref[...] += term

        @pl.when(j == FF_STEPS - 1)
        def _():
            dxn, dgr = _rms_bwd(x_ref[...], gp_ref[...], acc_ref[...])
            dx_ref[...] = dxo_ref[...] + dxn
            _acc_rows8(dgp_ref, _rows8(dgr), i == 0)

    row = lambda i, j: (i, 0)
    vec = pl.BlockSpec((1, D_MODEL), lambda i, j: (0, 0))
    acc8 = pl.BlockSpec((SUBLANES, D_MODEL), lambda i, j: (0, 0))
    return pl.pallas_call(
        body, name=name, grid=(t // tm, FF_STEPS),
        in_specs=[pl.BlockSpec((tm, D_MODEL), row)] * 3 + [
            pl.BlockSpec((tm, FF_STEP), lambda i, j: (i, j)),
            vec, pl.BlockSpec((D_MODEL, FF_STEP), lambda i, j: (0, j)),
            pl.BlockSpec((FF_STEP, D_MODEL), lambda i, j: (j, 0)), vec],
        out_specs=[pl.BlockSpec((tm, D_MODEL), row), pl.BlockSpec((tm, D_MODEL), row),
                   pl.BlockSpec((tm, FF_STEP), lambda i, j: (i, j)), acc8, acc8],
        out_shape=[jax.ShapeDtypeStruct((t, D_MODEL), F32), jax.ShapeDtypeStruct((t, D_MODEL), BF16),
                   jax.ShapeDtypeStruct((t, D_FF), BF16),
                   jax.ShapeDtypeStruct((SUBLANES, D_MODEL), F32), jax.ShapeDtypeStruct((SUBLANES, D_MODEL), F32)],
        scratch_shapes=[pltpu.VMEM((tm, D_MODEL), F32)],
        compiler_params=_params(("arbitrary", "arbitrary")),
    )(dxo, x, y, r, g_pre, w1, w2, g_post)


def _lower_bound(table):
    e = jnp.exp(table - jnp.max(table, axis=0, keepdims=True))
    return e[0:1, :] / jnp.sum(e, axis=0, keepdims=True)


def _hgrn2_block(q_ref, f_ref, lb):
    tb = f_ref.shape[0]
    sig = _sigmoid(f_ref[...])
    f = lb + (1.0 - lb) * sig
    qraw = q_ref[...]
    sq = _sigmoid(qraw)
    r = lax.broadcasted_iota(jnp.int32, (tb, tb), 0)
    c = lax.broadcasted_iota(jnp.int32, (tb, tb), 1)
    same = (r // SUB_CHUNK) == (c // SUB_CHUNK)
    logf = jnp.log(f)
    gsum = _mask_dot(same & (c <= r), logf)
    glast = _mask_dot(same, logf)
    return dict(sig=sig, f=f, kk=1.0 - f, qraw=qraw, sq=sq, qs=qraw * sq, gsum=gsum,
                eg=jnp.exp(gsum), ekd=jnp.exp(glast - gsum), a=jnp.exp(glast))


def _head_sums(x):
    parts = [jnp.broadcast_to(jnp.sum(x[:, h * HEAD_A:(h + 1) * HEAD_A], axis=1, keepdims=True), (x.shape[0], HEAD_A))
             for h in range(A_HEADS)]
    return jnp.concatenate(parts, axis=1)


def _hgrn2_intra(g, kk, qs, v):
    row = lax.broadcasted_iota(jnp.int32, g.shape, 0)
    o = _head_sums(qs * kk) * v
    for j in range(1, SUB_CHUNK):
        decay = jnp.exp(jnp.where(row >= j, g - pltpu.roll(g, j, 0), NEG))
        o = o + _head_sums(qs * pltpu.roll(kk, j, 0) * decay) * pltpu.roll(v, j, 0)
    return o


def _hgrn2_intra_bwd(g, kk, qs, v, do):
    row = lax.broadcasted_iota(jnp.int32, g.shape, 0)
    dsc = _head_sums(do * v)
    dqs, dkk, dv = dsc * kk, dsc * qs, _head_sums(qs * kk) * do
    for j in range(1, SUB_CHUNK):
        k_dn = pltpu.roll(kk, j, 0)
        decay = jnp.exp(jnp.where(row >= j, g - pltpu.roll(g, j, 0), NEG))
        d_score = _head_sums(do * pltpu.roll(v, j, 0)) * decay
        dqs = dqs + d_score * k_dn
        dkk = dkk + pltpu.roll(d_score * qs, SUB_CHUNK - j, 0)
        dv = dv + pltpu.roll(_head_sums(qs * k_dn * decay) * do, SUB_CHUNK - j, 0)
    return dqs, dkk, dv


def _hgrn2_fwd(proj, lb_table, a_norm, name):
    t = proj.shape[0]
    tb = HGRN_BLOCK
    n_tb = SEQ // tb
    n_seq = t // SEQ
    n_sub = tb // SUB_CHUNK

    def body(q_ref, f_ref, i_ref, g_ref, lbt_ref, an_ref, o_ref, pre_ref, sts_ref, st_ref,
             gs_ref, kk_ref, qs_ref, eg_ref, ekd_ref, a_ref):
        @pl.when(pl.program_id(1) == 0)
        def _():
            st_ref[...] = jnp.zeros_like(st_ref)

        an = an_ref[...]
        blk = _hgrn2_block(q_ref, f_ref, _lower_bound(lbt_ref[...]))
        for ref, key in ((gs_ref, "gsum"), (kk_ref, "kk"), (qs_ref, "qs"), (eg_ref, "eg"), (ekd_ref, "ekd"), (a_ref, "a")):
            ref[...] = blk[key]

        def step(c, carry):
            rows = pl.ds(pl.multiple_of(c * SUB_CHUNK, SUB_CHUNK), SUB_CHUNK)
            kk, qs, v = kk_ref[rows, :], qs_ref[rows, :], i_ref[rows, :]
            o = _hgrn2_intra(gs_ref[rows, :], kk, qs, v)
            qg, kd, vb = (qs * eg_ref[rows, :]).astype(BF16), (kk * ekd_ref[rows, :]).astype(BF16), v.astype(BF16)
            for h in range(A_HEADS):
                lanes = slice(h * HEAD_A, (h + 1) * HEAD_A)
                st = st_ref[h]
                sts_ref[0, c, h] = st
                o_h = o[:, lanes] + _dot_nt(qg[:, lanes], st.astype(BF16))
                st_ref[h] = st * a_ref[rows, lanes][0:1] + _dot_tn(vb[:, lanes], kd[:, lanes])
                pre_ref[rows, lanes] = o_h
                graw = g_ref[rows, lanes]
                o_ref[rows, lanes] = (_rms(o_h, an[:, lanes]) * (graw * _sigmoid(graw))).astype(BF16)
            return carry

        lax.fori_loop(0, n_sub, step, 0, unroll=2)

    def col(k):
        return pl.BlockSpec((tb, A_WIDTH), lambda b, s, k=k: (b * n_tb + s, k))

    out_rows = pl.BlockSpec((tb, A_WIDTH), lambda b, s: (b * n_tb + s, 0))
    return pl.pallas_call(
        body, name=name, grid=(n_seq, n_tb),
        in_specs=[col(0), col(1), col(2), col(3),
                  pl.BlockSpec((3, A_WIDTH), lambda b, s: (0, 0)), pl.BlockSpec((1, A_WIDTH), lambda b, s: (0, 0))],
        out_specs=[out_rows, out_rows,
                   pl.BlockSpec((1, n_sub, A_HEADS, HEAD_A, HEAD_A), lambda b, s: (b * n_tb + s, 0, 0, 0, 0))],
        out_shape=[jax.ShapeDtypeStruct((t, D_MODEL), BF16), jax.ShapeDtypeStruct((t, A_WIDTH), F32),
                   jax.ShapeDtypeStruct((n_seq * n_tb, n_sub, A_HEADS, HEAD_A, HEAD_A), F32)],
        scratch_shapes=[pltpu.VMEM((A_HEADS, HEAD_A, HEAD_A), F32)] + [pltpu.VMEM((tb, A_WIDTH), F32)] * 6,
        compiler_params=_params(("parallel", "arbitrary")),
    )(proj, proj, proj, proj, lb_table, a_norm)


def _hgrn2_bwd(proj, dcat, pre, states, lb_table, a_norm, name):
    t = proj.shape[0]
    tb = HGRN_BLOCK
    n_tb = SEQ // tb
    n_seq = t // SEQ
    n_sub = tb // SUB_CHUNK

    def body(q_ref, f_ref, i_ref, g_ref, do_ref, pre_ref, sts_ref, lbt_ref, an_ref, dp_ref, dlb_ref, dan_ref, dst_ref,
             gs_ref, kk_ref, qs_ref, eg_ref, ekd_ref, a_ref, dpre_ref, dlf_ref, dqs_ref, dkk_ref):
        b, s = pl.program_id(0), pl.program_id(1)

        @pl.when(s == 0)
        def _():
            dst_ref[...] = jnp.zeros_like(dst_ref)

        @pl.when((b == 0) & (s == 0))
        def _():
            dlb_ref[...] = jnp.zeros_like(dlb_ref)
            dan_ref[...] = jnp.zeros_like(dan_ref)

        lb = _lower_bound(lbt_ref[...])
        an = an_ref[...]
        heads = [slice(h * HEAD_A, (h + 1) * HEAD_A) for h in range(A_HEADS)]
        blk = _hgrn2_block(q_ref, f_ref, lb)
        for ref, key in ((gs_ref, "gsum"), (kk_ref, "kk"), (qs_ref, "qs"), (eg_ref, "eg"), (ekd_ref, "ekd"), (a_ref, "a")):
            ref[...] = blk[key]
        for h, lanes in enumerate(heads):
            graw, o = g_ref[:, lanes], pre_ref[:, lanes]
            sg = _sigmoid(graw)
            dout = do_ref[:, lanes]
            d_o, dgr = _rms_bwd(o, an[:, lanes], dout * (graw * sg))
            dan_ref[0:1, lanes] += jnp.sum(dgr, axis=0, keepdims=True)
            dp_ref[:, 3 * A_WIDTH + h * HEAD_A:3 * A_WIDTH + (h + 1) * HEAD_A] = (
                dout * _rms(o, an[:, lanes]) * (sg * (1.0 + graw * (1.0 - sg)))).astype(BF16)
            dpre_ref[:, lanes] = d_o

        tri_t = (lax.broadcasted_iota(jnp.int32, (SUB_CHUNK, SUB_CHUNK), 0)
                 <= lax.broadcasted_iota(jnp.int32, (SUB_CHUNK, SUB_CHUNK), 1)).astype(F32)

        def back(k, carry):
            c = n_sub - 1 - k
            rows = pl.ds(pl.multiple_of(c * SUB_CHUNK, SUB_CHUNK), SUB_CHUNK)
            g, kk, qs, v, d_o = gs_ref[rows, :], kk_ref[rows, :], qs_ref[rows, :], i_ref[rows, :], dpre_ref[rows, :]
            eg, ekd, a = eg_ref[rows, :], ekd_ref[rows, :], a_ref[rows, :]
            dqs, dkk, dv = _hgrn2_intra_bwd(g, kk, qs, v, d_o)
            qg_f, kd_f = qs * eg, kk * ekd
            qg, kd, vb, dob = qg_f.astype(BF16), kd_f.astype(BF16), v.astype(BF16), d_o.astype(BF16)
            dqg, dkd, da, dv_st = [], [], [], []
            for h, lanes in enumerate(heads):
                st, dst = sts_ref[0, c, h], dst_ref[h]
                dstb = dst.astype(BF16)
                dqg.append(_dot(dob[:, lanes], st.astype(BF16)))
                dv_st.append(_dot_nt(kd[:, lanes], dstb))
                dkd.append(_dot(vb[:, lanes], dstb))
                da.append(jnp.broadcast_to(jnp.sum(dst * st, axis=0, keepdims=True), (SUB_CHUNK, HEAD_A)))
                dst_ref[h] = dst * a[0:1, lanes] + _dot_tn(dob[:, lanes], qg[:, lanes])
            dqg, dkd, da, dv_st = [jnp.concatenate(p, axis=1) for p in (dqg, dkd, da, dv_st)]
            d_gsum = qs * dqs - kk * dkk + dqg * qg_f - dkd * kd_f
            d_glast = jnp.sum(dkd * kd_f, axis=0, keepdims=True) + da * a
            dlf_ref[rows, :] = jnp.dot(tri_t, d_gsum, precision=lax.Precision.HIGHEST,
                                       preferred_element_type=F32) + d_glast
            dqs_ref[rows, :] = dqs + dqg * eg
            dkk_ref[rows, :] = dkk + dkd * ekd
            dp_ref[rows, 2 * A_WIDTH:3 * A_WIDTH] = (dv + dv_st).astype(BF16)
            return carry

        lax.fori_loop(0, n_sub, back, 0, unroll=2)
        sig, sq, qraw = blk["sig"], blk["sq"], blk["qraw"]
        d_f = dlf_ref[...] / blk["f"] - dkk_ref[...]
        dlb_ref[0:1, :] += jnp.sum(d_f * (1.0 - sig), axis=0, keepdims=True)
        dp_ref[:, 0:A_WIDTH] = (dqs_ref[...] * (sq * (1.0 + qraw * (1.0 - sq)))).astype(BF16)
        dp_ref[:, A_WIDTH:2 * A_WIDTH] = (d_f * (1.0 - lb) * sig * (1.0 - sig)).astype(BF16)

    def rev(s):
        return n_tb - 1 - s

    def col(k):
        return pl.BlockSpec((tb, A_WIDTH), lambda b, s, k=k: (b * n_tb + rev(s), k))

    acc8 = pl.BlockSpec((SUBLANES, A_WIDTH), lambda b, s: (0, 0))
    return pl.pallas_call(
        body, name=name, grid=(n_seq, n_tb),
        in_specs=[col(0), col(1), col(2), col(3), col(0), col(0),
                  pl.BlockSpec((1, n_sub, A_HEADS, HEAD_A, HEAD_A), lambda b, s: (b * n_tb + rev(s), 0, 0, 0, 0)),
                  pl.BlockSpec((3, A_WIDTH), lambda b, s: (0, 0)), pl.BlockSpec((1, A_WIDTH), lambda b, s: (0, 0))],
        out_specs=[pl.BlockSpec((tb, 4 * A_WIDTH), lambda b, s: (b * n_tb + rev(s), 0)), acc8, acc8],
        out_shape=[jax.ShapeDtypeStruct((t, EVEN_IN), BF16)] + [jax.ShapeDtypeStruct((SUBLANES, A_WIDTH), F32)] * 2,
        scratch_shapes=[pltpu.VMEM((A_HEADS, HEAD_A, HEAD_A), F32)] + [pltpu.VMEM((tb, A_WIDTH), F32)] * 10,
        compiler_params=_params(("arbitrary", "arbitrary")),
    )(proj, proj, proj, proj, dcat, pre, states, lb_table, a_norm)


GMLP_ROWS = 512


def _gmlp_chunk(ub, vb, ln_g, ln_b, ws, bias):
    u = [_gelu(a) for a in ub]
    v = [_gelu(a) for a in vb]
    mu = sum(jnp.sum(a, axis=-1, keepdims=True) for a in v) * (1.0 / B_WIDTH)
    cen = [a - mu for a in v]
    var = sum(jnp.sum(a * a, axis=-1, keepdims=True) for a in cen) * (1.0 / B_WIDTH)
    inv = lax.rsqrt(var + EPS)
    r = lax.broadcasted_iota(jnp.int32, (B_CHUNK, B_CHUNK), 0)
    c = lax.broadcasted_iota(jnp.int32, (B_CHUNK, B_CHUNK), 1)
    outs = []
    for g in range(B_GROUPS):
        vn = (cen[g] * inv * ln_g[g] + ln_b[g]).astype(BF16)
        wm = jnp.where(c <= r, ws[g], 0.0).astype(BF16)
        outs.append(u[g] * (_dot(wm, vn) + bias[g]))
    return outs


def _lane_groups(ref, rows=slice(None)):
    return [ref[rows, g * LANES:(g + 1) * LANES] for g in range(B_GROUPS)]


def _gmlp_fwd(proj, mixed, ln_g, ln_b, ws, bias_t, name):
    t = proj.shape[0]
    tm = GMLP_ROWS

    def body(u_ref, v_ref, lg_ref, lb_ref, ws_ref, bt_ref, _, o_ref):
        for ch in range(tm // B_CHUNK):
            rows = slice(ch * B_CHUNK, (ch + 1) * B_CHUNK)
            outs = _gmlp_chunk(_lane_groups(u_ref, rows), _lane_groups(v_ref, rows), _lane_groups(lg_ref),
                               _lane_groups(lb_ref), [ws_ref[g] for g in range(B_GROUPS)],
                               [bt_ref[:, g:g + 1] for g in range(B_GROUPS)])
            for g in range(B_GROUPS):
                o_ref[rows, g * LANES:(g + 1) * LANES] = outs[g].astype(BF16)

    vec = pl.BlockSpec((1, B_WIDTH), lambda i: (0, 0))
    return pl.pallas_call(
        body, name=name, grid=(t // tm,),
        in_specs=[pl.BlockSpec((tm, B_WIDTH), lambda i: (i, 4)), pl.BlockSpec((tm, B_WIDTH), lambda i: (i, 5)), vec, vec,
                  pl.BlockSpec((B_GROUPS, B_CHUNK, B_CHUNK), lambda i: (0, 0, 0)),
                  pl.BlockSpec((B_CHUNK, B_GROUPS), lambda i: (0, 0)), pl.BlockSpec(memory_space=pl.ANY)],
        out_specs=pl.BlockSpec((tm, B_WIDTH), lambda i: (i, 1)),
        out_shape=jax.ShapeDtypeStruct(mixed.shape, BF16),
        input_output_aliases={6: 0},
        compiler_params=_params(("parallel",)),
    )(proj, proj, ln_g, ln_b, ws, bias_t, mixed)


def _gmlp_bwd(proj, dcat, dproj, ln_g, ln_b, ws, bias_t, name):
    t = proj.shape[0]
    tm = GMLP_ROWS

    def body(u_ref, v_ref, do_ref, lg_ref, lb_ref, ws_ref, bt_ref, _, duv_ref, dlg_ref, dlb_ref, dws_ref, dbt_ref):
        @pl.when(pl.program_id(0) == 0)
        def _():
            dlg_ref[...] = jnp.zeros_like(dlg_ref)
            dlb_ref[...] = jnp.zeros_like(dlb_ref)
            dws_ref[...] = jnp.zeros_like(dws_ref)
            dbt_ref[...] = jnp.zeros_like(dbt_ref)

        for ch in range(tm // B_CHUNK):
            rows = slice(ch * B_CHUNK, (ch + 1) * B_CHUNK)
            _, vjp = jax.vjp(
                _gmlp_chunk, _lane_groups(u_ref, rows), _lane_groups(v_ref, rows), _lane_groups(lg_ref),
                _lane_groups(lb_ref), [ws_ref[g] for g in range(B_GROUPS)],
                [bt_ref[:, g:g + 1] for g in range(B_GROUPS)])
            du, dv, dlg, dlb, dw, dbt = vjp(_lane_groups(do_ref, rows))
            for g in range(B_GROUPS):
                lanes = slice(g * LANES, (g + 1) * LANES)
                duv_ref[rows, lanes] = du[g].astype(BF16)
                duv_ref[rows, B_WIDTH + g * LANES:B_WIDTH + (g + 1) * LANES] = dv[g].astype(BF16)
                dlg_ref[0:1, lanes] += dlg[g]
                dlb_ref[0:1, lanes] += dlb[g]
                dws_ref[g] += dw[g]
                dbt_ref[:, g:g + 1] += dbt[g]

    vec = pl.BlockSpec((1, B_WIDTH), lambda i: (0, 0))
    acc8 = pl.BlockSpec((SUBLANES, B_WIDTH), lambda i: (0, 0))
    ws_spec = pl.BlockSpec((B_GROUPS, B_CHUNK, B_CHUNK), lambda i: (0, 0, 0))
    bt_spec = pl.BlockSpec((B_CHUNK, B_GROUPS), lambda i: (0, 0))
    return pl.pallas_call(
        body, name=name, grid=(t // tm,),
        in_specs=[pl.BlockSpec((tm, B_WIDTH), lambda i: (i, 4)), pl.BlockSpec((tm, B_WIDTH), lambda i: (i, 5)),
                  pl.BlockSpec((tm, B_WIDTH), lambda i: (i, 1)), vec, vec, ws_spec, bt_spec,
                  pl.BlockSpec(memory_space=pl.ANY)],
        out_specs=[pl.BlockSpec((tm, 2 * B_WIDTH), lambda i: (i, 2)), acc8, acc8, ws_spec, bt_spec],
        out_shape=[jax.ShapeDtypeStruct(dproj.shape, BF16), jax.ShapeDtypeStruct((SUBLANES, B_WIDTH), F32),
                   jax.ShapeDtypeStruct((SUBLANES, B_WIDTH), F32),
                   jax.ShapeDtypeStruct((B_GROUPS, B_CHUNK, B_CHUNK), F32),
                   jax.ShapeDtypeStruct((B_CHUNK, B_GROUPS), F32)],
        input_output_aliases={7: 0},
        compiler_params=_params(("arbitrary",)),
    )(proj, proj, dcat, ln_g, ln_b, ws, bias_t, dproj)


QK_SCALE = 1.0 / math.sqrt(C_HEAD_DIM)
ATTN_UNROLL = 8
LANE_GROUPS = D_MODEL // LANES
Q_BLOCKS = SEQ // C_BLOCK


def _attn_window(i, d):
    sub_blocks = Q_BLOCKS // d
    q0 = pl.multiple_of(i * C_BLOCK, C_BLOCK)
    k0 = pl.multiple_of(jnp.maximum(i - 1, 0) * C_BLOCK, C_BLOCK)
    key = k0 + lax.broadcasted_iota(jnp.int32, (C_BLOCK, 2 * C_BLOCK), 1)
    dist = (q0 + lax.broadcasted_iota(jnp.int32, (C_BLOCK, 2 * C_BLOCK), 0)) - key
    own_subsequence = (key >= q0) | (i % sub_blocks > 0)
    return pl.ds(q0, C_BLOCK), pl.ds(k0, 2 * C_BLOCK), (dist >= 0) & (dist <= C_BLOCK) & own_subsequence


def _head_masks():
    lane = lax.broadcasted_iota(jnp.int32, (C_BLOCK, LANES), 1)
    return [lane < C_HEAD_DIM, lane >= C_HEAD_DIM]


def _flat_spec(col_of):
    return pl.BlockSpec((1, SEQ, LANES), lambda b, g: (b, 0, col_of(g)))


def _put_heads(tile, g, col0, col1):
    lane = lax.broadcasted_iota(jnp.int32, tile.shape, 1)
    return jnp.where(lane == 2 * g, col0, jnp.where(lane == 2 * g + 1, col1, tile))


def _get_head(tile, h):
    lane = lax.broadcasted_iota(jnp.int32, tile.shape, 1)
    return jnp.sum(jnp.where(lane == h, tile, 0.0), axis=1, keepdims=True)


PER_HEAD_SPEC = pl.BlockSpec((1, SEQ, LANES), lambda b, g: (b, 0, 0))


def _attn_branch_fwd(qkv, name):
    n_seq, d, l, _ = qkv.shape
    flat = qkv.reshape(n_seq, SEQ, ODD_IN)

    def body(q_ref, k_ref, v_ref, o_ref, m_ref, l_ref):
        heads = _head_masks()
        g = pl.program_id(1)

        @pl.when(g == 0)
        def _():
            m_ref[...] = jnp.zeros_like(m_ref)
            l_ref[...] = jnp.zeros_like(l_ref)

        def block(i, carry):
            rows, keys, mask = _attn_window(i, d)
            q, k, v = q_ref[0, rows, :], k_ref[0, keys, :], v_ref[0, keys, :]
            res = []
            for hm in heads:
                s = jnp.where(mask, _dot_nt(jnp.where(hm, q, 0), k), NEG)
                m = jnp.max(s, axis=-1, keepdims=True)
                p = jnp.exp(s - m)
                res.append((_dot(p.astype(BF16), v), m, jnp.sum(p, axis=-1, keepdims=True)))
            o_ref[0, rows, :] = jnp.where(heads[0], res[0][0], res[1][0]).astype(BF16)
            m_ref[0, rows, :] = _put_heads(m_ref[0, rows, :], g, res[0][1], res[1][1])
            l_ref[0, rows, :] = _put_heads(l_ref[0, rows, :], g, res[0][2], res[1][2])
            return carry

        lax.fori_loop(0, Q_BLOCKS, block, 0, unroll=ATTN_UNROLL)

    o, m, l_sum = pl.pallas_call(
        body, name=name, grid=(n_seq, LANE_GROUPS),
        in_specs=[_flat_spec(lambda g: g), _flat_spec(lambda g: LANE_GROUPS + g),
                  _flat_spec(lambda g: 2 * LANE_GROUPS + g)],
        out_specs=[_flat_spec(lambda g: g), PER_HEAD_SPEC, PER_HEAD_SPEC],
        out_shape=[jax.ShapeDtypeStruct((n_seq, SEQ, D_MODEL), BF16)] + [jax.ShapeDtypeStruct((n_seq, SEQ, LANES), F32)] * 2,
        compiler_params=_params(("parallel", "arbitrary")),
    )(flat, flat, flat)
    return [o.reshape(n_seq, d, l, D_MODEL), m.reshape(n_seq, d, l, LANES), l_sum.reshape(n_seq, d, l, LANES)]


def _attn_merge(branches, name):
    n_seq = branches[0][0].shape[0]
    t = n_seq * SEQ
    tm = MERGE_TILE

    def body(*refs):
        ins = refs[:9]
        o_ref, ob_ref, lse_ref = refs[9:12]
        nat = refs[12:]
        for b, d in enumerate(C_DILATIONS[1:]):
            for k in range(3):
                _load_dilated(ins[3 + 3 * b + k], d, nat[3 * b + k])
        ms = [ins[1][0, 0], nat[1][0], nat[4][0]]
        ls = [ins[2][0, 0], nat[2][0], nat[5][0]]
        m_all = jnp.maximum(jnp.maximum(ms[0], ms[1]), ms[2])
        ws = [jnp.exp(ms[b] - m_all) for b in range(3)]
        lane = lax.broadcasted_iota(jnp.int32, m_all.shape, 1)
        total = jnp.where(lane < C_HEADS, ws[0] * ls[0] + ws[1] * ls[1] + ws[2] * ls[2], 1.0)
        lse_ref[...] = m_all + jnp.log(total)
        first_head = lane < C_HEAD_DIM
        for p in range(LANE_GROUPS):
            lanes = slice(p * LANES, (p + 1) * LANES)
            spread = lambda c: jnp.where(first_head, c[:, 2 * p:2 * p + 1], c[:, 2 * p + 1:2 * p + 2])
            os_ = [ins[0][0, 0, :, lanes], nat[0][p], nat[3][p]]
            o = (spread(ws[0]) * os_[0] + spread(ws[1]) * os_[1] + spread(ws[2]) * os_[2]) / spread(total)
            o_ref[:, lanes] = o
            ob_ref[:, lanes] = o.astype(BF16)

    row = pl.BlockSpec((tm, D_MODEL), lambda i: (i, 0))
    flat = [a for br in branches for a in br]
    in_specs = []
    for wide, narrow in zip(_dilated_specs(tm, D_MODEL, lambda: 0), _dilated_specs(tm, LANES, lambda: 0)):
        in_specs += [wide, narrow, narrow]
    per_head = pltpu.VMEM((1, tm, LANES), F32)
    return pl.pallas_call(
        body, name=name, grid=(t // tm,), in_specs=in_specs,
        out_specs=[row, row, pl.BlockSpec((tm, LANES), lambda i: (i, 0))],
        out_shape=[jax.ShapeDtypeStruct((t, D_MODEL), F32), jax.ShapeDtypeStruct((t, D_MODEL), BF16),
                   jax.ShapeDtypeStruct((t, LANES), F32)],
        scratch_shapes=[pltpu.VMEM((LANE_GROUPS, tm, LANES), F32), per_head, per_head] * 2,
        compiler_params=_params(("parallel",)),
    )(*flat)


def _attn_branch_bwd(qkv, dout, lse, delta, name):
    n_seq, d, l, _ = qkv.shape
    flat = lambda a: a.reshape(n_seq, SEQ, a.shape[-1])

    def body(q_ref, k_ref, v_ref, do_ref, lse_nat_ref, dl_nat_ref, dq_ref, dk_ref, dv_ref, lse_ref, dl_ref,
             dkt_ref, dvt_ref):
        heads = _head_masks()
        g = pl.program_id(1)
        dkt_ref[...] = jnp.zeros_like(dkt_ref)
        dvt_ref[...] = jnp.zeros_like(dvt_ref)
        for nat_ref, dst_ref in ((lse_nat_ref, lse_ref), (dl_nat_ref, dl_ref)):
            for r in range(d):
                rows = pl.ds(r, l, stride=d) if d > 1 else slice(None)
                dst_ref[r * l:(r + 1) * l, :] = nat_ref.at[0][rows, :]

        def block(i, carry):
            rows, keys, mask = _attn_window(i, d)
            q, do = q_ref[0, rows, :], do_ref[0, rows, :]
            k, v = k_ref[0, keys, :], v_ref[0, keys, :]
            lse_b, dl_b = lse_ref[rows, :], dl_ref[rows, :]
            dq, dk, dv = [], None, None
            for hh, hm in enumerate(heads):
                qh, doh = jnp.where(hm, q, 0), jnp.where(hm, do, 0)
                s = jnp.where(mask, _dot_nt(qh, k), NEG)
                p = jnp.exp(s - _get_head(lse_b, 2 * g + hh))
                ds = (p * (_dot_nt(doh, v) - _get_head(dl_b, 2 * g + hh))).astype(BF16)
                dq.append(_dot(ds, k) * QK_SCALE)
                dk_h, dv_h = _dot_tn(qh, ds), _dot_tn(doh, p.astype(BF16))
                dk = dk_h if dk is None else dk + dk_h
                dv = dv_h if dv is None else dv + dv_h
            dq_ref[0, rows, :] = jnp.where(heads[0], dq[0], dq[1]).astype(BF16)
            dkt_ref[:, keys] += dk
            dvt_ref[:, keys] += dv
            return carry

        lax.fori_loop(0, Q_BLOCKS, block, 0, unroll=ATTN_UNROLL)
        for c in range(SEQ // ROW_TILE):
            rows = slice(c * ROW_TILE, (c + 1) * ROW_TILE)
            dk_ref[0, rows, :] = dkt_ref[:, rows].T.astype(BF16)
            dv_ref[0, rows, :] = dvt_ref[:, rows].T.astype(BF16)

    act = _flat_spec(lambda g: g)
    outs = pl.pallas_call(
        body, name=name, grid=(n_seq, LANE_GROUPS),
        in_specs=[_flat_spec(lambda g: g), _flat_spec(lambda g: LANE_GROUPS + g),
                  _flat_spec(lambda g: 2 * LANE_GROUPS + g), act, PER_HEAD_SPEC, PER_HEAD_SPEC],
        out_specs=[act] * 3,
        out_shape=[jax.ShapeDtypeStruct((n_seq, SEQ, D_MODEL), BF16)] * 3,
        scratch_shapes=[pltpu.VMEM((SEQ, LANES), F32)] * 2 + [pltpu.VMEM((LANES, SEQ), F32)] * 2,
        compiler_params=_params(("parallel", "parallel")),
    )(flat(qkv), flat(qkv), flat(qkv), flat(dout), lse, delta)
    return [o.reshape(n_seq, d, l, D_MODEL) for o in outs]


def _attn_combine_bwd(grads, rope, name):
    n_seq = grads[0][0].shape[0]
    t = n_seq * SEQ
    tm = MERGE_TILE

    def body(*refs):
        c_ref, s_ref, o_ref, nat4_ref, nat16_ref = refs[9:]
        for sec in range(3):
            _load_dilated(refs[3 + sec], 4, nat4_ref)
            _load_dilated(refs[6 + sec], 16, nat16_ref)
            for p in range(LANE_GROUPS):
                blk = refs[sec][0, 0, :, p * LANES:(p + 1) * LANES] + nat4_ref[p] + nat16_ref[p]
                if sec < 2:
                    blk = blk * c_ref[...] - _swap_halves(blk) * s_ref[...]
                o_ref[:, sec * D_MODEL + p * LANES:sec * D_MODEL + (p + 1) * LANES] = blk.astype(BF16)

    tab = pl.BlockSpec((tm, LANES), lambda i: (i, 0))
    flat = [a for br in grads for a in br]
    in_specs = []
    for spec in _dilated_specs(tm, D_MODEL, lambda: 0):
        in_specs += [spec] * 3
    return pl.pallas_call(
        body, name=name, grid=(t // tm,), in_specs=in_specs + [tab, tab],
        out_specs=pl.BlockSpec((tm, ODD_IN), lambda i: (i, 0)),
        out_shape=jax.ShapeDtypeStruct((t, ODD_IN), BF16),
        scratch_shapes=[pltpu.VMEM((LANE_GROUPS, tm, LANES), F32)] * 2,
        compiler_params=_params(("parallel",)),
    )(*flat, *rope)


def _adamw(w, g, m, v):
    m = ADAM_B1 * m + (1.0 - ADAM_B1) * g
    v = ADAM_B2 * v + (1.0 - ADAM_B2) * jnp.square(g)
    m_hat = m / (1.0 - ADAM_B1 ** ADAM_STEP)
    v_hat = v / (1.0 - ADAM_B2 ** ADAM_STEP)
    delta = -ADAM_LR * (m_hat / (jnp.sqrt(v_hat) + ADAM_EPS) + ADAM_WD * w)
    return delta, m, v


def _adamw_sharded(parts, w, m, v, after, name):
    n_layers, rows, cols = w.shape
    tr = min(rows, 256)

    def body(*refs):
        p_refs = refs[:n_layers]
        w_ref, m_ref, v_ref, _, g_ref, d_ref, mo_ref, vo_ref = refs[n_layers:]
        layer = pl.program_id(0)
        g = None
        for l, p_ref in enumerate(p_refs):
            g_l = p_ref[0].astype(F32)
            for s in range(1, N_DEV):
                g_l = g_l + p_ref[s].astype(F32)
            g = g_l if g is None else jnp.where(layer == l, g_l, g)
        delta, mn, vn = _adamw(w_ref[0], g, m_ref[0], v_ref[0])
        g_ref[0] = g
        d_ref[0] = delta
        mo_ref[0] = mn
        vo_ref[0] = vn

    def part_spec(l):
        return pl.BlockSpec((N_DEV, tr, cols), lambda a, i: (0, jnp.where(a == l, i, 0), 0))

    row = pl.BlockSpec((1, tr, cols), lambda a, i: (a, i, 0))
    return pl.pallas_call(
        body, name=name, grid=(n_layers, rows // tr),
        in_specs=[part_spec(l) for l in range(n_layers)] + [row, row, row, pl.BlockSpec(memory_space=pl.ANY)],
        out_specs=[row] * 4, out_shape=[jax.ShapeDtypeStruct(w.shape, F32)] * 4,
        compiler_params=_params(("arbitrary", "arbitrary")),
    )(*parts, w, m, v, after)


def _small_update(gathered, where, weights, moments_m, moments_v, lb_index, name):
    n = len(weights)
    n_g = len(gathered)

    def body(*refs):
        g_refs = refs[:n_g]
        w_refs, m_refs, v_refs = refs[n_g:n_g + n], refs[n_g + n:n_g + 2 * n], refs[n_g + 2 * n:n_g + 3 * n]
        outs = refs[n_g + 3 * n:]

        def total(k):
            array, rows, lanes = where[k]
            ref = g_refs[array]
            index = (slice(None),) * (len(ref.shape) - 1) if rows is None else (rows, lanes)
            acc = ref[(0,) + index]
            for s in range(1, N_DEV):
                acc = acc + ref[(s,) + index]
            return acc

        loss_rows = total(n)
        outs[0][...] = jnp.sum(jnp.sum(loss_rows, axis=1, keepdims=True), axis=0, keepdims=True)
        for k in range(n):
            part = total(k)
            if k == lb_index:
                dlb = jnp.sum(part, axis=0, keepdims=True)
                tab = w_refs[k][...]
                e = jnp.exp(tab - jnp.max(tab, axis=0, keepdims=True))
                p = e / jnp.sum(e, axis=0, keepdims=True)
                first = lax.broadcasted_iota(jnp.int32, p.shape, 0) == 0
                grads = [(slice(None), p * (jnp.where(first, dlb, 0.0) - p[0:1, :] * dlb))]
            elif part.shape == w_refs[k].shape:
                grads = [(slice(None), part)]
            else:
                grads = [(slice(l, l + 1), jnp.sum(part[l * SUBLANES:(l + 1) * SUBLANES], axis=0, keepdims=True))
                         for l in range(w_refs[k].shape[0])]
            for rows, g in grads:
                delta, mn, vn = _adamw(w_refs[k][rows], g, m_refs[k][rows], v_refs[k][rows])
                outs[1 + 4 * k][rows] = g
                outs[2 + 4 * k][rows] = delta
                outs[3 + 4 * k][rows] = mn
                outs[4 + 4 * k][rows] = vn

    vmem = pl.BlockSpec(memory_space=pltpu.VMEM)
    out_shape = [jax.ShapeDtypeStruct((1, 1), F32)]
    for w in weights:
        out_shape += [jax.ShapeDtypeStruct(w.shape, F32)] * 4
    args = list(gathered) + list(weights) + list(moments_m) + list(moments_v)
    return pl.pallas_call(
        body, name=name, in_specs=[vmem] * len(args), out_specs=[vmem] * len(out_shape), out_shape=out_shape,
        compiler_params=pltpu.CompilerParams(vmem_limit_bytes=VMEM_LIMIT),
    )(*args)


def kernel(x, positions, norm_mix_pre, norm_mix_post, norm_ffn_pre, norm_ffn_post, w_in_even, lb_table, a_norm, b_ln_g, b_ln_b, b_ws, b_bias, w_out_even, w_in_odd, w_out_odd, w_ff1, w_ff2, loss_target, m_norm_mix_pre, m_norm_mix_post, m_norm_ffn_pre, m_norm_ffn_post, m_w_in_even, m_lb_table, m_a_norm, m_b_ln_g, m_b_ln_b, m_b_ws, m_b_bias, m_w_out_even, m_w_in_odd, m_w_out_odd, m_w_ff1, m_w_ff2, v_norm_mix_pre, v_norm_mix_post, v_norm_ffn_pre, v_norm_ffn_post, v_w_in_even, v_lb_table, v_a_norm, v_b_ln_g, v_b_ln_b, v_b_ws, v_b_bias, v_w_out_even, v_w_in_odd, v_w_out_odd, v_w_ff1, v_w_ff2):
    n_seq = x.shape[0]
    t = n_seq * SEQ
    x0 = x.reshape(t, D_MODEL)
    target = loss_target.reshape(t, D_MODEL)

    me = _my_slot().astype(jnp.int32).reshape(1)

    order = ["in_e", "out_e", "ff1_0", "ff2_0", "in_o", "out_o", "ff1_1", "ff2_1"]
    shards = dict(in_e=w_in_even[0], out_e=w_out_even[0], in_o=w_in_odd[0], out_o=w_out_odd[0],
                  ff1_0=w_ff1[0], ff1_1=w_ff1[1], ff2_0=w_ff2[0], ff2_1=w_ff2[1])
    by_columns = ("in_e", "in_o", "ff1_0", "ff1_1")

    def place(k, after):
        if k in by_columns:
            return _place_own_columns(shards[k], me, "place_" + k, after)
        return _place_own(shards[k], me, "place_" + k, False, after=after)

    gathers = {}
    send0, recv0, land0, _, token0 = _exchange_start([place(order[0], None)], [None], "gather_start_first")
    gathers[order[0]] = (land0[0], send0[0], recv0[0])
    sends, recvs, lands, _, g_token = _exchange_start([place(k, token0) for k in order[1:]],
                                                      [None] * (len(order) - 1), "gather_start")
    for k, land, send, recv in zip(order[1:], lands, sends, recvs):
        gathers[k] = (land, send, recv)

    def get_w(keys, after):
        lands_k, sends_k, recvs_k = zip(*[gathers[k] for k in keys])
        return _exchange_wait(list(lands_k), [None] * len(keys), list(sends_k), list(recvs_k), after,
                              "gather_wait_" + keys[0])

    sent = {}

    def put_g(group, blocks):
        keys = list(blocks)
        own = [_place_own(blocks[k], me, "own_" + k, True) for k in keys]
        send_sems, recv_sems, own, srcs, token = _exchange_start(own, [blocks[k] for k in keys], "scatter_start_" + group)
        sent[group] = (keys, own, srcs, send_sems, recv_sems)
        return token

    rope = _rope_tables(positions)
    bias_t = b_bias[0].T
    grads = _local_step(x0, target, rope, norm_mix_pre, norm_mix_post, norm_ffn_pre, norm_ffn_post, lb_table,
                        a_norm, b_ln_g, b_ln_b, b_ws[0], bias_t, get_w, put_g, g_token)
    (dx0, loss_part, dg_mix_pre, dg_mix_post, dg_ffn_pre, dg_ffn_post, d_lb, d_a_norm, d_ln_g, d_ln_b, d_ws,
     d_bias_t) = grads

    packed = jnp.concatenate([dg_mix_pre, dg_mix_post, dg_ffn_pre, dg_ffn_post,
                              jnp.concatenate([d_lb, d_a_norm], axis=1), jnp.concatenate([d_ln_g, d_ln_b], axis=1),
                              loss_part], axis=0)
    small_lands = [_place_own(a, me, "own_small%d" % k, False, F32) for k, a in enumerate((packed, d_ws, d_bias_t))]
    s_send, s_recv, small_lands, _, after = _exchange_start(small_lands, [None] * 3, "gather_small_start")

    big = dict(w_in_even=(["in_e"], w_in_even, m_w_in_even, v_w_in_even),
               w_out_even=(["out_e"], w_out_even, m_w_out_even, v_w_out_even),
               w_in_odd=(["in_o"], w_in_odd, m_w_in_odd, v_w_in_odd),
               w_out_odd=(["out_o"], w_out_odd, m_w_out_odd, v_w_out_odd),
               w_ff1=(["ff1_0", "ff1_1"], w_ff1, m_w_ff1, v_w_ff1), w_ff2=(["ff2_0", "ff2_1"], w_ff2, m_w_ff2, v_w_ff2))
    recv, big_out = {}, {}
    for groups, names in ((("ffn1", "ffn0"), ("w_ff1", "w_ff2")), (("mix1",), ("w_in_odd", "w_out_odd")),
                          (("mix0",), ("w_in_even", "w_out_even"))):
        for group in groups:
            keys, own, srcs, send_sems, recv_sems = sent[group]
            recv.update(zip(keys, _exchange_wait(own, srcs, send_sems, recv_sems, after, "scatter_wait_" + group)))
        for nm in names:
            keys, w, m, v = big[nm]
            big_out[nm] = _adamw_sharded([recv[k] for k in keys], w, m, v, after, "adamw_" + nm)
            after = big_out[nm][0]
    big_out = [big_out[nm] for nm in ("w_in_even", "w_out_even", "w_in_odd", "w_out_odd", "w_ff1", "w_ff2")]
    gathered = _exchange_wait(small_lands, [None] * 3, s_send, s_recv, after, "gather_small_wait")
    rows8 = lambda k: slice(SUBLANES * k, SUBLANES * (k + 1))
    left, right, every = slice(0, A_WIDTH), slice(A_WIDTH, 2 * A_WIDTH), slice(None)
    where = [(0, slice(0, 16), every), (0, slice(16, 32), every), (0, slice(32, 48), every), (0, slice(48, 64), every),
             (0, rows8(8), left), (0, rows8(8), right), (0, rows8(9), left), (0, rows8(9), right),
             (1, None, None), (2, None, None), (0, rows8(10), every)]
    small_w = [norm_mix_pre, norm_mix_post, norm_ffn_pre, norm_ffn_post, lb_table, a_norm, b_ln_g, b_ln_b,
               b_ws[0], bias_t]
    small_m = [m_norm_mix_pre, m_norm_mix_post, m_norm_ffn_pre, m_norm_ffn_post, m_lb_table, m_a_norm, m_b_ln_g,
               m_b_ln_b, m_b_ws[0], m_b_bias[0].T]
    small_v = [v_norm_mix_pre, v_norm_mix_post, v_norm_ffn_pre, v_norm_ffn_post, v_lb_table, v_a_norm, v_b_ln_g,
               v_b_ln_b, v_b_ws[0], v_b_bias[0].T]
    small_out = _small_update(gathered, where, small_w, small_m, small_v, 4, "small_update")
    loss = small_out[0].reshape(())
    small = [small_out[1 + 4 * k:5 + 4 * k] for k in range(len(small_w))]
    small[8] = [a[None] for a in small[8]]
    small[9] = [a.T[None] for a in small[9]]

    per_weight = small[0:4] + [big_out[0]] + small[4:10] + big_out[1:6]
    grad_x = dx0.reshape(x.shape)
    out = [loss, grad_x]
    for kind in range(4):
        out += [p[kind] for p in per_weight]
    return tuple(out)


def _local_step(x0, target, rope, norm_mix_pre, norm_mix_post, norm_ffn_pre, norm_ffn_post, lb_table, a_norm,
                b_ln_g, b_ln_b, ws, bias_t, get_w, put_g, token):
    def gain(a, l, tok):
        return a[l:l + 1] if tok is None else a[l:l + 1] + tok[0:1, 0:1]

    full = lambda a: a.reshape(-1, D_MODEL)
    owners = lambda a: a.reshape((N_DEV, -1) + a.shape[1:])

    (g_in_e,) = get_w(["in_e"], token)
    proj, h_mix0 = _norm_inproj(x0, gain(norm_mix_pre, 0, token), g_in_e, "inproj_even")
    mixed, pre_a, states = _hgrn2_fwd(proj, lb_table, a_norm, "hgrn2_fwd")
    mixed = _gmlp_fwd(proj, mixed, b_ln_g, b_ln_b, ws, bias_t, "gmlp_fwd")
    w_out_e = full(get_w(["out_e"], mixed)[0])
    x1, mix0 = _outproj([mixed], w_out_e, x0, gain(norm_mix_post, 0, None), "outproj_even")
    w1_0, w2_0 = get_w(["ff1_0", "ff2_0"], x1)
    w2_0 = full(w2_0)
    x2, y0, h_ffn0, r0 = _ffn_fwd(x1, gain(norm_ffn_pre, 0, None), w1_0, w2_0, gain(norm_ffn_post, 0, None), "ffn_fwd_0")
    (g_in_o,) = get_w(["in_o"], x2)
    *qkv, h_mix1 = _norm_inproj_rope(x2, gain(norm_mix_pre, 1, None), g_in_o, rope, "inproj_odd")
    branches = [_attn_branch_fwd(a, "attn_fwd_d%d" % d) for a, d in zip(qkv, C_DILATIONS)]
    attn, attn_b, lse = _attn_merge(branches, "attn_merge")
    w_out_o = full(get_w(["out_o"], attn_b)[0])
    x3, mix1 = _outproj([attn_b], w_out_o, x2, gain(norm_mix_post, 1, None), "outproj_odd")
    w1_1, w2_1 = get_w(["ff1_1", "ff2_1"], x3)
    w2_1 = full(w2_1)
    dx4, y1, h_ffn1, r1, loss_part = _ffn_fwd(x3, gain(norm_ffn_pre, 1, None), w1_1, w2_1, gain(norm_ffn_post, 1, None),
                                              "ffn_fwd_1", target)

    dx3, dy1, da1, dg_ffn_pre1, dg_ffn_post1 = _ffn_bwd(
        dx4, x3, y1, r1, gain(norm_ffn_pre, 1, None), w1_1, w2_1, gain(norm_ffn_post, 1, None), "ffn_bwd_1")
    gw_ff1_1 = _grad_w(h_ffn1, da1, True, "grad_w_ff1_1")
    gw_ff2_1 = _grad_w(r1, dy1, False, "grad_w_ff2_1")
    tok = put_g("ffn1", dict(ff1_1=gw_ff1_1, ff2_1=owners(gw_ff2_1)))
    *dattn, delta, dz1, dg_mix_post1 = _outproj_bwd_attn(dx3, mix1, gain(norm_mix_post, 1, tok), w_out_o, attn,
                                                  "outproj_bwd_odd")
    gw_out_o = _grad_w(attn_b, dz1, False, "grad_w_out_odd")
    per_seq = lambda a: a.reshape(-1, SEQ, LANES)
    grads_c = [_attn_branch_bwd(qkv[b], dattn[b], per_seq(lse), per_seq(delta), "attn_bwd_d%d" % d)
               for b, d in enumerate(C_DILATIONS)]
    dqkv = _attn_combine_bwd(grads_c, rope, "attn_combine_bwd")
    gw_in_o = _grad_w(h_mix1, dqkv, True, "grad_w_in_odd")
    tok = put_g("mix1", dict(out_o=owners(gw_out_o), in_o=gw_in_o))
    dx2, dg_mix_pre1 = _inproj_bwd(dqkv, g_in_o, dx3, x2, gain(norm_mix_pre, 1, tok), "inproj_bwd_odd")

    dx1, dy0, da0, dg_ffn_pre0, dg_ffn_post0 = _ffn_bwd(
        dx2, x1, y0, r0, gain(norm_ffn_pre, 0, None), w1_0, w2_0, gain(norm_ffn_post, 0, None), "ffn_bwd_0")
    gw_ff1_0 = _grad_w(h_ffn0, da0, True, "grad_w_ff1_0")
    gw_ff2_0 = _grad_w(r0, dy0, False, "grad_w_ff2_0")
    tok = put_g("ffn0", dict(ff1_0=gw_ff1_0, ff2_0=owners(gw_ff2_0)))
    dcat, dz0, dg_mix_post0 = _outproj_bwd(dx1, mix0, gain(norm_mix_post, 0, tok), w_out_e, "outproj_bwd_even")
    gw_out_e = _grad_w(mixed, dz0, False, "grad_w_out_even")
    dproj, d_lb, d_a_norm = _hgrn2_bwd(proj, dcat, pre_a, states, lb_table, a_norm, "hgrn2_bwd")
    dproj, d_ln_g, d_ln_b, d_ws, d_bias_t = _gmlp_bwd(proj, dcat, dproj, b_ln_g, b_ln_b, ws, bias_t, "gmlp_bwd")
    gw_in_e = _grad_w(h_mix0, dproj, True, "grad_w_in_even")
    tok = put_g("mix0", dict(out_e=owners(gw_out_e), in_e=gw_in_e))
    dx0, dg_mix_pre0 = _inproj_bwd(dproj, g_in_e, dx1, x0, gain(norm_mix_pre, 0, tok), "inproj_bwd_even")

    layers = lambda a, b: jnp.concatenate([a, b], axis=0)
    return (dx0, loss_part, layers(dg_mix_pre0, dg_mix_pre1), layers(dg_mix_post0, dg_mix_post1),
            layers(dg_ffn_pre0, dg_ffn_pre1), layers(dg_ffn_post0, dg_ffn_post1),
            d_lb, d_a_norm, d_ln_g, d_ln_b, d_ws, d_bias_t)
```

```python
import math

import jax
import jax.numpy as jnp
from jax import lax
from jax.experimental import pallas as pl
from jax.experimental.pallas import tpu as pltpu

F32 = jnp.float32
BF16 = jnp.bfloat16
MESH = pl.DeviceIdType.MESH

N_DEV = 8
D_MODEL = 1024
SEQ = 2048
EPS = 1e-6
A_WIDTH = 512
A_HEADS = 4
HEAD_A = 128
B_WIDTH = 512
B_GROUPS = 4
B_CHUNK = 128
C_HEADS = 16
C_HEAD_DIM = 64
C_ROT_HALF = 8
ROPE_THETA = 500000.0
C_DILATIONS = (1, 4, 16)
C_BLOCK = 128
D_FF = 4096
EVEN_IN = 3072
ODD_IN = 3072

ADAM_LR = 0.001
ADAM_B1 = 0.9
ADAM_B2 = 0.999
ADAM_EPS = 1e-08
ADAM_WD = 0.01
ADAM_STEP = 10

LANES = 128
SUBLANES = 8
ROW_TILE = 512
PROJ_TILE = 1024
PROJ_COLS = 768
MERGE_TILE = 256
SUB_CHUNK = 16
HGRN_BLOCK = 256
NEG = -1e30
VMEM_LIMIT = 56 * 1024 * 1024


def _params(sem):
    return pltpu.CompilerParams(dimension_semantics=sem, vmem_limit_bytes=VMEM_LIMIT)


def _dot(a, b):
    return jnp.dot(a, b, preferred_element_type=F32)


def _dot_nt(a, b):
    return lax.dot_general(a, b, (((1,), (1,)), ((), ())), preferred_element_type=F32)


def _dot_tn(a, b):
    return lax.dot_general(a, b, (((0,), (0,)), ((), ())), preferred_element_type=F32)


def _rms(x, g):
    r = lax.rsqrt(jnp.mean(x * x, axis=-1, keepdims=True) + EPS)
    return x * r * g


def _rms_bwd(x, g, dy):
    r = lax.rsqrt(jnp.mean(x * x, axis=-1, keepdims=True) + EPS)
    dyg = dy * g
    dx = r * dyg - x * (r * r * r) * jnp.mean(x * dyg, axis=-1, keepdims=True)
    return dx, dy * x * r


def _split3(x):
    hi = x.astype(BF16)
    rest = x - hi.astype(F32)
    mid = rest.astype(BF16)
    return hi, mid, (rest - mid.astype(F32)).astype(BF16)


def _mask_dot(mask, x):
    m = mask.astype(BF16)
    hi, mid, lo = _split3(x)
    return _dot(m, hi) + (_dot(m, mid) + _dot(m, lo))


def _dot_mask(x, mask):
    m = mask.astype(BF16)
    hi, mid, lo = _split3(x)
    return _dot(hi, m) + (_dot(mid, m) + _dot(lo, m))


def _rows8(v):
    return v.reshape(v.shape[0] // SUBLANES, SUBLANES, v.shape[1]).sum(axis=0)


def _sigmoid(x):
    return 1.0 / (1.0 + jnp.exp(-x))


def _gelu(x):
    return 0.5 * x * (1.0 + jnp.tanh(math.sqrt(2.0 / math.pi) * (x + 0.044715 * (x * x * x))))


def _acc_rows8(ref, val, first):
    @pl.when(first)
    def _():
        ref[...] = val

    @pl.when(jnp.logical_not(first))
    def _():
        ref[...] += val


def _my_slot():
    return 4 * lax.axis_index("x") + 2 * lax.axis_index("y") + lax.axis_index("c")


def _peer(r):
    x, y, c = lax.axis_index("x"), lax.axis_index("y"), lax.axis_index("c")
    px = 1 - x if (r >> 2) & 1 else x
    py = 1 - y if (r >> 1) & 1 else y
    pc = 1 - c if r & 1 else c
    return (px, py, pc), 4 * px + 2 * py + pc


HBM_SPEC = pl.BlockSpec(memory_space=pltpu.HBM)
SEM_SPEC = pl.BlockSpec(memory_space=pltpu.SEMAPHORE)
SPLIT_EFFECT = pltpu.SideEffectType.DATAFLOW_SIDE_EFFECTING


def _split_copies(land_ref, src_ref, send_sem, recv_sem):
    me = _my_slot()
    copies = []
    for r in range(1, N_DEV):
        peer, slot = _peer(r)
        src = _slot(land_ref, me) if src_ref is None else _slot(src_ref, slot)
        copies.append(pltpu.make_async_remote_copy(
            src_ref=src, dst_ref=_slot(land_ref, me), send_sem=send_sem, recv_sem=recv_sem,
            device_id=peer, device_id_type=MESH))
    return copies


def _slot(ref, s):
    if len(ref.shape) == 2:
        c = ref.shape[1] // N_DEV
        return ref.at[:, pl.ds(pl.multiple_of(s * c, LANES), c)]
    return ref.at[s]


def _exchange_start(lands, sources, name):
    n = len(lands)
    given = [s for s in sources if s is not None]
    arrays = list(lands) + given

    def body(*refs):
        land_refs, src_refs = refs[:n], list(refs[n:n + len(given)])
        sems = refs[len(arrays):len(arrays) + 2 * n]
        token = refs[-1]
        for k in range(n):
            src_ref = None if sources[k] is None else src_refs.pop(0)
            for copy in _split_copies(land_refs[k], src_ref, sems[k], sems[n + k]):
                copy.start()
        token[...] = jnp.zeros_like(token)

    outs = pl.pallas_call(
        body, name=name,
        out_shape=(pltpu.SemaphoreType.DMA(()),) * (2 * n) + tuple(pltpu.HBM(a.shape, a.dtype) for a in arrays)
        + (jax.ShapeDtypeStruct((SUBLANES, LANES), F32),),
        in_specs=[HBM_SPEC] * len(arrays),
        out_specs=(SEM_SPEC,) * (2 * n) + (HBM_SPEC,) * len(arrays) + (pl.BlockSpec(memory_space=pltpu.VMEM),),
        input_output_aliases={i: 2 * n + i for i in range(len(arrays))},
        compiler_params=pltpu.CompilerParams(has_side_effects=SPLIT_EFFECT),
    )(*[pltpu.with_memory_space_constraint(a, pltpu.HBM) for a in arrays])
    return list(outs[:n]), list(outs[n:2 * n]), list(outs[2 * n:3 * n]), list(outs[3 * n:-1]), outs[-1]


def _exchange_wait(lands, sources, send_sems, recv_sems, after, name):
    n = len(lands)
    given = [s for s in sources if s is not None]
    arrays = list(lands) + given

    def body(*refs):
        land_refs, src_refs = refs[:n], list(refs[n:n + len(given)])
        sems = refs[len(arrays):len(arrays) + 2 * n]
        for i in range(n):
            src_ref = None if sources[i] is None else src_refs.pop(0)
            copies = _split_copies(land_refs[i], src_ref, sems[i], sems[n + i])
            for copy in copies:
                copy.wait_recv()
            for copy in copies:
                copy.wait_send()

    outs = pl.pallas_call(
        body, name=name, out_shape=tuple(pltpu.HBM(a.shape, a.dtype) for a in arrays),
        in_specs=[HBM_SPEC] * len(arrays) + [SEM_SPEC] * (2 * n) + [pl.BlockSpec(memory_space=pl.ANY)],
        out_specs=(HBM_SPEC,) * len(arrays),
        input_output_aliases={i: i for i in range(len(arrays))},
        compiler_params=pltpu.CompilerParams(has_side_effects=SPLIT_EFFECT),
    )(*arrays, *send_sems, *recv_sems, after)
    return list(outs[:n])


def _place_own(a, me, name, own_block, dtype=BF16, after=None):
    shape = a.shape[1:] if own_block else a.shape
    cols = shape[-1]
    a3 = a.reshape((N_DEV if own_block else 1, -1, cols))
    rows = a3.shape[1]
    tr = min(rows, 512)

    def body(me_ref, a_ref, _, o_ref):
        o_ref[...] = a_ref[...].astype(dtype)

    grid_spec = pltpu.PrefetchScalarGridSpec(
        num_scalar_prefetch=1, grid=(rows // tr,),
        in_specs=[pl.BlockSpec((1, tr, cols), lambda i, me_ref: (me_ref[0] if own_block else 0, i, 0)),
                  pl.BlockSpec(memory_space=pl.ANY)],
        out_specs=pl.BlockSpec((1, tr, cols), lambda i, me_ref: (me_ref[0], i, 0)))
    out = pl.pallas_call(
        body, name=name, grid_spec=grid_spec, out_shape=jax.ShapeDtypeStruct((N_DEV, rows, cols), dtype),
        compiler_params=_params(("arbitrary",)),
    )(me, a3, a3 if after is None else after)
    return out.reshape((N_DEV,) + shape)


def _place_own_columns(a, me, name, after=None):
    rows, cols = a.shape
    tr = min(rows, 512)

    def body(me_ref, a_ref, _, o_ref):
        o_ref[...] = a_ref[...].astype(BF16)

    grid_spec = pltpu.PrefetchScalarGridSpec(
        num_scalar_prefetch=1, grid=(rows // tr,),
        in_specs=[pl.BlockSpec((tr, cols), lambda i, me_ref: (i, 0)), pl.BlockSpec(memory_space=pl.ANY)],
        out_specs=pl.BlockSpec((tr, cols), lambda i, me_ref: (i, me_ref[0])))
    return pl.pallas_call(
        body, name=name, grid_spec=grid_spec, out_shape=jax.ShapeDtypeStruct((rows, N_DEV * cols), BF16),
        compiler_params=_params(("arbitrary",)),
    )(me, a, a if after is None else after)


def _rope_tables(positions):
    in_head = jnp.arange(LANES) % C_HEAD_DIM
    inv = ROPE_THETA ** (-(in_head % C_ROT_HALF).astype(F32) / C_ROT_HALF)
    ang = positions.reshape(-1)[:, None].astype(F32) * inv
    rotated = in_head < 2 * C_ROT_HALF
    sin = jnp.sin(ang)
    return (jnp.where(rotated, jnp.cos(ang), 1.0),
            jnp.where(in_head < C_ROT_HALF, -sin, jnp.where(rotated, sin, 0.0)))


def _swap_halves(x):
    lane = lax.broadcasted_iota(jnp.int32, x.shape, 1) % C_HEAD_DIM
    return jnp.where(lane < C_ROT_HALF, pltpu.roll(x, LANES - C_ROT_HALF, 1), pltpu.roll(x, C_ROT_HALF, 1))


def _norm_inproj(x, g, w, name):
    t = x.shape[0]
    n = w.shape[1]
    tm, tn = PROJ_TILE, PROJ_COLS

    def body(x_ref, g_ref, w_ref, o_ref, h_ref):
        @pl.when(pl.program_id(1) == 0)
        def _():
            h_ref[...] = _rms(x_ref[...], g_ref[...]).astype(BF16)

        o_ref[...] = _dot(h_ref[...], w_ref[...])

    return pl.pallas_call(
        body, name=name, grid=(t // tm, n // tn),
        in_specs=[pl.BlockSpec((tm, D_MODEL), lambda i, j: (i, 0)), pl.BlockSpec((1, D_MODEL), lambda i, j: (0, 0)),
                  pl.BlockSpec((D_MODEL, tn), lambda i, j: (0, j))],
        out_specs=[pl.BlockSpec((tm, tn), lambda i, j: (i, j)), pl.BlockSpec((tm, D_MODEL), lambda i, j: (i, 0))],
        out_shape=[jax.ShapeDtypeStruct((t, n), F32), jax.ShapeDtypeStruct((t, D_MODEL), BF16)],
        compiler_params=_params(("parallel", "arbitrary")),
    )(x, g, w)


def _dilated_specs(tm, width, col_of):
    per_seq = SEQ // tm
    specs = []
    for d in C_DILATIONS:
        specs.append(pl.BlockSpec(
            (1, d, tm // d, width), lambda i, *rest: (i // per_seq, 0, i % per_seq, col_of(*rest))))
    return specs


def _dilated_shapes(n_seq, cols, dtype):
    return [jax.ShapeDtypeStruct((n_seq, d, SEQ // d, cols), dtype) for d in C_DILATIONS]


def _store_dilated(src_ref, out_refs, dtype):
    groups, tm, _ = src_ref.shape
    for d, o_ref in zip(C_DILATIONS, out_refs):
        for r in range(d):
            rows = pl.ds(r, tm // d, stride=d) if d > 1 else slice(None)
            for p in range(groups):
                o_ref[0, r, :, p * LANES:(p + 1) * LANES] = src_ref.at[p][rows, :].astype(dtype)


def _load_dilated(in_ref, d, dst_ref):
    groups, tm, _ = dst_ref.shape
    for r in range(d):
        rows = pl.ds(r, tm // d, stride=d)
        for p in range(groups):
            dst_ref.at[p][rows, :] = in_ref[0, r, :, p * LANES:(p + 1) * LANES].astype(F32)


def _norm_inproj_rope(x, g, w, rope, name):
    t = x.shape[0]
    n = w.shape[1]
    tm, nb = PROJ_TILE, PROJ_COLS

    def body(x_ref, g_ref, w_ref, c_ref, s_ref, o1_ref, o4_ref, o16_ref, h_ref, tile_ref):
        j = pl.program_id(1)

        @pl.when(j == 0)
        def _():
            h_ref[...] = _rms(x_ref[...], g_ref[...]).astype(BF16)

        acc = _dot(h_ref[...], w_ref[...])
        for p in range(nb // LANES):
            blk = acc[:, p * LANES:(p + 1) * LANES]
            roped = blk * c_ref[...] + _swap_halves(blk) * s_ref[...]
            piece = j * (nb // LANES) + p
            is_qk = piece < 2 * (D_MODEL // LANES)
            tile_ref[p] = jnp.where(is_qk, roped, blk) * jnp.where(piece < D_MODEL // LANES, QK_SCALE, 1.0)
        _store_dilated(tile_ref, (o1_ref, o4_ref, o16_ref), BF16)

    return pl.pallas_call(
        body, name=name, grid=(t // tm, n // nb),
        in_specs=[pl.BlockSpec((tm, D_MODEL), lambda i, j: (i, 0)), pl.BlockSpec((1, D_MODEL), lambda i, j: (0, 0)),
                  pl.BlockSpec((D_MODEL, nb), lambda i, j: (0, j)),
                  pl.BlockSpec((tm, LANES), lambda i, j: (i, 0)), pl.BlockSpec((tm, LANES), lambda i, j: (i, 0))],
        out_specs=_dilated_specs(tm, nb, lambda j: j) + [pl.BlockSpec((tm, D_MODEL), lambda i, j: (i, 0))],
        out_shape=_dilated_shapes(t // SEQ, n, BF16) + [jax.ShapeDtypeStruct((t, D_MODEL), BF16)],
        scratch_shapes=[pltpu.VMEM((nb // LANES, tm, LANES), F32)],
        compiler_params=_params(("parallel", "arbitrary")),
    )(x, g, w, *rope)


def _outproj(parts, w, x, g, name):
    t = x.shape[0]
    tm = PROJ_TILE
    n = len(parts)
    widths = [p.shape[1] for p in parts]

    def body(*refs):
        p_refs = refs[:n]
        w_ref, x_ref, g_ref, xo_ref, mix_ref = refs[n:]
        mix = None
        off = 0
        for p_ref, wd in zip(p_refs, widths):
            term = _dot(p_ref[...].astype(BF16), w_ref[off:off + wd, :])
            mix = term if mix is None else mix + term
            off += wd
        mix_ref[...] = mix
        xo_ref[...] = x_ref[...] + _rms(mix, g_ref[...])

    row = lambda i: (i, 0)
    return pl.pallas_call(
        body, name=name, grid=(t // tm,),
        in_specs=[pl.BlockSpec((tm, wd), row) for wd in widths] + [
            pl.BlockSpec((sum(widths), D_MODEL), lambda i: (0, 0)),
            pl.BlockSpec((tm, D_MODEL), row), pl.BlockSpec((1, D_MODEL), lambda i: (0, 0))],
        out_specs=[pl.BlockSpec((tm, D_MODEL), row)] * 2,
        out_shape=[jax.ShapeDtypeStruct((t, D_MODEL), F32)] * 2,
        compiler_params=_params(("parallel",)),
    )(*parts, w, x, g)


def _outproj_bwd(dx, mix, g, w, name):
    t = dx.shape[0]
    tm = PROJ_TILE
    k = w.shape[0]

    def body(dx_ref, mix_ref, g_ref, w_ref, dcat_ref, dz_ref, dg_ref):
        dz, dgr = _rms_bwd(mix_ref[...], g_ref[...], dx_ref[...])
        dzb = dz.astype(BF16)
        dz_ref[...] = dzb
        dcat_ref[...] = _dot_nt(dzb, w_ref[...])
        _acc_rows8(dg_ref, _rows8(dgr), pl.program_id(0) == 0)

    row = lambda i: (i, 0)
    return pl.pallas_call(
        body, name=name, grid=(t // tm,),
        in_specs=[pl.BlockSpec((tm, D_MODEL), row), pl.BlockSpec((tm, D_MODEL), row),
                  pl.BlockSpec((1, D_MODEL), lambda i: (0, 0)), pl.BlockSpec((k, D_MODEL), lambda i: (0, 0))],
        out_specs=[pl.BlockSpec((tm, k), row), pl.BlockSpec((tm, D_MODEL), row),
                   pl.BlockSpec((SUBLANES, D_MODEL), lambda i: (0, 0))],
        out_shape=[jax.ShapeDtypeStruct((t, k), F32), jax.ShapeDtypeStruct((t, D_MODEL), BF16),
                   jax.ShapeDtypeStruct((SUBLANES, D_MODEL), F32)],
        compiler_params=_params(("arbitrary",)),
    )(dx, mix, g, w)


def _outproj_bwd_attn(dx, mix, g, w, out, name):
    t = dx.shape[0]
    tm = MERGE_TILE

    def body(dx_ref, mix_ref, g_ref, w_ref, out_ref, do1, do4, do16, dl_ref, dz_ref, dg_ref, tile_ref):
        dz, dgr = _rms_bwd(mix_ref[...], g_ref[...], dx_ref[...])
        dzb = dz.astype(BF16)
        dz_ref[...] = dzb
        _acc_rows8(dg_ref, _rows8(dgr), pl.program_id(0) == 0)
        dout = _dot_nt(dzb, w_ref[...])
        for p in range(LANE_GROUPS):
            tile_ref[p] = dout[:, p * LANES:(p + 1) * LANES]
        _store_dilated(tile_ref, (do1, do4, do16), BF16)
        column = lax.broadcasted_iota(jnp.int32, (D_MODEL, LANES), 0) // C_HEAD_DIM
        head = lax.broadcasted_iota(jnp.int32, (D_MODEL, LANES), 1)
        dl_ref[...] = _dot_mask(dout * out_ref[...], column == head)

    row = lambda i: (i, 0)
    n_seq = t // SEQ
    return pl.pallas_call(
        body, name=name, grid=(t // tm,),
        in_specs=[pl.BlockSpec((tm, D_MODEL), row), pl.BlockSpec((tm, D_MODEL), row),
                  pl.BlockSpec((1, D_MODEL), lambda i: (0, 0)), pl.BlockSpec((D_MODEL, D_MODEL), lambda i: (0, 0)),
                  pl.BlockSpec((tm, D_MODEL), row)],
        out_specs=_dilated_specs(tm, D_MODEL, lambda: 0) + [
            pl.BlockSpec((tm, LANES), row), pl.BlockSpec((tm, D_MODEL), row),
            pl.BlockSpec((SUBLANES, D_MODEL), lambda i: (0, 0))],
        out_shape=_dilated_shapes(n_seq, D_MODEL, BF16) + [
            jax.ShapeDtypeStruct((t, LANES), F32), jax.ShapeDtypeStruct((t, D_MODEL), BF16),
            jax.ShapeDtypeStruct((SUBLANES, D_MODEL), F32)],
        scratch_shapes=[pltpu.VMEM((LANE_GROUPS, tm, LANES), F32)],
        compiler_params=_params(("arbitrary",)),
    )(dx, mix, g, w, out)


def _inproj_bwd(dproj, w, dx, x, g, name):
    t = x.shape[0]
    n = w.shape[1]
    tm = ROW_TILE

    def body(dp_ref, w_ref, dx_ref, x_ref, g_ref, o_ref, dg_ref):
        dxn, dgr = _rms_bwd(x_ref[...], g_ref[...], _dot_nt(dp_ref[...], w_ref[...]))
        o_ref[...] = dx_ref[...] + dxn
        _acc_rows8(dg_ref, _rows8(dgr), pl.program_id(0) == 0)

    row = lambda i: (i, 0)
    return pl.pallas_call(
        body, name=name, grid=(t // tm,),
        in_specs=[pl.BlockSpec((tm, n), row), pl.BlockSpec((D_MODEL, n), lambda i: (0, 0)),
                  pl.BlockSpec((tm, D_MODEL), row), pl.BlockSpec((tm, D_MODEL), row),
                  pl.BlockSpec((1, D_MODEL), lambda i: (0, 0))],
        out_specs=[pl.BlockSpec((tm, D_MODEL), row), pl.BlockSpec((SUBLANES, D_MODEL), lambda i: (0, 0))],
        out_shape=[jax.ShapeDtypeStruct((t, D_MODEL), F32), jax.ShapeDtypeStruct((SUBLANES, D_MODEL), F32)],
        compiler_params=_params(("arbitrary",)),
    )(dproj, w, dx, x, g)


def _grad_w(a, b, col_blocks, name, after=None):
    t, k = a.shape
    n = b.shape[1]
    tk = min(k, 1024)
    per_owner = n // N_DEV
    tn = 2 * per_owner if col_blocks else min(n, 1024)

    def body(a_ref, b_ref, _, o_ref, at_ref):
        @pl.when(pl.program_id(1) == 0)
        def _():
            for c in range(t // ROW_TILE):
                rows = slice(c * ROW_TILE, (c + 1) * ROW_TILE)
                at_ref[:, rows] = a_ref[rows, :].T

        res = _dot(at_ref[...], b_ref[...]).astype(BF16)
        if col_blocks:
            o_ref[0] = res[:, :per_owner]
            o_ref[1] = res[:, per_owner:]
        else:
            o_ref[...] = res

    if col_blocks:
        out_spec = pl.BlockSpec((2, tk, per_owner), lambda i, j: (j, i, 0))
        out_shape = jax.ShapeDtypeStruct((N_DEV, k, per_owner), BF16)
    else:
        out_spec = pl.BlockSpec((tk, tn), lambda i, j: (i, j))
        out_shape = jax.ShapeDtypeStruct((k, n), BF16)
    return pl.pallas_call(
        body, name=name, grid=(k // tk, n // tn),
        in_specs=[pl.BlockSpec((t, tk), lambda i, j: (0, i)), pl.BlockSpec((t, tn), lambda i, j: (0, j)),
                  pl.BlockSpec(memory_space=pl.ANY)],
        out_specs=out_spec, out_shape=out_shape,
        scratch_shapes=[pltpu.VMEM((tk, t), BF16)],
        compiler_params=_params(("parallel", "arbitrary")),
    )(a, b, a if after is None else after)


FF_STEP = 1024
FF_STEPS = D_FF // FF_STEP


def _ffn_fwd(x, g_pre, w1, w2, g_post, name, target=None):
    t = x.shape[0]
    tm = PROJ_TILE

    def body(*refs):
        if target is None:
            x_ref, gp_ref, w1_ref, w2_ref, gq_ref, xo_ref, y_ref, h_ref, r_ref = refs
        else:
            x_ref, gp_ref, w1_ref, w2_ref, gq_ref, t_ref, xo_ref, y_ref, h_ref, r_ref, l_ref = refs
        i, j = pl.program_id(0), pl.program_id(1)

        @pl.when(j == 0)
        def _():
            h_ref[...] = _rms(x_ref[...], gp_ref[...]).astype(BF16)

        a = _dot(h_ref[...], w1_ref[...])
        r = jnp.square(jnp.maximum(a, 0.0)).astype(BF16)
        r_ref[...] = r
        term = _dot(r, w2_ref[...])

        @pl.when(j == 0)
        def _():
            y_ref[...] = term

        @pl.when(j > 0)
        def _():
            y_ref[...] += term

        @pl.when(j == FF_STEPS - 1)
        def _():
            x_new = x_ref[...] + _rms(y_ref[...], gq_ref[...])
            if target is None:
                xo_ref[...] = x_new
            else:
                diff = x_new - t_ref[...]
                xo_ref[...] = diff * (1.0 / D_MODEL)
                _acc_rows8(l_ref, _rows8(diff * diff) * (0.5 / D_MODEL), i == 0)

    row = lambda i, j: (i, 0)
    vec = pl.BlockSpec((1, D_MODEL), lambda i, j: (0, 0))
    in_specs = [pl.BlockSpec((tm, D_MODEL), row), vec, pl.BlockSpec((D_MODEL, FF_STEP), lambda i, j: (0, j)),
                pl.BlockSpec((FF_STEP, D_MODEL), lambda i, j: (j, 0)), vec]
    out_specs = [pl.BlockSpec((tm, D_MODEL), row)] * 3 + [pl.BlockSpec((tm, FF_STEP), lambda i, j: (i, j))]
    out_shape = [jax.ShapeDtypeStruct((t, D_MODEL), F32), jax.ShapeDtypeStruct((t, D_MODEL), F32),
                 jax.ShapeDtypeStruct((t, D_MODEL), BF16), jax.ShapeDtypeStruct((t, D_FF), BF16)]
    args = [x, g_pre, w1, w2, g_post]
    if target is not None:
        in_specs.append(pl.BlockSpec((tm, D_MODEL), row))
        out_specs.append(pl.BlockSpec((SUBLANES, D_MODEL), lambda i, j: (0, 0)))
        out_shape.append(jax.ShapeDtypeStruct((SUBLANES, D_MODEL), F32))
        args.append(target)
    return pl.pallas_call(
        body, name=name, grid=(t // tm, FF_STEPS), in_specs=in_specs, out_specs=out_specs, out_shape=out_shape,
        compiler_params=_params(("parallel" if target is None else "arbitrary", "arbitrary")),
    )(*args)


def _ffn_bwd(dxo, x, y, r, g_pre, w1, w2, g_post, name):
    t = x.shape[0]
    tm = ROW_TILE

    def body(dxo_ref, x_ref, y_ref, r_ref, gp_ref, w1_ref, w2_ref, gq_ref,
             dx_ref, dy_ref, da_ref, dgp_ref, dgq_ref, acc_ref):
        i, j = pl.program_id(0), pl.program_id(1)

        @pl.when(j == 0)
        def _():
            dy, dgr = _rms_bwd(y_ref[...], gq_ref[...], dxo_ref[...])
            dy_ref[...] = dy.astype(BF16)
            _acc_rows8(dgq_ref, _rows8(dgr), i == 0)

        dr = _dot_nt(dy_ref[...], w2_ref[...])
        da = (dr * (2.0 * jnp.sqrt(r_ref[...].astype(F32)))).astype(BF16)
        da_ref[...] = da
        term = _dot_nt(da, w1_ref[...])

        @pl.when(j == 0)
        def _():
            acc_ref[...] = term

        @pl.when(j > 0)
        def _():
            acc_ref[...] += term

        @pl.when(j == FF_STEPS - 1)
        def _():
            dxn, dgr = _rms_bwd(x_ref[...], gp_ref[...], acc_ref[...])
            dx_ref[...] = dxo_ref[...] + dxn
            _acc_rows8(dgp_ref, _rows8(dgr), i == 0)

    row = lambda i, j: (i, 0)
    vec = pl.BlockSpec((1, D_MODEL), lambda i, j: (0, 0))
    acc8 = pl.BlockSpec((SUBLANES, D_MODEL), lambda i, j: (0, 0))
    return pl.pallas_call(
        body, name=name, grid=(t // tm, FF_STEPS),
        in_specs=[pl.BlockSpec((tm, D_MODEL), row)] * 3 + [
            pl.BlockSpec((tm, FF_STEP), lambda i, j: (i, j)),
            vec, pl.BlockSpec((D_MODEL, FF_STEP), lambda i, j: (0, j)),
            pl.BlockSpec((FF_STEP, D_MODEL), lambda i, j: (j, 0)), vec],
        out_specs=[pl.BlockSpec((tm, D_MODEL), row), pl.BlockSpec((tm, D_MODEL), row),
                   pl.BlockSpec((tm, FF_STEP), lambda i, j: (i, j)), acc8, acc8],
        out_shape=[jax.ShapeDtypeStruct((t, D_MODEL), F32), jax.ShapeDtypeStruct((t, D_MODEL), BF16),
                   jax.ShapeDtypeStruct((t, D_FF), BF16),
                   jax.ShapeDtypeStruct((SUBLANES, D_MODEL), F32), jax.ShapeDtypeStruct((SUBLANES, D_MODEL), F32)],
        scratch_shapes=[pltpu.VMEM((tm, D_MODEL), F32)],
        compiler_params=_params(("arbitrary", "arbitrary")),
    )(dxo, x, y, r, g_pre, w1, w2, g_post)


def _lower_bound(table):
    e = jnp.exp(table - jnp.max(table, axis=0, keepdims=True))
    return e[0:1, :] / jnp.sum(e, axis=0, keepdims=True)


def _hgrn2_block(q_ref, f_ref, lb):
    tb = f_ref.shape[0]
    sig = _sigmoid(f_ref[...])
    f = lb + (1.0 - lb) * sig
    qraw = q_ref[...]
    sq = _sigmoid(qraw)
    r = lax.broadcasted_iota(jnp.int32, (tb, tb), 0)
    c = lax.broadcasted_iota(jnp.int32, (tb, tb), 1)
    same = (r // SUB_CHUNK) == (c // SUB_CHUNK)
    logf = jnp.log(f)
    gsum = _mask_dot(same & (c <= r), logf)
    glast = _mask_dot(same, logf)
    return dict(sig=sig, f=f, kk=1.0 - f, qraw=qraw, sq=sq, qs=qraw * sq, gsum=gsum,
                eg=jnp.exp(gsum), ekd=jnp.exp(glast - gsum), a=jnp.exp(glast))


def _head_sums(x):
    parts = [jnp.broadcast_to(jnp.sum(x[:, h * HEAD_A:(h + 1) * HEAD_A], axis=1, keepdims=True), (x.shape[0], HEAD_A))
             for h in range(A_HEADS)]
    return jnp.concatenate(parts, axis=1)


def _hgrn2_intra(g, kk, qs, v):
    row = lax.broadcasted_iota(jnp.int32, g.shape, 0)
    o = _head_sums(qs * kk) * v
    for j in range(1, SUB_CHUNK):
        decay = jnp.exp(jnp.where(row >= j, g - pltpu.roll(g, j, 0), NEG))
        o = o + _head_sums(qs * pltpu.roll(kk, j, 0) * decay) * pltpu.roll(v, j, 0)
    return o


def _hgrn2_intra_bwd(g, kk, qs, v, do):
    row = lax.broadcasted_iota(jnp.int32, g.shape, 0)
    dsc = _head_sums(do * v)
    dqs, dkk, dv = dsc * kk, dsc * qs, _head_sums(qs * kk) * do
    for j in range(1, SUB_CHUNK):
        k_dn = pltpu.roll(kk, j, 0)
        decay = jnp.exp(jnp.where(row >= j, g - pltpu.roll(g, j, 0), NEG))
        d_score = _head_sums(do * pltpu.roll(v, j, 0)) * decay
        dqs = dqs + d_score * k_dn
        dkk = dkk + pltpu.roll(d_score * qs, SUB_CHUNK - j, 0)
        dv = dv + pltpu.roll(_head_sums(qs * k_dn * decay) * do, SUB_CHUNK - j, 0)
    return dqs, dkk, dv


def _hgrn2_fwd(proj, lb_table, a_norm, name):
    t = proj.shape[0]
    tb = HGRN_BLOCK
    n_tb = SEQ // tb
    n_seq = t // SEQ
    n_sub = tb // SUB_CHUNK

    def body(q_ref, f_ref, i_ref, g_ref, lbt_ref, an_ref, o_ref, pre_ref, sts_ref, st_ref,
             gs_ref, kk_ref, qs_ref, eg_ref, ekd_ref, a_ref):
        @pl.when(pl.program_id(1) == 0)
        def _():
            st_ref[...] = jnp.zeros_like(st_ref)

        an = an_ref[...]
        blk = _hgrn2_block(q_ref, f_ref, _lower_bound(lbt_ref[...]))
        for ref, key in ((gs_ref, "gsum"), (kk_ref, "kk"), (qs_ref, "qs"), (eg_ref, "eg"), (ekd_ref, "ekd"), (a_ref, "a")):
            ref[...] = blk[key]

        def step(c, carry):
            rows = pl.ds(pl.multiple_of(c * SUB_CHUNK, SUB_CHUNK), SUB_CHUNK)
            kk, qs, v = kk_ref[rows, :], qs_ref[rows, :], i_ref[rows, :]
            o = _hgrn2_intra(gs_ref[rows, :], kk, qs, v)
            qg, kd, vb = (qs * eg_ref[rows, :]).astype(BF16), (kk * ekd_ref[rows, :]).astype(BF16), v.astype(BF16)
            for h in range(A_HEADS):
                lanes = slice(h * HEAD_A, (h + 1) * HEAD_A)
                st = st_ref[h]
                sts_ref[0, c, h] = st
                o_h = o[:, lanes] + _dot_nt(qg[:, lanes], st.astype(BF16))
                st_ref[h] = st * a_ref[rows, lanes][0:1] + _dot_tn(vb[:, lanes], kd[:, lanes])
                pre_ref[rows, lanes] = o_h
                graw = g_ref[rows, lanes]
                o_ref[rows, lanes] = (_rms(o_h, an[:, lanes]) * (graw * _sigmoid(graw))).astype(BF16)
            return carry

        lax.fori_loop(0, n_sub, step, 0, unroll=2)

    def col(k):
        return pl.BlockSpec((tb, A_WIDTH), lambda b, s, k=k: (b * n_tb + s, k))

    out_rows = pl.BlockSpec((tb, A_WIDTH), lambda b, s: (b * n_tb + s, 0))
    return pl.pallas_call(
        body, name=name, grid=(n_seq, n_tb),
        in_specs=[col(0), col(1), col(2), col(3),
                  pl.BlockSpec((3, A_WIDTH), lambda b, s: (0, 0)), pl.BlockSpec((1, A_WIDTH), lambda b, s: (0, 0))],
        out_specs=[out_rows, out_rows,
                   pl.BlockSpec((1, n_sub, A_HEADS, HEAD_A, HEAD_A), lambda b, s: (b * n_tb + s, 0, 0, 0, 0))],
        out_shape=[jax.ShapeDtypeStruct((t, D_MODEL), BF16), jax.ShapeDtypeStruct((t, A_WIDTH), F32),
                   jax.ShapeDtypeStruct((n_seq * n_tb, n_sub, A_HEADS, HEAD_A, HEAD_A), F32)],
        scratch_shapes=[pltpu.VMEM((A_HEADS, HEAD_A, HEAD_A), F32)] + [pltpu.VMEM((tb, A_WIDTH), F32)] * 6,
        compiler_params=_params(("parallel", "arbitrary")),
    )(proj, proj, proj, proj, lb_table, a_norm)


def _hgrn2_bwd(proj, dcat, pre, states, lb_table, a_norm, name):
    t = proj.shape[0]
    tb = HGRN_BLOCK
    n_tb = SEQ // tb
    n_seq = t // SEQ
    n_sub = tb // SUB_CHUNK

    def body(q_ref, f_ref, i_ref, g_ref, do_ref, pre_ref, sts_ref, lbt_ref, an_ref, dp_ref, dlb_ref, dan_ref, dst_ref,
             gs_ref, kk_ref, qs_ref, eg_ref, ekd_ref, a_ref, dpre_ref, dlf_ref, dqs_ref, dkk_ref):
        b, s = pl.program_id(0), pl.program_id(1)

        @pl.when(s == 0)
        def _():
            dst_ref[...] = jnp.zeros_like(dst_ref)

        @pl.when((b == 0) & (s == 0))
        def _():
            dlb_ref[...] = jnp.zeros_like(dlb_ref)
            dan_ref[...] = jnp.zeros_like(dan_ref)

        lb = _lower_bound(lbt_ref[...])
        an = an_ref[...]
        heads = [slice(h * HEAD_A, (h + 1) * HEAD_A) for h in range(A_HEADS)]
        blk = _hgrn2_block(q_ref, f_ref, lb)
        for ref, key in ((gs_ref, "gsum"), (kk_ref, "kk"), (qs_ref, "qs"), (eg_ref, "eg"), (ekd_ref, "ekd"), (a_ref, "a")):
            ref[...] = blk[key]
        for h, lanes in enumerate(heads):
            graw, o = g_ref[:, lanes], pre_ref[:, lanes]
            sg = _sigmoid(graw)
            dout = do_ref[:, lanes]
            d_o, dgr = _rms_bwd(o, an[:, lanes], dout * (graw * sg))
            dan_ref[0:1, lanes] += jnp.sum(dgr, axis=0, keepdims=True)
            dp_ref[:, 3 * A_WIDTH + h * HEAD_A:3 * A_WIDTH + (h + 1) * HEAD_A] = (
                dout * _rms(o, an[:, lanes]) * (sg * (1.0 + graw * (1.0 - sg)))).astype(BF16)
            dpre_ref[:, lanes] = d_o

        tri_t = (lax.broadcasted_iota(jnp.int32, (SUB_CHUNK, SUB_CHUNK), 0)
                 <= lax.broadcasted_iota(jnp.int32, (SUB_CHUNK, SUB_CHUNK), 1)).astype(F32)

        def back(k, carry):
            c = n_sub - 1 - k
            rows = pl.ds(pl.multiple_of(c * SUB_CHUNK, SUB_CHUNK), SUB_CHUNK)
            g, kk, qs, v, d_o = gs_ref[rows, :], kk_ref[rows, :], qs_ref[rows, :], i_ref[rows, :], dpre_ref[rows, :]
            eg, ekd, a = eg_ref[rows, :], ekd_ref[rows, :], a_ref[rows, :]
            dqs, dkk, dv = _hgrn2_intra_bwd(g, kk, qs, v, d_o)
            qg_f, kd_f = qs * eg, kk * ekd
            qg, kd, vb, dob = qg_f.astype(BF16), kd_f.astype(BF16), v.astype(BF16), d_o.astype(BF16)
            dqg, dkd, da, dv_st = [], [], [], []
            for h, lanes in enumerate(heads):
                st, dst = sts_ref[0, c, h], dst_ref[h]
                dstb = dst.astype(BF16)
                dqg.append(_dot(dob[:, lanes], st.astype(BF16)))
                dv_st.append(_dot_nt(kd[:, lanes], dstb))
                dkd.append(_dot(vb[:, lanes], dstb))
                da.append(jnp.broadcast_to(jnp.sum(dst * st, axis=0, keepdims=True), (SUB_CHUNK, HEAD_A)))
                dst_ref[h] = dst * a[0:1, lanes] + _dot_tn(dob[:, lanes], qg[:, lanes])
            dqg, dkd, da, dv_st = [jnp.concatenate(p, axis=1) for p in (dqg, dkd, da, dv_st)]
            d_gsum = qs * dqs - kk * dkk + dqg * qg_f - dkd * kd_f
            d_glast = jnp.sum(dkd * kd_f, axis=0, keepdims=True) + da * a
            dlf_ref[rows, :] = jnp.dot(tri_t, d_gsum, precision=lax.Precision.HIGHEST,
                                       preferred_element_type=F32) + d_glast
            dqs_ref[rows, :] = dqs + dqg * eg
            dkk_ref[rows, :] = dkk + dkd * ekd
            dp_ref[rows, 2 * A_WIDTH:3 * A_WIDTH] = (dv + dv_st).astype(BF16)
            return carry

        lax.fori_loop(0, n_sub, back, 0, unroll=2)
        sig, sq, qraw = blk["sig"], blk["sq"], blk["qraw"]
        d_f = dlf_ref[...] / blk["f"] - dkk_ref[...]
        dlb_ref[0:1, :] += jnp.sum(d_f * (1.0 - sig), axis=0, keepdims=True)
        dp_ref[:, 0:A_WIDTH] = (dqs_ref[...] * (sq * (1.0 + qraw * (1.0 - sq)))).astype(BF16)
        dp_ref[:, A_WIDTH:2 * A_WIDTH] = (d_f * (1.0 - lb) * sig * (1.0 - sig)).astype(BF16)

    def rev(s):
        return n_tb - 1 - s

    def col(k):
        return pl.BlockSpec((tb, A_WIDTH), lambda b, s, k=k: (b * n_tb + rev(s), k))

    acc8 = pl.BlockSpec((SUBLANES, A_WIDTH), lambda b, s: (0, 0))
    return pl.pallas_call(
        body, name=name, grid=(n_seq, n_tb),
        in_specs=[col(0), col(1), col(2), col(3), col(0), col(0),
                  pl.BlockSpec((1, n_sub, A_HEADS, HEAD_A, HEAD_A), lambda b, s: (b * n_tb + rev(s), 0, 0, 0, 0)),
                  pl.BlockSpec((3, A_WIDTH), lambda b, s: (0, 0)), pl.BlockSpec((1, A_WIDTH), lambda b, s: (0, 0))],
        out_specs=[pl.BlockSpec((tb, 4 * A_WIDTH), lambda b, s: (b * n_tb + rev(s), 0)), acc8, acc8],
        out_shape=[jax.ShapeDtypeStruct((t, EVEN_IN), BF16)] + [jax.ShapeDtypeStruct((SUBLANES, A_WIDTH), F32)] * 2,
        scratch_shapes=[pltpu.VMEM((A_HEADS, HEAD_A, HEAD_A), F32)] + [pltpu.VMEM((tb, A_WIDTH), F32)] * 10,
        compiler_params=_params(("arbitrary", "arbitrary")),
    )(proj, proj, proj, proj, dcat, pre, states, lb_table, a_norm)


GMLP_ROWS = 512


def _gmlp_chunk(ub, vb, ln_g, ln_b, ws, bias):
    u = [_gelu(a) for a in ub]
    v = [_gelu(a) for a in vb]
    mu = sum(jnp.sum(a, axis=-1, keepdims=True) for a in v) * (1.0 / B_WIDTH)
    cen = [a - mu for a in v]
    var = sum(jnp.sum(a * a, axis=-1, keepdims=True) for a in cen) * (1.0 / B_WIDTH)
    inv = lax.rsqrt(var + EPS)
    r = lax.broadcasted_iota(jnp.int32, (B_CHUNK, B_CHUNK), 0)
    c = lax.broadcasted_iota(jnp.int32, (B_CHUNK, B_CHUNK), 1)
    outs = []
    for g in range(B_GROUPS):
        vn = (cen[g] * inv * ln_g[g] + ln_b[g]).astype(BF16)
        wm = jnp.where(c <= r, ws[g], 0.0).astype(BF16)
        outs.append(u[g] * (_dot(wm, vn) + bias[g]))
    return outs


def _lane_groups(ref, rows=slice(None)):
    return [ref[rows, g * LANES:(g + 1) * LANES] for g in range(B_GROUPS)]


def _gmlp_fwd(proj, mixed, ln_g, ln_b, ws, bias_t, name):
    t = proj.shape[0]
    tm = GMLP_ROWS

    def body(u_ref, v_ref, lg_ref, lb_ref, ws_ref, bt_ref, _, o_ref):
        for ch in range(tm // B_CHUNK):
            rows = slice(ch * B_CHUNK, (ch + 1) * B_CHUNK)
            outs = _gmlp_chunk(_lane_groups(u_ref, rows), _lane_groups(v_ref, rows), _lane_groups(lg_ref),
                               _lane_groups(lb_ref), [ws_ref[g] for g in range(B_GROUPS)],
                               [bt_ref[:, g:g + 1] for g in range(B_GROUPS)])
            for g in range(B_GROUPS):
                o_ref[rows, g * LANES:(g + 1) * LANES] = outs[g].astype(BF16)

    vec = pl.BlockSpec((1, B_WIDTH), lambda i: (0, 0))
    return pl.pallas_call(
        body, name=name, grid=(t // tm,),
        in_specs=[pl.BlockSpec((tm, B_WIDTH), lambda i: (i, 4)), pl.BlockSpec((tm, B_WIDTH), lambda i: (i, 5)), vec, vec,
                  pl.BlockSpec((B_GROUPS, B_CHUNK, B_CHUNK), lambda i: (0, 0, 0)),
                  pl.BlockSpec((B_CHUNK, B_GROUPS), lambda i: (0, 0)), pl.BlockSpec(memory_space=pl.ANY)],
        out_specs=pl.BlockSpec((tm, B_WIDTH), lambda i: (i, 1)),
        out_shape=jax.ShapeDtypeStruct(mixed.shape, BF16),
        input_output_aliases={6: 0},
        compiler_params=_params(("parallel",)),
    )(proj, proj, ln_g, ln_b, ws, bias_t, mixed)


def _gmlp_bwd(proj, dcat, dproj, ln_g, ln_b, ws, bias_t, name):
    t = proj.shape[0]
    tm = GMLP_ROWS

    def body(u_ref, v_ref, do_ref, lg_ref, lb_ref, ws_ref, bt_ref, _, duv_ref, dlg_ref, dlb_ref, dws_ref, dbt_ref):
        @pl.when(pl.program_id(0) == 0)
        def _():
            dlg_ref[...] = jnp.zeros_like(dlg_ref)
            dlb_ref[...] = jnp.zeros_like(dlb_ref)
            dws_ref[...] = jnp.zeros_like(dws_ref)
            dbt_ref[...] = jnp.zeros_like(dbt_ref)

        for ch in range(tm // B_CHUNK):
            rows = slice(ch * B_CHUNK, (ch + 1) * B_CHUNK)
            _, vjp = jax.vjp(
                _gmlp_chunk, _lane_groups(u_ref, rows), _lane_groups(v_ref, rows), _lane_groups(lg_ref),
                _lane_groups(lb_ref), [ws_ref[g] for g in range(B_GROUPS)],
                [bt_ref[:, g:g + 1] for g in range(B_GROUPS)])
            du, dv, dlg, dlb, dw, dbt = vjp(_lane_groups(do_ref, rows))
            for g in range(B_GROUPS):
                lanes = slice(g * LANES, (g + 1) * LANES)
                duv_ref[rows, lanes] = du[g].astype(BF16)
                duv_ref[rows, B_WIDTH + g * LANES:B_WIDTH + (g + 1) * LANES] = dv[g].astype(BF16)
                dlg_ref[0:1, lanes] += dlg[g]
                dlb_ref[0:1, lanes] += dlb[g]
                dws_ref[g] += dw[g]
                dbt_ref[:, g:g + 1] += dbt[g]

    vec = pl.BlockSpec((1, B_WIDTH), lambda i: (0, 0))
    acc8 = pl.BlockSpec((SUBLANES, B_WIDTH), lambda i: (0, 0))
    ws_spec = pl.BlockSpec((B_GROUPS, B_CHUNK, B_CHUNK), lambda i: (0, 0, 0))
    bt_spec = pl.BlockSpec((B_CHUNK, B_GROUPS), lambda i: (0, 0))
    return pl.pallas_call(
        body, name=name, grid=(t // tm,),
        in_specs=[pl.BlockSpec((tm, B_WIDTH), lambda i: (i, 4)), pl.BlockSpec((tm, B_WIDTH), lambda i: (i, 5)),
                  pl.BlockSpec((tm, B_WIDTH), lambda i: (i, 1)), vec, vec, ws_spec, bt_spec,
                  pl.BlockSpec(memory_space=pl.ANY)],
        out_specs=[pl.BlockSpec((tm, 2 * B_WIDTH), lambda i: (i, 2)), acc8, acc8, ws_spec, bt_spec],
        out_shape=[jax.ShapeDtypeStruct(dproj.shape, BF16), jax.ShapeDtypeStruct((SUBLANES, B_WIDTH), F32),
                   jax.ShapeDtypeStruct((SUBLANES, B_WIDTH), F32),
                   jax.ShapeDtypeStruct((B_GROUPS, B_CHUNK, B_CHUNK), F32),
                   jax.ShapeDtypeStruct((B_CHUNK, B_GROUPS), F32)],
        input_output_aliases={7: 0},
        compiler_params=_params(("arbitrary",)),
    )(proj, proj, dcat, ln_g, ln_b, ws, bias_t, dproj)


QK_SCALE = 1.0 / math.sqrt(C_HEAD_DIM)
ATTN_UNROLL = 8
LANE_GROUPS = D_MODEL // LANES
Q_BLOCKS = SEQ // C_BLOCK


def _attn_window(i, d):
    sub_blocks = Q_BLOCKS // d
    q0 = pl.multiple_of(i * C_BLOCK, C_BLOCK)
    k0 = pl.multiple_of(jnp.maximum(i - 1, 0) * C_BLOCK, C_BLOCK)
    key = k0 + lax.broadcasted_iota(jnp.int32, (C_BLOCK, 2 * C_BLOCK), 1)
    dist = (q0 + lax.broadcasted_iota(jnp.int32, (C_BLOCK, 2 * C_BLOCK), 0)) - key
    own_subsequence = (key >= q0) | (i % sub_blocks > 0)
    return pl.ds(q0, C_BLOCK), pl.ds(k0, 2 * C_BLOCK), (dist >= 0) & (dist <= C_BLOCK) & own_subsequence


def _head_masks():
    lane = lax.broadcasted_iota(jnp.int32, (C_BLOCK, LANES), 1)
    return [lane < C_HEAD_DIM, lane >= C_HEAD_DIM]


def _flat_spec(col_of):
    return pl.BlockSpec((1, SEQ, LANES), lambda b, g: (b, 0, col_of(g)))


def _put_heads(tile, g, col0, col1):
    lane = lax.broadcasted_iota(jnp.int32, tile.shape, 1)
    return jnp.where(lane == 2 * g, col0, jnp.where(lane == 2 * g + 1, col1, tile))


def _get_head(tile, h):
    lane = lax.broadcasted_iota(jnp.int32, tile.shape, 1)
    return jnp.sum(jnp.where(lane == h, tile, 0.0), axis=1, keepdims=True)


PER_HEAD_SPEC = pl.BlockSpec((1, SEQ, LANES), lambda b, g: (b, 0, 0))


def _attn_branch_fwd(qkv, name):
    n_seq, d, l, _ = qkv.shape
    flat = qkv.reshape(n_seq, SEQ, ODD_IN)

    def body(q_ref, k_ref, v_ref, o_ref, m_ref, l_ref):
        heads = _head_masks()
        g = pl.program_id(1)

        @pl.when(g == 0)
        def _():
            m_ref[...] = jnp.zeros_like(m_ref)
            l_ref[...] = jnp.zeros_like(l_ref)

        def block(i, carry):
            rows, keys, mask = _attn_window(i, d)
            q, k, v = q_ref[0, rows, :], k_ref[0, keys, :], v_ref[0, keys, :]
            res = []
            for hm in heads:
                s = jnp.where(mask, _dot_nt(jnp.where(hm, q, 0), k), NEG)
                m = jnp.max(s, axis=-1, keepdims=True)
                p = jnp.exp(s - m)
                res.append((_dot(p.astype(BF16), v), m, jnp.sum(p, axis=-1, keepdims=True)))
            o_ref[0, rows, :] = jnp.where(heads[0], res[0][0], res[1][0])
            m_ref[0, rows, :] = _put_heads(m_ref[0, rows, :], g, res[0][1], res[1][1])
            l_ref[0, rows, :] = _put_heads(l_ref[0, rows, :], g, res[0][2], res[1][2])
            return carry

        lax.fori_loop(0, Q_BLOCKS, block, 0, unroll=ATTN_UNROLL)

    o, m, l_sum = pl.pallas_call(
        body, name=name, grid=(n_seq, LANE_GROUPS),
        in_specs=[_flat_spec(lambda g: g), _flat_spec(lambda g: LANE_GROUPS + g),
                  _flat_spec(lambda g: 2 * LANE_GROUPS + g)],
        out_specs=[_flat_spec(lambda g: g), PER_HEAD_SPEC, PER_HEAD_SPEC],
        out_shape=[jax.ShapeDtypeStruct((n_seq, SEQ, D_MODEL), F32)] + [jax.ShapeDtypeStruct((n_seq, SEQ, LANES), F32)] * 2,
        compiler_params=_params(("parallel", "arbitrary")),
    )(flat, flat, flat)
    return [o.reshape(n_seq, d, l, D_MODEL), m.reshape(n_seq, d, l, LANES), l_sum.reshape(n_seq, d, l, LANES)]


def _attn_merge(branches, name):
    n_seq = branches[0][0].shape[0]
    t = n_seq * SEQ
    tm = MERGE_TILE

    def body(*refs):
        ins = refs[:9]
        o_ref, ob_ref, lse_ref = refs[9:12]
        nat = refs[12:]
        for b, d in enumerate(C_DILATIONS[1:]):
            for k in range(3):
                _load_dilated(ins[3 + 3 * b + k], d, nat[3 * b + k])
        ms = [ins[1][0, 0], nat[1][0], nat[4][0]]
        ls = [ins[2][0, 0], nat[2][0], nat[5][0]]
        m_all = jnp.maximum(jnp.maximum(ms[0], ms[1]), ms[2])
        ws = [jnp.exp(ms[b] - m_all) for b in range(3)]
        lane = lax.broadcasted_iota(jnp.int32, m_all.shape, 1)
        total = jnp.where(lane < C_HEADS, ws[0] * ls[0] + ws[1] * ls[1] + ws[2] * ls[2], 1.0)
        lse_ref[...] = m_all + jnp.log(total)
        first_head = lane < C_HEAD_DIM
        for p in range(LANE_GROUPS):
            lanes = slice(p * LANES, (p + 1) * LANES)
            spread = lambda c: jnp.where(first_head, c[:, 2 * p:2 * p + 1], c[:, 2 * p + 1:2 * p + 2])
            os_ = [ins[0][0, 0, :, lanes], nat[0][p], nat[3][p]]
            o = (spread(ws[0]) * os_[0] + spread(ws[1]) * os_[1] + spread(ws[2]) * os_[2]) / spread(total)
            o_ref[:, lanes] = o
            ob_ref[:, lanes] = o.astype(BF16)

    row = pl.BlockSpec((tm, D_MODEL), lambda i: (i, 0))
    flat = [a for br in branches for a in br]
    in_specs = []
    for wide, narrow in zip(_dilated_specs(tm, D_MODEL, lambda: 0), _dilated_specs(tm, LANES, lambda: 0)):
        in_specs += [wide, narrow, narrow]
    per_head = pltpu.VMEM((1, tm, LANES), F32)
    return pl.pallas_call(
        body, name=name, grid=(t // tm,), in_specs=in_specs,
        out_specs=[row, row, pl.BlockSpec((tm, LANES), lambda i: (i, 0))],
        out_shape=[jax.ShapeDtypeStruct((t, D_MODEL), F32), jax.ShapeDtypeStruct((t, D_MODEL), BF16),
                   jax.ShapeDtypeStruct((t, LANES), F32)],
        scratch_shapes=[pltpu.VMEM((LANE_GROUPS, tm, LANES), F32), per_head, per_head] * 2,
        compiler_params=_params(("parallel",)),
    )(*flat)


def _attn_branch_bwd(qkv, dout, lse, delta, name):
    n_seq, d, l, _ = qkv.shape
    flat = lambda a: a.reshape(n_seq, SEQ, a.shape[-1])

    def body(q_ref, k_ref, v_ref, do_ref, lse_nat_ref, dl_nat_ref, dq_ref, dk_ref, dv_ref, lse_ref, dl_ref,
             dkt_ref, dvt_ref):
        heads = _head_masks()
        g = pl.program_id(1)
        dkt_ref[...] = jnp.zeros_like(dkt_ref)
        dvt_ref[...] = jnp.zeros_like(dvt_ref)
        for nat_ref, dst_ref in ((lse_nat_ref, lse_ref), (dl_nat_ref, dl_ref)):
            for r in range(d):
                rows = pl.ds(r, l, stride=d) if d > 1 else slice(None)
                dst_ref[r * l:(r + 1) * l, :] = nat_ref.at[0][rows, :]

        def block(i, carry):
            rows, keys, mask = _attn_window(i, d)
            q, do = q_ref[0, rows, :], do_ref[0, rows, :]
            k, v = k_ref[0, keys, :], v_ref[0, keys, :]
            lse_b, dl_b = lse_ref[rows, :], dl_ref[rows, :]
            dq, dk, dv = [], None, None
            for hh, hm in enumerate(heads):
                qh, doh = jnp.where(hm, q, 0), jnp.where(hm, do, 0)
                s = jnp.where(mask, _dot_nt(qh, k), NEG)
                p = jnp.exp(s - _get_head(lse_b, 2 * g + hh))
                ds = (p * (_dot_nt(doh, v) - _get_head(dl_b, 2 * g + hh))).astype(BF16)
                dq.append(_dot(ds, k) * QK_SCALE)
                dk_h, dv_h = _dot_tn(qh, ds), _dot_tn(doh, p.astype(BF16))
                dk = dk_h if dk is None else dk + dk_h
                dv = dv_h if dv is None else dv + dv_h
            dq_ref[0, rows, :] = jnp.where(heads[0], dq[0], dq[1]).astype(BF16)
            dkt_ref[:, keys] += dk
            dvt_ref[:, keys] += dv
            return carry

        lax.fori_loop(0, Q_BLOCKS, block, 0, unroll=ATTN_UNROLL)
        for c in range(SEQ // ROW_TILE):
            rows = slice(c * ROW_TILE, (c + 1) * ROW_TILE)
            dk_ref[0, rows, :] = dkt_ref[:, rows].T.astype(BF16)
            dv_ref[0, rows, :] = dvt_ref[:, rows].T.astype(BF16)

    act = _flat_spec(lambda g: g)
    outs = pl.pallas_call(
        body, name=name, grid=(n_seq, LANE_GROUPS),
        in_specs=[_flat_spec(lambda g: g), _flat_spec(lambda g: LANE_GROUPS + g),
                  _flat_spec(lambda g: 2 * LANE_GROUPS + g), act, PER_HEAD_SPEC, PER_HEAD_SPEC],
        out_specs=[act] * 3,
        out_shape=[jax.ShapeDtypeStruct((n_seq, SEQ, D_MODEL), BF16)] * 3,
        scratch_shapes=[pltpu.VMEM((SEQ, LANES), F32)] * 2 + [pltpu.VMEM((LANES, SEQ), F32)] * 2,
        compiler_params=_params(("parallel", "parallel")),
    )(flat(qkv), flat(qkv), flat(qkv), flat(dout), lse, delta)
    return [o.reshape(n_seq, d, l, D_MODEL) for o in outs]


def _attn_combine_bwd(grads, rope, name):
    n_seq = grads[0][0].shape[0]
    t = n_seq * SEQ
    tm = MERGE_TILE

    def body(*refs):
        c_ref, s_ref, o_ref, nat4_ref, nat16_ref = refs[9:]
        for sec in range(3):
            _load_dilated(refs[3 + sec], 4, nat4_ref)
            _load_dilated(refs[6 + sec], 16, nat16_ref)
            for p in range(LANE_GROUPS):
                blk = refs[sec][0, 0, :, p * LANES:(p + 1) * LANES] + nat4_ref[p] + nat16_ref[p]
                if sec < 2:
                    blk = blk * c_ref[...] - _swap_halves(blk) * s_ref[...]
                o_ref[:, sec * D_MODEL + p * LANES:sec * D_MODEL + (p + 1) * LANES] = blk.astype(BF16)

    tab = pl.BlockSpec((tm, LANES), lambda i: (i, 0))
    flat = [a for br in grads for a in br]
    in_specs = []
    for spec in _dilated_specs(tm, D_MODEL, lambda: 0):
        in_specs += [spec] * 3
    return pl.pallas_call(
        body, name=name, grid=(t // tm,), in_specs=in_specs + [tab, tab],
        out_specs=pl.BlockSpec((tm, ODD_IN), lambda i: (i, 0)),
        out_shape=jax.ShapeDtypeStruct((t, ODD_IN), BF16),
        scratch_shapes=[pltpu.VMEM((LANE_GROUPS, tm, LANES), F32)] * 2,
        compiler_params=_params(("parallel",)),
    )(*flat, *rope)


def _adamw(w, g, m, v):
    m = ADAM_B1 * m + (1.0 - ADAM_B1) * g
    v = ADAM_B2 * v + (1.0 - ADAM_B2) * jnp.square(g)
    m_hat = m / (1.0 - ADAM_B1 ** ADAM_STEP)
    v_hat = v / (1.0 - ADAM_B2 ** ADAM_STEP)
    delta = -ADAM_LR * (m_hat / (jnp.sqrt(v_hat) + ADAM_EPS) + ADAM_WD * w)
    return delta, m, v


def _adamw_sharded(parts, w, m, v, after, name):
    n_layers, rows, cols = w.shape
    tr = min(rows, 256)

    def body(*refs):
        p_refs = refs[:n_layers]
        w_ref, m_ref, v_ref, _, g_ref, d_ref, mo_ref, vo_ref = refs[n_layers:]
        layer = pl.program_id(0)
        g = None
        for l, p_ref in enumerate(p_refs):
            g_l = p_ref[0].astype(F32)
            for s in range(1, N_DEV):
                g_l = g_l + p_ref[s].astype(F32)
            g = g_l if g is None else jnp.where(layer == l, g_l, g)
        delta, mn, vn = _adamw(w_ref[0], g, m_ref[0], v_ref[0])
        g_ref[0] = g
        d_ref[0] = delta
        mo_ref[0] = mn
        vo_ref[0] = vn

    def part_spec(l):
        return pl.BlockSpec((N_DEV, tr, cols), lambda a, i: (0, jnp.where(a == l, i, 0), 0))

    row = pl.BlockSpec((1, tr, cols), lambda a, i: (a, i, 0))
    return pl.pallas_call(
        body, name=name, grid=(n_layers, rows // tr),
        in_specs=[part_spec(l) for l in range(n_layers)] + [row, row, row, pl.BlockSpec(memory_space=pl.ANY)],
        out_specs=[row] * 4, out_shape=[jax.ShapeDtypeStruct(w.shape, F32)] * 4,
        compiler_params=_params(("arbitrary", "arbitrary")),
    )(*parts, w, m, v, after)


def _small_update(gathered, where, weights, moments_m, moments_v, lb_index, name):
    n = len(weights)
    n_g = len(gathered)

    def body(*refs):
        g_refs = refs[:n_g]
        w_refs, m_refs, v_refs = refs[n_g:n_g + n], refs[n_g + n:n_g + 2 * n], refs[n_g + 2 * n:n_g + 3 * n]
        outs = refs[n_g + 3 * n:]

        def total(k):
            array, rows, lanes = where[k]
            ref = g_refs[array]
            index = (slice(None),) * (len(ref.shape) - 1) if rows is None else (rows, lanes)
            acc = ref[(0,) + index]
            for s in range(1, N_DEV):
                acc = acc + ref[(s,) + index]
            return acc

        loss_rows = total(n)
        outs[0][...] = jnp.sum(jnp.sum(loss_rows, axis=1, keepdims=True), axis=0, keepdims=True)
        for k in range(n):
            part = total(k)
            if k == lb_index:
                dlb = jnp.sum(part, axis=0, keepdims=True)
                tab = w_refs[k][...]
                e = jnp.exp(tab - jnp.max(tab, axis=0, keepdims=True))
                p = e / jnp.sum(e, axis=0, keepdims=True)
                first = lax.broadcasted_iota(jnp.int32, p.shape, 0) == 0
                grads = [(slice(None), p * (jnp.where(first, dlb, 0.0) - p[0:1, :] * dlb))]
            elif part.shape == w_refs[k].shape:
                grads = [(slice(None), part)]
            else:
                grads = [(slice(l, l + 1), jnp.sum(part[l * SUBLANES:(l + 1) * SUBLANES], axis=0, keepdims=True))
                         for l in range(w_refs[k].shape[0])]
            for rows, g in grads:
                delta, mn, vn = _adamw(w_refs[k][rows], g, m_refs[k][rows], v_refs[k][rows])
                outs[1 + 4 * k][rows] = g
                outs[2 + 4 * k][rows] = delta
                outs[3 + 4 * k][rows] = mn
                outs[4 + 4 * k][rows] = vn

    vmem = pl.BlockSpec(memory_space=pltpu.VMEM)
    out_shape = [jax.ShapeDtypeStruct((1, 1), F32)]
    for w in weights:
        out_shape += [jax.ShapeDtypeStruct(w.shape, F32)] * 4
    args = list(gathered) + list(weights) + list(moments_m) + list(moments_v)
    return pl.pallas_call(
        body, name=name, in_specs=[vmem] * len(args), out_specs=[vmem] * len(out_shape), out_shape=out_shape,
        compiler_params=pltpu.CompilerParams(vmem_limit_bytes=VMEM_LIMIT),
    )(*args)


def kernel(x, positions, norm_mix_pre, norm_mix_post, norm_ffn_pre, norm_ffn_post, w_in_even, lb_table, a_norm, b_ln_g, b_ln_b, b_ws, b_bias, w_out_even, w_in_odd, w_out_odd, w_ff1, w_ff2, loss_target, m_norm_mix_pre, m_norm_mix_post, m_norm_ffn_pre, m_norm_ffn_post, m_w_in_even, m_lb_table, m_a_norm, m_b_ln_g, m_b_ln_b, m_b_ws, m_b_bias, m_w_out_even, m_w_in_odd, m_w_out_odd, m_w_ff1, m_w_ff2, v_norm_mix_pre, v_norm_mix_post, v_norm_ffn_pre, v_norm_ffn_post, v_w_in_even, v_lb_table, v_a_norm, v_b_ln_g, v_b_ln_b, v_b_ws, v_b_bias, v_w_out_even, v_w_in_odd, v_w_out_odd, v_w_ff1, v_w_ff2):
    n_seq = x.shape[0]
    t = n_seq * SEQ
    x0 = x.reshape(t, D_MODEL)
    target = loss_target.reshape(t, D_MODEL)

    me = _my_slot().astype(jnp.int32).reshape(1)

    order = ["in_e", "out_e", "ff1_0", "ff2_0", "in_o", "out_o", "ff1_1", "ff2_1"]
    shards = dict(in_e=w_in_even[0], out_e=w_out_even[0], in_o=w_in_odd[0], out_o=w_out_odd[0],
                  ff1_0=w_ff1[0], ff1_1=w_ff1[1], ff2_0=w_ff2[0], ff2_1=w_ff2[1])
    by_columns = ("in_e", "in_o", "ff1_0", "ff1_1")

    def place(k, after):
        if k in by_columns:
            return _place_own_columns(shards[k], me, "place_" + k, after)
        return _place_own(shards[k], me, "place_" + k, False, after=after)

    gathers = {}
    send0, recv0, land0, _, token0 = _exchange_start([place(order[0], None)], [None], "gather_start_first")
    gathers[order[0]] = (land0[0], send0[0], recv0[0])
    sends, recvs, lands, _, g_token = _exchange_start([place(k, token0) for k in order[1:]],
                                                      [None] * (len(order) - 1), "gather_start")
    for k, land, send, recv in zip(order[1:], lands, sends, recvs):
        gathers[k] = (land, send, recv)

    def get_w(keys, after):
        lands_k, sends_k, recvs_k = zip(*[gathers[k] for k in keys])
        return _exchange_wait(list(lands_k), [None] * len(keys), list(sends_k), list(recvs_k), after,
                              "gather_wait_" + keys[0])

    sent = {}

    def put_g(group, blocks):
        keys = list(blocks)
        own = [_place_own(blocks[k], me, "own_" + k, True) for k in keys]
        send_sems, recv_sems, own, srcs, token = _exchange_start(own, [blocks[k] for k in keys], "scatter_start_" + group)
        sent[group] = (keys, own, srcs, send_sems, recv_sems, token)
        return token

    small_sent = []

    def put_small(parts):
        (loss_part, dg_mix_pre, dg_mix_post, dg_ffn_pre, dg_ffn_post, d_lb, d_a_norm, d_ln_g, d_ln_b, d_ws,
         d_bias_t) = parts
        packed = jnp.concatenate([dg_mix_pre, dg_mix_post, dg_ffn_pre, dg_ffn_post,
                                  jnp.concatenate([d_lb, d_a_norm], axis=1), jnp.concatenate([d_ln_g, d_ln_b], axis=1),
                                  loss_part], axis=0)
        lands_s = [_place_own(a, me, "own_small%d" % k, False, F32) for k, a in enumerate((packed, d_ws, d_bias_t))]
        send_s, recv_s, lands_s, _, token_s = _exchange_start(lands_s, [None] * 3, "gather_small_start")
        small_sent.append((lands_s, send_s, recv_s))
        return token_s

    rope = _rope_tables(positions)
    bias_t = b_bias[0].T
    dx0 = _local_step(x0, target, rope, norm_mix_pre, norm_mix_post, norm_ffn_pre, norm_ffn_post, lb_table,
                      a_norm, b_ln_g, b_ln_b, b_ws[0], bias_t, get_w, put_g, put_small, g_token)
    (small_lands, s_send, s_recv), after = small_sent[0], sent["mix0"][-1]

    big = dict(w_in_even=(["in_e"], w_in_even, m_w_in_even, v_w_in_even),
               w_out_even=(["out_e"], w_out_even, m_w_out_even, v_w_out_even),
               w_in_odd=(["in_o"], w_in_odd, m_w_in_odd, v_w_in_odd),
               w_out_odd=(["out_o"], w_out_odd, m_w_out_odd, v_w_out_odd),
               w_ff1=(["ff1_0", "ff1_1"], w_ff1, m_w_ff1, v_w_ff1), w_ff2=(["ff2_0", "ff2_1"], w_ff2, m_w_ff2, v_w_ff2))
    recv, big_out = {}, {}
    for groups, names in ((("ffn1", "ffn0"), ("w_ff1", "w_ff2")), (("mix1",), ("w_in_odd", "w_out_odd")),
                          (("mix0",), ("w_in_even", "w_out_even"))):
        for group in groups:
            keys, own, srcs, send_sems, recv_sems, _ = sent[group]
            recv.update(zip(keys, _exchange_wait(own, srcs, send_sems, recv_sems, after, "scatter_wait_" + group)))
        for nm in names:
            keys, w, m, v = big[nm]
            big_out[nm] = _adamw_sharded([recv[k] for k in keys], w, m, v, after, "adamw_" + nm)
            after = big_out[nm][0]
    big_out = [big_out[nm] for nm in ("w_in_even", "w_out_even", "w_in_odd", "w_out_odd", "w_ff1", "w_ff2")]
    gathered = _exchange_wait(small_lands, [None] * 3, s_send, s_recv, after, "gather_small_wait")
    rows8 = lambda k: slice(SUBLANES * k, SUBLANES * (k + 1))
    left, right, every = slice(0, A_WIDTH), slice(A_WIDTH, 2 * A_WIDTH), slice(None)
    where = [(0, slice(0, 16), every), (0, slice(16, 32), every), (0, slice(32, 48), every), (0, slice(48, 64), every),
             (0, rows8(8), left), (0, rows8(8), right), (0, rows8(9), left), (0, rows8(9), right),
             (1, None, None), (2, None, None), (0, rows8(10), every)]
    small_w = [norm_mix_pre, norm_mix_post, norm_ffn_pre, norm_ffn_post, lb_table, a_norm, b_ln_g, b_ln_b,
               b_ws[0], bias_t]
    small_m = [m_norm_mix_pre, m_norm_mix_post, m_norm_ffn_pre, m_norm_ffn_post, m_lb_table, m_a_norm, m_b_ln_g,
               m_b_ln_b, m_b_ws[0], m_b_bias[0].T]
    small_v = [v_norm_mix_pre, v_norm_mix_post, v_norm_ffn_pre, v_norm_ffn_post, v_lb_table, v_a_norm, v_b_ln_g,
               v_b_ln_b, v_b_ws[0], v_b_bias[0].T]
    small_out = _small_update(gathered, where, small_w, small_m, small_v, 4, "small_update")
    loss = small_out[0].reshape(())
    small = [small_out[1 + 4 * k:5 + 4 * k] for k in range(len(small_w))]
    small[8] = [a[None] for a in small[8]]
    small[9] = [a.T[None] for a in small[9]]

    per_weight = small[0:4] + [big_out[0]] + small[4:10] + big_out[1:6]
    grad_x = dx0.reshape(x.shape)
    out = [loss, grad_x]
    for kind in range(4):
        out += [p[kind] for p in per_weight]
    return tuple(out)


def _local_step(x0, target, rope, norm_mix_pre, norm_mix_post, norm_ffn_pre, norm_ffn_post, lb_table, a_norm,
                b_ln_g, b_ln_b, ws, bias_t, get_w, put_g, put_small, token):
    def gain(a, l, tok):
        return a[l:l + 1] if tok is None else a[l:l + 1] + tok[0:1, 0:1]

    full = lambda a: a.reshape(-1, D_MODEL)
    owners = lambda a: a.reshape((N_DEV, -1) + a.shape[1:])

    (g_in_e,) = get_w(["in_e"], token)
    proj, h_mix0 = _norm_inproj(x0, gain(norm_mix_pre, 0, token), g_in_e, "inproj_even")
    mixed, pre_a, states = _hgrn2_fwd(proj, lb_table, a_norm, "hgrn2_fwd")
    mixed = _gmlp_fwd(proj, mixed, b_ln_g, b_ln_b, ws, bias_t, "gmlp_fwd")
    w_out_e = full(get_w(["out_e"], mixed)[0])
    x1, mix0 = _outproj([mixed], w_out_e, x0, gain(norm_mix_post, 0, None), "outproj_even")
    w1_0, w2_0 = get_w(["ff1_0", "ff2_0"], x1)
    w2_0 = full(w2_0)
    x2, y0, h_ffn0, r0 = _ffn_fwd(x1, gain(norm_ffn_pre, 0, None), w1_0, w2_0, gain(norm_ffn_post, 0, None), "ffn_fwd_0")
    (g_in_o,) = get_w(["in_o"], x2)
    *qkv, h_mix1 = _norm_inproj_rope(x2, gain(norm_mix_pre, 1, None), g_in_o, rope, "inproj_odd")
    branches = [_attn_branch_fwd(a, "attn_fwd_d%d" % d) for a, d in zip(qkv, C_DILATIONS)]
    attn, attn_b, lse = _attn_merge(branches, "attn_merge")
    w_out_o = full(get_w(["out_o"], attn_b)[0])
    x3, mix1 = _outproj([attn_b], w_out_o, x2, gain(norm_mix_post, 1, None), "outproj_odd")
    w1_1, w2_1 = get_w(["ff1_1", "ff2_1"], x3)
    w2_1 = full(w2_1)
    dx4, y1, h_ffn1, r1, loss_part = _ffn_fwd(x3, gain(norm_ffn_pre, 1, None), w1_1, w2_1, gain(norm_ffn_post, 1, None),
                                              "ffn_fwd_1", target)

    dx3, dy1, da1, dg_ffn_pre1, dg_ffn_post1 = _ffn_bwd(
        dx4, x3, y1, r1, gain(norm_ffn_pre, 1, None), w1_1, w2_1, gain(norm_ffn_post, 1, None), "ffn_bwd_1")
    gw_ff1_1 = _grad_w(h_ffn1, da1, True, "grad_w_ff1_1")
    gw_ff2_1 = _grad_w(r1, dy1, False, "grad_w_ff2_1")
    tok = put_g("ffn1", dict(ff1_1=gw_ff1_1, ff2_1=owners(gw_ff2_1)))
    *dattn, delta, dz1, dg_mix_post1 = _outproj_bwd_attn(dx3, mix1, gain(norm_mix_post, 1, tok), w_out_o, attn,
                                                  "outproj_bwd_odd")
    gw_out_o = _grad_w(attn_b, dz1, False, "grad_w_out_odd")
    per_seq = lambda a: a.reshape(-1, SEQ, LANES)
    grads_c = [_attn_branch_bwd(qkv[b], dattn[b], per_seq(lse), per_seq(delta), "attn_bwd_d%d" % d)
               for b, d in enumerate(C_DILATIONS)]
    dqkv = _attn_combine_bwd(grads_c, rope, "attn_combine_bwd")
    gw_in_o = _grad_w(h_mix1, dqkv, True, "grad_w_in_odd")
    tok = put_g("mix1", dict(out_o=owners(gw_out_o), in_o=gw_in_o))
    dx2, dg_mix_pre1 = _inproj_bwd(dqkv, g_in_o, dx3, x2, gain(norm_mix_pre, 1, tok), "inproj_bwd_odd")

    dx1, dy0, da0, dg_ffn_pre0, dg_ffn_post0 = _ffn_bwd(
        dx2, x1, y0, r0, gain(norm_ffn_pre, 0, None), w1_0, w2_0, gain(norm_ffn_post, 0, None), "ffn_bwd_0")
    gw_ff1_0 = _grad_w(h_ffn0, da0, True, "grad_w_ff1_0")
    gw_ff2_0 = _grad_w(r0, dy0, False, "grad_w_ff2_0")
    tok = put_g("ffn0", dict(ff1_0=gw_ff1_0, ff2_0=owners(gw_ff2_0)))
    dcat, dz0, dg_mix_post0 = _outproj_bwd(dx1, mix0, gain(norm_mix_post, 0, tok), w_out_e, "outproj_bwd_even")
    gw_out_e = _grad_w(mixed, dz0, False, "grad_w_out_even")
    dproj, d_lb, d_a_norm = _hgrn2_bwd(proj, dcat, pre_a, states, lb_table, a_norm, "hgrn2_bwd")
    dproj, d_ln_g, d_ln_b, d_ws, d_bias_t = _gmlp_bwd(proj, dcat, dproj, b_ln_g, b_ln_b, ws, bias_t, "gmlp_bwd")
    dx0, dg_mix_pre0 = _inproj_bwd(dproj, g_in_e, dx1, x0, gain(norm_mix_pre, 0, None), "inproj_bwd_even")
    layers = lambda a, b: jnp.concatenate([a, b], axis=0)
    tok = put_small((loss_part, layers(dg_mix_pre0, dg_mix_pre1), layers(dg_mix_post0, dg_mix_post1),
                     layers(dg_ffn_pre0, dg_ffn_pre1), layers(dg_ffn_post0, dg_ffn_post1),
                     d_lb, d_a_norm, d_ln_g, d_ln_b, d_ws, d_bias_t))
    gw_in_e = _grad_w(h_mix0, dproj, True, "grad_w_in_even", after=tok)
    put_g("mix0", dict(out_e=owners(gw_out_e), in_e=gw_in_e))
    return dx0
```

```python
import math

import jax
import jax.numpy as jnp
from jax import lax
from jax.experimental import pallas as pl
from jax.experimental.pallas import tpu as pltpu

F32 = jnp.float32
BF16 = jnp.bfloat16
MESH = pl.DeviceIdType.MESH

N_DEV = 8
D_MODEL = 1024
SEQ = 2048
EPS = 1e-6
A_WIDTH = 512
A_HEADS = 4
HEAD_A = 128
B_WIDTH = 512
B_GROUPS = 4
B_CHUNK = 128
C_HEADS = 16
C_HEAD_DIM = 64
C_ROT_HALF = 8
ROPE_THETA = 500000.0
C_DILATIONS = (1, 4, 16)
C_BLOCK = 128
D_FF = 4096
EVEN_IN = 3072
ODD_IN = 3072

ADAM_LR = 0.001
ADAM_B1 = 0.9
ADAM_B2 = 0.999
ADAM_EPS = 1e-08
ADAM_WD = 0.01
ADAM_STEP = 10

LANES = 128
SUBLANES = 8
ROW_TILE = 512
PROJ_TILE = 1024
PROJ_COLS = 768
MERGE_TILE = 256
SUB_CHUNK = 16
HGRN_BLOCK = 256
NEG = -1e30
VMEM_LIMIT = 56 * 1024 * 1024


def _params(sem):
    return pltpu.CompilerParams(dimension_semantics=sem, vmem_limit_bytes=VMEM_LIMIT)


def _dot(a, b):
    return jnp.dot(a, b, preferred_element_type=F32)


def _dot_nt(a, b):
    return lax.dot_general(a, b, (((1,), (1,)), ((), ())), preferred_element_type=F32)


def _dot_tn(a, b):
    return lax.dot_general(a, b, (((0,), (0,)), ((), ())), preferred_element_type=F32)


def _rms(x, g):
    r = lax.rsqrt(jnp.mean(x * x, axis=-1, keepdims=True) + EPS)
    return x * r * g


def _rms_bwd(x, g, dy):
    r = lax.rsqrt(jnp.mean(x * x, axis=-1, keepdims=True) + EPS)
    dyg = dy * g
    dx = r * dyg - x * (r * r * r) * jnp.mean(x * dyg, axis=-1, keepdims=True)
    return dx, dy * x * r


def _split3(x):
    hi = x.astype(BF16)
    rest = x - hi.astype(F32)
    mid = rest.astype(BF16)
    return hi, mid, (rest - mid.astype(F32)).astype(BF16)


def _mask_dot(mask, x):
    m = mask.astype(BF16)
    hi, mid, lo = _split3(x)
    return _dot(m, hi) + (_dot(m, mid) + _dot(m, lo))


def _dot_mask(x, mask):
    m = mask.astype(BF16)
    hi, mid, lo = _split3(x)
    return _dot(hi, m) + (_dot(mid, m) + _dot(lo, m))


def _rows8(v):
    return v.reshape(v.shape[0] // SUBLANES, SUBLANES, v.shape[1]).sum(axis=0)


def _sigmoid(x):
    return 1.0 / (1.0 + jnp.exp(-x))


def _gelu(x):
    return 0.5 * x * (1.0 + jnp.tanh(math.sqrt(2.0 / math.pi) * (x + 0.044715 * (x * x * x))))


def _acc_rows8(ref, val, first):
    @pl.when(first)
    def _():
        ref[...] = val

    @pl.when(jnp.logical_not(first))
    def _():
        ref[...] += val


def _my_slot():
    return 4 * lax.axis_index("x") + 2 * lax.axis_index("y") + lax.axis_index("c")


def _peer(r):
    x, y, c = lax.axis_index("x"), lax.axis_index("y"), lax.axis_index("c")
    px = 1 - x if (r >> 2) & 1 else x
    py = 1 - y if (r >> 1) & 1 else y
    pc = 1 - c if r & 1 else c
    return (px, py, pc), 4 * px + 2 * py + pc


HBM_SPEC = pl.BlockSpec(memory_space=pltpu.HBM)
SEM_SPEC = pl.BlockSpec(memory_space=pltpu.SEMAPHORE)
SPLIT_EFFECT = pltpu.SideEffectType.DATAFLOW_SIDE_EFFECTING


def _split_copies(land_ref, src_ref, send_sem, recv_sem):
    me = _my_slot()
    copies = []
    for r in range(1, N_DEV):
        peer, slot = _peer(r)
        src = _slot(land_ref, me) if src_ref is None else _slot(src_ref, slot)
        copies.append(pltpu.make_async_remote_copy(
            src_ref=src, dst_ref=_slot(land_ref, me), send_sem=send_sem, recv_sem=recv_sem,
            device_id=peer, device_id_type=MESH))
    return copies


def _slot(ref, s):
    if len(ref.shape) == 2:
        c = ref.shape[1] // N_DEV
        return ref.at[:, pl.ds(pl.multiple_of(s * c, LANES), c)]
    return ref.at[s]


def _exchange_start(lands, sources, name):
    n = len(lands)
    given = [s for s in sources if s is not None]
    arrays = list(lands) + given

    def body(*refs):
        land_refs, src_refs = refs[:n], list(refs[n:n + len(given)])
        sems = refs[len(arrays):len(arrays) + 2 * n]
        token = refs[-1]
        for k in range(n):
            src_ref = None if sources[k] is None else src_refs.pop(0)
            for copy in _split_copies(land_refs[k], src_ref, sems[k], sems[n + k]):
                copy.start()
        token[...] = jnp.zeros_like(token)

    outs = pl.pallas_call(
        body, name=name,
        out_shape=(pltpu.SemaphoreType.DMA(()),) * (2 * n) + tuple(pltpu.HBM(a.shape, a.dtype) for a in arrays)
        + (jax.ShapeDtypeStruct((SUBLANES, LANES), F32),),
        in_specs=[HBM_SPEC] * len(arrays),
        out_specs=(SEM_SPEC,) * (2 * n) + (HBM_SPEC,) * len(arrays) + (pl.BlockSpec(memory_space=pltpu.VMEM),),
        input_output_aliases={i: 2 * n + i for i in range(len(arrays))},
        compiler_params=pltpu.CompilerParams(has_side_effects=SPLIT_EFFECT),
    )(*[pltpu.with_memory_space_constraint(a, pltpu.HBM) for a in arrays])
    return list(outs[:n]), list(outs[n:2 * n]), list(outs[2 * n:3 * n]), list(outs[3 * n:-1]), outs[-1]


def _exchange_wait(lands, sources, send_sems, recv_sems, after, name):
    n = len(lands)
    given = [s for s in sources if s is not None]
    arrays = list(lands) + given

    def body(*refs):
        land_refs, src_refs = refs[:n], list(refs[n:n + len(given)])
        sems = refs[len(arrays):len(arrays) + 2 * n]
        for i in range(n):
            src_ref = None if sources[i] is None else src_refs.pop(0)
            copies = _split_copies(land_refs[i], src_ref, sems[i], sems[n + i])
            for copy in copies:
                copy.wait_recv()
            for copy in copies:
                copy.wait_send()

    outs = pl.pallas_call(
        body, name=name, out_shape=tuple(pltpu.HBM(a.shape, a.dtype) for a in arrays),
        in_specs=[HBM_SPEC] * len(arrays) + [SEM_SPEC] * (2 * n) + [pl.BlockSpec(memory_space=pl.ANY)],
        out_specs=(HBM_SPEC,) * len(arrays),
        input_output_aliases={i: i for i in range(len(arrays))},
        compiler_params=pltpu.CompilerParams(has_side_effects=SPLIT_EFFECT),
    )(*arrays, *send_sems, *recv_sems, after)
    return list(outs[:n])


def _place_own(a, me, name, own_block, dtype=BF16, after=None):
    shape = a.shape[1:] if own_block else a.shape
    cols = shape[-1]
    a3 = a.reshape((N_DEV if own_block else 1, -1, cols))
    rows = a3.shape[1]
    tr = min(rows, 512)

    def body(me_ref, a_ref, _, o_ref):
        o_ref[...] = a_ref[...].astype(dtype)

    grid_spec = pltpu.PrefetchScalarGridSpec(
        num_scalar_prefetch=1, grid=(rows // tr,),
        in_specs=[pl.BlockSpec((1, tr, cols), lambda i, me_ref: (me_ref[0] if own_block else 0, i, 0)),
                  pl.BlockSpec(memory_space=pl.ANY)],
        out_specs=pl.BlockSpec((1, tr, cols), lambda i, me_ref: (me_ref[0], i, 0)))
    out = pl.pallas_call(
        body, name=name, grid_spec=grid_spec, out_shape=jax.ShapeDtypeStruct((N_DEV, rows, cols), dtype),
        compiler_params=_params(("arbitrary",)),
    )(me, a3, a3 if after is None else after)
    return out.reshape((N_DEV,) + shape)


def _place_own_columns(a, me, name, after=None):
    rows, cols = a.shape
    tr = min(rows, 512)

    def body(me_ref, a_ref, _, o_ref):
        o_ref[...] = a_ref[...].astype(BF16)

    grid_spec = pltpu.PrefetchScalarGridSpec(
        num_scalar_prefetch=1, grid=(rows // tr,),
        in_specs=[pl.BlockSpec((tr, cols), lambda i, me_ref: (i, 0)), pl.BlockSpec(memory_space=pl.ANY)],
        out_specs=pl.BlockSpec((tr, cols), lambda i, me_ref: (i, me_ref[0])))
    return pl.pallas_call(
        body, name=name, grid_spec=grid_spec, out_shape=jax.ShapeDtypeStruct((rows, N_DEV * cols), BF16),
        compiler_params=_params(("arbitrary",)),
    )(me, a, a if after is None else after)


def _rope_tables(positions):
    in_head = jnp.arange(LANES) % C_HEAD_DIM
    inv = ROPE_THETA ** (-(in_head % C_ROT_HALF).astype(F32) / C_ROT_HALF)
    ang = positions.reshape(-1)[:, None].astype(F32) * inv
    rotated = in_head < 2 * C_ROT_HALF
    sin = jnp.sin(ang)
    return (jnp.where(rotated, jnp.cos(ang), 1.0),
            jnp.where(in_head < C_ROT_HALF, -sin, jnp.where(rotated, sin, 0.0)))


def _swap_halves(x):
    lane = lax.broadcasted_iota(jnp.int32, x.shape, 1) % C_HEAD_DIM
    return jnp.where(lane < C_ROT_HALF, pltpu.roll(x, LANES - C_ROT_HALF, 1), pltpu.roll(x, C_ROT_HALF, 1))


def _norm_inproj(x, g, w, name):
    t = x.shape[0]
    n = w.shape[1]
    tm, tn = PROJ_TILE, PROJ_COLS

    def body(x_ref, g_ref, w_ref, o_ref, h_ref):
        @pl.when(pl.program_id(1) == 0)
        def _():
            h_ref[...] = _rms(x_ref[...], g_ref[...]).astype(BF16)

        o_ref[...] = _dot(h_ref[...], w_ref[...])

    return pl.pallas_call(
        body, name=name, grid=(t // tm, n // tn),
        in_specs=[pl.BlockSpec((tm, D_MODEL), lambda i, j: (i, 0)), pl.BlockSpec((1, D_MODEL), lambda i, j: (0, 0)),
                  pl.BlockSpec((D_MODEL, tn), lambda i, j: (0, j))],
        out_specs=[pl.BlockSpec((tm, tn), lambda i, j: (i, j)), pl.BlockSpec((tm, D_MODEL), lambda i, j: (i, 0))],
        out_shape=[jax.ShapeDtypeStruct((t, n), F32), jax.ShapeDtypeStruct((t, D_MODEL), BF16)],
        compiler_params=_params(("parallel", "arbitrary")),
    )(x, g, w)


def _dilated_specs(tm, width, col_of):
    per_seq = SEQ // tm
    specs = []
    for d in C_DILATIONS:
        specs.append(pl.BlockSpec(
            (1, d, tm // d, width), lambda i, *rest: (i // per_seq, 0, i % per_seq, col_of(*rest))))
    return specs


def _dilated_shapes(n_seq, cols, dtype):
    return [jax.ShapeDtypeStruct((n_seq, d, SEQ // d, cols), dtype) for d in C_DILATIONS]


def _store_dilated(src_ref, out_refs, dtype):
    groups, tm, _ = src_ref.shape
    for d, o_ref in zip(C_DILATIONS, out_refs):
        for r in range(d):
            rows = pl.ds(r, tm // d, stride=d) if d > 1 else slice(None)
            for p in range(groups):
                o_ref[0, r, :, p * LANES:(p + 1) * LANES] = src_ref.at[p][rows, :].astype(dtype)


def _load_dilated(in_ref, d, dst_ref):
    groups, tm, _ = dst_ref.shape
    for r in range(d):
        rows = pl.ds(r, tm // d, stride=d)
        for p in range(groups):
            dst_ref.at[p][rows, :] = in_ref[0, r, :, p * LANES:(p + 1) * LANES].astype(F32)


def _norm_inproj_rope(x, g, w, rope, name):
    t = x.shape[0]
    n = w.shape[1]
    tm, nb = PROJ_TILE, PROJ_COLS

    def body(x_ref, g_ref, w_ref, c_ref, s_ref, o1_ref, o4_ref, o16_ref, h_ref, tile_ref):
        j = pl.program_id(1)

        @pl.when(j == 0)
        def _():
            h_ref[...] = _rms(x_ref[...], g_ref[...]).astype(BF16)

        acc = _dot(h_ref[...], w_ref[...])
        for p in range(nb // LANES):
            blk = acc[:, p * LANES:(p + 1) * LANES]
            roped = blk * c_ref[...] + _swap_halves(blk) * s_ref[...]
            piece = j * (nb // LANES) + p
            is_qk = piece < 2 * (D_MODEL // LANES)
            tile_ref[p] = jnp.where(is_qk, roped, blk) * jnp.where(piece < D_MODEL // LANES, QK_SCALE, 1.0)
        _store_dilated(tile_ref, (o1_ref, o4_ref, o16_ref), BF16)

    return pl.pallas_call(
        body, name=name, grid=(t // tm, n // nb),
        in_specs=[pl.BlockSpec((tm, D_MODEL), lambda i, j: (i, 0)), pl.BlockSpec((1, D_MODEL), lambda i, j: (0, 0)),
                  pl.BlockSpec((D_MODEL, nb), lambda i, j: (0, j)),
                  pl.BlockSpec((tm, LANES), lambda i, j: (i, 0)), pl.BlockSpec((tm, LANES), lambda i, j: (i, 0))],
        out_specs=_dilated_specs(tm, nb, lambda j: j) + [pl.BlockSpec((tm, D_MODEL), lambda i, j: (i, 0))],
        out_shape=_dilated_shapes(t // SEQ, n, BF16) + [jax.ShapeDtypeStruct((t, D_MODEL), BF16)],
        scratch_shapes=[pltpu.VMEM((nb // LANES, tm, LANES), F32)],
        compiler_params=_params(("parallel", "arbitrary")),
    )(x, g, w, *rope)


def _outproj(parts, w, x, g, name):
    t = x.shape[0]
    tm = PROJ_TILE
    n = len(parts)
    widths = [p.shape[1] for p in parts]

    def body(*refs):
        p_refs = refs[:n]
        w_ref, x_ref, g_ref, xo_ref, mix_ref = refs[n:]
        mix = None
        off = 0
        for p_ref, wd in zip(p_refs, widths):
            term = _dot(p_ref[...].astype(BF16), w_ref[off:off + wd, :])
            mix = term if mix is None else mix + term
            off += wd
        mix_ref[...] = mix
        xo_ref[...] = x_ref[...] + _rms(mix, g_ref[...])

    row = lambda i: (i, 0)
    return pl.pallas_call(
        body, name=name, grid=(t // tm,),
        in_specs=[pl.BlockSpec((tm, wd), row) for wd in widths] + [
            pl.BlockSpec((sum(widths), D_MODEL), lambda i: (0, 0)),
            pl.BlockSpec((tm, D_MODEL), row), pl.BlockSpec((1, D_MODEL), lambda i: (0, 0))],
        out_specs=[pl.BlockSpec((tm, D_MODEL), row)] * 2,
        out_shape=[jax.ShapeDtypeStruct((t, D_MODEL), F32)] * 2,
        compiler_params=_params(("parallel",)),
    )(*parts, w, x, g)


def _outproj_bwd(dx, mix, g, w, name):
    t = dx.shape[0]
    tm = PROJ_TILE
    k = w.shape[0]

    def body(dx_ref, mix_ref, g_ref, w_ref, dcat_ref, dz_ref, dg_ref):
        dz, dgr = _rms_bwd(mix_ref[...], g_ref[...], dx_ref[...])
        dzb = dz.astype(BF16)
        dz_ref[...] = dzb
        dcat_ref[...] = _dot_nt(dzb, w_ref[...])
        _acc_rows8(dg_ref, _rows8(dgr), pl.program_id(0) == 0)

    row = lambda i: (i, 0)
    return pl.pallas_call(
        body, name=name, grid=(t // tm,),
        in_specs=[pl.BlockSpec((tm, D_MODEL), row), pl.BlockSpec((tm, D_MODEL), row),
                  pl.BlockSpec((1, D_MODEL), lambda i: (0, 0)), pl.BlockSpec((k, D_MODEL), lambda i: (0, 0))],
        out_specs=[pl.BlockSpec((tm, k), row), pl.BlockSpec((tm, D_MODEL), row),
                   pl.BlockSpec((SUBLANES, D_MODEL), lambda i: (0, 0))],
        out_shape=[jax.ShapeDtypeStruct((t, k), F32), jax.ShapeDtypeStruct((t, D_MODEL), BF16),
                   jax.ShapeDtypeStruct((SUBLANES, D_MODEL), F32)],
        compiler_params=_params(("arbitrary",)),
    )(dx, mix, g, w)


def _outproj_bwd_attn(dx, mix, g, w, out, name):
    t = dx.shape[0]
    tm = MERGE_TILE

    def body(dx_ref, mix_ref, g_ref, w_ref, out_ref, do1, do4, do16, dl_ref, dz_ref, dg_ref, tile_ref):
        dz, dgr = _rms_bwd(mix_ref[...], g_ref[...], dx_ref[...])
        dzb = dz.astype(BF16)
        dz_ref[...] = dzb
        _acc_rows8(dg_ref, _rows8(dgr), pl.program_id(0) == 0)
        dout = _dot_nt(dzb, w_ref[...])
        for p in range(LANE_GROUPS):
            tile_ref[p] = dout[:, p * LANES:(p + 1) * LANES]
        _store_dilated(tile_ref, (do1, do4, do16), BF16)
        column = lax.broadcasted_iota(jnp.int32, (D_MODEL, LANES), 0) // C_HEAD_DIM
        head = lax.broadcasted_iota(jnp.int32, (D_MODEL, LANES), 1)
        dl_ref[...] = _dot_mask(dout * out_ref[...], column == head)

    row = lambda i: (i, 0)
    n_seq = t // SEQ
    return pl.pallas_call(
        body, name=name, grid=(t // tm,),
        in_specs=[pl.BlockSpec((tm, D_MODEL), row), pl.BlockSpec((tm, D_MODEL), row),
                  pl.BlockSpec((1, D_MODEL), lambda i: (0, 0)), pl.BlockSpec((D_MODEL, D_MODEL), lambda i: (0, 0)),
                  pl.BlockSpec((tm, D_MODEL), row)],
        out_specs=_dilated_specs(tm, D_MODEL, lambda: 0) + [
            pl.BlockSpec((tm, LANES), row), pl.BlockSpec((tm, D_MODEL), row),
            pl.BlockSpec((SUBLANES, D_MODEL), lambda i: (0, 0))],
        out_shape=_dilated_shapes(n_seq, D_MODEL, BF16) + [
            jax.ShapeDtypeStruct((t, LANES), F32), jax.ShapeDtypeStruct((t, D_MODEL), BF16),
            jax.ShapeDtypeStruct((SUBLANES, D_MODEL), F32)],
        scratch_shapes=[pltpu.VMEM((LANE_GROUPS, tm, LANES), F32)],
        compiler_params=_params(("arbitrary",)),
    )(dx, mix, g, w, out)


def _inproj_bwd(dproj, w, dx, x, g, name):
    t = x.shape[0]
    n = w.shape[1]
    tm = ROW_TILE

    def body(dp_ref, w_ref, dx_ref, x_ref, g_ref, o_ref, dg_ref):
        dxn, dgr = _rms_bwd(x_ref[...], g_ref[...], _dot_nt(dp_ref[...], w_ref[...]))
        o_ref[...] = dx_ref[...] + dxn
        _acc_rows8(dg_ref, _rows8(dgr), pl.program_id(0) == 0)

    row = lambda i: (i, 0)
    return pl.pallas_call(
        body, name=name, grid=(t // tm,),
        in_specs=[pl.BlockSpec((tm, n), row), pl.BlockSpec((D_MODEL, n), lambda i: (0, 0)),
                  pl.BlockSpec((tm, D_MODEL), row), pl.BlockSpec((tm, D_MODEL), row),
                  pl.BlockSpec((1, D_MODEL), lambda i: (0, 0))],
        out_specs=[pl.BlockSpec((tm, D_MODEL), row), pl.BlockSpec((SUBLANES, D_MODEL), lambda i: (0, 0))],
        out_shape=[jax.ShapeDtypeStruct((t, D_MODEL), F32), jax.ShapeDtypeStruct((SUBLANES, D_MODEL), F32)],
        compiler_params=_params(("arbitrary",)),
    )(dproj, w, dx, x, g)


def _grad_w(a, b, col_blocks, name):
    t, k = a.shape
    n = b.shape[1]
    tk = min(k, 1024)
    per_owner = n // N_DEV
    tn = 2 * per_owner if col_blocks else min(n, 1024)

    def body(a_ref, b_ref, o_ref, at_ref):
        @pl.when(pl.program_id(1) == 0)
        def _():
            for c in range(t // ROW_TILE):
                rows = slice(c * ROW_TILE, (c + 1) * ROW_TILE)
                at_ref[:, rows] = a_ref[rows, :].T

        res = _dot(at_ref[...], b_ref[...]).astype(BF16)
        if col_blocks:
            o_ref[0] = res[:, :per_owner]
            o_ref[1] = res[:, per_owner:]
        else:
            o_ref[...] = res

    if col_blocks:
        out_spec = pl.BlockSpec((2, tk, per_owner), lambda i, j: (j, i, 0))
        out_shape = jax.ShapeDtypeStruct((N_DEV, k, per_owner), BF16)
    else:
        out_spec = pl.BlockSpec((tk, tn), lambda i, j: (i, j))
        out_shape = jax.ShapeDtypeStruct((k, n), BF16)
    return pl.pallas_call(
        body, name=name, grid=(k // tk, n // tn),
        in_specs=[pl.BlockSpec((t, tk), lambda i, j: (0, i)), pl.BlockSpec((t, tn), lambda i, j: (0, j))],
        out_specs=out_spec, out_shape=out_shape,
        scratch_shapes=[pltpu.VMEM((tk, t), BF16)],
        compiler_params=_params(("parallel", "arbitrary")),
    )(a, b)


FF_STEP = 1024
FF_STEPS = D_FF // FF_STEP


def _ffn_fwd(x, g_pre, w1, w2, g_post, name, target=None):
    t = x.shape[0]
    tm = PROJ_TILE

    def body(*refs):
        if target is None:
            x_ref, gp_ref, w1_ref, w2_ref, gq_ref, xo_ref, y_ref, h_ref, r_ref = refs
        else:
            x_ref, gp_ref, w1_ref, w2_ref, gq_ref, t_ref, xo_ref, y_ref, h_ref, r_ref, l_ref = refs
        i, j = pl.program_id(0), pl.program_id(1)

        @pl.when(j == 0)
        def _():
            h_ref[...] = _rms(x_ref[...], gp_ref[...]).astype(BF16)

        a = _dot(h_ref[...], w1_ref[...])
        r = jnp.square(jnp.maximum(a, 0.0)).astype(BF16)
        r_ref[...] = r
        term = _dot(r, w2_ref[...])

        @pl.when(j == 0)
        def _():
            y_ref[...] = term

        @pl.when(j > 0)
        def _():
            y_ref[...] += term

        @pl.when(j == FF_STEPS - 1)
        def _():
            x_new = x_ref[...] + _rms(y_ref[...], gq_ref[...])
            if target is None:
                xo_ref[...] = x_new
            else:
                diff = x_new - t_ref[...]
                xo_ref[...] = diff * (1.0 / D_MODEL)
                _acc_rows8(l_ref, _rows8(diff * diff) * (0.5 / D_MODEL), i == 0)

    row = lambda i, j: (i, 0)
    vec = pl.BlockSpec((1, D_MODEL), lambda i, j: (0, 0))
    in_specs = [pl.BlockSpec((tm, D_MODEL), row), vec, pl.BlockSpec((D_MODEL, FF_STEP), lambda i, j: (0, j)),
                pl.BlockSpec((FF_STEP, D_MODEL), lambda i, j: (j, 0)), vec]
    out_specs = [pl.BlockSpec((tm, D_MODEL), row)] * 3 + [pl.BlockSpec((tm, FF_STEP), lambda i, j: (i, j))]
    out_shape = [jax.ShapeDtypeStruct((t, D_MODEL), F32), jax.ShapeDtypeStruct((t, D_MODEL), F32),
                 jax.ShapeDtypeStruct((t, D_MODEL), BF16), jax.ShapeDtypeStruct((t, D_FF), BF16)]
    args = [x, g_pre, w1, w2, g_post]
    if target is not None:
        in_specs.append(pl.BlockSpec((tm, D_MODEL), row))
        out_specs.append(pl.BlockSpec((SUBLANES, D_MODEL), lambda i, j: (0, 0)))
        out_shape.append(jax.ShapeDtypeStruct((SUBLANES, D_MODEL), F32))
        args.append(target)
    return pl.pallas_call(
        body, name=name, grid=(t // tm, FF_STEPS), in_specs=in_specs, out_specs=out_specs, out_shape=out_shape,
        compiler_params=_params(("parallel" if target is None else "arbitrary", "arbitrary")),
    )(*args)


def _ffn_bwd(dxo, x, y, r, g_pre, w1, w2, g_post, name):
    t = x.shape[0]
    tm = ROW_TILE

    def body(dxo_ref, x_ref, y_ref, r_ref, gp_ref, w1_ref, w2_ref, gq_ref,
             dx_ref, dy_ref, da_ref, dgp_ref, dgq_ref, acc_ref):
        i, j = pl.program_id(0), pl.program_id(1)

        @pl.when(j == 0)
        def _():
            dy, dgr = _rms_bwd(y_ref[...], gq_ref[...], dxo_ref[...])
            dy_ref[...] = dy.astype(BF16)
            _acc_rows8(dgq_ref, _rows8(dgr), i == 0)

        dr = _dot_nt(dy_ref[...], w2_ref[...])
        da = (dr * (2.0 * jnp.sqrt(r_ref[...].astype(F32)))).astype(BF16)
        da_ref[...] = da
        term = _dot_nt(da, w1_ref[...])

        @pl.when(j == 0)
        def _():
            acc_ref[...] = term

        @pl.when(j > 0)
        def _():
            acc_ref[...] += term

        @pl.when(j == FF_STEPS - 1)
        def _():
            dxn, dgr = _rms_bwd(x_ref[...], gp_ref[...], acc_ref[...])
            dx_ref[...] = dxo_ref[...] + dxn
            _acc_rows8(dgp_ref, _rows8(dgr), i == 0)

    row = lambda i, j: (i, 0)
    vec = pl.BlockSpec((1, D_MODEL), lambda i, j: (0, 0))
    acc8 = pl.BlockSpec((SUBLANES, D_MODEL), lambda i, j: (0, 0))
    return pl.pallas_call(
        body, name=name, grid=(t // tm, FF_STEPS),
        in_specs=[pl.BlockSpec((tm, D_MODEL), row)] * 3 + [
            pl.BlockSpec((tm, FF_STEP), lambda i, j: (i, j)),
            vec, pl.BlockSpec((D_MODEL, FF_STEP), lambda i, j: (0, j)),
            pl.BlockSpec((FF_STEP, D_MODEL), lambda i, j: (j, 0)), vec],
        out_specs=[pl.BlockSpec((tm, D_MODEL), row), pl.BlockSpec((tm, D_MODEL), row),
                   pl.BlockSpec((tm, FF_STEP), lambda i, j: (i, j)), acc8, acc8],
        out_shape=[jax.ShapeDtypeStruct((t, D_MODEL), F32), jax.ShapeDtypeStruct((t, D_MODEL), BF16),
                   jax.ShapeDtypeStruct((t, D_FF), BF16),
                   jax.ShapeDtypeStruct((SUBLANES, D_MODEL), F32), jax.ShapeDtypeStruct((SUBLANES, D_MODEL), F32)],
        scratch_shapes=[pltpu.VMEM((tm, D_MODEL), F32)],
        compiler_params=_params(("arbitrary", "arbitrary")),
    )(dxo, x, y, r, g_pre, w1, w2, g_post)


def _lower_bound(table):
    e = jnp.exp(table - jnp.max(table, axis=0, keepdims=True))
    return e[0:1, :] / jnp.sum(e, axis=0, keepdims=True)


def _hgrn2_block(q_ref, f_ref, lb):
    tb = f_ref.shape[0]
    sig = _sigmoid(f_ref[...])
    f = lb + (1.0 - lb) * sig
    qraw = q_ref[...]
    sq = _sigmoid(qraw)
    r = lax.broadcasted_iota(jnp.int32, (tb, tb), 0)
    c = lax.broadcasted_iota(jnp.int32, (tb, tb), 1)
    same = (r // SUB_CHUNK) == (c // SUB_CHUNK)
    logf = jnp.log(f)
    gsum = _mask_dot(same & (c <= r), logf)
    glast = _mask_dot(same, logf)
    return dict(sig=sig, f=f, kk=1.0 - f, qraw=qraw, sq=sq, qs=qraw * sq, gsum=gsum,
                eg=jnp.exp(gsum), ekd=jnp.exp(glast - gsum), a=jnp.exp(glast))


def _head_sums(x):
    parts = [jnp.broadcast_to(jnp.sum(x[:, h * HEAD_A:(h + 1) * HEAD_A], axis=1, keepdims=True), (x.shape[0], HEAD_A))
             for h in range(A_HEADS)]
    return jnp.concatenate(parts, axis=1)


def _hgrn2_intra(g, kk, qs, v):
    row = lax.broadcasted_iota(jnp.int32, g.shape, 0)
    o = _head_sums(qs * kk) * v
    for j in range(1, SUB_CHUNK):
        decay = jnp.exp(jnp.where(row >= j, g - pltpu.roll(g, j, 0), NEG))
        o = o + _head_sums(qs * pltpu.roll(kk, j, 0) * decay) * pltpu.roll(v, j, 0)
    return o


def _hgrn2_intra_bwd(g, kk, qs, v, do):
    row = lax.broadcasted_iota(jnp.int32, g.shape, 0)
    dsc = _head_sums(do * v)
    dqs, dkk, dv = dsc * kk, dsc * qs, _head_sums(qs * kk) * do
    for j in range(1, SUB_CHUNK):
        k_dn = pltpu.roll(kk, j, 0)
        decay = jnp.exp(jnp.where(row >= j, g - pltpu.roll(g, j, 0), NEG))
        d_score = _head_sums(do * pltpu.roll(v, j, 0)) * decay
        dqs = dqs + d_score * k_dn
        dkk = dkk + pltpu.roll(d_score * qs, SUB_CHUNK - j, 0)
        dv = dv + pltpu.roll(_head_sums(qs * k_dn * decay) * do, SUB_CHUNK - j, 0)
    return dqs, dkk, dv


def _hgrn2_fwd(proj, lb_table, a_norm, name):
    t = proj.shape[0]
    tb = HGRN_BLOCK
    n_tb = SEQ // tb
    n_seq = t // SEQ
    n_sub = tb // SUB_CHUNK

    def body(q_ref, f_ref, i_ref, g_ref, lbt_ref, an_ref, o_ref, pre_ref, sts_ref, st_ref,
             gs_ref, kk_ref, qs_ref, eg_ref, ekd_ref, a_ref):
        @pl.when(pl.program_id(1) == 0)
        def _():
            st_ref[...] = jnp.zeros_like(st_ref)

        an = an_ref[...]
        blk = _hgrn2_block(q_ref, f_ref, _lower_bound(lbt_ref[...]))
        for ref, key in ((gs_ref, "gsum"), (kk_ref, "kk"), (qs_ref, "qs"), (eg_ref, "eg"), (ekd_ref, "ekd"), (a_ref, "a")):
            ref[...] = blk[key]

        def step(c, carry):
            rows = pl.ds(pl.multiple_of(c * SUB_CHUNK, SUB_CHUNK), SUB_CHUNK)
            kk, qs, v = kk_ref[rows, :], qs_ref[rows, :], i_ref[rows, :]
            o = _hgrn2_intra(gs_ref[rows, :], kk, qs, v)
            qg, kd, vb = (qs * eg_ref[rows, :]).astype(BF16), (kk * ekd_ref[rows, :]).astype(BF16), v.astype(BF16)
            for h in range(A_HEADS):
                lanes = slice(h * HEAD_A, (h + 1) * HEAD_A)
                st = st_ref[h]
                sts_ref[0, c, h] = st
                o_h = o[:, lanes] + _dot_nt(qg[:, lanes], st.astype(BF16))
                st_ref[h] = st * a_ref[rows, lanes][0:1] + _dot_tn(vb[:, lanes], kd[:, lanes])
                pre_ref[rows, lanes] = o_h
                graw = g_ref[rows, lanes]
                o_ref[rows, lanes] = (_rms(o_h, an[:, lanes]) * (graw * _sigmoid(graw))).astype(BF16)
            return carry

        lax.fori_loop(0, n_sub, step, 0, unroll=2)

    def col(k):
        return pl.BlockSpec((tb, A_WIDTH), lambda b, s, k=k: (b * n_tb + s, k))

    out_rows = pl.BlockSpec((tb, A_WIDTH), lambda b, s: (b * n_tb + s, 0))
    return pl.pallas_call(
        body, name=name, grid=(n_seq, n_tb),
        in_specs=[col(0), col(1), col(2), col(3),
                  pl.BlockSpec((3, A_WIDTH), lambda b, s: (0, 0)), pl.BlockSpec((1, A_WIDTH), lambda b, s: (0, 0))],
        out_specs=[out_rows, out_rows,
                   pl.BlockSpec((1, n_sub, A_HEADS, HEAD_A, HEAD_A), lambda b, s: (b * n_tb + s, 0, 0, 0, 0))],
        out_shape=[jax.ShapeDtypeStruct((t, D_MODEL), BF16), jax.ShapeDtypeStruct((t, A_WIDTH), F32),
                   jax.ShapeDtypeStruct((n_seq * n_tb, n_sub, A_HEADS, HEAD_A, HEAD_A), F32)],
        scratch_shapes=[pltpu.VMEM((A_HEADS, HEAD_A, HEAD_A), F32)] + [pltpu.VMEM((tb, A_WIDTH), F32)] * 6,
        compiler_params=_params(("parallel", "arbitrary")),
    )(proj, proj, proj, proj, lb_table, a_norm)


def _hgrn2_bwd(proj, dcat, pre, states, lb_table, a_norm, name):
    t = proj.shape[0]
    tb = HGRN_BLOCK
    n_tb = SEQ // tb
    n_seq = t // SEQ
    n_sub = tb // SUB_CHUNK

    def body(q_ref, f_ref, i_ref, g_ref, do_ref, pre_ref, sts_ref, lbt_ref, an_ref, dp_ref, dlb_ref, dan_ref, dst_ref,
             gs_ref, kk_ref, qs_ref, eg_ref, ekd_ref, a_ref, dpre_ref, dlf_ref, dqs_ref, dkk_ref):
        b, s = pl.program_id(0), pl.program_id(1)

        @pl.when(s == 0)
        def _():
            dst_ref[...] = jnp.zeros_like(dst_ref)

        @pl.when((b == 0) & (s == 0))
        def _():
            dlb_ref[...] = jnp.zeros_like(dlb_ref)
            dan_ref[...] = jnp.zeros_like(dan_ref)

        lb = _lower_bound(lbt_ref[...])
        an = an_ref[...]
        heads = [slice(h * HEAD_A, (h + 1) * HEAD_A) for h in range(A_HEADS)]
        blk = _hgrn2_block(q_ref, f_ref, lb)
        for ref, key in ((gs_ref, "gsum"), (kk_ref, "kk"), (qs_ref, "qs"), (eg_ref, "eg"), (ekd_ref, "ekd"), (a_ref, "a")):
            ref[...] = blk[key]
        for h, lanes in enumerate(heads):
            graw, o = g_ref[:, lanes], pre_ref[:, lanes]
            sg = _sigmoid(graw)
            dout = do_ref[:, lanes]
            d_o, dgr = _rms_bwd(o, an[:, lanes], dout * (graw * sg))
            dan_ref[0:1, lanes] += jnp.sum(dgr, axis=0, keepdims=True)
            dp_ref[:, 3 * A_WIDTH + h * HEAD_A:3 * A_WIDTH + (h + 1) * HEAD_A] = (
                dout * _rms(o, an[:, lanes]) * (sg * (1.0 + graw * (1.0 - sg)))).astype(BF16)
            dpre_ref[:, lanes] = d_o

        tri_t = (lax.broadcasted_iota(jnp.int32, (SUB_CHUNK, SUB_CHUNK), 0)
                 <= lax.broadcasted_iota(jnp.int32, (SUB_CHUNK, SUB_CHUNK), 1)).astype(F32)

        def back(k, carry):
            c = n_sub - 1 - k
            rows = pl.ds(pl.multiple_of(c * SUB_CHUNK, SUB_CHUNK), SUB_CHUNK)
            g, kk, qs, v, d_o = gs_ref[rows, :], kk_ref[rows, :], qs_ref[rows, :], i_ref[rows, :], dpre_ref[rows, :]
            eg, ekd, a = eg_ref[rows, :], ekd_ref[rows, :], a_ref[rows, :]
            dqs, dkk, dv = _hgrn2_intra_bwd(g, kk, qs, v, d_o)
            qg_f, kd_f = qs * eg, kk * ekd
            qg, kd, vb, dob = qg_f.astype(BF16), kd_f.astype(BF16), v.astype(BF16), d_o.astype(BF16)
            dqg, dkd, da, dv_st = [], [], [], []
            for h, lanes in enumerate(heads):
                st, dst = sts_ref[0, c, h], dst_ref[h]
                dstb = dst.astype(BF16)
                dqg.append(_dot(dob[:, lanes], st.astype(BF16)))
                dv_st.append(_dot_nt(kd[:, lanes], dstb))
                dkd.append(_dot(vb[:, lanes], dstb))
                da.append(jnp.broadcast_to(jnp.sum(dst * st, axis=0, keepdims=True), (SUB_CHUNK, HEAD_A)))
                dst_ref[h] = dst * a[0:1, lanes] + _dot_tn(dob[:, lanes], qg[:, lanes])
            dqg, dkd, da, dv_st = [jnp.concatenate(p, axis=1) for p in (dqg, dkd, da, dv_st)]
            d_gsum = qs * dqs - kk * dkk + dqg * qg_f - dkd * kd_f
            d_glast = jnp.sum(dkd * kd_f, axis=0, keepdims=True) + da * a
            dlf_ref[rows, :] = jnp.dot(tri_t, d_gsum, precision=lax.Precision.HIGHEST,
                                       preferred_element_type=F32) + d_glast
            dqs_ref[rows, :] = dqs + dqg * eg
            dkk_ref[rows, :] = dkk + dkd * ekd
            dp_ref[rows, 2 * A_WIDTH:3 * A_WIDTH] = (dv + dv_st).astype(BF16)
            return carry

        lax.fori_loop(0, n_sub, back, 0, unroll=2)
        sig, sq, qraw = blk["sig"], blk["sq"], blk["qraw"]
        d_f = dlf_ref[...] / blk["f"] - dkk_ref[...]
        dlb_ref[0:1, :] += jnp.sum(d_f * (1.0 - sig), axis=0, keepdims=True)
        dp_ref[:, 0:A_WIDTH] = (dqs_ref[...] * (sq * (1.0 + qraw * (1.0 - sq)))).astype(BF16)
        dp_ref[:, A_WIDTH:2 * A_WIDTH] = (d_f * (1.0 - lb) * sig * (1.0 - sig)).astype(BF16)

    def rev(s):
        return n_tb - 1 - s

    def col(k):
        return pl.BlockSpec((tb, A_WIDTH), lambda b, s, k=k: (b * n_tb + rev(s), k))

    acc8 = pl.BlockSpec((SUBLANES, A_WIDTH), lambda b, s: (0, 0))
    return pl.pallas_call(
        body, name=name, grid=(n_seq, n_tb),
        in_specs=[col(0), col(1), col(2), col(3), col(0), col(0),
                  pl.BlockSpec((1, n_sub, A_HEADS, HEAD_A, HEAD_A), lambda b, s: (b * n_tb + rev(s), 0, 0, 0, 0)),
                  pl.BlockSpec((3, A_WIDTH), lambda b, s: (0, 0)), pl.BlockSpec((1, A_WIDTH), lambda b, s: (0, 0))],
        out_specs=[pl.BlockSpec((tb, 4 * A_WIDTH), lambda b, s: (b * n_tb + rev(s), 0)), acc8, acc8],
        out_shape=[jax.ShapeDtypeStruct((t, EVEN_IN), BF16)] + [jax.ShapeDtypeStruct((SUBLANES, A_WIDTH), F32)] * 2,
        scratch_shapes=[pltpu.VMEM((A_HEADS, HEAD_A, HEAD_A), F32)] + [pltpu.VMEM((tb, A_WIDTH), F32)] * 10,
        compiler_params=_params(("arbitrary", "arbitrary")),
    )(proj, proj, proj, proj, dcat, pre, states, lb_table, a_norm)


GMLP_ROWS = 512


def _gmlp_chunk(ub, vb, ln_g, ln_b, ws, bias):
    u = [_gelu(a) for a in ub]
    v = [_gelu(a) for a in vb]
    mu = sum(jnp.sum(a, axis=-1, keepdims=True) for a in v) * (1.0 / B_WIDTH)
    cen = [a - mu for a in v]
    var = sum(jnp.sum(a * a, axis=-1, keepdims=True) for a in cen) * (1.0 / B_WIDTH)
    inv = lax.rsqrt(var + EPS)
    r = lax.broadcasted_iota(jnp.int32, (B_CHUNK, B_CHUNK), 0)
    c = lax.broadcasted_iota(jnp.int32, (B_CHUNK, B_CHUNK), 1)
    outs = []
    for g in range(B_GROUPS):
        vn = (cen[g] * inv * ln_g[g] + ln_b[g]).astype(BF16)
        wm = jnp.where(c <= r, ws[g], 0.0).astype(BF16)
        outs.append(u[g] * (_dot(wm, vn) + bias[g]))
    return outs


def _lane_groups(ref, rows=slice(None)):
    return [ref[rows, g * LANES:(g + 1) * LANES] for g in range(B_GROUPS)]


def _gmlp_fwd(proj, mixed, ln_g, ln_b, ws, bias_t, name):
    t = proj.shape[0]
    tm = GMLP_ROWS

    def body(u_ref, v_ref, lg_ref, lb_ref, ws_ref, bt_ref, _, o_ref):
        for ch in range(tm // B_CHUNK):
            rows = slice(ch * B_CHUNK, (ch + 1) * B_CHUNK)
            outs = _gmlp_chunk(_lane_groups(u_ref, rows), _lane_groups(v_ref, rows), _lane_groups(lg_ref),
                               _lane_groups(lb_ref), [ws_ref[g] for g in range(B_GROUPS)],
                               [bt_ref[:, g:g + 1] for g in range(B_GROUPS)])
            for g in range(B_GROUPS):
                o_ref[rows, g * LANES:(g + 1) * LANES] = outs[g].astype(BF16)

    vec = pl.BlockSpec((1, B_WIDTH), lambda i: (0, 0))
    return pl.pallas_call(
        body, name=name, grid=(t // tm,),
        in_specs=[pl.BlockSpec((tm, B_WIDTH), lambda i: (i, 4)), pl.BlockSpec((tm, B_WIDTH), lambda i: (i, 5)), vec, vec,
                  pl.BlockSpec((B_GROUPS, B_CHUNK, B_CHUNK), lambda i: (0, 0, 0)),
                  pl.BlockSpec((B_CHUNK, B_GROUPS), lambda i: (0, 0)), pl.BlockSpec(memory_space=pl.ANY)],
        out_specs=pl.BlockSpec((tm, B_WIDTH), lambda i: (i, 1)),
        out_shape=jax.ShapeDtypeStruct(mixed.shape, BF16),
        input_output_aliases={6: 0},
        compiler_params=_params(("parallel",)),
    )(proj, proj, ln_g, ln_b, ws, bias_t, mixed)


def _gmlp_bwd(proj, dcat, dproj, ln_g, ln_b, ws, bias_t, name):
    t = proj.shape[0]
    tm = GMLP_ROWS

    def body(u_ref, v_ref, do_ref, lg_ref, lb_ref, ws_ref, bt_ref, _, duv_ref, dlg_ref, dlb_ref, dws_ref, dbt_ref):
        @pl.when(pl.program_id(0) == 0)
        def _():
            dlg_ref[...] = jnp.zeros_like(dlg_ref)
            dlb_ref[...] = jnp.zeros_like(dlb_ref)
            dws_ref[...] = jnp.zeros_like(dws_ref)
            dbt_ref[...] = jnp.zeros_like(dbt_ref)

        for ch in range(tm // B_CHUNK):
            rows = slice(ch * B_CHUNK, (ch + 1) * B_CHUNK)
            _, vjp = jax.vjp(
                _gmlp_chunk, _lane_groups(u_ref, rows), _lane_groups(v_ref, rows), _lane_groups(lg_ref),
                _lane_groups(lb_ref), [ws_ref[g] for g in range(B_GROUPS)],
                [bt_ref[:, g:g + 1] for g in range(B_GROUPS)])
            du, dv, dlg, dlb, dw, dbt = vjp(_lane_groups(do_ref, rows))
            for g in range(B_GROUPS):
                lanes = slice(g * LANES, (g + 1) * LANES)
                duv_ref[rows, lanes] = du[g].astype(BF16)
                duv_ref[rows, B_WIDTH + g * LANES:B_WIDTH + (g + 1) * LANES] = dv[g].astype(BF16)
                dlg_ref[0:1, lanes] += dlg[g]
                dlb_ref[0:1, lanes] += dlb[g]
                dws_ref[g] += dw[g]
                dbt_ref[:, g:g + 1] += dbt[g]

    vec = pl.BlockSpec((1, B_WIDTH), lambda i: (0, 0))
    acc8 = pl.BlockSpec((SUBLANES, B_WIDTH), lambda i: (0, 0))
    ws_spec = pl.BlockSpec((B_GROUPS, B_CHUNK, B_CHUNK), lambda i: (0, 0, 0))
    bt_spec = pl.BlockSpec((B_CHUNK, B_GROUPS), lambda i: (0, 0))
    return pl.pallas_call(
        body, name=name, grid=(t // tm,),
        in_specs=[pl.BlockSpec((tm, B_WIDTH), lambda i: (i, 4)), pl.BlockSpec((tm, B_WIDTH), lambda i: (i, 5)),
                  pl.BlockSpec((tm, B_WIDTH), lambda i: (i, 1)), vec, vec, ws_spec, bt_spec,
                  pl.BlockSpec(memory_space=pl.ANY)],
        out_specs=[pl.BlockSpec((tm, 2 * B_WIDTH), lambda i: (i, 2)), acc8, acc8, ws_spec, bt_spec],
        out_shape=[jax.ShapeDtypeStruct(dproj.shape, BF16), jax.ShapeDtypeStruct((SUBLANES, B_WIDTH), F32),
                   jax.ShapeDtypeStruct((SUBLANES, B_WIDTH), F32),
                   jax.ShapeDtypeStruct((B_GROUPS, B_CHUNK, B_CHUNK), F32),
                   jax.ShapeDtypeStruct((B_CHUNK, B_GROUPS), F32)],
        input_output_aliases={7: 0},
        compiler_params=_params(("arbitrary",)),
    )(proj, proj, dcat, ln_g, ln_b, ws, bias_t, dproj)


QK_SCALE = 1.0 / math.sqrt(C_HEAD_DIM)
ATTN_UNROLL = 16
LANE_GROUPS = D_MODEL // LANES
Q_BLOCKS = SEQ // C_BLOCK


def _attn_window(i, d):
    sub_blocks = Q_BLOCKS // d
    q0 = pl.multiple_of(i * C_BLOCK, C_BLOCK)
    k0 = pl.multiple_of(jnp.maximum(i - 1, 0) * C_BLOCK, C_BLOCK)
    key = k0 + lax.broadcasted_iota(jnp.int32, (C_BLOCK, 2 * C_BLOCK), 1)
    dist = (q0 + lax.broadcasted_iota(jnp.int32, (C_BLOCK, 2 * C_BLOCK), 0)) - key
    own_subsequence = (key >= q0) | (i % sub_blocks > 0)
    return pl.ds(q0, C_BLOCK), pl.ds(k0, 2 * C_BLOCK), (dist >= 0) & (dist <= C_BLOCK) & own_subsequence


def _head_masks():
    lane = lax.broadcasted_iota(jnp.int32, (C_BLOCK, LANES), 1)
    return [lane < C_HEAD_DIM, lane >= C_HEAD_DIM]


def _flat_spec(col_of):
    return pl.BlockSpec((1, SEQ, LANES), lambda b, g: (b, 0, col_of(g)))


def _put_heads(tile, g, col0, col1):
    lane = lax.broadcasted_iota(jnp.int32, tile.shape, 1)
    return jnp.where(lane == 2 * g, col0, jnp.where(lane == 2 * g + 1, col1, tile))


def _get_head(tile, h):
    lane = lax.broadcasted_iota(jnp.int32, tile.shape, 1)
    return jnp.sum(jnp.where(lane == h, tile, 0.0), axis=1, keepdims=True)


PER_HEAD_SPEC = pl.BlockSpec((1, SEQ, LANES), lambda b, g: (b, 0, 0))


def _attn_branch_fwd(qkv, name):
    n_seq, d, l, _ = qkv.shape
    flat = qkv.reshape(n_seq, SEQ, ODD_IN)

    def body(q_ref, k_ref, v_ref, o_ref, m_ref, l_ref):
        heads = _head_masks()
        g = pl.program_id(1)

        @pl.when(g == 0)
        def _():
            m_ref[...] = jnp.zeros_like(m_ref)
            l_ref[...] = jnp.zeros_like(l_ref)

        def block(i, carry):
            rows, keys, mask = _attn_window(i, d)
            q, k, v = q_ref[0, rows, :], k_ref[0, keys, :], v_ref[0, keys, :]
            res = []
            for hm in heads:
                s = jnp.where(mask, _dot_nt(jnp.where(hm, q, 0), k), NEG)
                m = jnp.max(s, axis=-1, keepdims=True)
                p = jnp.exp(s - m)
                res.append((_dot(p.astype(BF16), v), m, jnp.sum(p, axis=-1, keepdims=True)))
            o_ref[0, rows, :] = jnp.where(heads[0], res[0][0], res[1][0])
            m_ref[0, rows, :] = _put_heads(m_ref[0, rows, :], g, res[0][1], res[1][1])
            l_ref[0, rows, :] = _put_heads(l_ref[0, rows, :], g, res[0][2], res[1][2])
            return carry

        lax.fori_loop(0, Q_BLOCKS, block, 0, unroll=ATTN_UNROLL)

    o, m, l_sum = pl.pallas_call(
        body, name=name, grid=(n_seq, LANE_GROUPS),
        in_specs=[_flat_spec(lambda g: g), _flat_spec(lambda g: LANE_GROUPS + g),
                  _flat_spec(lambda g: 2 * LANE_GROUPS + g)],
        out_specs=[_flat_spec(lambda g: g), PER_HEAD_SPEC, PER_HEAD_SPEC],
        out_shape=[jax.ShapeDtypeStruct((n_seq, SEQ, D_MODEL), F32)] + [jax.ShapeDtypeStruct((n_seq, SEQ, LANES), F32)] * 2,
        compiler_params=_params(("parallel", "arbitrary")),
    )(flat, flat, flat)
    return [o.reshape(n_seq, d, l, D_MODEL), m.reshape(n_seq, d, l, LANES), l_sum.reshape(n_seq, d, l, LANES)]


def _attn_merge(branches, name):
    n_seq = branches[0][0].shape[0]
    t = n_seq * SEQ
    tm = MERGE_TILE

    def body(*refs):
        ins = refs[:9]
        o_ref, ob_ref, lse_ref = refs[9:12]
        nat = refs[12:]
        for b, d in enumerate(C_DILATIONS[1:]):
            for k in range(3):
                _load_dilated(ins[3 + 3 * b + k], d, nat[3 * b + k])
        ms = [ins[1][0, 0], nat[1][0], nat[4][0]]
        ls = [ins[2][0, 0], nat[2][0], nat[5][0]]
        m_all = jnp.maximum(jnp.maximum(ms[0], ms[1]), ms[2])
        ws = [jnp.exp(ms[b] - m_all) for b in range(3)]
        lane = lax.broadcasted_iota(jnp.int32, m_all.shape, 1)
        total = jnp.where(lane < C_HEADS, ws[0] * ls[0] + ws[1] * ls[1] + ws[2] * ls[2], 1.0)
        lse_ref[...] = m_all + jnp.log(total)
        first_head = lane < C_HEAD_DIM
        for p in range(LANE_GROUPS):
            lanes = slice(p * LANES, (p + 1) * LANES)
            spread = lambda c: jnp.where(first_head, c[:, 2 * p:2 * p + 1], c[:, 2 * p + 1:2 * p + 2])
            os_ = [ins[0][0, 0, :, lanes], nat[0][p], nat[3][p]]
            o = (spread(ws[0]) * os_[0] + spread(ws[1]) * os_[1] + spread(ws[2]) * os_[2]) / spread(total)
            o_ref[:, lanes] = o
            ob_ref[:, lanes] = o.astype(BF16)

    row = pl.BlockSpec((tm, D_MODEL), lambda i: (i, 0))
    flat = [a for br in branches for a in br]
    in_specs = []
    for wide, narrow in zip(_dilated_specs(tm, D_MODEL, lambda: 0), _dilated_specs(tm, LANES, lambda: 0)):
        in_specs += [wide, narrow, narrow]
    per_head = pltpu.VMEM((1, tm, LANES), F32)
    return pl.pallas_call(
        body, name=name, grid=(t // tm,), in_specs=in_specs,
        out_specs=[row, row, pl.BlockSpec((tm, LANES), lambda i: (i, 0))],
        out_shape=[jax.ShapeDtypeStruct((t, D_MODEL), F32), jax.ShapeDtypeStruct((t, D_MODEL), BF16),
                   jax.ShapeDtypeStruct((t, LANES), F32)],
        scratch_shapes=[pltpu.VMEM((LANE_GROUPS, tm, LANES), F32), per_head, per_head] * 2,
        compiler_params=_params(("parallel",)),
    )(*flat)


def _attn_branch_bwd(qkv, dout, lse, delta, name):
    n_seq, d, l, _ = qkv.shape
    flat = lambda a: a.reshape(n_seq, SEQ, a.shape[-1])

    def body(q_ref, k_ref, v_ref, do_ref, lse_nat_ref, dl_nat_ref, dq_ref, dk_ref, dv_ref, lse_ref, dl_ref,
             dkt_ref, dvt_ref):
        heads = _head_masks()
        g = pl.program_id(1)
        dkt_ref[...] = jnp.zeros_like(dkt_ref)
        dvt_ref[...] = jnp.zeros_like(dvt_ref)
        for nat_ref, dst_ref in ((lse_nat_ref, lse_ref), (dl_nat_ref, dl_ref)):
            for r in range(d):
                rows = pl.ds(r, l, stride=d) if d > 1 else slice(None)
                dst_ref[r * l:(r + 1) * l, :] = nat_ref.at[0][rows, :]

        def block(i, carry):
            rows, keys, mask = _attn_window(i, d)
            q, do = q_ref[0, rows, :], do_ref[0, rows, :]
            k, v = k_ref[0, keys, :], v_ref[0, keys, :]
            lse_b, dl_b = lse_ref[rows, :], dl_ref[rows, :]
            dq, dk, dv = [], None, None
            for hh, hm in enumerate(heads):
                qh, doh = jnp.where(hm, q, 0), jnp.where(hm, do, 0)
                s = jnp.where(mask, _dot_nt(qh, k), NEG)
                p = jnp.exp(s - _get_head(lse_b, 2 * g + hh))
                ds = (p * (_dot_nt(doh, v) - _get_head(dl_b, 2 * g + hh))).astype(BF16)
                dq.append(_dot(ds, k) * QK_SCALE)
                dk_h, dv_h = _dot_tn(qh, ds), _dot_tn(doh, p.astype(BF16))
                dk = dk_h if dk is None else dk + dk_h
                dv = dv_h if dv is None else dv + dv_h
            dq_ref[0, rows, :] = jnp.where(heads[0], dq[0], dq[1]).astype(BF16)
            dkt_ref[:, keys] += dk
            dvt_ref[:, keys] += dv
            return carry

        lax.fori_loop(0, Q_BLOCKS, block, 0, unroll=ATTN_UNROLL)
        for c in range(SEQ // ROW_TILE):
            rows = slice(c * ROW_TILE, (c + 1) * ROW_TILE)
            dk_ref[0, rows, :] = dkt_ref[:, rows].T.astype(BF16)
            dv_ref[0, rows, :] = dvt_ref[:, rows].T.astype(BF16)

    act = _flat_spec(lambda g: g)
    outs = pl.pallas_call(
        body, name=name, grid=(n_seq, LANE_GROUPS),
        in_specs=[_flat_spec(lambda g: g), _flat_spec(lambda g: LANE_GROUPS + g),
                  _flat_spec(lambda g: 2 * LANE_GROUPS + g), act, PER_HEAD_SPEC, PER_HEAD_SPEC],
        out_specs=[act] * 3,
        out_shape=[jax.ShapeDtypeStruct((n_seq, SEQ, D_MODEL), BF16)] * 3,
        scratch_shapes=[pltpu.VMEM((SEQ, LANES), F32)] * 2 + [pltpu.VMEM((LANES, SEQ), F32)] * 2,
        compiler_params=_params(("parallel", "parallel")),
    )(flat(qkv), flat(qkv), flat(qkv), flat(dout), lse, delta)
    return [o.reshape(n_seq, d, l, D_MODEL) for o in outs]


def _attn_combine_bwd(grads, rope, name):
    n_seq = grads[0][0].shape[0]
    t = n_seq * SEQ
    tm = MERGE_TILE

    def body(*refs):
        c_ref, s_ref, o_ref, nat4_ref, nat16_ref = refs[9:]
        for sec in range(3):
            _load_dilated(refs[3 + sec], 4, nat4_ref)
            _load_dilated(refs[6 + sec], 16, nat16_ref)
            for p in range(LANE_GROUPS):
                blk = refs[sec][0, 0, :, p * LANES:(p + 1) * LANES] + nat4_ref[p] + nat16_ref[p]
                if sec < 2:
                    blk = blk * c_ref[...] - _swap_halves(blk) * s_ref[...]
                o_ref[:, sec * D_MODEL + p * LANES:sec * D_MODEL + (p + 1) * LANES] = blk.astype(BF16)

    tab = pl.BlockSpec((tm, LANES), lambda i: (i, 0))
    flat = [a for br in grads for a in br]
    in_specs = []
    for spec in _dilated_specs(tm, D_MODEL, lambda: 0):
        in_specs += [spec] * 3
    return pl.pallas_call(
        body, name=name, grid=(t // tm,), in_specs=in_specs + [tab, tab],
        out_specs=pl.BlockSpec((tm, ODD_IN), lambda i: (i, 0)),
        out_shape=jax.ShapeDtypeStruct((t, ODD_IN), BF16),
        scratch_shapes=[pltpu.VMEM((LANE_GROUPS, tm, LANES), F32)] * 2,
        compiler_params=_params(("parallel",)),
    )(*flat, *rope)


def _adamw(w, g, m, v):
    m = ADAM_B1 * m + (1.0 - ADAM_B1) * g
    v = ADAM_B2 * v + (1.0 - ADAM_B2) * jnp.square(g)
    m_hat = m / (1.0 - ADAM_B1 ** ADAM_STEP)
    v_hat = v / (1.0 - ADAM_B2 ** ADAM_STEP)
    delta = -ADAM_LR * (m_hat / (jnp.sqrt(v_hat) + ADAM_EPS) + ADAM_WD * w)
    return delta, m, v


def _adamw_sharded(parts, w, m, v, after, name):
    n_layers, rows, cols = w.shape
    tr = min(rows, 256)

    def body(*refs):
        p_refs = refs[:n_layers]
        w_ref, m_ref, v_ref, _, g_ref, d_ref, mo_ref, vo_ref = refs[n_layers:]
        layer = pl.program_id(0)
        g = None
        for l, p_ref in enumerate(p_refs):
            g_l = p_ref[0].astype(F32)
            for s in range(1, N_DEV):
                g_l = g_l + p_ref[s].astype(F32)
            g = g_l if g is None else jnp.where(layer == l, g_l, g)
        delta, mn, vn = _adamw(w_ref[0], g, m_ref[0], v_ref[0])
        g_ref[0] = g
        d_ref[0] = delta
        mo_ref[0] = mn
        vo_ref[0] = vn

    def part_spec(l):
        return pl.BlockSpec((N_DEV, tr, cols), lambda a, i: (0, jnp.where(a == l, i, 0), 0))

    row = pl.BlockSpec((1, tr, cols), lambda a, i: (a, i, 0))
    return pl.pallas_call(
        body, name=name, grid=(n_layers, rows // tr),
        in_specs=[part_spec(l) for l in range(n_layers)] + [row, row, row, pl.BlockSpec(memory_space=pl.ANY)],
        out_specs=[row] * 4, out_shape=[jax.ShapeDtypeStruct(w.shape, F32)] * 4,
        compiler_params=_params(("arbitrary", "arbitrary")),
    )(*parts, w, m, v, after)


def _small_update(gathered, where, weights, moments_m, moments_v, lb_index, name):
    n = len(weights)
    n_g = len(gathered)

    def body(*refs):
        g_refs = refs[:n_g]
        w_refs, m_refs, v_refs = refs[n_g:n_g + n], refs[n_g + n:n_g + 2 * n], refs[n_g + 2 * n:n_g + 3 * n]
        outs = refs[n_g + 3 * n:]

        def total(k):
            array, rows, lanes = where[k]
            ref = g_refs[array]
            index = (slice(None),) * (len(ref.shape) - 1) if rows is None else (rows, lanes)
            acc = ref[(0,) + index]
            for s in range(1, N_DEV):
                acc = acc + ref[(s,) + index]
            return acc

        loss_rows = total(n)
        outs[0][...] = jnp.sum(jnp.sum(loss_rows, axis=1, keepdims=True), axis=0, keepdims=True)
        for k in range(n):
            part = total(k)
            if k == lb_index:
                dlb = jnp.sum(part, axis=0, keepdims=True)
                tab = w_refs[k][...]
                e = jnp.exp(tab - jnp.max(tab, axis=0, keepdims=True))
                p = e / jnp.sum(e, axis=0, keepdims=True)
                first = lax.broadcasted_iota(jnp.int32, p.shape, 0) == 0
                grads = [(slice(None), p * (jnp.where(first, dlb, 0.0) - p[0:1, :] * dlb))]
            elif part.shape == w_refs[k].shape:
                grads = [(slice(None), part)]
            else:
                grads = [(slice(l, l + 1), jnp.sum(part[l * SUBLANES:(l + 1) * SUBLANES], axis=0, keepdims=True))
                         for l in range(w_refs[k].shape[0])]
            for rows, g in grads:
                delta, mn, vn = _adamw(w_refs[k][rows], g, m_refs[k][rows], v_refs[k][rows])
                outs[1 + 4 * k][rows] = g
                outs[2 + 4 * k][rows] = delta
                outs[3 + 4 * k][rows] = mn
                outs[4 + 4 * k][rows] = vn

    vmem = pl.BlockSpec(memory_space=pltpu.VMEM)
    out_shape = [jax.ShapeDtypeStruct((1, 1), F32)]
    for w in weights:
        out_shape += [jax.ShapeDtypeStruct(w.shape, F32)] * 4
    args = list(gathered) + list(weights) + list(moments_m) + list(moments_v)
    return pl.pallas_call(
        body, name=name, in_specs=[vmem] * len(args), out_specs=[vmem] * len(out_shape), out_shape=out_shape,
        compiler_params=pltpu.CompilerParams(vmem_limit_bytes=VMEM_LIMIT),
    )(*args)


def kernel(x, positions, norm_mix_pre, norm_mix_post, norm_ffn_pre, norm_ffn_post, w_in_even, lb_table, a_norm, b_ln_g, b_ln_b, b_ws, b_bias, w_out_even, w_in_odd, w_out_odd, w_ff1, w_ff2, loss_target, m_norm_mix_pre, m_norm_mix_post, m_norm_ffn_pre, m_norm_ffn_post, m_w_in_even, m_lb_table, m_a_norm, m_b_ln_g, m_b_ln_b, m_b_ws, m_b_bias, m_w_out_even, m_w_in_odd, m_w_out_odd, m_w_ff1, m_w_ff2, v_norm_mix_pre, v_norm_mix_post, v_norm_ffn_pre, v_norm_ffn_post, v_w_in_even, v_lb_table, v_a_norm, v_b_ln_g, v_b_ln_b, v_b_ws, v_b_bias, v_w_out_even, v_w_in_odd, v_w_out_odd, v_w_ff1, v_w_ff2):
    n_seq = x.shape[0]
    t = n_seq * SEQ
    x0 = x.reshape(t, D_MODEL)
    target = loss_target.reshape(t, D_MODEL)

    me = _my_slot().astype(jnp.int32).reshape(1)

    order = ["in_e", "out_e", "ff1_0", "ff2_0", "in_o", "out_o", "ff1_1", "ff2_1"]
    shards = dict(in_e=w_in_even[0], out_e=w_out_even[0], in_o=w_in_odd[0], out_o=w_out_odd[0],
                  ff1_0=w_ff1[0], ff1_1=w_ff1[1], ff2_0=w_ff2[0], ff2_1=w_ff2[1])
    by_columns = ("in_e", "in_o", "ff1_0", "ff1_1")

    def place(k, after):
        if k in by_columns:
            return _place_own_columns(shards[k], me, "place_" + k, after)
        return _place_own(shards[k], me, "place_" + k, False, after=after)

    gathers = {}
    send0, recv0, land0, _, token0 = _exchange_start([place(order[0], None)], [None], "gather_start_first")
    gathers[order[0]] = (land0[0], send0[0], recv0[0])
    sends, recvs, lands, _, g_token = _exchange_start([place(k, token0) for k in order[1:]],
                                                      [None] * (len(order) - 1), "gather_start")
    for k, land, send, recv in zip(order[1:], lands, sends, recvs):
        gathers[k] = (land, send, recv)

    def get_w(keys, after):
        lands_k, sends_k, recvs_k = zip(*[gathers[k] for k in keys])
        return _exchange_wait(list(lands_k), [None] * len(keys), list(sends_k), list(recvs_k), after,
                              "gather_wait_" + keys[0])

    sent = {}

    def put_g(group, blocks):
        keys = list(blocks)
        own = [_place_own(blocks[k], me, "own_" + k, True) for k in keys]
        send_sems, recv_sems, own, srcs, token = _exchange_start(own, [blocks[k] for k in keys], "scatter_start_" + group)
        sent[group] = (keys, own, srcs, send_sems, recv_sems)
        return token

    rope = _rope_tables(positions)
    bias_t = b_bias[0].T
    grads = _local_step(x0, target, rope, norm_mix_pre, norm_mix_post, norm_ffn_pre, norm_ffn_post, lb_table,
                        a_norm, b_ln_g, b_ln_b, b_ws[0], bias_t, get_w, put_g, g_token)
    (dx0, loss_part, dg_mix_pre, dg_mix_post, dg_ffn_pre, dg_ffn_post, d_lb, d_a_norm, d_ln_g, d_ln_b, d_ws,
     d_bias_t) = grads

    packed = jnp.concatenate([dg_mix_pre, dg_mix_post, dg_ffn_pre, dg_ffn_post,
                              jnp.concatenate([d_lb, d_a_norm], axis=1), jnp.concatenate([d_ln_g, d_ln_b], axis=1),
                              loss_part], axis=0)
    small_lands = [_place_own(a, me, "own_small%d" % k, False, F32) for k, a in enumerate((packed, d_ws, d_bias_t))]
    s_send, s_recv, small_lands, _, after = _exchange_start(small_lands, [None] * 3, "gather_small_start")

    big = dict(w_in_even=(["in_e"], w_in_even, m_w_in_even, v_w_in_even),
               w_out_even=(["out_e"], w_out_even, m_w_out_even, v_w_out_even),
               w_in_odd=(["in_o"], w_in_odd, m_w_in_odd, v_w_in_odd),
               w_out_odd=(["out_o"], w_out_odd, m_w_out_odd, v_w_out_odd),
               w_ff1=(["ff1_0", "ff1_1"], w_ff1, m_w_ff1, v_w_ff1), w_ff2=(["ff2_0", "ff2_1"], w_ff2, m_w_ff2, v_w_ff2))
    recv, big_out = {}, {}
    for groups, names in ((("ffn1", "ffn0"), ("w_ff1", "w_ff2")), (("mix1",), ("w_in_odd", "w_out_odd")),
                          (("mix0",), ("w_in_even", "w_out_even"))):
        for group in groups:
            keys, own, srcs, send_sems, recv_sems = sent[group]
            recv.update(zip(keys, _exchange_wait(own, srcs, send_sems, recv_sems, after, "scatter_wait_" + group)))
        for nm in names:
            keys, w, m, v = big[nm]
            big_out[nm] = _adamw_sharded([recv[k] for k in keys], w, m, v, after, "adamw_" + nm)
            after = big_out[nm][0]
    big_out = [big_out[nm] for nm in ("w_in_even", "w_out_even", "w_in_odd", "w_out_odd", "w_ff1", "w_ff2")]
    gathered = _exchange_wait(small_lands, [None] * 3, s_send, s_recv, after, "gather_small_wait")
    rows8 = lambda k: slice(SUBLANES * k, SUBLANES * (k + 1))
    left, right, every = slice(0, A_WIDTH), slice(A_WIDTH, 2 * A_WIDTH), slice(None)
    where = [(0, slice(0, 16), every), (0, slice(16, 32), every), (0, slice(32, 48), every), (0, slice(48, 64), every),
             (0, rows8(8), left), (0, rows8(8), right), (0, rows8(9), left), (0, rows8(9), right),
             (1, None, None), (2, None, None), (0, rows8(10), every)]
    small_w = [norm_mix_pre, norm_mix_post, norm_ffn_pre, norm_ffn_post, lb_table, a_norm, b_ln_g, b_ln_b,
               b_ws[0], bias_t]
    small_m = [m_norm_mix_pre, m_norm_mix_post, m_norm_ffn_pre, m_norm_ffn_post, m_lb_table, m_a_norm, m_b_ln_g,
               m_b_ln_b, m_b_ws[0], m_b_bias[0].T]
    small_v = [v_norm_mix_pre, v_norm_mix_post, v_norm_ffn_pre, v_norm_ffn_post, v_lb_table, v_a_norm, v_b_ln_g,
               v_b_ln_b, v_b_ws[0], v_b_bias[0].T]
    small_out = _small_update(gathered, where, small_w, small_m, small_v, 4, "small_update")
    loss = small_out[0].reshape(())
    small = [small_out[1 + 4 * k:5 + 4 * k] for k in range(len(small_w))]
    small[8] = [a[None] for a in small[8]]
    small[9] = [a.T[None] for a in small[9]]

    per_weight = small[0:4] + [big_out[0]] + small[4:10] + big_out[1:6]
    grad_x = dx0.reshape(x.shape)
    out = [loss, grad_x]
    for kind in range(4):
        out += [p[kind] for p in per_weight]
    return tuple(out)


def _local_step(x0, target, rope, norm_mix_pre, norm_mix_post, norm_ffn_pre, norm_ffn_post, lb_table, a_norm,
                b_ln_g, b_ln_b, ws, bias_t, get_w, put_g, token):
    def gain(a, l, tok):
        return a[l:l + 1] if tok is None else a[l:l + 1] + tok[0:1, 0:1]

    full = lambda a: a.reshape(-1, D_MODEL)
    owners = lambda a: a.reshape((N_DEV, -1) + a.shape[1:])

    (g_in_e,) = get_w(["in_e"], token)
    proj, h_mix0 = _norm_inproj(x0, gain(norm_mix_pre, 0, token), g_in_e, "inproj_even")
    mixed, pre_a, states = _hgrn2_fwd(proj, lb_table, a_norm, "hgrn2_fwd")
    mixed = _gmlp_fwd(proj, mixed, b_ln_g, b_ln_b, ws, bias_t, "gmlp_fwd")
    w_out_e = full(get_w(["out_e"], mixed)[0])
    x1, mix0 = _outproj([mixed], w_out_e, x0, gain(norm_mix_post, 0, None), "outproj_even")
    w1_0, w2_0 = get_w(["ff1_0", "ff2_0"], x1)
    w2_0 = full(w2_0)
    x2, y0, h_ffn0, r0 = _ffn_fwd(x1, gain(norm_ffn_pre, 0, None), w1_0, w2_0, gain(norm_ffn_post, 0, None), "ffn_fwd_0")
    (g_in_o,) = get_w(["in_o"], x2)
    *qkv, h_mix1 = _norm_inproj_rope(x2, gain(norm_mix_pre, 1, None), g_in_o, rope, "inproj_odd")
    branches = [_attn_branch_fwd(a, "attn_fwd_d%d" % d) for a, d in zip(qkv, C_DILATIONS)]
    attn, attn_b, lse = _attn_merge(branches, "attn_merge")
    w_out_o = full(get_w(["out_o"], attn_b)[0])
    x3, mix1 = _outproj([attn_b], w_out_o, x2, gain(norm_mix_post, 1, None), "outproj_odd")
    w1_1, w2_1 = get_w(["ff1_1", "ff2_1"], x3)
    w2_1 = full(w2_1)
    dx4, y1, h_ffn1, r1, loss_part = _ffn_fwd(x3, gain(norm_ffn_pre, 1, None), w1_1, w2_1, gain(norm_ffn_post, 1, None),
                                              "ffn_fwd_1", target)

    dx3, dy1, da1, dg_ffn_pre1, dg_ffn_post1 = _ffn_bwd(
        dx4, x3, y1, r1, gain(norm_ffn_pre, 1, None), w1_1, w2_1, gain(norm_ffn_post, 1, None), "ffn_bwd_1")
    gw_ff1_1 = _grad_w(h_ffn1, da1, True, "grad_w_ff1_1")
    gw_ff2_1 = _grad_w(r1, dy1, False, "grad_w_ff2_1")
    tok = put_g("ffn1", dict(ff1_1=gw_ff1_1, ff2_1=owners(gw_ff2_1)))
    *dattn, delta, dz1, dg_mix_post1 = _outproj_bwd_attn(dx3, mix1, gain(norm_mix_post, 1, tok), w_out_o, attn,
                                                  "outproj_bwd_odd")
    gw_out_o = _grad_w(attn_b, dz1, False, "grad_w_out_odd")
    per_seq = lambda a: a.reshape(-1, SEQ, LANES)
    grads_c = [_attn_branch_bwd(qkv[b], dattn[b], per_seq(lse), per_seq(delta), "attn_bwd_d%d" % d)
               for b, d in enumerate(C_DILATIONS)]
    dqkv = _attn_combine_bwd(grads_c, rope, "attn_combine_bwd")
    gw_in_o = _grad_w(h_mix1, dqkv, True, "grad_w_in_odd")
    tok = put_g("mix1", dict(out_o=owners(gw_out_o), in_o=gw_in_o))
    dx2, dg_mix_pre1 = _inproj_bwd(dqkv, g_in_o, dx3, x2, gain(norm_mix_pre, 1, tok), "inproj_bwd_odd")

    dx1, dy0, da0, dg_ffn_pre0, dg_ffn_post0 = _ffn_bwd(
        dx2, x1, y0, r0, gain(norm_ffn_pre, 0, None), w1_0, w2_0, gain(norm_ffn_post, 0, None), "ffn_bwd_0")
    gw_ff1_0 = _grad_w(h_ffn0, da0, True, "grad_w_ff1_0")
    gw_ff2_0 = _grad_w(r0, dy0, False, "grad_w_ff2_0")
    tok = put_g("ffn0", dict(ff1_0=gw_ff1_0, ff2_0=owners(gw_ff2_0)))
    dcat, dz0, dg_mix_post0 = _outproj_bwd(dx1, mix0, gain(norm_mix_post, 0, tok), w_out_e, "outproj_bwd_even")
    gw_out_e = _grad_w(mixed, dz0, False, "grad_w_out_even")
    dproj, d_lb, d_a_norm = _hgrn2_bwd(proj, dcat, pre_a, states, lb_table, a_norm, "hgrn2_bwd")
    dproj, d_ln_g, d_ln_b, d_ws, d_bias_t = _gmlp_bwd(proj, dcat, dproj, b_ln_g, b_ln_b, ws, bias_t, "gmlp_bwd")
    gw_in_e = _grad_w(h_mix0, dproj, True, "grad_w_in_even")
    tok = put_g("mix0", dict(out_e=owners(gw_out_e), in_e=gw_in_e))
    dx0, dg_mix_pre0 = _inproj_bwd(dproj, g_in_e, dx1, x0, gain(norm_mix_pre, 0, tok), "inproj_bwd_even")

    layers = lambda a, b: jnp.concatenate([a, b], axis=0)
    return (dx0, loss_part, layers(dg_mix_pre0, dg_mix_pre1), layers(dg_mix_post0, dg_mix_post1),
            layers(dg_ffn_pre0, dg_ffn_pre1), layers(dg_ffn_post0, dg_ffn_post1),
            d_lb, d_a_norm, d_ln_g, d_ln_b, d_ws, d_bias_t)
```

```python
import math

import jax
import jax.numpy as jnp
from jax import lax
from jax.experimental import pallas as pl
from jax.experimental.pallas import tpu as pltpu

F32 = jnp.float32
BF16 = jnp.bfloat16
MESH = pl.DeviceIdType.MESH

N_DEV = 8
D_MODEL = 1024
SEQ = 2048
EPS = 1e-6
A_WIDTH = 512
A_HEADS = 4
HEAD_A = 128
B_WIDTH = 512
B_GROUPS = 4
B_CHUNK = 128
C_HEADS = 16
C_HEAD_DIM = 64
C_ROT_HALF = 8
ROPE_THETA = 500000.0
C_DILATIONS = (1, 4, 16)
C_BLOCK = 128
D_FF = 4096
EVEN_IN = 3072
ODD_IN = 3072

ADAM_LR = 0.001
ADAM_B1 = 0.9
ADAM_B2 = 0.999
ADAM_EPS = 1e-08
ADAM_WD = 0.01
ADAM_STEP = 10

LANES = 128
SUBLANES = 8
ROW_TILE = 512
PROJ_TILE = 1024
PROJ_COLS = 768
MERGE_TILE = 256
SUB_CHUNK = 16
HGRN_BLOCK = 256
NEG = -1e30
VMEM_LIMIT = 56 * 1024 * 1024


def _params(sem):
    return pltpu.CompilerParams(dimension_semantics=sem, vmem_limit_bytes=VMEM_LIMIT)


def _dot(a, b):
    return jnp.dot(a, b, preferred_element_type=F32)


def _dot_nt(a, b):
    return lax.dot_general(a, b, (((1,), (1,)), ((), ())), preferred_element_type=F32)


def _dot_tn(a, b):
    return lax.dot_general(a, b, (((0,), (0,)), ((), ())), preferred_element_type=F32)


def _rms(x, g):
    r = lax.rsqrt(jnp.mean(x * x, axis=-1, keepdims=True) + EPS)
    return x * r * g


def _rms_bwd(x, g, dy):
    r = lax.rsqrt(jnp.mean(x * x, axis=-1, keepdims=True) + EPS)
    dyg = dy * g
    dx = r * dyg - x * (r * r * r) * jnp.mean(x * dyg, axis=-1, keepdims=True)
    return dx, dy * x * r


def _split3(x):
    hi = x.astype(BF16)
    rest = x - hi.astype(F32)
    mid = rest.astype(BF16)
    return hi, mid, (rest - mid.astype(F32)).astype(BF16)


def _mask_dot(mask, x):
    m = mask.astype(BF16)
    hi, mid, lo = _split3(x)
    return _dot(m, hi) + (_dot(m, mid) + _dot(m, lo))


def _dot_mask(x, mask):
    m = mask.astype(BF16)
    hi, mid, lo = _split3(x)
    return _dot(hi, m) + (_dot(mid, m) + _dot(lo, m))


def _rows8(v):
    return v.reshape(v.shape[0] // SUBLANES, SUBLANES, v.shape[1]).sum(axis=0)


def _sigmoid(x):
    return 1.0 / (1.0 + jnp.exp(-x))


def _gelu(x):
    return 0.5 * x * (1.0 + jnp.tanh(math.sqrt(2.0 / math.pi) * (x + 0.044715 * (x * x * x))))


def _acc_rows8(ref, val, first):
    @pl.when(first)
    def _():
        ref[...] = val

    @pl.when(jnp.logical_not(first))
    def _():
        ref[...] += val


def _my_slot():
    return 4 * lax.axis_index("x") + 2 * lax.axis_index("y") + lax.axis_index("c")


def _peer(r):
    x, y, c = lax.axis_index("x"), lax.axis_index("y"), lax.axis_index("c")
    px = 1 - x if (r >> 2) & 1 else x
    py = 1 - y if (r >> 1) & 1 else y
    pc = 1 - c if r & 1 else c
    return (px, py, pc), 4 * px + 2 * py + pc


HBM_SPEC = pl.BlockSpec(memory_space=pltpu.HBM)
SEM_SPEC = pl.BlockSpec(memory_space=pltpu.SEMAPHORE)
SPLIT_EFFECT = pltpu.SideEffectType.DATAFLOW_SIDE_EFFECTING


def _split_copies(land_ref, src_ref, send_sem, recv_sem):
    me = _my_slot()
    copies = []
    for r in range(1, N_DEV):
        peer, slot = _peer(r)
        src = _slot(land_ref, me) if src_ref is None else _slot(src_ref, slot)
        copies.append(pltpu.make_async_remote_copy(
            src_ref=src, dst_ref=_slot(land_ref, me), send_sem=send_sem, recv_sem=recv_sem,
            device_id=peer, device_id_type=MESH))
    return copies


def _slot(ref, s):
    if len(ref.shape) == 2:
        c = ref.shape[1] // N_DEV
        return ref.at[:, pl.ds(pl.multiple_of(s * c, LANES), c)]
    return ref.at[s]


def _exchange_start(lands, sources, name):
    n = len(lands)
    given = [s for s in sources if s is not None]
    arrays = list(lands) + given

    def body(*refs):
        land_refs, src_refs = refs[:n], list(refs[n:n + len(given)])
        sems = refs[len(arrays):len(arrays) + 2 * n]
        token = refs[-1]
        for k in range(n):
            src_ref = None if sources[k] is None else src_refs.pop(0)
            for copy in _split_copies(land_refs[k], src_ref, sems[k], sems[n + k]):
                copy.start()
        token[...] = jnp.zeros_like(token)

    outs = pl.pallas_call(
        body, name=name,
        out_shape=(pltpu.SemaphoreType.DMA(()),) * (2 * n) + tuple(pltpu.HBM(a.shape, a.dtype) for a in arrays)
        + (jax.ShapeDtypeStruct((SUBLANES, LANES), F32),),
        in_specs=[HBM_SPEC] * len(arrays),
        out_specs=(SEM_SPEC,) * (2 * n) + (HBM_SPEC,) * len(arrays) + (pl.BlockSpec(memory_space=pltpu.VMEM),),
        input_output_aliases={i: 2 * n + i for i in range(len(arrays))},
        compiler_params=pltpu.CompilerParams(has_side_effects=SPLIT_EFFECT),
    )(*[pltpu.with_memory_space_constraint(a, pltpu.HBM) for a in arrays])
    return list(outs[:n]), list(outs[n:2 * n]), list(outs[2 * n:3 * n]), list(outs[3 * n:-1]), outs[-1]


def _exchange_wait(lands, sources, send_sems, recv_sems, after, name):
    n = len(lands)
    given = [s for s in sources if s is not None]
    arrays = list(lands) + given

    def body(*refs):
        land_refs, src_refs = refs[:n], list(refs[n:n + len(given)])
        sems = refs[len(arrays):len(arrays) + 2 * n]
        for i in range(n):
            src_ref = None if sources[i] is None else src_refs.pop(0)
            copies = _split_copies(land_refs[i], src_ref, sems[i], sems[n + i])
            for copy in copies:
                copy.wait_recv()
            for copy in copies:
                copy.wait_send()

    outs = pl.pallas_call(
        body, name=name, out_shape=tuple(pltpu.HBM(a.shape, a.dtype) for a in arrays),
        in_specs=[HBM_SPEC] * len(arrays) + [SEM_SPEC] * (2 * n) + [pl.BlockSpec(memory_space=pl.ANY)],
        out_specs=(HBM_SPEC,) * len(arrays),
        input_output_aliases={i: i for i in range(len(arrays))},
        compiler_params=pltpu.CompilerParams(has_side_effects=SPLIT_EFFECT),
    )(*arrays, *send_sems, *recv_sems, after)
    return list(outs[:n])


def _place_own(a, me, name, own_block, dtype=BF16, after=None):
    shape = a.shape[1:] if own_block else a.shape
    cols = shape[-1]
    a3 = a.reshape((N_DEV if own_block else 1, -1, cols))
    rows = a3.shape[1]
    tr = min(rows, 512)

    def body(me_ref, a_ref, _, o_ref):
        o_ref[...] = a_ref[...].astype(dtype)

    grid_spec = pltpu.PrefetchScalarGridSpec(
        num_scalar_prefetch=1, grid=(rows // tr,),
        in_specs=[pl.BlockSpec((1, tr, cols), lambda i, me_ref: (me_ref[0] if own_block else 0, i, 0)),
                  pl.BlockSpec(memory_space=pl.ANY)],
        out_specs=pl.BlockSpec((1, tr, cols), lambda i, me_ref: (me_ref[0], i, 0)))
    out = pl.pallas_call(
        body, name=name, grid_spec=grid_spec, out_shape=jax.ShapeDtypeStruct((N_DEV, rows, cols), dtype),
        compiler_params=_params(("arbitrary",)),
    )(me, a3, a3 if after is None else after)
    return out.reshape((N_DEV,) + shape)


def _place_own_columns(a, me, name, after=None):
    rows, cols = a.shape
    tr = min(rows, 512)

    def body(me_ref, a_ref, _, o_ref):
        o_ref[...] = a_ref[...].astype(BF16)

    grid_spec = pltpu.PrefetchScalarGridSpec(
        num_scalar_prefetch=1, grid=(rows // tr,),
        in_specs=[pl.BlockSpec((tr, cols), lambda i, me_ref: (i, 0)), pl.BlockSpec(memory_space=pl.ANY)],
        out_specs=pl.BlockSpec((tr, cols), lambda i, me_ref: (i, me_ref[0])))
    return pl.pallas_call(
        body, name=name, grid_spec=grid_spec, out_shape=jax.ShapeDtypeStruct((rows, N_DEV * cols), BF16),
        compiler_params=_params(("arbitrary",)),
    )(me, a, a if after is None else after)


def _rope_tables(positions):
    in_head = jnp.arange(LANES) % C_HEAD_DIM
    inv = ROPE_THETA ** (-(in_head % C_ROT_HALF).astype(F32) / C_ROT_HALF)
    ang = positions.reshape(-1)[:, None].astype(F32) * inv
    rotated = in_head < 2 * C_ROT_HALF
    sin = jnp.sin(ang)
    return (jnp.where(rotated, jnp.cos(ang), 1.0),
            jnp.where(in_head < C_ROT_HALF, -sin, jnp.where(rotated, sin, 0.0)))


def _swap_halves(x):
    lane = lax.broadcasted_iota(jnp.int32, x.shape, 1) % C_HEAD_DIM
    return jnp.where(lane < C_ROT_HALF, pltpu.roll(x, LANES - C_ROT_HALF, 1), pltpu.roll(x, C_ROT_HALF, 1))


def _norm_inproj(x, g, w, name):
    t = x.shape[0]
    n = w.shape[1]
    tm, tn = PROJ_TILE, PROJ_COLS

    def body(x_ref, g_ref, w_ref, o_ref, h_ref):
        @pl.when(pl.program_id(1) == 0)
        def _():
            h_ref[...] = _rms(x_ref[...], g_ref[...]).astype(BF16)

        o_ref[...] = _dot(h_ref[...], w_ref[...])

    return pl.pallas_call(
        body, name=name, grid=(t // tm, n // tn),
        in_specs=[pl.BlockSpec((tm, D_MODEL), lambda i, j: (i, 0)), pl.BlockSpec((1, D_MODEL), lambda i, j: (0, 0)),
                  pl.BlockSpec((D_MODEL, tn), lambda i, j: (0, j))],
        out_specs=[pl.BlockSpec((tm, tn), lambda i, j: (i, j)), pl.BlockSpec((tm, D_MODEL), lambda i, j: (i, 0))],
        out_shape=[jax.ShapeDtypeStruct((t, n), F32), jax.ShapeDtypeStruct((t, D_MODEL), BF16)],
        compiler_params=_params(("parallel", "arbitrary")),
    )(x, g, w)


def _dilated_specs(tm, width, col_of):
    per_seq = SEQ // tm
    specs = []
    for d in C_DILATIONS:
        specs.append(pl.BlockSpec(
            (1, d, tm // d, width), lambda i, *rest: (i // per_seq, 0, i % per_seq, col_of(*rest))))
    return specs


def _dilated_shapes(n_seq, cols, dtype):
    return [jax.ShapeDtypeStruct((n_seq, d, SEQ // d, cols), dtype) for d in C_DILATIONS]


def _store_dilated(src_ref, out_refs, dtype):
    groups, tm, _ = src_ref.shape
    for d, o_ref in zip(C_DILATIONS, out_refs):
        for r in range(d):
            rows = pl.ds(r, tm // d, stride=d) if d > 1 else slice(None)
            for p in range(groups):
                o_ref[0, r, :, p * LANES:(p + 1) * LANES] = src_ref.at[p][rows, :].astype(dtype)


def _load_dilated(in_ref, d, dst_ref):
    groups, tm, _ = dst_ref.shape
    for r in range(d):
        rows = pl.ds(r, tm // d, stride=d)
        for p in range(groups):
            dst_ref.at[p][rows, :] = in_ref[0, r, :, p * LANES:(p + 1) * LANES].astype(F32)


def _norm_inproj_rope(x, g, w, rope, name):
    t = x.shape[0]
    n = w.shape[1]
    tm, nb = PROJ_TILE, PROJ_COLS

    def body(x_ref, g_ref, w_ref, c_ref, s_ref, o1_ref, o4_ref, o16_ref, h_ref, tile_ref):
        j = pl.program_id(1)

        @pl.when(j == 0)
        def _():
            h_ref[...] = _rms(x_ref[...], g_ref[...]).astype(BF16)

        acc = _dot(h_ref[...], w_ref[...])
        for p in range(nb // LANES):
            blk = acc[:, p * LANES:(p + 1) * LANES]
            roped = blk * c_ref[...] + _swap_halves(blk) * s_ref[...]
            piece = j * (nb // LANES) + p
            is_qk = piece < 2 * (D_MODEL // LANES)
            tile_ref[p] = jnp.where(is_qk, roped, blk) * jnp.where(piece < D_MODEL // LANES, QK_SCALE, 1.0)
        _store_dilated(tile_ref, (o1_ref, o4_ref, o16_ref), BF16)

    return pl.pallas_call(
        body, name=name, grid=(t // tm, n // nb),
        in_specs=[pl.BlockSpec((tm, D_MODEL), lambda i, j: (i, 0)), pl.BlockSpec((1, D_MODEL), lambda i, j: (0, 0)),
                  pl.BlockSpec((D_MODEL, nb), lambda i, j: (0, j)),
                  pl.BlockSpec((tm, LANES), lambda i, j: (i, 0)), pl.BlockSpec((tm, LANES), lambda i, j: (i, 0))],
        out_specs=_dilated_specs(tm, nb, lambda j: j) + [pl.BlockSpec((tm, D_MODEL), lambda i, j: (i, 0))],
        out_shape=_dilated_shapes(t // SEQ, n, BF16) + [jax.ShapeDtypeStruct((t, D_MODEL), BF16)],
        scratch_shapes=[pltpu.VMEM((nb // LANES, tm, LANES), F32)],
        compiler_params=_params(("parallel", "arbitrary")),
    )(x, g, w, *rope)


def _outproj(parts, w, x, g, name):
    t = x.shape[0]
    tm = PROJ_TILE
    n = len(parts)
    widths = [p.shape[1] for p in parts]

    def body(*refs):
        p_refs = refs[:n]
        w_ref, x_ref, g_ref, xo_ref, mix_ref = refs[n:]
        mix = None
        off = 0
        for p_ref, wd in zip(p_refs, widths):
            term = _dot(p_ref[...].astype(BF16), w_ref[off:off + wd, :])
            mix = term if mix is None else mix + term
            off += wd
        mix_ref[...] = mix
        xo_ref[...] = x_ref[...] + _rms(mix, g_ref[...])

    row = lambda i: (i, 0)
    return pl.pallas_call(
        body, name=name, grid=(t // tm,),
        in_specs=[pl.BlockSpec((tm, wd), row) for wd in widths] + [
            pl.BlockSpec((sum(widths), D_MODEL), lambda i: (0, 0)),
            pl.BlockSpec((tm, D_MODEL), row), pl.BlockSpec((1, D_MODEL), lambda i: (0, 0))],
        out_specs=[pl.BlockSpec((tm, D_MODEL), row)] * 2,
        out_shape=[jax.ShapeDtypeStruct((t, D_MODEL), F32)] * 2,
        compiler_params=_params(("parallel",)),
    )(*parts, w, x, g)


def _outproj_bwd(dx, mix, g, w, name):
    t = dx.shape[0]
    tm = PROJ_TILE
    k = w.shape[0]

    def body(dx_ref, mix_ref, g_ref, w_ref, dcat_ref, dz_ref, dg_ref):
        dz, dgr = _rms_bwd(mix_ref[...], g_ref[...], dx_ref[...])
        dzb = dz.astype(BF16)
        dz_ref[...] = dzb
        dcat_ref[...] = _dot_nt(dzb, w_ref[...])
        _acc_rows8(dg_ref, _rows8(dgr), pl.program_id(0) == 0)

    row = lambda i: (i, 0)
    return pl.pallas_call(
        body, name=name, grid=(t // tm,),
        in_specs=[pl.BlockSpec((tm, D_MODEL), row), pl.BlockSpec((tm, D_MODEL), row),
                  pl.BlockSpec((1, D_MODEL), lambda i: (0, 0)), pl.BlockSpec((k, D_MODEL), lambda i: (0, 0))],
        out_specs=[pl.BlockSpec((tm, k), row), pl.BlockSpec((tm, D_MODEL), row),
                   pl.BlockSpec((SUBLANES, D_MODEL), lambda i: (0, 0))],
        out_shape=[jax.ShapeDtypeStruct((t, k), F32), jax.ShapeDtypeStruct((t, D_MODEL), BF16),
                   jax.ShapeDtypeStruct((SUBLANES, D_MODEL), F32)],
        compiler_params=_params(("arbitrary",)),
    )(dx, mix, g, w)


def _outproj_bwd_attn(dx, mix, g, w, out, name):
    t = dx.shape[0]
    tm = MERGE_TILE

    def body(dx_ref, mix_ref, g_ref, w_ref, out_ref, do1, do4, do16, dl_ref, dz_ref, dg_ref, tile_ref):
        dz, dgr = _rms_bwd(mix_ref[...], g_ref[...], dx_ref[...])
        dzb = dz.astype(BF16)
        dz_ref[...] = dzb
        _acc_rows8(dg_ref, _rows8(dgr), pl.program_id(0) == 0)
        dout = _dot_nt(dzb, w_ref[...])
        for p in range(LANE_GROUPS):
            tile_ref[p] = dout[:, p * LANES:(p + 1) * LANES]
        _store_dilated(tile_ref, (do1, do4, do16), BF16)
        column = lax.broadcasted_iota(jnp.int32, (D_MODEL, LANES), 0) // C_HEAD_DIM
        head = lax.broadcasted_iota(jnp.int32, (D_MODEL, LANES), 1)
        dl_ref[...] = _dot_mask(dout * out_ref[...], column == head)

    row = lambda i: (i, 0)
    n_seq = t // SEQ
    return pl.pallas_call(
        body, name=name, grid=(t // tm,),
        in_specs=[pl.BlockSpec((tm, D_MODEL), row), pl.BlockSpec((tm, D_MODEL), row),
                  pl.BlockSpec((1, D_MODEL), lambda i: (0, 0)), pl.BlockSpec((D_MODEL, D_MODEL), lambda i: (0, 0)),
                  pl.BlockSpec((tm, D_MODEL), row)],
        out_specs=_dilated_specs(tm, D_MODEL, lambda: 0) + [
            pl.BlockSpec((tm, LANES), row), pl.BlockSpec((tm, D_MODEL), row),
            pl.BlockSpec((SUBLANES, D_MODEL), lambda i: (0, 0))],
        out_shape=_dilated_shapes(n_seq, D_MODEL, BF16) + [
            jax.ShapeDtypeStruct((t, LANES), F32), jax.ShapeDtypeStruct((t, D_MODEL), BF16),
            jax.ShapeDtypeStruct((SUBLANES, D_MODEL), F32)],
        scratch_shapes=[pltpu.VMEM((LANE_GROUPS, tm, LANES), F32)],
        compiler_params=_params(("arbitrary",)),
    )(dx, mix, g, w, out)


def _inproj_bwd(dproj, w, dx, x, g, name):
    t = x.shape[0]
    n = w.shape[1]
    tm = ROW_TILE

    def body(dp_ref, w_ref, dx_ref, x_ref, g_ref, o_ref, dg_ref):
        dxn, dgr = _rms_bwd(x_ref[...], g_ref[...], _dot_nt(dp_ref[...], w_ref[...]))
        o_ref[...] = dx_ref[...] + dxn
        _acc_rows8(dg_ref, _rows8(dgr), pl.program_id(0) == 0)

    row = lambda i: (i, 0)
    return pl.pallas_call(
        body, name=name, grid=(t // tm,),
        in_specs=[pl.BlockSpec((tm, n), row), pl.BlockSpec((D_MODEL, n), lambda i: (0, 0)),
                  pl.BlockSpec((tm, D_MODEL), row), pl.BlockSpec((tm, D_MODEL), row),
                  pl.BlockSpec((1, D_MODEL), lambda i: (0, 0))],
        out_specs=[pl.BlockSpec((tm, D_MODEL), row), pl.BlockSpec((SUBLANES, D_MODEL), lambda i: (0, 0))],
        out_shape=[jax.ShapeDtypeStruct((t, D_MODEL), F32), jax.ShapeDtypeStruct((SUBLANES, D_MODEL), F32)],
        compiler_params=_params(("arbitrary",)),
    )(dproj, w, dx, x, g)


def _grad_w(a, b, col_blocks, name):
    t, k = a.shape
    n = b.shape[1]
    tk = min(k, 1024)
    per_owner = n // N_DEV
    tn = 2 * per_owner if col_blocks else min(n, 1024)

    def body(a_ref, b_ref, o_ref, at_ref):
        @pl.when(pl.program_id(1) == 0)
        def _():
            for c in range(t // ROW_TILE):
                rows = slice(c * ROW_TILE, (c + 1) * ROW_TILE)
                at_ref[:, rows] = a_ref[rows, :].T

        res = _dot(at_ref[...], b_ref[...]).astype(BF16)
        if col_blocks:
            o_ref[0] = res[:, :per_owner]
            o_ref[1] = res[:, per_owner:]
        else:
            o_ref[...] = res

    if col_blocks:
        out_spec = pl.BlockSpec((2, tk, per_owner), lambda i, j: (j, i, 0))
        out_shape = jax.ShapeDtypeStruct((N_DEV, k, per_owner), BF16)
    else:
        out_spec = pl.BlockSpec((tk, tn), lambda i, j: (i, j))
        out_shape = jax.ShapeDtypeStruct((k, n), BF16)
    return pl.pallas_call(
        body, name=name, grid=(k // tk, n // tn),
        in_specs=[pl.BlockSpec((t, tk), lambda i, j: (0, i)), pl.BlockSpec((t, tn), lambda i, j: (0, j))],
        out_specs=out_spec, out_shape=out_shape,
        scratch_shapes=[pltpu.VMEM((tk, t), BF16)],
        compiler_params=_params(("parallel", "arbitrary")),
    )(a, b)


FF_STEP = 1024
FF_STEPS = D_FF // FF_STEP


def _ffn_fwd(x, g_pre, w1, w2, g_post, name, target=None):
    t = x.shape[0]
    tm = PROJ_TILE

    def body(*refs):
        if target is None:
            x_ref, gp_ref, w1_ref, w2_ref, gq_ref, xo_ref, y_ref, h_ref, r_ref = refs
        else:
            x_ref, gp_ref, w1_ref, w2_ref, gq_ref, t_ref, xo_ref, y_ref, h_ref, r_ref, l_ref = refs
        i, j = pl.program_id(0), pl.program_id(1)

        @pl.when(j == 0)
        def _():
            h_ref[...] = _rms(x_ref[...], gp_ref[...]).astype(BF16)

        a = _dot(h_ref[...], w1_ref[...])
        r = jnp.square(jnp.maximum(a, 0.0)).astype(BF16)
        r_ref[...] = r
        term = _dot(r, w2_ref[...])

        @pl.when(j == 0)
        def _():
            y_ref[...] = term

        @pl.when(j > 0)
        def _():
            y_ref[...] += term

        @pl.when(j == FF_STEPS - 1)
        def _():
            x_new = x_ref[...] + _rms(y_ref[...], gq_ref[...])
            if target is None:
                xo_ref[...] = x_new
            else:
                diff = x_new - t_ref[...]
                xo_ref[...] = diff * (1.0 / D_MODEL)
                _acc_rows8(l_ref, _rows8(diff * diff) * (0.5 / D_MODEL), i == 0)

    row = lambda i, j: (i, 0)
    vec = pl.BlockSpec((1, D_MODEL), lambda i, j: (0, 0))
    in_specs = [pl.BlockSpec((tm, D_MODEL), row), vec, pl.BlockSpec((D_MODEL, FF_STEP), lambda i, j: (0, j)),
                pl.BlockSpec((FF_STEP, D_MODEL), lambda i, j: (j, 0)), vec]
    out_specs = [pl.BlockSpec((tm, D_MODEL), row)] * 3 + [pl.BlockSpec((tm, FF_STEP), lambda i, j: (i, j))]
    out_shape = [jax.ShapeDtypeStruct((t, D_MODEL), F32), jax.ShapeDtypeStruct((t, D_MODEL), F32),
                 jax.ShapeDtypeStruct((t, D_MODEL), BF16), jax.ShapeDtypeStruct((t, D_FF), BF16)]
    args = [x, g_pre, w1, w2, g_post]
    if target is not None:
        in_specs.append(pl.BlockSpec((tm, D_MODEL), row))
        out_specs.append(pl.BlockSpec((SUBLANES, D_MODEL), lambda i, j: (0, 0)))
        out_shape.append(jax.ShapeDtypeStruct((SUBLANES, D_MODEL), F32))
        args.append(target)
    return pl.pallas_call(
        body, name=name, grid=(t // tm, FF_STEPS), in_specs=in_specs, out_specs=out_specs, out_shape=out_shape,
        compiler_params=_params(("parallel" if target is None else "arbitrary", "arbitrary")),
    )(*args)


def _ffn_bwd(dxo, x, y, r, g_pre, w1, w2, g_post, name):
    t = x.shape[0]
    tm = ROW_TILE

    def body(dxo_ref, x_ref, y_ref, r_ref, gp_ref, w1_ref, w2_ref, gq_ref,
             dx_ref, dy_ref, da_ref, dgp_ref, dgq_ref, acc_ref):
        i, j = pl.program_id(0), pl.program_id(1)

        @pl.when(j == 0)
        def _():
            dy, dgr = _rms_bwd(y_ref[...], gq_ref[...], dxo_ref[...])
            dy_ref[...] = dy.astype(BF16)
            _acc_rows8(dgq_ref, _rows8(dgr), i == 0)

        dr = _dot_nt(dy_ref[...], w2_ref[...])
        da = (dr * (2.0 * jnp.sqrt(r_ref[...].astype(F32)))).astype(BF16)
        da_ref[...] = da
        term = _dot_nt(da, w1_ref[...])

        @pl.when(j == 0)
        def _():
            acc_ref[...] = term

        @pl.when(j > 0)
        def _():
            acc_ref[...] += term

        @pl.when(j == FF_STEPS - 1)
        def _():
            dxn, dgr = _rms_bwd(x_ref[...], gp_ref[...], acc_ref[...])
            dx_ref[...] = dxo_ref[...] + dxn
            _acc_rows8(dgp_ref, _rows8(dgr), i == 0)

    row = lambda i, j: (i, 0)
    vec = pl.BlockSpec((1, D_MODEL), lambda i, j: (0, 0))
    acc8 = pl.BlockSpec((SUBLANES, D_MODEL), lambda i, j: (0, 0))
    return pl.pallas_call(
        body, name=name, grid=(t // tm, FF_STEPS),
        in_specs=[pl.BlockSpec((tm, D_MODEL), row)] * 3 + [
            pl.BlockSpec((tm, FF_STEP), lambda i, j: (i, j)),
            vec, pl.BlockSpec((D_MODEL, FF_STEP), lambda i, j: (0, j)),
            pl.BlockSpec((FF_STEP, D_MODEL), lambda i, j: (j, 0)), vec],
        out_specs=[pl.BlockSpec((tm, D_MODEL), row), pl.BlockSpec((tm, D_MODEL), row),
                   pl.BlockSpec((tm, FF_STEP), lambda i, j: (i, j)), acc8, acc8],
        out_shape=[jax.ShapeDtypeStruct((t, D_MODEL), F32), jax.ShapeDtypeStruct((t, D_MODEL), BF16),
                   jax.ShapeDtypeStruct((t, D_FF), BF16),
                   jax.ShapeDtypeStruct((SUBLANES, D_MODEL), F32), jax.ShapeDtypeStruct((SUBLANES, D_MODEL), F32)],
        scratch_shapes=[pltpu.VMEM((tm, D_MODEL), F32)],
        compiler_params=_params(("arbitrary", "arbitrary")),
    )(dxo, x, y, r, g_pre, w1, w2, g_post)


def _lower_bound(table):
    e = jnp.exp(table - jnp.max(table, axis=0, keepdims=True))
    return e[0:1, :] / jnp.sum(e, axis=0, keepdims=True)


def _hgrn2_block(q_ref, f_ref, lb):
    tb = f_ref.shape[0]
    sig = _sigmoid(f_ref[...])
    f = lb + (1.0 - lb) * sig
    qraw = q_ref[...]
    sq = _sigmoid(qraw)
    r = lax.broadcasted_iota(jnp.int32, (tb, tb), 0)
    c = lax.broadcasted_iota(jnp.int32, (tb, tb), 1)
    same = (r // SUB_CHUNK) == (c // SUB_CHUNK)
    logf = jnp.log(f)
    gsum = _mask_dot(same & (c <= r), logf)
    glast = _mask_dot(same, logf)
    return dict(sig=sig, f=f, kk=1.0 - f, qraw=qraw, sq=sq, qs=qraw * sq, gsum=gsum,
                eg=jnp.exp(gsum), ekd=jnp.exp(glast - gsum), a=jnp.exp(glast))


def _head_sums(x):
    parts = [jnp.broadcast_to(jnp.sum(x[:, h * HEAD_A:(h + 1) * HEAD_A], axis=1, keepdims=True), (x.shape[0], HEAD_A))
             for h in range(A_HEADS)]
    return jnp.concatenate(parts, axis=1)


def _hgrn2_intra(g, kk, qs, v):
    row = lax.broadcasted_iota(jnp.int32, g.shape, 0)
    o = _head_sums(qs * kk) * v
    for j in range(1, SUB_CHUNK):
        decay = jnp.exp(jnp.where(row >= j, g - pltpu.roll(g, j, 0), NEG))
        o = o + _head_sums(qs * pltpu.roll(kk, j, 0) * decay) * pltpu.roll(v, j, 0)
    return o


def _hgrn2_intra_bwd(g, kk, qs, v, do):
    row = lax.broadcasted_iota(jnp.int32, g.shape, 0)
    dsc = _head_sums(do * v)
    dqs, dkk, dv = dsc * kk, dsc * qs, _head_sums(qs * kk) * do
    for j in range(1, SUB_CHUNK):
        k_dn = pltpu.roll(kk, j, 0)
        decay = jnp.exp(jnp.where(row >= j, g - pltpu.roll(g, j, 0), NEG))
        d_score = _head_sums(do * pltpu.roll(v, j, 0)) * decay
        dqs = dqs + d_score * k_dn
        dkk = dkk + pltpu.roll(d_score * qs, SUB_CHUNK - j, 0)
        dv = dv + pltpu.roll(_head_sums(qs * k_dn * decay) * do, SUB_CHUNK - j, 0)
    return dqs, dkk, dv


def _hgrn2_fwd(proj, lb_table, a_norm, name):
    t = proj.shape[0]
    tb = HGRN_BLOCK
    n_tb = SEQ // tb
    n_seq = t // SEQ
    n_sub = tb // SUB_CHUNK

    def body(q_ref, f_ref, i_ref, g_ref, lbt_ref, an_ref, o_ref, pre_ref, sts_ref, st_ref,
             gs_ref, kk_ref, qs_ref, eg_ref, ekd_ref, a_ref):
        @pl.when(pl.program_id(1) == 0)
        def _():
            st_ref[...] = jnp.zeros_like(st_ref)

        an = an_ref[...]
        blk = _hgrn2_block(q_ref, f_ref, _lower_bound(lbt_ref[...]))
        for ref, key in ((gs_ref, "gsum"), (kk_ref, "kk"), (qs_ref, "qs"), (eg_ref, "eg"), (ekd_ref, "ekd"), (a_ref, "a")):
            ref[...] = blk[key]

        def step(c, carry):
            rows = pl.ds(pl.multiple_of(c * SUB_CHUNK, SUB_CHUNK), SUB_CHUNK)
            kk, qs, v = kk_ref[rows, :], qs_ref[rows, :], i_ref[rows, :]
            o = _hgrn2_intra(gs_ref[rows, :], kk, qs, v)
            qg, kd, vb = (qs * eg_ref[rows, :]).astype(BF16), (kk * ekd_ref[rows, :]).astype(BF16), v.astype(BF16)
            for h in range(A_HEADS):
                lanes = slice(h * HEAD_A, (h + 1) * HEAD_A)
                st = st_ref[h]
                sts_ref[0, c, h] = st
                o_h = o[:, lanes] + _dot_nt(qg[:, lanes], st.astype(BF16))
                st_ref[h] = st * a_ref[rows, lanes][0:1] + _dot_tn(vb[:, lanes], kd[:, lanes])
                pre_ref[rows, lanes] = o_h
                graw = g_ref[rows, lanes]
                o_ref[rows, lanes] = (_rms(o_h, an[:, lanes]) * (graw * _sigmoid(graw))).astype(BF16)
            return carry

        lax.fori_loop(0, n_sub, step, 0, unroll=2)

    def col(k):
        return pl.BlockSpec((tb, A_WIDTH), lambda b, s, k=k: (b * n_tb + s, k))

    out_rows = pl.BlockSpec((tb, A_WIDTH), lambda b, s: (b * n_tb + s, 0))
    return pl.pallas_call(
        body, name=name, grid=(n_seq, n_tb),
        in_specs=[col(0), col(1), col(2), col(3),
                  pl.BlockSpec((3, A_WIDTH), lambda b, s: (0, 0)), pl.BlockSpec((1, A_WIDTH), lambda b, s: (0, 0))],
        out_specs=[out_rows, out_rows,
                   pl.BlockSpec((1, n_sub, A_HEADS, HEAD_A, HEAD_A), lambda b, s: (b * n_tb + s, 0, 0, 0, 0))],
        out_shape=[jax.ShapeDtypeStruct((t, D_MODEL), BF16), jax.ShapeDtypeStruct((t, A_WIDTH), F32),
                   jax.ShapeDtypeStruct((n_seq * n_tb, n_sub, A_HEADS, HEAD_A, HEAD_A), F32)],
        scratch_shapes=[pltpu.VMEM((A_HEADS, HEAD_A, HEAD_A), F32)] + [pltpu.VMEM((tb, A_WIDTH), F32)] * 6,
        compiler_params=_params(("parallel", "arbitrary")),
    )(proj, proj, proj, proj, lb_table, a_norm)


def _hgrn2_bwd(proj, dcat, pre, states, lb_table, a_norm, name):
    t = proj.shape[0]
    tb = HGRN_BLOCK
    n_tb = SEQ // tb
    n_seq = t // SEQ
    n_sub = tb // SUB_CHUNK

    def body(q_ref, f_ref, i_ref, g_ref, do_ref, pre_ref, sts_ref, lbt_ref, an_ref, dp_ref, dlb_ref, dan_ref, dst_ref,
             gs_ref, kk_ref, qs_ref, eg_ref, ekd_ref, a_ref, dpre_ref, dlf_ref, dqs_ref, dkk_ref):
        b, s = pl.program_id(0), pl.program_id(1)

        @pl.when(s == 0)
        def _():
            dst_ref[...] = jnp.zeros_like(dst_ref)

        @pl.when((b == 0) & (s == 0))
        def _():
            dlb_ref[...] = jnp.zeros_like(dlb_ref)
            dan_ref[...] = jnp.zeros_like(dan_ref)

        lb = _lower_bound(lbt_ref[...])
        an = an_ref[...]
        heads = [slice(h * HEAD_A, (h + 1) * HEAD_A) for h in range(A_HEADS)]
        blk = _hgrn2_block(q_ref, f_ref, lb)
        for ref, key in ((gs_ref, "gsum"), (kk_ref, "kk"), (qs_ref, "qs"), (eg_ref, "eg"), (ekd_ref, "ekd"), (a_ref, "a")):
            ref[...] = blk[key]
        for h, lanes in enumerate(heads):
            graw, o = g_ref[:, lanes], pre_ref[:, lanes]
            sg = _sigmoid(graw)
            dout = do_ref[:, lanes]
            d_o, dgr = _rms_bwd(o, an[:, lanes], dout * (graw * sg))
            dan_ref[0:1, lanes] += jnp.sum(dgr, axis=0, keepdims=True)
            dp_ref[:, 3 * A_WIDTH + h * HEAD_A:3 * A_WIDTH + (h + 1) * HEAD_A] = (
                dout * _rms(o, an[:, lanes]) * (sg * (1.0 + graw * (1.0 - sg)))).astype(BF16)
            dpre_ref[:, lanes] = d_o

        tri_t = (lax.broadcasted_iota(jnp.int32, (SUB_CHUNK, SUB_CHUNK), 0)
                 <= lax.broadcasted_iota(jnp.int32, (SUB_CHUNK, SUB_CHUNK), 1)).astype(F32)

        def back(k, carry):
            c = n_sub - 1 - k
            rows = pl.ds(pl.multiple_of(c * SUB_CHUNK, SUB_CHUNK), SUB_CHUNK)
            g, kk, qs, v, d_o = gs_ref[rows, :], kk_ref[rows, :], qs_ref[rows, :], i_ref[rows, :], dpre_ref[rows, :]
            eg, ekd, a = eg_ref[rows, :], ekd_ref[rows, :], a_ref[rows, :]
            dqs, dkk, dv = _hgrn2_intra_bwd(g, kk, qs, v, d_o)
            qg_f, kd_f = qs * eg, kk * ekd
            qg, kd, vb, dob = qg_f.astype(BF16), kd_f.astype(BF16), v.astype(BF16), d_o.astype(BF16)
            dqg, dkd, da, dv_st = [], [], [], []
            for h, lanes in enumerate(heads):
                st, dst = sts_ref[0, c, h], dst_ref[h]
                dstb = dst.astype(BF16)
                dqg.append(_dot(dob[:, lanes], st.astype(BF16)))
                dv_st.append(_dot_nt(kd[:, lanes], dstb))
                dkd.append(_dot(vb[:, lanes], dstb))
                da.append(jnp.broadcast_to(jnp.sum(dst * st, axis=0, keepdims=True), (SUB_CHUNK, HEAD_A)))
                dst_ref[h] = dst * a[0:1, lanes] + _dot_tn(dob[:, lanes], qg[:, lanes])
            dqg, dkd, da, dv_st = [jnp.concatenate(p, axis=1) for p in (dqg, dkd, da, dv_st)]
            d_gsum = qs * dqs - kk * dkk + dqg * qg_f - dkd * kd_f
            d_glast = jnp.sum(dkd * kd_f, axis=0, keepdims=True) + da * a
            dlf_ref[rows, :] = jnp.dot(tri_t, d_gsum, precision=lax.Precision.HIGHEST,
                                       preferred_element_type=F32) + d_glast
            dqs_ref[rows, :] = dqs + dqg * eg
            dkk_ref[rows, :] = dkk + dkd * ekd
            dp_ref[rows, 2 * A_WIDTH:3 * A_WIDTH] = (dv + dv_st).astype(BF16)
            return carry

        lax.fori_loop(0, n_sub, back, 0, unroll=2)
        sig, sq, qraw = blk["sig"], blk["sq"], blk["qraw"]
        d_f = dlf_ref[...] / blk["f"] - dkk_ref[...]
        dlb_ref[0:1, :] += jnp.sum(d_f * (1.0 - sig), axis=0, keepdims=True)
        dp_ref[:, 0:A_WIDTH] = (dqs_ref[...] * (sq * (1.0 + qraw * (1.0 - sq)))).astype(BF16)
        dp_ref[:, A_WIDTH:2 * A_WIDTH] = (d_f * (1.0 - lb) * sig * (1.0 - sig)).astype(BF16)

    def rev(s):
        return n_tb - 1 - s

    def col(k):
        return pl.BlockSpec((tb, A_WIDTH), lambda b, s, k=k: (b * n_tb + rev(s), k))

    acc8 = pl.BlockSpec((SUBLANES, A_WIDTH), lambda b, s: (0, 0))
    return pl.pallas_call(
        body, name=name, grid=(n_seq, n_tb),
        in_specs=[col(0), col(1), col(2), col(3), col(0), col(0),
                  pl.BlockSpec((1, n_sub, A_HEADS, HEAD_A, HEAD_A), lambda b, s: (b * n_tb + rev(s), 0, 0, 0, 0)),
                  pl.BlockSpec((3, A_WIDTH), lambda b, s: (0, 0)), pl.BlockSpec((1, A_WIDTH), lambda b, s: (0, 0))],
        out_specs=[pl.BlockSpec((tb, 4 * A_WIDTH), lambda b, s: (b * n_tb + rev(s), 0)), acc8, acc8],
        out_shape=[jax.ShapeDtypeStruct((t, EVEN_IN), BF16)] + [jax.ShapeDtypeStruct((SUBLANES, A_WIDTH), F32)] * 2,
        scratch_shapes=[pltpu.VMEM((A_HEADS, HEAD_A, HEAD_A), F32)] + [pltpu.VMEM((tb, A_WIDTH), F32)] * 10,
        compiler_params=_params(("arbitrary", "arbitrary")),
    )(proj, proj, proj, proj, dcat, pre, states, lb_table, a_norm)


GMLP_ROWS = 512


def _gmlp_chunk(ub, vb, ln_g, ln_b, ws, bias):
    u = [_gelu(a) for a in ub]
    v = [_gelu(a) for a in vb]
    mu = sum(jnp.sum(a, axis=-1, keepdims=True) for a in v) * (1.0 / B_WIDTH)
    cen = [a - mu for a in v]
    var = sum(jnp.sum(a * a, axis=-1, keepdims=True) for a in cen) * (1.0 / B_WIDTH)
    inv = lax.rsqrt(var + EPS)
    r = lax.broadcasted_iota(jnp.int32, (B_CHUNK, B_CHUNK), 0)
    c = lax.broadcasted_iota(jnp.int32, (B_CHUNK, B_CHUNK), 1)
    outs = []
    for g in range(B_GROUPS):
        vn = (cen[g] * inv * ln_g[g] + ln_b[g]).astype(BF16)
        wm = jnp.where(c <= r, ws[g], 0.0).astype(BF16)
        outs.append(u[g] * (_dot(wm, vn) + bias[g]))
    return outs


def _lane_groups(ref, rows=slice(None)):
    return [ref[rows, g * LANES:(g + 1) * LANES] for g in range(B_GROUPS)]


def _gmlp_fwd(proj, mixed, ln_g, ln_b, ws, bias_t, name):
    t = proj.shape[0]
    tm = GMLP_ROWS

    def body(u_ref, v_ref, lg_ref, lb_ref, ws_ref, bt_ref, _, o_ref):
        for ch in range(tm // B_CHUNK):
            rows = slice(ch * B_CHUNK, (ch + 1) * B_CHUNK)
            outs = _gmlp_chunk(_lane_groups(u_ref, rows), _lane_groups(v_ref, rows), _lane_groups(lg_ref),
                               _lane_groups(lb_ref), [ws_ref[g] for g in range(B_GROUPS)],
                               [bt_ref[:, g:g + 1] for g in range(B_GROUPS)])
            for g in range(B_GROUPS):
                o_ref[rows, g * LANES:(g + 1) * LANES] = outs[g].astype(BF16)

    vec = pl.BlockSpec((1, B_WIDTH), lambda i: (0, 0))
    return pl.pallas_call(
        body, name=name, grid=(t // tm,),
        in_specs=[pl.BlockSpec((tm, B_WIDTH), lambda i: (i, 4)), pl.BlockSpec((tm, B_WIDTH), lambda i: (i, 5)), vec, vec,
                  pl.BlockSpec((B_GROUPS, B_CHUNK, B_CHUNK), lambda i: (0, 0, 0)),
                  pl.BlockSpec((B_CHUNK, B_GROUPS), lambda i: (0, 0)), pl.BlockSpec(memory_space=pl.ANY)],
        out_specs=pl.BlockSpec((tm, B_WIDTH), lambda i: (i, 1)),
        out_shape=jax.ShapeDtypeStruct(mixed.shape, BF16),
        input_output_aliases={6: 0},
        compiler_params=_params(("parallel",)),
    )(proj, proj, ln_g, ln_b, ws, bias_t, mixed)


def _gmlp_bwd(proj, dcat, dproj, ln_g, ln_b, ws, bias_t, name):
    t = proj.shape[0]
    tm = GMLP_ROWS

    def body(u_ref, v_ref, do_ref, lg_ref, lb_ref, ws_ref, bt_ref, _, duv_ref, dlg_ref, dlb_ref, dws_ref, dbt_ref):
        @pl.when(pl.program_id(0) == 0)
        def _():
            dlg_ref[...] = jnp.zeros_like(dlg_ref)
            dlb_ref[...] = jnp.zeros_like(dlb_ref)
            dws_ref[...] = jnp.zeros_like(dws_ref)
            dbt_ref[...] = jnp.zeros_like(dbt_ref)

        for ch in range(tm // B_CHUNK):
            rows = slice(ch * B_CHUNK, (ch + 1) * B_CHUNK)
            _, vjp = jax.vjp(
                _gmlp_chunk, _lane_groups(u_ref, rows), _lane_groups(v_ref, rows), _lane_groups(lg_ref),
                _lane_groups(lb_ref), [ws_ref[g] for g in range(B_GROUPS)],
                [bt_ref[:, g:g + 1] for g in range(B_GROUPS)])
            du, dv, dlg, dlb, dw, dbt = vjp(_lane_groups(do_ref, rows))
            for g in range(B_GROUPS):
                lanes = slice(g * LANES, (g + 1) * LANES)
                duv_ref[rows, lanes] = du[g].astype(BF16)
                duv_ref[rows, B_WIDTH + g * LANES:B_WIDTH + (g + 1) * LANES] = dv[g].astype(BF16)
                dlg_ref[0:1, lanes] += dlg[g]
                dlb_ref[0:1, lanes] += dlb[g]
                dws_ref[g] += dw[g]
                dbt_ref[:, g:g + 1] += dbt[g]

    vec = pl.BlockSpec((1, B_WIDTH), lambda i: (0, 0))
    acc8 = pl.BlockSpec((SUBLANES, B_WIDTH), lambda i: (0, 0))
    ws_spec = pl.BlockSpec((B_GROUPS, B_CHUNK, B_CHUNK), lambda i: (0, 0, 0))
    bt_spec = pl.BlockSpec((B_CHUNK, B_GROUPS), lambda i: (0, 0))
    return pl.pallas_call(
        body, name=name, grid=(t // tm,),
        in_specs=[pl.BlockSpec((tm, B_WIDTH), lambda i: (i, 4)), pl.BlockSpec((tm, B_WIDTH), lambda i: (i, 5)),
                  pl.BlockSpec((tm, B_WIDTH), lambda i: (i, 1)), vec, vec, ws_spec, bt_spec,
                  pl.BlockSpec(memory_space=pl.ANY)],
        out_specs=[pl.BlockSpec((tm, 2 * B_WIDTH), lambda i: (i, 2)), acc8, acc8, ws_spec, bt_spec],
        out_shape=[jax.ShapeDtypeStruct(dproj.shape, BF16), jax.ShapeDtypeStruct((SUBLANES, B_WIDTH), F32),
                   jax.ShapeDtypeStruct((SUBLANES, B_WIDTH), F32),
                   jax.ShapeDtypeStruct((B_GROUPS, B_CHUNK, B_CHUNK), F32),
                   jax.ShapeDtypeStruct((B_CHUNK, B_GROUPS), F32)],
        input_output_aliases={7: 0},
        compiler_params=_params(("arbitrary",)),
    )(proj, proj, dcat, ln_g, ln_b, ws, bias_t, dproj)


QK_SCALE = 1.0 / math.sqrt(C_HEAD_DIM)
ATTN_UNROLL = 16
ATTN_PAIRS = 2
LANE_GROUPS = D_MODEL // LANES
ATTN_STEPS = LANE_GROUPS // ATTN_PAIRS
Q_BLOCKS = SEQ // C_BLOCK


def _attn_window(i, d):
    sub_blocks = Q_BLOCKS // d
    q0 = pl.multiple_of(i * C_BLOCK, C_BLOCK)
    k0 = pl.multiple_of(jnp.maximum(i - 1, 0) * C_BLOCK, C_BLOCK)
    key = k0 + lax.broadcasted_iota(jnp.int32, (C_BLOCK, 2 * C_BLOCK), 1)
    dist = (q0 + lax.broadcasted_iota(jnp.int32, (C_BLOCK, 2 * C_BLOCK), 0)) - key
    own_subsequence = (key >= q0) | (i % sub_blocks > 0)
    return pl.ds(q0, C_BLOCK), pl.ds(k0, 2 * C_BLOCK), (dist >= 0) & (dist <= C_BLOCK) & own_subsequence


def _head_masks():
    lane = lax.broadcasted_iota(jnp.int32, (C_BLOCK, LANES), 1)
    return [lane < C_HEAD_DIM, lane >= C_HEAD_DIM]


def _flat_spec(col_of):
    return pl.BlockSpec((1, SEQ, ATTN_PAIRS * LANES), lambda b, g: (b, 0, col_of(g)))


def _put_heads(tile, g, col0, col1):
    lane = lax.broadcasted_iota(jnp.int32, tile.shape, 1)
    return jnp.where(lane == 2 * g, col0, jnp.where(lane == 2 * g + 1, col1, tile))


def _get_head(tile, h):
    lane = lax.broadcasted_iota(jnp.int32, tile.shape, 1)
    return jnp.sum(jnp.where(lane == h, tile, 0.0), axis=1, keepdims=True)


PER_HEAD_SPEC = pl.BlockSpec((1, SEQ, LANES), lambda b, g: (b, 0, 0))


def _attn_branch_fwd(qkv, name):
    n_seq, d, l, _ = qkv.shape
    flat = qkv.reshape(n_seq, SEQ, ODD_IN)

    def body(q_ref, k_ref, v_ref, o_ref, m_ref, l_ref):
        heads = _head_masks()
        g = pl.program_id(1)

        @pl.when(g == 0)
        def _():
            m_ref[...] = jnp.zeros_like(m_ref)
            l_ref[...] = jnp.zeros_like(l_ref)

        def block(i, carry):
            rows, keys, mask = _attn_window(i, d)
            m_tile, l_tile = m_ref[0, rows, :], l_ref[0, rows, :]
            for pair in range(ATTN_PAIRS):
                lanes = slice(pair * LANES, (pair + 1) * LANES)
                q, k, v = q_ref[0, rows, lanes], k_ref[0, keys, lanes], v_ref[0, keys, lanes]
                res = []
                for hm in heads:
                    s = jnp.where(mask, _dot_nt(jnp.where(hm, q, 0), k), NEG)
                    m = jnp.max(s, axis=-1, keepdims=True)
                    p = jnp.exp(s - m)
                    res.append((_dot(p.astype(BF16), v), m, jnp.sum(p, axis=-1, keepdims=True)))
                o_ref[0, rows, lanes] = jnp.where(heads[0], res[0][0], res[1][0])
                m_tile = _put_heads(m_tile, g * ATTN_PAIRS + pair, res[0][1], res[1][1])
                l_tile = _put_heads(l_tile, g * ATTN_PAIRS + pair, res[0][2], res[1][2])
            m_ref[0, rows, :] = m_tile
            l_ref[0, rows, :] = l_tile
            return carry

        lax.fori_loop(0, Q_BLOCKS, block, 0, unroll=ATTN_UNROLL)

    o, m, l_sum = pl.pallas_call(
        body, name=name, grid=(n_seq, ATTN_STEPS),
        in_specs=[_flat_spec(lambda g: g), _flat_spec(lambda g: ATTN_STEPS + g),
                  _flat_spec(lambda g: 2 * ATTN_STEPS + g)],
        out_specs=[_flat_spec(lambda g: g), PER_HEAD_SPEC, PER_HEAD_SPEC],
        out_shape=[jax.ShapeDtypeStruct((n_seq, SEQ, D_MODEL), F32)] + [jax.ShapeDtypeStruct((n_seq, SEQ, LANES), F32)] * 2,
        compiler_params=_params(("parallel", "arbitrary")),
    )(flat, flat, flat)
    return [o.reshape(n_seq, d, l, D_MODEL), m.reshape(n_seq, d, l, LANES), l_sum.reshape(n_seq, d, l, LANES)]


def _attn_merge(branches, name):
    n_seq = branches[0][0].shape[0]
    t = n_seq * SEQ
    tm = MERGE_TILE

    def body(*refs):
        ins = refs[:9]
        o_ref, ob_ref, lse_ref = refs[9:12]
        nat = refs[12:]
        for b, d in enumerate(C_DILATIONS[1:]):
            for k in range(3):
                _load_dilated(ins[3 + 3 * b + k], d, nat[3 * b + k])
        ms = [ins[1][0, 0], nat[1][0], nat[4][0]]
        ls = [ins[2][0, 0], nat[2][0], nat[5][0]]
        m_all = jnp.maximum(jnp.maximum(ms[0], ms[1]), ms[2])
        ws = [jnp.exp(ms[b] - m_all) for b in range(3)]
        lane = lax.broadcasted_iota(jnp.int32, m_all.shape, 1)
        total = jnp.where(lane < C_HEADS, ws[0] * ls[0] + ws[1] * ls[1] + ws[2] * ls[2], 1.0)
        lse_ref[...] = m_all + jnp.log(total)
        first_head = lane < C_HEAD_DIM
        for p in range(LANE_GROUPS):
            lanes = slice(p * LANES, (p + 1) * LANES)
            spread = lambda c: jnp.where(first_head, c[:, 2 * p:2 * p + 1], c[:, 2 * p + 1:2 * p + 2])
            os_ = [ins[0][0, 0, :, lanes], nat[0][p], nat[3][p]]
            o = (spread(ws[0]) * os_[0] + spread(ws[1]) * os_[1] + spread(ws[2]) * os_[2]) / spread(total)
            o_ref[:, lanes] = o
            ob_ref[:, lanes] = o.astype(BF16)

    row = pl.BlockSpec((tm, D_MODEL), lambda i: (i, 0))
    flat = [a for br in branches for a in br]
    in_specs = []
    for wide, narrow in zip(_dilated_specs(tm, D_MODEL, lambda: 0), _dilated_specs(tm, LANES, lambda: 0)):
        in_specs += [wide, narrow, narrow]
    per_head = pltpu.VMEM((1, tm, LANES), F32)
    return pl.pallas_call(
        body, name=name, grid=(t // tm,), in_specs=in_specs,
        out_specs=[row, row, pl.BlockSpec((tm, LANES), lambda i: (i, 0))],
        out_shape=[jax.ShapeDtypeStruct((t, D_MODEL), F32), jax.ShapeDtypeStruct((t, D_MODEL), BF16),
                   jax.ShapeDtypeStruct((t, LANES), F32)],
        scratch_shapes=[pltpu.VMEM((LANE_GROUPS, tm, LANES), F32), per_head, per_head] * 2,
        compiler_params=_params(("parallel",)),
    )(*flat)


def _attn_branch_bwd(qkv, dout, lse, delta, name):
    n_seq, d, l, _ = qkv.shape
    flat = lambda a: a.reshape(n_seq, SEQ, a.shape[-1])

    def body(q_ref, k_ref, v_ref, do_ref, lse_nat_ref, dl_nat_ref, dq_ref, dk_ref, dv_ref, lse_ref, dl_ref,
             dkt_ref, dvt_ref):
        heads = _head_masks()
        g = pl.program_id(1)
        dkt_ref[...] = jnp.zeros_like(dkt_ref)
        dvt_ref[...] = jnp.zeros_like(dvt_ref)
        for nat_ref, dst_ref in ((lse_nat_ref, lse_ref), (dl_nat_ref, dl_ref)):
            for r in range(d):
                rows = pl.ds(r, l, stride=d) if d > 1 else slice(None)
                dst_ref[r * l:(r + 1) * l, :] = nat_ref.at[0][rows, :]

        def block(i, carry):
            rows, keys, mask = _attn_window(i, d)
            lse_b, dl_b = lse_ref[rows, :], dl_ref[rows, :]
            for pair in range(ATTN_PAIRS):
                lanes = slice(pair * LANES, (pair + 1) * LANES)
                q, do = q_ref[0, rows, lanes], do_ref[0, rows, lanes]
                k, v = k_ref[0, keys, lanes], v_ref[0, keys, lanes]
                dq, dk, dv = [], None, None
                for hh, hm in enumerate(heads):
                    head = 2 * (g * ATTN_PAIRS + pair) + hh
                    qh, doh = jnp.where(hm, q, 0), jnp.where(hm, do, 0)
                    s = jnp.where(mask, _dot_nt(qh, k), NEG)
                    p = jnp.exp(s - _get_head(lse_b, head))
                    ds = (p * (_dot_nt(doh, v) - _get_head(dl_b, head))).astype(BF16)
                    dq.append(_dot(ds, k) * QK_SCALE)
                    dk_h, dv_h = _dot_tn(qh, ds), _dot_tn(doh, p.astype(BF16))
                    dk = dk_h if dk is None else dk + dk_h
                    dv = dv_h if dv is None else dv + dv_h
                dq_ref[0, rows, lanes] = jnp.where(heads[0], dq[0], dq[1]).astype(BF16)
                dkt_ref[lanes, keys] += dk
                dvt_ref[lanes, keys] += dv
            return carry

        lax.fori_loop(0, Q_BLOCKS, block, 0, unroll=ATTN_UNROLL)
        for c in range(SEQ // ROW_TILE):
            rows = slice(c * ROW_TILE, (c + 1) * ROW_TILE)
            dk_ref[0, rows, :] = dkt_ref[:, rows].T.astype(BF16)
            dv_ref[0, rows, :] = dvt_ref[:, rows].T.astype(BF16)

    act = _flat_spec(lambda g: g)
    outs = pl.pallas_call(
        body, name=name, grid=(n_seq, ATTN_STEPS),
        in_specs=[_flat_spec(lambda g: g), _flat_spec(lambda g: ATTN_STEPS + g),
                  _flat_spec(lambda g: 2 * ATTN_STEPS + g), act, PER_HEAD_SPEC, PER_HEAD_SPEC],
        out_specs=[act] * 3,
        out_shape=[jax.ShapeDtypeStruct((n_seq, SEQ, D_MODEL), BF16)] * 3,
        scratch_shapes=[pltpu.VMEM((SEQ, LANES), F32)] * 2 + [pltpu.VMEM((ATTN_PAIRS * LANES, SEQ), F32)] * 2,
        compiler_params=_params(("parallel", "parallel")),
    )(flat(qkv), flat(qkv), flat(qkv), flat(dout), lse, delta)
    return [o.reshape(n_seq, d, l, D_MODEL) for o in outs]


def _attn_combine_bwd(grads, rope, name):
    n_seq = grads[0][0].shape[0]
    t = n_seq * SEQ
    tm = MERGE_TILE

    def body(*refs):
        c_ref, s_ref, o_ref, nat4_ref, nat16_ref = refs[9:]
        for sec in range(3):
            _load_dilated(refs[3 + sec], 4, nat4_ref)
            _load_dilated(refs[6 + sec], 16, nat16_ref)
            for p in range(LANE_GROUPS):
                blk = refs[sec][0, 0, :, p * LANES:(p + 1) * LANES] + nat4_ref[p] + nat16_ref[p]
                if sec < 2:
                    blk = blk * c_ref[...] - _swap_halves(blk) * s_ref[...]
                o_ref[:, sec * D_MODEL + p * LANES:sec * D_MODEL + (p + 1) * LANES] = blk.astype(BF16)

    tab = pl.BlockSpec((tm, LANES), lambda i: (i, 0))
    flat = [a for br in grads for a in br]
    in_specs = []
    for spec in _dilated_specs(tm, D_MODEL, lambda: 0):
        in_specs += [spec] * 3
    return pl.pallas_call(
        body, name=name, grid=(t // tm,), in_specs=in_specs + [tab, tab],
        out_specs=pl.BlockSpec((tm, ODD_IN), lambda i: (i, 0)),
        out_shape=jax.ShapeDtypeStruct((t, ODD_IN), BF16),
        scratch_shapes=[pltpu.VMEM((LANE_GROUPS, tm, LANES), F32)] * 2,
        compiler_params=_params(("parallel",)),
    )(*flat, *rope)


def _adamw(w, g, m, v):
    m = ADAM_B1 * m + (1.0 - ADAM_B1) * g
    v = ADAM_B2 * v + (1.0 - ADAM_B2) * jnp.square(g)
    m_hat = m / (1.0 - ADAM_B1 ** ADAM_STEP)
    v_hat = v / (1.0 - ADAM_B2 ** ADAM_STEP)
    delta = -ADAM_LR * (m_hat / (jnp.sqrt(v_hat) + ADAM_EPS) + ADAM_WD * w)
    return delta, m, v


def _adamw_sharded(parts, w, m, v, after, name):
    n_layers, rows, cols = w.shape
    tr = min(rows, 256)

    def body(*refs):
        p_refs = refs[:n_layers]
        w_ref, m_ref, v_ref, _, g_ref, d_ref, mo_ref, vo_ref = refs[n_layers:]
        layer = pl.program_id(0)
        g = None
        for l, p_ref in enumerate(p_refs):
            g_l = p_ref[0].astype(F32)
            for s in range(1, N_DEV):
                g_l = g_l + p_ref[s].astype(F32)
            g = g_l if g is None else jnp.where(layer == l, g_l, g)
        delta, mn, vn = _adamw(w_ref[0], g, m_ref[0], v_ref[0])
        g_ref[0] = g
        d_ref[0] = delta
        mo_ref[0] = mn
        vo_ref[0] = vn

    def part_spec(l):
        return pl.BlockSpec((N_DEV, tr, cols), lambda a, i: (0, jnp.where(a == l, i, 0), 0))

    row = pl.BlockSpec((1, tr, cols), lambda a, i: (a, i, 0))
    return pl.pallas_call(
        body, name=name, grid=(n_layers, rows // tr),
        in_specs=[part_spec(l) for l in range(n_layers)] + [row, row, row, pl.BlockSpec(memory_space=pl.ANY)],
        out_specs=[row] * 4, out_shape=[jax.ShapeDtypeStruct(w.shape, F32)] * 4,
        compiler_params=_params(("arbitrary", "arbitrary")),
    )(*parts, w, m, v, after)


def _small_update(gathered, where, weights, moments_m, moments_v, lb_index, name):
    n = len(weights)
    n_g = len(gathered)

    def body(*refs):
        g_refs = refs[:n_g]
        w_refs, m_refs, v_refs = refs[n_g:n_g + n], refs[n_g + n:n_g + 2 * n], refs[n_g + 2 * n:n_g + 3 * n]
        outs = refs[n_g + 3 * n:]

        def total(k):
            array, rows, lanes = where[k]
            ref = g_refs[array]
            index = (slice(None),) * (len(ref.shape) - 1) if rows is None else (rows, lanes)
            acc = ref[(0,) + index]
            for s in range(1, N_DEV):
                acc = acc + ref[(s,) + index]
            return acc

        loss_rows = total(n)
        outs[0][...] = jnp.sum(jnp.sum(loss_rows, axis=1, keepdims=True), axis=0, keepdims=True)
        for k in range(n):
            part = total(k)
            if k == lb_index:
                dlb = jnp.sum(part, axis=0, keepdims=True)
                tab = w_refs[k][...]
                e = jnp.exp(tab - jnp.max(tab, axis=0, keepdims=True))
                p = e / jnp.sum(e, axis=0, keepdims=True)
                first = lax.broadcasted_iota(jnp.int32, p.shape, 0) == 0
                grads = [(slice(None), p * (jnp.where(first, dlb, 0.0) - p[0:1, :] * dlb))]
            elif part.shape == w_refs[k].shape:
                grads = [(slice(None), part)]
            else:
                grads = [(slice(l, l + 1), jnp.sum(part[l * SUBLANES:(l + 1) * SUBLANES], axis=0, keepdims=True))
                         for l in range(w_refs[k].shape[0])]
            for rows, g in grads:
                delta, mn, vn = _adamw(w_refs[k][rows], g, m_refs[k][rows], v_refs[k][rows])
                outs[1 + 4 * k][rows] = g
                outs[2 + 4 * k][rows] = delta
                outs[3 + 4 * k][rows] = mn
                outs[4 + 4 * k][rows] = vn

    vmem = pl.BlockSpec(memory_space=pltpu.VMEM)
    out_shape = [jax.ShapeDtypeStruct((1, 1), F32)]
    for w in weights:
        out_shape += [jax.ShapeDtypeStruct(w.shape, F32)] * 4
    args = list(gathered) + list(weights) + list(moments_m) + list(moments_v)
    return pl.pallas_call(
        body, name=name, in_specs=[vmem] * len(args), out_specs=[vmem] * len(out_shape), out_shape=out_shape,
        compiler_params=pltpu.CompilerParams(vmem_limit_bytes=VMEM_LIMIT),
    )(*args)


def kernel(x, positions, norm_mix_pre, norm_mix_post, norm_ffn_pre, norm_ffn_post, w_in_even, lb_table, a_norm, b_ln_g, b_ln_b, b_ws, b_bias, w_out_even, w_in_odd, w_out_odd, w_ff1, w_ff2, loss_target, m_norm_mix_pre, m_norm_mix_post, m_norm_ffn_pre, m_norm_ffn_post, m_w_in_even, m_lb_table, m_a_norm, m_b_ln_g, m_b_ln_b, m_b_ws, m_b_bias, m_w_out_even, m_w_in_odd, m_w_out_odd, m_w_ff1, m_w_ff2, v_norm_mix_pre, v_norm_mix_post, v_norm_ffn_pre, v_norm_ffn_post, v_w_in_even, v_lb_table, v_a_norm, v_b_ln_g, v_b_ln_b, v_b_ws, v_b_bias, v_w_out_even, v_w_in_odd, v_w_out_odd, v_w_ff1, v_w_ff2):
    n_seq = x.shape[0]
    t = n_seq * SEQ
    x0 = x.reshape(t, D_MODEL)
    target = loss_target.reshape(t, D_MODEL)

    me = _my_slot().astype(jnp.int32).reshape(1)

    order = ["in_e", "out_e", "ff1_0", "ff2_0", "in_o", "out_o", "ff1_1", "ff2_1"]
    shards = dict(in_e=w_in_even[0], out_e=w_out_even[0], in_o=w_in_odd[0], out_o=w_out_odd[0],
                  ff1_0=w_ff1[0], ff1_1=w_ff1[1], ff2_0=w_ff2[0], ff2_1=w_ff2[1])
    by_columns = ("in_e", "in_o", "ff1_0", "ff1_1")

    def place(k, after):
        if k in by_columns:
            return _place_own_columns(shards[k], me, "place_" + k, after)
        return _place_own(shards[k], me, "place_" + k, False, after=after)

    gathers = {}
    send0, recv0, land0, _, token0 = _exchange_start([place(order[0], None)], [None], "gather_start_first")
    gathers[order[0]] = (land0[0], send0[0], recv0[0])
    sends, recvs, lands, _, g_token = _exchange_start([place(k, token0) for k in order[1:]],
                                                      [None] * (len(order) - 1), "gather_start")
    for k, land, send, recv in zip(order[1:], lands, sends, recvs):
        gathers[k] = (land, send, recv)

    def get_w(keys, after):
        lands_k, sends_k, recvs_k = zip(*[gathers[k] for k in keys])
        return _exchange_wait(list(lands_k), [None] * len(keys), list(sends_k), list(recvs_k), after,
                              "gather_wait_" + keys[0])

    sent = {}

    def put_g(group, blocks):
        keys = list(blocks)
        own = [_place_own(blocks[k], me, "own_" + k, True) for k in keys]
        send_sems, recv_sems, own, srcs, token = _exchange_start(own, [blocks[k] for k in keys], "scatter_start_" + group)
        sent[group] = (keys, own, srcs, send_sems, recv_sems)
        return token

    rope = _rope_tables(positions)
    bias_t = b_bias[0].T
    grads = _local_step(x0, target, rope, norm_mix_pre, norm_mix_post, norm_ffn_pre, norm_ffn_post, lb_table,
                        a_norm, b_ln_g, b_ln_b, b_ws[0], bias_t, get_w, put_g, g_token)
    (dx0, loss_part, dg_mix_pre, dg_mix_post, dg_ffn_pre, dg_ffn_post, d_lb, d_a_norm, d_ln_g, d_ln_b, d_ws,
     d_bias_t) = grads

    packed = jnp.concatenate([dg_mix_pre, dg_mix_post, dg_ffn_pre, dg_ffn_post,
                              jnp.concatenate([d_lb, d_a_norm], axis=1), jnp.concatenate([d_ln_g, d_ln_b], axis=1),
                              loss_part], axis=0)
    small_lands = [_place_own(a, me, "own_small%d" % k, False, F32) for k, a in enumerate((packed, d_ws, d_bias_t))]
    s_send, s_recv, small_lands, _, after = _exchange_start(small_lands, [None] * 3, "gather_small_start")

    big = dict(w_in_even=(["in_e"], w_in_even, m_w_in_even, v_w_in_even),
               w_out_even=(["out_e"], w_out_even, m_w_out_even, v_w_out_even),
               w_in_odd=(["in_o"], w_in_odd, m_w_in_odd, v_w_in_odd),
               w_out_odd=(["out_o"], w_out_odd, m_w_out_odd, v_w_out_odd),
               w_ff1=(["ff1_0", "ff1_1"], w_ff1, m_w_ff1, v_w_ff1), w_ff2=(["ff2_0", "ff2_1"], w_ff2, m_w_ff2, v_w_ff2))
    recv, big_out = {}, {}
    for groups, names in ((("ffn1", "ffn0"), ("w_ff1", "w_ff2")), (("mix1",), ("w_in_odd", "w_out_odd")),
                          (("mix0",), ("w_in_even", "w_out_even"))):
        for group in groups:
            keys, own, srcs, send_sems, recv_sems = sent[group]
            recv.update(zip(keys, _exchange_wait(own, srcs, send_sems, recv_sems, after, "scatter_wait_" + group)))
        for nm in names:
            keys, w, m, v = big[nm]
            big_out[nm] = _adamw_sharded([recv[k] for k in keys], w, m, v, after, "adamw_" + nm)
            after = big_out[nm][0]
    big_out = [big_out[nm] for nm in ("w_in_even", "w_out_even", "w_in_odd", "w_out_odd", "w_ff1", "w_ff2")]
    gathered = _exchange_wait(small_lands, [None] * 3, s_send, s_recv, after, "gather_small_wait")
    rows8 = lambda k: slice(SUBLANES * k, SUBLANES * (k + 1))
    left, right, every = slice(0, A_WIDTH), slice(A_WIDTH, 2 * A_WIDTH), slice(None)
    where = [(0, slice(0, 16), every), (0, slice(16, 32), every), (0, slice(32, 48), every), (0, slice(48, 64), every),
             (0, rows8(8), left), (0, rows8(8), right), (0, rows8(9), left), (0, rows8(9), right),
             (1, None, None), (2, None, None), (0, rows8(10), every)]
    small_w = [norm_mix_pre, norm_mix_post, norm_ffn_pre, norm_ffn_post, lb_table, a_norm, b_ln_g, b_ln_b,
               b_ws[0], bias_t]
    small_m = [m_norm_mix_pre, m_norm_mix_post, m_norm_ffn_pre, m_norm_ffn_post, m_lb_table, m_a_norm, m_b_ln_g,
               m_b_ln_b, m_b_ws[0], m_b_bias[0].T]
    small_v = [v_norm_mix_pre, v_norm_mix_post, v_norm_ffn_pre, v_norm_ffn_post, v_lb_table, v_a_norm, v_b_ln_g,
               v_b_ln_b, v_b_ws[0], v_b_bias[0].T]
    small_out = _small_update(gathered, where, small_w, small_m, small_v, 4, "small_update")
    loss = small_out[0].reshape(())
    small = [small_out[1 + 4 * k:5 + 4 * k] for k in range(len(small_w))]
    small[8] = [a[None] for a in small[8]]
    small[9] = [a.T[None] for a in small[9]]

    per_weight = small[0:4] + [big_out[0]] + small[4:10] + big_out[1:6]
    grad_x = dx0.reshape(x.shape)
    out = [loss, grad_x]
    for kind in range(4):
        out += [p[kind] for p in per_weight]
    return tuple(out)


def _local_step(x0, target, rope, norm_mix_pre, norm_mix_post, norm_ffn_pre, norm_ffn_post, lb_table, a_norm,
                b_ln_g, b_ln_b, ws, bias_t, get_w, put_g, token):
    def gain(a, l, tok):
        return a[l:l + 1] if tok is None else a[l:l + 1] + tok[0:1, 0:1]

    full = lambda a: a.reshape(-1, D_MODEL)
    owners = lambda a: a.reshape((N_DEV, -1) + a.shape[1:])

    (g_in_e,) = get_w(["in_e"], token)
    proj, h_mix0 = _norm_inproj(x0, gain(norm_mix_pre, 0, token), g_in_e, "inproj_even")
    mixed, pre_a, states = _hgrn2_fwd(proj, lb_table, a_norm, "hgrn2_fwd")
    mixed = _gmlp_fwd(proj, mixed, b_ln_g, b_ln_b, ws, bias_t, "gmlp_fwd")
    w_out_e = full(get_w(["out_e"], mixed)[0])
    x1, mix0 = _outproj([mixed], w_out_e, x0, gain(norm_mix_post, 0, None), "outproj_even")
    w1_0, w2_0 = get_w(["ff1_0", "ff2_0"], x1)
    w2_0 = full(w2_0)
    x2, y0, h_ffn0, r0 = _ffn_fwd(x1, gain(norm_ffn_pre, 0, None), w1_0, w2_0, gain(norm_ffn_post, 0, None), "ffn_fwd_0")
    (g_in_o,) = get_w(["in_o"], x2)
    *qkv, h_mix1 = _norm_inproj_rope(x2, gain(norm_mix_pre, 1, None), g_in_o, rope, "inproj_odd")
    branches = [_attn_branch_fwd(a, "attn_fwd_d%d" % d) for a, d in zip(qkv, C_DILATIONS)]
    attn, attn_b, lse = _attn_merge(branches, "attn_merge")
    w_out_o = full(get_w(["out_o"], attn_b)[0])
    x3, mix1 = _outproj([attn_b], w_out_o, x2, gain(norm_mix_post, 1, None), "outproj_odd")
    w1_1, w2_1 = get_w(["ff1_1", "ff2_1"], x3)
    w2_1 = full(w2_1)
    dx4, y1, h_ffn1, r1, loss_part = _ffn_fwd(x3, gain(norm_ffn_pre, 1, None), w1_1, w2_1, gain(norm_ffn_post, 1, None),
                                              "ffn_fwd_1", target)

    dx3, dy1, da1, dg_ffn_pre1, dg_ffn_post1 = _ffn_bwd(
        dx4, x3, y1, r1, gain(norm_ffn_pre, 1, None), w1_1, w2_1, gain(norm_ffn_post, 1, None), "ffn_bwd_1")
    gw_ff1_1 = _grad_w(h_ffn1, da1, True, "grad_w_ff1_1")
    gw_ff2_1 = _grad_w(r1, dy1, False, "grad_w_ff2_1")
    tok = put_g("ffn1", dict(ff1_1=gw_ff1_1, ff2_1=owners(gw_ff2_1)))
    *dattn, delta, dz1, dg_mix_post1 = _outproj_bwd_attn(dx3, mix1, gain(norm_mix_post, 1, tok), w_out_o, attn,
                                                  "outproj_bwd_odd")
    gw_out_o = _grad_w(attn_b, dz1, False, "grad_w_out_odd")
    per_seq = lambda a: a.reshape(-1, SEQ, LANES)
    grads_c = [_attn_branch_bwd(qkv[b], dattn[b], per_seq(lse), per_seq(delta), "attn_bwd_d%d" % d)
               for b, d in enumerate(C_DILATIONS)]
    dqkv = _attn_combine_bwd(grads_c, rope, "attn_combine_bwd")
    gw_in_o = _grad_w(h_mix1, dqkv, True, "grad_w_in_odd")
    tok = put_g("mix1", dict(out_o=owners(gw_out_o), in_o=gw_in_o))
    dx2, dg_mix_pre1 = _inproj_bwd(dqkv, g_in_o, dx3, x2, gain(norm_mix_pre, 1, tok), "inproj_bwd_odd")

    dx1, dy0, da0, dg_ffn_pre0, dg_ffn_post0 = _ffn_bwd(
        dx2, x1, y0, r0, gain(norm_ffn_pre, 0, None), w1_0, w2_0, gain(norm_ffn_post, 0, None), "ffn_bwd_0")
    gw_ff1_0 = _grad_w(h_ffn0, da0, True, "grad_w_ff1_0")
    gw_ff2_0 = _grad_w(r0, dy0, False, "grad_w_ff2_0")
    tok = put_g("ffn0", dict(ff1_0=gw_ff1_0, ff2_0=owners(gw_ff2_0)))
    dcat, dz0, dg_mix_post0 = _outproj_bwd(dx1, mix0, gain(norm_mix_post, 0, tok), w_out_e, "outproj_bwd_even")
    gw_out_e = _grad_w(mixed, dz0, False, "grad_w_out_even")
    dproj, d_lb, d_a_norm = _hgrn2_bwd(proj, dcat, pre_a, states, lb_table, a_norm, "hgrn2_bwd")
    dproj, d_ln_g, d_ln_b, d_ws, d_bias_t = _gmlp_bwd(proj, dcat, dproj, b_ln_g, b_ln_b, ws, bias_t, "gmlp_bwd")
    gw_in_e = _grad_w(h_mix0, dproj, True, "grad_w_in_even")
    tok = put_g("mix0", dict(out_e=owners(gw_out_e), in_e=gw_in_e))
    dx0, dg_mix_pre0 = _inproj_bwd(dproj, g_in_e, dx1, x0, gain(norm_mix_pre, 0, tok), "inproj_bwd_even")

    layers = lambda a, b: jnp.concatenate([a, b], axis=0)
    return (dx0, loss_part, layers(dg_mix_pre0, dg_mix_pre1), layers(dg_mix_post0, dg_mix_post1),
            layers(dg_ffn_pre0, dg_ffn_pre1), layers(dg_ffn_post0, dg_ffn_post1),
            d_lb, d_a_norm, d_ln_g, d_ln_b, d_ws, d_bias_t)
```

```python
import math

import jax
import jax.numpy as jnp
from jax import lax
from jax.experimental import pallas as pl
from jax.experimental.pallas import tpu as pltpu

F32 = jnp.float32
BF16 = jnp.bfloat16
MESH = pl.DeviceIdType.MESH

N_DEV = 8
D_MODEL = 1024
SEQ = 2048
EPS = 1e-6
A_WIDTH = 512
A_HEADS = 4
HEAD_A = 128
B_WIDTH = 512
B_GROUPS = 4
B_CHUNK = 128
C_HEADS = 16
C_HEAD_DIM = 64
C_ROT_HALF = 8
ROPE_THETA = 500000.0
C_DILATIONS = (1, 4, 16)
C_BLOCK = 128
D_FF = 4096
EVEN_IN = 3072
ODD_IN = 3072

ADAM_LR = 0.001
ADAM_B1 = 0.9
ADAM_B2 = 0.999
ADAM_EPS = 1e-08
ADAM_WD = 0.01
ADAM_STEP = 10

LANES = 128
SUBLANES = 8
ROW_TILE = 512
PROJ_TILE = 1024
PROJ_COLS = 768
MERGE_TILE = 256
SUB_CHUNK = 16
HGRN_BLOCK = 256
NEG = -1e30
VMEM_LIMIT = 56 * 1024 * 1024


def _params(sem):
    return pltpu.CompilerParams(dimension_semantics=sem, vmem_limit_bytes=VMEM_LIMIT)


def _dot(a, b):
    return jnp.dot(a, b, preferred_element_type=F32)


def _dot_nt(a, b):
    return lax.dot_general(a, b, (((1,), (1,)), ((), ())), preferred_element_type=F32)


def _dot_tn(a, b):
    return lax.dot_general(a, b, (((0,), (0,)), ((), ())), preferred_element_type=F32)


def _rms(x, g):
    r = lax.rsqrt(jnp.mean(x * x, axis=-1, keepdims=True) + EPS)
    return x * r * g


def _rms_bwd(x, g, dy):
    r = lax.rsqrt(jnp.mean(x * x, axis=-1, keepdims=True) + EPS)
    dyg = dy * g
    dx = r * dyg - x * (r * r * r) * jnp.mean(x * dyg, axis=-1, keepdims=True)
    return dx, dy * x * r


def _split3(x):
    hi = x.astype(BF16)
    rest = x - hi.astype(F32)
    mid = rest.astype(BF16)
    return hi, mid, (rest - mid.astype(F32)).astype(BF16)


def _mask_dot(mask, x):
    m = mask.astype(BF16)
    hi, mid, lo = _split3(x)
    return _dot(m, hi) + (_dot(m, mid) + _dot(m, lo))


def _dot_mask(x, mask):
    m = mask.astype(BF16)
    hi, mid, lo = _split3(x)
    return _dot(hi, m) + (_dot(mid, m) + _dot(lo, m))


def _rows8(v):
    return v.reshape(v.shape[0] // SUBLANES, SUBLANES, v.shape[1]).sum(axis=0)


def _sigmoid(x):
    return 1.0 / (1.0 + jnp.exp(-x))


def _gelu(x):
    return 0.5 * x * (1.0 + jnp.tanh(math.sqrt(2.0 / math.pi) * (x + 0.044715 * (x * x * x))))


def _acc_rows8(ref, val, first):
    @pl.when(first)
    def _():
        ref[...] = val

    @pl.when(jnp.logical_not(first))
    def _():
        ref[...] += val


def _my_slot():
    return 4 * lax.axis_index("x") + 2 * lax.axis_index("y") + lax.axis_index("c")


def _peer(r):
    x, y, c = lax.axis_index("x"), lax.axis_index("y"), lax.axis_index("c")
    px = 1 - x if (r >> 2) & 1 else x
    py = 1 - y if (r >> 1) & 1 else y
    pc = 1 - c if r & 1 else c
    return (px, py, pc), 4 * px + 2 * py + pc


HBM_SPEC = pl.BlockSpec(memory_space=pltpu.HBM)
SEM_SPEC = pl.BlockSpec(memory_space=pltpu.SEMAPHORE)
SPLIT_EFFECT = pltpu.SideEffectType.DATAFLOW_SIDE_EFFECTING


def _split_copies(land_ref, src_ref, send_sem, recv_sem):
    me = _my_slot()
    copies = []
    for r in range(1, N_DEV):
        peer, slot = _peer(r)
        src = _slot(land_ref, me) if src_ref is None else _slot(src_ref, slot)
        copies.append(pltpu.make_async_remote_copy(
            src_ref=src, dst_ref=_slot(land_ref, me), send_sem=send_sem, recv_sem=recv_sem,
            device_id=peer, device_id_type=MESH))
    return copies


def _slot(ref, s):
    if len(ref.shape) == 2:
        c = ref.shape[1] // N_DEV
        return ref.at[:, pl.ds(pl.multiple_of(s * c, LANES), c)]
    return ref.at[s]


def _exchange_start(lands, sources, name):
    n = len(lands)
    given = [s for s in sources if s is not None]
    arrays = list(lands) + given

    def body(*refs):
        land_refs, src_refs = refs[:n], list(refs[n:n + len(given)])
        sems = refs[len(arrays):len(arrays) + 2 * n]
        token = refs[-1]
        for k in range(n):
            src_ref = None if sources[k] is None else src_refs.pop(0)
            for copy in _split_copies(land_refs[k], src_ref, sems[k], sems[n + k]):
                copy.start()
        token[...] = jnp.zeros_like(token)

    outs = pl.pallas_call(
        body, name=name,
        out_shape=(pltpu.SemaphoreType.DMA(()),) * (2 * n) + tuple(pltpu.HBM(a.shape, a.dtype) for a in arrays)
        + (jax.ShapeDtypeStruct((SUBLANES, LANES), F32),),
        in_specs=[HBM_SPEC] * len(arrays),
        out_specs=(SEM_SPEC,) * (2 * n) + (HBM_SPEC,) * len(arrays) + (pl.BlockSpec(memory_space=pltpu.VMEM),),
        input_output_aliases={i: 2 * n + i for i in range(len(arrays))},
        compiler_params=pltpu.CompilerParams(has_side_effects=SPLIT_EFFECT),
    )(*[pltpu.with_memory_space_constraint(a, pltpu.HBM) for a in arrays])
    return list(outs[:n]), list(outs[n:2 * n]), list(outs[2 * n:3 * n]), list(outs[3 * n:-1]), outs[-1]


def _exchange_wait(lands, sources, send_sems, recv_sems, after, name):
    n = len(lands)
    given = [s for s in sources if s is not None]
    arrays = list(lands) + given

    def body(*refs):
        land_refs, src_refs = refs[:n], list(refs[n:n + len(given)])
        sems = refs[len(arrays):len(arrays) + 2 * n]
        for i in range(n):
            src_ref = None if sources[i] is None else src_refs.pop(0)
            copies = _split_copies(land_refs[i], src_ref, sems[i], sems[n + i])
            for copy in copies:
                copy.wait_recv()
            for copy in copies:
                copy.wait_send()

    outs = pl.pallas_call(
        body, name=name, out_shape=tuple(pltpu.HBM(a.shape, a.dtype) for a in arrays),
        in_specs=[HBM_SPEC] * len(arrays) + [SEM_SPEC] * (2 * n) + [pl.BlockSpec(memory_space=pl.ANY)],
        out_specs=(HBM_SPEC,) * len(arrays),
        input_output_aliases={i: i for i in range(len(arrays))},
        compiler_params=pltpu.CompilerParams(has_side_effects=SPLIT_EFFECT),
    )(*arrays, *send_sems, *recv_sems, after)
    return list(outs[:n])


def _place_own(a, me, name, own_block, dtype=BF16, after=None):
    shape = a.shape[1:] if own_block else a.shape
    cols = shape[-1]
    a3 = a.reshape((N_DEV if own_block else 1, -1, cols))
    rows = a3.shape[1]
    tr = min(rows, 512)

    def body(me_ref, a_ref, _, o_ref):
        o_ref[...] = a_ref[...].astype(dtype)

    grid_spec = pltpu.PrefetchScalarGridSpec(
        num_scalar_prefetch=1, grid=(rows // tr,),
        in_specs=[pl.BlockSpec((1, tr, cols), lambda i, me_ref: (me_ref[0] if own_block else 0, i, 0)),
                  pl.BlockSpec(memory_space=pl.ANY)],
        out_specs=pl.BlockSpec((1, tr, cols), lambda i, me_ref: (me_ref[0], i, 0)))
    out = pl.pallas_call(
        body, name=name, grid_spec=grid_spec, out_shape=jax.ShapeDtypeStruct((N_DEV, rows, cols), dtype),
        compiler_params=_params(("arbitrary",)),
    )(me, a3, a3 if after is None else after)
    return out.reshape((N_DEV,) + shape)


def _place_own_columns(a, me, name, after=None):
    rows, cols = a.shape
    tr = min(rows, 512)

    def body(me_ref, a_ref, _, o_ref):
        o_ref[...] = a_ref[...].astype(BF16)

    grid_spec = pltpu.PrefetchScalarGridSpec(
        num_scalar_prefetch=1, grid=(rows // tr,),
        in_specs=[pl.BlockSpec((tr, cols), lambda i, me_ref: (i, 0)), pl.BlockSpec(memory_space=pl.ANY)],
        out_specs=pl.BlockSpec((tr, cols), lambda i, me_ref: (i, me_ref[0])))
    return pl.pallas_call(
        body, name=name, grid_spec=grid_spec, out_shape=jax.ShapeDtypeStruct((rows, N_DEV * cols), BF16),
        compiler_params=_params(("arbitrary",)),
    )(me, a, a if after is None else after)


def _rope_tables(positions):
    in_head = jnp.arange(LANES) % C_HEAD_DIM
    inv = ROPE_THETA ** (-(in_head % C_ROT_HALF).astype(F32) / C_ROT_HALF)
    ang = positions.reshape(-1)[:, None].astype(F32) * inv
    rotated = in_head < 2 * C_ROT_HALF
    sin = jnp.sin(ang)
    return (jnp.where(rotated, jnp.cos(ang), 1.0),
            jnp.where(in_head < C_ROT_HALF, -sin, jnp.where(rotated, sin, 0.0)))


def _swap_halves(x):
    lane = lax.broadcasted_iota(jnp.int32, x.shape, 1) % C_HEAD_DIM
    return jnp.where(lane < C_ROT_HALF, pltpu.roll(x, LANES - C_ROT_HALF, 1), pltpu.roll(x, C_ROT_HALF, 1))


def _norm_inproj(x, g, w, name):
    t = x.shape[0]
    n = w.shape[1]
    tm, tn = PROJ_TILE, PROJ_COLS

    def body(x_ref, g_ref, w_ref, o_ref, h_ref):
        @pl.when(pl.program_id(1) == 0)
        def _():
            h_ref[...] = _rms(x_ref[...], g_ref[...]).astype(BF16)

        o_ref[...] = _dot(h_ref[...], w_ref[...])

    return pl.pallas_call(
        body, name=name, grid=(t // tm, n // tn),
        in_specs=[pl.BlockSpec((tm, D_MODEL), lambda i, j: (i, 0)), pl.BlockSpec((1, D_MODEL), lambda i, j: (0, 0)),
                  pl.BlockSpec((D_MODEL, tn), lambda i, j: (0, j))],
        out_specs=[pl.BlockSpec((tm, tn), lambda i, j: (i, j)), pl.BlockSpec((tm, D_MODEL), lambda i, j: (i, 0))],
        out_shape=[jax.ShapeDtypeStruct((t, n), F32), jax.ShapeDtypeStruct((t, D_MODEL), BF16)],
        compiler_params=_params(("parallel", "arbitrary")),
    )(x, g, w)


def _dilated_specs(tm, width, col_of):
    per_seq = SEQ // tm
    specs = []
    for d in C_DILATIONS:
        specs.append(pl.BlockSpec(
            (1, d, tm // d, width), lambda i, *rest: (i // per_seq, 0, i % per_seq, col_of(*rest))))
    return specs


def _dilated_shapes(n_seq, cols, dtype):
    return [jax.ShapeDtypeStruct((n_seq, d, SEQ // d, cols), dtype) for d in C_DILATIONS]


def _store_dilated(src_ref, out_refs, dtype):
    groups, tm, _ = src_ref.shape
    for d, o_ref in zip(C_DILATIONS, out_refs):
        for r in range(d):
            rows = pl.ds(r, tm // d, stride=d) if d > 1 else slice(None)
            for p in range(groups):
                o_ref[0, r, :, p * LANES:(p + 1) * LANES] = src_ref.at[p][rows, :].astype(dtype)


def _load_dilated(in_ref, d, dst_ref):
    groups, tm, _ = dst_ref.shape
    for r in range(d):
        rows = pl.ds(r, tm // d, stride=d)
        for p in range(groups):
            dst_ref.at[p][rows, :] = in_ref[0, r, :, p * LANES:(p + 1) * LANES].astype(F32)


def _norm_inproj_rope(x, g, w, rope, name):
    t = x.shape[0]
    n = w.shape[1]
    tm, nb = PROJ_TILE, PROJ_COLS

    def body(x_ref, g_ref, w_ref, c_ref, s_ref, o1_ref, o4_ref, o16_ref, h_ref, tile_ref):
        j = pl.program_id(1)

        @pl.when(j == 0)
        def _():
            h_ref[...] = _rms(x_ref[...], g_ref[...]).astype(BF16)

        acc = _dot(h_ref[...], w_ref[...])
        for p in range(nb // LANES):
            blk = acc[:, p * LANES:(p + 1) * LANES]
            roped = blk * c_ref[...] + _swap_halves(blk) * s_ref[...]
            piece = j * (nb // LANES) + p
            is_qk = piece < 2 * (D_MODEL // LANES)
            tile_ref[p] = jnp.where(is_qk, roped, blk) * jnp.where(piece < D_MODEL // LANES, QK_SCALE, 1.0)
        _store_dilated(tile_ref, (o1_ref, o4_ref, o16_ref), BF16)

    return pl.pallas_call(
        body, name=name, grid=(t // tm, n // nb),
        in_specs=[pl.BlockSpec((tm, D_MODEL), lambda i, j: (i, 0)), pl.BlockSpec((1, D_MODEL), lambda i, j: (0, 0)),
                  pl.BlockSpec((D_MODEL, nb), lambda i, j: (0, j)),
                  pl.BlockSpec((tm, LANES), lambda i, j: (i, 0)), pl.BlockSpec((tm, LANES), lambda i, j: (i, 0))],
        out_specs=_dilated_specs(tm, nb, lambda j: j) + [pl.BlockSpec((tm, D_MODEL), lambda i, j: (i, 0))],
        out_shape=_dilated_shapes(t // SEQ, n, BF16) + [jax.ShapeDtypeStruct((t, D_MODEL), BF16)],
        scratch_shapes=[pltpu.VMEM((nb // LANES, tm, LANES), F32)],
        compiler_params=_params(("parallel", "arbitrary")),
    )(x, g, w, *rope)


def _outproj(parts, w, x, g, name):
    t = x.shape[0]
    tm = PROJ_TILE
    n = len(parts)
    widths = [p.shape[1] for p in parts]

    def body(*refs):
        p_refs = refs[:n]
        w_ref, x_ref, g_ref, xo_ref, mix_ref = refs[n:]
        mix = None
        off = 0
        for p_ref, wd in zip(p_refs, widths):
            term = _dot(p_ref[...].astype(BF16), w_ref[off:off + wd, :])
            mix = term if mix is None else mix + term
            off += wd
        mix_ref[...] = mix
        xo_ref[...] = x_ref[...] + _rms(mix, g_ref[...])

    row = lambda i: (i, 0)
    return pl.pallas_call(
        body, name=name, grid=(t // tm,),
        in_specs=[pl.BlockSpec((tm, wd), row) for wd in widths] + [
            pl.BlockSpec((sum(widths), D_MODEL), lambda i: (0, 0)),
            pl.BlockSpec((tm, D_MODEL), row), pl.BlockSpec((1, D_MODEL), lambda i: (0, 0))],
        out_specs=[pl.BlockSpec((tm, D_MODEL), row)] * 2,
        out_shape=[jax.ShapeDtypeStruct((t, D_MODEL), F32)] * 2,
        compiler_params=_params(("parallel",)),
    )(*parts, w, x, g)


def _outproj_bwd(dx, mix, g, w, name):
    t = dx.shape[0]
    tm = PROJ_TILE
    k = w.shape[0]

    def body(dx_ref, mix_ref, g_ref, w_ref, dcat_ref, dz_ref, dg_ref):
        dz, dgr = _rms_bwd(mix_ref[...], g_ref[...], dx_ref[...])
        dzb = dz.astype(BF16)
        dz_ref[...] = dzb
        dcat_ref[...] = _dot_nt(dzb, w_ref[...])
        _acc_rows8(dg_ref, _rows8(dgr), pl.program_id(0) == 0)

    row = lambda i: (i, 0)
    return pl.pallas_call(
        body, name=name, grid=(t // tm,),
        in_specs=[pl.BlockSpec((tm, D_MODEL), row), pl.BlockSpec((tm, D_MODEL), row),
                  pl.BlockSpec((1, D_MODEL), lambda i: (0, 0)), pl.BlockSpec((k, D_MODEL), lambda i: (0, 0))],
        out_specs=[pl.BlockSpec((tm, k), row), pl.BlockSpec((tm, D_MODEL), row),
                   pl.BlockSpec((SUBLANES, D_MODEL), lambda i: (0, 0))],
        out_shape=[jax.ShapeDtypeStruct((t, k), F32), jax.ShapeDtypeStruct((t, D_MODEL), BF16),
                   jax.ShapeDtypeStruct((SUBLANES, D_MODEL), F32)],
        compiler_params=_params(("arbitrary",)),
    )(dx, mix, g, w)


def _outproj_bwd_attn(dx, mix, g, w, out, name):
    t = dx.shape[0]
    tm = MERGE_TILE

    def body(dx_ref, mix_ref, g_ref, w_ref, out_ref, do1, do4, do16, dl_ref, dz_ref, dg_ref, tile_ref):
        dz, dgr = _rms_bwd(mix_ref[...], g_ref[...], dx_ref[...])
        dzb = dz.astype(BF16)
        dz_ref[...] = dzb
        _acc_rows8(dg_ref, _rows8(dgr), pl.program_id(0) == 0)
        dout = _dot_nt(dzb, w_ref[...])
        for p in range(LANE_GROUPS):
            tile_ref[p] = dout[:, p * LANES:(p + 1) * LANES]
        _store_dilated(tile_ref, (do1, do4, do16), BF16)
        column = lax.broadcasted_iota(jnp.int32, (D_MODEL, LANES), 0) // C_HEAD_DIM
        head = lax.broadcasted_iota(jnp.int32, (D_MODEL, LANES), 1)
        dl_ref[...] = _dot_mask(dout * out_ref[...], column == head)

    row = lambda i: (i, 0)
    n_seq = t // SEQ
    return pl.pallas_call(
        body, name=name, grid=(t // tm,),
        in_specs=[pl.BlockSpec((tm, D_MODEL), row), pl.BlockSpec((tm, D_MODEL), row),
                  pl.BlockSpec((1, D_MODEL), lambda i: (0, 0)), pl.BlockSpec((D_MODEL, D_MODEL), lambda i: (0, 0)),
                  pl.BlockSpec((tm, D_MODEL), row)],
        out_specs=_dilated_specs(tm, D_MODEL, lambda: 0) + [
            pl.BlockSpec((tm, LANES), row), pl.BlockSpec((tm, D_MODEL), row),
            pl.BlockSpec((SUBLANES, D_MODEL), lambda i: (0, 0))],
        out_shape=_dilated_shapes(n_seq, D_MODEL, BF16) + [
            jax.ShapeDtypeStruct((t, LANES), F32), jax.ShapeDtypeStruct((t, D_MODEL), BF16),
            jax.ShapeDtypeStruct((SUBLANES, D_MODEL), F32)],
        scratch_shapes=[pltpu.VMEM((LANE_GROUPS, tm, LANES), F32)],
        compiler_params=_params(("arbitrary",)),
    )(dx, mix, g, w, out)


def _inproj_bwd(dproj, w, dx, x, g, name):
    t = x.shape[0]
    n = w.shape[1]
    tm = ROW_TILE

    def body(dp_ref, w_ref, dx_ref, x_ref, g_ref, o_ref, dg_ref):
        dxn, dgr = _rms_bwd(x_ref[...], g_ref[...], _dot_nt(dp_ref[...], w_ref[...]))
        o_ref[...] = dx_ref[...] + dxn
        _acc_rows8(dg_ref, _rows8(dgr), pl.program_id(0) == 0)

    row = lambda i: (i, 0)
    return pl.pallas_call(
        body, name=name, grid=(t // tm,),
        in_specs=[pl.BlockSpec((tm, n), row), pl.BlockSpec((D_MODEL, n), lambda i: (0, 0)),
                  pl.BlockSpec((tm, D_MODEL), row), pl.BlockSpec((tm, D_MODEL), row),
                  pl.BlockSpec((1, D_MODEL), lambda i: (0, 0))],
        out_specs=[pl.BlockSpec((tm, D_MODEL), row), pl.BlockSpec((SUBLANES, D_MODEL), lambda i: (0, 0))],
        out_shape=[jax.ShapeDtypeStruct((t, D_MODEL), F32), jax.ShapeDtypeStruct((SUBLANES, D_MODEL), F32)],
        compiler_params=_params(("arbitrary",)),
    )(dproj, w, dx, x, g)


def _grad_w(a, b, col_blocks, name):
    t, k = a.shape
    n = b.shape[1]
    tk = min(k, 1024)
    per_owner = n // N_DEV
    tn = 2 * per_owner if col_blocks else min(n, 1024)

    def body(a_ref, b_ref, o_ref, at_ref):
        @pl.when(pl.program_id(1) == 0)
        def _():
            for c in range(t // ROW_TILE):
                rows = slice(c * ROW_TILE, (c + 1) * ROW_TILE)
                at_ref[:, rows] = a_ref[rows, :].T

        res = _dot(at_ref[...], b_ref[...]).astype(BF16)
        if col_blocks:
            o_ref[0] = res[:, :per_owner]
            o_ref[1] = res[:, per_owner:]
        else:
            o_ref[...] = res

    if col_blocks:
        out_spec = pl.BlockSpec((2, tk, per_owner), lambda i, j: (j, i, 0))
        out_shape = jax.ShapeDtypeStruct((N_DEV, k, per_owner), BF16)
    else:
        out_spec = pl.BlockSpec((tk, tn), lambda i, j: (i, j))
        out_shape = jax.ShapeDtypeStruct((k, n), BF16)
    return pl.pallas_call(
        body, name=name, grid=(k // tk, n // tn),
        in_specs=[pl.BlockSpec((t, tk), lambda i, j: (0, i)), pl.BlockSpec((t, tn), lambda i, j: (0, j))],
        out_specs=out_spec, out_shape=out_shape,
        scratch_shapes=[pltpu.VMEM((tk, t), BF16)],
        compiler_params=_params(("parallel", "arbitrary")),
    )(a, b)


FF_STEP = 1024
FF_STEPS = D_FF // FF_STEP
FF_BWD_STEP = 512


def _ffn_fwd(x, g_pre, w1, w2, g_post, name, target=None):
    t = x.shape[0]
    tm = PROJ_TILE

    def body(*refs):
        if target is None:
            x_ref, gp_ref, w1_ref, w2_ref, gq_ref, xo_ref, y_ref, h_ref, r_ref = refs
        else:
            x_ref, gp_ref, w1_ref, w2_ref, gq_ref, t_ref, xo_ref, y_ref, h_ref, r_ref, l_ref = refs
        i, j = pl.program_id(0), pl.program_id(1)

        @pl.when(j == 0)
        def _():
            h_ref[...] = _rms(x_ref[...], gp_ref[...]).astype(BF16)

        a = _dot(h_ref[...], w1_ref[...])
        r = jnp.square(jnp.maximum(a, 0.0)).astype(BF16)
        r_ref[...] = r
        term = _dot(r, w2_ref[...])

        @pl.when(j == 0)
        def _():
            y_ref[...] = term

        @pl.when(j > 0)
        def _():
            y_ref[...] += term

        @pl.when(j == FF_STEPS - 1)
        def _():
            x_new = x_ref[...] + _rms(y_ref[...], gq_ref[...])
            if target is None:
                xo_ref[...] = x_new
            else:
                diff = x_new - t_ref[...]
                xo_ref[...] = diff * (1.0 / D_MODEL)
                _acc_rows8(l_ref, _rows8(diff * diff) * (0.5 / D_MODEL), i == 0)

    row = lambda i, j: (i, 0)
    vec = pl.BlockSpec((1, D_MODEL), lambda i, j: (0, 0))
    in_specs = [pl.BlockSpec((tm, D_MODEL), row), vec, pl.BlockSpec((D_MODEL, FF_STEP), lambda i, j: (0, j)),
                pl.BlockSpec((FF_STEP, D_MODEL), lambda i, j: (j, 0)), vec]
    out_specs = [pl.BlockSpec((tm, D_MODEL), row)] * 3 + [pl.BlockSpec((tm, FF_STEP), lambda i, j: (i, j))]
    out_shape = [jax.ShapeDtypeStruct((t, D_MODEL), F32), jax.ShapeDtypeStruct((t, D_MODEL), F32),
                 jax.ShapeDtypeStruct((t, D_MODEL), BF16), jax.ShapeDtypeStruct((t, D_FF), BF16)]
    args = [x, g_pre, w1, w2, g_post]
    if target is not None:
        in_specs.append(pl.BlockSpec((tm, D_MODEL), row))
        out_specs.append(pl.BlockSpec((SUBLANES, D_MODEL), lambda i, j: (0, 0)))
        out_shape.append(jax.ShapeDtypeStruct((SUBLANES, D_MODEL), F32))
        args.append(target)
    return pl.pallas_call(
        body, name=name, grid=(t // tm, FF_STEPS), in_specs=in_specs, out_specs=out_specs, out_shape=out_shape,
        compiler_params=_params(("parallel" if target is None else "arbitrary", "arbitrary")),
    )(*args)


def _ffn_bwd(dxo, x, y, r, g_pre, w1, w2, g_post, name):
    t = x.shape[0]
    tm, step = PROJ_TILE, FF_BWD_STEP
    steps = D_FF // step

    def body(dxo_ref, x_ref, y_ref, r_ref, gp_ref, w1_ref, w2_ref, gq_ref,
             dx_ref, dy_ref, da_ref, dgp_ref, dgq_ref, acc_ref):
        i, j = pl.program_id(0), pl.program_id(1)

        @pl.when(j == 0)
        def _():
            dy, dgr = _rms_bwd(y_ref[...], gq_ref[...], dxo_ref[...])
            dy_ref[...] = dy.astype(BF16)
            _acc_rows8(dgq_ref, _rows8(dgr), i == 0)

        dr = _dot_nt(dy_ref[...], w2_ref[...])
        da = (dr * (2.0 * jnp.sqrt(r_ref[...].astype(F32)))).astype(BF16)
        da_ref[...] = da
        term = _dot_nt(da, w1_ref[...])

        @pl.when(j == 0)
        def _():
            acc_ref[...] = term

        @pl.when(j > 0)
        def _():
            acc_ref[...] += term

        @pl.when(j == steps - 1)
        def _():
            dxn, dgr = _rms_bwd(x_ref[...], gp_ref[...], acc_ref[...])
            dx_ref[...] = dxo_ref[...] + dxn
            _acc_rows8(dgp_ref, _rows8(dgr), i == 0)

    row = lambda i, j: (i, 0)
    vec = pl.BlockSpec((1, D_MODEL), lambda i, j: (0, 0))
    acc8 = pl.BlockSpec((SUBLANES, D_MODEL), lambda i, j: (0, 0))
    return pl.pallas_call(
        body, name=name, grid=(t // tm, steps),
        in_specs=[pl.BlockSpec((tm, D_MODEL), row)] * 3 + [
            pl.BlockSpec((tm, step), lambda i, j: (i, j)),
            vec, pl.BlockSpec((D_MODEL, step), lambda i, j: (0, j)),
            pl.BlockSpec((step, D_MODEL), lambda i, j: (j, 0)), vec],
        out_specs=[pl.BlockSpec((tm, D_MODEL), row), pl.BlockSpec((tm, D_MODEL), row),
                   pl.BlockSpec((tm, step), lambda i, j: (i, j)), acc8, acc8],
        out_shape=[jax.ShapeDtypeStruct((t, D_MODEL), F32), jax.ShapeDtypeStruct((t, D_MODEL), BF16),
                   jax.ShapeDtypeStruct((t, D_FF), BF16),
                   jax.ShapeDtypeStruct((SUBLANES, D_MODEL), F32), jax.ShapeDtypeStruct((SUBLANES, D_MODEL), F32)],
        scratch_shapes=[pltpu.VMEM((tm, D_MODEL), F32)],
        compiler_params=_params(("arbitrary", "arbitrary")),
    )(dxo, x, y, r, g_pre, w1, w2, g_post)


def _lower_bound(table):
    e = jnp.exp(table - jnp.max(table, axis=0, keepdims=True))
    return e[0:1, :] / jnp.sum(e, axis=0, keepdims=True)


def _hgrn2_block(q_ref, f_ref, lb):
    tb = f_ref.shape[0]
    sig = _sigmoid(f_ref[...])
    f = lb + (1.0 - lb) * sig
    qraw = q_ref[...]
    sq = _sigmoid(qraw)
    r = lax.broadcasted_iota(jnp.int32, (tb, tb), 0)
    c = lax.broadcasted_iota(jnp.int32, (tb, tb), 1)
    same = (r // SUB_CHUNK) == (c // SUB_CHUNK)
    logf = jnp.log(f)
    gsum = _mask_dot(same & (c <= r), logf)
    glast = _mask_dot(same, logf)
    return dict(sig=sig, f=f, kk=1.0 - f, qraw=qraw, sq=sq, qs=qraw * sq, gsum=gsum,
                eg=jnp.exp(gsum), ekd=jnp.exp(glast - gsum), a=jnp.exp(glast))


def _head_sums(x):
    parts = [jnp.broadcast_to(jnp.sum(x[:, h * HEAD_A:(h + 1) * HEAD_A], axis=1, keepdims=True), (x.shape[0], HEAD_A))
             for h in range(A_HEADS)]
    return jnp.concatenate(parts, axis=1)


def _hgrn2_intra(g, kk, qs, v):
    row = lax.broadcasted_iota(jnp.int32, g.shape, 0)
    o = _head_sums(qs * kk) * v
    for j in range(1, SUB_CHUNK):
        decay = jnp.exp(jnp.where(row >= j, g - pltpu.roll(g, j, 0), NEG))
        o = o + _head_sums(qs * pltpu.roll(kk, j, 0) * decay) * pltpu.roll(v, j, 0)
    return o


def _hgrn2_intra_bwd(g, kk, qs, v, do):
    row = lax.broadcasted_iota(jnp.int32, g.shape, 0)
    dsc = _head_sums(do * v)
    dqs, dkk, dv = dsc * kk, dsc * qs, _head_sums(qs * kk) * do
    for j in range(1, SUB_CHUNK):
        k_dn = pltpu.roll(kk, j, 0)
        decay = jnp.exp(jnp.where(row >= j, g - pltpu.roll(g, j, 0), NEG))
        d_score = _head_sums(do * pltpu.roll(v, j, 0)) * decay
        dqs = dqs + d_score * k_dn
        dkk = dkk + pltpu.roll(d_score * qs, SUB_CHUNK - j, 0)
        dv = dv + pltpu.roll(_head_sums(qs * k_dn * decay) * do, SUB_CHUNK - j, 0)
    return dqs, dkk, dv


def _hgrn2_fwd(proj, lb_table, a_norm, name):
    t = proj.shape[0]
    tb = HGRN_BLOCK
    n_tb = SEQ // tb
    n_seq = t // SEQ
    n_sub = tb // SUB_CHUNK

    def body(q_ref, f_ref, i_ref, g_ref, lbt_ref, an_ref, o_ref, pre_ref, sts_ref, st_ref,
             gs_ref, kk_ref, qs_ref, eg_ref, ekd_ref, a_ref):
        @pl.when(pl.program_id(1) == 0)
        def _():
            st_ref[...] = jnp.zeros_like(st_ref)

        an = an_ref[...]
        blk = _hgrn2_block(q_ref, f_ref, _lower_bound(lbt_ref[...]))
        for ref, key in ((gs_ref, "gsum"), (kk_ref, "kk"), (qs_ref, "qs"), (eg_ref, "eg"), (ekd_ref, "ekd"), (a_ref, "a")):
            ref[...] = blk[key]

        def step(c, carry):
            rows = pl.ds(pl.multiple_of(c * SUB_CHUNK, SUB_CHUNK), SUB_CHUNK)
            kk, qs, v = kk_ref[rows, :], qs_ref[rows, :], i_ref[rows, :]
            o = _hgrn2_intra(gs_ref[rows, :], kk, qs, v)
            qg, kd, vb = (qs * eg_ref[rows, :]).astype(BF16), (kk * ekd_ref[rows, :]).astype(BF16), v.astype(BF16)
            for h in range(A_HEADS):
                lanes = slice(h * HEAD_A, (h + 1) * HEAD_A)
                st = st_ref[h]
                sts_ref[0, c, h] = st
                o_h = o[:, lanes] + _dot_nt(qg[:, lanes], st.astype(BF16))
                st_ref[h] = st * a_ref[rows, lanes][0:1] + _dot_tn(vb[:, lanes], kd[:, lanes])
                pre_ref[rows, lanes] = o_h
                graw = g_ref[rows, lanes]
                o_ref[rows, lanes] = (_rms(o_h, an[:, lanes]) * (graw * _sigmoid(graw))).astype(BF16)
            return carry

        lax.fori_loop(0, n_sub, step, 0, unroll=2)

    def col(k):
        return pl.BlockSpec((tb, A_WIDTH), lambda b, s, k=k: (b * n_tb + s, k))

    out_rows = pl.BlockSpec((tb, A_WIDTH), lambda b, s: (b * n_tb + s, 0))
    return pl.pallas_call(
        body, name=name, grid=(n_seq, n_tb),
        in_specs=[col(0), col(1), col(2), col(3),
                  pl.BlockSpec((3, A_WIDTH), lambda b, s: (0, 0)), pl.BlockSpec((1, A_WIDTH), lambda b, s: (0, 0))],
        out_specs=[out_rows, out_rows,
                   pl.BlockSpec((1, n_sub, A_HEADS, HEAD_A, HEAD_A), lambda b, s: (b * n_tb + s, 0, 0, 0, 0))],
        out_shape=[jax.ShapeDtypeStruct((t, D_MODEL), BF16), jax.ShapeDtypeStruct((t, A_WIDTH), F32),
                   jax.ShapeDtypeStruct((n_seq * n_tb, n_sub, A_HEADS, HEAD_A, HEAD_A), F32)],
        scratch_shapes=[pltpu.VMEM((A_HEADS, HEAD_A, HEAD_A), F32)] + [pltpu.VMEM((tb, A_WIDTH), F32)] * 6,
        compiler_params=_params(("parallel", "arbitrary")),
    )(proj, proj, proj, proj, lb_table, a_norm)


def _hgrn2_bwd(proj, dcat, pre, states, lb_table, a_norm, name):
    t = proj.shape[0]
    tb = HGRN_BLOCK
    n_tb = SEQ // tb
    n_seq = t // SEQ
    n_sub = tb // SUB_CHUNK

    def body(q_ref, f_ref, i_ref, g_ref, do_ref, pre_ref, sts_ref, lbt_ref, an_ref, dp_ref, dlb_ref, dan_ref, dst_ref,
             gs_ref, kk_ref, qs_ref, eg_ref, ekd_ref, a_ref, dpre_ref, dlf_ref, dqs_ref, dkk_ref):
        b, s = pl.program_id(0), pl.program_id(1)

        @pl.when(s == 0)
        def _():
            dst_ref[...] = jnp.zeros_like(dst_ref)

        @pl.when((b == 0) & (s == 0))
        def _():
            dlb_ref[...] = jnp.zeros_like(dlb_ref)
            dan_ref[...] = jnp.zeros_like(dan_ref)

        lb = _lower_bound(lbt_ref[...])
        an = an_ref[...]
        heads = [slice(h * HEAD_A, (h + 1) * HEAD_A) for h in range(A_HEADS)]
        blk = _hgrn2_block(q_ref, f_ref, lb)
        for ref, key in ((gs_ref, "gsum"), (kk_ref, "kk"), (qs_ref, "qs"), (eg_ref, "eg"), (ekd_ref, "ekd"), (a_ref, "a")):
            ref[...] = blk[key]
        for h, lanes in enumerate(heads):
            graw, o = g_ref[:, lanes], pre_ref[:, lanes]
            sg = _sigmoid(graw)
            dout = do_ref[:, lanes]
            d_o, dgr = _rms_bwd(o, an[:, lanes], dout * (graw * sg))
            dan_ref[0:1, lanes] += jnp.sum(dgr, axis=0, keepdims=True)
            dp_ref[:, 3 * A_WIDTH + h * HEAD_A:3 * A_WIDTH + (h + 1) * HEAD_A] = (
                dout * _rms(o, an[:, lanes]) * (sg * (1.0 + graw * (1.0 - sg)))).astype(BF16)
            dpre_ref[:, lanes] = d_o

        tri_t = (lax.broadcasted_iota(jnp.int32, (SUB_CHUNK, SUB_CHUNK), 0)
                 <= lax.broadcasted_iota(jnp.int32, (SUB_CHUNK, SUB_CHUNK), 1)).astype(F32)

        def back(k, carry):
            c = n_sub - 1 - k
            rows = pl.ds(pl.multiple_of(c * SUB_CHUNK, SUB_CHUNK), SUB_CHUNK)
            g, kk, qs, v, d_o = gs_ref[rows, :], kk_ref[rows, :], qs_ref[rows, :], i_ref[rows, :], dpre_ref[rows, :]
            eg, ekd, a = eg_ref[rows, :], ekd_ref[rows, :], a_ref[rows, :]
            dqs, dkk, dv = _hgrn2_intra_bwd(g, kk, qs, v, d_o)
            qg_f, kd_f = qs * eg, kk * ekd
            qg, kd, vb, dob = qg_f.astype(BF16), kd_f.astype(BF16), v.astype(BF16), d_o.astype(BF16)
            dqg, dkd, da, dv_st = [], [], [], []
            for h, lanes in enumerate(heads):
                st, dst = sts_ref[0, c, h], dst_ref[h]
                dstb = dst.astype(BF16)
                dqg.append(_dot(dob[:, lanes], st.astype(BF16)))
                dv_st.append(_dot_nt(kd[:, lanes], dstb))
                dkd.append(_dot(vb[:, lanes], dstb))
                da.append(jnp.broadcast_to(jnp.sum(dst * st, axis=0, keepdims=True), (SUB_CHUNK, HEAD_A)))
                dst_ref[h] = dst * a[0:1, lanes] + _dot_tn(dob[:, lanes], qg[:, lanes])
            dqg, dkd, da, dv_st = [jnp.concatenate(p, axis=1) for p in (dqg, dkd, da, dv_st)]
            d_gsum = qs * dqs - kk * dkk + dqg * qg_f - dkd * kd_f
            d_glast = jnp.sum(dkd * kd_f, axis=0, keepdims=True) + da * a
            dlf_ref[rows, :] = jnp.dot(tri_t, d_gsum, precision=lax.Precision.HIGHEST,
                                       preferred_element_type=F32) + d_glast
            dqs_ref[rows, :] = dqs + dqg * eg
            dkk_ref[rows, :] = dkk + dkd * ekd
            dp_ref[rows, 2 * A_WIDTH:3 * A_WIDTH] = (dv + dv_st).astype(BF16)
            return carry

        lax.fori_loop(0, n_sub, back, 0, unroll=2)
        sig, sq, qraw = blk["sig"], blk["sq"], blk["qraw"]
        d_f = dlf_ref[...] / blk["f"] - dkk_ref[...]
        dlb_ref[0:1, :] += jnp.sum(d_f * (1.0 - sig), axis=0, keepdims=True)
        dp_ref[:, 0:A_WIDTH] = (dqs_ref[...] * (sq * (1.0 + qraw * (1.0 - sq)))).astype(BF16)
        dp_ref[:, A_WIDTH:2 * A_WIDTH] = (d_f * (1.0 - lb) * sig * (1.0 - sig)).astype(BF16)

    def rev(s):
        return n_tb - 1 - s

    def col(k):
        return pl.BlockSpec((tb, A_WIDTH), lambda b, s, k=k: (b * n_tb + rev(s), k))

    acc8 = pl.BlockSpec((SUBLANES, A_WIDTH), lambda b, s: (0, 0))
    return pl.pallas_call(
        body, name=name, grid=(n_seq, n_tb),
        in_specs=[col(0), col(1), col(2), col(3), col(0), col(0),
                  pl.BlockSpec((1, n_sub, A_HEADS, HEAD_A, HEAD_A), lambda b, s: (b * n_tb + rev(s), 0, 0, 0, 0)),
                  pl.BlockSpec((3, A_WIDTH), lambda b, s: (0, 0)), pl.BlockSpec((1, A_WIDTH), lambda b, s: (0, 0))],
        out_specs=[pl.BlockSpec((tb, 4 * A_WIDTH), lambda b, s: (b * n_tb + rev(s), 0)), acc8, acc8],
        out_shape=[jax.ShapeDtypeStruct((t, EVEN_IN), BF16)] + [jax.ShapeDtypeStruct((SUBLANES, A_WIDTH), F32)] * 2,
        scratch_shapes=[pltpu.VMEM((A_HEADS, HEAD_A, HEAD_A), F32)] + [pltpu.VMEM((tb, A_WIDTH), F32)] * 10,
        compiler_params=_params(("arbitrary", "arbitrary")),
    )(proj, proj, proj, proj, dcat, pre, states, lb_table, a_norm)


GMLP_ROWS = 512


def _gmlp_chunk(ub, vb, ln_g, ln_b, ws, bias):
    u = [_gelu(a) for a in ub]
    v = [_gelu(a) for a in vb]
    mu = sum(jnp.sum(a, axis=-1, keepdims=True) for a in v) * (1.0 / B_WIDTH)
    cen = [a - mu for a in v]
    var = sum(jnp.sum(a * a, axis=-1, keepdims=True) for a in cen) * (1.0 / B_WIDTH)
    inv = lax.rsqrt(var + EPS)
    r = lax.broadcasted_iota(jnp.int32, (B_CHUNK, B_CHUNK), 0)
    c = lax.broadcasted_iota(jnp.int32, (B_CHUNK, B_CHUNK), 1)
    outs = []
    for g in range(B_GROUPS):
        vn = (cen[g] * inv * ln_g[g] + ln_b[g]).astype(BF16)
        wm = jnp.where(c <= r, ws[g], 0.0).astype(BF16)
        outs.append(u[g] * (_dot(wm, vn) + bias[g]))
    return outs


def _lane_groups(ref, rows=slice(None)):
    return [ref[rows, g * LANES:(g + 1) * LANES] for g in range(B_GROUPS)]


def _gmlp_fwd(proj, mixed, ln_g, ln_b, ws, bias_t, name):
    t = proj.shape[0]
    tm = GMLP_ROWS

    def body(u_ref, v_ref, lg_ref, lb_ref, ws_ref, bt_ref, _, o_ref):
        for ch in range(tm // B_CHUNK):
            rows = slice(ch * B_CHUNK, (ch + 1) * B_CHUNK)
            outs = _gmlp_chunk(_lane_groups(u_ref, rows), _lane_groups(v_ref, rows), _lane_groups(lg_ref),
                               _lane_groups(lb_ref), [ws_ref[g] for g in range(B_GROUPS)],
                               [bt_ref[:, g:g + 1] for g in range(B_GROUPS)])
            for g in range(B_GROUPS):
                o_ref[rows, g * LANES:(g + 1) * LANES] = outs[g].astype(BF16)

    vec = pl.BlockSpec((1, B_WIDTH), lambda i: (0, 0))
    return pl.pallas_call(
        body, name=name, grid=(t // tm,),
        in_specs=[pl.BlockSpec((tm, B_WIDTH), lambda i: (i, 4)), pl.BlockSpec((tm, B_WIDTH), lambda i: (i, 5)), vec, vec,
                  pl.BlockSpec((B_GROUPS, B_CHUNK, B_CHUNK), lambda i: (0, 0, 0)),
                  pl.BlockSpec((B_CHUNK, B_GROUPS), lambda i: (0, 0)), pl.BlockSpec(memory_space=pl.ANY)],
        out_specs=pl.BlockSpec((tm, B_WIDTH), lambda i: (i, 1)),
        out_shape=jax.ShapeDtypeStruct(mixed.shape, BF16),
        input_output_aliases={6: 0},
        compiler_params=_params(("parallel",)),
    )(proj, proj, ln_g, ln_b, ws, bias_t, mixed)


def _gmlp_bwd(proj, dcat, dproj, ln_g, ln_b, ws, bias_t, name):
    t = proj.shape[0]
    tm = GMLP_ROWS

    def body(u_ref, v_ref, do_ref, lg_ref, lb_ref, ws_ref, bt_ref, _, duv_ref, dlg_ref, dlb_ref, dws_ref, dbt_ref):
        @pl.when(pl.program_id(0) == 0)
        def _():
            dlg_ref[...] = jnp.zeros_like(dlg_ref)
            dlb_ref[...] = jnp.zeros_like(dlb_ref)
            dws_ref[...] = jnp.zeros_like(dws_ref)
            dbt_ref[...] = jnp.zeros_like(dbt_ref)

        for ch in range(tm // B_CHUNK):
            rows = slice(ch * B_CHUNK, (ch + 1) * B_CHUNK)
            _, vjp = jax.vjp(
                _gmlp_chunk, _lane_groups(u_ref, rows), _lane_groups(v_ref, rows), _lane_groups(lg_ref),
                _lane_groups(lb_ref), [ws_ref[g] for g in range(B_GROUPS)],
                [bt_ref[:, g:g + 1] for g in range(B_GROUPS)])
            du, dv, dlg, dlb, dw, dbt = vjp(_lane_groups(do_ref, rows))
            for g in range(B_GROUPS):
                lanes = slice(g * LANES, (g + 1) * LANES)
                duv_ref[rows, lanes] = du[g].astype(BF16)
                duv_ref[rows, B_WIDTH + g * LANES:B_WIDTH + (g + 1) * LANES] = dv[g].astype(BF16)
                dlg_ref[0:1, lanes] += dlg[g]
                dlb_ref[0:1, lanes] += dlb[g]
                dws_ref[g] += dw[g]
                dbt_ref[:, g:g + 1] += dbt[g]

    vec = pl.BlockSpec((1, B_WIDTH), lambda i: (0, 0))
    acc8 = pl.BlockSpec((SUBLANES, B_WIDTH), lambda i: (0, 0))
    ws_spec = pl.BlockSpec((B_GROUPS, B_CHUNK, B_CHUNK), lambda i: (0, 0, 0))
    bt_spec = pl.BlockSpec((B_CHUNK, B_GROUPS), lambda i: (0, 0))
    return pl.pallas_call(
        body, name=name, grid=(t // tm,),
        in_specs=[pl.BlockSpec((tm, B_WIDTH), lambda i: (i, 4)), pl.BlockSpec((tm, B_WIDTH), lambda i: (i, 5)),
                  pl.BlockSpec((tm, B_WIDTH), lambda i: (i, 1)), vec, vec, ws_spec, bt_spec,
                  pl.BlockSpec(memory_space=pl.ANY)],
        out_specs=[pl.BlockSpec((tm, 2 * B_WIDTH), lambda i: (i, 2)), acc8, acc8, ws_spec, bt_spec],
        out_shape=[jax.ShapeDtypeStruct(dproj.shape, BF16), jax.ShapeDtypeStruct((SUBLANES, B_WIDTH), F32),
                   jax.ShapeDtypeStruct((SUBLANES, B_WIDTH), F32),
                   jax.ShapeDtypeStruct((B_GROUPS, B_CHUNK, B_CHUNK), F32),
                   jax.ShapeDtypeStruct((B_CHUNK, B_GROUPS), F32)],
        input_output_aliases={7: 0},
        compiler_params=_params(("arbitrary",)),
    )(proj, proj, dcat, ln_g, ln_b, ws, bias_t, dproj)


QK_SCALE = 1.0 / math.sqrt(C_HEAD_DIM)
ATTN_UNROLL = 16
ATTN_PAIRS = 2
LANE_GROUPS = D_MODEL // LANES
ATTN_STEPS = LANE_GROUPS // ATTN_PAIRS
Q_BLOCKS = SEQ // C_BLOCK


def _attn_window(i, d):
    sub_blocks = Q_BLOCKS // d
    q0 = pl.multiple_of(i * C_BLOCK, C_BLOCK)
    k0 = pl.multiple_of(jnp.maximum(i - 1, 0) * C_BLOCK, C_BLOCK)
    key = k0 + lax.broadcasted_iota(jnp.int32, (C_BLOCK, 2 * C_BLOCK), 1)
    dist = (q0 + lax.broadcasted_iota(jnp.int32, (C_BLOCK, 2 * C_BLOCK), 0)) - key
    own_subsequence = (key >= q0) | (i % sub_blocks > 0)
    return pl.ds(q0, C_BLOCK), pl.ds(k0, 2 * C_BLOCK), (dist >= 0) & (dist <= C_BLOCK) & own_subsequence


def _head_masks():
    lane = lax.broadcasted_iota(jnp.int32, (C_BLOCK, LANES), 1)
    return [lane < C_HEAD_DIM, lane >= C_HEAD_DIM]


def _flat_spec(col_of):
    return pl.BlockSpec((1, SEQ, ATTN_PAIRS * LANES), lambda b, g: (b, 0, col_of(g)))


def _put_heads(tile, g, col0, col1):
    lane = lax.broadcasted_iota(jnp.int32, tile.shape, 1)
    return jnp.where(lane == 2 * g, col0, jnp.where(lane == 2 * g + 1, col1, tile))


def _get_head(tile, h):
    lane = lax.broadcasted_iota(jnp.int32, tile.shape, 1)
    return jnp.sum(jnp.where(lane == h, tile, 0.0), axis=1, keepdims=True)


PER_HEAD_SPEC = pl.BlockSpec((1, SEQ, LANES), lambda b, g: (b, 0, 0))


def _attn_branch_fwd(qkv, name):
    n_seq, d, l, _ = qkv.shape
    flat = qkv.reshape(n_seq, SEQ, ODD_IN)

    def body(q_ref, k_ref, v_ref, o_ref, m_ref, l_ref):
        heads = _head_masks()
        g = pl.program_id(1)

        @pl.when(g == 0)
        def _():
            m_ref[...] = jnp.zeros_like(m_ref)
            l_ref[...] = jnp.zeros_like(l_ref)

        def block(i, carry):
            rows, keys, mask = _attn_window(i, d)
            m_tile, l_tile = m_ref[0, rows, :], l_ref[0, rows, :]
            for pair in range(ATTN_PAIRS):
                lanes = slice(pair * LANES, (pair + 1) * LANES)
                q, k, v = q_ref[0, rows, lanes], k_ref[0, keys, lanes], v_ref[0, keys, lanes]
                res = []
                for hm in heads:
                    s = jnp.where(mask, _dot_nt(jnp.where(hm, q, 0), k), NEG)
                    m = jnp.max(s, axis=-1, keepdims=True)
                    p = jnp.exp(s - m)
                    res.append((_dot(p.astype(BF16), v), m, jnp.sum(p, axis=-1, keepdims=True)))
                o_ref[0, rows, lanes] = jnp.where(heads[0], res[0][0], res[1][0])
                m_tile = _put_heads(m_tile, g * ATTN_PAIRS + pair, res[0][1], res[1][1])
                l_tile = _put_heads(l_tile, g * ATTN_PAIRS + pair, res[0][2], res[1][2])
            m_ref[0, rows, :] = m_tile
            l_ref[0, rows, :] = l_tile
            return carry

        lax.fori_loop(0, Q_BLOCKS, block, 0, unroll=ATTN_UNROLL)

    o, m, l_sum = pl.pallas_call(
        body, name=name, grid=(n_seq, ATTN_STEPS),
        in_specs=[_flat_spec(lambda g: g), _flat_spec(lambda g: ATTN_STEPS + g),
                  _flat_spec(lambda g: 2 * ATTN_STEPS + g)],
        out_specs=[_flat_spec(lambda g: g), PER_HEAD_SPEC, PER_HEAD_SPEC],
        out_shape=[jax.ShapeDtypeStruct((n_seq, SEQ, D_MODEL), F32)] + [jax.ShapeDtypeStruct((n_seq, SEQ, LANES), F32)] * 2,
        compiler_params=_params(("parallel", "arbitrary")),
    )(flat, flat, flat)
    return [o.reshape(n_seq, d, l, D_MODEL), m.reshape(n_seq, d, l, LANES), l_sum.reshape(n_seq, d, l, LANES)]


def _attn_merge(branches, name):
    n_seq = branches[0][0].shape[0]
    t = n_seq * SEQ
    tm = MERGE_TILE

    def body(*refs):
        ins = refs[:9]
        o_ref, ob_ref, lse_ref = refs[9:12]
        nat = refs[12:]
        for b, d in enumerate(C_DILATIONS[1:]):
            for k in range(3):
                _load_dilated(ins[3 + 3 * b + k], d, nat[3 * b + k])
        ms = [ins[1][0, 0], nat[1][0], nat[4][0]]
        ls = [ins[2][0, 0], nat[2][0], nat[5][0]]
        m_all = jnp.maximum(jnp.maximum(ms[0], ms[1]), ms[2])
        ws = [jnp.exp(ms[b] - m_all) for b in range(3)]
        lane = lax.broadcasted_iota(jnp.int32, m_all.shape, 1)
        total = jnp.where(lane < C_HEADS, ws[0] * ls[0] + ws[1] * ls[1] + ws[2] * ls[2], 1.0)
        lse_ref[...] = m_all + jnp.log(total)
        first_head = lane < C_HEAD_DIM
        for p in range(LANE_GROUPS):
            lanes = slice(p * LANES, (p + 1) * LANES)
            spread = lambda c: jnp.where(first_head, c[:, 2 * p:2 * p + 1], c[:, 2 * p + 1:2 * p + 2])
            os_ = [ins[0][0, 0, :, lanes], nat[0][p], nat[3][p]]
            o = (spread(ws[0]) * os_[0] + spread(ws[1]) * os_[1] + spread(ws[2]) * os_[2]) / spread(total)
            o_ref[:, lanes] = o
            ob_ref[:, lanes] = o.astype(BF16)

    row = pl.BlockSpec((tm, D_MODEL), lambda i: (i, 0))
    flat = [a for br in branches for a in br]
    in_specs = []
    for wide, narrow in zip(_dilated_specs(tm, D_MODEL, lambda: 0), _dilated_specs(tm, LANES, lambda: 0)):
        in_specs += [wide, narrow, narrow]
    per_head = pltpu.VMEM((1, tm, LANES), F32)
    return pl.pallas_call(
        body, name=name, grid=(t // tm,), in_specs=in_specs,
        out_specs=[row, row, pl.BlockSpec((tm, LANES), lambda i: (i, 0))],
        out_shape=[jax.ShapeDtypeStruct((t, D_MODEL), F32), jax.ShapeDtypeStruct((t, D_MODEL), BF16),
                   jax.ShapeDtypeStruct((t, LANES), F32)],
        scratch_shapes=[pltpu.VMEM((LANE_GROUPS, tm, LANES), F32), per_head, per_head] * 2,
        compiler_params=_params(("parallel",)),
    )(*flat)


def _attn_branch_bwd(qkv, dout, lse, delta, name):
    n_seq, d, l, _ = qkv.shape
    flat = lambda a: a.reshape(n_seq, SEQ, a.shape[-1])

    def body(q_ref, k_ref, v_ref, do_ref, lse_nat_ref, dl_nat_ref, dq_ref, dk_ref, dv_ref, lse_ref, dl_ref,
             dkt_ref, dvt_ref):
        heads = _head_masks()
        g = pl.program_id(1)
        dkt_ref[...] = jnp.zeros_like(dkt_ref)
        dvt_ref[...] = jnp.zeros_like(dvt_ref)
        for nat_ref, dst_ref in ((lse_nat_ref, lse_ref), (dl_nat_ref, dl_ref)):
            for r in range(d):
                rows = pl.ds(r, l, stride=d) if d > 1 else slice(None)
                dst_ref[r * l:(r + 1) * l, :] = nat_ref.at[0][rows, :]

        def block(i, carry):
            rows, keys, mask = _attn_window(i, d)
            lse_b, dl_b = lse_ref[rows, :], dl_ref[rows, :]
            for pair in range(ATTN_PAIRS):
                lanes = slice(pair * LANES, (pair + 1) * LANES)
                q, do = q_ref[0, rows, lanes], do_ref[0, rows, lanes]
                k, v = k_ref[0, keys, lanes], v_ref[0, keys, lanes]
                dq, dk, dv = [], None, None
                for hh, hm in enumerate(heads):
                    head = 2 * (g * ATTN_PAIRS + pair) + hh
                    qh, doh = jnp.where(hm, q, 0), jnp.where(hm, do, 0)
                    s = jnp.where(mask, _dot_nt(qh, k), NEG)
                    p = jnp.exp(s - _get_head(lse_b, head))
                    ds = (p * (_dot_nt(doh, v) - _get_head(dl_b, head))).astype(BF16)
                    dq.append(_dot(ds, k) * QK_SCALE)
                    dk_h, dv_h = _dot_tn(qh, ds), _dot_tn(doh, p.astype(BF16))
                    dk = dk_h if dk is None else dk + dk_h
                    dv = dv_h if dv is None else dv + dv_h
                dq_ref[0, rows, lanes] = jnp.where(heads[0], dq[0], dq[1]).astype(BF16)
                dkt_ref[lanes, keys] += dk
                dvt_ref[lanes, keys] += dv
            return carry

        lax.fori_loop(0, Q_BLOCKS, block, 0, unroll=ATTN_UNROLL)
        for c in range(SEQ // ROW_TILE):
            rows = slice(c * ROW_TILE, (c + 1) * ROW_TILE)
            dk_ref[0, rows, :] = dkt_ref[:, rows].T.astype(BF16)
            dv_ref[0, rows, :] = dvt_ref[:, rows].T.astype(BF16)

    act = _flat_spec(lambda g: g)
    outs = pl.pallas_call(
        body, name=name, grid=(n_seq, ATTN_STEPS),
        in_specs=[_flat_spec(lambda g: g), _flat_spec(lambda g: ATTN_STEPS + g),
                  _flat_spec(lambda g: 2 * ATTN_STEPS + g), act, PER_HEAD_SPEC, PER_HEAD_SPEC],
        out_specs=[act] * 3,
        out_shape=[jax.ShapeDtypeStruct((n_seq, SEQ, D_MODEL), BF16)] * 3,
        scratch_shapes=[pltpu.VMEM((SEQ, LANES), F32)] * 2 + [pltpu.VMEM((ATTN_PAIRS * LANES, SEQ), F32)] * 2,
        compiler_params=_params(("parallel", "parallel")),
    )(flat(qkv), flat(qkv), flat(qkv), flat(dout), lse, delta)
    return [o.reshape(n_seq, d, l, D_MODEL) for o in outs]


def _attn_combine_bwd(grads, rope, name):
    n_seq = grads[0][0].shape[0]
    t = n_seq * SEQ
    tm = MERGE_TILE

    def body(*refs):
        c_ref, s_ref, o_ref, nat4_ref, nat16_ref = refs[9:]
        for sec in range(3):
            _load_dilated(refs[3 + sec], 4, nat4_ref)
            _load_dilated(refs[6 + sec], 16, nat16_ref)
            for p in range(LANE_GROUPS):
                blk = refs[sec][0, 0, :, p * LANES:(p + 1) * LANES] + nat4_ref[p] + nat16_ref[p]
                if sec < 2:
                    blk = blk * c_ref[...] - _swap_halves(blk) * s_ref[...]
                o_ref[:, sec * D_MODEL + p * LANES:sec * D_MODEL + (p + 1) * LANES] = blk.astype(BF16)

    tab = pl.BlockSpec((tm, LANES), lambda i: (i, 0))
    flat = [a for br in grads for a in br]
    in_specs = []
    for spec in _dilated_specs(tm, D_MODEL, lambda: 0):
        in_specs += [spec] * 3
    return pl.pallas_call(
        body, name=name, grid=(t // tm,), in_specs=in_specs + [tab, tab],
        out_specs=pl.BlockSpec((tm, ODD_IN), lambda i: (i, 0)),
        out_shape=jax.ShapeDtypeStruct((t, ODD_IN), BF16),
        scratch_shapes=[pltpu.VMEM((LANE_GROUPS, tm, LANES), F32)] * 2,
        compiler_params=_params(("parallel",)),
    )(*flat, *rope)


def _adamw(w, g, m, v):
    m = ADAM_B1 * m + (1.0 - ADAM_B1) * g
    v = ADAM_B2 * v + (1.0 - ADAM_B2) * jnp.square(g)
    m_hat = m / (1.0 - ADAM_B1 ** ADAM_STEP)
    v_hat = v / (1.0 - ADAM_B2 ** ADAM_STEP)
    delta = -ADAM_LR * (m_hat / (jnp.sqrt(v_hat) + ADAM_EPS) + ADAM_WD * w)
    return delta, m, v


def _adamw_sharded(parts, w, m, v, after, name):
    n_layers, rows, cols = w.shape
    tr = min(rows, 256)

    def body(*refs):
        p_refs = refs[:n_layers]
        w_ref, m_ref, v_ref, _, g_ref, d_ref, mo_ref, vo_ref = refs[n_layers:]
        layer = pl.program_id(0)
        g = None
        for l, p_ref in enumerate(p_refs):
            g_l = p_ref[0].astype(F32)
            for s in range(1, N_DEV):
                g_l = g_l + p_ref[s].astype(F32)
            g = g_l if g is None else jnp.where(layer == l, g_l, g)
        delta, mn, vn = _adamw(w_ref[0], g, m_ref[0], v_ref[0])
        g_ref[0] = g
        d_ref[0] = delta
        mo_ref[0] = mn
        vo_ref[0] = vn

    def part_spec(l):
        return pl.BlockSpec((N_DEV, tr, cols), lambda a, i: (0, jnp.where(a == l, i, 0), 0))

    row = pl.BlockSpec((1, tr, cols), lambda a, i: (a, i, 0))
    return pl.pallas_call(
        body, name=name, grid=(n_layers, rows // tr),
        in_specs=[part_spec(l) for l in range(n_layers)] + [row, row, row, pl.BlockSpec(memory_space=pl.ANY)],
        out_specs=[row] * 4, out_shape=[jax.ShapeDtypeStruct(w.shape, F32)] * 4,
        compiler_params=_params(("arbitrary", "arbitrary")),
    )(*parts, w, m, v, after)


def _small_update(gathered, where, weights, moments_m, moments_v, lb_index, name):
    n = len(weights)
    n_g = len(gathered)

    def body(*refs):
        g_refs = refs[:n_g]
        w_refs, m_refs, v_refs = refs[n_g:n_g + n], refs[n_g + n:n_g + 2 * n], refs[n_g + 2 * n:n_g + 3 * n]
        outs = refs[n_g + 3 * n:]

        def total(k):
            array, rows, lanes = where[k]
            ref = g_refs[array]
            index = (slice(None),) * (len(ref.shape) - 1) if rows is None else (rows, lanes)
            acc = ref[(0,) + index]
            for s in range(1, N_DEV):
                acc = acc + ref[(s,) + index]
            return acc

        loss_rows = total(n)
        outs[0][...] = jnp.sum(jnp.sum(loss_rows, axis=1, keepdims=True), axis=0, keepdims=True)
        for k in range(n):
            part = total(k)
            if k == lb_index:
                dlb = jnp.sum(part, axis=0, keepdims=True)
                tab = w_refs[k][...]
                e = jnp.exp(tab - jnp.max(tab, axis=0, keepdims=True))
                p = e / jnp.sum(e, axis=0, keepdims=True)
                first = lax.broadcasted_iota(jnp.int32, p.shape, 0) == 0
                grads = [(slice(None), p * (jnp.where(first, dlb, 0.0) - p[0:1, :] * dlb))]
            elif part.shape == w_refs[k].shape:
                grads = [(slice(None), part)]
            else:
                grads = [(slice(l, l + 1), jnp.sum(part[l * SUBLANES:(l + 1) * SUBLANES], axis=0, keepdims=True))
                         for l in range(w_refs[k].shape[0])]
            for rows, g in grads:
                delta, mn, vn = _adamw(w_refs[k][rows], g, m_refs[k][rows], v_refs[k][rows])
                outs[1 + 4 * k][rows] = g
                outs[2 + 4 * k][rows] = delta
                outs[3 + 4 * k][rows] = mn
                outs[4 + 4 * k][rows] = vn

    vmem = pl.BlockSpec(memory_space=pltpu.VMEM)
    out_shape = [jax.ShapeDtypeStruct((1, 1), F32)]
    for w in weights:
        out_shape += [jax.ShapeDtypeStruct(w.shape, F32)] * 4
    args = list(gathered) + list(weights) + list(moments_m) + list(moments_v)
    return pl.pallas_call(
        body, name=name, in_specs=[vmem] * len(args), out_specs=[vmem] * len(out_shape), out_shape=out_shape,
        compiler_params=pltpu.CompilerParams(vmem_limit_bytes=VMEM_LIMIT),
    )(*args)


def kernel(x, positions, norm_mix_pre, norm_mix_post, norm_ffn_pre, norm_ffn_post, w_in_even, lb_table, a_norm, b_ln_g, b_ln_b, b_ws, b_bias, w_out_even, w_in_odd, w_out_odd, w_ff1, w_ff2, loss_target, m_norm_mix_pre, m_norm_mix_post, m_norm_ffn_pre, m_norm_ffn_post, m_w_in_even, m_lb_table, m_a_norm, m_b_ln_g, m_b_ln_b, m_b_ws, m_b_bias, m_w_out_even, m_w_in_odd, m_w_out_odd, m_w_ff1, m_w_ff2, v_norm_mix_pre, v_norm_mix_post, v_norm_ffn_pre, v_norm_ffn_post, v_w_in_even, v_lb_table, v_a_norm, v_b_ln_g, v_b_ln_b, v_b_ws, v_b_bias, v_w_out_even, v_w_in_odd, v_w_out_odd, v_w_ff1, v_w_ff2):
    n_seq = x.shape[0]
    t = n_seq * SEQ
    x0 = x.reshape(t, D_MODEL)
    target = loss_target.reshape(t, D_MODEL)

    me = _my_slot().astype(jnp.int32).reshape(1)

    order = ["in_e", "out_e", "ff1_0", "ff2_0", "in_o", "out_o", "ff1_1", "ff2_1"]
    shards = dict(in_e=w_in_even[0], out_e=w_out_even[0], in_o=w_in_odd[0], out_o=w_out_odd[0],
                  ff1_0=w_ff1[0], ff1_1=w_ff1[1], ff2_0=w_ff2[0], ff2_1=w_ff2[1])
    by_columns = ("in_e", "in_o", "ff1_0", "ff1_1")

    def place(k, after):
        if k in by_columns:
            return _place_own_columns(shards[k], me, "place_" + k, after)
        return _place_own(shards[k], me, "place_" + k, False, after=after)

    gathers = {}
    send0, recv0, land0, _, token0 = _exchange_start([place(order[0], None)], [None], "gather_start_first")
    gathers[order[0]] = (land0[0], send0[0], recv0[0])
    sends, recvs, lands, _, g_token = _exchange_start([place(k, token0) for k in order[1:]],
                                                      [None] * (len(order) - 1), "gather_start")
    for k, land, send, recv in zip(order[1:], lands, sends, recvs):
        gathers[k] = (land, send, recv)

    def get_w(keys, after):
        lands_k, sends_k, recvs_k = zip(*[gathers[k] for k in keys])
        return _exchange_wait(list(lands_k), [None] * len(keys), list(sends_k), list(recvs_k), after,
                              "gather_wait_" + keys[0])

    sent = {}

    def put_g(group, blocks):
        keys = list(blocks)
        own = [_place_own(blocks[k], me, "own_" + k, True) for k in keys]
        send_sems, recv_sems, own, srcs, token = _exchange_start(own, [blocks[k] for k in keys], "scatter_start_" + group)
        sent[group] = (keys, own, srcs, send_sems, recv_sems)
        return token

    rope = _rope_tables(positions)
    bias_t = b_bias[0].T
    grads = _local_step(x0, target, rope, norm_mix_pre, norm_mix_post, norm_ffn_pre, norm_ffn_post, lb_table,
                        a_norm, b_ln_g, b_ln_b, b_ws[0], bias_t, get_w, put_g, g_token)
    (dx0, loss_part, dg_mix_pre, dg_mix_post, dg_ffn_pre, dg_ffn_post, d_lb, d_a_norm, d_ln_g, d_ln_b, d_ws,
     d_bias_t) = grads

    packed = jnp.concatenate([dg_mix_pre, dg_mix_post, dg_ffn_pre, dg_ffn_post,
                              jnp.concatenate([d_lb, d_a_norm], axis=1), jnp.concatenate([d_ln_g, d_ln_b], axis=1),
                              loss_part], axis=0)
    small_lands = [_place_own(a, me, "own_small%d" % k, False, F32) for k, a in enumerate((packed, d_ws, d_bias_t))]
    s_send, s_recv, small_lands, _, after = _exchange_start(small_lands, [None] * 3, "gather_small_start")

    big = dict(w_in_even=(["in_e"], w_in_even, m_w_in_even, v_w_in_even),
               w_out_even=(["out_e"], w_out_even, m_w_out_even, v_w_out_even),
               w_in_odd=(["in_o"], w_in_odd, m_w_in_odd, v_w_in_odd),
               w_out_odd=(["out_o"], w_out_odd, m_w_out_odd, v_w_out_odd),
               w_ff1=(["ff1_0", "ff1_1"], w_ff1, m_w_ff1, v_w_ff1), w_ff2=(["ff2_0", "ff2_1"], w_ff2, m_w_ff2, v_w_ff2))
    recv, big_out = {}, {}
    for groups, names in ((("ffn1", "ffn0"), ("w_ff1", "w_ff2")), (("mix1",), ("w_in_odd", "w_out_odd")),
                          (("mix0",), ("w_in_even", "w_out_even"))):
        for group in groups:
            keys, own, srcs, send_sems, recv_sems = sent[group]
            recv.update(zip(keys, _exchange_wait(own, srcs, send_sems, recv_sems, after, "scatter_wait_" + group)))
        for nm in names:
            keys, w, m, v = big[nm]
            big_out[nm] = _adamw_sharded([recv[k] for k in keys], w, m, v, after, "adamw_" + nm)
            after = big_out[nm][0]
    big_out = [big_out[nm] for nm in ("w_in_even", "w_out_even", "w_in_odd", "w_out_odd", "w_ff1", "w_ff2")]
    gathered = _exchange_wait(small_lands, [None] * 3, s_send, s_recv, after, "gather_small_wait")
    rows8 = lambda k: slice(SUBLANES * k, SUBLANES * (k + 1))
    left, right, every = slice(0, A_WIDTH), slice(A_WIDTH, 2 * A_WIDTH), slice(None)
    where = [(0, slice(0, 16), every), (0, slice(16, 32), every), (0, slice(32, 48), every), (0, slice(48, 64), every),
             (0, rows8(8), left), (0, rows8(8), right), (0, rows8(9), left), (0, rows8(9), right),
             (1, None, None), (2, None, None), (0, rows8(10), every)]
    small_w = [norm_mix_pre, norm_mix_post, norm_ffn_pre, norm_ffn_post, lb_table, a_norm, b_ln_g, b_ln_b,
               b_ws[0], bias_t]
    small_m = [m_norm_mix_pre, m_norm_mix_post, m_norm_ffn_pre, m_norm_ffn_post, m_lb_table, m_a_norm, m_b_ln_g,
               m_b_ln_b, m_b_ws[0], m_b_bias[0].T]
    small_v = [v_norm_mix_pre, v_norm_mix_post, v_norm_ffn_pre, v_norm_ffn_post, v_lb_table, v_a_norm, v_b_ln_g,
               v_b_ln_b, v_b_ws[0], v_b_bias[0].T]
    small_out = _small_update(gathered, where, small_w, small_m, small_v, 4, "small_update")
    loss = small_out[0].reshape(())
    small = [small_out[1 + 4 * k:5 + 4 * k] for k in range(len(small_w))]
    small[8] = [a[None] for a in small[8]]
    small[9] = [a.T[None] for a in small[9]]

    per_weight = small[0:4] + [big_out[0]] + small[4:10] + big_out[1:6]
    grad_x = dx0.reshape(x.shape)
    out = [loss, grad_x]
    for kind in range(4):
        out += [p[kind] for p in per_weight]
    return tuple(out)


def _local_step(x0, target, rope, norm_mix_pre, norm_mix_post, norm_ffn_pre, norm_ffn_post, lb_table, a_norm,
                b_ln_g, b_ln_b, ws, bias_t, get_w, put_g, token):
    def gain(a, l, tok):
        return a[l:l + 1] if tok is None else a[l:l + 1] + tok[0:1, 0:1]

    full = lambda a: a.reshape(-1, D_MODEL)
    owners = lambda a: a.reshape((N_DEV, -1) + a.shape[1:])

    (g_in_e,) = get_w(["in_e"], token)
    proj, h_mix0 = _norm_inproj(x0, gain(norm_mix_pre, 0, token), g_in_e, "inproj_even")
    mixed, pre_a, states = _hgrn2_fwd(proj, lb_table, a_norm, "hgrn2_fwd")
    mixed = _gmlp_fwd(proj, mixed, b_ln_g, b_ln_b, ws, bias_t, "gmlp_fwd")
    w_out_e = full(get_w(["out_e"], mixed)[0])
    x1, mix0 = _outproj([mixed], w_out_e, x0, gain(norm_mix_post, 0, None), "outproj_even")
    w1_0, w2_0 = get_w(["ff1_0", "ff2_0"], x1)
    w2_0 = full(w2_0)
    x2, y0, h_ffn0, r0 = _ffn_fwd(x1, gain(norm_ffn_pre, 0, None), w1_0, w2_0, gain(norm_ffn_post, 0, None), "ffn_fwd_0")
    (g_in_o,) = get_w(["in_o"], x2)
    *qkv, h_mix1 = _norm_inproj_rope(x2, gain(norm_mix_pre, 1, None), g_in_o, rope, "inproj_odd")
    branches = [_attn_branch_fwd(a, "attn_fwd_d%d" % d) for a, d in zip(qkv, C_DILATIONS)]
    attn, attn_b, lse = _attn_merge(branches, "attn_merge")
    w_out_o = full(get_w(["out_o"], attn_b)[0])
    x3, mix1 = _outproj([attn_b], w_out_o, x2, gain(norm_mix_post, 1, None), "outproj_odd")
    w1_1, w2_1 = get_w(["ff1_1", "ff2_1"], x3)
    w2_1 = full(w2_1)
    dx4, y1, h_ffn1, r1, loss_part = _ffn_fwd(x3, gain(norm_ffn_pre, 1, None), w1_1, w2_1, gain(norm_ffn_post, 1, None),
                                              "ffn_fwd_1", target)

    dx3, dy1, da1, dg_ffn_pre1, dg_ffn_post1 = _ffn_bwd(
        dx4, x3, y1, r1, gain(norm_ffn_pre, 1, None), w1_1, w2_1, gain(norm_ffn_post, 1, None), "ffn_bwd_1")
    gw_ff1_1 = _grad_w(h_ffn1, da1, True, "grad_w_ff1_1")
    gw_ff2_1 = _grad_w(r1, dy1, False, "grad_w_ff2_1")
    tok = put_g("ffn1", dict(ff1_1=gw_ff1_1, ff2_1=owners(gw_ff2_1)))
    *dattn, delta, dz1, dg_mix_post1 = _outproj_bwd_attn(dx3, mix1, gain(norm_mix_post, 1, tok), w_out_o, attn,
                                                  "outproj_bwd_odd")
    gw_out_o = _grad_w(attn_b, dz1, False, "grad_w_out_odd")
    per_seq = lambda a: a.reshape(-1, SEQ, LANES)
    grads_c = [_attn_branch_bwd(qkv[b], dattn[b], per_seq(lse), per_seq(delta), "attn_bwd_d%d" % d)
               for b, d in enumerate(C_DILATIONS)]
    dqkv = _attn_combine_bwd(grads_c, rope, "attn_combine_bwd")
    gw_in_o = _grad_w(h_mix1, dqkv, True, "grad_w_in_odd")
    tok = put_g("mix1", dict(out_o=owners(gw_out_o), in_o=gw_in_o))
    dx2, dg_mix_pre1 = _inproj_bwd(dqkv, g_in_o, dx3, x2, gain(norm_mix_pre, 1, tok), "inproj_bwd_odd")

    dx1, dy0, da0, dg_ffn_pre0, dg_ffn_post0 = _ffn_bwd(
        dx2, x1, y0, r0, gain(norm_ffn_pre, 0, None), w1_0, w2_0, gain(norm_ffn_post, 0, None), "ffn_bwd_0")
    gw_ff1_0 = _grad_w(h_ffn0, da0, True, "grad_w_ff1_0")
    gw_ff2_0 = _grad_w(r0, dy0, False, "grad_w_ff2_0")
    tok = put_g("ffn0", dict(ff1_0=gw_ff1_0, ff2_0=owners(gw_ff2_0)))
    dcat, dz0, dg_mix_post0 = _outproj_bwd(dx1, mix0, gain(norm_mix_post, 0, tok), w_out_e, "outproj_bwd_even")
    gw_out_e = _grad_w(mixed, dz0, False, "grad_w_out_even")
    dproj, d_lb, d_a_norm = _hgrn2_bwd(proj, dcat, pre_a, states, lb_table, a_norm, "hgrn2_bwd")
    dproj, d_ln_g, d_ln_b, d_ws, d_bias_t = _gmlp_bwd(proj, dcat, dproj, b_ln_g, b_ln_b, ws, bias_t, "gmlp_bwd")
    gw_in_e = _grad_w(h_mix0, dproj, True, "grad_w_in_even")
    tok = put_g("mix0", dict(out_e=owners(gw_out_e), in_e=gw_in_e))
    dx0, dg_mix_pre0 = _inproj_bwd(dproj, g_in_e, dx1, x0, gain(norm_mix_pre, 0, tok), "inproj_bwd_even")

    layers = lambda a, b: jnp.concatenate([a, b], axis=0)
    return (dx0, loss_part, layers(dg_mix_pre0, dg_mix_pre1), layers(dg_mix_post0, dg_mix_post1),
            layers(dg_ffn_pre0, dg_ffn_pre1), layers(dg_ffn_post0, dg_ffn_post1),
            d_lb, d_a_norm, d_ln_g, d_ln_b, d_ws, d_bias_t)
```

```python
import math

import jax
import jax.numpy as jnp
from jax import lax
from jax.experimental import pallas as pl
from jax.experimental.pallas import tpu as pltpu

F32 = jnp.float32
BF16 = jnp.bfloat16
MESH = pl.DeviceIdType.MESH

N_DEV = 8
D_MODEL = 1024
SEQ = 2048
EPS = 1e-6
A_WIDTH = 512
A_HEADS = 4
HEAD_A = 128
B_WIDTH = 512
B_GROUPS = 4
B_CHUNK = 128
C_HEADS = 16
C_HEAD_DIM = 64
C_ROT_HALF = 8
ROPE_THETA = 500000.0
C_DILATIONS = (1, 4, 16)
C_BLOCK = 128
D_FF = 4096
EVEN_IN = 3072
ODD_IN = 3072

ADAM_LR = 0.001
ADAM_B1 = 0.9
ADAM_B2 = 0.999
ADAM_EPS = 1e-08
ADAM_WD = 0.01
ADAM_STEP = 10

LANES = 128
SUBLANES = 8
ROW_TILE = 512
PROJ_TILE = 1024
PROJ_COLS = 768
MERGE_TILE = 256
SUB_CHUNK = 16
HGRN_BLOCK = 256
NEG = -1e30
VMEM_LIMIT = 56 * 1024 * 1024


def _params(sem):
    return pltpu.CompilerParams(dimension_semantics=sem, vmem_limit_bytes=VMEM_LIMIT)


def _dot(a, b):
    return jnp.dot(a, b, preferred_element_type=F32)


def _dot_nt(a, b):
    return lax.dot_general(a, b, (((1,), (1,)), ((), ())), preferred_element_type=F32)


def _dot_tn(a, b):
    return lax.dot_general(a, b, (((0,), (0,)), ((), ())), preferred_element_type=F32)


def _rms(x, g):
    r = lax.rsqrt(jnp.mean(x * x, axis=-1, keepdims=True) + EPS)
    return x * r * g


def _rms_bwd(x, g, dy):
    r = lax.rsqrt(jnp.mean(x * x, axis=-1, keepdims=True) + EPS)
    dyg = dy * g
    dx = r * dyg - x * (r * r * r) * jnp.mean(x * dyg, axis=-1, keepdims=True)
    return dx, dy * x * r


def _split3(x):
    hi = x.astype(BF16)
    rest = x - hi.astype(F32)
    mid = rest.astype(BF16)
    return hi, mid, (rest - mid.astype(F32)).astype(BF16)


def _mask_dot(mask, x):
    m = mask.astype(BF16)
    hi, mid, lo = _split3(x)
    return _dot(m, hi) + (_dot(m, mid) + _dot(m, lo))


def _dot_mask(x, mask):
    m = mask.astype(BF16)
    hi, mid, lo = _split3(x)
    return _dot(hi, m) + (_dot(mid, m) + _dot(lo, m))


def _rows8(v):
    return v.reshape(v.shape[0] // SUBLANES, SUBLANES, v.shape[1]).sum(axis=0)


def _sigmoid(x):
    return 1.0 / (1.0 + jnp.exp(-x))


def _gelu(x):
    return 0.5 * x * (1.0 + jnp.tanh(math.sqrt(2.0 / math.pi) * (x + 0.044715 * (x * x * x))))


def _acc_rows8(ref, val, first):
    @pl.when(first)
    def _():
        ref[...] = val

    @pl.when(jnp.logical_not(first))
    def _():
        ref[...] += val


def _my_slot():
    return 4 * lax.axis_index("x") + 2 * lax.axis_index("y") + lax.axis_index("c")


def _peer(r):
    x, y, c = lax.axis_index("x"), lax.axis_index("y"), lax.axis_index("c")
    px = 1 - x if (r >> 2) & 1 else x
    py = 1 - y if (r >> 1) & 1 else y
    pc = 1 - c if r & 1 else c
    return (px, py, pc), 4 * px + 2 * py + pc


HBM_SPEC = pl.BlockSpec(memory_space=pltpu.HBM)
SEM_SPEC = pl.BlockSpec(memory_space=pltpu.SEMAPHORE)
SPLIT_EFFECT = pltpu.SideEffectType.DATAFLOW_SIDE_EFFECTING


def _split_copies(land_ref, src_ref, send_sem, recv_sem):
    me = _my_slot()
    copies = []
    for r in range(1, N_DEV):
        peer, slot = _peer(r)
        src = _slot(land_ref, me) if src_ref is None else _slot(src_ref, slot)
        copies.append(pltpu.make_async_remote_copy(
            src_ref=src, dst_ref=_slot(land_ref, me), send_sem=send_sem, recv_sem=recv_sem,
            device_id=peer, device_id_type=MESH))
    return copies


def _slot(ref, s):
    if len(ref.shape) == 2:
        c = ref.shape[1] // N_DEV
        return ref.at[:, pl.ds(pl.multiple_of(s * c, LANES), c)]
    return ref.at[s]


def _exchange_start(lands, sources, name):
    n = len(lands)
    given = [s for s in sources if s is not None]
    arrays = list(lands) + given

    def body(*refs):
        land_refs, src_refs = refs[:n], list(refs[n:n + len(given)])
        sems = refs[len(arrays):len(arrays) + 2 * n]
        token = refs[-1]
        for k in range(n):
            src_ref = None if sources[k] is None else src_refs.pop(0)
            for copy in _split_copies(land_refs[k], src_ref, sems[k], sems[n + k]):
                copy.start()
        token[...] = jnp.zeros_like(token)

    outs = pl.pallas_call(
        body, name=name,
        out_shape=(pltpu.SemaphoreType.DMA(()),) * (2 * n) + tuple(pltpu.HBM(a.shape, a.dtype) for a in arrays)
        + (jax.ShapeDtypeStruct((SUBLANES, LANES), F32),),
        in_specs=[HBM_SPEC] * len(arrays),
        out_specs=(SEM_SPEC,) * (2 * n) + (HBM_SPEC,) * len(arrays) + (pl.BlockSpec(memory_space=pltpu.VMEM),),
        input_output_aliases={i: 2 * n + i for i in range(len(arrays))},
        compiler_params=pltpu.CompilerParams(has_side_effects=SPLIT_EFFECT),
    )(*[pltpu.with_memory_space_constraint(a, pltpu.HBM) for a in arrays])
    return list(outs[:n]), list(outs[n:2 * n]), list(outs[2 * n:3 * n]), list(outs[3 * n:-1]), outs[-1]


def _exchange_wait(lands, sources, send_sems, recv_sems, after, name):
    n = len(lands)
    given = [s for s in sources if s is not None]
    arrays = list(lands) + given

    def body(*refs):
        land_refs, src_refs = refs[:n], list(refs[n:n + len(given)])
        sems = refs[len(arrays):len(arrays) + 2 * n]
        for i in range(n):
            src_ref = None if sources[i] is None else src_refs.pop(0)
            copies = _split_copies(land_refs[i], src_ref, sems[i], sems[n + i])
            for copy in copies:
                copy.wait_recv()
            for copy in copies:
                copy.wait_send()

    outs = pl.pallas_call(
        body, name=name, out_shape=tuple(pltpu.HBM(a.shape, a.dtype) for a in arrays),
        in_specs=[HBM_SPEC] * len(arrays) + [SEM_SPEC] * (2 * n) + [pl.BlockSpec(memory_space=pl.ANY)],
        out_specs=(HBM_SPEC,) * len(arrays),
        input_output_aliases={i: i for i in range(len(arrays))},
        compiler_params=pltpu.CompilerParams(has_side_effects=SPLIT_EFFECT),
    )(*arrays, *send_sems, *recv_sems, after)
    return list(outs[:n])


def _place_own(a, me, name, own_block, dtype=BF16, after=None):
    shape = a.shape[1:] if own_block else a.shape
    cols = shape[-1]
    a3 = a.reshape((N_DEV if own_block else 1, -1, cols))
    rows = a3.shape[1]
    tr = min(rows, 512)

    def body(me_ref, a_ref, _, o_ref):
        o_ref[...] = a_ref[...].astype(dtype)

    grid_spec = pltpu.PrefetchScalarGridSpec(
        num_scalar_prefetch=1, grid=(rows // tr,),
        in_specs=[pl.BlockSpec((1, tr, cols), lambda i, me_ref: (me_ref[0] if own_block else 0, i, 0)),
                  pl.BlockSpec(memory_space=pl.ANY)],
        out_specs=pl.BlockSpec((1, tr, cols), lambda i, me_ref: (me_ref[0], i, 0)))
    out = pl.pallas_call(
        body, name=name, grid_spec=grid_spec, out_shape=jax.ShapeDtypeStruct((N_DEV, rows, cols), dtype),
        compiler_params=_params(("arbitrary",)),
    )(me, a3, a3 if after is None else after)
    return out.reshape((N_DEV,) + shape)


def _place_own_columns(a, me, name, after=None):
    rows, cols = a.shape
    tr = min(rows, 512)

    def body(me_ref, a_ref, _, o_ref):
        o_ref[...] = a_ref[...].astype(BF16)

    grid_spec = pltpu.PrefetchScalarGridSpec(
        num_scalar_prefetch=1, grid=(rows // tr,),
        in_specs=[pl.BlockSpec((tr, cols), lambda i, me_ref: (i, 0)), pl.BlockSpec(memory_space=pl.ANY)],
        out_specs=pl.BlockSpec((tr, cols), lambda i, me_ref: (i, me_ref[0])))
    return pl.pallas_call(
        body, name=name, grid_spec=grid_spec, out_shape=jax.ShapeDtypeStruct((rows, N_DEV * cols), BF16),
        compiler_params=_params(("arbitrary",)),
    )(me, a, a if after is None else after)


def _rope_tables(positions):
    in_head = jnp.arange(LANES) % C_HEAD_DIM
    inv = ROPE_THETA ** (-(in_head % C_ROT_HALF).astype(F32) / C_ROT_HALF)
    ang = positions.reshape(-1)[:, None].astype(F32) * inv
    rotated = in_head < 2 * C_ROT_HALF
    sin = jnp.sin(ang)
    return (jnp.where(rotated, jnp.cos(ang), 1.0),
            jnp.where(in_head < C_ROT_HALF, -sin, jnp.where(rotated, sin, 0.0)))


def _swap_halves(x):
    lane = lax.broadcasted_iota(jnp.int32, x.shape, 1) % C_HEAD_DIM
    return jnp.where(lane < C_ROT_HALF, pltpu.roll(x, LANES - C_ROT_HALF, 1), pltpu.roll(x, C_ROT_HALF, 1))


def _norm_inproj(x, g, w, name):
    t = x.shape[0]
    n = w.shape[1]
    tm, tn = PROJ_TILE, PROJ_COLS

    def body(x_ref, g_ref, w_ref, o_ref, h_ref):
        @pl.when(pl.program_id(1) == 0)
        def _():
            h_ref[...] = _rms(x_ref[...], g_ref[...]).astype(BF16)

        cols = pl.ds(pl.multiple_of(pl.program_id(1) * tn, LANES), tn)
        o_ref[...] = _dot(h_ref[...], w_ref[:, cols])

    return pl.pallas_call(
        body, name=name, grid=(t // tm, n // tn),
        in_specs=[pl.BlockSpec((tm, D_MODEL), lambda i, j: (i, 0)), pl.BlockSpec((1, D_MODEL), lambda i, j: (0, 0)),
                  pl.BlockSpec((D_MODEL, n), lambda i, j: (0, 0))],
        out_specs=[pl.BlockSpec((tm, tn), lambda i, j: (i, j)), pl.BlockSpec((tm, D_MODEL), lambda i, j: (i, 0))],
        out_shape=[jax.ShapeDtypeStruct((t, n), F32), jax.ShapeDtypeStruct((t, D_MODEL), BF16)],
        compiler_params=_params(("parallel", "arbitrary")),
    )(x, g, w)


def _dilated_specs(tm, width, col_of):
    per_seq = SEQ // tm
    specs = []
    for d in C_DILATIONS:
        specs.append(pl.BlockSpec(
            (1, d, tm // d, width), lambda i, *rest: (i // per_seq, 0, i % per_seq, col_of(*rest))))
    return specs


def _dilated_shapes(n_seq, cols, dtype):
    return [jax.ShapeDtypeStruct((n_seq, d, SEQ // d, cols), dtype) for d in C_DILATIONS]


def _store_dilated(src_ref, out_refs, dtype):
    groups, tm, _ = src_ref.shape
    for d, o_ref in zip(C_DILATIONS, out_refs):
        for r in range(d):
            rows = pl.ds(r, tm // d, stride=d) if d > 1 else slice(None)
            for p in range(groups):
                o_ref[0, r, :, p * LANES:(p + 1) * LANES] = src_ref.at[p][rows, :].astype(dtype)


def _load_dilated(in_ref, d, dst_ref):
    groups, tm, _ = dst_ref.shape
    for r in range(d):
        rows = pl.ds(r, tm // d, stride=d)
        for p in range(groups):
            dst_ref.at[p][rows, :] = in_ref[0, r, :, p * LANES:(p + 1) * LANES].astype(F32)


def _norm_inproj_rope(x, g, w, rope, name):
    t = x.shape[0]
    n = w.shape[1]
    tm, nb = PROJ_TILE, PROJ_COLS

    def body(x_ref, g_ref, w_ref, c_ref, s_ref, o1_ref, o4_ref, o16_ref, h_ref, tile_ref):
        j = pl.program_id(1)

        @pl.when(j == 0)
        def _():
            h_ref[...] = _rms(x_ref[...], g_ref[...]).astype(BF16)

        acc = _dot(h_ref[...], w_ref[:, pl.ds(pl.multiple_of(j * nb, LANES), nb)])
        for p in range(nb // LANES):
            blk = acc[:, p * LANES:(p + 1) * LANES]
            roped = blk * c_ref[...] + _swap_halves(blk) * s_ref[...]
            piece = j * (nb // LANES) + p
            is_qk = piece < 2 * (D_MODEL // LANES)
            tile_ref[p] = jnp.where(is_qk, roped, blk) * jnp.where(piece < D_MODEL // LANES, QK_SCALE, 1.0)
        _store_dilated(tile_ref, (o1_ref, o4_ref, o16_ref), BF16)

    return pl.pallas_call(
        body, name=name, grid=(t // tm, n // nb),
        in_specs=[pl.BlockSpec((tm, D_MODEL), lambda i, j: (i, 0)), pl.BlockSpec((1, D_MODEL), lambda i, j: (0, 0)),
                  pl.BlockSpec((D_MODEL, n), lambda i, j: (0, 0)),
                  pl.BlockSpec((tm, LANES), lambda i, j: (i, 0)), pl.BlockSpec((tm, LANES), lambda i, j: (i, 0))],
        out_specs=_dilated_specs(tm, nb, lambda j: j) + [pl.BlockSpec((tm, D_MODEL), lambda i, j: (i, 0))],
        out_shape=_dilated_shapes(t // SEQ, n, BF16) + [jax.ShapeDtypeStruct((t, D_MODEL), BF16)],
        scratch_shapes=[pltpu.VMEM((nb // LANES, tm, LANES), F32)],
        compiler_params=_params(("parallel", "arbitrary")),
    )(x, g, w, *rope)


def _outproj(parts, w, x, g, name):
    t = x.shape[0]
    tm = PROJ_TILE
    n = len(parts)
    widths = [p.shape[1] for p in parts]

    def body(*refs):
        p_refs = refs[:n]
        w_ref, x_ref, g_ref, xo_ref, mix_ref = refs[n:]
        mix = None
        off = 0
        for p_ref, wd in zip(p_refs, widths):
            term = _dot(p_ref[...].astype(BF16), w_ref[off:off + wd, :])
            mix = term if mix is None else mix + term
            off += wd
        mix_ref[...] = mix
        xo_ref[...] = x_ref[...] + _rms(mix, g_ref[...])

    row = lambda i: (i, 0)
    return pl.pallas_call(
        body, name=name, grid=(t // tm,),
        in_specs=[pl.BlockSpec((tm, wd), row) for wd in widths] + [
            pl.BlockSpec((sum(widths), D_MODEL), lambda i: (0, 0)),
            pl.BlockSpec((tm, D_MODEL), row), pl.BlockSpec((1, D_MODEL), lambda i: (0, 0))],
        out_specs=[pl.BlockSpec((tm, D_MODEL), row)] * 2,
        out_shape=[jax.ShapeDtypeStruct((t, D_MODEL), F32)] * 2,
        compiler_params=_params(("parallel",)),
    )(*parts, w, x, g)


def _outproj_bwd(dx, mix, g, w, name):
    t = dx.shape[0]
    tm = PROJ_TILE
    k = w.shape[0]

    def body(dx_ref, mix_ref, g_ref, w_ref, dcat_ref, dz_ref, dg_ref):
        dz, dgr = _rms_bwd(mix_ref[...], g_ref[...], dx_ref[...])
        dzb = dz.astype(BF16)
        dz_ref[...] = dzb
        dcat_ref[...] = _dot_nt(dzb, w_ref[...])
        _acc_rows8(dg_ref, _rows8(dgr), pl.program_id(0) == 0)

    row = lambda i: (i, 0)
    return pl.pallas_call(
        body, name=name, grid=(t // tm,),
        in_specs=[pl.BlockSpec((tm, D_MODEL), row), pl.BlockSpec((tm, D_MODEL), row),
                  pl.BlockSpec((1, D_MODEL), lambda i: (0, 0)), pl.BlockSpec((k, D_MODEL), lambda i: (0, 0))],
        out_specs=[pl.BlockSpec((tm, k), row), pl.BlockSpec((tm, D_MODEL), row),
                   pl.BlockSpec((SUBLANES, D_MODEL), lambda i: (0, 0))],
        out_shape=[jax.ShapeDtypeStruct((t, k), F32), jax.ShapeDtypeStruct((t, D_MODEL), BF16),
                   jax.ShapeDtypeStruct((SUBLANES, D_MODEL), F32)],
        compiler_params=_params(("arbitrary",)),
    )(dx, mix, g, w)


def _outproj_bwd_attn(dx, mix, g, w, out, name):
    t = dx.shape[0]
    tm = MERGE_TILE

    def body(dx_ref, mix_ref, g_ref, w_ref, out_ref, do1, do4, do16, dl_ref, dz_ref, dg_ref, tile_ref):
        dz, dgr = _rms_bwd(mix_ref[...], g_ref[...], dx_ref[...])
        dzb = dz.astype(BF16)
        dz_ref[...] = dzb
        _acc_rows8(dg_ref, _rows8(dgr), pl.program_id(0) == 0)
        dout = _dot_nt(dzb, w_ref[...])
        for p in range(LANE_GROUPS):
            tile_ref[p] = dout[:, p * LANES:(p + 1) * LANES]
        _store_dilated(tile_ref, (do1, do4, do16), BF16)
        column = lax.broadcasted_iota(jnp.int32, (D_MODEL, LANES), 0) // C_HEAD_DIM
        head = lax.broadcasted_iota(jnp.int32, (D_MODEL, LANES), 1)
        dl_ref[...] = _dot_mask(dout * out_ref[...], column == head)

    row = lambda i: (i, 0)
    n_seq = t // SEQ
    return pl.pallas_call(
        body, name=name, grid=(t // tm,),
        in_specs=[pl.BlockSpec((tm, D_MODEL), row), pl.BlockSpec((tm, D_MODEL), row),
                  pl.BlockSpec((1, D_MODEL), lambda i: (0, 0)), pl.BlockSpec((D_MODEL, D_MODEL), lambda i: (0, 0)),
                  pl.BlockSpec((tm, D_MODEL), row)],
        out_specs=_dilated_specs(tm, D_MODEL, lambda: 0) + [
            pl.BlockSpec((tm, LANES), row), pl.BlockSpec((tm, D_MODEL), row),
            pl.BlockSpec((SUBLANES, D_MODEL), lambda i: (0, 0))],
        out_shape=_dilated_shapes(n_seq, D_MODEL, BF16) + [
            jax.ShapeDtypeStruct((t, LANES), F32), jax.ShapeDtypeStruct((t, D_MODEL), BF16),
            jax.ShapeDtypeStruct((SUBLANES, D_MODEL), F32)],
        scratch_shapes=[pltpu.VMEM((LANE_GROUPS, tm, LANES), F32)],
        compiler_params=_params(("arbitrary",)),
    )(dx, mix, g, w, out)


def _inproj_bwd(dproj, w, dx, x, g, name):
    t = x.shape[0]
    n = w.shape[1]
    tm = ROW_TILE

    def body(dp_ref, w_ref, dx_ref, x_ref, g_ref, o_ref, dg_ref):
        dxn, dgr = _rms_bwd(x_ref[...], g_ref[...], _dot_nt(dp_ref[...], w_ref[...]))
        o_ref[...] = dx_ref[...] + dxn
        _acc_rows8(dg_ref, _rows8(dgr), pl.program_id(0) == 0)

    row = lambda i: (i, 0)
    return pl.pallas_call(
        body, name=name, grid=(t // tm,),
        in_specs=[pl.BlockSpec((tm, n), row), pl.BlockSpec((D_MODEL, n), lambda i: (0, 0)),
                  pl.BlockSpec((tm, D_MODEL), row), pl.BlockSpec((tm, D_MODEL), row),
                  pl.BlockSpec((1, D_MODEL), lambda i: (0, 0))],
        out_specs=[pl.BlockSpec((tm, D_MODEL), row), pl.BlockSpec((SUBLANES, D_MODEL), lambda i: (0, 0))],
        out_shape=[jax.ShapeDtypeStruct((t, D_MODEL), F32), jax.ShapeDtypeStruct((SUBLANES, D_MODEL), F32)],
        compiler_params=_params(("arbitrary",)),
    )(dproj, w, dx, x, g)


def _grad_w(a, b, col_blocks, name):
    t, k = a.shape
    n = b.shape[1]
    tk = min(k, 1024)
    per_owner = n // N_DEV
    tn = 2 * per_owner if col_blocks else min(n, 1024)

    def body(a_ref, b_ref, o_ref, at_ref):
        @pl.when(pl.program_id(1) == 0)
        def _():
            for c in range(t // ROW_TILE):
                rows = slice(c * ROW_TILE, (c + 1) * ROW_TILE)
                at_ref[:, rows] = a_ref[rows, :].T

        res = _dot(at_ref[...], b_ref[...]).astype(BF16)
        if col_blocks:
            o_ref[0] = res[:, :per_owner]
            o_ref[1] = res[:, per_owner:]
        else:
            o_ref[...] = res

    if col_blocks:
        out_spec = pl.BlockSpec((2, tk, per_owner), lambda i, j: (j, i, 0))
        out_shape = jax.ShapeDtypeStruct((N_DEV, k, per_owner), BF16)
    else:
        out_spec = pl.BlockSpec((tk, tn), lambda i, j: (i, j))
        out_shape = jax.ShapeDtypeStruct((k, n), BF16)
    return pl.pallas_call(
        body, name=name, grid=(k // tk, n // tn),
        in_specs=[pl.BlockSpec((t, tk), lambda i, j: (0, i)), pl.BlockSpec((t, tn), lambda i, j: (0, j))],
        out_specs=out_spec, out_shape=out_shape,
        scratch_shapes=[pltpu.VMEM((tk, t), BF16)],
        compiler_params=_params(("parallel", "arbitrary")),
    )(a, b)


FF_STEP = 1024
FF_STEPS = D_FF // FF_STEP


def _ffn_fwd(x, g_pre, w1, w2, g_post, name, target=None):
    t = x.shape[0]
    tm = PROJ_TILE

    def body(*refs):
        if target is None:
            x_ref, gp_ref, w1_ref, w2_ref, gq_ref, xo_ref, y_ref, h_ref, r_ref = refs
        else:
            x_ref, gp_ref, w1_ref, w2_ref, gq_ref, t_ref, xo_ref, y_ref, h_ref, r_ref, l_ref = refs
        i, j = pl.program_id(0), pl.program_id(1)

        @pl.when(j == 0)
        def _():
            h_ref[...] = _rms(x_ref[...], gp_ref[...]).astype(BF16)

        a = _dot(h_ref[...], w1_ref[...])
        r = jnp.square(jnp.maximum(a, 0.0)).astype(BF16)
        r_ref[...] = r
        term = _dot(r, w2_ref[...])

        @pl.when(j == 0)
        def _():
            y_ref[...] = term

        @pl.when(j > 0)
        def _():
            y_ref[...] += term

        @pl.when(j == FF_STEPS - 1)
        def _():
            x_new = x_ref[...] + _rms(y_ref[...], gq_ref[...])
            if target is None:
                xo_ref[...] = x_new
            else:
                diff = x_new - t_ref[...]
                xo_ref[...] = diff * (1.0 / D_MODEL)
                _acc_rows8(l_ref, _rows8(diff * diff) * (0.5 / D_MODEL), i == 0)

    row = lambda i, j: (i, 0)
    vec = pl.BlockSpec((1, D_MODEL), lambda i, j: (0, 0))
    in_specs = [pl.BlockSpec((tm, D_MODEL), row), vec, pl.BlockSpec((D_MODEL, FF_STEP), lambda i, j: (0, j)),
                pl.BlockSpec((FF_STEP, D_MODEL), lambda i, j: (j, 0)), vec]
    out_specs = [pl.BlockSpec((tm, D_MODEL), row)] * 3 + [pl.BlockSpec((tm, FF_STEP), lambda i, j: (i, j))]
    out_shape = [jax.ShapeDtypeStruct((t, D_MODEL), F32), jax.ShapeDtypeStruct((t, D_MODEL), F32),
                 jax.ShapeDtypeStruct((t, D_MODEL), BF16), jax.ShapeDtypeStruct((t, D_FF), BF16)]
    args = [x, g_pre, w1, w2, g_post]
    if target is not None:
        in_specs.append(pl.BlockSpec((tm, D_MODEL), row))
        out_specs.append(pl.BlockSpec((SUBLANES, D_MODEL), lambda i, j: (0, 0)))
        out_shape.append(jax.ShapeDtypeStruct((SUBLANES, D_MODEL), F32))
        args.append(target)
    return pl.pallas_call(
        body, name=name, grid=(t // tm, FF_STEPS), in_specs=in_specs, out_specs=out_specs, out_shape=out_shape,
        compiler_params=_params(("parallel" if target is None else "arbitrary", "arbitrary")),
    )(*args)


def _ffn_bwd(dxo, x, y, r, g_pre, w1, w2, g_post, name):
    t = x.shape[0]
    tm = ROW_TILE

    def body(dxo_ref, x_ref, y_ref, r_ref, gp_ref, w1_ref, w2_ref, gq_ref,
             dx_ref, dy_ref, da_ref, dgp_ref, dgq_ref, acc_ref):
        i, j = pl.program_id(0), pl.program_id(1)

        @pl.when(j == 0)
        def _():
            dy, dgr = _rms_bwd(y_ref[...], gq_ref[...], dxo_ref[...])
            dy_ref[...] = dy.astype(BF16)
            _acc_rows8(dgq_ref, _rows8(dgr), i == 0)

        dr = _dot_nt(dy_ref[...], w2_ref[...])
        da = (dr * (2.0 * jnp.sqrt(r_ref[...].astype(F32)))).astype(BF16)
        da_ref[...] = da
        term = _dot_nt(da, w1_ref[...])

        @pl.when(j == 0)
        def _():
            acc_ref[...] = term

        @pl.when(j > 0)
        def _():
            acc_ref[...] += term

        @pl.when(j == FF_STEPS - 1)
        def _():
            dxn, dgr = _rms_bwd(x_ref[...], gp_ref[...], acc_ref[...])
            dx_ref[...] = dxo_ref[...] + dxn
            _acc_rows8(dgp_ref, _rows8(dgr), i == 0)

    row = lambda i, j: (i, 0)
    vec = pl.BlockSpec((1, D_MODEL), lambda i, j: (0, 0))
    acc8 = pl.BlockSpec((SUBLANES, D_MODEL), lambda i, j: (0, 0))
    return pl.pallas_call(
        body, name=name, grid=(t // tm, FF_STEPS),
        in_specs=[pl.BlockSpec((tm, D_MODEL), row)] * 3 + [
            pl.BlockSpec((tm, FF_STEP), lambda i, j: (i, j)),
            vec, pl.BlockSpec((D_MODEL, FF_STEP), lambda i, j: (0, j)),
            pl.BlockSpec((FF_STEP, D_MODEL), lambda i, j: (j, 0)), vec],
        out_specs=[pl.BlockSpec((tm, D_MODEL), row), pl.BlockSpec((tm, D_MODEL), row),
                   pl.BlockSpec((tm, FF_STEP), lambda i, j: (i, j)), acc8, acc8],
        out_shape=[jax.ShapeDtypeStruct((t, D_MODEL), F32), jax.ShapeDtypeStruct((t, D_MODEL), BF16),
                   jax.ShapeDtypeStruct((t, D_FF), BF16),
                   jax.ShapeDtypeStruct((SUBLANES, D_MODEL), F32), jax.ShapeDtypeStruct((SUBLANES, D_MODEL), F32)],
        scratch_shapes=[pltpu.VMEM((tm, D_MODEL), F32)],
        compiler_params=_params(("arbitrary", "arbitrary")),
    )(dxo, x, y, r, g_pre, w1, w2, g_post)


def _lower_bound(table):
    e = jnp.exp(table - jnp.max(table, axis=0, keepdims=True))
    return e[0:1, :] / jnp.sum(e, axis=0, keepdims=True)


def _hgrn2_block(q_ref, f_ref, lb):
    tb = f_ref.shape[0]
    sig = _sigmoid(f_ref[...])
    f = lb + (1.0 - lb) * sig
    qraw = q_ref[...]
    sq = _sigmoid(qraw)
    r = lax.broadcasted_iota(jnp.int32, (tb, tb), 0)
    c = lax.broadcasted_iota(jnp.int32, (tb, tb), 1)
    same = (r // SUB_CHUNK) == (c // SUB_CHUNK)
    logf = jnp.log(f)
    gsum = _mask_dot(same & (c <= r), logf)
    glast = _mask_dot(same, logf)
    return dict(sig=sig, f=f, kk=1.0 - f, qraw=qraw, sq=sq, qs=qraw * sq, gsum=gsum,
                eg=jnp.exp(gsum), ekd=jnp.exp(glast - gsum), a=jnp.exp(glast))


def _head_sums(x):
    parts = [jnp.broadcast_to(jnp.sum(x[:, h * HEAD_A:(h + 1) * HEAD_A], axis=1, keepdims=True), (x.shape[0], HEAD_A))
             for h in range(A_HEADS)]
    return jnp.concatenate(parts, axis=1)


def _hgrn2_intra(g, kk, qs, v):
    row = lax.broadcasted_iota(jnp.int32, g.shape, 0)
    o = _head_sums(qs * kk) * v
    for j in range(1, SUB_CHUNK):
        decay = jnp.exp(jnp.where(row >= j, g - pltpu.roll(g, j, 0), NEG))
        o = o + _head_sums(qs * pltpu.roll(kk, j, 0) * decay) * pltpu.roll(v, j, 0)
    return o


def _hgrn2_intra_bwd(g, kk, qs, v, do):
    row = lax.broadcasted_iota(jnp.int32, g.shape, 0)
    dsc = _head_sums(do * v)
    dqs, dkk, dv = dsc * kk, dsc * qs, _head_sums(qs * kk) * do
    for j in range(1, SUB_CHUNK):
        k_dn = pltpu.roll(kk, j, 0)
        decay = jnp.exp(jnp.where(row >= j, g - pltpu.roll(g, j, 0), NEG))
        d_score = _head_sums(do * pltpu.roll(v, j, 0)) * decay
        dqs = dqs + d_score * k_dn
        dkk = dkk + pltpu.roll(d_score * qs, SUB_CHUNK - j, 0)
        dv = dv + pltpu.roll(_head_sums(qs * k_dn * decay) * do, SUB_CHUNK - j, 0)
    return dqs, dkk, dv


def _hgrn2_fwd(proj, lb_table, a_norm, name):
    t = proj.shape[0]
    tb = HGRN_BLOCK
    n_tb = SEQ // tb
    n_seq = t // SEQ
    n_sub = tb // SUB_CHUNK

    def body(q_ref, f_ref, i_ref, g_ref, lbt_ref, an_ref, o_ref, pre_ref, sts_ref, st_ref,
             gs_ref, kk_ref, qs_ref, eg_ref, ekd_ref, a_ref):
        @pl.when(pl.program_id(1) == 0)
        def _():
            st_ref[...] = jnp.zeros_like(st_ref)

        an = an_ref[...]
        blk = _hgrn2_block(q_ref, f_ref, _lower_bound(lbt_ref[...]))
        for ref, key in ((gs_ref, "gsum"), (kk_ref, "kk"), (qs_ref, "qs"), (eg_ref, "eg"), (ekd_ref, "ekd"), (a_ref, "a")):
            ref[...] = blk[key]

        def step(c, carry):
            rows = pl.ds(pl.multiple_of(c * SUB_CHUNK, SUB_CHUNK), SUB_CHUNK)
            kk, qs, v = kk_ref[rows, :], qs_ref[rows, :], i_ref[rows, :]
            o = _hgrn2_intra(gs_ref[rows, :], kk, qs, v)
            qg, kd, vb = (qs * eg_ref[rows, :]).astype(BF16), (kk * ekd_ref[rows, :]).astype(BF16), v.astype(BF16)
            for h in range(A_HEADS):
                lanes = slice(h * HEAD_A, (h + 1) * HEAD_A)
                st = st_ref[h]
                sts_ref[0, c, h] = st
                o_h = o[:, lanes] + _dot_nt(qg[:, lanes], st.astype(BF16))
                st_ref[h] = st * a_ref[rows, lanes][0:1] + _dot_tn(vb[:, lanes], kd[:, lanes])
                pre_ref[rows, lanes] = o_h
                graw = g_ref[rows, lanes]
                o_ref[rows, lanes] = (_rms(o_h, an[:, lanes]) * (graw * _sigmoid(graw))).astype(BF16)
            return carry

        lax.fori_loop(0, n_sub, step, 0, unroll=2)

    def col(k):
        return pl.BlockSpec((tb, A_WIDTH), lambda b, s, k=k: (b * n_tb + s, k))

    out_rows = pl.BlockSpec((tb, A_WIDTH), lambda b, s: (b * n_tb + s, 0))
    return pl.pallas_call(
        body, name=name, grid=(n_seq, n_tb),
        in_specs=[col(0), col(1), col(2), col(3),
                  pl.BlockSpec((3, A_WIDTH), lambda b, s: (0, 0)), pl.BlockSpec((1, A_WIDTH), lambda b, s: (0, 0))],
        out_specs=[out_rows, out_rows,
                   pl.BlockSpec((1, n_sub, A_HEADS, HEAD_A, HEAD_A), lambda b, s: (b * n_tb + s, 0, 0, 0, 0))],
        out_shape=[jax.ShapeDtypeStruct((t, D_MODEL), BF16), jax.ShapeDtypeStruct((t, A_WIDTH), F32),
                   jax.ShapeDtypeStruct((n_seq * n_tb, n_sub, A_HEADS, HEAD_A, HEAD_A), F32)],
        scratch_shapes=[pltpu.VMEM((A_HEADS, HEAD_A, HEAD_A), F32)] + [pltpu.VMEM((tb, A_WIDTH), F32)] * 6,
        compiler_params=_params(("parallel", "arbitrary")),
    )(proj, proj, proj, proj, lb_table, a_norm)


def _hgrn2_bwd(proj, dcat, pre, states, lb_table, a_norm, name):
    t = proj.shape[0]
    tb = HGRN_BLOCK
    n_tb = SEQ // tb
    n_seq = t // SEQ
    n_sub = tb // SUB_CHUNK

    def body(q_ref, f_ref, i_ref, g_ref, do_ref, pre_ref, sts_ref, lbt_ref, an_ref, dp_ref, dlb_ref, dan_ref, dst_ref,
             gs_ref, kk_ref, qs_ref, eg_ref, ekd_ref, a_ref, dpre_ref, dlf_ref, dqs_ref, dkk_ref):
        b, s = pl.program_id(0), pl.program_id(1)

        @pl.when(s == 0)
        def _():
            dst_ref[...] = jnp.zeros_like(dst_ref)

        @pl.when((b == 0) & (s == 0))
        def _():
            dlb_ref[...] = jnp.zeros_like(dlb_ref)
            dan_ref[...] = jnp.zeros_like(dan_ref)

        lb = _lower_bound(lbt_ref[...])
        an = an_ref[...]
        heads = [slice(h * HEAD_A, (h + 1) * HEAD_A) for h in range(A_HEADS)]
        blk = _hgrn2_block(q_ref, f_ref, lb)
        for ref, key in ((gs_ref, "gsum"), (kk_ref, "kk"), (qs_ref, "qs"), (eg_ref, "eg"), (ekd_ref, "ekd"), (a_ref, "a")):
            ref[...] = blk[key]
        for h, lanes in enumerate(heads):
            graw, o = g_ref[:, lanes], pre_ref[:, lanes]
            sg = _sigmoid(graw)
            dout = do_ref[:, lanes]
            d_o, dgr = _rms_bwd(o, an[:, lanes], dout * (graw * sg))
            dan_ref[0:1, lanes] += jnp.sum(dgr, axis=0, keepdims=True)
            dp_ref[:, 3 * A_WIDTH + h * HEAD_A:3 * A_WIDTH + (h + 1) * HEAD_A] = (
                dout * _rms(o, an[:, lanes]) * (sg * (1.0 + graw * (1.0 - sg)))).astype(BF16)
            dpre_ref[:, lanes] = d_o

        tri_t = (lax.broadcasted_iota(jnp.int32, (SUB_CHUNK, SUB_CHUNK), 0)
                 <= lax.broadcasted_iota(jnp.int32, (SUB_CHUNK, SUB_CHUNK), 1)).astype(F32)

        def back(k, carry):
            c = n_sub - 1 - k
            rows = pl.ds(pl.multiple_of(c * SUB_CHUNK, SUB_CHUNK), SUB_CHUNK)
            g, kk, qs, v, d_o = gs_ref[rows, :], kk_ref[rows, :], qs_ref[rows, :], i_ref[rows, :], dpre_ref[rows, :]
            eg, ekd, a = eg_ref[rows, :], ekd_ref[rows, :], a_ref[rows, :]
            dqs, dkk, dv = _hgrn2_intra_bwd(g, kk, qs, v, d_o)
            qg_f, kd_f = qs * eg, kk * ekd
            qg, kd, vb, dob = qg_f.astype(BF16), kd_f.astype(BF16), v.astype(BF16), d_o.astype(BF16)
            dqg, dkd, da, dv_st = [], [], [], []
            for h, lanes in enumerate(heads):
                st, dst = sts_ref[0, c, h], dst_ref[h]
                dstb = dst.astype(BF16)
                dqg.append(_dot(dob[:, lanes], st.astype(BF16)))
                dv_st.append(_dot_nt(kd[:, lanes], dstb))
                dkd.append(_dot(vb[:, lanes], dstb))
                da.append(jnp.broadcast_to(jnp.sum(dst * st, axis=0, keepdims=True), (SUB_CHUNK, HEAD_A)))
                dst_ref[h] = dst * a[0:1, lanes] + _dot_tn(dob[:, lanes], qg[:, lanes])
            dqg, dkd, da, dv_st = [jnp.concatenate(p, axis=1) for p in (dqg, dkd, da, dv_st)]
            d_gsum = qs * dqs - kk * dkk + dqg * qg_f - dkd * kd_f
            d_glast = jnp.sum(dkd * kd_f, axis=0, keepdims=True) + da * a
            dlf_ref[rows, :] = jnp.dot(tri_t, d_gsum, precision=lax.Precision.HIGHEST,
                                       preferred_element_type=F32) + d_glast
            dqs_ref[rows, :] = dqs + dqg * eg
            dkk_ref[rows, :] = dkk + dkd * ekd
            dp_ref[rows, 2 * A_WIDTH:3 * A_WIDTH] = (dv + dv_st).astype(BF16)
            return carry

        lax.fori_loop(0, n_sub, back, 0, unroll=2)
        sig, sq, qraw = blk["sig"], blk["sq"], blk["qraw"]
        d_f = dlf_ref[...] / blk["f"] - dkk_ref[...]
        dlb_ref[0:1, :] += jnp.sum(d_f * (1.0 - sig), axis=0, keepdims=True)
        dp_ref[:, 0:A_WIDTH] = (dqs_ref[...] * (sq * (1.0 + qraw * (1.0 - sq)))).astype(BF16)
        dp_ref[:, A_WIDTH:2 * A_WIDTH] = (d_f * (1.0 - lb) * sig * (1.0 - sig)).astype(BF16)

    def rev(s):
        return n_tb - 1 - s

    def col(k):
        return pl.BlockSpec((tb, A_WIDTH), lambda b, s, k=k: (b * n_tb + rev(s), k))

    acc8 = pl.BlockSpec((SUBLANES, A_WIDTH), lambda b, s: (0, 0))
    return pl.pallas_call(
        body, name=name, grid=(n_seq, n_tb),
        in_specs=[col(0), col(1), col(2), col(3), col(0), col(0),
                  pl.BlockSpec((1, n_sub, A_HEADS, HEAD_A, HEAD_A), lambda b, s: (b * n_tb + rev(s), 0, 0, 0, 0)),
                  pl.BlockSpec((3, A_WIDTH), lambda b, s: (0, 0)), pl.BlockSpec((1, A_WIDTH), lambda b, s: (0, 0))],
        out_specs=[pl.BlockSpec((tb, 4 * A_WIDTH), lambda b, s: (b * n_tb + rev(s), 0)), acc8, acc8],
        out_shape=[jax.ShapeDtypeStruct((t, EVEN_IN), BF16)] + [jax.ShapeDtypeStruct((SUBLANES, A_WIDTH), F32)] * 2,
        scratch_shapes=[pltpu.VMEM((A_HEADS, HEAD_A, HEAD_A), F32)] + [pltpu.VMEM((tb, A_WIDTH), F32)] * 10,
        compiler_params=_params(("arbitrary", "arbitrary")),
    )(proj, proj, proj, proj, dcat, pre, states, lb_table, a_norm)


GMLP_ROWS = 512


def _gmlp_chunk(ub, vb, ln_g, ln_b, ws, bias):
    u = [_gelu(a) for a in ub]
    v = [_gelu(a) for a in vb]
    mu = sum(jnp.sum(a, axis=-1, keepdims=True) for a in v) * (1.0 / B_WIDTH)
    cen = [a - mu for a in v]
    var = sum(jnp.sum(a * a, axis=-1, keepdims=True) for a in cen) * (1.0 / B_WIDTH)
    inv = lax.rsqrt(var + EPS)
    r = lax.broadcasted_iota(jnp.int32, (B_CHUNK, B_CHUNK), 0)
    c = lax.broadcasted_iota(jnp.int32, (B_CHUNK, B_CHUNK), 1)
    outs = []
    for g in range(B_GROUPS):
        vn = (cen[g] * inv * ln_g[g] + ln_b[g]).astype(BF16)
        wm = jnp.where(c <= r, ws[g], 0.0).astype(BF16)
        outs.append(u[g] * (_dot(wm, vn) + bias[g]))
    return outs


def _lane_groups(ref, rows=slice(None)):
    return [ref[rows, g * LANES:(g + 1) * LANES] for g in range(B_GROUPS)]


def _gmlp_fwd(proj, mixed, ln_g, ln_b, ws, bias_t, name):
    t = proj.shape[0]
    tm = GMLP_ROWS

    def body(u_ref, v_ref, lg_ref, lb_ref, ws_ref, bt_ref, _, o_ref):
        for ch in range(tm // B_CHUNK):
            rows = slice(ch * B_CHUNK, (ch + 1) * B_CHUNK)
            outs = _gmlp_chunk(_lane_groups(u_ref, rows), _lane_groups(v_ref, rows), _lane_groups(lg_ref),
                               _lane_groups(lb_ref), [ws_ref[g] for g in range(B_GROUPS)],
                               [bt_ref[:, g:g + 1] for g in range(B_GROUPS)])
            for g in range(B_GROUPS):
                o_ref[rows, g * LANES:(g + 1) * LANES] = outs[g].astype(BF16)

    vec = pl.BlockSpec((1, B_WIDTH), lambda i: (0, 0))
    return pl.pallas_call(
        body, name=name, grid=(t // tm,),
        in_specs=[pl.BlockSpec((tm, B_WIDTH), lambda i: (i, 4)), pl.BlockSpec((tm, B_WIDTH), lambda i: (i, 5)), vec, vec,
                  pl.BlockSpec((B_GROUPS, B_CHUNK, B_CHUNK), lambda i: (0, 0, 0)),
                  pl.BlockSpec((B_CHUNK, B_GROUPS), lambda i: (0, 0)), pl.BlockSpec(memory_space=pl.ANY)],
        out_specs=pl.BlockSpec((tm, B_WIDTH), lambda i: (i, 1)),
        out_shape=jax.ShapeDtypeStruct(mixed.shape, BF16),
        input_output_aliases={6: 0},
        compiler_params=_params(("parallel",)),
    )(proj, proj, ln_g, ln_b, ws, bias_t, mixed)


def _gmlp_bwd(proj, dcat, dproj, ln_g, ln_b, ws, bias_t, name):
    t = proj.shape[0]
    tm = GMLP_ROWS

    def body(u_ref, v_ref, do_ref, lg_ref, lb_ref, ws_ref, bt_ref, _, duv_ref, dlg_ref, dlb_ref, dws_ref, dbt_ref):
        @pl.when(pl.program_id(0) == 0)
        def _():
            dlg_ref[...] = jnp.zeros_like(dlg_ref)
            dlb_ref[...] = jnp.zeros_like(dlb_ref)
            dws_ref[...] = jnp.zeros_like(dws_ref)
            dbt_ref[...] = jnp.zeros_like(dbt_ref)

        for ch in range(tm // B_CHUNK):
            rows = slice(ch * B_CHUNK, (ch + 1) * B_CHUNK)
            _, vjp = jax.vjp(
                _gmlp_chunk, _lane_groups(u_ref, rows), _lane_groups(v_ref, rows), _lane_groups(lg_ref),
                _lane_groups(lb_ref), [ws_ref[g] for g in range(B_GROUPS)],
                [bt_ref[:, g:g + 1] for g in range(B_GROUPS)])
            du, dv, dlg, dlb, dw, dbt = vjp(_lane_groups(do_ref, rows))
            for g in range(B_GROUPS):
                lanes = slice(g * LANES, (g + 1) * LANES)
                duv_ref[rows, lanes] = du[g].astype(BF16)
                duv_ref[rows, B_WIDTH + g * LANES:B_WIDTH + (g + 1) * LANES] = dv[g].astype(BF16)
                dlg_ref[0:1, lanes] += dlg[g]
                dlb_ref[0:1, lanes] += dlb[g]
                dws_ref[g] += dw[g]
                dbt_ref[:, g:g + 1] += dbt[g]

    vec = pl.BlockSpec((1, B_WIDTH), lambda i: (0, 0))
    acc8 = pl.BlockSpec((SUBLANES, B_WIDTH), lambda i: (0, 0))
    ws_spec = pl.BlockSpec((B_GROUPS, B_CHUNK, B_CHUNK), lambda i: (0, 0, 0))
    bt_spec = pl.BlockSpec((B_CHUNK, B_GROUPS), lambda i: (0, 0))
    return pl.pallas_call(
        body, name=name, grid=(t // tm,),
        in_specs=[pl.BlockSpec((tm, B_WIDTH), lambda i: (i, 4)), pl.BlockSpec((tm, B_WIDTH), lambda i: (i, 5)),
                  pl.BlockSpec((tm, B_WIDTH), lambda i: (i, 1)), vec, vec, ws_spec, bt_spec,
                  pl.BlockSpec(memory_space=pl.ANY)],
        out_specs=[pl.BlockSpec((tm, 2 * B_WIDTH), lambda i: (i, 2)), acc8, acc8, ws_spec, bt_spec],
        out_shape=[jax.ShapeDtypeStruct(dproj.shape, BF16), jax.ShapeDtypeStruct((SUBLANES, B_WIDTH), F32),
                   jax.ShapeDtypeStruct((SUBLANES, B_WIDTH), F32),
                   jax.ShapeDtypeStruct((B_GROUPS, B_CHUNK, B_CHUNK), F32),
                   jax.ShapeDtypeStruct((B_CHUNK, B_GROUPS), F32)],
        input_output_aliases={7: 0},
        compiler_params=_params(("arbitrary",)),
    )(proj, proj, dcat, ln_g, ln_b, ws, bias_t, dproj)


QK_SCALE = 1.0 / math.sqrt(C_HEAD_DIM)
ATTN_UNROLL = 16
ATTN_PAIRS = 2
LANE_GROUPS = D_MODEL // LANES
ATTN_STEPS = LANE_GROUPS // ATTN_PAIRS
Q_BLOCKS = SEQ // C_BLOCK


def _attn_window(i, d):
    sub_blocks = Q_BLOCKS // d
    q0 = pl.multiple_of(i * C_BLOCK, C_BLOCK)
    k0 = pl.multiple_of(jnp.maximum(i - 1, 0) * C_BLOCK, C_BLOCK)
    key = k0 + lax.broadcasted_iota(jnp.int32, (C_BLOCK, 2 * C_BLOCK), 1)
    dist = (q0 + lax.broadcasted_iota(jnp.int32, (C_BLOCK, 2 * C_BLOCK), 0)) - key
    own_subsequence = (key >= q0) | (i % sub_blocks > 0)
    return pl.ds(q0, C_BLOCK), pl.ds(k0, 2 * C_BLOCK), (dist >= 0) & (dist <= C_BLOCK) & own_subsequence


def _head_masks():
    lane = lax.broadcasted_iota(jnp.int32, (C_BLOCK, LANES), 1)
    return [lane < C_HEAD_DIM, lane >= C_HEAD_DIM]


def _flat_spec(col_of):
    return pl.BlockSpec((1, SEQ, ATTN_PAIRS * LANES), lambda b, g: (b, 0, col_of(g)))


def _put_heads(tile, g, col0, col1):
    lane = lax.broadcasted_iota(jnp.int32, tile.shape, 1)
    return jnp.where(lane == 2 * g, col0, jnp.where(lane == 2 * g + 1, col1, tile))


def _get_head(tile, h):
    lane = lax.broadcasted_iota(jnp.int32, tile.shape, 1)
    return jnp.sum(jnp.where(lane == h, tile, 0.0), axis=1, keepdims=True)


PER_HEAD_SPEC = pl.BlockSpec((1, SEQ, LANES), lambda b, g: (b, 0, 0))


def _attn_branch_fwd(qkv, name):
    n_seq, d, l, _ = qkv.shape
    flat = qkv.reshape(n_seq, SEQ, ODD_IN)

    def body(q_ref, k_ref, v_ref, o_ref, m_ref, l_ref):
        heads = _head_masks()
        g = pl.program_id(1)

        @pl.when(g == 0)
        def _():
            m_ref[...] = jnp.zeros_like(m_ref)
            l_ref[...] = jnp.zeros_like(l_ref)

        def block(i, carry):
            rows, keys, mask = _attn_window(i, d)
            m_tile, l_tile = m_ref[0, rows, :], l_ref[0, rows, :]
            for pair in range(ATTN_PAIRS):
                lanes = slice(pair * LANES, (pair + 1) * LANES)
                q, k, v = q_ref[0, rows, lanes], k_ref[0, keys, lanes], v_ref[0, keys, lanes]
                res = []
                for hm in heads:
                    s = jnp.where(mask, _dot_nt(jnp.where(hm, q, 0), k), NEG)
                    m = jnp.max(s, axis=-1, keepdims=True)
                    p = jnp.exp(s - m)
                    res.append((_dot(p.astype(BF16), v), m, jnp.sum(p, axis=-1, keepdims=True)))
                o_ref[0, rows, lanes] = jnp.where(heads[0], res[0][0], res[1][0])
                m_tile = _put_heads(m_tile, g * ATTN_PAIRS + pair, res[0][1], res[1][1])
                l_tile = _put_heads(l_tile, g * ATTN_PAIRS + pair, res[0][2], res[1][2])
            m_ref[0, rows, :] = m_tile
            l_ref[0, rows, :] = l_tile
            return carry

        lax.fori_loop(0, Q_BLOCKS, block, 0, unroll=ATTN_UNROLL)

    o, m, l_sum = pl.pallas_call(
        body, name=name, grid=(n_seq, ATTN_STEPS),
        in_specs=[_flat_spec(lambda g: g), _flat_spec(lambda g: ATTN_STEPS + g),
                  _flat_spec(lambda g: 2 * ATTN_STEPS + g)],
        out_specs=[_flat_spec(lambda g: g), PER_HEAD_SPEC, PER_HEAD_SPEC],
        out_shape=[jax.ShapeDtypeStruct((n_seq, SEQ, D_MODEL), F32)] + [jax.ShapeDtypeStruct((n_seq, SEQ, LANES), F32)] * 2,
        compiler_params=_params(("parallel", "arbitrary")),
    )(flat, flat, flat)
    return [o.reshape(n_seq, d, l, D_MODEL), m.reshape(n_seq, d, l, LANES), l_sum.reshape(n_seq, d, l, LANES)]


def _attn_merge(branches, name):
    n_seq = branches[0][0].shape[0]
    t = n_seq * SEQ
    tm = MERGE_TILE

    def body(*refs):
        ins = refs[:9]
        o_ref, ob_ref, lse_ref = refs[9:12]
        nat = refs[12:]
        for b, d in enumerate(C_DILATIONS[1:]):
            for k in range(3):
                _load_dilated(ins[3 + 3 * b + k], d, nat[3 * b + k])
        ms = [ins[1][0, 0], nat[1][0], nat[4][0]]
        ls = [ins[2][0, 0], nat[2][0], nat[5][0]]
        m_all = jnp.maximum(jnp.maximum(ms[0], ms[1]), ms[2])
        ws = [jnp.exp(ms[b] - m_all) for b in range(3)]
        lane = lax.broadcasted_iota(jnp.int32, m_all.shape, 1)
        total = jnp.where(lane < C_HEADS, ws[0] * ls[0] + ws[1] * ls[1] + ws[2] * ls[2], 1.0)
        lse_ref[...] = m_all + jnp.log(total)
        first_head = lane < C_HEAD_DIM
        for p in range(LANE_GROUPS):
            lanes = slice(p * LANES, (p + 1) * LANES)
            spread = lambda c: jnp.where(first_head, c[:, 2 * p:2 * p + 1], c[:, 2 * p + 1:2 * p + 2])
            os_ = [ins[0][0, 0, :, lanes], nat[0][p], nat[3][p]]
            o = (spread(ws[0]) * os_[0] + spread(ws[1]) * os_[1] + spread(ws[2]) * os_[2]) / spread(total)
            o_ref[:, lanes] = o
            ob_ref[:, lanes] = o.astype(BF16)

    row = pl.BlockSpec((tm, D_MODEL), lambda i: (i, 0))
    flat = [a for br in branches for a in br]
    in_specs = []
    for wide, narrow in zip(_dilated_specs(tm, D_MODEL, lambda: 0), _dilated_specs(tm, LANES, lambda: 0)):
        in_specs += [wide, narrow, narrow]
    per_head = pltpu.VMEM((1, tm, LANES), F32)
    return pl.pallas_call(
        body, name=name, grid=(t // tm,), in_specs=in_specs,
        out_specs=[row, row, pl.BlockSpec((tm, LANES), lambda i: (i, 0))],
        out_shape=[jax.ShapeDtypeStruct((t, D_MODEL), F32), jax.ShapeDtypeStruct((t, D_MODEL), BF16),
                   jax.ShapeDtypeStruct((t, LANES), F32)],
        scratch_shapes=[pltpu.VMEM((LANE_GROUPS, tm, LANES), F32), per_head, per_head] * 2,
        compiler_params=_params(("parallel",)),
    )(*flat)


def _attn_branch_bwd(qkv, dout, lse, delta, name):
    n_seq, d, l, _ = qkv.shape
    flat = lambda a: a.reshape(n_seq, SEQ, a.shape[-1])

    def body(q_ref, k_ref, v_ref, do_ref, lse_nat_ref, dl_nat_ref, dq_ref, dk_ref, dv_ref, lse_ref, dl_ref,
             dkt_ref, dvt_ref):
        heads = _head_masks()
        g = pl.program_id(1)
        dkt_ref[...] = jnp.zeros_like(dkt_ref)
        dvt_ref[...] = jnp.zeros_like(dvt_ref)
        for nat_ref, dst_ref in ((lse_nat_ref, lse_ref), (dl_nat_ref, dl_ref)):
            for r in range(d):
                rows = pl.ds(r, l, stride=d) if d > 1 else slice(None)
                dst_ref[r * l:(r + 1) * l, :] = nat_ref.at[0][rows, :]

        def block(i, carry):
            rows, keys, mask = _attn_window(i, d)
            lse_b, dl_b = lse_ref[rows, :], dl_ref[rows, :]
            for pair in range(ATTN_PAIRS):
                lanes = slice(pair * LANES, (pair + 1) * LANES)
                q, do = q_ref[0, rows, lanes], do_ref[0, rows, lanes]
                k, v = k_ref[0, keys, lanes], v_ref[0, keys, lanes]
                dq, dk, dv = [], None, None
                for hh, hm in enumerate(heads):
                    head = 2 * (g * ATTN_PAIRS + pair) + hh
                    qh, doh = jnp.where(hm, q, 0), jnp.where(hm, do, 0)
                    s = jnp.where(mask, _dot_nt(qh, k), NEG)
                    p = jnp.exp(s - _get_head(lse_b, head))
                    ds = (p * (_dot_nt(doh, v) - _get_head(dl_b, head))).astype(BF16)
                    dq.append(_dot(ds, k) * QK_SCALE)
                    dk_h, dv_h = _dot_tn(qh, ds), _dot_tn(doh, p.astype(BF16))
                    dk = dk_h if dk is None else dk + dk_h
                    dv = dv_h if dv is None else dv + dv_h
                dq_ref[0, rows, lanes] = jnp.where(heads[0], dq[0], dq[1]).astype(BF16)
                dkt_ref[lanes, keys] += dk
                dvt_ref[lanes, keys] += dv
            return carry

        lax.fori_loop(0, Q_BLOCKS, block, 0, unroll=ATTN_UNROLL)
        for c in range(SEQ // ROW_TILE):
            rows = slice(c * ROW_TILE, (c + 1) * ROW_TILE)
            dk_ref[0, rows, :] = dkt_ref[:, rows].T.astype(BF16)
            dv_ref[0, rows, :] = dvt_ref[:, rows].T.astype(BF16)

    act = _flat_spec(lambda g: g)
    outs = pl.pallas_call(
        body, name=name, grid=(n_seq, ATTN_STEPS),
        in_specs=[_flat_spec(lambda g: g), _flat_spec(lambda g: ATTN_STEPS + g),
                  _flat_spec(lambda g: 2 * ATTN_STEPS + g), act, PER_HEAD_SPEC, PER_HEAD_SPEC],
        out_specs=[act] * 3,
        out_shape=[jax.ShapeDtypeStruct((n_seq, SEQ, D_MODEL), BF16)] * 3,
        scratch_shapes=[pltpu.VMEM((SEQ, LANES), F32)] * 2 + [pltpu.VMEM((ATTN_PAIRS * LANES, SEQ), F32)] * 2,
        compiler_params=_params(("parallel", "parallel")),
    )(flat(qkv), flat(qkv), flat(qkv), flat(dout), lse, delta)
    return [o.reshape(n_seq, d, l, D_MODEL) for o in outs]


def _attn_combine_bwd(grads, rope, name):
    n_seq = grads[0][0].shape[0]
    t = n_seq * SEQ
    tm = MERGE_TILE

    def body(*refs):
        c_ref, s_ref, o_ref, nat4_ref, nat16_ref = refs[9:]
        for sec in range(3):
            _load_dilated(refs[3 + sec], 4, nat4_ref)
            _load_dilated(refs[6 + sec], 16, nat16_ref)
            for p in range(LANE_GROUPS):
                blk = refs[sec][0, 0, :, p * LANES:(p + 1) * LANES] + nat4_ref[p] + nat16_ref[p]
                if sec < 2:
                    blk = blk * c_ref[...] - _swap_halves(blk) * s_ref[...]
                o_ref[:, sec * D_MODEL + p * LANES:sec * D_MODEL + (p + 1) * LANES] = blk.astype(BF16)

    tab = pl.BlockSpec((tm, LANES), lambda i: (i, 0))
    flat = [a for br in grads for a in br]
    in_specs = []
    for spec in _dilated_specs(tm, D_MODEL, lambda: 0):
        in_specs += [spec] * 3
    return pl.pallas_call(
        body, name=name, grid=(t // tm,), in_specs=in_specs + [tab, tab],
        out_specs=pl.BlockSpec((tm, ODD_IN), lambda i: (i, 0)),
        out_shape=jax.ShapeDtypeStruct((t, ODD_IN), BF16),
        scratch_shapes=[pltpu.VMEM((LANE_GROUPS, tm, LANES), F32)] * 2,
        compiler_params=_params(("parallel",)),
    )(*flat, *rope)


def _adamw(w, g, m, v):
    m = ADAM_B1 * m + (1.0 - ADAM_B1) * g
    v = ADAM_B2 * v + (1.0 - ADAM_B2) * jnp.square(g)
    m_hat = m / (1.0 - ADAM_B1 ** ADAM_STEP)
    v_hat = v / (1.0 - ADAM_B2 ** ADAM_STEP)
    delta = -ADAM_LR * (m_hat / (jnp.sqrt(v_hat) + ADAM_EPS) + ADAM_WD * w)
    return delta, m, v


def _adamw_sharded(parts, w, m, v, after, name):
    n_layers, rows, cols = w.shape
    tr = min(rows, 256)

    def body(*refs):
        p_refs = refs[:n_layers]
        w_ref, m_ref, v_ref, _, g_ref, d_ref, mo_ref, vo_ref = refs[n_layers:]
        layer = pl.program_id(0)
        g = None
        for l, p_ref in enumerate(p_refs):
            g_l = p_ref[0].astype(F32)
            for s in range(1, N_DEV):
                g_l = g_l + p_ref[s].astype(F32)
            g = g_l if g is None else jnp.where(layer == l, g_l, g)
        delta, mn, vn = _adamw(w_ref[0], g, m_ref[0], v_ref[0])
        g_ref[0] = g
        d_ref[0] = delta
        mo_ref[0] = mn
        vo_ref[0] = vn

    def part_spec(l):
        return pl.BlockSpec((N_DEV, tr, cols), lambda a, i: (0, jnp.where(a == l, i, 0), 0))

    row = pl.BlockSpec((1, tr, cols), lambda a, i: (a, i, 0))
    return pl.pallas_call(
        body, name=name, grid=(n_layers, rows // tr),
        in_specs=[part_spec(l) for l in range(n_layers)] + [row, row, row, pl.BlockSpec(memory_space=pl.ANY)],
        out_specs=[row] * 4, out_shape=[jax.ShapeDtypeStruct(w.shape, F32)] * 4,
        compiler_params=_params(("arbitrary", "arbitrary")),
    )(*parts, w, m, v, after)


def _small_update(gathered, where, weights, moments_m, moments_v, lb_index, name):
    n = len(weights)
    n_g = len(gathered)

    def body(*refs):
        g_refs = refs[:n_g]
        w_refs, m_refs, v_refs = refs[n_g:n_g + n], refs[n_g + n:n_g + 2 * n], refs[n_g + 2 * n:n_g + 3 * n]
        outs = refs[n_g + 3 * n:]

        def total(k):
            array, rows, lanes = where[k]
            ref = g_refs[array]
            index = (slice(None),) * (len(ref.shape) - 1) if rows is None else (rows, lanes)
            acc = ref[(0,) + index]
            for s in range(1, N_DEV):
                acc = acc + ref[(s,) + index]
            return acc

        loss_rows = total(n)
        outs[0][...] = jnp.sum(jnp.sum(loss_rows, axis=1, keepdims=True), axis=0, keepdims=True)
        for k in range(n):
            part = total(k)
            if k == lb_index:
                dlb = jnp.sum(part, axis=0, keepdims=True)
                tab = w_refs[k][...]
                e = jnp.exp(tab - jnp.max(tab, axis=0, keepdims=True))
                p = e / jnp.sum(e, axis=0, keepdims=True)
                first = lax.broadcasted_iota(jnp.int32, p.shape, 0) == 0
                grads = [(slice(None), p * (jnp.where(first, dlb, 0.0) - p[0:1, :] * dlb))]
            elif part.shape == w_refs[k].shape:
                grads = [(slice(None), part)]
            else:
                grads = [(slice(l, l + 1), jnp.sum(part[l * SUBLANES:(l + 1) * SUBLANES], axis=0, keepdims=True))
                         for l in range(w_refs[k].shape[0])]
            for rows, g in grads:
                delta, mn, vn = _adamw(w_refs[k][rows], g, m_refs[k][rows], v_refs[k][rows])
                outs[1 + 4 * k][rows] = g
                outs[2 + 4 * k][rows] = delta
                outs[3 + 4 * k][rows] = mn
                outs[4 + 4 * k][rows] = vn

    vmem = pl.BlockSpec(memory_space=pltpu.VMEM)
    out_shape = [jax.ShapeDtypeStruct((1, 1), F32)]
    for w in weights:
        out_shape += [jax.ShapeDtypeStruct(w.shape, F32)] * 4
    args = list(gathered) + list(weights) + list(moments_m) + list(moments_v)
    return pl.pallas_call(
        body, name=name, in_specs=[vmem] * len(args), out_specs=[vmem] * len(out_shape), out_shape=out_shape,
        compiler_params=pltpu.CompilerParams(vmem_limit_bytes=VMEM_LIMIT),
    )(*args)


def kernel(x, positions, norm_mix_pre, norm_mix_post, norm_ffn_pre, norm_ffn_post, w_in_even, lb_table, a_norm, b_ln_g, b_ln_b, b_ws, b_bias, w_out_even, w_in_odd, w_out_odd, w_ff1, w_ff2, loss_target, m_norm_mix_pre, m_norm_mix_post, m_norm_ffn_pre, m_norm_ffn_post, m_w_in_even, m_lb_table, m_a_norm, m_b_ln_g, m_b_ln_b, m_b_ws, m_b_bias, m_w_out_even, m_w_in_odd, m_w_out_odd, m_w_ff1, m_w_ff2, v_norm_mix_pre, v_norm_mix_post, v_norm_ffn_pre, v_norm_ffn_post, v_w_in_even, v_lb_table, v_a_norm, v_b_ln_g, v_b_ln_b, v_b_ws, v_b_bias, v_w_out_even, v_w_in_odd, v_w_out_odd, v_w_ff1, v_w_ff2):
    n_seq = x.shape[0]
    t = n_seq * SEQ
    x0 = x.reshape(t, D_MODEL)
    target = loss_target.reshape(t, D_MODEL)

    me = _my_slot().astype(jnp.int32).reshape(1)

    order = ["in_e", "out_e", "ff1_0", "ff2_0", "in_o", "out_o", "ff1_1", "ff2_1"]
    shards = dict(in_e=w_in_even[0], out_e=w_out_even[0], in_o=w_in_odd[0], out_o=w_out_odd[0],
                  ff1_0=w_ff1[0], ff1_1=w_ff1[1], ff2_0=w_ff2[0], ff2_1=w_ff2[1])
    by_columns = ("in_e", "in_o", "ff1_0", "ff1_1")

    def place(k, after):
        if k in by_columns:
            return _place_own_columns(shards[k], me, "place_" + k, after)
        return _place_own(shards[k], me, "place_" + k, False, after=after)

    gathers = {}
    send0, recv0, land0, _, token0 = _exchange_start([place(order[0], None)], [None], "gather_start_first")
    gathers[order[0]] = (land0[0], send0[0], recv0[0])
    sends, recvs, lands, _, g_token = _exchange_start([place(k, token0) for k in order[1:]],
                                                      [None] * (len(order) - 1), "gather_start")
    for k, land, send, recv in zip(order[1:], lands, sends, recvs):
        gathers[k] = (land, send, recv)

    def get_w(keys, after):
        lands_k, sends_k, recvs_k = zip(*[gathers[k] for k in keys])
        return _exchange_wait(list(lands_k), [None] * len(keys), list(sends_k), list(recvs_k), after,
                              "gather_wait_" + keys[0])

    sent = {}

    def put_g(group, blocks):
        keys = list(blocks)
        own = [_place_own(blocks[k], me, "own_" + k, True) for k in keys]
        send_sems, recv_sems, own, srcs, token = _exchange_start(own, [blocks[k] for k in keys], "scatter_start_" + group)
        sent[group] = (keys, own, srcs, send_sems, recv_sems)
        return token

    rope = _rope_tables(positions)
    bias_t = b_bias[0].T
    grads = _local_step(x0, target, rope, norm_mix_pre, norm_mix_post, norm_ffn_pre, norm_ffn_post, lb_table,
                        a_norm, b_ln_g, b_ln_b, b_ws[0], bias_t, get_w, put_g, g_token)
    (dx0, loss_part, dg_mix_pre, dg_mix_post, dg_ffn_pre, dg_ffn_post, d_lb, d_a_norm, d_ln_g, d_ln_b, d_ws,
     d_bias_t) = grads

    packed = jnp.concatenate([dg_mix_pre, dg_mix_post, dg_ffn_pre, dg_ffn_post,
                              jnp.concatenate([d_lb, d_a_norm], axis=1), jnp.concatenate([d_ln_g, d_ln_b], axis=1),
                              loss_part], axis=0)
    small_lands = [_place_own(a, me, "own_small%d" % k, False, F32) for k, a in enumerate((packed, d_ws, d_bias_t))]
    s_send, s_recv, small_lands, _, after = _exchange_start(small_lands, [None] * 3, "gather_small_start")

    big = dict(w_in_even=(["in_e"], w_in_even, m_w_in_even, v_w_in_even),
               w_out_even=(["out_e"], w_out_even, m_w_out_even, v_w_out_even),
               w_in_odd=(["in_o"], w_in_odd, m_w_in_odd, v_w_in_odd),
               w_out_odd=(["out_o"], w_out_odd, m_w_out_odd, v_w_out_odd),
               w_ff1=(["ff1_0", "ff1_1"], w_ff1, m_w_ff1, v_w_ff1), w_ff2=(["ff2_0", "ff2_1"], w_ff2, m_w_ff2, v_w_ff2))
    recv, big_out = {}, {}
    for groups, names in ((("ffn1", "ffn0"), ("w_ff1", "w_ff2")), (("mix1",), ("w_in_odd", "w_out_odd")),
                          (("mix0",), ("w_in_even", "w_out_even"))):
        for group in groups:
            keys, own, srcs, send_sems, recv_sems = sent[group]
            recv.update(zip(keys, _exchange_wait(own, srcs, send_sems, recv_sems, after, "scatter_wait_" + group)))
        for nm in names:
            keys, w, m, v = big[nm]
            big_out[nm] = _adamw_sharded([recv[k] for k in keys], w, m, v, after, "adamw_" + nm)
            after = big_out[nm][0]
    big_out = [big_out[nm] for nm in ("w_in_even", "w_out_even", "w_in_odd", "w_out_odd", "w_ff1", "w_ff2")]
    gathered = _exchange_wait(small_lands, [None] * 3, s_send, s_recv, after, "gather_small_wait")
    rows8 = lambda k: slice(SUBLANES * k, SUBLANES * (k + 1))
    left, right, every = slice(0, A_WIDTH), slice(A_WIDTH, 2 * A_WIDTH), slice(None)
    where = [(0, slice(0, 16), every), (0, slice(16, 32), every), (0, slice(32, 48), every), (0, slice(48, 64), every),
             (0, rows8(8), left), (0, rows8(8), right), (0, rows8(9), left), (0, rows8(9), right),
             (1, None, None), (2, None, None), (0, rows8(10), every)]
    small_w = [norm_mix_pre, norm_mix_post, norm_ffn_pre, norm_ffn_post, lb_table, a_norm, b_ln_g, b_ln_b,
               b_ws[0], bias_t]
    small_m = [m_norm_mix_pre, m_norm_mix_post, m_norm_ffn_pre, m_norm_ffn_post, m_lb_table, m_a_norm, m_b_ln_g,
               m_b_ln_b, m_b_ws[0], m_b_bias[0].T]
    small_v = [v_norm_mix_pre, v_norm_mix_post, v_norm_ffn_pre, v_norm_ffn_post, v_lb_table, v_a_norm, v_b_ln_g,
               v_b_ln_b, v_b_ws[0], v_b_bias[0].T]
    small_out = _small_update(gathered, where, small_w, small_m, small_v, 4, "small_update")
    loss = small_out[0].reshape(())
    small = [small_out[1 + 4 * k:5 + 4 * k] for k in range(len(small_w))]
    small[8] = [a[None] for a in small[8]]
    small[9] = [a.T[None] for a in small[9]]

    per_weight = small[0:4] + [big_out[0]] + small[4:10] + big_out[1:6]
    grad_x = dx0.reshape(x.shape)
    out = [loss, grad_x]
    for kind in range(4):
        out += [p[kind] for p in per_weight]
    return tuple(out)


def _local_step(x0, target, rope, norm_mix_pre, norm_mix_post, norm_ffn_pre, norm_ffn_post, lb_table, a_norm,
                b_ln_g, b_ln_b, ws, bias_t, get_w, put_g, token):
    def gain(a, l, tok):
        return a[l:l + 1] if tok is None else a[l:l + 1] + tok[0:1, 0:1]

    full = lambda a: a.reshape(-1, D_MODEL)
    owners = lambda a: a.reshape((N_DEV, -1) + a.shape[1:])

    (g_in_e,) = get_w(["in_e"], token)
    proj, h_mix0 = _norm_inproj(x0, gain(norm_mix_pre, 0, token), g_in_e, "inproj_even")
    mixed, pre_a, states = _hgrn2_fwd(proj, lb_table, a_norm, "hgrn2_fwd")
    mixed = _gmlp_fwd(proj, mixed, b_ln_g, b_ln_b, ws, bias_t, "gmlp_fwd")
    w_out_e = full(get_w(["out_e"], mixed)[0])
    x1, mix0 = _outproj([mixed], w_out_e, x0, gain(norm_mix_post, 0, None), "outproj_even")
    w1_0, w2_0 = get_w(["ff1_0", "ff2_0"], x1)
    w2_0 = full(w2_0)
    x2, y0, h_ffn0, r0 = _ffn_fwd(x1, gain(norm_ffn_pre, 0, None), w1_0, w2_0, gain(norm_ffn_post, 0, None), "ffn_fwd_0")
    (g_in_o,) = get_w(["in_o"], x2)
    *qkv, h_mix1 = _norm_inproj_rope(x2, gain(norm_mix_pre, 1, None), g_in_o, rope, "inproj_odd")
    branches = [_attn_branch_fwd(a, "attn_fwd_d%d" % d) for a, d in zip(qkv, C_DILATIONS)]
    attn, attn_b, lse = _attn_merge(branches, "attn_merge")
    w_out_o = full(get_w(["out_o"], attn_b)[0])
    x3, mix1 = _outproj([attn_b], w_out_o, x2, gain(norm_mix_post, 1, None), "outproj_odd")
    w1_1, w2_1 = get_w(["ff1_1", "ff2_1"], x3)
    w2_1 = full(w2_1)
    dx4, y1, h_ffn1, r1, loss_part = _ffn_fwd(x3, gain(norm_ffn_pre, 1, None), w1_1, w2_1, gain(norm_ffn_post, 1, None),
                                              "ffn_fwd_1", target)

    dx3, dy1, da1, dg_ffn_pre1, dg_ffn_post1 = _ffn_bwd(
        dx4, x3, y1, r1, gain(norm_ffn_pre, 1, None), w1_1, w2_1, gain(norm_ffn_post, 1, None), "ffn_bwd_1")
    gw_ff1_1 = _grad_w(h_ffn1, da1, True, "grad_w_ff1_1")
    gw_ff2_1 = _grad_w(r1, dy1, False, "grad_w_ff2_1")
    tok = put_g("ffn1", dict(ff1_1=gw_ff1_1, ff2_1=owners(gw_ff2_1)))
    *dattn, delta, dz1, dg_mix_post1 = _outproj_bwd_attn(dx3, mix1, gain(norm_mix_post, 1, tok), w_out_o, attn,
                                                  "outproj_bwd_odd")
    gw_out_o = _grad_w(attn_b, dz1, False, "grad_w_out_odd")
    per_seq = lambda a: a.reshape(-1, SEQ, LANES)
    grads_c = [_attn_branch_bwd(qkv[b], dattn[b], per_seq(lse), per_seq(delta), "attn_bwd_d%d" % d)
               for b, d in enumerate(C_DILATIONS)]
    dqkv = _attn_combine_bwd(grads_c, rope, "attn_combine_bwd")
    gw_in_o = _grad_w(h_mix1, dqkv, True, "grad_w_in_odd")
    tok = put_g("mix1", dict(out_o=owners(gw_out_o), in_o=gw_in_o))
    dx2, dg_mix_pre1 = _inproj_bwd(dqkv, g_in_o, dx3, x2, gain(norm_mix_pre, 1, tok), "inproj_bwd_odd")

    dx1, dy0, da0, dg_ffn_pre0, dg_ffn_post0 = _ffn_bwd(
        dx2, x1, y0, r0, gain(norm_ffn_pre, 0, None), w1_0, w2_0, gain(norm_ffn_post, 0, None), "ffn_bwd_0")
    gw_ff1_0 = _grad_w(h_ffn0, da0, True, "grad_w_ff1_0")
    gw_ff2_0 = _grad_w(r0, dy0, False, "grad_w_ff2_0")
    tok = put_g("ffn0", dict(ff1_0=gw_ff1_0, ff2_0=owners(gw_ff2_0)))
    dcat, dz0, dg_mix_post0 = _outproj_bwd(dx1, mix0, gain(norm_mix_post, 0, tok), w_out_e, "outproj_bwd_even")
    gw_out_e = _grad_w(mixed, dz0, False, "grad_w_out_even")
    dproj, d_lb, d_a_norm = _hgrn2_bwd(proj, dcat, pre_a, states, lb_table, a_norm, "hgrn2_bwd")
    dproj, d_ln_g, d_ln_b, d_ws, d_bias_t = _gmlp_bwd(proj, dcat, dproj, b_ln_g, b_ln_b, ws, bias_t, "gmlp_bwd")
    gw_in_e = _grad_w(h_mix0, dproj, True, "grad_w_in_even")
    tok = put_g("mix0", dict(out_e=owners(gw_out_e), in_e=gw_in_e))
    dx0, dg_mix_pre0 = _inproj_bwd(dproj, g_in_e, dx1, x0, gain(norm_mix_pre, 0, tok), "inproj_bwd_even")

    layers = lambda a, b: jnp.concatenate([a, b], axis=0)
    return (dx0, loss_part, layers(dg_mix_pre0, dg_mix_pre1), layers(dg_mix_post0, dg_mix_post1),
            layers(dg_ffn_pre0, dg_ffn_pre1), layers(dg_ffn_post0, dg_ffn_post1),
            d_lb, d_a_norm, d_ln_g, d_ln_b, d_ws, d_bias_t)
```

```python
import math

import jax
import jax.numpy as jnp
from jax import lax
from jax.experimental import pallas as pl
from jax.experimental.pallas import tpu as pltpu

F32 = jnp.float32
BF16 = jnp.bfloat16
MESH = pl.DeviceIdType.MESH

N_DEV = 8
D_MODEL = 1024
SEQ = 2048
EPS = 1e-6
A_WIDTH = 512
A_HEADS = 4
HEAD_A = 128
B_WIDTH = 512
B_GROUPS = 4
B_CHUNK = 128
C_HEADS = 16
C_HEAD_DIM = 64
C_ROT_HALF = 8
ROPE_THETA = 500000.0
C_DILATIONS = (1, 4, 16)
C_BLOCK = 128
D_FF = 4096
EVEN_IN = 3072
ODD_IN = 3072

ADAM_LR = 0.001
ADAM_B1 = 0.9
ADAM_B2 = 0.999
ADAM_EPS = 1e-08
ADAM_WD = 0.01
ADAM_STEP = 10

LANES = 128
SUBLANES = 8
ROW_TILE = 512
PROJ_TILE = 1024
PROJ_COLS = 768
MERGE_TILE = 256
SUB_CHUNK = 16
HGRN_BLOCK = 256
NEG = -1e30
VMEM_LIMIT = 56 * 1024 * 1024


def _params(sem):
    return pltpu.CompilerParams(dimension_semantics=sem, vmem_limit_bytes=VMEM_LIMIT)


def _dot(a, b):
    return jnp.dot(a, b, preferred_element_type=F32)


def _dot_nt(a, b):
    return lax.dot_general(a, b, (((1,), (1,)), ((), ())), preferred_element_type=F32)


def _dot_tn(a, b):
    return lax.dot_general(a, b, (((0,), (0,)), ((), ())), preferred_element_type=F32)


def _rms(x, g):
    r = lax.rsqrt(jnp.mean(x * x, axis=-1, keepdims=True) + EPS)
    return x * r * g


def _rms_bwd(x, g, dy):
    r = lax.rsqrt(jnp.mean(x * x, axis=-1, keepdims=True) + EPS)
    dyg = dy * g
    dx = r * dyg - x * (r * r * r) * jnp.mean(x * dyg, axis=-1, keepdims=True)
    return dx, dy * x * r


def _split3(x):
    hi = x.astype(BF16)
    rest = x - hi.astype(F32)
    mid = rest.astype(BF16)
    return hi, mid, (rest - mid.astype(F32)).astype(BF16)


def _mask_dot(mask, x):
    m = mask.astype(BF16)
    hi, mid, lo = _split3(x)
    return _dot(m, hi) + (_dot(m, mid) + _dot(m, lo))


def _dot_mask(x, mask):
    m = mask.astype(BF16)
    hi, mid, lo = _split3(x)
    return _dot(hi, m) + (_dot(mid, m) + _dot(lo, m))


def _rows8(v):
    return v.reshape(v.shape[0] // SUBLANES, SUBLANES, v.shape[1]).sum(axis=0)


def _sigmoid(x):
    return 1.0 / (1.0 + jnp.exp(-x))


def _gelu(x):
    return 0.5 * x * (1.0 + jnp.tanh(math.sqrt(2.0 / math.pi) * (x + 0.044715 * (x * x * x))))


def _acc_rows8(ref, val, first):
    @pl.when(first)
    def _():
        ref[...] = val

    @pl.when(jnp.logical_not(first))
    def _():
        ref[...] += val


def _my_slot():
    return 4 * lax.axis_index("x") + 2 * lax.axis_index("y") + lax.axis_index("c")


def _peer(r):
    x, y, c = lax.axis_index("x"), lax.axis_index("y"), lax.axis_index("c")
    px = 1 - x if (r >> 2) & 1 else x
    py = 1 - y if (r >> 1) & 1 else y
    pc = 1 - c if r & 1 else c
    return (px, py, pc), 4 * px + 2 * py + pc


HBM_SPEC = pl.BlockSpec(memory_space=pltpu.HBM)
SEM_SPEC = pl.BlockSpec(memory_space=pltpu.SEMAPHORE)
SPLIT_EFFECT = pltpu.SideEffectType.DATAFLOW_SIDE_EFFECTING


def _split_copies(land_ref, src_ref, send_sem, recv_sem):
    me = _my_slot()
    copies = []
    for r in range(1, N_DEV):
        peer, slot = _peer(r)
        src = _slot(land_ref, me) if src_ref is None else _slot(src_ref, slot)
        copies.append(pltpu.make_async_remote_copy(
            src_ref=src, dst_ref=_slot(land_ref, me), send_sem=send_sem, recv_sem=recv_sem,
            device_id=peer, device_id_type=MESH))
    return copies


def _slot(ref, s):
    if len(ref.shape) == 2:
        c = ref.shape[1] // N_DEV
        return ref.at[:, pl.ds(pl.multiple_of(s * c, LANES), c)]
    return ref.at[s]


def _exchange_start(lands, sources, name):
    n = len(lands)
    given = [s for s in sources if s is not None]
    arrays = list(lands) + given

    def body(*refs):
        land_refs, src_refs = refs[:n], list(refs[n:n + len(given)])
        sems = refs[len(arrays):len(arrays) + 2 * n]
        token = refs[-1]
        for k in range(n):
            src_ref = None if sources[k] is None else src_refs.pop(0)
            for copy in _split_copies(land_refs[k], src_ref, sems[k], sems[n + k]):
                copy.start()
        token[...] = jnp.zeros_like(token)

    outs = pl.pallas_call(
        body, name=name,
        out_shape=(pltpu.SemaphoreType.DMA(()),) * (2 * n) + tuple(pltpu.HBM(a.shape, a.dtype) for a in arrays)
        + (jax.ShapeDtypeStruct((SUBLANES, LANES), F32),),
        in_specs=[HBM_SPEC] * len(arrays),
        out_specs=(SEM_SPEC,) * (2 * n) + (HBM_SPEC,) * len(arrays) + (pl.BlockSpec(memory_space=pltpu.VMEM),),
        input_output_aliases={i: 2 * n + i for i in range(len(arrays))},
        compiler_params=pltpu.CompilerParams(has_side_effects=SPLIT_EFFECT),
    )(*[pltpu.with_memory_space_constraint(a, pltpu.HBM) for a in arrays])
    return list(outs[:n]), list(outs[n:2 * n]), list(outs[2 * n:3 * n]), list(outs[3 * n:-1]), outs[-1]


def _exchange_wait(lands, sources, send_sems, recv_sems, after, name):
    n = len(lands)
    given = [s for s in sources if s is not None]
    arrays = list(lands) + given

    def body(*refs):
        land_refs, src_refs = refs[:n], list(refs[n:n + len(given)])
        sems = refs[len(arrays):len(arrays) + 2 * n]
        for i in range(n):
            src_ref = None if sources[i] is None else src_refs.pop(0)
            copies = _split_copies(land_refs[i], src_ref, sems[i], sems[n + i])
            for copy in copies:
                copy.wait_recv()
            for copy in copies:
                copy.wait_send()

    outs = pl.pallas_call(
        body, name=name, out_shape=tuple(pltpu.HBM(a.shape, a.dtype) for a in arrays),
        in_specs=[HBM_SPEC] * len(arrays) + [SEM_SPEC] * (2 * n) + [pl.BlockSpec(memory_space=pl.ANY)],
        out_specs=(HBM_SPEC,) * len(arrays),
        input_output_aliases={i: i for i in range(len(arrays))},
        compiler_params=pltpu.CompilerParams(has_side_effects=SPLIT_EFFECT),
    )(*arrays, *send_sems, *recv_sems, after)
    return list(outs[:n])


def _place_own(a, me, name, own_block, dtype=BF16, after=None):
    shape = a.shape[1:] if own_block else a.shape
    cols = shape[-1]
    a3 = a.reshape((N_DEV if own_block else 1, -1, cols))
    rows = a3.shape[1]
    tr = min(rows, 512)

    def body(me_ref, a_ref, _, o_ref):
        o_ref[...] = a_ref[...].astype(dtype)

    grid_spec = pltpu.PrefetchScalarGridSpec(
        num_scalar_prefetch=1, grid=(rows // tr,),
        in_specs=[pl.BlockSpec((1, tr, cols), lambda i, me_ref: (me_ref[0] if own_block else 0, i, 0)),
                  pl.BlockSpec(memory_space=pl.ANY)],
        out_specs=pl.BlockSpec((1, tr, cols), lambda i, me_ref: (me_ref[0], i, 0)))
    out = pl.pallas_call(
        body, name=name, grid_spec=grid_spec, out_shape=jax.ShapeDtypeStruct((N_DEV, rows, cols), dtype),
        compiler_params=_params(("arbitrary",)),
    )(me, a3, a3 if after is None else after)
    return out.reshape((N_DEV,) + shape)


def _place_own_columns(a, me, name, after=None):
    rows, cols = a.shape
    tr = min(rows, 512)

    def body(me_ref, a_ref, _, o_ref):
        o_ref[...] = a_ref[...].astype(BF16)

    grid_spec = pltpu.PrefetchScalarGridSpec(
        num_scalar_prefetch=1, grid=(rows // tr,),
        in_specs=[pl.BlockSpec((tr, cols), lambda i, me_ref: (i, 0)), pl.BlockSpec(memory_space=pl.ANY)],
        out_specs=pl.BlockSpec((tr, cols), lambda i, me_ref: (i, me_ref[0])))
    return pl.pallas_call(
        body, name=name, grid_spec=grid_spec, out_shape=jax.ShapeDtypeStruct((rows, N_DEV * cols), BF16),
        compiler_params=_params(("arbitrary",)),
    )(me, a, a if after is None else after)


def _rope_tables(positions):
    in_head = jnp.arange(LANES) % C_HEAD_DIM
    inv = ROPE_THETA ** (-(in_head % C_ROT_HALF).astype(F32) / C_ROT_HALF)
    ang = positions.reshape(-1)[:, None].astype(F32) * inv
    rotated = in_head < 2 * C_ROT_HALF
    sin = jnp.sin(ang)
    return (jnp.where(rotated, jnp.cos(ang), 1.0),
            jnp.where(in_head < C_ROT_HALF, -sin, jnp.where(rotated, sin, 0.0)))


def _swap_halves(x):
    lane = lax.broadcasted_iota(jnp.int32, x.shape, 1) % C_HEAD_DIM
    return jnp.where(lane < C_ROT_HALF, pltpu.roll(x, LANES - C_ROT_HALF, 1), pltpu.roll(x, C_ROT_HALF, 1))


def _norm_inproj(x, g, w, name):
    t = x.shape[0]
    n = w.shape[1]
    tm, tn = ROW_TILE, n

    def body(x_ref, g_ref, w_ref, o_ref, h_ref):
        @pl.when(pl.program_id(1) == 0)
        def _():
            h_ref[...] = _rms(x_ref[...], g_ref[...]).astype(BF16)

        o_ref[...] = _dot(h_ref[...], w_ref[...])

    return pl.pallas_call(
        body, name=name, grid=(t // tm, n // tn),
        in_specs=[pl.BlockSpec((tm, D_MODEL), lambda i, j: (i, 0)), pl.BlockSpec((1, D_MODEL), lambda i, j: (0, 0)),
                  pl.BlockSpec((D_MODEL, tn), lambda i, j: (0, j))],
        out_specs=[pl.BlockSpec((tm, tn), lambda i, j: (i, j)), pl.BlockSpec((tm, D_MODEL), lambda i, j: (i, 0))],
        out_shape=[jax.ShapeDtypeStruct((t, n), F32), jax.ShapeDtypeStruct((t, D_MODEL), BF16)],
        compiler_params=_params(("parallel", "arbitrary")),
    )(x, g, w)


def _dilated_specs(tm, width, col_of):
    per_seq = SEQ // tm
    specs = []
    for d in C_DILATIONS:
        specs.append(pl.BlockSpec(
            (1, d, tm // d, width), lambda i, *rest: (i // per_seq, 0, i % per_seq, col_of(*rest))))
    return specs


def _dilated_shapes(n_seq, cols, dtype):
    return [jax.ShapeDtypeStruct((n_seq, d, SEQ // d, cols), dtype) for d in C_DILATIONS]


def _store_dilated(src_ref, out_refs, dtype):
    groups, tm, _ = src_ref.shape
    for d, o_ref in zip(C_DILATIONS, out_refs):
        for r in range(d):
            rows = pl.ds(r, tm // d, stride=d) if d > 1 else slice(None)
            for p in range(groups):
                o_ref[0, r, :, p * LANES:(p + 1) * LANES] = src_ref.at[p][rows, :].astype(dtype)


def _load_dilated(in_ref, d, dst_ref):
    groups, tm, _ = dst_ref.shape
    for r in range(d):
        rows = pl.ds(r, tm // d, stride=d)
        for p in range(groups):
            dst_ref.at[p][rows, :] = in_ref[0, r, :, p * LANES:(p + 1) * LANES].astype(F32)


def _norm_inproj_rope(x, g, w, rope, name):
    t = x.shape[0]
    n = w.shape[1]
    tm, nb = PROJ_TILE, PROJ_COLS

    def body(x_ref, g_ref, w_ref, c_ref, s_ref, o1_ref, o4_ref, o16_ref, h_ref, tile_ref):
        j = pl.program_id(1)

        @pl.when(j == 0)
        def _():
            h_ref[...] = _rms(x_ref[...], g_ref[...]).astype(BF16)

        acc = _dot(h_ref[...], w_ref[...])
        for p in range(nb // LANES):
            blk = acc[:, p * LANES:(p + 1) * LANES]
            roped = blk * c_ref[...] + _swap_halves(blk) * s_ref[...]
            piece = j * (nb // LANES) + p
            is_qk = piece < 2 * (D_MODEL // LANES)
            tile_ref[p] = jnp.where(is_qk, roped, blk) * jnp.where(piece < D_MODEL // LANES, QK_SCALE, 1.0)
        _store_dilated(tile_ref, (o1_ref, o4_ref, o16_ref), BF16)

    return pl.pallas_call(
        body, name=name, grid=(t // tm, n // nb),
        in_specs=[pl.BlockSpec((tm, D_MODEL), lambda i, j: (i, 0)), pl.BlockSpec((1, D_MODEL), lambda i, j: (0, 0)),
                  pl.BlockSpec((D_MODEL, nb), lambda i, j: (0, j)),
                  pl.BlockSpec((tm, LANES), lambda i, j: (i, 0)), pl.BlockSpec((tm, LANES), lambda i, j: (i, 0))],
        out_specs=_dilated_specs(tm, nb, lambda j: j) + [pl.BlockSpec((tm, D_MODEL), lambda i, j: (i, 0))],
        out_shape=_dilated_shapes(t // SEQ, n, BF16) + [jax.ShapeDtypeStruct((t, D_MODEL), BF16)],
        scratch_shapes=[pltpu.VMEM((nb // LANES, tm, LANES), F32)],
        compiler_params=_params(("parallel", "arbitrary")),
    )(x, g, w, *rope)


def _outproj(parts, w, x, g, name):
    t = x.shape[0]
    tm = PROJ_TILE
    n = len(parts)
    widths = [p.shape[1] for p in parts]

    def body(*refs):
        p_refs = refs[:n]
        w_ref, x_ref, g_ref, xo_ref, mix_ref = refs[n:]
        mix = None
        off = 0
        for p_ref, wd in zip(p_refs, widths):
            term = _dot(p_ref[...].astype(BF16), w_ref[off:off + wd, :])
            mix = term if mix is None else mix + term
            off += wd
        mix_ref[...] = mix
        xo_ref[...] = x_ref[...] + _rms(mix, g_ref[...])

    row = lambda i: (i, 0)
    return pl.pallas_call(
        body, name=name, grid=(t // tm,),
        in_specs=[pl.BlockSpec((tm, wd), row) for wd in widths] + [
            pl.BlockSpec((sum(widths), D_MODEL), lambda i: (0, 0)),
            pl.BlockSpec((tm, D_MODEL), row), pl.BlockSpec((1, D_MODEL), lambda i: (0, 0))],
        out_specs=[pl.BlockSpec((tm, D_MODEL), row)] * 2,
        out_shape=[jax.ShapeDtypeStruct((t, D_MODEL), F32)] * 2,
        compiler_params=_params(("parallel",)),
    )(*parts, w, x, g)


def _outproj_bwd(dx, mix, g, w, name):
    t = dx.shape[0]
    tm = PROJ_TILE
    k = w.shape[0]

    def body(dx_ref, mix_ref, g_ref, w_ref, dcat_ref, dz_ref, dg_ref):
        dz, dgr = _rms_bwd(mix_ref[...], g_ref[...], dx_ref[...])
        dzb = dz.astype(BF16)
        dz_ref[...] = dzb
        dcat_ref[...] = _dot_nt(dzb, w_ref[...])
        _acc_rows8(dg_ref, _rows8(dgr), pl.program_id(0) == 0)

    row = lambda i: (i, 0)
    return pl.pallas_call(
        body, name=name, grid=(t // tm,),
        in_specs=[pl.BlockSpec((tm, D_MODEL), row), pl.BlockSpec((tm, D_MODEL), row),
                  pl.BlockSpec((1, D_MODEL), lambda i: (0, 0)), pl.BlockSpec((k, D_MODEL), lambda i: (0, 0))],
        out_specs=[pl.BlockSpec((tm, k), row), pl.BlockSpec((tm, D_MODEL), row),
                   pl.BlockSpec((SUBLANES, D_MODEL), lambda i: (0, 0))],
        out_shape=[jax.ShapeDtypeStruct((t, k), F32), jax.ShapeDtypeStruct((t, D_MODEL), BF16),
                   jax.ShapeDtypeStruct((SUBLANES, D_MODEL), F32)],
        compiler_params=_params(("arbitrary",)),
    )(dx, mix, g, w)


def _outproj_bwd_attn(dx, mix, g, w, out, name):
    t = dx.shape[0]
    tm = MERGE_TILE

    def body(dx_ref, mix_ref, g_ref, w_ref, out_ref, do1, do4, do16, dl_ref, dz_ref, dg_ref, tile_ref):
        dz, dgr = _rms_bwd(mix_ref[...], g_ref[...], dx_ref[...])
        dzb = dz.astype(BF16)
        dz_ref[...] = dzb
        _acc_rows8(dg_ref, _rows8(dgr), pl.program_id(0) == 0)
        dout = _dot_nt(dzb, w_ref[...])
        for p in range(LANE_GROUPS):
            tile_ref[p] = dout[:, p * LANES:(p + 1) * LANES]
        _store_dilated(tile_ref, (do1, do4, do16), BF16)
        column = lax.broadcasted_iota(jnp.int32, (D_MODEL, LANES), 0) // C_HEAD_DIM
        head = lax.broadcasted_iota(jnp.int32, (D_MODEL, LANES), 1)
        dl_ref[...] = _dot_mask(dout * out_ref[...], column == head)

    row = lambda i: (i, 0)
    n_seq = t // SEQ
    return pl.pallas_call(
        body, name=name, grid=(t // tm,),
        in_specs=[pl.BlockSpec((tm, D_MODEL), row), pl.BlockSpec((tm, D_MODEL), row),
                  pl.BlockSpec((1, D_MODEL), lambda i: (0, 0)), pl.BlockSpec((D_MODEL, D_MODEL), lambda i: (0, 0)),
                  pl.BlockSpec((tm, D_MODEL), row)],
        out_specs=_dilated_specs(tm, D_MODEL, lambda: 0) + [
            pl.BlockSpec((tm, LANES), row), pl.BlockSpec((tm, D_MODEL), row),
            pl.BlockSpec((SUBLANES, D_MODEL), lambda i: (0, 0))],
        out_shape=_dilated_shapes(n_seq, D_MODEL, BF16) + [
            jax.ShapeDtypeStruct((t, LANES), F32), jax.ShapeDtypeStruct((t, D_MODEL), BF16),
            jax.ShapeDtypeStruct((SUBLANES, D_MODEL), F32)],
        scratch_shapes=[pltpu.VMEM((LANE_GROUPS, tm, LANES), F32)],
        compiler_params=_params(("arbitrary",)),
    )(dx, mix, g, w, out)


def _inproj_bwd(dproj, w, dx, x, g, name):
    t = x.shape[0]
    n = w.shape[1]
    tm = ROW_TILE

    def body(dp_ref, w_ref, dx_ref, x_ref, g_ref, o_ref, dg_ref):
        dxn, dgr = _rms_bwd(x_ref[...], g_ref[...], _dot_nt(dp_ref[...], w_ref[...]))
        o_ref[...] = dx_ref[...] + dxn
        _acc_rows8(dg_ref, _rows8(dgr), pl.program_id(0) == 0)

    row = lambda i: (i, 0)
    return pl.pallas_call(
        body, name=name, grid=(t // tm,),
        in_specs=[pl.BlockSpec((tm, n), row), pl.BlockSpec((D_MODEL, n), lambda i: (0, 0)),
                  pl.BlockSpec((tm, D_MODEL), row), pl.BlockSpec((tm, D_MODEL), row),
                  pl.BlockSpec((1, D_MODEL), lambda i: (0, 0))],
        out_specs=[pl.BlockSpec((tm, D_MODEL), row), pl.BlockSpec((SUBLANES, D_MODEL), lambda i: (0, 0))],
        out_shape=[jax.ShapeDtypeStruct((t, D_MODEL), F32), jax.ShapeDtypeStruct((SUBLANES, D_MODEL), F32)],
        compiler_params=_params(("arbitrary",)),
    )(dproj, w, dx, x, g)


def _grad_w(a, b, col_blocks, name):
    t, k = a.shape
    n = b.shape[1]
    tk = min(k, 1024)
    per_owner = n // N_DEV
    tn = 2 * per_owner if col_blocks else min(n, 1024)

    def body(a_ref, b_ref, o_ref, at_ref):
        @pl.when(pl.program_id(1) == 0)
        def _():
            for c in range(t // ROW_TILE):
                rows = slice(c * ROW_TILE, (c + 1) * ROW_TILE)
                at_ref[:, rows] = a_ref[rows, :].T

        res = _dot(at_ref[...], b_ref[...]).astype(BF16)
        if col_blocks:
            o_ref[0] = res[:, :per_owner]
            o_ref[1] = res[:, per_owner:]
        else:
            o_ref[...] = res

    if col_blocks:
        out_spec = pl.BlockSpec((2, tk, per_owner), lambda i, j: (j, i, 0))
        out_shape = jax.ShapeDtypeStruct((N_DEV, k, per_owner), BF16)
    else:
        out_spec = pl.BlockSpec((tk, tn), lambda i, j: (i, j))
        out_shape = jax.ShapeDtypeStruct((k, n), BF16)
    return pl.pallas_call(
        body, name=name, grid=(k // tk, n // tn),
        in_specs=[pl.BlockSpec((t, tk), lambda i, j: (0, i)), pl.BlockSpec((t, tn), lambda i, j: (0, j))],
        out_specs=out_spec, out_shape=out_shape,
        scratch_shapes=[pltpu.VMEM((tk, t), BF16)],
        compiler_params=_params(("parallel", "arbitrary")),
    )(a, b)


FF_STEP = 1024
FF_STEPS = D_FF // FF_STEP


def _ffn_fwd(x, g_pre, w1, w2, g_post, name, target=None):
    t = x.shape[0]
    tm = PROJ_TILE

    def body(*refs):
        if target is None:
            x_ref, gp_ref, w1_ref, w2_ref, gq_ref, xo_ref, y_ref, h_ref, r_ref = refs
        else:
            x_ref, gp_ref, w1_ref, w2_ref, gq_ref, t_ref, xo_ref, y_ref, h_ref, r_ref, l_ref = refs
        i, j = pl.program_id(0), pl.program_id(1)

        @pl.when(j == 0)
        def _():
            h_ref[...] = _rms(x_ref[...], gp_ref[...]).astype(BF16)

        a = _dot(h_ref[...], w1_ref[...])
        r = jnp.square(jnp.maximum(a, 0.0)).astype(BF16)
        r_ref[...] = r
        term = _dot(r, w2_ref[...])

        @pl.when(j == 0)
        def _():
            y_ref[...] = term

        @pl.when(j > 0)
        def _():
            y_ref[...] += term

        @pl.when(j == FF_STEPS - 1)
        def _():
            x_new = x_ref[...] + _rms(y_ref[...], gq_ref[...])
            if target is None:
                xo_ref[...] = x_new
            else:
                diff = x_new - t_ref[...]
                xo_ref[...] = diff * (1.0 / D_MODEL)
                _acc_rows8(l_ref, _rows8(diff * diff) * (0.5 / D_MODEL), i == 0)

    row = lambda i, j: (i, 0)
    vec = pl.BlockSpec((1, D_MODEL), lambda i, j: (0, 0))
    in_specs = [pl.BlockSpec((tm, D_MODEL), row), vec, pl.BlockSpec((D_MODEL, FF_STEP), lambda i, j: (0, j)),
                pl.BlockSpec((FF_STEP, D_MODEL), lambda i, j: (j, 0)), vec]
    out_specs = [pl.BlockSpec((tm, D_MODEL), row)] * 3 + [pl.BlockSpec((tm, FF_STEP), lambda i, j: (i, j))]
    out_shape = [jax.ShapeDtypeStruct((t, D_MODEL), F32), jax.ShapeDtypeStruct((t, D_MODEL), F32),
                 jax.ShapeDtypeStruct((t, D_MODEL), BF16), jax.ShapeDtypeStruct((t, D_FF), BF16)]
    args = [x, g_pre, w1, w2, g_post]
    if target is not None:
        in_specs.append(pl.BlockSpec((tm, D_MODEL), row))
        out_specs.append(pl.BlockSpec((SUBLANES, D_MODEL), lambda i, j: (0, 0)))
        out_shape.append(jax.ShapeDtypeStruct((SUBLANES, D_MODEL), F32))
        args.append(target)
    return pl.pallas_call(
        body, name=name, grid=(t // tm, FF_STEPS), in_specs=in_specs, out_specs=out_specs, out_shape=out_shape,
        compiler_params=_params(("parallel" if target is None else "arbitrary", "arbitrary")),
    )(*args)


def _ffn_bwd(dxo, x, y, r, g_pre, w1, w2, g_post, name):
    t = x.shape[0]
    tm = ROW_TILE

    def body(dxo_ref, x_ref, y_ref, r_ref, gp_ref, w1_ref, w2_ref, gq_ref,
             dx_ref, dy_ref, da_ref, dgp_ref, dgq_ref, acc_ref):
        i, j = pl.program_id(0), pl.program_id(1)

        @pl.when(j == 0)
        def _():
            dy, dgr = _rms_bwd(y_ref[...], gq_ref[...], dxo_ref[...])
            dy_ref[...] = dy.astype(BF16)
            _acc_rows8(dgq_ref, _rows8(dgr), i == 0)

        dr = _dot_nt(dy_ref[...], w2_ref[...])
        da = (dr * (2.0 * jnp.sqrt(r_ref[...].astype(F32)))).astype(BF16)
        da_ref[...] = da
        term = _dot_nt(da, w1_ref[...])

        @pl.when(j == 0)
        def _():
            acc_ref[...] = term

        @pl.when(j > 0)
        def _():
            acc_ref[...] += term

        @pl.when(j == FF_STEPS - 1)
        def _():
            dxn, dgr = _rms_bwd(x_ref[...], gp_ref[...], acc_ref[...])
            dx_ref[...] = dxo_ref[...] + dxn
            _acc_rows8(dgp_ref, _rows8(dgr), i == 0)

    row = lambda i, j: (i, 0)
    vec = pl.BlockSpec((1, D_MODEL), lambda i, j: (0, 0))
    acc8 = pl.BlockSpec((SUBLANES, D_MODEL), lambda i, j: (0, 0))
    return pl.pallas_call(
        body, name=name, grid=(t // tm, FF_STEPS),
        in_specs=[pl.BlockSpec((tm, D_MODEL), row)] * 3 + [
            pl.BlockSpec((tm, FF_STEP), lambda i, j: (i, j)),
            vec, pl.BlockSpec((D_MODEL, FF_STEP), lambda i, j: (0, j)),
            pl.BlockSpec((FF_STEP, D_MODEL), lambda i, j: (j, 0)), vec],
        out_specs=[pl.BlockSpec((tm, D_MODEL), row), pl.BlockSpec((tm, D_MODEL), row),
                   pl.BlockSpec((tm, FF_STEP), lambda i, j: (i, j)), acc8, acc8],
        out_shape=[jax.ShapeDtypeStruct((t, D_MODEL), F32), jax.ShapeDtypeStruct((t, D_MODEL), BF16),
                   jax.ShapeDtypeStruct((t, D_FF), BF16),
                   jax.ShapeDtypeStruct((SUBLANES, D_MODEL), F32), jax.ShapeDtypeStruct((SUBLANES, D_MODEL), F32)],
        scratch_shapes=[pltpu.VMEM((tm, D_MODEL), F32)],
        compiler_params=_params(("arbitrary", "arbitrary")),
    )(dxo, x, y, r, g_pre, w1, w2, g_post)


def _lower_bound(table):
    e = jnp.exp(table - jnp.max(table, axis=0, keepdims=True))
    return e[0:1, :] / jnp.sum(e, axis=0, keepdims=True)


def _hgrn2_block(q_ref, f_ref, lb):
    tb = f_ref.shape[0]
    sig = _sigmoid(f_ref[...])
    f = lb + (1.0 - lb) * sig
    qraw = q_ref[...]
    sq = _sigmoid(qraw)
    r = lax.broadcasted_iota(jnp.int32, (tb, tb), 0)
    c = lax.broadcasted_iota(jnp.int32, (tb, tb), 1)
    same = (r // SUB_CHUNK) == (c // SUB_CHUNK)
    logf = jnp.log(f)
    gsum = _mask_dot(same & (c <= r), logf)
    glast = _mask_dot(same, logf)
    return dict(sig=sig, f=f, kk=1.0 - f, qraw=qraw, sq=sq, qs=qraw * sq, gsum=gsum,
                eg=jnp.exp(gsum), ekd=jnp.exp(glast - gsum), a=jnp.exp(glast))


def _head_sums(x):
    parts = [jnp.broadcast_to(jnp.sum(x[:, h * HEAD_A:(h + 1) * HEAD_A], axis=1, keepdims=True), (x.shape[0], HEAD_A))
             for h in range(A_HEADS)]
    return jnp.concatenate(parts, axis=1)


def _hgrn2_intra(g, kk, qs, v):
    row = lax.broadcasted_iota(jnp.int32, g.shape, 0)
    o = _head_sums(qs * kk) * v
    for j in range(1, SUB_CHUNK):
        decay = jnp.exp(jnp.where(row >= j, g - pltpu.roll(g, j, 0), NEG))
        o = o + _head_sums(qs * pltpu.roll(kk, j, 0) * decay) * pltpu.roll(v, j, 0)
    return o


def _hgrn2_intra_bwd(g, kk, qs, v, do):
    row = lax.broadcasted_iota(jnp.int32, g.shape, 0)
    dsc = _head_sums(do * v)
    dqs, dkk, dv = dsc * kk, dsc * qs, _head_sums(qs * kk) * do
    for j in range(1, SUB_CHUNK):
        k_dn = pltpu.roll(kk, j, 0)
        decay = jnp.exp(jnp.where(row >= j, g - pltpu.roll(g, j, 0), NEG))
        d_score = _head_sums(do * pltpu.roll(v, j, 0)) * decay
        dqs = dqs + d_score * k_dn
        dkk = dkk + pltpu.roll(d_score * qs, SUB_CHUNK - j, 0)
        dv = dv + pltpu.roll(_head_sums(qs * k_dn * decay) * do, SUB_CHUNK - j, 0)
    return dqs, dkk, dv


def _hgrn2_fwd(proj, lb_table, a_norm, name):
    t = proj.shape[0]
    tb = HGRN_BLOCK
    n_tb = SEQ // tb
    n_seq = t // SEQ
    n_sub = tb // SUB_CHUNK

    def body(q_ref, f_ref, i_ref, g_ref, lbt_ref, an_ref, o_ref, pre_ref, sts_ref, st_ref,
             gs_ref, kk_ref, qs_ref, eg_ref, ekd_ref, a_ref):
        @pl.when(pl.program_id(1) == 0)
        def _():
            st_ref[...] = jnp.zeros_like(st_ref)

        an = an_ref[...]
        blk = _hgrn2_block(q_ref, f_ref, _lower_bound(lbt_ref[...]))
        for ref, key in ((gs_ref, "gsum"), (kk_ref, "kk"), (qs_ref, "qs"), (eg_ref, "eg"), (ekd_ref, "ekd"), (a_ref, "a")):
            ref[...] = blk[key]

        def step(c, carry):
            rows = pl.ds(pl.multiple_of(c * SUB_CHUNK, SUB_CHUNK), SUB_CHUNK)
            kk, qs, v = kk_ref[rows, :], qs_ref[rows, :], i_ref[rows, :]
            o = _hgrn2_intra(gs_ref[rows, :], kk, qs, v)
            qg, kd, vb = (qs * eg_ref[rows, :]).astype(BF16), (kk * ekd_ref[rows, :]).astype(BF16), v.astype(BF16)
            for h in range(A_HEADS):
                lanes = slice(h * HEAD_A, (h + 1) * HEAD_A)
                st = st_ref[h]
                sts_ref[0, c, h] = st
                o_h = o[:, lanes] + _dot_nt(qg[:, lanes], st.astype(BF16))
                st_ref[h] = st * a_ref[rows, lanes][0:1] + _dot_tn(vb[:, lanes], kd[:, lanes])
                pre_ref[rows, lanes] = o_h
                graw = g_ref[rows, lanes]
                o_ref[rows, lanes] = (_rms(o_h, an[:, lanes]) * (graw * _sigmoid(graw))).astype(BF16)
            return carry

        lax.fori_loop(0, n_sub, step, 0, unroll=2)

    def col(k):
        return pl.BlockSpec((tb, A_WIDTH), lambda b, s, k=k: (b * n_tb + s, k))

    out_rows = pl.BlockSpec((tb, A_WIDTH), lambda b, s: (b * n_tb + s, 0))
    return pl.pallas_call(
        body, name=name, grid=(n_seq, n_tb),
        in_specs=[col(0), col(1), col(2), col(3),
                  pl.BlockSpec((3, A_WIDTH), lambda b, s: (0, 0)), pl.BlockSpec((1, A_WIDTH), lambda b, s: (0, 0))],
        out_specs=[out_rows, out_rows,
                   pl.BlockSpec((1, n_sub, A_HEADS, HEAD_A, HEAD_A), lambda b, s: (b * n_tb + s, 0, 0, 0, 0))],
        out_shape=[jax.ShapeDtypeStruct((t, D_MODEL), BF16), jax.ShapeDtypeStruct((t, A_WIDTH), F32),
                   jax.ShapeDtypeStruct((n_seq * n_tb, n_sub, A_HEADS, HEAD_A, HEAD_A), F32)],
        scratch_shapes=[pltpu.VMEM((A_HEADS, HEAD_A, HEAD_A), F32)] + [pltpu.VMEM((tb, A_WIDTH), F32)] * 6,
        compiler_params=_params(("parallel", "arbitrary")),
    )(proj, proj, proj, proj, lb_table, a_norm)


def _hgrn2_bwd(proj, dcat, pre, states, lb_table, a_norm, name):
    t = proj.shape[0]
    tb = HGRN_BLOCK
    n_tb = SEQ // tb
    n_seq = t // SEQ
    n_sub = tb // SUB_CHUNK

    def body(q_ref, f_ref, i_ref, g_ref, do_ref, pre_ref, sts_ref, lbt_ref, an_ref, dp_ref, dlb_ref, dan_ref, dst_ref,
             gs_ref, kk_ref, qs_ref, eg_ref, ekd_ref, a_ref, dpre_ref, dlf_ref, dqs_ref, dkk_ref):
        b, s = pl.program_id(0), pl.program_id(1)

        @pl.when(s == 0)
        def _():
            dst_ref[...] = jnp.zeros_like(dst_ref)

        @pl.when((b == 0) & (s == 0))
        def _():
            dlb_ref[...] = jnp.zeros_like(dlb_ref)
            dan_ref[...] = jnp.zeros_like(dan_ref)

        lb = _lower_bound(lbt_ref[...])
        an = an_ref[...]
        heads = [slice(h * HEAD_A, (h + 1) * HEAD_A) for h in range(A_HEADS)]
        blk = _hgrn2_block(q_ref, f_ref, lb)
        for ref, key in ((gs_ref, "gsum"), (kk_ref, "kk"), (qs_ref, "qs"), (eg_ref, "eg"), (ekd_ref, "ekd"), (a_ref, "a")):
            ref[...] = blk[key]
        for h, lanes in enumerate(heads):
            graw, o = g_ref[:, lanes], pre_ref[:, lanes]
            sg = _sigmoid(graw)
            dout = do_ref[:, lanes]
            d_o, dgr = _rms_bwd(o, an[:, lanes], dout * (graw * sg))
            dan_ref[0:1, lanes] += jnp.sum(dgr, axis=0, keepdims=True)
            dp_ref[:, 3 * A_WIDTH + h * HEAD_A:3 * A_WIDTH + (h + 1) * HEAD_A] = (
                dout * _rms(o, an[:, lanes]) * (sg * (1.0 + graw * (1.0 - sg)))).astype(BF16)
            dpre_ref[:, lanes] = d_o

        tri_t = (lax.broadcasted_iota(jnp.int32, (SUB_CHUNK, SUB_CHUNK), 0)
                 <= lax.broadcasted_iota(jnp.int32, (SUB_CHUNK, SUB_CHUNK), 1)).astype(F32)

        def back(k, carry):
            c = n_sub - 1 - k
            rows = pl.ds(pl.multiple_of(c * SUB_CHUNK, SUB_CHUNK), SUB_CHUNK)
            g, kk, qs, v, d_o = gs_ref[rows, :], kk_ref[rows, :], qs_ref[rows, :], i_ref[rows, :], dpre_ref[rows, :]
            eg, ekd, a = eg_ref[rows, :], ekd_ref[rows, :], a_ref[rows, :]
            dqs, dkk, dv = _hgrn2_intra_bwd(g, kk, qs, v, d_o)
            qg_f, kd_f = qs * eg, kk * ekd
            qg, kd, vb, dob = qg_f.astype(BF16), kd_f.astype(BF16), v.astype(BF16), d_o.astype(BF16)
            dqg, dkd, da, dv_st = [], [], [], []
            for h, lanes in enumerate(heads):
                st, dst = sts_ref[0, c, h], dst_ref[h]
                dstb = dst.astype(BF16)
                dqg.append(_dot(dob[:, lanes], st.astype(BF16)))
                dv_st.append(_dot_nt(kd[:, lanes], dstb))
                dkd.append(_dot(vb[:, lanes], dstb))
                da.append(jnp.broadcast_to(jnp.sum(dst * st, axis=0, keepdims=True), (SUB_CHUNK, HEAD_A)))
                dst_ref[h] = dst * a[0:1, lanes] + _dot_tn(dob[:, lanes], qg[:, lanes])
            dqg, dkd, da, dv_st = [jnp.concatenate(p, axis=1) for p in (dqg, dkd, da, dv_st)]
            d_gsum = qs * dqs - kk * dkk + dqg * qg_f - dkd * kd_f
            d_glast = jnp.sum(dkd * kd_f, axis=0, keepdims=True) + da * a
            dlf_ref[rows, :] = jnp.dot(tri_t, d_gsum, precision=lax.Precision.HIGHEST,
                                       preferred_element_type=F32) + d_glast
            dqs_ref[rows, :] = dqs + dqg * eg
            dkk_ref[rows, :] = dkk + dkd * ekd
            dp_ref[rows, 2 * A_WIDTH:3 * A_WIDTH] = (dv + dv_st).astype(BF16)
            return carry

        lax.fori_loop(0, n_sub, back, 0, unroll=2)
        sig, sq, qraw = blk["sig"], blk["sq"], blk["qraw"]
        d_f = dlf_ref[...] / blk["f"] - dkk_ref[...]
        dlb_ref[0:1, :] += jnp.sum(d_f * (1.0 - sig), axis=0, keepdims=True)
        dp_ref[:, 0:A_WIDTH] = (dqs_ref[...] * (sq * (1.0 + qraw * (1.0 - sq)))).astype(BF16)
        dp_ref[:, A_WIDTH:2 * A_WIDTH] = (d_f * (1.0 - lb) * sig * (1.0 - sig)).astype(BF16)

    def rev(s):
        return n_tb - 1 - s

    def col(k):
        return pl.BlockSpec((tb, A_WIDTH), lambda b, s, k=k: (b * n_tb + rev(s), k))

    acc8 = pl.BlockSpec((SUBLANES, A_WIDTH), lambda b, s: (0, 0))
    return pl.pallas_call(
        body, name=name, grid=(n_seq, n_tb),
        in_specs=[col(0), col(1), col(2), col(3), col(0), col(0),
                  pl.BlockSpec((1, n_sub, A_HEADS, HEAD_A, HEAD_A), lambda b, s: (b * n_tb + rev(s), 0, 0, 0, 0)),
                  pl.BlockSpec((3, A_WIDTH), lambda b, s: (0, 0)), pl.BlockSpec((1, A_WIDTH), lambda b, s: (0, 0))],
        out_specs=[pl.BlockSpec((tb, 4 * A_WIDTH), lambda b, s: (b * n_tb + rev(s), 0)), acc8, acc8],
        out_shape=[jax.ShapeDtypeStruct((t, EVEN_IN), BF16)] + [jax.ShapeDtypeStruct((SUBLANES, A_WIDTH), F32)] * 2,
        scratch_shapes=[pltpu.VMEM((A_HEADS, HEAD_A, HEAD_A), F32)] + [pltpu.VMEM((tb, A_WIDTH), F32)] * 10,
        compiler_params=_params(("arbitrary", "arbitrary")),
    )(proj, proj, proj, proj, dcat, pre, states, lb_table, a_norm)


GMLP_ROWS = 512


def _gmlp_chunk(ub, vb, ln_g, ln_b, ws, bias):
    u = [_gelu(a) for a in ub]
    v = [_gelu(a) for a in vb]
    mu = sum(jnp.sum(a, axis=-1, keepdims=True) for a in v) * (1.0 / B_WIDTH)
    cen = [a - mu for a in v]
    var = sum(jnp.sum(a * a, axis=-1, keepdims=True) for a in cen) * (1.0 / B_WIDTH)
    inv = lax.rsqrt(var + EPS)
    r = lax.broadcasted_iota(jnp.int32, (B_CHUNK, B_CHUNK), 0)
    c = lax.broadcasted_iota(jnp.int32, (B_CHUNK, B_CHUNK), 1)
    outs = []
    for g in range(B_GROUPS):
        vn = (cen[g] * inv * ln_g[g] + ln_b[g]).astype(BF16)
        wm = jnp.where(c <= r, ws[g], 0.0).astype(BF16)
        outs.append(u[g] * (_dot(wm, vn) + bias[g]))
    return outs


def _lane_groups(ref, rows=slice(None)):
    return [ref[rows, g * LANES:(g + 1) * LANES] for g in range(B_GROUPS)]


def _gmlp_fwd(proj, mixed, ln_g, ln_b, ws, bias_t, name):
    t = proj.shape[0]
    tm = GMLP_ROWS

    def body(u_ref, v_ref, lg_ref, lb_ref, ws_ref, bt_ref, _, o_ref):
        for ch in range(tm // B_CHUNK):
            rows = slice(ch * B_CHUNK, (ch + 1) * B_CHUNK)
            outs = _gmlp_chunk(_lane_groups(u_ref, rows), _lane_groups(v_ref, rows), _lane_groups(lg_ref),
                               _lane_groups(lb_ref), [ws_ref[g] for g in range(B_GROUPS)],
                               [bt_ref[:, g:g + 1] for g in range(B_GROUPS)])
            for g in range(B_GROUPS):
                o_ref[rows, g * LANES:(g + 1) * LANES] = outs[g].astype(BF16)

    vec = pl.BlockSpec((1, B_WIDTH), lambda i: (0, 0))
    return pl.pallas_call(
        body, name=name, grid=(t // tm,),
        in_specs=[pl.BlockSpec((tm, B_WIDTH), lambda i: (i, 4)), pl.BlockSpec((tm, B_WIDTH), lambda i: (i, 5)), vec, vec,
                  pl.BlockSpec((B_GROUPS, B_CHUNK, B_CHUNK), lambda i: (0, 0, 0)),
                  pl.BlockSpec((B_CHUNK, B_GROUPS), lambda i: (0, 0)), pl.BlockSpec(memory_space=pl.ANY)],
        out_specs=pl.BlockSpec((tm, B_WIDTH), lambda i: (i, 1)),
        out_shape=jax.ShapeDtypeStruct(mixed.shape, BF16),
        input_output_aliases={6: 0},
        compiler_params=_params(("parallel",)),
    )(proj, proj, ln_g, ln_b, ws, bias_t, mixed)


def _gmlp_bwd(proj, dcat, dproj, ln_g, ln_b, ws, bias_t, name):
    t = proj.shape[0]
    tm = GMLP_ROWS

    def body(u_ref, v_ref, do_ref, lg_ref, lb_ref, ws_ref, bt_ref, _, duv_ref, dlg_ref, dlb_ref, dws_ref, dbt_ref):
        @pl.when(pl.program_id(0) == 0)
        def _():
            dlg_ref[...] = jnp.zeros_like(dlg_ref)
            dlb_ref[...] = jnp.zeros_like(dlb_ref)
            dws_ref[...] = jnp.zeros_like(dws_ref)
            dbt_ref[...] = jnp.zeros_like(dbt_ref)

        for ch in range(tm // B_CHUNK):
            rows = slice(ch * B_CHUNK, (ch + 1) * B_CHUNK)
            _, vjp = jax.vjp(
                _gmlp_chunk, _lane_groups(u_ref, rows), _lane_groups(v_ref, rows), _lane_groups(lg_ref),
                _lane_groups(lb_ref), [ws_ref[g] for g in range(B_GROUPS)],
                [bt_ref[:, g:g + 1] for g in range(B_GROUPS)])
            du, dv, dlg, dlb, dw, dbt = vjp(_lane_groups(do_ref, rows))
            for g in range(B_GROUPS):
                lanes = slice(g * LANES, (g + 1) * LANES)
                duv_ref[rows, lanes] = du[g].astype(BF16)
                duv_ref[rows, B_WIDTH + g * LANES:B_WIDTH + (g + 1) * LANES] = dv[g].astype(BF16)
                dlg_ref[0:1, lanes] += dlg[g]
                dlb_ref[0:1, lanes] += dlb[g]
                dws_ref[g] += dw[g]
                dbt_ref[:, g:g + 1] += dbt[g]

    vec = pl.BlockSpec((1, B_WIDTH), lambda i: (0, 0))
    acc8 = pl.BlockSpec((SUBLANES, B_WIDTH), lambda i: (0, 0))
    ws_spec = pl.BlockSpec((B_GROUPS, B_CHUNK, B_CHUNK), lambda i: (0, 0, 0))
    bt_spec = pl.BlockSpec((B_CHUNK, B_GROUPS), lambda i: (0, 0))
    return pl.pallas_call(
        body, name=name, grid=(t // tm,),
        in_specs=[pl.BlockSpec((tm, B_WIDTH), lambda i: (i, 4)), pl.BlockSpec((tm, B_WIDTH), lambda i: (i, 5)),
                  pl.BlockSpec((tm, B_WIDTH), lambda i: (i, 1)), vec, vec, ws_spec, bt_spec,
                  pl.BlockSpec(memory_space=pl.ANY)],
        out_specs=[pl.BlockSpec((tm, 2 * B_WIDTH), lambda i: (i, 2)), acc8, acc8, ws_spec, bt_spec],
        out_shape=[jax.ShapeDtypeStruct(dproj.shape, BF16), jax.ShapeDtypeStruct((SUBLANES, B_WIDTH), F32),
                   jax.ShapeDtypeStruct((SUBLANES, B_WIDTH), F32),
                   jax.ShapeDtypeStruct((B_GROUPS, B_CHUNK, B_CHUNK), F32),
                   jax.ShapeDtypeStruct((B_CHUNK, B_GROUPS), F32)],
        input_output_aliases={7: 0},
        compiler_params=_params(("arbitrary",)),
    )(proj, proj, dcat, ln_g, ln_b, ws, bias_t, dproj)


QK_SCALE = 1.0 / math.sqrt(C_HEAD_DIM)
ATTN_UNROLL = 16
ATTN_PAIRS = 2
LANE_GROUPS = D_MODEL // LANES
ATTN_STEPS = LANE_GROUPS // ATTN_PAIRS
Q_BLOCKS = SEQ // C_BLOCK


def _attn_window(i, d):
    sub_blocks = Q_BLOCKS // d
    q0 = pl.multiple_of(i * C_BLOCK, C_BLOCK)
    k0 = pl.multiple_of(jnp.maximum(i - 1, 0) * C_BLOCK, C_BLOCK)
    key = k0 + lax.broadcasted_iota(jnp.int32, (C_BLOCK, 2 * C_BLOCK), 1)
    dist = (q0 + lax.broadcasted_iota(jnp.int32, (C_BLOCK, 2 * C_BLOCK), 0)) - key
    own_subsequence = (key >= q0) | (i % sub_blocks > 0)
    return pl.ds(q0, C_BLOCK), pl.ds(k0, 2 * C_BLOCK), (dist >= 0) & (dist <= C_BLOCK) & own_subsequence


def _head_masks():
    lane = lax.broadcasted_iota(jnp.int32, (C_BLOCK, LANES), 1)
    return [lane < C_HEAD_DIM, lane >= C_HEAD_DIM]


def _flat_spec(col_of):
    return pl.BlockSpec((1, SEQ, ATTN_PAIRS * LANES), lambda b, g: (b, 0, col_of(g)))


def _put_heads(tile, g, col0, col1):
    lane = lax.broadcasted_iota(jnp.int32, tile.shape, 1)
    return jnp.where(lane == 2 * g, col0, jnp.where(lane == 2 * g + 1, col1, tile))


def _get_head(tile, h):
    lane = lax.broadcasted_iota(jnp.int32, tile.shape, 1)
    return jnp.sum(jnp.where(lane == h, tile, 0.0), axis=1, keepdims=True)


PER_HEAD_SPEC = pl.BlockSpec((1, SEQ, LANES), lambda b, g: (b, 0, 0))


def _attn_branch_fwd(qkv, name):
    n_seq, d, l, _ = qkv.shape
    flat = qkv.reshape(n_seq, SEQ, ODD_IN)

    def body(q_ref, k_ref, v_ref, o_ref, m_ref, l_ref):
        heads = _head_masks()
        g = pl.program_id(1)

        @pl.when(g == 0)
        def _():
            m_ref[...] = jnp.zeros_like(m_ref)
            l_ref[...] = jnp.zeros_like(l_ref)

        def block(i, carry):
            rows, keys, mask = _attn_window(i, d)
            m_tile, l_tile = m_ref[0, rows, :], l_ref[0, rows, :]
            for pair in range(ATTN_PAIRS):
                lanes = slice(pair * LANES, (pair + 1) * LANES)
                q, k, v = q_ref[0, rows, lanes], k_ref[0, keys, lanes], v_ref[0, keys, lanes]
                res = []
                for hm in heads:
                    s = jnp.where(mask, _dot_nt(jnp.where(hm, q, 0), k), NEG)
                    m = jnp.max(s, axis=-1, keepdims=True)
                    p = jnp.exp(s - m)
                    res.append((_dot(p.astype(BF16), v), m, jnp.sum(p, axis=-1, keepdims=True)))
                o_ref[0, rows, lanes] = jnp.where(heads[0], res[0][0], res[1][0])
                m_tile = _put_heads(m_tile, g * ATTN_PAIRS + pair, res[0][1], res[1][1])
                l_tile = _put_heads(l_tile, g * ATTN_PAIRS + pair, res[0][2], res[1][2])
            m_ref[0, rows, :] = m_tile
            l_ref[0, rows, :] = l_tile
            return carry

        lax.fori_loop(0, Q_BLOCKS, block, 0, unroll=ATTN_UNROLL)

    o, m, l_sum = pl.pallas_call(
        body, name=name, grid=(n_seq, ATTN_STEPS),
        in_specs=[_flat_spec(lambda g: g), _flat_spec(lambda g: ATTN_STEPS + g),
                  _flat_spec(lambda g: 2 * ATTN_STEPS + g)],
        out_specs=[_flat_spec(lambda g: g), PER_HEAD_SPEC, PER_HEAD_SPEC],
        out_shape=[jax.ShapeDtypeStruct((n_seq, SEQ, D_MODEL), F32)] + [jax.ShapeDtypeStruct((n_seq, SEQ, LANES), F32)] * 2,
        compiler_params=_params(("parallel", "arbitrary")),
    )(flat, flat, flat)
    return [o.reshape(n_seq, d, l, D_MODEL), m.reshape(n_seq, d, l, LANES), l_sum.reshape(n_seq, d, l, LANES)]


def _attn_merge(branches, name):
    n_seq = branches[0][0].shape[0]
    t = n_seq * SEQ
    tm = MERGE_TILE

    def body(*refs):
        ins = refs[:9]
        o_ref, ob_ref, lse_ref = refs[9:12]
        nat = refs[12:]
        for b, d in enumerate(C_DILATIONS[1:]):
            for k in range(3):
                _load_dilated(ins[3 + 3 * b + k], d, nat[3 * b + k])
        ms = [ins[1][0, 0], nat[1][0], nat[4][0]]
        ls = [ins[2][0, 0], nat[2][0], nat[5][0]]
        m_all = jnp.maximum(jnp.maximum(ms[0], ms[1]), ms[2])
        ws = [jnp.exp(ms[b] - m_all) for b in range(3)]
        lane = lax.broadcasted_iota(jnp.int32, m_all.shape, 1)
        total = jnp.where(lane < C_HEADS, ws[0] * ls[0] + ws[1] * ls[1] + ws[2] * ls[2], 1.0)
        lse_ref[...] = m_all + jnp.log(total)
        first_head = lane < C_HEAD_DIM
        for p in range(LANE_GROUPS):
            lanes = slice(p * LANES, (p + 1) * LANES)
            spread = lambda c: jnp.where(first_head, c[:, 2 * p:2 * p + 1], c[:, 2 * p + 1:2 * p + 2])
            os_ = [ins[0][0, 0, :, lanes], nat[0][p], nat[3][p]]
            o = (spread(ws[0]) * os_[0] + spread(ws[1]) * os_[1] + spread(ws[2]) * os_[2]) / spread(total)
            o_ref[:, lanes] = o
            ob_ref[:, lanes] = o.astype(BF16)

    row = pl.BlockSpec((tm, D_MODEL), lambda i: (i, 0))
    flat = [a for br in branches for a in br]
    in_specs = []
    for wide, narrow in zip(_dilated_specs(tm, D_MODEL, lambda: 0), _dilated_specs(tm, LANES, lambda: 0)):
        in_specs += [wide, narrow, narrow]
    per_head = pltpu.VMEM((1, tm, LANES), F32)
    return pl.pallas_call(
        body, name=name, grid=(t // tm,), in_specs=in_specs,
        out_specs=[row, row, pl.BlockSpec((tm, LANES), lambda i: (i, 0))],
        out_shape=[jax.ShapeDtypeStruct((t, D_MODEL), F32), jax.ShapeDtypeStruct((t, D_MODEL), BF16),
                   jax.ShapeDtypeStruct((t, LANES), F32)],
        scratch_shapes=[pltpu.VMEM((LANE_GROUPS, tm, LANES), F32), per_head, per_head] * 2,
        compiler_params=_params(("parallel",)),
    )(*flat)


def _attn_branch_bwd(qkv, dout, lse, delta, name):
    n_seq, d, l, _ = qkv.shape
    flat = lambda a: a.reshape(n_seq, SEQ, a.shape[-1])

    def body(q_ref, k_ref, v_ref, do_ref, lse_nat_ref, dl_nat_ref, dq_ref, dk_ref, dv_ref, lse_ref, dl_ref,
             dkt_ref, dvt_ref):
        heads = _head_masks()
        g = pl.program_id(1)
        dkt_ref[...] = jnp.zeros_like(dkt_ref)
        dvt_ref[...] = jnp.zeros_like(dvt_ref)
        for nat_ref, dst_ref in ((lse_nat_ref, lse_ref), (dl_nat_ref, dl_ref)):
            for r in range(d):
                rows = pl.ds(r, l, stride=d) if d > 1 else slice(None)
                dst_ref[r * l:(r + 1) * l, :] = nat_ref.at[0][rows, :]

        def block(i, carry):
            rows, keys, mask = _attn_window(i, d)
            lse_b, dl_b = lse_ref[rows, :], dl_ref[rows, :]
            for pair in range(ATTN_PAIRS):
                lanes = slice(pair * LANES, (pair + 1) * LANES)
                q, do = q_ref[0, rows, lanes], do_ref[0, rows, lanes]
                k, v = k_ref[0, keys, lanes], v_ref[0, keys, lanes]
                dq, dk, dv = [], None, None
                for hh, hm in enumerate(heads):
                    head = 2 * (g * ATTN_PAIRS + pair) + hh
                    qh, doh = jnp.where(hm, q, 0), jnp.where(hm, do, 0)
                    s = jnp.where(mask, _dot_nt(qh, k), NEG)
                    p = jnp.exp(s - _get_head(lse_b, head))
                    ds = (p * (_dot_nt(doh, v) - _get_head(dl_b, head))).astype(BF16)
                    dq.append(_dot(ds, k) * QK_SCALE)
                    dk_h, dv_h = _dot_tn(qh, ds), _dot_tn(doh, p.astype(BF16))
                    dk = dk_h if dk is None else dk + dk_h
                    dv = dv_h if dv is None else dv + dv_h
                dq_ref[0, rows, lanes] = jnp.where(heads[0], dq[0], dq[1]).astype(BF16)
                dkt_ref[lanes, keys] += dk
                dvt_ref[lanes, keys] += dv
            return carry

        lax.fori_loop(0, Q_BLOCKS, block, 0, unroll=ATTN_UNROLL)
        for c in range(SEQ // ROW_TILE):
            rows = slice(c * ROW_TILE, (c + 1) * ROW_TILE)
            dk_ref[0, rows, :] = dkt_ref[:, rows].T.astype(BF16)
            dv_ref[0, rows, :] = dvt_ref[:, rows].T.astype(BF16)

    act = _flat_spec(lambda g: g)
    outs = pl.pallas_call(
        body, name=name, grid=(n_seq, ATTN_STEPS),
        in_specs=[_flat_spec(lambda g: g), _flat_spec(lambda g: ATTN_STEPS + g),
                  _flat_spec(lambda g: 2 * ATTN_STEPS + g), act, PER_HEAD_SPEC, PER_HEAD_SPEC],
        out_specs=[act] * 3,
        out_shape=[jax.ShapeDtypeStruct((n_seq, SEQ, D_MODEL), BF16)] * 3,
        scratch_shapes=[pltpu.VMEM((SEQ, LANES), F32)] * 2 + [pltpu.VMEM((ATTN_PAIRS * LANES, SEQ), F32)] * 2,
        compiler_params=_params(("parallel", "parallel")),
    )(flat(qkv), flat(qkv), flat(qkv), flat(dout), lse, delta)
    return [o.reshape(n_seq, d, l, D_MODEL) for o in outs]


def _attn_combine_bwd(grads, rope, name):
    n_seq = grads[0][0].shape[0]
    t = n_seq * SEQ
    tm = MERGE_TILE

    def body(*refs):
        c_ref, s_ref, o_ref, nat4_ref, nat16_ref = refs[9:]
        for sec in range(3):
            _load_dilated(refs[3 + sec], 4, nat4_ref)
            _load_dilated(refs[6 + sec], 16, nat16_ref)
            for p in range(LANE_GROUPS):
                blk = refs[sec][0, 0, :, p * LANES:(p + 1) * LANES] + nat4_ref[p] + nat16_ref[p]
                if sec < 2:
                    blk = blk * c_ref[...] - _swap_halves(blk) * s_ref[...]
                o_ref[:, sec * D_MODEL + p * LANES:sec * D_MODEL + (p + 1) * LANES] = blk.astype(BF16)

    tab = pl.BlockSpec((tm, LANES), lambda i: (i, 0))
    flat = [a for br in grads for a in br]
    in_specs = []
    for spec in _dilated_specs(tm, D_MODEL, lambda: 0):
        in_specs += [spec] * 3
    return pl.pallas_call(
        body, name=name, grid=(t // tm,), in_specs=in_specs + [tab, tab],
        out_specs=pl.BlockSpec((tm, ODD_IN), lambda i: (i, 0)),
        out_shape=jax.ShapeDtypeStruct((t, ODD_IN), BF16),
        scratch_shapes=[pltpu.VMEM((LANE_GROUPS, tm, LANES), F32)] * 2,
        compiler_params=_params(("parallel",)),
    )(*flat, *rope)


def _adamw(w, g, m, v):
    m = ADAM_B1 * m + (1.0 - ADAM_B1) * g
    v = ADAM_B2 * v + (1.0 - ADAM_B2) * jnp.square(g)
    m_hat = m / (1.0 - ADAM_B1 ** ADAM_STEP)
    v_hat = v / (1.0 - ADAM_B2 ** ADAM_STEP)
    delta = -ADAM_LR * (m_hat / (jnp.sqrt(v_hat) + ADAM_EPS) + ADAM_WD * w)
    return delta, m, v


def _adamw_sharded(parts, w, m, v, after, name):
    n_layers, rows, cols = w.shape
    tr = min(rows, 256)

    def body(*refs):
        p_refs = refs[:n_layers]
        w_ref, m_ref, v_ref, _, g_ref, d_ref, mo_ref, vo_ref = refs[n_layers:]
        layer = pl.program_id(0)
        g = None
        for l, p_ref in enumerate(p_refs):
            g_l = p_ref[0].astype(F32)
            for s in range(1, N_DEV):
                g_l = g_l + p_ref[s].astype(F32)
            g = g_l if g is None else jnp.where(layer == l, g_l, g)
        delta, mn, vn = _adamw(w_ref[0], g, m_ref[0], v_ref[0])
        g_ref[0] = g
        d_ref[0] = delta
        mo_ref[0] = mn
        vo_ref[0] = vn

    def part_spec(l):
        return pl.BlockSpec((N_DEV, tr, cols), lambda a, i: (0, jnp.where(a == l, i, 0), 0))

    row = pl.BlockSpec((1, tr, cols), lambda a, i: (a, i, 0))
    return pl.pallas_call(
        body, name=name, grid=(n_layers, rows // tr),
        in_specs=[part_spec(l) for l in range(n_layers)] + [row, row, row, pl.BlockSpec(memory_space=pl.ANY)],
        out_specs=[row] * 4, out_shape=[jax.ShapeDtypeStruct(w.shape, F32)] * 4,
        compiler_params=_params(("arbitrary", "arbitrary")),
    )(*parts, w, m, v, after)


def _small_update(gathered, where, weights, moments_m, moments_v, lb_index, name):
    n = len(weights)
    n_g = len(gathered)

    def body(*refs):
        g_refs = refs[:n_g]
        w_refs, m_refs, v_refs = refs[n_g:n_g + n], refs[n_g + n:n_g + 2 * n], refs[n_g + 2 * n:n_g + 3 * n]
        outs = refs[n_g + 3 * n:]

        def total(k):
            array, rows, lanes = where[k]
            ref = g_refs[array]
            index = (slice(None),) * (len(ref.shape) - 1) if rows is None else (rows, lanes)
            acc = ref[(0,) + index]
            for s in range(1, N_DEV):
                acc = acc + ref[(s,) + index]
            return acc

        loss_rows = total(n)
        outs[0][...] = jnp.sum(jnp.sum(loss_rows, axis=1, keepdims=True), axis=0, keepdims=True)
        for k in range(n):
            part = total(k)
            if k == lb_index:
                dlb = jnp.sum(part, axis=0, keepdims=True)
                tab = w_refs[k][...]
                e = jnp.exp(tab - jnp.max(tab, axis=0, keepdims=True))
                p = e / jnp.sum(e, axis=0, keepdims=True)
                first = lax.broadcasted_iota(jnp.int32, p.shape, 0) == 0
                grads = [(slice(None), p * (jnp.where(first, dlb, 0.0) - p[0:1, :] * dlb))]
            elif part.shape == w_refs[k].shape:
                grads = [(slice(None), part)]
            else:
                grads = [(slice(l, l + 1), jnp.sum(part[l * SUBLANES:(l + 1) * SUBLANES], axis=0, keepdims=True))
                         for l in range(w_refs[k].shape[0])]
            for rows, g in grads:
                delta, mn, vn = _adamw(w_refs[k][rows], g, m_refs[k][rows], v_refs[k][rows])
                outs[1 + 4 * k][rows] = g
                outs[2 + 4 * k][rows] = delta
                outs[3 + 4 * k][rows] = mn
                outs[4 + 4 * k][rows] = vn

    vmem = pl.BlockSpec(memory_space=pltpu.VMEM)
    out_shape = [jax.ShapeDtypeStruct((1, 1), F32)]
    for w in weights:
        out_shape += [jax.ShapeDtypeStruct(w.shape, F32)] * 4
    args = list(gathered) + list(weights) + list(moments_m) + list(moments_v)
    return pl.pallas_call(
        body, name=name, in_specs=[vmem] * len(args), out_specs=[vmem] * len(out_shape), out_shape=out_shape,
        compiler_params=pltpu.CompilerParams(vmem_limit_bytes=VMEM_LIMIT),
    )(*args)


def kernel(x, positions, norm_mix_pre, norm_mix_post, norm_ffn_pre, norm_ffn_post, w_in_even, lb_table, a_norm, b_ln_g, b_ln_b, b_ws, b_bias, w_out_even, w_in_odd, w_out_odd, w_ff1, w_ff2, loss_target, m_norm_mix_pre, m_norm_mix_post, m_norm_ffn_pre, m_norm_ffn_post, m_w_in_even, m_lb_table, m_a_norm, m_b_ln_g, m_b_ln_b, m_b_ws, m_b_bias, m_w_out_even, m_w_in_odd, m_w_out_odd, m_w_ff1, m_w_ff2, v_norm_mix_pre, v_norm_mix_post, v_norm_ffn_pre, v_norm_ffn_post, v_w_in_even, v_lb_table, v_a_norm, v_b_ln_g, v_b_ln_b, v_b_ws, v_b_bias, v_w_out_even, v_w_in_odd, v_w_out_odd, v_w_ff1, v_w_ff2):
    n_seq = x.shape[0]
    t = n_seq * SEQ
    x0 = x.reshape(t, D_MODEL)
    target = loss_target.reshape(t, D_MODEL)

    me = _my_slot().astype(jnp.int32).reshape(1)

    order = ["in_e", "out_e", "ff1_0", "ff2_0", "in_o", "out_o", "ff1_1", "ff2_1"]
    shards = dict(in_e=w_in_even[0], out_e=w_out_even[0], in_o=w_in_odd[0], out_o=w_out_odd[0],
                  ff1_0=w_ff1[0], ff1_1=w_ff1[1], ff2_0=w_ff2[0], ff2_1=w_ff2[1])
    by_columns = ("in_e", "in_o", "ff1_0", "ff1_1")

    def place(k, after):
        if k in by_columns:
            return _place_own_columns(shards[k], me, "place_" + k, after)
        return _place_own(shards[k], me, "place_" + k, False, after=after)

    gathers = {}
    send0, recv0, land0, _, token0 = _exchange_start([place(order[0], None)], [None], "gather_start_first")
    gathers[order[0]] = (land0[0], send0[0], recv0[0])
    sends, recvs, lands, _, g_token = _exchange_start([place(k, token0) for k in order[1:]],
                                                      [None] * (len(order) - 1), "gather_start")
    for k, land, send, recv in zip(order[1:], lands, sends, recvs):
        gathers[k] = (land, send, recv)

    def get_w(keys, after):
        lands_k, sends_k, recvs_k = zip(*[gathers[k] for k in keys])
        return _exchange_wait(list(lands_k), [None] * len(keys), list(sends_k), list(recvs_k), after,
                              "gather_wait_" + keys[0])

    sent = {}

    def put_g(group, blocks):
        keys = list(blocks)
        own = [_place_own(blocks[k], me, "own_" + k, True) for k in keys]
        send_sems, recv_sems, own, srcs, token = _exchange_start(own, [blocks[k] for k in keys], "scatter_start_" + group)
        sent[group] = (keys, own, srcs, send_sems, recv_sems)
        return token

    rope = _rope_tables(positions)
    bias_t = b_bias[0].T
    grads = _local_step(x0, target, rope, norm_mix_pre, norm_mix_post, norm_ffn_pre, norm_ffn_post, lb_table,
                        a_norm, b_ln_g, b_ln_b, b_ws[0], bias_t, get_w, put_g, g_token)
    (dx0, loss_part, dg_mix_pre, dg_mix_post, dg_ffn_pre, dg_ffn_post, d_lb, d_a_norm, d_ln_g, d_ln_b, d_ws,
     d_bias_t) = grads

    packed = jnp.concatenate([dg_mix_pre, dg_mix_post, dg_ffn_pre, dg_ffn_post,
                              jnp.concatenate([d_lb, d_a_norm], axis=1), jnp.concatenate([d_ln_g, d_ln_b], axis=1),
                              loss_part], axis=0)
    small_lands = [_place_own(a, me, "own_small%d" % k, False, F32) for k, a in enumerate((packed, d_ws, d_bias_t))]
    s_send, s_recv, small_lands, _, after = _exchange_start(small_lands, [None] * 3, "gather_small_start")

    big = dict(w_in_even=(["in_e"], w_in_even, m_w_in_even, v_w_in_even),
               w_out_even=(["out_e"], w_out_even, m_w_out_even, v_w_out_even),
               w_in_odd=(["in_o"], w_in_odd, m_w_in_odd, v_w_in_odd),
               w_out_odd=(["out_o"], w_out_odd, m_w_out_odd, v_w_out_odd),
               w_ff1=(["ff1_0", "ff1_1"], w_ff1, m_w_ff1, v_w_ff1), w_ff2=(["ff2_0", "ff2_1"], w_ff2, m_w_ff2, v_w_ff2))
    recv, big_out = {}, {}
    for groups, names in ((("ffn1", "ffn0"), ("w_ff1", "w_ff2")), (("mix1",), ("w_in_odd", "w_out_odd")),
                          (("mix0",), ("w_in_even", "w_out_even"))):
        for group in groups:
            keys, own, srcs, send_sems, recv_sems = sent[group]
            recv.update(zip(keys, _exchange_wait(own, srcs, send_sems, recv_sems, after, "scatter_wait_" + group)))
        for nm in names:
            keys, w, m, v = big[nm]
            big_out[nm] = _adamw_sharded([recv[k] for k in keys], w, m, v, after, "adamw_" + nm)
            after = big_out[nm][0]
    big_out = [big_out[nm] for nm in ("w_in_even", "w_out_even", "w_in_odd", "w_out_odd", "w_ff1", "w_ff2")]
    gathered = _exchange_wait(small_lands, [None] * 3, s_send, s_recv, after, "gather_small_wait")
    rows8 = lambda k: slice(SUBLANES * k, SUBLANES * (k + 1))
    left, right, every = slice(0, A_WIDTH), slice(A_WIDTH, 2 * A_WIDTH), slice(None)
    where = [(0, slice(0, 16), every), (0, slice(16, 32), every), (0, slice(32, 48), every), (0, slice(48, 64), every),
             (0, rows8(8), left), (0, rows8(8), right), (0, rows8(9), left), (0, rows8(9), right),
             (1, None, None), (2, None, None), (0, rows8(10), every)]
    small_w = [norm_mix_pre, norm_mix_post, norm_ffn_pre, norm_ffn_post, lb_table, a_norm, b_ln_g, b_ln_b,
               b_ws[0], bias_t]
    small_m = [m_norm_mix_pre, m_norm_mix_post, m_norm_ffn_pre, m_norm_ffn_post, m_lb_table, m_a_norm, m_b_ln_g,
               m_b_ln_b, m_b_ws[0], m_b_bias[0].T]
    small_v = [v_norm_mix_pre, v_norm_mix_post, v_norm_ffn_pre, v_norm_ffn_post, v_lb_table, v_a_norm, v_b_ln_g,
               v_b_ln_b, v_b_ws[0], v_b_bias[0].T]
    small_out = _small_update(gathered, where, small_w, small_m, small_v, 4, "small_update")
    loss = small_out[0].reshape(())
    small = [small_out[1 + 4 * k:5 + 4 * k] for k in range(len(small_w))]
    small[8] = [a[None] for a in small[8]]
    small[9] = [a.T[None] for a in small[9]]

    per_weight = small[0:4] + [big_out[0]] + small[4:10] + big_out[1:6]
    grad_x = dx0.reshape(x.shape)
    out = [loss, grad_x]
    for kind in range(4):
        out += [p[kind] for p in per_weight]
    return tuple(out)


def _local_step(x0, target, rope, norm_mix_pre, norm_mix_post, norm_ffn_pre, norm_ffn_post, lb_table, a_norm,
                b_ln_g, b_ln_b, ws, bias_t, get_w, put_g, token):
    def gain(a, l, tok):
        return a[l:l + 1] if tok is None else a[l:l + 1] + tok[0:1, 0:1]

    full = lambda a: a.reshape(-1, D_MODEL)
    owners = lambda a: a.reshape((N_DEV, -1) + a.shape[1:])

    (g_in_e,) = get_w(["in_e"], token)
    proj, h_mix0 = _norm_inproj(x0, gain(norm_mix_pre, 0, token), g_in_e, "inproj_even")
    mixed, pre_a, states = _hgrn2_fwd(proj, lb_table, a_norm, "hgrn2_fwd")
    mixed = _gmlp_fwd(proj, mixed, b_ln_g, b_ln_b, ws, bias_t, "gmlp_fwd")
    w_out_e = full(get_w(["out_e"], mixed)[0])
    x1, mix0 = _outproj([mixed], w_out_e, x0, gain(norm_mix_post, 0, None), "outproj_even")
    w1_0, w2_0 = get_w(["ff1_0", "ff2_0"], x1)
    w2_0 = full(w2_0)
    x2, y0, h_ffn0, r0 = _ffn_fwd(x1, gain(norm_ffn_pre, 0, None), w1_0, w2_0, gain(norm_ffn_post, 0, None), "ffn_fwd_0")
    (g_in_o,) = get_w(["in_o"], x2)
    *qkv, h_mix1 = _norm_inproj_rope(x2, gain(norm_mix_pre, 1, None), g_in_o, rope, "inproj_odd")
    branches = [_attn_branch_fwd(a, "attn_fwd_d%d" % d) for a, d in zip(qkv, C_DILATIONS)]
    attn, attn_b, lse = _attn_merge(branches, "attn_merge")
    w_out_o = full(get_w(["out_o"], attn_b)[0])
    x3, mix1 = _outproj([attn_b], w_out_o, x2, gain(norm_mix_post, 1, None), "outproj_odd")
    w1_1, w2_1 = get_w(["ff1_1", "ff2_1"], x3)
    w2_1 = full(w2_1)
    dx4, y1, h_ffn1, r1, loss_part = _ffn_fwd(x3, gain(norm_ffn_pre, 1, None), w1_1, w2_1, gain(norm_ffn_post, 1, None),
                                              "ffn_fwd_1", target)

    dx3, dy1, da1, dg_ffn_pre1, dg_ffn_post1 = _ffn_bwd(
        dx4, x3, y1, r1, gain(norm_ffn_pre, 1, None), w1_1, w2_1, gain(norm_ffn_post, 1, None), "ffn_bwd_1")
    gw_ff1_1 = _grad_w(h_ffn1, da1, True, "grad_w_ff1_1")
    gw_ff2_1 = _grad_w(r1, dy1, False, "grad_w_ff2_1")
    tok = put_g("ffn1", dict(ff1_1=gw_ff1_1, ff2_1=owners(gw_ff2_1)))
    *dattn, delta, dz1, dg_mix_post1 = _outproj_bwd_attn(dx3, mix1, gain(norm_mix_post, 1, tok), w_out_o, attn,
                                                  "outproj_bwd_odd")
    gw_out_o = _grad_w(attn_b, dz1, False, "grad_w_out_odd")
    per_seq = lambda a: a.reshape(-1, SEQ, LANES)
    grads_c = [_attn_branch_bwd(qkv[b], dattn[b], per_seq(lse), per_seq(delta), "attn_bwd_d%d" % d)
               for b, d in enumerate(C_DILATIONS)]
    dqkv = _attn_combine_bwd(grads_c, rope, "attn_combine_bwd")
    gw_in_o = _grad_w(h_mix1, dqkv, True, "grad_w_in_odd")
    tok = put_g("mix1", dict(out_o=owners(gw_out_o), in_o=gw_in_o))
    dx2, dg_mix_pre1 = _inproj_bwd(dqkv, g_in_o, dx3, x2, gain(norm_mix_pre, 1, tok), "inproj_bwd_odd")

    dx1, dy0, da0, dg_ffn_pre0, dg_ffn_post0 = _ffn_bwd(
        dx2, x1, y0, r0, gain(norm_ffn_pre, 0, None), w1_0, w2_0, gain(norm_ffn_post, 0, None), "ffn_bwd_0")
    gw_ff1_0 = _grad_w(h_ffn0, da0, True, "grad_w_ff1_0")
    gw_ff2_0 = _grad_w(r0, dy0, False, "grad_w_ff2_0")
    tok = put_g("ffn0", dict(ff1_0=gw_ff1_0, ff2_0=owners(gw_ff2_0)))
    dcat, dz0, dg_mix_post0 = _outproj_bwd(dx1, mix0, gain(norm_mix_post, 0, tok), w_out_e, "outproj_bwd_even")
    gw_out_e = _grad_w(mixed, dz0, False, "grad_w_out_even")
    dproj, d_lb, d_a_norm = _hgrn2_bwd(proj, dcat, pre_a, states, lb_table, a_norm, "hgrn2_bwd")
    dproj, d_ln_g, d_ln_b, d_ws, d_bias_t = _gmlp_bwd(proj, dcat, dproj, b_ln_g, b_ln_b, ws, bias_t, "gmlp_bwd")
    gw_in_e = _grad_w(h_mix0, dproj, True, "grad_w_in_even")
    tok = put_g("mix0", dict(out_e=owners(gw_out_e), in_e=gw_in_e))
    dx0, dg_mix_pre0 = _inproj_bwd(dproj, g_in_e, dx1, x0, gain(norm_mix_pre, 0, tok), "inproj_bwd_even")

    layers = lambda a, b: jnp.concatenate([a, b], axis=0)
    return (dx0, loss_part, layers(dg_mix_pre0, dg_mix_pre1), layers(dg_mix_post0, dg_mix_post1),
            layers(dg_ffn_pre0, dg_ffn_pre1), layers(dg_ffn_post0, dg_ffn_post1),
            d_lb, d_a_norm, d_ln_g, d_ln_b, d_ws, d_bias_t)
```

```python
import math

import jax
import jax.numpy as jnp
from jax import lax
from jax.experimental import pallas as pl
from jax.experimental.pallas import tpu as pltpu

F32 = jnp.float32
BF16 = jnp.bfloat16
MESH = pl.DeviceIdType.MESH

N_DEV = 8
D_MODEL = 1024
SEQ = 2048
EPS = 1e-6
A_WIDTH = 512
A_HEADS = 4
HEAD_A = 128
B_WIDTH = 512
B_GROUPS = 4
B_CHUNK = 128
C_HEADS = 16
C_HEAD_DIM = 64
C_ROT_HALF = 8
ROPE_THETA = 500000.0
C_DILATIONS = (1, 4, 16)
C_BLOCK = 128
D_FF = 4096
EVEN_IN = 3072
ODD_IN = 3072

ADAM_LR = 0.001
ADAM_B1 = 0.9
ADAM_B2 = 0.999
ADAM_EPS = 1e-08
ADAM_WD = 0.01
ADAM_STEP = 10

LANES = 128
SUBLANES = 8
ROW_TILE = 512
PROJ_TILE = 1024
PROJ_COLS = 768
MERGE_TILE = 256
SUB_CHUNK = 16
HGRN_BLOCK = 256
NEG = -1e30
VMEM_LIMIT = 56 * 1024 * 1024


def _params(sem):
    return pltpu.CompilerParams(dimension_semantics=sem, vmem_limit_bytes=VMEM_LIMIT)


def _dot(a, b):
    return jnp.dot(a, b, preferred_element_type=F32)


def _dot_nt(a, b):
    return lax.dot_general(a, b, (((1,), (1,)), ((), ())), preferred_element_type=F32)


def _dot_tn(a, b):
    return lax.dot_general(a, b, (((0,), (0,)), ((), ())), preferred_element_type=F32)


def _rms(x, g):
    r = lax.rsqrt(jnp.mean(x * x, axis=-1, keepdims=True) + EPS)
    return x * r * g


def _rms_bwd(x, g, dy):
    r = lax.rsqrt(jnp.mean(x * x, axis=-1, keepdims=True) + EPS)
    dyg = dy * g
    dx = r * dyg - x * (r * r * r) * jnp.mean(x * dyg, axis=-1, keepdims=True)
    return dx, dy * x * r


def _split3(x):
    hi = x.astype(BF16)
    rest = x - hi.astype(F32)
    mid = rest.astype(BF16)
    return hi, mid, (rest - mid.astype(F32)).astype(BF16)


def _mask_dot(mask, x):
    m = mask.astype(BF16)
    hi, mid, lo = _split3(x)
    return _dot(m, hi) + (_dot(m, mid) + _dot(m, lo))


def _dot_mask(x, mask):
    m = mask.astype(BF16)
    hi, mid, lo = _split3(x)
    return _dot(hi, m) + (_dot(mid, m) + _dot(lo, m))


def _rows8(v):
    return v.reshape(v.shape[0] // SUBLANES, SUBLANES, v.shape[1]).sum(axis=0)


def _sigmoid(x):
    return 1.0 / (1.0 + jnp.exp(-x))


def _gelu(x):
    return 0.5 * x * (1.0 + jnp.tanh(math.sqrt(2.0 / math.pi) * (x + 0.044715 * (x * x * x))))


def _acc_rows8(ref, val, first):
    @pl.when(first)
    def _():
        ref[...] = val

    @pl.when(jnp.logical_not(first))
    def _():
        ref[...] += val


def _my_slot():
    return 4 * lax.axis_index("x") + 2 * lax.axis_index("y") + lax.axis_index("c")


def _peer(r):
    x, y, c = lax.axis_index("x"), lax.axis_index("y"), lax.axis_index("c")
    px = 1 - x if (r >> 2) & 1 else x
    py = 1 - y if (r >> 1) & 1 else y
    pc = 1 - c if r & 1 else c
    return (px, py, pc), 4 * px + 2 * py + pc


HBM_SPEC = pl.BlockSpec(memory_space=pltpu.HBM)
SEM_SPEC = pl.BlockSpec(memory_space=pltpu.SEMAPHORE)
SPLIT_EFFECT = pltpu.SideEffectType.DATAFLOW_SIDE_EFFECTING


def _split_copies(land_ref, src_ref, send_sem, recv_sem):
    me = _my_slot()
    copies = []
    for r in range(1, N_DEV):
        peer, slot = _peer(r)
        src = _slot(land_ref, me) if src_ref is None else _slot(src_ref, slot)
        copies.append(pltpu.make_async_remote_copy(
            src_ref=src, dst_ref=_slot(land_ref, me), send_sem=send_sem, recv_sem=recv_sem,
            device_id=peer, device_id_type=MESH))
    return copies


def _slot(ref, s):
    if len(ref.shape) == 2:
        c = ref.shape[1] // N_DEV
        return ref.at[:, pl.ds(pl.multiple_of(s * c, LANES), c)]
    return ref.at[s]


def _exchange_start(lands, sources, name):
    n = len(lands)
    given = [s for s in sources if s is not None]
    arrays = list(lands) + given

    def body(*refs):
        land_refs, src_refs = refs[:n], list(refs[n:n + len(given)])
        sems = refs[len(arrays):len(arrays) + 2 * n]
        token = refs[-1]
        for k in range(n):
            src_ref = None if sources[k] is None else src_refs.pop(0)
            for copy in _split_copies(land_refs[k], src_ref, sems[k], sems[n + k]):
                copy.start()
        token[...] = jnp.zeros_like(token)

    outs = pl.pallas_call(
        body, name=name,
        out_shape=(pltpu.SemaphoreType.DMA(()),) * (2 * n) + tuple(pltpu.HBM(a.shape, a.dtype) for a in arrays)
        + (jax.ShapeDtypeStruct((SUBLANES, LANES), F32),),
        in_specs=[HBM_SPEC] * len(arrays),
        out_specs=(SEM_SPEC,) * (2 * n) + (HBM_SPEC,) * len(arrays) + (pl.BlockSpec(memory_space=pltpu.VMEM),),
        input_output_aliases={i: 2 * n + i for i in range(len(arrays))},
        compiler_params=pltpu.CompilerParams(has_side_effects=SPLIT_EFFECT),
    )(*[pltpu.with_memory_space_constraint(a, pltpu.HBM) for a in arrays])
    return list(outs[:n]), list(outs[n:2 * n]), list(outs[2 * n:3 * n]), list(outs[3 * n:-1]), outs[-1]


def _exchange_wait(lands, sources, send_sems, recv_sems, after, name):
    n = len(lands)
    given = [s for s in sources if s is not None]
    arrays = list(lands) + given

    def body(*refs):
        land_refs, src_refs = refs[:n], list(refs[n:n + len(given)])
        sems = refs[len(arrays):len(arrays) + 2 * n]
        for i in range(n):
            src_ref = None if sources[i] is None else src_refs.pop(0)
            copies = _split_copies(land_refs[i], src_ref, sems[i], sems[n + i])
            for copy in copies:
                copy.wait_recv()
            for copy in copies:
                copy.wait_send()

    outs = pl.pallas_call(
        body, name=name, out_shape=tuple(pltpu.HBM(a.shape, a.dtype) for a in arrays),
        in_specs=[HBM_SPEC] * len(arrays) + [SEM_SPEC] * (2 * n) + [pl.BlockSpec(memory_space=pl.ANY)],
        out_specs=(HBM_SPEC,) * len(arrays),
        input_output_aliases={i: i for i in range(len(arrays))},
        compiler_params=pltpu.CompilerParams(has_side_effects=SPLIT_EFFECT),
    )(*arrays, *send_sems, *recv_sems, after)
    return list(outs[:n])


def _place_own(a, me, name, own_block, dtype=BF16, after=None):
    shape = a.shape[1:] if own_block else a.shape
    cols = shape[-1]
    a3 = a.reshape((N_DEV if own_block else 1, -1, cols))
    rows = a3.shape[1]
    tr = min(rows, 512)

    def body(me_ref, a_ref, _, o_ref):
        o_ref[...] = a_ref[...].astype(dtype)

    grid_spec = pltpu.PrefetchScalarGridSpec(
        num_scalar_prefetch=1, grid=(rows // tr,),
        in_specs=[pl.BlockSpec((1, tr, cols), lambda i, me_ref: (me_ref[0] if own_block else 0, i, 0)),
                  pl.BlockSpec(memory_space=pl.ANY)],
        out_specs=pl.BlockSpec((1, tr, cols), lambda i, me_ref: (me_ref[0], i, 0)))
    out = pl.pallas_call(
        body, name=name, grid_spec=grid_spec, out_shape=jax.ShapeDtypeStruct((N_DEV, rows, cols), dtype),
        compiler_params=_params(("arbitrary",)),
    )(me, a3, a3 if after is None else after)
    return out.reshape((N_DEV,) + shape)


def _place_own_columns(a, me, name, after=None):
    rows, cols = a.shape
    tr = min(rows, 512)

    def body(me_ref, a_ref, _, o_ref):
        o_ref[...] = a_ref[...].astype(BF16)

    grid_spec = pltpu.PrefetchScalarGridSpec(
        num_scalar_prefetch=1, grid=(rows // tr,),
        in_specs=[pl.BlockSpec((tr, cols), lambda i, me_ref: (i, 0)), pl.BlockSpec(memory_space=pl.ANY)],
        out_specs=pl.BlockSpec((tr, cols), lambda i, me_ref: (i, me_ref[0])))
    return pl.pallas_call(
        body, name=name, grid_spec=grid_spec, out_shape=jax.ShapeDtypeStruct((rows, N_DEV * cols), BF16),
        compiler_params=_params(("arbitrary",)),
    )(me, a, a if after is None else after)


def _rope_tables(positions):
    in_head = jnp.arange(LANES) % C_HEAD_DIM
    inv = ROPE_THETA ** (-(in_head % C_ROT_HALF).astype(F32) / C_ROT_HALF)
    ang = positions.reshape(-1)[:, None].astype(F32) * inv
    rotated = in_head < 2 * C_ROT_HALF
    sin = jnp.sin(ang)
    return (jnp.where(rotated, jnp.cos(ang), 1.0),
            jnp.where(in_head < C_ROT_HALF, -sin, jnp.where(rotated, sin, 0.0)))


def _swap_halves(x):
    lane = lax.broadcasted_iota(jnp.int32, x.shape, 1) % C_HEAD_DIM
    return jnp.where(lane < C_ROT_HALF, pltpu.roll(x, LANES - C_ROT_HALF, 1), pltpu.roll(x, C_ROT_HALF, 1))


def _norm_inproj(x, g, w, name):
    t = x.shape[0]
    n = w.shape[1]
    tm, tn = ROW_TILE, n

    def body(x_ref, g_ref, w_ref, o_ref, h_ref):
        @pl.when(pl.program_id(1) == 0)
        def _():
            h_ref[...] = _rms(x_ref[...], g_ref[...]).astype(BF16)

        o_ref[...] = _dot(h_ref[...], w_ref[...])

    return pl.pallas_call(
        body, name=name, grid=(t // tm, n // tn),
        in_specs=[pl.BlockSpec((tm, D_MODEL), lambda i, j: (i, 0)), pl.BlockSpec((1, D_MODEL), lambda i, j: (0, 0)),
                  pl.BlockSpec((D_MODEL, tn), lambda i, j: (0, j))],
        out_specs=[pl.BlockSpec((tm, tn), lambda i, j: (i, j)), pl.BlockSpec((tm, D_MODEL), lambda i, j: (i, 0))],
        out_shape=[jax.ShapeDtypeStruct((t, n), F32), jax.ShapeDtypeStruct((t, D_MODEL), BF16)],
        compiler_params=_params(("parallel", "arbitrary")),
    )(x, g, w)


def _dilated_specs(tm, width, col_of):
    per_seq = SEQ // tm
    specs = []
    for d in C_DILATIONS:
        specs.append(pl.BlockSpec(
            (1, d, tm // d, width), lambda i, *rest: (i // per_seq, 0, i % per_seq, col_of(*rest))))
    return specs


def _dilated_shapes(n_seq, cols, dtype):
    return [jax.ShapeDtypeStruct((n_seq, d, SEQ // d, cols), dtype) for d in C_DILATIONS]


def _store_dilated(src_ref, out_refs, dtype):
    groups, tm, _ = src_ref.shape
    for d, o_ref in zip(C_DILATIONS, out_refs):
        for r in range(d):
            rows = pl.ds(r, tm // d, stride=d) if d > 1 else slice(None)
            for p in range(groups):
                o_ref[0, r, :, p * LANES:(p + 1) * LANES] = src_ref.at[p][rows, :].astype(dtype)


def _load_dilated(in_ref, d, dst_ref):
    groups, tm, _ = dst_ref.shape
    for r in range(d):
        rows = pl.ds(r, tm // d, stride=d)
        for p in range(groups):
            dst_ref.at[p][rows, :] = in_ref[0, r, :, p * LANES:(p + 1) * LANES].astype(F32)


def _norm_inproj_rope(x, g, w, rope, name):
    t = x.shape[0]
    n = w.shape[1]
    tm, nb = ROW_TILE, n

    def body(x_ref, g_ref, w_ref, c_ref, s_ref, o1_ref, o4_ref, o16_ref, h_ref, tile_ref):
        j = pl.program_id(1)

        @pl.when(j == 0)
        def _():
            h_ref[...] = _rms(x_ref[...], g_ref[...]).astype(BF16)

        acc = _dot(h_ref[...], w_ref[...])
        for p in range(nb // LANES):
            blk = acc[:, p * LANES:(p + 1) * LANES]
            roped = blk * c_ref[...] + _swap_halves(blk) * s_ref[...]
            piece = j * (nb // LANES) + p
            is_qk = piece < 2 * (D_MODEL // LANES)
            tile_ref[p] = jnp.where(is_qk, roped, blk) * jnp.where(piece < D_MODEL // LANES, QK_SCALE, 1.0)
        _store_dilated(tile_ref, (o1_ref, o4_ref, o16_ref), BF16)

    return pl.pallas_call(
        body, name=name, grid=(t // tm, n // nb),
        in_specs=[pl.BlockSpec((tm, D_MODEL), lambda i, j: (i, 0)), pl.BlockSpec((1, D_MODEL), lambda i, j: (0, 0)),
                  pl.BlockSpec((D_MODEL, nb), lambda i, j: (0, j)),
                  pl.BlockSpec((tm, LANES), lambda i, j: (i, 0)), pl.BlockSpec((tm, LANES), lambda i, j: (i, 0))],
        out_specs=_dilated_specs(tm, nb, lambda j: j) + [pl.BlockSpec((tm, D_MODEL), lambda i, j: (i, 0))],
        out_shape=_dilated_shapes(t // SEQ, n, BF16) + [jax.ShapeDtypeStruct((t, D_MODEL), BF16)],
        scratch_shapes=[pltpu.VMEM((nb // LANES, tm, LANES), F32)],
        compiler_params=_params(("parallel", "arbitrary")),
    )(x, g, w, *rope)


def _outproj(parts, w, x, g, name):
    t = x.shape[0]
    tm = PROJ_TILE
    n = len(parts)
    widths = [p.shape[1] for p in parts]

    def body(*refs):
        p_refs = refs[:n]
        w_ref, x_ref, g_ref, xo_ref, mix_ref = refs[n:]
        mix = None
        off = 0
        for p_ref, wd in zip(p_refs, widths):
            term = _dot(p_ref[...].astype(BF16), w_ref[off:off + wd, :])
            mix = term if mix is None else mix + term
            off += wd
        mix_ref[...] = mix
        xo_ref[...] = x_ref[...] + _rms(mix, g_ref[...])

    row = lambda i: (i, 0)
    return pl.pallas_call(
        body, name=name, grid=(t // tm,),
        in_specs=[pl.BlockSpec((tm, wd), row) for wd in widths] + [
            pl.BlockSpec((sum(widths), D_MODEL), lambda i: (0, 0)),
            pl.BlockSpec((tm, D_MODEL), row), pl.BlockSpec((1, D_MODEL), lambda i: (0, 0))],
        out_specs=[pl.BlockSpec((tm, D_MODEL), row)] * 2,
        out_shape=[jax.ShapeDtypeStruct((t, D_MODEL), F32)] * 2,
        compiler_params=_params(("parallel",)),
    )(*parts, w, x, g)


def _outproj_bwd(dx, mix, g, w, name):
    t = dx.shape[0]
    tm = PROJ_TILE
    k = w.shape[0]

    def body(dx_ref, mix_ref, g_ref, w_ref, dcat_ref, dz_ref, dg_ref):
        dz, dgr = _rms_bwd(mix_ref[...], g_ref[...], dx_ref[...])
        dzb = dz.astype(BF16)
        dz_ref[...] = dzb
        dcat_ref[...] = _dot_nt(dzb, w_ref[...])
        _acc_rows8(dg_ref, _rows8(dgr), pl.program_id(0) == 0)

    row = lambda i: (i, 0)
    return pl.pallas_call(
        body, name=name, grid=(t // tm,),
        in_specs=[pl.BlockSpec((tm, D_MODEL), row), pl.BlockSpec((tm, D_MODEL), row),
                  pl.BlockSpec((1, D_MODEL), lambda i: (0, 0)), pl.BlockSpec((k, D_MODEL), lambda i: (0, 0))],
        out_specs=[pl.BlockSpec((tm, k), row), pl.BlockSpec((tm, D_MODEL), row),
                   pl.BlockSpec((SUBLANES, D_MODEL), lambda i: (0, 0))],
        out_shape=[jax.ShapeDtypeStruct((t, k), F32), jax.ShapeDtypeStruct((t, D_MODEL), BF16),
                   jax.ShapeDtypeStruct((SUBLANES, D_MODEL), F32)],
        compiler_params=_params(("arbitrary",)),
    )(dx, mix, g, w)


def _outproj_bwd_attn(dx, mix, g, w, out, name):
    t = dx.shape[0]
    tm = MERGE_TILE

    def body(dx_ref, mix_ref, g_ref, w_ref, out_ref, do1, do4, do16, dl_ref, dz_ref, dg_ref, tile_ref):
        dz, dgr = _rms_bwd(mix_ref[...], g_ref[...], dx_ref[...])
        dzb = dz.astype(BF16)
        dz_ref[...] = dzb
        _acc_rows8(dg_ref, _rows8(dgr), pl.program_id(0) == 0)
        dout = _dot_nt(dzb, w_ref[...])
        for p in range(LANE_GROUPS):
            tile_ref[p] = dout[:, p * LANES:(p + 1) * LANES]
        _store_dilated(tile_ref, (do1, do4, do16), BF16)
        column = lax.broadcasted_iota(jnp.int32, (D_MODEL, LANES), 0) // C_HEAD_DIM
        head = lax.broadcasted_iota(jnp.int32, (D_MODEL, LANES), 1)
        dl_ref[...] = _dot_mask(dout * out_ref[...], column == head)

    row = lambda i: (i, 0)
    n_seq = t // SEQ
    return pl.pallas_call(
        body, name=name, grid=(t // tm,),
        in_specs=[pl.BlockSpec((tm, D_MODEL), row), pl.BlockSpec((tm, D_MODEL), row),
                  pl.BlockSpec((1, D_MODEL), lambda i: (0, 0)), pl.BlockSpec((D_MODEL, D_MODEL), lambda i: (0, 0)),
                  pl.BlockSpec((tm, D_MODEL), row)],
        out_specs=_dilated_specs(tm, D_MODEL, lambda: 0) + [
            pl.BlockSpec((tm, LANES), row), pl.BlockSpec((tm, D_MODEL), row),
            pl.BlockSpec((SUBLANES, D_MODEL), lambda i: (0, 0))],
        out_shape=_dilated_shapes(n_seq, D_MODEL, BF16) + [
            jax.ShapeDtypeStruct((t, LANES), F32), jax.ShapeDtypeStruct((t, D_MODEL), BF16),
            jax.ShapeDtypeStruct((SUBLANES, D_MODEL), F32)],
        scratch_shapes=[pltpu.VMEM((LANE_GROUPS, tm, LANES), F32)],
        compiler_params=_params(("arbitrary",)),
    )(dx, mix, g, w, out)


def _inproj_bwd(dproj, w, dx, x, g, name):
    t = x.shape[0]
    n = w.shape[1]
    tm = ROW_TILE

    def body(dp_ref, w_ref, dx_ref, x_ref, g_ref, o_ref, dg_ref):
        dxn, dgr = _rms_bwd(x_ref[...], g_ref[...], _dot_nt(dp_ref[...], w_ref[...]))
        o_ref[...] = dx_ref[...] + dxn
        _acc_rows8(dg_ref, _rows8(dgr), pl.program_id(0) == 0)

    row = lambda i: (i, 0)
    return pl.pallas_call(
        body, name=name, grid=(t // tm,),
        in_specs=[pl.BlockSpec((tm, n), row), pl.BlockSpec((D_MODEL, n), lambda i: (0, 0)),
                  pl.BlockSpec((tm, D_MODEL), row), pl.BlockSpec((tm, D_MODEL), row),
                  pl.BlockSpec((1, D_MODEL), lambda i: (0, 0))],
        out_specs=[pl.BlockSpec((tm, D_MODEL), row), pl.BlockSpec((SUBLANES, D_MODEL), lambda i: (0, 0))],
        out_shape=[jax.ShapeDtypeStruct((t, D_MODEL), F32), jax.ShapeDtypeStruct((SUBLANES, D_MODEL), F32)],
        compiler_params=_params(("arbitrary",)),
    )(dproj, w, dx, x, g)


def _grad_w(a, b, col_blocks, name):
    t, k = a.shape
    n = b.shape[1]
    tk = min(k, 1024)
    per_owner = n // N_DEV
    tn = 2 * per_owner if col_blocks else min(n, 1024)

    def body(a_ref, b_ref, o_ref, at_ref):
        @pl.when(pl.program_id(1) == 0)
        def _():
            for c in range(t // ROW_TILE):
                rows = slice(c * ROW_TILE, (c + 1) * ROW_TILE)
                at_ref[:, rows] = a_ref[rows, :].T

        res = _dot(at_ref[...], b_ref[...]).astype(BF16)
        if col_blocks:
            o_ref[0] = res[:, :per_owner]
            o_ref[1] = res[:, per_owner:]
        else:
            o_ref[...] = res

    if col_blocks:
        out_spec = pl.BlockSpec((2, tk, per_owner), lambda i, j: (j, i, 0))
        out_shape = jax.ShapeDtypeStruct((N_DEV, k, per_owner), BF16)
    else:
        out_spec = pl.BlockSpec((tk, tn), lambda i, j: (i, j))
        out_shape = jax.ShapeDtypeStruct((k, n), BF16)
    return pl.pallas_call(
        body, name=name, grid=(k // tk, n // tn),
        in_specs=[pl.BlockSpec((t, tk), lambda i, j: (0, i)), pl.BlockSpec((t, tn), lambda i, j: (0, j))],
        out_specs=out_spec, out_shape=out_shape,
        scratch_shapes=[pltpu.VMEM((tk, t), BF16)],
        compiler_params=_params(("parallel", "arbitrary")),
    )(a, b)


FF_STEP = 1024
FF_STEPS = D_FF // FF_STEP


def _ffn_fwd(x, g_pre, w1, w2, g_post, name, target=None):
    t = x.shape[0]
    tm = PROJ_TILE

    def body(*refs):
        if target is None:
            x_ref, gp_ref, w1_ref, w2_ref, gq_ref, xo_ref, y_ref, h_ref, r_ref = refs
        else:
            x_ref, gp_ref, w1_ref, w2_ref, gq_ref, t_ref, xo_ref, y_ref, h_ref, r_ref, l_ref = refs
        i, j = pl.program_id(0), pl.program_id(1)

        @pl.when(j == 0)
        def _():
            h_ref[...] = _rms(x_ref[...], gp_ref[...]).astype(BF16)

        a = _dot(h_ref[...], w1_ref[...])
        r = jnp.square(jnp.maximum(a, 0.0)).astype(BF16)
        r_ref[...] = r
        term = _dot(r, w2_ref[...])

        @pl.when(j == 0)
        def _():
            y_ref[...] = term

        @pl.when(j > 0)
        def _():
            y_ref[...] += term

        @pl.when(j == FF_STEPS - 1)
        def _():
            x_new = x_ref[...] + _rms(y_ref[...], gq_ref[...])
            if target is None:
                xo_ref[...] = x_new
            else:
                diff = x_new - t_ref[...]
                xo_ref[...] = diff * (1.0 / D_MODEL)
                _acc_rows8(l_ref, _rows8(diff * diff) * (0.5 / D_MODEL), i == 0)

    row = lambda i, j: (i, 0)
    vec = pl.BlockSpec((1, D_MODEL), lambda i, j: (0, 0))
    in_specs = [pl.BlockSpec((tm, D_MODEL), row), vec, pl.BlockSpec((D_MODEL, FF_STEP), lambda i, j: (0, j)),
                pl.BlockSpec((FF_STEP, D_MODEL), lambda i, j: (j, 0)), vec]
    out_specs = [pl.BlockSpec((tm, D_MODEL), row)] * 3 + [pl.BlockSpec((tm, FF_STEP), lambda i, j: (i, j))]
    out_shape = [jax.ShapeDtypeStruct((t, D_MODEL), F32), jax.ShapeDtypeStruct((t, D_MODEL), F32),
                 jax.ShapeDtypeStruct((t, D_MODEL), BF16), jax.ShapeDtypeStruct((t, D_FF), BF16)]
    args = [x, g_pre, w1, w2, g_post]
    if target is not None:
        in_specs.append(pl.BlockSpec((tm, D_MODEL), row))
        out_specs.append(pl.BlockSpec((SUBLANES, D_MODEL), lambda i, j: (0, 0)))
        out_shape.append(jax.ShapeDtypeStruct((SUBLANES, D_MODEL), F32))
        args.append(target)
    return pl.pallas_call(
        body, name=name, grid=(t // tm, FF_STEPS), in_specs=in_specs, out_specs=out_specs, out_shape=out_shape,
        compiler_params=_params(("parallel" if target is None else "arbitrary", "arbitrary")),
    )(*args)


def _ffn_bwd(dxo, x, y, r, g_pre, w1, w2, g_post, name):
    t = x.shape[0]
    tm = ROW_TILE

    def body(dxo_ref, x_ref, y_ref, r_ref, gp_ref, w1_ref, w2_ref, gq_ref,
             dx_ref, dy_ref, da_ref, dgp_ref, dgq_ref, acc_ref):
        i, j = pl.program_id(0), pl.program_id(1)

        @pl.when(j == 0)
        def _():
            dy, dgr = _rms_bwd(y_ref[...], gq_ref[...], dxo_ref[...])
            dy_ref[...] = dy.astype(BF16)
            _acc_rows8(dgq_ref, _rows8(dgr), i == 0)

        dr = _dot_nt(dy_ref[...], w2_ref[...])
        da = (dr * (2.0 * jnp.sqrt(r_ref[...].astype(F32)))).astype(BF16)
        da_ref[...] = da
        term = _dot_nt(da, w1_ref[...])

        @pl.when(j == 0)
        def _():
            acc_ref[...] = term

        @pl.when(j > 0)
        def _():
            acc_ref[...] += term

        @pl.when(j == FF_STEPS - 1)
        def _():
            dxn, dgr = _rms_bwd(x_ref[...], gp_ref[...], acc_ref[...])
            dx_ref[...] = dxo_ref[...] + dxn
            _acc_rows8(dgp_ref, _rows8(dgr), i == 0)

    row = lambda i, j: (i, 0)
    vec = pl.BlockSpec((1, D_MODEL), lambda i, j: (0, 0))
    acc8 = pl.BlockSpec((SUBLANES, D_MODEL), lambda i, j: (0, 0))
    return pl.pallas_call(
        body, name=name, grid=(t // tm, FF_STEPS),
        in_specs=[pl.BlockSpec((tm, D_MODEL), row)] * 3 + [
            pl.BlockSpec((tm, FF_STEP), lambda i, j: (i, j)),
            vec, pl.BlockSpec((D_MODEL, FF_STEP), lambda i, j: (0, j)),
            pl.BlockSpec((FF_STEP, D_MODEL), lambda i, j: (j, 0)), vec],
        out_specs=[pl.BlockSpec((tm, D_MODEL), row), pl.BlockSpec((tm, D_MODEL), row),
                   pl.BlockSpec((tm, FF_STEP), lambda i, j: (i, j)), acc8, acc8],
        out_shape=[jax.ShapeDtypeStruct((t, D_MODEL), F32), jax.ShapeDtypeStruct((t, D_MODEL), BF16),
                   jax.ShapeDtypeStruct((t, D_FF), BF16),
                   jax.ShapeDtypeStruct((SUBLANES, D_MODEL), F32), jax.ShapeDtypeStruct((SUBLANES, D_MODEL), F32)],
        scratch_shapes=[pltpu.VMEM((tm, D_MODEL), F32)],
        compiler_params=_params(("arbitrary", "arbitrary")),
    )(dxo, x, y, r, g_pre, w1, w2, g_post)


def _lower_bound(table):
    e = jnp.exp(table - jnp.max(table, axis=0, keepdims=True))
    return e[0:1, :] / jnp.sum(e, axis=0, keepdims=True)


def _hgrn2_block(q_ref, f_ref, lb):
    tb = f_ref.shape[0]
    sig = _sigmoid(f_ref[...])
    f = lb + (1.0 - lb) * sig
    qraw = q_ref[...]
    sq = _sigmoid(qraw)
    r = lax.broadcasted_iota(jnp.int32, (tb, tb), 0)
    c = lax.broadcasted_iota(jnp.int32, (tb, tb), 1)
    same = (r // SUB_CHUNK) == (c // SUB_CHUNK)
    logf = jnp.log(f)
    gsum = _mask_dot(same & (c <= r), logf)
    glast = _mask_dot(same, logf)
    return dict(sig=sig, f=f, kk=1.0 - f, qraw=qraw, sq=sq, qs=qraw * sq, gsum=gsum,
                eg=jnp.exp(gsum), ekd=jnp.exp(glast - gsum), a=jnp.exp(glast))


def _head_sums(x):
    parts = [jnp.broadcast_to(jnp.sum(x[:, h * HEAD_A:(h + 1) * HEAD_A], axis=1, keepdims=True), (x.shape[0], HEAD_A))
             for h in range(A_HEADS)]
    return jnp.concatenate(parts, axis=1)


def _hgrn2_intra(g, kk, qs, v):
    row = lax.broadcasted_iota(jnp.int32, g.shape, 0)
    o = _head_sums(qs * kk) * v
    for j in range(1, SUB_CHUNK):
        decay = jnp.exp(jnp.where(row >= j, g - pltpu.roll(g, j, 0), NEG))
        o = o + _head_sums(qs * pltpu.roll(kk, j, 0) * decay) * pltpu.roll(v, j, 0)
    return o


def _hgrn2_intra_bwd(g, kk, qs, v, do):
    row = lax.broadcasted_iota(jnp.int32, g.shape, 0)
    dsc = _head_sums(do * v)
    dqs, dkk, dv = dsc * kk, dsc * qs, _head_sums(qs * kk) * do
    for j in range(1, SUB_CHUNK):
        k_dn = pltpu.roll(kk, j, 0)
        decay = jnp.exp(jnp.where(row >= j, g - pltpu.roll(g, j, 0), NEG))
        d_score = _head_sums(do * pltpu.roll(v, j, 0)) * decay
        dqs = dqs + d_score * k_dn
        dkk = dkk + pltpu.roll(d_score * qs, SUB_CHUNK - j, 0)
        dv = dv + pltpu.roll(_head_sums(qs * k_dn * decay) * do, SUB_CHUNK - j, 0)
    return dqs, dkk, dv


def _hgrn2_fwd(proj, lb_table, a_norm, name):
    t = proj.shape[0]
    tb = HGRN_BLOCK
    n_tb = SEQ // tb
    n_seq = t // SEQ
    n_sub = tb // SUB_CHUNK

    def body(q_ref, f_ref, i_ref, g_ref, lbt_ref, an_ref, o_ref, pre_ref, sts_ref, st_ref,
             gs_ref, kk_ref, qs_ref, eg_ref, ekd_ref, a_ref):
        @pl.when(pl.program_id(1) == 0)
        def _():
            st_ref[...] = jnp.zeros_like(st_ref)

        an = an_ref[...]
        blk = _hgrn2_block(q_ref, f_ref, _lower_bound(lbt_ref[...]))
        for ref, key in ((gs_ref, "gsum"), (kk_ref, "kk"), (qs_ref, "qs"), (eg_ref, "eg"), (ekd_ref, "ekd"), (a_ref, "a")):
            ref[...] = blk[key]

        def step(c, carry):
            rows = pl.ds(pl.multiple_of(c * SUB_CHUNK, SUB_CHUNK), SUB_CHUNK)
            kk, qs, v = kk_ref[rows, :], qs_ref[rows, :], i_ref[rows, :]
            o = _hgrn2_intra(gs_ref[rows, :], kk, qs, v)
            qg, kd, vb = (qs * eg_ref[rows, :]).astype(BF16), (kk * ekd_ref[rows, :]).astype(BF16), v.astype(BF16)
            for h in range(A_HEADS):
                lanes = slice(h * HEAD_A, (h + 1) * HEAD_A)
                st = st_ref[h]
                sts_ref[0, c, h] = st
                o_h = o[:, lanes] + _dot_nt(qg[:, lanes], st.astype(BF16))
                st_ref[h] = st * a_ref[rows, lanes][0:1] + _dot_tn(vb[:, lanes], kd[:, lanes])
                pre_ref[rows, lanes] = o_h
                graw = g_ref[rows, lanes]
                o_ref[rows, lanes] = (_rms(o_h, an[:, lanes]) * (graw * _sigmoid(graw))).astype(BF16)
            return carry

        lax.fori_loop(0, n_sub, step, 0, unroll=2)

    def col(k):
        return pl.BlockSpec((tb, A_WIDTH), lambda b, s, k=k: (b * n_tb + s, k))

    out_rows = pl.BlockSpec((tb, A_WIDTH), lambda b, s: (b * n_tb + s, 0))
    return pl.pallas_call(
        body, name=name, grid=(n_seq, n_tb),
        in_specs=[col(0), col(1), col(2), col(3),
                  pl.BlockSpec((3, A_WIDTH), lambda b, s: (0, 0)), pl.BlockSpec((1, A_WIDTH), lambda b, s: (0, 0))],
        out_specs=[out_rows, out_rows,
                   pl.BlockSpec((1, n_sub, A_HEADS, HEAD_A, HEAD_A), lambda b, s: (b * n_tb + s, 0, 0, 0, 0))],
        out_shape=[jax.ShapeDtypeStruct((t, D_MODEL), BF16), jax.ShapeDtypeStruct((t, A_WIDTH), F32),
                   jax.ShapeDtypeStruct((n_seq * n_tb, n_sub, A_HEADS, HEAD_A, HEAD_A), F32)],
        scratch_shapes=[pltpu.VMEM((A_HEADS, HEAD_A, HEAD_A), F32)] + [pltpu.VMEM((tb, A_WIDTH), F32)] * 6,
        compiler_params=_params(("parallel", "arbitrary")),
    )(proj, proj, proj, proj, lb_table, a_norm)


def _hgrn2_bwd(proj, dcat, pre, states, lb_table, a_norm, name):
    t = proj.shape[0]
    tb = HGRN_BLOCK
    n_tb = SEQ // tb
    n_seq = t // SEQ
    n_sub = tb // SUB_CHUNK

    def body(q_ref, f_ref, i_ref, g_ref, do_ref, pre_ref, sts_ref, lbt_ref, an_ref, dp_ref, dlb_ref, dan_ref, dst_ref,
             gs_ref, kk_ref, qs_ref, eg_ref, ekd_ref, a_ref, dpre_ref, dlf_ref, dqs_ref, dkk_ref):
        b, s = pl.program_id(0), pl.program_id(1)

        @pl.when(s == 0)
        def _():
            dst_ref[...] = jnp.zeros_like(dst_ref)

        @pl.when((b == 0) & (s == 0))
        def _():
            dlb_ref[...] = jnp.zeros_like(dlb_ref)
            dan_ref[...] = jnp.zeros_like(dan_ref)

        lb = _lower_bound(lbt_ref[...])
        an = an_ref[...]
        heads = [slice(h * HEAD_A, (h + 1) * HEAD_A) for h in range(A_HEADS)]
        blk = _hgrn2_block(q_ref, f_ref, lb)
        for ref, key in ((gs_ref, "gsum"), (kk_ref, "kk"), (qs_ref, "qs"), (eg_ref, "eg"), (ekd_ref, "ekd"), (a_ref, "a")):
            ref[...] = blk[key]
        for h, lanes in enumerate(heads):
            graw, o = g_ref[:, lanes], pre_ref[:, lanes]
            sg = _sigmoid(graw)
            dout = do_ref[:, lanes]
            d_o, dgr = _rms_bwd(o, an[:, lanes], dout * (graw * sg))
            dan_ref[0:1, lanes] += jnp.sum(dgr, axis=0, keepdims=True)
            dp_ref[:, 3 * A_WIDTH + h * HEAD_A:3 * A_WIDTH + (h + 1) * HEAD_A] = (
                dout * _rms(o, an[:, lanes]) * (sg * (1.0 + graw * (1.0 - sg)))).astype(BF16)
            dpre_ref[:, lanes] = d_o

        tri_t = (lax.broadcasted_iota(jnp.int32, (SUB_CHUNK, SUB_CHUNK), 0)
                 <= lax.broadcasted_iota(jnp.int32, (SUB_CHUNK, SUB_CHUNK), 1)).astype(F32)

        def back(k, carry):
            c = n_sub - 1 - k
            rows = pl.ds(pl.multiple_of(c * SUB_CHUNK, SUB_CHUNK), SUB_CHUNK)
            g, kk, qs, v, d_o = gs_ref[rows, :], kk_ref[rows, :], qs_ref[rows, :], i_ref[rows, :], dpre_ref[rows, :]
            eg, ekd, a = eg_ref[rows, :], ekd_ref[rows, :], a_ref[rows, :]
            dqs, dkk, dv = _hgrn2_intra_bwd(g, kk, qs, v, d_o)
            qg_f, kd_f = qs * eg, kk * ekd
            qg, kd, vb, dob = qg_f.astype(BF16), kd_f.astype(BF16), v.astype(BF16), d_o.astype(BF16)
            dqg, dkd, da, dv_st = [], [], [], []
            for h, lanes in enumerate(heads):
                st, dst = sts_ref[0, c, h], dst_ref[h]
                dstb = dst.astype(BF16)
                dqg.append(_dot(dob[:, lanes], st.astype(BF16)))
                dv_st.append(_dot_nt(kd[:, lanes], dstb))
                dkd.append(_dot(vb[:, lanes], dstb))
                da.append(jnp.broadcast_to(jnp.sum(dst * st, axis=0, keepdims=True), (SUB_CHUNK, HEAD_A)))
                dst_ref[h] = dst * a[0:1, lanes] + _dot_tn(dob[:, lanes], qg[:, lanes])
            dqg, dkd, da, dv_st = [jnp.concatenate(p, axis=1) for p in (dqg, dkd, da, dv_st)]
            d_gsum = qs * dqs - kk * dkk + dqg * qg_f - dkd * kd_f
            d_glast = jnp.sum(dkd * kd_f, axis=0, keepdims=True) + da * a
            dlf_ref[rows, :] = jnp.dot(tri_t, d_gsum, precision=lax.Precision.HIGHEST,
                                       preferred_element_type=F32) + d_glast
            dqs_ref[rows, :] = dqs + dqg * eg
            dkk_ref[rows, :] = dkk + dkd * ekd
            dp_ref[rows, 2 * A_WIDTH:3 * A_WIDTH] = (dv + dv_st).astype(BF16)
            return carry

        lax.fori_loop(0, n_sub, back, 0, unroll=2)
        sig, sq, qraw = blk["sig"], blk["sq"], blk["qraw"]
        d_f = dlf_ref[...] / blk["f"] - dkk_ref[...]
        dlb_ref[0:1, :] += jnp.sum(d_f * (1.0 - sig), axis=0, keepdims=True)
        dp_ref[:, 0:A_WIDTH] = (dqs_ref[...] * (sq * (1.0 + qraw * (1.0 - sq)))).astype(BF16)
        dp_ref[:, A_WIDTH:2 * A_WIDTH] = (d_f * (1.0 - lb) * sig * (1.0 - sig)).astype(BF16)

    def rev(s):
        return n_tb - 1 - s

    def col(k):
        return pl.BlockSpec((tb, A_WIDTH), lambda b, s, k=k: (b * n_tb + rev(s), k))

    acc8 = pl.BlockSpec((SUBLANES, A_WIDTH), lambda b, s: (0, 0))
    return pl.pallas_call(
        body, name=name, grid=(n_seq, n_tb),
        in_specs=[col(0), col(1), col(2), col(3), col(0), col(0),
                  pl.BlockSpec((1, n_sub, A_HEADS, HEAD_A, HEAD_A), lambda b, s: (b * n_tb + rev(s), 0, 0, 0, 0)),
                  pl.BlockSpec((3, A_WIDTH), lambda b, s: (0, 0)), pl.BlockSpec((1, A_WIDTH), lambda b, s: (0, 0))],
        out_specs=[pl.BlockSpec((tb, 4 * A_WIDTH), lambda b, s: (b * n_tb + rev(s), 0)), acc8, acc8],
        out_shape=[jax.ShapeDtypeStruct((t, EVEN_IN), BF16)] + [jax.ShapeDtypeStruct((SUBLANES, A_WIDTH), F32)] * 2,
        scratch_shapes=[pltpu.VMEM((A_HEADS, HEAD_A, HEAD_A), F32)] + [pltpu.VMEM((tb, A_WIDTH), F32)] * 10,
        compiler_params=_params(("arbitrary", "arbitrary")),
    )(proj, proj, proj, proj, dcat, pre, states, lb_table, a_norm)


GMLP_ROWS = 512


def _gmlp_chunk(ub, vb, ln_g, ln_b, ws, bias):
    u = [_gelu(a) for a in ub]
    v = [_gelu(a) for a in vb]
    mu = sum(jnp.sum(a, axis=-1, keepdims=True) for a in v) * (1.0 / B_WIDTH)
    cen = [a - mu for a in v]
    var = sum(jnp.sum(a * a, axis=-1, keepdims=True) for a in cen) * (1.0 / B_WIDTH)
    inv = lax.rsqrt(var + EPS)
    r = lax.broadcasted_iota(jnp.int32, (B_CHUNK, B_CHUNK), 0)
    c = lax.broadcasted_iota(jnp.int32, (B_CHUNK, B_CHUNK), 1)
    outs = []
    for g in range(B_GROUPS):
        vn = (cen[g] * inv * ln_g[g] + ln_b[g]).astype(BF16)
        wm = jnp.where(c <= r, ws[g], 0.0).astype(BF16)
        outs.append(u[g] * (_dot(wm, vn) + bias[g]))
    return outs


def _lane_groups(ref, rows=slice(None)):
    return [ref[rows, g * LANES:(g + 1) * LANES] for g in range(B_GROUPS)]


def _gmlp_fwd(proj, mixed, ln_g, ln_b, ws, bias_t, name):
    t = proj.shape[0]
    tm = GMLP_ROWS

    def body(u_ref, v_ref, lg_ref, lb_ref, ws_ref, bt_ref, _, o_ref):
        for ch in range(tm // B_CHUNK):
            rows = slice(ch * B_CHUNK, (ch + 1) * B_CHUNK)
            outs = _gmlp_chunk(_lane_groups(u_ref, rows), _lane_groups(v_ref, rows), _lane_groups(lg_ref),
                               _lane_groups(lb_ref), [ws_ref[g] for g in range(B_GROUPS)],
                               [bt_ref[:, g:g + 1] for g in range(B_GROUPS)])
            for g in range(B_GROUPS):
                o_ref[rows, g * LANES:(g + 1) * LANES] = outs[g].astype(BF16)

    vec = pl.BlockSpec((1, B_WIDTH), lambda i: (0, 0))
    return pl.pallas_call(
        body, name=name, grid=(t // tm,),
        in_specs=[pl.BlockSpec((tm, B_WIDTH), lambda i: (i, 4)), pl.BlockSpec((tm, B_WIDTH), lambda i: (i, 5)), vec, vec,
                  pl.BlockSpec((B_GROUPS, B_CHUNK, B_CHUNK), lambda i: (0, 0, 0)),
                  pl.BlockSpec((B_CHUNK, B_GROUPS), lambda i: (0, 0)), pl.BlockSpec(memory_space=pl.ANY)],
        out_specs=pl.BlockSpec((tm, B_WIDTH), lambda i: (i, 1)),
        out_shape=jax.ShapeDtypeStruct(mixed.shape, BF16),
        input_output_aliases={6: 0},
        compiler_params=_params(("parallel",)),
    )(proj, proj, ln_g, ln_b, ws, bias_t, mixed)


def _gmlp_bwd(proj, dcat, dproj, ln_g, ln_b, ws, bias_t, name):
    t = proj.shape[0]
    tm = GMLP_ROWS

    def body(u_ref, v_ref, do_ref, lg_ref, lb_ref, ws_ref, bt_ref, _, duv_ref, dlg_ref, dlb_ref, dws_ref, dbt_ref):
        @pl.when(pl.program_id(0) == 0)
        def _():
            dlg_ref[...] = jnp.zeros_like(dlg_ref)
            dlb_ref[...] = jnp.zeros_like(dlb_ref)
            dws_ref[...] = jnp.zeros_like(dws_ref)
            dbt_ref[...] = jnp.zeros_like(dbt_ref)

        for ch in range(tm // B_CHUNK):
            rows = slice(ch * B_CHUNK, (ch + 1) * B_CHUNK)
            _, vjp = jax.vjp(
                _gmlp_chunk, _lane_groups(u_ref, rows), _lane_groups(v_ref, rows), _lane_groups(lg_ref),
                _lane_groups(lb_ref), [ws_ref[g] for g in range(B_GROUPS)],
                [bt_ref[:, g:g + 1] for g in range(B_GROUPS)])
            du, dv, dlg, dlb, dw, dbt = vjp(_lane_groups(do_ref, rows))
            for g in range(B_GROUPS):
                lanes = slice(g * LANES, (g + 1) * LANES)
                duv_ref[rows, lanes] = du[g].astype(BF16)
                duv_ref[rows, B_WIDTH + g * LANES:B_WIDTH + (g + 1) * LANES] = dv[g].astype(BF16)
                dlg_ref[0:1, lanes] += dlg[g]
                dlb_ref[0:1, lanes] += dlb[g]
                dws_ref[g] += dw[g]
                dbt_ref[:, g:g + 1] += dbt[g]

    vec = pl.BlockSpec((1, B_WIDTH), lambda i: (0, 0))
    acc8 = pl.BlockSpec((SUBLANES, B_WIDTH), lambda i: (0, 0))
    ws_spec = pl.BlockSpec((B_GROUPS, B_CHUNK, B_CHUNK), lambda i: (0, 0, 0))
    bt_spec = pl.BlockSpec((B_CHUNK, B_GROUPS), lambda i: (0, 0))
    return pl.pallas_call(
        body, name=name, grid=(t // tm,),
        in_specs=[pl.BlockSpec((tm, B_WIDTH), lambda i: (i, 4)), pl.BlockSpec((tm, B_WIDTH), lambda i: (i, 5)),
                  pl.BlockSpec((tm, B_WIDTH), lambda i: (i, 1)), vec, vec, ws_spec, bt_spec,
                  pl.BlockSpec(memory_space=pl.ANY)],
        out_specs=[pl.BlockSpec((tm, 2 * B_WIDTH), lambda i: (i, 2)), acc8, acc8, ws_spec, bt_spec],
        out_shape=[jax.ShapeDtypeStruct(dproj.shape, BF16), jax.ShapeDtypeStruct((SUBLANES, B_WIDTH), F32),
                   jax.ShapeDtypeStruct((SUBLANES, B_WIDTH), F32),
                   jax.ShapeDtypeStruct((B_GROUPS, B_CHUNK, B_CHUNK), F32),
                   jax.ShapeDtypeStruct((B_CHUNK, B_GROUPS), F32)],
        input_output_aliases={7: 0},
        compiler_params=_params(("arbitrary",)),
    )(proj, proj, dcat, ln_g, ln_b, ws, bias_t, dproj)


QK_SCALE = 1.0 / math.sqrt(C_HEAD_DIM)
ATTN_UNROLL = 16
ATTN_PAIRS = 2
LANE_GROUPS = D_MODEL // LANES
ATTN_STEPS = LANE_GROUPS // ATTN_PAIRS
Q_BLOCKS = SEQ // C_BLOCK


def _attn_window(i, d):
    sub_blocks = Q_BLOCKS // d
    q0 = pl.multiple_of(i * C_BLOCK, C_BLOCK)
    k0 = pl.multiple_of(jnp.maximum(i - 1, 0) * C_BLOCK, C_BLOCK)
    key = k0 + lax.broadcasted_iota(jnp.int32, (C_BLOCK, 2 * C_BLOCK), 1)
    dist = (q0 + lax.broadcasted_iota(jnp.int32, (C_BLOCK, 2 * C_BLOCK), 0)) - key
    own_subsequence = (key >= q0) | (i % sub_blocks > 0)
    return pl.ds(q0, C_BLOCK), pl.ds(k0, 2 * C_BLOCK), (dist >= 0) & (dist <= C_BLOCK) & own_subsequence


def _head_masks():
    lane = lax.broadcasted_iota(jnp.int32, (C_BLOCK, LANES), 1)
    return [lane < C_HEAD_DIM, lane >= C_HEAD_DIM]


def _flat_spec(col_of):
    return pl.BlockSpec((1, SEQ, ATTN_PAIRS * LANES), lambda b, g: (b, 0, col_of(g)))


def _put_heads(tile, g, col0, col1):
    lane = lax.broadcasted_iota(jnp.int32, tile.shape, 1)
    return jnp.where(lane == 2 * g, col0, jnp.where(lane == 2 * g + 1, col1, tile))


def _get_head(tile, h):
    lane = lax.broadcasted_iota(jnp.int32, tile.shape, 1)
    return jnp.sum(jnp.where(lane == h, tile, 0.0), axis=1, keepdims=True)


PER_HEAD_SPEC = pl.BlockSpec((1, SEQ, LANES), lambda b, g: (b, 0, 0))


def _attn_branch_fwd(qkv, name):
    n_seq, d, l, _ = qkv.shape
    flat = qkv.reshape(n_seq, SEQ, ODD_IN)

    def body(q_ref, k_ref, v_ref, o_ref, m_ref, l_ref):
        heads = _head_masks()
        g = pl.program_id(1)

        @pl.when(g == 0)
        def _():
            m_ref[...] = jnp.zeros_like(m_ref)
            l_ref[...] = jnp.zeros_like(l_ref)

        def block(i, carry):
            rows, keys, mask = _attn_window(i, d)
            m_tile, l_tile = m_ref[0, rows, :], l_ref[0, rows, :]
            for pair in range(ATTN_PAIRS):
                lanes = slice(pair * LANES, (pair + 1) * LANES)
                q, k, v = q_ref[0, rows, lanes], k_ref[0, keys, lanes], v_ref[0, keys, lanes]
                res = []
                for hm in heads:
                    s = jnp.where(mask, _dot_nt(jnp.where(hm, q, 0), k), NEG)
                    m = jnp.max(s, axis=-1, keepdims=True)
                    p = jnp.exp(s - m)
                    res.append((_dot(p.astype(BF16), v), m, jnp.sum(p, axis=-1, keepdims=True)))
                o_ref[0, rows, lanes] = jnp.where(heads[0], res[0][0], res[1][0])
                m_tile = _put_heads(m_tile, g * ATTN_PAIRS + pair, res[0][1], res[1][1])
                l_tile = _put_heads(l_tile, g * ATTN_PAIRS + pair, res[0][2], res[1][2])
            m_ref[0, rows, :] = m_tile
            l_ref[0, rows, :] = l_tile
            return carry

        lax.fori_loop(0, Q_BLOCKS, block, 0, unroll=ATTN_UNROLL)

    o, m, l_sum = pl.pallas_call(
        body, name=name, grid=(n_seq, ATTN_STEPS),
        in_specs=[_flat_spec(lambda g: g), _flat_spec(lambda g: ATTN_STEPS + g),
                  _flat_spec(lambda g: 2 * ATTN_STEPS + g)],
        out_specs=[_flat_spec(lambda g: g), PER_HEAD_SPEC, PER_HEAD_SPEC],
        out_shape=[jax.ShapeDtypeStruct((n_seq, SEQ, D_MODEL), F32)] + [jax.ShapeDtypeStruct((n_seq, SEQ, LANES), F32)] * 2,
        compiler_params=_params(("parallel", "arbitrary")),
    )(flat, flat, flat)
    return [o.reshape(n_seq, d, l, D_MODEL), m.reshape(n_seq, d, l, LANES), l_sum.reshape(n_seq, d, l, LANES)]


def _attn_merge(branches, name):
    n_seq = branches[0][0].shape[0]
    t = n_seq * SEQ
    tm = MERGE_TILE

    def body(*refs):
        ins = refs[:9]
        o_ref, ob_ref, lse_ref = refs[9:12]
        nat = refs[12:]
        for b, d in enumerate(C_DILATIONS[1:]):
            for k in range(3):
                _load_dilated(ins[3 + 3 * b + k], d, nat[3 * b + k])
        ms = [ins[1][0, 0], nat[1][0], nat[4][0]]
        ls = [ins[2][0, 0], nat[2][0], nat[5][0]]
        m_all = jnp.maximum(jnp.maximum(ms[0], ms[1]), ms[2])
        ws = [jnp.exp(ms[b] - m_all) for b in range(3)]
        lane = lax.broadcasted_iota(jnp.int32, m_all.shape, 1)
        total = jnp.where(lane < C_HEADS, ws[0] * ls[0] + ws[1] * ls[1] + ws[2] * ls[2], 1.0)
        lse_ref[...] = m_all + jnp.log(total)
        first_head = lane < C_HEAD_DIM
        for p in range(LANE_GROUPS):
            lanes = slice(p * LANES, (p + 1) * LANES)
            spread = lambda c: jnp.where(first_head, c[:, 2 * p:2 * p + 1], c[:, 2 * p + 1:2 * p + 2])
            os_ = [ins[0][0, 0, :, lanes], nat[0][p], nat[3][p]]
            o = (spread(ws[0]) * os_[0] + spread(ws[1]) * os_[1] + spread(ws[2]) * os_[2]) / spread(total)
            o_ref[:, lanes] = o
            ob_ref[:, lanes] = o.astype(BF16)

    row = pl.BlockSpec((tm, D_MODEL), lambda i: (i, 0))
    flat = [a for br in branches for a in br]
    in_specs = []
    for wide, narrow in zip(_dilated_specs(tm, D_MODEL, lambda: 0), _dilated_specs(tm, LANES, lambda: 0)):
        in_specs += [wide, narrow, narrow]
    per_head = pltpu.VMEM((1, tm, LANES), F32)
    return pl.pallas_call(
        body, name=name, grid=(t // tm,), in_specs=in_specs,
        out_specs=[row, row, pl.BlockSpec((tm, LANES), lambda i: (i, 0))],
        out_shape=[jax.ShapeDtypeStruct((t, D_MODEL), F32), jax.ShapeDtypeStruct((t, D_MODEL), BF16),
                   jax.ShapeDtypeStruct((t, LANES), F32)],
        scratch_shapes=[pltpu.VMEM((LANE_GROUPS, tm, LANES), F32), per_head, per_head] * 2,
        compiler_params=_params(("parallel",)),
    )(*flat)


def _attn_branch_bwd(qkv, dout, lse, delta, name):
    n_seq, d, l, _ = qkv.shape
    flat = lambda a: a.reshape(n_seq, SEQ, a.shape[-1])

    def body(q_ref, k_ref, v_ref, do_ref, lse_nat_ref, dl_nat_ref, dq_ref, dk_ref, dv_ref, lse_ref, dl_ref,
             dkt_ref, dvt_ref):
        heads = _head_masks()
        g = pl.program_id(1)
        dkt_ref[...] = jnp.zeros_like(dkt_ref)
        dvt_ref[...] = jnp.zeros_like(dvt_ref)
        for nat_ref, dst_ref in ((lse_nat_ref, lse_ref), (dl_nat_ref, dl_ref)):
            for r in range(d):
                rows = pl.ds(r, l, stride=d) if d > 1 else slice(None)
                dst_ref[r * l:(r + 1) * l, :] = nat_ref.at[0][rows, :]

        def block(i, carry):
            rows, keys, mask = _attn_window(i, d)
            lse_b, dl_b = lse_ref[rows, :], dl_ref[rows, :]
            for pair in range(ATTN_PAIRS):
                lanes = slice(pair * LANES, (pair + 1) * LANES)
                q, do = q_ref[0, rows, lanes], do_ref[0, rows, lanes]
                k, v = k_ref[0, keys, lanes], v_ref[0, keys, lanes]
                dq, dk, dv = [], None, None
                for hh, hm in enumerate(heads):
                    head = 2 * (g * ATTN_PAIRS + pair) + hh
                    qh, doh = jnp.where(hm, q, 0), jnp.where(hm, do, 0)
                    s = jnp.where(mask, _dot_nt(qh, k), NEG)
                    p = jnp.exp(s - _get_head(lse_b, head))
                    ds = (p * (_dot_nt(doh, v) - _get_head(dl_b, head))).astype(BF16)
                    dq.append(_dot(ds, k) * QK_SCALE)
                    dk_h, dv_h = _dot_tn(qh, ds), _dot_tn(doh, p.astype(BF16))
                    dk = dk_h if dk is None else dk + dk_h
                    dv = dv_h if dv is None else dv + dv_h
                dq_ref[0, rows, lanes] = jnp.where(heads[0], dq[0], dq[1]).astype(BF16)
                dkt_ref[lanes, keys] += dk
                dvt_ref[lanes, keys] += dv
            return carry

        lax.fori_loop(0, Q_BLOCKS, block, 0, unroll=ATTN_UNROLL)
        for c in range(SEQ // ROW_TILE):
            rows = slice(c * ROW_TILE, (c + 1) * ROW_TILE)
            dk_ref[0, rows, :] = dkt_ref[:, rows].T.astype(BF16)
            dv_ref[0, rows, :] = dvt_ref[:, rows].T.astype(BF16)

    act = _flat_spec(lambda g: g)
    outs = pl.pallas_call(
        body, name=name, grid=(n_seq, ATTN_STEPS),
        in_specs=[_flat_spec(lambda g: g), _flat_spec(lambda g: ATTN_STEPS + g),
                  _flat_spec(lambda g: 2 * ATTN_STEPS + g), act, PER_HEAD_SPEC, PER_HEAD_SPEC],
        out_specs=[act] * 3,
        out_shape=[jax.ShapeDtypeStruct((n_seq, SEQ, D_MODEL), BF16)] * 3,
        scratch_shapes=[pltpu.VMEM((SEQ, LANES), F32)] * 2 + [pltpu.VMEM((ATTN_PAIRS * LANES, SEQ), F32)] * 2,
        compiler_params=_params(("parallel", "parallel")),
    )(flat(qkv), flat(qkv), flat(qkv), flat(dout), lse, delta)
    return [o.reshape(n_seq, d, l, D_MODEL) for o in outs]


def _attn_combine_bwd(grads, rope, name):
    n_seq = grads[0][0].shape[0]
    t = n_seq * SEQ
    tm = MERGE_TILE

    def body(*refs):
        c_ref, s_ref, o_ref, nat4_ref, nat16_ref = refs[9:]
        for sec in range(3):
            _load_dilated(refs[3 + sec], 4, nat4_ref)
            _load_dilated(refs[6 + sec], 16, nat16_ref)
            for p in range(LANE_GROUPS):
                blk = refs[sec][0, 0, :, p * LANES:(p + 1) * LANES] + nat4_ref[p] + nat16_ref[p]
                if sec < 2:
                    blk = blk * c_ref[...] - _swap_halves(blk) * s_ref[...]
                o_ref[:, sec * D_MODEL + p * LANES:sec * D_MODEL + (p + 1) * LANES] = blk.astype(BF16)

    tab = pl.BlockSpec((tm, LANES), lambda i: (i, 0))
    flat = [a for br in grads for a in br]
    in_specs = []
    for spec in _dilated_specs(tm, D_MODEL, lambda: 0):
        in_specs += [spec] * 3
    return pl.pallas_call(
        body, name=name, grid=(t // tm,), in_specs=in_specs + [tab, tab],
        out_specs=pl.BlockSpec((tm, ODD_IN), lambda i: (i, 0)),
        out_shape=jax.ShapeDtypeStruct((t, ODD_IN), BF16),
        scratch_shapes=[pltpu.VMEM((LANE_GROUPS, tm, LANES), F32)] * 2,
        compiler_params=_params(("parallel",)),
    )(*flat, *rope)


def _adamw(w, g, m, v):
    m = ADAM_B1 * m + (1.0 - ADAM_B1) * g
    v = ADAM_B2 * v + (1.0 - ADAM_B2) * jnp.square(g)
    m_hat = m / (1.0 - ADAM_B1 ** ADAM_STEP)
    v_hat = v / (1.0 - ADAM_B2 ** ADAM_STEP)
    delta = -ADAM_LR * (m_hat / (jnp.sqrt(v_hat) + ADAM_EPS) + ADAM_WD * w)
    return delta, m, v


def _adamw_sharded(parts, w, m, v, after, name):
    n_layers, rows, cols = w.shape
    tr = min(rows, 256)

    def body(*refs):
        p_refs = refs[:n_layers]
        w_ref, m_ref, v_ref, _, g_ref, d_ref, mo_ref, vo_ref = refs[n_layers:]
        layer = pl.program_id(0)
        g = None
        for l, p_ref in enumerate(p_refs):
            g_l = p_ref[0].astype(F32)
            for s in range(1, N_DEV):
                g_l = g_l + p_ref[s].astype(F32)
            g = g_l if g is None else jnp.where(layer == l, g_l, g)
        delta, mn, vn = _adamw(w_ref[0], g, m_ref[0], v_ref[0])
        g_ref[0] = g
        d_ref[0] = delta
        mo_ref[0] = mn
        vo_ref[0] = vn

    def part_spec(l):
        return pl.BlockSpec((N_DEV, tr, cols), lambda a, i: (0, jnp.where(a == l, i, 0), 0))

    row = pl.BlockSpec((1, tr, cols), lambda a, i: (a, i, 0))
    return pl.pallas_call(
        body, name=name, grid=(n_layers, rows // tr),
        in_specs=[part_spec(l) for l in range(n_layers)] + [row, row, row, pl.BlockSpec(memory_space=pl.ANY)],
        out_specs=[row] * 4, out_shape=[jax.ShapeDtypeStruct(w.shape, F32)] * 4,
        compiler_params=_params(("arbitrary", "arbitrary")),
    )(*parts, w, m, v, after)


def _small_update(gathered, where, weights, moments_m, moments_v, lb_index, name):
    n = len(weights)
    n_g = len(gathered)

    def body(*refs):
        g_refs = refs[:n_g]
        w_refs, m_refs, v_refs = refs[n_g:n_g + n], refs[n_g + n:n_g + 2 * n], refs[n_g + 2 * n:n_g + 3 * n]
        outs = refs[n_g + 3 * n:]

        def total(k):
            array, rows, lanes = where[k]
            ref = g_refs[array]
            index = (slice(None),) * (len(ref.shape) - 1) if rows is None else (rows, lanes)
            acc = ref[(0,) + index]
            for s in range(1, N_DEV):
                acc = acc + ref[(s,) + index]
            return acc

        loss_rows = total(n)
        outs[0][...] = jnp.sum(jnp.sum(loss_rows, axis=1, keepdims=True), axis=0, keepdims=True)
        for k in range(n):
            part = total(k)
            if k == lb_index:
                dlb = jnp.sum(part, axis=0, keepdims=True)
                tab = w_refs[k][...]
                e = jnp.exp(tab - jnp.max(tab, axis=0, keepdims=True))
                p = e / jnp.sum(e, axis=0, keepdims=True)
                first = lax.broadcasted_iota(jnp.int32, p.shape, 0) == 0
                grads = [(slice(None), p * (jnp.where(first, dlb, 0.0) - p[0:1, :] * dlb))]
            elif part.shape == w_refs[k].shape:
                grads = [(slice(None), part)]
            else:
                grads = [(slice(l, l + 1), jnp.sum(part[l * SUBLANES:(l + 1) * SUBLANES], axis=0, keepdims=True))
                         for l in range(w_refs[k].shape[0])]
            for rows, g in grads:
                delta, mn, vn = _adamw(w_refs[k][rows], g, m_refs[k][rows], v_refs[k][rows])
                outs[1 + 4 * k][rows] = g
                outs[2 + 4 * k][rows] = delta
                outs[3 + 4 * k][rows] = mn
                outs[4 + 4 * k][rows] = vn

    vmem = pl.BlockSpec(memory_space=pltpu.VMEM)
    out_shape = [jax.ShapeDtypeStruct((1, 1), F32)]
    for w in weights:
        out_shape += [jax.ShapeDtypeStruct(w.shape, F32)] * 4
    args = list(gathered) + list(weights) + list(moments_m) + list(moments_v)
    return pl.pallas_call(
        body, name=name, in_specs=[vmem] * len(args), out_specs=[vmem] * len(out_shape), out_shape=out_shape,
        compiler_params=pltpu.CompilerParams(vmem_limit_bytes=VMEM_LIMIT),
    )(*args)


def kernel(x, positions, norm_mix_pre, norm_mix_post, norm_ffn_pre, norm_ffn_post, w_in_even, lb_table, a_norm, b_ln_g, b_ln_b, b_ws, b_bias, w_out_even, w_in_odd, w_out_odd, w_ff1, w_ff2, loss_target, m_norm_mix_pre, m_norm_mix_post, m_norm_ffn_pre, m_norm_ffn_post, m_w_in_even, m_lb_table, m_a_norm, m_b_ln_g, m_b_ln_b, m_b_ws, m_b_bias, m_w_out_even, m_w_in_odd, m_w_out_odd, m_w_ff1, m_w_ff2, v_norm_mix_pre, v_norm_mix_post, v_norm_ffn_pre, v_norm_ffn_post, v_w_in_even, v_lb_table, v_a_norm, v_b_ln_g, v_b_ln_b, v_b_ws, v_b_bias, v_w_out_even, v_w_in_odd, v_w_out_odd, v_w_ff1, v_w_ff2):
    n_seq = x.shape[0]
    t = n_seq * SEQ
    x0 = x.reshape(t, D_MODEL)
    target = loss_target.reshape(t, D_MODEL)

    me = _my_slot().astype(jnp.int32).reshape(1)

    order = ["in_e", "out_e", "ff1_0", "ff2_0", "in_o", "out_o", "ff1_1", "ff2_1"]
    shards = dict(in_e=w_in_even[0], out_e=w_out_even[0], in_o=w_in_odd[0], out_o=w_out_odd[0],
                  ff1_0=w_ff1[0], ff1_1=w_ff1[1], ff2_0=w_ff2[0], ff2_1=w_ff2[1])
    by_columns = ("in_e", "in_o", "ff1_0", "ff1_1")

    def place(k, after):
        if k in by_columns:
            return _place_own_columns(shards[k], me, "place_" + k, after)
        return _place_own(shards[k], me, "place_" + k, False, after=after)

    gathers = {}
    send0, recv0, land0, _, token0 = _exchange_start([place(order[0], None)], [None], "gather_start_first")
    gathers[order[0]] = (land0[0], send0[0], recv0[0])
    sends, recvs, lands, _, g_token = _exchange_start([place(k, token0) for k in order[1:]],
                                                      [None] * (len(order) - 1), "gather_start")
    for k, land, send, recv in zip(order[1:], lands, sends, recvs):
        gathers[k] = (land, send, recv)

    def get_w(keys, after):
        lands_k, sends_k, recvs_k = zip(*[gathers[k] for k in keys])
        return _exchange_wait(list(lands_k), [None] * len(keys), list(sends_k), list(recvs_k), after,
                              "gather_wait_" + keys[0])

    sent = {}

    def put_g(group, blocks):
        keys = list(blocks)
        own = [_place_own(blocks[k], me, "own_" + k, True) for k in keys]
        send_sems, recv_sems, own, srcs, token = _exchange_start(own, [blocks[k] for k in keys], "scatter_start_" + group)
        sent[group] = (keys, own, srcs, send_sems, recv_sems)
        return token

    rope = _rope_tables(positions)
    bias_t = b_bias[0].T
    grads = _local_step(x0, target, rope, norm_mix_pre, norm_mix_post, norm_ffn_pre, norm_ffn_post, lb_table,
                        a_norm, b_ln_g, b_ln_b, b_ws[0], bias_t, get_w, put_g, g_token)
    (dx0, loss_part, dg_mix_pre, dg_mix_post, dg_ffn_pre, dg_ffn_post, d_lb, d_a_norm, d_ln_g, d_ln_b, d_ws,
     d_bias_t) = grads

    packed = jnp.concatenate([dg_mix_pre, dg_mix_post, dg_ffn_pre, dg_ffn_post,
                              jnp.concatenate([d_lb, d_a_norm], axis=1), jnp.concatenate([d_ln_g, d_ln_b], axis=1),
                              loss_part], axis=0)
    small_lands = [_place_own(a, me, "own_small%d" % k, False, F32) for k, a in enumerate((packed, d_ws, d_bias_t))]
    s_send, s_recv, small_lands, _, after = _exchange_start(small_lands, [None] * 3, "gather_small_start")

    big = dict(w_in_even=(["in_e"], w_in_even, m_w_in_even, v_w_in_even),
               w_out_even=(["out_e"], w_out_even, m_w_out_even, v_w_out_even),
               w_in_odd=(["in_o"], w_in_odd, m_w_in_odd, v_w_in_odd),
               w_out_odd=(["out_o"], w_out_odd, m_w_out_odd, v_w_out_odd),
               w_ff1=(["ff1_0", "ff1_1"], w_ff1, m_w_ff1, v_w_ff1), w_ff2=(["ff2_0", "ff2_1"], w_ff2, m_w_ff2, v_w_ff2))
    recv, big_out = {}, {}
    for groups, names in ((("ffn1", "ffn0"), ("w_ff1", "w_ff2")), (("mix1",), ("w_in_odd", "w_out_odd")),
                          (("mix0",), ("w_in_even", "w_out_even"))):
        for group in groups:
            keys, own, srcs, send_sems, recv_sems = sent[group]
            recv.update(zip(keys, _exchange_wait(own, srcs, send_sems, recv_sems, after, "scatter_wait_" + group)))
        for nm in names:
            keys, w, m, v = big[nm]
            big_out[nm] = _adamw_sharded([recv[k] for k in keys], w, m, v, after, "adamw_" + nm)
            after = big_out[nm][0]
    big_out = [big_out[nm] for nm in ("w_in_even", "w_out_even", "w_in_odd", "w_out_odd", "w_ff1", "w_ff2")]
    gathered = _exchange_wait(small_lands, [None] * 3, s_send, s_recv, after, "gather_small_wait")
    rows8 = lambda k: slice(SUBLANES * k, SUBLANES * (k + 1))
    left, right, every = slice(0, A_WIDTH), slice(A_WIDTH, 2 * A_WIDTH), slice(None)
    where = [(0, slice(0, 16), every), (0, slice(16, 32), every), (0, slice(32, 48), every), (0, slice(48, 64), every),
             (0, rows8(8), left), (0, rows8(8), right), (0, rows8(9), left), (0, rows8(9), right),
             (1, None, None), (2, None, None), (0, rows8(10), every)]
    small_w = [norm_mix_pre, norm_mix_post, norm_ffn_pre, norm_ffn_post, lb_table, a_norm, b_ln_g, b_ln_b,
               b_ws[0], bias_t]
    small_m = [m_norm_mix_pre, m_norm_mix_post, m_norm_ffn_pre, m_norm_ffn_post, m_lb_table, m_a_norm, m_b_ln_g,
               m_b_ln_b, m_b_ws[0], m_b_bias[0].T]
    small_v = [v_norm_mix_pre, v_norm_mix_post, v_norm_ffn_pre, v_norm_ffn_post, v_lb_table, v_a_norm, v_b_ln_g,
               v_b_ln_b, v_b_ws[0], v_b_bias[0].T]
    small_out = _small_update(gathered, where, small_w, small_m, small_v, 4, "small_update")
    loss = small_out[0].reshape(())
    small = [small_out[1 + 4 * k:5 + 4 * k] for k in range(len(small_w))]
    small[8] = [a[None] for a in small[8]]
    small[9] = [a.T[None] for a in small[9]]

    per_weight = small[0:4] + [big_out[0]] + small[4:10] + big_out[1:6]
    grad_x = dx0.reshape(x.shape)
    out = [loss, grad_x]
    for kind in range(4):
        out += [p[kind] for p in per_weight]
    return tuple(out)


def _local_step(x0, target, rope, norm_mix_pre, norm_mix_post, norm_ffn_pre, norm_ffn_post, lb_table, a_norm,
                b_ln_g, b_ln_b, ws, bias_t, get_w, put_g, token):
    def gain(a, l, tok):
        return a[l:l + 1] if tok is None else a[l:l + 1] + tok[0:1, 0:1]

    full = lambda a: a.reshape(-1, D_MODEL)
    owners = lambda a: a.reshape((N_DEV, -1) + a.shape[1:])

    (g_in_e,) = get_w(["in_e"], token)
    proj, h_mix0 = _norm_inproj(x0, gain(norm_mix_pre, 0, token), g_in_e, "inproj_even")
    mixed, pre_a, states = _hgrn2_fwd(proj, lb_table, a_norm, "hgrn2_fwd")
    mixed = _gmlp_fwd(proj, mixed, b_ln_g, b_ln_b, ws, bias_t, "gmlp_fwd")
    w_out_e = full(get_w(["out_e"], mixed)[0])
    x1, mix0 = _outproj([mixed], w_out_e, x0, gain(norm_mix_post, 0, None), "outproj_even")
    w1_0, w2_0 = get_w(["ff1_0", "ff2_0"], x1)
    w2_0 = full(w2_0)
    x2, y0, h_ffn0, r0 = _ffn_fwd(x1, gain(norm_ffn_pre, 0, None), w1_0, w2_0, gain(norm_ffn_post, 0, None), "ffn_fwd_0")
    (g_in_o,) = get_w(["in_o"], x2)
    *qkv, h_mix1 = _norm_inproj_rope(x2, gain(norm_mix_pre, 1, None), g_in_o, rope, "inproj_odd")
    branches = [_attn_branch_fwd(a, "attn_fwd_d%d" % d) for a, d in zip(qkv, C_DILATIONS)]
    attn, attn_b, lse = _attn_merge(branches, "attn_merge")
    w_out_o = full(get_w(["out_o"], attn_b)[0])
    x3, mix1 = _outproj([attn_b], w_out_o, x2, gain(norm_mix_post, 1, None), "outproj_odd")
    w1_1, w2_1 = get_w(["ff1_1", "ff2_1"], x3)
    w2_1 = full(w2_1)
    dx4, y1, h_ffn1, r1, loss_part = _ffn_fwd(x3, gain(norm_ffn_pre, 1, None), w1_1, w2_1, gain(norm_ffn_post, 1, None),
                                              "ffn_fwd_1", target)

    dx3, dy1, da1, dg_ffn_pre1, dg_ffn_post1 = _ffn_bwd(
        dx4, x3, y1, r1, gain(norm_ffn_pre, 1, None), w1_1, w2_1, gain(norm_ffn_post, 1, None), "ffn_bwd_1")
    gw_ff1_1 = _grad_w(h_ffn1, da1, True, "grad_w_ff1_1")
    gw_ff2_1 = _grad_w(r1, dy1, False, "grad_w_ff2_1")
    tok = put_g("ffn1", dict(ff1_1=gw_ff1_1, ff2_1=owners(gw_ff2_1)))
    *dattn, delta, dz1, dg_mix_post1 = _outproj_bwd_attn(dx3, mix1, gain(norm_mix_post, 1, tok), w_out_o, attn,
                                                  "outproj_bwd_odd")
    gw_out_o = _grad_w(attn_b, dz1, False, "grad_w_out_odd")
    per_seq = lambda a: a.reshape(-1, SEQ, LANES)
    grads_c = [_attn_branch_bwd(qkv[b], dattn[b], per_seq(lse), per_seq(delta), "attn_bwd_d%d" % d)
               for b, d in enumerate(C_DILATIONS)]
    dqkv = _attn_combine_bwd(grads_c, rope, "attn_combine_bwd")
    gw_in_o = _grad_w(h_mix1, dqkv, True, "grad_w_in_odd")
    tok = put_g("mix1", dict(out_o=owners(gw_out_o), in_o=gw_in_o))
    dx2, dg_mix_pre1 = _inproj_bwd(dqkv, g_in_o, dx3, x2, gain(norm_mix_pre, 1, tok), "inproj_bwd_odd")

    dx1, dy0, da0, dg_ffn_pre0, dg_ffn_post0 = _ffn_bwd(
        dx2, x1, y0, r0, gain(norm_ffn_pre, 0, None), w1_0, w2_0, gain(norm_ffn_post, 0, None), "ffn_bwd_0")
    gw_ff1_0 = _grad_w(h_ffn0, da0, True, "grad_w_ff1_0")
    gw_ff2_0 = _grad_w(r0, dy0, False, "grad_w_ff2_0")
    tok = put_g("ffn0", dict(ff1_0=gw_ff1_0, ff2_0=owners(gw_ff2_0)))
    dcat, dz0, dg_mix_post0 = _outproj_bwd(dx1, mix0, gain(norm_mix_post, 0, tok), w_out_e, "outproj_bwd_even")
    gw_out_e = _grad_w(mixed, dz0, False, "grad_w_out_even")
    dproj, d_lb, d_a_norm = _hgrn2_bwd(proj, dcat, pre_a, states, lb_table, a_norm, "hgrn2_bwd")
    dproj, d_ln_g, d_ln_b, d_ws, d_bias_t = _gmlp_bwd(proj, dcat, dproj, b_ln_g, b_ln_b, ws, bias_t, "gmlp_bwd")
    gw_in_e = _grad_w(h_mix0, dproj, True, "grad_w_in_even")
    tok = put_g("mix0", dict(out_e=owners(gw_out_e), in_e=gw_in_e))
    dx0, dg_mix_pre0 = _inproj_bwd(dproj, g_in_e, dx1, x0, gain(norm_mix_pre, 0, tok), "inproj_bwd_even")

    layers = lambda a, b: jnp.concatenate([a, b], axis=0)
    return (dx0, loss_part, layers(dg_mix_pre0, dg_mix_pre1), layers(dg_mix_post0, dg_mix_post1),
            layers(dg_ffn_pre0, dg_ffn_pre1), layers(dg_ffn_post0, dg_ffn_post1),
            d_lb, d_a_norm, d_ln_g, d_ln_b, d_ws, d_bias_t)
```

```python
import math

import jax
import jax.numpy as jnp
from jax import lax
from jax.experimental import pallas as pl
from jax.experimental.pallas import tpu as pltpu

F32 = jnp.float32
BF16 = jnp.bfloat16
MESH = pl.DeviceIdType.MESH

N_DEV = 8
D_MODEL = 1024
SEQ = 2048
EPS = 1e-6
A_WIDTH = 512
A_HEADS = 4
HEAD_A = 128
B_WIDTH = 512
B_GROUPS = 4
B_CHUNK = 128
C_HEADS = 16
C_HEAD_DIM = 64
C_ROT_HALF = 8
ROPE_THETA = 500000.0
C_DILATIONS = (1, 4, 16)
C_BLOCK = 128
D_FF = 4096
EVEN_IN = 3072
ODD_IN = 3072

ADAM_LR = 0.001
ADAM_B1 = 0.9
ADAM_B2 = 0.999
ADAM_EPS = 1e-08
ADAM_WD = 0.01
ADAM_STEP = 10

LANES = 128
SUBLANES = 8
ROW_TILE = 512
PROJ_TILE = 1024
MERGE_TILE = 256
SUB_CHUNK = 16
HGRN_BLOCK = 256
NEG = -1e30
VMEM_LIMIT = 56 * 1024 * 1024


def _params(sem):
    return pltpu.CompilerParams(dimension_semantics=sem, vmem_limit_bytes=VMEM_LIMIT)


def _dot(a, b):
    return jnp.dot(a, b, preferred_element_type=F32)


def _dot_nt(a, b):
    return lax.dot_general(a, b, (((1,), (1,)), ((), ())), preferred_element_type=F32)


def _dot_tn(a, b):
    return lax.dot_general(a, b, (((0,), (0,)), ((), ())), preferred_element_type=F32)


def _rms(x, g):
    r = lax.rsqrt(jnp.mean(x * x, axis=-1, keepdims=True) + EPS)
    return x * r * g


def _rms_bwd(x, g, dy):
    r = lax.rsqrt(jnp.mean(x * x, axis=-1, keepdims=True) + EPS)
    dyg = dy * g
    dx = r * dyg - x * (r * r * r) * jnp.mean(x * dyg, axis=-1, keepdims=True)
    return dx, dy * x * r


def _split3(x):
    hi = x.astype(BF16)
    rest = x - hi.astype(F32)
    mid = rest.astype(BF16)
    return hi, mid, (rest - mid.astype(F32)).astype(BF16)


def _mask_dot(mask, x):
    m = mask.astype(BF16)
    hi, mid, lo = _split3(x)
    return _dot(m, hi) + (_dot(m, mid) + _dot(m, lo))


def _dot_mask(x, mask):
    m = mask.astype(BF16)
    hi, mid, lo = _split3(x)
    return _dot(hi, m) + (_dot(mid, m) + _dot(lo, m))


def _rows8(v):
    return v.reshape(v.shape[0] // SUBLANES, SUBLANES, v.shape[1]).sum(axis=0)


def _sigmoid(x):
    return 1.0 / (1.0 + jnp.exp(-x))


def _gelu(x):
    return 0.5 * x * (1.0 + jnp.tanh(math.sqrt(2.0 / math.pi) * (x + 0.044715 * (x * x * x))))


def _acc_rows8(ref, val, first):
    @pl.when(first)
    def _():
        ref[...] = val

    @pl.when(jnp.logical_not(first))
    def _():
        ref[...] += val


def _my_slot():
    return 4 * lax.axis_index("x") + 2 * lax.axis_index("y") + lax.axis_index("c")


def _peer(r):
    x, y, c = lax.axis_index("x"), lax.axis_index("y"), lax.axis_index("c")
    px = 1 - x if (r >> 2) & 1 else x
    py = 1 - y if (r >> 1) & 1 else y
    pc = 1 - c if r & 1 else c
    return (px, py, pc), 4 * px + 2 * py + pc


HBM_SPEC = pl.BlockSpec(memory_space=pltpu.HBM)
SEM_SPEC = pl.BlockSpec(memory_space=pltpu.SEMAPHORE)
SPLIT_EFFECT = pltpu.SideEffectType.DATAFLOW_SIDE_EFFECTING


def _split_copies(land_ref, src_ref, send_sem, recv_sem):
    me = _my_slot()
    copies = []
    for r in range(1, N_DEV):
        peer, slot = _peer(r)
        src = _slot(land_ref, me) if src_ref is None else _slot(src_ref, slot)
        copies.append(pltpu.make_async_remote_copy(
            src_ref=src, dst_ref=_slot(land_ref, me), send_sem=send_sem, recv_sem=recv_sem,
            device_id=peer, device_id_type=MESH))
    return copies


def _slot(ref, s):
    if len(ref.shape) == 2:
        c = ref.shape[1] // N_DEV
        return ref.at[:, pl.ds(pl.multiple_of(s * c, LANES), c)]
    return ref.at[s]


def _exchange_start(lands, sources, name):
    n = len(lands)
    given = [s for s in sources if s is not None]
    arrays = list(lands) + given

    def body(*refs):
        land_refs, src_refs = refs[:n], list(refs[n:n + len(given)])
        sems = refs[len(arrays):len(arrays) + 2 * n]
        token = refs[-1]
        for k in range(n):
            src_ref = None if sources[k] is None else src_refs.pop(0)
            for copy in _split_copies(land_refs[k], src_ref, sems[k], sems[n + k]):
                copy.start()
        token[...] = jnp.zeros_like(token)

    outs = pl.pallas_call(
        body, name=name,
        out_shape=(pltpu.SemaphoreType.DMA(()),) * (2 * n) + tuple(pltpu.HBM(a.shape, a.dtype) for a in arrays)
        + (jax.ShapeDtypeStruct((SUBLANES, LANES), F32),),
        in_specs=[HBM_SPEC] * len(arrays),
        out_specs=(SEM_SPEC,) * (2 * n) + (HBM_SPEC,) * len(arrays) + (pl.BlockSpec(memory_space=pltpu.VMEM),),
        input_output_aliases={i: 2 * n + i for i in range(len(arrays))},
        compiler_params=pltpu.CompilerParams(has_side_effects=SPLIT_EFFECT),
    )(*[pltpu.with_memory_space_constraint(a, pltpu.HBM) for a in arrays])
    return list(outs[:n]), list(outs[n:2 * n]), list(outs[2 * n:3 * n]), list(outs[3 * n:-1]), outs[-1]


def _exchange_wait(lands, sources, send_sems, recv_sems, after, name):
    n = len(lands)
    given = [s for s in sources if s is not None]
    arrays = list(lands) + given

    def body(*refs):
        land_refs, src_refs = refs[:n], list(refs[n:n + len(given)])
        sems = refs[len(arrays):len(arrays) + 2 * n]
        for i in range(n):
            src_ref = None if sources[i] is None else src_refs.pop(0)
            copies = _split_copies(land_refs[i], src_ref, sems[i], sems[n + i])
            for copy in copies:
                copy.wait_recv()
            for copy in copies:
                copy.wait_send()

    outs = pl.pallas_call(
        body, name=name, out_shape=tuple(pltpu.HBM(a.shape, a.dtype) for a in arrays),
        in_specs=[HBM_SPEC] * len(arrays) + [SEM_SPEC] * (2 * n) + [pl.BlockSpec(memory_space=pl.ANY)],
        out_specs=(HBM_SPEC,) * len(arrays),
        input_output_aliases={i: i for i in range(len(arrays))},
        compiler_params=pltpu.CompilerParams(has_side_effects=SPLIT_EFFECT),
    )(*arrays, *send_sems, *recv_sems, after)
    return list(outs[:n])


def _place_own(a, me, name, own_block, dtype=BF16, after=None):
    shape = a.shape[1:] if own_block else a.shape
    cols = shape[-1]
    a3 = a.reshape((N_DEV if own_block else 1, -1, cols))
    rows = a3.shape[1]
    tr = min(rows, 512)

    def body(me_ref, a_ref, _, o_ref):
        o_ref[...] = a_ref[...].astype(dtype)

    grid_spec = pltpu.PrefetchScalarGridSpec(
        num_scalar_prefetch=1, grid=(rows // tr,),
        in_specs=[pl.BlockSpec((1, tr, cols), lambda i, me_ref: (me_ref[0] if own_block else 0, i, 0)),
                  pl.BlockSpec(memory_space=pl.ANY)],
        out_specs=pl.BlockSpec((1, tr, cols), lambda i, me_ref: (me_ref[0], i, 0)))
    out = pl.pallas_call(
        body, name=name, grid_spec=grid_spec, out_shape=jax.ShapeDtypeStruct((N_DEV, rows, cols), dtype),
        compiler_params=_params(("arbitrary",)),
    )(me, a3, a3 if after is None else after)
    return out.reshape((N_DEV,) + shape)


def _place_own_columns(a, me, name, after=None):
    rows, cols = a.shape
    tr = min(rows, 512)

    def body(me_ref, a_ref, _, o_ref):
        o_ref[...] = a_ref[...].astype(BF16)

    grid_spec = pltpu.PrefetchScalarGridSpec(
        num_scalar_prefetch=1, grid=(rows // tr,),
        in_specs=[pl.BlockSpec((tr, cols), lambda i, me_ref: (i, 0)), pl.BlockSpec(memory_space=pl.ANY)],
        out_specs=pl.BlockSpec((tr, cols), lambda i, me_ref: (i, me_ref[0])))
    return pl.pallas_call(
        body, name=name, grid_spec=grid_spec, out_shape=jax.ShapeDtypeStruct((rows, N_DEV * cols), BF16),
        compiler_params=_params(("arbitrary",)),
    )(me, a, a if after is None else after)


def _rope_tables(positions):
    in_head = jnp.arange(LANES) % C_HEAD_DIM
    inv = ROPE_THETA ** (-(in_head % C_ROT_HALF).astype(F32) / C_ROT_HALF)
    ang = positions.reshape(-1)[:, None].astype(F32) * inv
    rotated = in_head < 2 * C_ROT_HALF
    sin = jnp.sin(ang)
    return (jnp.where(rotated, jnp.cos(ang), 1.0),
            jnp.where(in_head < C_ROT_HALF, -sin, jnp.where(rotated, sin, 0.0)))


def _swap_halves(x):
    lane = lax.broadcasted_iota(jnp.int32, x.shape, 1) % C_HEAD_DIM
    return jnp.where(lane < C_ROT_HALF, pltpu.roll(x, LANES - C_ROT_HALF, 1), pltpu.roll(x, C_ROT_HALF, 1))


def _norm_inproj(x, g, w, name):
    t = x.shape[0]
    n = w.shape[1]
    tm, tn = ROW_TILE, n

    def body(x_ref, g_ref, w_ref, o_ref, h_ref):
        @pl.when(pl.program_id(1) == 0)
        def _():
            h_ref[...] = _rms(x_ref[...], g_ref[...]).astype(BF16)

        o_ref[...] = _dot(h_ref[...], w_ref[...])

    return pl.pallas_call(
        body, name=name, grid=(t // tm, n // tn),
        in_specs=[pl.BlockSpec((tm, D_MODEL), lambda i, j: (i, 0)), pl.BlockSpec((1, D_MODEL), lambda i, j: (0, 0)),
                  pl.BlockSpec((D_MODEL, tn), lambda i, j: (0, j))],
        out_specs=[pl.BlockSpec((tm, tn), lambda i, j: (i, j)), pl.BlockSpec((tm, D_MODEL), lambda i, j: (i, 0))],
        out_shape=[jax.ShapeDtypeStruct((t, n), F32), jax.ShapeDtypeStruct((t, D_MODEL), BF16)],
        compiler_params=_params(("parallel", "arbitrary")),
    )(x, g, w)


def _dilated_specs(tm, width, col_of):
    per_seq = SEQ // tm
    specs = []
    for d in C_DILATIONS:
        specs.append(pl.BlockSpec(
            (1, d, tm // d, width), lambda i, *rest: (i // per_seq, 0, i % per_seq, col_of(*rest))))
    return specs


def _dilated_shapes(n_seq, cols, dtype):
    return [jax.ShapeDtypeStruct((n_seq, d, SEQ // d, cols), dtype) for d in C_DILATIONS]


def _store_dilated(src_ref, out_refs, dtype):
    groups, tm, _ = src_ref.shape
    for d, o_ref in zip(C_DILATIONS, out_refs):
        for r in range(d):
            rows = pl.ds(r, tm // d, stride=d) if d > 1 else slice(None)
            for p in range(groups):
                o_ref[0, r, :, p * LANES:(p + 1) * LANES] = src_ref.at[p][rows, :].astype(dtype)


def _load_dilated(in_ref, d, dst_ref):
    groups, tm, _ = dst_ref.shape
    for r in range(d):
        rows = pl.ds(r, tm // d, stride=d)
        for p in range(groups):
            dst_ref.at[p][rows, :] = in_ref[0, r, :, p * LANES:(p + 1) * LANES].astype(F32)


def _norm_inproj_rope(x, g, w, rope, name):
    t = x.shape[0]
    n = w.shape[1]
    tm, nb = ROW_TILE, n

    def body(x_ref, g_ref, w_ref, c_ref, s_ref, o1_ref, o4_ref, o16_ref, h_ref, tile_ref):
        j = pl.program_id(1)

        @pl.when(j == 0)
        def _():
            h_ref[...] = _rms(x_ref[...], g_ref[...]).astype(BF16)

        acc = _dot(h_ref[...], w_ref[...])
        for p in range(nb // LANES):
            blk = acc[:, p * LANES:(p + 1) * LANES]
            roped = blk * c_ref[...] + _swap_halves(blk) * s_ref[...]
            piece = j * (nb // LANES) + p
            is_qk = piece < 2 * (D_MODEL // LANES)
            tile_ref[p] = jnp.where(is_qk, roped, blk) * jnp.where(piece < D_MODEL // LANES, QK_SCALE, 1.0)
        _store_dilated(tile_ref, (o1_ref, o4_ref, o16_ref), BF16)

    return pl.pallas_call(
        body, name=name, grid=(t // tm, n // nb),
        in_specs=[pl.BlockSpec((tm, D_MODEL), lambda i, j: (i, 0)), pl.BlockSpec((1, D_MODEL), lambda i, j: (0, 0)),
                  pl.BlockSpec((D_MODEL, nb), lambda i, j: (0, j)),
                  pl.BlockSpec((tm, LANES), lambda i, j: (i, 0)), pl.BlockSpec((tm, LANES), lambda i, j: (i, 0))],
        out_specs=_dilated_specs(tm, nb, lambda j: j) + [pl.BlockSpec((tm, D_MODEL), lambda i, j: (i, 0))],
        out_shape=_dilated_shapes(t // SEQ, n, BF16) + [jax.ShapeDtypeStruct((t, D_MODEL), BF16)],
        scratch_shapes=[pltpu.VMEM((nb // LANES, tm, LANES), F32)],
        compiler_params=_params(("parallel", "arbitrary")),
    )(x, g, w, *rope)


def _outproj(parts, w, x, g, name):
    t = x.shape[0]
    tm = PROJ_TILE
    n = len(parts)
    widths = [p.shape[1] for p in parts]

    def body(*refs):
        p_refs = refs[:n]
        w_ref, x_ref, g_ref, xo_ref, mix_ref = refs[n:]
        mix = None
        off = 0
        for p_ref, wd in zip(p_refs, widths):
            term = _dot(p_ref[...].astype(BF16), w_ref[off:off + wd, :])
            mix = term if mix is None else mix + term
            off += wd
        mix_ref[...] = mix
        xo_ref[...] = x_ref[...] + _rms(mix, g_ref[...])

    row = lambda i: (i, 0)
    return pl.pallas_call(
        body, name=name, grid=(t // tm,),
        in_specs=[pl.BlockSpec((tm, wd), row) for wd in widths] + [
            pl.BlockSpec((sum(widths), D_MODEL), lambda i: (0, 0)),
            pl.BlockSpec((tm, D_MODEL), row), pl.BlockSpec((1, D_MODEL), lambda i: (0, 0))],
        out_specs=[pl.BlockSpec((tm, D_MODEL), row)] * 2,
        out_shape=[jax.ShapeDtypeStruct((t, D_MODEL), F32)] * 2,
        compiler_params=_params(("parallel",)),
    )(*parts, w, x, g)


def _outproj_bwd(dx, mix, g, w, name):
    t = dx.shape[0]
    tm = PROJ_TILE
    k = w.shape[0]

    def body(dx_ref, mix_ref, g_ref, w_ref, dcat_ref, dz_ref, dg_ref):
        dz, dgr = _rms_bwd(mix_ref[...], g_ref[...], dx_ref[...])
        dzb = dz.astype(BF16)
        dz_ref[...] = dzb
        dcat_ref[...] = _dot_nt(dzb, w_ref[...])
        _acc_rows8(dg_ref, _rows8(dgr), pl.program_id(0) == 0)

    row = lambda i: (i, 0)
    return pl.pallas_call(
        body, name=name, grid=(t // tm,),
        in_specs=[pl.BlockSpec((tm, D_MODEL), row), pl.BlockSpec((tm, D_MODEL), row),
                  pl.BlockSpec((1, D_MODEL), lambda i: (0, 0)), pl.BlockSpec((k, D_MODEL), lambda i: (0, 0))],
        out_specs=[pl.BlockSpec((tm, k), row), pl.BlockSpec((tm, D_MODEL), row),
                   pl.BlockSpec((SUBLANES, D_MODEL), lambda i: (0, 0))],
        out_shape=[jax.ShapeDtypeStruct((t, k), F32), jax.ShapeDtypeStruct((t, D_MODEL), BF16),
                   jax.ShapeDtypeStruct((SUBLANES, D_MODEL), F32)],
        compiler_params=_params(("arbitrary",)),
    )(dx, mix, g, w)


def _outproj_bwd_attn(dx, mix, g, w, out, name):
    t = dx.shape[0]
    tm = MERGE_TILE

    def body(dx_ref, mix_ref, g_ref, w_ref, out_ref, do1, do4, do16, dl_ref, dz_ref, dg_ref, tile_ref):
        dz, dgr = _rms_bwd(mix_ref[...], g_ref[...], dx_ref[...])
        dzb = dz.astype(BF16)
        dz_ref[...] = dzb
        _acc_rows8(dg_ref, _rows8(dgr), pl.program_id(0) == 0)
        dout = _dot_nt(dzb, w_ref[...])
        for p in range(LANE_GROUPS):
            tile_ref[p] = dout[:, p * LANES:(p + 1) * LANES]
        _store_dilated(tile_ref, (do1, do4, do16), BF16)
        column = lax.broadcasted_iota(jnp.int32, (D_MODEL, LANES), 0) // C_HEAD_DIM
        head = lax.broadcasted_iota(jnp.int32, (D_MODEL, LANES), 1)
        dl_ref[...] = _dot_mask(dout * out_ref[...], column == head)

    row = lambda i: (i, 0)
    n_seq = t // SEQ
    return pl.pallas_call(
        body, name=name, grid=(t // tm,),
        in_specs=[pl.BlockSpec((tm, D_MODEL), row), pl.BlockSpec((tm, D_MODEL), row),
                  pl.BlockSpec((1, D_MODEL), lambda i: (0, 0)), pl.BlockSpec((D_MODEL, D_MODEL), lambda i: (0, 0)),
                  pl.BlockSpec((tm, D_MODEL), row)],
        out_specs=_dilated_specs(tm, D_MODEL, lambda: 0) + [
            pl.BlockSpec((tm, LANES), row), pl.BlockSpec((tm, D_MODEL), row),
            pl.BlockSpec((SUBLANES, D_MODEL), lambda i: (0, 0))],
        out_shape=_dilated_shapes(n_seq, D_MODEL, BF16) + [
            jax.ShapeDtypeStruct((t, LANES), F32), jax.ShapeDtypeStruct((t, D_MODEL), BF16),
            jax.ShapeDtypeStruct((SUBLANES, D_MODEL), F32)],
        scratch_shapes=[pltpu.VMEM((LANE_GROUPS, tm, LANES), F32)],
        compiler_params=_params(("arbitrary",)),
    )(dx, mix, g, w, out)


def _inproj_bwd(dproj, w, dx, x, g, name):
    t = x.shape[0]
    n = w.shape[1]
    tm = ROW_TILE

    def body(dp_ref, w_ref, dx_ref, x_ref, g_ref, o_ref, dg_ref):
        dxn, dgr = _rms_bwd(x_ref[...], g_ref[...], _dot_nt(dp_ref[...], w_ref[...]))
        o_ref[...] = dx_ref[...] + dxn
        _acc_rows8(dg_ref, _rows8(dgr), pl.program_id(0) == 0)

    row = lambda i: (i, 0)
    return pl.pallas_call(
        body, name=name, grid=(t // tm,),
        in_specs=[pl.BlockSpec((tm, n), row), pl.BlockSpec((D_MODEL, n), lambda i: (0, 0)),
                  pl.BlockSpec((tm, D_MODEL), row), pl.BlockSpec((tm, D_MODEL), row),
                  pl.BlockSpec((1, D_MODEL), lambda i: (0, 0))],
        out_specs=[pl.BlockSpec((tm, D_MODEL), row), pl.BlockSpec((SUBLANES, D_MODEL), lambda i: (0, 0))],
        out_shape=[jax.ShapeDtypeStruct((t, D_MODEL), F32), jax.ShapeDtypeStruct((SUBLANES, D_MODEL), F32)],
        compiler_params=_params(("arbitrary",)),
    )(dproj, w, dx, x, g)


def _grad_w(a, b, col_blocks, name):
    a_planes, b_planes = a.ndim == 3, b.ndim == 3
    t = a.shape[-2]
    k = a.shape[0] * a.shape[2] if a_planes else a.shape[1]
    n = b.shape[0] * b.shape[2] if b_planes else b.shape[1]
    tk = a.shape[2] if a_planes else min(k, 1024)
    per_owner = n // N_DEV
    tn = 2 * per_owner if col_blocks else min(n, 1024)
    assert not b_planes or b.shape[2] == tn

    def body(a_ref, b_ref, o_ref, at_ref):
        @pl.when(pl.program_id(1) == 0)
        def _():
            for c in range(t // ROW_TILE):
                rows = slice(c * ROW_TILE, (c + 1) * ROW_TILE)
                at_ref[:, rows] = (a_ref[0, rows, :] if a_planes else a_ref[rows, :]).T

        res = _dot(at_ref[...], b_ref[0] if b_planes else b_ref[...]).astype(BF16)
        if col_blocks:
            o_ref[0] = res[:, :per_owner]
            o_ref[1] = res[:, per_owner:]
        else:
            o_ref[...] = res

    if col_blocks:
        out_spec = pl.BlockSpec((2, tk, per_owner), lambda i, j: (j, i, 0))
        out_shape = jax.ShapeDtypeStruct((N_DEV, k, per_owner), BF16)
    else:
        out_spec = pl.BlockSpec((tk, tn), lambda i, j: (i, j))
        out_shape = jax.ShapeDtypeStruct((k, n), BF16)
    a_spec = pl.BlockSpec((1, t, tk), lambda i, j: (i, 0, 0)) if a_planes else pl.BlockSpec((t, tk), lambda i, j: (0, i))
    b_spec = pl.BlockSpec((1, t, tn), lambda i, j: (j, 0, 0)) if b_planes else pl.BlockSpec((t, tn), lambda i, j: (0, j))
    return pl.pallas_call(
        body, name=name, grid=(k // tk, n // tn),
        in_specs=[a_spec, b_spec], out_specs=out_spec, out_shape=out_shape,
        scratch_shapes=[pltpu.VMEM((tk, t), BF16)],
        compiler_params=_params(("parallel", "arbitrary")),
    )(a, b)


FF_STEP = 1024
FF_STEPS = D_FF // FF_STEP


def _ffn_fwd(x, g_pre, w1, w2, g_post, name, target=None):
    t = x.shape[0]
    tm = PROJ_TILE

    def body(*refs):
        if target is None:
            x_ref, gp_ref, w1_ref, w2_ref, gq_ref, xo_ref, y_ref, h_ref, r_ref = refs
        else:
            x_ref, gp_ref, w1_ref, w2_ref, gq_ref, t_ref, xo_ref, y_ref, h_ref, r_ref, l_ref = refs
        i, j = pl.program_id(0), pl.program_id(1)

        @pl.when(j == 0)
        def _():
            h_ref[...] = _rms(x_ref[...], gp_ref[...]).astype(BF16)

        a = _dot(h_ref[...], w1_ref[...])
        r = jnp.square(jnp.maximum(a, 0.0)).astype(BF16)
        r_ref[0] = r
        term = _dot(r, w2_ref[...])

        @pl.when(j == 0)
        def _():
            y_ref[...] = term

        @pl.when(j > 0)
        def _():
            y_ref[...] += term

        @pl.when(j == FF_STEPS - 1)
        def _():
            x_new = x_ref[...] + _rms(y_ref[...], gq_ref[...])
            if target is None:
                xo_ref[...] = x_new
            else:
                diff = x_new - t_ref[...]
                xo_ref[...] = diff * (1.0 / D_MODEL)
                _acc_rows8(l_ref, _rows8(diff * diff) * (0.5 / D_MODEL), i == 0)

    row = lambda i, j: (i, 0)
    vec = pl.BlockSpec((1, D_MODEL), lambda i, j: (0, 0))
    in_specs = [pl.BlockSpec((tm, D_MODEL), row), vec, pl.BlockSpec((D_MODEL, FF_STEP), lambda i, j: (0, j)),
                pl.BlockSpec((FF_STEP, D_MODEL), lambda i, j: (j, 0)), vec]
    out_specs = [pl.BlockSpec((tm, D_MODEL), row)] * 3 + [pl.BlockSpec((1, tm, FF_STEP), lambda i, j: (j, i, 0))]
    out_shape = [jax.ShapeDtypeStruct((t, D_MODEL), F32), jax.ShapeDtypeStruct((t, D_MODEL), F32),
                 jax.ShapeDtypeStruct((t, D_MODEL), BF16), jax.ShapeDtypeStruct((FF_STEPS, t, FF_STEP), BF16)]
    args = [x, g_pre, w1, w2, g_post]
    if target is not None:
        in_specs.append(pl.BlockSpec((tm, D_MODEL), row))
        out_specs.append(pl.BlockSpec((SUBLANES, D_MODEL), lambda i, j: (0, 0)))
        out_shape.append(jax.ShapeDtypeStruct((SUBLANES, D_MODEL), F32))
        args.append(target)
    return pl.pallas_call(
        body, name=name, grid=(t // tm, FF_STEPS), in_specs=in_specs, out_specs=out_specs, out_shape=out_shape,
        compiler_params=_params(("parallel" if target is None else "arbitrary", "arbitrary")),
    )(*args)


def _ffn_bwd(dxo, x, y, r, g_pre, w1, w2, g_post, name):
    t = x.shape[0]
    tm = ROW_TILE

    def body(dxo_ref, x_ref, y_ref, r_ref, gp_ref, w1_ref, w2_ref, gq_ref,
             dx_ref, dy_ref, da_ref, dgp_ref, dgq_ref, acc_ref):
        i, j = pl.program_id(0), pl.program_id(1)

        @pl.when(j == 0)
        def _():
            dy, dgr = _rms_bwd(y_ref[...], gq_ref[...], dxo_ref[...])
            dy_ref[...] = dy.astype(BF16)
            _acc_rows8(dgq_ref, _rows8(dgr), i == 0)

        dr = _dot_nt(dy_ref[...], w2_ref[...])
        da = (dr * (2.0 * jnp.sqrt(r_ref[0].astype(F32)))).astype(BF16)
        da_ref[0] = da
        term = _dot_nt(da, w1_ref[...])

        @pl.when(j == 0)
        def _():
            acc_ref[...] = term

        @pl.when(j > 0)
        def _():
            acc_ref[...] += term

        @pl.when(j == FF_STEPS - 1)
        def _():
            dxn, dgr = _rms_bwd(x_ref[...], gp_ref[...], acc_ref[...])
            dx_ref[...] = dxo_ref[...] + dxn
            _acc_rows8(dgp_ref, _rows8(dgr), i == 0)

    row = lambda i, j: (i, 0)
    vec = pl.BlockSpec((1, D_MODEL), lambda i, j: (0, 0))
    acc8 = pl.BlockSpec((SUBLANES, D_MODEL), lambda i, j: (0, 0))
    return pl.pallas_call(
        body, name=name, grid=(t // tm, FF_STEPS),
        in_specs=[pl.BlockSpec((tm, D_MODEL), row)] * 3 + [
            pl.BlockSpec((1, tm, FF_STEP), lambda i, j: (j, i, 0)),
            vec, pl.BlockSpec((D_MODEL, FF_STEP), lambda i, j: (0, j)),
            pl.BlockSpec((FF_STEP, D_MODEL), lambda i, j: (j, 0)), vec],
        out_specs=[pl.BlockSpec((tm, D_MODEL), row), pl.BlockSpec((tm, D_MODEL), row),
                   pl.BlockSpec((1, tm, FF_STEP), lambda i, j: (j, i, 0)), acc8, acc8],
        out_shape=[jax.ShapeDtypeStruct((t, D_MODEL), F32), jax.ShapeDtypeStruct((t, D_MODEL), BF16),
                   jax.ShapeDtypeStruct((FF_STEPS, t, FF_STEP), BF16),
                   jax.ShapeDtypeStruct((SUBLANES, D_MODEL), F32), jax.ShapeDtypeStruct((SUBLANES, D_MODEL), F32)],
        scratch_shapes=[pltpu.VMEM((tm, D_MODEL), F32)],
        compiler_params=_params(("arbitrary", "arbitrary")),
    )(dxo, x, y, r, g_pre, w1, w2, g_post)


def _lower_bound(table):
    e = jnp.exp(table - jnp.max(table, axis=0, keepdims=True))
    return e[0:1, :] / jnp.sum(e, axis=0, keepdims=True)


def _hgrn2_block(q_ref, f_ref, lb):
    tb = f_ref.shape[0]
    sig = _sigmoid(f_ref[...])
    f = lb + (1.0 - lb) * sig
    qraw = q_ref[...]
    sq = _sigmoid(qraw)
    r = lax.broadcasted_iota(jnp.int32, (tb, tb), 0)
    c = lax.broadcasted_iota(jnp.int32, (tb, tb), 1)
    same = (r // SUB_CHUNK) == (c // SUB_CHUNK)
    logf = jnp.log(f)
    gsum = _mask_dot(same & (c <= r), logf)
    glast = _mask_dot(same, logf)
    return dict(sig=sig, f=f, kk=1.0 - f, qraw=qraw, sq=sq, qs=qraw * sq, gsum=gsum,
                eg=jnp.exp(gsum), ekd=jnp.exp(glast - gsum), a=jnp.exp(glast))


def _head_sums(x):
    parts = [jnp.broadcast_to(jnp.sum(x[:, h * HEAD_A:(h + 1) * HEAD_A], axis=1, keepdims=True), (x.shape[0], HEAD_A))
             for h in range(A_HEADS)]
    return jnp.concatenate(parts, axis=1)


def _hgrn2_intra(g, kk, qs, v):
    row = lax.broadcasted_iota(jnp.int32, g.shape, 0)
    o = _head_sums(qs * kk) * v
    for j in range(1, SUB_CHUNK):
        decay = jnp.exp(jnp.where(row >= j, g - pltpu.roll(g, j, 0), NEG))
        o = o + _head_sums(qs * pltpu.roll(kk, j, 0) * decay) * pltpu.roll(v, j, 0)
    return o


def _hgrn2_intra_bwd(g, kk, qs, v, do):
    row = lax.broadcasted_iota(jnp.int32, g.shape, 0)
    dsc = _head_sums(do * v)
    dqs, dkk, dv = dsc * kk, dsc * qs, _head_sums(qs * kk) * do
    for j in range(1, SUB_CHUNK):
        k_dn = pltpu.roll(kk, j, 0)
        decay = jnp.exp(jnp.where(row >= j, g - pltpu.roll(g, j, 0), NEG))
        d_score = _head_sums(do * pltpu.roll(v, j, 0)) * decay
        dqs = dqs + d_score * k_dn
        dkk = dkk + pltpu.roll(d_score * qs, SUB_CHUNK - j, 0)
        dv = dv + pltpu.roll(_head_sums(qs * k_dn * decay) * do, SUB_CHUNK - j, 0)
    return dqs, dkk, dv


def _hgrn2_fwd(proj, lb_table, a_norm, name):
    t = proj.shape[0]
    tb = HGRN_BLOCK
    n_tb = SEQ // tb
    n_seq = t // SEQ
    n_sub = tb // SUB_CHUNK

    def body(q_ref, f_ref, i_ref, g_ref, lbt_ref, an_ref, o_ref, pre_ref, sts_ref, st_ref,
             gs_ref, kk_ref, qs_ref, eg_ref, ekd_ref, a_ref):
        @pl.when(pl.program_id(1) == 0)
        def _():
            st_ref[...] = jnp.zeros_like(st_ref)

        an = an_ref[...]
        blk = _hgrn2_block(q_ref, f_ref, _lower_bound(lbt_ref[...]))
        for ref, key in ((gs_ref, "gsum"), (kk_ref, "kk"), (qs_ref, "qs"), (eg_ref, "eg"), (ekd_ref, "ekd"), (a_ref, "a")):
            ref[...] = blk[key]

        def step(c, carry):
            rows = pl.ds(pl.multiple_of(c * SUB_CHUNK, SUB_CHUNK), SUB_CHUNK)
            kk, qs, v = kk_ref[rows, :], qs_ref[rows, :], i_ref[rows, :]
            o = _hgrn2_intra(gs_ref[rows, :], kk, qs, v)
            qg, kd, vb = (qs * eg_ref[rows, :]).astype(BF16), (kk * ekd_ref[rows, :]).astype(BF16), v.astype(BF16)
            for h in range(A_HEADS):
                lanes = slice(h * HEAD_A, (h + 1) * HEAD_A)
                st = st_ref[h]
                sts_ref[0, c, h] = st
                o_h = o[:, lanes] + _dot_nt(qg[:, lanes], st.astype(BF16))
                st_ref[h] = st * a_ref[rows, lanes][0:1] + _dot_tn(vb[:, lanes], kd[:, lanes])
                pre_ref[rows, lanes] = o_h
                graw = g_ref[rows, lanes]
                o_ref[rows, lanes] = (_rms(o_h, an[:, lanes]) * (graw * _sigmoid(graw))).astype(BF16)
            return carry

        lax.fori_loop(0, n_sub, step, 0, unroll=2)

    def col(k):
        return pl.BlockSpec((tb, A_WIDTH), lambda b, s, k=k: (b * n_tb + s, k))

    out_rows = pl.BlockSpec((tb, A_WIDTH), lambda b, s: (b * n_tb + s, 0))
    return pl.pallas_call(
        body, name=name, grid=(n_seq, n_tb),
        in_specs=[col(0), col(1), col(2), col(3),
                  pl.BlockSpec((3, A_WIDTH), lambda b, s: (0, 0)), pl.BlockSpec((1, A_WIDTH), lambda b, s: (0, 0))],
        out_specs=[out_rows, out_rows,
                   pl.BlockSpec((1, n_sub, A_HEADS, HEAD_A, HEAD_A), lambda b, s: (b * n_tb + s, 0, 0, 0, 0))],
        out_shape=[jax.ShapeDtypeStruct((t, D_MODEL), BF16), jax.ShapeDtypeStruct((t, A_WIDTH), F32),
                   jax.ShapeDtypeStruct((n_seq * n_tb, n_sub, A_HEADS, HEAD_A, HEAD_A), F32)],
        scratch_shapes=[pltpu.VMEM((A_HEADS, HEAD_A, HEAD_A), F32)] + [pltpu.VMEM((tb, A_WIDTH), F32)] * 6,
        compiler_params=_params(("parallel", "arbitrary")),
    )(proj, proj, proj, proj, lb_table, a_norm)


def _hgrn2_bwd(proj, dcat, pre, states, lb_table, a_norm, name):
    t = proj.shape[0]
    tb = HGRN_BLOCK
    n_tb = SEQ // tb
    n_seq = t // SEQ
    n_sub = tb // SUB_CHUNK

    def body(q_ref, f_ref, i_ref, g_ref, do_ref, pre_ref, sts_ref, lbt_ref, an_ref, dp_ref, dlb_ref, dan_ref, dst_ref,
             gs_ref, kk_ref, qs_ref, eg_ref, ekd_ref, a_ref, dpre_ref, dlf_ref, dqs_ref, dkk_ref):
        b, s = pl.program_id(0), pl.program_id(1)

        @pl.when(s == 0)
        def _():
            dst_ref[...] = jnp.zeros_like(dst_ref)

        @pl.when((b == 0) & (s == 0))
        def _():
            dlb_ref[...] = jnp.zeros_like(dlb_ref)
            dan_ref[...] = jnp.zeros_like(dan_ref)

        lb = _lower_bound(lbt_ref[...])
        an = an_ref[...]
        heads = [slice(h * HEAD_A, (h + 1) * HEAD_A) for h in range(A_HEADS)]
        blk = _hgrn2_block(q_ref, f_ref, lb)
        for ref, key in ((gs_ref, "gsum"), (kk_ref, "kk"), (qs_ref, "qs"), (eg_ref, "eg"), (ekd_ref, "ekd"), (a_ref, "a")):
            ref[...] = blk[key]
        for h, lanes in enumerate(heads):
            graw, o = g_ref[:, lanes], pre_ref[:, lanes]
            sg = _sigmoid(graw)
            dout = do_ref[:, lanes]
            d_o, dgr = _rms_bwd(o, an[:, lanes], dout * (graw * sg))
            dan_ref[0:1, lanes] += jnp.sum(dgr, axis=0, keepdims=True)
            dp_ref[:, 3 * A_WIDTH + h * HEAD_A:3 * A_WIDTH + (h + 1) * HEAD_A] = (
                dout * _rms(o, an[:, lanes]) * (sg * (1.0 + graw * (1.0 - sg)))).astype(BF16)
            dpre_ref[:, lanes] = d_o

        tri_t = (lax.broadcasted_iota(jnp.int32, (SUB_CHUNK, SUB_CHUNK), 0)
                 <= lax.broadcasted_iota(jnp.int32, (SUB_CHUNK, SUB_CHUNK), 1)).astype(F32)

        def back(k, carry):
            c = n_sub - 1 - k
            rows = pl.ds(pl.multiple_of(c * SUB_CHUNK, SUB_CHUNK), SUB_CHUNK)
            g, kk, qs, v, d_o = gs_ref[rows, :], kk_ref[rows, :], qs_ref[rows, :], i_ref[rows, :], dpre_ref[rows, :]
            eg, ekd, a = eg_ref[rows, :], ekd_ref[rows, :], a_ref[rows, :]
            dqs, dkk, dv = _hgrn2_intra_bwd(g, kk, qs, v, d_o)
            qg_f, kd_f = qs * eg, kk * ekd
            qg, kd, vb, dob = qg_f.astype(BF16), kd_f.astype(BF16), v.astype(BF16), d_o.astype(BF16)
            dqg, dkd, da, dv_st = [], [], [], []
            for h, lanes in enumerate(heads):
                st, dst = sts_ref[0, c, h], dst_ref[h]
                dstb = dst.astype(BF16)
                dqg.append(_dot(dob[:, lanes], st.astype(BF16)))
                dv_st.append(_dot_nt(kd[:, lanes], dstb))
                dkd.append(_dot(vb[:, lanes], dstb))
                da.append(jnp.broadcast_to(jnp.sum(dst * st, axis=0, keepdims=True), (SUB_CHUNK, HEAD_A)))
                dst_ref[h] = dst * a[0:1, lanes] + _dot_tn(dob[:, lanes], qg[:, lanes])
            dqg, dkd, da, dv_st = [jnp.concatenate(p, axis=1) for p in (dqg, dkd, da, dv_st)]
            d_gsum = qs * dqs - kk * dkk + dqg * qg_f - dkd * kd_f
            d_glast = jnp.sum(dkd * kd_f, axis=0, keepdims=True) + da * a
            dlf_ref[rows, :] = jnp.dot(tri_t, d_gsum, precision=lax.Precision.HIGHEST,
                                       preferred_element_type=F32) + d_glast
            dqs_ref[rows, :] = dqs + dqg * eg
            dkk_ref[rows, :] = dkk + dkd * ekd
            dp_ref[rows, 2 * A_WIDTH:3 * A_WIDTH] = (dv + dv_st).astype(BF16)
            return carry

        lax.fori_loop(0, n_sub, back, 0, unroll=2)
        sig, sq, qraw = blk["sig"], blk["sq"], blk["qraw"]
        d_f = dlf_ref[...] / blk["f"] - dkk_ref[...]
        dlb_ref[0:1, :] += jnp.sum(d_f * (1.0 - sig), axis=0, keepdims=True)
        dp_ref[:, 0:A_WIDTH] = (dqs_ref[...] * (sq * (1.0 + qraw * (1.0 - sq)))).astype(BF16)
        dp_ref[:, A_WIDTH:2 * A_WIDTH] = (d_f * (1.0 - lb) * sig * (1.0 - sig)).astype(BF16)

    def rev(s):
        return n_tb - 1 - s

    def col(k):
        return pl.BlockSpec((tb, A_WIDTH), lambda b, s, k=k: (b * n_tb + rev(s), k))

    acc8 = pl.BlockSpec((SUBLANES, A_WIDTH), lambda b, s: (0, 0))
    return pl.pallas_call(
        body, name=name, grid=(n_seq, n_tb),
        in_specs=[col(0), col(1), col(2), col(3), col(0), col(0),
                  pl.BlockSpec((1, n_sub, A_HEADS, HEAD_A, HEAD_A), lambda b, s: (b * n_tb + rev(s), 0, 0, 0, 0)),
                  pl.BlockSpec((3, A_WIDTH), lambda b, s: (0, 0)), pl.BlockSpec((1, A_WIDTH), lambda b, s: (0, 0))],
        out_specs=[pl.BlockSpec((tb, 4 * A_WIDTH), lambda b, s: (b * n_tb + rev(s), 0)), acc8, acc8],
        out_shape=[jax.ShapeDtypeStruct((t, EVEN_IN), BF16)] + [jax.ShapeDtypeStruct((SUBLANES, A_WIDTH), F32)] * 2,
        scratch_shapes=[pltpu.VMEM((A_HEADS, HEAD_A, HEAD_A), F32)] + [pltpu.VMEM((tb, A_WIDTH), F32)] * 10,
        compiler_params=_params(("arbitrary", "arbitrary")),
    )(proj, proj, proj, proj, dcat, pre, states, lb_table, a_norm)


GMLP_ROWS = 512


def _gmlp_chunk(ub, vb, ln_g, ln_b, ws, bias):
    u = [_gelu(a) for a in ub]
    v = [_gelu(a) for a in vb]
    mu = sum(jnp.sum(a, axis=-1, keepdims=True) for a in v) * (1.0 / B_WIDTH)
    cen = [a - mu for a in v]
    var = sum(jnp.sum(a * a, axis=-1, keepdims=True) for a in cen) * (1.0 / B_WIDTH)
    inv = lax.rsqrt(var + EPS)
    r = lax.broadcasted_iota(jnp.int32, (B_CHUNK, B_CHUNK), 0)
    c = lax.broadcasted_iota(jnp.int32, (B_CHUNK, B_CHUNK), 1)
    outs = []
    for g in range(B_GROUPS):
        vn = (cen[g] * inv * ln_g[g] + ln_b[g]).astype(BF16)
        wm = jnp.where(c <= r, ws[g], 0.0).astype(BF16)
        outs.append(u[g] * (_dot(wm, vn) + bias[g]))
    return outs


def _lane_groups(ref, rows=slice(None)):
    return [ref[rows, g * LANES:(g + 1) * LANES] for g in range(B_GROUPS)]


def _gmlp_fwd(proj, mixed, ln_g, ln_b, ws, bias_t, name):
    t = proj.shape[0]
    tm = GMLP_ROWS

    def body(u_ref, v_ref, lg_ref, lb_ref, ws_ref, bt_ref, _, o_ref):
        for ch in range(tm // B_CHUNK):
            rows = slice(ch * B_CHUNK, (ch + 1) * B_CHUNK)
            outs = _gmlp_chunk(_lane_groups(u_ref, rows), _lane_groups(v_ref, rows), _lane_groups(lg_ref),
                               _lane_groups(lb_ref), [ws_ref[g] for g in range(B_GROUPS)],
                               [bt_ref[:, g:g + 1] for g in range(B_GROUPS)])
            for g in range(B_GROUPS):
                o_ref[rows, g * LANES:(g + 1) * LANES] = outs[g].astype(BF16)

    vec = pl.BlockSpec((1, B_WIDTH), lambda i: (0, 0))
    return pl.pallas_call(
        body, name=name, grid=(t // tm,),
        in_specs=[pl.BlockSpec((tm, B_WIDTH), lambda i: (i, 4)), pl.BlockSpec((tm, B_WIDTH), lambda i: (i, 5)), vec, vec,
                  pl.BlockSpec((B_GROUPS, B_CHUNK, B_CHUNK), lambda i: (0, 0, 0)),
                  pl.BlockSpec((B_CHUNK, B_GROUPS), lambda i: (0, 0)), pl.BlockSpec(memory_space=pl.ANY)],
        out_specs=pl.BlockSpec((tm, B_WIDTH), lambda i: (i, 1)),
        out_shape=jax.ShapeDtypeStruct(mixed.shape, BF16),
        input_output_aliases={6: 0},
        compiler_params=_params(("parallel",)),
    )(proj, proj, ln_g, ln_b, ws, bias_t, mixed)


def _gmlp_bwd(proj, dcat, dproj, ln_g, ln_b, ws, bias_t, name):
    t = proj.shape[0]
    tm = GMLP_ROWS

    def body(u_ref, v_ref, do_ref, lg_ref, lb_ref, ws_ref, bt_ref, _, duv_ref, dlg_ref, dlb_ref, dws_ref, dbt_ref):
        @pl.when(pl.program_id(0) == 0)
        def _():
            dlg_ref[...] = jnp.zeros_like(dlg_ref)
            dlb_ref[...] = jnp.zeros_like(dlb_ref)
            dws_ref[...] = jnp.zeros_like(dws_ref)
            dbt_ref[...] = jnp.zeros_like(dbt_ref)

        for ch in range(tm // B_CHUNK):
            rows = slice(ch * B_CHUNK, (ch + 1) * B_CHUNK)
            _, vjp = jax.vjp(
                _gmlp_chunk, _lane_groups(u_ref, rows), _lane_groups(v_ref, rows), _lane_groups(lg_ref),
                _lane_groups(lb_ref), [ws_ref[g] for g in range(B_GROUPS)],
                [bt_ref[:, g:g + 1] for g in range(B_GROUPS)])
            du, dv, dlg, dlb, dw, dbt = vjp(_lane_groups(do_ref, rows))
            for g in range(B_GROUPS):
                lanes = slice(g * LANES, (g + 1) * LANES)
                duv_ref[rows, lanes] = du[g].astype(BF16)
                duv_ref[rows, B_WIDTH + g * LANES:B_WIDTH + (g + 1) * LANES] = dv[g].astype(BF16)
                dlg_ref[0:1, lanes] += dlg[g]
                dlb_ref[0:1, lanes] += dlb[g]
                dws_ref[g] += dw[g]
                dbt_ref[:, g:g + 1] += dbt[g]

    vec = pl.BlockSpec((1, B_WIDTH), lambda i: (0, 0))
    acc8 = pl.BlockSpec((SUBLANES, B_WIDTH), lambda i: (0, 0))
    ws_spec = pl.BlockSpec((B_GROUPS, B_CHUNK, B_CHUNK), lambda i: (0, 0, 0))
    bt_spec = pl.BlockSpec((B_CHUNK, B_GROUPS), lambda i: (0, 0))
    return pl.pallas_call(
        body, name=name, grid=(t // tm,),
        in_specs=[pl.BlockSpec((tm, B_WIDTH), lambda i: (i, 4)), pl.BlockSpec((tm, B_WIDTH), lambda i: (i, 5)),
                  pl.BlockSpec((tm, B_WIDTH), lambda i: (i, 1)), vec, vec, ws_spec, bt_spec,
                  pl.BlockSpec(memory_space=pl.ANY)],
        out_specs=[pl.BlockSpec((tm, 2 * B_WIDTH), lambda i: (i, 2)), acc8, acc8, ws_spec, bt_spec],
        out_shape=[jax.ShapeDtypeStruct(dproj.shape, BF16), jax.ShapeDtypeStruct((SUBLANES, B_WIDTH), F32),
                   jax.ShapeDtypeStruct((SUBLANES, B_WIDTH), F32),
                   jax.ShapeDtypeStruct((B_GROUPS, B_CHUNK, B_CHUNK), F32),
                   jax.ShapeDtypeStruct((B_CHUNK, B_GROUPS), F32)],
        input_output_aliases={7: 0},
        compiler_params=_params(("arbitrary",)),
    )(proj, proj, dcat, ln_g, ln_b, ws, bias_t, dproj)


QK_SCALE = 1.0 / math.sqrt(C_HEAD_DIM)
ATTN_UNROLL = 16
ATTN_PAIRS = 2
LANE_GROUPS = D_MODEL // LANES
ATTN_STEPS = LANE_GROUPS // ATTN_PAIRS
Q_BLOCKS = SEQ // C_BLOCK


def _attn_window(i, d):
    sub_blocks = Q_BLOCKS // d
    q0 = pl.multiple_of(i * C_BLOCK, C_BLOCK)
    k0 = pl.multiple_of(jnp.maximum(i - 1, 0) * C_BLOCK, C_BLOCK)
    key = k0 + lax.broadcasted_iota(jnp.int32, (C_BLOCK, 2 * C_BLOCK), 1)
    dist = (q0 + lax.broadcasted_iota(jnp.int32, (C_BLOCK, 2 * C_BLOCK), 0)) - key
    own_subsequence = (key >= q0) | (i % sub_blocks > 0)
    return pl.ds(q0, C_BLOCK), pl.ds(k0, 2 * C_BLOCK), (dist >= 0) & (dist <= C_BLOCK) & own_subsequence


def _head_masks():
    lane = lax.broadcasted_iota(jnp.int32, (C_BLOCK, LANES), 1)
    return [lane < C_HEAD_DIM, lane >= C_HEAD_DIM]


def _flat_spec(col_of):
    return pl.BlockSpec((1, SEQ, ATTN_PAIRS * LANES), lambda b, g: (b, 0, col_of(g)))


def _put_heads(tile, g, col0, col1):
    lane = lax.broadcasted_iota(jnp.int32, tile.shape, 1)
    return jnp.where(lane == 2 * g, col0, jnp.where(lane == 2 * g + 1, col1, tile))


def _get_head(tile, h):
    lane = lax.broadcasted_iota(jnp.int32, tile.shape, 1)
    return jnp.sum(jnp.where(lane == h, tile, 0.0), axis=1, keepdims=True)


PER_HEAD_SPEC = pl.BlockSpec((1, SEQ, LANES), lambda b, g: (b, 0, 0))


def _attn_branch_fwd(qkv, name):
    n_seq, d, l, _ = qkv.shape
    flat = qkv.reshape(n_seq, SEQ, ODD_IN)

    def body(q_ref, k_ref, v_ref, o_ref, m_ref, l_ref):
        heads = _head_masks()
        g = pl.program_id(1)

        @pl.when(g == 0)
        def _():
            m_ref[...] = jnp.zeros_like(m_ref)
            l_ref[...] = jnp.zeros_like(l_ref)

        def block(i, carry):
            rows, keys, mask = _attn_window(i, d)
            m_tile, l_tile = m_ref[0, rows, :], l_ref[0, rows, :]
            for pair in range(ATTN_PAIRS):
                lanes = slice(pair * LANES, (pair + 1) * LANES)
                q, k, v = q_ref[0, rows, lanes], k_ref[0, keys, lanes], v_ref[0, keys, lanes]
                res = []
                for hm in heads:
                    s = jnp.where(mask, _dot_nt(jnp.where(hm, q, 0), k), NEG)
                    m = jnp.max(s, axis=-1, keepdims=True)
                    p = jnp.exp(s - m)
                    res.append((_dot(p.astype(BF16), v), m, jnp.sum(p, axis=-1, keepdims=True)))
                o_ref[0, rows, lanes] = jnp.where(heads[0], res[0][0], res[1][0])
                m_tile = _put_heads(m_tile, g * ATTN_PAIRS + pair, res[0][1], res[1][1])
                l_tile = _put_heads(l_tile, g * ATTN_PAIRS + pair, res[0][2], res[1][2])
            m_ref[0, rows, :] = m_tile
            l_ref[0, rows, :] = l_tile
            return carry

        lax.fori_loop(0, Q_BLOCKS, block, 0, unroll=ATTN_UNROLL)

    o, m, l_sum = pl.pallas_call(
        body, name=name, grid=(n_seq, ATTN_STEPS),
        in_specs=[_flat_spec(lambda g: g), _flat_spec(lambda g: ATTN_STEPS + g),
                  _flat_spec(lambda g: 2 * ATTN_STEPS + g)],
        out_specs=[_flat_spec(lambda g: g), PER_HEAD_SPEC, PER_HEAD_SPEC],
        out_shape=[jax.ShapeDtypeStruct((n_seq, SEQ, D_MODEL), F32)] + [jax.ShapeDtypeStruct((n_seq, SEQ, LANES), F32)] * 2,
        compiler_params=_params(("parallel", "arbitrary")),
    )(flat, flat, flat)
    return [o.reshape(n_seq, d, l, D_MODEL), m.reshape(n_seq, d, l, LANES), l_sum.reshape(n_seq, d, l, LANES)]


def _attn_merge(branches, name):
    n_seq = branches[0][0].shape[0]
    t = n_seq * SEQ
    tm = MERGE_TILE

    def body(*refs):
        ins = refs[:9]
        o_ref, ob_ref, lse_ref = refs[9:12]
        nat = refs[12:]
        for b, d in enumerate(C_DILATIONS[1:]):
            for k in range(3):
                _load_dilated(ins[3 + 3 * b + k], d, nat[3 * b + k])
        ms = [ins[1][0, 0], nat[1][0], nat[4][0]]
        ls = [ins[2][0, 0], nat[2][0], nat[5][0]]
        m_all = jnp.maximum(jnp.maximum(ms[0], ms[1]), ms[2])
        ws = [jnp.exp(ms[b] - m_all) for b in range(3)]
        lane = lax.broadcasted_iota(jnp.int32, m_all.shape, 1)
        total = jnp.where(lane < C_HEADS, ws[0] * ls[0] + ws[1] * ls[1] + ws[2] * ls[2], 1.0)
        lse_ref[...] = m_all + jnp.log(total)
        first_head = lane < C_HEAD_DIM
        for p in range(LANE_GROUPS):
            lanes = slice(p * LANES, (p + 1) * LANES)
            spread = lambda c: jnp.where(first_head, c[:, 2 * p:2 * p + 1], c[:, 2 * p + 1:2 * p + 2])
            os_ = [ins[0][0, 0, :, lanes], nat[0][p], nat[3][p]]
            o = (spread(ws[0]) * os_[0] + spread(ws[1]) * os_[1] + spread(ws[2]) * os_[2]) / spread(total)
            o_ref[:, lanes] = o
            ob_ref[:, lanes] = o.astype(BF16)

    row = pl.BlockSpec((tm, D_MODEL), lambda i: (i, 0))
    flat = [a for br in branches for a in br]
    in_specs = []
    for wide, narrow in zip(_dilated_specs(tm, D_MODEL, lambda: 0), _dilated_specs(tm, LANES, lambda: 0)):
        in_specs += [wide, narrow, narrow]
    per_head = pltpu.VMEM((1, tm, LANES), F32)
    return pl.pallas_call(
        body, name=name, grid=(t // tm,), in_specs=in_specs,
        out_specs=[row, row, pl.BlockSpec((tm, LANES), lambda i: (i, 0))],
        out_shape=[jax.ShapeDtypeStruct((t, D_MODEL), F32), jax.ShapeDtypeStruct((t, D_MODEL), BF16),
                   jax.ShapeDtypeStruct((t, LANES), F32)],
        scratch_shapes=[pltpu.VMEM((LANE_GROUPS, tm, LANES), F32), per_head, per_head] * 2,
        compiler_params=_params(("parallel",)),
    )(*flat)


def _attn_branch_bwd(qkv, dout, lse, delta, name):
    n_seq, d, l, _ = qkv.shape
    flat = lambda a: a.reshape(n_seq, SEQ, a.shape[-1])

    def body(q_ref, k_ref, v_ref, do_ref, lse_nat_ref, dl_nat_ref, dq_ref, dk_ref, dv_ref, lse_ref, dl_ref,
             dkt_ref, dvt_ref):
        heads = _head_masks()
        g = pl.program_id(1)
        dkt_ref[...] = jnp.zeros_like(dkt_ref)
        dvt_ref[...] = jnp.zeros_like(dvt_ref)
        for nat_ref, dst_ref in ((lse_nat_ref, lse_ref), (dl_nat_ref, dl_ref)):
            for r in range(d):
                rows = pl.ds(r, l, stride=d) if d > 1 else slice(None)
                dst_ref[r * l:(r + 1) * l, :] = nat_ref.at[0][rows, :]

        def block(i, carry):
            rows, keys, mask = _attn_window(i, d)
            lse_b, dl_b = lse_ref[rows, :], dl_ref[rows, :]
            for pair in range(ATTN_PAIRS):
                lanes = slice(pair * LANES, (pair + 1) * LANES)
                q, do = q_ref[0, rows, lanes], do_ref[0, rows, lanes]
                k, v = k_ref[0, keys, lanes], v_ref[0, keys, lanes]
                dq, dk, dv = [], None, None
                for hh, hm in enumerate(heads):
                    head = 2 * (g * ATTN_PAIRS + pair) + hh
                    qh, doh = jnp.where(hm, q, 0), jnp.where(hm, do, 0)
                    s = jnp.where(mask, _dot_nt(qh, k), NEG)
                    p = jnp.exp(s - _get_head(lse_b, head))
                    ds = (p * (_dot_nt(doh, v) - _get_head(dl_b, head))).astype(BF16)
                    dq.append(_dot(ds, k) * QK_SCALE)
                    dk_h, dv_h = _dot_tn(qh, ds), _dot_tn(doh, p.astype(BF16))
                    dk = dk_h if dk is None else dk + dk_h
                    dv = dv_h if dv is None else dv + dv_h
                dq_ref[0, rows, lanes] = jnp.where(heads[0], dq[0], dq[1]).astype(BF16)
                dkt_ref[lanes, keys] += dk
                dvt_ref[lanes, keys] += dv
            return carry

        lax.fori_loop(0, Q_BLOCKS, block, 0, unroll=ATTN_UNROLL)
        for c in range(SEQ // ROW_TILE):
            rows = slice(c * ROW_TILE, (c + 1) * ROW_TILE)
            dk_ref[0, rows, :] = dkt_ref[:, rows].T.astype(BF16)
            dv_ref[0, rows, :] = dvt_ref[:, rows].T.astype(BF16)

    act = _flat_spec(lambda g: g)
    outs = pl.pallas_call(
        body, name=name, grid=(n_seq, ATTN_STEPS),
        in_specs=[_flat_spec(lambda g: g), _flat_spec(lambda g: ATTN_STEPS + g),
                  _flat_spec(lambda g: 2 * ATTN_STEPS + g), act, PER_HEAD_SPEC, PER_HEAD_SPEC],
        out_specs=[act] * 3,
        out_shape=[jax.ShapeDtypeStruct((n_seq, SEQ, D_MODEL), BF16)] * 3,
        scratch_shapes=[pltpu.VMEM((SEQ, LANES), F32)] * 2 + [pltpu.VMEM((ATTN_PAIRS * LANES, SEQ), F32)] * 2,
        compiler_params=_params(("parallel", "parallel")),
    )(flat(qkv), flat(qkv), flat(qkv), flat(dout), lse, delta)
    return [o.reshape(n_seq, d, l, D_MODEL) for o in outs]


def _attn_combine_bwd(grads, rope, name):
    n_seq = grads[0][0].shape[0]
    t = n_seq * SEQ
    tm = MERGE_TILE

    def body(*refs):
        c_ref, s_ref, o_ref, nat4_ref, nat16_ref = refs[9:]
        for sec in range(3):
            _load_dilated(refs[3 + sec], 4, nat4_ref)
            _load_dilated(refs[6 + sec], 16, nat16_ref)
            for p in range(LANE_GROUPS):
                blk = refs[sec][0, 0, :, p * LANES:(p + 1) * LANES] + nat4_ref[p] + nat16_ref[p]
                if sec < 2:
                    blk = blk * c_ref[...] - _swap_halves(blk) * s_ref[...]
                o_ref[:, sec * D_MODEL + p * LANES:sec * D_MODEL + (p + 1) * LANES] = blk.astype(BF16)

    tab = pl.BlockSpec((tm, LANES), lambda i: (i, 0))
    flat = [a for br in grads for a in br]
    in_specs = []
    for spec in _dilated_specs(tm, D_MODEL, lambda: 0):
        in_specs += [spec] * 3
    return pl.pallas_call(
        body, name=name, grid=(t // tm,), in_specs=in_specs + [tab, tab],
        out_specs=pl.BlockSpec((tm, ODD_IN), lambda i: (i, 0)),
        out_shape=jax.ShapeDtypeStruct((t, ODD_IN), BF16),
        scratch_shapes=[pltpu.VMEM((LANE_GROUPS, tm, LANES), F32)] * 2,
        compiler_params=_params(("parallel",)),
    )(*flat, *rope)


def _adamw(w, g, m, v):
    m = ADAM_B1 * m + (1.0 - ADAM_B1) * g
    v = ADAM_B2 * v + (1.0 - ADAM_B2) * jnp.square(g)
    m_hat = m / (1.0 - ADAM_B1 ** ADAM_STEP)
    v_hat = v / (1.0 - ADAM_B2 ** ADAM_STEP)
    delta = -ADAM_LR * (m_hat / (jnp.sqrt(v_hat) + ADAM_EPS) + ADAM_WD * w)
    return delta, m, v


def _adamw_sharded(parts, w, m, v, after, name):
    n_layers, rows, cols = w.shape
    tr = min(rows, 256)

    def body(*refs):
        p_refs = refs[:n_layers]
        w_ref, m_ref, v_ref, _, g_ref, d_ref, mo_ref, vo_ref = refs[n_layers:]
        layer = pl.program_id(0)
        g = None
        for l, p_ref in enumerate(p_refs):
            g_l = p_ref[0].astype(F32)
            for s in range(1, N_DEV):
                g_l = g_l + p_ref[s].astype(F32)
            g = g_l if g is None else jnp.where(layer == l, g_l, g)
        delta, mn, vn = _adamw(w_ref[0], g, m_ref[0], v_ref[0])
        g_ref[0] = g
        d_ref[0] = delta
        mo_ref[0] = mn
        vo_ref[0] = vn

    def part_spec(l):
        return pl.BlockSpec((N_DEV, tr, cols), lambda a, i: (0, jnp.where(a == l, i, 0), 0))

    row = pl.BlockSpec((1, tr, cols), lambda a, i: (a, i, 0))
    return pl.pallas_call(
        body, name=name, grid=(n_layers, rows // tr),
        in_specs=[part_spec(l) for l in range(n_layers)] + [row, row, row, pl.BlockSpec(memory_space=pl.ANY)],
        out_specs=[row] * 4, out_shape=[jax.ShapeDtypeStruct(w.shape, F32)] * 4,
        compiler_params=_params(("arbitrary", "arbitrary")),
    )(*parts, w, m, v, after)


def _small_update(gathered, where, weights, moments_m, moments_v, lb_index, name):
    n = len(weights)
    n_g = len(gathered)

    def body(*refs):
        g_refs = refs[:n_g]
        w_refs, m_refs, v_refs = refs[n_g:n_g + n], refs[n_g + n:n_g + 2 * n], refs[n_g + 2 * n:n_g + 3 * n]
        outs = refs[n_g + 3 * n:]

        def total(k):
            array, rows, lanes = where[k]
            ref = g_refs[array]
            index = (slice(None),) * (len(ref.shape) - 1) if rows is None else (rows, lanes)
            acc = ref[(0,) + index]
            for s in range(1, N_DEV):
                acc = acc + ref[(s,) + index]
            return acc

        loss_rows = total(n)
        outs[0][...] = jnp.sum(jnp.sum(loss_rows, axis=1, keepdims=True), axis=0, keepdims=True)
        for k in range(n):
            part = total(k)
            if k == lb_index:
                dlb = jnp.sum(part, axis=0, keepdims=True)
                tab = w_refs[k][...]
                e = jnp.exp(tab - jnp.max(tab, axis=0, keepdims=True))
                p = e / jnp.sum(e, axis=0, keepdims=True)
                first = lax.broadcasted_iota(jnp.int32, p.shape, 0) == 0
                grads = [(slice(None), p * (jnp.where(first, dlb, 0.0) - p[0:1, :] * dlb))]
            elif part.shape == w_refs[k].shape:
                grads = [(slice(None), part)]
            else:
                grads = [(slice(l, l + 1), jnp.sum(part[l * SUBLANES:(l + 1) * SUBLANES], axis=0, keepdims=True))
                         for l in range(w_refs[k].shape[0])]
            for rows, g in grads:
                delta, mn, vn = _adamw(w_refs[k][rows], g, m_refs[k][rows], v_refs[k][rows])
                outs[1 + 4 * k][rows] = g
                outs[2 + 4 * k][rows] = delta
                outs[3 + 4 * k][rows] = mn
                outs[4 + 4 * k][rows] = vn

    vmem = pl.BlockSpec(memory_space=pltpu.VMEM)
    out_shape = [jax.ShapeDtypeStruct((1, 1), F32)]
    for w in weights:
        out_shape += [jax.ShapeDtypeStruct(w.shape, F32)] * 4
    args = list(gathered) + list(weights) + list(moments_m) + list(moments_v)
    return pl.pallas_call(
        body, name=name, in_specs=[vmem] * len(args), out_specs=[vmem] * len(out_shape), out_shape=out_shape,
        compiler_params=pltpu.CompilerParams(vmem_limit_bytes=VMEM_LIMIT),
    )(*args)


def kernel(x, positions, norm_mix_pre, norm_mix_post, norm_ffn_pre, norm_ffn_post, w_in_even, lb_table, a_norm, b_ln_g, b_ln_b, b_ws, b_bias, w_out_even, w_in_odd, w_out_odd, w_ff1, w_ff2, loss_target, m_norm_mix_pre, m_norm_mix_post, m_norm_ffn_pre, m_norm_ffn_post, m_w_in_even, m_lb_table, m_a_norm, m_b_ln_g, m_b_ln_b, m_b_ws, m_b_bias, m_w_out_even, m_w_in_odd, m_w_out_odd, m_w_ff1, m_w_ff2, v_norm_mix_pre, v_norm_mix_post, v_norm_ffn_pre, v_norm_ffn_post, v_w_in_even, v_lb_table, v_a_norm, v_b_ln_g, v_b_ln_b, v_b_ws, v_b_bias, v_w_out_even, v_w_in_odd, v_w_out_odd, v_w_ff1, v_w_ff2):
    n_seq = x.shape[0]
    t = n_seq * SEQ
    x0 = x.reshape(t, D_MODEL)
    target = loss_target.reshape(t, D_MODEL)

    me = _my_slot().astype(jnp.int32).reshape(1)

    order = ["in_e", "out_e", "ff1_0", "ff2_0", "in_o", "out_o", "ff1_1", "ff2_1"]
    shards = dict(in_e=w_in_even[0], out_e=w_out_even[0], in_o=w_in_odd[0], out_o=w_out_odd[0],
                  ff1_0=w_ff1[0], ff1_1=w_ff1[1], ff2_0=w_ff2[0], ff2_1=w_ff2[1])
    by_columns = ("in_e", "in_o", "ff1_0", "ff1_1")

    def place(k, after):
        if k in by_columns:
            return _place_own_columns(shards[k], me, "place_" + k, after)
        return _place_own(shards[k], me, "place_" + k, False, after=after)

    gathers = {}
    send0, recv0, land0, _, token0 = _exchange_start([place(order[0], None)], [None], "gather_start_first")
    gathers[order[0]] = (land0[0], send0[0], recv0[0])
    sends, recvs, lands, _, g_token = _exchange_start([place(k, token0) for k in order[1:]],
                                                      [None] * (len(order) - 1), "gather_start")
    for k, land, send, recv in zip(order[1:], lands, sends, recvs):
        gathers[k] = (land, send, recv)

    def get_w(keys, after):
        lands_k, sends_k, recvs_k = zip(*[gathers[k] for k in keys])
        return _exchange_wait(list(lands_k), [None] * len(keys), list(sends_k), list(recvs_k), after,
                              "gather_wait_" + keys[0])

    sent = {}

    def put_g(group, blocks):
        keys = list(blocks)
        own = [_place_own(blocks[k], me, "own_" + k, True) for k in keys]
        send_sems, recv_sems, own, srcs, token = _exchange_start(own, [blocks[k] for k in keys], "scatter_start_" + group)
        sent[group] = (keys, own, srcs, send_sems, recv_sems)
        return token

    rope = _rope_tables(positions)
    bias_t = b_bias[0].T
    grads = _local_step(x0, target, rope, norm_mix_pre, norm_mix_post, norm_ffn_pre, norm_ffn_post, lb_table,
                        a_norm, b_ln_g, b_ln_b, b_ws[0], bias_t, get_w, put_g, g_token)
    (dx0, loss_part, dg_mix_pre, dg_mix_post, dg_ffn_pre, dg_ffn_post, d_lb, d_a_norm, d_ln_g, d_ln_b, d_ws,
     d_bias_t) = grads

    packed = jnp.concatenate([dg_mix_pre, dg_mix_post, dg_ffn_pre, dg_ffn_post,
                              jnp.concatenate([d_lb, d_a_norm], axis=1), jnp.concatenate([d_ln_g, d_ln_b], axis=1),
                              loss_part], axis=0)
    small_lands = [_place_own(a, me, "own_small%d" % k, False, F32) for k, a in enumerate((packed, d_ws, d_bias_t))]
    s_send, s_recv, small_lands, _, after = _exchange_start(small_lands, [None] * 3, "gather_small_start")

    big = dict(w_in_even=(["in_e"], w_in_even, m_w_in_even, v_w_in_even),
               w_out_even=(["out_e"], w_out_even, m_w_out_even, v_w_out_even),
               w_in_odd=(["in_o"], w_in_odd, m_w_in_odd, v_w_in_odd),
               w_out_odd=(["out_o"], w_out_odd, m_w_out_odd, v_w_out_odd),
               w_ff1=(["ff1_0", "ff1_1"], w_ff1, m_w_ff1, v_w_ff1), w_ff2=(["ff2_0", "ff2_1"], w_ff2, m_w_ff2, v_w_ff2))
    recv, big_out = {}, {}
    for groups, names in ((("ffn1", "ffn0"), ("w_ff1", "w_ff2")), (("mix1",), ("w_in_odd", "w_out_odd")),
                          (("mix0",), ("w_in_even", "w_out_even"))):
        for group in groups:
            keys, own, srcs, send_sems, recv_sems = sent[group]
            recv.update(zip(keys, _exchange_wait(own, srcs, send_sems, recv_sems, after, "scatter_wait_" + group)))
        for nm in names:
            keys, w, m, v = big[nm]
            big_out[nm] = _adamw_sharded([recv[k] for k in keys], w, m, v, after, "adamw_" + nm)
            after = big_out[nm][0]
    big_out = [big_out[nm] for nm in ("w_in_even", "w_out_even", "w_in_odd", "w_out_odd", "w_ff1", "w_ff2")]
    gathered = _exchange_wait(small_lands, [None] * 3, s_send, s_recv, after, "gather_small_wait")
    rows8 = lambda k: slice(SUBLANES * k, SUBLANES * (k + 1))
    left, right, every = slice(0, A_WIDTH), slice(A_WIDTH, 2 * A_WIDTH), slice(None)
    where = [(0, slice(0, 16), every), (0, slice(16, 32), every), (0, slice(32, 48), every), (0, slice(48, 64), every),
             (0, rows8(8), left), (0, rows8(8), right), (0, rows8(9), left), (0, rows8(9), right),
             (1, None, None), (2, None, None), (0, rows8(10), every)]
    small_w = [norm_mix_pre, norm_mix_post, norm_ffn_pre, norm_ffn_post, lb_table, a_norm, b_ln_g, b_ln_b,
               b_ws[0], bias_t]
    small_m = [m_norm_mix_pre, m_norm_mix_post, m_norm_ffn_pre, m_norm_ffn_post, m_lb_table, m_a_norm, m_b_ln_g,
               m_b_ln_b, m_b_ws[0], m_b_bias[0].T]
    small_v = [v_norm_mix_pre, v_norm_mix_post, v_norm_ffn_pre, v_norm_ffn_post, v_lb_table, v_a_norm, v_b_ln_g,
               v_b_ln_b, v_b_ws[0], v_b_bias[0].T]
    small_out = _small_update(gathered, where, small_w, small_m, small_v, 4, "small_update")
    loss = small_out[0].reshape(())
    small = [small_out[1 + 4 * k:5 + 4 * k] for k in range(len(small_w))]
    small[8] = [a[None] for a in small[8]]
    small[9] = [a.T[None] for a in small[9]]

    per_weight = small[0:4] + [big_out[0]] + small[4:10] + big_out[1:6]
    grad_x = dx0.reshape(x.shape)
    out = [loss, grad_x]
    for kind in range(4):
        out += [p[kind] for p in per_weight]
    return tuple(out)


def _local_step(x0, target, rope, norm_mix_pre, norm_mix_post, norm_ffn_pre, norm_ffn_post, lb_table, a_norm,
                b_ln_g, b_ln_b, ws, bias_t, get_w, put_g, token):
    def gain(a, l, tok):
        return a[l:l + 1] if tok is None else a[l:l + 1] + tok[0:1, 0:1]

    full = lambda a: a.reshape(-1, D_MODEL)
    owners = lambda a: a.reshape((N_DEV, -1) + a.shape[1:])

    (g_in_e,) = get_w(["in_e"], token)
    proj, h_mix0 = _norm_inproj(x0, gain(norm_mix_pre, 0, token), g_in_e, "inproj_even")
    mixed, pre_a, states = _hgrn2_fwd(proj, lb_table, a_norm, "hgrn2_fwd")
    mixed = _gmlp_fwd(proj, mixed, b_ln_g, b_ln_b, ws, bias_t, "gmlp_fwd")
    w_out_e = full(get_w(["out_e"], mixed)[0])
    x1, mix0 = _outproj([mixed], w_out_e, x0, gain(norm_mix_post, 0, None), "outproj_even")
    w1_0, w2_0 = get_w(["ff1_0", "ff2_0"], x1)
    w2_0 = full(w2_0)
    x2, y0, h_ffn0, r0 = _ffn_fwd(x1, gain(norm_ffn_pre, 0, None), w1_0, w2_0, gain(norm_ffn_post, 0, None), "ffn_fwd_0")
    (g_in_o,) = get_w(["in_o"], x2)
    *qkv, h_mix1 = _norm_inproj_rope(x2, gain(norm_mix_pre, 1, None), g_in_o, rope, "inproj_odd")
    branches = [_attn_branch_fwd(a, "attn_fwd_d%d" % d) for a, d in zip(qkv, C_DILATIONS)]
    attn, attn_b, lse = _attn_merge(branches, "attn_merge")
    w_out_o = full(get_w(["out_o"], attn_b)[0])
    x3, mix1 = _outproj([attn_b], w_out_o, x2, gain(norm_mix_post, 1, None), "outproj_odd")
    w1_1, w2_1 = get_w(["ff1_1", "ff2_1"], x3)
    w2_1 = full(w2_1)
    dx4, y1, h_ffn1, r1, loss_part = _ffn_fwd(x3, gain(norm_ffn_pre, 1, None), w1_1, w2_1, gain(norm_ffn_post, 1, None),
                                              "ffn_fwd_1", target)

    dx3, dy1, da1, dg_ffn_pre1, dg_ffn_post1 = _ffn_bwd(
        dx4, x3, y1, r1, gain(norm_ffn_pre, 1, None), w1_1, w2_1, gain(norm_ffn_post, 1, None), "ffn_bwd_1")
    gw_ff1_1 = _grad_w(h_ffn1, da1, True, "grad_w_ff1_1")
    gw_ff2_1 = _grad_w(r1, dy1, False, "grad_w_ff2_1")
    tok = put_g("ffn1", dict(ff1_1=gw_ff1_1, ff2_1=owners(gw_ff2_1)))
    *dattn, delta, dz1, dg_mix_post1 = _outproj_bwd_attn(dx3, mix1, gain(norm_mix_post, 1, tok), w_out_o, attn,
                                                  "outproj_bwd_odd")
    gw_out_o = _grad_w(attn_b, dz1, False, "grad_w_out_odd")
    per_seq = lambda a: a.reshape(-1, SEQ, LANES)
    grads_c = [_attn_branch_bwd(qkv[b], dattn[b], per_seq(lse), per_seq(delta), "attn_bwd_d%d" % d)
               for b, d in enumerate(C_DILATIONS)]
    dqkv = _attn_combine_bwd(grads_c, rope, "attn_combine_bwd")
    gw_in_o = _grad_w(h_mix1, dqkv, True, "grad_w_in_odd")
    tok = put_g("mix1", dict(out_o=owners(gw_out_o), in_o=gw_in_o))
    dx2, dg_mix_pre1 = _inproj_bwd(dqkv, g_in_o, dx3, x2, gain(norm_mix_pre, 1, tok), "inproj_bwd_odd")

    dx1, dy0, da0, dg_ffn_pre0, dg_ffn_post0 = _ffn_bwd(
        dx2, x1, y0, r0, gain(norm_ffn_pre, 0, None), w1_0, w2_0, gain(norm_ffn_post, 0, None), "ffn_bwd_0")
    gw_ff1_0 = _grad_w(h_ffn0, da0, True, "grad_w_ff1_0")
    gw_ff2_0 = _grad_w(r0, dy0, False, "grad_w_ff2_0")
    tok = put_g("ffn0", dict(ff1_0=gw_ff1_0, ff2_0=owners(gw_ff2_0)))
    dcat, dz0, dg_mix_post0 = _outproj_bwd(dx1, mix0, gain(norm_mix_post, 0, tok), w_out_e, "outproj_bwd_even")
    gw_out_e = _grad_w(mixed, dz0, False, "grad_w_out_even")
    dproj, d_lb, d_a_norm = _hgrn2_bwd(proj, dcat, pre_a, states, lb_table, a_norm, "hgrn2_bwd")
    dproj, d_ln_g, d_ln_b, d_ws, d_bias_t = _gmlp_bwd(proj, dcat, dproj, b_ln_g, b_ln_b, ws, bias_t, "gmlp_bwd")
    gw_in_e = _grad_w(h_mix0, dproj, True, "grad_w_in_even")
    tok = put_g("mix0", dict(out_e=owners(gw_out_e), in_e=gw_in_e))
    dx0, dg_mix_pre0 = _inproj_bwd(dproj, g_in_e, dx1, x0, gain(norm_mix_pre, 0, tok), "inproj_bwd_even")

    layers = lambda a, b: jnp.concatenate([a, b], axis=0)
    return (dx0, loss_part, layers(dg_mix_pre0, dg_mix_pre1), layers(dg_mix_post0, dg_mix_post1),
            layers(dg_ffn_pre0, dg_ffn_pre1), layers(dg_ffn_post0, dg_ffn_post1),
            d_lb, d_a_norm, d_ln_g, d_ln_b, d_ws, d_bias_t)
```

```python
import math

import jax
import jax.numpy as jnp
from jax import lax
from jax.experimental import pallas as pl
from jax.experimental.pallas import tpu as pltpu

F32 = jnp.float32
BF16 = jnp.bfloat16
MESH = pl.DeviceIdType.MESH

N_DEV = 8
D_MODEL = 1024
SEQ = 2048
EPS = 1e-6
A_WIDTH = 512
A_HEADS = 4
HEAD_A = 128
B_WIDTH = 512
B_GROUPS = 4
B_CHUNK = 128
C_HEADS = 16
C_HEAD_DIM = 64
C_ROT_HALF = 8
ROPE_THETA = 500000.0
C_DILATIONS = (1, 4, 16)
C_BLOCK = 128
D_FF = 4096
EVEN_IN = 3072
ODD_IN = 3072

ADAM_LR = 0.001
ADAM_B1 = 0.9
ADAM_B2 = 0.999
ADAM_EPS = 1e-08
ADAM_WD = 0.01
ADAM_STEP = 10

LANES = 128
SUBLANES = 8
ROW_TILE = 512
PROJ_TILE = 1024
MERGE_TILE = 256
SUB_CHUNK = 16
HGRN_BLOCK = 256
NEG = -1e30
VMEM_LIMIT = 56 * 1024 * 1024


def _params(sem):
    return pltpu.CompilerParams(dimension_semantics=sem, vmem_limit_bytes=VMEM_LIMIT)


def _dot(a, b):
    return jnp.dot(a, b, preferred_element_type=F32)


def _dot_nt(a, b):
    return lax.dot_general(a, b, (((1,), (1,)), ((), ())), preferred_element_type=F32)


def _dot_tn(a, b):
    return lax.dot_general(a, b, (((0,), (0,)), ((), ())), preferred_element_type=F32)


def _rms(x, g):
    r = lax.rsqrt(jnp.mean(x * x, axis=-1, keepdims=True) + EPS)
    return x * r * g


def _rms_bwd(x, g, dy):
    r = lax.rsqrt(jnp.mean(x * x, axis=-1, keepdims=True) + EPS)
    dyg = dy * g
    dx = r * dyg - x * (r * r * r) * jnp.mean(x * dyg, axis=-1, keepdims=True)
    return dx, dy * x * r


def _split3(x):
    hi = x.astype(BF16)
    rest = x - hi.astype(F32)
    mid = rest.astype(BF16)
    return hi, mid, (rest - mid.astype(F32)).astype(BF16)


def _mask_dot(mask, x):
    m = mask.astype(BF16)
    hi, mid, lo = _split3(x)
    return _dot(m, hi) + (_dot(m, mid) + _dot(m, lo))


def _dot_mask(x, mask):
    m = mask.astype(BF16)
    hi, mid, lo = _split3(x)
    return _dot(hi, m) + (_dot(mid, m) + _dot(lo, m))


def _rows8(v):
    return v.reshape(v.shape[0] // SUBLANES, SUBLANES, v.shape[1]).sum(axis=0)


def _sigmoid(x):
    return 1.0 / (1.0 + jnp.exp(-x))


def _gelu(x):
    return 0.5 * x * (1.0 + jnp.tanh(math.sqrt(2.0 / math.pi) * (x + 0.044715 * (x * x * x))))


def _acc_rows8(ref, val, first):
    @pl.when(first)
    def _():
        ref[...] = val

    @pl.when(jnp.logical_not(first))
    def _():
        ref[...] += val


def _my_slot():
    return 4 * lax.axis_index("x") + 2 * lax.axis_index("y") + lax.axis_index("c")


def _peer(r):
    x, y, c = lax.axis_index("x"), lax.axis_index("y"), lax.axis_index("c")
    px = 1 - x if (r >> 2) & 1 else x
    py = 1 - y if (r >> 1) & 1 else y
    pc = 1 - c if r & 1 else c
    return (px, py, pc), 4 * px + 2 * py + pc


HBM_SPEC = pl.BlockSpec(memory_space=pltpu.HBM)
SEM_SPEC = pl.BlockSpec(memory_space=pltpu.SEMAPHORE)
SPLIT_EFFECT = pltpu.SideEffectType.DATAFLOW_SIDE_EFFECTING


def _split_copies(land_ref, src_ref, send_sem, recv_sem):
    me = _my_slot()
    copies = []
    for r in range(1, N_DEV):
        peer, slot = _peer(r)
        src = _slot(land_ref, me) if src_ref is None else _slot(src_ref, slot)
        copies.append(pltpu.make_async_remote_copy(
            src_ref=src, dst_ref=_slot(land_ref, me), send_sem=send_sem, recv_sem=recv_sem,
            device_id=peer, device_id_type=MESH))
    return copies


def _slot(ref, s):
    if len(ref.shape) == 2:
        c = ref.shape[1] // N_DEV
        return ref.at[:, pl.ds(pl.multiple_of(s * c, LANES), c)]
    return ref.at[s]


def _exchange_start(lands, sources, name):
    n = len(lands)
    given = [s for s in sources if s is not None]
    arrays = list(lands) + given

    def body(*refs):
        land_refs, src_refs = refs[:n], list(refs[n:n + len(given)])
        sems = refs[len(arrays):len(arrays) + 2 * n]
        token = refs[-1]
        for k in range(n):
            src_ref = None if sources[k] is None else src_refs.pop(0)
            for copy in _split_copies(land_refs[k], src_ref, sems[k], sems[n + k]):
                copy.start()
        token[...] = jnp.zeros_like(token)

    outs = pl.pallas_call(
        body, name=name,
        out_shape=(pltpu.SemaphoreType.DMA(()),) * (2 * n) + tuple(pltpu.HBM(a.shape, a.dtype) for a in arrays)
        + (jax.ShapeDtypeStruct((SUBLANES, LANES), F32),),
        in_specs=[HBM_SPEC] * len(arrays),
        out_specs=(SEM_SPEC,) * (2 * n) + (HBM_SPEC,) * len(arrays) + (pl.BlockSpec(memory_space=pltpu.VMEM),),
        input_output_aliases={i: 2 * n + i for i in range(len(arrays))},
        compiler_params=pltpu.CompilerParams(has_side_effects=SPLIT_EFFECT),
    )(*[pltpu.with_memory_space_constraint(a, pltpu.HBM) for a in arrays])
    return list(outs[:n]), list(outs[n:2 * n]), list(outs[2 * n:3 * n]), list(outs[3 * n:-1]), outs[-1]


def _exchange_wait(lands, sources, send_sems, recv_sems, after, name):
    n = len(lands)
    given = [s for s in sources if s is not None]
    arrays = list(lands) + given

    def body(*refs):
        land_refs, src_refs = refs[:n], list(refs[n:n + len(given)])
        sems = refs[len(arrays):len(arrays) + 2 * n]
        for i in range(n):
            src_ref = None if sources[i] is None else src_refs.pop(0)
            copies = _split_copies(land_refs[i], src_ref, sems[i], sems[n + i])
            for copy in copies:
                copy.wait_recv()
            for copy in copies:
                copy.wait_send()

    outs = pl.pallas_call(
        body, name=name, out_shape=tuple(pltpu.HBM(a.shape, a.dtype) for a in arrays),
        in_specs=[HBM_SPEC] * len(arrays) + [SEM_SPEC] * (2 * n) + [pl.BlockSpec(memory_space=pl.ANY)],
        out_specs=(HBM_SPEC,) * len(arrays),
        input_output_aliases={i: i for i in range(len(arrays))},
        compiler_params=pltpu.CompilerParams(has_side_effects=SPLIT_EFFECT),
    )(*arrays, *send_sems, *recv_sems, after)
    return list(outs[:n])


def _place_own(a, me, name, own_block, dtype=BF16, after=None):
    shape = a.shape[1:] if own_block else a.shape
    cols = shape[-1]
    a3 = a.reshape((N_DEV if own_block else 1, -1, cols))
    rows = a3.shape[1]
    tr = min(rows, 512)

    def body(me_ref, a_ref, _, o_ref):
        o_ref[...] = a_ref[...].astype(dtype)

    grid_spec = pltpu.PrefetchScalarGridSpec(
        num_scalar_prefetch=1, grid=(rows // tr,),
        in_specs=[pl.BlockSpec((1, tr, cols), lambda i, me_ref: (me_ref[0] if own_block else 0, i, 0)),
                  pl.BlockSpec(memory_space=pl.ANY)],
        out_specs=pl.BlockSpec((1, tr, cols), lambda i, me_ref: (me_ref[0], i, 0)))
    out = pl.pallas_call(
        body, name=name, grid_spec=grid_spec, out_shape=jax.ShapeDtypeStruct((N_DEV, rows, cols), dtype),
        compiler_params=_params(("arbitrary",)),
    )(me, a3, a3 if after is None else after)
    return out.reshape((N_DEV,) + shape)


def _place_own_columns(a, me, name, after=None):
    rows, cols = a.shape
    tr = min(rows, 512)

    def body(me_ref, a_ref, _, o_ref):
        o_ref[...] = a_ref[...].astype(BF16)

    grid_spec = pltpu.PrefetchScalarGridSpec(
        num_scalar_prefetch=1, grid=(rows // tr,),
        in_specs=[pl.BlockSpec((tr, cols), lambda i, me_ref: (i, 0)), pl.BlockSpec(memory_space=pl.ANY)],
        out_specs=pl.BlockSpec((tr, cols), lambda i, me_ref: (i, me_ref[0])))
    return pl.pallas_call(
        body, name=name, grid_spec=grid_spec, out_shape=jax.ShapeDtypeStruct((rows, N_DEV * cols), BF16),
        compiler_params=_params(("arbitrary",)),
    )(me, a, a if after is None else after)


def _rope_tables(positions):
    in_head = jnp.arange(LANES) % C_HEAD_DIM
    inv = ROPE_THETA ** (-(in_head % C_ROT_HALF).astype(F32) / C_ROT_HALF)
    ang = positions.reshape(-1)[:, None].astype(F32) * inv
    rotated = in_head < 2 * C_ROT_HALF
    sin = jnp.sin(ang)
    return (jnp.where(rotated, jnp.cos(ang), 1.0),
            jnp.where(in_head < C_ROT_HALF, -sin, jnp.where(rotated, sin, 0.0)))


def _swap_halves(x):
    lane = lax.broadcasted_iota(jnp.int32, x.shape, 1) % C_HEAD_DIM
    return jnp.where(lane < C_ROT_HALF, pltpu.roll(x, LANES - C_ROT_HALF, 1), pltpu.roll(x, C_ROT_HALF, 1))


def _norm_inproj(x, g, w, name):
    t = x.shape[0]
    n = w.shape[1]
    tm, tn = ROW_TILE, n

    def body(x_ref, g_ref, w_ref, o_ref, h_ref):
        @pl.when(pl.program_id(1) == 0)
        def _():
            h_ref[...] = _rms(x_ref[...], g_ref[...]).astype(BF16)

        o_ref[...] = _dot(h_ref[...], w_ref[...])

    return pl.pallas_call(
        body, name=name, grid=(t // tm, n // tn),
        in_specs=[pl.BlockSpec((tm, D_MODEL), lambda i, j: (i, 0)), pl.BlockSpec((1, D_MODEL), lambda i, j: (0, 0)),
                  pl.BlockSpec((D_MODEL, tn), lambda i, j: (0, j))],
        out_specs=[pl.BlockSpec((tm, tn), lambda i, j: (i, j)), pl.BlockSpec((tm, D_MODEL), lambda i, j: (i, 0))],
        out_shape=[jax.ShapeDtypeStruct((t, n), F32), jax.ShapeDtypeStruct((t, D_MODEL), BF16)],
        compiler_params=_params(("parallel", "arbitrary")),
    )(x, g, w)


def _dilated_specs(tm, width, col_of):
    per_seq = SEQ // tm
    specs = []
    for d in C_DILATIONS:
        specs.append(pl.BlockSpec(
            (1, d, tm // d, width), lambda i, *rest: (i // per_seq, 0, i % per_seq, col_of(*rest))))
    return specs


def _dilated_shapes(n_seq, cols, dtype):
    return [jax.ShapeDtypeStruct((n_seq, d, SEQ // d, cols), dtype) for d in C_DILATIONS]


def _store_dilated(src_ref, out_refs, dtype):
    groups, tm, _ = src_ref.shape
    for d, o_ref in zip(C_DILATIONS, out_refs):
        for r in range(d):
            rows = pl.ds(r, tm // d, stride=d) if d > 1 else slice(None)
            for p in range(groups):
                o_ref[0, r, :, p * LANES:(p + 1) * LANES] = src_ref.at[p][rows, :].astype(dtype)


def _load_dilated(in_ref, d, dst_ref):
    groups, tm, _ = dst_ref.shape
    for r in range(d):
        rows = pl.ds(r, tm // d, stride=d)
        for p in range(groups):
            dst_ref.at[p][rows, :] = in_ref[0, r, :, p * LANES:(p + 1) * LANES].astype(F32)


def _norm_inproj_rope(x, g, w, rope, name):
    t = x.shape[0]
    n = w.shape[1]
    tm, nb = ROW_TILE, n

    def body(x_ref, g_ref, w_ref, c_ref, s_ref, o1_ref, o4_ref, o16_ref, h_ref, tile_ref):
        j = pl.program_id(1)

        @pl.when(j == 0)
        def _():
            h_ref[...] = _rms(x_ref[...], g_ref[...]).astype(BF16)

        acc = _dot(h_ref[...], w_ref[...])
        for p in range(nb // LANES):
            blk = acc[:, p * LANES:(p + 1) * LANES]
            roped = blk * c_ref[...] + _swap_halves(blk) * s_ref[...]
            piece = j * (nb // LANES) + p
            is_qk = piece < 2 * (D_MODEL // LANES)
            tile_ref[p] = jnp.where(is_qk, roped, blk) * jnp.where(piece < D_MODEL // LANES, QK_SCALE, 1.0)
        _store_dilated(tile_ref, (o1_ref, o4_ref, o16_ref), BF16)

    return pl.pallas_call(
        body, name=name, grid=(t // tm, n // nb),
        in_specs=[pl.BlockSpec((tm, D_MODEL), lambda i, j: (i, 0)), pl.BlockSpec((1, D_MODEL), lambda i, j: (0, 0)),
                  pl.BlockSpec((D_MODEL, nb), lambda i, j: (0, j)),
                  pl.BlockSpec((tm, LANES), lambda i, j: (i, 0)), pl.BlockSpec((tm, LANES), lambda i, j: (i, 0))],
        out_specs=_dilated_specs(tm, nb, lambda j: j) + [pl.BlockSpec((tm, D_MODEL), lambda i, j: (i, 0))],
        out_shape=_dilated_shapes(t // SEQ, n, BF16) + [jax.ShapeDtypeStruct((t, D_MODEL), BF16)],
        scratch_shapes=[pltpu.VMEM((nb // LANES, tm, LANES), F32)],
        compiler_params=_params(("parallel", "arbitrary")),
    )(x, g, w, *rope)


def _outproj(parts, w, x, g, name):
    t = x.shape[0]
    tm = PROJ_TILE
    n = len(parts)
    widths = [p.shape[1] for p in parts]

    def body(*refs):
        p_refs = refs[:n]
        w_ref, x_ref, g_ref, xo_ref, mix_ref = refs[n:]
        mix = None
        off = 0
        for p_ref, wd in zip(p_refs, widths):
            term = _dot(p_ref[...].astype(BF16), w_ref[off:off + wd, :])
            mix = term if mix is None else mix + term
            off += wd
        mix_ref[...] = mix
        xo_ref[...] = x_ref[...] + _rms(mix, g_ref[...])

    row = lambda i: (i, 0)
    return pl.pallas_call(
        body, name=name, grid=(t // tm,),
        in_specs=[pl.BlockSpec((tm, wd), row) for wd in widths] + [
            pl.BlockSpec((sum(widths), D_MODEL), lambda i: (0, 0)),
            pl.BlockSpec((tm, D_MODEL), row), pl.BlockSpec((1, D_MODEL), lambda i: (0, 0))],
        out_specs=[pl.BlockSpec((tm, D_MODEL), row)] * 2,
        out_shape=[jax.ShapeDtypeStruct((t, D_MODEL), F32)] * 2,
        compiler_params=_params(("parallel",)),
    )(*parts, w, x, g)


def _outproj_bwd(dx, mix, g, w, name):
    t = dx.shape[0]
    tm = PROJ_TILE
    k = w.shape[0]

    def body(dx_ref, mix_ref, g_ref, w_ref, dcat_ref, dz_ref, dg_ref):
        dz, dgr = _rms_bwd(mix_ref[...], g_ref[...], dx_ref[...])
        dzb = dz.astype(BF16)
        dz_ref[...] = dzb
        dcat_ref[...] = _dot_nt(dzb, w_ref[...])
        _acc_rows8(dg_ref, _rows8(dgr), pl.program_id(0) == 0)

    row = lambda i: (i, 0)
    return pl.pallas_call(
        body, name=name, grid=(t // tm,),
        in_specs=[pl.BlockSpec((tm, D_MODEL), row), pl.BlockSpec((tm, D_MODEL), row),
                  pl.BlockSpec((1, D_MODEL), lambda i: (0, 0)), pl.BlockSpec((k, D_MODEL), lambda i: (0, 0))],
        out_specs=[pl.BlockSpec((tm, k), row), pl.BlockSpec((tm, D_MODEL), row),
                   pl.BlockSpec((SUBLANES, D_MODEL), lambda i: (0, 0))],
        out_shape=[jax.ShapeDtypeStruct((t, k), F32), jax.ShapeDtypeStruct((t, D_MODEL), BF16),
                   jax.ShapeDtypeStruct((SUBLANES, D_MODEL), F32)],
        compiler_params=_params(("arbitrary",)),
    )(dx, mix, g, w)


def _outproj_bwd_attn(dx, mix, g, w, out, name):
    t = dx.shape[0]
    tm = MERGE_TILE

    def body(dx_ref, mix_ref, g_ref, w_ref, out_ref, do1, do4, do16, dl_ref, dz_ref, dg_ref, tile_ref):
        dz, dgr = _rms_bwd(mix_ref[...], g_ref[...], dx_ref[...])
        dzb = dz.astype(BF16)
        dz_ref[...] = dzb
        _acc_rows8(dg_ref, _rows8(dgr), pl.program_id(0) == 0)
        dout = _dot_nt(dzb, w_ref[...])
        for p in range(LANE_GROUPS):
            tile_ref[p] = dout[:, p * LANES:(p + 1) * LANES]
        _store_dilated(tile_ref, (do1, do4, do16), BF16)
        column = lax.broadcasted_iota(jnp.int32, (D_MODEL, LANES), 0) // C_HEAD_DIM
        head = lax.broadcasted_iota(jnp.int32, (D_MODEL, LANES), 1)
        dl_ref[...] = _dot_mask(dout * out_ref[...], column == head)

    row = lambda i: (i, 0)
    n_seq = t // SEQ
    return pl.pallas_call(
        body, name=name, grid=(t // tm,),
        in_specs=[pl.BlockSpec((tm, D_MODEL), row), pl.BlockSpec((tm, D_MODEL), row),
                  pl.BlockSpec((1, D_MODEL), lambda i: (0, 0)), pl.BlockSpec((D_MODEL, D_MODEL), lambda i: (0, 0)),
                  pl.BlockSpec((tm, D_MODEL), row)],
        out_specs=_dilated_specs(tm, D_MODEL, lambda: 0) + [
            pl.BlockSpec((tm, LANES), row), pl.BlockSpec((tm, D_MODEL), row),
            pl.BlockSpec((SUBLANES, D_MODEL), lambda i: (0, 0))],
        out_shape=_dilated_shapes(n_seq, D_MODEL, BF16) + [
            jax.ShapeDtypeStruct((t, LANES), F32), jax.ShapeDtypeStruct((t, D_MODEL), BF16),
            jax.ShapeDtypeStruct((SUBLANES, D_MODEL), F32)],
        scratch_shapes=[pltpu.VMEM((LANE_GROUPS, tm, LANES), F32)],
        compiler_params=_params(("arbitrary",)),
    )(dx, mix, g, w, out)


def _inproj_bwd(dproj, w, dx, x, g, name):
    t = x.shape[0]
    n = w.shape[1]
    tm = ROW_TILE
    steps = t // tm
    ring = 3

    def body(dp_hbm, w_ref, dx_ref, x_ref, g_ref, o_ref, dg_ref, ring_ref, sem):
        s = pl.program_id(0)

        def fetch(step, slot):
            rows = pl.ds(step * tm if isinstance(step, int) else pl.multiple_of(step * tm, tm), tm)
            return pltpu.make_async_copy(dp_hbm.at[rows, :], ring_ref.at[slot], sem.at[slot])

        @pl.when(s == 0)
        def _():
            for k in range(ring - 1):
                fetch(k, k).start()

        @pl.when(s + ring - 1 < steps)
        def _():
            fetch(s + ring - 1, (s + ring - 1) % ring).start()

        fetch(s, s % ring).wait()
        dxn, dgr = _rms_bwd(x_ref[...], g_ref[...], _dot_nt(ring_ref[s % ring], w_ref[...]))
        o_ref[...] = dx_ref[...] + dxn
        _acc_rows8(dg_ref, _rows8(dgr), s == 0)

    row = lambda i: (i, 0)
    return pl.pallas_call(
        body, name=name, grid=(steps,),
        in_specs=[pl.BlockSpec(memory_space=pl.ANY), pl.BlockSpec((D_MODEL, n), lambda i: (0, 0)),
                  pl.BlockSpec((tm, D_MODEL), row), pl.BlockSpec((tm, D_MODEL), row),
                  pl.BlockSpec((1, D_MODEL), lambda i: (0, 0))],
        out_specs=[pl.BlockSpec((tm, D_MODEL), row), pl.BlockSpec((SUBLANES, D_MODEL), lambda i: (0, 0))],
        out_shape=[jax.ShapeDtypeStruct((t, D_MODEL), F32), jax.ShapeDtypeStruct((SUBLANES, D_MODEL), F32)],
        scratch_shapes=[pltpu.VMEM((ring, tm, n), BF16), pltpu.SemaphoreType.DMA((ring,))],
        compiler_params=_params(("arbitrary",)),
    )(dproj, w, dx, x, g)


def _grad_w(a, b, col_blocks, name):
    a_planes, b_planes = a.ndim == 3, b.ndim == 3
    t = a.shape[-2]
    k = a.shape[0] * a.shape[2] if a_planes else a.shape[1]
    n = b.shape[0] * b.shape[2] if b_planes else b.shape[1]
    tk = a.shape[2] if a_planes else min(k, 1024)
    per_owner = n // N_DEV
    tn = 2 * per_owner if col_blocks else min(n, 1024)
    assert not b_planes or b.shape[2] == tn

    def body(a_ref, b_ref, o_ref, at_ref):
        @pl.when(pl.program_id(1) == 0)
        def _():
            for c in range(t // ROW_TILE):
                rows = slice(c * ROW_TILE, (c + 1) * ROW_TILE)
                at_ref[:, rows] = (a_ref[0, rows, :] if a_planes else a_ref[rows, :]).T

        res = _dot(at_ref[...], b_ref[0] if b_planes else b_ref[...]).astype(BF16)
        if col_blocks:
            o_ref[0] = res[:, :per_owner]
            o_ref[1] = res[:, per_owner:]
        else:
            o_ref[...] = res

    if col_blocks:
        out_spec = pl.BlockSpec((2, tk, per_owner), lambda i, j: (j, i, 0))
        out_shape = jax.ShapeDtypeStruct((N_DEV, k, per_owner), BF16)
    else:
        out_spec = pl.BlockSpec((tk, tn), lambda i, j: (i, j))
        out_shape = jax.ShapeDtypeStruct((k, n), BF16)
    a_spec = pl.BlockSpec((1, t, tk), lambda i, j: (i, 0, 0)) if a_planes else pl.BlockSpec((t, tk), lambda i, j: (0, i))
    b_spec = pl.BlockSpec((1, t, tn), lambda i, j: (j, 0, 0)) if b_planes else pl.BlockSpec((t, tn), lambda i, j: (0, j))
    return pl.pallas_call(
        body, name=name, grid=(k // tk, n // tn),
        in_specs=[a_spec, b_spec], out_specs=out_spec, out_shape=out_shape,
        scratch_shapes=[pltpu.VMEM((tk, t), BF16)],
        compiler_params=_params(("parallel", "arbitrary")),
    )(a, b)


FF_STEP = 1024
FF_STEPS = D_FF // FF_STEP


def _ffn_fwd(x, g_pre, w1, w2, g_post, name, target=None):
    t = x.shape[0]
    tm = PROJ_TILE

    def body(*refs):
        if target is None:
            x_ref, gp_ref, w1_ref, w2_ref, gq_ref, xo_ref, y_ref, h_ref, r_ref = refs
        else:
            x_ref, gp_ref, w1_ref, w2_ref, gq_ref, t_ref, xo_ref, y_ref, h_ref, r_ref, l_ref = refs
        i, j = pl.program_id(0), pl.program_id(1)

        @pl.when(j == 0)
        def _():
            h_ref[...] = _rms(x_ref[...], gp_ref[...]).astype(BF16)

        a = _dot(h_ref[...], w1_ref[...])
        r = jnp.square(jnp.maximum(a, 0.0)).astype(BF16)
        r_ref[0] = r
        term = _dot(r, w2_ref[...])

        @pl.when(j == 0)
        def _():
            y_ref[...] = term

        @pl.when(j > 0)
        def _():
            y_ref[...] += term

        @pl.when(j == FF_STEPS - 1)
        def _():
            x_new = x_ref[...] + _rms(y_ref[...], gq_ref[...])
            if target is None:
                xo_ref[...] = x_new
            else:
                diff = x_new - t_ref[...]
                xo_ref[...] = diff * (1.0 / D_MODEL)
                _acc_rows8(l_ref, _rows8(diff * diff) * (0.5 / D_MODEL), i == 0)

    row = lambda i, j: (i, 0)
    vec = pl.BlockSpec((1, D_MODEL), lambda i, j: (0, 0))
    in_specs = [pl.BlockSpec((tm, D_MODEL), row), vec, pl.BlockSpec((D_MODEL, FF_STEP), lambda i, j: (0, j)),
                pl.BlockSpec((FF_STEP, D_MODEL), lambda i, j: (j, 0)), vec]
    out_specs = [pl.BlockSpec((tm, D_MODEL), row)] * 3 + [pl.BlockSpec((1, tm, FF_STEP), lambda i, j: (j, i, 0))]
    out_shape = [jax.ShapeDtypeStruct((t, D_MODEL), F32), jax.ShapeDtypeStruct((t, D_MODEL), F32),
                 jax.ShapeDtypeStruct((t, D_MODEL), BF16), jax.ShapeDtypeStruct((FF_STEPS, t, FF_STEP), BF16)]
    args = [x, g_pre, w1, w2, g_post]
    if target is not None:
        in_specs.append(pl.BlockSpec((tm, D_MODEL), row))
        out_specs.append(pl.BlockSpec((SUBLANES, D_MODEL), lambda i, j: (0, 0)))
        out_shape.append(jax.ShapeDtypeStruct((SUBLANES, D_MODEL), F32))
        args.append(target)
    return pl.pallas_call(
        body, name=name, grid=(t // tm, FF_STEPS), in_specs=in_specs, out_specs=out_specs, out_shape=out_shape,
        compiler_params=_params(("parallel" if target is None else "arbitrary", "arbitrary")),
    )(*args)


def _ffn_bwd(dxo, x, y, r, g_pre, w1, w2, g_post, name):
    t = x.shape[0]
    tm = ROW_TILE

    def body(dxo_ref, x_ref, y_ref, r_ref, gp_ref, w1_ref, w2_ref, gq_ref,
             dx_ref, dy_ref, da_ref, dgp_ref, dgq_ref, acc_ref):
        i, j = pl.program_id(0), pl.program_id(1)

        @pl.when(j == 0)
        def _():
            dy, dgr = _rms_bwd(y_ref[...], gq_ref[...], dxo_ref[...])
            dy_ref[...] = dy.astype(BF16)
            _acc_rows8(dgq_ref, _rows8(dgr), i == 0)

        dr = _dot_nt(dy_ref[...], w2_ref[...])
        da = (dr * (2.0 * jnp.sqrt(r_ref[0].astype(F32)))).astype(BF16)
        da_ref[0] = da
        term = _dot_nt(da, w1_ref[...])

        @pl.when(j == 0)
        def _():
            acc_ref[...] = term

        @pl.when(j > 0)
        def _():
            acc_ref[...] += term

        @pl.when(j == FF_STEPS - 1)
        def _():
            dxn, dgr = _rms_bwd(x_ref[...], gp_ref[...], acc_ref[...])
            dx_ref[...] = dxo_ref[...] + dxn
            _acc_rows8(dgp_ref, _rows8(dgr), i == 0)

    row = lambda i, j: (i, 0)
    vec = pl.BlockSpec((1, D_MODEL), lambda i, j: (0, 0))
    acc8 = pl.BlockSpec((SUBLANES, D_MODEL), lambda i, j: (0, 0))
    return pl.pallas_call(
        body, name=name, grid=(t // tm, FF_STEPS),
        in_specs=[pl.BlockSpec((tm, D_MODEL), row)] * 3 + [
            pl.BlockSpec((1, tm, FF_STEP), lambda i, j: (j, i, 0)),
            vec, pl.BlockSpec((D_MODEL, FF_STEP), lambda i, j: (0, j)),
            pl.BlockSpec((FF_STEP, D_MODEL), lambda i, j: (j, 0)), vec],
        out_specs=[pl.BlockSpec((tm, D_MODEL), row), pl.BlockSpec((tm, D_MODEL), row),
                   pl.BlockSpec((1, tm, FF_STEP), lambda i, j: (j, i, 0)), acc8, acc8],
        out_shape=[jax.ShapeDtypeStruct((t, D_MODEL), F32), jax.ShapeDtypeStruct((t, D_MODEL), BF16),
                   jax.ShapeDtypeStruct((FF_STEPS, t, FF_STEP), BF16),
                   jax.ShapeDtypeStruct((SUBLANES, D_MODEL), F32), jax.ShapeDtypeStruct((SUBLANES, D_MODEL), F32)],
        scratch_shapes=[pltpu.VMEM((tm, D_MODEL), F32)],
        compiler_params=_params(("arbitrary", "arbitrary")),
    )(dxo, x, y, r, g_pre, w1, w2, g_post)


def _lower_bound(table):
    e = jnp.exp(table - jnp.max(table, axis=0, keepdims=True))
    return e[0:1, :] / jnp.sum(e, axis=0, keepdims=True)


def _hgrn2_block(q_ref, f_ref, lb):
    tb = f_ref.shape[0]
    sig = _sigmoid(f_ref[...])
    f = lb + (1.0 - lb) * sig
    qraw = q_ref[...]
    sq = _sigmoid(qraw)
    r = lax.broadcasted_iota(jnp.int32, (tb, tb), 0)
    c = lax.broadcasted_iota(jnp.int32, (tb, tb), 1)
    same = (r // SUB_CHUNK) == (c // SUB_CHUNK)
    logf = jnp.log(f)
    gsum = _mask_dot(same & (c <= r), logf)
    glast = _mask_dot(same, logf)
    return dict(sig=sig, f=f, kk=1.0 - f, qraw=qraw, sq=sq, qs=qraw * sq, gsum=gsum,
                eg=jnp.exp(gsum), ekd=jnp.exp(glast - gsum), a=jnp.exp(glast))


def _head_sums(x):
    parts = [jnp.broadcast_to(jnp.sum(x[:, h * HEAD_A:(h + 1) * HEAD_A], axis=1, keepdims=True), (x.shape[0], HEAD_A))
             for h in range(A_HEADS)]
    return jnp.concatenate(parts, axis=1)


def _hgrn2_intra(g, kk, qs, v):
    row = lax.broadcasted_iota(jnp.int32, g.shape, 0)
    o = _head_sums(qs * kk) * v
    for j in range(1, SUB_CHUNK):
        decay = jnp.exp(jnp.where(row >= j, g - pltpu.roll(g, j, 0), NEG))
        o = o + _head_sums(qs * pltpu.roll(kk, j, 0) * decay) * pltpu.roll(v, j, 0)
    return o


def _hgrn2_intra_bwd(g, kk, qs, v, do):
    row = lax.broadcasted_iota(jnp.int32, g.shape, 0)
    dsc = _head_sums(do * v)
    dqs, dkk, dv = dsc * kk, dsc * qs, _head_sums(qs * kk) * do
    for j in range(1, SUB_CHUNK):
        k_dn = pltpu.roll(kk, j, 0)
        decay = jnp.exp(jnp.where(row >= j, g - pltpu.roll(g, j, 0), NEG))
        d_score = _head_sums(do * pltpu.roll(v, j, 0)) * decay
        dqs = dqs + d_score * k_dn
        dkk = dkk + pltpu.roll(d_score * qs, SUB_CHUNK - j, 0)
        dv = dv + pltpu.roll(_head_sums(qs * k_dn * decay) * do, SUB_CHUNK - j, 0)
    return dqs, dkk, dv


def _hgrn2_fwd(proj, lb_table, a_norm, name):
    t = proj.shape[0]
    tb = HGRN_BLOCK
    n_tb = SEQ // tb
    n_seq = t // SEQ
    n_sub = tb // SUB_CHUNK

    def body(q_ref, f_ref, i_ref, g_ref, lbt_ref, an_ref, o_ref, pre_ref, sts_ref, st_ref,
             gs_ref, kk_ref, qs_ref, eg_ref, ekd_ref, a_ref):
        @pl.when(pl.program_id(1) == 0)
        def _():
            st_ref[...] = jnp.zeros_like(st_ref)

        an = an_ref[...]
        blk = _hgrn2_block(q_ref, f_ref, _lower_bound(lbt_ref[...]))
        for ref, key in ((gs_ref, "gsum"), (kk_ref, "kk"), (qs_ref, "qs"), (eg_ref, "eg"), (ekd_ref, "ekd"), (a_ref, "a")):
            ref[...] = blk[key]

        def step(c, carry):
            rows = pl.ds(pl.multiple_of(c * SUB_CHUNK, SUB_CHUNK), SUB_CHUNK)
            kk, qs, v = kk_ref[rows, :], qs_ref[rows, :], i_ref[rows, :]
            o = _hgrn2_intra(gs_ref[rows, :], kk, qs, v)
            qg, kd, vb = (qs * eg_ref[rows, :]).astype(BF16), (kk * ekd_ref[rows, :]).astype(BF16), v.astype(BF16)
            for h in range(A_HEADS):
                lanes = slice(h * HEAD_A, (h + 1) * HEAD_A)
                st = st_ref[h]
                sts_ref[0, c, h] = st
                o_h = o[:, lanes] + _dot_nt(qg[:, lanes], st.astype(BF16))
                st_ref[h] = st * a_ref[rows, lanes][0:1] + _dot_tn(vb[:, lanes], kd[:, lanes])
                pre_ref[rows, lanes] = o_h
                graw = g_ref[rows, lanes]
                o_ref[rows, lanes] = (_rms(o_h, an[:, lanes]) * (graw * _sigmoid(graw))).astype(BF16)
            return carry

        lax.fori_loop(0, n_sub, step, 0, unroll=2)

    def col(k):
        return pl.BlockSpec((tb, A_WIDTH), lambda b, s, k=k: (b * n_tb + s, k))

    out_rows = pl.BlockSpec((tb, A_WIDTH), lambda b, s: (b * n_tb + s, 0))
    return pl.pallas_call(
        body, name=name, grid=(n_seq, n_tb),
        in_specs=[col(0), col(1), col(2), col(3),
                  pl.BlockSpec((3, A_WIDTH), lambda b, s: (0, 0)), pl.BlockSpec((1, A_WIDTH), lambda b, s: (0, 0))],
        out_specs=[out_rows, out_rows,
                   pl.BlockSpec((1, n_sub, A_HEADS, HEAD_A, HEAD_A), lambda b, s: (b * n_tb + s, 0, 0, 0, 0))],
        out_shape=[jax.ShapeDtypeStruct((t, D_MODEL), BF16), jax.ShapeDtypeStruct((t, A_WIDTH), F32),
                   jax.ShapeDtypeStruct((n_seq * n_tb, n_sub, A_HEADS, HEAD_A, HEAD_A), F32)],
        scratch_shapes=[pltpu.VMEM((A_HEADS, HEAD_A, HEAD_A), F32)] + [pltpu.VMEM((tb, A_WIDTH), F32)] * 6,
        compiler_params=_params(("parallel", "arbitrary")),
    )(proj, proj, proj, proj, lb_table, a_norm)


def _hgrn2_bwd(proj, dcat, pre, states, lb_table, a_norm, name):
    t = proj.shape[0]
    tb = HGRN_BLOCK
    n_tb = SEQ // tb
    n_seq = t // SEQ
    n_sub = tb // SUB_CHUNK

    def body(q_ref, f_ref, i_ref, g_ref, do_ref, pre_ref, sts_ref, lbt_ref, an_ref, dp_ref, dlb_ref, dan_ref, dst_ref,
             gs_ref, kk_ref, qs_ref, eg_ref, ekd_ref, a_ref, dpre_ref, dlf_ref, dqs_ref, dkk_ref):
        b, s = pl.program_id(0), pl.program_id(1)

        @pl.when(s == 0)
        def _():
            dst_ref[...] = jnp.zeros_like(dst_ref)

        @pl.when((b == 0) & (s == 0))
        def _():
            dlb_ref[...] = jnp.zeros_like(dlb_ref)
            dan_ref[...] = jnp.zeros_like(dan_ref)

        lb = _lower_bound(lbt_ref[...])
        an = an_ref[...]
        heads = [slice(h * HEAD_A, (h + 1) * HEAD_A) for h in range(A_HEADS)]
        blk = _hgrn2_block(q_ref, f_ref, lb)
        for ref, key in ((gs_ref, "gsum"), (kk_ref, "kk"), (qs_ref, "qs"), (eg_ref, "eg"), (ekd_ref, "ekd"), (a_ref, "a")):
            ref[...] = blk[key]
        for h, lanes in enumerate(heads):
            graw, o = g_ref[:, lanes], pre_ref[:, lanes]
            sg = _sigmoid(graw)
            dout = do_ref[:, lanes]
            d_o, dgr = _rms_bwd(o, an[:, lanes], dout * (graw * sg))
            dan_ref[0:1, lanes] += jnp.sum(dgr, axis=0, keepdims=True)
            dp_ref[:, 3 * A_WIDTH + h * HEAD_A:3 * A_WIDTH + (h + 1) * HEAD_A] = (
                dout * _rms(o, an[:, lanes]) * (sg * (1.0 + graw * (1.0 - sg)))).astype(BF16)
            dpre_ref[:, lanes] = d_o

        tri_t = (lax.broadcasted_iota(jnp.int32, (SUB_CHUNK, SUB_CHUNK), 0)
                 <= lax.broadcasted_iota(jnp.int32, (SUB_CHUNK, SUB_CHUNK), 1)).astype(F32)

        def back(k, carry):
            c = n_sub - 1 - k
            rows = pl.ds(pl.multiple_of(c * SUB_CHUNK, SUB_CHUNK), SUB_CHUNK)
            g, kk, qs, v, d_o = gs_ref[rows, :], kk_ref[rows, :], qs_ref[rows, :], i_ref[rows, :], dpre_ref[rows, :]
            eg, ekd, a = eg_ref[rows, :], ekd_ref[rows, :], a_ref[rows, :]
            dqs, dkk, dv = _hgrn2_intra_bwd(g, kk, qs, v, d_o)
            qg_f, kd_f = qs * eg, kk * ekd
            qg, kd, vb, dob = qg_f.astype(BF16), kd_f.astype(BF16), v.astype(BF16), d_o.astype(BF16)
            dqg, dkd, da, dv_st = [], [], [], []
            for h, lanes in enumerate(heads):
                st, dst = sts_ref[0, c, h], dst_ref[h]
                dstb = dst.astype(BF16)
                dqg.append(_dot(dob[:, lanes], st.astype(BF16)))
                dv_st.append(_dot_nt(kd[:, lanes], dstb))
                dkd.append(_dot(vb[:, lanes], dstb))
                da.append(jnp.broadcast_to(jnp.sum(dst * st, axis=0, keepdims=True), (SUB_CHUNK, HEAD_A)))
                dst_ref[h] = dst * a[0:1, lanes] + _dot_tn(dob[:, lanes], qg[:, lanes])
            dqg, dkd, da, dv_st = [jnp.concatenate(p, axis=1) for p in (dqg, dkd, da, dv_st)]
            d_gsum = qs * dqs - kk * dkk + dqg * qg_f - dkd * kd_f
            d_glast = jnp.sum(dkd * kd_f, axis=0, keepdims=True) + da * a
            dlf_ref[rows, :] = jnp.dot(tri_t, d_gsum, precision=lax.Precision.HIGHEST,
                                       preferred_element_type=F32) + d_glast
            dqs_ref[rows, :] = dqs + dqg * eg
            dkk_ref[rows, :] = dkk + dkd * ekd
            dp_ref[rows, 2 * A_WIDTH:3 * A_WIDTH] = (dv + dv_st).astype(BF16)
            return carry

        lax.fori_loop(0, n_sub, back, 0, unroll=2)
        sig, sq, qraw = blk["sig"], blk["sq"], blk["qraw"]
        d_f = dlf_ref[...] / blk["f"] - dkk_ref[...]
        dlb_ref[0:1, :] += jnp.sum(d_f * (1.0 - sig), axis=0, keepdims=True)
        dp_ref[:, 0:A_WIDTH] = (dqs_ref[...] * (sq * (1.0 + qraw * (1.0 - sq)))).astype(BF16)
        dp_ref[:, A_WIDTH:2 * A_WIDTH] = (d_f * (1.0 - lb) * sig * (1.0 - sig)).astype(BF16)

    def rev(s):
        return n_tb - 1 - s

    def col(k):
        return pl.BlockSpec((tb, A_WIDTH), lambda b, s, k=k: (b * n_tb + rev(s), k))

    acc8 = pl.BlockSpec((SUBLANES, A_WIDTH), lambda b, s: (0, 0))
    return pl.pallas_call(
        body, name=name, grid=(n_seq, n_tb),
        in_specs=[col(0), col(1), col(2), col(3), col(0), col(0),
                  pl.BlockSpec((1, n_sub, A_HEADS, HEAD_A, HEAD_A), lambda b, s: (b * n_tb + rev(s), 0, 0, 0, 0)),
                  pl.BlockSpec((3, A_WIDTH), lambda b, s: (0, 0)), pl.BlockSpec((1, A_WIDTH), lambda b, s: (0, 0))],
        out_specs=[pl.BlockSpec((tb, 4 * A_WIDTH), lambda b, s: (b * n_tb + rev(s), 0)), acc8, acc8],
        out_shape=[jax.ShapeDtypeStruct((t, EVEN_IN), BF16)] + [jax.ShapeDtypeStruct((SUBLANES, A_WIDTH), F32)] * 2,
        scratch_shapes=[pltpu.VMEM((A_HEADS, HEAD_A, HEAD_A), F32)] + [pltpu.VMEM((tb, A_WIDTH), F32)] * 10,
        compiler_params=_params(("arbitrary", "arbitrary")),
    )(proj, proj, proj, proj, dcat, pre, states, lb_table, a_norm)


GMLP_ROWS = 512


def _gmlp_chunk(ub, vb, ln_g, ln_b, ws, bias):
    u = [_gelu(a) for a in ub]
    v = [_gelu(a) for a in vb]
    mu = sum(jnp.sum(a, axis=-1, keepdims=True) for a in v) * (1.0 / B_WIDTH)
    cen = [a - mu for a in v]
    var = sum(jnp.sum(a * a, axis=-1, keepdims=True) for a in cen) * (1.0 / B_WIDTH)
    inv = lax.rsqrt(var + EPS)
    r = lax.broadcasted_iota(jnp.int32, (B_CHUNK, B_CHUNK), 0)
    c = lax.broadcasted_iota(jnp.int32, (B_CHUNK, B_CHUNK), 1)
    outs = []
    for g in range(B_GROUPS):
        vn = (cen[g] * inv * ln_g[g] + ln_b[g]).astype(BF16)
        wm = jnp.where(c <= r, ws[g], 0.0).astype(BF16)
        outs.append(u[g] * (_dot(wm, vn) + bias[g]))
    return outs


def _lane_groups(ref, rows=slice(None)):
    return [ref[rows, g * LANES:(g + 1) * LANES] for g in range(B_GROUPS)]


def _gmlp_fwd(proj, mixed, ln_g, ln_b, ws, bias_t, name):
    t = proj.shape[0]
    tm = GMLP_ROWS

    def body(u_ref, v_ref, lg_ref, lb_ref, ws_ref, bt_ref, _, o_ref):
        for ch in range(tm // B_CHUNK):
            rows = slice(ch * B_CHUNK, (ch + 1) * B_CHUNK)
            outs = _gmlp_chunk(_lane_groups(u_ref, rows), _lane_groups(v_ref, rows), _lane_groups(lg_ref),
                               _lane_groups(lb_ref), [ws_ref[g] for g in range(B_GROUPS)],
                               [bt_ref[:, g:g + 1] for g in range(B_GROUPS)])
            for g in range(B_GROUPS):
                o_ref[rows, g * LANES:(g + 1) * LANES] = outs[g].astype(BF16)

    vec = pl.BlockSpec((1, B_WIDTH), lambda i: (0, 0))
    return pl.pallas_call(
        body, name=name, grid=(t // tm,),
        in_specs=[pl.BlockSpec((tm, B_WIDTH), lambda i: (i, 4)), pl.BlockSpec((tm, B_WIDTH), lambda i: (i, 5)), vec, vec,
                  pl.BlockSpec((B_GROUPS, B_CHUNK, B_CHUNK), lambda i: (0, 0, 0)),
                  pl.BlockSpec((B_CHUNK, B_GROUPS), lambda i: (0, 0)), pl.BlockSpec(memory_space=pl.ANY)],
        out_specs=pl.BlockSpec((tm, B_WIDTH), lambda i: (i, 1)),
        out_shape=jax.ShapeDtypeStruct(mixed.shape, BF16),
        input_output_aliases={6: 0},
        compiler_params=_params(("parallel",)),
    )(proj, proj, ln_g, ln_b, ws, bias_t, mixed)


def _gmlp_bwd(proj, dcat, dproj, ln_g, ln_b, ws, bias_t, name):
    t = proj.shape[0]
    tm = GMLP_ROWS

    def body(u_ref, v_ref, do_ref, lg_ref, lb_ref, ws_ref, bt_ref, _, duv_ref, dlg_ref, dlb_ref, dws_ref, dbt_ref):
        @pl.when(pl.program_id(0) == 0)
        def _():
            dlg_ref[...] = jnp.zeros_like(dlg_ref)
            dlb_ref[...] = jnp.zeros_like(dlb_ref)
            dws_ref[...] = jnp.zeros_like(dws_ref)
            dbt_ref[...] = jnp.zeros_like(dbt_ref)

        for ch in range(tm // B_CHUNK):
            rows = slice(ch * B_CHUNK, (ch + 1) * B_CHUNK)
            _, vjp = jax.vjp(
                _gmlp_chunk, _lane_groups(u_ref, rows), _lane_groups(v_ref, rows), _lane_groups(lg_ref),
                _lane_groups(lb_ref), [ws_ref[g] for g in range(B_GROUPS)],
                [bt_ref[:, g:g + 1] for g in range(B_GROUPS)])
            du, dv, dlg, dlb, dw, dbt = vjp(_lane_groups(do_ref, rows))
            for g in range(B_GROUPS):
                lanes = slice(g * LANES, (g + 1) * LANES)
                duv_ref[rows, lanes] = du[g].astype(BF16)
                duv_ref[rows, B_WIDTH + g * LANES:B_WIDTH + (g + 1) * LANES] = dv[g].astype(BF16)
                dlg_ref[0:1, lanes] += dlg[g]
                dlb_ref[0:1, lanes] += dlb[g]
                dws_ref[g] += dw[g]
                dbt_ref[:, g:g + 1] += dbt[g]

    vec = pl.BlockSpec((1, B_WIDTH), lambda i: (0, 0))
    acc8 = pl.BlockSpec((SUBLANES, B_WIDTH), lambda i: (0, 0))
    ws_spec = pl.BlockSpec((B_GROUPS, B_CHUNK, B_CHUNK), lambda i: (0, 0, 0))
    bt_spec = pl.BlockSpec((B_CHUNK, B_GROUPS), lambda i: (0, 0))
    return pl.pallas_call(
        body, name=name, grid=(t // tm,),
        in_specs=[pl.BlockSpec((tm, B_WIDTH), lambda i: (i, 4)), pl.BlockSpec((tm, B_WIDTH), lambda i: (i, 5)),
                  pl.BlockSpec((tm, B_WIDTH), lambda i: (i, 1)), vec, vec, ws_spec, bt_spec,
                  pl.BlockSpec(memory_space=pl.ANY)],
        out_specs=[pl.BlockSpec((tm, 2 * B_WIDTH), lambda i: (i, 2)), acc8, acc8, ws_spec, bt_spec],
        out_shape=[jax.ShapeDtypeStruct(dproj.shape, BF16), jax.ShapeDtypeStruct((SUBLANES, B_WIDTH), F32),
                   jax.ShapeDtypeStruct((SUBLANES, B_WIDTH), F32),
                   jax.ShapeDtypeStruct((B_GROUPS, B_CHUNK, B_CHUNK), F32),
                   jax.ShapeDtypeStruct((B_CHUNK, B_GROUPS), F32)],
        input_output_aliases={7: 0},
        compiler_params=_params(("arbitrary",)),
    )(proj, proj, dcat, ln_g, ln_b, ws, bias_t, dproj)


QK_SCALE = 1.0 / math.sqrt(C_HEAD_DIM)
ATTN_UNROLL = 16
ATTN_PAIRS = 2
LANE_GROUPS = D_MODEL // LANES
ATTN_STEPS = LANE_GROUPS // ATTN_PAIRS
Q_BLOCKS = SEQ // C_BLOCK


def _attn_window(i, d):
    sub_blocks = Q_BLOCKS // d
    q0 = pl.multiple_of(i * C_BLOCK, C_BLOCK)
    k0 = pl.multiple_of(jnp.maximum(i - 1, 0) * C_BLOCK, C_BLOCK)
    key = k0 + lax.broadcasted_iota(jnp.int32, (C_BLOCK, 2 * C_BLOCK), 1)
    dist = (q0 + lax.broadcasted_iota(jnp.int32, (C_BLOCK, 2 * C_BLOCK), 0)) - key
    own_subsequence = (key >= q0) | (i % sub_blocks > 0)
    return pl.ds(q0, C_BLOCK), pl.ds(k0, 2 * C_BLOCK), (dist >= 0) & (dist <= C_BLOCK) & own_subsequence


def _head_masks():
    lane = lax.broadcasted_iota(jnp.int32, (C_BLOCK, LANES), 1)
    return [lane < C_HEAD_DIM, lane >= C_HEAD_DIM]


def _flat_spec(col_of):
    return pl.BlockSpec((1, SEQ, ATTN_PAIRS * LANES), lambda b, g: (b, 0, col_of(g)))


def _put_heads(tile, g, col0, col1):
    lane = lax.broadcasted_iota(jnp.int32, tile.shape, 1)
    return jnp.where(lane == 2 * g, col0, jnp.where(lane == 2 * g + 1, col1, tile))


def _get_head(tile, h):
    lane = lax.broadcasted_iota(jnp.int32, tile.shape, 1)
    return jnp.sum(jnp.where(lane == h, tile, 0.0), axis=1, keepdims=True)


PER_HEAD_SPEC = pl.BlockSpec((1, SEQ, LANES), lambda b, g: (b, 0, 0))


def _attn_branch_fwd(qkv, name):
    n_seq, d, l, _ = qkv.shape
    flat = qkv.reshape(n_seq, SEQ, ODD_IN)

    def body(q_ref, k_ref, v_ref, o_ref, m_ref, l_ref):
        heads = _head_masks()
        g = pl.program_id(1)

        @pl.when(g == 0)
        def _():
            m_ref[...] = jnp.zeros_like(m_ref)
            l_ref[...] = jnp.zeros_like(l_ref)

        def block(i, carry):
            rows, keys, mask = _attn_window(i, d)
            m_tile, l_tile = m_ref[0, rows, :], l_ref[0, rows, :]
            for pair in range(ATTN_PAIRS):
                lanes = slice(pair * LANES, (pair + 1) * LANES)
                q, k, v = q_ref[0, rows, lanes], k_ref[0, keys, lanes], v_ref[0, keys, lanes]
                res = []
                for hm in heads:
                    s = jnp.where(mask, _dot_nt(jnp.where(hm, q, 0), k), NEG)
                    m = jnp.max(s, axis=-1, keepdims=True)
                    p = jnp.exp(s - m)
                    res.append((_dot(p.astype(BF16), v), m, jnp.sum(p, axis=-1, keepdims=True)))
                o_ref[0, rows, lanes] = jnp.where(heads[0], res[0][0], res[1][0])
                m_tile = _put_heads(m_tile, g * ATTN_PAIRS + pair, res[0][1], res[1][1])
                l_tile = _put_heads(l_tile, g * ATTN_PAIRS + pair, res[0][2], res[1][2])
            m_ref[0, rows, :] = m_tile
            l_ref[0, rows, :] = l_tile
            return carry

        lax.fori_loop(0, Q_BLOCKS, block, 0, unroll=ATTN_UNROLL)

    o, m, l_sum = pl.pallas_call(
        body, name=name, grid=(n_seq, ATTN_STEPS),
        in_specs=[_flat_spec(lambda g: g), _flat_spec(lambda g: ATTN_STEPS + g),
                  _flat_spec(lambda g: 2 * ATTN_STEPS + g)],
        out_specs=[_flat_spec(lambda g: g), PER_HEAD_SPEC, PER_HEAD_SPEC],
        out_shape=[jax.ShapeDtypeStruct((n_seq, SEQ, D_MODEL), F32)] + [jax.ShapeDtypeStruct((n_seq, SEQ, LANES), F32)] * 2,
        compiler_params=_params(("parallel", "arbitrary")),
    )(flat, flat, flat)
    return [o.reshape(n_seq, d, l, D_MODEL), m.reshape(n_seq, d, l, LANES), l_sum.reshape(n_seq, d, l, LANES)]


def _attn_merge(branches, name):
    n_seq = branches[0][0].shape[0]
    t = n_seq * SEQ
    tm = MERGE_TILE

    def body(*refs):
        ins = refs[:9]
        o_ref, ob_ref, lse_ref = refs[9:12]
        nat = refs[12:]
        for b, d in enumerate(C_DILATIONS[1:]):
            for k in range(3):
                _load_dilated(ins[3 + 3 * b + k], d, nat[3 * b + k])
        ms = [ins[1][0, 0], nat[1][0], nat[4][0]]
        ls = [ins[2][0, 0], nat[2][0], nat[5][0]]
        m_all = jnp.maximum(jnp.maximum(ms[0], ms[1]), ms[2])
        ws = [jnp.exp(ms[b] - m_all) for b in range(3)]
        lane = lax.broadcasted_iota(jnp.int32, m_all.shape, 1)
        total = jnp.where(lane < C_HEADS, ws[0] * ls[0] + ws[1] * ls[1] + ws[2] * ls[2], 1.0)
        lse_ref[...] = m_all + jnp.log(total)
        first_head = lane < C_HEAD_DIM
        for p in range(LANE_GROUPS):
            lanes = slice(p * LANES, (p + 1) * LANES)
            spread = lambda c: jnp.where(first_head, c[:, 2 * p:2 * p + 1], c[:, 2 * p + 1:2 * p + 2])
            os_ = [ins[0][0, 0, :, lanes], nat[0][p], nat[3][p]]
            o = (spread(ws[0]) * os_[0] + spread(ws[1]) * os_[1] + spread(ws[2]) * os_[2]) / spread(total)
            o_ref[:, lanes] = o
            ob_ref[:, lanes] = o.astype(BF16)

    row = pl.BlockSpec((tm, D_MODEL), lambda i: (i, 0))
    flat = [a for br in branches for a in br]
    in_specs = []
    for wide, narrow in zip(_dilated_specs(tm, D_MODEL, lambda: 0), _dilated_specs(tm, LANES, lambda: 0)):
        in_specs += [wide, narrow, narrow]
    per_head = pltpu.VMEM((1, tm, LANES), F32)
    return pl.pallas_call(
        body, name=name, grid=(t // tm,), in_specs=in_specs,
        out_specs=[row, row, pl.BlockSpec((tm, LANES), lambda i: (i, 0))],
        out_shape=[jax.ShapeDtypeStruct((t, D_MODEL), F32), jax.ShapeDtypeStruct((t, D_MODEL), BF16),
                   jax.ShapeDtypeStruct((t, LANES), F32)],
        scratch_shapes=[pltpu.VMEM((LANE_GROUPS, tm, LANES), F32), per_head, per_head] * 2,
        compiler_params=_params(("parallel",)),
    )(*flat)


def _attn_branch_bwd(qkv, dout, lse, delta, name):
    n_seq, d, l, _ = qkv.shape
    flat = lambda a: a.reshape(n_seq, SEQ, a.shape[-1])

    def body(q_ref, k_ref, v_ref, do_ref, lse_nat_ref, dl_nat_ref, dq_ref, dk_ref, dv_ref, lse_ref, dl_ref,
             dkt_ref, dvt_ref):
        heads = _head_masks()
        g = pl.program_id(1)
        dkt_ref[...] = jnp.zeros_like(dkt_ref)
        dvt_ref[...] = jnp.zeros_like(dvt_ref)
        for nat_ref, dst_ref in ((lse_nat_ref, lse_ref), (dl_nat_ref, dl_ref)):
            for r in range(d):
                rows = pl.ds(r, l, stride=d) if d > 1 else slice(None)
                dst_ref[r * l:(r + 1) * l, :] = nat_ref.at[0][rows, :]

        def block(i, carry):
            rows, keys, mask = _attn_window(i, d)
            lse_b, dl_b = lse_ref[rows, :], dl_ref[rows, :]
            for pair in range(ATTN_PAIRS):
                lanes = slice(pair * LANES, (pair + 1) * LANES)
                q, do = q_ref[0, rows, lanes], do_ref[0, rows, lanes]
                k, v = k_ref[0, keys, lanes], v_ref[0, keys, lanes]
                dq, dk, dv = [], None, None
                for hh, hm in enumerate(heads):
                    head = 2 * (g * ATTN_PAIRS + pair) + hh
                    qh, doh = jnp.where(hm, q, 0), jnp.where(hm, do, 0)
                    s = jnp.where(mask, _dot_nt(qh, k), NEG)
                    p = jnp.exp(s - _get_head(lse_b, head))
                    ds = (p * (_dot_nt(doh, v) - _get_head(dl_b, head))).astype(BF16)
                    dq.append(_dot(ds, k) * QK_SCALE)
                    dk_h, dv_h = _dot_tn(qh, ds), _dot_tn(doh, p.astype(BF16))
                    dk = dk_h if dk is None else dk + dk_h
                    dv = dv_h if dv is None else dv + dv_h
                dq_ref[0, rows, lanes] = jnp.where(heads[0], dq[0], dq[1]).astype(BF16)
                dkt_ref[lanes, keys] += dk
                dvt_ref[lanes, keys] += dv
            return carry

        lax.fori_loop(0, Q_BLOCKS, block, 0, unroll=ATTN_UNROLL)
        for c in range(SEQ // ROW_TILE):
            rows = slice(c * ROW_TILE, (c + 1) * ROW_TILE)
            dk_ref[0, rows, :] = dkt_ref[:, rows].T.astype(BF16)
            dv_ref[0, rows, :] = dvt_ref[:, rows].T.astype(BF16)

    act = _flat_spec(lambda g: g)
    outs = pl.pallas_call(
        body, name=name, grid=(n_seq, ATTN_STEPS),
        in_specs=[_flat_spec(lambda g: g), _flat_spec(lambda g: ATTN_STEPS + g),
                  _flat_spec(lambda g: 2 * ATTN_STEPS + g), act, PER_HEAD_SPEC, PER_HEAD_SPEC],
        out_specs=[act] * 3,
        out_shape=[jax.ShapeDtypeStruct((n_seq, SEQ, D_MODEL), BF16)] * 3,
        scratch_shapes=[pltpu.VMEM((SEQ, LANES), F32)] * 2 + [pltpu.VMEM((ATTN_PAIRS * LANES, SEQ), F32)] * 2,
        compiler_params=_params(("parallel", "parallel")),
    )(flat(qkv), flat(qkv), flat(qkv), flat(dout), lse, delta)
    return [o.reshape(n_seq, d, l, D_MODEL) for o in outs]


def _attn_combine_bwd(grads, rope, name):
    n_seq = grads[0][0].shape[0]
    t = n_seq * SEQ
    tm = MERGE_TILE

    def body(*refs):
        c_ref, s_ref, o_ref, nat4_ref, nat16_ref = refs[9:]
        for sec in range(3):
            _load_dilated(refs[3 + sec], 4, nat4_ref)
            _load_dilated(refs[6 + sec], 16, nat16_ref)
            for p in range(LANE_GROUPS):
                blk = refs[sec][0, 0, :, p * LANES:(p + 1) * LANES] + nat4_ref[p] + nat16_ref[p]
                if sec < 2:
                    blk = blk * c_ref[...] - _swap_halves(blk) * s_ref[...]
                o_ref[:, sec * D_MODEL + p * LANES:sec * D_MODEL + (p + 1) * LANES] = blk.astype(BF16)

    tab = pl.BlockSpec((tm, LANES), lambda i: (i, 0))
    flat = [a for br in grads for a in br]
    in_specs = []
    for spec in _dilated_specs(tm, D_MODEL, lambda: 0):
        in_specs += [spec] * 3
    return pl.pallas_call(
        body, name=name, grid=(t // tm,), in_specs=in_specs + [tab, tab],
        out_specs=pl.BlockSpec((tm, ODD_IN), lambda i: (i, 0)),
        out_shape=jax.ShapeDtypeStruct((t, ODD_IN), BF16),
        scratch_shapes=[pltpu.VMEM((LANE_GROUPS, tm, LANES), F32)] * 2,
        compiler_params=_params(("parallel",)),
    )(*flat, *rope)


def _adamw(w, g, m, v):
    m = ADAM_B1 * m + (1.0 - ADAM_B1) * g
    v = ADAM_B2 * v + (1.0 - ADAM_B2) * jnp.square(g)
    m_hat = m / (1.0 - ADAM_B1 ** ADAM_STEP)
    v_hat = v / (1.0 - ADAM_B2 ** ADAM_STEP)
    delta = -ADAM_LR * (m_hat / (jnp.sqrt(v_hat) + ADAM_EPS) + ADAM_WD * w)
    return delta, m, v


def _adamw_sharded(parts, w, m, v, after, name):
    n_layers, rows, cols = w.shape
    tr = min(rows, 256)

    def body(*refs):
        p_refs = refs[:n_layers]
        w_ref, m_ref, v_ref, _, g_ref, d_ref, mo_ref, vo_ref = refs[n_layers:]
        layer = pl.program_id(0)
        g = None
        for l, p_ref in enumerate(p_refs):
            g_l = p_ref[0].astype(F32)
            for s in range(1, N_DEV):
                g_l = g_l + p_ref[s].astype(F32)
            g = g_l if g is None else jnp.where(layer == l, g_l, g)
        delta, mn, vn = _adamw(w_ref[0], g, m_ref[0], v_ref[0])
        g_ref[0] = g
        d_ref[0] = delta
        mo_ref[0] = mn
        vo_ref[0] = vn

    def part_spec(l):
        return pl.BlockSpec((N_DEV, tr, cols), lambda a, i: (0, jnp.where(a == l, i, 0), 0))

    row = pl.BlockSpec((1, tr, cols), lambda a, i: (a, i, 0))
    return pl.pallas_call(
        body, name=name, grid=(n_layers, rows // tr),
        in_specs=[part_spec(l) for l in range(n_layers)] + [row, row, row, pl.BlockSpec(memory_space=pl.ANY)],
        out_specs=[row] * 4, out_shape=[jax.ShapeDtypeStruct(w.shape, F32)] * 4,
        compiler_params=_params(("arbitrary", "arbitrary")),
    )(*parts, w, m, v, after)


def _small_update(gathered, where, weights, moments_m, moments_v, lb_index, name):
    n = len(weights)
    n_g = len(gathered)

    def body(*refs):
        g_refs = refs[:n_g]
        w_refs, m_refs, v_refs = refs[n_g:n_g + n], refs[n_g + n:n_g + 2 * n], refs[n_g + 2 * n:n_g + 3 * n]
        outs = refs[n_g + 3 * n:]

        def total(k):
            array, rows, lanes = where[k]
            ref = g_refs[array]
            index = (slice(None),) * (len(ref.shape) - 1) if rows is None else (rows, lanes)
            acc = ref[(0,) + index]
            for s in range(1, N_DEV):
                acc = acc + ref[(s,) + index]
            return acc

        loss_rows = total(n)
        outs[0][...] = jnp.sum(jnp.sum(loss_rows, axis=1, keepdims=True), axis=0, keepdims=True)
        for k in range(n):
            part = total(k)
            if k == lb_index:
                dlb = jnp.sum(part, axis=0, keepdims=True)
                tab = w_refs[k][...]
                e = jnp.exp(tab - jnp.max(tab, axis=0, keepdims=True))
                p = e / jnp.sum(e, axis=0, keepdims=True)
                first = lax.broadcasted_iota(jnp.int32, p.shape, 0) == 0
                grads = [(slice(None), p * (jnp.where(first, dlb, 0.0) - p[0:1, :] * dlb))]
            elif part.shape == w_refs[k].shape:
                grads = [(slice(None), part)]
            else:
                grads = [(slice(l, l + 1), jnp.sum(part[l * SUBLANES:(l + 1) * SUBLANES], axis=0, keepdims=True))
                         for l in range(w_refs[k].shape[0])]
            for rows, g in grads:
                delta, mn, vn = _adamw(w_refs[k][rows], g, m_refs[k][rows], v_refs[k][rows])
                outs[1 + 4 * k][rows] = g
                outs[2 + 4 * k][rows] = delta
                outs[3 + 4 * k][rows] = mn
                outs[4 + 4 * k][rows] = vn

    vmem = pl.BlockSpec(memory_space=pltpu.VMEM)
    out_shape = [jax.ShapeDtypeStruct((1, 1), F32)]
    for w in weights:
        out_shape += [jax.ShapeDtypeStruct(w.shape, F32)] * 4
    args = list(gathered) + list(weights) + list(moments_m) + list(moments_v)
    return pl.pallas_call(
        body, name=name, in_specs=[vmem] * len(args), out_specs=[vmem] * len(out_shape), out_shape=out_shape,
        compiler_params=pltpu.CompilerParams(vmem_limit_bytes=VMEM_LIMIT),
    )(*args)


def kernel(x, positions, norm_mix_pre, norm_mix_post, norm_ffn_pre, norm_ffn_post, w_in_even, lb_table, a_norm, b_ln_g, b_ln_b, b_ws, b_bias, w_out_even, w_in_odd, w_out_odd, w_ff1, w_ff2, loss_target, m_norm_mix_pre, m_norm_mix_post, m_norm_ffn_pre, m_norm_ffn_post, m_w_in_even, m_lb_table, m_a_norm, m_b_ln_g, m_b_ln_b, m_b_ws, m_b_bias, m_w_out_even, m_w_in_odd, m_w_out_odd, m_w_ff1, m_w_ff2, v_norm_mix_pre, v_norm_mix_post, v_norm_ffn_pre, v_norm_ffn_post, v_w_in_even, v_lb_table, v_a_norm, v_b_ln_g, v_b_ln_b, v_b_ws, v_b_bias, v_w_out_even, v_w_in_odd, v_w_out_odd, v_w_ff1, v_w_ff2):
    n_seq = x.shape[0]
    t = n_seq * SEQ
    x0 = x.reshape(t, D_MODEL)
    target = loss_target.reshape(t, D_MODEL)

    me = _my_slot().astype(jnp.int32).reshape(1)

    order = ["in_e", "out_e", "ff1_0", "ff2_0", "in_o", "out_o", "ff1_1", "ff2_1"]
    shards = dict(in_e=w_in_even[0], out_e=w_out_even[0], in_o=w_in_odd[0], out_o=w_out_odd[0],
                  ff1_0=w_ff1[0], ff1_1=w_ff1[1], ff2_0=w_ff2[0], ff2_1=w_ff2[1])
    by_columns = ("in_e", "in_o", "ff1_0", "ff1_1")

    def place(k, after):
        if k in by_columns:
            return _place_own_columns(shards[k], me, "place_" + k, after)
        return _place_own(shards[k], me, "place_" + k, False, after=after)

    gathers = {}
    send0, recv0, land0, _, token0 = _exchange_start([place(order[0], None)], [None], "gather_start_first")
    gathers[order[0]] = (land0[0], send0[0], recv0[0])
    sends, recvs, lands, _, g_token = _exchange_start([place(k, token0) for k in order[1:]],
                                                      [None] * (len(order) - 1), "gather_start")
    for k, land, send, recv in zip(order[1:], lands, sends, recvs):
        gathers[k] = (land, send, recv)

    def get_w(keys, after):
        lands_k, sends_k, recvs_k = zip(*[gathers[k] for k in keys])
        return _exchange_wait(list(lands_k), [None] * len(keys), list(sends_k), list(recvs_k), after,
                              "gather_wait_" + keys[0])

    sent = {}

    def put_g(group, blocks):
        keys = list(blocks)
        own = [_place_own(blocks[k], me, "own_" + k, True) for k in keys]
        send_sems, recv_sems, own, srcs, token = _exchange_start(own, [blocks[k] for k in keys], "scatter_start_" + group)
        sent[group] = (keys, own, srcs, send_sems, recv_sems)
        return token

    rope = _rope_tables(positions)
    bias_t = b_bias[0].T
    grads = _local_step(x0, target, rope, norm_mix_pre, norm_mix_post, norm_ffn_pre, norm_ffn_post, lb_table,
                        a_norm, b_ln_g, b_ln_b, b_ws[0], bias_t, get_w, put_g, g_token)
    (dx0, loss_part, dg_mix_pre, dg_mix_post, dg_ffn_pre, dg_ffn_post, d_lb, d_a_norm, d_ln_g, d_ln_b, d_ws,
     d_bias_t) = grads

    packed = jnp.concatenate([dg_mix_pre, dg_mix_post, dg_ffn_pre, dg_ffn_post,
                              jnp.concatenate([d_lb, d_a_norm], axis=1), jnp.concatenate([d_ln_g, d_ln_b], axis=1),
                              loss_part], axis=0)
    small_lands = [_place_own(a, me, "own_small%d" % k, False, F32) for k, a in enumerate((packed, d_ws, d_bias_t))]
    s_send, s_recv, small_lands, _, after = _exchange_start(small_lands, [None] * 3, "gather_small_start")

    big = dict(w_in_even=(["in_e"], w_in_even, m_w_in_even, v_w_in_even),
               w_out_even=(["out_e"], w_out_even, m_w_out_even, v_w_out_even),
               w_in_odd=(["in_o"], w_in_odd, m_w_in_odd, v_w_in_odd),
               w_out_odd=(["out_o"], w_out_odd, m_w_out_odd, v_w_out_odd),
               w_ff1=(["ff1_0", "ff1_1"], w_ff1, m_w_ff1, v_w_ff1), w_ff2=(["ff2_0", "ff2_1"], w_ff2, m_w_ff2, v_w_ff2))
    recv, big_out = {}, {}
    for groups, names in ((("ffn1", "ffn0"), ("w_ff1", "w_ff2")), (("mix1",), ("w_in_odd", "w_out_odd")),
                          (("mix0",), ("w_in_even", "w_out_even"))):
        for group in groups:
            keys, own, srcs, send_sems, recv_sems = sent[group]
            recv.update(zip(keys, _exchange_wait(own, srcs, send_sems, recv_sems, after, "scatter_wait_" + group)))
        for nm in names:
            keys, w, m, v = big[nm]
            big_out[nm] = _adamw_sharded([recv[k] for k in keys], w, m, v, after, "adamw_" + nm)
            after = big_out[nm][0]
    big_out = [big_out[nm] for nm in ("w_in_even", "w_out_even", "w_in_odd", "w_out_odd", "w_ff1", "w_ff2")]
    gathered = _exchange_wait(small_lands, [None] * 3, s_send, s_recv, after, "gather_small_wait")
    rows8 = lambda k: slice(SUBLANES * k, SUBLANES * (k + 1))
    left, right, every = slice(0, A_WIDTH), slice(A_WIDTH, 2 * A_WIDTH), slice(None)
    where = [(0, slice(0, 16), every), (0, slice(16, 32), every), (0, slice(32, 48), every), (0, slice(48, 64), every),
             (0, rows8(8), left), (0, rows8(8), right), (0, rows8(9), left), (0, rows8(9), right),
             (1, None, None), (2, None, None), (0, rows8(10), every)]
    small_w = [norm_mix_pre, norm_mix_post, norm_ffn_pre, norm_ffn_post, lb_table, a_norm, b_ln_g, b_ln_b,
               b_ws[0], bias_t]
    small_m = [m_norm_mix_pre, m_norm_mix_post, m_norm_ffn_pre, m_norm_ffn_post, m_lb_table, m_a_norm, m_b_ln_g,
               m_b_ln_b, m_b_ws[0], m_b_bias[0].T]
    small_v = [v_norm_mix_pre, v_norm_mix_post, v_norm_ffn_pre, v_norm_ffn_post, v_lb_table, v_a_norm, v_b_ln_g,
               v_b_ln_b, v_b_ws[0], v_b_bias[0].T]
    small_out = _small_update(gathered, where, small_w, small_m, small_v, 4, "small_update")
    loss = small_out[0].reshape(())
    small = [small_out[1 + 4 * k:5 + 4 * k] for k in range(len(small_w))]
    small[8] = [a[None] for a in small[8]]
    small[9] = [a.T[None] for a in small[9]]

    per_weight = small[0:4] + [big_out[0]] + small[4:10] + big_out[1:6]
    grad_x = dx0.reshape(x.shape)
    out = [loss, grad_x]
    for kind in range(4):
        out += [p[kind] for p in per_weight]
    return tuple(out)


def _local_step(x0, target, rope, norm_mix_pre, norm_mix_post, norm_ffn_pre, norm_ffn_post, lb_table, a_norm,
                b_ln_g, b_ln_b, ws, bias_t, get_w, put_g, token):
    def gain(a, l, tok):
        return a[l:l + 1] if tok is None else a[l:l + 1] + tok[0:1, 0:1]

    full = lambda a: a.reshape(-1, D_MODEL)
    owners = lambda a: a.reshape((N_DEV, -1) + a.shape[1:])

    (g_in_e,) = get_w(["in_e"], token)
    proj, h_mix0 = _norm_inproj(x0, gain(norm_mix_pre, 0, token), g_in_e, "inproj_even")
    mixed, pre_a, states = _hgrn2_fwd(proj, lb_table, a_norm, "hgrn2_fwd")
    mixed = _gmlp_fwd(proj, mixed, b_ln_g, b_ln_b, ws, bias_t, "gmlp_fwd")
    w_out_e = full(get_w(["out_e"], mixed)[0])
    x1, mix0 = _outproj([mixed], w_out_e, x0, gain(norm_mix_post, 0, None), "outproj_even")
    w1_0, w2_0 = get_w(["ff1_0", "ff2_0"], x1)
    w2_0 = full(w2_0)
    x2, y0, h_ffn0, r0 = _ffn_fwd(x1, gain(norm_ffn_pre, 0, None), w1_0, w2_0, gain(norm_ffn_post, 0, None), "ffn_fwd_0")
    (g_in_o,) = get_w(["in_o"], x2)
    *qkv, h_mix1 = _norm_inproj_rope(x2, gain(norm_mix_pre, 1, None), g_in_o, rope, "inproj_odd")
    branches = [_attn_branch_fwd(a, "attn_fwd_d%d" % d) for a, d in zip(qkv, C_DILATIONS)]
    attn, attn_b, lse = _attn_merge(branches, "attn_merge")
    w_out_o = full(get_w(["out_o"], attn_b)[0])
    x3, mix1 = _outproj([attn_b], w_out_o, x2, gain(norm_mix_post, 1, None), "outproj_odd")
    w1_1, w2_1 = get_w(["ff1_1", "ff2_1"], x3)
    w2_1 = full(w2_1)
    dx4, y1, h_ffn1, r1, loss_part = _ffn_fwd(x3, gain(norm_ffn_pre, 1, None), w1_1, w2_1, gain(norm_ffn_post, 1, None),
                                              "ffn_fwd_1", target)

    dx3, dy1, da1, dg_ffn_pre1, dg_ffn_post1 = _ffn_bwd(
        dx4, x3, y1, r1, gain(norm_ffn_pre, 1, None), w1_1, w2_1, gain(norm_ffn_post, 1, None), "ffn_bwd_1")
    gw_ff1_1 = _grad_w(h_ffn1, da1, True, "grad_w_ff1_1")
    gw_ff2_1 = _grad_w(r1, dy1, False, "grad_w_ff2_1")
    tok = put_g("ffn1", dict(ff1_1=gw_ff1_1, ff2_1=owners(gw_ff2_1)))
    *dattn, delta, dz1, dg_mix_post1 = _outproj_bwd_attn(dx3, mix1, gain(norm_mix_post, 1, tok), w_out_o, attn,
                                                  "outproj_bwd_odd")
    gw_out_o = _grad_w(attn_b, dz1, False, "grad_w_out_odd")
    per_seq = lambda a: a.reshape(-1, SEQ, LANES)
    grads_c = [_attn_branch_bwd(qkv[b], dattn[b], per_seq(lse), per_seq(delta), "attn_bwd_d%d" % d)
               for b, d in enumerate(C_DILATIONS)]
    dqkv = _attn_combine_bwd(grads_c, rope, "attn_combine_bwd")
    gw_in_o = _grad_w(h_mix1, dqkv, True, "grad_w_in_odd")
    tok = put_g("mix1", dict(out_o=owners(gw_out_o), in_o=gw_in_o))
    dx2, dg_mix_pre1 = _inproj_bwd(dqkv, g_in_o, dx3, x2, gain(norm_mix_pre, 1, tok), "inproj_bwd_odd")

    dx1, dy0, da0, dg_ffn_pre0, dg_ffn_post0 = _ffn_bwd(
        dx2, x1, y0, r0, gain(norm_ffn_pre, 0, None), w1_0, w2_0, gain(norm_ffn_post, 0, None), "ffn_bwd_0")
    gw_ff1_0 = _grad_w(h_ffn0, da0, True, "grad_w_ff1_0")
    gw_ff2_0 = _grad_w(r0, dy0, False, "grad_w_ff2_0")
    tok = put_g("ffn0", dict(ff1_0=gw_ff1_0, ff2_0=owners(gw_ff2_0)))
    dcat, dz0, dg_mix_post0 = _outproj_bwd(dx1, mix0, gain(norm_mix_post, 0, tok), w_out_e, "outproj_bwd_even")
    gw_out_e = _grad_w(mixed, dz0, False, "grad_w_out_even")
    dproj, d_lb, d_a_norm = _hgrn2_bwd(proj, dcat, pre_a, states, lb_table, a_norm, "hgrn2_bwd")
    dproj, d_ln_g, d_ln_b, d_ws, d_bias_t = _gmlp_bwd(proj, dcat, dproj, b_ln_g, b_ln_b, ws, bias_t, "gmlp_bwd")
    gw_in_e = _grad_w(h_mix0, dproj, True, "grad_w_in_even")
    tok = put_g("mix0", dict(out_e=owners(gw_out_e), in_e=gw_in_e))
    dx0, dg_mix_pre0 = _inproj_bwd(dproj, g_in_e, dx1, x0, gain(norm_mix_pre, 0, tok), "inproj_bwd_even")

    layers = lambda a, b: jnp.concatenate([a, b], axis=0)
    return (dx0, loss_part, layers(dg_mix_pre0, dg_mix_pre1), layers(dg_mix_post0, dg_mix_post1),
            layers(dg_ffn_pre0, dg_ffn_pre1), layers(dg_ffn_post0, dg_ffn_post1),
            d_lb, d_a_norm, d_ln_g, d_ln_b, d_ws, d_bias_t)
```

```python
import math

import jax
import jax.numpy as jnp
from jax import lax
from jax.experimental import pallas as pl
from jax.experimental.pallas import tpu as pltpu

F32 = jnp.float32
BF16 = jnp.bfloat16
MESH = pl.DeviceIdType.MESH

N_DEV = 8
D_MODEL = 1024
SEQ = 2048
EPS = 1e-6
A_WIDTH = 512
A_HEADS = 4
HEAD_A = 128
B_WIDTH = 512
B_GROUPS = 4
B_CHUNK = 128
C_HEADS = 16
C_HEAD_DIM = 64
C_ROT_HALF = 8
ROPE_THETA = 500000.0
C_DILATIONS = (1, 4, 16)
C_BLOCK = 128
D_FF = 4096
EVEN_IN = 3072
ODD_IN = 3072

ADAM_LR = 0.001
ADAM_B1 = 0.9
ADAM_B2 = 0.999
ADAM_EPS = 1e-08
ADAM_WD = 0.01
ADAM_STEP = 10

LANES = 128
SUBLANES = 8
ROW_TILE = 512
PROJ_TILE = 1024
MERGE_TILE = 256
SUB_CHUNK = 16
HGRN_BLOCK = 256
NEG = -1e30
VMEM_LIMIT = 56 * 1024 * 1024


def _params(sem):
    return pltpu.CompilerParams(dimension_semantics=sem, vmem_limit_bytes=VMEM_LIMIT)


def _dot(a, b):
    return jnp.dot(a, b, preferred_element_type=F32)


def _dot_nt(a, b):
    return lax.dot_general(a, b, (((1,), (1,)), ((), ())), preferred_element_type=F32)


def _dot_tn(a, b):
    return lax.dot_general(a, b, (((0,), (0,)), ((), ())), preferred_element_type=F32)


def _rms(x, g):
    r = lax.rsqrt(jnp.mean(x * x, axis=-1, keepdims=True) + EPS)
    return x * r * g


def _rms_bwd(x, g, dy):
    r = lax.rsqrt(jnp.mean(x * x, axis=-1, keepdims=True) + EPS)
    dyg = dy * g
    dx = r * dyg - x * (r * r * r) * jnp.mean(x * dyg, axis=-1, keepdims=True)
    return dx, dy * x * r


def _split3(x):
    hi = x.astype(BF16)
    rest = x - hi.astype(F32)
    mid = rest.astype(BF16)
    return hi, mid, (rest - mid.astype(F32)).astype(BF16)


def _mask_dot(mask, x):
    m = mask.astype(BF16)
    hi, mid, lo = _split3(x)
    return _dot(m, hi) + (_dot(m, mid) + _dot(m, lo))


def _dot_mask(x, mask):
    m = mask.astype(BF16)
    hi, mid, lo = _split3(x)
    return _dot(hi, m) + (_dot(mid, m) + _dot(lo, m))


def _rows8(v):
    return v.reshape(v.shape[0] // SUBLANES, SUBLANES, v.shape[1]).sum(axis=0)


def _sigmoid(x):
    return 1.0 / (1.0 + jnp.exp(-x))


def _gelu(x):
    return 0.5 * x * (1.0 + jnp.tanh(math.sqrt(2.0 / math.pi) * (x + 0.044715 * (x * x * x))))


def _acc_rows8(ref, val, first):
    @pl.when(first)
    def _():
        ref[...] = val

    @pl.when(jnp.logical_not(first))
    def _():
        ref[...] += val


def _my_slot():
    return 4 * lax.axis_index("x") + 2 * lax.axis_index("y") + lax.axis_index("c")


def _peer(r):
    x, y, c = lax.axis_index("x"), lax.axis_index("y"), lax.axis_index("c")
    px = 1 - x if (r >> 2) & 1 else x
    py = 1 - y if (r >> 1) & 1 else y
    pc = 1 - c if r & 1 else c
    return (px, py, pc), 4 * px + 2 * py + pc


HBM_SPEC = pl.BlockSpec(memory_space=pltpu.HBM)
SEM_SPEC = pl.BlockSpec(memory_space=pltpu.SEMAPHORE)
SPLIT_EFFECT = pltpu.SideEffectType.DATAFLOW_SIDE_EFFECTING


def _split_copies(land_ref, src_ref, send_sem, recv_sem):
    me = _my_slot()
    copies = []
    for r in range(1, N_DEV):
        peer, slot = _peer(r)
        src = _slot(land_ref, me) if src_ref is None else _slot(src_ref, slot)
        copies.append(pltpu.make_async_remote_copy(
            src_ref=src, dst_ref=_slot(land_ref, me), send_sem=send_sem, recv_sem=recv_sem,
            device_id=peer, device_id_type=MESH))
    return copies


def _slot(ref, s):
    if len(ref.shape) == 2:
        c = ref.shape[1] // N_DEV
        return ref.at[:, pl.ds(pl.multiple_of(s * c, LANES), c)]
    return ref.at[s]


def _exchange_start(lands, sources, name):
    n = len(lands)
    given = [s for s in sources if s is not None]
    arrays = list(lands) + given

    def body(*refs):
        land_refs, src_refs = refs[:n], list(refs[n:n + len(given)])
        sems = refs[len(arrays):len(arrays) + 2 * n]
        token = refs[-1]
        for k in range(n):
            src_ref = None if sources[k] is None else src_refs.pop(0)
            for copy in _split_copies(land_refs[k], src_ref, sems[k], sems[n + k]):
                copy.start()
        token[...] = jnp.zeros_like(token)

    outs = pl.pallas_call(
        body, name=name,
        out_shape=(pltpu.SemaphoreType.DMA(()),) * (2 * n) + tuple(pltpu.HBM(a.shape, a.dtype) for a in arrays)
        + (jax.ShapeDtypeStruct((SUBLANES, LANES), F32),),
        in_specs=[HBM_SPEC] * len(arrays),
        out_specs=(SEM_SPEC,) * (2 * n) + (HBM_SPEC,) * len(arrays) + (pl.BlockSpec(memory_space=pltpu.VMEM),),
        input_output_aliases={i: 2 * n + i for i in range(len(arrays))},
        compiler_params=pltpu.CompilerParams(has_side_effects=SPLIT_EFFECT),
    )(*[pltpu.with_memory_space_constraint(a, pltpu.HBM) for a in arrays])
    return list(outs[:n]), list(outs[n:2 * n]), list(outs[2 * n:3 * n]), list(outs[3 * n:-1]), outs[-1]


def _exchange_wait(lands, sources, send_sems, recv_sems, after, name):
    n = len(lands)
    given = [s for s in sources if s is not None]
    arrays = list(lands) + given

    def body(*refs):
        land_refs, src_refs = refs[:n], list(refs[n:n + len(given)])
        sems = refs[len(arrays):len(arrays) + 2 * n]
        for i in range(n):
            src_ref = None if sources[i] is None else src_refs.pop(0)
            copies = _split_copies(land_refs[i], src_ref, sems[i], sems[n + i])
            for copy in copies:
                copy.wait_recv()
            for copy in copies:
                copy.wait_send()

    outs = pl.pallas_call(
        body, name=name, out_shape=tuple(pltpu.HBM(a.shape, a.dtype) for a in arrays),
        in_specs=[HBM_SPEC] * len(arrays) + [SEM_SPEC] * (2 * n) + [pl.BlockSpec(memory_space=pl.ANY)],
        out_specs=(HBM_SPEC,) * len(arrays),
        input_output_aliases={i: i for i in range(len(arrays))},
        compiler_params=pltpu.CompilerParams(has_side_effects=SPLIT_EFFECT),
    )(*arrays, *send_sems, *recv_sems, after)
    return list(outs[:n])


def _place_own(a, me, name, own_block, dtype=BF16, after=None):
    shape = a.shape[1:] if own_block else a.shape
    cols = shape[-1]
    a3 = a.reshape((N_DEV if own_block else 1, -1, cols))
    rows = a3.shape[1]
    tr = min(rows, 512)

    def body(me_ref, a_ref, _, o_ref):
        o_ref[...] = a_ref[...].astype(dtype)

    grid_spec = pltpu.PrefetchScalarGridSpec(
        num_scalar_prefetch=1, grid=(rows // tr,),
        in_specs=[pl.BlockSpec((1, tr, cols), lambda i, me_ref: (me_ref[0] if own_block else 0, i, 0)),
                  pl.BlockSpec(memory_space=pl.ANY)],
        out_specs=pl.BlockSpec((1, tr, cols), lambda i, me_ref: (me_ref[0], i, 0)))
    out = pl.pallas_call(
        body, name=name, grid_spec=grid_spec, out_shape=jax.ShapeDtypeStruct((N_DEV, rows, cols), dtype),
        compiler_params=_params(("arbitrary",)),
    )(me, a3, a3 if after is None else after)
    return out.reshape((N_DEV,) + shape)


def _place_own_columns(a, me, name, after=None):
    rows, cols = a.shape
    tr = min(rows, 512)

    def body(me_ref, a_ref, _, o_ref):
        o_ref[...] = a_ref[...].astype(BF16)

    grid_spec = pltpu.PrefetchScalarGridSpec(
        num_scalar_prefetch=1, grid=(rows // tr,),
        in_specs=[pl.BlockSpec((tr, cols), lambda i, me_ref: (i, 0)), pl.BlockSpec(memory_space=pl.ANY)],
        out_specs=pl.BlockSpec((tr, cols), lambda i, me_ref: (i, me_ref[0])))
    return pl.pallas_call(
        body, name=name, grid_spec=grid_spec, out_shape=jax.ShapeDtypeStruct((rows, N_DEV * cols), BF16),
        compiler_params=_params(("arbitrary",)),
    )(me, a, a if after is None else after)


def _rope_tables(positions):
    in_head = jnp.arange(LANES) % C_HEAD_DIM
    inv = ROPE_THETA ** (-(in_head % C_ROT_HALF).astype(F32) / C_ROT_HALF)
    ang = positions.reshape(-1)[:, None].astype(F32) * inv
    rotated = in_head < 2 * C_ROT_HALF
    sin = jnp.sin(ang)
    return (jnp.where(rotated, jnp.cos(ang), 1.0),
            jnp.where(in_head < C_ROT_HALF, -sin, jnp.where(rotated, sin, 0.0)))


def _swap_halves(x):
    lane = lax.broadcasted_iota(jnp.int32, x.shape, 1) % C_HEAD_DIM
    return jnp.where(lane < C_ROT_HALF, pltpu.roll(x, LANES - C_ROT_HALF, 1), pltpu.roll(x, C_ROT_HALF, 1))


def _norm_inproj(x, g, w, name):
    t = x.shape[0]
    n = w.shape[1]
    tm, tn = ROW_TILE, n

    def body(x_ref, g_ref, w_ref, o_ref, h_ref):
        @pl.when(pl.program_id(1) == 0)
        def _():
            h_ref[...] = _rms(x_ref[...], g_ref[...]).astype(BF16)

        o_ref[...] = _dot(h_ref[...], w_ref[...])

    return pl.pallas_call(
        body, name=name, grid=(t // tm, n // tn),
        in_specs=[pl.BlockSpec((tm, D_MODEL), lambda i, j: (i, 0)), pl.BlockSpec((1, D_MODEL), lambda i, j: (0, 0)),
                  pl.BlockSpec((D_MODEL, tn), lambda i, j: (0, j))],
        out_specs=[pl.BlockSpec((tm, tn), lambda i, j: (i, j)), pl.BlockSpec((tm, D_MODEL), lambda i, j: (i, 0))],
        out_shape=[jax.ShapeDtypeStruct((t, n), F32), jax.ShapeDtypeStruct((t, D_MODEL), BF16)],
        compiler_params=_params(("parallel", "arbitrary")),
    )(x, g, w)


def _dilated_specs(tm, width, col_of):
    per_seq = SEQ // tm
    specs = []
    for d in C_DILATIONS:
        specs.append(pl.BlockSpec(
            (1, d, tm // d, width), lambda i, *rest: (i // per_seq, 0, i % per_seq, col_of(*rest))))
    return specs


def _dilated_shapes(n_seq, cols, dtype):
    return [jax.ShapeDtypeStruct((n_seq, d, SEQ // d, cols), dtype) for d in C_DILATIONS]


def _store_dilated(src_ref, out_refs, dtype):
    groups, tm, _ = src_ref.shape
    for d, o_ref in zip(C_DILATIONS, out_refs):
        for r in range(d):
            rows = pl.ds(r, tm // d, stride=d) if d > 1 else slice(None)
            for p in range(groups):
                o_ref[0, r, :, p * LANES:(p + 1) * LANES] = src_ref.at[p][rows, :].astype(dtype)


def _load_dilated(in_ref, d, dst_ref):
    groups, tm, _ = dst_ref.shape
    for r in range(d):
        rows = pl.ds(r, tm // d, stride=d)
        for p in range(groups):
            dst_ref.at[p][rows, :] = in_ref[0, r, :, p * LANES:(p + 1) * LANES].astype(F32)


def _norm_inproj_rope(x, g, w, rope, name):
    t = x.shape[0]
    n = w.shape[1]
    tm, nb = ROW_TILE, n

    def body(x_ref, g_ref, w_ref, c_ref, s_ref, o1_ref, o4_ref, o16_ref, h_ref, tile_ref):
        j = pl.program_id(1)

        @pl.when(j == 0)
        def _():
            h_ref[...] = _rms(x_ref[...], g_ref[...]).astype(BF16)

        acc = _dot(h_ref[...], w_ref[...])
        for p in range(nb // LANES):
            blk = acc[:, p * LANES:(p + 1) * LANES]
            roped = blk * c_ref[...] + _swap_halves(blk) * s_ref[...]
            piece = j * (nb // LANES) + p
            is_qk = piece < 2 * (D_MODEL // LANES)
            tile_ref[p] = jnp.where(is_qk, roped, blk) * jnp.where(piece < D_MODEL // LANES, QK_SCALE, 1.0)
        _store_dilated(tile_ref, (o1_ref, o4_ref, o16_ref), BF16)

    return pl.pallas_call(
        body, name=name, grid=(t // tm, n // nb),
        in_specs=[pl.BlockSpec((tm, D_MODEL), lambda i, j: (i, 0)), pl.BlockSpec((1, D_MODEL), lambda i, j: (0, 0)),
                  pl.BlockSpec((D_MODEL, nb), lambda i, j: (0, j)),
                  pl.BlockSpec((tm, LANES), lambda i, j: (i, 0)), pl.BlockSpec((tm, LANES), lambda i, j: (i, 0))],
        out_specs=_dilated_specs(tm, nb, lambda j: j) + [pl.BlockSpec((tm, D_MODEL), lambda i, j: (i, 0))],
        out_shape=_dilated_shapes(t // SEQ, n, BF16) + [jax.ShapeDtypeStruct((t, D_MODEL), BF16)],
        scratch_shapes=[pltpu.VMEM((nb // LANES, tm, LANES), F32)],
        compiler_params=_params(("parallel", "arbitrary")),
    )(x, g, w, *rope)


def _outproj(parts, w, x, g, name):
    t = x.shape[0]
    tm = PROJ_TILE
    n = len(parts)
    widths = [p.shape[1] for p in parts]

    def body(*refs):
        p_refs = refs[:n]
        w_ref, x_ref, g_ref, xo_ref, mix_ref = refs[n:]
        mix = None
        off = 0
        for p_ref, wd in zip(p_refs, widths):
            term = _dot(p_ref[...].astype(BF16), w_ref[off:off + wd, :])
            mix = term if mix is None else mix + term
            off += wd
        mix_ref[...] = mix
        xo_ref[...] = x_ref[...] + _rms(mix, g_ref[...])

    row = lambda i: (i, 0)
    return pl.pallas_call(
        body, name=name, grid=(t // tm,),
        in_specs=[pl.BlockSpec((tm, wd), row) for wd in widths] + [
            pl.BlockSpec((sum(widths), D_MODEL), lambda i: (0, 0)),
            pl.BlockSpec((tm, D_MODEL), row), pl.BlockSpec((1, D_MODEL), lambda i: (0, 0))],
        out_specs=[pl.BlockSpec((tm, D_MODEL), row)] * 2,
        out_shape=[jax.ShapeDtypeStruct((t, D_MODEL), F32)] * 2,
        compiler_params=_params(("parallel",)),
    )(*parts, w, x, g)


def _outproj_bwd(dx, mix, g, w, name):
    t = dx.shape[0]
    tm = PROJ_TILE
    k = w.shape[0]

    def body(dx_ref, mix_ref, g_ref, w_ref, dcat_ref, dz_ref, dg_ref):
        dz, dgr = _rms_bwd(mix_ref[...], g_ref[...], dx_ref[...])
        dzb = dz.astype(BF16)
        dz_ref[...] = dzb
        dcat_ref[...] = _dot_nt(dzb, w_ref[...])
        _acc_rows8(dg_ref, _rows8(dgr), pl.program_id(0) == 0)

    row = lambda i: (i, 0)
    return pl.pallas_call(
        body, name=name, grid=(t // tm,),
        in_specs=[pl.BlockSpec((tm, D_MODEL), row), pl.BlockSpec((tm, D_MODEL), row),
                  pl.BlockSpec((1, D_MODEL), lambda i: (0, 0)), pl.BlockSpec((k, D_MODEL), lambda i: (0, 0))],
        out_specs=[pl.BlockSpec((tm, k), row), pl.BlockSpec((tm, D_MODEL), row),
                   pl.BlockSpec((SUBLANES, D_MODEL), lambda i: (0, 0))],
        out_shape=[jax.ShapeDtypeStruct((t, k), F32), jax.ShapeDtypeStruct((t, D_MODEL), BF16),
                   jax.ShapeDtypeStruct((SUBLANES, D_MODEL), F32)],
        compiler_params=_params(("arbitrary",)),
    )(dx, mix, g, w)


def _outproj_bwd_attn(dx, mix, g, w, out, name):
    t = dx.shape[0]
    tm = MERGE_TILE

    def body(dx_ref, mix_ref, g_ref, w_ref, out_ref, do1, do4, do16, dl_ref, dz_ref, dg_ref, tile_ref):
        dz, dgr = _rms_bwd(mix_ref[...], g_ref[...], dx_ref[...])
        dzb = dz.astype(BF16)
        dz_ref[...] = dzb
        _acc_rows8(dg_ref, _rows8(dgr), pl.program_id(0) == 0)
        dout = _dot_nt(dzb, w_ref[...])
        for p in range(LANE_GROUPS):
            tile_ref[p] = dout[:, p * LANES:(p + 1) * LANES]
        _store_dilated(tile_ref, (do1, do4, do16), BF16)
        column = lax.broadcasted_iota(jnp.int32, (D_MODEL, LANES), 0) // C_HEAD_DIM
        head = lax.broadcasted_iota(jnp.int32, (D_MODEL, LANES), 1)
        dl_ref[...] = _dot_mask(dout * out_ref[...], column == head)

    row = lambda i: (i, 0)
    n_seq = t // SEQ
    return pl.pallas_call(
        body, name=name, grid=(t // tm,),
        in_specs=[pl.BlockSpec((tm, D_MODEL), row), pl.BlockSpec((tm, D_MODEL), row),
                  pl.BlockSpec((1, D_MODEL), lambda i: (0, 0)), pl.BlockSpec((D_MODEL, D_MODEL), lambda i: (0, 0)),
                  pl.BlockSpec((tm, D_MODEL), row)],
        out_specs=_dilated_specs(tm, D_MODEL, lambda: 0) + [
            pl.BlockSpec((tm, LANES), row), pl.BlockSpec((tm, D_MODEL), row),
            pl.BlockSpec((SUBLANES, D_MODEL), lambda i: (0, 0))],
        out_shape=_dilated_shapes(n_seq, D_MODEL, BF16) + [
            jax.ShapeDtypeStruct((t, LANES), F32), jax.ShapeDtypeStruct((t, D_MODEL), BF16),
            jax.ShapeDtypeStruct((SUBLANES, D_MODEL), F32)],
        scratch_shapes=[pltpu.VMEM((LANE_GROUPS, tm, LANES), F32)],
        compiler_params=_params(("arbitrary",)),
    )(dx, mix, g, w, out)


def _inproj_bwd(dproj, w, dx, x, g, name):
    t = x.shape[0]
    n = w.shape[1]
    tm = ROW_TILE

    def body(dp_ref, w_ref, dx_ref, x_ref, g_ref, o_ref, dg_ref):
        dxn, dgr = _rms_bwd(x_ref[...], g_ref[...], _dot_nt(dp_ref[...], w_ref[...]))
        o_ref[...] = dx_ref[...] + dxn
        _acc_rows8(dg_ref, _rows8(dgr), pl.program_id(0) == 0)

    row = lambda i: (i, 0)
    return pl.pallas_call(
        body, name=name, grid=(t // tm,),
        in_specs=[pl.BlockSpec((tm, n), row), pl.BlockSpec((D_MODEL, n), lambda i: (0, 0)),
                  pl.BlockSpec((tm, D_MODEL), row), pl.BlockSpec((tm, D_MODEL), row),
                  pl.BlockSpec((1, D_MODEL), lambda i: (0, 0))],
        out_specs=[pl.BlockSpec((tm, D_MODEL), row), pl.BlockSpec((SUBLANES, D_MODEL), lambda i: (0, 0))],
        out_shape=[jax.ShapeDtypeStruct((t, D_MODEL), F32), jax.ShapeDtypeStruct((SUBLANES, D_MODEL), F32)],
        compiler_params=_params(("arbitrary",)),
    )(dproj, w, dx, x, g)


def _grad_w(a, b, col_blocks, name):
    a_planes, b_planes = a.ndim == 3, b.ndim == 3
    t = a.shape[-2]
    k = a.shape[0] * a.shape[2] if a_planes else a.shape[1]
    n = b.shape[0] * b.shape[2] if b_planes else b.shape[1]
    tk = a.shape[2] if a_planes else min(k, 1024)
    per_owner = n // N_DEV
    tn = 2 * per_owner if col_blocks else min(n, 1024)
    assert not b_planes or b.shape[2] == tn

    def body(a_ref, b_ref, o_ref, at_ref):
        @pl.when(pl.program_id(1) == 0)
        def _():
            for c in range(t // ROW_TILE):
                rows = slice(c * ROW_TILE, (c + 1) * ROW_TILE)
                at_ref[:, rows] = (a_ref[0, rows, :] if a_planes else a_ref[rows, :]).T

        res = _dot(at_ref[...], b_ref[0] if b_planes else b_ref[...]).astype(BF16)
        if col_blocks:
            o_ref[0] = res[:, :per_owner]
            o_ref[1] = res[:, per_owner:]
        else:
            o_ref[...] = res

    if col_blocks:
        out_spec = pl.BlockSpec((2, tk, per_owner), lambda i, j: (j, i, 0))
        out_shape = jax.ShapeDtypeStruct((N_DEV, k, per_owner), BF16)
    else:
        out_spec = pl.BlockSpec((tk, tn), lambda i, j: (i, j))
        out_shape = jax.ShapeDtypeStruct((k, n), BF16)
    a_spec = pl.BlockSpec((1, t, tk), lambda i, j: (i, 0, 0)) if a_planes else pl.BlockSpec((t, tk), lambda i, j: (0, i))
    b_spec = pl.BlockSpec((1, t, tn), lambda i, j: (j, 0, 0)) if b_planes else pl.BlockSpec((t, tn), lambda i, j: (0, j))
    return pl.pallas_call(
        body, name=name, grid=(k // tk, n // tn),
        in_specs=[a_spec, b_spec], out_specs=out_spec, out_shape=out_shape,
        scratch_shapes=[pltpu.VMEM((tk, t), BF16)],
        compiler_params=_params(("parallel", "arbitrary")),
    )(a, b)


FF_STEP = 1024
FF_STEPS = D_FF // FF_STEP


def _ffn_fwd(x, g_pre, w1, w2, g_post, name, target=None):
    t = x.shape[0]
    tm = PROJ_TILE

    tiles = t // tm

    def body(*refs):
        *refs, x_ref, sem = refs
        if target is None:
            x_hbm, gp_ref, w1_ref, w2_ref, gq_ref, xo_ref, y_ref, h_ref, r_ref = refs
        else:
            x_hbm, gp_ref, w1_ref, w2_ref, gq_ref, t_ref, xo_ref, y_ref, h_ref, r_ref, l_ref = refs
        i, j = pl.program_id(0), pl.program_id(1)
        slot = i % 2

        def fetch(tile, into):
            rows = pl.ds(tile * tm if isinstance(tile, int) else pl.multiple_of(tile * tm, tm), tm)
            return pltpu.make_async_copy(x_hbm.at[rows, :], x_ref.at[into], sem.at[into])

        @pl.when(j == 0)
        def _():
            @pl.when(i == 0)
            def _():
                fetch(0, 0).start()

            @pl.when(i + 1 < tiles)
            def _():
                fetch(i + 1, (i + 1) % 2).start()

            fetch(i, slot).wait()
            h_ref[...] = _rms(x_ref[slot], gp_ref[...]).astype(BF16)

        a = _dot(h_ref[...], w1_ref[...])
        r = jnp.square(jnp.maximum(a, 0.0)).astype(BF16)
        r_ref[0] = r
        term = _dot(r, w2_ref[...])

        @pl.when(j == 0)
        def _():
            y_ref[...] = term

        @pl.when(j > 0)
        def _():
            y_ref[...] += term

        @pl.when(j == FF_STEPS - 1)
        def _():
            x_new = x_ref[slot] + _rms(y_ref[...], gq_ref[...])
            if target is None:
                xo_ref[...] = x_new
            else:
                diff = x_new - t_ref[...]
                xo_ref[...] = diff * (1.0 / D_MODEL)
                _acc_rows8(l_ref, _rows8(diff * diff) * (0.5 / D_MODEL), i == 0)

    row = lambda i, j: (i, 0)
    vec = pl.BlockSpec((1, D_MODEL), lambda i, j: (0, 0))
    in_specs = [pl.BlockSpec(memory_space=pl.ANY), vec, pl.BlockSpec((D_MODEL, FF_STEP), lambda i, j: (0, j)),
                pl.BlockSpec((FF_STEP, D_MODEL), lambda i, j: (j, 0)), vec]
    out_specs = [pl.BlockSpec((tm, D_MODEL), row)] * 3 + [pl.BlockSpec((1, tm, FF_STEP), lambda i, j: (j, i, 0))]
    out_shape = [jax.ShapeDtypeStruct((t, D_MODEL), F32), jax.ShapeDtypeStruct((t, D_MODEL), F32),
                 jax.ShapeDtypeStruct((t, D_MODEL), BF16), jax.ShapeDtypeStruct((FF_STEPS, t, FF_STEP), BF16)]
    args = [x, g_pre, w1, w2, g_post]
    if target is not None:
        in_specs.append(pl.BlockSpec((tm, D_MODEL), row))
        out_specs.append(pl.BlockSpec((SUBLANES, D_MODEL), lambda i, j: (0, 0)))
        out_shape.append(jax.ShapeDtypeStruct((SUBLANES, D_MODEL), F32))
        args.append(target)
    return pl.pallas_call(
        body, name=name, grid=(tiles, FF_STEPS), in_specs=in_specs, out_specs=out_specs, out_shape=out_shape,
        scratch_shapes=[pltpu.VMEM((2, tm, D_MODEL), F32), pltpu.SemaphoreType.DMA((2,))],
        compiler_params=_params(("arbitrary", "arbitrary")),
    )(*args)


def _ffn_bwd(dxo, x, y, r, g_pre, w1, w2, g_post, name):
    t = x.shape[0]
    tm = ROW_TILE

    def body(dxo_ref, x_ref, y_ref, r_ref, gp_ref, w1_ref, w2_ref, gq_ref,
             dx_ref, dy_ref, da_ref, dgp_ref, dgq_ref, acc_ref):
        i, j = pl.program_id(0), pl.program_id(1)

        @pl.when(j == 0)
        def _():
            dy, dgr = _rms_bwd(y_ref[...], gq_ref[...], dxo_ref[...])
            dy_ref[...] = dy.astype(BF16)
            _acc_rows8(dgq_ref, _rows8(dgr), i == 0)

        dr = _dot_nt(dy_ref[...], w2_ref[...])
        da = (dr * (2.0 * jnp.sqrt(r_ref[0].astype(F32)))).astype(BF16)
        da_ref[0] = da
        term = _dot_nt(da, w1_ref[...])

        @pl.when(j == 0)
        def _():
            acc_ref[...] = term

        @pl.when(j > 0)
        def _():
            acc_ref[...] += term

        @pl.when(j == FF_STEPS - 1)
        def _():
            dxn, dgr = _rms_bwd(x_ref[...], gp_ref[...], acc_ref[...])
            dx_ref[...] = dxo_ref[...] + dxn
            _acc_rows8(dgp_ref, _rows8(dgr), i == 0)

    row = lambda i, j: (i, 0)
    vec = pl.BlockSpec((1, D_MODEL), lambda i, j: (0, 0))
    acc8 = pl.BlockSpec((SUBLANES, D_MODEL), lambda i, j: (0, 0))
    return pl.pallas_call(
        body, name=name, grid=(t // tm, FF_STEPS),
        in_specs=[pl.BlockSpec((tm, D_MODEL), row)] * 3 + [
            pl.BlockSpec((1, tm, FF_STEP), lambda i, j: (j, i, 0)),
            vec, pl.BlockSpec((D_MODEL, FF_STEP), lambda i, j: (0, j)),
            pl.BlockSpec((FF_STEP, D_MODEL), lambda i, j: (j, 0)), vec],
        out_specs=[pl.BlockSpec((tm, D_MODEL), row), pl.BlockSpec((tm, D_MODEL), row),
                   pl.BlockSpec((1, tm, FF_STEP), lambda i, j: (j, i, 0)), acc8, acc8],
        out_shape=[jax.ShapeDtypeStruct((t, D_MODEL), F32), jax.ShapeDtypeStruct((t, D_MODEL), BF16),
                   jax.ShapeDtypeStruct((FF_STEPS, t, FF_STEP), BF16),
                   jax.ShapeDtypeStruct((SUBLANES, D_MODEL), F32), jax.ShapeDtypeStruct((SUBLANES, D_MODEL), F32)],
        scratch_shapes=[pltpu.VMEM((tm, D_MODEL), F32)],
        compiler_params=_params(("arbitrary", "arbitrary")),
    )(dxo, x, y, r, g_pre, w1, w2, g_post)


def _lower_bound(table):
    e = jnp.exp(table - jnp.max(table, axis=0, keepdims=True))
    return e[0:1, :] / jnp.sum(e, axis=0, keepdims=True)


def _hgrn2_block(q_ref, f_ref, lb):
    tb = f_ref.shape[0]
    sig = _sigmoid(f_ref[...])
    f = lb + (1.0 - lb) * sig
    qraw = q_ref[...]
    sq = _sigmoid(qraw)
    r = lax.broadcasted_iota(jnp.int32, (tb, tb), 0)
    c = lax.broadcasted_iota(jnp.int32, (tb, tb), 1)
    same = (r // SUB_CHUNK) == (c // SUB_CHUNK)
    logf = jnp.log(f)
    gsum = _mask_dot(same & (c <= r), logf)
    glast = _mask_dot(same, logf)
    return dict(sig=sig, f=f, kk=1.0 - f, qraw=qraw, sq=sq, qs=qraw * sq, gsum=gsum,
                eg=jnp.exp(gsum), ekd=jnp.exp(glast - gsum), a=jnp.exp(glast))


def _head_sums(x):
    parts = [jnp.broadcast_to(jnp.sum(x[:, h * HEAD_A:(h + 1) * HEAD_A], axis=1, keepdims=True), (x.shape[0], HEAD_A))
             for h in range(A_HEADS)]
    return jnp.concatenate(parts, axis=1)


def _hgrn2_intra(g, kk, qs, v):
    row = lax.broadcasted_iota(jnp.int32, g.shape, 0)
    o = _head_sums(qs * kk) * v
    for j in range(1, SUB_CHUNK):
        decay = jnp.exp(jnp.where(row >= j, g - pltpu.roll(g, j, 0), NEG))
        o = o + _head_sums(qs * pltpu.roll(kk, j, 0) * decay) * pltpu.roll(v, j, 0)
    return o


def _hgrn2_intra_bwd(g, kk, qs, v, do):
    row = lax.broadcasted_iota(jnp.int32, g.shape, 0)
    dsc = _head_sums(do * v)
    dqs, dkk, dv = dsc * kk, dsc * qs, _head_sums(qs * kk) * do
    for j in range(1, SUB_CHUNK):
        k_dn = pltpu.roll(kk, j, 0)
        decay = jnp.exp(jnp.where(row >= j, g - pltpu.roll(g, j, 0), NEG))
        d_score = _head_sums(do * pltpu.roll(v, j, 0)) * decay
        dqs = dqs + d_score * k_dn
        dkk = dkk + pltpu.roll(d_score * qs, SUB_CHUNK - j, 0)
        dv = dv + pltpu.roll(_head_sums(qs * k_dn * decay) * do, SUB_CHUNK - j, 0)
    return dqs, dkk, dv


def _hgrn2_fwd(proj, lb_table, a_norm, name):
    t = proj.shape[0]
    tb = HGRN_BLOCK
    n_tb = SEQ // tb
    n_seq = t // SEQ
    n_sub = tb // SUB_CHUNK

    def body(q_ref, f_ref, i_ref, g_ref, lbt_ref, an_ref, o_ref, pre_ref, sts_ref, st_ref,
             gs_ref, kk_ref, qs_ref, eg_ref, ekd_ref, a_ref):
        @pl.when(pl.program_id(1) == 0)
        def _():
            st_ref[...] = jnp.zeros_like(st_ref)

        an = an_ref[...]
        blk = _hgrn2_block(q_ref, f_ref, _lower_bound(lbt_ref[...]))
        for ref, key in ((gs_ref, "gsum"), (kk_ref, "kk"), (qs_ref, "qs"), (eg_ref, "eg"), (ekd_ref, "ekd"), (a_ref, "a")):
            ref[...] = blk[key]

        def step(c, carry):
            rows = pl.ds(pl.multiple_of(c * SUB_CHUNK, SUB_CHUNK), SUB_CHUNK)
            kk, qs, v = kk_ref[rows, :], qs_ref[rows, :], i_ref[rows, :]
            o = _hgrn2_intra(gs_ref[rows, :], kk, qs, v)
            qg, kd, vb = (qs * eg_ref[rows, :]).astype(BF16), (kk * ekd_ref[rows, :]).astype(BF16), v.astype(BF16)
            for h in range(A_HEADS):
                lanes = slice(h * HEAD_A, (h + 1) * HEAD_A)
                st = st_ref[h]
                sts_ref[0, c, h] = st
                o_h = o[:, lanes] + _dot_nt(qg[:, lanes], st.astype(BF16))
                st_ref[h] = st * a_ref[rows, lanes][0:1] + _dot_tn(vb[:, lanes], kd[:, lanes])
                pre_ref[rows, lanes] = o_h
                graw = g_ref[rows, lanes]
                o_ref[rows, lanes] = (_rms(o_h, an[:, lanes]) * (graw * _sigmoid(graw))).astype(BF16)
            return carry

        lax.fori_loop(0, n_sub, step, 0, unroll=2)

    def col(k):
        return pl.BlockSpec((tb, A_WIDTH), lambda b, s, k=k: (b * n_tb + s, k))

    out_rows = pl.BlockSpec((tb, A_WIDTH), lambda b, s: (b * n_tb + s, 0))
    return pl.pallas_call(
        body, name=name, grid=(n_seq, n_tb),
        in_specs=[col(0), col(1), col(2), col(3),
                  pl.BlockSpec((3, A_WIDTH), lambda b, s: (0, 0)), pl.BlockSpec((1, A_WIDTH), lambda b, s: (0, 0))],
        out_specs=[out_rows, out_rows,
                   pl.BlockSpec((1, n_sub, A_HEADS, HEAD_A, HEAD_A), lambda b, s: (b * n_tb + s, 0, 0, 0, 0))],
        out_shape=[jax.ShapeDtypeStruct((t, D_MODEL), BF16), jax.ShapeDtypeStruct((t, A_WIDTH), F32),
                   jax.ShapeDtypeStruct((n_seq * n_tb, n_sub, A_HEADS, HEAD_A, HEAD_A), F32)],
        scratch_shapes=[pltpu.VMEM((A_HEADS, HEAD_A, HEAD_A), F32)] + [pltpu.VMEM((tb, A_WIDTH), F32)] * 6,
        compiler_params=_params(("parallel", "arbitrary")),
    )(proj, proj, proj, proj, lb_table, a_norm)


def _hgrn2_bwd(proj, dcat, pre, states, lb_table, a_norm, name):
    t = proj.shape[0]
    tb = HGRN_BLOCK
    n_tb = SEQ // tb
    n_seq = t // SEQ
    n_sub = tb // SUB_CHUNK

    def body(q_ref, f_ref, i_ref, g_ref, do_ref, pre_ref, sts_ref, lbt_ref, an_ref, dp_ref, dlb_ref, dan_ref, dst_ref,
             gs_ref, kk_ref, qs_ref, eg_ref, ekd_ref, a_ref, dpre_ref, dlf_ref, dqs_ref, dkk_ref):
        b, s = pl.program_id(0), pl.program_id(1)

        @pl.when(s == 0)
        def _():
            dst_ref[...] = jnp.zeros_like(dst_ref)

        @pl.when((b == 0) & (s == 0))
        def _():
            dlb_ref[...] = jnp.zeros_like(dlb_ref)
            dan_ref[...] = jnp.zeros_like(dan_ref)

        lb = _lower_bound(lbt_ref[...])
        an = an_ref[...]
        heads = [slice(h * HEAD_A, (h + 1) * HEAD_A) for h in range(A_HEADS)]
        blk = _hgrn2_block(q_ref, f_ref, lb)
        for ref, key in ((gs_ref, "gsum"), (kk_ref, "kk"), (qs_ref, "qs"), (eg_ref, "eg"), (ekd_ref, "ekd"), (a_ref, "a")):
            ref[...] = blk[key]
        for h, lanes in enumerate(heads):
            graw, o = g_ref[:, lanes], pre_ref[:, lanes]
            sg = _sigmoid(graw)
            dout = do_ref[:, lanes]
            d_o, dgr = _rms_bwd(o, an[:, lanes], dout * (graw * sg))
            dan_ref[0:1, lanes] += jnp.sum(dgr, axis=0, keepdims=True)
            dp_ref[:, 3 * A_WIDTH + h * HEAD_A:3 * A_WIDTH + (h + 1) * HEAD_A] = (
                dout * _rms(o, an[:, lanes]) * (sg * (1.0 + graw * (1.0 - sg)))).astype(BF16)
            dpre_ref[:, lanes] = d_o

        tri_t = (lax.broadcasted_iota(jnp.int32, (SUB_CHUNK, SUB_CHUNK), 0)
                 <= lax.broadcasted_iota(jnp.int32, (SUB_CHUNK, SUB_CHUNK), 1)).astype(F32)

        def back(k, carry):
            c = n_sub - 1 - k
            rows = pl.ds(pl.multiple_of(c * SUB_CHUNK, SUB_CHUNK), SUB_CHUNK)
            g, kk, qs, v, d_o = gs_ref[rows, :], kk_ref[rows, :], qs_ref[rows, :], i_ref[rows, :], dpre_ref[rows, :]
            eg, ekd, a = eg_ref[rows, :], ekd_ref[rows, :], a_ref[rows, :]
            dqs, dkk, dv = _hgrn2_intra_bwd(g, kk, qs, v, d_o)
            qg_f, kd_f = qs * eg, kk * ekd
            qg, kd, vb, dob = qg_f.astype(BF16), kd_f.astype(BF16), v.astype(BF16), d_o.astype(BF16)
            dqg, dkd, da, dv_st = [], [], [], []
            for h, lanes in enumerate(heads):
                st, dst = sts_ref[0, c, h], dst_ref[h]
                dstb = dst.astype(BF16)
                dqg.append(_dot(dob[:, lanes], st.astype(BF16)))
                dv_st.append(_dot_nt(kd[:, lanes], dstb))
                dkd.append(_dot(vb[:, lanes], dstb))
                da.append(jnp.broadcast_to(jnp.sum(dst * st, axis=0, keepdims=True), (SUB_CHUNK, HEAD_A)))
                dst_ref[h] = dst * a[0:1, lanes] + _dot_tn(dob[:, lanes], qg[:, lanes])
            dqg, dkd, da, dv_st = [jnp.concatenate(p, axis=1) for p in (dqg, dkd, da, dv_st)]
            d_gsum = qs * dqs - kk * dkk + dqg * qg_f - dkd * kd_f
            d_glast = jnp.sum(dkd * kd_f, axis=0, keepdims=True) + da * a
            dlf_ref[rows, :] = jnp.dot(tri_t, d_gsum, precision=lax.Precision.HIGHEST,
                                       preferred_element_type=F32) + d_glast
            dqs_ref[rows, :] = dqs + dqg * eg
            dkk_ref[rows, :] = dkk + dkd * ekd
            dp_ref[rows, 2 * A_WIDTH:3 * A_WIDTH] = (dv + dv_st).astype(BF16)
            return carry

        lax.fori_loop(0, n_sub, back, 0, unroll=2)
        sig, sq, qraw = blk["sig"], blk["sq"], blk["qraw"]
        d_f = dlf_ref[...] / blk["f"] - dkk_ref[...]
        dlb_ref[0:1, :] += jnp.sum(d_f * (1.0 - sig), axis=0, keepdims=True)
        dp_ref[:, 0:A_WIDTH] = (dqs_ref[...] * (sq * (1.0 + qraw * (1.0 - sq)))).astype(BF16)
        dp_ref[:, A_WIDTH:2 * A_WIDTH] = (d_f * (1.0 - lb) * sig * (1.0 - sig)).astype(BF16)

    def rev(s):
        return n_tb - 1 - s

    def col(k):
        return pl.BlockSpec((tb, A_WIDTH), lambda b, s, k=k: (b * n_tb + rev(s), k))

    acc8 = pl.BlockSpec((SUBLANES, A_WIDTH), lambda b, s: (0, 0))
    return pl.pallas_call(
        body, name=name, grid=(n_seq, n_tb),
        in_specs=[col(0), col(1), col(2), col(3), col(0), col(0),
                  pl.BlockSpec((1, n_sub, A_HEADS, HEAD_A, HEAD_A), lambda b, s: (b * n_tb + rev(s), 0, 0, 0, 0)),
                  pl.BlockSpec((3, A_WIDTH), lambda b, s: (0, 0)), pl.BlockSpec((1, A_WIDTH), lambda b, s: (0, 0))],
        out_specs=[pl.BlockSpec((tb, 4 * A_WIDTH), lambda b, s: (b * n_tb + rev(s), 0)), acc8, acc8],
        out_shape=[jax.ShapeDtypeStruct((t, EVEN_IN), BF16)] + [jax.ShapeDtypeStruct((SUBLANES, A_WIDTH), F32)] * 2,
        scratch_shapes=[pltpu.VMEM((A_HEADS, HEAD_A, HEAD_A), F32)] + [pltpu.VMEM((tb, A_WIDTH), F32)] * 10,
        compiler_params=_params(("arbitrary", "arbitrary")),
    )(proj, proj, proj, proj, dcat, pre, states, lb_table, a_norm)


GMLP_ROWS = 512


def _gmlp_chunk(ub, vb, ln_g, ln_b, ws, bias):
    u = [_gelu(a) for a in ub]
    v = [_gelu(a) for a in vb]
    mu = sum(jnp.sum(a, axis=-1, keepdims=True) for a in v) * (1.0 / B_WIDTH)
    cen = [a - mu for a in v]
    var = sum(jnp.sum(a * a, axis=-1, keepdims=True) for a in cen) * (1.0 / B_WIDTH)
    inv = lax.rsqrt(var + EPS)
    r = lax.broadcasted_iota(jnp.int32, (B_CHUNK, B_CHUNK), 0)
    c = lax.broadcasted_iota(jnp.int32, (B_CHUNK, B_CHUNK), 1)
    outs = []
    for g in range(B_GROUPS):
        vn = (cen[g] * inv * ln_g[g] + ln_b[g]).astype(BF16)
        wm = jnp.where(c <= r, ws[g], 0.0).astype(BF16)
        outs.append(u[g] * (_dot(wm, vn) + bias[g]))
    return outs


def _lane_groups(ref, rows=slice(None)):
    return [ref[rows, g * LANES:(g + 1) * LANES] for g in range(B_GROUPS)]


def _gmlp_fwd(proj, mixed, ln_g, ln_b, ws, bias_t, name):
    t = proj.shape[0]
    tm = GMLP_ROWS

    def body(u_ref, v_ref, lg_ref, lb_ref, ws_ref, bt_ref, _, o_ref):
        for ch in range(tm // B_CHUNK):
            rows = slice(ch * B_CHUNK, (ch + 1) * B_CHUNK)
            outs = _gmlp_chunk(_lane_groups(u_ref, rows), _lane_groups(v_ref, rows), _lane_groups(lg_ref),
                               _lane_groups(lb_ref), [ws_ref[g] for g in range(B_GROUPS)],
                               [bt_ref[:, g:g + 1] for g in range(B_GROUPS)])
            for g in range(B_GROUPS):
                o_ref[rows, g * LANES:(g + 1) * LANES] = outs[g].astype(BF16)

    vec = pl.BlockSpec((1, B_WIDTH), lambda i: (0, 0))
    return pl.pallas_call(
        body, name=name, grid=(t // tm,),
        in_specs=[pl.BlockSpec((tm, B_WIDTH), lambda i: (i, 4)), pl.BlockSpec((tm, B_WIDTH), lambda i: (i, 5)), vec, vec,
                  pl.BlockSpec((B_GROUPS, B_CHUNK, B_CHUNK), lambda i: (0, 0, 0)),
                  pl.BlockSpec((B_CHUNK, B_GROUPS), lambda i: (0, 0)), pl.BlockSpec(memory_space=pl.ANY)],
        out_specs=pl.BlockSpec((tm, B_WIDTH), lambda i: (i, 1)),
        out_shape=jax.ShapeDtypeStruct(mixed.shape, BF16),
        input_output_aliases={6: 0},
        compiler_params=_params(("parallel",)),
    )(proj, proj, ln_g, ln_b, ws, bias_t, mixed)


def _gmlp_bwd(proj, dcat, dproj, ln_g, ln_b, ws, bias_t, name):
    t = proj.shape[0]
    tm = GMLP_ROWS

    def body(u_ref, v_ref, do_ref, lg_ref, lb_ref, ws_ref, bt_ref, _, duv_ref, dlg_ref, dlb_ref, dws_ref, dbt_ref):
        @pl.when(pl.program_id(0) == 0)
        def _():
            dlg_ref[...] = jnp.zeros_like(dlg_ref)
            dlb_ref[...] = jnp.zeros_like(dlb_ref)
            dws_ref[...] = jnp.zeros_like(dws_ref)
            dbt_ref[...] = jnp.zeros_like(dbt_ref)

        for ch in range(tm // B_CHUNK):
            rows = slice(ch * B_CHUNK, (ch + 1) * B_CHUNK)
            _, vjp = jax.vjp(
                _gmlp_chunk, _lane_groups(u_ref, rows), _lane_groups(v_ref, rows), _lane_groups(lg_ref),
                _lane_groups(lb_ref), [ws_ref[g] for g in range(B_GROUPS)],
                [bt_ref[:, g:g + 1] for g in range(B_GROUPS)])
            du, dv, dlg, dlb, dw, dbt = vjp(_lane_groups(do_ref, rows))
            for g in range(B_GROUPS):
                lanes = slice(g * LANES, (g + 1) * LANES)
                duv_ref[rows, lanes] = du[g].astype(BF16)
                duv_ref[rows, B_WIDTH + g * LANES:B_WIDTH + (g + 1) * LANES] = dv[g].astype(BF16)
                dlg_ref[0:1, lanes] += dlg[g]
                dlb_ref[0:1, lanes] += dlb[g]
                dws_ref[g] += dw[g]
                dbt_ref[:, g:g + 1] += dbt[g]

    vec = pl.BlockSpec((1, B_WIDTH), lambda i: (0, 0))
    acc8 = pl.BlockSpec((SUBLANES, B_WIDTH), lambda i: (0, 0))
    ws_spec = pl.BlockSpec((B_GROUPS, B_CHUNK, B_CHUNK), lambda i: (0, 0, 0))
    bt_spec = pl.BlockSpec((B_CHUNK, B_GROUPS), lambda i: (0, 0))
    return pl.pallas_call(
        body, name=name, grid=(t // tm,),
        in_specs=[pl.BlockSpec((tm, B_WIDTH), lambda i: (i, 4)), pl.BlockSpec((tm, B_WIDTH), lambda i: (i, 5)),
                  pl.BlockSpec((tm, B_WIDTH), lambda i: (i, 1)), vec, vec, ws_spec, bt_spec,
                  pl.BlockSpec(memory_space=pl.ANY)],
        out_specs=[pl.BlockSpec((tm, 2 * B_WIDTH), lambda i: (i, 2)), acc8, acc8, ws_spec, bt_spec],
        out_shape=[jax.ShapeDtypeStruct(dproj.shape, BF16), jax.ShapeDtypeStruct((SUBLANES, B_WIDTH), F32),
                   jax.ShapeDtypeStruct((SUBLANES, B_WIDTH), F32),
                   jax.ShapeDtypeStruct((B_GROUPS, B_CHUNK, B_CHUNK), F32),
                   jax.ShapeDtypeStruct((B_CHUNK, B_GROUPS), F32)],
        input_output_aliases={7: 0},
        compiler_params=_params(("arbitrary",)),
    )(proj, proj, dcat, ln_g, ln_b, ws, bias_t, dproj)


QK_SCALE = 1.0 / math.sqrt(C_HEAD_DIM)
ATTN_UNROLL = 16
ATTN_PAIRS = 2
LANE_GROUPS = D_MODEL // LANES
ATTN_STEPS = LANE_GROUPS // ATTN_PAIRS
Q_BLOCKS = SEQ // C_BLOCK


def _attn_window(i, d):
    sub_blocks = Q_BLOCKS // d
    q0 = pl.multiple_of(i * C_BLOCK, C_BLOCK)
    k0 = pl.multiple_of(jnp.maximum(i - 1, 0) * C_BLOCK, C_BLOCK)
    key = k0 + lax.broadcasted_iota(jnp.int32, (C_BLOCK, 2 * C_BLOCK), 1)
    dist = (q0 + lax.broadcasted_iota(jnp.int32, (C_BLOCK, 2 * C_BLOCK), 0)) - key
    own_subsequence = (key >= q0) | (i % sub_blocks > 0)
    return pl.ds(q0, C_BLOCK), pl.ds(k0, 2 * C_BLOCK), (dist >= 0) & (dist <= C_BLOCK) & own_subsequence


def _head_masks():
    lane = lax.broadcasted_iota(jnp.int32, (C_BLOCK, LANES), 1)
    return [lane < C_HEAD_DIM, lane >= C_HEAD_DIM]


def _flat_spec(col_of):
    return pl.BlockSpec((1, SEQ, ATTN_PAIRS * LANES), lambda b, g: (b, 0, col_of(g)))


def _put_heads(tile, g, col0, col1):
    lane = lax.broadcasted_iota(jnp.int32, tile.shape, 1)
    return jnp.where(lane == 2 * g, col0, jnp.where(lane == 2 * g + 1, col1, tile))


def _get_head(tile, h):
    lane = lax.broadcasted_iota(jnp.int32, tile.shape, 1)
    return jnp.sum(jnp.where(lane == h, tile, 0.0), axis=1, keepdims=True)


PER_HEAD_SPEC = pl.BlockSpec((1, SEQ, LANES), lambda b, g: (b, 0, 0))


def _attn_branch_fwd(qkv, name):
    n_seq, d, l, _ = qkv.shape
    flat = qkv.reshape(n_seq, SEQ, ODD_IN)

    def body(q_ref, k_ref, v_ref, o_ref, m_ref, l_ref):
        heads = _head_masks()
        g = pl.program_id(1)

        @pl.when(g == 0)
        def _():
            m_ref[...] = jnp.zeros_like(m_ref)
            l_ref[...] = jnp.zeros_like(l_ref)

        def block(i, carry):
            rows, keys, mask = _attn_window(i, d)
            m_tile, l_tile = m_ref[0, rows, :], l_ref[0, rows, :]
            for pair in range(ATTN_PAIRS):
                lanes = slice(pair * LANES, (pair + 1) * LANES)
                q, k, v = q_ref[0, rows, lanes], k_ref[0, keys, lanes], v_ref[0, keys, lanes]
                res = []
                for hm in heads:
                    s = jnp.where(mask, _dot_nt(jnp.where(hm, q, 0), k), NEG)
                    m = jnp.max(s, axis=-1, keepdims=True)
                    p = jnp.exp(s - m)
                    res.append((_dot(p.astype(BF16), v), m, jnp.sum(p, axis=-1, keepdims=True)))
                o_ref[0, rows, lanes] = jnp.where(heads[0], res[0][0], res[1][0])
                m_tile = _put_heads(m_tile, g * ATTN_PAIRS + pair, res[0][1], res[1][1])
                l_tile = _put_heads(l_tile, g * ATTN_PAIRS + pair, res[0][2], res[1][2])
            m_ref[0, rows, :] = m_tile
            l_ref[0, rows, :] = l_tile
            return carry

        lax.fori_loop(0, Q_BLOCKS, block, 0, unroll=ATTN_UNROLL)

    o, m, l_sum = pl.pallas_call(
        body, name=name, grid=(n_seq, ATTN_STEPS),
        in_specs=[_flat_spec(lambda g: g), _flat_spec(lambda g: ATTN_STEPS + g),
                  _flat_spec(lambda g: 2 * ATTN_STEPS + g)],
        out_specs=[_flat_spec(lambda g: g), PER_HEAD_SPEC, PER_HEAD_SPEC],
        out_shape=[jax.ShapeDtypeStruct((n_seq, SEQ, D_MODEL), F32)] + [jax.ShapeDtypeStruct((n_seq, SEQ, LANES), F32)] * 2,
        compiler_params=_params(("parallel", "arbitrary")),
    )(flat, flat, flat)
    return [o.reshape(n_seq, d, l, D_MODEL), m.reshape(n_seq, d, l, LANES), l_sum.reshape(n_seq, d, l, LANES)]


def _attn_merge(branches, name):
    n_seq = branches[0][0].shape[0]
    t = n_seq * SEQ
    tm = MERGE_TILE

    def body(*refs):
        ins = refs[:9]
        o_ref, ob_ref, lse_ref = refs[9:12]
        nat = refs[12:]
        for b, d in enumerate(C_DILATIONS[1:]):
            for k in range(3):
                _load_dilated(ins[3 + 3 * b + k], d, nat[3 * b + k])
        ms = [ins[1][0, 0], nat[1][0], nat[4][0]]
        ls = [ins[2][0, 0], nat[2][0], nat[5][0]]
        m_all = jnp.maximum(jnp.maximum(ms[0], ms[1]), ms[2])
        ws = [jnp.exp(ms[b] - m_all) for b in range(3)]
        lane = lax.broadcasted_iota(jnp.int32, m_all.shape, 1)
        total = jnp.where(lane < C_HEADS, ws[0] * ls[0] + ws[1] * ls[1] + ws[2] * ls[2], 1.0)
        lse_ref[...] = m_all + jnp.log(total)
        first_head = lane < C_HEAD_DIM
        for p in range(LANE_GROUPS):
            lanes = slice(p * LANES, (p + 1) * LANES)
            spread = lambda c: jnp.where(first_head, c[:, 2 * p:2 * p + 1], c[:, 2 * p + 1:2 * p + 2])
            os_ = [ins[0][0, 0, :, lanes], nat[0][p], nat[3][p]]
            o = (spread(ws[0]) * os_[0] + spread(ws[1]) * os_[1] + spread(ws[2]) * os_[2]) / spread(total)
            o_ref[:, lanes] = o
            ob_ref[:, lanes] = o.astype(BF16)

    row = pl.BlockSpec((tm, D_MODEL), lambda i: (i, 0))
    flat = [a for br in branches for a in br]
    in_specs = []
    for wide, narrow in zip(_dilated_specs(tm, D_MODEL, lambda: 0), _dilated_specs(tm, LANES, lambda: 0)):
        in_specs += [wide, narrow, narrow]
    per_head = pltpu.VMEM((1, tm, LANES), F32)
    return pl.pallas_call(
        body, name=name, grid=(t // tm,), in_specs=in_specs,
        out_specs=[row, row, pl.BlockSpec((tm, LANES), lambda i: (i, 0))],
        out_shape=[jax.ShapeDtypeStruct((t, D_MODEL), F32), jax.ShapeDtypeStruct((t, D_MODEL), BF16),
                   jax.ShapeDtypeStruct((t, LANES), F32)],
        scratch_shapes=[pltpu.VMEM((LANE_GROUPS, tm, LANES), F32), per_head, per_head] * 2,
        compiler_params=_params(("parallel",)),
    )(*flat)


def _attn_branch_bwd(qkv, dout, lse, delta, name):
    n_seq, d, l, _ = qkv.shape
    flat = lambda a: a.reshape(n_seq, SEQ, a.shape[-1])

    def body(q_ref, k_ref, v_ref, do_ref, lse_nat_ref, dl_nat_ref, dq_ref, dk_ref, dv_ref, lse_ref, dl_ref,
             dkt_ref, dvt_ref):
        heads = _head_masks()
        g = pl.program_id(1)
        dkt_ref[...] = jnp.zeros_like(dkt_ref)
        dvt_ref[...] = jnp.zeros_like(dvt_ref)
        for nat_ref, dst_ref in ((lse_nat_ref, lse_ref), (dl_nat_ref, dl_ref)):
            for r in range(d):
                rows = pl.ds(r, l, stride=d) if d > 1 else slice(None)
                dst_ref[r * l:(r + 1) * l, :] = nat_ref.at[0][rows, :]

        def block(i, carry):
            rows, keys, mask = _attn_window(i, d)
            lse_b, dl_b = lse_ref[rows, :], dl_ref[rows, :]
            for pair in range(ATTN_PAIRS):
                lanes = slice(pair * LANES, (pair + 1) * LANES)
                q, do = q_ref[0, rows, lanes], do_ref[0, rows, lanes]
                k, v = k_ref[0, keys, lanes], v_ref[0, keys, lanes]
                dq, dk, dv = [], None, None
                for hh, hm in enumerate(heads):
                    head = 2 * (g * ATTN_PAIRS + pair) + hh
                    qh, doh = jnp.where(hm, q, 0), jnp.where(hm, do, 0)
                    s = jnp.where(mask, _dot_nt(qh, k), NEG)
                    p = jnp.exp(s - _get_head(lse_b, head))
                    ds = (p * (_dot_nt(doh, v) - _get_head(dl_b, head))).astype(BF16)
                    dq.append(_dot(ds, k) * QK_SCALE)
                    dk_h, dv_h = _dot_tn(qh, ds), _dot_tn(doh, p.astype(BF16))
                    dk = dk_h if dk is None else dk + dk_h
                    dv = dv_h if dv is None else dv + dv_h
                dq_ref[0, rows, lanes] = jnp.where(heads[0], dq[0], dq[1]).astype(BF16)
                dkt_ref[lanes, keys] += dk
                dvt_ref[lanes, keys] += dv
            return carry

        lax.fori_loop(0, Q_BLOCKS, block, 0, unroll=ATTN_UNROLL)
        for c in range(SEQ // ROW_TILE):
            rows = slice(c * ROW_TILE, (c + 1) * ROW_TILE)
            dk_ref[0, rows, :] = dkt_ref[:, rows].T.astype(BF16)
            dv_ref[0, rows, :] = dvt_ref[:, rows].T.astype(BF16)

    act = _flat_spec(lambda g: g)
    outs = pl.pallas_call(
        body, name=name, grid=(n_seq, ATTN_STEPS),
        in_specs=[_flat_spec(lambda g: g), _flat_spec(lambda g: ATTN_STEPS + g),
                  _flat_spec(lambda g: 2 * ATTN_STEPS + g), act, PER_HEAD_SPEC, PER_HEAD_SPEC],
        out_specs=[act] * 3,
        out_shape=[jax.ShapeDtypeStruct((n_seq, SEQ, D_MODEL), BF16)] * 3,
        scratch_shapes=[pltpu.VMEM((SEQ, LANES), F32)] * 2 + [pltpu.VMEM((ATTN_PAIRS * LANES, SEQ), F32)] * 2,
        compiler_params=_params(("parallel", "parallel")),
    )(flat(qkv), flat(qkv), flat(qkv), flat(dout), lse, delta)
    return [o.reshape(n_seq, d, l, D_MODEL) for o in outs]


def _attn_combine_bwd(grads, rope, name):
    n_seq = grads[0][0].shape[0]
    t = n_seq * SEQ
    tm = MERGE_TILE

    def body(*refs):
        c_ref, s_ref, o_ref, nat4_ref, nat16_ref = refs[9:]
        for sec in range(3):
            _load_dilated(refs[3 + sec], 4, nat4_ref)
            _load_dilated(refs[6 + sec], 16, nat16_ref)
            for p in range(LANE_GROUPS):
                blk = refs[sec][0, 0, :, p * LANES:(p + 1) * LANES] + nat4_ref[p] + nat16_ref[p]
                if sec < 2:
                    blk = blk * c_ref[...] - _swap_halves(blk) * s_ref[...]
                o_ref[:, sec * D_MODEL + p * LANES:sec * D_MODEL + (p + 1) * LANES] = blk.astype(BF16)

    tab = pl.BlockSpec((tm, LANES), lambda i: (i, 0))
    flat = [a for br in grads for a in br]
    in_specs = []
    for spec in _dilated_specs(tm, D_MODEL, lambda: 0):
        in_specs += [spec] * 3
    return pl.pallas_call(
        body, name=name, grid=(t // tm,), in_specs=in_specs + [tab, tab],
        out_specs=pl.BlockSpec((tm, ODD_IN), lambda i: (i, 0)),
        out_shape=jax.ShapeDtypeStruct((t, ODD_IN), BF16),
        scratch_shapes=[pltpu.VMEM((LANE_GROUPS, tm, LANES), F32)] * 2,
        compiler_params=_params(("parallel",)),
    )(*flat, *rope)


def _adamw(w, g, m, v):
    m = ADAM_B1 * m + (1.0 - ADAM_B1) * g
    v = ADAM_B2 * v + (1.0 - ADAM_B2) * jnp.square(g)
    m_hat = m / (1.0 - ADAM_B1 ** ADAM_STEP)
    v_hat = v / (1.0 - ADAM_B2 ** ADAM_STEP)
    delta = -ADAM_LR * (m_hat / (jnp.sqrt(v_hat) + ADAM_EPS) + ADAM_WD * w)
    return delta, m, v


def _adamw_sharded(parts, w, m, v, after, name):
    n_layers, rows, cols = w.shape
    tr = min(rows, 256)

    def body(*refs):
        p_refs = refs[:n_layers]
        w_ref, m_ref, v_ref, _, g_ref, d_ref, mo_ref, vo_ref = refs[n_layers:]
        layer = pl.program_id(0)
        g = None
        for l, p_ref in enumerate(p_refs):
            g_l = p_ref[0].astype(F32)
            for s in range(1, N_DEV):
                g_l = g_l + p_ref[s].astype(F32)
            g = g_l if g is None else jnp.where(layer == l, g_l, g)
        delta, mn, vn = _adamw(w_ref[0], g, m_ref[0], v_ref[0])
        g_ref[0] = g
        d_ref[0] = delta
        mo_ref[0] = mn
        vo_ref[0] = vn

    def part_spec(l):
        return pl.BlockSpec((N_DEV, tr, cols), lambda a, i: (0, jnp.where(a == l, i, 0), 0))

    row = pl.BlockSpec((1, tr, cols), lambda a, i: (a, i, 0))
    return pl.pallas_call(
        body, name=name, grid=(n_layers, rows // tr),
        in_specs=[part_spec(l) for l in range(n_layers)] + [row, row, row, pl.BlockSpec(memory_space=pl.ANY)],
        out_specs=[row] * 4, out_shape=[jax.ShapeDtypeStruct(w.shape, F32)] * 4,
        compiler_params=_params(("arbitrary", "arbitrary")),
    )(*parts, w, m, v, after)


def _small_update(gathered, where, weights, moments_m, moments_v, lb_index, name):
    n = len(weights)
    n_g = len(gathered)

    def body(*refs):
        g_refs = refs[:n_g]
        w_refs, m_refs, v_refs = refs[n_g:n_g + n], refs[n_g + n:n_g + 2 * n], refs[n_g + 2 * n:n_g + 3 * n]
        outs = refs[n_g + 3 * n:]

        def total(k):
            array, rows, lanes = where[k]
            ref = g_refs[array]
            index = (slice(None),) * (len(ref.shape) - 1) if rows is None else (rows, lanes)
            acc = ref[(0,) + index]
            for s in range(1, N_DEV):
                acc = acc + ref[(s,) + index]
            return acc

        loss_rows = total(n)
        outs[0][...] = jnp.sum(jnp.sum(loss_rows, axis=1, keepdims=True), axis=0, keepdims=True)
        for k in range(n):
            part = total(k)
            if k == lb_index:
                dlb = jnp.sum(part, axis=0, keepdims=True)
                tab = w_refs[k][...]
                e = jnp.exp(tab - jnp.max(tab, axis=0, keepdims=True))
                p = e / jnp.sum(e, axis=0, keepdims=True)
                first = lax.broadcasted_iota(jnp.int32, p.shape, 0) == 0
                grads = [(slice(None), p * (jnp.where(first, dlb, 0.0) - p[0:1, :] * dlb))]
            elif part.shape == w_refs[k].shape:
                grads = [(slice(None), part)]
            else:
                grads = [(slice(l, l + 1), jnp.sum(part[l * SUBLANES:(l + 1) * SUBLANES], axis=0, keepdims=True))
                         for l in range(w_refs[k].shape[0])]
            for rows, g in grads:
                delta, mn, vn = _adamw(w_refs[k][rows], g, m_refs[k][rows], v_refs[k][rows])
                outs[1 + 4 * k][rows] = g
                outs[2 + 4 * k][rows] = delta
                outs[3 + 4 * k][rows] = mn
                outs[4 + 4 * k][rows] = vn

    vmem = pl.BlockSpec(memory_space=pltpu.VMEM)
    out_shape = [jax.ShapeDtypeStruct((1, 1), F32)]
    for w in weights:
        out_shape += [jax.ShapeDtypeStruct(w.shape, F32)] * 4
    args = list(gathered) + list(weights) + list(moments_m) + list(moments_v)
    return pl.pallas_call(
        body, name=name, in_specs=[vmem] * len(args), out_specs=[vmem] * len(out_shape), out_shape=out_shape,
        compiler_params=pltpu.CompilerParams(vmem_limit_bytes=VMEM_LIMIT),
    )(*args)


def kernel(x, positions, norm_mix_pre, norm_mix_post, norm_ffn_pre, norm_ffn_post, w_in_even, lb_table, a_norm, b_ln_g, b_ln_b, b_ws, b_bias, w_out_even, w_in_odd, w_out_odd, w_ff1, w_ff2, loss_target, m_norm_mix_pre, m_norm_mix_post, m_norm_ffn_pre, m_norm_ffn_post, m_w_in_even, m_lb_table, m_a_norm, m_b_ln_g, m_b_ln_b, m_b_ws, m_b_bias, m_w_out_even, m_w_in_odd, m_w_out_odd, m_w_ff1, m_w_ff2, v_norm_mix_pre, v_norm_mix_post, v_norm_ffn_pre, v_norm_ffn_post, v_w_in_even, v_lb_table, v_a_norm, v_b_ln_g, v_b_ln_b, v_b_ws, v_b_bias, v_w_out_even, v_w_in_odd, v_w_out_odd, v_w_ff1, v_w_ff2):
    n_seq = x.shape[0]
    t = n_seq * SEQ
    x0 = x.reshape(t, D_MODEL)
    target = loss_target.reshape(t, D_MODEL)

    me = _my_slot().astype(jnp.int32).reshape(1)

    order = ["in_e", "out_e", "ff1_0", "ff2_0", "in_o", "out_o", "ff1_1", "ff2_1"]
    shards = dict(in_e=w_in_even[0], out_e=w_out_even[0], in_o=w_in_odd[0], out_o=w_out_odd[0],
                  ff1_0=w_ff1[0], ff1_1=w_ff1[1], ff2_0=w_ff2[0], ff2_1=w_ff2[1])
    by_columns = ("in_e", "in_o", "ff1_0", "ff1_1")

    def place(k, after):
        if k in by_columns:
            return _place_own_columns(shards[k], me, "place_" + k, after)
        return _place_own(shards[k], me, "place_" + k, False, after=after)

    gathers = {}
    send0, recv0, land0, _, token0 = _exchange_start([place(order[0], None)], [None], "gather_start_first")
    gathers[order[0]] = (land0[0], send0[0], recv0[0])
    sends, recvs, lands, _, g_token = _exchange_start([place(k, token0) for k in order[1:]],
                                                      [None] * (len(order) - 1), "gather_start")
    for k, land, send, recv in zip(order[1:], lands, sends, recvs):
        gathers[k] = (land, send, recv)

    def get_w(keys, after):
        lands_k, sends_k, recvs_k = zip(*[gathers[k] for k in keys])
        return _exchange_wait(list(lands_k), [None] * len(keys), list(sends_k), list(recvs_k), after,
                              "gather_wait_" + keys[0])

    sent = {}

    def put_g(group, blocks):
        keys = list(blocks)
        own = [_place_own(blocks[k], me, "own_" + k, True) for k in keys]
        send_sems, recv_sems, own, srcs, token = _exchange_start(own, [blocks[k] for k in keys], "scatter_start_" + group)
        sent[group] = (keys, own, srcs, send_sems, recv_sems)
        return token

    rope = _rope_tables(positions)
    bias_t = b_bias[0].T
    grads = _local_step(x0, target, rope, norm_mix_pre, norm_mix_post, norm_ffn_pre, norm_ffn_post, lb_table,
                        a_norm, b_ln_g, b_ln_b, b_ws[0], bias_t, get_w, put_g, g_token)
    (dx0, loss_part, dg_mix_pre, dg_mix_post, dg_ffn_pre, dg_ffn_post, d_lb, d_a_norm, d_ln_g, d_ln_b, d_ws,
     d_bias_t) = grads

    packed = jnp.concatenate([dg_mix_pre, dg_mix_post, dg_ffn_pre, dg_ffn_post,
                              jnp.concatenate([d_lb, d_a_norm], axis=1), jnp.concatenate([d_ln_g, d_ln_b], axis=1),
                              loss_part], axis=0)
    small_lands = [_place_own(a, me, "own_small%d" % k, False, F32) for k, a in enumerate((packed, d_ws, d_bias_t))]
    s_send, s_recv, small_lands, _, after = _exchange_start(small_lands, [None] * 3, "gather_small_start")

    big = dict(w_in_even=(["in_e"], w_in_even, m_w_in_even, v_w_in_even),
               w_out_even=(["out_e"], w_out_even, m_w_out_even, v_w_out_even),
               w_in_odd=(["in_o"], w_in_odd, m_w_in_odd, v_w_in_odd),
               w_out_odd=(["out_o"], w_out_odd, m_w_out_odd, v_w_out_odd),
               w_ff1=(["ff1_0", "ff1_1"], w_ff1, m_w_ff1, v_w_ff1), w_ff2=(["ff2_0", "ff2_1"], w_ff2, m_w_ff2, v_w_ff2))
    recv, big_out = {}, {}
    for groups, names in ((("ffn1", "ffn0"), ("w_ff1", "w_ff2")), (("mix1",), ("w_in_odd", "w_out_odd")),
                          (("mix0",), ("w_in_even", "w_out_even"))):
        for group in groups:
            keys, own, srcs, send_sems, recv_sems = sent[group]
            recv.update(zip(keys, _exchange_wait(own, srcs, send_sems, recv_sems, after, "scatter_wait_" + group)))
        for nm in names:
            keys, w, m, v = big[nm]
            big_out[nm] = _adamw_sharded([recv[k] for k in keys], w, m, v, after, "adamw_" + nm)
            after = big_out[nm][0]
    big_out = [big_out[nm] for nm in ("w_in_even", "w_out_even", "w_in_odd", "w_out_odd", "w_ff1", "w_ff2")]
    gathered = _exchange_wait(small_lands, [None] * 3, s_send, s_recv, after, "gather_small_wait")
    rows8 = lambda k: slice(SUBLANES * k, SUBLANES * (k + 1))
    left, right, every = slice(0, A_WIDTH), slice(A_WIDTH, 2 * A_WIDTH), slice(None)
    where = [(0, slice(0, 16), every), (0, slice(16, 32), every), (0, slice(32, 48), every), (0, slice(48, 64), every),
             (0, rows8(8), left), (0, rows8(8), right), (0, rows8(9), left), (0, rows8(9), right),
             (1, None, None), (2, None, None), (0, rows8(10), every)]
    small_w = [norm_mix_pre, norm_mix_post, norm_ffn_pre, norm_ffn_post, lb_table, a_norm, b_ln_g, b_ln_b,
               b_ws[0], bias_t]
    small_m = [m_norm_mix_pre, m_norm_mix_post, m_norm_ffn_pre, m_norm_ffn_post, m_lb_table, m_a_norm, m_b_ln_g,
               m_b_ln_b, m_b_ws[0], m_b_bias[0].T]
    small_v = [v_norm_mix_pre, v_norm_mix_post, v_norm_ffn_pre, v_norm_ffn_post, v_lb_table, v_a_norm, v_b_ln_g,
               v_b_ln_b, v_b_ws[0], v_b_bias[0].T]
    small_out = _small_update(gathered, where, small_w, small_m, small_v, 4, "small_update")
    loss = small_out[0].reshape(())
    small = [small_out[1 + 4 * k:5 + 4 * k] for k in range(len(small_w))]
    small[8] = [a[None] for a in small[8]]
    small[9] = [a.T[None] for a in small[9]]

    per_weight = small[0:4] + [big_out[0]] + small[4:10] + big_out[1:6]
    grad_x = dx0.reshape(x.shape)
    out = [loss, grad_x]
    for kind in range(4):
        out += [p[kind] for p in per_weight]
    return tuple(out)


def _local_step(x0, target, rope, norm_mix_pre, norm_mix_post, norm_ffn_pre, norm_ffn_post, lb_table, a_norm,
                b_ln_g, b_ln_b, ws, bias_t, get_w, put_g, token):
    def gain(a, l, tok):
        return a[l:l + 1] if tok is None else a[l:l + 1] + tok[0:1, 0:1]

    full = lambda a: a.reshape(-1, D_MODEL)
    owners = lambda a: a.reshape((N_DEV, -1) + a.shape[1:])

    (g_in_e,) = get_w(["in_e"], token)
    proj, h_mix0 = _norm_inproj(x0, gain(norm_mix_pre, 0, token), g_in_e, "inproj_even")
    mixed, pre_a, states = _hgrn2_fwd(proj, lb_table, a_norm, "hgrn2_fwd")
    mixed = _gmlp_fwd(proj, mixed, b_ln_g, b_ln_b, ws, bias_t, "gmlp_fwd")
    w_out_e = full(get_w(["out_e"], mixed)[0])
    x1, mix0 = _outproj([mixed], w_out_e, x0, gain(norm_mix_post, 0, None), "outproj_even")
    w1_0, w2_0 = get_w(["ff1_0", "ff2_0"], x1)
    w2_0 = full(w2_0)
    x2, y0, h_ffn0, r0 = _ffn_fwd(x1, gain(norm_ffn_pre, 0, None), w1_0, w2_0, gain(norm_ffn_post, 0, None), "ffn_fwd_0")
    (g_in_o,) = get_w(["in_o"], x2)
    *qkv, h_mix1 = _norm_inproj_rope(x2, gain(norm_mix_pre, 1, None), g_in_o, rope, "inproj_odd")
    branches = [_attn_branch_fwd(a, "attn_fwd_d%d" % d) for a, d in zip(qkv, C_DILATIONS)]
    attn, attn_b, lse = _attn_merge(branches, "attn_merge")
    w_out_o = full(get_w(["out_o"], attn_b)[0])
    x3, mix1 = _outproj([attn_b], w_out_o, x2, gain(norm_mix_post, 1, None), "outproj_odd")
    w1_1, w2_1 = get_w(["ff1_1", "ff2_1"], x3)
    w2_1 = full(w2_1)
    dx4, y1, h_ffn1, r1, loss_part = _ffn_fwd(x3, gain(norm_ffn_pre, 1, None), w1_1, w2_1, gain(norm_ffn_post, 1, None),
                                              "ffn_fwd_1", target)

    dx3, dy1, da1, dg_ffn_pre1, dg_ffn_post1 = _ffn_bwd(
        dx4, x3, y1, r1, gain(norm_ffn_pre, 1, None), w1_1, w2_1, gain(norm_ffn_post, 1, None), "ffn_bwd_1")
    gw_ff1_1 = _grad_w(h_ffn1, da1, True, "grad_w_ff1_1")
    gw_ff2_1 = _grad_w(r1, dy1, False, "grad_w_ff2_1")
    tok = put_g("ffn1", dict(ff1_1=gw_ff1_1, ff2_1=owners(gw_ff2_1)))
    *dattn, delta, dz1, dg_mix_post1 = _outproj_bwd_attn(dx3, mix1, gain(norm_mix_post, 1, tok), w_out_o, attn,
                                                  "outproj_bwd_odd")
    gw_out_o = _grad_w(attn_b, dz1, False, "grad_w_out_odd")
    per_seq = lambda a: a.reshape(-1, SEQ, LANES)
    grads_c = [_attn_branch_bwd(qkv[b], dattn[b], per_seq(lse), per_seq(delta), "attn_bwd_d%d" % d)
               for b, d in enumerate(C_DILATIONS)]
    dqkv = _attn_combine_bwd(grads_c, rope, "attn_combine_bwd")
    gw_in_o = _grad_w(h_mix1, dqkv, True, "grad_w_in_odd")
    tok = put_g("mix1", dict(out_o=owners(gw_out_o), in_o=gw_in_o))
    dx2, dg_mix_pre1 = _inproj_bwd(dqkv, g_in_o, dx3, x2, gain(norm_mix_pre, 1, tok), "inproj_bwd_odd")

    dx1, dy0, da0, dg_ffn_pre0, dg_ffn_post0 = _ffn_bwd(
        dx2, x1, y0, r0, gain(norm_ffn_pre, 0, None), w1_0, w2_0, gain(norm_ffn_post, 0, None), "ffn_bwd_0")
    gw_ff1_0 = _grad_w(h_ffn0, da0, True, "grad_w_ff1_0")
    gw_ff2_0 = _grad_w(r0, dy0, False, "grad_w_ff2_0")
    tok = put_g("ffn0", dict(ff1_0=gw_ff1_0, ff2_0=owners(gw_ff2_0)))
    dcat, dz0, dg_mix_post0 = _outproj_bwd(dx1, mix0, gain(norm_mix_post, 0, tok), w_out_e, "outproj_bwd_even")
    gw_out_e = _grad_w(mixed, dz0, False, "grad_w_out_even")
    dproj, d_lb, d_a_norm = _hgrn2_bwd(proj, dcat, pre_a, states, lb_table, a_norm, "hgrn2_bwd")
    dproj, d_ln_g, d_ln_b, d_ws, d_bias_t = _gmlp_bwd(proj, dcat, dproj, b_ln_g, b_ln_b, ws, bias_t, "gmlp_bwd")
    gw_in_e = _grad_w(h_mix0, dproj, True, "grad_w_in_even")
    tok = put_g("mix0", dict(out_e=owners(gw_out_e), in_e=gw_in_e))
    dx0, dg_mix_pre0 = _inproj_bwd(dproj, g_in_e, dx1, x0, gain(norm_mix_pre, 0, tok), "inproj_bwd_even")

    layers = lambda a, b: jnp.concatenate([a, b], axis=0)
    return (dx0, loss_part, layers(dg_mix_pre0, dg_mix_pre1), layers(dg_mix_post0, dg_mix_post1),
            layers(dg_ffn_pre0, dg_ffn_pre1), layers(dg_ffn_post0, dg_ffn_post1),
            d_lb, d_a_norm, d_ln_g, d_ln_b, d_ws, d_bias_t)
```
